```python
import math
import jax, jax.numpy as jnp
from jax import lax
import numpy as np

D_MODEL = 1024
BATCH = 8
SEQ = 8192
DEPTH = 1

PLE_DIM = 256
D_MIX = D_MODEL
RET_HEADS = 4
RET_HEAD_DIM = 128
RET_WIDTH = RET_HEADS * RET_HEAD_DIM
RET_CHUNK = 128
MLA_HEADS = 8
MLA_NOPE_DIM = 64
MLA_ROPE_DIM = 32
MLA_QK_DIM = MLA_NOPE_DIM + MLA_ROPE_DIM
MLA_V_DIM = 64
MLA_WIDTH = MLA_HEADS * MLA_V_DIM
MLA_Q_LORA = 384
MLA_KV_LORA = 256
Q_BLOCK = 128
IN_COLS = 4 * RET_WIDTH + MLA_Q_LORA + MLA_KV_LORA + MLA_ROPE_DIM
D_FF = ((8 * D_MODEL // 3 + 255) // 256) * 256
ROPE_BASE = 10000.0
EPS = 1e-6

kernel_name = "hybrid_retention_mla_parallel_heads"


def rmsnorm(x, w):
    xf = x.astype(jnp.float32)
    y = xf * lax.rsqrt(jnp.mean(xf * xf, axis=-1, keepdims=True) + EPS)
    return (y * w.astype(jnp.float32)).astype(x.dtype)


def head_groupnorm(x, w):
    xf = x.astype(jnp.float32)
    mu = jnp.mean(xf, axis=-1, keepdims=True)
    var = jnp.mean(jnp.square(xf - mu), axis=-1, keepdims=True)
    y = (xf - mu) * lax.rsqrt(var + EPS)
    B, S, H, d = x.shape
    return (y.reshape(B, S, H * d) * w.astype(jnp.float32)).astype(x.dtype)


def rope(x, positions):
    d = x.shape[-1]
    half = d // 2
    inv = 1.0 / (ROPE_BASE ** (jnp.arange(half, dtype=jnp.float32) / half))
    ang = positions.astype(jnp.float32)[..., None] * inv
    cos = jnp.cos(ang)[:, :, None, :].astype(x.dtype)
    sin = jnp.sin(ang)[:, :, None, :].astype(x.dtype)
    x1, x2 = x[..., :half], x[..., half:]
    return jnp.concatenate([x1 * cos - x2 * sin, x2 * cos + x1 * sin], axis=-1)


def retention_chunkwise(q, k, v):
    B, S, H, d = q.shape
    C = RET_CHUNK
    N = S // C
    dt = q.dtype
    log_g = jnp.log(1.0 - 2.0 ** (-5.0 - jnp.arange(H, dtype=jnp.float32)))
    j = jnp.arange(C, dtype=jnp.float32)
    diff = j[:, None] - j[None, :]
    D = jnp.where(diff[None] >= 0, jnp.exp(jnp.maximum(diff, 0.0)[None] * log_g[:, None, None]), 0.0).astype(dt)
    zeta = jnp.exp((C - 1 - j)[None, :] * log_g[:, None]).astype(dt)
    xi = jnp.exp((j + 1)[None, :] * log_g[:, None]).astype(dt)
    g_chunk = jnp.exp(C * log_g).astype(dt)

    qc = q.reshape(B, N, C, H, d)
    kc = k.reshape(B, N, C, H, d)
    vc = v.reshape(B, N, C, H, d)
    scores = jnp.einsum('bnchd,bnmhd->bnhcm', qc, kc) * D[None, None]
    inner = jnp.einsum('bnhcm,bnmhe->bnche', scores, vc)
    U = jnp.einsum('bnmhd,bnmhe,hm->nbhde', kc, vc, zeta)

    def step(R, u):
        return g_chunk[None, :, None, None] * R + u, R

    _, R_prev = lax.scan(step, jnp.zeros_like(U[0]), U)
    cross = jnp.einsum('bnchd,nbhde->bnche', qc, R_prev) * xi.T[None, None, :, :, None]
    return (inner + cross).reshape(B, S, H, d)


def mla_causal(q, k, v):
    B, S, H, dqk = q.shape
    dv = v.shape[-1]
    NB = S // Q_BLOCK
    scale = 1.0 / math.sqrt(dqk)
    kt = k.transpose(0, 2, 1, 3)
    vt = v.transpose(0, 2, 1, 3)
    qb = q.reshape(B, NB, Q_BLOCK, H, dqk).transpose(1, 0, 3, 2, 4)
    kpos = jnp.arange(S)

    def block(args):
        qi, bi = args
        s = jnp.einsum('bhqd,bhkd->bhqk', qi, kt).astype(jnp.float32) * scale
        qpos = bi * Q_BLOCK + jnp.arange(Q_BLOCK)
        mask = kpos[None, :] <= qpos[:, None]
        s = jnp.where(mask[None, None], s, -1e30)
        pr = jax.nn.softmax(s, axis=-1).astype(vt.dtype)
        return jnp.einsum('bhqk,bhkd->bhqd', pr, vt)

    out = lax.map(block, (qb, jnp.arange(NB)))
    return out.transpose(1, 0, 3, 2, 4).reshape(B, S, H * dv)


def token_mixer(xn, positions, w_in, ret_gn_w, mla_q_norm, w_uq, mla_kv_norm, w_ukv, w_o):
    B, S, _ = xn.shape
    proj = xn @ w_in
    o = 0
    rq = proj[..., o:o + RET_WIDTH]; o += RET_WIDTH
    rk = proj[..., o:o + RET_WIDTH]; o += RET_WIDTH
    rv = proj[..., o:o + RET_WIDTH]; o += RET_WIDTH
    rg = proj[..., o:o + RET_WIDTH]; o += RET_WIDTH
    cq = proj[..., o:o + MLA_Q_LORA]; o += MLA_Q_LORA
    ckv = proj[..., o:o + MLA_KV_LORA]; o += MLA_KV_LORA
    kr = proj[..., o:o + MLA_ROPE_DIM]

    shp = (B, S, RET_HEADS, RET_HEAD_DIM)
    rq = rope(rq.reshape(shp), positions)
    rk = rope(rk.reshape(shp), positions) * (RET_HEAD_DIM ** -0.5)
    ry = retention_chunkwise(rq, rk, rv.reshape(shp))
    ret_out = jax.nn.silu(rg) * head_groupnorm(ry, ret_gn_w)

    qh = (rmsnorm(cq, mla_q_norm) @ w_uq).reshape(B, S, MLA_HEADS, MLA_QK_DIM)
    q = jnp.concatenate([qh[..., :MLA_NOPE_DIM], rope(qh[..., MLA_NOPE_DIM:], positions)], axis=-1)
    kvh = (rmsnorm(ckv, mla_kv_norm) @ w_ukv).reshape(B, S, MLA_HEADS, MLA_NOPE_DIM + MLA_V_DIM)
    k_rope = rope(kr[:, :, None, :], positions)
    k = jnp.concatenate([kvh[..., :MLA_NOPE_DIM],
                         jnp.broadcast_to(k_rope, (B, S, MLA_HEADS, MLA_ROPE_DIM))], axis=-1)
    v = kvh[..., MLA_NOPE_DIM:]
    mla_out = mla_causal(q, k, v)

    return jnp.concatenate([ret_out, mla_out], axis=-1) @ w_o


def _fwd_setup_inputs(seed: int = 0) -> dict:
    key = jax.random.key(seed)
    ks = jax.random.split(key, 24)
    L = DEPTH

    def nrm(k, shape, fan_in):
        return jax.random.normal(k, shape, jnp.float32) * (fan_in ** -0.5)

    def gain(k, shape):
        return 1.0 + 0.05 * jax.random.normal(k, shape, jnp.float32)

    return {
        "x": jax.random.normal(ks[0], (BATCH, SEQ, D_MODEL), jnp.float32),
        "p": jax.random.normal(ks[1], (DEPTH, BATCH, SEQ, PLE_DIM), jnp.float32),
        "positions": jnp.broadcast_to(jnp.arange(SEQ, dtype=jnp.int32)[None], (BATCH, SEQ)),
        "pre_mix_norm": gain(ks[2], (L, D_MODEL)),
        "w_in": nrm(ks[3], (L, D_MODEL, IN_COLS), D_MODEL),
        "ret_gn_w": gain(ks[4], (L, RET_WIDTH)),
        "mla_q_norm": gain(ks[5], (L, MLA_Q_LORA)),
        "w_uq": nrm(ks[6], (L, MLA_Q_LORA, MLA_HEADS * MLA_QK_DIM), MLA_Q_LORA),
        "mla_kv_norm": gain(ks[7], (L, MLA_KV_LORA)),
        "w_ukv": nrm(ks[8], (L, MLA_KV_LORA, MLA_HEADS * (MLA_NOPE_DIM + MLA_V_DIM)), MLA_KV_LORA),
        "w_o": nrm(ks[9], (L, D_MIX, D_MODEL), D_MIX),
        "post_mix_norm": gain(ks[10], (L, D_MODEL)),
        "pre_ffn_norm": gain(ks[11], (L, D_MODEL)),
        "w_gate": nrm(ks[12], (L, D_MODEL, D_FF), D_MODEL),
        "w_up": nrm(ks[13], (L, D_MODEL, D_FF), D_MODEL),
        "w_down": nrm(ks[14], (L, D_FF, D_MODEL), D_FF),
        "post_ffn_norm": gain(ks[15], (L, D_MODEL)),
        "w_ple_proj": nrm(ks[16], (L, PLE_DIM, D_MODEL), PLE_DIM),
        "ple_norm": gain(ks[17], (L, D_MODEL)),
        "w_ple_gate": nrm(ks[18], (L, D_MODEL, D_MODEL), D_MODEL),
        "b_ple_gate": 0.02 * jax.random.normal(ks[19], (L, D_MODEL), jnp.float32),
    }


def _fwd_reference(x, p, positions, pre_mix_norm, w_in, ret_gn_w, mla_q_norm, w_uq, mla_kv_norm,
              w_ukv, w_o, post_mix_norm, pre_ffn_norm, w_gate, w_up, w_down, post_ffn_norm,
              w_ple_proj, ple_norm, w_ple_gate, b_ple_gate):
    h = x
    for i in range(DEPTH):
        xn = rmsnorm(h, pre_mix_norm[i])
        mix = token_mixer(xn, positions, w_in[i], ret_gn_w[i], mla_q_norm[i], w_uq[i],
                          mla_kv_norm[i], w_ukv[i], w_o[i])
        h = h + rmsnorm(mix, post_mix_norm[i])
        hn = rmsnorm(h, pre_ffn_norm[i])
        ff = (jax.nn.silu(hn @ w_gate[i]) * (hn @ w_up[i])) @ w_down[i]
        h = h + rmsnorm(ff, post_ffn_norm[i])
        e = rmsnorm(p[i] @ w_ple_proj[i], ple_norm[i])
        gate = jax.nn.sigmoid(h @ w_ple_gate[i] + b_ple_gate[i])
        h = h + e * gate
    return h


import jax as _jax
import jax.numpy as _jnp

TWIN_FORMAT = 'train_step'
FWD_PARAMS = ['x', 'p', 'positions', 'pre_mix_norm', 'w_in', 'ret_gn_w', 'mla_q_norm', 'w_uq', 'mla_kv_norm', 'w_ukv', 'w_o', 'post_mix_norm', 'pre_ffn_norm', 'w_gate', 'w_up', 'w_down', 'post_ffn_norm', 'w_ple_proj', 'ple_norm', 'w_ple_gate', 'b_ple_gate']
TWIN_WEIGHTS = ['pre_mix_norm', 'w_in', 'ret_gn_w', 'mla_q_norm', 'w_uq', 'mla_kv_norm', 'w_ukv', 'w_o', 'post_mix_norm', 'pre_ffn_norm', 'w_gate', 'w_up', 'w_down', 'post_ffn_norm', 'w_ple_proj', 'ple_norm', 'w_ple_gate', 'b_ple_gate']
TWIN_DIFF_INPUT = 'x'
TWIN_INPUTS = ['x', 'p', 'positions', 'pre_mix_norm', 'w_in', 'ret_gn_w', 'mla_q_norm', 'w_uq', 'mla_kv_norm', 'w_ukv', 'w_o', 'post_mix_norm', 'pre_ffn_norm', 'w_gate', 'w_up', 'w_down', 'post_ffn_norm', 'w_ple_proj', 'ple_norm', 'w_ple_gate', 'b_ple_gate', 'loss_target', 'm_pre_mix_norm', 'm_w_in', 'm_ret_gn_w', 'm_mla_q_norm', 'm_w_uq', 'm_mla_kv_norm', 'm_w_ukv', 'm_w_o', 'm_post_mix_norm', 'm_pre_ffn_norm', 'm_w_gate', 'm_w_up', 'm_w_down', 'm_post_ffn_norm', 'm_w_ple_proj', 'm_ple_norm', 'm_w_ple_gate', 'm_b_ple_gate', 'v_pre_mix_norm', 'v_w_in', 'v_ret_gn_w', 'v_mla_q_norm', 'v_w_uq', 'v_mla_kv_norm', 'v_w_ukv', 'v_w_o', 'v_post_mix_norm', 'v_pre_ffn_norm', 'v_w_gate', 'v_w_up', 'v_w_down', 'v_post_ffn_norm', 'v_w_ple_proj', 'v_ple_norm', 'v_w_ple_gate', 'v_b_ple_gate']
TWIN_OUTPUTS = ['loss', 'grad_x', 'grad_pre_mix_norm', 'grad_w_in', 'grad_ret_gn_w', 'grad_mla_q_norm', 'grad_w_uq', 'grad_mla_kv_norm', 'grad_w_ukv', 'grad_w_o', 'grad_post_mix_norm', 'grad_pre_ffn_norm', 'grad_w_gate', 'grad_w_up', 'grad_w_down', 'grad_post_ffn_norm', 'grad_w_ple_proj', 'grad_ple_norm', 'grad_w_ple_gate', 'grad_b_ple_gate', 'delta_pre_mix_norm', 'delta_w_in', 'delta_ret_gn_w', 'delta_mla_q_norm', 'delta_w_uq', 'delta_mla_kv_norm', 'delta_w_ukv', 'delta_w_o', 'delta_post_mix_norm', 'delta_pre_ffn_norm', 'delta_w_gate', 'delta_w_up', 'delta_w_down', 'delta_post_ffn_norm', 'delta_w_ple_proj', 'delta_ple_norm', 'delta_w_ple_gate', 'delta_b_ple_gate', 'new_m_pre_mix_norm', 'new_m_w_in', 'new_m_ret_gn_w', 'new_m_mla_q_norm', 'new_m_w_uq', 'new_m_mla_kv_norm', 'new_m_w_ukv', 'new_m_w_o', 'new_m_post_mix_norm', 'new_m_pre_ffn_norm', 'new_m_w_gate', 'new_m_w_up', 'new_m_w_down', 'new_m_post_ffn_norm', 'new_m_w_ple_proj', 'new_m_ple_norm', 'new_m_w_ple_gate', 'new_m_b_ple_gate', 'new_v_pre_mix_norm', 'new_v_w_in', 'new_v_ret_gn_w', 'new_v_mla_q_norm', 'new_v_w_uq', 'new_v_mla_kv_norm', 'new_v_w_ukv', 'new_v_w_o', 'new_v_post_mix_norm', 'new_v_pre_ffn_norm', 'new_v_w_gate', 'new_v_w_up', 'new_v_w_down', 'new_v_post_ffn_norm', 'new_v_w_ple_proj', 'new_v_ple_norm', 'new_v_w_ple_gate', 'new_v_b_ple_gate']
TWIN_LEAF_KINDS = {'loss': 'loss', 'grad_x': 'grad_x', 'grad_pre_mix_norm': 'grad_w', 'grad_w_in': 'grad_w', 'grad_ret_gn_w': 'grad_w', 'grad_mla_q_norm': 'grad_w', 'grad_w_uq': 'grad_w', 'grad_mla_kv_norm': 'grad_w', 'grad_w_ukv': 'grad_w', 'grad_w_o': 'grad_w', 'grad_post_mix_norm': 'grad_w', 'grad_pre_ffn_norm': 'grad_w', 'grad_w_gate': 'grad_w', 'grad_w_up': 'grad_w', 'grad_w_down': 'grad_w', 'grad_post_ffn_norm': 'grad_w', 'grad_w_ple_proj': 'grad_w', 'grad_ple_norm': 'grad_w', 'grad_w_ple_gate': 'grad_w', 'grad_b_ple_gate': 'grad_w', 'delta_pre_mix_norm': 'delta_w', 'delta_w_in': 'delta_w', 'delta_ret_gn_w': 'delta_w', 'delta_mla_q_norm': 'delta_w', 'delta_w_uq': 'delta_w', 'delta_mla_kv_norm': 'delta_w', 'delta_w_ukv': 'delta_w', 'delta_w_o': 'delta_w', 'delta_post_mix_norm': 'delta_w', 'delta_pre_ffn_norm': 'delta_w', 'delta_w_gate': 'delta_w', 'delta_w_up': 'delta_w', 'delta_w_down': 'delta_w', 'delta_post_ffn_norm': 'delta_w', 'delta_w_ple_proj': 'delta_w', 'delta_ple_norm': 'delta_w', 'delta_w_ple_gate': 'delta_w', 'delta_b_ple_gate': 'delta_w', 'new_m_pre_mix_norm': 'new_m', 'new_m_w_in': 'new_m', 'new_m_ret_gn_w': 'new_m', 'new_m_mla_q_norm': 'new_m', 'new_m_w_uq': 'new_m', 'new_m_mla_kv_norm': 'new_m', 'new_m_w_ukv': 'new_m', 'new_m_w_o': 'new_m', 'new_m_post_mix_norm': 'new_m', 'new_m_pre_ffn_norm': 'new_m', 'new_m_w_gate': 'new_m', 'new_m_w_up': 'new_m', 'new_m_w_down': 'new_m', 'new_m_post_ffn_norm': 'new_m', 'new_m_w_ple_proj': 'new_m', 'new_m_ple_norm': 'new_m', 'new_m_w_ple_gate': 'new_m', 'new_m_b_ple_gate': 'new_m', 'new_v_pre_mix_norm': 'new_v', 'new_v_w_in': 'new_v', 'new_v_ret_gn_w': 'new_v', 'new_v_mla_q_norm': 'new_v', 'new_v_w_uq': 'new_v', 'new_v_mla_kv_norm': 'new_v', 'new_v_w_ukv': 'new_v', 'new_v_w_o': 'new_v', 'new_v_post_mix_norm': 'new_v', 'new_v_pre_ffn_norm': 'new_v', 'new_v_w_gate': 'new_v', 'new_v_w_up': 'new_v', 'new_v_w_down': 'new_v', 'new_v_post_ffn_norm': 'new_v', 'new_v_w_ple_proj': 'new_v', 'new_v_ple_norm': 'new_v', 'new_v_w_ple_gate': 'new_v', 'new_v_b_ple_gate': 'new_v'}


def _forward(args):
    return _fwd_reference(*[args[k] for k in FWD_PARAMS])


def _output_shape():
    def fwd():
        inp = _fwd_setup_inputs(0)
        return _fwd_reference(*[inp[k] for k in FWD_PARAMS])
    out = _jax.eval_shape(fwd)
    return out.shape, out.dtype

N_MICROBATCH = 1
ADAM_LR = 0.001
ADAM_B1 = 0.9
ADAM_B2 = 0.999
ADAM_EPS = 1e-08
ADAM_WD = 0.01
ADAM_STEP = 10
PER_EXAMPLE_BATCH_AXIS = {'x': 0, 'p': 1, 'positions': 0, 'loss_target': 0}
SHARED_INPUTS = []
_WEIGHT_DTYPES = {'pre_mix_norm': _jnp.float32, 'w_in': _jnp.float32, 'ret_gn_w': _jnp.float32, 'mla_q_norm': _jnp.float32, 'w_uq': _jnp.float32, 'mla_kv_norm': _jnp.float32, 'w_ukv': _jnp.float32, 'w_o': _jnp.float32, 'post_mix_norm': _jnp.float32, 'pre_ffn_norm': _jnp.float32, 'w_gate': _jnp.float32, 'w_up': _jnp.float32, 'w_down': _jnp.float32, 'post_ffn_norm': _jnp.float32, 'w_ple_proj': _jnp.float32, 'ple_norm': _jnp.float32, 'w_ple_gate': _jnp.float32, 'b_ple_gate': _jnp.float32}
MOMENT_SCALE = {'pre_mix_norm': 1.468087e+00, 'w_in': 8.729296e-01, 'ret_gn_w': 1.409381e+00, 'mla_q_norm': 3.191733e-01, 'w_uq': 2.202845e-01, 'mla_kv_norm': 5.526673e-01, 'w_ukv': 2.804236e-01, 'w_o': 9.045493e-01, 'post_mix_norm': 6.511836e+01, 'pre_ffn_norm': 9.988391e-01, 'w_gate': 3.050944e-01, 'w_up': 5.353338e-01, 'w_down': 8.912930e-01, 'post_ffn_norm': 6.483779e+01, 'w_ple_proj': 3.074708e-01, 'ple_norm': 2.156014e+01, 'w_ple_gate': 2.160345e-01, 'b_ple_gate': 5.316170e+00}


def _to_microbatches(a, axis):
    t = _jnp.moveaxis(a, axis, 0)
    t = t.reshape((N_MICROBATCH, t.shape[0] // N_MICROBATCH) + t.shape[1:])
    return _jnp.moveaxis(t, 1, axis + 1)


def setup_inputs(seed: int = 0) -> dict:
    inp = _fwd_setup_inputs(seed)
    key = _jax.random.fold_in(_jax.random.key(seed), 7919)
    shape, _ = _output_shape()
    out = dict(inp)
    out["loss_target"] = _jax.random.normal(_jax.random.fold_in(key, 0), shape, _jnp.float32)
    for i, name in enumerate(TWIN_WEIGHTS):
        w = inp[name].astype(_jnp.float32)
        if MOMENT_SCALE is None:
            s = _jnp.sqrt(_jnp.mean(_jnp.square(w)) + 1e-30)
        else:
            s = MOMENT_SCALE[name]
        km, kv = _jax.random.split(_jax.random.fold_in(key, i + 1))
        out[name] = w
        out["m_" + name] = s * _jax.random.normal(km, w.shape, _jnp.float32)
        out["v_" + name] = (s * s) * _jax.random.uniform(kv, w.shape, _jnp.float32, 0.5, 1.5)
    if N_MICROBATCH > 1:
        for name, axis in PER_EXAMPLE_BATCH_AXIS.items():
            out[name] = _to_microbatches(out[name], axis)
    return {'x': out['x'], 'p': out['p'], 'positions': out['positions'], 'pre_mix_norm': out['pre_mix_norm'], 'w_in': out['w_in'], 'ret_gn_w': out['ret_gn_w'], 'mla_q_norm': out['mla_q_norm'], 'w_uq': out['w_uq'], 'mla_kv_norm': out['mla_kv_norm'], 'w_ukv': out['w_ukv'], 'w_o': out['w_o'], 'post_mix_norm': out['post_mix_norm'], 'pre_ffn_norm': out['pre_ffn_norm'], 'w_gate': out['w_gate'], 'w_up': out['w_up'], 'w_down': out['w_down'], 'post_ffn_norm': out['post_ffn_norm'], 'w_ple_proj': out['w_ple_proj'], 'ple_norm': out['ple_norm'], 'w_ple_gate': out['w_ple_gate'], 'b_ple_gate': out['b_ple_gate'], 'loss_target': out['loss_target'], 'm_pre_mix_norm': out['m_pre_mix_norm'], 'm_w_in': out['m_w_in'], 'm_ret_gn_w': out['m_ret_gn_w'], 'm_mla_q_norm': out['m_mla_q_norm'], 'm_w_uq': out['m_w_uq'], 'm_mla_kv_norm': out['m_mla_kv_norm'], 'm_w_ukv': out['m_w_ukv'], 'm_w_o': out['m_w_o'], 'm_post_mix_norm': out['m_post_mix_norm'], 'm_pre_ffn_norm': out['m_pre_ffn_norm'], 'm_w_gate': out['m_w_gate'], 'm_w_up': out['m_w_up'], 'm_w_down': out['m_w_down'], 'm_post_ffn_norm': out['m_post_ffn_norm'], 'm_w_ple_proj': out['m_w_ple_proj'], 'm_ple_norm': out['m_ple_norm'], 'm_w_ple_gate': out['m_w_ple_gate'], 'm_b_ple_gate': out['m_b_ple_gate'], 'v_pre_mix_norm': out['v_pre_mix_norm'], 'v_w_in': out['v_w_in'], 'v_ret_gn_w': out['v_ret_gn_w'], 'v_mla_q_norm': out['v_mla_q_norm'], 'v_w_uq': out['v_w_uq'], 'v_mla_kv_norm': out['v_mla_kv_norm'], 'v_w_ukv': out['v_w_ukv'], 'v_w_o': out['v_w_o'], 'v_post_mix_norm': out['v_post_mix_norm'], 'v_pre_ffn_norm': out['v_pre_ffn_norm'], 'v_w_gate': out['v_w_gate'], 'v_w_up': out['v_w_up'], 'v_w_down': out['v_w_down'], 'v_post_ffn_norm': out['v_post_ffn_norm'], 'v_w_ple_proj': out['v_w_ple_proj'], 'v_ple_norm': out['v_ple_norm'], 'v_w_ple_gate': out['v_w_ple_gate'], 'v_b_ple_gate': out['v_b_ple_gate']}


def _loss(weights, diff, rest, loss_target):
    with _jax.named_scope("forward"):
        args = {**rest, TWIN_DIFF_INPUT: diff, **{k: w.astype(_WEIGHT_DTYPES[k]) for k, w in weights.items()}}
        y = _forward(args)
    with _jax.named_scope("loss_head"):
        err = _jnp.square(y.astype(_jnp.float32) - loss_target)
        return 0.5 * _jnp.sum(_jnp.mean(err, axis=-1)) if err.ndim else 0.5 * err


def _adamw(w, g, m, v):
    m = ADAM_B1 * m + (1.0 - ADAM_B1) * g
    v = ADAM_B2 * v + (1.0 - ADAM_B2) * _jnp.square(g)
    m_hat = m / (1.0 - ADAM_B1 ** ADAM_STEP)
    v_hat = v / (1.0 - ADAM_B2 ** ADAM_STEP)
    delta = -ADAM_LR * (m_hat / (_jnp.sqrt(v_hat) + ADAM_EPS) + ADAM_WD * w)
    return delta, m, v


def reference(x, p, positions, pre_mix_norm, w_in, ret_gn_w, mla_q_norm, w_uq, mla_kv_norm, w_ukv, w_o, post_mix_norm, pre_ffn_norm, w_gate, w_up, w_down, post_ffn_norm, w_ple_proj, ple_norm, w_ple_gate, b_ple_gate, loss_target, m_pre_mix_norm, m_w_in, m_ret_gn_w, m_mla_q_norm, m_w_uq, m_mla_kv_norm, m_w_ukv, m_w_o, m_post_mix_norm, m_pre_ffn_norm, m_w_gate, m_w_up, m_w_down, m_post_ffn_norm, m_w_ple_proj, m_ple_norm, m_w_ple_gate, m_b_ple_gate, v_pre_mix_norm, v_w_in, v_ret_gn_w, v_mla_q_norm, v_w_uq, v_mla_kv_norm, v_w_ukv, v_w_o, v_post_mix_norm, v_pre_ffn_norm, v_w_gate, v_w_up, v_w_down, v_post_ffn_norm, v_w_ple_proj, v_ple_norm, v_w_ple_gate, v_b_ple_gate):
    given = dict(x=x, p=p, positions=positions, pre_mix_norm=pre_mix_norm, w_in=w_in, ret_gn_w=ret_gn_w, mla_q_norm=mla_q_norm, w_uq=w_uq, mla_kv_norm=mla_kv_norm, w_ukv=w_ukv, w_o=w_o, post_mix_norm=post_mix_norm, pre_ffn_norm=pre_ffn_norm, w_gate=w_gate, w_up=w_up, w_down=w_down, post_ffn_norm=post_ffn_norm, w_ple_proj=w_ple_proj, ple_norm=ple_norm, w_ple_gate=w_ple_gate, b_ple_gate=b_ple_gate, loss_target=loss_target, m_pre_mix_norm=m_pre_mix_norm, m_w_in=m_w_in, m_ret_gn_w=m_ret_gn_w, m_mla_q_norm=m_mla_q_norm, m_w_uq=m_w_uq, m_mla_kv_norm=m_mla_kv_norm, m_w_ukv=m_w_ukv, m_w_o=m_w_o, m_post_mix_norm=m_post_mix_norm, m_pre_ffn_norm=m_pre_ffn_norm, m_w_gate=m_w_gate, m_w_up=m_w_up, m_w_down=m_w_down, m_post_ffn_norm=m_post_ffn_norm, m_w_ple_proj=m_w_ple_proj, m_ple_norm=m_ple_norm, m_w_ple_gate=m_w_ple_gate, m_b_ple_gate=m_b_ple_gate, v_pre_mix_norm=v_pre_mix_norm, v_w_in=v_w_in, v_ret_gn_w=v_ret_gn_w, v_mla_q_norm=v_mla_q_norm, v_w_uq=v_w_uq, v_mla_kv_norm=v_mla_kv_norm, v_w_ukv=v_w_ukv, v_w_o=v_w_o, v_post_mix_norm=v_post_mix_norm, v_pre_ffn_norm=v_pre_ffn_norm, v_w_gate=v_w_gate, v_w_up=v_w_up, v_w_down=v_w_down, v_post_ffn_norm=v_post_ffn_norm, v_w_ple_proj=v_w_ple_proj, v_ple_norm=v_ple_norm, v_w_ple_gate=v_w_ple_gate, v_b_ple_gate=v_b_ple_gate)
    weights = {n: given[n] for n in TWIN_WEIGHTS}
    shared = {n: given[n] for n in SHARED_INPUTS}
    per_example = {n: given[n] for n in ['x', 'p', 'positions']}
    grad_fn = _jax.value_and_grad(_loss, argnums=(0, 1))

    def one_microbatch(ex, loss_target):
        ex = dict(ex)
        diff = ex.pop(TWIN_DIFF_INPUT)
        return grad_fn(weights, diff, {**shared, **ex}, loss_target)

    if N_MICROBATCH == 1:
        loss, (grad_w, grad_x) = one_microbatch(per_example, given["loss_target"])
    else:
        def body(carry, xs):
            loss_sum, grad_sum = carry
            l_k, (gw_k, gx_k) = one_microbatch(xs[0], xs[1])
            with _jax.named_scope("update"):
                return (loss_sum + l_k, _jax.tree.map(_jnp.add, grad_sum, gw_k)), gx_k

        init = (_jnp.zeros((), _jnp.float32), _jax.tree.map(_jnp.zeros_like, weights))
        (loss, grad_w), grad_x = _jax.lax.scan(body, init, (per_example, given["loss_target"]))
    with _jax.named_scope("update"):
        delta_w, new_m, new_v = {}, {}, {}
        for n in TWIN_WEIGHTS:
            delta_w[n], new_m[n], new_v[n] = _adamw(weights[n], grad_w[n], given["m_" + n], given["v_" + n])
    return (loss, grad_x, *[grad_w[n] for n in TWIN_WEIGHTS], *[delta_w[n] for n in TWIN_WEIGHTS],
            *[new_m[n] for n in TWIN_WEIGHTS], *[new_v[n] for n in TWIN_WEIGHTS])
```

```python
import functools
import math

import jax
import jax.numpy as jnp
import numpy as np
from jax import lax
from jax.experimental import pallas as pl
from jax.experimental.pallas import tpu as pltpu

F32 = jnp.float32
BF16 = jnp.bfloat16
MESH = pl.DeviceIdType.MESH

D_MODEL = 1024
D_FF = 2816
PLE_DIM = 256
RET_HEADS = 4
RET_DIM = 128
RET_WIDTH = 512
RET_CHUNK = 128
MLA_HEADS = 8
MLA_NOPE = 64
MLA_ROPE = 32
MLA_V = 64
Q_LORA = 384
KV_LORA = 256
IN_COLS = 2720
IN_COLS_P = 2816
ROPE_BASE = 10000.0
EPS = 1e-6
SCALE_MLA = 1.0 / math.sqrt(MLA_NOPE + MLA_ROPE)
SCALE_RET = RET_DIM ** -0.5
NEG = -1e30

ADAM_LR = 0.001
ADAM_B1 = 0.9
ADAM_B2 = 0.999
ADAM_EPS = 1e-08
ADAM_WD = 0.01
ADAM_STEP = 10

N_CHIPS = 4
N_DEV = 8
VMEM_MB = 56

BIG = (
    ("w_in", 1024, 2720, 1),
    ("w_uq", 384, 768, 1),
    ("w_ukv", 256, 1024, 1),
    ("w_o", 1024, 1024, 0),
    ("w_gate", 1024, 2816, 1),
    ("w_up", 1024, 2816, 1),
    ("w_down", 2816, 1024, 0),
    ("w_ple_proj", 256, 1024, 1),
    ("w_ple_gate", 1024, 1024, 0),
)
SMALL = (
    ("pre_mix_norm", 1024),
    ("ret_gn_w", 512),
    ("mla_q_norm", 384),
    ("mla_kv_norm", 256),
    ("post_mix_norm", 1024),
    ("pre_ffn_norm", 1024),
    ("post_ffn_norm", 1024),
    ("ple_norm", 1024),
    ("b_ple_gate", 1024),
)
ALL_W = ("pre_mix_norm", "w_in", "ret_gn_w", "mla_q_norm", "w_uq", "mla_kv_norm", "w_ukv", "w_o", "post_mix_norm",
         "pre_ffn_norm", "w_gate", "w_up", "w_down", "post_ffn_norm", "w_ple_proj", "ple_norm", "w_ple_gate", "b_ple_gate")
PACK_COLS = 1024
SHARD_ELEMS = sum(r * c for _, r, c, _ in BIG) // N_CHIPS
PACK_ROWS = -(-SHARD_ELEMS // PACK_COLS // 32) * 32
HALF_ROWS = PACK_ROWS // 2
SMALL_ROWS = 16


def _cp(sem=None, mb=VMEM_MB, **kw):
    return pltpu.CompilerParams(dimension_semantics=sem, vmem_limit_bytes=mb * 1024 * 1024, **kw)


def _bf(x):
    return x.astype(BF16)


def _dot(a, b):
    return jnp.dot(_bf(a), _bf(b), preferred_element_type=F32)


def _dot_nt(a, b):
    return lax.dot_general(_bf(a), _bf(b), (((1,), (1,)), ((), ())), preferred_element_type=F32)


def _dot_tn(a, b):
    return lax.dot_general(_bf(a), _bf(b), (((0,), (0,)), ((), ())), preferred_element_type=F32)


def _sig(x):
    return 1.0 / (1.0 + jnp.exp(-x))


def _rms(x, g):
    r = lax.rsqrt(jnp.mean(x * x, axis=-1, keepdims=True) + EPS)
    return x * r * g


def _rms_bwd(dy, x, g):
    r = lax.rsqrt(jnp.mean(x * x, axis=-1, keepdims=True) + EPS)
    xh = x * r
    dxh = dy * g
    dx = r * (dxh - xh * jnp.mean(dxh * xh, axis=-1, keepdims=True))
    return dx, dy * xh


def _colsum(x):
    return jnp.sum(x, axis=0, keepdims=True)


def _rope_ret(x, cr, sr):
    return x * cr + pltpu.roll(x, 64, 1) * sr


def _unrope_ret(dy, cr, sr):
    return dy * cr + pltpu.roll(dy * sr, 64, 1)


def _rope_mla(x, cm, sa, sb):
    return x * cm + pltpu.roll(x, 112, 1) * sa + pltpu.roll(x, 16, 1) * sb


def _unrope_mla(dy, cm, sa, sb):
    return dy * cm + pltpu.roll(dy * sa, 16, 1) + pltpu.roll(dy * sb, 112, 1)


def _rows(tm, w, col=0):
    return pl.BlockSpec((tm, w), lambda i: (i, col))


def _full(*shape):
    return pl.BlockSpec(shape, lambda i: (0,) * len(shape))


def _sds(shape, dtype):
    return jax.ShapeDtypeStruct(shape, dtype)


def _rope_tables(pos_f, S):
    tm = min(512, S)
    inv_r = (1.0 / (np.float32(ROPE_BASE) ** (np.arange(64, dtype=np.float32) / np.float32(64)))).astype(np.float32)
    inv_m16 = (1.0 / (np.float32(ROPE_BASE) ** (np.arange(16, dtype=np.float32) / np.float32(16)))).astype(np.float32)
    inv_r = np.concatenate([inv_r, inv_r])[None, :]
    inv_m = np.zeros((1, 128), np.float32)
    inv_m[0, 64:80] = inv_m16
    inv_m[0, 80:96] = inv_m16

    def body(pos_ref, invr_ref, invm_ref, cr_ref, sr_ref, cm_ref, sa_ref, sb_ref):
        pos = pos_ref[...]
        lane = lax.broadcasted_iota(jnp.int32, (tm, 128), 1)
        ar = pos * invr_ref[...]
        s = jnp.sin(ar)
        cr_ref[...] = jnp.cos(ar)
        sr_ref[...] = jnp.where(lane < 64, -s, s)
        am = pos * invm_ref[...]
        c2 = jnp.cos(am)
        s2 = jnp.sin(am)
        cm_ref[...] = jnp.where(lane < 64, 1.0, jnp.where(lane < 96, c2, 0.0))
        sa_ref[...] = jnp.where((lane >= 64) & (lane < 80), -s2, 0.0)
        sb_ref[...] = jnp.where((lane >= 80) & (lane < 96), s2, 0.0)

    return pl.pallas_call(
        body, name="rope_tables", grid=(S // tm,),
        in_specs=[_rows(tm, 1), _full(1, 128), _full(1, 128)],
        out_specs=[_rows(tm, 128)] * 5,
        out_shape=[_sds((S, 128), F32)] * 5,
        compiler_params=_cp(("parallel",)),
    )(pos_f, jnp.asarray(inv_r), jnp.asarray(inv_m))


def _rms_fwd(x, g, S):
    tm = min(512, S)

    def body(x_ref, g_ref, o_ref):
        o_ref[...] = _rms(x_ref[...], g_ref[...]).astype(BF16)

    return pl.pallas_call(
        body, name="rms_pre", grid=(S // tm,),
        in_specs=[_rows(tm, D_MODEL), _full(1, D_MODEL)],
        out_specs=_rows(tm, D_MODEL), out_shape=_sds((S, D_MODEL), BF16),
        compiler_params=_cp(("parallel",)),
    )(x, g)


def _inproj(xn, w_in, tabs, S):
    tm = min(256, S)

    def body(xn_ref, w_ref, cr_ref, sr_ref, cm_ref, sa_ref, sb_ref, rq_ref, rk_ref, rv_ref, rg_ref, cq_ref, ckv_ref, kr_ref):
        xb = xn_ref[...]
        cr = cr_ref[...]
        sr = sr_ref[...]
        q = jnp.dot(xb, w_ref[:, 0:512], preferred_element_type=F32)
        k = jnp.dot(xb, w_ref[:, 512:1024], preferred_element_type=F32)
        for h in range(RET_HEADS):
            sl = slice(h * 128, (h + 1) * 128)
            rq_ref[:, sl] = _rope_ret(q[:, sl], cr, sr).astype(BF16)
            rk_ref[:, sl] = (_rope_ret(k[:, sl], cr, sr) * SCALE_RET).astype(BF16)
        rv_ref[...] = jnp.dot(xb, w_ref[:, 1024:1536], preferred_element_type=F32).astype(BF16)
        rg_ref[...] = jnp.dot(xb, w_ref[:, 1536:2048], preferred_element_type=F32)
        cq_ref[...] = jnp.dot(xb, w_ref[:, 2048:2432], preferred_element_type=F32)
        ckv_ref[...] = jnp.dot(xb, w_ref[:, 2432:2688], preferred_element_type=F32)
        kr = jnp.dot(xb, w_ref[:, 2688:2816], preferred_element_type=F32)
        kr_ref[...] = _rope_mla(kr, cm_ref[...], sa_ref[...], sb_ref[...])

    return pl.pallas_call(
        body, name="inproj", grid=(S // tm,),
        in_specs=[_rows(tm, D_MODEL), _full(D_MODEL, IN_COLS_P)] + [_rows(tm, 128)] * 5,
        out_specs=[_rows(tm, 512)] * 4 + [_rows(tm, Q_LORA), _rows(tm, KV_LORA), _rows(tm, 128)],
        out_shape=[_sds((S, 512), BF16)] * 3 + [_sds((S, 512), F32), _sds((S, Q_LORA), F32), _sds((S, KV_LORA), F32),
                                                  _sds((S, 128), F32)],
        compiler_params=_cp(("parallel",)),
    )(xn, w_in, *tabs)


def _mla_up(cq, ckv, kr, gq, gkv, w_uq, w_ukv, tabs, S):
    tm = min(256, S)

    def body(cq_ref, ckv_ref, kr_ref, gq_ref, gkv_ref, wuq_ref, wukv_ref, cm_ref, sa_ref, sb_ref,
             cqn_ref, ckvn_ref, qp_ref, kp_ref, v_ref):
        cm = cm_ref[...]
        sa = sa_ref[...]
        sb = sb_ref[...]
        cqn = _rms(cq_ref[...], gq_ref[...]).astype(BF16)
        cqn_ref[...] = cqn
        ckvn = _rms(ckv_ref[...], gkv_ref[...]).astype(BF16)
        ckvn_ref[...] = ckvn
        qh = jnp.dot(cqn, wuq_ref[...], preferred_element_type=F32)
        kv = jnp.dot(ckvn, wukv_ref[...], preferred_element_type=F32)
        kr_blk = kr_ref[...]
        for h in range(MLA_HEADS):
            sl = slice(h * 128, (h + 1) * 128)
            qp_ref[:, sl] = (_rope_mla(qh[:, sl], cm, sa, sb) * SCALE_MLA).astype(BF16)
            kp_ref[:, sl] = (kv[:, sl] + kr_blk).astype(BF16)
        v_ref[...] = kv[:, 1024:1536].astype(BF16)

    return pl.pallas_call(
        body, name="mla_up", grid=(S // tm,),
        in_specs=[_rows(tm, Q_LORA), _rows(tm, KV_LORA), _rows(tm, 128), _full(1, Q_LORA), _full(1, KV_LORA),
                  _full(Q_LORA, 1024), _full(KV_LORA, 1536)] + [_rows(tm, 128)] * 3,
        out_specs=[_rows(tm, Q_LORA), _rows(tm, KV_LORA), _rows(tm, 1024), _rows(tm, 1024), _rows(tm, 512)],
        out_shape=[_sds((S, Q_LORA), BF16), _sds((S, KV_LORA), BF16), _sds((S, 1024), BF16), _sds((S, 1024), BF16),
                   _sds((S, 512), BF16)],
        compiler_params=_cp(("parallel",)),
    )(cq, ckv, kr, gq, gkv, w_uq, w_ukv, *tabs[2:])


def _tri_pairs(nq, k_major):
    if k_major:
        pairs = [(qb, kb) for kb in range(nq) for qb in range(kb, nq)]
    else:
        pairs = [(qb, kb) for qb in range(nq) for kb in range(qb + 1)]
    qb_of = np.array([p[0] for p in pairs], np.int32)
    kb_of = np.array([p[1] for p in pairs], np.int32)
    return jnp.asarray(qb_of), jnp.asarray(kb_of), len(pairs)


def _flash_fwd(qp, kp, v, S):
    tq = min(512, S)
    nq = S // tq
    qb_of, kb_of, T = _tri_pairs(nq, k_major=False)

    def body(qb_ref, kb_ref, q_ref, k_ref, v_ref, o_ref, lse_ref, m_sc, l_sc, acc_sc):
        t = pl.program_id(1)
        qb = qb_ref[t]
        kb = kb_ref[t]

        @pl.when(kb == 0)
        def _():
            m_sc[...] = jnp.full(m_sc.shape, NEG, F32)
            l_sc[...] = jnp.zeros(l_sc.shape, F32)
            acc_sc[...] = jnp.zeros(acc_sc.shape, F32)

        lane = lax.broadcasted_iota(jnp.int32, (tq, 128), 1)
        first = lane < 64

        def step(masked):
            vv = v_ref[...]
            pv = []
            alpha = []
            for a in range(2):
                sl = slice(a * 128, (a + 1) * 128)
                s = _dot_nt(q_ref[:, sl], k_ref[:, sl])
                if masked:
                    row = lax.broadcasted_iota(jnp.int32, (tq, tq), 0)
                    col = lax.broadcasted_iota(jnp.int32, (tq, tq), 1)
                    s = jnp.where(col <= row, s, NEG)
                m_prev = m_sc[a][:, 0:1]
                m_new = jnp.maximum(m_prev, jnp.max(s, axis=1, keepdims=True))
                al = jnp.exp(m_prev - m_new)
                p = jnp.exp(s - m_new)
                l_sc[a] = jnp.broadcast_to(al * l_sc[a][:, 0:1] + jnp.sum(p, axis=1, keepdims=True), (tq, 128))
                m_sc[a] = jnp.broadcast_to(m_new, (tq, 128))
                pv.append(jnp.dot(p.astype(BF16), vv, preferred_element_type=F32))
                alpha.append(jnp.broadcast_to(al, (tq, 128)))
            acc_sc[...] = acc_sc[...] * jnp.where(first, alpha[0], alpha[1]) + jnp.where(first, pv[0], pv[1])

        @pl.when(kb < qb)
        def _():
            step(False)

        @pl.when(kb == qb)
        def _():
            step(True)
            l_sel = jnp.where(first, l_sc[0], l_sc[1])
            o_ref[...] = (acc_sc[...] / l_sel).astype(BF16)
            lse_ref[...] = jnp.where(first, m_sc[0], m_sc[1]) + jnp.log(l_sel)

    grid_spec = pltpu.PrefetchScalarGridSpec(
        num_scalar_prefetch=2, grid=(MLA_HEADS // 2, T),
        in_specs=[pl.BlockSpec((tq, 256), lambda j, t, qb, kb: (qb[t], j)),
                  pl.BlockSpec((tq, 256), lambda j, t, qb, kb: (kb[t], j)),
                  pl.BlockSpec((tq, 128), lambda j, t, qb, kb: (kb[t], j))],
        out_specs=[pl.BlockSpec((tq, 128), lambda j, t, qb, kb: (qb[t], j)),
                   pl.BlockSpec((tq, 128), lambda j, t, qb, kb: (qb[t], j))],
        scratch_shapes=[pltpu.VMEM((2, tq, 128), F32), pltpu.VMEM((2, tq, 128), F32), pltpu.VMEM((tq, 128), F32)],
    )
    return pl.pallas_call(
        body, name="flash_fwd", grid_spec=grid_spec,
        out_shape=[_sds((S, 512), BF16), _sds((S, 512), F32)],
        compiler_params=_cp(("parallel", "arbitrary")),
    )(qb_of, kb_of, qp, kp, v)


def _decay_table():
    log_g = np.log(1.0 - 2.0 ** (-5.0 - np.arange(RET_HEADS, dtype=np.float32))).astype(np.float32)
    return jnp.asarray(np.broadcast_to(log_g[:, None, None], (RET_HEADS, 8, 128)).copy())


def _decay_terms(lg_ref):
    C = RET_CHUNK
    lg = lg_ref[0:1, :]
    row = lax.broadcasted_iota(jnp.int32, (C, C), 0)
    col = lax.broadcasted_iota(jnp.int32, (C, C), 1)
    diff = (row - col).astype(F32)
    dmat = jnp.where(diff >= 0, jnp.exp(jnp.maximum(diff, 0.0) * lg), 0.0)
    j = lax.broadcasted_iota(jnp.int32, (C, 1), 0).astype(F32)
    lg1 = lg[:, 0:1]
    zeta = jnp.exp((C - 1 - j) * lg1)
    xi = jnp.exp((j + 1.0) * lg1)
    g_chunk = jnp.exp(C * lg1)
    return dmat, zeta, xi, g_chunk


def _ret_fwd(rq, rk, rv, rg, gn_w, S):
    C = RET_CHUNK
    N = S // C

    def body(lg_ref, q_ref, k_ref, v_ref, rg_ref, w_ref, ry_ref, ro_ref, rprev_ref, r_sc):
        n = pl.program_id(1)

        @pl.when(n == 0)
        def _():
            r_sc[...] = jnp.zeros(r_sc.shape, F32)

        dmat, zeta, xi, g_chunk = _decay_terms(lg_ref)
        q = q_ref[...]
        k = k_ref[...]
        v = v_ref[...]
        r_prev = r_sc[...].astype(BF16)
        rprev_ref[...] = r_prev
        sc = _dot_nt(q, k) * dmat
        ry = _dot(sc, v) + jnp.dot(q, r_prev, preferred_element_type=F32) * xi
        ry_ref[...] = ry
        r_sc[...] = g_chunk * r_sc[...] + _dot_tn(k, zeta * v.astype(F32))
        mu = jnp.mean(ry, axis=-1, keepdims=True)
        yc = ry - mu
        yh = yc * lax.rsqrt(jnp.mean(yc * yc, axis=-1, keepdims=True) + EPS)
        g = rg_ref[...]
        ro_ref[...] = (g * _sig(g) * (yh * w_ref[...])).astype(BF16)

    blk = lambda off: pl.BlockSpec((C, 128), lambda h, n: (n, off + h))
    return pl.pallas_call(
        body, name="ret_fwd", grid=(RET_HEADS, N),
        in_specs=[pl.BlockSpec((None, 8, 128), lambda h, n: (h, 0, 0)), blk(0), blk(0), blk(0), blk(0),
                  pl.BlockSpec((1, 128), lambda h, n: (0, h))],
        out_specs=[blk(0), blk(0), pl.BlockSpec((None, 128, 128), lambda h, n: (h * N + n, 0, 0))],
        out_shape=[_sds((S, 512), F32), _sds((S, 512), BF16), _sds((RET_HEADS * N, 128, 128), BF16)],
        scratch_shapes=[pltpu.VMEM((128, 128), F32)],
        compiler_params=_cp(("parallel", "arbitrary")),
    )(_decay_table(), rq, rk, rv, rg, gn_w)


def _outproj(ro, mo, x, w_o, g_post, g_pre, S):
    tm = min(256, S)

    def body(ro_ref, mo_ref, x_ref, wo_ref, g1_ref, g2_ref, mix_ref, h1_ref, hn_ref):
        mix = (jnp.dot(ro_ref[...], wo_ref[0:512, :], preferred_element_type=F32)
               + jnp.dot(mo_ref[...], wo_ref[512:1024, :], preferred_element_type=F32))
        mix_ref[...] = mix
        h1 = x_ref[...] + _rms(mix, g1_ref[...])
        h1_ref[...] = h1
        hn_ref[...] = _rms(h1, g2_ref[...]).astype(BF16)

    return pl.pallas_call(
        body, name="outproj", grid=(S // tm,),
        in_specs=[_rows(tm, 512), _rows(tm, 512), _rows(tm, D_MODEL), _full(D_MODEL, D_MODEL), _full(1, D_MODEL),
                  _full(1, D_MODEL)],
        out_specs=[_rows(tm, D_MODEL)] * 3,
        out_shape=[_sds((S, D_MODEL), F32), _sds((S, D_MODEL), F32), _sds((S, D_MODEL), BF16)],
        compiler_params=_cp(("parallel",)),
    )(ro, mo, x, w_o, g_post, g_pre)


def _ffn_up(hn, w_gate, w_up, S):
    tm = min(512, S)
    tn = D_FF // 2

    def body(hn_ref, wg_ref, wu_ref, gate_ref, up_ref, act_ref):
        hn_b = hn_ref[...]
        g = jnp.dot(hn_b, wg_ref[...], preferred_element_type=F32)
        u = jnp.dot(hn_b, wu_ref[...], preferred_element_type=F32)
        gate_ref[...] = g.astype(BF16)
        up_ref[...] = u.astype(BF16)
        act_ref[...] = (g * _sig(g) * u).astype(BF16)

    wspec = pl.BlockSpec((D_MODEL, tn), lambda j, i: (0, j))
    ospec = pl.BlockSpec((tm, tn), lambda j, i: (i, j))
    return pl.pallas_call(
        body, name="ffn_up", grid=(2, S // tm),
        in_specs=[pl.BlockSpec((tm, D_MODEL), lambda j, i: (i, 0)), wspec, wspec],
        out_specs=[ospec] * 3, out_shape=[_sds((S, D_FF), BF16)] * 3,
        compiler_params=_cp(("parallel", "parallel")),
    )(hn, w_gate, w_up)


def _ffn_down(act, w_down, h1, g, S):
    tm = min(256, S)

    def body(act_ref, wd_ref, h1_ref, g_ref, ff_ref, h2_ref):
        ff = jnp.dot(act_ref[...], wd_ref[...], preferred_element_type=F32)
        ff_ref[...] = ff
        h2_ref[...] = h1_ref[...] + _rms(ff, g_ref[...])

    return pl.pallas_call(
        body, name="ffn_down", grid=(S // tm,),
        in_specs=[_rows(tm, D_FF), _full(D_FF, D_MODEL), _rows(tm, D_MODEL), _full(1, D_MODEL)],
        out_specs=[_rows(tm, D_MODEL)] * 2, out_shape=[_sds((S, D_MODEL), F32)] * 2,
        compiler_params=_cp(("parallel",)),
    )(act, w_down, h1, g)


def _ple_loss(p, h2, tgt, w_pp, w_pg, b_pg, g_ple, S):
    tm = min(256, S)

    def body(p_ref, h2_ref, t_ref, wp_ref, wg_ref, b_ref, gp_ref,
             dz_ref, dpe_ref, dh2_ref, h2b_ref, loss_ref, dgp_ref, db_ref):
        @pl.when(pl.program_id(0) == 0)
        def _():
            loss_ref[...] = jnp.zeros(loss_ref.shape, F32)
            dgp_ref[...] = jnp.zeros(dgp_ref.shape, F32)
            db_ref[...] = jnp.zeros(db_ref.shape, F32)

        gp = gp_ref[...]
        pe = _dot(p_ref[...], wp_ref[...])
        r = lax.rsqrt(jnp.mean(pe * pe, axis=-1, keepdims=True) + EPS)
        peh = pe * r
        e = peh * gp
        h2 = h2_ref[...]
        h2b = h2.astype(BF16)
        h2b_ref[...] = h2b
        gt = _sig(jnp.dot(h2b, wg_ref[...], preferred_element_type=F32) + b_ref[...])
        diff = h2 + e * gt - t_ref[...]
        loss_ref[...] += _colsum(diff * diff)
        dh3 = diff * (1.0 / D_MODEL)
        de = dh3 * gt
        dz = dh3 * e * gt * (1.0 - gt)
        db_ref[...] += _colsum(dz)
        dgp_ref[...] += _colsum(de * peh)
        dpeh = de * gp
        dpe = r * (dpeh - peh * jnp.mean(dpeh * peh, axis=-1, keepdims=True))
        dzb = dz.astype(BF16)
        dz_ref[...] = dzb
        dpe_ref[...] = dpe.astype(BF16)
        dh2_ref[...] = dh3 + _dot_nt(dzb, wg_ref[...])

    return pl.pallas_call(
        body, name="ple_loss", grid=(S // tm,),
        in_specs=[_rows(tm, PLE_DIM), _rows(tm, D_MODEL), _rows(tm, D_MODEL), _full(PLE_DIM, D_MODEL),
                  _full(D_MODEL, D_MODEL), _full(1, D_MODEL), _full(1, D_MODEL)],
        out_specs=[_rows(tm, D_MODEL)] * 4 + [_full(1, D_MODEL)] * 3,
        out_shape=[_sds((S, D_MODEL), BF16), _sds((S, D_MODEL), BF16), _sds((S, D_MODEL), F32), _sds((S, D_MODEL), BF16)]
        + [_sds((1, D_MODEL), F32)] * 3,
        compiler_params=_cp(("arbitrary",)),
    )(p, h2, tgt, w_pp, w_pg, b_pg, g_ple)


def _wgrad(a, b, name, S):
    M = a.shape[1]
    N = b.shape[1]
    ts = min(512, S)
    nsplit = 2 if M * N >= 2 * 1024 * 1024 else 1
    tn = N // nsplit

    def body(a_ref, b_ref, o_ref):
        @pl.when(pl.program_id(1) == 0)
        def _():
            o_ref[...] = jnp.zeros(o_ref.shape, F32)

        o_ref[...] += _dot_tn(a_ref[...], b_ref[...])

    return pl.pallas_call(
        body, name=name, grid=(nsplit, S // ts),
        in_specs=[pl.BlockSpec((ts, M), lambda j, s: (s, 0)), pl.BlockSpec((ts, tn), lambda j, s: (s, j))],
        out_specs=pl.BlockSpec((M, tn), lambda j, s: (0, j)), out_shape=_sds((M, N), F32),
        compiler_params=_cp(("parallel", "arbitrary")),
    )(a, b)


def _ffn_down_bwd(dh2, ff, g, w_down, gate, up, S):
    tm = min(256, S)
    tn = D_FF // 2

    def body(dh2_ref, ff_ref, g_ref, wd_ref, gate_ref, up_ref, dff_ref, dgate_ref, dup_ref, dg_ref):
        @pl.when(pl.program_id(0) == 0)
        def _():
            dg_ref[...] = jnp.zeros(dg_ref.shape, F32)

        dff, ga = _rms_bwd(dh2_ref[...], ff_ref[...], g_ref[...])
        dg_ref[...] += _colsum(ga)
        dffb = dff.astype(BF16)
        dff_ref[...] = dffb
        for seg in range(2):
            sl = slice(seg * tn, (seg + 1) * tn)
            dact = _dot_nt(dffb, wd_ref[sl, :])
            gt = gate_ref[:, sl].astype(F32)
            u = up_ref[:, sl].astype(F32)
            s = _sig(gt)
            dgate_ref[:, sl] = (dact * u * (s * (1.0 + gt * (1.0 - s)))).astype(BF16)
            dup_ref[:, sl] = (dact * (gt * s)).astype(BF16)

    return pl.pallas_call(
        body, name="ffn_down_bwd", grid=(S // tm,),
        in_specs=[_rows(tm, D_MODEL), _rows(tm, D_MODEL), _full(1, D_MODEL), _full(D_FF, D_MODEL), _rows(tm, D_FF),
                  _rows(tm, D_FF)],
        out_specs=[_rows(tm, D_MODEL), _rows(tm, D_FF), _rows(tm, D_FF), _full(1, D_MODEL)],
        out_shape=[_sds((S, D_MODEL), BF16), _sds((S, D_FF), BF16), _sds((S, D_FF), BF16), _sds((1, D_MODEL), F32)],
        compiler_params=_cp(("arbitrary",)),
    )(dh2, ff, g, w_down, gate, up)


def _ffn_up_bwd(dgate, dup, w_gate, w_up, h1, mix, dh2, g_pre, g_post, w_o, S):
    tm = min(256, S)

    def body(dgate_ref, dup_ref, wg_ref, wu_ref, h1_ref, mix_ref, dh2_ref, g2_ref, g1_ref, wo_ref,
             dh1_ref, dmix_ref, dro_ref, dmo_ref, dg2_ref, dg1_ref):
        @pl.when(pl.program_id(0) == 0)
        def _():
            dg2_ref[...] = jnp.zeros(dg2_ref.shape, F32)
            dg1_ref[...] = jnp.zeros(dg1_ref.shape, F32)

        dhn = _dot_nt(dgate_ref[...], wg_ref[...]) + _dot_nt(dup_ref[...], wu_ref[...])
        d1, ga = _rms_bwd(dhn, h1_ref[...], g2_ref[...])
        dg2_ref[...] += _colsum(ga)
        dh1 = dh2_ref[...] + d1
        dh1_ref[...] = dh1
        dmix, gb = _rms_bwd(dh1, mix_ref[...], g1_ref[...])
        dg1_ref[...] += _colsum(gb)
        dmixb = dmix.astype(BF16)
        dmix_ref[...] = dmixb
        dcat = _dot_nt(dmixb, wo_ref[...])
        dro_ref[...] = dcat[:, 0:512].astype(BF16)
        dmo_ref[...] = dcat[:, 512:1024].astype(BF16)

    return pl.pallas_call(
        body, name="ffn_up_bwd", grid=(S // tm,),
        in_specs=[_rows(tm, D_FF), _rows(tm, D_FF), _full(D_MODEL, D_FF), _full(D_MODEL, D_FF), _rows(tm, D_MODEL),
                  _rows(tm, D_MODEL), _rows(tm, D_MODEL), _full(1, D_MODEL), _full(1, D_MODEL), _full(D_MODEL, D_MODEL)],
        out_specs=[_rows(tm, D_MODEL), _rows(tm, D_MODEL), _rows(tm, 512), _rows(tm, 512), _full(1, D_MODEL),
                   _full(1, D_MODEL)],
        out_shape=[_sds((S, D_MODEL), F32), _sds((S, D_MODEL), BF16), _sds((S, 512), BF16), _sds((S, 512), BF16),
                   _sds((1, D_MODEL), F32), _sds((1, D_MODEL), F32)],
        compiler_params=_cp(("arbitrary",)),
    )(dgate, dup, w_gate, w_up, h1, mix, dh2, g_pre, g_post, w_o)


def _attn_delta(o, do, S):
    tm = min(512, S)

    def body(o_ref, do_ref, d_ref):
        prod = o_ref[...].astype(F32) * do_ref[...].astype(F32)
        for h in range(MLA_HEADS):
            sl = slice(h * 64, (h + 1) * 64)
            d_ref[:, sl] = jnp.broadcast_to(jnp.sum(prod[:, sl], axis=1, keepdims=True), (tm, 64))

    return pl.pallas_call(
        body, name="attn_delta", grid=(S // tm,),
        in_specs=[_rows(tm, 512), _rows(tm, 512)], out_specs=_rows(tm, 512), out_shape=_sds((S, 512), F32),
        compiler_params=_cp(("parallel",)),
    )(o, do)


def _flash_bwd(qp, kp, v, do, lse, delta, S):
    tq = min(512, S)
    nq = S // tq
    qb_of, kb_of, T = _tri_pairs(nq, k_major=True)

    def body(qb_ref, kb_ref, q_ref, k_ref, v_ref, do_ref, lse_ref, dl_ref, dq_ref, dk_ref, dv_ref, dk_sc, dv_sc):
        t = pl.program_id(1)
        qb = qb_ref[t]
        kb = kb_ref[t]

        @pl.when(t == 0)
        def _():
            dq_ref[...] = jnp.zeros(dq_ref.shape, F32)

        @pl.when(qb == kb)
        def _():
            dk_sc[...] = jnp.zeros(dk_sc.shape, F32)
            dv_sc[...] = jnp.zeros(dv_sc.shape, F32)

        lane = lax.broadcasted_iota(jnp.int32, (tq, 128), 1)
        q0 = pl.multiple_of(qb * tq, tq)

        def step(masked):
            vv = v_ref[...]
            do_all = do_ref[...]
            for a in range(2):
                sl = slice(a * 128, (a + 1) * 128)
                q = q_ref[:, sl]
                k = k_ref[:, sl]
                s = _dot_nt(q, k)
                if masked:
                    row = lax.broadcasted_iota(jnp.int32, (tq, tq), 0)
                    col = lax.broadcasted_iota(jnp.int32, (tq, tq), 1)
                    s = jnp.where(col <= row, s, NEG)
                p = jnp.exp(s - lse_ref[:, a * 64:a * 64 + 1])
                do_a = jnp.where((lane < 64) if a == 0 else (lane >= 64), do_all, jnp.zeros_like(do_all))
                dp = _dot_nt(do_a, vv)
                ds = (p * (dp - dl_ref[:, a * 64:a * 64 + 1])).astype(BF16)
                dv_sc[...] += _dot_tn(p, do_a)
                dk_sc[:, sl] += _dot_tn(ds, q)
                dq_ref[pl.ds(q0, tq), sl] += jnp.dot(ds, k, preferred_element_type=F32)

        @pl.when(qb > kb)
        def _():
            step(False)

        @pl.when(qb == kb)
        def _():
            step(True)

        @pl.when(qb == nq - 1)
        def _():
            dk_ref[...] = dk_sc[...]
            dv_ref[...] = dv_sc[...]

    grid_spec = pltpu.PrefetchScalarGridSpec(
        num_scalar_prefetch=2, grid=(MLA_HEADS // 2, T),
        in_specs=[pl.BlockSpec((tq, 256), lambda j, t, qb, kb: (qb[t], j)),
                  pl.BlockSpec((tq, 256), lambda j, t, qb, kb: (kb[t], j)),
                  pl.BlockSpec((tq, 128), lambda j, t, qb, kb: (kb[t], j)),
                  pl.BlockSpec((tq, 128), lambda j, t, qb, kb: (qb[t], j)),
                  pl.BlockSpec((tq, 128), lambda j, t, qb, kb: (qb[t], j)),
                  pl.BlockSpec((tq, 128), lambda j, t, qb, kb: (qb[t], j))],
        out_specs=[pl.BlockSpec((S, 256), lambda j, t, qb, kb: (0, j)),
                   pl.BlockSpec((tq, 256), lambda j, t, qb, kb: (kb[t], j)),
                   pl.BlockSpec((tq, 128), lambda j, t, qb, kb: (kb[t], j))],
        scratch_shapes=[pltpu.VMEM((tq, 256), F32), pltpu.VMEM((tq, 128), F32)],
    )
    return pl.pallas_call(
        body, name="flash_bwd", grid_spec=grid_spec,
        out_shape=[_sds((S, 1024), F32), _sds((S, 1024), F32), _sds((S, 512), F32)],
        compiler_params=_cp(("parallel", "arbitrary")),
    )(qb_of, kb_of, qp, kp, v, do, lse, delta)


def _mla_up_bwd(dqp, dkp, dv, cq, ckv, gq, gkv, w_uq, w_ukv, tabs, S):
    tm = min(256, S)

    def body(dq_ref, dk_ref, dv_ref, cq_ref, ckv_ref, gq_ref, gkv_ref, wuq_ref, wukv_ref, cm_ref, sa_ref, sb_ref,
             dqh_ref, dkv_ref, dcq_ref, dckv_ref, dkr_ref, dgq_ref, dgkv_ref):
        @pl.when(pl.program_id(0) == 0)
        def _():
            dgq_ref[...] = jnp.zeros(dgq_ref.shape, F32)
            dgkv_ref[...] = jnp.zeros(dgkv_ref.shape, F32)

        cm = cm_ref[...]
        sa = sa_ref[...]
        sb = sb_ref[...]
        lane = lax.broadcasted_iota(jnp.int32, (tm, 128), 1)
        dkr_r = jnp.zeros((tm, 128), F32)
        for h in range(MLA_HEADS):
            sl = slice(h * 128, (h + 1) * 128)
            dqh_ref[:, sl] = (_unrope_mla(dq_ref[:, sl], cm, sa, sb) * SCALE_MLA).astype(BF16)
            gk = dk_ref[:, sl]
            dkr_r = dkr_r + gk
            dkv_ref[:, sl] = gk.astype(BF16)
        dkr_r = jnp.where((lane >= 64) & (lane < 96), dkr_r, 0.0)
        dkr_ref[...] = _unrope_mla(dkr_r, cm, sa, sb).astype(BF16)
        dkv_ref[:, 1024:1536] = dv_ref[...].astype(BF16)
        dcq, ga = _rms_bwd(_dot_nt(dqh_ref[...], wuq_ref[...]), cq_ref[...], gq_ref[...])
        dcq_ref[...] = dcq.astype(BF16)
        dgq_ref[...] += _colsum(ga)
        dckv, gb = _rms_bwd(_dot_nt(dkv_ref[...], wukv_ref[...]), ckv_ref[...], gkv_ref[...])
        dckv_ref[...] = dckv.astype(BF16)
        dgkv_ref[...] += _colsum(gb)

    return pl.pallas_call(
        body, name="mla_up_bwd", grid=(S // tm,),
        in_specs=[_rows(tm, 1024), _rows(tm, 1024), _rows(tm, 512), _rows(tm, Q_LORA), _rows(tm, KV_LORA),
                  _full(1, Q_LORA), _full(1, KV_LORA), _full(Q_LORA, 1024), _full(KV_LORA, 1536)] + [_rows(tm, 128)] * 3,
        out_specs=[_rows(tm, 1024), _rows(tm, 1536), _rows(tm, Q_LORA), _rows(tm, KV_LORA), _rows(tm, 128),
                   _full(1, Q_LORA), _full(1, KV_LORA)],
        out_shape=[_sds((S, 1024), BF16), _sds((S, 1536), BF16), _sds((S, Q_LORA), BF16), _sds((S, KV_LORA), BF16),
                   _sds((S, 128), BF16), _sds((1, Q_LORA), F32), _sds((1, KV_LORA), F32)],
        compiler_params=_cp(("arbitrary",)),
    )(dqp, dkp, dv, cq, ckv, gq, gkv, w_uq, w_ukv, *tabs[2:])


def _ret_bwd(rq, rk, rv, rprev, ry, rg, dro, gn_w, tabs, S):
    C = RET_CHUNK
    N = S // C

    def body(lg_ref, q_ref, k_ref, v_ref, rp_ref, ry_ref, rg_ref, dro_ref, w_ref, cr_ref, sr_ref,
             drq_ref, drk_ref, drv_ref, drg_ref, dw_ref, g_sc):
        n = pl.program_id(1)

        @pl.when(n == 0)
        def _():
            g_sc[...] = jnp.zeros(g_sc.shape, F32)
            dw_ref[...] = jnp.zeros(dw_ref.shape, F32)

        dmat, zeta, xi, g_chunk = _decay_terms(lg_ref)
        w = w_ref[...]
        ry = ry_ref[...]
        mu = jnp.mean(ry, axis=-1, keepdims=True)
        yc = ry - mu
        rstd = lax.rsqrt(jnp.mean(yc * yc, axis=-1, keepdims=True) + EPS)
        yh = yc * rstd
        g = rg_ref[...]
        s = _sig(g)
        dout = dro_ref[...].astype(F32)
        drg_ref[...] = (dout * (yh * w) * (s * (1.0 + g * (1.0 - s)))).astype(BF16)
        dgn = dout * (g * s)
        dw_ref[...] += _colsum(dgn * yh)
        dyh = dgn * w
        dry = rstd * (dyh - jnp.mean(dyh, axis=-1, keepdims=True) - yh * jnp.mean(dyh * yh, axis=-1, keepdims=True))
        do = dry.astype(BF16)

        q = q_ref[...]
        k = k_ref[...]
        v = v_ref[...]
        gfut = g_sc[...].astype(BF16)
        sc = (_dot_nt(q, k) * dmat).astype(BF16)
        dsc = (_dot_nt(do, v) * dmat).astype(BF16)
        dq = jnp.dot(dsc, k, preferred_element_type=F32) + _dot_nt(do, rp_ref[...]) * xi
        dk = _dot_tn(dsc, q) + _dot_nt(v, gfut) * zeta
        dv = _dot_tn(sc, do) + jnp.dot(k, gfut, preferred_element_type=F32) * zeta
        g_sc[...] = g_chunk * g_sc[...] + _dot_tn(q, xi * dry)
        cr = cr_ref[...]
        sr = sr_ref[...]
        drq_ref[...] = _unrope_ret(dq, cr, sr).astype(BF16)
        drk_ref[...] = _unrope_ret(dk * SCALE_RET, cr, sr).astype(BF16)
        drv_ref[...] = dv.astype(BF16)

    blk = pl.BlockSpec((C, 128), lambda h, n: (N - 1 - n, h))
    tab = pl.BlockSpec((C, 128), lambda h, n: (N - 1 - n, 0))
    return pl.pallas_call(
        body, name="ret_bwd", grid=(RET_HEADS, N),
        in_specs=[pl.BlockSpec((None, 8, 128), lambda h, n: (h, 0, 0)), blk, blk, blk,
                  pl.BlockSpec((None, 128, 128), lambda h, n: (h * N + N - 1 - n, 0, 0)), blk, blk, blk,
                  pl.BlockSpec((1, 128), lambda h, n: (0, h)), tab, tab],
        out_specs=[blk, blk, blk, blk, pl.BlockSpec((1, 128), lambda h, n: (0, h))],
        out_shape=[_sds((S, 512), BF16)] * 4 + [_sds((1, 512), F32)],
        scratch_shapes=[pltpu.VMEM((128, 128), F32)],
        compiler_params=_cp(("parallel", "arbitrary")),
    )(_decay_table(), rq, rk, rv, rprev, ry, rg, dro, gn_w, tabs[0], tabs[1])


def _inproj_bwd(drq, drk, drv, drg, dcq, dckv, dkr, w_in, dh1, x, g, S):
    tm = min(256, S)

    def body(drq_ref, drk_ref, drv_ref, drg_ref, dcq_ref, dckv_ref, dkr_ref, w_ref, dh1_ref, x_ref, g_ref,
             gx_ref, dproj_ref, dg_ref):
        @pl.when(pl.program_id(0) == 0)
        def _():
            dg_ref[...] = jnp.zeros(dg_ref.shape, F32)

        dproj_ref[:, 0:512] = drq_ref[...]
        dproj_ref[:, 512:1024] = drk_ref[...]
        dproj_ref[:, 1024:1536] = drv_ref[...]
        dproj_ref[:, 1536:2048] = drg_ref[...]
        dproj_ref[:, 2048:2432] = dcq_ref[...]
        dproj_ref[:, 2432:2688] = dckv_ref[...]
        dproj_ref[:, 2688:2816] = dkr_ref[...]
        dx, ga = _rms_bwd(_dot_nt(dproj_ref[...], w_ref[...]), x_ref[...], g_ref[...])
        gx_ref[...] = dh1_ref[...] + dx
        dg_ref[...] += _colsum(ga)

    return pl.pallas_call(
        body, name="inproj_bwd", grid=(S // tm,),
        in_specs=[_rows(tm, 512)] * 4 + [_rows(tm, Q_LORA), _rows(tm, KV_LORA), _rows(tm, 128),
                                         _full(D_MODEL, IN_COLS_P), _rows(tm, D_MODEL), _rows(tm, D_MODEL),
                                         _full(1, D_MODEL)],
        out_specs=[_rows(tm, D_MODEL), _rows(tm, IN_COLS_P), _full(1, D_MODEL)],
        out_shape=[_sds((S, D_MODEL), F32), _sds((S, IN_COLS_P), BF16), _sds((1, D_MODEL), F32)],
        compiler_params=_cp(("arbitrary",)),
    )(drq, drk, drv, drg, dcq, dckv, dkr, w_in, dh1, x, g)


def _pad_weights(w):
    w_in = w["w_in"]
    z = lambda r, c: jnp.zeros((r, c), BF16)
    w_in_p = jnp.concatenate([w_in[:, :2688], z(1024, 64), w_in[:, 2688:2720], z(1024, 32)], axis=1)
    w_uq_p = jnp.pad(w["w_uq"].reshape(Q_LORA, MLA_HEADS, 96), ((0, 0), (0, 0), (0, 32))).reshape(Q_LORA, 1024)
    ukv = w["w_ukv"].reshape(KV_LORA, MLA_HEADS, 128)
    k_part = jnp.pad(ukv[:, :, :64], ((0, 0), (0, 0), (0, 64))).reshape(KV_LORA, 1024)
    w_ukv_p = jnp.concatenate([k_part, ukv[:, :, 64:].reshape(KV_LORA, 512)], axis=1)
    return w_in_p, w_uq_p, w_ukv_p


def _local_step(x, p, pos_f, tgt, w, sm):
    S = x.shape[0]
    w_in_p, w_uq_p, w_ukv_p = _pad_weights(w)
    tabs = _rope_tables(pos_f, S)

    xn = _rms_fwd(x, sm["pre_mix_norm"], S)
    rq, rk, rv, rg, cq, ckv, kr = _inproj(xn, w_in_p, tabs, S)
    cqn, ckvn, qp, kp, v = _mla_up(cq, ckv, kr, sm["mla_q_norm"], sm["mla_kv_norm"], w_uq_p, w_ukv_p, tabs, S)
    mo, lse = _flash_fwd(qp, kp, v, S)
    ry, ro, rprev = _ret_fwd(rq, rk, rv, rg, sm["ret_gn_w"], S)
    mix, h1, hn = _outproj(ro, mo, x, w["w_o"], sm["post_mix_norm"], sm["pre_ffn_norm"], S)
    gate, up, act = _ffn_up(hn, w["w_gate"], w["w_up"], S)
    ff, h2 = _ffn_down(act, w["w_down"], h1, sm["post_ffn_norm"], S)
    dz, dpe, dh2, h2b, loss_vec, d_ple_norm, d_b = _ple_loss(
        p, h2, tgt, w["w_ple_proj"], w["w_ple_gate"], sm["b_ple_gate"], sm["ple_norm"], S)

    gw = {}
    gs = {"ple_norm": d_ple_norm, "b_ple_gate": d_b}
    gw["w_ple_gate"] = _wgrad(h2b, dz, "wgrad_ple_gate", S)
    gw["w_ple_proj"] = _wgrad(p, dpe, "wgrad_ple_proj", S)
    dff, dgate, dup, gs["post_ffn_norm"] = _ffn_down_bwd(dh2, ff, sm["post_ffn_norm"], w["w_down"], gate, up, S)
    gw["w_down"] = _wgrad(act, dff, "wgrad_down", S)
    gw["w_gate"] = _wgrad(hn, dgate, "wgrad_gate", S)
    gw["w_up"] = _wgrad(hn, dup, "wgrad_up", S)
    dh1, dmix, dro, dmo, gs["pre_ffn_norm"], gs["post_mix_norm"] = _ffn_up_bwd(
        dgate, dup, w["w_gate"], w["w_up"], h1, mix, dh2, sm["pre_ffn_norm"], sm["post_mix_norm"], w["w_o"], S)
    gw["w_o"] = jnp.concatenate([_wgrad(ro, dmix, "wgrad_o_ret", S), _wgrad(mo, dmix, "wgrad_o_mla", S)], axis=0)

    delta = _attn_delta(mo, dmo, S)
    dqp, dkp, dv = _flash_bwd(qp, kp, v, dmo, lse, delta, S)
    dqh, dkv, dcq, dckv, dkr, gs["mla_q_norm"], gs["mla_kv_norm"] = _mla_up_bwd(
        dqp, dkp, dv, cq, ckv, sm["mla_q_norm"], sm["mla_kv_norm"], w_uq_p, w_ukv_p, tabs, S)
    g_uq_p = _wgrad(cqn, dqh, "wgrad_uq", S)
    g_ukv_p = _wgrad(ckvn, dkv, "wgrad_ukv", S)
    gw["w_uq"] = g_uq_p.reshape(Q_LORA, MLA_HEADS, 128)[:, :, :96].reshape(Q_LORA, 768)
    gw["w_ukv"] = jnp.concatenate(
        [g_ukv_p[:, :1024].reshape(KV_LORA, MLA_HEADS, 128)[:, :, :64], g_ukv_p[:, 1024:].reshape(KV_LORA, MLA_HEADS, 64)],
        axis=2).reshape(KV_LORA, 1024)

    drq, drk, drv, drg, gs["ret_gn_w"] = _ret_bwd(rq, rk, rv, rprev, ry, rg, dro, sm["ret_gn_w"], tabs, S)
    grad_x, dproj, gs["pre_mix_norm"] = _inproj_bwd(drq, drk, drv, drg, dcq, dckv, dkr, w_in_p, dh1, x,
                                                    sm["pre_mix_norm"], S)
    g_in_p = _wgrad(xn, dproj, "wgrad_in", S)
    gw["w_in"] = jnp.concatenate([g_in_p[:, :2688], g_in_p[:, 2752:2784]], axis=1)
    return loss_vec, grad_x, gw, gs


def _my_place():
    x = lax.axis_index("x")
    y = lax.axis_index("y")
    c = lax.axis_index("c")
    return x, y, c


def _other_chips(x, y):
    return [(1 - x, y), (x, 1 - y), (1 - x, 1 - y)]


_ANY = pl.BlockSpec(memory_space=pl.ANY)


def _allgather_weights(wpk):
    H = HALF_ROWS

    def body(w_ref, out_ref, send1, recv1, send2, recv2, lsem):
        x, y, c = _my_place()
        me = 2 * x + y
        chips = _other_chips(x, y)
        half = pl.ds(pl.multiple_of(c * H, 32), H)
        other = pl.ds(pl.multiple_of((1 - c) * H, 32), H)
        mine = pltpu.make_async_copy(w_ref, out_ref.at[me], lsem)
        mine.start()

        def over_ici(k, src_chip, to):
            return pltpu.make_async_remote_copy(
                src_ref=w_ref.at[half], dst_ref=out_ref.at[src_chip, half], send_sem=send1.at[k], recv_sem=recv1.at[k],
                device_id=to, device_id_type=MESH)

        def to_sibling(k, chip, rows):
            return pltpu.make_async_remote_copy(
                src_ref=out_ref.at[chip, rows], dst_ref=out_ref.at[chip, rows], send_sem=send2.at[k],
                recv_sem=recv2.at[k], device_id=(x, y, 1 - c), device_id_type=MESH)

        first = [over_ici(k, me, (cx, cy, c)) for k, (cx, cy) in enumerate(chips)]
        for cp in first:
            cp.start()
        passed = []
        for k, (cx, cy) in enumerate(chips):
            over_ici(k, 2 * cx + cy, (cx, cy, c)).wait_recv()
            fwd = to_sibling(k, 2 * cx + cy, half)
            fwd.start()
            passed.append(fwd)
        for k, (cx, cy) in enumerate(chips):
            to_sibling(k, 2 * cx + cy, other).wait_recv()
        for cp in first + passed:
            cp.wait_send()
        mine.wait()

    return pl.pallas_call(
        body, name="allgather_weights",
        in_specs=[_ANY], out_specs=_ANY, out_shape=_sds((N_CHIPS, PACK_ROWS, PACK_COLS), BF16),
        scratch_shapes=[pltpu.SemaphoreType.DMA((3,)), pltpu.SemaphoreType.DMA((3,)), pltpu.SemaphoreType.DMA((3,)),
                        pltpu.SemaphoreType.DMA((3,)), pltpu.SemaphoreType.DMA],
    )(wpk)


def _swap_halves(gpk):
    H = HALF_ROWS

    def body(g_ref, out_ref, send, recv):
        x, y, c = _my_place()
        other = pl.ds(pl.multiple_of((1 - c) * H, 8), H)
        cp = pltpu.make_async_remote_copy(
            src_ref=g_ref.at[:, other], dst_ref=out_ref, send_sem=send, recv_sem=recv,
            device_id=(x, y, 1 - c), device_id_type=MESH)
        cp.start()
        cp.wait()

    return pl.pallas_call(
        body, name="rs_swap_halves",
        in_specs=[_ANY], out_specs=_ANY, out_shape=_sds((N_CHIPS, HALF_ROWS, PACK_COLS), F32),
        scratch_shapes=[pltpu.SemaphoreType.DMA, pltpu.SemaphoreType.DMA],
    )(gpk)


def _add_halves(gpk, got, c_idx):
    tr = 440
    nb = HALF_ROWS // tr

    def body(c_ref, a_ref, b_ref, o_ref):
        o_ref[...] = a_ref[...] + b_ref[...]

    grid_spec = pltpu.PrefetchScalarGridSpec(
        num_scalar_prefetch=1, grid=(N_CHIPS, nb),
        in_specs=[pl.BlockSpec((None, tr, PACK_COLS), lambda j, i, c: (j, c[0] * nb + i, 0)),
                  pl.BlockSpec((None, tr, PACK_COLS), lambda j, i, c: (j, i, 0))],
        out_specs=pl.BlockSpec((None, tr, PACK_COLS), lambda j, i, c: (j, i, 0)),
    )
    return pl.pallas_call(
        body, name="rs_add_halves", grid_spec=grid_spec, out_shape=_sds((N_CHIPS, HALF_ROWS, PACK_COLS), F32),
        compiler_params=_cp(("parallel", "parallel")),
    )(c_idx, gpk, got)


def _scatter_chips(tsum):
    def body(t_ref, out_ref, send, recv, lsem):
        x, y, c = _my_place()
        me = 2 * x + y
        chips = _other_chips(x, y)
        mine = pltpu.make_async_copy(t_ref.at[me], out_ref.at[me], lsem)
        mine.start()
        cps = [pltpu.make_async_remote_copy(
            src_ref=t_ref.at[2 * cx + cy], dst_ref=out_ref.at[me], send_sem=send.at[k], recv_sem=recv.at[k],
            device_id=(cx, cy, c), device_id_type=MESH) for k, (cx, cy) in enumerate(chips)]
        for cp in cps:
            cp.start()
        for cp in cps:
            cp.wait()
        mine.wait()

    return pl.pallas_call(
        body, name="rs_scatter_chips",
        in_specs=[_ANY], out_specs=_ANY, out_shape=_sds((N_CHIPS, HALF_ROWS, PACK_COLS), F32),
        scratch_shapes=[pltpu.SemaphoreType.DMA((3,)), pltpu.SemaphoreType.DMA((3,)), pltpu.SemaphoreType.DMA],
    )(tsum)


def _add_chips(parts):
    tr = 440

    def body(p_ref, o_ref):
        o_ref[...] = ((p_ref[0] + p_ref[1]) + p_ref[2]) + p_ref[3]

    return pl.pallas_call(
        body, name="rs_add_chips", grid=(HALF_ROWS // tr,),
        in_specs=[pl.BlockSpec((N_CHIPS, tr, PACK_COLS), lambda i: (0, i, 0))],
        out_specs=pl.BlockSpec((tr, PACK_COLS), lambda i: (i, 0)), out_shape=_sds((HALF_ROWS, PACK_COLS), F32),
        compiler_params=_cp(("parallel",)),
    )(parts)


def _join_halves(red):
    H = HALF_ROWS

    def body(r_ref, out_ref, send, recv, lsem):
        x, y, c = _my_place()
        half = pl.ds(pl.multiple_of(c * H, 8), H)
        mine = pltpu.make_async_copy(r_ref, out_ref.at[half], lsem)
        mine.start()
        cp = pltpu.make_async_remote_copy(
            src_ref=r_ref, dst_ref=out_ref.at[half], send_sem=send, recv_sem=recv,
            device_id=(x, y, 1 - c), device_id_type=MESH)
        cp.start()
        cp.wait()
        mine.wait()

    return pl.pallas_call(
        body, name="rs_join_halves",
        in_specs=[_ANY], out_specs=_ANY, out_shape=_sds((PACK_ROWS, PACK_COLS), F32),
        scratch_shapes=[pltpu.SemaphoreType.DMA, pltpu.SemaphoreType.DMA, pltpu.SemaphoreType.DMA],
    )(red)


def _allreduce_small(vec):
    def body(v_ref, out_ref, slots, send, recv, lsem):
        x, y, c = _my_place()
        me = 4 * x + 2 * y + c
        mine = pltpu.make_async_copy(v_ref, slots.at[me], lsem)
        mine.start()
        cps = []
        for r in range(1, N_DEV):
            px = x ^ (r >> 2)
            py = y ^ ((r >> 1) & 1)
            pc = c ^ (r & 1)
            cps.append(pltpu.make_async_remote_copy(
                src_ref=v_ref, dst_ref=slots.at[me], send_sem=send.at[r - 1], recv_sem=recv.at[r - 1],
                device_id=(px, py, pc), device_id_type=MESH))
        for cp in cps:
            cp.start()
        for cp in cps:
            cp.wait()
        mine.wait()
        acc = slots[0]
        for d in range(1, N_DEV):
            acc = acc + slots[d]
        out_ref[...] = acc
        loss = jnp.sum(acc[9:10, :], axis=1, keepdims=True) * (0.5 / D_MODEL)
        out_ref[9:10, :] = jnp.broadcast_to(loss, (1, PACK_COLS))

    vm = pl.BlockSpec(memory_space=pltpu.VMEM)
    return pl.pallas_call(
        body, name="allreduce_small",
        in_specs=[vm], out_specs=vm, out_shape=_sds((SMALL_ROWS, PACK_COLS), F32),
        scratch_shapes=[pltpu.VMEM((N_DEV, SMALL_ROWS, PACK_COLS), F32), pltpu.SemaphoreType.DMA((N_DEV - 1,)),
                        pltpu.SemaphoreType.DMA((N_DEV - 1,)), pltpu.SemaphoreType.DMA],
    )(vec)


def _adamw(wt, g, m, v, name):
    R, C = wt.shape
    tr = R
    for cand in (256, 128, 64, 32, 16, 8):
        if R % cand == 0:
            tr = cand
            break

    def body(w_ref, g_ref, m_ref, v_ref, d_ref, nm_ref, nv_ref):
        gg = g_ref[...]
        m_new = ADAM_B1 * m_ref[...] + (1.0 - ADAM_B1) * gg
        v_new = ADAM_B2 * v_ref[...] + (1.0 - ADAM_B2) * (gg * gg)
        m_hat = m_new / (1.0 - ADAM_B1 ** ADAM_STEP)
        v_hat = v_new / (1.0 - ADAM_B2 ** ADAM_STEP)
        d_ref[...] = -ADAM_LR * (m_hat / (jnp.sqrt(v_hat) + ADAM_EPS) + ADAM_WD * w_ref[...])
        nm_ref[...] = m_new
        nv_ref[...] = v_new

    spec = pl.BlockSpec((tr, C), lambda i: (i, 0))
    return pl.pallas_call(
        body, name=name, grid=(R // tr,), in_specs=[spec] * 4, out_specs=[spec] * 3, out_shape=[_sds((R, C), F32)] * 3,
        compiler_params=_cp(("parallel",)),
    )(wt, g, m, v)


def _shard_shape(r, c, axis):
    return (r // N_CHIPS, c) if axis == 0 else (r, c // N_CHIPS)


def _pack_rows(flat):
    return jnp.pad(flat, (0, PACK_ROWS * PACK_COLS - flat.shape[0])).reshape(PACK_ROWS, PACK_COLS)


def _pack_shards(mats):
    return _pack_rows(jnp.concatenate([mats[n].reshape(-1) for n, _, _, _ in BIG]))


def _unpack_shards(pk):
    flat = pk.reshape(-1)
    out = {}
    off = 0
    for n, r, c, ax in BIG:
        shp = _shard_shape(r, c, ax)
        sz = shp[0] * shp[1]
        out[n] = flat[off:off + sz].reshape(shp)
        off += sz
    return out


def _full_from_packs(allpk):
    per_chip = [_unpack_shards(allpk[j]) for j in range(N_CHIPS)]
    return {n: jnp.concatenate([per_chip[j][n] for j in range(N_CHIPS)], axis=ax) for n, _, _, ax in BIG}


def _packs_from_full(gw):
    packs = []
    for j in range(N_CHIPS):
        shards = {}
        for n, r, c, ax in BIG:
            shp = _shard_shape(r, c, ax)
            shards[n] = gw[n][j * shp[0]:(j + 1) * shp[0], :] if ax == 0 else gw[n][:, j * shp[1]:(j + 1) * shp[1]]
        packs.append(_pack_shards(shards))
    return jnp.stack(packs)


def _pack_small(vals, loss_vec=None):
    rows = [jnp.pad(vals[n].reshape(-1), (0, PACK_COLS - sz)) for n, sz in SMALL]
    rows.append(loss_vec.reshape(-1) if loss_vec is not None else jnp.zeros((PACK_COLS,), F32))
    rows += [jnp.zeros((PACK_COLS,), F32)] * (SMALL_ROWS - len(rows))
    return jnp.stack(rows)


def kernel(x, p, positions, pre_mix_norm, w_in, ret_gn_w, mla_q_norm, w_uq, mla_kv_norm, w_ukv, w_o, post_mix_norm, pre_ffn_norm, w_gate, w_up, w_down, post_ffn_norm, w_ple_proj, ple_norm, w_ple_gate, b_ple_gate, loss_target, m_pre_mix_norm, m_w_in, m_ret_gn_w, m_mla_q_norm, m_w_uq, m_mla_kv_norm, m_w_ukv, m_w_o, m_post_mix_norm, m_pre_ffn_norm, m_w_gate, m_w_up, m_w_down, m_post_ffn_norm, m_w_ple_proj, m_ple_norm, m_w_ple_gate, m_b_ple_gate, v_pre_mix_norm, v_w_in, v_ret_gn_w, v_mla_q_norm, v_w_uq, v_mla_kv_norm, v_w_ukv, v_w_o, v_post_mix_norm, v_pre_ffn_norm, v_w_gate, v_w_up, v_w_down, v_post_ffn_norm, v_w_ple_proj, v_ple_norm, v_w_ple_gate, v_b_ple_gate):
    wts = dict(pre_mix_norm=pre_mix_norm, w_in=w_in, ret_gn_w=ret_gn_w, mla_q_norm=mla_q_norm, w_uq=w_uq,
               mla_kv_norm=mla_kv_norm, w_ukv=w_ukv, w_o=w_o, post_mix_norm=post_mix_norm, pre_ffn_norm=pre_ffn_norm,
               w_gate=w_gate, w_up=w_up, w_down=w_down, post_ffn_norm=post_ffn_norm, w_ple_proj=w_ple_proj,
               ple_norm=ple_norm, w_ple_gate=w_ple_gate, b_ple_gate=b_ple_gate)
    mom = dict(pre_mix_norm=m_pre_mix_norm, w_in=m_w_in, ret_gn_w=m_ret_gn_w, mla_q_norm=m_mla_q_norm, w_uq=m_w_uq,
               mla_kv_norm=m_mla_kv_norm, w_ukv=m_w_ukv, w_o=m_w_o, post_mix_norm=m_post_mix_norm,
               pre_ffn_norm=m_pre_ffn_norm, w_gate=m_w_gate, w_up=m_w_up, w_down=m_w_down, post_ffn_norm=m_post_ffn_norm,
               w_ple_proj=m_w_ple_proj, ple_norm=m_ple_norm, w_ple_gate=m_w_ple_gate, b_ple_gate=m_b_ple_gate)
    var = dict(pre_mix_norm=v_pre_mix_norm, w_in=v_w_in, ret_gn_w=v_ret_gn_w, mla_q_norm=v_mla_q_norm, w_uq=v_w_uq,
               mla_kv_norm=v_mla_kv_norm, w_ukv=v_w_ukv, w_o=v_w_o, post_mix_norm=v_post_mix_norm,
               pre_ffn_norm=v_pre_ffn_norm, w_gate=v_w_gate, w_up=v_w_up, w_down=v_w_down, post_ffn_norm=v_post_ffn_norm,
               w_ple_proj=v_w_ple_proj, ple_norm=v_ple_norm, w_ple_gate=v_w_ple_gate, b_ple_gate=v_b_ple_gate)

    S = x.shape[1]
    shard2d = {n: wts[n][0] for n, _, _, _ in BIG}
    small2d = {n: wts[n] for n, _ in SMALL}

    allpk = _allgather_weights(_pack_shards({n: shard2d[n].astype(BF16) for n in shard2d}))
    w_full = _full_from_packs(allpk)

    pos_f = positions.astype(F32).reshape(S, 1)
    loss_vec, grad_x, gw, gs = _local_step(x[0], p[0, 0], pos_f, loss_target[0], w_full, small2d)

    gpk = _packs_from_full(gw)
    c_idx = lax.axis_index("c").astype(jnp.int32).reshape(1)
    chip_sum = _add_halves(gpk, _swap_halves(gpk), c_idx)
    reduced = _join_halves(_add_chips(_scatter_chips(chip_sum)))
    g_shard = _unpack_shards(reduced)

    small_sum = _allreduce_small(_pack_small(gs, loss_vec))
    loss = small_sum[9, 0]
    g_small = {n: small_sum[i:i + 1, :sz] for i, (n, sz) in enumerate(SMALL)}

    grads, delta, new_m, new_v = {}, {}, {}, {}
    for n, _, _, _ in BIG:
        d, nm, nv = _adamw(shard2d[n], g_shard[n], mom[n][0], var[n][0], "adamw_" + n)
        grads[n], delta[n], new_m[n], new_v[n] = g_shard[n][None], d[None], nm[None], nv[None]
    d, nm, nv = _adamw(_pack_small(small2d), small_sum, _pack_small(mom), _pack_small(var), "adamw_small")
    for i, (n, sz) in enumerate(SMALL):
        grads[n] = g_small[n]
        delta[n], new_m[n], new_v[n] = d[i:i + 1, :sz], nm[i:i + 1, :sz], nv[i:i + 1, :sz]

    return (loss, grad_x[None], *[grads[n] for n in ALL_W], *[delta[n] for n in ALL_W],
            *[new_m[n] for n in ALL_W], *[new_v[n] for n in ALL_W])
```

```python
import functools
import math

import jax
import jax.numpy as jnp
import numpy as np
from jax import lax
from jax.experimental import pallas as pl
from jax.experimental.pallas import tpu as pltpu

F32 = jnp.float32
BF16 = jnp.bfloat16
MESH = pl.DeviceIdType.MESH

D_MODEL = 1024
D_FF = 2816
PLE_DIM = 256
RET_HEADS = 4
RET_DIM = 128
RET_WIDTH = 512
RET_CHUNK = 128
RET_GROUP = 8
MLA_HEADS = 8
MLA_NOPE = 64
MLA_ROPE = 32
MLA_V = 64
Q_LORA = 384
KV_LORA = 256
IN_COLS = 2720
IN_COLS_P = 2816
ROPE_BASE = 10000.0
EPS = 1e-6
SCALE_MLA = 1.0 / math.sqrt(MLA_NOPE + MLA_ROPE)
SCALE_RET = RET_DIM ** -0.5
NEG = -1e30

ADAM_LR = 0.001
ADAM_B1 = 0.9
ADAM_B2 = 0.999
ADAM_EPS = 1e-08
ADAM_WD = 0.01
ADAM_STEP = 10

N_CHIPS = 4
N_DEV = 8
VMEM_MB = 56

BIG = (
    ("w_in", 1024, 2720, 1),
    ("w_uq", 384, 768, 1),
    ("w_ukv", 256, 1024, 1),
    ("w_o", 1024, 1024, 0),
    ("w_gate", 1024, 2816, 1),
    ("w_up", 1024, 2816, 1),
    ("w_down", 2816, 1024, 0),
    ("w_ple_proj", 256, 1024, 1),
    ("w_ple_gate", 1024, 1024, 0),
)
SMALL = (
    ("pre_mix_norm", 1024),
    ("ret_gn_w", 512),
    ("mla_q_norm", 384),
    ("mla_kv_norm", 256),
    ("post_mix_norm", 1024),
    ("pre_ffn_norm", 1024),
    ("post_ffn_norm", 1024),
    ("ple_norm", 1024),
    ("b_ple_gate", 1024),
)
ALL_W = ("pre_mix_norm", "w_in", "ret_gn_w", "mla_q_norm", "w_uq", "mla_kv_norm", "w_ukv", "w_o", "post_mix_norm",
         "pre_ffn_norm", "w_gate", "w_up", "w_down", "post_ffn_norm", "w_ple_proj", "ple_norm", "w_ple_gate", "b_ple_gate")
PACK_COLS = 1024
SHARD_ELEMS = sum(r * c for _, r, c, _ in BIG) // N_CHIPS
PACK_ROWS = -(-SHARD_ELEMS // PACK_COLS // 32) * 32
HALF_ROWS = PACK_ROWS // 2
SMALL_ROWS = 16


def _cp(sem=None, mb=VMEM_MB, **kw):
    return pltpu.CompilerParams(dimension_semantics=sem, vmem_limit_bytes=mb * 1024 * 1024, **kw)


def _bf(x):
    return x.astype(BF16)


def _dot(a, b):
    return jnp.dot(_bf(a), _bf(b), preferred_element_type=F32)


def _dot_nt(a, b):
    return lax.dot_general(_bf(a), _bf(b), (((1,), (1,)), ((), ())), preferred_element_type=F32)


def _dot_tn(a, b):
    return lax.dot_general(_bf(a), _bf(b), (((0,), (0,)), ((), ())), preferred_element_type=F32)


def _sig(x):
    return 1.0 / (1.0 + jnp.exp(-x))


def _rms(x, g):
    r = lax.rsqrt(jnp.mean(x * x, axis=-1, keepdims=True) + EPS)
    return x * r * g


def _rms_bwd(dy, x, g):
    r = lax.rsqrt(jnp.mean(x * x, axis=-1, keepdims=True) + EPS)
    xh = x * r
    dxh = dy * g
    dx = r * (dxh - xh * jnp.mean(dxh * xh, axis=-1, keepdims=True))
    return dx, dy * xh


def _colsum(x):
    return jnp.sum(x, axis=0, keepdims=True)


def _rope_ret(x, cr, sr):
    return x * cr + pltpu.roll(x, 64, 1) * sr


def _unrope_ret(dy, cr, sr):
    return dy * cr + pltpu.roll(dy * sr, 64, 1)


def _rope_mla(x, cm, sa, sb):
    return x * cm + pltpu.roll(x, 112, 1) * sa + pltpu.roll(x, 16, 1) * sb


def _unrope_mla(dy, cm, sa, sb):
    return dy * cm + pltpu.roll(dy * sa, 16, 1) + pltpu.roll(dy * sb, 112, 1)


def _rows(tm, w, col=0):
    return pl.BlockSpec((tm, w), lambda i: (i, col))


def _full(*shape):
    return pl.BlockSpec(shape, lambda i: (0,) * len(shape))


def _sds(shape, dtype):
    return jax.ShapeDtypeStruct(shape, dtype)


def _rope_tables(pos_f, S):
    tm = min(512, S)
    inv_r = (1.0 / (np.float32(ROPE_BASE) ** (np.arange(64, dtype=np.float32) / np.float32(64)))).astype(np.float32)
    inv_m16 = (1.0 / (np.float32(ROPE_BASE) ** (np.arange(16, dtype=np.float32) / np.float32(16)))).astype(np.float32)
    inv_r = np.concatenate([inv_r, inv_r])[None, :]
    inv_m = np.zeros((1, 128), np.float32)
    inv_m[0, 64:80] = inv_m16
    inv_m[0, 80:96] = inv_m16

    def body(pos_ref, invr_ref, invm_ref, cr_ref, sr_ref, cm_ref, sa_ref, sb_ref):
        pos = pos_ref[...]
        lane = lax.broadcasted_iota(jnp.int32, (tm, 128), 1)
        ar = pos * invr_ref[...]
        s = jnp.sin(ar)
        cr_ref[...] = jnp.cos(ar)
        sr_ref[...] = jnp.where(lane < 64, -s, s)
        am = pos * invm_ref[...]
        c2 = jnp.cos(am)
        s2 = jnp.sin(am)
        cm_ref[...] = jnp.where(lane < 64, 1.0, jnp.where(lane < 96, c2, 0.0))
        sa_ref[...] = jnp.where((lane >= 64) & (lane < 80), -s2, 0.0)
        sb_ref[...] = jnp.where((lane >= 80) & (lane < 96), s2, 0.0)

    return pl.pallas_call(
        body, name="rope_tables", grid=(S // tm,),
        in_specs=[_rows(tm, 1), _full(1, 128), _full(1, 128)],
        out_specs=[_rows(tm, 128)] * 5,
        out_shape=[_sds((S, 128), F32)] * 5,
        compiler_params=_cp(("parallel",)),
    )(pos_f, jnp.asarray(inv_r), jnp.asarray(inv_m))


def _rms_fwd(x, g, S):
    tm = min(512, S)

    def body(x_ref, g_ref, o_ref):
        o_ref[...] = _rms(x_ref[...], g_ref[...]).astype(BF16)

    return pl.pallas_call(
        body, name="rms_pre", grid=(S // tm,),
        in_specs=[_rows(tm, D_MODEL), _full(1, D_MODEL)],
        out_specs=_rows(tm, D_MODEL), out_shape=_sds((S, D_MODEL), BF16),
        compiler_params=_cp(("parallel",)),
    )(x, g)


def _inproj(xn, w_in, tabs, S):
    tm = min(256, S)

    def body(xn_ref, w_ref, cr_ref, sr_ref, cm_ref, sa_ref, sb_ref, rq_ref, rk_ref, rv_ref, rg_ref, cq_ref, ckv_ref, kr_ref):
        xb = xn_ref[...]
        cr = cr_ref[...]
        sr = sr_ref[...]
        q = jnp.dot(xb, w_ref[:, 0:512], preferred_element_type=F32)
        k = jnp.dot(xb, w_ref[:, 512:1024], preferred_element_type=F32)
        for h in range(RET_HEADS):
            sl = slice(h * 128, (h + 1) * 128)
            rq_ref[:, sl] = _rope_ret(q[:, sl], cr, sr).astype(BF16)
            rk_ref[:, sl] = (_rope_ret(k[:, sl], cr, sr) * SCALE_RET).astype(BF16)
        rv_ref[...] = jnp.dot(xb, w_ref[:, 1024:1536], preferred_element_type=F32).astype(BF16)
        rg_ref[...] = jnp.dot(xb, w_ref[:, 1536:2048], preferred_element_type=F32)
        cq_ref[...] = jnp.dot(xb, w_ref[:, 2048:2432], preferred_element_type=F32)
        ckv_ref[...] = jnp.dot(xb, w_ref[:, 2432:2688], preferred_element_type=F32)
        kr = jnp.dot(xb, w_ref[:, 2688:2816], preferred_element_type=F32)
        kr_ref[...] = _rope_mla(kr, cm_ref[...], sa_ref[...], sb_ref[...])

    return pl.pallas_call(
        body, name="inproj", grid=(S // tm,),
        in_specs=[_rows(tm, D_MODEL), _full(D_MODEL, IN_COLS_P)] + [_rows(tm, 128)] * 5,
        out_specs=[_rows(tm, 512)] * 4 + [_rows(tm, Q_LORA), _rows(tm, KV_LORA), _rows(tm, 128)],
        out_shape=[_sds((S, 512), BF16)] * 3 + [_sds((S, 512), F32), _sds((S, Q_LORA), F32), _sds((S, KV_LORA), F32),
                                                  _sds((S, 128), F32)],
        compiler_params=_cp(("parallel",)),
    )(xn, w_in, *tabs)


def _mla_up(cq, ckv, kr, gq, gkv, w_uq, w_ukv, tabs, S):
    tm = min(256, S)

    def body(cq_ref, ckv_ref, kr_ref, gq_ref, gkv_ref, wuq_ref, wukv_ref, cm_ref, sa_ref, sb_ref,
             cqn_ref, ckvn_ref, qp_ref, kp_ref, v_ref):
        cm = cm_ref[...]
        sa = sa_ref[...]
        sb = sb_ref[...]
        cqn = _rms(cq_ref[...], gq_ref[...]).astype(BF16)
        cqn_ref[...] = cqn
        ckvn = _rms(ckv_ref[...], gkv_ref[...]).astype(BF16)
        ckvn_ref[...] = ckvn
        qh = jnp.dot(cqn, wuq_ref[...], preferred_element_type=F32)
        kv = jnp.dot(ckvn, wukv_ref[...], preferred_element_type=F32)
        kr_blk = kr_ref[...]
        for h in range(MLA_HEADS):
            sl = slice(h * 128, (h + 1) * 128)
            qp_ref[:, sl] = (_rope_mla(qh[:, sl], cm, sa, sb) * SCALE_MLA).astype(BF16)
            kp_ref[:, sl] = (kv[:, sl] + kr_blk).astype(BF16)
        v_ref[...] = kv[:, 1024:1536].astype(BF16)

    return pl.pallas_call(
        body, name="mla_up", grid=(S // tm,),
        in_specs=[_rows(tm, Q_LORA), _rows(tm, KV_LORA), _rows(tm, 128), _full(1, Q_LORA), _full(1, KV_LORA),
                  _full(Q_LORA, 1024), _full(KV_LORA, 1536)] + [_rows(tm, 128)] * 3,
        out_specs=[_rows(tm, Q_LORA), _rows(tm, KV_LORA), _rows(tm, 1024), _rows(tm, 1024), _rows(tm, 512)],
        out_shape=[_sds((S, Q_LORA), BF16), _sds((S, KV_LORA), BF16), _sds((S, 1024), BF16), _sds((S, 1024), BF16),
                   _sds((S, 512), BF16)],
        compiler_params=_cp(("parallel",)),
    )(cq, ckv, kr, gq, gkv, w_uq, w_ukv, *tabs[2:])


def _tri_pairs(nq, k_major):
    if k_major:
        pairs = [(qb, kb) for kb in range(nq) for qb in range(kb, nq)]
    else:
        pairs = [(qb, kb) for qb in range(nq) for kb in range(qb + 1)]
    qb_of = np.array([p[0] for p in pairs], np.int32)
    kb_of = np.array([p[1] for p in pairs], np.int32)
    return jnp.asarray(qb_of), jnp.asarray(kb_of), len(pairs)


def _flash_fwd(qp, kp, v, S):
    tq = min(512, S)
    nq = S // tq
    qb_of, kb_of, T = _tri_pairs(nq, k_major=False)

    def body(qb_ref, kb_ref, q_ref, k_ref, v_ref, o_ref, lse_ref, m_sc, l_sc, acc_sc):
        t = pl.program_id(1)
        qb = qb_ref[t]
        kb = kb_ref[t]

        @pl.when(kb == 0)
        def _():
            m_sc[...] = jnp.full(m_sc.shape, NEG, F32)
            l_sc[...] = jnp.zeros(l_sc.shape, F32)
            acc_sc[...] = jnp.zeros(acc_sc.shape, F32)

        lane = lax.broadcasted_iota(jnp.int32, (tq, 128), 1)
        first = lane < 64
        rep = tq // 128

        def step(masked):
            vv = v_ref[...]
            pv = []
            alpha = []
            for a in range(2):
                sl = slice(a * 128, (a + 1) * 128)
                s = _dot_nt(q_ref[:, sl], k_ref[:, sl])
                if masked:
                    row = lax.broadcasted_iota(jnp.int32, (tq, tq), 0)
                    col = lax.broadcasted_iota(jnp.int32, (tq, tq), 1)
                    s = jnp.where(col <= row, s, NEG)
                m_prev = m_sc[a]
                m_new = jnp.maximum(m_prev, jnp.max(s, axis=1, keepdims=True))
                al = jnp.exp(m_prev - m_new)
                p = jnp.exp(s - jnp.tile(m_new, (1, rep)))
                l_sc[a] = al * l_sc[a] + jnp.sum(p, axis=1, keepdims=True)
                m_sc[a] = m_new
                pv.append(jnp.dot(p.astype(BF16), vv, preferred_element_type=F32))
                alpha.append(al)
            acc_sc[...] = acc_sc[...] * jnp.where(first, alpha[0], alpha[1]) + jnp.where(first, pv[0], pv[1])

        @pl.when(kb < qb)
        def _():
            step(False)

        @pl.when(kb == qb)
        def _():
            step(True)
            o_ref[...] = (acc_sc[...] / jnp.where(first, l_sc[0], l_sc[1])).astype(BF16)
            for a in range(2):
                lse_ref[:, a * 128:(a + 1) * 128] = m_sc[a] + jnp.log(l_sc[a])

    grid_spec = pltpu.PrefetchScalarGridSpec(
        num_scalar_prefetch=2, grid=(MLA_HEADS // 2, T),
        in_specs=[pl.BlockSpec((tq, 256), lambda j, t, qb, kb: (qb[t], j)),
                  pl.BlockSpec((tq, 256), lambda j, t, qb, kb: (kb[t], j)),
                  pl.BlockSpec((tq, 128), lambda j, t, qb, kb: (kb[t], j))],
        out_specs=[pl.BlockSpec((tq, 128), lambda j, t, qb, kb: (qb[t], j)),
                   pl.BlockSpec((tq, 256), lambda j, t, qb, kb: (qb[t], j))],
        scratch_shapes=[pltpu.VMEM((2, tq, 128), F32), pltpu.VMEM((2, tq, 128), F32), pltpu.VMEM((tq, 128), F32)],
    )
    return pl.pallas_call(
        body, name="flash_fwd", grid_spec=grid_spec,
        out_shape=[_sds((S, 512), BF16), _sds((S, 1024), F32)],
        compiler_params=_cp(("parallel", "arbitrary")),
    )(qb_of, kb_of, qp, kp, v)


def _decay_table():
    log_g = np.log(1.0 - 2.0 ** (-5.0 - np.arange(RET_HEADS, dtype=np.float32))).astype(np.float32)
    return jnp.asarray(np.broadcast_to(log_g[:, None, None], (RET_HEADS, 8, 128)).copy())


def _decay_terms(lg_ref):
    C = RET_CHUNK
    lg = lg_ref[0:1, :]
    row = lax.broadcasted_iota(jnp.int32, (C, C), 0)
    col = lax.broadcasted_iota(jnp.int32, (C, C), 1)
    diff = (row - col).astype(F32)
    dmat = jnp.where(diff >= 0, jnp.exp(jnp.maximum(diff, 0.0) * lg), 0.0)
    j = lax.broadcasted_iota(jnp.int32, (C, 1), 0).astype(F32)
    lg1 = lg[:, 0:1]
    zeta = jnp.exp((C - 1 - j) * lg1)
    xi = jnp.exp((j + 1.0) * lg1)
    g_chunk = jnp.exp(C * lg1)
    return dmat, zeta, xi, g_chunk


def _ret_fwd(rq, rk, rv, rg, gn_w, S):
    C = RET_CHUNK
    N = S // C
    G = min(RET_GROUP, N)
    NB = N // G

    def body(lg_ref, q_ref, k_ref, v_ref, rg_ref, w_ref, ry_ref, ro_ref, rprev_ref, r_sc):
        @pl.when(pl.program_id(1) == 0)
        def _():
            r_sc[...] = jnp.zeros(r_sc.shape, F32)

        dmat, zeta, xi, g_chunk = _decay_terms(lg_ref)
        w = w_ref[...]
        r = r_sc[...]
        for i in range(G):
            rows = slice(i * C, (i + 1) * C)
            q = q_ref[rows, :]
            k = k_ref[rows, :]
            v = v_ref[rows, :]
            r_prev = r.astype(BF16)
            rprev_ref[i] = r_prev
            sc = _dot_nt(q, k) * dmat
            ry = _dot(sc, v) + jnp.dot(q, r_prev, preferred_element_type=F32) * xi
            ry_ref[rows, :] = ry
            r = g_chunk * r + _dot_tn(k, zeta * v.astype(F32))
            mu = jnp.mean(ry, axis=-1, keepdims=True)
            yc = ry - mu
            yh = yc * lax.rsqrt(jnp.mean(yc * yc, axis=-1, keepdims=True) + EPS)
            g = rg_ref[rows, :]
            ro_ref[rows, :] = (g * _sig(g) * (yh * w)).astype(BF16)
        r_sc[...] = r

    blk = pl.BlockSpec((G * C, 128), lambda h, n: (n, h))
    return pl.pallas_call(
        body, name="ret_fwd", grid=(RET_HEADS, NB),
        in_specs=[pl.BlockSpec((None, 8, 128), lambda h, n: (h, 0, 0)), blk, blk, blk, blk,
                  pl.BlockSpec((1, 128), lambda h, n: (0, h))],
        out_specs=[blk, blk, pl.BlockSpec((G, 128, 128), lambda h, n: (h * NB + n, 0, 0))],
        out_shape=[_sds((S, 512), F32), _sds((S, 512), BF16), _sds((RET_HEADS * N, 128, 128), BF16)],
        scratch_shapes=[pltpu.VMEM((128, 128), F32)],
        compiler_params=_cp(("parallel", "arbitrary")),
    )(_decay_table(), rq, rk, rv, rg, gn_w)


def _outproj(ro, mo, x, w_o, g_post, g_pre, S):
    tm = min(256, S)

    def body(ro_ref, mo_ref, x_ref, wo_ref, g1_ref, g2_ref, mix_ref, h1_ref, hn_ref):
        mix = (jnp.dot(ro_ref[...], wo_ref[0:512, :], preferred_element_type=F32)
               + jnp.dot(mo_ref[...], wo_ref[512:1024, :], preferred_element_type=F32))
        mix_ref[...] = mix
        h1 = x_ref[...] + _rms(mix, g1_ref[...])
        h1_ref[...] = h1
        hn_ref[...] = _rms(h1, g2_ref[...]).astype(BF16)

    return pl.pallas_call(
        body, name="outproj", grid=(S // tm,),
        in_specs=[_rows(tm, 512), _rows(tm, 512), _rows(tm, D_MODEL), _full(D_MODEL, D_MODEL), _full(1, D_MODEL),
                  _full(1, D_MODEL)],
        out_specs=[_rows(tm, D_MODEL)] * 3,
        out_shape=[_sds((S, D_MODEL), F32), _sds((S, D_MODEL), F32), _sds((S, D_MODEL), BF16)],
        compiler_params=_cp(("parallel",)),
    )(ro, mo, x, w_o, g_post, g_pre)


def _ffn_up(hn, w_gate, w_up, S):
    tm = min(512, S)
    tn = D_FF // 2

    def body(hn_ref, wg_ref, wu_ref, gate_ref, up_ref, act_ref):
        hn_b = hn_ref[...]
        g = jnp.dot(hn_b, wg_ref[...], preferred_element_type=F32)
        u = jnp.dot(hn_b, wu_ref[...], preferred_element_type=F32)
        gate_ref[...] = g.astype(BF16)
        up_ref[...] = u.astype(BF16)
        act_ref[...] = (g * _sig(g) * u).astype(BF16)

    wspec = pl.BlockSpec((D_MODEL, tn), lambda j, i: (0, j))
    ospec = pl.BlockSpec((tm, tn), lambda j, i: (i, j))
    return pl.pallas_call(
        body, name="ffn_up", grid=(2, S // tm),
        in_specs=[pl.BlockSpec((tm, D_MODEL), lambda j, i: (i, 0)), wspec, wspec],
        out_specs=[ospec] * 3, out_shape=[_sds((S, D_FF), BF16)] * 3,
        compiler_params=_cp(("parallel", "parallel")),
    )(hn, w_gate, w_up)


def _ffn_down(act, w_down, h1, g, S):
    tm = min(256, S)

    def body(act_ref, wd_ref, h1_ref, g_ref, ff_ref, h2_ref):
        ff = jnp.dot(act_ref[...], wd_ref[...], preferred_element_type=F32)
        ff_ref[...] = ff
        h2_ref[...] = h1_ref[...] + _rms(ff, g_ref[...])

    return pl.pallas_call(
        body, name="ffn_down", grid=(S // tm,),
        in_specs=[_rows(tm, D_FF), _full(D_FF, D_MODEL), _rows(tm, D_MODEL), _full(1, D_MODEL)],
        out_specs=[_rows(tm, D_MODEL)] * 2, out_shape=[_sds((S, D_MODEL), F32)] * 2,
        compiler_params=_cp(("parallel",)),
    )(act, w_down, h1, g)


def _ple_loss(p, h2, tgt, w_pp, w_pg, b_pg, g_ple, S):
    tm = min(256, S)

    def body(p_ref, h2_ref, t_ref, wp_ref, wg_ref, b_ref, gp_ref,
             dz_ref, dpe_ref, dh2_ref, h2b_ref, loss_ref, dgp_ref, db_ref):
        @pl.when(pl.program_id(0) == 0)
        def _():
            loss_ref[...] = jnp.zeros(loss_ref.shape, F32)
            dgp_ref[...] = jnp.zeros(dgp_ref.shape, F32)
            db_ref[...] = jnp.zeros(db_ref.shape, F32)

        gp = gp_ref[...]
        pe = _dot(p_ref[...], wp_ref[...])
        r = lax.rsqrt(jnp.mean(pe * pe, axis=-1, keepdims=True) + EPS)
        peh = pe * r
        e = peh * gp
        h2 = h2_ref[...]
        h2b = h2.astype(BF16)
        h2b_ref[...] = h2b
        gt = _sig(jnp.dot(h2b, wg_ref[...], preferred_element_type=F32) + b_ref[...])
        diff = h2 + e * gt - t_ref[...]
        loss_ref[...] += _colsum(diff * diff)
        dh3 = diff * (1.0 / D_MODEL)
        de = dh3 * gt
        dz = dh3 * e * gt * (1.0 - gt)
        db_ref[...] += _colsum(dz)
        dgp_ref[...] += _colsum(de * peh)
        dpeh = de * gp
        dpe = r * (dpeh - peh * jnp.mean(dpeh * peh, axis=-1, keepdims=True))
        dzb = dz.astype(BF16)
        dz_ref[...] = dzb
        dpe_ref[...] = dpe.astype(BF16)
        dh2_ref[...] = dh3 + _dot_nt(dzb, wg_ref[...])

    return pl.pallas_call(
        body, name="ple_loss", grid=(S // tm,),
        in_specs=[_rows(tm, PLE_DIM), _rows(tm, D_MODEL), _rows(tm, D_MODEL), _full(PLE_DIM, D_MODEL),
                  _full(D_MODEL, D_MODEL), _full(1, D_MODEL), _full(1, D_MODEL)],
        out_specs=[_rows(tm, D_MODEL)] * 4 + [_full(1, D_MODEL)] * 3,
        out_shape=[_sds((S, D_MODEL), BF16), _sds((S, D_MODEL), BF16), _sds((S, D_MODEL), F32), _sds((S, D_MODEL), BF16)]
        + [_sds((1, D_MODEL), F32)] * 3,
        compiler_params=_cp(("arbitrary",)),
    )(p, h2, tgt, w_pp, w_pg, b_pg, g_ple)


def _wgrad(a, b, name, S):
    M = a.shape[1]
    N = b.shape[1]
    ts = min(512, S)
    nsplit = 2 if M * N >= 2 * 1024 * 1024 else 1
    tn = N // nsplit

    def body(a_ref, b_ref, o_ref):
        @pl.when(pl.program_id(1) == 0)
        def _():
            o_ref[...] = jnp.zeros(o_ref.shape, F32)

        o_ref[...] += _dot_tn(a_ref[...], b_ref[...])

    return pl.pallas_call(
        body, name=name, grid=(nsplit, S // ts),
        in_specs=[pl.BlockSpec((ts, M), lambda j, s: (s, 0)), pl.BlockSpec((ts, tn), lambda j, s: (s, j))],
        out_specs=pl.BlockSpec((M, tn), lambda j, s: (0, j)), out_shape=_sds((M, N), F32),
        compiler_params=_cp(("parallel", "arbitrary")),
    )(a, b)


def _ffn_down_bwd(dh2, ff, g, w_down, gate, up, S):
    tm = min(256, S)
    tn = D_FF // 2

    def body(dh2_ref, ff_ref, g_ref, wd_ref, gate_ref, up_ref, dff_ref, dgate_ref, dup_ref, dg_ref):
        @pl.when(pl.program_id(0) == 0)
        def _():
            dg_ref[...] = jnp.zeros(dg_ref.shape, F32)

        dff, ga = _rms_bwd(dh2_ref[...], ff_ref[...], g_ref[...])
        dg_ref[...] += _colsum(ga)
        dffb = dff.astype(BF16)
        dff_ref[...] = dffb
        for seg in range(2):
            sl = slice(seg * tn, (seg + 1) * tn)
            dact = _dot_nt(dffb, wd_ref[sl, :])
            gt = gate_ref[:, sl].astype(F32)
            u = up_ref[:, sl].astype(F32)
            s = _sig(gt)
            dgate_ref[:, sl] = (dact * u * (s * (1.0 + gt * (1.0 - s)))).astype(BF16)
            dup_ref[:, sl] = (dact * (gt * s)).astype(BF16)

    return pl.pallas_call(
        body, name="ffn_down_bwd", grid=(S // tm,),
        in_specs=[_rows(tm, D_MODEL), _rows(tm, D_MODEL), _full(1, D_MODEL), _full(D_FF, D_MODEL), _rows(tm, D_FF),
                  _rows(tm, D_FF)],
        out_specs=[_rows(tm, D_MODEL), _rows(tm, D_FF), _rows(tm, D_FF), _full(1, D_MODEL)],
        out_shape=[_sds((S, D_MODEL), BF16), _sds((S, D_FF), BF16), _sds((S, D_FF), BF16), _sds((1, D_MODEL), F32)],
        compiler_params=_cp(("arbitrary",)),
    )(dh2, ff, g, w_down, gate, up)


def _ffn_up_bwd(dgate, dup, w_gate, w_up, h1, mix, dh2, g_pre, g_post, w_o, S):
    tm = min(256, S)

    def body(dgate_ref, dup_ref, wg_ref, wu_ref, h1_ref, mix_ref, dh2_ref, g2_ref, g1_ref, wo_ref,
             dh1_ref, dmix_ref, dro_ref, dmo_ref, dg2_ref, dg1_ref):
        @pl.when(pl.program_id(0) == 0)
        def _():
            dg2_ref[...] = jnp.zeros(dg2_ref.shape, F32)
            dg1_ref[...] = jnp.zeros(dg1_ref.shape, F32)

        dhn = _dot_nt(dgate_ref[...], wg_ref[...]) + _dot_nt(dup_ref[...], wu_ref[...])
        d1, ga = _rms_bwd(dhn, h1_ref[...], g2_ref[...])
        dg2_ref[...] += _colsum(ga)
        dh1 = dh2_ref[...] + d1
        dh1_ref[...] = dh1
        dmix, gb = _rms_bwd(dh1, mix_ref[...], g1_ref[...])
        dg1_ref[...] += _colsum(gb)
        dmixb = dmix.astype(BF16)
        dmix_ref[...] = dmixb
        dcat = _dot_nt(dmixb, wo_ref[...])
        dro_ref[...] = dcat[:, 0:512].astype(BF16)
        dmo_ref[...] = dcat[:, 512:1024].astype(BF16)

    return pl.pallas_call(
        body, name="ffn_up_bwd", grid=(S // tm,),
        in_specs=[_rows(tm, D_FF), _rows(tm, D_FF), _full(D_MODEL, D_FF), _full(D_MODEL, D_FF), _rows(tm, D_MODEL),
                  _rows(tm, D_MODEL), _rows(tm, D_MODEL), _full(1, D_MODEL), _full(1, D_MODEL), _full(D_MODEL, D_MODEL)],
        out_specs=[_rows(tm, D_MODEL), _rows(tm, D_MODEL), _rows(tm, 512), _rows(tm, 512), _full(1, D_MODEL),
                   _full(1, D_MODEL)],
        out_shape=[_sds((S, D_MODEL), F32), _sds((S, D_MODEL), BF16), _sds((S, 512), BF16), _sds((S, 512), BF16),
                   _sds((1, D_MODEL), F32), _sds((1, D_MODEL), F32)],
        compiler_params=_cp(("arbitrary",)),
    )(dgate, dup, w_gate, w_up, h1, mix, dh2, g_pre, g_post, w_o)


def _attn_delta(o, do, S):
    tm = min(512, S)

    def body(o_ref, do_ref, d_ref):
        prod = o_ref[...].astype(F32) * do_ref[...].astype(F32)
        for h in range(MLA_HEADS):
            sl = slice(h * 64, (h + 1) * 64)
            d_ref[:, h * 128:(h + 1) * 128] = jnp.broadcast_to(jnp.sum(prod[:, sl], axis=1, keepdims=True), (tm, 128))

    return pl.pallas_call(
        body, name="attn_delta", grid=(S // tm,),
        in_specs=[_rows(tm, 512), _rows(tm, 512)], out_specs=_rows(tm, 1024), out_shape=_sds((S, 1024), F32),
        compiler_params=_cp(("parallel",)),
    )(o, do)


def _flash_bwd(qp, kp, v, do, lse, delta, S):
    tq = min(512, S)
    nq = S // tq
    qb_of, kb_of, T = _tri_pairs(nq, k_major=True)

    def body(qb_ref, kb_ref, q_ref, k_ref, v_ref, do_ref, lse_ref, dl_ref, dq_ref, dk_ref, dv_ref, dk_sc, dv_sc):
        t = pl.program_id(1)
        qb = qb_ref[t]
        kb = kb_ref[t]

        @pl.when(t == 0)
        def _():
            dq_ref[...] = jnp.zeros(dq_ref.shape, F32)

        @pl.when(qb == kb)
        def _():
            dk_sc[...] = jnp.zeros(dk_sc.shape, F32)
            dv_sc[...] = jnp.zeros(dv_sc.shape, F32)

        lane = lax.broadcasted_iota(jnp.int32, (tq, 128), 1)
        rep = tq // 128
        q0 = pl.multiple_of(qb * tq, tq)

        def step(masked):
            vv = v_ref[...]
            do_all = do_ref[...]
            for a in range(2):
                sl = slice(a * 128, (a + 1) * 128)
                q = q_ref[:, sl]
                k = k_ref[:, sl]
                s = _dot_nt(q, k)
                if masked:
                    row = lax.broadcasted_iota(jnp.int32, (tq, tq), 0)
                    col = lax.broadcasted_iota(jnp.int32, (tq, tq), 1)
                    s = jnp.where(col <= row, s, NEG)
                p = jnp.exp(s - jnp.tile(lse_ref[:, sl], (1, rep)))
                do_a = jnp.where((lane < 64) if a == 0 else (lane >= 64), do_all, jnp.zeros_like(do_all))
                dp = _dot_nt(do_a, vv)
                ds = (p * (dp - jnp.tile(dl_ref[:, sl], (1, rep)))).astype(BF16)
                dv_sc[...] += _dot_tn(p, do_a)
                dk_sc[:, sl] += _dot_tn(ds, q)
                dq_ref[pl.ds(q0, tq), sl] += jnp.dot(ds, k, preferred_element_type=F32)

        @pl.when(qb > kb)
        def _():
            step(False)

        @pl.when(qb == kb)
        def _():
            step(True)

        @pl.when(qb == nq - 1)
        def _():
            dk_ref[...] = dk_sc[...]
            dv_ref[...] = dv_sc[...]

    grid_spec = pltpu.PrefetchScalarGridSpec(
        num_scalar_prefetch=2, grid=(MLA_HEADS // 2, T),
        in_specs=[pl.BlockSpec((tq, 256), lambda j, t, qb, kb: (qb[t], j)),
                  pl.BlockSpec((tq, 256), lambda j, t, qb, kb: (kb[t], j)),
                  pl.BlockSpec((tq, 128), lambda j, t, qb, kb: (kb[t], j)),
                  pl.BlockSpec((tq, 128), lambda j, t, qb, kb: (qb[t], j)),
                  pl.BlockSpec((tq, 256), lambda j, t, qb, kb: (qb[t], j)),
                  pl.BlockSpec((tq, 256), lambda j, t, qb, kb: (qb[t], j))],
        out_specs=[pl.BlockSpec((S, 256), lambda j, t, qb, kb: (0, j)),
                   pl.BlockSpec((tq, 256), lambda j, t, qb, kb: (kb[t], j)),
                   pl.BlockSpec((tq, 128), lambda j, t, qb, kb: (kb[t], j))],
        scratch_shapes=[pltpu.VMEM((tq, 256), F32), pltpu.VMEM((tq, 128), F32)],
    )
    return pl.pallas_call(
        body, name="flash_bwd", grid_spec=grid_spec,
        out_shape=[_sds((S, 1024), F32), _sds((S, 1024), F32), _sds((S, 512), F32)],
        compiler_params=_cp(("parallel", "arbitrary")),
    )(qb_of, kb_of, qp, kp, v, do, lse, delta)


def _mla_up_bwd(dqp, dkp, dv, cq, ckv, gq, gkv, w_uq, w_ukv, tabs, S):
    tm = min(256, S)

    def body(dq_ref, dk_ref, dv_ref, cq_ref, ckv_ref, gq_ref, gkv_ref, wuq_ref, wukv_ref, cm_ref, sa_ref, sb_ref,
             dqh_ref, dkv_ref, dcq_ref, dckv_ref, dkr_ref, dgq_ref, dgkv_ref):
        @pl.when(pl.program_id(0) == 0)
        def _():
            dgq_ref[...] = jnp.zeros(dgq_ref.shape, F32)
            dgkv_ref[...] = jnp.zeros(dgkv_ref.shape, F32)

        cm = cm_ref[...]
        sa = sa_ref[...]
        sb = sb_ref[...]
        lane = lax.broadcasted_iota(jnp.int32, (tm, 128), 1)
        dkr_r = jnp.zeros((tm, 128), F32)
        for h in range(MLA_HEADS):
            sl = slice(h * 128, (h + 1) * 128)
            dqh_ref[:, sl] = (_unrope_mla(dq_ref[:, sl], cm, sa, sb) * SCALE_MLA).astype(BF16)
            gk = dk_ref[:, sl]
            dkr_r = dkr_r + gk
            dkv_ref[:, sl] = gk.astype(BF16)
        dkr_r = jnp.where((lane >= 64) & (lane < 96), dkr_r, 0.0)
        dkr_ref[...] = _unrope_mla(dkr_r, cm, sa, sb).astype(BF16)
        dkv_ref[:, 1024:1536] = dv_ref[...].astype(BF16)
        dcq, ga = _rms_bwd(_dot_nt(dqh_ref[...], wuq_ref[...]), cq_ref[...], gq_ref[...])
        dcq_ref[...] = dcq.astype(BF16)
        dgq_ref[...] += _colsum(ga)
        dckv, gb = _rms_bwd(_dot_nt(dkv_ref[...], wukv_ref[...]), ckv_ref[...], gkv_ref[...])
        dckv_ref[...] = dckv.astype(BF16)
        dgkv_ref[...] += _colsum(gb)

    return pl.pallas_call(
        body, name="mla_up_bwd", grid=(S // tm,),
        in_specs=[_rows(tm, 1024), _rows(tm, 1024), _rows(tm, 512), _rows(tm, Q_LORA), _rows(tm, KV_LORA),
                  _full(1, Q_LORA), _full(1, KV_LORA), _full(Q_LORA, 1024), _full(KV_LORA, 1536)] + [_rows(tm, 128)] * 3,
        out_specs=[_rows(tm, 1024), _rows(tm, 1536), _rows(tm, Q_LORA), _rows(tm, KV_LORA), _rows(tm, 128),
                   _full(1, Q_LORA), _full(1, KV_LORA)],
        out_shape=[_sds((S, 1024), BF16), _sds((S, 1536), BF16), _sds((S, Q_LORA), BF16), _sds((S, KV_LORA), BF16),
                   _sds((S, 128), BF16), _sds((1, Q_LORA), F32), _sds((1, KV_LORA), F32)],
        compiler_params=_cp(("arbitrary",)),
    )(dqp, dkp, dv, cq, ckv, gq, gkv, w_uq, w_ukv, *tabs[2:])


def _ret_bwd(rq, rk, rv, rprev, ry, rg, dro, gn_w, tabs, S):
    C = RET_CHUNK
    N = S // C
    G = min(RET_GROUP, N)
    NB = N // G

    def body(lg_ref, q_ref, k_ref, v_ref, rp_ref, ry_ref, rg_ref, dro_ref, w_ref, cr_ref, sr_ref,
             drq_ref, drk_ref, drv_ref, drg_ref, dw_ref, g_sc):
        @pl.when(pl.program_id(1) == 0)
        def _():
            g_sc[...] = jnp.zeros(g_sc.shape, F32)
            dw_ref[...] = jnp.zeros(dw_ref.shape, F32)

        dmat, zeta, xi, g_chunk = _decay_terms(lg_ref)
        w = w_ref[...]
        gacc = g_sc[...]
        dw = jnp.zeros((1, 128), F32)
        for i in reversed(range(G)):
            rows = slice(i * C, (i + 1) * C)
            ry = ry_ref[rows, :]
            mu = jnp.mean(ry, axis=-1, keepdims=True)
            yc = ry - mu
            rstd = lax.rsqrt(jnp.mean(yc * yc, axis=-1, keepdims=True) + EPS)
            yh = yc * rstd
            g = rg_ref[rows, :]
            s = _sig(g)
            dout = dro_ref[rows, :].astype(F32)
            drg_ref[rows, :] = (dout * (yh * w) * (s * (1.0 + g * (1.0 - s)))).astype(BF16)
            dgn = dout * (g * s)
            dw = dw + _colsum(dgn * yh)
            dyh = dgn * w
            dry = rstd * (dyh - jnp.mean(dyh, axis=-1, keepdims=True) - yh * jnp.mean(dyh * yh, axis=-1, keepdims=True))
            do = dry.astype(BF16)

            q = q_ref[rows, :]
            k = k_ref[rows, :]
            v = v_ref[rows, :]
            gfut = gacc.astype(BF16)
            sc = (_dot_nt(q, k) * dmat).astype(BF16)
            dsc = (_dot_nt(do, v) * dmat).astype(BF16)
            dq = jnp.dot(dsc, k, preferred_element_type=F32) + _dot_nt(do, rp_ref[i]) * xi
            dk = _dot_tn(dsc, q) + _dot_nt(v, gfut) * zeta
            dv = _dot_tn(sc, do) + jnp.dot(k, gfut, preferred_element_type=F32) * zeta
            gacc = g_chunk * gacc + _dot_tn(q, xi * dry)
            cr = cr_ref[rows, :]
            sr = sr_ref[rows, :]
            drq_ref[rows, :] = _unrope_ret(dq, cr, sr).astype(BF16)
            drk_ref[rows, :] = _unrope_ret(dk * SCALE_RET, cr, sr).astype(BF16)
            drv_ref[rows, :] = dv.astype(BF16)
        g_sc[...] = gacc
        dw_ref[...] += dw

    blk = pl.BlockSpec((G * C, 128), lambda h, n: (NB - 1 - n, h))
    tab = pl.BlockSpec((G * C, 128), lambda h, n: (NB - 1 - n, 0))
    return pl.pallas_call(
        body, name="ret_bwd", grid=(RET_HEADS, NB),
        in_specs=[pl.BlockSpec((None, 8, 128), lambda h, n: (h, 0, 0)), blk, blk, blk,
                  pl.BlockSpec((G, 128, 128), lambda h, n: (h * NB + NB - 1 - n, 0, 0)), blk, blk, blk,
                  pl.BlockSpec((1, 128), lambda h, n: (0, h)), tab, tab],
        out_specs=[blk, blk, blk, blk, pl.BlockSpec((1, 128), lambda h, n: (0, h))],
        out_shape=[_sds((S, 512), BF16)] * 4 + [_sds((1, 512), F32)],
        scratch_shapes=[pltpu.VMEM((128, 128), F32)],
        compiler_params=_cp(("parallel", "arbitrary")),
    )(_decay_table(), rq, rk, rv, rprev, ry, rg, dro, gn_w, tabs[0], tabs[1])


def _inproj_bwd(drq, drk, drv, drg, dcq, dckv, dkr, w_in, dh1, x, g, S):
    tm = min(256, S)

    def body(drq_ref, drk_ref, drv_ref, drg_ref, dcq_ref, dckv_ref, dkr_ref, w_ref, dh1_ref, x_ref, g_ref,
             gx_ref, dproj_ref, dg_ref):
        @pl.when(pl.program_id(0) == 0)
        def _():
            dg_ref[...] = jnp.zeros(dg_ref.shape, F32)

        dproj_ref[:, 0:512] = drq_ref[...]
        dproj_ref[:, 512:1024] = drk_ref[...]
        dproj_ref[:, 1024:1536] = drv_ref[...]
        dproj_ref[:, 1536:2048] = drg_ref[...]
        dproj_ref[:, 2048:2432] = dcq_ref[...]
        dproj_ref[:, 2432:2688] = dckv_ref[...]
        dproj_ref[:, 2688:2816] = dkr_ref[...]
        dx, ga = _rms_bwd(_dot_nt(dproj_ref[...], w_ref[...]), x_ref[...], g_ref[...])
        gx_ref[...] = dh1_ref[...] + dx
        dg_ref[...] += _colsum(ga)

    return pl.pallas_call(
        body, name="inproj_bwd", grid=(S // tm,),
        in_specs=[_rows(tm, 512)] * 4 + [_rows(tm, Q_LORA), _rows(tm, KV_LORA), _rows(tm, 128),
                                         _full(D_MODEL, IN_COLS_P), _rows(tm, D_MODEL), _rows(tm, D_MODEL),
                                         _full(1, D_MODEL)],
        out_specs=[_rows(tm, D_MODEL), _rows(tm, IN_COLS_P), _full(1, D_MODEL)],
        out_shape=[_sds((S, D_MODEL), F32), _sds((S, IN_COLS_P), BF16), _sds((1, D_MODEL), F32)],
        compiler_params=_cp(("arbitrary",)),
    )(drq, drk, drv, drg, dcq, dckv, dkr, w_in, dh1, x, g)


def _pad_weights(w):
    w_in = w["w_in"]
    z = lambda r, c: jnp.zeros((r, c), BF16)
    w_in_p = jnp.concatenate([w_in[:, :2688], z(1024, 64), w_in[:, 2688:2720], z(1024, 32)], axis=1)
    w_uq_p = jnp.pad(w["w_uq"].reshape(Q_LORA, MLA_HEADS, 96), ((0, 0), (0, 0), (0, 32))).reshape(Q_LORA, 1024)
    ukv = w["w_ukv"].reshape(KV_LORA, MLA_HEADS, 128)
    k_part = jnp.pad(ukv[:, :, :64], ((0, 0), (0, 0), (0, 64))).reshape(KV_LORA, 1024)
    w_ukv_p = jnp.concatenate([k_part, ukv[:, :, 64:].reshape(KV_LORA, 512)], axis=1)
    return w_in_p, w_uq_p, w_ukv_p


def _local_step(x, p, pos_f, tgt, w, sm):
    S = x.shape[0]
    w_in_p, w_uq_p, w_ukv_p = _pad_weights(w)
    tabs = _rope_tables(pos_f, S)

    xn = _rms_fwd(x, sm["pre_mix_norm"], S)
    rq, rk, rv, rg, cq, ckv, kr = _inproj(xn, w_in_p, tabs, S)
    cqn, ckvn, qp, kp, v = _mla_up(cq, ckv, kr, sm["mla_q_norm"], sm["mla_kv_norm"], w_uq_p, w_ukv_p, tabs, S)
    mo, lse = _flash_fwd(qp, kp, v, S)
    ry, ro, rprev = _ret_fwd(rq, rk, rv, rg, sm["ret_gn_w"], S)
    mix, h1, hn = _outproj(ro, mo, x, w["w_o"], sm["post_mix_norm"], sm["pre_ffn_norm"], S)
    gate, up, act = _ffn_up(hn, w["w_gate"], w["w_up"], S)
    ff, h2 = _ffn_down(act, w["w_down"], h1, sm["post_ffn_norm"], S)
    dz, dpe, dh2, h2b, loss_vec, d_ple_norm, d_b = _ple_loss(
        p, h2, tgt, w["w_ple_proj"], w["w_ple_gate"], sm["b_ple_gate"], sm["ple_norm"], S)

    gw = {}
    gs = {"ple_norm": d_ple_norm, "b_ple_gate": d_b}
    gw["w_ple_gate"] = _wgrad(h2b, dz, "wgrad_ple_gate", S)
    gw["w_ple_proj"] = _wgrad(p, dpe, "wgrad_ple_proj", S)
    dff, dgate, dup, gs["post_ffn_norm"] = _ffn_down_bwd(dh2, ff, sm["post_ffn_norm"], w["w_down"], gate, up, S)
    gw["w_down"] = _wgrad(act, dff, "wgrad_down", S)
    gw["w_gate"] = _wgrad(hn, dgate, "wgrad_gate", S)
    gw["w_up"] = _wgrad(hn, dup, "wgrad_up", S)
    dh1, dmix, dro, dmo, gs["pre_ffn_norm"], gs["post_mix_norm"] = _ffn_up_bwd(
        dgate, dup, w["w_gate"], w["w_up"], h1, mix, dh2, sm["pre_ffn_norm"], sm["post_mix_norm"], w["w_o"], S)
    gw["w_o"] = jnp.concatenate([_wgrad(ro, dmix, "wgrad_o_ret", S), _wgrad(mo, dmix, "wgrad_o_mla", S)], axis=0)

    delta = _attn_delta(mo, dmo, S)
    dqp, dkp, dv = _flash_bwd(qp, kp, v, dmo, lse, delta, S)
    dqh, dkv, dcq, dckv, dkr, gs["mla_q_norm"], gs["mla_kv_norm"] = _mla_up_bwd(
        dqp, dkp, dv, cq, ckv, sm["mla_q_norm"], sm["mla_kv_norm"], w_uq_p, w_ukv_p, tabs, S)
    g_uq_p = _wgrad(cqn, dqh, "wgrad_uq", S)
    g_ukv_p = _wgrad(ckvn, dkv, "wgrad_ukv", S)
    gw["w_uq"] = g_uq_p.reshape(Q_LORA, MLA_HEADS, 128)[:, :, :96].reshape(Q_LORA, 768)
    gw["w_ukv"] = jnp.concatenate(
        [g_ukv_p[:, :1024].reshape(KV_LORA, MLA_HEADS, 128)[:, :, :64], g_ukv_p[:, 1024:].reshape(KV_LORA, MLA_HEADS, 64)],
        axis=2).reshape(KV_LORA, 1024)

    drq, drk, drv, drg, gs["ret_gn_w"] = _ret_bwd(rq, rk, rv, rprev, ry, rg, dro, sm["ret_gn_w"], tabs, S)
    grad_x, dproj, gs["pre_mix_norm"] = _inproj_bwd(drq, drk, drv, drg, dcq, dckv, dkr, w_in_p, dh1, x,
                                                    sm["pre_mix_norm"], S)
    g_in_p = _wgrad(xn, dproj, "wgrad_in", S)
    gw["w_in"] = jnp.concatenate([g_in_p[:, :2688], g_in_p[:, 2752:2784]], axis=1)
    return loss_vec, grad_x, gw, gs


def _my_place():
    x = lax.axis_index("x")
    y = lax.axis_index("y")
    c = lax.axis_index("c")
    return x, y, c


def _other_chips(x, y):
    return [(1 - x, y), (x, 1 - y), (1 - x, 1 - y)]


_ANY = pl.BlockSpec(memory_space=pl.ANY)


def _allgather_weights(wpk):
    H = HALF_ROWS

    def body(w_ref, out_ref, send1, recv1, send2, recv2, lsem):
        x, y, c = _my_place()
        me = 2 * x + y
        chips = _other_chips(x, y)
        half = pl.ds(pl.multiple_of(c * H, 32), H)
        other = pl.ds(pl.multiple_of((1 - c) * H, 32), H)
        mine = pltpu.make_async_copy(w_ref, out_ref.at[me], lsem)
        mine.start()

        def over_ici(k, src_chip, to):
            return pltpu.make_async_remote_copy(
                src_ref=w_ref.at[half], dst_ref=out_ref.at[src_chip, half], send_sem=send1.at[k], recv_sem=recv1.at[k],
                device_id=to, device_id_type=MESH)

        def to_sibling(k, chip, rows):
            return pltpu.make_async_remote_copy(
                src_ref=out_ref.at[chip, rows], dst_ref=out_ref.at[chip, rows], send_sem=send2.at[k],
                recv_sem=recv2.at[k], device_id=(x, y, 1 - c), device_id_type=MESH)

        first = [over_ici(k, me, (cx, cy, c)) for k, (cx, cy) in enumerate(chips)]
        for cp in first:
            cp.start()
        passed = []
        for k, (cx, cy) in enumerate(chips):
            over_ici(k, 2 * cx + cy, (cx, cy, c)).wait_recv()
            fwd = to_sibling(k, 2 * cx + cy, half)
            fwd.start()
            passed.append(fwd)
        for k, (cx, cy) in enumerate(chips):
            to_sibling(k, 2 * cx + cy, other).wait_recv()
        for cp in first + passed:
            cp.wait_send()
        mine.wait()

    return pl.pallas_call(
        body, name="allgather_weights",
        in_specs=[_ANY], out_specs=_ANY, out_shape=_sds((N_CHIPS, PACK_ROWS, PACK_COLS), BF16),
        scratch_shapes=[pltpu.SemaphoreType.DMA((3,)), pltpu.SemaphoreType.DMA((3,)), pltpu.SemaphoreType.DMA((3,)),
                        pltpu.SemaphoreType.DMA((3,)), pltpu.SemaphoreType.DMA],
    )(wpk)


def _swap_halves(gpk):
    H = HALF_ROWS

    def body(g_ref, out_ref, send, recv):
        x, y, c = _my_place()
        other = pl.ds(pl.multiple_of((1 - c) * H, 8), H)
        cp = pltpu.make_async_remote_copy(
            src_ref=g_ref.at[:, other], dst_ref=out_ref, send_sem=send, recv_sem=recv,
            device_id=(x, y, 1 - c), device_id_type=MESH)
        cp.start()
        cp.wait()

    return pl.pallas_call(
        body, name="rs_swap_halves",
        in_specs=[_ANY], out_specs=_ANY, out_shape=_sds((N_CHIPS, HALF_ROWS, PACK_COLS), F32),
        scratch_shapes=[pltpu.SemaphoreType.DMA, pltpu.SemaphoreType.DMA],
    )(gpk)


def _add_halves(gpk, got, c_idx):
    tr = 440
    nb = HALF_ROWS // tr

    def body(c_ref, a_ref, b_ref, o_ref):
        o_ref[...] = a_ref[...] + b_ref[...]

    grid_spec = pltpu.PrefetchScalarGridSpec(
        num_scalar_prefetch=1, grid=(N_CHIPS, nb),
        in_specs=[pl.BlockSpec((None, tr, PACK_COLS), lambda j, i, c: (j, c[0] * nb + i, 0)),
                  pl.BlockSpec((None, tr, PACK_COLS), lambda j, i, c: (j, i, 0))],
        out_specs=pl.BlockSpec((None, tr, PACK_COLS), lambda j, i, c: (j, i, 0)),
    )
    return pl.pallas_call(
        body, name="rs_add_halves", grid_spec=grid_spec, out_shape=_sds((N_CHIPS, HALF_ROWS, PACK_COLS), F32),
        compiler_params=_cp(("parallel", "parallel")),
    )(c_idx, gpk, got)


def _scatter_chips(tsum):
    def body(t_ref, out_ref, send, recv, lsem):
        x, y, c = _my_place()
        me = 2 * x + y
        chips = _other_chips(x, y)
        mine = pltpu.make_async_copy(t_ref.at[me], out_ref.at[me], lsem)
        mine.start()
        cps = [pltpu.make_async_remote_copy(
            src_ref=t_ref.at[2 * cx + cy], dst_ref=out_ref.at[me], send_sem=send.at[k], recv_sem=recv.at[k],
            device_id=(cx, cy, c), device_id_type=MESH) for k, (cx, cy) in enumerate(chips)]
        for cp in cps:
            cp.start()
        for cp in cps:
            cp.wait()
        mine.wait()

    return pl.pallas_call(
        body, name="rs_scatter_chips",
        in_specs=[_ANY], out_specs=_ANY, out_shape=_sds((N_CHIPS, HALF_ROWS, PACK_COLS), F32),
        scratch_shapes=[pltpu.SemaphoreType.DMA((3,)), pltpu.SemaphoreType.DMA((3,)), pltpu.SemaphoreType.DMA],
    )(tsum)


def _add_chips(parts):
    tr = 440

    def body(p_ref, o_ref):
        o_ref[...] = ((p_ref[0] + p_ref[1]) + p_ref[2]) + p_ref[3]

    return pl.pallas_call(
        body, name="rs_add_chips", grid=(HALF_ROWS // tr,),
        in_specs=[pl.BlockSpec((N_CHIPS, tr, PACK_COLS), lambda i: (0, i, 0))],
        out_specs=pl.BlockSpec((tr, PACK_COLS), lambda i: (i, 0)), out_shape=_sds((HALF_ROWS, PACK_COLS), F32),
        compiler_params=_cp(("parallel",)),
    )(parts)


def _join_halves(red):
    H = HALF_ROWS

    def body(r_ref, out_ref, send, recv, lsem):
        x, y, c = _my_place()
        half = pl.ds(pl.multiple_of(c * H, 8), H)
        mine = pltpu.make_async_copy(r_ref, out_ref.at[half], lsem)
        mine.start()
        cp = pltpu.make_async_remote_copy(
            src_ref=r_ref, dst_ref=out_ref.at[half], send_sem=send, recv_sem=recv,
            device_id=(x, y, 1 - c), device_id_type=MESH)
        cp.start()
        cp.wait()
        mine.wait()

    return pl.pallas_call(
        body, name="rs_join_halves",
        in_specs=[_ANY], out_specs=_ANY, out_shape=_sds((PACK_ROWS, PACK_COLS), F32),
        scratch_shapes=[pltpu.SemaphoreType.DMA, pltpu.SemaphoreType.DMA, pltpu.SemaphoreType.DMA],
    )(red)


def _allreduce_small(vec):
    def body(v_ref, out_ref, slots, send, recv, lsem):
        x, y, c = _my_place()
        me = 4 * x + 2 * y + c
        mine = pltpu.make_async_copy(v_ref, slots.at[me], lsem)
        mine.start()
        cps = []
        for r in range(1, N_DEV):
            px = x ^ (r >> 2)
            py = y ^ ((r >> 1) & 1)
            pc = c ^ (r & 1)
            cps.append(pltpu.make_async_remote_copy(
                src_ref=v_ref, dst_ref=slots.at[me], send_sem=send.at[r - 1], recv_sem=recv.at[r - 1],
                device_id=(px, py, pc), device_id_type=MESH))
        for cp in cps:
            cp.start()
        for cp in cps:
            cp.wait()
        mine.wait()
        acc = slots[0]
        for d in range(1, N_DEV):
            acc = acc + slots[d]
        out_ref[...] = acc
        loss = jnp.sum(acc[9:10, :], axis=1, keepdims=True) * (0.5 / D_MODEL)
        out_ref[9:10, :] = jnp.broadcast_to(loss, (1, PACK_COLS))

    vm = pl.BlockSpec(memory_space=pltpu.VMEM)
    return pl.pallas_call(
        body, name="allreduce_small",
        in_specs=[vm], out_specs=vm, out_shape=_sds((SMALL_ROWS, PACK_COLS), F32),
        scratch_shapes=[pltpu.VMEM((N_DEV, SMALL_ROWS, PACK_COLS), F32), pltpu.SemaphoreType.DMA((N_DEV - 1,)),
                        pltpu.SemaphoreType.DMA((N_DEV - 1,)), pltpu.SemaphoreType.DMA],
    )(vec)


N_BIG = len(BIG)


def _half(c, rows, align):
    h = rows // 2
    return pl.ds(pl.multiple_of(c * h, align), h)


def _gather_shards(shards):
    n = len(shards)

    def body(*refs):
        ins, outs = refs[:n], refs[n:2 * n]
        send1, recv1, send2, recv2, lsem = refs[2 * n:]
        x, y, c = _my_place()
        me = 2 * x + y
        chips = _other_chips(x, y)
        sib = (x, y, 1 - c)
        local, first, passed = [], [], []
        for t in range(n):
            rows = ins[t].shape[0]
            half = _half(c, rows, 16)
            cp = pltpu.make_async_copy(ins[t], outs[t].at[me], lsem.at[t])
            cp.start()
            local.append(cp)
            for k, (cx, cy) in enumerate(chips):
                rc = pltpu.make_async_remote_copy(
                    src_ref=ins[t].at[half], dst_ref=outs[t].at[me, half], send_sem=send1.at[t, k],
                    recv_sem=recv1.at[t, k], device_id=(cx, cy, c), device_id_type=MESH)
                rc.start()
                first.append(rc)
        for k, (cx, cy) in enumerate(chips):
            src = 2 * cx + cy
            for t in range(n):
                half = _half(c, ins[t].shape[0], 16)
                pltpu.make_async_remote_copy(
                    src_ref=ins[t].at[half], dst_ref=outs[t].at[src, half], send_sem=send1.at[t, k],
                    recv_sem=recv1.at[t, k], device_id=(cx, cy, c), device_id_type=MESH).wait_recv()
                fw = pltpu.make_async_remote_copy(
                    src_ref=outs[t].at[src, half], dst_ref=outs[t].at[src, half], send_sem=send2.at[t, k],
                    recv_sem=recv2.at[t, k], device_id=sib, device_id_type=MESH)
                fw.start()
                passed.append(fw)
        for k, (cx, cy) in enumerate(chips):
            src = 2 * cx + cy
            for t in range(n):
                other = _half(1 - c, ins[t].shape[0], 16)
                pltpu.make_async_remote_copy(
                    src_ref=outs[t].at[src, other], dst_ref=outs[t].at[src, other], send_sem=send2.at[t, k],
                    recv_sem=recv2.at[t, k], device_id=sib, device_id_type=MESH).wait_recv()
        for cp in first + passed:
            cp.wait_send()
        for cp in local:
            cp.wait()

    return pl.pallas_call(
        body, name="gather_weights",
        in_specs=[_ANY] * n, out_specs=[_ANY] * n,
        out_shape=[_sds((N_CHIPS,) + s.shape, BF16) for s in shards],
        scratch_shapes=[pltpu.SemaphoreType.DMA((n, 3))] * 4 + [pltpu.SemaphoreType.DMA((n,))],
    )(*shards)


def _swap_half_rows(gs):
    n = len(gs)

    def body(*refs):
        ins, outs = refs[:n], refs[n:2 * n]
        send, recv = refs[2 * n:]
        x, y, c = _my_place()
        cps = []
        for t in range(n):
            other = _half(1 - c, ins[t].shape[1], 8)
            cp = pltpu.make_async_remote_copy(
                src_ref=ins[t].at[:, other], dst_ref=outs[t], send_sem=send.at[t], recv_sem=recv.at[t],
                device_id=(x, y, 1 - c), device_id_type=MESH)
            cp.start()
            cps.append(cp)
        for cp in cps:
            cp.wait()

    return pl.pallas_call(
        body, name="rs_swap_halves",
        in_specs=[_ANY] * n, out_specs=[_ANY] * n,
        out_shape=[_sds((N_CHIPS, g.shape[1] // 2, g.shape[2]), F32) for g in gs],
        scratch_shapes=[pltpu.SemaphoreType.DMA((n,)), pltpu.SemaphoreType.DMA((n,))],
    )(*gs)


def _add_half_rows(g, got, c_idx, name):
    _, rows, cols = g.shape
    h = rows // 2

    def body(c_ref, a_ref, b_ref, o_ref):
        o_ref[...] = (a_ref[...] + b_ref[...]).astype(BF16)

    grid_spec = pltpu.PrefetchScalarGridSpec(
        num_scalar_prefetch=1, grid=(N_CHIPS,),
        in_specs=[pl.BlockSpec((None, h, cols), lambda j, c: (j, c[0], 0)),
                  pl.BlockSpec((None, h, cols), lambda j, c: (j, 0, 0))],
        out_specs=pl.BlockSpec((None, h, cols), lambda j, c: (j, 0, 0)),
    )
    return pl.pallas_call(
        body, name=name, grid_spec=grid_spec, out_shape=_sds((N_CHIPS, h, cols), BF16),
        compiler_params=_cp(("parallel",)),
    )(c_idx, g, got)


def _scatter_to_chips(ts):
    n = len(ts)

    def body(*refs):
        ins, outs = refs[:n], refs[n:2 * n]
        send, recv, lsem = refs[2 * n:]
        x, y, c = _my_place()
        me = 2 * x + y
        chips = _other_chips(x, y)
        cps, local = [], []
        for t in range(n):
            cp = pltpu.make_async_copy(ins[t].at[me], outs[t].at[me], lsem.at[t])
            cp.start()
            local.append(cp)
            for k, (cx, cy) in enumerate(chips):
                rc = pltpu.make_async_remote_copy(
                    src_ref=ins[t].at[2 * cx + cy], dst_ref=outs[t].at[me], send_sem=send.at[t, k],
                    recv_sem=recv.at[t, k], device_id=(cx, cy, c), device_id_type=MESH)
                rc.start()
                cps.append(rc)
        for cp in cps:
            cp.wait()
        for cp in local:
            cp.wait()

    return pl.pallas_call(
        body, name="rs_scatter_chips",
        in_specs=[_ANY] * n, out_specs=[_ANY] * n, out_shape=[_sds(t.shape, BF16) for t in ts],
        scratch_shapes=[pltpu.SemaphoreType.DMA((n, 3)), pltpu.SemaphoreType.DMA((n, 3)), pltpu.SemaphoreType.DMA((n,))],
    )(*ts)


def _add_four(parts, name):
    _, h, cols = parts.shape

    def body(p_ref, o_ref):
        o_ref[...] = ((p_ref[0].astype(F32) + p_ref[1].astype(F32)) + p_ref[2].astype(F32)) + p_ref[3].astype(F32)

    return pl.pallas_call(
        body, name=name, grid=(1,),
        in_specs=[pl.BlockSpec((N_CHIPS, h, cols), lambda i: (0, 0, 0))],
        out_specs=pl.BlockSpec((h, cols), lambda i: (0, 0)), out_shape=_sds((h, cols), F32),
        compiler_params=_cp(("arbitrary",)),
    )(parts)


def _join_half_rows(rs):
    n = len(rs)

    def body(*refs):
        ins, outs = refs[:n], refs[n:2 * n]
        send, recv, lsem = refs[2 * n:]
        x, y, c = _my_place()
        cps = []
        for t in range(n):
            half = _half(c, outs[t].shape[0], 8)
            cp = pltpu.make_async_copy(ins[t], outs[t].at[half], lsem.at[t])
            cp.start()
            cps.append(cp)
            rc = pltpu.make_async_remote_copy(
                src_ref=ins[t], dst_ref=outs[t].at[half], send_sem=send.at[t], recv_sem=recv.at[t],
                device_id=(x, y, 1 - c), device_id_type=MESH)
            rc.start()
            cps.append(rc)
        for cp in cps:
            cp.wait()

    return pl.pallas_call(
        body, name="rs_join_halves",
        in_specs=[_ANY] * n, out_specs=[_ANY] * n,
        out_shape=[_sds((2 * r.shape[0], r.shape[1]), F32) for r in rs],
        scratch_shapes=[pltpu.SemaphoreType.DMA((n,))] * 3,
    )(*rs)


def _by_chip(full, rows, cols, axis):
    if axis == 0:
        return full.reshape(N_CHIPS, rows // N_CHIPS, cols)
    return full.reshape(rows, N_CHIPS, cols // N_CHIPS).transpose(1, 0, 2)


def _from_chips(parts, axis):
    _, r, c = parts.shape
    if axis == 0:
        return parts.reshape(N_CHIPS * r, c)
    return parts.transpose(1, 0, 2).reshape(r, N_CHIPS * c)


def _adamw(wt, g, m, v, name):
    R, C = wt.shape
    tr = R
    for cand in (256, 128, 64, 32, 16, 8):
        if R % cand == 0:
            tr = cand
            break

    def body(w_ref, g_ref, m_ref, v_ref, d_ref, nm_ref, nv_ref):
        gg = g_ref[...]
        m_new = ADAM_B1 * m_ref[...] + (1.0 - ADAM_B1) * gg
        v_new = ADAM_B2 * v_ref[...] + (1.0 - ADAM_B2) * (gg * gg)
        m_hat = m_new / (1.0 - ADAM_B1 ** ADAM_STEP)
        v_hat = v_new / (1.0 - ADAM_B2 ** ADAM_STEP)
        d_ref[...] = -ADAM_LR * (m_hat / (jnp.sqrt(v_hat) + ADAM_EPS) + ADAM_WD * w_ref[...])
        nm_ref[...] = m_new
        nv_ref[...] = v_new

    spec = pl.BlockSpec((tr, C), lambda i: (i, 0))
    return pl.pallas_call(
        body, name=name, grid=(R // tr,), in_specs=[spec] * 4, out_specs=[spec] * 3, out_shape=[_sds((R, C), F32)] * 3,
        compiler_params=_cp(("parallel",)),
    )(wt, g, m, v)


def _shard_shape(r, c, axis):
    return (r // N_CHIPS, c) if axis == 0 else (r, c // N_CHIPS)


def _pack_rows(flat):
    return jnp.pad(flat, (0, PACK_ROWS * PACK_COLS - flat.shape[0])).reshape(PACK_ROWS, PACK_COLS)


def _pack_shards(mats):
    return _pack_rows(jnp.concatenate([mats[n].reshape(-1) for n, _, _, _ in BIG]))


def _unpack_shards(pk):
    flat = pk.reshape(-1)
    out = {}
    off = 0
    for n, r, c, ax in BIG:
        shp = _shard_shape(r, c, ax)
        sz = shp[0] * shp[1]
        out[n] = flat[off:off + sz].reshape(shp)
        off += sz
    return out


def _full_from_packs(allpk):
    per_chip = [_unpack_shards(allpk[j]) for j in range(N_CHIPS)]
    return {n: jnp.concatenate([per_chip[j][n] for j in range(N_CHIPS)], axis=ax) for n, _, _, ax in BIG}


def _packs_from_full(gw):
    packs = []
    for j in range(N_CHIPS):
        shards = {}
        for n, r, c, ax in BIG:
            shp = _shard_shape(r, c, ax)
            shards[n] = gw[n][j * shp[0]:(j + 1) * shp[0], :] if ax == 0 else gw[n][:, j * shp[1]:(j + 1) * shp[1]]
        packs.append(_pack_shards(shards))
    return jnp.stack(packs)


def _pack_small(vals, loss_vec=None):
    rows = [jnp.pad(vals[n].reshape(-1), (0, PACK_COLS - sz)) for n, sz in SMALL]
    rows.append(loss_vec.reshape(-1) if loss_vec is not None else jnp.zeros((PACK_COLS,), F32))
    rows += [jnp.zeros((PACK_COLS,), F32)] * (SMALL_ROWS - len(rows))
    return jnp.stack(rows)


def kernel(x, p, positions, pre_mix_norm, w_in, ret_gn_w, mla_q_norm, w_uq, mla_kv_norm, w_ukv, w_o, post_mix_norm, pre_ffn_norm, w_gate, w_up, w_down, post_ffn_norm, w_ple_proj, ple_norm, w_ple_gate, b_ple_gate, loss_target, m_pre_mix_norm, m_w_in, m_ret_gn_w, m_mla_q_norm, m_w_uq, m_mla_kv_norm, m_w_ukv, m_w_o, m_post_mix_norm, m_pre_ffn_norm, m_w_gate, m_w_up, m_w_down, m_post_ffn_norm, m_w_ple_proj, m_ple_norm, m_w_ple_gate, m_b_ple_gate, v_pre_mix_norm, v_w_in, v_ret_gn_w, v_mla_q_norm, v_w_uq, v_mla_kv_norm, v_w_ukv, v_w_o, v_post_mix_norm, v_pre_ffn_norm, v_w_gate, v_w_up, v_w_down, v_post_ffn_norm, v_w_ple_proj, v_ple_norm, v_w_ple_gate, v_b_ple_gate):
    wts = dict(pre_mix_norm=pre_mix_norm, w_in=w_in, ret_gn_w=ret_gn_w, mla_q_norm=mla_q_norm, w_uq=w_uq,
               mla_kv_norm=mla_kv_norm, w_ukv=w_ukv, w_o=w_o, post_mix_norm=post_mix_norm, pre_ffn_norm=pre_ffn_norm,
               w_gate=w_gate, w_up=w_up, w_down=w_down, post_ffn_norm=post_ffn_norm, w_ple_proj=w_ple_proj,
               ple_norm=ple_norm, w_ple_gate=w_ple_gate, b_ple_gate=b_ple_gate)
    mom = dict(pre_mix_norm=m_pre_mix_norm, w_in=m_w_in, ret_gn_w=m_ret_gn_w, mla_q_norm=m_mla_q_norm, w_uq=m_w_uq,
               mla_kv_norm=m_mla_kv_norm, w_ukv=m_w_ukv, w_o=m_w_o, post_mix_norm=m_post_mix_norm,
               pre_ffn_norm=m_pre_ffn_norm, w_gate=m_w_gate, w_up=m_w_up, w_down=m_w_down, post_ffn_norm=m_post_ffn_norm,
               w_ple_proj=m_w_ple_proj, ple_norm=m_ple_norm, w_ple_gate=m_w_ple_gate, b_ple_gate=m_b_ple_gate)
    var = dict(pre_mix_norm=v_pre_mix_norm, w_in=v_w_in, ret_gn_w=v_ret_gn_w, mla_q_norm=v_mla_q_norm, w_uq=v_w_uq,
               mla_kv_norm=v_mla_kv_norm, w_ukv=v_w_ukv, w_o=v_w_o, post_mix_norm=v_post_mix_norm,
               pre_ffn_norm=v_pre_ffn_norm, w_gate=v_w_gate, w_up=v_w_up, w_down=v_w_down, post_ffn_norm=v_post_ffn_norm,
               w_ple_proj=v_w_ple_proj, ple_norm=v_ple_norm, w_ple_gate=v_w_ple_gate, b_ple_gate=v_b_ple_gate)

    S = x.shape[1]
    shard2d = {n: wts[n][0] for n, _, _, _ in BIG}
    small2d = {n: wts[n] for n, _ in SMALL}

    gathered = _gather_shards([shard2d[n].astype(BF16) for n, _, _, _ in BIG])
    w_full = {n: _from_chips(gathered[i], ax) for i, (n, _, _, ax) in enumerate(BIG)}

    pos_f = positions.astype(F32).reshape(S, 1)
    loss_vec, grad_x, gw, gs = _local_step(x[0], p[0, 0], pos_f, loss_target[0], w_full, small2d)

    g4 = [_by_chip(gw[n], r, c, ax) for n, r, c, ax in BIG]
    c_idx = lax.axis_index("c").astype(jnp.int32).reshape(1)
    got = _swap_half_rows(g4)
    chip_sum = [_add_half_rows(g4[i], got[i], c_idx, "rs_add_halves_" + BIG[i][0]) for i in range(N_BIG)]
    parts = _scatter_to_chips(chip_sum)
    reduced = _join_half_rows([_add_four(parts[i], "rs_add_chips_" + BIG[i][0]) for i in range(N_BIG)])
    g_shard = {n: reduced[i] for i, (n, _, _, _) in enumerate(BIG)}

    small_sum = _allreduce_small(_pack_small(gs, loss_vec))
    loss = small_sum[9, 0]
    g_small = {n: small_sum[i:i + 1, :sz] for i, (n, sz) in enumerate(SMALL)}

    grads, delta, new_m, new_v = {}, {}, {}, {}
    for n, _, _, _ in BIG:
        d, nm, nv = _adamw(shard2d[n], g_shard[n], mom[n][0], var[n][0], "adamw_" + n)
        grads[n], delta[n], new_m[n], new_v[n] = g_shard[n][None], d[None], nm[None], nv[None]
    d, nm, nv = _adamw(_pack_small(small2d), small_sum, _pack_small(mom), _pack_small(var), "adamw_small")
    for i, (n, sz) in enumerate(SMALL):
        grads[n] = g_small[n]
        delta[n], new_m[n], new_v[n] = d[i:i + 1, :sz], nm[i:i + 1, :sz], nv[i:i + 1, :sz]

    return (loss, grad_x[None], *[grads[n] for n in ALL_W], *[delta[n] for n in ALL_W],
            *[new_m[n] for n in ALL_W], *[new_v[n] for n in ALL_W])
```

```python
import functools
import math

import jax
import jax.numpy as jnp
import numpy as np
from jax import lax
from jax.experimental import pallas as pl
from jax.experimental.pallas import tpu as pltpu

F32 = jnp.float32
BF16 = jnp.bfloat16
MESH = pl.DeviceIdType.MESH

D_MODEL = 1024
D_FF = 2816
PLE_DIM = 256
RET_HEADS = 4
RET_DIM = 128
RET_WIDTH = 512
RET_CHUNK = 128
RET_GROUP = 8
MLA_HEADS = 8
MLA_NOPE = 64
MLA_ROPE = 32
MLA_V = 64
Q_LORA = 384
KV_LORA = 256
IN_COLS = 2720
IN_COLS_P = 2816
ROPE_BASE = 10000.0
EPS = 1e-6
SCALE_MLA = 1.0 / math.sqrt(MLA_NOPE + MLA_ROPE)
SCALE_RET = RET_DIM ** -0.5
NEG = -1e30

ADAM_LR = 0.001
ADAM_B1 = 0.9
ADAM_B2 = 0.999
ADAM_EPS = 1e-08
ADAM_WD = 0.01
ADAM_STEP = 10

N_CHIPS = 4
N_DEV = 8
VMEM_MB = 56

BIG = (
    ("w_in", 1024, 2720, 1),
    ("w_uq", 384, 768, 1),
    ("w_ukv", 256, 1024, 1),
    ("w_o", 1024, 1024, 0),
    ("w_gate", 1024, 2816, 1),
    ("w_up", 1024, 2816, 1),
    ("w_down", 2816, 1024, 0),
    ("w_ple_proj", 256, 1024, 1),
    ("w_ple_gate", 1024, 1024, 0),
)
SMALL = (
    ("pre_mix_norm", 1024),
    ("ret_gn_w", 512),
    ("mla_q_norm", 384),
    ("mla_kv_norm", 256),
    ("post_mix_norm", 1024),
    ("pre_ffn_norm", 1024),
    ("post_ffn_norm", 1024),
    ("ple_norm", 1024),
    ("b_ple_gate", 1024),
)
ALL_W = ("pre_mix_norm", "w_in", "ret_gn_w", "mla_q_norm", "w_uq", "mla_kv_norm", "w_ukv", "w_o", "post_mix_norm",
         "pre_ffn_norm", "w_gate", "w_up", "w_down", "post_ffn_norm", "w_ple_proj", "ple_norm", "w_ple_gate", "b_ple_gate")
PACK_COLS = 1024
SHARD_ELEMS = sum(r * c for _, r, c, _ in BIG) // N_CHIPS
PACK_ROWS = -(-SHARD_ELEMS // PACK_COLS // 32) * 32
HALF_ROWS = PACK_ROWS // 2
SMALL_ROWS = 16


def _cp(sem=None, mb=VMEM_MB, **kw):
    return pltpu.CompilerParams(dimension_semantics=sem, vmem_limit_bytes=mb * 1024 * 1024, **kw)


def _bf(x):
    return x.astype(BF16)


def _dot(a, b):
    return jnp.dot(_bf(a), _bf(b), preferred_element_type=F32)


def _dot_nt(a, b):
    return lax.dot_general(_bf(a), _bf(b), (((1,), (1,)), ((), ())), preferred_element_type=F32)


def _dot_tn(a, b):
    return lax.dot_general(_bf(a), _bf(b), (((0,), (0,)), ((), ())), preferred_element_type=F32)


def _sig(x):
    return 1.0 / (1.0 + jnp.exp(-x))


def _rms(x, g):
    r = lax.rsqrt(jnp.mean(x * x, axis=-1, keepdims=True) + EPS)
    return x * r * g


def _rms_bwd(dy, x, g):
    r = lax.rsqrt(jnp.mean(x * x, axis=-1, keepdims=True) + EPS)
    xh = x * r
    dxh = dy * g
    dx = r * (dxh - xh * jnp.mean(dxh * xh, axis=-1, keepdims=True))
    return dx, dy * xh


def _colsum(x):
    return jnp.sum(x, axis=0, keepdims=True)


def _rope_ret(x, cr, sr):
    return x * cr + pltpu.roll(x, 64, 1) * sr


def _unrope_ret(dy, cr, sr):
    return dy * cr + pltpu.roll(dy * sr, 64, 1)


def _rope_mla(x, cm, sa, sb):
    return x * cm + pltpu.roll(x, 112, 1) * sa + pltpu.roll(x, 16, 1) * sb


def _unrope_mla(dy, cm, sa, sb):
    return dy * cm + pltpu.roll(dy * sa, 16, 1) + pltpu.roll(dy * sb, 112, 1)


def _rows(tm, w, col=0):
    return pl.BlockSpec((tm, w), lambda i: (i, col))


def _full(*shape):
    return pl.BlockSpec(shape, lambda i: (0,) * len(shape))


def _sds(shape, dtype):
    return jax.ShapeDtypeStruct(shape, dtype)


def _rope_tables(pos_f, S):
    tm = min(512, S)
    inv_r = (1.0 / (np.float32(ROPE_BASE) ** (np.arange(64, dtype=np.float32) / np.float32(64)))).astype(np.float32)
    inv_m16 = (1.0 / (np.float32(ROPE_BASE) ** (np.arange(16, dtype=np.float32) / np.float32(16)))).astype(np.float32)
    inv_r = np.concatenate([inv_r, inv_r])[None, :]
    inv_m = np.zeros((1, 128), np.float32)
    inv_m[0, 64:80] = inv_m16
    inv_m[0, 80:96] = inv_m16

    def body(pos_ref, invr_ref, invm_ref, cr_ref, sr_ref, cm_ref, sa_ref, sb_ref):
        pos = pos_ref[...]
        lane = lax.broadcasted_iota(jnp.int32, (tm, 128), 1)
        ar = pos * invr_ref[...]
        s = jnp.sin(ar)
        cr_ref[...] = jnp.cos(ar)
        sr_ref[...] = jnp.where(lane < 64, -s, s)
        am = pos * invm_ref[...]
        c2 = jnp.cos(am)
        s2 = jnp.sin(am)
        cm_ref[...] = jnp.where(lane < 64, 1.0, jnp.where(lane < 96, c2, 0.0))
        sa_ref[...] = jnp.where((lane >= 64) & (lane < 80), -s2, 0.0)
        sb_ref[...] = jnp.where((lane >= 80) & (lane < 96), s2, 0.0)

    return pl.pallas_call(
        body, name="rope_tables", grid=(S // tm,),
        in_specs=[_rows(tm, 1), _full(1, 128), _full(1, 128)],
        out_specs=[_rows(tm, 128)] * 5,
        out_shape=[_sds((S, 128), F32)] * 5,
        compiler_params=_cp(("parallel",)),
    )(pos_f, jnp.asarray(inv_r), jnp.asarray(inv_m))


def _rms_fwd(x, g, S):
    tm = min(512, S)

    def body(x_ref, g_ref, o_ref):
        o_ref[...] = _rms(x_ref[...], g_ref[...]).astype(BF16)

    return pl.pallas_call(
        body, name="rms_pre", grid=(S // tm,),
        in_specs=[_rows(tm, D_MODEL), _full(1, D_MODEL)],
        out_specs=_rows(tm, D_MODEL), out_shape=_sds((S, D_MODEL), BF16),
        compiler_params=_cp(("parallel",)),
    )(x, g)


def _inproj(xn, w_in, tabs, S):
    tm = min(256, S)

    def body(xn_ref, w_ref, cr_ref, sr_ref, cm_ref, sa_ref, sb_ref, rq_ref, rk_ref, rv_ref, rg_ref, cq_ref, ckv_ref, kr_ref):
        xb = xn_ref[...]
        cr = cr_ref[...]
        sr = sr_ref[...]
        q = jnp.dot(xb, w_ref[:, 0:512], preferred_element_type=F32)
        k = jnp.dot(xb, w_ref[:, 512:1024], preferred_element_type=F32)
        for h in range(RET_HEADS):
            sl = slice(h * 128, (h + 1) * 128)
            rq_ref[:, sl] = _rope_ret(q[:, sl], cr, sr).astype(BF16)
            rk_ref[:, sl] = (_rope_ret(k[:, sl], cr, sr) * SCALE_RET).astype(BF16)
        rv_ref[...] = jnp.dot(xb, w_ref[:, 1024:1536], preferred_element_type=F32).astype(BF16)
        rg_ref[...] = jnp.dot(xb, w_ref[:, 1536:2048], preferred_element_type=F32)
        cq_ref[...] = jnp.dot(xb, w_ref[:, 2048:2432], preferred_element_type=F32)
        ckv_ref[...] = jnp.dot(xb, w_ref[:, 2432:2688], preferred_element_type=F32)
        kr = jnp.dot(xb, w_ref[:, 2688:2816], preferred_element_type=F32)
        kr_ref[...] = _rope_mla(kr, cm_ref[...], sa_ref[...], sb_ref[...])

    return pl.pallas_call(
        body, name="inproj", grid=(S // tm,),
        in_specs=[_rows(tm, D_MODEL), _full(D_MODEL, IN_COLS_P)] + [_rows(tm, 128)] * 5,
        out_specs=[_rows(tm, 512)] * 4 + [_rows(tm, Q_LORA), _rows(tm, KV_LORA), _rows(tm, 128)],
        out_shape=[_sds((S, 512), BF16)] * 3 + [_sds((S, 512), F32), _sds((S, Q_LORA), F32), _sds((S, KV_LORA), F32),
                                                  _sds((S, 128), F32)],
        compiler_params=_cp(("parallel",)),
    )(xn, w_in, *tabs)


def _mla_up(cq, ckv, kr, gq, gkv, w_uq, w_ukv, tabs, S):
    tm = min(256, S)

    def body(cq_ref, ckv_ref, kr_ref, gq_ref, gkv_ref, wuq_ref, wukv_ref, cm_ref, sa_ref, sb_ref,
             cqn_ref, ckvn_ref, qp_ref, kp_ref, v_ref):
        cm = cm_ref[...]
        sa = sa_ref[...]
        sb = sb_ref[...]
        cqn = _rms(cq_ref[...], gq_ref[...]).astype(BF16)
        cqn_ref[...] = cqn
        ckvn = _rms(ckv_ref[...], gkv_ref[...]).astype(BF16)
        ckvn_ref[...] = ckvn
        qh = jnp.dot(cqn, wuq_ref[...], preferred_element_type=F32)
        kv = jnp.dot(ckvn, wukv_ref[...], preferred_element_type=F32)
        kr_blk = kr_ref[...]
        for h in range(MLA_HEADS):
            sl = slice(h * 128, (h + 1) * 128)
            qp_ref[:, sl] = (_rope_mla(qh[:, sl], cm, sa, sb) * SCALE_MLA).astype(BF16)
            kp_ref[:, sl] = (kv[:, sl] + kr_blk).astype(BF16)
        v_ref[...] = kv[:, 1024:1536].astype(BF16)

    return pl.pallas_call(
        body, name="mla_up", grid=(S // tm,),
        in_specs=[_rows(tm, Q_LORA), _rows(tm, KV_LORA), _rows(tm, 128), _full(1, Q_LORA), _full(1, KV_LORA),
                  _full(Q_LORA, 1024), _full(KV_LORA, 1536)] + [_rows(tm, 128)] * 3,
        out_specs=[_rows(tm, Q_LORA), _rows(tm, KV_LORA), _rows(tm, 1024), _rows(tm, 1024), _rows(tm, 512)],
        out_shape=[_sds((S, Q_LORA), BF16), _sds((S, KV_LORA), BF16), _sds((S, 1024), BF16), _sds((S, 1024), BF16),
                   _sds((S, 512), BF16)],
        compiler_params=_cp(("parallel",)),
    )(cq, ckv, kr, gq, gkv, w_uq, w_ukv, *tabs[2:])


def _tri_pairs(nq, k_major):
    if k_major:
        pairs = [(qb, kb) for kb in range(nq) for qb in range(kb, nq)]
    else:
        pairs = [(qb, kb) for qb in range(nq) for kb in range(qb + 1)]
    qb_of = np.array([p[0] for p in pairs], np.int32)
    kb_of = np.array([p[1] for p in pairs], np.int32)
    return jnp.asarray(qb_of), jnp.asarray(kb_of), len(pairs)


def _flash_fwd(qp, kp, v, S):
    tq = min(512, S)
    nq = S // tq
    qb_of, kb_of, T = _tri_pairs(nq, k_major=False)

    def body(qb_ref, kb_ref, q_ref, k_ref, v_ref, o_ref, lse_ref, m_sc, l_sc, acc_sc):
        t = pl.program_id(1)
        qb = qb_ref[t]
        kb = kb_ref[t]

        @pl.when(kb == 0)
        def _():
            m_sc[...] = jnp.full(m_sc.shape, NEG, F32)
            l_sc[...] = jnp.zeros(l_sc.shape, F32)
            acc_sc[...] = jnp.zeros(acc_sc.shape, F32)

        lane = lax.broadcasted_iota(jnp.int32, (tq, 128), 1)
        first = lane < 64
        rep = tq // 128

        def step(masked):
            vv = v_ref[...]
            pv = []
            alpha = []
            for a in range(2):
                sl = slice(a * 128, (a + 1) * 128)
                s = _dot_nt(q_ref[:, sl], k_ref[:, sl])
                if masked:
                    row = lax.broadcasted_iota(jnp.int32, (tq, tq), 0)
                    col = lax.broadcasted_iota(jnp.int32, (tq, tq), 1)
                    s = jnp.where(col <= row, s, NEG)
                m_prev = m_sc[a]
                m_new = jnp.maximum(m_prev, jnp.max(s, axis=1, keepdims=True))
                al = jnp.exp(m_prev - m_new)
                p = jnp.exp(s - jnp.tile(m_new, (1, rep)))
                l_sc[a] = al * l_sc[a] + jnp.sum(p, axis=1, keepdims=True)
                m_sc[a] = m_new
                pv.append(jnp.dot(p.astype(BF16), vv, preferred_element_type=F32))
                alpha.append(al)
            acc_sc[...] = acc_sc[...] * jnp.where(first, alpha[0], alpha[1]) + jnp.where(first, pv[0], pv[1])

        @pl.when(kb < qb)
        def _():
            step(False)

        @pl.when(kb == qb)
        def _():
            step(True)
            o_ref[...] = (acc_sc[...] / jnp.where(first, l_sc[0], l_sc[1])).astype(BF16)
            for a in range(2):
                lse_ref[:, a * 128:(a + 1) * 128] = m_sc[a] + jnp.log(l_sc[a])

    grid_spec = pltpu.PrefetchScalarGridSpec(
        num_scalar_prefetch=2, grid=(MLA_HEADS // 2, T),
        in_specs=[pl.BlockSpec((tq, 256), lambda j, t, qb, kb: (qb[t], j)),
                  pl.BlockSpec((tq, 256), lambda j, t, qb, kb: (kb[t], j)),
                  pl.BlockSpec((tq, 128), lambda j, t, qb, kb: (kb[t], j))],
        out_specs=[pl.BlockSpec((tq, 128), lambda j, t, qb, kb: (qb[t], j)),
                   pl.BlockSpec((tq, 256), lambda j, t, qb, kb: (qb[t], j))],
        scratch_shapes=[pltpu.VMEM((2, tq, 128), F32), pltpu.VMEM((2, tq, 128), F32), pltpu.VMEM((tq, 128), F32)],
    )
    return pl.pallas_call(
        body, name="flash_fwd", grid_spec=grid_spec,
        out_shape=[_sds((S, 512), BF16), _sds((S, 1024), F32)],
        compiler_params=_cp(("parallel", "arbitrary")),
    )(qb_of, kb_of, qp, kp, v)


def _decay_table():
    log_g = np.log(1.0 - 2.0 ** (-5.0 - np.arange(RET_HEADS, dtype=np.float32))).astype(np.float32)
    return jnp.asarray(np.broadcast_to(log_g[:, None, None], (RET_HEADS, 8, 128)).copy())


def _decay_terms(lg_ref):
    C = RET_CHUNK
    lg = lg_ref[0:1, :]
    row = lax.broadcasted_iota(jnp.int32, (C, C), 0)
    col = lax.broadcasted_iota(jnp.int32, (C, C), 1)
    diff = (row - col).astype(F32)
    dmat = jnp.where(diff >= 0, jnp.exp(jnp.maximum(diff, 0.0) * lg), 0.0)
    j = lax.broadcasted_iota(jnp.int32, (C, 1), 0).astype(F32)
    lg1 = lg[:, 0:1]
    zeta = jnp.exp((C - 1 - j) * lg1)
    xi = jnp.exp((j + 1.0) * lg1)
    g_chunk = jnp.exp(C * lg1)
    return dmat, zeta, xi, g_chunk


def _ret_fwd(rq, rk, rv, rg, gn_w, S):
    C = RET_CHUNK
    N = S // C
    G = min(RET_GROUP, N)
    NB = N // G

    def body(lg_ref, q_ref, k_ref, v_ref, rg_ref, w_ref, ry_ref, ro_ref, rprev_ref, r_sc):
        @pl.when(pl.program_id(1) == 0)
        def _():
            r_sc[...] = jnp.zeros(r_sc.shape, F32)

        dmat, zeta, xi, g_chunk = _decay_terms(lg_ref)
        w = w_ref[...]
        r = r_sc[...]
        for i in range(G):
            rows = slice(i * C, (i + 1) * C)
            q = q_ref[rows, :]
            k = k_ref[rows, :]
            v = v_ref[rows, :]
            r_prev = r.astype(BF16)
            rprev_ref[i] = r_prev
            sc = _dot_nt(q, k) * dmat
            ry = _dot(sc, v) + jnp.dot(q, r_prev, preferred_element_type=F32) * xi
            ry_ref[rows, :] = ry
            r = g_chunk * r + _dot_tn(k, zeta * v.astype(F32))
            mu = jnp.mean(ry, axis=-1, keepdims=True)
            yc = ry - mu
            yh = yc * lax.rsqrt(jnp.mean(yc * yc, axis=-1, keepdims=True) + EPS)
            g = rg_ref[rows, :]
            ro_ref[rows, :] = (g * _sig(g) * (yh * w)).astype(BF16)
        r_sc[...] = r

    blk = pl.BlockSpec((G * C, 128), lambda h, n: (n, h))
    return pl.pallas_call(
        body, name="ret_fwd", grid=(RET_HEADS, NB),
        in_specs=[pl.BlockSpec((None, 8, 128), lambda h, n: (h, 0, 0)), blk, blk, blk, blk,
                  pl.BlockSpec((1, 128), lambda h, n: (0, h))],
        out_specs=[blk, blk, pl.BlockSpec((G, 128, 128), lambda h, n: (h * NB + n, 0, 0))],
        out_shape=[_sds((S, 512), F32), _sds((S, 512), BF16), _sds((RET_HEADS * N, 128, 128), BF16)],
        scratch_shapes=[pltpu.VMEM((128, 128), F32)],
        compiler_params=_cp(("parallel", "arbitrary")),
    )(_decay_table(), rq, rk, rv, rg, gn_w)


def _outproj(ro, mo, x, w_o, g_post, g_pre, S):
    tm = min(256, S)

    def body(ro_ref, mo_ref, x_ref, wo_ref, g1_ref, g2_ref, mix_ref, h1_ref, hn_ref):
        mix = (jnp.dot(ro_ref[...], wo_ref[0:512, :], preferred_element_type=F32)
               + jnp.dot(mo_ref[...], wo_ref[512:1024, :], preferred_element_type=F32))
        mix_ref[...] = mix
        h1 = x_ref[...] + _rms(mix, g1_ref[...])
        h1_ref[...] = h1
        hn_ref[...] = _rms(h1, g2_ref[...]).astype(BF16)

    return pl.pallas_call(
        body, name="outproj", grid=(S // tm,),
        in_specs=[_rows(tm, 512), _rows(tm, 512), _rows(tm, D_MODEL), _full(D_MODEL, D_MODEL), _full(1, D_MODEL),
                  _full(1, D_MODEL)],
        out_specs=[_rows(tm, D_MODEL)] * 3,
        out_shape=[_sds((S, D_MODEL), F32), _sds((S, D_MODEL), F32), _sds((S, D_MODEL), BF16)],
        compiler_params=_cp(("parallel",)),
    )(ro, mo, x, w_o, g_post, g_pre)


def _ffn_up(hn, w_gate, w_up, S):
    tm = min(512, S)
    tn = D_FF // 2

    def body(hn_ref, wg_ref, wu_ref, gate_ref, up_ref, act_ref):
        hn_b = hn_ref[...]
        g = jnp.dot(hn_b, wg_ref[...], preferred_element_type=F32)
        u = jnp.dot(hn_b, wu_ref[...], preferred_element_type=F32)
        gate_ref[...] = g.astype(BF16)
        up_ref[...] = u.astype(BF16)
        act_ref[...] = (g * _sig(g) * u).astype(BF16)

    wspec = pl.BlockSpec((D_MODEL, tn), lambda j, i: (0, j))
    ospec = pl.BlockSpec((tm, tn), lambda j, i: (i, j))
    return pl.pallas_call(
        body, name="ffn_up", grid=(2, S // tm),
        in_specs=[pl.BlockSpec((tm, D_MODEL), lambda j, i: (i, 0)), wspec, wspec],
        out_specs=[ospec] * 3, out_shape=[_sds((S, D_FF), BF16)] * 3,
        compiler_params=_cp(("parallel", "parallel")),
    )(hn, w_gate, w_up)


def _ffn_down(act, w_down, h1, g, S):
    tm = min(256, S)

    def body(act_ref, wd_ref, h1_ref, g_ref, ff_ref, h2_ref):
        ff = jnp.dot(act_ref[...], wd_ref[...], preferred_element_type=F32)
        ff_ref[...] = ff
        h2_ref[...] = h1_ref[...] + _rms(ff, g_ref[...])

    return pl.pallas_call(
        body, name="ffn_down", grid=(S // tm,),
        in_specs=[_rows(tm, D_FF), _full(D_FF, D_MODEL), _rows(tm, D_MODEL), _full(1, D_MODEL)],
        out_specs=[_rows(tm, D_MODEL)] * 2, out_shape=[_sds((S, D_MODEL), F32)] * 2,
        compiler_params=_cp(("parallel",)),
    )(act, w_down, h1, g)


def _ple_loss(p, h2, tgt, w_pp, w_pg, b_pg, g_ple, S):
    tm = min(256, S)

    def body(p_ref, h2_ref, t_ref, wp_ref, wg_ref, b_ref, gp_ref,
             dz_ref, dpe_ref, dh2_ref, h2b_ref, loss_ref, dgp_ref, db_ref):
        @pl.when(pl.program_id(0) == 0)
        def _():
            loss_ref[...] = jnp.zeros(loss_ref.shape, F32)
            dgp_ref[...] = jnp.zeros(dgp_ref.shape, F32)
            db_ref[...] = jnp.zeros(db_ref.shape, F32)

        gp = gp_ref[...]
        pe = _dot(p_ref[...], wp_ref[...])
        r = lax.rsqrt(jnp.mean(pe * pe, axis=-1, keepdims=True) + EPS)
        peh = pe * r
        e = peh * gp
        h2 = h2_ref[...]
        h2b = h2.astype(BF16)
        h2b_ref[...] = h2b
        gt = _sig(jnp.dot(h2b, wg_ref[...], preferred_element_type=F32) + b_ref[...])
        diff = h2 + e * gt - t_ref[...]
        loss_ref[...] += _colsum(diff * diff)
        dh3 = diff * (1.0 / D_MODEL)
        de = dh3 * gt
        dz = dh3 * e * gt * (1.0 - gt)
        db_ref[...] += _colsum(dz)
        dgp_ref[...] += _colsum(de * peh)
        dpeh = de * gp
        dpe = r * (dpeh - peh * jnp.mean(dpeh * peh, axis=-1, keepdims=True))
        dzb = dz.astype(BF16)
        dz_ref[...] = dzb
        dpe_ref[...] = dpe.astype(BF16)
        dh2_ref[...] = dh3 + _dot_nt(dzb, wg_ref[...])

    return pl.pallas_call(
        body, name="ple_loss", grid=(S // tm,),
        in_specs=[_rows(tm, PLE_DIM), _rows(tm, D_MODEL), _rows(tm, D_MODEL), _full(PLE_DIM, D_MODEL),
                  _full(D_MODEL, D_MODEL), _full(1, D_MODEL), _full(1, D_MODEL)],
        out_specs=[_rows(tm, D_MODEL)] * 4 + [_full(1, D_MODEL)] * 3,
        out_shape=[_sds((S, D_MODEL), BF16), _sds((S, D_MODEL), BF16), _sds((S, D_MODEL), F32), _sds((S, D_MODEL), BF16)]
        + [_sds((1, D_MODEL), F32)] * 3,
        compiler_params=_cp(("arbitrary",)),
    )(p, h2, tgt, w_pp, w_pg, b_pg, g_ple)


def _wgrad(a, b, name, S):
    M = a.shape[1]
    N = b.shape[1]
    ts = min(512, S)
    nsplit = 2 if M * N >= 2 * 1024 * 1024 else 1
    tn = N // nsplit

    def body(a_ref, b_ref, o_ref):
        @pl.when(pl.program_id(1) == 0)
        def _():
            o_ref[...] = jnp.zeros(o_ref.shape, F32)

        o_ref[...] += _dot_tn(a_ref[...], b_ref[...])

    return pl.pallas_call(
        body, name=name, grid=(nsplit, S // ts),
        in_specs=[pl.BlockSpec((ts, M), lambda j, s: (s, 0)), pl.BlockSpec((ts, tn), lambda j, s: (s, j))],
        out_specs=pl.BlockSpec((M, tn), lambda j, s: (0, j)), out_shape=_sds((M, N), F32),
        compiler_params=_cp(("parallel", "arbitrary")),
    )(a, b)


def _ffn_down_bwd(dh2, ff, g, w_down, gate, up, S):
    tm = min(256, S)
    tn = D_FF // 2

    def body(dh2_ref, ff_ref, g_ref, wd_ref, gate_ref, up_ref, dff_ref, dgate_ref, dup_ref, dg_ref):
        @pl.when(pl.program_id(0) == 0)
        def _():
            dg_ref[...] = jnp.zeros(dg_ref.shape, F32)

        dff, ga = _rms_bwd(dh2_ref[...], ff_ref[...], g_ref[...])
        dg_ref[...] += _colsum(ga)
        dffb = dff.astype(BF16)
        dff_ref[...] = dffb
        for seg in range(2):
            sl = slice(seg * tn, (seg + 1) * tn)
            dact = _dot_nt(dffb, wd_ref[sl, :])
            gt = gate_ref[:, sl].astype(F32)
            u = up_ref[:, sl].astype(F32)
            s = _sig(gt)
            dgate_ref[:, sl] = (dact * u * (s * (1.0 + gt * (1.0 - s)))).astype(BF16)
            dup_ref[:, sl] = (dact * (gt * s)).astype(BF16)

    return pl.pallas_call(
        body, name="ffn_down_bwd", grid=(S // tm,),
        in_specs=[_rows(tm, D_MODEL), _rows(tm, D_MODEL), _full(1, D_MODEL), _full(D_FF, D_MODEL), _rows(tm, D_FF),
                  _rows(tm, D_FF)],
        out_specs=[_rows(tm, D_MODEL), _rows(tm, D_FF), _rows(tm, D_FF), _full(1, D_MODEL)],
        out_shape=[_sds((S, D_MODEL), BF16), _sds((S, D_FF), BF16), _sds((S, D_FF), BF16), _sds((1, D_MODEL), F32)],
        compiler_params=_cp(("arbitrary",)),
    )(dh2, ff, g, w_down, gate, up)


def _ffn_up_bwd(dgate, dup, w_gate, w_up, h1, mix, dh2, g_pre, g_post, w_o, S):
    tm = min(256, S)

    def body(dgate_ref, dup_ref, wg_ref, wu_ref, h1_ref, mix_ref, dh2_ref, g2_ref, g1_ref, wo_ref,
             dh1_ref, dmix_ref, dro_ref, dmo_ref, dg2_ref, dg1_ref):
        @pl.when(pl.program_id(0) == 0)
        def _():
            dg2_ref[...] = jnp.zeros(dg2_ref.shape, F32)
            dg1_ref[...] = jnp.zeros(dg1_ref.shape, F32)

        dhn = _dot_nt(dgate_ref[...], wg_ref[...]) + _dot_nt(dup_ref[...], wu_ref[...])
        d1, ga = _rms_bwd(dhn, h1_ref[...], g2_ref[...])
        dg2_ref[...] += _colsum(ga)
        dh1 = dh2_ref[...] + d1
        dh1_ref[...] = dh1
        dmix, gb = _rms_bwd(dh1, mix_ref[...], g1_ref[...])
        dg1_ref[...] += _colsum(gb)
        dmixb = dmix.astype(BF16)
        dmix_ref[...] = dmixb
        dcat = _dot_nt(dmixb, wo_ref[...])
        dro_ref[...] = dcat[:, 0:512].astype(BF16)
        dmo_ref[...] = dcat[:, 512:1024].astype(BF16)

    return pl.pallas_call(
        body, name="ffn_up_bwd", grid=(S // tm,),
        in_specs=[_rows(tm, D_FF), _rows(tm, D_FF), _full(D_MODEL, D_FF), _full(D_MODEL, D_FF), _rows(tm, D_MODEL),
                  _rows(tm, D_MODEL), _rows(tm, D_MODEL), _full(1, D_MODEL), _full(1, D_MODEL), _full(D_MODEL, D_MODEL)],
        out_specs=[_rows(tm, D_MODEL), _rows(tm, D_MODEL), _rows(tm, 512), _rows(tm, 512), _full(1, D_MODEL),
                   _full(1, D_MODEL)],
        out_shape=[_sds((S, D_MODEL), F32), _sds((S, D_MODEL), BF16), _sds((S, 512), BF16), _sds((S, 512), BF16),
                   _sds((1, D_MODEL), F32), _sds((1, D_MODEL), F32)],
        compiler_params=_cp(("arbitrary",)),
    )(dgate, dup, w_gate, w_up, h1, mix, dh2, g_pre, g_post, w_o)


def _attn_delta(o, do, S):
    tm = min(512, S)

    def body(o_ref, do_ref, d_ref):
        prod = o_ref[...].astype(F32) * do_ref[...].astype(F32)
        for h in range(MLA_HEADS):
            sl = slice(h * 64, (h + 1) * 64)
            d_ref[:, h * 128:(h + 1) * 128] = jnp.broadcast_to(jnp.sum(prod[:, sl], axis=1, keepdims=True), (tm, 128))

    return pl.pallas_call(
        body, name="attn_delta", grid=(S // tm,),
        in_specs=[_rows(tm, 512), _rows(tm, 512)], out_specs=_rows(tm, 1024), out_shape=_sds((S, 1024), F32),
        compiler_params=_cp(("parallel",)),
    )(o, do)


def _flash_bwd(qp, kp, v, do, lse, delta, S):
    tq = min(512, S)
    nq = S // tq
    qb_of, kb_of, T = _tri_pairs(nq, k_major=True)

    def body(qb_ref, kb_ref, q_ref, k_ref, v_ref, do_ref, lse_ref, dl_ref, dq_ref, dk_ref, dv_ref, dk_sc, dv_sc):
        t = pl.program_id(1)
        qb = qb_ref[t]
        kb = kb_ref[t]

        @pl.when(t == 0)
        def _():
            dq_ref[...] = jnp.zeros(dq_ref.shape, F32)

        @pl.when(qb == kb)
        def _():
            dk_sc[...] = jnp.zeros(dk_sc.shape, F32)
            dv_sc[...] = jnp.zeros(dv_sc.shape, F32)

        lane = lax.broadcasted_iota(jnp.int32, (tq, 128), 1)
        rep = tq // 128
        q0 = pl.multiple_of(qb * tq, tq)

        def step(masked):
            vv = v_ref[...]
            do_all = do_ref[...]
            for a in range(2):
                sl = slice(a * 128, (a + 1) * 128)
                q = q_ref[:, sl]
                k = k_ref[:, sl]
                s = _dot_nt(q, k)
                if masked:
                    row = lax.broadcasted_iota(jnp.int32, (tq, tq), 0)
                    col = lax.broadcasted_iota(jnp.int32, (tq, tq), 1)
                    s = jnp.where(col <= row, s, NEG)
                p = jnp.exp(s - jnp.tile(lse_ref[:, sl], (1, rep)))
                do_a = jnp.where((lane < 64) if a == 0 else (lane >= 64), do_all, jnp.zeros_like(do_all))
                dp = _dot_nt(do_a, vv)
                ds = (p * (dp - jnp.tile(dl_ref[:, sl], (1, rep)))).astype(BF16)
                dv_sc[...] += _dot_tn(p, do_a)
                dk_sc[:, sl] += _dot_tn(ds, q)
                dq_ref[pl.ds(q0, tq), sl] += jnp.dot(ds, k, preferred_element_type=F32)

        @pl.when(qb > kb)
        def _():
            step(False)

        @pl.when(qb == kb)
        def _():
            step(True)

        @pl.when(qb == nq - 1)
        def _():
            dk_ref[...] = dk_sc[...]
            dv_ref[...] = dv_sc[...]

    grid_spec = pltpu.PrefetchScalarGridSpec(
        num_scalar_prefetch=2, grid=(MLA_HEADS // 2, T),
        in_specs=[pl.BlockSpec((tq, 256), lambda j, t, qb, kb: (qb[t], j)),
                  pl.BlockSpec((tq, 256), lambda j, t, qb, kb: (kb[t], j)),
                  pl.BlockSpec((tq, 128), lambda j, t, qb, kb: (kb[t], j)),
                  pl.BlockSpec((tq, 128), lambda j, t, qb, kb: (qb[t], j)),
                  pl.BlockSpec((tq, 256), lambda j, t, qb, kb: (qb[t], j)),
                  pl.BlockSpec((tq, 256), lambda j, t, qb, kb: (qb[t], j))],
        out_specs=[pl.BlockSpec((S, 256), lambda j, t, qb, kb: (0, j)),
                   pl.BlockSpec((tq, 256), lambda j, t, qb, kb: (kb[t], j)),
                   pl.BlockSpec((tq, 128), lambda j, t, qb, kb: (kb[t], j))],
        scratch_shapes=[pltpu.VMEM((tq, 256), F32), pltpu.VMEM((tq, 128), F32)],
    )
    return pl.pallas_call(
        body, name="flash_bwd", grid_spec=grid_spec,
        out_shape=[_sds((S, 1024), F32), _sds((S, 1024), F32), _sds((S, 512), F32)],
        compiler_params=_cp(("parallel", "arbitrary")),
    )(qb_of, kb_of, qp, kp, v, do, lse, delta)


def _mla_up_bwd(dqp, dkp, dv, cq, ckv, gq, gkv, w_uq, w_ukv, tabs, S):
    tm = min(256, S)

    def body(dq_ref, dk_ref, dv_ref, cq_ref, ckv_ref, gq_ref, gkv_ref, wuq_ref, wukv_ref, cm_ref, sa_ref, sb_ref,
             dqh_ref, dkv_ref, dcq_ref, dckv_ref, dkr_ref, dgq_ref, dgkv_ref):
        @pl.when(pl.program_id(0) == 0)
        def _():
            dgq_ref[...] = jnp.zeros(dgq_ref.shape, F32)
            dgkv_ref[...] = jnp.zeros(dgkv_ref.shape, F32)

        cm = cm_ref[...]
        sa = sa_ref[...]
        sb = sb_ref[...]
        lane = lax.broadcasted_iota(jnp.int32, (tm, 128), 1)
        dkr_r = jnp.zeros((tm, 128), F32)
        for h in range(MLA_HEADS):
            sl = slice(h * 128, (h + 1) * 128)
            dqh_ref[:, sl] = (_unrope_mla(dq_ref[:, sl], cm, sa, sb) * SCALE_MLA).astype(BF16)
            gk = dk_ref[:, sl]
            dkr_r = dkr_r + gk
            dkv_ref[:, sl] = gk.astype(BF16)
        dkr_r = jnp.where((lane >= 64) & (lane < 96), dkr_r, 0.0)
        dkr_ref[...] = _unrope_mla(dkr_r, cm, sa, sb).astype(BF16)
        dkv_ref[:, 1024:1536] = dv_ref[...].astype(BF16)
        dcq, ga = _rms_bwd(_dot_nt(dqh_ref[...], wuq_ref[...]), cq_ref[...], gq_ref[...])
        dcq_ref[...] = dcq.astype(BF16)
        dgq_ref[...] += _colsum(ga)
        dckv, gb = _rms_bwd(_dot_nt(dkv_ref[...], wukv_ref[...]), ckv_ref[...], gkv_ref[...])
        dckv_ref[...] = dckv.astype(BF16)
        dgkv_ref[...] += _colsum(gb)

    return pl.pallas_call(
        body, name="mla_up_bwd", grid=(S // tm,),
        in_specs=[_rows(tm, 1024), _rows(tm, 1024), _rows(tm, 512), _rows(tm, Q_LORA), _rows(tm, KV_LORA),
                  _full(1, Q_LORA), _full(1, KV_LORA), _full(Q_LORA, 1024), _full(KV_LORA, 1536)] + [_rows(tm, 128)] * 3,
        out_specs=[_rows(tm, 1024), _rows(tm, 1536), _rows(tm, Q_LORA), _rows(tm, KV_LORA), _rows(tm, 128),
                   _full(1, Q_LORA), _full(1, KV_LORA)],
        out_shape=[_sds((S, 1024), BF16), _sds((S, 1536), BF16), _sds((S, Q_LORA), BF16), _sds((S, KV_LORA), BF16),
                   _sds((S, 128), BF16), _sds((1, Q_LORA), F32), _sds((1, KV_LORA), F32)],
        compiler_params=_cp(("arbitrary",)),
    )(dqp, dkp, dv, cq, ckv, gq, gkv, w_uq, w_ukv, *tabs[2:])


def _ret_bwd(rq, rk, rv, rprev, ry, rg, dro, gn_w, tabs, S):
    C = RET_CHUNK
    N = S // C
    G = min(RET_GROUP, N)
    NB = N // G

    def body(lg_ref, q_ref, k_ref, v_ref, rp_ref, ry_ref, rg_ref, dro_ref, w_ref, cr_ref, sr_ref,
             drq_ref, drk_ref, drv_ref, drg_ref, dw_ref, g_sc):
        @pl.when(pl.program_id(1) == 0)
        def _():
            g_sc[...] = jnp.zeros(g_sc.shape, F32)
            dw_ref[...] = jnp.zeros(dw_ref.shape, F32)

        dmat, zeta, xi, g_chunk = _decay_terms(lg_ref)
        w = w_ref[...]
        gacc = g_sc[...]
        dw = jnp.zeros((1, 128), F32)
        for i in reversed(range(G)):
            rows = slice(i * C, (i + 1) * C)
            ry = ry_ref[rows, :]
            mu = jnp.mean(ry, axis=-1, keepdims=True)
            yc = ry - mu
            rstd = lax.rsqrt(jnp.mean(yc * yc, axis=-1, keepdims=True) + EPS)
            yh = yc * rstd
            g = rg_ref[rows, :]
            s = _sig(g)
            dout = dro_ref[rows, :].astype(F32)
            drg_ref[rows, :] = (dout * (yh * w) * (s * (1.0 + g * (1.0 - s)))).astype(BF16)
            dgn = dout * (g * s)
            dw = dw + _colsum(dgn * yh)
            dyh = dgn * w
            dry = rstd * (dyh - jnp.mean(dyh, axis=-1, keepdims=True) - yh * jnp.mean(dyh * yh, axis=-1, keepdims=True))
            do = dry.astype(BF16)

            q = q_ref[rows, :]
            k = k_ref[rows, :]
            v = v_ref[rows, :]
            gfut = gacc.astype(BF16)
            sc = (_dot_nt(q, k) * dmat).astype(BF16)
            dsc = (_dot_nt(do, v) * dmat).astype(BF16)
            dq = jnp.dot(dsc, k, preferred_element_type=F32) + _dot_nt(do, rp_ref[i]) * xi
            dk = _dot_tn(dsc, q) + _dot_nt(v, gfut) * zeta
            dv = _dot_tn(sc, do) + jnp.dot(k, gfut, preferred_element_type=F32) * zeta
            gacc = g_chunk * gacc + _dot_tn(q, xi * dry)
            cr = cr_ref[rows, :]
            sr = sr_ref[rows, :]
            drq_ref[rows, :] = _unrope_ret(dq, cr, sr).astype(BF16)
            drk_ref[rows, :] = _unrope_ret(dk * SCALE_RET, cr, sr).astype(BF16)
            drv_ref[rows, :] = dv.astype(BF16)
        g_sc[...] = gacc
        dw_ref[...] += dw

    blk = pl.BlockSpec((G * C, 128), lambda h, n: (NB - 1 - n, h))
    tab = pl.BlockSpec((G * C, 128), lambda h, n: (NB - 1 - n, 0))
    return pl.pallas_call(
        body, name="ret_bwd", grid=(RET_HEADS, NB),
        in_specs=[pl.BlockSpec((None, 8, 128), lambda h, n: (h, 0, 0)), blk, blk, blk,
                  pl.BlockSpec((G, 128, 128), lambda h, n: (h * NB + NB - 1 - n, 0, 0)), blk, blk, blk,
                  pl.BlockSpec((1, 128), lambda h, n: (0, h)), tab, tab],
        out_specs=[blk, blk, blk, blk, pl.BlockSpec((1, 128), lambda h, n: (0, h))],
        out_shape=[_sds((S, 512), BF16)] * 4 + [_sds((1, 512), F32)],
        scratch_shapes=[pltpu.VMEM((128, 128), F32)],
        compiler_params=_cp(("parallel", "arbitrary")),
    )(_decay_table(), rq, rk, rv, rprev, ry, rg, dro, gn_w, tabs[0], tabs[1])


def _inproj_bwd(drq, drk, drv, drg, dcq, dckv, dkr, w_in, dh1, x, g, S):
    tm = min(256, S)

    def body(drq_ref, drk_ref, drv_ref, drg_ref, dcq_ref, dckv_ref, dkr_ref, w_ref, dh1_ref, x_ref, g_ref,
             gx_ref, dproj_ref, dg_ref):
        @pl.when(pl.program_id(0) == 0)
        def _():
            dg_ref[...] = jnp.zeros(dg_ref.shape, F32)

        dproj_ref[:, 0:512] = drq_ref[...]
        dproj_ref[:, 512:1024] = drk_ref[...]
        dproj_ref[:, 1024:1536] = drv_ref[...]
        dproj_ref[:, 1536:2048] = drg_ref[...]
        dproj_ref[:, 2048:2432] = dcq_ref[...]
        dproj_ref[:, 2432:2688] = dckv_ref[...]
        dproj_ref[:, 2688:2816] = dkr_ref[...]
        dx, ga = _rms_bwd(_dot_nt(dproj_ref[...], w_ref[...]), x_ref[...], g_ref[...])
        gx_ref[...] = dh1_ref[...] + dx
        dg_ref[...] += _colsum(ga)

    return pl.pallas_call(
        body, name="inproj_bwd", grid=(S // tm,),
        in_specs=[_rows(tm, 512)] * 4 + [_rows(tm, Q_LORA), _rows(tm, KV_LORA), _rows(tm, 128),
                                         _full(D_MODEL, IN_COLS_P), _rows(tm, D_MODEL), _rows(tm, D_MODEL),
                                         _full(1, D_MODEL)],
        out_specs=[_rows(tm, D_MODEL), _rows(tm, IN_COLS_P), _full(1, D_MODEL)],
        out_shape=[_sds((S, D_MODEL), F32), _sds((S, IN_COLS_P), BF16), _sds((1, D_MODEL), F32)],
        compiler_params=_cp(("arbitrary",)),
    )(drq, drk, drv, drg, dcq, dckv, dkr, w_in, dh1, x, g)


def _pad_weights(w):
    w_in = w["w_in"]
    z = lambda r, c: jnp.zeros((r, c), BF16)
    w_in_p = jnp.concatenate([w_in[:, :2688], z(1024, 64), w_in[:, 2688:2720], z(1024, 32)], axis=1)
    w_uq_p = jnp.pad(w["w_uq"].reshape(Q_LORA, MLA_HEADS, 96), ((0, 0), (0, 0), (0, 32))).reshape(Q_LORA, 1024)
    ukv = w["w_ukv"].reshape(KV_LORA, MLA_HEADS, 128)
    k_part = jnp.pad(ukv[:, :, :64], ((0, 0), (0, 0), (0, 64))).reshape(KV_LORA, 1024)
    w_ukv_p = jnp.concatenate([k_part, ukv[:, :, 64:].reshape(KV_LORA, 512)], axis=1)
    return w_in_p, w_uq_p, w_ukv_p


def _local_step(x, p, pos_f, tgt, w, sm):
    S = x.shape[0]
    w_in_p, w_uq_p, w_ukv_p = _pad_weights(w)
    tabs = _rope_tables(pos_f, S)

    xn = _rms_fwd(x, sm["pre_mix_norm"], S)
    rq, rk, rv, rg, cq, ckv, kr = _inproj(xn, w_in_p, tabs, S)
    cqn, ckvn, qp, kp, v = _mla_up(cq, ckv, kr, sm["mla_q_norm"], sm["mla_kv_norm"], w_uq_p, w_ukv_p, tabs, S)
    mo, lse = _flash_fwd(qp, kp, v, S)
    ry, ro, rprev = _ret_fwd(rq, rk, rv, rg, sm["ret_gn_w"], S)
    mix, h1, hn = _outproj(ro, mo, x, w["w_o"], sm["post_mix_norm"], sm["pre_ffn_norm"], S)
    gate, up, act = _ffn_up(hn, w["w_gate"], w["w_up"], S)
    ff, h2 = _ffn_down(act, w["w_down"], h1, sm["post_ffn_norm"], S)
    dz, dpe, dh2, h2b, loss_vec, d_ple_norm, d_b = _ple_loss(
        p, h2, tgt, w["w_ple_proj"], w["w_ple_gate"], sm["b_ple_gate"], sm["ple_norm"], S)

    gw = {}
    gs = {"ple_norm": d_ple_norm, "b_ple_gate": d_b}
    gw["w_ple_gate"] = _wgrad(h2b, dz, "wgrad_ple_gate", S)
    gw["w_ple_proj"] = _wgrad(p, dpe, "wgrad_ple_proj", S)
    dff, dgate, dup, gs["post_ffn_norm"] = _ffn_down_bwd(dh2, ff, sm["post_ffn_norm"], w["w_down"], gate, up, S)
    gw["w_down"] = _wgrad(act, dff, "wgrad_down", S)
    gw["w_gate"] = _wgrad(hn, dgate, "wgrad_gate", S)
    gw["w_up"] = _wgrad(hn, dup, "wgrad_up", S)
    dh1, dmix, dro, dmo, gs["pre_ffn_norm"], gs["post_mix_norm"] = _ffn_up_bwd(
        dgate, dup, w["w_gate"], w["w_up"], h1, mix, dh2, sm["pre_ffn_norm"], sm["post_mix_norm"], w["w_o"], S)
    gw["w_o"] = jnp.concatenate([_wgrad(ro, dmix, "wgrad_o_ret", S), _wgrad(mo, dmix, "wgrad_o_mla", S)], axis=0)

    delta = _attn_delta(mo, dmo, S)
    dqp, dkp, dv = _flash_bwd(qp, kp, v, dmo, lse, delta, S)
    dqh, dkv, dcq, dckv, dkr, gs["mla_q_norm"], gs["mla_kv_norm"] = _mla_up_bwd(
        dqp, dkp, dv, cq, ckv, sm["mla_q_norm"], sm["mla_kv_norm"], w_uq_p, w_ukv_p, tabs, S)
    g_uq_p = _wgrad(cqn, dqh, "wgrad_uq", S)
    g_ukv_p = _wgrad(ckvn, dkv, "wgrad_ukv", S)
    gw["w_uq"] = g_uq_p.reshape(Q_LORA, MLA_HEADS, 128)[:, :, :96].reshape(Q_LORA, 768)
    gw["w_ukv"] = jnp.concatenate(
        [g_ukv_p[:, :1024].reshape(KV_LORA, MLA_HEADS, 128)[:, :, :64], g_ukv_p[:, 1024:].reshape(KV_LORA, MLA_HEADS, 64)],
        axis=2).reshape(KV_LORA, 1024)

    drq, drk, drv, drg, gs["ret_gn_w"] = _ret_bwd(rq, rk, rv, rprev, ry, rg, dro, sm["ret_gn_w"], tabs, S)
    grad_x, dproj, gs["pre_mix_norm"] = _inproj_bwd(drq, drk, drv, drg, dcq, dckv, dkr, w_in_p, dh1, x,
                                                    sm["pre_mix_norm"], S)
    g_in_p = _wgrad(xn, dproj, "wgrad_in", S)
    gw["w_in"] = jnp.concatenate([g_in_p[:, :2688], g_in_p[:, 2752:2784]], axis=1)
    return loss_vec, grad_x, gw, gs


def _my_place():
    x = lax.axis_index("x")
    y = lax.axis_index("y")
    c = lax.axis_index("c")
    return x, y, c


def _other_chips(x, y):
    return [(1 - x, y), (x, 1 - y), (1 - x, 1 - y)]


_ANY = pl.BlockSpec(memory_space=pl.ANY)


def _allgather_weights(wpk):
    H = HALF_ROWS

    def body(w_ref, out_ref, send1, recv1, send2, recv2, lsem):
        x, y, c = _my_place()
        me = 2 * x + y
        chips = _other_chips(x, y)
        half = pl.ds(pl.multiple_of(c * H, 32), H)
        other = pl.ds(pl.multiple_of((1 - c) * H, 32), H)
        mine = pltpu.make_async_copy(w_ref, out_ref.at[me], lsem)
        mine.start()

        def over_ici(k, src_chip, to):
            return pltpu.make_async_remote_copy(
                src_ref=w_ref.at[half], dst_ref=out_ref.at[src_chip, half], send_sem=send1.at[k], recv_sem=recv1.at[k],
                device_id=to, device_id_type=MESH)

        def to_sibling(k, chip, rows):
            return pltpu.make_async_remote_copy(
                src_ref=out_ref.at[chip, rows], dst_ref=out_ref.at[chip, rows], send_sem=send2.at[k],
                recv_sem=recv2.at[k], device_id=(x, y, 1 - c), device_id_type=MESH)

        first = [over_ici(k, me, (cx, cy, c)) for k, (cx, cy) in enumerate(chips)]
        for cp in first:
            cp.start()
        passed = []
        for k, (cx, cy) in enumerate(chips):
            over_ici(k, 2 * cx + cy, (cx, cy, c)).wait_recv()
            fwd = to_sibling(k, 2 * cx + cy, half)
            fwd.start()
            passed.append(fwd)
        for k, (cx, cy) in enumerate(chips):
            to_sibling(k, 2 * cx + cy, other).wait_recv()
        for cp in first + passed:
            cp.wait_send()
        mine.wait()

    return pl.pallas_call(
        body, name="allgather_weights",
        in_specs=[_ANY], out_specs=_ANY, out_shape=_sds((N_CHIPS, PACK_ROWS, PACK_COLS), BF16),
        scratch_shapes=[pltpu.SemaphoreType.DMA((3,)), pltpu.SemaphoreType.DMA((3,)), pltpu.SemaphoreType.DMA((3,)),
                        pltpu.SemaphoreType.DMA((3,)), pltpu.SemaphoreType.DMA],
    )(wpk)


def _swap_halves(gpk):
    H = HALF_ROWS

    def body(g_ref, out_ref, send, recv):
        x, y, c = _my_place()
        other = pl.ds(pl.multiple_of((1 - c) * H, 8), H)
        cp = pltpu.make_async_remote_copy(
            src_ref=g_ref.at[:, other], dst_ref=out_ref, send_sem=send, recv_sem=recv,
            device_id=(x, y, 1 - c), device_id_type=MESH)
        cp.start()
        cp.wait()

    return pl.pallas_call(
        body, name="rs_swap_halves",
        in_specs=[_ANY], out_specs=_ANY, out_shape=_sds((N_CHIPS, HALF_ROWS, PACK_COLS), F32),
        scratch_shapes=[pltpu.SemaphoreType.DMA, pltpu.SemaphoreType.DMA],
    )(gpk)


def _add_halves(gpk, got, c_idx):
    tr = 440
    nb = HALF_ROWS // tr

    def body(c_ref, a_ref, b_ref, o_ref):
        o_ref[...] = a_ref[...] + b_ref[...]

    grid_spec = pltpu.PrefetchScalarGridSpec(
        num_scalar_prefetch=1, grid=(N_CHIPS, nb),
        in_specs=[pl.BlockSpec((None, tr, PACK_COLS), lambda j, i, c: (j, c[0] * nb + i, 0)),
                  pl.BlockSpec((None, tr, PACK_COLS), lambda j, i, c: (j, i, 0))],
        out_specs=pl.BlockSpec((None, tr, PACK_COLS), lambda j, i, c: (j, i, 0)),
    )
    return pl.pallas_call(
        body, name="rs_add_halves", grid_spec=grid_spec, out_shape=_sds((N_CHIPS, HALF_ROWS, PACK_COLS), F32),
        compiler_params=_cp(("parallel", "parallel")),
    )(c_idx, gpk, got)


def _scatter_chips(tsum):
    def body(t_ref, out_ref, send, recv, lsem):
        x, y, c = _my_place()
        me = 2 * x + y
        chips = _other_chips(x, y)
        mine = pltpu.make_async_copy(t_ref.at[me], out_ref.at[me], lsem)
        mine.start()
        cps = [pltpu.make_async_remote_copy(
            src_ref=t_ref.at[2 * cx + cy], dst_ref=out_ref.at[me], send_sem=send.at[k], recv_sem=recv.at[k],
            device_id=(cx, cy, c), device_id_type=MESH) for k, (cx, cy) in enumerate(chips)]
        for cp in cps:
            cp.start()
        for cp in cps:
            cp.wait()
        mine.wait()

    return pl.pallas_call(
        body, name="rs_scatter_chips",
        in_specs=[_ANY], out_specs=_ANY, out_shape=_sds((N_CHIPS, HALF_ROWS, PACK_COLS), F32),
        scratch_shapes=[pltpu.SemaphoreType.DMA((3,)), pltpu.SemaphoreType.DMA((3,)), pltpu.SemaphoreType.DMA],
    )(tsum)


def _add_chips(parts):
    tr = 440

    def body(p_ref, o_ref):
        o_ref[...] = ((p_ref[0] + p_ref[1]) + p_ref[2]) + p_ref[3]

    return pl.pallas_call(
        body, name="rs_add_chips", grid=(HALF_ROWS // tr,),
        in_specs=[pl.BlockSpec((N_CHIPS, tr, PACK_COLS), lambda i: (0, i, 0))],
        out_specs=pl.BlockSpec((tr, PACK_COLS), lambda i: (i, 0)), out_shape=_sds((HALF_ROWS, PACK_COLS), F32),
        compiler_params=_cp(("parallel",)),
    )(parts)


def _join_halves(red):
    H = HALF_ROWS

    def body(r_ref, out_ref, send, recv, lsem):
        x, y, c = _my_place()
        half = pl.ds(pl.multiple_of(c * H, 8), H)
        mine = pltpu.make_async_copy(r_ref, out_ref.at[half], lsem)
        mine.start()
        cp = pltpu.make_async_remote_copy(
            src_ref=r_ref, dst_ref=out_ref.at[half], send_sem=send, recv_sem=recv,
            device_id=(x, y, 1 - c), device_id_type=MESH)
        cp.start()
        cp.wait()
        mine.wait()

    return pl.pallas_call(
        body, name="rs_join_halves",
        in_specs=[_ANY], out_specs=_ANY, out_shape=_sds((PACK_ROWS, PACK_COLS), F32),
        scratch_shapes=[pltpu.SemaphoreType.DMA, pltpu.SemaphoreType.DMA, pltpu.SemaphoreType.DMA],
    )(red)


def _allreduce_small(vec):
    def body(v_ref, out_ref, slots, send, recv, lsem):
        x, y, c = _my_place()
        me = 4 * x + 2 * y + c
        mine = pltpu.make_async_copy(v_ref, slots.at[me], lsem)
        mine.start()
        cps = []
        for r in range(1, N_DEV):
            px = x ^ (r >> 2)
            py = y ^ ((r >> 1) & 1)
            pc = c ^ (r & 1)
            cps.append(pltpu.make_async_remote_copy(
                src_ref=v_ref, dst_ref=slots.at[me], send_sem=send.at[r - 1], recv_sem=recv.at[r - 1],
                device_id=(px, py, pc), device_id_type=MESH))
        for cp in cps:
            cp.start()
        for cp in cps:
            cp.wait()
        mine.wait()
        acc = slots[0]
        for d in range(1, N_DEV):
            acc = acc + slots[d]
        out_ref[...] = acc
        loss = jnp.sum(acc[9:10, :], axis=1, keepdims=True) * (0.5 / D_MODEL)
        out_ref[9:10, :] = jnp.broadcast_to(loss, (1, PACK_COLS))

    vm = pl.BlockSpec(memory_space=pltpu.VMEM)
    return pl.pallas_call(
        body, name="allreduce_small",
        in_specs=[vm], out_specs=vm, out_shape=_sds((SMALL_ROWS, PACK_COLS), F32),
        scratch_shapes=[pltpu.VMEM((N_DEV, SMALL_ROWS, PACK_COLS), F32), pltpu.SemaphoreType.DMA((N_DEV - 1,)),
                        pltpu.SemaphoreType.DMA((N_DEV - 1,)), pltpu.SemaphoreType.DMA],
    )(vec)


N_BIG = len(BIG)


def _half(c, rows, align):
    h = rows // 2
    return pl.ds(pl.multiple_of(c * h, align), h)


def _gather_shards(shards):
    n = len(shards)

    def body(*refs):
        ins, outs = refs[:n], refs[n:2 * n]
        send1, recv1, send2, recv2, send3, recv3 = refs[2 * n:]
        x, y, c = _my_place()
        me = 2 * x + y
        chips = _other_chips(x, y)
        sib = (x, y, 1 - c)
        local, first, passed = [], [], []
        for t in range(n):
            rows = ins[t].shape[0]
            half = _half(c, rows, 16)
            cp = pltpu.make_async_remote_copy(
                src_ref=ins[t], dst_ref=outs[t].at[me], send_sem=send3.at[t], recv_sem=recv3.at[t],
                device_id=sib, device_id_type=MESH)
            cp.start()
            local.append(cp)
            for k, (cx, cy) in enumerate(chips):
                rc = pltpu.make_async_remote_copy(
                    src_ref=ins[t].at[half], dst_ref=outs[t].at[me, half], send_sem=send1.at[t, k],
                    recv_sem=recv1.at[t, k], device_id=(cx, cy, c), device_id_type=MESH)
                rc.start()
                first.append(rc)
        for k, (cx, cy) in enumerate(chips):
            src = 2 * cx + cy
            for t in range(n):
                half = _half(c, ins[t].shape[0], 16)
                pltpu.make_async_remote_copy(
                    src_ref=ins[t].at[half], dst_ref=outs[t].at[src, half], send_sem=send1.at[t, k],
                    recv_sem=recv1.at[t, k], device_id=(cx, cy, c), device_id_type=MESH).wait_recv()
                fw = pltpu.make_async_remote_copy(
                    src_ref=outs[t].at[src, half], dst_ref=outs[t].at[src, half], send_sem=send2.at[t, k],
                    recv_sem=recv2.at[t, k], device_id=sib, device_id_type=MESH)
                fw.start()
                passed.append(fw)
        for k, (cx, cy) in enumerate(chips):
            src = 2 * cx + cy
            for t in range(n):
                other = _half(1 - c, ins[t].shape[0], 16)
                pltpu.make_async_remote_copy(
                    src_ref=outs[t].at[src, other], dst_ref=outs[t].at[src, other], send_sem=send2.at[t, k],
                    recv_sem=recv2.at[t, k], device_id=sib, device_id_type=MESH).wait_recv()
        for cp in first + passed:
            cp.wait_send()
        for cp in local:
            cp.wait()

    return pl.pallas_call(
        body, name="gather_weights",
        in_specs=[_ANY] * n, out_specs=[_ANY] * n,
        out_shape=[_sds((N_CHIPS,) + s.shape, BF16) for s in shards],
        scratch_shapes=[pltpu.SemaphoreType.DMA((n, 3))] * 4 + [pltpu.SemaphoreType.DMA((n,))] * 2,
    )(*shards)


def _swap_half_rows(gs):
    n = len(gs)

    def body(*refs):
        ins, outs = refs[:n], refs[n:2 * n]
        send, recv = refs[2 * n:]
        x, y, c = _my_place()
        cps = []
        for t in range(n):
            other = _half(1 - c, ins[t].shape[1], 8)
            cp = pltpu.make_async_remote_copy(
                src_ref=ins[t].at[:, other], dst_ref=outs[t], send_sem=send.at[t], recv_sem=recv.at[t],
                device_id=(x, y, 1 - c), device_id_type=MESH)
            cp.start()
            cps.append(cp)
        for cp in cps:
            cp.wait()

    return pl.pallas_call(
        body, name="rs_swap_halves",
        in_specs=[_ANY] * n, out_specs=[_ANY] * n,
        out_shape=[_sds((N_CHIPS, g.shape[1] // 2, g.shape[2]), F32) for g in gs],
        scratch_shapes=[pltpu.SemaphoreType.DMA((n,)), pltpu.SemaphoreType.DMA((n,))],
    )(*gs)


def _add_half_rows(g, got, c_idx, name):
    _, rows, cols = g.shape
    h = rows // 2

    def body(c_ref, a_ref, b_ref, o_ref):
        o_ref[...] = (a_ref[...] + b_ref[...]).astype(BF16)

    grid_spec = pltpu.PrefetchScalarGridSpec(
        num_scalar_prefetch=1, grid=(N_CHIPS,),
        in_specs=[pl.BlockSpec((None, h, cols), lambda j, c: (j, c[0], 0)),
                  pl.BlockSpec((None, h, cols), lambda j, c: (j, 0, 0))],
        out_specs=pl.BlockSpec((None, h, cols), lambda j, c: (j, 0, 0)),
    )
    return pl.pallas_call(
        body, name=name, grid_spec=grid_spec, out_shape=_sds((N_CHIPS, h, cols), BF16),
        compiler_params=_cp(("parallel",)),
    )(c_idx, g, got)


def _scatter_to_chips(ts):
    n = len(ts)

    def body(*refs):
        ins, outs = refs[:n], refs[n:2 * n]
        send, recv = refs[2 * n:]
        x, y, c = _my_place()
        chips = _other_chips(x, y)
        cps = []
        for t in range(n):
            for k, (cx, cy) in enumerate(chips):
                rc = pltpu.make_async_remote_copy(
                    src_ref=ins[t].at[2 * cx + cy], dst_ref=outs[t].at[k], send_sem=send.at[t, k],
                    recv_sem=recv.at[t, k], device_id=(cx, cy, c), device_id_type=MESH)
                rc.start()
                cps.append(rc)
        for cp in cps:
            cp.wait()

    return pl.pallas_call(
        body, name="rs_scatter_chips",
        in_specs=[_ANY] * n, out_specs=[_ANY] * n, out_shape=[_sds((3,) + t.shape[1:], BF16) for t in ts],
        scratch_shapes=[pltpu.SemaphoreType.DMA((n, 3)), pltpu.SemaphoreType.DMA((n, 3))],
    )(*ts)


def _add_four(mine, parts, place, name):
    _, h, cols = parts.shape

    def body(pl_ref, m_ref, p_ref, o_ref):
        o_ref[...] = ((m_ref[...].astype(F32) + p_ref[0].astype(F32)) + p_ref[1].astype(F32)) + p_ref[2].astype(F32)

    grid_spec = pltpu.PrefetchScalarGridSpec(
        num_scalar_prefetch=1, grid=(1,),
        in_specs=[pl.BlockSpec((None, h, cols), lambda i, pc: (pc[0], 0, 0)),
                  pl.BlockSpec((3, h, cols), lambda i, pc: (0, 0, 0))],
        out_specs=pl.BlockSpec((h, cols), lambda i, pc: (pc[1], 0)),
    )
    return pl.pallas_call(
        body, name=name, grid_spec=grid_spec, out_shape=_sds((2 * h, cols), F32),
        compiler_params=_cp(("arbitrary",)),
    )(place, mine, parts)


def _join_half_rows(rs):
    n = len(rs)

    def body(*refs):
        ins, outs = refs[:n], refs[n:2 * n]
        send, recv = refs[2 * n:]
        x, y, c = _my_place()
        cps = []
        for t in range(n):
            half = _half(c, outs[t].shape[0], 8)
            rc = pltpu.make_async_remote_copy(
                src_ref=ins[t].at[half], dst_ref=outs[t].at[half], send_sem=send.at[t], recv_sem=recv.at[t],
                device_id=(x, y, 1 - c), device_id_type=MESH)
            rc.start()
            cps.append(rc)
        for cp in cps:
            cp.wait()

    return pl.pallas_call(
        body, name="rs_join_halves",
        in_specs=[_ANY] * n, out_specs=[_ANY] * n,
        out_shape=[_sds(r.shape, F32) for r in rs],
        input_output_aliases={i: i for i in range(n)},
        scratch_shapes=[pltpu.SemaphoreType.DMA((n,))] * 2,
    )(*rs)


def _by_chip(full, rows, cols, axis):
    if axis == 0:
        return full.reshape(N_CHIPS, rows // N_CHIPS, cols)
    return full.reshape(rows, N_CHIPS, cols // N_CHIPS).transpose(1, 0, 2)


def _from_chips(parts, axis):
    _, r, c = parts.shape
    if axis == 0:
        return parts.reshape(N_CHIPS * r, c)
    return parts.transpose(1, 0, 2).reshape(r, N_CHIPS * c)


def _adamw(wt, g, m, v, name):
    R, C = wt.shape
    tr = R
    for cand in (256, 128, 64, 32, 16, 8):
        if R % cand == 0:
            tr = cand
            break

    def body(w_ref, g_ref, m_ref, v_ref, d_ref, nm_ref, nv_ref):
        gg = g_ref[...]
        m_new = ADAM_B1 * m_ref[...] + (1.0 - ADAM_B1) * gg
        v_new = ADAM_B2 * v_ref[...] + (1.0 - ADAM_B2) * (gg * gg)
        m_hat = m_new / (1.0 - ADAM_B1 ** ADAM_STEP)
        v_hat = v_new / (1.0 - ADAM_B2 ** ADAM_STEP)
        d_ref[...] = -ADAM_LR * (m_hat / (jnp.sqrt(v_hat) + ADAM_EPS) + ADAM_WD * w_ref[...])
        nm_ref[...] = m_new
        nv_ref[...] = v_new

    spec = pl.BlockSpec((tr, C), lambda i: (i, 0))
    return pl.pallas_call(
        body, name=name, grid=(R // tr,), in_specs=[spec] * 4, out_specs=[spec] * 3, out_shape=[_sds((R, C), F32)] * 3,
        compiler_params=_cp(("parallel",)),
    )(wt, g, m, v)


def _shard_shape(r, c, axis):
    return (r // N_CHIPS, c) if axis == 0 else (r, c // N_CHIPS)


def _pack_rows(flat):
    return jnp.pad(flat, (0, PACK_ROWS * PACK_COLS - flat.shape[0])).reshape(PACK_ROWS, PACK_COLS)


def _pack_shards(mats):
    return _pack_rows(jnp.concatenate([mats[n].reshape(-1) for n, _, _, _ in BIG]))


def _unpack_shards(pk):
    flat = pk.reshape(-1)
    out = {}
    off = 0
    for n, r, c, ax in BIG:
        shp = _shard_shape(r, c, ax)
        sz = shp[0] * shp[1]
        out[n] = flat[off:off + sz].reshape(shp)
        off += sz
    return out


def _full_from_packs(allpk):
    per_chip = [_unpack_shards(allpk[j]) for j in range(N_CHIPS)]
    return {n: jnp.concatenate([per_chip[j][n] for j in range(N_CHIPS)], axis=ax) for n, _, _, ax in BIG}


def _packs_from_full(gw):
    packs = []
    for j in range(N_CHIPS):
        shards = {}
        for n, r, c, ax in BIG:
            shp = _shard_shape(r, c, ax)
            shards[n] = gw[n][j * shp[0]:(j + 1) * shp[0], :] if ax == 0 else gw[n][:, j * shp[1]:(j + 1) * shp[1]]
        packs.append(_pack_shards(shards))
    return jnp.stack(packs)


def _pack_small(vals, loss_vec=None):
    rows = [jnp.pad(vals[n].reshape(-1), (0, PACK_COLS - sz)) for n, sz in SMALL]
    rows.append(loss_vec.reshape(-1) if loss_vec is not None else jnp.zeros((PACK_COLS,), F32))
    rows += [jnp.zeros((PACK_COLS,), F32)] * (SMALL_ROWS - len(rows))
    return jnp.stack(rows)


def kernel(x, p, positions, pre_mix_norm, w_in, ret_gn_w, mla_q_norm, w_uq, mla_kv_norm, w_ukv, w_o, post_mix_norm, pre_ffn_norm, w_gate, w_up, w_down, post_ffn_norm, w_ple_proj, ple_norm, w_ple_gate, b_ple_gate, loss_target, m_pre_mix_norm, m_w_in, m_ret_gn_w, m_mla_q_norm, m_w_uq, m_mla_kv_norm, m_w_ukv, m_w_o, m_post_mix_norm, m_pre_ffn_norm, m_w_gate, m_w_up, m_w_down, m_post_ffn_norm, m_w_ple_proj, m_ple_norm, m_w_ple_gate, m_b_ple_gate, v_pre_mix_norm, v_w_in, v_ret_gn_w, v_mla_q_norm, v_w_uq, v_mla_kv_norm, v_w_ukv, v_w_o, v_post_mix_norm, v_pre_ffn_norm, v_w_gate, v_w_up, v_w_down, v_post_ffn_norm, v_w_ple_proj, v_ple_norm, v_w_ple_gate, v_b_ple_gate):
    wts = dict(pre_mix_norm=pre_mix_norm, w_in=w_in, ret_gn_w=ret_gn_w, mla_q_norm=mla_q_norm, w_uq=w_uq,
               mla_kv_norm=mla_kv_norm, w_ukv=w_ukv, w_o=w_o, post_mix_norm=post_mix_norm, pre_ffn_norm=pre_ffn_norm,
               w_gate=w_gate, w_up=w_up, w_down=w_down, post_ffn_norm=post_ffn_norm, w_ple_proj=w_ple_proj,
               ple_norm=ple_norm, w_ple_gate=w_ple_gate, b_ple_gate=b_ple_gate)
    mom = dict(pre_mix_norm=m_pre_mix_norm, w_in=m_w_in, ret_gn_w=m_ret_gn_w, mla_q_norm=m_mla_q_norm, w_uq=m_w_uq,
               mla_kv_norm=m_mla_kv_norm, w_ukv=m_w_ukv, w_o=m_w_o, post_mix_norm=m_post_mix_norm,
               pre_ffn_norm=m_pre_ffn_norm, w_gate=m_w_gate, w_up=m_w_up, w_down=m_w_down, post_ffn_norm=m_post_ffn_norm,
               w_ple_proj=m_w_ple_proj, ple_norm=m_ple_norm, w_ple_gate=m_w_ple_gate, b_ple_gate=m_b_ple_gate)
    var = dict(pre_mix_norm=v_pre_mix_norm, w_in=v_w_in, ret_gn_w=v_ret_gn_w, mla_q_norm=v_mla_q_norm, w_uq=v_w_uq,
               mla_kv_norm=v_mla_kv_norm, w_ukv=v_w_ukv, w_o=v_w_o, post_mix_norm=v_post_mix_norm,
               pre_ffn_norm=v_pre_ffn_norm, w_gate=v_w_gate, w_up=v_w_up, w_down=v_w_down, post_ffn_norm=v_post_ffn_norm,
               w_ple_proj=v_w_ple_proj, ple_norm=v_ple_norm, w_ple_gate=v_w_ple_gate, b_ple_gate=v_b_ple_gate)

    S = x.shape[1]
    shard2d = {n: wts[n][0] for n, _, _, _ in BIG}
    small2d = {n: wts[n] for n, _ in SMALL}

    gathered = _gather_shards([shard2d[n].astype(BF16) for n, _, _, _ in BIG])
    w_full = {n: _from_chips(gathered[i], ax) for i, (n, _, _, ax) in enumerate(BIG)}

    pos_f = positions.astype(F32).reshape(S, 1)
    loss_vec, grad_x, gw, gs = _local_step(x[0], p[0, 0], pos_f, loss_target[0], w_full, small2d)

    g4 = [_by_chip(gw[n], r, c, ax) for n, r, c, ax in BIG]
    c_idx = lax.axis_index("c").astype(jnp.int32).reshape(1)
    got = _swap_half_rows(g4)
    chip_sum = [_add_half_rows(g4[i], got[i], c_idx, "rs_add_halves_" + BIG[i][0]) for i in range(N_BIG)]
    parts = _scatter_to_chips(chip_sum)
    place = jnp.stack([2 * lax.axis_index("x") + lax.axis_index("y"), lax.axis_index("c")]).astype(jnp.int32)
    reduced = _join_half_rows(
        [_add_four(chip_sum[i], parts[i], place, "rs_add_chips_" + BIG[i][0]) for i in range(N_BIG)])
    g_shard = {n: reduced[i] for i, (n, _, _, _) in enumerate(BIG)}

    small_sum = _allreduce_small(_pack_small(gs, loss_vec))
    loss = small_sum[9, 0]
    g_small = {n: small_sum[i:i + 1, :sz] for i, (n, sz) in enumerate(SMALL)}

    grads, delta, new_m, new_v = {}, {}, {}, {}
    for n, _, _, _ in BIG:
        d, nm, nv = _adamw(shard2d[n], g_shard[n], mom[n][0], var[n][0], "adamw_" + n)
        grads[n], delta[n], new_m[n], new_v[n] = g_shard[n][None], d[None], nm[None], nv[None]
    d, nm, nv = _adamw(_pack_small(small2d), small_sum, _pack_small(mom), _pack_small(var), "adamw_small")
    for i, (n, sz) in enumerate(SMALL):
        grads[n] = g_small[n]
        delta[n], new_m[n], new_v[n] = d[i:i + 1, :sz], nm[i:i + 1, :sz], nv[i:i + 1, :sz]

    return (loss, grad_x[None], *[grads[n] for n in ALL_W], *[delta[n] for n in ALL_W],
            *[new_m[n] for n in ALL_W], *[new_v[n] for n in ALL_W])
```

```python
import functools
import math

import jax
import jax.numpy as jnp
import numpy as np
from jax import lax
from jax.experimental import pallas as pl
from jax.experimental.pallas import tpu as pltpu

F32 = jnp.float32
BF16 = jnp.bfloat16
MESH = pl.DeviceIdType.MESH

D_MODEL = 1024
D_FF = 2816
PLE_DIM = 256
RET_HEADS = 4
RET_DIM = 128
RET_WIDTH = 512
RET_CHUNK = 128
RET_GROUP = 8
MLA_HEADS = 8
MLA_NOPE = 64
MLA_ROPE = 32
MLA_V = 64
Q_LORA = 384
KV_LORA = 256
IN_COLS = 2720
IN_COLS_P = 2816
ROPE_BASE = 10000.0
EPS = 1e-6
SCALE_MLA = 1.0 / math.sqrt(MLA_NOPE + MLA_ROPE)
SCALE_RET = RET_DIM ** -0.5
NEG = -1e30

ADAM_LR = 0.001
ADAM_B1 = 0.9
ADAM_B2 = 0.999
ADAM_EPS = 1e-08
ADAM_WD = 0.01
ADAM_STEP = 10

N_CHIPS = 4
N_DEV = 8
VMEM_MB = 56

BIG = (
    ("w_in", 1024, 2720, 1),
    ("w_uq", 384, 768, 1),
    ("w_ukv", 256, 1024, 1),
    ("w_o", 1024, 1024, 0),
    ("w_gate", 1024, 2816, 1),
    ("w_up", 1024, 2816, 1),
    ("w_down", 2816, 1024, 0),
    ("w_ple_proj", 256, 1024, 1),
    ("w_ple_gate", 1024, 1024, 0),
)
SMALL = (
    ("pre_mix_norm", 1024),
    ("ret_gn_w", 512),
    ("mla_q_norm", 384),
    ("mla_kv_norm", 256),
    ("post_mix_norm", 1024),
    ("pre_ffn_norm", 1024),
    ("post_ffn_norm", 1024),
    ("ple_norm", 1024),
    ("b_ple_gate", 1024),
)
ALL_W = ("pre_mix_norm", "w_in", "ret_gn_w", "mla_q_norm", "w_uq", "mla_kv_norm", "w_ukv", "w_o", "post_mix_norm",
         "pre_ffn_norm", "w_gate", "w_up", "w_down", "post_ffn_norm", "w_ple_proj", "ple_norm", "w_ple_gate", "b_ple_gate")
PACK_COLS = 1024
SHARD_ELEMS = sum(r * c for _, r, c, _ in BIG) // N_CHIPS
PACK_ROWS = -(-SHARD_ELEMS // PACK_COLS // 32) * 32
HALF_ROWS = PACK_ROWS // 2
SMALL_ROWS = 16


def _cp(sem=None, mb=VMEM_MB, **kw):
    return pltpu.CompilerParams(dimension_semantics=sem, vmem_limit_bytes=mb * 1024 * 1024, **kw)


def _bf(x):
    return x.astype(BF16)


def _dot(a, b):
    return jnp.dot(_bf(a), _bf(b), preferred_element_type=F32)


def _dot_nt(a, b):
    return lax.dot_general(_bf(a), _bf(b), (((1,), (1,)), ((), ())), preferred_element_type=F32)


def _dot_tn(a, b):
    return lax.dot_general(_bf(a), _bf(b), (((0,), (0,)), ((), ())), preferred_element_type=F32)


def _sig(x):
    return 1.0 / (1.0 + jnp.exp(-x))


def _rms(x, g):
    r = lax.rsqrt(jnp.mean(x * x, axis=-1, keepdims=True) + EPS)
    return x * r * g


def _rms_bwd(dy, x, g):
    r = lax.rsqrt(jnp.mean(x * x, axis=-1, keepdims=True) + EPS)
    xh = x * r
    dxh = dy * g
    dx = r * (dxh - xh * jnp.mean(dxh * xh, axis=-1, keepdims=True))
    return dx, dy * xh


def _colsum(x):
    return jnp.sum(x, axis=0, keepdims=True)


def _rope_ret(x, cr, sr):
    return x * cr + pltpu.roll(x, 64, 1) * sr


def _unrope_ret(dy, cr, sr):
    return dy * cr + pltpu.roll(dy * sr, 64, 1)


def _rope_mla(x, cm, sa, sb):
    return x * cm + pltpu.roll(x, 112, 1) * sa + pltpu.roll(x, 16, 1) * sb


def _unrope_mla(dy, cm, sa, sb):
    return dy * cm + pltpu.roll(dy * sa, 16, 1) + pltpu.roll(dy * sb, 112, 1)


def _rows(tm, w, col=0):
    return pl.BlockSpec((tm, w), lambda i: (i, col))


def _full(*shape):
    return pl.BlockSpec(shape, lambda i: (0,) * len(shape))


def _sds(shape, dtype):
    return jax.ShapeDtypeStruct(shape, dtype)


def _rope_tables(pos_f, S):
    tm = min(512, S)
    inv_r = (1.0 / (np.float32(ROPE_BASE) ** (np.arange(64, dtype=np.float32) / np.float32(64)))).astype(np.float32)
    inv_m16 = (1.0 / (np.float32(ROPE_BASE) ** (np.arange(16, dtype=np.float32) / np.float32(16)))).astype(np.float32)
    inv_r = np.concatenate([inv_r, inv_r])[None, :]
    inv_m = np.zeros((1, 128), np.float32)
    inv_m[0, 64:80] = inv_m16
    inv_m[0, 80:96] = inv_m16

    def body(pos_ref, invr_ref, invm_ref, cr_ref, sr_ref, cm_ref, sa_ref, sb_ref):
        pos = pos_ref[...]
        lane = lax.broadcasted_iota(jnp.int32, (tm, 128), 1)
        ar = pos * invr_ref[...]
        s = jnp.sin(ar)
        cr_ref[...] = jnp.cos(ar)
        sr_ref[...] = jnp.where(lane < 64, -s, s)
        am = pos * invm_ref[...]
        c2 = jnp.cos(am)
        s2 = jnp.sin(am)
        cm_ref[...] = jnp.where(lane < 64, 1.0, jnp.where(lane < 96, c2, 0.0))
        sa_ref[...] = jnp.where((lane >= 64) & (lane < 80), -s2, 0.0)
        sb_ref[...] = jnp.where((lane >= 80) & (lane < 96), s2, 0.0)

    return pl.pallas_call(
        body, name="rope_tables", grid=(S // tm,),
        in_specs=[_rows(tm, 1), _full(1, 128), _full(1, 128)],
        out_specs=[_rows(tm, 128)] * 5,
        out_shape=[_sds((S, 128), F32)] * 5,
        compiler_params=_cp(("parallel",)),
    )(pos_f, jnp.asarray(inv_r), jnp.asarray(inv_m))


def _rms_fwd(x, g, S):
    tm = min(512, S)

    def body(x_ref, g_ref, o_ref):
        o_ref[...] = _rms(x_ref[...], g_ref[...]).astype(BF16)

    return pl.pallas_call(
        body, name="rms_pre", grid=(S // tm,),
        in_specs=[_rows(tm, D_MODEL), _full(1, D_MODEL)],
        out_specs=_rows(tm, D_MODEL), out_shape=_sds((S, D_MODEL), BF16),
        compiler_params=_cp(("parallel",)),
    )(x, g)


def _inproj(xn, w_in, tabs, S):
    tm = min(256, S)

    def body(xn_ref, w_ref, cr_ref, sr_ref, cm_ref, sa_ref, sb_ref, rq_ref, rk_ref, rv_ref, rg_ref, cq_ref, ckv_ref, kr_ref):
        xb = xn_ref[...]
        cr = cr_ref[...]
        sr = sr_ref[...]
        q = jnp.dot(xb, w_ref[:, 0:512], preferred_element_type=F32)
        k = jnp.dot(xb, w_ref[:, 512:1024], preferred_element_type=F32)
        for h in range(RET_HEADS):
            sl = slice(h * 128, (h + 1) * 128)
            rq_ref[:, sl] = _rope_ret(q[:, sl], cr, sr).astype(BF16)
            rk_ref[:, sl] = (_rope_ret(k[:, sl], cr, sr) * SCALE_RET).astype(BF16)
        rv_ref[...] = jnp.dot(xb, w_ref[:, 1024:1536], preferred_element_type=F32).astype(BF16)
        rg_ref[...] = jnp.dot(xb, w_ref[:, 1536:2048], preferred_element_type=F32)
        cq_ref[...] = jnp.dot(xb, w_ref[:, 2048:2432], preferred_element_type=F32)
        ckv_ref[...] = jnp.dot(xb, w_ref[:, 2432:2688], preferred_element_type=F32)
        kr = jnp.dot(xb, w_ref[:, 2688:2816], preferred_element_type=F32)
        kr_ref[...] = _rope_mla(kr, cm_ref[...], sa_ref[...], sb_ref[...])

    return pl.pallas_call(
        body, name="inproj", grid=(S // tm,),
        in_specs=[_rows(tm, D_MODEL), _full(D_MODEL, IN_COLS_P)] + [_rows(tm, 128)] * 5,
        out_specs=[_rows(tm, 512)] * 4 + [_rows(tm, Q_LORA), _rows(tm, KV_LORA), _rows(tm, 128)],
        out_shape=[_sds((S, 512), BF16)] * 3 + [_sds((S, 512), F32), _sds((S, Q_LORA), F32), _sds((S, KV_LORA), F32),
                                                  _sds((S, 128), F32)],
        compiler_params=_cp(("parallel",)),
    )(xn, w_in, *tabs)


def _mla_up(cq, ckv, kr, gq, gkv, w_uq, w_ukv, tabs, S):
    tm = min(256, S)

    def body(cq_ref, ckv_ref, kr_ref, gq_ref, gkv_ref, wuq_ref, wukv_ref, cm_ref, sa_ref, sb_ref,
             cqn_ref, ckvn_ref, qp_ref, kp_ref, v_ref, kt_ref, vt_ref):
        cm = cm_ref[...]
        sa = sa_ref[...]
        sb = sb_ref[...]
        cqn = _rms(cq_ref[...], gq_ref[...]).astype(BF16)
        cqn_ref[...] = cqn
        ckvn = _rms(ckv_ref[...], gkv_ref[...]).astype(BF16)
        ckvn_ref[...] = ckvn
        qh = jnp.dot(cqn, wuq_ref[...], preferred_element_type=F32)
        kv = jnp.dot(ckvn, wukv_ref[...], preferred_element_type=F32)
        kr_blk = kr_ref[...]
        for h in range(MLA_HEADS):
            sl = slice(h * 128, (h + 1) * 128)
            qp_ref[:, sl] = (_rope_mla(qh[:, sl], cm, sa, sb) * SCALE_MLA).astype(BF16)
            kh = kv[:, sl] + kr_blk
            kp_ref[:, sl] = kh.astype(BF16)
            kt_ref[sl, :] = kh.T.astype(BF16)
        for h in range(MLA_HEADS // 2):
            vh = kv[:, 1024 + h * 128:1024 + (h + 1) * 128]
            v_ref[:, h * 128:(h + 1) * 128] = vh.astype(BF16)
            vt_ref[h * 128:(h + 1) * 128, :] = vh.T.astype(BF16)

    cols = lambda r: pl.BlockSpec((r, tm), lambda i: (0, i))
    return pl.pallas_call(
        body, name="mla_up", grid=(S // tm,),
        in_specs=[_rows(tm, Q_LORA), _rows(tm, KV_LORA), _rows(tm, 128), _full(1, Q_LORA), _full(1, KV_LORA),
                  _full(Q_LORA, 1024), _full(KV_LORA, 1536)] + [_rows(tm, 128)] * 3,
        out_specs=[_rows(tm, Q_LORA), _rows(tm, KV_LORA), _rows(tm, 1024), _rows(tm, 1024), _rows(tm, 512),
                   cols(1024), cols(512)],
        out_shape=[_sds((S, Q_LORA), BF16), _sds((S, KV_LORA), BF16), _sds((S, 1024), BF16), _sds((S, 1024), BF16),
                   _sds((S, 512), BF16), _sds((1024, S), BF16), _sds((512, S), BF16)],
        compiler_params=_cp(("parallel",)),
    )(cq, ckv, kr, gq, gkv, w_uq, w_ukv, *tabs[2:])


def _tri_pairs(nq, k_major):
    if k_major:
        pairs = [(qb, kb) for kb in range(nq) for qb in range(kb, nq)]
    else:
        pairs = [(qb, kb) for qb in range(nq) for kb in range(qb + 1)]
    qb_of = np.array([p[0] for p in pairs], np.int32)
    kb_of = np.array([p[1] for p in pairs], np.int32)
    return jnp.asarray(qb_of), jnp.asarray(kb_of), len(pairs)


ATT_ROWS = 32
FWD_HEADS = 4


def _causal_keep(r0, rows, tq):
    key = r0 + lax.broadcasted_iota(jnp.int32, (rows, tq), 0)
    qry = lax.broadcasted_iota(jnp.int32, (rows, tq), 1)
    return key <= qry


def _flash_fwd(qp, kp, vt, S):
    tq = min(512, S)
    nq = S // tq
    RB = ATT_ROWS
    NH = FWD_HEADS
    qb_of, kb_of, T = _tri_pairs(nq, k_major=False)

    def body(qb_ref, kb_ref, q_ref, k_ref, vt_ref, o_ref, lse_ref, m_sc, l_sc, acc_sc, s_sc, p_sc):
        t = pl.program_id(1)
        qb = qb_ref[t]
        kb = kb_ref[t]

        @pl.when(kb == 0)
        def _():
            m_sc[...] = jnp.full(m_sc.shape, NEG, F32)
            l_sc[...] = jnp.zeros(l_sc.shape, F32)
            acc_sc[...] = jnp.zeros(acc_sc.shape, F32)

        def step(masked):
            for a in range(NH):
                sl = slice(a * 128, (a + 1) * 128)
                s_sc[a] = _dot_nt(k_ref[:, sl], q_ref[:, sl])
            m_new, al = [], []
            for a in range(NH):
                mx = [jnp.full((8, tq), NEG, F32) for _ in range(RB // 8)]
                for r in range(0, tq, RB):
                    sc = s_sc[a, r:r + RB, :]
                    if masked:
                        sc = jnp.where(_causal_keep(r, RB, tq), sc, NEG)
                        s_sc[a, r:r + RB, :] = sc
                    for i in range(RB // 8):
                        mx[i] = jnp.maximum(mx[i], sc[i * 8:(i + 1) * 8, :])
                mx8 = jnp.maximum(jnp.maximum(mx[0], mx[1]), jnp.maximum(mx[2], mx[3]))
                m_prev = m_sc[a]
                m_new.append(jnp.maximum(m_prev, jnp.max(mx8, axis=0, keepdims=True)))
                al.append(jnp.exp(m_prev - m_new[a]))
                m_sc[a] = m_new[a]
            for a in range(NH):
                ls = [jnp.zeros((8, tq), F32) for _ in range(RB // 8)]
                for r in range(0, tq, RB):
                    p = jnp.exp(s_sc[a, r:r + RB, :] - m_new[a])
                    for i in range(RB // 8):
                        ls[i] = ls[i] + p[i * 8:(i + 1) * 8, :]
                    p_sc[a, r:r + RB, :] = p.astype(BF16)
                l_sc[a] = al[a] * l_sc[a] + jnp.sum((ls[0] + ls[1]) + (ls[2] + ls[3]), axis=0, keepdims=True)
                pair = slice((a // 2) * 128, (a // 2 + 1) * 128)
                pv = jnp.dot(vt_ref[pair, :], p_sc[a], preferred_element_type=F32)
                rs = slice(a * 64, (a + 1) * 64)
                own = slice((a % 2) * 64, (a % 2 + 1) * 64)
                acc_sc[rs, :] = acc_sc[rs, :] * al[a] + pv[own, :]

        @pl.when(kb < qb)
        def _():
            step(False)

        @pl.when(kb == qb)
        def _():
            step(True)
            for a in range(NH):
                rs = slice(a * 64, (a + 1) * 64)
                acc_sc[rs, :] = acc_sc[rs, :] / l_sc[a]
                lse_ref[a:a + 1, :] = m_sc[a] + jnp.log(l_sc[a])
            o_ref[...] = acc_sc[...].T.astype(BF16)

    grid_spec = pltpu.PrefetchScalarGridSpec(
        num_scalar_prefetch=2, grid=(MLA_HEADS // NH, T),
        in_specs=[pl.BlockSpec((tq, 128 * NH), lambda j, t, qb, kb: (qb[t], j)),
                  pl.BlockSpec((tq, 128 * NH), lambda j, t, qb, kb: (kb[t], j)),
                  pl.BlockSpec((64 * NH, tq), lambda j, t, qb, kb: (j, kb[t]))],
        out_specs=[pl.BlockSpec((tq, 64 * NH), lambda j, t, qb, kb: (qb[t], j)),
                   pl.BlockSpec((None, NH, tq), lambda j, t, qb, kb: (j, 0, qb[t]))],
        scratch_shapes=[pltpu.VMEM((NH, 1, tq), F32), pltpu.VMEM((NH, 1, tq), F32), pltpu.VMEM((64 * NH, tq), F32),
                        pltpu.VMEM((NH, tq, tq), F32), pltpu.VMEM((NH, tq, tq), BF16)],
    )
    out, lse = pl.pallas_call(
        body, name="flash_fwd", grid_spec=grid_spec,
        out_shape=[_sds((S, 512), BF16), _sds((MLA_HEADS // NH, NH, S), F32)],
        compiler_params=_cp(("parallel", "arbitrary")),
    )(qb_of, kb_of, qp, kp, vt)
    return out, lse.reshape(MLA_HEADS // 2, 2, S)


def _decay_table():
    log_g = np.log(1.0 - 2.0 ** (-5.0 - np.arange(RET_HEADS, dtype=np.float32))).astype(np.float32)
    return jnp.asarray(np.broadcast_to(log_g[:, None, None], (RET_HEADS, 8, 128)).copy())


def _decay_terms(lg_ref):
    C = RET_CHUNK
    lg = lg_ref[0:1, :]
    row = lax.broadcasted_iota(jnp.int32, (C, C), 0)
    col = lax.broadcasted_iota(jnp.int32, (C, C), 1)
    diff = (row - col).astype(F32)
    dmat = jnp.where(diff >= 0, jnp.exp(jnp.maximum(diff, 0.0) * lg), 0.0)
    j = lax.broadcasted_iota(jnp.int32, (C, 1), 0).astype(F32)
    lg1 = lg[:, 0:1]
    zeta = jnp.exp((C - 1 - j) * lg1)
    xi = jnp.exp((j + 1.0) * lg1)
    g_chunk = jnp.exp(C * lg1)
    return dmat, zeta, xi, g_chunk


def _ret_fwd(rq, rk, rv, rg, gn_w, S):
    C = RET_CHUNK
    N = S // C
    G = min(RET_GROUP, N)
    NB = N // G

    def body(lg_ref, q_ref, k_ref, v_ref, rg_ref, w_ref, ry_ref, ro_ref, rprev_ref, r_sc):
        @pl.when(pl.program_id(1) == 0)
        def _():
            r_sc[...] = jnp.zeros(r_sc.shape, F32)

        dmat, zeta, xi, g_chunk = _decay_terms(lg_ref)
        w = w_ref[...]
        r = r_sc[...]
        for i in range(G):
            rows = slice(i * C, (i + 1) * C)
            q = q_ref[rows, :]
            k = k_ref[rows, :]
            v = v_ref[rows, :]
            r_prev = r.astype(BF16)
            rprev_ref[i] = r_prev
            sc = _dot_nt(q, k) * dmat
            ry = _dot(sc, v) + jnp.dot(q, r_prev, preferred_element_type=F32) * xi
            ry_ref[rows, :] = ry
            r = g_chunk * r + _dot_tn(k, zeta * v.astype(F32))
            mu = jnp.mean(ry, axis=-1, keepdims=True)
            yc = ry - mu
            yh = yc * lax.rsqrt(jnp.mean(yc * yc, axis=-1, keepdims=True) + EPS)
            g = rg_ref[rows, :]
            ro_ref[rows, :] = (g * _sig(g) * (yh * w)).astype(BF16)
        r_sc[...] = r

    blk = pl.BlockSpec((G * C, 128), lambda h, n: (n, h))
    return pl.pallas_call(
        body, name="ret_fwd", grid=(RET_HEADS, NB),
        in_specs=[pl.BlockSpec((None, 8, 128), lambda h, n: (h, 0, 0)), blk, blk, blk, blk,
                  pl.BlockSpec((1, 128), lambda h, n: (0, h))],
        out_specs=[blk, blk, pl.BlockSpec((G, 128, 128), lambda h, n: (h * NB + n, 0, 0))],
        out_shape=[_sds((S, 512), F32), _sds((S, 512), BF16), _sds((RET_HEADS * N, 128, 128), BF16)],
        scratch_shapes=[pltpu.VMEM((128, 128), F32)],
        compiler_params=_cp(("parallel", "arbitrary")),
    )(_decay_table(), rq, rk, rv, rg, gn_w)


def _outproj(ro, mo, x, w_o, g_post, g_pre, S):
    tm = min(256, S)

    def body(ro_ref, mo_ref, x_ref, wo_ref, g1_ref, g2_ref, mix_ref, h1_ref, hn_ref):
        mix = (jnp.dot(ro_ref[...], wo_ref[0:512, :], preferred_element_type=F32)
               + jnp.dot(mo_ref[...], wo_ref[512:1024, :], preferred_element_type=F32))
        mix_ref[...] = mix
        h1 = x_ref[...] + _rms(mix, g1_ref[...])
        h1_ref[...] = h1
        hn_ref[...] = _rms(h1, g2_ref[...]).astype(BF16)

    return pl.pallas_call(
        body, name="outproj", grid=(S // tm,),
        in_specs=[_rows(tm, 512), _rows(tm, 512), _rows(tm, D_MODEL), _full(D_MODEL, D_MODEL), _full(1, D_MODEL),
                  _full(1, D_MODEL)],
        out_specs=[_rows(tm, D_MODEL)] * 3,
        out_shape=[_sds((S, D_MODEL), F32), _sds((S, D_MODEL), F32), _sds((S, D_MODEL), BF16)],
        compiler_params=_cp(("parallel",)),
    )(ro, mo, x, w_o, g_post, g_pre)


def _ffn_up(hn, w_gate, w_up, S):
    tm = min(512, S)
    tn = D_FF // 2

    def body(hn_ref, wg_ref, wu_ref, gate_ref, up_ref, act_ref):
        hn_b = hn_ref[...]
        g = jnp.dot(hn_b, wg_ref[...], preferred_element_type=F32)
        u = jnp.dot(hn_b, wu_ref[...], preferred_element_type=F32)
        gate_ref[...] = g.astype(BF16)
        up_ref[...] = u.astype(BF16)
        act_ref[...] = (g * _sig(g) * u).astype(BF16)

    wspec = pl.BlockSpec((D_MODEL, tn), lambda j, i: (0, j))
    ospec = pl.BlockSpec((tm, tn), lambda j, i: (i, j))
    return pl.pallas_call(
        body, name="ffn_up", grid=(2, S // tm),
        in_specs=[pl.BlockSpec((tm, D_MODEL), lambda j, i: (i, 0)), wspec, wspec],
        out_specs=[ospec] * 3, out_shape=[_sds((S, D_FF), BF16)] * 3,
        compiler_params=_cp(("parallel", "parallel")),
    )(hn, w_gate, w_up)


def _ffn_down(act, w_down, h1, g, S):
    tm = min(256, S)

    def body(act_ref, wd_ref, h1_ref, g_ref, ff_ref, h2_ref):
        ff = jnp.dot(act_ref[...], wd_ref[...], preferred_element_type=F32)
        ff_ref[...] = ff
        h2_ref[...] = h1_ref[...] + _rms(ff, g_ref[...])

    return pl.pallas_call(
        body, name="ffn_down", grid=(S // tm,),
        in_specs=[_rows(tm, D_FF), _full(D_FF, D_MODEL), _rows(tm, D_MODEL), _full(1, D_MODEL)],
        out_specs=[_rows(tm, D_MODEL)] * 2, out_shape=[_sds((S, D_MODEL), F32)] * 2,
        compiler_params=_cp(("parallel",)),
    )(act, w_down, h1, g)


def _ple_loss(p, h2, tgt, w_pp, w_pg, b_pg, g_ple, S):
    tm = min(256, S)

    def body(p_ref, h2_ref, t_ref, wp_ref, wg_ref, b_ref, gp_ref,
             dz_ref, dpe_ref, dh2_ref, h2b_ref, loss_ref, dgp_ref, db_ref):
        @pl.when(pl.program_id(0) == 0)
        def _():
            loss_ref[...] = jnp.zeros(loss_ref.shape, F32)
            dgp_ref[...] = jnp.zeros(dgp_ref.shape, F32)
            db_ref[...] = jnp.zeros(db_ref.shape, F32)

        gp = gp_ref[...]
        pe = _dot(p_ref[...], wp_ref[...])
        r = lax.rsqrt(jnp.mean(pe * pe, axis=-1, keepdims=True) + EPS)
        peh = pe * r
        e = peh * gp
        h2 = h2_ref[...]
        h2b = h2.astype(BF16)
        h2b_ref[...] = h2b
        gt = _sig(jnp.dot(h2b, wg_ref[...], preferred_element_type=F32) + b_ref[...])
        diff = h2 + e * gt - t_ref[...]
        loss_ref[...] += _colsum(diff * diff)
        dh3 = diff * (1.0 / D_MODEL)
        de = dh3 * gt
        dz = dh3 * e * gt * (1.0 - gt)
        db_ref[...] += _colsum(dz)
        dgp_ref[...] += _colsum(de * peh)
        dpeh = de * gp
        dpe = r * (dpeh - peh * jnp.mean(dpeh * peh, axis=-1, keepdims=True))
        dzb = dz.astype(BF16)
        dz_ref[...] = dzb
        dpe_ref[...] = dpe.astype(BF16)
        dh2_ref[...] = dh3 + _dot_nt(dzb, wg_ref[...])

    return pl.pallas_call(
        body, name="ple_loss", grid=(S // tm,),
        in_specs=[_rows(tm, PLE_DIM), _rows(tm, D_MODEL), _rows(tm, D_MODEL), _full(PLE_DIM, D_MODEL),
                  _full(D_MODEL, D_MODEL), _full(1, D_MODEL), _full(1, D_MODEL)],
        out_specs=[_rows(tm, D_MODEL)] * 4 + [_full(1, D_MODEL)] * 3,
        out_shape=[_sds((S, D_MODEL), BF16), _sds((S, D_MODEL), BF16), _sds((S, D_MODEL), F32), _sds((S, D_MODEL), BF16)]
        + [_sds((1, D_MODEL), F32)] * 3,
        compiler_params=_cp(("arbitrary",)),
    )(p, h2, tgt, w_pp, w_pg, b_pg, g_ple)


def _wgrad(a, b, name, S):
    M = a.shape[1]
    N = b.shape[1]
    ts = min(512, S)
    nsplit = 2 if M * N >= 2 * 1024 * 1024 else 1
    tn = N // nsplit

    def body(a_ref, b_ref, o_ref):
        @pl.when(pl.program_id(1) == 0)
        def _():
            o_ref[...] = jnp.zeros(o_ref.shape, F32)

        o_ref[...] += _dot_tn(a_ref[...], b_ref[...])

    return pl.pallas_call(
        body, name=name, grid=(nsplit, S // ts),
        in_specs=[pl.BlockSpec((ts, M), lambda j, s: (s, 0)), pl.BlockSpec((ts, tn), lambda j, s: (s, j))],
        out_specs=pl.BlockSpec((M, tn), lambda j, s: (0, j)), out_shape=_sds((M, N), F32),
        compiler_params=_cp(("parallel", "arbitrary")),
    )(a, b)


def _ffn_down_bwd(dh2, ff, g, w_down, gate, up, S):
    tm = min(256, S)
    tn = D_FF // 2

    def body(dh2_ref, ff_ref, g_ref, wd_ref, gate_ref, up_ref, dff_ref, dgate_ref, dup_ref, dg_ref):
        @pl.when(pl.program_id(0) == 0)
        def _():
            dg_ref[...] = jnp.zeros(dg_ref.shape, F32)

        dff, ga = _rms_bwd(dh2_ref[...], ff_ref[...], g_ref[...])
        dg_ref[...] += _colsum(ga)
        dffb = dff.astype(BF16)
        dff_ref[...] = dffb
        for seg in range(2):
            sl = slice(seg * tn, (seg + 1) * tn)
            dact = _dot_nt(dffb, wd_ref[sl, :])
            gt = gate_ref[:, sl].astype(F32)
            u = up_ref[:, sl].astype(F32)
            s = _sig(gt)
            dgate_ref[:, sl] = (dact * u * (s * (1.0 + gt * (1.0 - s)))).astype(BF16)
            dup_ref[:, sl] = (dact * (gt * s)).astype(BF16)

    return pl.pallas_call(
        body, name="ffn_down_bwd", grid=(S // tm,),
        in_specs=[_rows(tm, D_MODEL), _rows(tm, D_MODEL), _full(1, D_MODEL), _full(D_FF, D_MODEL), _rows(tm, D_FF),
                  _rows(tm, D_FF)],
        out_specs=[_rows(tm, D_MODEL), _rows(tm, D_FF), _rows(tm, D_FF), _full(1, D_MODEL)],
        out_shape=[_sds((S, D_MODEL), BF16), _sds((S, D_FF), BF16), _sds((S, D_FF), BF16), _sds((1, D_MODEL), F32)],
        compiler_params=_cp(("arbitrary",)),
    )(dh2, ff, g, w_down, gate, up)


def _ffn_up_bwd(dgate, dup, w_gate, w_up, h1, mix, dh2, g_pre, g_post, w_o, S):
    tm = min(256, S)

    def body(dgate_ref, dup_ref, wg_ref, wu_ref, h1_ref, mix_ref, dh2_ref, g2_ref, g1_ref, wo_ref,
             dh1_ref, dmix_ref, dro_ref, dmo_ref, dg2_ref, dg1_ref):
        @pl.when(pl.program_id(0) == 0)
        def _():
            dg2_ref[...] = jnp.zeros(dg2_ref.shape, F32)
            dg1_ref[...] = jnp.zeros(dg1_ref.shape, F32)

        dhn = _dot_nt(dgate_ref[...], wg_ref[...]) + _dot_nt(dup_ref[...], wu_ref[...])
        d1, ga = _rms_bwd(dhn, h1_ref[...], g2_ref[...])
        dg2_ref[...] += _colsum(ga)
        dh1 = dh2_ref[...] + d1
        dh1_ref[...] = dh1
        dmix, gb = _rms_bwd(dh1, mix_ref[...], g1_ref[...])
        dg1_ref[...] += _colsum(gb)
        dmixb = dmix.astype(BF16)
        dmix_ref[...] = dmixb
        dcat = _dot_nt(dmixb, wo_ref[...])
        dro_ref[...] = dcat[:, 0:512].astype(BF16)
        dmo_ref[...] = dcat[:, 512:1024].astype(BF16)

    return pl.pallas_call(
        body, name="ffn_up_bwd", grid=(S // tm,),
        in_specs=[_rows(tm, D_FF), _rows(tm, D_FF), _full(D_MODEL, D_FF), _full(D_MODEL, D_FF), _rows(tm, D_MODEL),
                  _rows(tm, D_MODEL), _rows(tm, D_MODEL), _full(1, D_MODEL), _full(1, D_MODEL), _full(D_MODEL, D_MODEL)],
        out_specs=[_rows(tm, D_MODEL), _rows(tm, D_MODEL), _rows(tm, 512), _rows(tm, 512), _full(1, D_MODEL),
                   _full(1, D_MODEL)],
        out_shape=[_sds((S, D_MODEL), F32), _sds((S, D_MODEL), BF16), _sds((S, 512), BF16), _sds((S, 512), BF16),
                   _sds((1, D_MODEL), F32), _sds((1, D_MODEL), F32)],
        compiler_params=_cp(("arbitrary",)),
    )(dgate, dup, w_gate, w_up, h1, mix, dh2, g_pre, g_post, w_o)


def _attn_delta(o, do, S):
    tm = min(512, S)

    def body(o_ref, do_ref, dot_ref, d_ref):
        do = do_ref[...].astype(F32)
        prod_t = (o_ref[...].astype(F32) * do).T
        dot_ref[...] = do.T.astype(BF16)
        for h in range(MLA_HEADS):
            d_ref[h // 2, (h % 2):(h % 2) + 1, :] = jnp.sum(prod_t[h * 64:(h + 1) * 64, :], axis=0, keepdims=True)

    return pl.pallas_call(
        body, name="attn_delta", grid=(S // tm,),
        in_specs=[_rows(tm, 512), _rows(tm, 512)],
        out_specs=[pl.BlockSpec((512, tm), lambda i: (0, i)), pl.BlockSpec((MLA_HEADS // 2, 2, tm), lambda i: (0, 0, i))],
        out_shape=[_sds((512, S), BF16), _sds((MLA_HEADS // 2, 2, S), F32)],
        compiler_params=_cp(("parallel",)),
    )(o, do)


def _flash_bwd(qp, kp, kt, v, do, dot, lse, delta, S):
    tq = min(512, S)
    nq = S // tq
    RB = ATT_ROWS
    qb_of, kb_of, T = _tri_pairs(nq, k_major=True)

    def body(qb_ref, kb_ref, q_ref, k_ref, kt_ref, v_ref, do_ref, dot_ref, lse_ref, dl_ref, dq_ref, dk_ref, dv_ref,
             dk_sc, dv_sc, s_sc, dp_sc, p_sc, ds_sc):
        t = pl.program_id(1)
        qb = qb_ref[t]
        kb = kb_ref[t]

        @pl.when(t == 0)
        def _():
            dq_ref[...] = jnp.zeros(dq_ref.shape, F32)

        @pl.when(qb == kb)
        def _():
            dk_sc[...] = jnp.zeros(dk_sc.shape, F32)
            dv_sc[...] = jnp.zeros(dv_sc.shape, F32)

        lane = lax.broadcasted_iota(jnp.int32, (tq, 128), 1)

        def step(masked):
            vv = v_ref[...]
            do_all = do_ref[...]
            mine = [lane < 64, lane >= 64]
            for a in range(2):
                sl = slice(a * 128, (a + 1) * 128)
                s_sc[a] = _dot_nt(k_ref[:, sl], q_ref[:, sl])
                dp_sc[a] = jnp.dot(jnp.where(mine[a], vv, jnp.zeros_like(vv)), dot_ref[...],
                                   preferred_element_type=F32)
            for a in range(2):
                sl = slice(a * 128, (a + 1) * 128)
                lse = lse_ref[a:a + 1, :]
                dl = dl_ref[a:a + 1, :]
                for r in range(0, tq, RB):
                    sc = s_sc[a, r:r + RB, :]
                    if masked:
                        sc = jnp.where(_causal_keep(r, RB, tq), sc, NEG)
                    p = jnp.exp(sc - lse)
                    p_sc[a, r:r + RB, :] = p.astype(BF16)
                    ds_sc[a, r:r + RB, :] = (p * (dp_sc[a, r:r + RB, :] - dl)).astype(BF16)
                ds = ds_sc[a]
                dv_sc[...] += jnp.dot(p_sc[a], jnp.where(mine[a], do_all, jnp.zeros_like(do_all)),
                                      preferred_element_type=F32)
                dk_sc[:, sl] += jnp.dot(ds, q_ref[:, sl], preferred_element_type=F32)
                dq_ref[qb, sl, :] += jnp.dot(kt_ref[sl, :], ds, preferred_element_type=F32)

        @pl.when(qb > kb)
        def _():
            step(False)

        @pl.when(qb == kb)
        def _():
            step(True)

        @pl.when(qb == nq - 1)
        def _():
            dk_ref[...] = dk_sc[...]
            dv_ref[...] = dv_sc[...]

    grid_spec = pltpu.PrefetchScalarGridSpec(
        num_scalar_prefetch=2, grid=(MLA_HEADS // 2, T),
        in_specs=[pl.BlockSpec((tq, 256), lambda j, t, qb, kb: (qb[t], j)),
                  pl.BlockSpec((tq, 256), lambda j, t, qb, kb: (kb[t], j)),
                  pl.BlockSpec((256, tq), lambda j, t, qb, kb: (j, kb[t])),
                  pl.BlockSpec((tq, 128), lambda j, t, qb, kb: (kb[t], j)),
                  pl.BlockSpec((tq, 128), lambda j, t, qb, kb: (qb[t], j)),
                  pl.BlockSpec((128, tq), lambda j, t, qb, kb: (j, qb[t])),
                  pl.BlockSpec((None, 2, tq), lambda j, t, qb, kb: (j, 0, qb[t])),
                  pl.BlockSpec((None, 2, tq), lambda j, t, qb, kb: (j, 0, qb[t]))],
        out_specs=[pl.BlockSpec((nq, 256, tq), lambda j, t, qb, kb: (0, j, 0)),
                   pl.BlockSpec((tq, 256), lambda j, t, qb, kb: (kb[t], j)),
                   pl.BlockSpec((tq, 128), lambda j, t, qb, kb: (kb[t], j))],
        scratch_shapes=[pltpu.VMEM((tq, 256), F32), pltpu.VMEM((tq, 128), F32), pltpu.VMEM((2, tq, tq), F32),
                        pltpu.VMEM((2, tq, tq), F32), pltpu.VMEM((2, tq, tq), BF16), pltpu.VMEM((2, tq, tq), BF16)],
    )
    return pl.pallas_call(
        body, name="flash_bwd", grid_spec=grid_spec,
        out_shape=[_sds((nq, 1024, tq), F32), _sds((S, 1024), F32), _sds((S, 512), F32)],
        compiler_params=_cp(("parallel", "arbitrary")),
    )(qb_of, kb_of, qp, kp, kt, v, do, dot, lse, delta)


def _mla_up_bwd(dqp, dkp, dv, cq, ckv, gq, gkv, w_uq, w_ukv, tabs, S):
    tm = min(256, S)

    def body(dq_ref, dk_ref, dv_ref, cq_ref, ckv_ref, gq_ref, gkv_ref, wuq_ref, wukv_ref, cm_ref, sa_ref, sb_ref,
             dqh_ref, dkv_ref, dcq_ref, dckv_ref, dkr_ref, dgq_ref, dgkv_ref):
        @pl.when(pl.program_id(0) == 0)
        def _():
            dgq_ref[...] = jnp.zeros(dgq_ref.shape, F32)
            dgkv_ref[...] = jnp.zeros(dgkv_ref.shape, F32)

        cm = cm_ref[...]
        sa = sa_ref[...]
        sb = sb_ref[...]
        lane = lax.broadcasted_iota(jnp.int32, (tm, 128), 1)
        dkr_r = jnp.zeros((tm, 128), F32)
        for h in range(MLA_HEADS):
            sl = slice(h * 128, (h + 1) * 128)
            dqh_ref[:, sl] = (_unrope_mla(dq_ref[sl, :].T, cm, sa, sb) * SCALE_MLA).astype(BF16)
            gk = dk_ref[:, sl]
            dkr_r = dkr_r + gk
            dkv_ref[:, sl] = gk.astype(BF16)
        dkr_r = jnp.where((lane >= 64) & (lane < 96), dkr_r, 0.0)
        dkr_ref[...] = _unrope_mla(dkr_r, cm, sa, sb).astype(BF16)
        dkv_ref[:, 1024:1536] = dv_ref[...].astype(BF16)
        dcq, ga = _rms_bwd(_dot_nt(dqh_ref[...], wuq_ref[...]), cq_ref[...], gq_ref[...])
        dcq_ref[...] = dcq.astype(BF16)
        dgq_ref[...] += _colsum(ga)
        dckv, gb = _rms_bwd(_dot_nt(dkv_ref[...], wukv_ref[...]), ckv_ref[...], gkv_ref[...])
        dckv_ref[...] = dckv.astype(BF16)
        dgkv_ref[...] += _colsum(gb)

    per_q = dqp.shape[2] // tm
    return pl.pallas_call(
        body, name="mla_up_bwd", grid=(S // tm,),
        in_specs=[pl.BlockSpec((None, 1024, tm), lambda i: (i // per_q, 0, i % per_q)),
                  _rows(tm, 1024), _rows(tm, 512), _rows(tm, Q_LORA), _rows(tm, KV_LORA),
                  _full(1, Q_LORA), _full(1, KV_LORA), _full(Q_LORA, 1024), _full(KV_LORA, 1536)] + [_rows(tm, 128)] * 3,
        out_specs=[_rows(tm, 1024), _rows(tm, 1536), _rows(tm, Q_LORA), _rows(tm, KV_LORA), _rows(tm, 128),
                   _full(1, Q_LORA), _full(1, KV_LORA)],
        out_shape=[_sds((S, 1024), BF16), _sds((S, 1536), BF16), _sds((S, Q_LORA), BF16), _sds((S, KV_LORA), BF16),
                   _sds((S, 128), BF16), _sds((1, Q_LORA), F32), _sds((1, KV_LORA), F32)],
        compiler_params=_cp(("arbitrary",)),
    )(dqp, dkp, dv, cq, ckv, gq, gkv, w_uq, w_ukv, *tabs[2:])


def _ret_bwd(rq, rk, rv, rprev, ry, rg, dro, gn_w, tabs, S):
    C = RET_CHUNK
    N = S // C
    G = min(RET_GROUP, N)
    NB = N // G

    def body(lg_ref, q_ref, k_ref, v_ref, rp_ref, ry_ref, rg_ref, dro_ref, w_ref, cr_ref, sr_ref,
             drq_ref, drk_ref, drv_ref, drg_ref, dw_ref, g_sc):
        @pl.when(pl.program_id(1) == 0)
        def _():
            g_sc[...] = jnp.zeros(g_sc.shape, F32)
            dw_ref[...] = jnp.zeros(dw_ref.shape, F32)

        dmat, zeta, xi, g_chunk = _decay_terms(lg_ref)
        w = w_ref[...]
        gacc = g_sc[...]
        dw = jnp.zeros((1, 128), F32)
        for i in reversed(range(G)):
            rows = slice(i * C, (i + 1) * C)
            ry = ry_ref[rows, :]
            mu = jnp.mean(ry, axis=-1, keepdims=True)
            yc = ry - mu
            rstd = lax.rsqrt(jnp.mean(yc * yc, axis=-1, keepdims=True) + EPS)
            yh = yc * rstd
            g = rg_ref[rows, :]
            s = _sig(g)
            dout = dro_ref[rows, :].astype(F32)
            drg_ref[rows, :] = (dout * (yh * w) * (s * (1.0 + g * (1.0 - s)))).astype(BF16)
            dgn = dout * (g * s)
            dw = dw + _colsum(dgn * yh)
            dyh = dgn * w
            dry = rstd * (dyh - jnp.mean(dyh, axis=-1, keepdims=True) - yh * jnp.mean(dyh * yh, axis=-1, keepdims=True))
            do = dry.astype(BF16)

            q = q_ref[rows, :]
            k = k_ref[rows, :]
            v = v_ref[rows, :]
            gfut = gacc.astype(BF16)
            sc = (_dot_nt(q, k) * dmat).astype(BF16)
            dsc = (_dot_nt(do, v) * dmat).astype(BF16)
            dq = jnp.dot(dsc, k, preferred_element_type=F32) + _dot_nt(do, rp_ref[i]) * xi
            dk = _dot_tn(dsc, q) + _dot_nt(v, gfut) * zeta
            dv = _dot_tn(sc, do) + jnp.dot(k, gfut, preferred_element_type=F32) * zeta
            gacc = g_chunk * gacc + _dot_tn(q, xi * dry)
            cr = cr_ref[rows, :]
            sr = sr_ref[rows, :]
            drq_ref[rows, :] = _unrope_ret(dq, cr, sr).astype(BF16)
            drk_ref[rows, :] = _unrope_ret(dk * SCALE_RET, cr, sr).astype(BF16)
            drv_ref[rows, :] = dv.astype(BF16)
        g_sc[...] = gacc
        dw_ref[...] += dw

    blk = pl.BlockSpec((G * C, 128), lambda h, n: (NB - 1 - n, h))
    tab = pl.BlockSpec((G * C, 128), lambda h, n: (NB - 1 - n, 0))
    return pl.pallas_call(
        body, name="ret_bwd", grid=(RET_HEADS, NB),
        in_specs=[pl.BlockSpec((None, 8, 128), lambda h, n: (h, 0, 0)), blk, blk, blk,
                  pl.BlockSpec((G, 128, 128), lambda h, n: (h * NB + NB - 1 - n, 0, 0)), blk, blk, blk,
                  pl.BlockSpec((1, 128), lambda h, n: (0, h)), tab, tab],
        out_specs=[blk, blk, blk, blk, pl.BlockSpec((1, 128), lambda h, n: (0, h))],
        out_shape=[_sds((S, 512), BF16)] * 4 + [_sds((1, 512), F32)],
        scratch_shapes=[pltpu.VMEM((128, 128), F32)],
        compiler_params=_cp(("parallel", "arbitrary")),
    )(_decay_table(), rq, rk, rv, rprev, ry, rg, dro, gn_w, tabs[0], tabs[1])


def _inproj_bwd(drq, drk, drv, drg, dcq, dckv, dkr, w_in, dh1, x, g, S):
    tm = min(256, S)

    def body(drq_ref, drk_ref, drv_ref, drg_ref, dcq_ref, dckv_ref, dkr_ref, w_ref, dh1_ref, x_ref, g_ref,
             gx_ref, dproj_ref, dg_ref):
        @pl.when(pl.program_id(0) == 0)
        def _():
            dg_ref[...] = jnp.zeros(dg_ref.shape, F32)

        dproj_ref[:, 0:512] = drq_ref[...]
        dproj_ref[:, 512:1024] = drk_ref[...]
        dproj_ref[:, 1024:1536] = drv_ref[...]
        dproj_ref[:, 1536:2048] = drg_ref[...]
        dproj_ref[:, 2048:2432] = dcq_ref[...]
        dproj_ref[:, 2432:2688] = dckv_ref[...]
        dproj_ref[:, 2688:2816] = dkr_ref[...]
        dx, ga = _rms_bwd(_dot_nt(dproj_ref[...], w_ref[...]), x_ref[...], g_ref[...])
        gx_ref[...] = dh1_ref[...] + dx
        dg_ref[...] += _colsum(ga)

    return pl.pallas_call(
        body, name="inproj_bwd", grid=(S // tm,),
        in_specs=[_rows(tm, 512)] * 4 + [_rows(tm, Q_LORA), _rows(tm, KV_LORA), _rows(tm, 128),
                                         _full(D_MODEL, IN_COLS_P), _rows(tm, D_MODEL), _rows(tm, D_MODEL),
                                         _full(1, D_MODEL)],
        out_specs=[_rows(tm, D_MODEL), _rows(tm, IN_COLS_P), _full(1, D_MODEL)],
        out_shape=[_sds((S, D_MODEL), F32), _sds((S, IN_COLS_P), BF16), _sds((1, D_MODEL), F32)],
        compiler_params=_cp(("arbitrary",)),
    )(drq, drk, drv, drg, dcq, dckv, dkr, w_in, dh1, x, g)


def _pad_weights(w):
    w_in = w["w_in"]
    z = lambda r, c: jnp.zeros((r, c), BF16)
    w_in_p = jnp.concatenate([w_in[:, :2688], z(1024, 64), w_in[:, 2688:2720], z(1024, 32)], axis=1)
    w_uq_p = jnp.pad(w["w_uq"].reshape(Q_LORA, MLA_HEADS, 96), ((0, 0), (0, 0), (0, 32))).reshape(Q_LORA, 1024)
    ukv = w["w_ukv"].reshape(KV_LORA, MLA_HEADS, 128)
    k_part = jnp.pad(ukv[:, :, :64], ((0, 0), (0, 0), (0, 64))).reshape(KV_LORA, 1024)
    w_ukv_p = jnp.concatenate([k_part, ukv[:, :, 64:].reshape(KV_LORA, 512)], axis=1)
    return w_in_p, w_uq_p, w_ukv_p


def _local_step(x, p, pos_f, tgt, w, sm):
    S = x.shape[0]
    w_in_p, w_uq_p, w_ukv_p = _pad_weights(w)
    tabs = _rope_tables(pos_f, S)

    xn = _rms_fwd(x, sm["pre_mix_norm"], S)
    rq, rk, rv, rg, cq, ckv, kr = _inproj(xn, w_in_p, tabs, S)
    cqn, ckvn, qp, kp, v, kt, vt = _mla_up(cq, ckv, kr, sm["mla_q_norm"], sm["mla_kv_norm"], w_uq_p, w_ukv_p, tabs, S)
    mo, lse = _flash_fwd(qp, kp, vt, S)
    ry, ro, rprev = _ret_fwd(rq, rk, rv, rg, sm["ret_gn_w"], S)
    mix, h1, hn = _outproj(ro, mo, x, w["w_o"], sm["post_mix_norm"], sm["pre_ffn_norm"], S)
    gate, up, act = _ffn_up(hn, w["w_gate"], w["w_up"], S)
    ff, h2 = _ffn_down(act, w["w_down"], h1, sm["post_ffn_norm"], S)
    dz, dpe, dh2, h2b, loss_vec, d_ple_norm, d_b = _ple_loss(
        p, h2, tgt, w["w_ple_proj"], w["w_ple_gate"], sm["b_ple_gate"], sm["ple_norm"], S)

    gw = {}
    gs = {"ple_norm": d_ple_norm, "b_ple_gate": d_b}
    gw["w_ple_gate"] = _wgrad(h2b, dz, "wgrad_ple_gate", S)
    gw["w_ple_proj"] = _wgrad(p, dpe, "wgrad_ple_proj", S)
    dff, dgate, dup, gs["post_ffn_norm"] = _ffn_down_bwd(dh2, ff, sm["post_ffn_norm"], w["w_down"], gate, up, S)
    gw["w_down"] = _wgrad(act, dff, "wgrad_down", S)
    gw["w_gate"] = _wgrad(hn, dgate, "wgrad_gate", S)
    gw["w_up"] = _wgrad(hn, dup, "wgrad_up", S)
    dh1, dmix, dro, dmo, gs["pre_ffn_norm"], gs["post_mix_norm"] = _ffn_up_bwd(
        dgate, dup, w["w_gate"], w["w_up"], h1, mix, dh2, sm["pre_ffn_norm"], sm["post_mix_norm"], w["w_o"], S)
    gw["w_o"] = jnp.concatenate([_wgrad(ro, dmix, "wgrad_o_ret", S), _wgrad(mo, dmix, "wgrad_o_mla", S)], axis=0)

    dmo_t, delta = _attn_delta(mo, dmo, S)
    dqp, dkp, dv = _flash_bwd(qp, kp, kt, v, dmo, dmo_t, lse, delta, S)
    dqh, dkv, dcq, dckv, dkr, gs["mla_q_norm"], gs["mla_kv_norm"] = _mla_up_bwd(
        dqp, dkp, dv, cq, ckv, sm["mla_q_norm"], sm["mla_kv_norm"], w_uq_p, w_ukv_p, tabs, S)
    g_uq_p = _wgrad(cqn, dqh, "wgrad_uq", S)
    g_ukv_p = _wgrad(ckvn, dkv, "wgrad_ukv", S)
    gw["w_uq"] = g_uq_p.reshape(Q_LORA, MLA_HEADS, 128)[:, :, :96].reshape(Q_LORA, 768)
    gw["w_ukv"] = jnp.concatenate(
        [g_ukv_p[:, :1024].reshape(KV_LORA, MLA_HEADS, 128)[:, :, :64], g_ukv_p[:, 1024:].reshape(KV_LORA, MLA_HEADS, 64)],
        axis=2).reshape(KV_LORA, 1024)

    drq, drk, drv, drg, gs["ret_gn_w"] = _ret_bwd(rq, rk, rv, rprev, ry, rg, dro, sm["ret_gn_w"], tabs, S)
    grad_x, dproj, gs["pre_mix_norm"] = _inproj_bwd(drq, drk, drv, drg, dcq, dckv, dkr, w_in_p, dh1, x,
                                                    sm["pre_mix_norm"], S)
    g_in_p = _wgrad(xn, dproj, "wgrad_in", S)
    gw["w_in"] = jnp.concatenate([g_in_p[:, :2688], g_in_p[:, 2752:2784]], axis=1)
    return loss_vec, grad_x, gw, gs


def _my_place():
    x = lax.axis_index("x")
    y = lax.axis_index("y")
    c = lax.axis_index("c")
    return x, y, c


def _other_chips(x, y):
    return [(1 - x, y), (x, 1 - y), (1 - x, 1 - y)]


_ANY = pl.BlockSpec(memory_space=pl.ANY)


def _allgather_weights(wpk):
    H = HALF_ROWS

    def body(w_ref, out_ref, send1, recv1, send2, recv2, lsem):
        x, y, c = _my_place()
        me = 2 * x + y
        chips = _other_chips(x, y)
        half = pl.ds(pl.multiple_of(c * H, 32), H)
        other = pl.ds(pl.multiple_of((1 - c) * H, 32), H)
        mine = pltpu.make_async_copy(w_ref, out_ref.at[me], lsem)
        mine.start()

        def over_ici(k, src_chip, to):
            return pltpu.make_async_remote_copy(
                src_ref=w_ref.at[half], dst_ref=out_ref.at[src_chip, half], send_sem=send1.at[k], recv_sem=recv1.at[k],
                device_id=to, device_id_type=MESH)

        def to_sibling(k, chip, rows):
            return pltpu.make_async_remote_copy(
                src_ref=out_ref.at[chip, rows], dst_ref=out_ref.at[chip, rows], send_sem=send2.at[k],
                recv_sem=recv2.at[k], device_id=(x, y, 1 - c), device_id_type=MESH)

        first = [over_ici(k, me, (cx, cy, c)) for k, (cx, cy) in enumerate(chips)]
        for cp in first:
            cp.start()
        passed = []
        for k, (cx, cy) in enumerate(chips):
            over_ici(k, 2 * cx + cy, (cx, cy, c)).wait_recv()
            fwd = to_sibling(k, 2 * cx + cy, half)
            fwd.start()
            passed.append(fwd)
        for k, (cx, cy) in enumerate(chips):
            to_sibling(k, 2 * cx + cy, other).wait_recv()
        for cp in first + passed:
            cp.wait_send()
        mine.wait()

    return pl.pallas_call(
        body, name="allgather_weights",
        in_specs=[_ANY], out_specs=_ANY, out_shape=_sds((N_CHIPS, PACK_ROWS, PACK_COLS), BF16),
        scratch_shapes=[pltpu.SemaphoreType.DMA((3,)), pltpu.SemaphoreType.DMA((3,)), pltpu.SemaphoreType.DMA((3,)),
                        pltpu.SemaphoreType.DMA((3,)), pltpu.SemaphoreType.DMA],
    )(wpk)


def _swap_halves(gpk):
    H = HALF_ROWS

    def body(g_ref, out_ref, send, recv):
        x, y, c = _my_place()
        other = pl.ds(pl.multiple_of((1 - c) * H, 8), H)
        cp = pltpu.make_async_remote_copy(
            src_ref=g_ref.at[:, other], dst_ref=out_ref, send_sem=send, recv_sem=recv,
            device_id=(x, y, 1 - c), device_id_type=MESH)
        cp.start()
        cp.wait()

    return pl.pallas_call(
        body, name="rs_swap_halves",
        in_specs=[_ANY], out_specs=_ANY, out_shape=_sds((N_CHIPS, HALF_ROWS, PACK_COLS), F32),
        scratch_shapes=[pltpu.SemaphoreType.DMA, pltpu.SemaphoreType.DMA],
    )(gpk)


def _add_halves(gpk, got, c_idx):
    tr = 440
    nb = HALF_ROWS // tr

    def body(c_ref, a_ref, b_ref, o_ref):
        o_ref[...] = a_ref[...] + b_ref[...]

    grid_spec = pltpu.PrefetchScalarGridSpec(
        num_scalar_prefetch=1, grid=(N_CHIPS, nb),
        in_specs=[pl.BlockSpec((None, tr, PACK_COLS), lambda j, i, c: (j, c[0] * nb + i, 0)),
                  pl.BlockSpec((None, tr, PACK_COLS), lambda j, i, c: (j, i, 0))],
        out_specs=pl.BlockSpec((None, tr, PACK_COLS), lambda j, i, c: (j, i, 0)),
    )
    return pl.pallas_call(
        body, name="rs_add_halves", grid_spec=grid_spec, out_shape=_sds((N_CHIPS, HALF_ROWS, PACK_COLS), F32),
        compiler_params=_cp(("parallel", "parallel")),
    )(c_idx, gpk, got)


def _scatter_chips(tsum):
    def body(t_ref, out_ref, send, recv, lsem):
        x, y, c = _my_place()
        me = 2 * x + y
        chips = _other_chips(x, y)
        mine = pltpu.make_async_copy(t_ref.at[me], out_ref.at[me], lsem)
        mine.start()
        cps = [pltpu.make_async_remote_copy(
            src_ref=t_ref.at[2 * cx + cy], dst_ref=out_ref.at[me], send_sem=send.at[k], recv_sem=recv.at[k],
            device_id=(cx, cy, c), device_id_type=MESH) for k, (cx, cy) in enumerate(chips)]
        for cp in cps:
            cp.start()
        for cp in cps:
            cp.wait()
        mine.wait()

    return pl.pallas_call(
        body, name="rs_scatter_chips",
        in_specs=[_ANY], out_specs=_ANY, out_shape=_sds((N_CHIPS, HALF_ROWS, PACK_COLS), F32),
        scratch_shapes=[pltpu.SemaphoreType.DMA((3,)), pltpu.SemaphoreType.DMA((3,)), pltpu.SemaphoreType.DMA],
    )(tsum)


def _add_chips(parts):
    tr = 440

    def body(p_ref, o_ref):
        o_ref[...] = ((p_ref[0] + p_ref[1]) + p_ref[2]) + p_ref[3]

    return pl.pallas_call(
        body, name="rs_add_chips", grid=(HALF_ROWS // tr,),
        in_specs=[pl.BlockSpec((N_CHIPS, tr, PACK_COLS), lambda i: (0, i, 0))],
        out_specs=pl.BlockSpec((tr, PACK_COLS), lambda i: (i, 0)), out_shape=_sds((HALF_ROWS, PACK_COLS), F32),
        compiler_params=_cp(("parallel",)),
    )(parts)


def _join_halves(red):
    H = HALF_ROWS

    def body(r_ref, out_ref, send, recv, lsem):
        x, y, c = _my_place()
        half = pl.ds(pl.multiple_of(c * H, 8), H)
        mine = pltpu.make_async_copy(r_ref, out_ref.at[half], lsem)
        mine.start()
        cp = pltpu.make_async_remote_copy(
            src_ref=r_ref, dst_ref=out_ref.at[half], send_sem=send, recv_sem=recv,
            device_id=(x, y, 1 - c), device_id_type=MESH)
        cp.start()
        cp.wait()
        mine.wait()

    return pl.pallas_call(
        body, name="rs_join_halves",
        in_specs=[_ANY], out_specs=_ANY, out_shape=_sds((PACK_ROWS, PACK_COLS), F32),
        scratch_shapes=[pltpu.SemaphoreType.DMA, pltpu.SemaphoreType.DMA, pltpu.SemaphoreType.DMA],
    )(red)


def _allreduce_small(vec):
    def body(v_ref, out_ref, slots, send, recv, lsem):
        x, y, c = _my_place()
        me = 4 * x + 2 * y + c
        mine = pltpu.make_async_copy(v_ref, slots.at[me], lsem)
        mine.start()
        cps = []
        for r in range(1, N_DEV):
            px = x ^ (r >> 2)
            py = y ^ ((r >> 1) & 1)
            pc = c ^ (r & 1)
            cps.append(pltpu.make_async_remote_copy(
                src_ref=v_ref, dst_ref=slots.at[me], send_sem=send.at[r - 1], recv_sem=recv.at[r - 1],
                device_id=(px, py, pc), device_id_type=MESH))
        for cp in cps:
            cp.start()
        for cp in cps:
            cp.wait()
        mine.wait()
        acc = slots[0]
        for d in range(1, N_DEV):
            acc = acc + slots[d]
        out_ref[...] = acc
        loss = jnp.sum(acc[9:10, :], axis=1, keepdims=True) * (0.5 / D_MODEL)
        out_ref[9:10, :] = jnp.broadcast_to(loss, (1, PACK_COLS))

    vm = pl.BlockSpec(memory_space=pltpu.VMEM)
    return pl.pallas_call(
        body, name="allreduce_small",
        in_specs=[vm], out_specs=vm, out_shape=_sds((SMALL_ROWS, PACK_COLS), F32),
        scratch_shapes=[pltpu.VMEM((N_DEV, SMALL_ROWS, PACK_COLS), F32), pltpu.SemaphoreType.DMA((N_DEV - 1,)),
                        pltpu.SemaphoreType.DMA((N_DEV - 1,)), pltpu.SemaphoreType.DMA],
    )(vec)


N_BIG = len(BIG)


def _half(c, rows, align):
    h = rows // 2
    return pl.ds(pl.multiple_of(c * h, align), h)


def _gather_shards(shards):
    n = len(shards)

    def body(*refs):
        ins, outs = refs[:n], refs[n:2 * n]
        send1, recv1, send2, recv2, send3, recv3 = refs[2 * n:]
        x, y, c = _my_place()
        me = 2 * x + y
        chips = _other_chips(x, y)
        sib = (x, y, 1 - c)
        local, first, passed = [], [], []
        for t in range(n):
            rows = ins[t].shape[0]
            half = _half(c, rows, 16)
            cp = pltpu.make_async_remote_copy(
                src_ref=ins[t], dst_ref=outs[t].at[me], send_sem=send3.at[t], recv_sem=recv3.at[t],
                device_id=sib, device_id_type=MESH)
            cp.start()
            local.append(cp)
            for k, (cx, cy) in enumerate(chips):
                rc = pltpu.make_async_remote_copy(
                    src_ref=ins[t].at[half], dst_ref=outs[t].at[me, half], send_sem=send1.at[t, k],
                    recv_sem=recv1.at[t, k], device_id=(cx, cy, c), device_id_type=MESH)
                rc.start()
                first.append(rc)
        for k, (cx, cy) in enumerate(chips):
            src = 2 * cx + cy
            for t in range(n):
                half = _half(c, ins[t].shape[0], 16)
                pltpu.make_async_remote_copy(
                    src_ref=ins[t].at[half], dst_ref=outs[t].at[src, half], send_sem=send1.at[t, k],
                    recv_sem=recv1.at[t, k], device_id=(cx, cy, c), device_id_type=MESH).wait_recv()
                fw = pltpu.make_async_remote_copy(
                    src_ref=outs[t].at[src, half], dst_ref=outs[t].at[src, half], send_sem=send2.at[t, k],
                    recv_sem=recv2.at[t, k], device_id=sib, device_id_type=MESH)
                fw.start()
                passed.append(fw)
        for k, (cx, cy) in enumerate(chips):
            src = 2 * cx + cy
            for t in range(n):
                other = _half(1 - c, ins[t].shape[0], 16)
                pltpu.make_async_remote_copy(
                    src_ref=outs[t].at[src, other], dst_ref=outs[t].at[src, other], send_sem=send2.at[t, k],
                    recv_sem=recv2.at[t, k], device_id=sib, device_id_type=MESH).wait_recv()
        for cp in first + passed:
            cp.wait_send()
        for cp in local:
            cp.wait()

    return pl.pallas_call(
        body, name="gather_weights",
        in_specs=[_ANY] * n, out_specs=[_ANY] * n,
        out_shape=[_sds((N_CHIPS,) + s.shape, BF16) for s in shards],
        scratch_shapes=[pltpu.SemaphoreType.DMA((n, 3))] * 4 + [pltpu.SemaphoreType.DMA((n,))] * 2,
    )(*shards)


def _swap_half_rows(gs):
    n = len(gs)

    def body(*refs):
        ins, outs = refs[:n], refs[n:2 * n]
        send, recv = refs[2 * n:]
        x, y, c = _my_place()
        cps = []
        for t in range(n):
            other = _half(1 - c, ins[t].shape[1], 8)
            cp = pltpu.make_async_remote_copy(
                src_ref=ins[t].at[:, other], dst_ref=outs[t], send_sem=send.at[t], recv_sem=recv.at[t],
                device_id=(x, y, 1 - c), device_id_type=MESH)
            cp.start()
            cps.append(cp)
        for cp in cps:
            cp.wait()

    return pl.pallas_call(
        body, name="rs_swap_halves",
        in_specs=[_ANY] * n, out_specs=[_ANY] * n,
        out_shape=[_sds((N_CHIPS, g.shape[1] // 2, g.shape[2]), F32) for g in gs],
        scratch_shapes=[pltpu.SemaphoreType.DMA((n,)), pltpu.SemaphoreType.DMA((n,))],
    )(*gs)


def _add_half_rows(g, got, c_idx, name):
    _, rows, cols = g.shape
    h = rows // 2

    def body(c_ref, a_ref, b_ref, o_ref):
        o_ref[...] = (a_ref[...] + b_ref[...]).astype(BF16)

    grid_spec = pltpu.PrefetchScalarGridSpec(
        num_scalar_prefetch=1, grid=(N_CHIPS,),
        in_specs=[pl.BlockSpec((None, h, cols), lambda j, c: (j, c[0], 0)),
                  pl.BlockSpec((None, h, cols), lambda j, c: (j, 0, 0))],
        out_specs=pl.BlockSpec((None, h, cols), lambda j, c: (j, 0, 0)),
    )
    return pl.pallas_call(
        body, name=name, grid_spec=grid_spec, out_shape=_sds((N_CHIPS, h, cols), BF16),
        compiler_params=_cp(("parallel",)),
    )(c_idx, g, got)


def _scatter_to_chips(ts):
    n = len(ts)

    def body(*refs):
        ins, outs = refs[:n], refs[n:2 * n]
        send, recv = refs[2 * n:]
        x, y, c = _my_place()
        chips = _other_chips(x, y)
        cps = []
        for t in range(n):
            for k, (cx, cy) in enumerate(chips):
                rc = pltpu.make_async_remote_copy(
                    src_ref=ins[t].at[2 * cx + cy], dst_ref=outs[t].at[k], send_sem=send.at[t, k],
                    recv_sem=recv.at[t, k], device_id=(cx, cy, c), device_id_type=MESH)
                rc.start()
                cps.append(rc)
        for cp in cps:
            cp.wait()

    return pl.pallas_call(
        body, name="rs_scatter_chips",
        in_specs=[_ANY] * n, out_specs=[_ANY] * n, out_shape=[_sds((3,) + t.shape[1:], BF16) for t in ts],
        scratch_shapes=[pltpu.SemaphoreType.DMA((n, 3)), pltpu.SemaphoreType.DMA((n, 3))],
    )(*ts)


def _add_four(mine, parts, place, name):
    _, h, cols = parts.shape

    def body(pl_ref, m_ref, p_ref, o_ref):
        o_ref[...] = ((m_ref[...].astype(F32) + p_ref[0].astype(F32)) + p_ref[1].astype(F32)) + p_ref[2].astype(F32)

    grid_spec = pltpu.PrefetchScalarGridSpec(
        num_scalar_prefetch=1, grid=(1,),
        in_specs=[pl.BlockSpec((None, h, cols), lambda i, pc: (pc[0], 0, 0)),
                  pl.BlockSpec((3, h, cols), lambda i, pc: (0, 0, 0))],
        out_specs=pl.BlockSpec((h, cols), lambda i, pc: (pc[1], 0)),
    )
    return pl.pallas_call(
        body, name=name, grid_spec=grid_spec, out_shape=_sds((2 * h, cols), F32),
        compiler_params=_cp(("arbitrary",)),
    )(place, mine, parts)


def _join_half_rows(rs):
    n = len(rs)

    def body(*refs):
        ins, outs = refs[:n], refs[n:2 * n]
        send, recv = refs[2 * n:]
        x, y, c = _my_place()
        cps = []
        for t in range(n):
            half = _half(c, outs[t].shape[0], 8)
            rc = pltpu.make_async_remote_copy(
                src_ref=ins[t].at[half], dst_ref=outs[t].at[half], send_sem=send.at[t], recv_sem=recv.at[t],
                device_id=(x, y, 1 - c), device_id_type=MESH)
            rc.start()
            cps.append(rc)
        for cp in cps:
            cp.wait()

    return pl.pallas_call(
        body, name="rs_join_halves",
        in_specs=[_ANY] * n, out_specs=[_ANY] * n,
        out_shape=[_sds(r.shape, F32) for r in rs],
        input_output_aliases={i: i for i in range(n)},
        scratch_shapes=[pltpu.SemaphoreType.DMA((n,))] * 2,
    )(*rs)


def _by_chip(full, rows, cols, axis):
    if axis == 0:
        return full.reshape(N_CHIPS, rows // N_CHIPS, cols)
    return full.reshape(rows, N_CHIPS, cols // N_CHIPS).transpose(1, 0, 2)


def _from_chips(parts, axis):
    _, r, c = parts.shape
    if axis == 0:
        return parts.reshape(N_CHIPS * r, c)
    return parts.transpose(1, 0, 2).reshape(r, N_CHIPS * c)


def _adamw(wt, g, m, v, name):
    R, C = wt.shape
    tr = R
    for cand in (256, 128, 64, 32, 16, 8):
        if R % cand == 0:
            tr = cand
            break

    def body(w_ref, g_ref, m_ref, v_ref, d_ref, nm_ref, nv_ref):
        gg = g_ref[...]
        m_new = ADAM_B1 * m_ref[...] + (1.0 - ADAM_B1) * gg
        v_new = ADAM_B2 * v_ref[...] + (1.0 - ADAM_B2) * (gg * gg)
        m_hat = m_new / (1.0 - ADAM_B1 ** ADAM_STEP)
        v_hat = v_new / (1.0 - ADAM_B2 ** ADAM_STEP)
        d_ref[...] = -ADAM_LR * (m_hat / (jnp.sqrt(v_hat) + ADAM_EPS) + ADAM_WD * w_ref[...])
        nm_ref[...] = m_new
        nv_ref[...] = v_new

    spec = pl.BlockSpec((tr, C), lambda i: (i, 0))
    return pl.pallas_call(
        body, name=name, grid=(R // tr,), in_specs=[spec] * 4, out_specs=[spec] * 3, out_shape=[_sds((R, C), F32)] * 3,
        compiler_params=_cp(("parallel",)),
    )(wt, g, m, v)


def _shard_shape(r, c, axis):
    return (r // N_CHIPS, c) if axis == 0 else (r, c // N_CHIPS)


def _pack_rows(flat):
    return jnp.pad(flat, (0, PACK_ROWS * PACK_COLS - flat.shape[0])).reshape(PACK_ROWS, PACK_COLS)


def _pack_shards(mats):
    return _pack_rows(jnp.concatenate([mats[n].reshape(-1) for n, _, _, _ in BIG]))


def _unpack_shards(pk):
    flat = pk.reshape(-1)
    out = {}
    off = 0
    for n, r, c, ax in BIG:
        shp = _shard_shape(r, c, ax)
        sz = shp[0] * shp[1]
        out[n] = flat[off:off + sz].reshape(shp)
        off += sz
    return out


def _full_from_packs(allpk):
    per_chip = [_unpack_shards(allpk[j]) for j in range(N_CHIPS)]
    return {n: jnp.concatenate([per_chip[j][n] for j in range(N_CHIPS)], axis=ax) for n, _, _, ax in BIG}


def _packs_from_full(gw):
    packs = []
    for j in range(N_CHIPS):
        shards = {}
        for n, r, c, ax in BIG:
            shp = _shard_shape(r, c, ax)
            shards[n] = gw[n][j * shp[0]:(j + 1) * shp[0], :] if ax == 0 else gw[n][:, j * shp[1]:(j + 1) * shp[1]]
        packs.append(_pack_shards(shards))
    return jnp.stack(packs)


def _pack_small(vals, loss_vec=None):
    rows = [jnp.pad(vals[n].reshape(-1), (0, PACK_COLS - sz)) for n, sz in SMALL]
    rows.append(loss_vec.reshape(-1) if loss_vec is not None else jnp.zeros((PACK_COLS,), F32))
    rows += [jnp.zeros((PACK_COLS,), F32)] * (SMALL_ROWS - len(rows))
    return jnp.stack(rows)


def kernel(x, p, positions, pre_mix_norm, w_in, ret_gn_w, mla_q_norm, w_uq, mla_kv_norm, w_ukv, w_o, post_mix_norm, pre_ffn_norm, w_gate, w_up, w_down, post_ffn_norm, w_ple_proj, ple_norm, w_ple_gate, b_ple_gate, loss_target, m_pre_mix_norm, m_w_in, m_ret_gn_w, m_mla_q_norm, m_w_uq, m_mla_kv_norm, m_w_ukv, m_w_o, m_post_mix_norm, m_pre_ffn_norm, m_w_gate, m_w_up, m_w_down, m_post_ffn_norm, m_w_ple_proj, m_ple_norm, m_w_ple_gate, m_b_ple_gate, v_pre_mix_norm, v_w_in, v_ret_gn_w, v_mla_q_norm, v_w_uq, v_mla_kv_norm, v_w_ukv, v_w_o, v_post_mix_norm, v_pre_ffn_norm, v_w_gate, v_w_up, v_w_down, v_post_ffn_norm, v_w_ple_proj, v_ple_norm, v_w_ple_gate, v_b_ple_gate):
    wts = dict(pre_mix_norm=pre_mix_norm, w_in=w_in, ret_gn_w=ret_gn_w, mla_q_norm=mla_q_norm, w_uq=w_uq,
               mla_kv_norm=mla_kv_norm, w_ukv=w_ukv, w_o=w_o, post_mix_norm=post_mix_norm, pre_ffn_norm=pre_ffn_norm,
               w_gate=w_gate, w_up=w_up, w_down=w_down, post_ffn_norm=post_ffn_norm, w_ple_proj=w_ple_proj,
               ple_norm=ple_norm, w_ple_gate=w_ple_gate, b_ple_gate=b_ple_gate)
    mom = dict(pre_mix_norm=m_pre_mix_norm, w_in=m_w_in, ret_gn_w=m_ret_gn_w, mla_q_norm=m_mla_q_norm, w_uq=m_w_uq,
               mla_kv_norm=m_mla_kv_norm, w_ukv=m_w_ukv, w_o=m_w_o, post_mix_norm=m_post_mix_norm,
               pre_ffn_norm=m_pre_ffn_norm, w_gate=m_w_gate, w_up=m_w_up, w_down=m_w_down, post_ffn_norm=m_post_ffn_norm,
               w_ple_proj=m_w_ple_proj, ple_norm=m_ple_norm, w_ple_gate=m_w_ple_gate, b_ple_gate=m_b_ple_gate)
    var = dict(pre_mix_norm=v_pre_mix_norm, w_in=v_w_in, ret_gn_w=v_ret_gn_w, mla_q_norm=v_mla_q_norm, w_uq=v_w_uq,
               mla_kv_norm=v_mla_kv_norm, w_ukv=v_w_ukv, w_o=v_w_o, post_mix_norm=v_post_mix_norm,
               pre_ffn_norm=v_pre_ffn_norm, w_gate=v_w_gate, w_up=v_w_up, w_down=v_w_down, post_ffn_norm=v_post_ffn_norm,
               w_ple_proj=v_w_ple_proj, ple_norm=v_ple_norm, w_ple_gate=v_w_ple_gate, b_ple_gate=v_b_ple_gate)

    S = x.shape[1]
    shard2d = {n: wts[n][0] for n, _, _, _ in BIG}
    small2d = {n: wts[n] for n, _ in SMALL}

    gathered = _gather_shards([shard2d[n].astype(BF16) for n, _, _, _ in BIG])
    w_full = {n: _from_chips(gathered[i], ax) for i, (n, _, _, ax) in enumerate(BIG)}

    pos_f = positions.astype(F32).reshape(S, 1)
    loss_vec, grad_x, gw, gs = _local_step(x[0], p[0, 0], pos_f, loss_target[0], w_full, small2d)

    g4 = [_by_chip(gw[n], r, c, ax) for n, r, c, ax in BIG]
    c_idx = lax.axis_index("c").astype(jnp.int32).reshape(1)
    got = _swap_half_rows(g4)
    chip_sum = [_add_half_rows(g4[i], got[i], c_idx, "rs_add_halves_" + BIG[i][0]) for i in range(N_BIG)]
    parts = _scatter_to_chips(chip_sum)
    place = jnp.stack([2 * lax.axis_index("x") + lax.axis_index("y"), lax.axis_index("c")]).astype(jnp.int32)
    reduced = _join_half_rows(
        [_add_four(chip_sum[i], parts[i], place, "rs_add_chips_" + BIG[i][0]) for i in range(N_BIG)])
    g_shard = {n: reduced[i] for i, (n, _, _, _) in enumerate(BIG)}

    small_sum = _allreduce_small(_pack_small(gs, loss_vec))
    loss = small_sum[9, 0]
    g_small = {n: small_sum[i:i + 1, :sz] for i, (n, sz) in enumerate(SMALL)}

    grads, delta, new_m, new_v = {}, {}, {}, {}
    for n, _, _, _ in BIG:
        d, nm, nv = _adamw(shard2d[n], g_shard[n], mom[n][0], var[n][0], "adamw_" + n)
        grads[n], delta[n], new_m[n], new_v[n] = g_shard[n][None], d[None], nm[None], nv[None]
    d, nm, nv = _adamw(_pack_small(small2d), small_sum, _pack_small(mom), _pack_small(var), "adamw_small")
    for i, (n, sz) in enumerate(SMALL):
        grads[n] = g_small[n]
        delta[n], new_m[n], new_v[n] = d[i:i + 1, :sz], nm[i:i + 1, :sz], nv[i:i + 1, :sz]

    return (loss, grad_x[None], *[grads[n] for n in ALL_W], *[delta[n] for n in ALL_W],
            *[new_m[n] for n in ALL_W], *[new_v[n] for n in ALL_W])
```

```python
import functools
import math

import jax
import jax.numpy as jnp
import numpy as np
from jax import lax
from jax.experimental import pallas as pl
from jax.experimental.pallas import tpu as pltpu

F32 = jnp.float32
BF16 = jnp.bfloat16
MESH = pl.DeviceIdType.MESH

D_MODEL = 1024
D_FF = 2816
PLE_DIM = 256
RET_HEADS = 4
RET_DIM = 128
RET_WIDTH = 512
RET_CHUNK = 128
RET_GROUP = 8
MLA_HEADS = 8
MLA_NOPE = 64
MLA_ROPE = 32
MLA_V = 64
Q_LORA = 384
KV_LORA = 256
IN_COLS = 2720
IN_COLS_P = 2816
ROPE_BASE = 10000.0
EPS = 1e-6
SCALE_MLA = 1.0 / math.sqrt(MLA_NOPE + MLA_ROPE)
SCALE_RET = RET_DIM ** -0.5
NEG = -1e30

ADAM_LR = 0.001
ADAM_B1 = 0.9
ADAM_B2 = 0.999
ADAM_EPS = 1e-08
ADAM_WD = 0.01
ADAM_STEP = 10

N_CHIPS = 4
N_DEV = 8
VMEM_MB = 56

BIG = (
    ("w_in", 1024, 2720, 1),
    ("w_uq", 384, 768, 1),
    ("w_ukv", 256, 1024, 1),
    ("w_o", 1024, 1024, 0),
    ("w_gate", 1024, 2816, 1),
    ("w_up", 1024, 2816, 1),
    ("w_down", 2816, 1024, 0),
    ("w_ple_proj", 256, 1024, 1),
    ("w_ple_gate", 1024, 1024, 0),
)
SMALL = (
    ("pre_mix_norm", 1024),
    ("ret_gn_w", 512),
    ("mla_q_norm", 384),
    ("mla_kv_norm", 256),
    ("post_mix_norm", 1024),
    ("pre_ffn_norm", 1024),
    ("post_ffn_norm", 1024),
    ("ple_norm", 1024),
    ("b_ple_gate", 1024),
)
ALL_W = ("pre_mix_norm", "w_in", "ret_gn_w", "mla_q_norm", "w_uq", "mla_kv_norm", "w_ukv", "w_o", "post_mix_norm",
         "pre_ffn_norm", "w_gate", "w_up", "w_down", "post_ffn_norm", "w_ple_proj", "ple_norm", "w_ple_gate", "b_ple_gate")
PACK_COLS = 1024
SHARD_ELEMS = sum(r * c for _, r, c, _ in BIG) // N_CHIPS
PACK_ROWS = -(-SHARD_ELEMS // PACK_COLS // 32) * 32
HALF_ROWS = PACK_ROWS // 2
SMALL_ROWS = 16


def _cp(sem=None, mb=VMEM_MB, **kw):
    return pltpu.CompilerParams(dimension_semantics=sem, vmem_limit_bytes=mb * 1024 * 1024, **kw)


def _bf(x):
    return x.astype(BF16)


def _dot(a, b):
    return jnp.dot(_bf(a), _bf(b), preferred_element_type=F32)


def _dot_nt(a, b):
    return lax.dot_general(_bf(a), _bf(b), (((1,), (1,)), ((), ())), preferred_element_type=F32)


def _dot_tn(a, b):
    return lax.dot_general(_bf(a), _bf(b), (((0,), (0,)), ((), ())), preferred_element_type=F32)


def _sig(x):
    return 1.0 / (1.0 + jnp.exp(-x))


def _rms(x, g):
    r = lax.rsqrt(jnp.mean(x * x, axis=-1, keepdims=True) + EPS)
    return x * r * g


def _rms_bwd(dy, x, g):
    r = lax.rsqrt(jnp.mean(x * x, axis=-1, keepdims=True) + EPS)
    xh = x * r
    dxh = dy * g
    dx = r * (dxh - xh * jnp.mean(dxh * xh, axis=-1, keepdims=True))
    return dx, dy * xh


def _colsum(x):
    return jnp.sum(x, axis=0, keepdims=True)


def _rope_ret(x, cr, sr):
    return x * cr + pltpu.roll(x, 64, 1) * sr


def _unrope_ret(dy, cr, sr):
    return dy * cr + pltpu.roll(dy * sr, 64, 1)


def _rope_mla(x, cm, sa, sb):
    return x * cm + pltpu.roll(x, 112, 1) * sa + pltpu.roll(x, 16, 1) * sb


def _unrope_mla(dy, cm, sa, sb):
    return dy * cm + pltpu.roll(dy * sa, 16, 1) + pltpu.roll(dy * sb, 112, 1)


def _rows(tm, w, col=0):
    return pl.BlockSpec((tm, w), lambda i: (i, col))


def _full(*shape):
    return pl.BlockSpec(shape, lambda i: (0,) * len(shape))


def _sds(shape, dtype):
    return jax.ShapeDtypeStruct(shape, dtype)


def _rope_tables(pos_f, S):
    tm = min(512, S)
    inv_r = (1.0 / (np.float32(ROPE_BASE) ** (np.arange(64, dtype=np.float32) / np.float32(64)))).astype(np.float32)
    inv_m16 = (1.0 / (np.float32(ROPE_BASE) ** (np.arange(16, dtype=np.float32) / np.float32(16)))).astype(np.float32)
    inv_r = np.concatenate([inv_r, inv_r])[None, :]
    inv_m = np.zeros((1, 128), np.float32)
    inv_m[0, 64:80] = inv_m16
    inv_m[0, 80:96] = inv_m16

    def body(pos_ref, invr_ref, invm_ref, cr_ref, sr_ref, cm_ref, sa_ref, sb_ref):
        pos = pos_ref[...]
        lane = lax.broadcasted_iota(jnp.int32, (tm, 128), 1)
        ar = pos * invr_ref[...]
        s = jnp.sin(ar)
        cr_ref[...] = jnp.cos(ar)
        sr_ref[...] = jnp.where(lane < 64, -s, s)
        am = pos * invm_ref[...]
        c2 = jnp.cos(am)
        s2 = jnp.sin(am)
        cm_ref[...] = jnp.where(lane < 64, 1.0, jnp.where(lane < 96, c2, 0.0))
        sa_ref[...] = jnp.where((lane >= 64) & (lane < 80), -s2, 0.0)
        sb_ref[...] = jnp.where((lane >= 80) & (lane < 96), s2, 0.0)

    return pl.pallas_call(
        body, name="rope_tables", grid=(S // tm,),
        in_specs=[_rows(tm, 1), _full(1, 128), _full(1, 128)],
        out_specs=[_rows(tm, 128)] * 5,
        out_shape=[_sds((S, 128), F32)] * 5,
        compiler_params=_cp(("parallel",)),
    )(pos_f, jnp.asarray(inv_r), jnp.asarray(inv_m))


def _rms_fwd(x, g, S):
    tm = min(512, S)

    def body(x_ref, g_ref, o_ref):
        o_ref[...] = _rms(x_ref[...], g_ref[...]).astype(BF16)

    return pl.pallas_call(
        body, name="rms_pre", grid=(S // tm,),
        in_specs=[_rows(tm, D_MODEL), _full(1, D_MODEL)],
        out_specs=_rows(tm, D_MODEL), out_shape=_sds((S, D_MODEL), BF16),
        compiler_params=_cp(("parallel",)),
    )(x, g)


def _inproj(xn, w_in, tabs, S):
    tm = min(256, S)

    def body(xn_ref, w_ref, cr_ref, sr_ref, cm_ref, sa_ref, sb_ref, rq_ref, rk_ref, rv_ref, rg_ref, cq_ref, ckv_ref, kr_ref):
        xb = xn_ref[...]
        cr = cr_ref[...]
        sr = sr_ref[...]
        q = jnp.dot(xb, w_ref[:, 0:512], preferred_element_type=F32)
        k = jnp.dot(xb, w_ref[:, 512:1024], preferred_element_type=F32)
        for h in range(RET_HEADS):
            sl = slice(h * 128, (h + 1) * 128)
            rq_ref[:, sl] = _rope_ret(q[:, sl], cr, sr).astype(BF16)
            rk_ref[:, sl] = (_rope_ret(k[:, sl], cr, sr) * SCALE_RET).astype(BF16)
        rv_ref[...] = jnp.dot(xb, w_ref[:, 1024:1536], preferred_element_type=F32).astype(BF16)
        rg_ref[...] = jnp.dot(xb, w_ref[:, 1536:2048], preferred_element_type=F32)
        cq_ref[...] = jnp.dot(xb, w_ref[:, 2048:2432], preferred_element_type=F32)
        ckv_ref[...] = jnp.dot(xb, w_ref[:, 2432:2688], preferred_element_type=F32)
        kr = jnp.dot(xb, w_ref[:, 2688:2816], preferred_element_type=F32)
        kr_ref[...] = _rope_mla(kr, cm_ref[...], sa_ref[...], sb_ref[...])

    return pl.pallas_call(
        body, name="inproj", grid=(S // tm,),
        in_specs=[_rows(tm, D_MODEL), _full(D_MODEL, IN_COLS_P)] + [_rows(tm, 128)] * 5,
        out_specs=[_rows(tm, 512)] * 4 + [_rows(tm, Q_LORA), _rows(tm, KV_LORA), _rows(tm, 128)],
        out_shape=[_sds((S, 512), BF16)] * 3 + [_sds((S, 512), F32), _sds((S, Q_LORA), F32), _sds((S, KV_LORA), F32),
                                                  _sds((S, 128), F32)],
        compiler_params=_cp(("parallel",)),
    )(xn, w_in, *tabs)


def _mla_up(cq, ckv, kr, gq, gkv, w_uq, w_ukv, tabs, S):
    tm = min(256, S)

    def body(cq_ref, ckv_ref, kr_ref, gq_ref, gkv_ref, wuq_ref, wukv_ref, cm_ref, sa_ref, sb_ref,
             cqn_ref, ckvn_ref, qp_ref, kp_ref, v_ref, kt_ref, vt_ref):
        cm = cm_ref[...]
        sa = sa_ref[...]
        sb = sb_ref[...]
        cqn = _rms(cq_ref[...], gq_ref[...]).astype(BF16)
        cqn_ref[...] = cqn
        ckvn = _rms(ckv_ref[...], gkv_ref[...]).astype(BF16)
        ckvn_ref[...] = ckvn
        qh = jnp.dot(cqn, wuq_ref[...], preferred_element_type=F32)
        kv = jnp.dot(ckvn, wukv_ref[...], preferred_element_type=F32)
        kr_blk = kr_ref[...]
        for h in range(MLA_HEADS):
            sl = slice(h * 128, (h + 1) * 128)
            qp_ref[:, sl] = (_rope_mla(qh[:, sl], cm, sa, sb) * SCALE_MLA).astype(BF16)
            kh = kv[:, sl] + kr_blk
            kp_ref[:, sl] = kh.astype(BF16)
            kt_ref[sl, :] = kh.T.astype(BF16)
        for h in range(MLA_HEADS // 2):
            vh = kv[:, 1024 + h * 128:1024 + (h + 1) * 128]
            v_ref[:, h * 128:(h + 1) * 128] = vh.astype(BF16)
            vt_ref[h * 128:(h + 1) * 128, :] = vh.T.astype(BF16)

    cols = lambda r: pl.BlockSpec((r, tm), lambda i: (0, i))
    return pl.pallas_call(
        body, name="mla_up", grid=(S // tm,),
        in_specs=[_rows(tm, Q_LORA), _rows(tm, KV_LORA), _rows(tm, 128), _full(1, Q_LORA), _full(1, KV_LORA),
                  _full(Q_LORA, 1024), _full(KV_LORA, 1536)] + [_rows(tm, 128)] * 3,
        out_specs=[_rows(tm, Q_LORA), _rows(tm, KV_LORA), _rows(tm, 1024), _rows(tm, 1024), _rows(tm, 512),
                   cols(1024), cols(512)],
        out_shape=[_sds((S, Q_LORA), BF16), _sds((S, KV_LORA), BF16), _sds((S, 1024), BF16), _sds((S, 1024), BF16),
                   _sds((S, 512), BF16), _sds((1024, S), BF16), _sds((512, S), BF16)],
        compiler_params=_cp(("parallel",)),
    )(cq, ckv, kr, gq, gkv, w_uq, w_ukv, *tabs[2:])


def _tri_pairs(nq, k_major):
    if k_major:
        pairs = [(qb, kb) for kb in range(nq) for qb in range(kb, nq)]
    else:
        pairs = [(qb, kb) for qb in range(nq) for kb in range(qb + 1)]
    qb_of = np.array([p[0] for p in pairs], np.int32)
    kb_of = np.array([p[1] for p in pairs], np.int32)
    return jnp.asarray(qb_of), jnp.asarray(kb_of), len(pairs)


ATT_ROWS = 32
FWD_HEADS = 4


def _causal_keep(r0, rows, tq):
    key = r0 + lax.broadcasted_iota(jnp.int32, (rows, tq), 0)
    qry = lax.broadcasted_iota(jnp.int32, (rows, tq), 1)
    return key <= qry


def _flash_fwd(qp, kp, vt, S, shards=()):
    tq = min(512, S)
    nq = S // tq
    RB = ATT_ROWS
    NH = FWD_HEADS
    qb_of, kb_of, T = _tri_pairs(nq, k_major=False)
    n = len(shards)
    steps = (MLA_HEADS // NH) * T

    def body(qb_ref, kb_ref, q_ref, k_ref, vt_ref, *rest):
        w_ins, (o_ref, lse_ref), w_outs = rest[:n], rest[n:n + 2], rest[n + 2:2 * n + 2]
        m_sc, l_sc, acc_sc, s_sc, p_sc = rest[2 * n + 2:2 * n + 7]
        sems = rest[2 * n + 7:]
        t = pl.program_id(1)
        qb = qb_ref[t]
        kb = kb_ref[t]
        lin = pl.program_id(0) * T + t

        if n:
            @pl.when(lin == 0)
            def _():
                _gather_phase(0, w_ins, w_outs, sems)

            @pl.when(lin == steps // 2)
            def _():
                _gather_phase(1, w_ins, w_outs, sems)

        @pl.when(kb == 0)
        def _():
            m_sc[...] = jnp.full(m_sc.shape, NEG, F32)
            l_sc[...] = jnp.zeros(l_sc.shape, F32)
            acc_sc[...] = jnp.zeros(acc_sc.shape, F32)

        def step(masked):
            for a in range(NH):
                sl = slice(a * 128, (a + 1) * 128)
                s_sc[a] = _dot_nt(k_ref[:, sl], q_ref[:, sl])
            m_new, al = [], []
            for a in range(NH):
                mx = [jnp.full((8, tq), NEG, F32) for _ in range(RB // 8)]
                for r in range(0, tq, RB):
                    sc = s_sc[a, r:r + RB, :]
                    if masked:
                        sc = jnp.where(_causal_keep(r, RB, tq), sc, NEG)
                        s_sc[a, r:r + RB, :] = sc
                    for i in range(RB // 8):
                        mx[i] = jnp.maximum(mx[i], sc[i * 8:(i + 1) * 8, :])
                mx8 = jnp.maximum(jnp.maximum(mx[0], mx[1]), jnp.maximum(mx[2], mx[3]))
                m_prev = m_sc[a]
                m_new.append(jnp.maximum(m_prev, jnp.max(mx8, axis=0, keepdims=True)))
                al.append(jnp.exp(m_prev - m_new[a]))
                m_sc[a] = m_new[a]
            for a in range(NH):
                ls = [jnp.zeros((8, tq), F32) for _ in range(RB // 8)]
                for r in range(0, tq, RB):
                    p = jnp.exp(s_sc[a, r:r + RB, :] - m_new[a])
                    for i in range(RB // 8):
                        ls[i] = ls[i] + p[i * 8:(i + 1) * 8, :]
                    p_sc[a, r:r + RB, :] = p.astype(BF16)
                l_sc[a] = al[a] * l_sc[a] + jnp.sum((ls[0] + ls[1]) + (ls[2] + ls[3]), axis=0, keepdims=True)
                pair = slice((a // 2) * 128, (a // 2 + 1) * 128)
                pv = jnp.dot(vt_ref[pair, :], p_sc[a], preferred_element_type=F32)
                rs = slice(a * 64, (a + 1) * 64)
                own = slice((a % 2) * 64, (a % 2 + 1) * 64)
                acc_sc[rs, :] = acc_sc[rs, :] * al[a] + pv[own, :]

        @pl.when(kb < qb)
        def _():
            step(False)

        @pl.when(kb == qb)
        def _():
            step(True)
            for a in range(NH):
                rs = slice(a * 64, (a + 1) * 64)
                acc_sc[rs, :] = acc_sc[rs, :] / l_sc[a]
                lse_ref[a:a + 1, :] = m_sc[a] + jnp.log(l_sc[a])
            o_ref[...] = acc_sc[...].T.astype(BF16)

        if n:
            @pl.when(lin == steps - 1)
            def _():
                _gather_phase(2, w_ins, w_outs, sems)

    grid_spec = pltpu.PrefetchScalarGridSpec(
        num_scalar_prefetch=2, grid=(MLA_HEADS // NH, T),
        in_specs=[pl.BlockSpec((tq, 128 * NH), lambda j, t, qb, kb: (qb[t], j)),
                  pl.BlockSpec((tq, 128 * NH), lambda j, t, qb, kb: (kb[t], j)),
                  pl.BlockSpec((64 * NH, tq), lambda j, t, qb, kb: (j, kb[t]))] + [_ANY] * n,
        out_specs=[pl.BlockSpec((tq, 64 * NH), lambda j, t, qb, kb: (qb[t], j)),
                   pl.BlockSpec((None, NH, tq), lambda j, t, qb, kb: (j, 0, qb[t]))] + [_ANY] * n,
        scratch_shapes=[pltpu.VMEM((NH, 1, tq), F32), pltpu.VMEM((NH, 1, tq), F32), pltpu.VMEM((64 * NH, tq), F32),
                        pltpu.VMEM((NH, tq, tq), F32), pltpu.VMEM((NH, tq, tq), BF16)] + (_gather_sems(n) if n else []),
    )
    out, lse, *gathered = pl.pallas_call(
        body, name="flash_fwd", grid_spec=grid_spec,
        out_shape=[_sds((S, 512), BF16), _sds((MLA_HEADS // NH, NH, S), F32)] + _gather_out_shapes(shards),
        compiler_params=_cp(("arbitrary", "arbitrary")),
    )(qb_of, kb_of, qp, kp, vt, *shards)
    return out, lse.reshape(MLA_HEADS // 2, 2, S), gathered


def _decay_table():
    log_g = np.log(1.0 - 2.0 ** (-5.0 - np.arange(RET_HEADS, dtype=np.float32))).astype(np.float32)
    return jnp.asarray(np.broadcast_to(log_g[:, None, None], (RET_HEADS, 8, 128)).copy())


def _decay_terms(lg_ref):
    C = RET_CHUNK
    lg = lg_ref[0:1, :]
    row = lax.broadcasted_iota(jnp.int32, (C, C), 0)
    col = lax.broadcasted_iota(jnp.int32, (C, C), 1)
    diff = (row - col).astype(F32)
    dmat = jnp.where(diff >= 0, jnp.exp(jnp.maximum(diff, 0.0) * lg), 0.0)
    j = lax.broadcasted_iota(jnp.int32, (C, 1), 0).astype(F32)
    lg1 = lg[:, 0:1]
    zeta = jnp.exp((C - 1 - j) * lg1)
    xi = jnp.exp((j + 1.0) * lg1)
    g_chunk = jnp.exp(C * lg1)
    return dmat, zeta, xi, g_chunk


def _ret_fwd(rq, rk, rv, rg, gn_w, S):
    C = RET_CHUNK
    N = S // C
    G = min(RET_GROUP, N)
    NB = N // G

    def body(lg_ref, q_ref, k_ref, v_ref, rg_ref, w_ref, ry_ref, ro_ref, rprev_ref, r_sc):
        @pl.when(pl.program_id(1) == 0)
        def _():
            r_sc[...] = jnp.zeros(r_sc.shape, F32)

        dmat, zeta, xi, g_chunk = _decay_terms(lg_ref)
        w = w_ref[...]
        r = r_sc[...]
        for i in range(G):
            rows = slice(i * C, (i + 1) * C)
            q = q_ref[rows, :]
            k = k_ref[rows, :]
            v = v_ref[rows, :]
            r_prev = r.astype(BF16)
            rprev_ref[i] = r_prev
            sc = _dot_nt(q, k) * dmat
            ry = _dot(sc, v) + jnp.dot(q, r_prev, preferred_element_type=F32) * xi
            ry_ref[rows, :] = ry
            r = g_chunk * r + _dot_tn(k, zeta * v.astype(F32))
            mu = jnp.mean(ry, axis=-1, keepdims=True)
            yc = ry - mu
            yh = yc * lax.rsqrt(jnp.mean(yc * yc, axis=-1, keepdims=True) + EPS)
            g = rg_ref[rows, :]
            ro_ref[rows, :] = (g * _sig(g) * (yh * w)).astype(BF16)
        r_sc[...] = r

    blk = pl.BlockSpec((G * C, 128), lambda h, n: (n, h))
    return pl.pallas_call(
        body, name="ret_fwd", grid=(RET_HEADS, NB),
        in_specs=[pl.BlockSpec((None, 8, 128), lambda h, n: (h, 0, 0)), blk, blk, blk, blk,
                  pl.BlockSpec((1, 128), lambda h, n: (0, h))],
        out_specs=[blk, blk, pl.BlockSpec((G, 128, 128), lambda h, n: (h * NB + n, 0, 0))],
        out_shape=[_sds((S, 512), F32), _sds((S, 512), BF16), _sds((RET_HEADS * N, 128, 128), BF16)],
        scratch_shapes=[pltpu.VMEM((128, 128), F32)],
        compiler_params=_cp(("parallel", "arbitrary")),
    )(_decay_table(), rq, rk, rv, rg, gn_w)


def _outproj(ro, mo, x, w_o, g_post, g_pre, S):
    tm = min(256, S)

    def body(ro_ref, mo_ref, x_ref, wo_ref, g1_ref, g2_ref, mix_ref, h1_ref, hn_ref):
        mix = (jnp.dot(ro_ref[...], wo_ref[0:512, :], preferred_element_type=F32)
               + jnp.dot(mo_ref[...], wo_ref[512:1024, :], preferred_element_type=F32))
        mix_ref[...] = mix
        h1 = x_ref[...] + _rms(mix, g1_ref[...])
        h1_ref[...] = h1
        hn_ref[...] = _rms(h1, g2_ref[...]).astype(BF16)

    return pl.pallas_call(
        body, name="outproj", grid=(S // tm,),
        in_specs=[_rows(tm, 512), _rows(tm, 512), _rows(tm, D_MODEL), _full(D_MODEL, D_MODEL), _full(1, D_MODEL),
                  _full(1, D_MODEL)],
        out_specs=[_rows(tm, D_MODEL)] * 3,
        out_shape=[_sds((S, D_MODEL), F32), _sds((S, D_MODEL), F32), _sds((S, D_MODEL), BF16)],
        compiler_params=_cp(("parallel",)),
    )(ro, mo, x, w_o, g_post, g_pre)


def _ffn_up(hn, w_gate, w_up, S):
    tm = min(512, S)
    tn = D_FF // 2

    def body(hn_ref, wg_ref, wu_ref, gate_ref, up_ref, act_ref):
        hn_b = hn_ref[...]
        g = jnp.dot(hn_b, wg_ref[...], preferred_element_type=F32)
        u = jnp.dot(hn_b, wu_ref[...], preferred_element_type=F32)
        gate_ref[...] = g.astype(BF16)
        up_ref[...] = u.astype(BF16)
        act_ref[...] = (g * _sig(g) * u).astype(BF16)

    wspec = pl.BlockSpec((D_MODEL, tn), lambda j, i: (0, j))
    ospec = pl.BlockSpec((tm, tn), lambda j, i: (i, j))
    return pl.pallas_call(
        body, name="ffn_up", grid=(2, S // tm),
        in_specs=[pl.BlockSpec((tm, D_MODEL), lambda j, i: (i, 0)), wspec, wspec],
        out_specs=[ospec] * 3, out_shape=[_sds((S, D_FF), BF16)] * 3,
        compiler_params=_cp(("parallel", "parallel")),
    )(hn, w_gate, w_up)


def _ffn_down(act, w_down, h1, g, S):
    tm = min(256, S)

    def body(act_ref, wd_ref, h1_ref, g_ref, ff_ref, h2_ref):
        ff = jnp.dot(act_ref[...], wd_ref[...], preferred_element_type=F32)
        ff_ref[...] = ff
        h2_ref[...] = h1_ref[...] + _rms(ff, g_ref[...])

    return pl.pallas_call(
        body, name="ffn_down", grid=(S // tm,),
        in_specs=[_rows(tm, D_FF), _full(D_FF, D_MODEL), _rows(tm, D_MODEL), _full(1, D_MODEL)],
        out_specs=[_rows(tm, D_MODEL)] * 2, out_shape=[_sds((S, D_MODEL), F32)] * 2,
        compiler_params=_cp(("parallel",)),
    )(act, w_down, h1, g)


def _ple_loss(p, h2, tgt, w_pp, w_pg, b_pg, g_ple, S):
    tm = min(256, S)

    def body(p_ref, h2_ref, t_ref, wp_ref, wg_ref, b_ref, gp_ref,
             dz_ref, dpe_ref, dh2_ref, h2b_ref, loss_ref, dgp_ref, db_ref):
        @pl.when(pl.program_id(0) == 0)
        def _():
            loss_ref[...] = jnp.zeros(loss_ref.shape, F32)
            dgp_ref[...] = jnp.zeros(dgp_ref.shape, F32)
            db_ref[...] = jnp.zeros(db_ref.shape, F32)

        gp = gp_ref[...]
        pe = _dot(p_ref[...], wp_ref[...])
        r = lax.rsqrt(jnp.mean(pe * pe, axis=-1, keepdims=True) + EPS)
        peh = pe * r
        e = peh * gp
        h2 = h2_ref[...]
        h2b = h2.astype(BF16)
        h2b_ref[...] = h2b
        gt = _sig(jnp.dot(h2b, wg_ref[...], preferred_element_type=F32) + b_ref[...])
        diff = h2 + e * gt - t_ref[...]
        loss_ref[...] += _colsum(diff * diff)
        dh3 = diff * (1.0 / D_MODEL)
        de = dh3 * gt
        dz = dh3 * e * gt * (1.0 - gt)
        db_ref[...] += _colsum(dz)
        dgp_ref[...] += _colsum(de * peh)
        dpeh = de * gp
        dpe = r * (dpeh - peh * jnp.mean(dpeh * peh, axis=-1, keepdims=True))
        dzb = dz.astype(BF16)
        dz_ref[...] = dzb
        dpe_ref[...] = dpe.astype(BF16)
        dh2_ref[...] = dh3 + _dot_nt(dzb, wg_ref[...])

    return pl.pallas_call(
        body, name="ple_loss", grid=(S // tm,),
        in_specs=[_rows(tm, PLE_DIM), _rows(tm, D_MODEL), _rows(tm, D_MODEL), _full(PLE_DIM, D_MODEL),
                  _full(D_MODEL, D_MODEL), _full(1, D_MODEL), _full(1, D_MODEL)],
        out_specs=[_rows(tm, D_MODEL)] * 4 + [_full(1, D_MODEL)] * 3,
        out_shape=[_sds((S, D_MODEL), BF16), _sds((S, D_MODEL), BF16), _sds((S, D_MODEL), F32), _sds((S, D_MODEL), BF16)]
        + [_sds((1, D_MODEL), F32)] * 3,
        compiler_params=_cp(("arbitrary",)),
    )(p, h2, tgt, w_pp, w_pg, b_pg, g_ple)


def _wgrad(a, b, name, S):
    M = a.shape[1]
    N = b.shape[1]
    ts = min(512, S)
    nsplit = 2 if M * N >= 2 * 1024 * 1024 else 1
    tn = N // nsplit

    def body(a_ref, b_ref, o_ref):
        @pl.when(pl.program_id(1) == 0)
        def _():
            o_ref[...] = jnp.zeros(o_ref.shape, F32)

        o_ref[...] += _dot_tn(a_ref[...], b_ref[...])

    return pl.pallas_call(
        body, name=name, grid=(nsplit, S // ts),
        in_specs=[pl.BlockSpec((ts, M), lambda j, s: (s, 0)), pl.BlockSpec((ts, tn), lambda j, s: (s, j))],
        out_specs=pl.BlockSpec((M, tn), lambda j, s: (0, j)), out_shape=_sds((M, N), F32),
        compiler_params=_cp(("parallel", "arbitrary")),
    )(a, b)


def _ffn_down_bwd(dh2, ff, g, w_down, gate, up, S):
    tm = min(256, S)
    tn = D_FF // 2

    def body(dh2_ref, ff_ref, g_ref, wd_ref, gate_ref, up_ref, dff_ref, dgate_ref, dup_ref, dg_ref):
        @pl.when(pl.program_id(0) == 0)
        def _():
            dg_ref[...] = jnp.zeros(dg_ref.shape, F32)

        dff, ga = _rms_bwd(dh2_ref[...], ff_ref[...], g_ref[...])
        dg_ref[...] += _colsum(ga)
        dffb = dff.astype(BF16)
        dff_ref[...] = dffb
        for seg in range(2):
            sl = slice(seg * tn, (seg + 1) * tn)
            dact = _dot_nt(dffb, wd_ref[sl, :])
            gt = gate_ref[:, sl].astype(F32)
            u = up_ref[:, sl].astype(F32)
            s = _sig(gt)
            dgate_ref[:, sl] = (dact * u * (s * (1.0 + gt * (1.0 - s)))).astype(BF16)
            dup_ref[:, sl] = (dact * (gt * s)).astype(BF16)

    return pl.pallas_call(
        body, name="ffn_down_bwd", grid=(S // tm,),
        in_specs=[_rows(tm, D_MODEL), _rows(tm, D_MODEL), _full(1, D_MODEL), _full(D_FF, D_MODEL), _rows(tm, D_FF),
                  _rows(tm, D_FF)],
        out_specs=[_rows(tm, D_MODEL), _rows(tm, D_FF), _rows(tm, D_FF), _full(1, D_MODEL)],
        out_shape=[_sds((S, D_MODEL), BF16), _sds((S, D_FF), BF16), _sds((S, D_FF), BF16), _sds((1, D_MODEL), F32)],
        compiler_params=_cp(("arbitrary",)),
    )(dh2, ff, g, w_down, gate, up)


def _ffn_up_bwd(dgate, dup, w_gate, w_up, h1, mix, dh2, g_pre, g_post, w_o, S, grads=()):
    tm = min(256, S)
    n = len(grads)
    last = S // tm - 1

    def body(dgate_ref, dup_ref, wg_ref, wu_ref, h1_ref, mix_ref, dh2_ref, g2_ref, g1_ref, wo_ref, *rest):
        g_ins = rest[:n]
        dh1_ref, dmix_ref, dro_ref, dmo_ref, dg2_ref, dg1_ref = rest[n:n + 6]
        g_outs, sems = rest[n + 6:2 * n + 6], rest[2 * n + 6:]

        @pl.when(pl.program_id(0) == 0)
        def _():
            dg2_ref[...] = jnp.zeros(dg2_ref.shape, F32)
            dg1_ref[...] = jnp.zeros(dg1_ref.shape, F32)
            for cp in (_swap_copies(g_ins, g_outs, sems) if n else []):
                cp.start()

        dhn = _dot_nt(dgate_ref[...], wg_ref[...]) + _dot_nt(dup_ref[...], wu_ref[...])
        d1, ga = _rms_bwd(dhn, h1_ref[...], g2_ref[...])
        dg2_ref[...] += _colsum(ga)
        dh1 = dh2_ref[...] + d1
        dh1_ref[...] = dh1
        dmix, gb = _rms_bwd(dh1, mix_ref[...], g1_ref[...])
        dg1_ref[...] += _colsum(gb)
        dmixb = dmix.astype(BF16)
        dmix_ref[...] = dmixb
        dcat = _dot_nt(dmixb, wo_ref[...])
        dro_ref[...] = dcat[:, 0:512].astype(BF16)
        dmo_ref[...] = dcat[:, 512:1024].astype(BF16)

        if n:
            @pl.when(pl.program_id(0) == last)
            def _():
                for cp in _swap_copies(g_ins, g_outs, sems):
                    cp.wait()

    dh1, dmix, dro, dmo, dg2, dg1, *got = pl.pallas_call(
        body, name="ffn_up_bwd", grid=(S // tm,),
        in_specs=[_rows(tm, D_FF), _rows(tm, D_FF), _full(D_MODEL, D_FF), _full(D_MODEL, D_FF), _rows(tm, D_MODEL),
                  _rows(tm, D_MODEL), _rows(tm, D_MODEL), _full(1, D_MODEL), _full(1, D_MODEL), _full(D_MODEL, D_MODEL)]
        + [_ANY] * n,
        out_specs=[_rows(tm, D_MODEL), _rows(tm, D_MODEL), _rows(tm, 512), _rows(tm, 512), _full(1, D_MODEL),
                   _full(1, D_MODEL)] + [_ANY] * n,
        out_shape=[_sds((S, D_MODEL), F32), _sds((S, D_MODEL), BF16), _sds((S, 512), BF16), _sds((S, 512), BF16),
                   _sds((1, D_MODEL), F32), _sds((1, D_MODEL), F32)] + _swap_out_shapes(grads),
        scratch_shapes=_swap_sems(n) if n else [],
        compiler_params=_cp(("arbitrary",)),
    )(dgate, dup, w_gate, w_up, h1, mix, dh2, g_pre, g_post, w_o, *grads)
    return dh1, dmix, dro, dmo, dg2, dg1, got


def _attn_delta(o, do, S):
    tm = min(512, S)

    def body(o_ref, do_ref, dot_ref, d_ref):
        do = do_ref[...].astype(F32)
        prod_t = (o_ref[...].astype(F32) * do).T
        dot_ref[...] = do.T.astype(BF16)
        for h in range(MLA_HEADS):
            d_ref[h // 2, (h % 2):(h % 2) + 1, :] = jnp.sum(prod_t[h * 64:(h + 1) * 64, :], axis=0, keepdims=True)

    return pl.pallas_call(
        body, name="attn_delta", grid=(S // tm,),
        in_specs=[_rows(tm, 512), _rows(tm, 512)],
        out_specs=[pl.BlockSpec((512, tm), lambda i: (0, i)), pl.BlockSpec((MLA_HEADS // 2, 2, tm), lambda i: (0, 0, i))],
        out_shape=[_sds((512, S), BF16), _sds((MLA_HEADS // 2, 2, S), F32)],
        compiler_params=_cp(("parallel",)),
    )(o, do)


def _flash_bwd(qp, kp, kt, v, do, dot, lse, delta, S, sums=()):
    tq = min(512, S)
    nq = S // tq
    RB = ATT_ROWS
    qb_of, kb_of, T = _tri_pairs(nq, k_major=True)
    n = len(sums)
    steps = (MLA_HEADS // 2) * T

    def body(qb_ref, kb_ref, q_ref, k_ref, kt_ref, v_ref, do_ref, dot_ref, lse_ref, dl_ref, *rest):
        g_ins, (dq_ref, dk_ref, dv_ref), g_outs = rest[:n], rest[n:n + 3], rest[n + 3:2 * n + 3]
        dk_sc, dv_sc, s_sc, dp_sc, p_sc, ds_sc = rest[2 * n + 3:2 * n + 9]
        sems = rest[2 * n + 9:]
        t = pl.program_id(1)
        qb = qb_ref[t]
        kb = kb_ref[t]
        lin = pl.program_id(0) * T + t

        if n:
            @pl.when(lin == 0)
            def _():
                for cp in _scatter_copies(g_ins, g_outs, sems):
                    cp.start()

        @pl.when(t == 0)
        def _():
            dq_ref[...] = jnp.zeros(dq_ref.shape, F32)

        @pl.when(qb == kb)
        def _():
            dk_sc[...] = jnp.zeros(dk_sc.shape, F32)
            dv_sc[...] = jnp.zeros(dv_sc.shape, F32)

        lane = lax.broadcasted_iota(jnp.int32, (tq, 128), 1)

        def step(masked):
            vv = v_ref[...]
            do_all = do_ref[...]
            mine = [lane < 64, lane >= 64]
            for a in range(2):
                sl = slice(a * 128, (a + 1) * 128)
                s_sc[a] = _dot_nt(k_ref[:, sl], q_ref[:, sl])
                dp_sc[a] = jnp.dot(jnp.where(mine[a], vv, jnp.zeros_like(vv)), dot_ref[...],
                                   preferred_element_type=F32)
            for a in range(2):
                sl = slice(a * 128, (a + 1) * 128)
                lse = lse_ref[a:a + 1, :]
                dl = dl_ref[a:a + 1, :]
                for r in range(0, tq, RB):
                    sc = s_sc[a, r:r + RB, :]
                    if masked:
                        sc = jnp.where(_causal_keep(r, RB, tq), sc, NEG)
                    p = jnp.exp(sc - lse)
                    p_sc[a, r:r + RB, :] = p.astype(BF16)
                    ds_sc[a, r:r + RB, :] = (p * (dp_sc[a, r:r + RB, :] - dl)).astype(BF16)
                ds = ds_sc[a]
                dv_sc[...] += jnp.dot(p_sc[a], jnp.where(mine[a], do_all, jnp.zeros_like(do_all)),
                                      preferred_element_type=F32)
                dk_sc[:, sl] += jnp.dot(ds, q_ref[:, sl], preferred_element_type=F32)
                dq_ref[qb, sl, :] += jnp.dot(kt_ref[sl, :], ds, preferred_element_type=F32)

        @pl.when(qb > kb)
        def _():
            step(False)

        @pl.when(qb == kb)
        def _():
            step(True)

        @pl.when(qb == nq - 1)
        def _():
            dk_ref[...] = dk_sc[...]
            dv_ref[...] = dv_sc[...]

        if n:
            @pl.when(lin == steps - 1)
            def _():
                for cp in _scatter_copies(g_ins, g_outs, sems):
                    cp.wait()

    grid_spec = pltpu.PrefetchScalarGridSpec(
        num_scalar_prefetch=2, grid=(MLA_HEADS // 2, T),
        in_specs=[pl.BlockSpec((tq, 256), lambda j, t, qb, kb: (qb[t], j)),
                  pl.BlockSpec((tq, 256), lambda j, t, qb, kb: (kb[t], j)),
                  pl.BlockSpec((256, tq), lambda j, t, qb, kb: (j, kb[t])),
                  pl.BlockSpec((tq, 128), lambda j, t, qb, kb: (kb[t], j)),
                  pl.BlockSpec((tq, 128), lambda j, t, qb, kb: (qb[t], j)),
                  pl.BlockSpec((128, tq), lambda j, t, qb, kb: (j, qb[t])),
                  pl.BlockSpec((None, 2, tq), lambda j, t, qb, kb: (j, 0, qb[t])),
                  pl.BlockSpec((None, 2, tq), lambda j, t, qb, kb: (j, 0, qb[t]))] + [_ANY] * n,
        out_specs=[pl.BlockSpec((nq, 256, tq), lambda j, t, qb, kb: (0, j, 0)),
                   pl.BlockSpec((tq, 256), lambda j, t, qb, kb: (kb[t], j)),
                   pl.BlockSpec((tq, 128), lambda j, t, qb, kb: (kb[t], j))] + [_ANY] * n,
        scratch_shapes=[pltpu.VMEM((tq, 256), F32), pltpu.VMEM((tq, 128), F32), pltpu.VMEM((2, tq, tq), F32),
                        pltpu.VMEM((2, tq, tq), F32), pltpu.VMEM((2, tq, tq), BF16), pltpu.VMEM((2, tq, tq), BF16)]
        + (_scatter_sems(n) if n else []),
    )
    dq, dk, dv, *parts = pl.pallas_call(
        body, name="flash_bwd", grid_spec=grid_spec,
        out_shape=[_sds((nq, 1024, tq), F32), _sds((S, 1024), F32), _sds((S, 512), F32)] + _scatter_out_shapes(sums),
        compiler_params=_cp(("arbitrary", "arbitrary")),
    )(qb_of, kb_of, qp, kp, kt, v, do, dot, lse, delta, *sums)
    return dq, dk, dv, parts


def _mla_up_bwd(dqp, dkp, dv, cq, ckv, gq, gkv, w_uq, w_ukv, tabs, S):
    tm = min(256, S)

    def body(dq_ref, dk_ref, dv_ref, cq_ref, ckv_ref, gq_ref, gkv_ref, wuq_ref, wukv_ref, cm_ref, sa_ref, sb_ref,
             dqh_ref, dkv_ref, dcq_ref, dckv_ref, dkr_ref, dgq_ref, dgkv_ref):
        @pl.when(pl.program_id(0) == 0)
        def _():
            dgq_ref[...] = jnp.zeros(dgq_ref.shape, F32)
            dgkv_ref[...] = jnp.zeros(dgkv_ref.shape, F32)

        cm = cm_ref[...]
        sa = sa_ref[...]
        sb = sb_ref[...]
        lane = lax.broadcasted_iota(jnp.int32, (tm, 128), 1)
        dkr_r = jnp.zeros((tm, 128), F32)
        for h in range(MLA_HEADS):
            sl = slice(h * 128, (h + 1) * 128)
            dqh_ref[:, sl] = (_unrope_mla(dq_ref[sl, :].T, cm, sa, sb) * SCALE_MLA).astype(BF16)
            gk = dk_ref[:, sl]
            dkr_r = dkr_r + gk
            dkv_ref[:, sl] = gk.astype(BF16)
        dkr_r = jnp.where((lane >= 64) & (lane < 96), dkr_r, 0.0)
        dkr_ref[...] = _unrope_mla(dkr_r, cm, sa, sb).astype(BF16)
        dkv_ref[:, 1024:1536] = dv_ref[...].astype(BF16)
        dcq, ga = _rms_bwd(_dot_nt(dqh_ref[...], wuq_ref[...]), cq_ref[...], gq_ref[...])
        dcq_ref[...] = dcq.astype(BF16)
        dgq_ref[...] += _colsum(ga)
        dckv, gb = _rms_bwd(_dot_nt(dkv_ref[...], wukv_ref[...]), ckv_ref[...], gkv_ref[...])
        dckv_ref[...] = dckv.astype(BF16)
        dgkv_ref[...] += _colsum(gb)

    per_q = dqp.shape[2] // tm
    return pl.pallas_call(
        body, name="mla_up_bwd", grid=(S // tm,),
        in_specs=[pl.BlockSpec((None, 1024, tm), lambda i: (i // per_q, 0, i % per_q)),
                  _rows(tm, 1024), _rows(tm, 512), _rows(tm, Q_LORA), _rows(tm, KV_LORA),
                  _full(1, Q_LORA), _full(1, KV_LORA), _full(Q_LORA, 1024), _full(KV_LORA, 1536)] + [_rows(tm, 128)] * 3,
        out_specs=[_rows(tm, 1024), _rows(tm, 1536), _rows(tm, Q_LORA), _rows(tm, KV_LORA), _rows(tm, 128),
                   _full(1, Q_LORA), _full(1, KV_LORA)],
        out_shape=[_sds((S, 1024), BF16), _sds((S, 1536), BF16), _sds((S, Q_LORA), BF16), _sds((S, KV_LORA), BF16),
                   _sds((S, 128), BF16), _sds((1, Q_LORA), F32), _sds((1, KV_LORA), F32)],
        compiler_params=_cp(("arbitrary",)),
    )(dqp, dkp, dv, cq, ckv, gq, gkv, w_uq, w_ukv, *tabs[2:])


def _ret_bwd(rq, rk, rv, rprev, ry, rg, dro, gn_w, tabs, S):
    C = RET_CHUNK
    N = S // C
    G = min(RET_GROUP, N)
    NB = N // G

    def body(lg_ref, q_ref, k_ref, v_ref, rp_ref, ry_ref, rg_ref, dro_ref, w_ref, cr_ref, sr_ref,
             drq_ref, drk_ref, drv_ref, drg_ref, dw_ref, g_sc):
        @pl.when(pl.program_id(1) == 0)
        def _():
            g_sc[...] = jnp.zeros(g_sc.shape, F32)
            dw_ref[...] = jnp.zeros(dw_ref.shape, F32)

        dmat, zeta, xi, g_chunk = _decay_terms(lg_ref)
        w = w_ref[...]
        gacc = g_sc[...]
        dw = jnp.zeros((1, 128), F32)
        for i in reversed(range(G)):
            rows = slice(i * C, (i + 1) * C)
            ry = ry_ref[rows, :]
            mu = jnp.mean(ry, axis=-1, keepdims=True)
            yc = ry - mu
            rstd = lax.rsqrt(jnp.mean(yc * yc, axis=-1, keepdims=True) + EPS)
            yh = yc * rstd
            g = rg_ref[rows, :]
            s = _sig(g)
            dout = dro_ref[rows, :].astype(F32)
            drg_ref[rows, :] = (dout * (yh * w) * (s * (1.0 + g * (1.0 - s)))).astype(BF16)
            dgn = dout * (g * s)
            dw = dw + _colsum(dgn * yh)
            dyh = dgn * w
            dry = rstd * (dyh - jnp.mean(dyh, axis=-1, keepdims=True) - yh * jnp.mean(dyh * yh, axis=-1, keepdims=True))
            do = dry.astype(BF16)

            q = q_ref[rows, :]
            k = k_ref[rows, :]
            v = v_ref[rows, :]
            gfut = gacc.astype(BF16)
            sc = (_dot_nt(q, k) * dmat).astype(BF16)
            dsc = (_dot_nt(do, v) * dmat).astype(BF16)
            dq = jnp.dot(dsc, k, preferred_element_type=F32) + _dot_nt(do, rp_ref[i]) * xi
            dk = _dot_tn(dsc, q) + _dot_nt(v, gfut) * zeta
            dv = _dot_tn(sc, do) + jnp.dot(k, gfut, preferred_element_type=F32) * zeta
            gacc = g_chunk * gacc + _dot_tn(q, xi * dry)
            cr = cr_ref[rows, :]
            sr = sr_ref[rows, :]
            drq_ref[rows, :] = _unrope_ret(dq, cr, sr).astype(BF16)
            drk_ref[rows, :] = _unrope_ret(dk * SCALE_RET, cr, sr).astype(BF16)
            drv_ref[rows, :] = dv.astype(BF16)
        g_sc[...] = gacc
        dw_ref[...] += dw

    blk = pl.BlockSpec((G * C, 128), lambda h, n: (NB - 1 - n, h))
    tab = pl.BlockSpec((G * C, 128), lambda h, n: (NB - 1 - n, 0))
    return pl.pallas_call(
        body, name="ret_bwd", grid=(RET_HEADS, NB),
        in_specs=[pl.BlockSpec((None, 8, 128), lambda h, n: (h, 0, 0)), blk, blk, blk,
                  pl.BlockSpec((G, 128, 128), lambda h, n: (h * NB + NB - 1 - n, 0, 0)), blk, blk, blk,
                  pl.BlockSpec((1, 128), lambda h, n: (0, h)), tab, tab],
        out_specs=[blk, blk, blk, blk, pl.BlockSpec((1, 128), lambda h, n: (0, h))],
        out_shape=[_sds((S, 512), BF16)] * 4 + [_sds((1, 512), F32)],
        scratch_shapes=[pltpu.VMEM((128, 128), F32)],
        compiler_params=_cp(("parallel", "arbitrary")),
    )(_decay_table(), rq, rk, rv, rprev, ry, rg, dro, gn_w, tabs[0], tabs[1])


def _inproj_bwd(drq, drk, drv, drg, dcq, dckv, dkr, w_in, dh1, x, g, S):
    tm = min(256, S)

    def body(drq_ref, drk_ref, drv_ref, drg_ref, dcq_ref, dckv_ref, dkr_ref, w_ref, dh1_ref, x_ref, g_ref,
             gx_ref, dproj_ref, dg_ref):
        @pl.when(pl.program_id(0) == 0)
        def _():
            dg_ref[...] = jnp.zeros(dg_ref.shape, F32)

        dproj_ref[:, 0:512] = drq_ref[...]
        dproj_ref[:, 512:1024] = drk_ref[...]
        dproj_ref[:, 1024:1536] = drv_ref[...]
        dproj_ref[:, 1536:2048] = drg_ref[...]
        dproj_ref[:, 2048:2432] = dcq_ref[...]
        dproj_ref[:, 2432:2688] = dckv_ref[...]
        dproj_ref[:, 2688:2816] = dkr_ref[...]
        dx, ga = _rms_bwd(_dot_nt(dproj_ref[...], w_ref[...]), x_ref[...], g_ref[...])
        gx_ref[...] = dh1_ref[...] + dx
        dg_ref[...] += _colsum(ga)

    return pl.pallas_call(
        body, name="inproj_bwd", grid=(S // tm,),
        in_specs=[_rows(tm, 512)] * 4 + [_rows(tm, Q_LORA), _rows(tm, KV_LORA), _rows(tm, 128),
                                         _full(D_MODEL, IN_COLS_P), _rows(tm, D_MODEL), _rows(tm, D_MODEL),
                                         _full(1, D_MODEL)],
        out_specs=[_rows(tm, D_MODEL), _rows(tm, IN_COLS_P), _full(1, D_MODEL)],
        out_shape=[_sds((S, D_MODEL), F32), _sds((S, IN_COLS_P), BF16), _sds((1, D_MODEL), F32)],
        compiler_params=_cp(("arbitrary",)),
    )(drq, drk, drv, drg, dcq, dckv, dkr, w_in, dh1, x, g)


def _pad_weights(w):
    w_in = w["w_in"]
    z = lambda r, c: jnp.zeros((r, c), BF16)
    w_in_p = jnp.concatenate([w_in[:, :2688], z(1024, 64), w_in[:, 2688:2720], z(1024, 32)], axis=1)
    w_uq_p = jnp.pad(w["w_uq"].reshape(Q_LORA, MLA_HEADS, 96), ((0, 0), (0, 0), (0, 32))).reshape(Q_LORA, 1024)
    ukv = w["w_ukv"].reshape(KV_LORA, MLA_HEADS, 128)
    k_part = jnp.pad(ukv[:, :, :64], ((0, 0), (0, 0), (0, 64))).reshape(KV_LORA, 1024)
    w_ukv_p = jnp.concatenate([k_part, ukv[:, :, 64:].reshape(KV_LORA, 512)], axis=1)
    return w_in_p, w_uq_p, w_ukv_p


BIG_SPEC = {n: (r, c, ax) for n, r, c, ax in BIG}
GATHER_FIRST = ("w_in", "w_uq", "w_ukv")
GATHER_LATE = tuple(n for n, _, _, _ in BIG if n not in GATHER_FIRST)
REDUCE_EARLY = ("w_ple_gate", "w_ple_proj", "w_down", "w_gate", "w_up")
REDUCE_LAST = tuple(n for n, _, _, _ in BIG if n not in REDUCE_EARLY)


def _local_step(x, p, pos_f, tgt, w, sm, late_shards=None, c_idx=None):
    S = x.shape[0]
    spread = late_shards is not None
    w = dict(w)
    w_in_p, w_uq_p, w_ukv_p = _pad_weights(w)
    tabs = _rope_tables(pos_f, S)

    xn = _rms_fwd(x, sm["pre_mix_norm"], S)
    rq, rk, rv, rg, cq, ckv, kr = _inproj(xn, w_in_p, tabs, S)
    cqn, ckvn, qp, kp, v, kt, vt = _mla_up(cq, ckv, kr, sm["mla_q_norm"], sm["mla_kv_norm"], w_uq_p, w_ukv_p, tabs, S)
    mo, lse, gathered = _flash_fwd(qp, kp, vt, S, [late_shards[n] for n in GATHER_LATE] if spread else ())
    for i, n in enumerate(GATHER_LATE if spread else ()):
        w[n] = _from_chips(gathered[i], BIG_SPEC[n][2])
    ry, ro, rprev = _ret_fwd(rq, rk, rv, rg, sm["ret_gn_w"], S)
    mix, h1, hn = _outproj(ro, mo, x, w["w_o"], sm["post_mix_norm"], sm["pre_ffn_norm"], S)
    gate, up, act = _ffn_up(hn, w["w_gate"], w["w_up"], S)
    ff, h2 = _ffn_down(act, w["w_down"], h1, sm["post_ffn_norm"], S)
    dz, dpe, dh2, h2b, loss_vec, d_ple_norm, d_b = _ple_loss(
        p, h2, tgt, w["w_ple_proj"], w["w_ple_gate"], sm["b_ple_gate"], sm["ple_norm"], S)

    gw = {}
    gs = {"ple_norm": d_ple_norm, "b_ple_gate": d_b}
    gw["w_ple_gate"] = _wgrad(h2b, dz, "wgrad_ple_gate", S)
    gw["w_ple_proj"] = _wgrad(p, dpe, "wgrad_ple_proj", S)
    dff, dgate, dup, gs["post_ffn_norm"] = _ffn_down_bwd(dh2, ff, sm["post_ffn_norm"], w["w_down"], gate, up, S)
    gw["w_down"] = _wgrad(act, dff, "wgrad_down", S)
    gw["w_gate"] = _wgrad(hn, dgate, "wgrad_gate", S)
    gw["w_up"] = _wgrad(hn, dup, "wgrad_up", S)
    g4 = [_by_chip(gw.pop(n), *BIG_SPEC[n]) for n in REDUCE_EARLY] if spread else []
    dh1, dmix, dro, dmo, gs["pre_ffn_norm"], gs["post_mix_norm"], got = _ffn_up_bwd(
        dgate, dup, w["w_gate"], w["w_up"], h1, mix, dh2, sm["pre_ffn_norm"], sm["post_mix_norm"], w["w_o"], S, g4)
    sums = [_add_half_rows(g4[i], got[i], c_idx, "rs_add_halves_" + n) for i, n in enumerate(REDUCE_EARLY)] if spread else []
    gw["w_o"] = jnp.concatenate([_wgrad(ro, dmix, "wgrad_o_ret", S), _wgrad(mo, dmix, "wgrad_o_mla", S)], axis=0)

    dmo_t, delta = _attn_delta(mo, dmo, S)
    dqp, dkp, dv, parts = _flash_bwd(qp, kp, kt, v, dmo, dmo_t, lse, delta, S, sums)
    dqh, dkv, dcq, dckv, dkr, gs["mla_q_norm"], gs["mla_kv_norm"] = _mla_up_bwd(
        dqp, dkp, dv, cq, ckv, sm["mla_q_norm"], sm["mla_kv_norm"], w_uq_p, w_ukv_p, tabs, S)
    g_uq_p = _wgrad(cqn, dqh, "wgrad_uq", S)
    g_ukv_p = _wgrad(ckvn, dkv, "wgrad_ukv", S)
    gw["w_uq"] = g_uq_p.reshape(Q_LORA, MLA_HEADS, 128)[:, :, :96].reshape(Q_LORA, 768)
    gw["w_ukv"] = jnp.concatenate(
        [g_ukv_p[:, :1024].reshape(KV_LORA, MLA_HEADS, 128)[:, :, :64], g_ukv_p[:, 1024:].reshape(KV_LORA, MLA_HEADS, 64)],
        axis=2).reshape(KV_LORA, 1024)

    drq, drk, drv, drg, gs["ret_gn_w"] = _ret_bwd(rq, rk, rv, rprev, ry, rg, dro, sm["ret_gn_w"], tabs, S)
    grad_x, dproj, gs["pre_mix_norm"] = _inproj_bwd(drq, drk, drv, drg, dcq, dckv, dkr, w_in_p, dh1, x,
                                                    sm["pre_mix_norm"], S)
    g_in_p = _wgrad(xn, dproj, "wgrad_in", S)
    gw["w_in"] = jnp.concatenate([g_in_p[:, :2688], g_in_p[:, 2752:2784]], axis=1)
    return loss_vec, grad_x, gw, gs, ((sums, parts) if spread else None)


def _my_place():
    x = lax.axis_index("x")
    y = lax.axis_index("y")
    c = lax.axis_index("c")
    return x, y, c


def _other_chips(x, y):
    return [(1 - x, y), (x, 1 - y), (1 - x, 1 - y)]


_ANY = pl.BlockSpec(memory_space=pl.ANY)


def _allgather_weights(wpk):
    H = HALF_ROWS

    def body(w_ref, out_ref, send1, recv1, send2, recv2, lsem):
        x, y, c = _my_place()
        me = 2 * x + y
        chips = _other_chips(x, y)
        half = pl.ds(pl.multiple_of(c * H, 32), H)
        other = pl.ds(pl.multiple_of((1 - c) * H, 32), H)
        mine = pltpu.make_async_copy(w_ref, out_ref.at[me], lsem)
        mine.start()

        def over_ici(k, src_chip, to):
            return pltpu.make_async_remote_copy(
                src_ref=w_ref.at[half], dst_ref=out_ref.at[src_chip, half], send_sem=send1.at[k], recv_sem=recv1.at[k],
                device_id=to, device_id_type=MESH)

        def to_sibling(k, chip, rows):
            return pltpu.make_async_remote_copy(
                src_ref=out_ref.at[chip, rows], dst_ref=out_ref.at[chip, rows], send_sem=send2.at[k],
                recv_sem=recv2.at[k], device_id=(x, y, 1 - c), device_id_type=MESH)

        first = [over_ici(k, me, (cx, cy, c)) for k, (cx, cy) in enumerate(chips)]
        for cp in first:
            cp.start()
        passed = []
        for k, (cx, cy) in enumerate(chips):
            over_ici(k, 2 * cx + cy, (cx, cy, c)).wait_recv()
            fwd = to_sibling(k, 2 * cx + cy, half)
            fwd.start()
            passed.append(fwd)
        for k, (cx, cy) in enumerate(chips):
            to_sibling(k, 2 * cx + cy, other).wait_recv()
        for cp in first + passed:
            cp.wait_send()
        mine.wait()

    return pl.pallas_call(
        body, name="allgather_weights",
        in_specs=[_ANY], out_specs=_ANY, out_shape=_sds((N_CHIPS, PACK_ROWS, PACK_COLS), BF16),
        scratch_shapes=[pltpu.SemaphoreType.DMA((3,)), pltpu.SemaphoreType.DMA((3,)), pltpu.SemaphoreType.DMA((3,)),
                        pltpu.SemaphoreType.DMA((3,)), pltpu.SemaphoreType.DMA],
    )(wpk)


def _swap_halves(gpk):
    H = HALF_ROWS

    def body(g_ref, out_ref, send, recv):
        x, y, c = _my_place()
        other = pl.ds(pl.multiple_of((1 - c) * H, 8), H)
        cp = pltpu.make_async_remote_copy(
            src_ref=g_ref.at[:, other], dst_ref=out_ref, send_sem=send, recv_sem=recv,
            device_id=(x, y, 1 - c), device_id_type=MESH)
        cp.start()
        cp.wait()

    return pl.pallas_call(
        body, name="rs_swap_halves",
        in_specs=[_ANY], out_specs=_ANY, out_shape=_sds((N_CHIPS, HALF_ROWS, PACK_COLS), F32),
        scratch_shapes=[pltpu.SemaphoreType.DMA, pltpu.SemaphoreType.DMA],
    )(gpk)


def _add_halves(gpk, got, c_idx):
    tr = 440
    nb = HALF_ROWS // tr

    def body(c_ref, a_ref, b_ref, o_ref):
        o_ref[...] = a_ref[...] + b_ref[...]

    grid_spec = pltpu.PrefetchScalarGridSpec(
        num_scalar_prefetch=1, grid=(N_CHIPS, nb),
        in_specs=[pl.BlockSpec((None, tr, PACK_COLS), lambda j, i, c: (j, c[0] * nb + i, 0)),
                  pl.BlockSpec((None, tr, PACK_COLS), lambda j, i, c: (j, i, 0))],
        out_specs=pl.BlockSpec((None, tr, PACK_COLS), lambda j, i, c: (j, i, 0)),
    )
    return pl.pallas_call(
        body, name="rs_add_halves", grid_spec=grid_spec, out_shape=_sds((N_CHIPS, HALF_ROWS, PACK_COLS), F32),
        compiler_params=_cp(("parallel", "parallel")),
    )(c_idx, gpk, got)


def _scatter_chips(tsum):
    def body(t_ref, out_ref, send, recv, lsem):
        x, y, c = _my_place()
        me = 2 * x + y
        chips = _other_chips(x, y)
        mine = pltpu.make_async_copy(t_ref.at[me], out_ref.at[me], lsem)
        mine.start()
        cps = [pltpu.make_async_remote_copy(
            src_ref=t_ref.at[2 * cx + cy], dst_ref=out_ref.at[me], send_sem=send.at[k], recv_sem=recv.at[k],
            device_id=(cx, cy, c), device_id_type=MESH) for k, (cx, cy) in enumerate(chips)]
        for cp in cps:
            cp.start()
        for cp in cps:
            cp.wait()
        mine.wait()

    return pl.pallas_call(
        body, name="rs_scatter_chips",
        in_specs=[_ANY], out_specs=_ANY, out_shape=_sds((N_CHIPS, HALF_ROWS, PACK_COLS), F32),
        scratch_shapes=[pltpu.SemaphoreType.DMA((3,)), pltpu.SemaphoreType.DMA((3,)), pltpu.SemaphoreType.DMA],
    )(tsum)


def _add_chips(parts):
    tr = 440

    def body(p_ref, o_ref):
        o_ref[...] = ((p_ref[0] + p_ref[1]) + p_ref[2]) + p_ref[3]

    return pl.pallas_call(
        body, name="rs_add_chips", grid=(HALF_ROWS // tr,),
        in_specs=[pl.BlockSpec((N_CHIPS, tr, PACK_COLS), lambda i: (0, i, 0))],
        out_specs=pl.BlockSpec((tr, PACK_COLS), lambda i: (i, 0)), out_shape=_sds((HALF_ROWS, PACK_COLS), F32),
        compiler_params=_cp(("parallel",)),
    )(parts)


def _join_halves(red):
    H = HALF_ROWS

    def body(r_ref, out_ref, send, recv, lsem):
        x, y, c = _my_place()
        half = pl.ds(pl.multiple_of(c * H, 8), H)
        mine = pltpu.make_async_copy(r_ref, out_ref.at[half], lsem)
        mine.start()
        cp = pltpu.make_async_remote_copy(
            src_ref=r_ref, dst_ref=out_ref.at[half], send_sem=send, recv_sem=recv,
            device_id=(x, y, 1 - c), device_id_type=MESH)
        cp.start()
        cp.wait()
        mine.wait()

    return pl.pallas_call(
        body, name="rs_join_halves",
        in_specs=[_ANY], out_specs=_ANY, out_shape=_sds((PACK_ROWS, PACK_COLS), F32),
        scratch_shapes=[pltpu.SemaphoreType.DMA, pltpu.SemaphoreType.DMA, pltpu.SemaphoreType.DMA],
    )(red)


def _allreduce_small(vec):
    def body(v_ref, out_ref, slots, send, recv, lsem):
        x, y, c = _my_place()
        me = 4 * x + 2 * y + c
        mine = pltpu.make_async_copy(v_ref, slots.at[me], lsem)
        mine.start()
        cps = []
        for r in range(1, N_DEV):
            px = x ^ (r >> 2)
            py = y ^ ((r >> 1) & 1)
            pc = c ^ (r & 1)
            cps.append(pltpu.make_async_remote_copy(
                src_ref=v_ref, dst_ref=slots.at[me], send_sem=send.at[r - 1], recv_sem=recv.at[r - 1],
                device_id=(px, py, pc), device_id_type=MESH))
        for cp in cps:
            cp.start()
        for cp in cps:
            cp.wait()
        mine.wait()
        acc = slots[0]
        for d in range(1, N_DEV):
            acc = acc + slots[d]
        out_ref[...] = acc
        loss = jnp.sum(acc[9:10, :], axis=1, keepdims=True) * (0.5 / D_MODEL)
        out_ref[9:10, :] = jnp.broadcast_to(loss, (1, PACK_COLS))

    vm = pl.BlockSpec(memory_space=pltpu.VMEM)
    return pl.pallas_call(
        body, name="allreduce_small",
        in_specs=[vm], out_specs=vm, out_shape=_sds((SMALL_ROWS, PACK_COLS), F32),
        scratch_shapes=[pltpu.VMEM((N_DEV, SMALL_ROWS, PACK_COLS), F32), pltpu.SemaphoreType.DMA((N_DEV - 1,)),
                        pltpu.SemaphoreType.DMA((N_DEV - 1,)), pltpu.SemaphoreType.DMA],
    )(vec)


N_BIG = len(BIG)


def _half(c, rows, align):
    h = rows // 2
    return pl.ds(pl.multiple_of(c * h, align), h)


def _gather_shards(shards):
    n = len(shards)

    def body(*refs):
        ins, outs, sems = refs[:n], refs[n:2 * n], refs[2 * n:]
        _gather_phase(0, ins, outs, sems)
        _gather_phase(1, ins, outs, sems)
        _gather_phase(2, ins, outs, sems)

    return pl.pallas_call(
        body, name="gather_weights",
        in_specs=[_ANY] * n, out_specs=[_ANY] * n,
        out_shape=_gather_out_shapes(shards), scratch_shapes=_gather_sems(n),
    )(*shards)


def _gather_out_shapes(shards):
    return [_sds((N_CHIPS,) + tuple(s.shape), BF16) for s in shards]


def _gather_sems(n):
    return [pltpu.SemaphoreType.DMA((n, 3))] * 4 + [pltpu.SemaphoreType.DMA((n,))] * 2


def _gather_phase(phase, ins, outs, sems):
    send1, recv1, send2, recv2, send3, recv3 = sems
    x, y, c = _my_place()
    me = 2 * x + y
    chips = _other_chips(x, y)
    sib = (x, y, 1 - c)
    for t in range(len(ins)):
        rows = ins[t].shape[0]
        half = _half(c, rows, 16)
        other = _half(1 - c, rows, 16)
        own = pltpu.make_async_remote_copy(
            src_ref=ins[t], dst_ref=outs[t].at[me], send_sem=send3.at[t], recv_sem=recv3.at[t],
            device_id=sib, device_id_type=MESH)
        if phase == 0:
            own.start()
        if phase == 2:
            own.wait()
        for k, (cx, cy) in enumerate(chips):
            src = 2 * cx + cy
            out = pltpu.make_async_remote_copy(
                src_ref=ins[t].at[half], dst_ref=outs[t].at[me, half], send_sem=send1.at[t, k],
                recv_sem=recv1.at[t, k], device_id=(cx, cy, c), device_id_type=MESH)
            landed = pltpu.make_async_remote_copy(
                src_ref=ins[t].at[half], dst_ref=outs[t].at[src, half], send_sem=send1.at[t, k],
                recv_sem=recv1.at[t, k], device_id=(cx, cy, c), device_id_type=MESH)
            fwd = pltpu.make_async_remote_copy(
                src_ref=outs[t].at[src, half], dst_ref=outs[t].at[src, half], send_sem=send2.at[t, k],
                recv_sem=recv2.at[t, k], device_id=sib, device_id_type=MESH)
            from_sib = pltpu.make_async_remote_copy(
                src_ref=outs[t].at[src, other], dst_ref=outs[t].at[src, other], send_sem=send2.at[t, k],
                recv_sem=recv2.at[t, k], device_id=sib, device_id_type=MESH)
            if phase == 0:
                out.start()
            if phase == 1:
                landed.wait_recv()
                fwd.start()
            if phase == 2:
                from_sib.wait_recv()
                out.wait_send()
                fwd.wait_send()


def _swap_copies(ins, outs, sems):
    send, recv = sems
    x, y, c = _my_place()
    return [pltpu.make_async_remote_copy(
        src_ref=ins[t].at[:, _half(1 - c, ins[t].shape[1], 8)], dst_ref=outs[t], send_sem=send.at[t],
        recv_sem=recv.at[t], device_id=(x, y, 1 - c), device_id_type=MESH) for t in range(len(ins))]


def _swap_out_shapes(gs):
    return [_sds((N_CHIPS, g.shape[1] // 2, g.shape[2]), F32) for g in gs]


def _swap_sems(n):
    return [pltpu.SemaphoreType.DMA((n,)), pltpu.SemaphoreType.DMA((n,))]


def _swap_half_rows(gs):
    n = len(gs)

    def body(*refs):
        cps = _swap_copies(refs[:n], refs[n:2 * n], refs[2 * n:])
        for cp in cps:
            cp.start()
        for cp in cps:
            cp.wait()

    return pl.pallas_call(
        body, name="rs_swap_halves",
        in_specs=[_ANY] * n, out_specs=[_ANY] * n, out_shape=_swap_out_shapes(gs), scratch_shapes=_swap_sems(n),
    )(*gs)


def _add_half_rows(g, got, c_idx, name):
    _, rows, cols = g.shape
    h = rows // 2

    def body(c_ref, a_ref, b_ref, o_ref):
        o_ref[...] = (a_ref[...] + b_ref[...]).astype(BF16)

    grid_spec = pltpu.PrefetchScalarGridSpec(
        num_scalar_prefetch=1, grid=(N_CHIPS,),
        in_specs=[pl.BlockSpec((None, h, cols), lambda j, c: (j, c[0], 0)),
                  pl.BlockSpec((None, h, cols), lambda j, c: (j, 0, 0))],
        out_specs=pl.BlockSpec((None, h, cols), lambda j, c: (j, 0, 0)),
    )
    return pl.pallas_call(
        body, name=name, grid_spec=grid_spec, out_shape=_sds((N_CHIPS, h, cols), BF16),
        compiler_params=_cp(("parallel",)),
    )(c_idx, g, got)


def _scatter_to_chips(ts):
    n = len(ts)

    def body(*refs):
        cps = _scatter_copies(refs[:n], refs[n:2 * n], refs[2 * n:])
        for cp in cps:
            cp.start()
        for cp in cps:
            cp.wait()

    return pl.pallas_call(
        body, name="rs_scatter_chips",
        in_specs=[_ANY] * n, out_specs=[_ANY] * n, out_shape=_scatter_out_shapes(ts), scratch_shapes=_scatter_sems(n),
    )(*ts)


def _scatter_copies(ins, outs, sems):
    send, recv = sems
    x, y, c = _my_place()
    return [pltpu.make_async_remote_copy(
        src_ref=ins[t].at[2 * cx + cy], dst_ref=outs[t].at[k], send_sem=send.at[t, k], recv_sem=recv.at[t, k],
        device_id=(cx, cy, c), device_id_type=MESH)
        for t in range(len(ins)) for k, (cx, cy) in enumerate(_other_chips(x, y))]


def _scatter_out_shapes(ts):
    return [_sds((3,) + tuple(t.shape[1:]), BF16) for t in ts]


def _scatter_sems(n):
    return [pltpu.SemaphoreType.DMA((n, 3)), pltpu.SemaphoreType.DMA((n, 3))]


def _add_four(mine, parts, place, name):
    _, h, cols = parts.shape

    def body(pl_ref, m_ref, p_ref, o_ref):
        o_ref[...] = ((m_ref[...].astype(F32) + p_ref[0].astype(F32)) + p_ref[1].astype(F32)) + p_ref[2].astype(F32)

    grid_spec = pltpu.PrefetchScalarGridSpec(
        num_scalar_prefetch=1, grid=(1,),
        in_specs=[pl.BlockSpec((None, h, cols), lambda i, pc: (pc[0], 0, 0)),
                  pl.BlockSpec((3, h, cols), lambda i, pc: (0, 0, 0))],
        out_specs=pl.BlockSpec((h, cols), lambda i, pc: (pc[1], 0)),
    )
    return pl.pallas_call(
        body, name=name, grid_spec=grid_spec, out_shape=_sds((2 * h, cols), F32),
        compiler_params=_cp(("arbitrary",)),
    )(place, mine, parts)


def _join_half_rows(rs):
    n = len(rs)

    def body(*refs):
        ins, outs = refs[:n], refs[n:2 * n]
        send, recv = refs[2 * n:]
        x, y, c = _my_place()
        cps = []
        for t in range(n):
            half = _half(c, outs[t].shape[0], 8)
            rc = pltpu.make_async_remote_copy(
                src_ref=ins[t].at[half], dst_ref=outs[t].at[half], send_sem=send.at[t], recv_sem=recv.at[t],
                device_id=(x, y, 1 - c), device_id_type=MESH)
            rc.start()
            cps.append(rc)
        for cp in cps:
            cp.wait()

    return pl.pallas_call(
        body, name="rs_join_halves",
        in_specs=[_ANY] * n, out_specs=[_ANY] * n,
        out_shape=[_sds(r.shape, F32) for r in rs],
        input_output_aliases={i: i for i in range(n)},
        scratch_shapes=[pltpu.SemaphoreType.DMA((n,))] * 2,
    )(*rs)


def _by_chip(full, rows, cols, axis):
    if axis == 0:
        return full.reshape(N_CHIPS, rows // N_CHIPS, cols)
    return full.reshape(rows, N_CHIPS, cols // N_CHIPS).transpose(1, 0, 2)


def _from_chips(parts, axis):
    _, r, c = parts.shape
    if axis == 0:
        return parts.reshape(N_CHIPS * r, c)
    return parts.transpose(1, 0, 2).reshape(r, N_CHIPS * c)


def _adamw(wt, g, m, v, name):
    R, C = wt.shape
    tr = R
    for cand in (256, 128, 64, 32, 16, 8):
        if R % cand == 0:
            tr = cand
            break

    def body(w_ref, g_ref, m_ref, v_ref, d_ref, nm_ref, nv_ref):
        gg = g_ref[...]
        m_new = ADAM_B1 * m_ref[...] + (1.0 - ADAM_B1) * gg
        v_new = ADAM_B2 * v_ref[...] + (1.0 - ADAM_B2) * (gg * gg)
        m_hat = m_new / (1.0 - ADAM_B1 ** ADAM_STEP)
        v_hat = v_new / (1.0 - ADAM_B2 ** ADAM_STEP)
        d_ref[...] = -ADAM_LR * (m_hat / (jnp.sqrt(v_hat) + ADAM_EPS) + ADAM_WD * w_ref[...])
        nm_ref[...] = m_new
        nv_ref[...] = v_new

    spec = pl.BlockSpec((tr, C), lambda i: (i, 0))
    return pl.pallas_call(
        body, name=name, grid=(R // tr,), in_specs=[spec] * 4, out_specs=[spec] * 3, out_shape=[_sds((R, C), F32)] * 3,
        compiler_params=_cp(("parallel",)),
    )(wt, g, m, v)


def _shard_shape(r, c, axis):
    return (r // N_CHIPS, c) if axis == 0 else (r, c // N_CHIPS)


def _pack_rows(flat):
    return jnp.pad(flat, (0, PACK_ROWS * PACK_COLS - flat.shape[0])).reshape(PACK_ROWS, PACK_COLS)


def _pack_shards(mats):
    return _pack_rows(jnp.concatenate([mats[n].reshape(-1) for n, _, _, _ in BIG]))


def _unpack_shards(pk):
    flat = pk.reshape(-1)
    out = {}
    off = 0
    for n, r, c, ax in BIG:
        shp = _shard_shape(r, c, ax)
        sz = shp[0] * shp[1]
        out[n] = flat[off:off + sz].reshape(shp)
        off += sz
    return out


def _full_from_packs(allpk):
    per_chip = [_unpack_shards(allpk[j]) for j in range(N_CHIPS)]
    return {n: jnp.concatenate([per_chip[j][n] for j in range(N_CHIPS)], axis=ax) for n, _, _, ax in BIG}


def _packs_from_full(gw):
    packs = []
    for j in range(N_CHIPS):
        shards = {}
        for n, r, c, ax in BIG:
            shp = _shard_shape(r, c, ax)
            shards[n] = gw[n][j * shp[0]:(j + 1) * shp[0], :] if ax == 0 else gw[n][:, j * shp[1]:(j + 1) * shp[1]]
        packs.append(_pack_shards(shards))
    return jnp.stack(packs)


def _pack_small(vals, loss_vec=None):
    rows = [jnp.pad(vals[n].reshape(-1), (0, PACK_COLS - sz)) for n, sz in SMALL]
    rows.append(loss_vec.reshape(-1) if loss_vec is not None else jnp.zeros((PACK_COLS,), F32))
    rows += [jnp.zeros((PACK_COLS,), F32)] * (SMALL_ROWS - len(rows))
    return jnp.stack(rows)


def kernel(x, p, positions, pre_mix_norm, w_in, ret_gn_w, mla_q_norm, w_uq, mla_kv_norm, w_ukv, w_o, post_mix_norm, pre_ffn_norm, w_gate, w_up, w_down, post_ffn_norm, w_ple_proj, ple_norm, w_ple_gate, b_ple_gate, loss_target, m_pre_mix_norm, m_w_in, m_ret_gn_w, m_mla_q_norm, m_w_uq, m_mla_kv_norm, m_w_ukv, m_w_o, m_post_mix_norm, m_pre_ffn_norm, m_w_gate, m_w_up, m_w_down, m_post_ffn_norm, m_w_ple_proj, m_ple_norm, m_w_ple_gate, m_b_ple_gate, v_pre_mix_norm, v_w_in, v_ret_gn_w, v_mla_q_norm, v_w_uq, v_mla_kv_norm, v_w_ukv, v_w_o, v_post_mix_norm, v_pre_ffn_norm, v_w_gate, v_w_up, v_w_down, v_post_ffn_norm, v_w_ple_proj, v_ple_norm, v_w_ple_gate, v_b_ple_gate):
    wts = dict(pre_mix_norm=pre_mix_norm, w_in=w_in, ret_gn_w=ret_gn_w, mla_q_norm=mla_q_norm, w_uq=w_uq,
               mla_kv_norm=mla_kv_norm, w_ukv=w_ukv, w_o=w_o, post_mix_norm=post_mix_norm, pre_ffn_norm=pre_ffn_norm,
               w_gate=w_gate, w_up=w_up, w_down=w_down, post_ffn_norm=post_ffn_norm, w_ple_proj=w_ple_proj,
               ple_norm=ple_norm, w_ple_gate=w_ple_gate, b_ple_gate=b_ple_gate)
    mom = dict(pre_mix_norm=m_pre_mix_norm, w_in=m_w_in, ret_gn_w=m_ret_gn_w, mla_q_norm=m_mla_q_norm, w_uq=m_w_uq,
               mla_kv_norm=m_mla_kv_norm, w_ukv=m_w_ukv, w_o=m_w_o, post_mix_norm=m_post_mix_norm,
               pre_ffn_norm=m_pre_ffn_norm, w_gate=m_w_gate, w_up=m_w_up, w_down=m_w_down, post_ffn_norm=m_post_ffn_norm,
               w_ple_proj=m_w_ple_proj, ple_norm=m_ple_norm, w_ple_gate=m_w_ple_gate, b_ple_gate=m_b_ple_gate)
    var = dict(pre_mix_norm=v_pre_mix_norm, w_in=v_w_in, ret_gn_w=v_ret_gn_w, mla_q_norm=v_mla_q_norm, w_uq=v_w_uq,
               mla_kv_norm=v_mla_kv_norm, w_ukv=v_w_ukv, w_o=v_w_o, post_mix_norm=v_post_mix_norm,
               pre_ffn_norm=v_pre_ffn_norm, w_gate=v_w_gate, w_up=v_w_up, w_down=v_w_down, post_ffn_norm=v_post_ffn_norm,
               w_ple_proj=v_w_ple_proj, ple_norm=v_ple_norm, w_ple_gate=v_w_ple_gate, b_ple_gate=v_b_ple_gate)

    S = x.shape[1]
    shard2d = {n: wts[n][0] for n, _, _, _ in BIG}
    small2d = {n: wts[n] for n, _ in SMALL}

    shard_bf = {n: shard2d[n].astype(BF16) for n in shard2d}
    gathered = _gather_shards([shard_bf[n] for n in GATHER_FIRST])
    w_first = {n: _from_chips(gathered[i], BIG_SPEC[n][2]) for i, n in enumerate(GATHER_FIRST)}

    pos_f = positions.astype(F32).reshape(S, 1)
    c_idx = lax.axis_index("c").astype(jnp.int32).reshape(1)
    loss_vec, grad_x, gw, gs, (sums_early, parts_early) = _local_step(
        x[0], p[0, 0], pos_f, loss_target[0], w_first, small2d, {n: shard_bf[n] for n in GATHER_LATE}, c_idx)

    g4 = [_by_chip(gw[n], *BIG_SPEC[n]) for n in REDUCE_LAST]
    got = _swap_half_rows(g4)
    sums_last = [_add_half_rows(g4[i], got[i], c_idx, "rs_add_halves_" + n) for i, n in enumerate(REDUCE_LAST)]
    parts_last = _scatter_to_chips(sums_last)
    place = jnp.stack([2 * lax.axis_index("x") + lax.axis_index("y"), lax.axis_index("c")]).astype(jnp.int32)
    names = REDUCE_EARLY + REDUCE_LAST
    reduced = _join_half_rows(
        [_add_four(sm_, pt_, place, "rs_add_chips_" + n)
         for n, sm_, pt_ in zip(names, sums_early + sums_last, list(parts_early) + list(parts_last))])
    g_shard = dict(zip(names, reduced))

    small_sum = _allreduce_small(_pack_small(gs, loss_vec))
    loss = small_sum[9, 0]
    g_small = {n: small_sum[i:i + 1, :sz] for i, (n, sz) in enumerate(SMALL)}

    grads, delta, new_m, new_v = {}, {}, {}, {}
    for n, _, _, _ in BIG:
        d, nm, nv = _adamw(shard2d[n], g_shard[n], mom[n][0], var[n][0], "adamw_" + n)
        grads[n], delta[n], new_m[n], new_v[n] = g_shard[n][None], d[None], nm[None], nv[None]
    d, nm, nv = _adamw(_pack_small(small2d), small_sum, _pack_small(mom), _pack_small(var), "adamw_small")
    for i, (n, sz) in enumerate(SMALL):
        grads[n] = g_small[n]
        delta[n], new_m[n], new_v[n] = d[i:i + 1, :sz], nm[i:i + 1, :sz], nv[i:i + 1, :sz]

    return (loss, grad_x[None], *[grads[n] for n in ALL_W], *[delta[n] for n in ALL_W],
            *[new_m[n] for n in ALL_W], *[new_v[n] for n in ALL_W])
```

```python
import functools
import math

import jax
import jax.numpy as jnp
import numpy as np
from jax import lax
from jax.experimental import pallas as pl
from jax.experimental.pallas import tpu as pltpu

F32 = jnp.float32
BF16 = jnp.bfloat16
MESH = pl.DeviceIdType.MESH

D_MODEL = 1024
D_FF = 2816
PLE_DIM = 256
RET_HEADS = 4
RET_DIM = 128
RET_WIDTH = 512
RET_CHUNK = 128
RET_GROUP = 8
MLA_HEADS = 8
MLA_NOPE = 64
MLA_ROPE = 32
MLA_V = 64
Q_LORA = 384
KV_LORA = 256
IN_COLS = 2720
IN_COLS_P = 2816
ROPE_BASE = 10000.0
EPS = 1e-6
SCALE_MLA = 1.0 / math.sqrt(MLA_NOPE + MLA_ROPE)
SCALE_RET = RET_DIM ** -0.5
NEG = -1e30

ADAM_LR = 0.001
ADAM_B1 = 0.9
ADAM_B2 = 0.999
ADAM_EPS = 1e-08
ADAM_WD = 0.01
ADAM_STEP = 10

N_CHIPS = 4
N_DEV = 8
VMEM_MB = 56

BIG = (
    ("w_in", 1024, 2720, 1),
    ("w_uq", 384, 768, 1),
    ("w_ukv", 256, 1024, 1),
    ("w_o", 1024, 1024, 0),
    ("w_gate", 1024, 2816, 1),
    ("w_up", 1024, 2816, 1),
    ("w_down", 2816, 1024, 0),
    ("w_ple_proj", 256, 1024, 1),
    ("w_ple_gate", 1024, 1024, 0),
)
SMALL = (
    ("pre_mix_norm", 1024),
    ("ret_gn_w", 512),
    ("mla_q_norm", 384),
    ("mla_kv_norm", 256),
    ("post_mix_norm", 1024),
    ("pre_ffn_norm", 1024),
    ("post_ffn_norm", 1024),
    ("ple_norm", 1024),
    ("b_ple_gate", 1024),
)
ALL_W = ("pre_mix_norm", "w_in", "ret_gn_w", "mla_q_norm", "w_uq", "mla_kv_norm", "w_ukv", "w_o", "post_mix_norm",
         "pre_ffn_norm", "w_gate", "w_up", "w_down", "post_ffn_norm", "w_ple_proj", "ple_norm", "w_ple_gate", "b_ple_gate")
PACK_COLS = 1024
SMALL_ROWS = 16


def _cp(sem=None, mb=VMEM_MB, **kw):
    return pltpu.CompilerParams(dimension_semantics=sem, vmem_limit_bytes=mb * 1024 * 1024, **kw)


def _bf(x):
    return x.astype(BF16)


def _dot(a, b):
    return jnp.dot(_bf(a), _bf(b), preferred_element_type=F32)


def _dot_nt(a, b):
    return lax.dot_general(_bf(a), _bf(b), (((1,), (1,)), ((), ())), preferred_element_type=F32)


def _dot_tn(a, b):
    return lax.dot_general(_bf(a), _bf(b), (((0,), (0,)), ((), ())), preferred_element_type=F32)


def _sig(x):
    return 1.0 / (1.0 + jnp.exp(-x))


def _rms(x, g):
    r = lax.rsqrt(jnp.mean(x * x, axis=-1, keepdims=True) + EPS)
    return x * r * g


def _rms_bwd(dy, x, g):
    r = lax.rsqrt(jnp.mean(x * x, axis=-1, keepdims=True) + EPS)
    xh = x * r
    dxh = dy * g
    dx = r * (dxh - xh * jnp.mean(dxh * xh, axis=-1, keepdims=True))
    return dx, dy * xh


def _colsum(x):
    return jnp.sum(x, axis=0, keepdims=True)


def _rope_ret(x, cr, sr):
    return x * cr + pltpu.roll(x, 64, 1) * sr


def _unrope_ret(dy, cr, sr):
    return dy * cr + pltpu.roll(dy * sr, 64, 1)


def _rope_mla(x, cm, sa, sb):
    return x * cm + pltpu.roll(x, 112, 1) * sa + pltpu.roll(x, 16, 1) * sb


def _unrope_mla(dy, cm, sa, sb):
    return dy * cm + pltpu.roll(dy * sa, 16, 1) + pltpu.roll(dy * sb, 112, 1)


def _rows(tm, w, col=0):
    return pl.BlockSpec((tm, w), lambda i: (i, col))


def _full(*shape):
    return pl.BlockSpec(shape, lambda i: (0,) * len(shape))


def _sds(shape, dtype):
    return jax.ShapeDtypeStruct(shape, dtype)


def _rope_tables(pos_f, S):
    tm = min(512, S)
    inv_r = (1.0 / (np.float32(ROPE_BASE) ** (np.arange(64, dtype=np.float32) / np.float32(64)))).astype(np.float32)
    inv_m16 = (1.0 / (np.float32(ROPE_BASE) ** (np.arange(16, dtype=np.float32) / np.float32(16)))).astype(np.float32)
    inv_r = np.concatenate([inv_r, inv_r])[None, :]
    inv_m = np.zeros((1, 128), np.float32)
    inv_m[0, 64:80] = inv_m16
    inv_m[0, 80:96] = inv_m16

    def body(pos_ref, invr_ref, invm_ref, cr_ref, sr_ref, cm_ref, sa_ref, sb_ref):
        pos = pos_ref[...]
        lane = lax.broadcasted_iota(jnp.int32, (tm, 128), 1)
        ar = pos * invr_ref[...]
        s = jnp.sin(ar)
        cr_ref[...] = jnp.cos(ar)
        sr_ref[...] = jnp.where(lane < 64, -s, s)
        am = pos * invm_ref[...]
        c2 = jnp.cos(am)
        s2 = jnp.sin(am)
        cm_ref[...] = jnp.where(lane < 64, 1.0, jnp.where(lane < 96, c2, 0.0))
        sa_ref[...] = jnp.where((lane >= 64) & (lane < 80), -s2, 0.0)
        sb_ref[...] = jnp.where((lane >= 80) & (lane < 96), s2, 0.0)

    return pl.pallas_call(
        body, name="rope_tables", grid=(S // tm,),
        in_specs=[_rows(tm, 1), _full(1, 128), _full(1, 128)],
        out_specs=[_rows(tm, 128)] * 5,
        out_shape=[_sds((S, 128), F32)] * 5,
        compiler_params=_cp(("parallel",)),
    )(pos_f, jnp.asarray(inv_r), jnp.asarray(inv_m))


def _inproj(x, g, w_in, tabs, S):
    tm = min(512, S)

    def body(x_ref, g_ref, w_ref, cr_ref, sr_ref, cm_ref, sa_ref, sb_ref,
             xn_ref, rq_ref, rk_ref, rv_ref, rg_ref, cq_ref, ckv_ref, kr_ref):
        xb = _rms(x_ref[...], g_ref[...]).astype(BF16)
        xn_ref[...] = xb
        cr = cr_ref[...]
        sr = sr_ref[...]
        q = jnp.dot(xb, w_ref[:, 0:512], preferred_element_type=F32)
        k = jnp.dot(xb, w_ref[:, 512:1024], preferred_element_type=F32)
        for h in range(RET_HEADS):
            sl = slice(h * 128, (h + 1) * 128)
            rq_ref[:, sl] = _rope_ret(q[:, sl], cr, sr).astype(BF16)
            rk_ref[:, sl] = (_rope_ret(k[:, sl], cr, sr) * SCALE_RET).astype(BF16)
        rv_ref[...] = jnp.dot(xb, w_ref[:, 1024:1536], preferred_element_type=F32).astype(BF16)
        rg_ref[...] = jnp.dot(xb, w_ref[:, 1536:2048], preferred_element_type=F32)
        cq_ref[...] = jnp.dot(xb, w_ref[:, 2048:2432], preferred_element_type=F32)
        ckv_ref[...] = jnp.dot(xb, w_ref[:, 2432:2688], preferred_element_type=F32)
        kr = jnp.dot(xb, w_ref[:, 2688:2816], preferred_element_type=F32)
        kr_ref[...] = _rope_mla(kr, cm_ref[...], sa_ref[...], sb_ref[...])

    return pl.pallas_call(
        body, name="inproj", grid=(S // tm,),
        in_specs=[_rows(tm, D_MODEL), _full(1, D_MODEL), _full(D_MODEL, IN_COLS_P)] + [_rows(tm, 128)] * 5,
        out_specs=[_rows(tm, D_MODEL)] + [_rows(tm, 512)] * 4 + [_rows(tm, Q_LORA), _rows(tm, KV_LORA), _rows(tm, 128)],
        out_shape=[_sds((S, D_MODEL), BF16)] + [_sds((S, 512), BF16)] * 3
        + [_sds((S, 512), F32), _sds((S, Q_LORA), F32), _sds((S, KV_LORA), F32), _sds((S, 128), F32)],
        compiler_params=_cp(("parallel",)),
    )(x, g, w_in, *tabs)


def _mla_up(cq, ckv, kr, gq, gkv, w_uq, w_ukv, tabs, S):
    tm = min(512, S)

    def body(cq_ref, ckv_ref, kr_ref, gq_ref, gkv_ref, wuq_ref, wukv_ref, cm_ref, sa_ref, sb_ref,
             cqn_ref, ckvn_ref, qp_ref, kp_ref, v_ref, kt_ref, vt_ref):
        cm = cm_ref[...]
        sa = sa_ref[...]
        sb = sb_ref[...]
        cqn = _rms(cq_ref[...], gq_ref[...]).astype(BF16)
        cqn_ref[...] = cqn
        ckvn = _rms(ckv_ref[...], gkv_ref[...]).astype(BF16)
        ckvn_ref[...] = ckvn
        qh = jnp.dot(cqn, wuq_ref[...], preferred_element_type=F32)
        kv = jnp.dot(ckvn, wukv_ref[...], preferred_element_type=F32)
        kr_blk = kr_ref[...]
        for h in range(MLA_HEADS):
            sl = slice(h * 128, (h + 1) * 128)
            qp_ref[:, sl] = (_rope_mla(qh[:, sl], cm, sa, sb) * SCALE_MLA).astype(BF16)
            kh = kv[:, sl] + kr_blk
            kp_ref[:, sl] = kh.astype(BF16)
            kt_ref[sl, :] = kh.T.astype(BF16)
        for h in range(MLA_HEADS // 2):
            vh = kv[:, 1024 + h * 128:1024 + (h + 1) * 128]
            v_ref[:, h * 128:(h + 1) * 128] = vh.astype(BF16)
            vt_ref[h * 128:(h + 1) * 128, :] = vh.T.astype(BF16)

    cols = lambda r: pl.BlockSpec((r, tm), lambda i: (0, i))
    return pl.pallas_call(
        body, name="mla_up", grid=(S // tm,),
        in_specs=[_rows(tm, Q_LORA), _rows(tm, KV_LORA), _rows(tm, 128), _full(1, Q_LORA), _full(1, KV_LORA),
                  _full(Q_LORA, 1024), _full(KV_LORA, 1536)] + [_rows(tm, 128)] * 3,
        out_specs=[_rows(tm, Q_LORA), _rows(tm, KV_LORA), _rows(tm, 1024), _rows(tm, 1024), _rows(tm, 512),
                   cols(1024), cols(512)],
        out_shape=[_sds((S, Q_LORA), BF16), _sds((S, KV_LORA), BF16), _sds((S, 1024), BF16), _sds((S, 1024), BF16),
                   _sds((S, 512), BF16), _sds((1024, S), BF16), _sds((512, S), BF16)],
        compiler_params=_cp(("parallel",)),
    )(cq, ckv, kr, gq, gkv, w_uq, w_ukv, *tabs[2:])


def _tri_pairs(nq, k_major):
    if k_major:
        pairs = [(qb, kb) for kb in range(nq) for qb in range(kb, nq)]
    else:
        pairs = [(qb, kb) for qb in range(nq) for kb in range(qb + 1)]
    qb_of = np.array([p[0] for p in pairs], np.int32)
    kb_of = np.array([p[1] for p in pairs], np.int32)
    return jnp.asarray(qb_of), jnp.asarray(kb_of), len(pairs)


ATT_ROWS = 32
FWD_HEADS = 4


def _causal_keep(r0, rows, tq):
    key = r0 + lax.broadcasted_iota(jnp.int32, (rows, tq), 0)
    qry = lax.broadcasted_iota(jnp.int32, (rows, tq), 1)
    return key <= qry


def _flash_fwd(qp, kp, vt, S, shards=()):
    tq = min(512, S)
    nq = S // tq
    RB = ATT_ROWS
    NH = FWD_HEADS
    qb_of, kb_of, T = _tri_pairs(nq, k_major=False)
    n = len(shards)
    steps = (MLA_HEADS // NH) * T

    def body(qb_ref, kb_ref, q_ref, k_ref, vt_ref, *rest):
        w_ins, (o_ref, lse_ref), w_outs = rest[:n], rest[n:n + 2], rest[n + 2:2 * n + 2]
        m_sc, l_sc, acc_sc, s_sc, p_sc = rest[2 * n + 2:2 * n + 7]
        sems = rest[2 * n + 7:]
        t = pl.program_id(1)
        qb = qb_ref[t]
        kb = kb_ref[t]
        lin = pl.program_id(0) * T + t

        if n:
            @pl.when(lin == 0)
            def _():
                _gather_phase(0, w_ins, w_outs, sems)

            @pl.when(lin == steps // 2)
            def _():
                _gather_phase(1, w_ins, w_outs, sems)

        @pl.when(kb == 0)
        def _():
            m_sc[...] = jnp.full(m_sc.shape, NEG, F32)
            l_sc[...] = jnp.zeros(l_sc.shape, F32)
            acc_sc[...] = jnp.zeros(acc_sc.shape, F32)

        def step(masked):
            for a in range(NH):
                sl = slice(a * 128, (a + 1) * 128)
                s_sc[a] = _dot_nt(k_ref[:, sl], q_ref[:, sl])
            m_new, al = [], []
            for a in range(NH):
                mx = [jnp.full((8, tq), NEG, F32) for _ in range(RB // 8)]
                for r in range(0, tq, RB):
                    sc = s_sc[a, r:r + RB, :]
                    if masked:
                        sc = jnp.where(_causal_keep(r, RB, tq), sc, NEG)
                        s_sc[a, r:r + RB, :] = sc
                    for i in range(RB // 8):
                        mx[i] = jnp.maximum(mx[i], sc[i * 8:(i + 1) * 8, :])
                mx8 = jnp.maximum(jnp.maximum(mx[0], mx[1]), jnp.maximum(mx[2], mx[3]))
                m_prev = m_sc[a]
                m_new.append(jnp.maximum(m_prev, jnp.max(mx8, axis=0, keepdims=True)))
                al.append(jnp.exp(m_prev - m_new[a]))
                m_sc[a] = m_new[a]
            for a in range(NH):
                ls = [jnp.zeros((8, tq), F32) for _ in range(RB // 8)]
                for r in range(0, tq, RB):
                    p = jnp.exp(s_sc[a, r:r + RB, :] - m_new[a])
                    for i in range(RB // 8):
                        ls[i] = ls[i] + p[i * 8:(i + 1) * 8, :]
                    p_sc[a, r:r + RB, :] = p.astype(BF16)
                l_sc[a] = al[a] * l_sc[a] + jnp.sum((ls[0] + ls[1]) + (ls[2] + ls[3]), axis=0, keepdims=True)
                pair = slice((a // 2) * 128, (a // 2 + 1) * 128)
                pv = jnp.dot(vt_ref[pair, :], p_sc[a], preferred_element_type=F32)
                rs = slice(a * 64, (a + 1) * 64)
                own = slice((a % 2) * 64, (a % 2 + 1) * 64)
                acc_sc[rs, :] = acc_sc[rs, :] * al[a] + pv[own, :]

        @pl.when(kb < qb)
        def _():
            step(False)

        @pl.when(kb == qb)
        def _():
            step(True)
            for a in range(NH):
                rs = slice(a * 64, (a + 1) * 64)
                acc_sc[rs, :] = acc_sc[rs, :] / l_sc[a]
                lse_ref[a:a + 1, :] = m_sc[a] + jnp.log(l_sc[a])
            o_ref[...] = acc_sc[...].T.astype(BF16)

        if n:
            @pl.when(lin == steps - 1)
            def _():
                _gather_phase(2, w_ins, w_outs, sems)

    grid_spec = pltpu.PrefetchScalarGridSpec(
        num_scalar_prefetch=2, grid=(MLA_HEADS // NH, T),
        in_specs=[pl.BlockSpec((tq, 128 * NH), lambda j, t, qb, kb: (qb[t], j)),
                  pl.BlockSpec((tq, 128 * NH), lambda j, t, qb, kb: (kb[t], j)),
                  pl.BlockSpec((64 * NH, tq), lambda j, t, qb, kb: (j, kb[t]))] + [_ANY] * n,
        out_specs=[pl.BlockSpec((tq, 64 * NH), lambda j, t, qb, kb: (qb[t], j)),
                   pl.BlockSpec((None, NH, tq), lambda j, t, qb, kb: (j, 0, qb[t]))] + [_ANY] * n,
        scratch_shapes=[pltpu.VMEM((NH, 1, tq), F32), pltpu.VMEM((NH, 1, tq), F32), pltpu.VMEM((64 * NH, tq), F32),
                        pltpu.VMEM((NH, tq, tq), F32), pltpu.VMEM((NH, tq, tq), BF16)] + (_gather_sems(n) if n else []),
    )
    out, lse, *gathered = pl.pallas_call(
        body, name="flash_fwd", grid_spec=grid_spec,
        out_shape=[_sds((S, 512), BF16), _sds((MLA_HEADS // NH, NH, S), F32)] + _gather_out_shapes(shards),
        compiler_params=_cp(("arbitrary", "arbitrary")),
    )(qb_of, kb_of, qp, kp, vt, *shards)
    return out, lse.reshape(MLA_HEADS // 2, 2, S), gathered


def _decay_table():
    log_g = np.log(1.0 - 2.0 ** (-5.0 - np.arange(RET_HEADS, dtype=np.float32))).astype(np.float32)
    return jnp.asarray(np.broadcast_to(log_g[:, None, None], (RET_HEADS, 8, 128)).copy())


def _decay_terms(lg_ref):
    C = RET_CHUNK
    lg = lg_ref[0:1, :]
    row = lax.broadcasted_iota(jnp.int32, (C, C), 0)
    col = lax.broadcasted_iota(jnp.int32, (C, C), 1)
    diff = (row - col).astype(F32)
    dmat = jnp.where(diff >= 0, jnp.exp(jnp.maximum(diff, 0.0) * lg), 0.0)
    j = lax.broadcasted_iota(jnp.int32, (C, 1), 0).astype(F32)
    lg1 = lg[:, 0:1]
    zeta = jnp.exp((C - 1 - j) * lg1)
    xi = jnp.exp((j + 1.0) * lg1)
    g_chunk = jnp.exp(C * lg1)
    return dmat, zeta, xi, g_chunk


def _ret_fwd(rq, rk, rv, rg, gn_w, S):
    C = RET_CHUNK
    N = S // C
    G = min(RET_GROUP, N)
    NB = N // G

    def body(lg_ref, q_ref, k_ref, v_ref, rg_ref, w_ref, ry_ref, ro_ref, rprev_ref, r_sc):
        @pl.when(pl.program_id(1) == 0)
        def _():
            r_sc[...] = jnp.zeros(r_sc.shape, F32)

        dmat, zeta, xi, g_chunk = _decay_terms(lg_ref)
        w = w_ref[...]
        r = r_sc[...]
        for i in range(G):
            rows = slice(i * C, (i + 1) * C)
            q = q_ref[rows, :]
            k = k_ref[rows, :]
            v = v_ref[rows, :]
            r_prev = r.astype(BF16)
            rprev_ref[i] = r_prev
            sc = _dot_nt(q, k) * dmat
            ry = _dot(sc, v) + jnp.dot(q, r_prev, preferred_element_type=F32) * xi
            ry_ref[rows, :] = ry
            r = g_chunk * r + _dot_tn(k, zeta * v.astype(F32))
            mu = jnp.mean(ry, axis=-1, keepdims=True)
            yc = ry - mu
            yh = yc * lax.rsqrt(jnp.mean(yc * yc, axis=-1, keepdims=True) + EPS)
            g = rg_ref[rows, :]
            ro_ref[rows, :] = (g * _sig(g) * (yh * w)).astype(BF16)
        r_sc[...] = r

    blk = pl.BlockSpec((G * C, 128), lambda h, n: (n, h))
    return pl.pallas_call(
        body, name="ret_fwd", grid=(RET_HEADS, NB),
        in_specs=[pl.BlockSpec((None, 8, 128), lambda h, n: (h, 0, 0)), blk, blk, blk, blk,
                  pl.BlockSpec((1, 128), lambda h, n: (0, h))],
        out_specs=[blk, blk, pl.BlockSpec((G, 128, 128), lambda h, n: (h * NB + n, 0, 0))],
        out_shape=[_sds((S, 512), F32), _sds((S, 512), BF16), _sds((RET_HEADS * N, 128, 128), BF16)],
        scratch_shapes=[pltpu.VMEM((128, 128), F32)],
        compiler_params=_cp(("parallel", "arbitrary")),
    )(_decay_table(), rq, rk, rv, rg, gn_w)


def _outproj(ro, mo, x, w_o, g_post, g_pre, S):
    tm = min(512, S)

    def body(ro_ref, mo_ref, x_ref, wo_ref, g1_ref, g2_ref, mix_ref, h1_ref, hn_ref):
        mix = (jnp.dot(ro_ref[...], wo_ref[0:512, :], preferred_element_type=F32)
               + jnp.dot(mo_ref[...], wo_ref[512:1024, :], preferred_element_type=F32))
        mix_ref[...] = mix.astype(BF16)
        h1 = x_ref[...] + _rms(mix, g1_ref[...])
        h1_ref[...] = h1
        hn_ref[...] = _rms(h1, g2_ref[...]).astype(BF16)

    return pl.pallas_call(
        body, name="outproj", grid=(S // tm,),
        in_specs=[_rows(tm, 512), _rows(tm, 512), _rows(tm, D_MODEL), _full(D_MODEL, D_MODEL), _full(1, D_MODEL),
                  _full(1, D_MODEL)],
        out_specs=[_rows(tm, D_MODEL)] * 3,
        out_shape=[_sds((S, D_MODEL), BF16), _sds((S, D_MODEL), F32), _sds((S, D_MODEL), BF16)],
        compiler_params=_cp(("parallel",)),
    )(ro, mo, x, w_o, g_post, g_pre)


def _ffn_up(hn, w_gate, w_up, S):
    tm = min(512, S)
    tn = D_FF // 2

    def body(hn_ref, wg_ref, wu_ref, gate_ref, up_ref, act_ref):
        hn_b = hn_ref[...]
        g = jnp.dot(hn_b, wg_ref[...], preferred_element_type=F32)
        u = jnp.dot(hn_b, wu_ref[...], preferred_element_type=F32)
        gate_ref[...] = g.astype(BF16)
        up_ref[...] = u.astype(BF16)
        act_ref[...] = (g * _sig(g) * u).astype(BF16)

    wspec = pl.BlockSpec((D_MODEL, tn), lambda j, i: (0, j))
    ospec = pl.BlockSpec((tm, tn), lambda j, i: (i, j))
    return pl.pallas_call(
        body, name="ffn_up", grid=(2, S // tm),
        in_specs=[pl.BlockSpec((tm, D_MODEL), lambda j, i: (i, 0)), wspec, wspec],
        out_specs=[ospec] * 3, out_shape=[_sds((S, D_FF), BF16)] * 3,
        compiler_params=_cp(("parallel", "parallel")),
    )(hn, w_gate, w_up)


def _ffn_down(act, w_down, h1, g, S):
    tm = min(512, S)

    def body(act_ref, wd_ref, h1_ref, g_ref, ff_ref, h2_ref):
        ff = jnp.dot(act_ref[...], wd_ref[...], preferred_element_type=F32)
        ff_ref[...] = ff.astype(BF16)
        h2_ref[...] = h1_ref[...] + _rms(ff, g_ref[...])

    return pl.pallas_call(
        body, name="ffn_down", grid=(S // tm,),
        in_specs=[_rows(tm, D_FF), _full(D_FF, D_MODEL), _rows(tm, D_MODEL), _full(1, D_MODEL)],
        out_specs=[_rows(tm, D_MODEL)] * 2, out_shape=[_sds((S, D_MODEL), BF16), _sds((S, D_MODEL), F32)],
        compiler_params=_cp(("parallel",)),
    )(act, w_down, h1, g)


def _ple_loss(p, h2, tgt, w_pp, w_pg, b_pg, g_ple, S):
    tm = min(256, S)

    def body(p_ref, h2_ref, t_ref, wp_ref, wg_ref, b_ref, gp_ref,
             dz_ref, dpe_ref, dh2_ref, h2b_ref, loss_ref, dgp_ref, db_ref):
        @pl.when(pl.program_id(0) == 0)
        def _():
            loss_ref[...] = jnp.zeros(loss_ref.shape, F32)
            dgp_ref[...] = jnp.zeros(dgp_ref.shape, F32)
            db_ref[...] = jnp.zeros(db_ref.shape, F32)

        gp = gp_ref[...]
        pe = _dot(p_ref[...], wp_ref[...])
        r = lax.rsqrt(jnp.mean(pe * pe, axis=-1, keepdims=True) + EPS)
        peh = pe * r
        e = peh * gp
        h2 = h2_ref[...]
        h2b = h2.astype(BF16)
        h2b_ref[...] = h2b
        gt = _sig(jnp.dot(h2b, wg_ref[...], preferred_element_type=F32) + b_ref[...])
        diff = h2 + e * gt - t_ref[...]
        loss_ref[...] += _colsum(diff * diff)
        dh3 = diff * (1.0 / D_MODEL)
        de = dh3 * gt
        dz = dh3 * e * gt * (1.0 - gt)
        db_ref[...] += _colsum(dz)
        dgp_ref[...] += _colsum(de * peh)
        dpeh = de * gp
        dpe = r * (dpeh - peh * jnp.mean(dpeh * peh, axis=-1, keepdims=True))
        dzb = dz.astype(BF16)
        dz_ref[...] = dzb
        dpe_ref[...] = dpe.astype(BF16)
        dh2_ref[...] = dh3 + _dot_nt(dzb, wg_ref[...])

    return pl.pallas_call(
        body, name="ple_loss", grid=(S // tm,),
        in_specs=[_rows(tm, PLE_DIM), _rows(tm, D_MODEL), _rows(tm, D_MODEL), _full(PLE_DIM, D_MODEL),
                  _full(D_MODEL, D_MODEL), _full(1, D_MODEL), _full(1, D_MODEL)],
        out_specs=[_rows(tm, D_MODEL)] * 4 + [_full(1, D_MODEL)] * 3,
        out_shape=[_sds((S, D_MODEL), BF16), _sds((S, D_MODEL), BF16), _sds((S, D_MODEL), F32), _sds((S, D_MODEL), BF16)]
        + [_sds((1, D_MODEL), F32)] * 3,
        compiler_params=_cp(("arbitrary",)),
    )(p, h2, tgt, w_pp, w_pg, b_pg, g_ple)


def _wgrad(a, b, name, S):
    M = a.shape[1]
    N = b.shape[1]
    ts = min(2048, S)
    nsplit = 2 if M * N >= 2 * 1024 * 1024 else 1
    tn = N // nsplit

    def body(a_ref, b_ref, o_ref):
        @pl.when(pl.program_id(1) == 0)
        def _():
            o_ref[...] = jnp.zeros(o_ref.shape, F32)

        o_ref[...] += _dot_tn(a_ref[...], b_ref[...])

    return pl.pallas_call(
        body, name=name, grid=(nsplit, S // ts),
        in_specs=[pl.BlockSpec((ts, M), lambda j, s: (s, 0)), pl.BlockSpec((ts, tn), lambda j, s: (s, j))],
        out_specs=pl.BlockSpec((M, tn), lambda j, s: (0, j)), out_shape=_sds((M, N), F32),
        compiler_params=_cp(("parallel", "arbitrary")),
    )(a, b)


def _ffn_down_bwd(dh2, ff, g, w_down, gate, up, S):
    tm = min(256, S)
    tn = D_FF // 2

    def body(dh2_ref, ff_ref, g_ref, wd_ref, gate_ref, up_ref, dff_ref, dgate_ref, dup_ref, dg_ref):
        @pl.when(pl.program_id(0) == 0)
        def _():
            dg_ref[...] = jnp.zeros(dg_ref.shape, F32)

        dff, ga = _rms_bwd(dh2_ref[...], ff_ref[...].astype(F32), g_ref[...])
        dg_ref[...] += _colsum(ga)
        dffb = dff.astype(BF16)
        dff_ref[...] = dffb
        for seg in range(2):
            sl = slice(seg * tn, (seg + 1) * tn)
            dact = _dot_nt(dffb, wd_ref[sl, :])
            gt = gate_ref[:, sl].astype(F32)
            u = up_ref[:, sl].astype(F32)
            s = _sig(gt)
            dgate_ref[:, sl] = (dact * u * (s * (1.0 + gt * (1.0 - s)))).astype(BF16)
            dup_ref[:, sl] = (dact * (gt * s)).astype(BF16)

    return pl.pallas_call(
        body, name="ffn_down_bwd", grid=(S // tm,),
        in_specs=[_rows(tm, D_MODEL), _rows(tm, D_MODEL), _full(1, D_MODEL), _full(D_FF, D_MODEL), _rows(tm, D_FF),
                  _rows(tm, D_FF)],
        out_specs=[_rows(tm, D_MODEL), _rows(tm, D_FF), _rows(tm, D_FF), _full(1, D_MODEL)],
        out_shape=[_sds((S, D_MODEL), BF16), _sds((S, D_FF), BF16), _sds((S, D_FF), BF16), _sds((1, D_MODEL), F32)],
        compiler_params=_cp(("arbitrary",)),
    )(dh2, ff, g, w_down, gate, up)


def _ffn_up_bwd(dgate, dup, w_gate, w_up, h1, mix, dh2, g_pre, g_post, w_o, S, grads=()):
    tm = min(256, S)
    n = len(grads)
    last = S // tm - 1

    def body(dgate_ref, dup_ref, wg_ref, wu_ref, h1_ref, mix_ref, dh2_ref, g2_ref, g1_ref, wo_ref, *rest):
        g_ins = rest[:n]
        dh1_ref, dmix_ref, dro_ref, dmo_ref, dg2_ref, dg1_ref = rest[n:n + 6]
        g_outs, sems = rest[n + 6:2 * n + 6], rest[2 * n + 6:]

        @pl.when(pl.program_id(0) == 0)
        def _():
            dg2_ref[...] = jnp.zeros(dg2_ref.shape, F32)
            dg1_ref[...] = jnp.zeros(dg1_ref.shape, F32)
            for cp in (_swap_copies(g_ins, g_outs, sems) if n else []):
                cp.start()

        dhn = _dot_nt(dgate_ref[...], wg_ref[...]) + _dot_nt(dup_ref[...], wu_ref[...])
        d1, ga = _rms_bwd(dhn, h1_ref[...], g2_ref[...])
        dg2_ref[...] += _colsum(ga)
        dh1 = dh2_ref[...] + d1
        dh1_ref[...] = dh1
        dmix, gb = _rms_bwd(dh1, mix_ref[...].astype(F32), g1_ref[...])
        dg1_ref[...] += _colsum(gb)
        dmixb = dmix.astype(BF16)
        dmix_ref[...] = dmixb
        dcat = _dot_nt(dmixb, wo_ref[...])
        dro_ref[...] = dcat[:, 0:512].astype(BF16)
        dmo_ref[...] = dcat[:, 512:1024].astype(BF16)

        if n:
            @pl.when(pl.program_id(0) == last)
            def _():
                for cp in _swap_copies(g_ins, g_outs, sems):
                    cp.wait()

    dh1, dmix, dro, dmo, dg2, dg1, *got = pl.pallas_call(
        body, name="ffn_up_bwd", grid=(S // tm,),
        in_specs=[_rows(tm, D_FF), _rows(tm, D_FF), _full(D_MODEL, D_FF), _full(D_MODEL, D_FF), _rows(tm, D_MODEL),
                  _rows(tm, D_MODEL), _rows(tm, D_MODEL), _full(1, D_MODEL), _full(1, D_MODEL), _full(D_MODEL, D_MODEL)]
        + [_ANY] * n,
        out_specs=[_rows(tm, D_MODEL), _rows(tm, D_MODEL), _rows(tm, 512), _rows(tm, 512), _full(1, D_MODEL),
                   _full(1, D_MODEL)] + [_ANY] * n,
        out_shape=[_sds((S, D_MODEL), F32), _sds((S, D_MODEL), BF16), _sds((S, 512), BF16), _sds((S, 512), BF16),
                   _sds((1, D_MODEL), F32), _sds((1, D_MODEL), F32)] + _swap_out_shapes(grads),
        scratch_shapes=_swap_sems(n) if n else [],
        compiler_params=_cp(("arbitrary",)),
    )(dgate, dup, w_gate, w_up, h1, mix, dh2, g_pre, g_post, w_o, *grads)
    return dh1, dmix, dro, dmo, dg2, dg1, got


def _attn_delta(o, do, S):
    tm = min(512, S)

    def body(o_ref, do_ref, dot_ref, d_ref):
        do = do_ref[...].astype(F32)
        prod_t = (o_ref[...].astype(F32) * do).T
        dot_ref[...] = do.T.astype(BF16)
        for h in range(MLA_HEADS):
            d_ref[h // 2, (h % 2):(h % 2) + 1, :] = jnp.sum(prod_t[h * 64:(h + 1) * 64, :], axis=0, keepdims=True)

    return pl.pallas_call(
        body, name="attn_delta", grid=(S // tm,),
        in_specs=[_rows(tm, 512), _rows(tm, 512)],
        out_specs=[pl.BlockSpec((512, tm), lambda i: (0, i)), pl.BlockSpec((MLA_HEADS // 2, 2, tm), lambda i: (0, 0, i))],
        out_shape=[_sds((512, S), BF16), _sds((MLA_HEADS // 2, 2, S), F32)],
        compiler_params=_cp(("parallel",)),
    )(o, do)


def _flash_bwd(qp, kp, kt, v, do, dot, lse, delta, S, sums=()):
    tq = min(512, S)
    nq = S // tq
    RB = ATT_ROWS
    qb_of, kb_of, T = _tri_pairs(nq, k_major=True)
    n = len(sums)
    steps = (MLA_HEADS // 2) * T

    def body(qb_ref, kb_ref, q_ref, k_ref, kt_ref, v_ref, do_ref, dot_ref, lse_ref, dl_ref, *rest):
        g_ins, (dq_ref, dk_ref, dv_ref), g_outs = rest[:n], rest[n:n + 3], rest[n + 3:2 * n + 3]
        dk_sc, dv_sc, s_sc, dp_sc, p_sc, ds_sc = rest[2 * n + 3:2 * n + 9]
        sems = rest[2 * n + 9:]
        t = pl.program_id(1)
        qb = qb_ref[t]
        kb = kb_ref[t]
        lin = pl.program_id(0) * T + t

        if n:
            @pl.when(lin == 0)
            def _():
                for cp in _scatter_copies(g_ins, g_outs, sems):
                    cp.start()

        @pl.when(t == 0)
        def _():
            dq_ref[...] = jnp.zeros(dq_ref.shape, F32)

        @pl.when(qb == kb)
        def _():
            dk_sc[...] = jnp.zeros(dk_sc.shape, F32)
            dv_sc[...] = jnp.zeros(dv_sc.shape, F32)

        lane = lax.broadcasted_iota(jnp.int32, (tq, 128), 1)

        def step(masked):
            vv = v_ref[...]
            do_all = do_ref[...]
            mine = [lane < 64, lane >= 64]
            for a in range(2):
                sl = slice(a * 128, (a + 1) * 128)
                s_sc[a] = _dot_nt(k_ref[:, sl], q_ref[:, sl])
                dp_sc[a] = jnp.dot(jnp.where(mine[a], vv, jnp.zeros_like(vv)), dot_ref[...],
                                   preferred_element_type=F32)
            for a in range(2):
                sl = slice(a * 128, (a + 1) * 128)
                lse = lse_ref[a:a + 1, :]
                dl = dl_ref[a:a + 1, :]
                for r in range(0, tq, RB):
                    sc = s_sc[a, r:r + RB, :]
                    if masked:
                        sc = jnp.where(_causal_keep(r, RB, tq), sc, NEG)
                    p = jnp.exp(sc - lse)
                    p_sc[a, r:r + RB, :] = p.astype(BF16)
                    ds_sc[a, r:r + RB, :] = (p * (dp_sc[a, r:r + RB, :] - dl)).astype(BF16)
                ds = ds_sc[a]
                dv_sc[...] += jnp.dot(p_sc[a], jnp.where(mine[a], do_all, jnp.zeros_like(do_all)),
                                      preferred_element_type=F32)
                dk_sc[:, sl] += jnp.dot(ds, q_ref[:, sl], preferred_element_type=F32)
                dq_ref[qb, sl, :] += jnp.dot(kt_ref[sl, :], ds, preferred_element_type=F32)

        @pl.when(qb > kb)
        def _():
            step(False)

        @pl.when(qb == kb)
        def _():
            step(True)

        @pl.when(qb == nq - 1)
        def _():
            dk_ref[...] = dk_sc[...].astype(BF16)
            dv_ref[...] = dv_sc[...].astype(BF16)

        if n:
            @pl.when(lin == steps - 1)
            def _():
                for cp in _scatter_copies(g_ins, g_outs, sems):
                    cp.wait()

    grid_spec = pltpu.PrefetchScalarGridSpec(
        num_scalar_prefetch=2, grid=(MLA_HEADS // 2, T),
        in_specs=[pl.BlockSpec((tq, 256), lambda j, t, qb, kb: (qb[t], j)),
                  pl.BlockSpec((tq, 256), lambda j, t, qb, kb: (kb[t], j)),
                  pl.BlockSpec((256, tq), lambda j, t, qb, kb: (j, kb[t])),
                  pl.BlockSpec((tq, 128), lambda j, t, qb, kb: (kb[t], j)),
                  pl.BlockSpec((tq, 128), lambda j, t, qb, kb: (qb[t], j)),
                  pl.BlockSpec((128, tq), lambda j, t, qb, kb: (j, qb[t])),
                  pl.BlockSpec((None, 2, tq), lambda j, t, qb, kb: (j, 0, qb[t])),
                  pl.BlockSpec((None, 2, tq), lambda j, t, qb, kb: (j, 0, qb[t]))] + [_ANY] * n,
        out_specs=[pl.BlockSpec((nq, 256, tq), lambda j, t, qb, kb: (0, j, 0)),
                   pl.BlockSpec((tq, 256), lambda j, t, qb, kb: (kb[t], j)),
                   pl.BlockSpec((tq, 128), lambda j, t, qb, kb: (kb[t], j))] + [_ANY] * n,
        scratch_shapes=[pltpu.VMEM((tq, 256), F32), pltpu.VMEM((tq, 128), F32), pltpu.VMEM((2, tq, tq), F32),
                        pltpu.VMEM((2, tq, tq), F32), pltpu.VMEM((2, tq, tq), BF16), pltpu.VMEM((2, tq, tq), BF16)]
        + (_scatter_sems(n) if n else []),
    )
    dq, dk, dv, *parts = pl.pallas_call(
        body, name="flash_bwd", grid_spec=grid_spec,
        out_shape=[_sds((nq, 1024, tq), F32), _sds((S, 1024), BF16), _sds((S, 512), BF16)] + _scatter_out_shapes(sums),
        compiler_params=_cp(("arbitrary", "arbitrary")),
    )(qb_of, kb_of, qp, kp, kt, v, do, dot, lse, delta, *sums)
    return dq, dk, dv, parts


def _mla_up_bwd(dqp, dkp, dv, cq, ckv, gq, gkv, w_uq, w_ukv, tabs, S):
    tm = min(512, S)

    def body(dq_ref, dk_ref, dv_ref, cq_ref, ckv_ref, gq_ref, gkv_ref, wuq_ref, wukv_ref, cm_ref, sa_ref, sb_ref,
             dqh_ref, dkv_ref, dcq_ref, dckv_ref, dkr_ref, dgq_ref, dgkv_ref):
        @pl.when(pl.program_id(0) == 0)
        def _():
            dgq_ref[...] = jnp.zeros(dgq_ref.shape, F32)
            dgkv_ref[...] = jnp.zeros(dgkv_ref.shape, F32)

        cm = cm_ref[...]
        sa = sa_ref[...]
        sb = sb_ref[...]
        lane = lax.broadcasted_iota(jnp.int32, (tm, 128), 1)
        dkr_r = jnp.zeros((tm, 128), F32)
        for h in range(MLA_HEADS):
            sl = slice(h * 128, (h + 1) * 128)
            dqh_ref[:, sl] = (_unrope_mla(dq_ref[sl, :].T, cm, sa, sb) * SCALE_MLA).astype(BF16)
            gk = dk_ref[:, sl]
            dkr_r = dkr_r + gk.astype(F32)
            dkv_ref[:, sl] = gk
        dkr_r = jnp.where((lane >= 64) & (lane < 96), dkr_r, 0.0)
        dkr_ref[...] = _unrope_mla(dkr_r, cm, sa, sb).astype(BF16)
        dkv_ref[:, 1024:1536] = dv_ref[...]
        dcq, ga = _rms_bwd(_dot_nt(dqh_ref[...], wuq_ref[...]), cq_ref[...], gq_ref[...])
        dcq_ref[...] = dcq.astype(BF16)
        dgq_ref[...] += _colsum(ga)
        dckv, gb = _rms_bwd(_dot_nt(dkv_ref[...], wukv_ref[...]), ckv_ref[...], gkv_ref[...])
        dckv_ref[...] = dckv.astype(BF16)
        dgkv_ref[...] += _colsum(gb)

    per_q = dqp.shape[2] // tm
    return pl.pallas_call(
        body, name="mla_up_bwd", grid=(S // tm,),
        in_specs=[pl.BlockSpec((None, 1024, tm), lambda i: (i // per_q, 0, i % per_q)),
                  _rows(tm, 1024), _rows(tm, 512), _rows(tm, Q_LORA), _rows(tm, KV_LORA),
                  _full(1, Q_LORA), _full(1, KV_LORA), _full(Q_LORA, 1024), _full(KV_LORA, 1536)] + [_rows(tm, 128)] * 3,
        out_specs=[_rows(tm, 1024), _rows(tm, 1536), _rows(tm, Q_LORA), _rows(tm, KV_LORA), _rows(tm, 128),
                   _full(1, Q_LORA), _full(1, KV_LORA)],
        out_shape=[_sds((S, 1024), BF16), _sds((S, 1536), BF16), _sds((S, Q_LORA), BF16), _sds((S, KV_LORA), BF16),
                   _sds((S, 128), BF16), _sds((1, Q_LORA), F32), _sds((1, KV_LORA), F32)],
        compiler_params=_cp(("arbitrary",)),
    )(dqp, dkp, dv, cq, ckv, gq, gkv, w_uq, w_ukv, *tabs[2:])


def _ret_bwd(rq, rk, rv, rprev, ry, rg, dro, gn_w, tabs, S):
    C = RET_CHUNK
    N = S // C
    G = min(RET_GROUP, N)
    NB = N // G

    def body(lg_ref, q_ref, k_ref, v_ref, rp_ref, ry_ref, rg_ref, dro_ref, w_ref, cr_ref, sr_ref,
             drq_ref, drk_ref, drv_ref, drg_ref, dw_ref, g_sc):
        @pl.when(pl.program_id(1) == 0)
        def _():
            g_sc[...] = jnp.zeros(g_sc.shape, F32)
            dw_ref[...] = jnp.zeros(dw_ref.shape, F32)

        dmat, zeta, xi, g_chunk = _decay_terms(lg_ref)
        w = w_ref[...]
        gacc = g_sc[...]
        dw = jnp.zeros((1, 128), F32)
        for i in reversed(range(G)):
            rows = slice(i * C, (i + 1) * C)
            ry = ry_ref[rows, :]
            mu = jnp.mean(ry, axis=-1, keepdims=True)
            yc = ry - mu
            rstd = lax.rsqrt(jnp.mean(yc * yc, axis=-1, keepdims=True) + EPS)
            yh = yc * rstd
            g = rg_ref[rows, :]
            s = _sig(g)
            dout = dro_ref[rows, :].astype(F32)
            drg_ref[rows, :] = (dout * (yh * w) * (s * (1.0 + g * (1.0 - s)))).astype(BF16)
            dgn = dout * (g * s)
            dw = dw + _colsum(dgn * yh)
            dyh = dgn * w
            dry = rstd * (dyh - jnp.mean(dyh, axis=-1, keepdims=True) - yh * jnp.mean(dyh * yh, axis=-1, keepdims=True))
            do = dry.astype(BF16)

            q = q_ref[rows, :]
            k = k_ref[rows, :]
            v = v_ref[rows, :]
            gfut = gacc.astype(BF16)
            sc = (_dot_nt(q, k) * dmat).astype(BF16)
            dsc = (_dot_nt(do, v) * dmat).astype(BF16)
            dq = jnp.dot(dsc, k, preferred_element_type=F32) + _dot_nt(do, rp_ref[i]) * xi
            dk = _dot_tn(dsc, q) + _dot_nt(v, gfut) * zeta
            dv = _dot_tn(sc, do) + jnp.dot(k, gfut, preferred_element_type=F32) * zeta
            gacc = g_chunk * gacc + _dot_tn(q, xi * dry)
            cr = cr_ref[rows, :]
            sr = sr_ref[rows, :]
            drq_ref[rows, :] = _unrope_ret(dq, cr, sr).astype(BF16)
            drk_ref[rows, :] = _unrope_ret(dk * SCALE_RET, cr, sr).astype(BF16)
            drv_ref[rows, :] = dv.astype(BF16)
        g_sc[...] = gacc
        dw_ref[...] += dw

    blk = pl.BlockSpec((G * C, 128), lambda h, n: (NB - 1 - n, h))
    tab = pl.BlockSpec((G * C, 128), lambda h, n: (NB - 1 - n, 0))
    return pl.pallas_call(
        body, name="ret_bwd", grid=(RET_HEADS, NB),
        in_specs=[pl.BlockSpec((None, 8, 128), lambda h, n: (h, 0, 0)), blk, blk, blk,
                  pl.BlockSpec((G, 128, 128), lambda h, n: (h * NB + NB - 1 - n, 0, 0)), blk, blk, blk,
                  pl.BlockSpec((1, 128), lambda h, n: (0, h)), tab, tab],
        out_specs=[blk, blk, blk, blk, pl.BlockSpec((1, 128), lambda h, n: (0, h))],
        out_shape=[_sds((S, 512), BF16)] * 4 + [_sds((1, 512), F32)],
        scratch_shapes=[pltpu.VMEM((128, 128), F32)],
        compiler_params=_cp(("parallel", "arbitrary")),
    )(_decay_table(), rq, rk, rv, rprev, ry, rg, dro, gn_w, tabs[0], tabs[1])


def _inproj_bwd(drq, drk, drv, drg, dcq, dckv, dkr, w_in, dh1, x, g, S):
    tm = min(512, S)

    def body(drq_ref, drk_ref, drv_ref, drg_ref, dcq_ref, dckv_ref, dkr_ref, w_ref, dh1_ref, x_ref, g_ref,
             gx_ref, dproj_ref, dg_ref):
        @pl.when(pl.program_id(0) == 0)
        def _():
            dg_ref[...] = jnp.zeros(dg_ref.shape, F32)

        dproj_ref[:, 0:512] = drq_ref[...]
        dproj_ref[:, 512:1024] = drk_ref[...]
        dproj_ref[:, 1024:1536] = drv_ref[...]
        dproj_ref[:, 1536:2048] = drg_ref[...]
        dproj_ref[:, 2048:2432] = dcq_ref[...]
        dproj_ref[:, 2432:2688] = dckv_ref[...]
        dproj_ref[:, 2688:2816] = dkr_ref[...]
        dx, ga = _rms_bwd(_dot_nt(dproj_ref[...], w_ref[...]), x_ref[...], g_ref[...])
        gx_ref[...] = dh1_ref[...] + dx
        dg_ref[...] += _colsum(ga)

    return pl.pallas_call(
        body, name="inproj_bwd", grid=(S // tm,),
        in_specs=[_rows(tm, 512)] * 4 + [_rows(tm, Q_LORA), _rows(tm, KV_LORA), _rows(tm, 128),
                                         _full(D_MODEL, IN_COLS_P), _rows(tm, D_MODEL), _rows(tm, D_MODEL),
                                         _full(1, D_MODEL)],
        out_specs=[_rows(tm, D_MODEL), _rows(tm, IN_COLS_P), _full(1, D_MODEL)],
        out_shape=[_sds((S, D_MODEL), F32), _sds((S, IN_COLS_P), BF16), _sds((1, D_MODEL), F32)],
        compiler_params=_cp(("arbitrary",)),
    )(drq, drk, drv, drg, dcq, dckv, dkr, w_in, dh1, x, g)


def _pad_weights(w):
    w_in = w["w_in"]
    z = lambda r, c: jnp.zeros((r, c), BF16)
    w_in_p = jnp.concatenate([w_in[:, :2688], z(1024, 64), w_in[:, 2688:2720], z(1024, 32)], axis=1)
    w_uq_p = jnp.pad(w["w_uq"].reshape(Q_LORA, MLA_HEADS, 96), ((0, 0), (0, 0), (0, 32))).reshape(Q_LORA, 1024)
    ukv = w["w_ukv"].reshape(KV_LORA, MLA_HEADS, 128)
    k_part = jnp.pad(ukv[:, :, :64], ((0, 0), (0, 0), (0, 64))).reshape(KV_LORA, 1024)
    w_ukv_p = jnp.concatenate([k_part, ukv[:, :, 64:].reshape(KV_LORA, 512)], axis=1)
    return w_in_p, w_uq_p, w_ukv_p


BIG_SPEC = {n: (r, c, ax) for n, r, c, ax in BIG}
GATHER_FIRST = ("w_in", "w_uq", "w_ukv")
GATHER_LATE = tuple(n for n, _, _, _ in BIG if n not in GATHER_FIRST)
REDUCE_EARLY = ("w_ple_gate", "w_ple_proj", "w_down", "w_gate", "w_up")
REDUCE_LAST = tuple(n for n, _, _, _ in BIG if n not in REDUCE_EARLY)


def _local_step(x, p, pos_f, tgt, w, sm, late_shards=None, c_idx=None):
    S = x.shape[0]
    spread = late_shards is not None
    w = dict(w)
    w_in_p, w_uq_p, w_ukv_p = _pad_weights(w)
    tabs = _rope_tables(pos_f, S)

    xn, rq, rk, rv, rg, cq, ckv, kr = _inproj(x, sm["pre_mix_norm"], w_in_p, tabs, S)
    cqn, ckvn, qp, kp, v, kt, vt = _mla_up(cq, ckv, kr, sm["mla_q_norm"], sm["mla_kv_norm"], w_uq_p, w_ukv_p, tabs, S)
    mo, lse, gathered = _flash_fwd(qp, kp, vt, S, [late_shards[n] for n in GATHER_LATE] if spread else ())
    for i, n in enumerate(GATHER_LATE if spread else ()):
        w[n] = _from_chips(gathered[i], BIG_SPEC[n][2])
    ry, ro, rprev = _ret_fwd(rq, rk, rv, rg, sm["ret_gn_w"], S)
    mix, h1, hn = _outproj(ro, mo, x, w["w_o"], sm["post_mix_norm"], sm["pre_ffn_norm"], S)
    gate, up, act = _ffn_up(hn, w["w_gate"], w["w_up"], S)
    ff, h2 = _ffn_down(act, w["w_down"], h1, sm["post_ffn_norm"], S)
    dz, dpe, dh2, h2b, loss_vec, d_ple_norm, d_b = _ple_loss(
        p, h2, tgt, w["w_ple_proj"], w["w_ple_gate"], sm["b_ple_gate"], sm["ple_norm"], S)

    gw = {}
    gs = {"ple_norm": d_ple_norm, "b_ple_gate": d_b}
    gw["w_ple_gate"] = _wgrad(h2b, dz, "wgrad_ple_gate", S)
    gw["w_ple_proj"] = _wgrad(p, dpe, "wgrad_ple_proj", S)
    dff, dgate, dup, gs["post_ffn_norm"] = _ffn_down_bwd(dh2, ff, sm["post_ffn_norm"], w["w_down"], gate, up, S)
    gw["w_down"] = _wgrad(act, dff, "wgrad_down", S)
    gw["w_gate"] = _wgrad(hn, dgate, "wgrad_gate", S)
    gw["w_up"] = _wgrad(hn, dup, "wgrad_up", S)
    g4 = [_by_chip(gw.pop(n), *BIG_SPEC[n]) for n in REDUCE_EARLY] if spread else []
    dh1, dmix, dro, dmo, gs["pre_ffn_norm"], gs["post_mix_norm"], got = _ffn_up_bwd(
        dgate, dup, w["w_gate"], w["w_up"], h1, mix, dh2, sm["pre_ffn_norm"], sm["post_mix_norm"], w["w_o"], S, g4)
    sums = [_add_half_rows(g4[i], got[i], c_idx, "rs_add_halves_" + n) for i, n in enumerate(REDUCE_EARLY)] if spread else []
    gw["w_o"] = jnp.concatenate([_wgrad(ro, dmix, "wgrad_o_ret", S), _wgrad(mo, dmix, "wgrad_o_mla", S)], axis=0)

    dmo_t, delta = _attn_delta(mo, dmo, S)
    dqp, dkp, dv, parts = _flash_bwd(qp, kp, kt, v, dmo, dmo_t, lse, delta, S, sums)
    dqh, dkv, dcq, dckv, dkr, gs["mla_q_norm"], gs["mla_kv_norm"] = _mla_up_bwd(
        dqp, dkp, dv, cq, ckv, sm["mla_q_norm"], sm["mla_kv_norm"], w_uq_p, w_ukv_p, tabs, S)
    g_uq_p = _wgrad(cqn, dqh, "wgrad_uq", S)
    g_ukv_p = _wgrad(ckvn, dkv, "wgrad_ukv", S)
    gw["w_uq"] = g_uq_p.reshape(Q_LORA, MLA_HEADS, 128)[:, :, :96].reshape(Q_LORA, 768)
    gw["w_ukv"] = jnp.concatenate(
        [g_ukv_p[:, :1024].reshape(KV_LORA, MLA_HEADS, 128)[:, :, :64], g_ukv_p[:, 1024:].reshape(KV_LORA, MLA_HEADS, 64)],
        axis=2).reshape(KV_LORA, 1024)

    drq, drk, drv, drg, gs["ret_gn_w"] = _ret_bwd(rq, rk, rv, rprev, ry, rg, dro, sm["ret_gn_w"], tabs, S)
    grad_x, dproj, gs["pre_mix_norm"] = _inproj_bwd(drq, drk, drv, drg, dcq, dckv, dkr, w_in_p, dh1, x,
                                                    sm["pre_mix_norm"], S)
    g_in_p = _wgrad(xn, dproj, "wgrad_in", S)
    gw["w_in"] = jnp.concatenate([g_in_p[:, :2688], g_in_p[:, 2752:2784]], axis=1)
    return loss_vec, grad_x, gw, gs, ((sums, parts) if spread else None)


def _my_place():
    x = lax.axis_index("x")
    y = lax.axis_index("y")
    c = lax.axis_index("c")
    return x, y, c


def _other_chips(x, y):
    return [(1 - x, y), (x, 1 - y), (1 - x, 1 - y)]


_ANY = pl.BlockSpec(memory_space=pl.ANY)


def _allreduce_small(vec):
    def body(v_ref, out_ref, slots, send, recv, lsem):
        x, y, c = _my_place()
        me = 4 * x + 2 * y + c
        mine = pltpu.make_async_copy(v_ref, slots.at[me], lsem)
        mine.start()
        cps = []
        for r in range(1, N_DEV):
            px = x ^ (r >> 2)
            py = y ^ ((r >> 1) & 1)
            pc = c ^ (r & 1)
            cps.append(pltpu.make_async_remote_copy(
                src_ref=v_ref, dst_ref=slots.at[me], send_sem=send.at[r - 1], recv_sem=recv.at[r - 1],
                device_id=(px, py, pc), device_id_type=MESH))
        for cp in cps:
            cp.start()
        for cp in cps:
            cp.wait()
        mine.wait()
        acc = slots[0]
        for d in range(1, N_DEV):
            acc = acc + slots[d]
        out_ref[...] = acc
        loss = jnp.sum(acc[9:10, :], axis=1, keepdims=True) * (0.5 / D_MODEL)
        out_ref[9:10, :] = jnp.broadcast_to(loss, (1, PACK_COLS))

    vm = pl.BlockSpec(memory_space=pltpu.VMEM)
    return pl.pallas_call(
        body, name="allreduce_small",
        in_specs=[vm], out_specs=vm, out_shape=_sds((SMALL_ROWS, PACK_COLS), F32),
        scratch_shapes=[pltpu.VMEM((N_DEV, SMALL_ROWS, PACK_COLS), F32), pltpu.SemaphoreType.DMA((N_DEV - 1,)),
                        pltpu.SemaphoreType.DMA((N_DEV - 1,)), pltpu.SemaphoreType.DMA],
    )(vec)


N_BIG = len(BIG)


def _half(c, rows, align):
    h = rows // 2
    return pl.ds(pl.multiple_of(c * h, align), h)


def _gather_shards(shards):
    n = len(shards)

    def body(*refs):
        ins, outs, sems = refs[:n], refs[n:2 * n], refs[2 * n:]
        _gather_phase(0, ins, outs, sems)
        _gather_phase(1, ins, outs, sems)
        _gather_phase(2, ins, outs, sems)

    return pl.pallas_call(
        body, name="gather_weights",
        in_specs=[_ANY] * n, out_specs=[_ANY] * n,
        out_shape=_gather_out_shapes(shards), scratch_shapes=_gather_sems(n),
    )(*shards)


def _gather_out_shapes(shards):
    return [_sds((N_CHIPS,) + tuple(s.shape), BF16) for s in shards]


def _gather_sems(n):
    return [pltpu.SemaphoreType.DMA((n, 3))] * 4 + [pltpu.SemaphoreType.DMA((n,))] * 2


def _gather_phase(phase, ins, outs, sems):
    send1, recv1, send2, recv2, send3, recv3 = sems
    x, y, c = _my_place()
    me = 2 * x + y
    chips = _other_chips(x, y)
    sib = (x, y, 1 - c)
    for t in range(len(ins)):
        rows = ins[t].shape[0]
        half = _half(c, rows, 16)
        other = _half(1 - c, rows, 16)
        own = pltpu.make_async_remote_copy(
            src_ref=ins[t], dst_ref=outs[t].at[me], send_sem=send3.at[t], recv_sem=recv3.at[t],
            device_id=sib, device_id_type=MESH)
        if phase == 0:
            own.start()
        if phase == 2:
            own.wait()
        for k, (cx, cy) in enumerate(chips):
            src = 2 * cx + cy
            out = pltpu.make_async_remote_copy(
                src_ref=ins[t].at[half], dst_ref=outs[t].at[me, half], send_sem=send1.at[t, k],
                recv_sem=recv1.at[t, k], device_id=(cx, cy, c), device_id_type=MESH)
            landed = pltpu.make_async_remote_copy(
                src_ref=ins[t].at[half], dst_ref=outs[t].at[src, half], send_sem=send1.at[t, k],
                recv_sem=recv1.at[t, k], device_id=(cx, cy, c), device_id_type=MESH)
            fwd = pltpu.make_async_remote_copy(
                src_ref=outs[t].at[src, half], dst_ref=outs[t].at[src, half], send_sem=send2.at[t, k],
                recv_sem=recv2.at[t, k], device_id=sib, device_id_type=MESH)
            from_sib = pltpu.make_async_remote_copy(
                src_ref=outs[t].at[src, other], dst_ref=outs[t].at[src, other], send_sem=send2.at[t, k],
                recv_sem=recv2.at[t, k], device_id=sib, device_id_type=MESH)
            if phase == 0:
                out.start()
            if phase == 1:
                landed.wait_recv()
                fwd.start()
            if phase == 2:
                from_sib.wait_recv()
                out.wait_send()
                fwd.wait_send()


def _swap_copies(ins, outs, sems):
    send, recv = sems
    x, y, c = _my_place()
    return [pltpu.make_async_remote_copy(
        src_ref=ins[t].at[:, _half(1 - c, ins[t].shape[1], 8)], dst_ref=outs[t], send_sem=send.at[t],
        recv_sem=recv.at[t], device_id=(x, y, 1 - c), device_id_type=MESH) for t in range(len(ins))]


def _swap_out_shapes(gs):
    return [_sds((N_CHIPS, g.shape[1] // 2, g.shape[2]), F32) for g in gs]


def _swap_sems(n):
    return [pltpu.SemaphoreType.DMA((n,)), pltpu.SemaphoreType.DMA((n,))]


def _swap_half_rows(gs):
    n = len(gs)

    def body(*refs):
        cps = _swap_copies(refs[:n], refs[n:2 * n], refs[2 * n:])
        for cp in cps:
            cp.start()
        for cp in cps:
            cp.wait()

    return pl.pallas_call(
        body, name="rs_swap_halves",
        in_specs=[_ANY] * n, out_specs=[_ANY] * n, out_shape=_swap_out_shapes(gs), scratch_shapes=_swap_sems(n),
    )(*gs)


def _add_half_rows(g, got, c_idx, name):
    _, rows, cols = g.shape
    h = rows // 2

    def body(c_ref, a_ref, b_ref, o_ref):
        o_ref[...] = (a_ref[...] + b_ref[...]).astype(BF16)

    grid_spec = pltpu.PrefetchScalarGridSpec(
        num_scalar_prefetch=1, grid=(N_CHIPS,),
        in_specs=[pl.BlockSpec((None, h, cols), lambda j, c: (j, c[0], 0)),
                  pl.BlockSpec((None, h, cols), lambda j, c: (j, 0, 0))],
        out_specs=pl.BlockSpec((None, h, cols), lambda j, c: (j, 0, 0)),
    )
    return pl.pallas_call(
        body, name=name, grid_spec=grid_spec, out_shape=_sds((N_CHIPS, h, cols), BF16),
        compiler_params=_cp(("parallel",)),
    )(c_idx, g, got)


def _scatter_to_chips(ts):
    n = len(ts)

    def body(*refs):
        cps = _scatter_copies(refs[:n], refs[n:2 * n], refs[2 * n:])
        for cp in cps:
            cp.start()
        for cp in cps:
            cp.wait()

    return pl.pallas_call(
        body, name="rs_scatter_chips",
        in_specs=[_ANY] * n, out_specs=[_ANY] * n, out_shape=_scatter_out_shapes(ts), scratch_shapes=_scatter_sems(n),
    )(*ts)


def _scatter_copies(ins, outs, sems):
    send, recv = sems
    x, y, c = _my_place()
    return [pltpu.make_async_remote_copy(
        src_ref=ins[t].at[2 * cx + cy], dst_ref=outs[t].at[k], send_sem=send.at[t, k], recv_sem=recv.at[t, k],
        device_id=(cx, cy, c), device_id_type=MESH)
        for t in range(len(ins)) for k, (cx, cy) in enumerate(_other_chips(x, y))]


def _scatter_out_shapes(ts):
    return [_sds((3,) + tuple(t.shape[1:]), BF16) for t in ts]


def _scatter_sems(n):
    return [pltpu.SemaphoreType.DMA((n, 3)), pltpu.SemaphoreType.DMA((n, 3))]


def _add_four(mine, parts, place, name):
    _, h, cols = parts.shape

    def body(pl_ref, m_ref, p_ref, o_ref):
        o_ref[...] = ((m_ref[...].astype(F32) + p_ref[0].astype(F32)) + p_ref[1].astype(F32)) + p_ref[2].astype(F32)

    grid_spec = pltpu.PrefetchScalarGridSpec(
        num_scalar_prefetch=1, grid=(1,),
        in_specs=[pl.BlockSpec((None, h, cols), lambda i, pc: (pc[0], 0, 0)),
                  pl.BlockSpec((3, h, cols), lambda i, pc: (0, 0, 0))],
        out_specs=pl.BlockSpec((h, cols), lambda i, pc: (pc[1], 0)),
    )
    return pl.pallas_call(
        body, name=name, grid_spec=grid_spec, out_shape=_sds((2 * h, cols), F32),
        compiler_params=_cp(("arbitrary",)),
    )(place, mine, parts)


def _join_half_rows(rs):
    n = len(rs)

    def body(*refs):
        ins, outs = refs[:n], refs[n:2 * n]
        send, recv = refs[2 * n:]
        x, y, c = _my_place()
        cps = []
        for t in range(n):
            half = _half(c, outs[t].shape[0], 8)
            rc = pltpu.make_async_remote_copy(
                src_ref=ins[t].at[half], dst_ref=outs[t].at[half], send_sem=send.at[t], recv_sem=recv.at[t],
                device_id=(x, y, 1 - c), device_id_type=MESH)
            rc.start()
            cps.append(rc)
        for cp in cps:
            cp.wait()

    return pl.pallas_call(
        body, name="rs_join_halves",
        in_specs=[_ANY] * n, out_specs=[_ANY] * n,
        out_shape=[_sds(r.shape, F32) for r in rs],
        input_output_aliases={i: i for i in range(n)},
        scratch_shapes=[pltpu.SemaphoreType.DMA((n,))] * 2,
    )(*rs)


def _by_chip(full, rows, cols, axis):
    if axis == 0:
        return full.reshape(N_CHIPS, rows // N_CHIPS, cols)
    return full.reshape(rows, N_CHIPS, cols // N_CHIPS).transpose(1, 0, 2)


def _from_chips(parts, axis):
    _, r, c = parts.shape
    if axis == 0:
        return parts.reshape(N_CHIPS * r, c)
    return parts.transpose(1, 0, 2).reshape(r, N_CHIPS * c)


def _adamw(wt, g, m, v, name):
    R, C = wt.shape
    tr = R
    for cand in (256, 128, 64, 32, 16, 8):
        if R % cand == 0:
            tr = cand
            break

    def body(w_ref, g_ref, m_ref, v_ref, d_ref, nm_ref, nv_ref):
        gg = g_ref[...]
        m_new = ADAM_B1 * m_ref[...] + (1.0 - ADAM_B1) * gg
        v_new = ADAM_B2 * v_ref[...] + (1.0 - ADAM_B2) * (gg * gg)
        m_hat = m_new / (1.0 - ADAM_B1 ** ADAM_STEP)
        v_hat = v_new / (1.0 - ADAM_B2 ** ADAM_STEP)
        d_ref[...] = -ADAM_LR * (m_hat / (jnp.sqrt(v_hat) + ADAM_EPS) + ADAM_WD * w_ref[...])
        nm_ref[...] = m_new
        nv_ref[...] = v_new

    spec = pl.BlockSpec((tr, C), lambda i: (i, 0))
    return pl.pallas_call(
        body, name=name, grid=(R // tr,), in_specs=[spec] * 4, out_specs=[spec] * 3, out_shape=[_sds((R, C), F32)] * 3,
        compiler_params=_cp(("parallel",)),
    )(wt, g, m, v)


def _pack_small(vals, loss_vec=None):
    rows = [jnp.pad(vals[n].reshape(-1), (0, PACK_COLS - sz)) for n, sz in SMALL]
    rows.append(loss_vec.reshape(-1) if loss_vec is not None else jnp.zeros((PACK_COLS,), F32))
    rows += [jnp.zeros((PACK_COLS,), F32)] * (SMALL_ROWS - len(rows))
    return jnp.stack(rows)


def kernel(x, p, positions, pre_mix_norm, w_in, ret_gn_w, mla_q_norm, w_uq, mla_kv_norm, w_ukv, w_o, post_mix_norm, pre_ffn_norm, w_gate, w_up, w_down, post_ffn_norm, w_ple_proj, ple_norm, w_ple_gate, b_ple_gate, loss_target, m_pre_mix_norm, m_w_in, m_ret_gn_w, m_mla_q_norm, m_w_uq, m_mla_kv_norm, m_w_ukv, m_w_o, m_post_mix_norm, m_pre_ffn_norm, m_w_gate, m_w_up, m_w_down, m_post_ffn_norm, m_w_ple_proj, m_ple_norm, m_w_ple_gate, m_b_ple_gate, v_pre_mix_norm, v_w_in, v_ret_gn_w, v_mla_q_norm, v_w_uq, v_mla_kv_norm, v_w_ukv, v_w_o, v_post_mix_norm, v_pre_ffn_norm, v_w_gate, v_w_up, v_w_down, v_post_ffn_norm, v_w_ple_proj, v_ple_norm, v_w_ple_gate, v_b_ple_gate):
    wts = dict(pre_mix_norm=pre_mix_norm, w_in=w_in, ret_gn_w=ret_gn_w, mla_q_norm=mla_q_norm, w_uq=w_uq,
               mla_kv_norm=mla_kv_norm, w_ukv=w_ukv, w_o=w_o, post_mix_norm=post_mix_norm, pre_ffn_norm=pre_ffn_norm,
               w_gate=w_gate, w_up=w_up, w_down=w_down, post_ffn_norm=post_ffn_norm, w_ple_proj=w_ple_proj,
               ple_norm=ple_norm, w_ple_gate=w_ple_gate, b_ple_gate=b_ple_gate)
    mom = dict(pre_mix_norm=m_pre_mix_norm, w_in=m_w_in, ret_gn_w=m_ret_gn_w, mla_q_norm=m_mla_q_norm, w_uq=m_w_uq,
               mla_kv_norm=m_mla_kv_norm, w_ukv=m_w_ukv, w_o=m_w_o, post_mix_norm=m_post_mix_norm,
               pre_ffn_norm=m_pre_ffn_norm, w_gate=m_w_gate, w_up=m_w_up, w_down=m_w_down, post_ffn_norm=m_post_ffn_norm,
               w_ple_proj=m_w_ple_proj, ple_norm=m_ple_norm, w_ple_gate=m_w_ple_gate, b_ple_gate=m_b_ple_gate)
    var = dict(pre_mix_norm=v_pre_mix_norm, w_in=v_w_in, ret_gn_w=v_ret_gn_w, mla_q_norm=v_mla_q_norm, w_uq=v_w_uq,
               mla_kv_norm=v_mla_kv_norm, w_ukv=v_w_ukv, w_o=v_w_o, post_mix_norm=v_post_mix_norm,
               pre_ffn_norm=v_pre_ffn_norm, w_gate=v_w_gate, w_up=v_w_up, w_down=v_w_down, post_ffn_norm=v_post_ffn_norm,
               w_ple_proj=v_w_ple_proj, ple_norm=v_ple_norm, w_ple_gate=v_w_ple_gate, b_ple_gate=v_b_ple_gate)

    S = x.shape[1]
    shard2d = {n: wts[n][0] for n, _, _, _ in BIG}
    small2d = {n: wts[n] for n, _ in SMALL}

    shard_bf = {n: shard2d[n].astype(BF16) for n in shard2d}
    gathered = _gather_shards([shard_bf[n] for n in GATHER_FIRST])
    w_first = {n: _from_chips(gathered[i], BIG_SPEC[n][2]) for i, n in enumerate(GATHER_FIRST)}

    pos_f = positions.astype(F32).reshape(S, 1)
    c_idx = lax.axis_index("c").astype(jnp.int32).reshape(1)
    loss_vec, grad_x, gw, gs, (sums_early, parts_early) = _local_step(
        x[0], p[0, 0], pos_f, loss_target[0], w_first, small2d, {n: shard_bf[n] for n in GATHER_LATE}, c_idx)

    g4 = [_by_chip(gw[n], *BIG_SPEC[n]) for n in REDUCE_LAST]
    got = _swap_half_rows(g4)
    sums_last = [_add_half_rows(g4[i], got[i], c_idx, "rs_add_halves_" + n) for i, n in enumerate(REDUCE_LAST)]
    parts_last = _scatter_to_chips(sums_last)
    place = jnp.stack([2 * lax.axis_index("x") + lax.axis_index("y"), lax.axis_index("c")]).astype(jnp.int32)
    names = REDUCE_EARLY + REDUCE_LAST
    reduced = _join_half_rows(
        [_add_four(sm_, pt_, place, "rs_add_chips_" + n)
         for n, sm_, pt_ in zip(names, sums_early + sums_last, list(parts_early) + list(parts_last))])
    g_shard = dict(zip(names, reduced))

    small_sum = _allreduce_small(_pack_small(gs, loss_vec))
    loss = small_sum[9, 0]
    g_small = {n: small_sum[i:i + 1, :sz] for i, (n, sz) in enumerate(SMALL)}

    grads, delta, new_m, new_v = {}, {}, {}, {}
    for n, _, _, _ in BIG:
        d, nm, nv = _adamw(shard2d[n], g_shard[n], mom[n][0], var[n][0], "adamw_" + n)
        grads[n], delta[n], new_m[n], new_v[n] = g_shard[n][None], d[None], nm[None], nv[None]
    d, nm, nv = _adamw(_pack_small(small2d), small_sum, _pack_small(mom), _pack_small(var), "adamw_small")
    for i, (n, sz) in enumerate(SMALL):
        grads[n] = g_small[n]
        delta[n], new_m[n], new_v[n] = d[i:i + 1, :sz], nm[i:i + 1, :sz], nv[i:i + 1, :sz]

    return (loss, grad_x[None], *[grads[n] for n in ALL_W], *[delta[n] for n in ALL_W],
            *[new_m[n] for n in ALL_W], *[new_v[n] for n in ALL_W])
```

```python
import functools
import math

import jax
import jax.numpy as jnp
import numpy as np
from jax import lax
from jax.experimental import pallas as pl
from jax.experimental.pallas import tpu as pltpu

F32 = jnp.float32
BF16 = jnp.bfloat16
MESH = pl.DeviceIdType.MESH

D_MODEL = 1024
D_FF = 2816
PLE_DIM = 256
RET_HEADS = 4
RET_DIM = 128
RET_WIDTH = 512
RET_CHUNK = 256
RET_GROUP = 4
MLA_HEADS = 8
MLA_NOPE = 64
MLA_ROPE = 32
MLA_V = 64
Q_LORA = 384
KV_LORA = 256
IN_COLS = 2720
IN_COLS_P = 2816
ROPE_BASE = 10000.0
EPS = 1e-6
SCALE_MLA = 1.0 / math.sqrt(MLA_NOPE + MLA_ROPE)
SCALE_RET = RET_DIM ** -0.5
NEG = -1e30

ADAM_LR = 0.001
ADAM_B1 = 0.9
ADAM_B2 = 0.999
ADAM_EPS = 1e-08
ADAM_WD = 0.01
ADAM_STEP = 10

N_CHIPS = 4
N_DEV = 8
VMEM_MB = 56

BIG = (
    ("w_in", 1024, 2720, 1),
    ("w_uq", 384, 768, 1),
    ("w_ukv", 256, 1024, 1),
    ("w_o", 1024, 1024, 0),
    ("w_gate", 1024, 2816, 1),
    ("w_up", 1024, 2816, 1),
    ("w_down", 2816, 1024, 0),
    ("w_ple_proj", 256, 1024, 1),
    ("w_ple_gate", 1024, 1024, 0),
)
SMALL = (
    ("pre_mix_norm", 1024),
    ("ret_gn_w", 512),
    ("mla_q_norm", 384),
    ("mla_kv_norm", 256),
    ("post_mix_norm", 1024),
    ("pre_ffn_norm", 1024),
    ("post_ffn_norm", 1024),
    ("ple_norm", 1024),
    ("b_ple_gate", 1024),
)
ALL_W = ("pre_mix_norm", "w_in", "ret_gn_w", "mla_q_norm", "w_uq", "mla_kv_norm", "w_ukv", "w_o", "post_mix_norm",
         "pre_ffn_norm", "w_gate", "w_up", "w_down", "post_ffn_norm", "w_ple_proj", "ple_norm", "w_ple_gate", "b_ple_gate")
PACK_COLS = 1024
SMALL_ROWS = 16


def _cp(sem=None, mb=VMEM_MB, **kw):
    return pltpu.CompilerParams(dimension_semantics=sem, vmem_limit_bytes=mb * 1024 * 1024, **kw)


def _bf(x):
    return x.astype(BF16)


def _dot(a, b):
    return jnp.dot(_bf(a), _bf(b), preferred_element_type=F32)


def _dot_nt(a, b):
    return lax.dot_general(_bf(a), _bf(b), (((1,), (1,)), ((), ())), preferred_element_type=F32)


def _dot_tn(a, b):
    return lax.dot_general(_bf(a), _bf(b), (((0,), (0,)), ((), ())), preferred_element_type=F32)


def _sig(x):
    return 1.0 / (1.0 + jnp.exp(-x))


def _rms(x, g):
    r = lax.rsqrt(jnp.mean(x * x, axis=-1, keepdims=True) + EPS)
    return x * r * g


def _rms_bwd(dy, x, g):
    r = lax.rsqrt(jnp.mean(x * x, axis=-1, keepdims=True) + EPS)
    xh = x * r
    dxh = dy * g
    dx = r * (dxh - xh * jnp.mean(dxh * xh, axis=-1, keepdims=True))
    return dx, dy * xh


def _colsum(x):
    return jnp.sum(x, axis=0, keepdims=True)


def _rope_ret(x, cr, sr):
    return x * cr + pltpu.roll(x, 64, 1) * sr


def _unrope_ret(dy, cr, sr):
    return dy * cr + pltpu.roll(dy * sr, 64, 1)


def _rope_mla(x, cm, sa, sb):
    return x * cm + pltpu.roll(x, 112, 1) * sa + pltpu.roll(x, 16, 1) * sb


def _unrope_mla(dy, cm, sa, sb):
    return dy * cm + pltpu.roll(dy * sa, 16, 1) + pltpu.roll(dy * sb, 112, 1)


def _rows(tm, w, col=0):
    return pl.BlockSpec((tm, w), lambda i: (i, col))


def _full(*shape):
    return pl.BlockSpec(shape, lambda i: (0,) * len(shape))


def _sds(shape, dtype):
    return jax.ShapeDtypeStruct(shape, dtype)


def _rope_tables(pos_f, S, shards=()):
    tm = min(512, S)
    n = len(shards)
    steps = S // tm
    inv_r = (1.0 / (np.float32(ROPE_BASE) ** (np.arange(64, dtype=np.float32) / np.float32(64)))).astype(np.float32)
    inv_m16 = (1.0 / (np.float32(ROPE_BASE) ** (np.arange(16, dtype=np.float32) / np.float32(16)))).astype(np.float32)
    inv_r = np.concatenate([inv_r, inv_r])[None, :]
    inv_m = np.zeros((1, 128), np.float32)
    inv_m[0, 64:80] = inv_m16
    inv_m[0, 80:96] = inv_m16

    def body(pos_ref, invr_ref, invm_ref, *rest):
        w_ins, (cr_ref, sr_ref, cm_ref, sa_ref, sb_ref) = rest[:n], rest[n:n + 5]
        w_outs, sems = rest[n + 5:2 * n + 5], rest[2 * n + 5:]
        i = pl.program_id(0)
        if n:
            @pl.when(i == 0)
            def _():
                _gather_phase(0, w_ins, w_outs, sems)

            @pl.when(i == steps // 2)
            def _():
                _gather_phase(1, w_ins, w_outs, sems)

        pos = pos_ref[...]
        lane = lax.broadcasted_iota(jnp.int32, (tm, 128), 1)
        ar = pos * invr_ref[...]
        s = jnp.sin(ar)
        cr_ref[...] = jnp.cos(ar)
        sr_ref[...] = jnp.where(lane < 64, -s, s)
        am = pos * invm_ref[...]
        c2 = jnp.cos(am)
        s2 = jnp.sin(am)
        cm_ref[...] = jnp.where(lane < 64, 1.0, jnp.where(lane < 96, c2, 0.0))
        sa_ref[...] = jnp.where((lane >= 64) & (lane < 80), -s2, 0.0)
        sb_ref[...] = jnp.where((lane >= 80) & (lane < 96), s2, 0.0)

        if n:
            @pl.when(i == steps - 1)
            def _():
                _gather_phase(2, w_ins, w_outs, sems)

    outs = pl.pallas_call(
        body, name="rope_tables", grid=(steps,),
        in_specs=[_rows(tm, 1), _full(1, 128), _full(1, 128)] + [_ANY] * n,
        out_specs=[_rows(tm, 128)] * 5 + [_ANY] * n,
        out_shape=[_sds((S, 128), F32)] * 5 + _gather_out_shapes(shards),
        scratch_shapes=_gather_sems(n) if n else [],
        compiler_params=_cp(("arbitrary",)),
    )(pos_f, jnp.asarray(inv_r), jnp.asarray(inv_m), *shards)
    return outs[:5], outs[5:]


def _inproj(x, g, w_in, tabs, S):
    tm = min(512, S)

    def body(x_ref, g_ref, w_ref, cr_ref, sr_ref, cm_ref, sa_ref, sb_ref,
             xn_ref, rq_ref, rk_ref, rv_ref, rg_ref, cq_ref, ckv_ref, kr_ref):
        xb = _rms(x_ref[...], g_ref[...]).astype(BF16)
        xn_ref[...] = xb
        cr = cr_ref[...]
        sr = sr_ref[...]
        q = jnp.dot(xb, w_ref[:, 0:512], preferred_element_type=F32)
        k = jnp.dot(xb, w_ref[:, 512:1024], preferred_element_type=F32)
        for h in range(RET_HEADS):
            sl = slice(h * 128, (h + 1) * 128)
            rq_ref[:, sl] = _rope_ret(q[:, sl], cr, sr).astype(BF16)
            rk_ref[:, sl] = (_rope_ret(k[:, sl], cr, sr) * SCALE_RET).astype(BF16)
        rv_ref[...] = jnp.dot(xb, w_ref[:, 1024:1536], preferred_element_type=F32).astype(BF16)
        rg_ref[...] = jnp.dot(xb, w_ref[:, 1536:2048], preferred_element_type=F32)
        cq_ref[...] = jnp.dot(xb, w_ref[:, 2048:2432], preferred_element_type=F32)
        ckv_ref[...] = jnp.dot(xb, w_ref[:, 2432:2688], preferred_element_type=F32)
        kr = jnp.dot(xb, w_ref[:, 2688:2816], preferred_element_type=F32)
        kr_ref[...] = _rope_mla(kr, cm_ref[...], sa_ref[...], sb_ref[...])

    return pl.pallas_call(
        body, name="inproj", grid=(S // tm,),
        in_specs=[_rows(tm, D_MODEL), _full(1, D_MODEL), _full(D_MODEL, IN_COLS_P)] + [_rows(tm, 128)] * 5,
        out_specs=[_rows(tm, D_MODEL)] + [_rows(tm, 512)] * 4 + [_rows(tm, Q_LORA), _rows(tm, KV_LORA), _rows(tm, 128)],
        out_shape=[_sds((S, D_MODEL), BF16)] + [_sds((S, 512), BF16)] * 3
        + [_sds((S, 512), F32), _sds((S, Q_LORA), F32), _sds((S, KV_LORA), F32), _sds((S, 128), F32)],
        compiler_params=_cp(("parallel",)),
    )(x, g, w_in, *tabs)


def _mla_up(cq, ckv, kr, gq, gkv, w_uq, w_ukv, tabs, S):
    tm = min(512, S)

    def body(cq_ref, ckv_ref, kr_ref, gq_ref, gkv_ref, wuq_ref, wukv_ref, cm_ref, sa_ref, sb_ref,
             cqn_ref, ckvn_ref, qp_ref, kp_ref, v_ref, kt_ref, vt_ref):
        cm = cm_ref[...]
        sa = sa_ref[...]
        sb = sb_ref[...]
        cqn = _rms(cq_ref[...], gq_ref[...]).astype(BF16)
        cqn_ref[...] = cqn
        ckvn = _rms(ckv_ref[...], gkv_ref[...]).astype(BF16)
        ckvn_ref[...] = ckvn
        qh = jnp.dot(cqn, wuq_ref[...], preferred_element_type=F32)
        kv = jnp.dot(ckvn, wukv_ref[...], preferred_element_type=F32)
        kr_blk = kr_ref[...]
        for h in range(MLA_HEADS):
            sl = slice(h * 128, (h + 1) * 128)
            qp_ref[:, sl] = (_rope_mla(qh[:, sl], cm, sa, sb) * SCALE_MLA).astype(BF16)
            kh = kv[:, sl] + kr_blk
            kp_ref[:, sl] = kh.astype(BF16)
            kt_ref[sl, :] = kh.T.astype(BF16)
        for h in range(MLA_HEADS // 2):
            vh = kv[:, 1024 + h * 128:1024 + (h + 1) * 128]
            v_ref[:, h * 128:(h + 1) * 128] = vh.astype(BF16)
            vt_ref[h * 128:(h + 1) * 128, :] = vh.T.astype(BF16)

    cols = lambda r: pl.BlockSpec((r, tm), lambda i: (0, i))
    return pl.pallas_call(
        body, name="mla_up", grid=(S // tm,),
        in_specs=[_rows(tm, Q_LORA), _rows(tm, KV_LORA), _rows(tm, 128), _full(1, Q_LORA), _full(1, KV_LORA),
                  _full(Q_LORA, 1024), _full(KV_LORA, 1536)] + [_rows(tm, 128)] * 3,
        out_specs=[_rows(tm, Q_LORA), _rows(tm, KV_LORA), _rows(tm, 1024), _rows(tm, 1024), _rows(tm, 512),
                   cols(1024), cols(512)],
        out_shape=[_sds((S, Q_LORA), BF16), _sds((S, KV_LORA), BF16), _sds((S, 1024), BF16), _sds((S, 1024), BF16),
                   _sds((S, 512), BF16), _sds((1024, S), BF16), _sds((512, S), BF16)],
        compiler_params=_cp(("parallel",)),
    )(cq, ckv, kr, gq, gkv, w_uq, w_ukv, *tabs[2:])


def _tri_pairs(nq, k_major):
    if k_major:
        pairs = [(qb, kb) for kb in range(nq) for qb in range(kb, nq)]
    else:
        pairs = [(qb, kb) for qb in range(nq) for kb in range(qb + 1)]
    qb_of = np.array([p[0] for p in pairs], np.int32)
    kb_of = np.array([p[1] for p in pairs], np.int32)
    return jnp.asarray(qb_of), jnp.asarray(kb_of), len(pairs)


ATT_ROWS = 32
FWD_HEADS = 8


def _causal_keep(r0, rows, tq):
    key = r0 + lax.broadcasted_iota(jnp.int32, (rows, tq), 0)
    qry = lax.broadcasted_iota(jnp.int32, (rows, tq), 1)
    return key <= qry


def _flash_fwd(qp, kp, vt, S, shards=()):
    tq = min(512, S)
    nq = S // tq
    RB = ATT_ROWS
    NH = FWD_HEADS
    qb_of, kb_of, T = _tri_pairs(nq, k_major=False)
    n = len(shards)
    steps = (MLA_HEADS // NH) * T

    def body(qb_ref, kb_ref, q_ref, k_ref, vt_ref, *rest):
        w_ins, (o_ref, lse_ref), w_outs = rest[:n], rest[n:n + 2], rest[n + 2:2 * n + 2]
        m_sc, l_sc, acc_sc, s_sc, p_sc = rest[2 * n + 2:2 * n + 7]
        sems = rest[2 * n + 7:]
        t = pl.program_id(1)
        qb = qb_ref[t]
        kb = kb_ref[t]
        lin = pl.program_id(0) * T + t

        if n:
            @pl.when(lin == 0)
            def _():
                _gather_phase(0, w_ins, w_outs, sems)

            @pl.when(lin == steps // 2)
            def _():
                _gather_phase(1, w_ins, w_outs, sems)

        @pl.when(kb == 0)
        def _():
            m_sc[...] = jnp.full(m_sc.shape, NEG, F32)
            l_sc[...] = jnp.zeros(l_sc.shape, F32)
            acc_sc[...] = jnp.zeros(acc_sc.shape, F32)

        def step(masked):
            for a in range(NH):
                sl = slice(a * 128, (a + 1) * 128)
                s_sc[a] = _dot_nt(k_ref[:, sl], q_ref[:, sl])
            m_new, al = [], []
            for a in range(NH):
                mx = [jnp.full((8, tq), NEG, F32) for _ in range(RB // 8)]
                for r in range(0, tq, RB):
                    sc = s_sc[a, r:r + RB, :]
                    if masked:
                        sc = jnp.where(_causal_keep(r, RB, tq), sc, NEG)
                        s_sc[a, r:r + RB, :] = sc
                    for i in range(RB // 8):
                        mx[i] = jnp.maximum(mx[i], sc[i * 8:(i + 1) * 8, :])
                mx8 = jnp.maximum(jnp.maximum(mx[0], mx[1]), jnp.maximum(mx[2], mx[3]))
                m_prev = m_sc[a]
                m_new.append(jnp.maximum(m_prev, jnp.max(mx8, axis=0, keepdims=True)))
                al.append(jnp.exp(m_prev - m_new[a]))
                m_sc[a] = m_new[a]
            for a in range(NH):
                ls = [jnp.zeros((8, tq), F32) for _ in range(RB // 8)]
                for r in range(0, tq, RB):
                    p = jnp.exp(s_sc[a, r:r + RB, :] - m_new[a])
                    for i in range(RB // 8):
                        ls[i] = ls[i] + p[i * 8:(i + 1) * 8, :]
                    p_sc[a, r:r + RB, :] = p.astype(BF16)
                l_sc[a] = al[a] * l_sc[a] + jnp.sum((ls[0] + ls[1]) + (ls[2] + ls[3]), axis=0, keepdims=True)
                pair = slice((a // 2) * 128, (a // 2 + 1) * 128)
                pv = jnp.dot(vt_ref[pair, :], p_sc[a], preferred_element_type=F32)
                rs = slice(a * 64, (a + 1) * 64)
                own = slice((a % 2) * 64, (a % 2 + 1) * 64)
                acc_sc[rs, :] = acc_sc[rs, :] * al[a] + pv[own, :]

        @pl.when(kb < qb)
        def _():
            step(False)

        @pl.when(kb == qb)
        def _():
            step(True)
            for a in range(NH):
                rs = slice(a * 64, (a + 1) * 64)
                acc_sc[rs, :] = acc_sc[rs, :] / l_sc[a]
                lse_ref[a:a + 1, :] = m_sc[a] + jnp.log(l_sc[a])
            o_ref[...] = acc_sc[...].T.astype(BF16)

        if n:
            @pl.when(lin == steps - 1)
            def _():
                _gather_phase(2, w_ins, w_outs, sems)

    grid_spec = pltpu.PrefetchScalarGridSpec(
        num_scalar_prefetch=2, grid=(MLA_HEADS // NH, T),
        in_specs=[pl.BlockSpec((tq, 128 * NH), lambda j, t, qb, kb: (qb[t], j)),
                  pl.BlockSpec((tq, 128 * NH), lambda j, t, qb, kb: (kb[t], j)),
                  pl.BlockSpec((64 * NH, tq), lambda j, t, qb, kb: (j, kb[t]))] + [_ANY] * n,
        out_specs=[pl.BlockSpec((tq, 64 * NH), lambda j, t, qb, kb: (qb[t], j)),
                   pl.BlockSpec((None, NH, tq), lambda j, t, qb, kb: (j, 0, qb[t]))] + [_ANY] * n,
        scratch_shapes=[pltpu.VMEM((NH, 1, tq), F32), pltpu.VMEM((NH, 1, tq), F32), pltpu.VMEM((64 * NH, tq), F32),
                        pltpu.VMEM((NH, tq, tq), F32), pltpu.VMEM((NH, tq, tq), BF16)] + (_gather_sems(n) if n else []),
    )
    out, lse, *gathered = pl.pallas_call(
        body, name="flash_fwd", grid_spec=grid_spec,
        out_shape=[_sds((S, 512), BF16), _sds((MLA_HEADS // NH, NH, S), F32)] + _gather_out_shapes(shards),
        compiler_params=_cp(("arbitrary", "arbitrary")),
    )(qb_of, kb_of, qp, kp, vt, *shards)
    return out, lse.reshape(MLA_HEADS // 2, 2, S), gathered


def _decay_table():
    log_g = np.log(1.0 - 2.0 ** (-5.0 - np.arange(RET_HEADS, dtype=np.float32))).astype(np.float32)
    return jnp.asarray(np.broadcast_to(log_g[:, None, None], (RET_HEADS, 8, 128)).copy())


def _decay_terms(lg_ref):
    C = RET_CHUNK
    lg = lg_ref[0:1, :]
    row = lax.broadcasted_iota(jnp.int32, (C, C), 0)
    col = lax.broadcasted_iota(jnp.int32, (C, C), 1)
    diff = (row - col).astype(F32)
    dmat = jnp.where(diff >= 0, jnp.exp(jnp.maximum(diff, 0.0) * jnp.tile(lg, (1, C // 128))), 0.0)
    j = lax.broadcasted_iota(jnp.int32, (C, 1), 0).astype(F32)
    lg1 = lg[:, 0:1]
    zeta = jnp.exp((C - 1 - j) * lg1)
    xi = jnp.exp((j + 1.0) * lg1)
    g_chunk = jnp.exp(C * lg1)
    return dmat, zeta, xi, g_chunk


def _ret_fwd(rq, rk, rv, rg, gn_w, S):
    C = RET_CHUNK
    N = S // C
    G = min(RET_GROUP, N)
    NB = N // G

    def body(lg_ref, q_ref, k_ref, v_ref, rg_ref, w_ref, ry_ref, ro_ref, rprev_ref, r_sc):
        @pl.when(pl.program_id(1) == 0)
        def _():
            r_sc[...] = jnp.zeros(r_sc.shape, F32)

        dmat, zeta, xi, g_chunk = _decay_terms(lg_ref)
        w = w_ref[...]
        r = r_sc[...]
        for i in range(G):
            rows = slice(i * C, (i + 1) * C)
            q = q_ref[rows, :]
            k = k_ref[rows, :]
            v = v_ref[rows, :]
            r_prev = r.astype(BF16)
            rprev_ref[i] = r_prev
            sc = _dot_nt(q, k) * dmat
            ry = _dot(sc, v) + jnp.dot(q, r_prev, preferred_element_type=F32) * xi
            ry_ref[rows, :] = ry
            r = g_chunk * r + _dot_tn(k, zeta * v.astype(F32))
            mu = jnp.mean(ry, axis=-1, keepdims=True)
            yc = ry - mu
            yh = yc * lax.rsqrt(jnp.mean(yc * yc, axis=-1, keepdims=True) + EPS)
            g = rg_ref[rows, :]
            ro_ref[rows, :] = (g * _sig(g) * (yh * w)).astype(BF16)
        r_sc[...] = r

    blk = pl.BlockSpec((G * C, 128), lambda h, n: (n, h))
    return pl.pallas_call(
        body, name="ret_fwd", grid=(RET_HEADS, NB),
        in_specs=[pl.BlockSpec((None, 8, 128), lambda h, n: (h, 0, 0)), blk, blk, blk, blk,
                  pl.BlockSpec((1, 128), lambda h, n: (0, h))],
        out_specs=[blk, blk, pl.BlockSpec((G, 128, 128), lambda h, n: (h * NB + n, 0, 0))],
        out_shape=[_sds((S, 512), F32), _sds((S, 512), BF16), _sds((RET_HEADS * N, 128, 128), BF16)],
        scratch_shapes=[pltpu.VMEM((128, 128), F32)],
        compiler_params=_cp(("parallel", "arbitrary")),
    )(_decay_table(), rq, rk, rv, rg, gn_w)


def _outproj(ro, mo, x, w_o, g_post, g_pre, S):
    tm = min(512, S)

    def body(ro_ref, mo_ref, x_ref, wo_ref, g1_ref, g2_ref, mix_ref, h1_ref, hn_ref):
        mix = (jnp.dot(ro_ref[...], wo_ref[0:512, :], preferred_element_type=F32)
               + jnp.dot(mo_ref[...], wo_ref[512:1024, :], preferred_element_type=F32))
        mix_ref[...] = mix.astype(BF16)
        h1 = x_ref[...] + _rms(mix, g1_ref[...])
        h1_ref[...] = h1
        hn_ref[...] = _rms(h1, g2_ref[...]).astype(BF16)

    return pl.pallas_call(
        body, name="outproj", grid=(S // tm,),
        in_specs=[_rows(tm, 512), _rows(tm, 512), _rows(tm, D_MODEL), _full(D_MODEL, D_MODEL), _full(1, D_MODEL),
                  _full(1, D_MODEL)],
        out_specs=[_rows(tm, D_MODEL)] * 3,
        out_shape=[_sds((S, D_MODEL), BF16), _sds((S, D_MODEL), F32), _sds((S, D_MODEL), BF16)],
        compiler_params=_cp(("parallel",)),
    )(ro, mo, x, w_o, g_post, g_pre)


def _ffn_up(hn, w_gate, w_up, S):
    tm = min(512, S)
    tn = D_FF // 2

    def body(hn_ref, wg_ref, wu_ref, gate_ref, up_ref, act_ref):
        hn_b = hn_ref[...]
        g = jnp.dot(hn_b, wg_ref[...], preferred_element_type=F32)
        u = jnp.dot(hn_b, wu_ref[...], preferred_element_type=F32)
        gate_ref[...] = g.astype(BF16)
        up_ref[...] = u.astype(BF16)
        act_ref[...] = (g * _sig(g) * u).astype(BF16)

    wspec = pl.BlockSpec((D_MODEL, tn), lambda j, i: (0, j))
    ospec = pl.BlockSpec((tm, tn), lambda j, i: (i, j))
    return pl.pallas_call(
        body, name="ffn_up", grid=(2, S // tm),
        in_specs=[pl.BlockSpec((tm, D_MODEL), lambda j, i: (i, 0)), wspec, wspec],
        out_specs=[ospec] * 3, out_shape=[_sds((S, D_FF), BF16)] * 3,
        compiler_params=_cp(("parallel", "parallel")),
    )(hn, w_gate, w_up)


def _ffn_down(act, w_down, h1, g, S):
    tm = min(512, S)

    def body(act_ref, wd_ref, h1_ref, g_ref, ff_ref, h2_ref):
        ff = jnp.dot(act_ref[...], wd_ref[...], preferred_element_type=F32)
        ff_ref[...] = ff.astype(BF16)
        h2_ref[...] = h1_ref[...] + _rms(ff, g_ref[...])

    return pl.pallas_call(
        body, name="ffn_down", grid=(S // tm,),
        in_specs=[_rows(tm, D_FF), _full(D_FF, D_MODEL), _rows(tm, D_MODEL), _full(1, D_MODEL)],
        out_specs=[_rows(tm, D_MODEL)] * 2, out_shape=[_sds((S, D_MODEL), BF16), _sds((S, D_MODEL), F32)],
        compiler_params=_cp(("parallel",)),
    )(act, w_down, h1, g)


def _ple_loss(p, h2, tgt, w_pp, w_pg, b_pg, g_ple, S):
    tm = min(256, S)

    def body(p_ref, h2_ref, t_ref, wp_ref, wg_ref, b_ref, gp_ref,
             dz_ref, dpe_ref, dh2_ref, h2b_ref, loss_ref, dgp_ref, db_ref):
        @pl.when(pl.program_id(0) == 0)
        def _():
            loss_ref[...] = jnp.zeros(loss_ref.shape, F32)
            dgp_ref[...] = jnp.zeros(dgp_ref.shape, F32)
            db_ref[...] = jnp.zeros(db_ref.shape, F32)

        gp = gp_ref[...]
        pe = _dot(p_ref[...], wp_ref[...])
        r = lax.rsqrt(jnp.mean(pe * pe, axis=-1, keepdims=True) + EPS)
        peh = pe * r
        e = peh * gp
        h2 = h2_ref[...]
        h2b = h2.astype(BF16)
        h2b_ref[...] = h2b
        gt = _sig(jnp.dot(h2b, wg_ref[...], preferred_element_type=F32) + b_ref[...])
        diff = h2 + e * gt - t_ref[...]
        loss_ref[...] += _colsum(diff * diff)
        dh3 = diff * (1.0 / D_MODEL)
        de = dh3 * gt
        dz = dh3 * e * gt * (1.0 - gt)
        db_ref[...] += _colsum(dz)
        dgp_ref[...] += _colsum(de * peh)
        dpeh = de * gp
        dpe = r * (dpeh - peh * jnp.mean(dpeh * peh, axis=-1, keepdims=True))
        dzb = dz.astype(BF16)
        dz_ref[...] = dzb
        dpe_ref[...] = dpe.astype(BF16)
        dh2_ref[...] = dh3 + _dot_nt(dzb, wg_ref[...])

    return pl.pallas_call(
        body, name="ple_loss", grid=(S // tm,),
        in_specs=[_rows(tm, PLE_DIM), _rows(tm, D_MODEL), _rows(tm, D_MODEL), _full(PLE_DIM, D_MODEL),
                  _full(D_MODEL, D_MODEL), _full(1, D_MODEL), _full(1, D_MODEL)],
        out_specs=[_rows(tm, D_MODEL)] * 4 + [_full(1, D_MODEL)] * 3,
        out_shape=[_sds((S, D_MODEL), BF16), _sds((S, D_MODEL), BF16), _sds((S, D_MODEL), F32), _sds((S, D_MODEL), BF16)]
        + [_sds((1, D_MODEL), F32)] * 3,
        compiler_params=_cp(("arbitrary",)),
    )(p, h2, tgt, w_pp, w_pg, b_pg, g_ple)


def _wgrad(a, b, name, S):
    M = a.shape[1]
    N = b.shape[1]
    ts = min(2048, S)
    nsplit = 2 if M * N >= 2 * 1024 * 1024 else 1
    tn = N // nsplit

    def body(a_ref, b_ref, o_ref):
        @pl.when(pl.program_id(1) == 0)
        def _():
            o_ref[...] = jnp.zeros(o_ref.shape, F32)

        o_ref[...] += _dot_tn(a_ref[...], b_ref[...])

    return pl.pallas_call(
        body, name=name, grid=(nsplit, S // ts),
        in_specs=[pl.BlockSpec((ts, M), lambda j, s: (s, 0)), pl.BlockSpec((ts, tn), lambda j, s: (s, j))],
        out_specs=pl.BlockSpec((M, tn), lambda j, s: (0, j)), out_shape=_sds((M, N), F32),
        compiler_params=_cp(("parallel", "arbitrary")),
    )(a, b)


def _ffn_down_bwd(dh2, ff, g, w_down, gate, up, S):
    tm = min(256, S)
    tn = D_FF // 2

    def body(dh2_ref, ff_ref, g_ref, wd_ref, gate_ref, up_ref, dff_ref, dgate_ref, dup_ref, dg_ref):
        @pl.when(pl.program_id(0) == 0)
        def _():
            dg_ref[...] = jnp.zeros(dg_ref.shape, F32)

        dff, ga = _rms_bwd(dh2_ref[...], ff_ref[...].astype(F32), g_ref[...])
        dg_ref[...] += _colsum(ga)
        dffb = dff.astype(BF16)
        dff_ref[...] = dffb
        for seg in range(2):
            sl = slice(seg * tn, (seg + 1) * tn)
            dact = _dot_nt(dffb, wd_ref[sl, :])
            gt = gate_ref[:, sl].astype(F32)
            u = up_ref[:, sl].astype(F32)
            s = _sig(gt)
            dgate_ref[:, sl] = (dact * u * (s * (1.0 + gt * (1.0 - s)))).astype(BF16)
            dup_ref[:, sl] = (dact * (gt * s)).astype(BF16)

    return pl.pallas_call(
        body, name="ffn_down_bwd", grid=(S // tm,),
        in_specs=[_rows(tm, D_MODEL), _rows(tm, D_MODEL), _full(1, D_MODEL), _full(D_FF, D_MODEL), _rows(tm, D_FF),
                  _rows(tm, D_FF)],
        out_specs=[_rows(tm, D_MODEL), _rows(tm, D_FF), _rows(tm, D_FF), _full(1, D_MODEL)],
        out_shape=[_sds((S, D_MODEL), BF16), _sds((S, D_FF), BF16), _sds((S, D_FF), BF16), _sds((1, D_MODEL), F32)],
        compiler_params=_cp(("arbitrary",)),
    )(dh2, ff, g, w_down, gate, up)


def _ffn_up_bwd(dgate, dup, w_gate, w_up, h1, mix, dh2, g_pre, g_post, w_o, S, grads=()):
    tm = min(256, S)
    n = len(grads)
    last = S // tm - 1

    def body(dgate_ref, dup_ref, wg_ref, wu_ref, h1_ref, mix_ref, dh2_ref, g2_ref, g1_ref, wo_ref, *rest):
        g_ins = rest[:n]
        dh1_ref, dmix_ref, dro_ref, dmo_ref, dg2_ref, dg1_ref = rest[n:n + 6]
        g_outs, sems = rest[n + 6:2 * n + 6], rest[2 * n + 6:]

        @pl.when(pl.program_id(0) == 0)
        def _():
            dg2_ref[...] = jnp.zeros(dg2_ref.shape, F32)
            dg1_ref[...] = jnp.zeros(dg1_ref.shape, F32)
            for cp in (_swap_copies(g_ins, g_outs, sems) if n else []):
                cp.start()

        dhn = _dot_nt(dgate_ref[...], wg_ref[...]) + _dot_nt(dup_ref[...], wu_ref[...])
        d1, ga = _rms_bwd(dhn, h1_ref[...], g2_ref[...])
        dg2_ref[...] += _colsum(ga)
        dh1 = dh2_ref[...] + d1
        dh1_ref[...] = dh1
        dmix, gb = _rms_bwd(dh1, mix_ref[...].astype(F32), g1_ref[...])
        dg1_ref[...] += _colsum(gb)
        dmixb = dmix.astype(BF16)
        dmix_ref[...] = dmixb
        dcat = _dot_nt(dmixb, wo_ref[...])
        dro_ref[...] = dcat[:, 0:512].astype(BF16)
        dmo_ref[...] = dcat[:, 512:1024].astype(BF16)

        if n:
            @pl.when(pl.program_id(0) == last)
            def _():
                for cp in _swap_copies(g_ins, g_outs, sems):
                    cp.wait()

    dh1, dmix, dro, dmo, dg2, dg1, *got = pl.pallas_call(
        body, name="ffn_up_bwd", grid=(S // tm,),
        in_specs=[_rows(tm, D_FF), _rows(tm, D_FF), _full(D_MODEL, D_FF), _full(D_MODEL, D_FF), _rows(tm, D_MODEL),
                  _rows(tm, D_MODEL), _rows(tm, D_MODEL), _full(1, D_MODEL), _full(1, D_MODEL), _full(D_MODEL, D_MODEL)]
        + [_ANY] * n,
        out_specs=[_rows(tm, D_MODEL), _rows(tm, D_MODEL), _rows(tm, 512), _rows(tm, 512), _full(1, D_MODEL),
                   _full(1, D_MODEL)] + [_ANY] * n,
        out_shape=[_sds((S, D_MODEL), F32), _sds((S, D_MODEL), BF16), _sds((S, 512), BF16), _sds((S, 512), BF16),
                   _sds((1, D_MODEL), F32), _sds((1, D_MODEL), F32)] + _swap_out_shapes(grads),
        scratch_shapes=_swap_sems(n) if n else [],
        compiler_params=_cp(("arbitrary",)),
    )(dgate, dup, w_gate, w_up, h1, mix, dh2, g_pre, g_post, w_o, *grads)
    return dh1, dmix, dro, dmo, dg2, dg1, got


def _attn_delta(o, do, S):
    tm = min(512, S)

    def body(o_ref, do_ref, dot_ref, d_ref):
        do = do_ref[...].astype(F32)
        prod_t = (o_ref[...].astype(F32) * do).T
        dot_ref[...] = do.T.astype(BF16)
        for h in range(MLA_HEADS):
            d_ref[h // 2, (h % 2):(h % 2) + 1, :] = jnp.sum(prod_t[h * 64:(h + 1) * 64, :], axis=0, keepdims=True)

    return pl.pallas_call(
        body, name="attn_delta", grid=(S // tm,),
        in_specs=[_rows(tm, 512), _rows(tm, 512)],
        out_specs=[pl.BlockSpec((512, tm), lambda i: (0, i)), pl.BlockSpec((MLA_HEADS // 2, 2, tm), lambda i: (0, 0, i))],
        out_shape=[_sds((512, S), BF16), _sds((MLA_HEADS // 2, 2, S), F32)],
        compiler_params=_cp(("parallel",)),
    )(o, do)


def _flash_bwd(qp, kp, kt, v, do, dot, lse, delta, S, sums=()):
    tq = min(512, S)
    nq = S // tq
    RB = ATT_ROWS
    qb_of, kb_of, T = _tri_pairs(nq, k_major=True)
    n = len(sums)
    steps = (MLA_HEADS // 2) * T

    def body(qb_ref, kb_ref, q_ref, k_ref, kt_ref, v_ref, do_ref, dot_ref, lse_ref, dl_ref, *rest):
        g_ins, (dq_ref, dk_ref, dv_ref), g_outs = rest[:n], rest[n:n + 3], rest[n + 3:2 * n + 3]
        dk_sc, dv_sc, s_sc, dp_sc, p_sc, ds_sc = rest[2 * n + 3:2 * n + 9]
        sems = rest[2 * n + 9:]
        t = pl.program_id(1)
        qb = qb_ref[t]
        kb = kb_ref[t]
        lin = pl.program_id(0) * T + t

        if n:
            @pl.when(lin == 0)
            def _():
                for cp in _scatter_copies(g_ins, g_outs, sems):
                    cp.start()

        @pl.when(t == 0)
        def _():
            dq_ref[...] = jnp.zeros(dq_ref.shape, F32)

        @pl.when(qb == kb)
        def _():
            dk_sc[...] = jnp.zeros(dk_sc.shape, F32)
            dv_sc[...] = jnp.zeros(dv_sc.shape, F32)

        lane = lax.broadcasted_iota(jnp.int32, (tq, 128), 1)

        def step(masked):
            vv = v_ref[...]
            do_all = do_ref[...]
            mine = [lane < 64, lane >= 64]
            for a in range(2):
                sl = slice(a * 128, (a + 1) * 128)
                s_sc[a] = _dot_nt(k_ref[:, sl], q_ref[:, sl])
                dp_sc[a] = jnp.dot(jnp.where(mine[a], vv, jnp.zeros_like(vv)), dot_ref[...],
                                   preferred_element_type=F32)
            for a in range(2):
                sl = slice(a * 128, (a + 1) * 128)
                lse = lse_ref[a:a + 1, :]
                dl = dl_ref[a:a + 1, :]
                for r in range(0, tq, RB):
                    sc = s_sc[a, r:r + RB, :]
                    if masked:
                        sc = jnp.where(_causal_keep(r, RB, tq), sc, NEG)
                    p = jnp.exp(sc - lse)
                    p_sc[a, r:r + RB, :] = p.astype(BF16)
                    ds_sc[a, r:r + RB, :] = (p * (dp_sc[a, r:r + RB, :] - dl)).astype(BF16)
                ds = ds_sc[a]
                dv_sc[...] += jnp.dot(p_sc[a], jnp.where(mine[a], do_all, jnp.zeros_like(do_all)),
                                      preferred_element_type=F32)
                dk_sc[:, sl] += jnp.dot(ds, q_ref[:, sl], preferred_element_type=F32)
                dq_ref[qb, sl, :] += jnp.dot(kt_ref[sl, :], ds, preferred_element_type=F32)

        @pl.when(qb > kb)
        def _():
            step(False)

        @pl.when(qb == kb)
        def _():
            step(True)

        @pl.when(qb == nq - 1)
        def _():
            dk_ref[...] = dk_sc[...].astype(BF16)
            dv_ref[...] = dv_sc[...].astype(BF16)

        if n:
            @pl.when(lin == steps - 1)
            def _():
                for cp in _scatter_copies(g_ins, g_outs, sems):
                    cp.wait()

    grid_spec = pltpu.PrefetchScalarGridSpec(
        num_scalar_prefetch=2, grid=(MLA_HEADS // 2, T),
        in_specs=[pl.BlockSpec((tq, 256), lambda j, t, qb, kb: (qb[t], j)),
                  pl.BlockSpec((tq, 256), lambda j, t, qb, kb: (kb[t], j)),
                  pl.BlockSpec((256, tq), lambda j, t, qb, kb: (j, kb[t])),
                  pl.BlockSpec((tq, 128), lambda j, t, qb, kb: (kb[t], j)),
                  pl.BlockSpec((tq, 128), lambda j, t, qb, kb: (qb[t], j)),
                  pl.BlockSpec((128, tq), lambda j, t, qb, kb: (j, qb[t])),
                  pl.BlockSpec((None, 2, tq), lambda j, t, qb, kb: (j, 0, qb[t])),
                  pl.BlockSpec((None, 2, tq), lambda j, t, qb, kb: (j, 0, qb[t]))] + [_ANY] * n,
        out_specs=[pl.BlockSpec((nq, 256, tq), lambda j, t, qb, kb: (0, j, 0)),
                   pl.BlockSpec((tq, 256), lambda j, t, qb, kb: (kb[t], j)),
                   pl.BlockSpec((tq, 128), lambda j, t, qb, kb: (kb[t], j))] + [_ANY] * n,
        scratch_shapes=[pltpu.VMEM((tq, 256), F32), pltpu.VMEM((tq, 128), F32), pltpu.VMEM((2, tq, tq), F32),
                        pltpu.VMEM((2, tq, tq), F32), pltpu.VMEM((2, tq, tq), BF16), pltpu.VMEM((2, tq, tq), BF16)]
        + (_scatter_sems(n) if n else []),
    )
    dq, dk, dv, *parts = pl.pallas_call(
        body, name="flash_bwd", grid_spec=grid_spec,
        out_shape=[_sds((nq, 1024, tq), F32), _sds((S, 1024), BF16), _sds((S, 512), BF16)] + _scatter_out_shapes(sums),
        compiler_params=_cp(("arbitrary", "arbitrary")),
    )(qb_of, kb_of, qp, kp, kt, v, do, dot, lse, delta, *sums)
    return dq, dk, dv, parts


def _mla_up_bwd(dqp, dkp, dv, cq, ckv, gq, gkv, w_uq, w_ukv, tabs, S):
    tm = min(512, S)

    def body(dq_ref, dk_ref, dv_ref, cq_ref, ckv_ref, gq_ref, gkv_ref, wuq_ref, wukv_ref, cm_ref, sa_ref, sb_ref,
             dqh_ref, dkv_ref, dcq_ref, dckv_ref, dkr_ref, dgq_ref, dgkv_ref):
        @pl.when(pl.program_id(0) == 0)
        def _():
            dgq_ref[...] = jnp.zeros(dgq_ref.shape, F32)
            dgkv_ref[...] = jnp.zeros(dgkv_ref.shape, F32)

        cm = cm_ref[...]
        sa = sa_ref[...]
        sb = sb_ref[...]
        lane = lax.broadcasted_iota(jnp.int32, (tm, 128), 1)
        dkr_r = jnp.zeros((tm, 128), F32)
        for h in range(MLA_HEADS):
            sl = slice(h * 128, (h + 1) * 128)
            dqh_ref[:, sl] = (_unrope_mla(dq_ref[sl, :].T, cm, sa, sb) * SCALE_MLA).astype(BF16)
            gk = dk_ref[:, sl]
            dkr_r = dkr_r + gk.astype(F32)
            dkv_ref[:, sl] = gk
        dkr_r = jnp.where((lane >= 64) & (lane < 96), dkr_r, 0.0)
        dkr_ref[...] = _unrope_mla(dkr_r, cm, sa, sb).astype(BF16)
        dkv_ref[:, 1024:1536] = dv_ref[...]
        dcq, ga = _rms_bwd(_dot_nt(dqh_ref[...], wuq_ref[...]), cq_ref[...], gq_ref[...])
        dcq_ref[...] = dcq.astype(BF16)
        dgq_ref[...] += _colsum(ga)
        dckv, gb = _rms_bwd(_dot_nt(dkv_ref[...], wukv_ref[...]), ckv_ref[...], gkv_ref[...])
        dckv_ref[...] = dckv.astype(BF16)
        dgkv_ref[...] += _colsum(gb)

    per_q = dqp.shape[2] // tm
    return pl.pallas_call(
        body, name="mla_up_bwd", grid=(S // tm,),
        in_specs=[pl.BlockSpec((None, 1024, tm), lambda i: (i // per_q, 0, i % per_q)),
                  _rows(tm, 1024), _rows(tm, 512), _rows(tm, Q_LORA), _rows(tm, KV_LORA),
                  _full(1, Q_LORA), _full(1, KV_LORA), _full(Q_LORA, 1024), _full(KV_LORA, 1536)] + [_rows(tm, 128)] * 3,
        out_specs=[_rows(tm, 1024), _rows(tm, 1536), _rows(tm, Q_LORA), _rows(tm, KV_LORA), _rows(tm, 128),
                   _full(1, Q_LORA), _full(1, KV_LORA)],
        out_shape=[_sds((S, 1024), BF16), _sds((S, 1536), BF16), _sds((S, Q_LORA), BF16), _sds((S, KV_LORA), BF16),
                   _sds((S, 128), BF16), _sds((1, Q_LORA), F32), _sds((1, KV_LORA), F32)],
        compiler_params=_cp(("arbitrary",)),
    )(dqp, dkp, dv, cq, ckv, gq, gkv, w_uq, w_ukv, *tabs[2:])


def _ret_bwd(rq, rk, rv, rprev, ry, rg, dro, gn_w, tabs, S):
    C = RET_CHUNK
    N = S // C
    G = min(RET_GROUP, N)
    NB = N // G

    def body(lg_ref, q_ref, k_ref, v_ref, rp_ref, ry_ref, rg_ref, dro_ref, w_ref, cr_ref, sr_ref,
             drq_ref, drk_ref, drv_ref, drg_ref, dw_ref, g_sc):
        @pl.when(pl.program_id(1) == 0)
        def _():
            g_sc[...] = jnp.zeros(g_sc.shape, F32)
            dw_ref[...] = jnp.zeros(dw_ref.shape, F32)

        dmat, zeta, xi, g_chunk = _decay_terms(lg_ref)
        w = w_ref[...]
        gacc = g_sc[...]
        dw = jnp.zeros((1, 128), F32)
        for i in reversed(range(G)):
            rows = slice(i * C, (i + 1) * C)
            ry = ry_ref[rows, :]
            mu = jnp.mean(ry, axis=-1, keepdims=True)
            yc = ry - mu
            rstd = lax.rsqrt(jnp.mean(yc * yc, axis=-1, keepdims=True) + EPS)
            yh = yc * rstd
            g = rg_ref[rows, :]
            s = _sig(g)
            dout = dro_ref[rows, :].astype(F32)
            drg_ref[rows, :] = (dout * (yh * w) * (s * (1.0 + g * (1.0 - s)))).astype(BF16)
            dgn = dout * (g * s)
            dw = dw + _colsum(dgn * yh)
            dyh = dgn * w
            dry = rstd * (dyh - jnp.mean(dyh, axis=-1, keepdims=True) - yh * jnp.mean(dyh * yh, axis=-1, keepdims=True))
            do = dry.astype(BF16)

            q = q_ref[rows, :]
            k = k_ref[rows, :]
            v = v_ref[rows, :]
            gfut = gacc.astype(BF16)
            sc = (_dot_nt(q, k) * dmat).astype(BF16)
            dsc = (_dot_nt(do, v) * dmat).astype(BF16)
            dq = jnp.dot(dsc, k, preferred_element_type=F32) + _dot_nt(do, rp_ref[i]) * xi
            dk = _dot_tn(dsc, q) + _dot_nt(v, gfut) * zeta
            dv = _dot_tn(sc, do) + jnp.dot(k, gfut, preferred_element_type=F32) * zeta
            gacc = g_chunk * gacc + _dot_tn(q, xi * dry)
            cr = cr_ref[rows, :]
            sr = sr_ref[rows, :]
            drq_ref[rows, :] = _unrope_ret(dq, cr, sr).astype(BF16)
            drk_ref[rows, :] = _unrope_ret(dk * SCALE_RET, cr, sr).astype(BF16)
            drv_ref[rows, :] = dv.astype(BF16)
        g_sc[...] = gacc
        dw_ref[...] += dw

    blk = pl.BlockSpec((G * C, 128), lambda h, n: (NB - 1 - n, h))
    tab = pl.BlockSpec((G * C, 128), lambda h, n: (NB - 1 - n, 0))
    return pl.pallas_call(
        body, name="ret_bwd", grid=(RET_HEADS, NB),
        in_specs=[pl.BlockSpec((None, 8, 128), lambda h, n: (h, 0, 0)), blk, blk, blk,
                  pl.BlockSpec((G, 128, 128), lambda h, n: (h * NB + NB - 1 - n, 0, 0)), blk, blk, blk,
                  pl.BlockSpec((1, 128), lambda h, n: (0, h)), tab, tab],
        out_specs=[blk, blk, blk, blk, pl.BlockSpec((1, 128), lambda h, n: (0, h))],
        out_shape=[_sds((S, 512), BF16)] * 4 + [_sds((1, 512), F32)],
        scratch_shapes=[pltpu.VMEM((128, 128), F32)],
        compiler_params=_cp(("parallel", "arbitrary")),
    )(_decay_table(), rq, rk, rv, rprev, ry, rg, dro, gn_w, tabs[0], tabs[1])


def _inproj_bwd(drq, drk, drv, drg, dcq, dckv, dkr, w_in, dh1, x, g, S):
    tm = min(512, S)

    def body(drq_ref, drk_ref, drv_ref, drg_ref, dcq_ref, dckv_ref, dkr_ref, w_ref, dh1_ref, x_ref, g_ref,
             gx_ref, dproj_ref, dg_ref):
        @pl.when(pl.program_id(0) == 0)
        def _():
            dg_ref[...] = jnp.zeros(dg_ref.shape, F32)

        dproj_ref[:, 0:512] = drq_ref[...]
        dproj_ref[:, 512:1024] = drk_ref[...]
        dproj_ref[:, 1024:1536] = drv_ref[...]
        dproj_ref[:, 1536:2048] = drg_ref[...]
        dproj_ref[:, 2048:2432] = dcq_ref[...]
        dproj_ref[:, 2432:2688] = dckv_ref[...]
        dproj_ref[:, 2688:2816] = dkr_ref[...]
        dx, ga = _rms_bwd(_dot_nt(dproj_ref[...], w_ref[...]), x_ref[...], g_ref[...])
        gx_ref[...] = dh1_ref[...] + dx
        dg_ref[...] += _colsum(ga)

    return pl.pallas_call(
        body, name="inproj_bwd", grid=(S // tm,),
        in_specs=[_rows(tm, 512)] * 4 + [_rows(tm, Q_LORA), _rows(tm, KV_LORA), _rows(tm, 128),
                                         _full(D_MODEL, IN_COLS_P), _rows(tm, D_MODEL), _rows(tm, D_MODEL),
                                         _full(1, D_MODEL)],
        out_specs=[_rows(tm, D_MODEL), _rows(tm, IN_COLS_P), _full(1, D_MODEL)],
        out_shape=[_sds((S, D_MODEL), F32), _sds((S, IN_COLS_P), BF16), _sds((1, D_MODEL), F32)],
        compiler_params=_cp(("arbitrary",)),
    )(drq, drk, drv, drg, dcq, dckv, dkr, w_in, dh1, x, g)


def _pad_weights(w):
    w_in = w["w_in"]
    z = lambda r, c: jnp.zeros((r, c), BF16)
    w_in_p = jnp.concatenate([w_in[:, :2688], z(1024, 64), w_in[:, 2688:2720], z(1024, 32)], axis=1)
    w_uq_p = jnp.pad(w["w_uq"].reshape(Q_LORA, MLA_HEADS, 96), ((0, 0), (0, 0), (0, 32))).reshape(Q_LORA, 1024)
    ukv = w["w_ukv"].reshape(KV_LORA, MLA_HEADS, 128)
    k_part = jnp.pad(ukv[:, :, :64], ((0, 0), (0, 0), (0, 64))).reshape(KV_LORA, 1024)
    w_ukv_p = jnp.concatenate([k_part, ukv[:, :, 64:].reshape(KV_LORA, 512)], axis=1)
    return w_in_p, w_uq_p, w_ukv_p


BIG_SPEC = {n: (r, c, ax) for n, r, c, ax in BIG}
GATHER_FIRST = ("w_in", "w_uq", "w_ukv")
GATHER_LATE = tuple(n for n, _, _, _ in BIG if n not in GATHER_FIRST)
REDUCE_EARLY = ("w_ple_gate", "w_ple_proj", "w_down", "w_gate", "w_up")
REDUCE_LAST = tuple(n for n, _, _, _ in BIG if n not in REDUCE_EARLY)


def _local_step(x, p, pos_f, tgt, w, sm, late_shards=None, c_idx=None):
    S = x.shape[0]
    spread = late_shards is not None
    w = dict(w)
    tabs, first = _rope_tables(pos_f, S, [late_shards[n] for n in GATHER_FIRST] if spread else ())
    for i, n in enumerate(GATHER_FIRST if spread else ()):
        w[n] = _from_chips(first[i], BIG_SPEC[n][2])
    w_in_p, w_uq_p, w_ukv_p = _pad_weights(w)

    xn, rq, rk, rv, rg, cq, ckv, kr = _inproj(x, sm["pre_mix_norm"], w_in_p, tabs, S)
    cqn, ckvn, qp, kp, v, kt, vt = _mla_up(cq, ckv, kr, sm["mla_q_norm"], sm["mla_kv_norm"], w_uq_p, w_ukv_p, tabs, S)
    mo, lse, gathered = _flash_fwd(qp, kp, vt, S, [late_shards[n] for n in GATHER_LATE] if spread else ())
    for i, n in enumerate(GATHER_LATE if spread else ()):
        w[n] = _from_chips(gathered[i], BIG_SPEC[n][2])
    ry, ro, rprev = _ret_fwd(rq, rk, rv, rg, sm["ret_gn_w"], S)
    mix, h1, hn = _outproj(ro, mo, x, w["w_o"], sm["post_mix_norm"], sm["pre_ffn_norm"], S)
    gate, up, act = _ffn_up(hn, w["w_gate"], w["w_up"], S)
    ff, h2 = _ffn_down(act, w["w_down"], h1, sm["post_ffn_norm"], S)
    dz, dpe, dh2, h2b, loss_vec, d_ple_norm, d_b = _ple_loss(
        p, h2, tgt, w["w_ple_proj"], w["w_ple_gate"], sm["b_ple_gate"], sm["ple_norm"], S)

    gw = {}
    gs = {"ple_norm": d_ple_norm, "b_ple_gate": d_b}
    gw["w_ple_gate"] = _wgrad(h2b, dz, "wgrad_ple_gate", S)
    gw["w_ple_proj"] = _wgrad(p, dpe, "wgrad_ple_proj", S)
    dff, dgate, dup, gs["post_ffn_norm"] = _ffn_down_bwd(dh2, ff, sm["post_ffn_norm"], w["w_down"], gate, up, S)
    gw["w_down"] = _wgrad(act, dff, "wgrad_down", S)
    gw["w_gate"] = _wgrad(hn, dgate, "wgrad_gate", S)
    gw["w_up"] = _wgrad(hn, dup, "wgrad_up", S)
    g4 = [_by_chip(gw.pop(n), *BIG_SPEC[n]) for n in REDUCE_EARLY] if spread else []
    dh1, dmix, dro, dmo, gs["pre_ffn_norm"], gs["post_mix_norm"], got = _ffn_up_bwd(
        dgate, dup, w["w_gate"], w["w_up"], h1, mix, dh2, sm["pre_ffn_norm"], sm["post_mix_norm"], w["w_o"], S, g4)
    sums = [_add_half_rows(g4[i], got[i], c_idx, "rs_add_halves_" + n) for i, n in enumerate(REDUCE_EARLY)] if spread else []
    gw["w_o"] = jnp.concatenate([_wgrad(ro, dmix, "wgrad_o_ret", S), _wgrad(mo, dmix, "wgrad_o_mla", S)], axis=0)

    dmo_t, delta = _attn_delta(mo, dmo, S)
    dqp, dkp, dv, parts = _flash_bwd(qp, kp, kt, v, dmo, dmo_t, lse, delta, S, sums)
    dqh, dkv, dcq, dckv, dkr, gs["mla_q_norm"], gs["mla_kv_norm"] = _mla_up_bwd(
        dqp, dkp, dv, cq, ckv, sm["mla_q_norm"], sm["mla_kv_norm"], w_uq_p, w_ukv_p, tabs, S)
    g_uq_p = _wgrad(cqn, dqh, "wgrad_uq", S)
    g_ukv_p = _wgrad(ckvn, dkv, "wgrad_ukv", S)
    gw["w_uq"] = g_uq_p.reshape(Q_LORA, MLA_HEADS, 128)[:, :, :96].reshape(Q_LORA, 768)
    gw["w_ukv"] = jnp.concatenate(
        [g_ukv_p[:, :1024].reshape(KV_LORA, MLA_HEADS, 128)[:, :, :64], g_ukv_p[:, 1024:].reshape(KV_LORA, MLA_HEADS, 64)],
        axis=2).reshape(KV_LORA, 1024)

    drq, drk, drv, drg, gs["ret_gn_w"] = _ret_bwd(rq, rk, rv, rprev, ry, rg, dro, sm["ret_gn_w"], tabs, S)
    grad_x, dproj, gs["pre_mix_norm"] = _inproj_bwd(drq, drk, drv, drg, dcq, dckv, dkr, w_in_p, dh1, x,
                                                    sm["pre_mix_norm"], S)
    g_in_p = _wgrad(xn, dproj, "wgrad_in", S)
    gw["w_in"] = jnp.concatenate([g_in_p[:, :2688], g_in_p[:, 2752:2784]], axis=1)
    return loss_vec, grad_x, gw, gs, ((sums, parts) if spread else None)


def _my_place():
    x = lax.axis_index("x")
    y = lax.axis_index("y")
    c = lax.axis_index("c")
    return x, y, c


def _other_chips(x, y):
    return [(1 - x, y), (x, 1 - y), (1 - x, 1 - y)]


_ANY = pl.BlockSpec(memory_space=pl.ANY)


def _allreduce_small(vec):
    def body(v_ref, out_ref, slots, send, recv, lsem):
        x, y, c = _my_place()
        me = 4 * x + 2 * y + c
        mine = pltpu.make_async_copy(v_ref, slots.at[me], lsem)
        mine.start()
        cps = []
        for r in range(1, N_DEV):
            px = x ^ (r >> 2)
            py = y ^ ((r >> 1) & 1)
            pc = c ^ (r & 1)
            cps.append(pltpu.make_async_remote_copy(
                src_ref=v_ref, dst_ref=slots.at[me], send_sem=send.at[r - 1], recv_sem=recv.at[r - 1],
                device_id=(px, py, pc), device_id_type=MESH))
        for cp in cps:
            cp.start()
        for cp in cps:
            cp.wait()
        mine.wait()
        acc = slots[0]
        for d in range(1, N_DEV):
            acc = acc + slots[d]
        out_ref[...] = acc
        loss = jnp.sum(acc[9:10, :], axis=1, keepdims=True) * (0.5 / D_MODEL)
        out_ref[9:10, :] = jnp.broadcast_to(loss, (1, PACK_COLS))

    vm = pl.BlockSpec(memory_space=pltpu.VMEM)
    return pl.pallas_call(
        body, name="allreduce_small",
        in_specs=[vm], out_specs=vm, out_shape=_sds((SMALL_ROWS, PACK_COLS), F32),
        scratch_shapes=[pltpu.VMEM((N_DEV, SMALL_ROWS, PACK_COLS), F32), pltpu.SemaphoreType.DMA((N_DEV - 1,)),
                        pltpu.SemaphoreType.DMA((N_DEV - 1,)), pltpu.SemaphoreType.DMA],
    )(vec)


N_BIG = len(BIG)


def _half(c, rows, align):
    h = rows // 2
    return pl.ds(pl.multiple_of(c * h, align), h)


def _gather_out_shapes(shards):
    return [_sds((N_CHIPS,) + tuple(s.shape), BF16) for s in shards]


def _gather_sems(n):
    return [pltpu.SemaphoreType.DMA((n, 3))] * 4 + [pltpu.SemaphoreType.DMA((n,))] * 2


def _gather_phase(phase, ins, outs, sems):
    send1, recv1, send2, recv2, send3, recv3 = sems
    x, y, c = _my_place()
    me = 2 * x + y
    chips = _other_chips(x, y)
    sib = (x, y, 1 - c)
    for t in range(len(ins)):
        rows = ins[t].shape[0]
        half = _half(c, rows, 16)
        other = _half(1 - c, rows, 16)
        own = pltpu.make_async_remote_copy(
            src_ref=ins[t], dst_ref=outs[t].at[me], send_sem=send3.at[t], recv_sem=recv3.at[t],
            device_id=sib, device_id_type=MESH)
        if phase == 0:
            own.start()
        if phase == 2:
            own.wait()
        for k, (cx, cy) in enumerate(chips):
            src = 2 * cx + cy
            out = pltpu.make_async_remote_copy(
                src_ref=ins[t].at[half], dst_ref=outs[t].at[me, half], send_sem=send1.at[t, k],
                recv_sem=recv1.at[t, k], device_id=(cx, cy, c), device_id_type=MESH)
            landed = pltpu.make_async_remote_copy(
                src_ref=ins[t].at[half], dst_ref=outs[t].at[src, half], send_sem=send1.at[t, k],
                recv_sem=recv1.at[t, k], device_id=(cx, cy, c), device_id_type=MESH)
            fwd = pltpu.make_async_remote_copy(
                src_ref=outs[t].at[src, half], dst_ref=outs[t].at[src, half], send_sem=send2.at[t, k],
                recv_sem=recv2.at[t, k], device_id=sib, device_id_type=MESH)
            from_sib = pltpu.make_async_remote_copy(
                src_ref=outs[t].at[src, other], dst_ref=outs[t].at[src, other], send_sem=send2.at[t, k],
                recv_sem=recv2.at[t, k], device_id=sib, device_id_type=MESH)
            if phase == 0:
                out.start()
            if phase == 1:
                landed.wait_recv()
                fwd.start()
            if phase == 2:
                from_sib.wait_recv()
                out.wait_send()
                fwd.wait_send()


def _swap_copies(ins, outs, sems):
    send, recv = sems
    x, y, c = _my_place()
    return [pltpu.make_async_remote_copy(
        src_ref=ins[t].at[:, _half(1 - c, ins[t].shape[1], 8)], dst_ref=outs[t], send_sem=send.at[t],
        recv_sem=recv.at[t], device_id=(x, y, 1 - c), device_id_type=MESH) for t in range(len(ins))]


def _swap_out_shapes(gs):
    return [_sds((N_CHIPS, g.shape[1] // 2, g.shape[2]), F32) for g in gs]


def _swap_sems(n):
    return [pltpu.SemaphoreType.DMA((n,)), pltpu.SemaphoreType.DMA((n,))]


def _swap_half_rows(gs):
    n = len(gs)

    def body(*refs):
        cps = _swap_copies(refs[:n], refs[n:2 * n], refs[2 * n:])
        for cp in cps:
            cp.start()
        for cp in cps:
            cp.wait()

    return pl.pallas_call(
        body, name="rs_swap_halves",
        in_specs=[_ANY] * n, out_specs=[_ANY] * n, out_shape=_swap_out_shapes(gs), scratch_shapes=_swap_sems(n),
    )(*gs)


def _add_half_rows(g, got, c_idx, name):
    _, rows, cols = g.shape
    h = rows // 2

    def body(c_ref, a_ref, b_ref, o_ref):
        o_ref[...] = (a_ref[...] + b_ref[...]).astype(BF16)

    grid_spec = pltpu.PrefetchScalarGridSpec(
        num_scalar_prefetch=1, grid=(N_CHIPS,),
        in_specs=[pl.BlockSpec((None, h, cols), lambda j, c: (j, c[0], 0)),
                  pl.BlockSpec((None, h, cols), lambda j, c: (j, 0, 0))],
        out_specs=pl.BlockSpec((None, h, cols), lambda j, c: (j, 0, 0)),
    )
    return pl.pallas_call(
        body, name=name, grid_spec=grid_spec, out_shape=_sds((N_CHIPS, h, cols), BF16),
        compiler_params=_cp(("parallel",)),
    )(c_idx, g, got)


def _scatter_to_chips(ts):
    n = len(ts)

    def body(*refs):
        cps = _scatter_copies(refs[:n], refs[n:2 * n], refs[2 * n:])
        for cp in cps:
            cp.start()
        for cp in cps:
            cp.wait()

    return pl.pallas_call(
        body, name="rs_scatter_chips",
        in_specs=[_ANY] * n, out_specs=[_ANY] * n, out_shape=_scatter_out_shapes(ts), scratch_shapes=_scatter_sems(n),
    )(*ts)


def _scatter_copies(ins, outs, sems):
    send, recv = sems
    x, y, c = _my_place()
    return [pltpu.make_async_remote_copy(
        src_ref=ins[t].at[2 * cx + cy], dst_ref=outs[t].at[k], send_sem=send.at[t, k], recv_sem=recv.at[t, k],
        device_id=(cx, cy, c), device_id_type=MESH)
        for t in range(len(ins)) for k, (cx, cy) in enumerate(_other_chips(x, y))]


def _scatter_out_shapes(ts):
    return [_sds((3,) + tuple(t.shape[1:]), BF16) for t in ts]


def _scatter_sems(n):
    return [pltpu.SemaphoreType.DMA((n, 3)), pltpu.SemaphoreType.DMA((n, 3))]


def _add_four(mine, parts, place, name):
    _, h, cols = parts.shape

    def body(pl_ref, m_ref, p_ref, o_ref):
        o_ref[...] = ((m_ref[...].astype(F32) + p_ref[0].astype(F32)) + p_ref[1].astype(F32)) + p_ref[2].astype(F32)

    grid_spec = pltpu.PrefetchScalarGridSpec(
        num_scalar_prefetch=1, grid=(1,),
        in_specs=[pl.BlockSpec((None, h, cols), lambda i, pc: (pc[0], 0, 0)),
                  pl.BlockSpec((3, h, cols), lambda i, pc: (0, 0, 0))],
        out_specs=pl.BlockSpec((h, cols), lambda i, pc: (pc[1], 0)),
    )
    return pl.pallas_call(
        body, name=name, grid_spec=grid_spec, out_shape=_sds((2 * h, cols), F32),
        compiler_params=_cp(("arbitrary",)),
    )(place, mine, parts)


def _join_half_rows(rs):
    n = len(rs)

    def body(*refs):
        ins, outs = refs[:n], refs[n:2 * n]
        send, recv = refs[2 * n:]
        x, y, c = _my_place()
        cps = []
        for t in range(n):
            half = _half(c, outs[t].shape[0], 8)
            rc = pltpu.make_async_remote_copy(
                src_ref=ins[t].at[half], dst_ref=outs[t].at[half], send_sem=send.at[t], recv_sem=recv.at[t],
                device_id=(x, y, 1 - c), device_id_type=MESH)
            rc.start()
            cps.append(rc)
        for cp in cps:
            cp.wait()

    return pl.pallas_call(
        body, name="rs_join_halves",
        in_specs=[_ANY] * n, out_specs=[_ANY] * n,
        out_shape=[_sds(r.shape, F32) for r in rs],
        input_output_aliases={i: i for i in range(n)},
        scratch_shapes=[pltpu.SemaphoreType.DMA((n,))] * 2,
    )(*rs)


def _by_chip(full, rows, cols, axis):
    if axis == 0:
        return full.reshape(N_CHIPS, rows // N_CHIPS, cols)
    return full.reshape(rows, N_CHIPS, cols // N_CHIPS).transpose(1, 0, 2)


def _from_chips(parts, axis):
    _, r, c = parts.shape
    if axis == 0:
        return parts.reshape(N_CHIPS * r, c)
    return parts.transpose(1, 0, 2).reshape(r, N_CHIPS * c)


def _adamw(wt, g, m, v, name):
    _, R, C = wt.shape
    tr = R
    for cand in (256, 128, 64, 32, 16, 8):
        if R % cand == 0:
            tr = cand
            break

    def body(w_ref, g_ref, m_ref, v_ref, d_ref, nm_ref, nv_ref):
        gg = g_ref[...]
        m_new = ADAM_B1 * m_ref[...] + (1.0 - ADAM_B1) * gg
        v_new = ADAM_B2 * v_ref[...] + (1.0 - ADAM_B2) * (gg * gg)
        m_hat = m_new / (1.0 - ADAM_B1 ** ADAM_STEP)
        v_hat = v_new / (1.0 - ADAM_B2 ** ADAM_STEP)
        d_ref[...] = -ADAM_LR * (m_hat / (jnp.sqrt(v_hat) + ADAM_EPS) + ADAM_WD * w_ref[...])
        nm_ref[...] = m_new
        nv_ref[...] = v_new

    spec = pl.BlockSpec((None, tr, C), lambda i: (0, i, 0))
    return pl.pallas_call(
        body, name=name, grid=(R // tr,), in_specs=[spec, pl.BlockSpec((tr, C), lambda i: (i, 0)), spec, spec],
        out_specs=[spec] * 3, out_shape=[_sds((1, R, C), F32)] * 3,
        compiler_params=_cp(("parallel",)),
    )(wt, g, m, v)


def _pack_small(vals, loss_vec=None):
    rows = [jnp.pad(vals[n].reshape(-1), (0, PACK_COLS - sz)) for n, sz in SMALL]
    rows.append(loss_vec.reshape(-1) if loss_vec is not None else jnp.zeros((PACK_COLS,), F32))
    rows += [jnp.zeros((PACK_COLS,), F32)] * (SMALL_ROWS - len(rows))
    return jnp.stack(rows)


def kernel(x, p, positions, pre_mix_norm, w_in, ret_gn_w, mla_q_norm, w_uq, mla_kv_norm, w_ukv, w_o, post_mix_norm, pre_ffn_norm, w_gate, w_up, w_down, post_ffn_norm, w_ple_proj, ple_norm, w_ple_gate, b_ple_gate, loss_target, m_pre_mix_norm, m_w_in, m_ret_gn_w, m_mla_q_norm, m_w_uq, m_mla_kv_norm, m_w_ukv, m_w_o, m_post_mix_norm, m_pre_ffn_norm, m_w_gate, m_w_up, m_w_down, m_post_ffn_norm, m_w_ple_proj, m_ple_norm, m_w_ple_gate, m_b_ple_gate, v_pre_mix_norm, v_w_in, v_ret_gn_w, v_mla_q_norm, v_w_uq, v_mla_kv_norm, v_w_ukv, v_w_o, v_post_mix_norm, v_pre_ffn_norm, v_w_gate, v_w_up, v_w_down, v_post_ffn_norm, v_w_ple_proj, v_ple_norm, v_w_ple_gate, v_b_ple_gate):
    wts = dict(pre_mix_norm=pre_mix_norm, w_in=w_in, ret_gn_w=ret_gn_w, mla_q_norm=mla_q_norm, w_uq=w_uq,
               mla_kv_norm=mla_kv_norm, w_ukv=w_ukv, w_o=w_o, post_mix_norm=post_mix_norm, pre_ffn_norm=pre_ffn_norm,
               w_gate=w_gate, w_up=w_up, w_down=w_down, post_ffn_norm=post_ffn_norm, w_ple_proj=w_ple_proj,
               ple_norm=ple_norm, w_ple_gate=w_ple_gate, b_ple_gate=b_ple_gate)
    mom = dict(pre_mix_norm=m_pre_mix_norm, w_in=m_w_in, ret_gn_w=m_ret_gn_w, mla_q_norm=m_mla_q_norm, w_uq=m_w_uq,
               mla_kv_norm=m_mla_kv_norm, w_ukv=m_w_ukv, w_o=m_w_o, post_mix_norm=m_post_mix_norm,
               pre_ffn_norm=m_pre_ffn_norm, w_gate=m_w_gate, w_up=m_w_up, w_down=m_w_down, post_ffn_norm=m_post_ffn_norm,
               w_ple_proj=m_w_ple_proj, ple_norm=m_ple_norm, w_ple_gate=m_w_ple_gate, b_ple_gate=m_b_ple_gate)
    var = dict(pre_mix_norm=v_pre_mix_norm, w_in=v_w_in, ret_gn_w=v_ret_gn_w, mla_q_norm=v_mla_q_norm, w_uq=v_w_uq,
               mla_kv_norm=v_mla_kv_norm, w_ukv=v_w_ukv, w_o=v_w_o, post_mix_norm=v_post_mix_norm,
               pre_ffn_norm=v_pre_ffn_norm, w_gate=v_w_gate, w_up=v_w_up, w_down=v_w_down, post_ffn_norm=v_post_ffn_norm,
               w_ple_proj=v_w_ple_proj, ple_norm=v_ple_norm, w_ple_gate=v_w_ple_gate, b_ple_gate=v_b_ple_gate)

    S = x.shape[1]
    shard2d = {n: wts[n][0] for n, _, _, _ in BIG}
    small2d = {n: wts[n] for n, _ in SMALL}

    shard_bf = {n: shard2d[n].astype(BF16) for n in shard2d}
    pos_f = positions.astype(F32).reshape(S, 1)
    c_idx = lax.axis_index("c").astype(jnp.int32).reshape(1)
    loss_vec, grad_x, gw, gs, (sums_early, parts_early) = _local_step(
        x[0], p[0, 0], pos_f, loss_target[0], {}, small2d, shard_bf, c_idx)

    g4 = [_by_chip(gw[n], *BIG_SPEC[n]) for n in REDUCE_LAST]
    got = _swap_half_rows(g4)
    sums_last = [_add_half_rows(g4[i], got[i], c_idx, "rs_add_halves_" + n) for i, n in enumerate(REDUCE_LAST)]
    parts_last = _scatter_to_chips(sums_last)
    place = jnp.stack([2 * lax.axis_index("x") + lax.axis_index("y"), lax.axis_index("c")]).astype(jnp.int32)
    names = REDUCE_EARLY + REDUCE_LAST
    reduced = _join_half_rows(
        [_add_four(sm_, pt_, place, "rs_add_chips_" + n)
         for n, sm_, pt_ in zip(names, sums_early + sums_last, list(parts_early) + list(parts_last))])
    g_shard = dict(zip(names, reduced))

    small_sum = _allreduce_small(_pack_small(gs, loss_vec))
    loss = small_sum[9, 0]
    g_small = {n: small_sum[i:i + 1, :sz] for i, (n, sz) in enumerate(SMALL)}

    grads, delta, new_m, new_v = {}, {}, {}, {}
    for n, _, _, _ in BIG:
        delta[n], new_m[n], new_v[n] = _adamw(wts[n], g_shard[n], mom[n], var[n], "adamw_" + n)
        grads[n] = g_shard[n][None]
    d, nm, nv = _adamw(_pack_small(small2d)[None], small_sum, _pack_small(mom)[None], _pack_small(var)[None],
                       "adamw_small")
    for i, (n, sz) in enumerate(SMALL):
        grads[n] = g_small[n]
        delta[n], new_m[n], new_v[n] = d[0, i:i + 1, :sz], nm[0, i:i + 1, :sz], nv[0, i:i + 1, :sz]

    return (loss, grad_x[None], *[grads[n] for n in ALL_W], *[delta[n] for n in ALL_W],
            *[new_m[n] for n in ALL_W], *[new_v[n] for n in ALL_W])
```

```python
import functools
import math

import jax
import jax.numpy as jnp
import numpy as np
from jax import lax
from jax.experimental import pallas as pl
from jax.experimental.pallas import tpu as pltpu

F32 = jnp.float32
BF16 = jnp.bfloat16
MESH = pl.DeviceIdType.MESH

D_MODEL = 1024
D_FF = 2816
PLE_DIM = 256
RET_HEADS = 4
RET_DIM = 128
RET_WIDTH = 512
RET_CHUNK = 256
RET_GROUP = 4
MLA_HEADS = 8
MLA_NOPE = 64
MLA_ROPE = 32
MLA_V = 64
Q_LORA = 384
KV_LORA = 256
IN_COLS = 2720
IN_COLS_P = 2816
ROPE_BASE = 10000.0
EPS = 1e-6
SCALE_MLA = 1.0 / math.sqrt(MLA_NOPE + MLA_ROPE)
SCALE_RET = RET_DIM ** -0.5
NEG = -1e30

ADAM_LR = 0.001
ADAM_B1 = 0.9
ADAM_B2 = 0.999
ADAM_EPS = 1e-08
ADAM_WD = 0.01
ADAM_STEP = 10

N_CHIPS = 4
N_DEV = 8
VMEM_MB = 56

BIG = (
    ("w_in", 1024, 2720, 1),
    ("w_uq", 384, 768, 1),
    ("w_ukv", 256, 1024, 1),
    ("w_o", 1024, 1024, 0),
    ("w_gate", 1024, 2816, 1),
    ("w_up", 1024, 2816, 1),
    ("w_down", 2816, 1024, 0),
    ("w_ple_proj", 256, 1024, 1),
    ("w_ple_gate", 1024, 1024, 0),
)
SMALL = (
    ("pre_mix_norm", 1024),
    ("ret_gn_w", 512),
    ("mla_q_norm", 384),
    ("mla_kv_norm", 256),
    ("post_mix_norm", 1024),
    ("pre_ffn_norm", 1024),
    ("post_ffn_norm", 1024),
    ("ple_norm", 1024),
    ("b_ple_gate", 1024),
)
ALL_W = ("pre_mix_norm", "w_in", "ret_gn_w", "mla_q_norm", "w_uq", "mla_kv_norm", "w_ukv", "w_o", "post_mix_norm",
         "pre_ffn_norm", "w_gate", "w_up", "w_down", "post_ffn_norm", "w_ple_proj", "ple_norm", "w_ple_gate", "b_ple_gate")
PACK_COLS = 1024
SMALL_ROWS = 16


def _cp(sem=None, mb=VMEM_MB, **kw):
    return pltpu.CompilerParams(dimension_semantics=sem, vmem_limit_bytes=mb * 1024 * 1024, **kw)


def _bf(x):
    return x.astype(BF16)


def _dot(a, b):
    return jnp.dot(_bf(a), _bf(b), preferred_element_type=F32)


def _dot_nt(a, b):
    return lax.dot_general(_bf(a), _bf(b), (((1,), (1,)), ((), ())), preferred_element_type=F32)


def _dot_tn(a, b):
    return lax.dot_general(_bf(a), _bf(b), (((0,), (0,)), ((), ())), preferred_element_type=F32)


def _sig(x):
    return 1.0 / (1.0 + jnp.exp(-x))


def _rms(x, g):
    r = lax.rsqrt(jnp.mean(x * x, axis=-1, keepdims=True) + EPS)
    return x * r * g


def _rms_bwd(dy, x, g):
    r = lax.rsqrt(jnp.mean(x * x, axis=-1, keepdims=True) + EPS)
    xh = x * r
    dxh = dy * g
    dx = r * (dxh - xh * jnp.mean(dxh * xh, axis=-1, keepdims=True))
    return dx, dy * xh


def _colsum(x):
    return jnp.sum(x, axis=0, keepdims=True)


def _rope_ret(x, cr, sr):
    return x * cr + pltpu.roll(x, 64, 1) * sr


def _unrope_ret(dy, cr, sr):
    return dy * cr + pltpu.roll(dy * sr, 64, 1)


def _rope_mla(x, cm, sa, sb):
    return x * cm + pltpu.roll(x, 112, 1) * sa + pltpu.roll(x, 16, 1) * sb


def _unrope_mla(dy, cm, sa, sb):
    return dy * cm + pltpu.roll(dy * sa, 16, 1) + pltpu.roll(dy * sb, 112, 1)


def _rows(tm, w, col=0):
    return pl.BlockSpec((tm, w), lambda i: (i, col))


def _full(*shape):
    return pl.BlockSpec(shape, lambda i: (0,) * len(shape), pipeline_mode=pl.Buffered(1))


def _acc(*shape):
    return pl.BlockSpec(shape, lambda i: (0,) * len(shape))


def _sds(shape, dtype):
    return jax.ShapeDtypeStruct(shape, dtype)


def _rope_tables(pos_f, S, shards=()):
    tm = min(512, S)
    n = len(shards)
    steps = S // tm
    inv_r = (1.0 / (np.float32(ROPE_BASE) ** (np.arange(64, dtype=np.float32) / np.float32(64)))).astype(np.float32)
    inv_m16 = (1.0 / (np.float32(ROPE_BASE) ** (np.arange(16, dtype=np.float32) / np.float32(16)))).astype(np.float32)
    inv_r = np.concatenate([inv_r, inv_r])[None, :]
    inv_m = np.zeros((1, 128), np.float32)
    inv_m[0, 64:80] = inv_m16
    inv_m[0, 80:96] = inv_m16

    def body(pos_ref, invr_ref, invm_ref, *rest):
        w_ins, (cr_ref, sr_ref, cm_ref, sa_ref, sb_ref) = rest[:n], rest[n:n + 5]
        w_outs, sems = rest[n + 5:2 * n + 5], rest[2 * n + 5:]
        i = pl.program_id(0)
        if n:
            @pl.when(i == 0)
            def _():
                _gather_phase(0, w_ins, w_outs, sems)

            @pl.when(i == steps // 2)
            def _():
                _gather_phase(1, w_ins, w_outs, sems)

        pos = pos_ref[...]
        lane = lax.broadcasted_iota(jnp.int32, (tm, 128), 1)
        ar = pos * invr_ref[...]
        s = jnp.sin(ar)
        cr_ref[...] = jnp.cos(ar)
        sr_ref[...] = jnp.where(lane < 64, -s, s)
        am = pos * invm_ref[...]
        c2 = jnp.cos(am)
        s2 = jnp.sin(am)
        cm_ref[...] = jnp.where(lane < 64, 1.0, jnp.where(lane < 96, c2, 0.0))
        sa_ref[...] = jnp.where((lane >= 64) & (lane < 80), -s2, 0.0)
        sb_ref[...] = jnp.where((lane >= 80) & (lane < 96), s2, 0.0)

        if n:
            @pl.when(i == steps - 1)
            def _():
                _gather_phase(2, w_ins, w_outs, sems)

    outs = pl.pallas_call(
        body, name="rope_tables", grid=(steps,),
        in_specs=[_rows(tm, 1), _full(1, 128), _full(1, 128)] + [_ANY] * n,
        out_specs=[_rows(tm, 128)] * 5 + [_ANY] * n,
        out_shape=[_sds((S, 128), F32)] * 5 + _gather_out_shapes(shards),
        scratch_shapes=_gather_sems(n) if n else [],
        compiler_params=_cp(("arbitrary",)),
    )(pos_f, jnp.asarray(inv_r), jnp.asarray(inv_m), *shards)
    return outs[:5], outs[5:]


def _inproj(x, g, w_in, tabs, S):
    tm = min(512, S)

    def body(x_ref, g_ref, w_ref, cr_ref, sr_ref, cm_ref, sa_ref, sb_ref,
             xn_ref, rq_ref, rk_ref, rv_ref, rg_ref, cq_ref, ckv_ref, kr_ref):
        xb = _rms(x_ref[...], g_ref[...]).astype(BF16)
        xn_ref[...] = xb
        cr = cr_ref[...]
        sr = sr_ref[...]
        q = jnp.dot(xb, w_ref[:, 0:512], preferred_element_type=F32)
        k = jnp.dot(xb, w_ref[:, 512:1024], preferred_element_type=F32)
        for h in range(RET_HEADS):
            sl = slice(h * 128, (h + 1) * 128)
            rq_ref[:, sl] = _rope_ret(q[:, sl], cr, sr).astype(BF16)
            rk_ref[:, sl] = (_rope_ret(k[:, sl], cr, sr) * SCALE_RET).astype(BF16)
        rv_ref[...] = jnp.dot(xb, w_ref[:, 1024:1536], preferred_element_type=F32).astype(BF16)
        rg_ref[...] = jnp.dot(xb, w_ref[:, 1536:2048], preferred_element_type=F32)
        cq_ref[...] = jnp.dot(xb, w_ref[:, 2048:2432], preferred_element_type=F32)
        ckv_ref[...] = jnp.dot(xb, w_ref[:, 2432:2688], preferred_element_type=F32)
        kr = jnp.dot(xb, w_ref[:, 2688:2816], preferred_element_type=F32)
        kr_ref[...] = _rope_mla(kr, cm_ref[...], sa_ref[...], sb_ref[...])

    return pl.pallas_call(
        body, name="inproj", grid=(S // tm,),
        in_specs=[_rows(tm, D_MODEL), _full(1, D_MODEL), _full(D_MODEL, IN_COLS_P)] + [_rows(tm, 128)] * 5,
        out_specs=[_rows(tm, D_MODEL)] + [_rows(tm, 512)] * 4 + [_rows(tm, Q_LORA), _rows(tm, KV_LORA), _rows(tm, 128)],
        out_shape=[_sds((S, D_MODEL), BF16)] + [_sds((S, 512), BF16)] * 3
        + [_sds((S, 512), F32), _sds((S, Q_LORA), F32), _sds((S, KV_LORA), F32), _sds((S, 128), F32)],
        compiler_params=_cp(("parallel",)),
    )(x, g, w_in, *tabs)


def _mla_up(cq, ckv, kr, gq, gkv, w_uq, w_ukv, tabs, S):
    tm = min(512, S)

    def body(cq_ref, ckv_ref, kr_ref, gq_ref, gkv_ref, wuq_ref, wukv_ref, cm_ref, sa_ref, sb_ref,
             cqn_ref, ckvn_ref, qp_ref, kp_ref, v_ref, kt_ref, vt_ref):
        cm = cm_ref[...]
        sa = sa_ref[...]
        sb = sb_ref[...]
        cqn = _rms(cq_ref[...], gq_ref[...]).astype(BF16)
        cqn_ref[...] = cqn
        ckvn = _rms(ckv_ref[...], gkv_ref[...]).astype(BF16)
        ckvn_ref[...] = ckvn
        qh = jnp.dot(cqn, wuq_ref[...], preferred_element_type=F32)
        kv = jnp.dot(ckvn, wukv_ref[...], preferred_element_type=F32)
        kr_blk = kr_ref[...]
        for h in range(MLA_HEADS):
            sl = slice(h * 128, (h + 1) * 128)
            qp_ref[:, sl] = (_rope_mla(qh[:, sl], cm, sa, sb) * SCALE_MLA).astype(BF16)
            kh = kv[:, sl] + kr_blk
            kp_ref[:, sl] = kh.astype(BF16)
            kt_ref[sl, :] = kh.T.astype(BF16)
        for h in range(MLA_HEADS // 2):
            vh = kv[:, 1024 + h * 128:1024 + (h + 1) * 128]
            v_ref[:, h * 128:(h + 1) * 128] = vh.astype(BF16)
            vt_ref[h * 128:(h + 1) * 128, :] = vh.T.astype(BF16)

    cols = lambda r: pl.BlockSpec((r, tm), lambda i: (0, i))
    return pl.pallas_call(
        body, name="mla_up", grid=(S // tm,),
        in_specs=[_rows(tm, Q_LORA), _rows(tm, KV_LORA), _rows(tm, 128), _full(1, Q_LORA), _full(1, KV_LORA),
                  _full(Q_LORA, 1024), _full(KV_LORA, 1536)] + [_rows(tm, 128)] * 3,
        out_specs=[_rows(tm, Q_LORA), _rows(tm, KV_LORA), _rows(tm, 1024), _rows(tm, 1024), _rows(tm, 512),
                   cols(1024), cols(512)],
        out_shape=[_sds((S, Q_LORA), BF16), _sds((S, KV_LORA), BF16), _sds((S, 1024), BF16), _sds((S, 1024), BF16),
                   _sds((S, 512), BF16), _sds((1024, S), BF16), _sds((512, S), BF16)],
        compiler_params=_cp(("parallel",)),
    )(cq, ckv, kr, gq, gkv, w_uq, w_ukv, *tabs[2:])


def _tri_pairs(nq, k_major):
    if k_major:
        pairs = [(qb, kb) for kb in range(nq) for qb in range(kb, nq)]
    else:
        pairs = [(qb, kb) for qb in range(nq) for kb in range(qb + 1)]
    qb_of = np.array([p[0] for p in pairs], np.int32)
    kb_of = np.array([p[1] for p in pairs], np.int32)
    return jnp.asarray(qb_of), jnp.asarray(kb_of), len(pairs)


ATT_ROWS = 32
FWD_HEADS = 8
BWD_HEADS = 4


def _causal_keep(r0, rows, tq):
    key = r0 + lax.broadcasted_iota(jnp.int32, (rows, tq), 0)
    qry = lax.broadcasted_iota(jnp.int32, (rows, tq), 1)
    return key <= qry


def _flash_fwd(qp, kp, vt, S, shards=()):
    tq = min(512, S)
    nq = S // tq
    RB = ATT_ROWS
    NH = FWD_HEADS
    qb_of, kb_of, T = _tri_pairs(nq, k_major=False)
    n = len(shards)
    steps = (MLA_HEADS // NH) * T

    def body(qb_ref, kb_ref, q_ref, k_ref, vt_ref, *rest):
        w_ins, (o_ref, lse_ref), w_outs = rest[:n], rest[n:n + 2], rest[n + 2:2 * n + 2]
        m_sc, l_sc, acc_sc, s_sc, p_sc = rest[2 * n + 2:2 * n + 7]
        sems = rest[2 * n + 7:]
        t = pl.program_id(1)
        qb = qb_ref[t]
        kb = kb_ref[t]
        lin = pl.program_id(0) * T + t

        if n:
            @pl.when(lin == 0)
            def _():
                _gather_phase(0, w_ins, w_outs, sems)

            @pl.when(lin == steps // 2)
            def _():
                _gather_phase(1, w_ins, w_outs, sems)

        @pl.when(kb == 0)
        def _():
            m_sc[...] = jnp.full(m_sc.shape, NEG, F32)
            l_sc[...] = jnp.zeros(l_sc.shape, F32)
            acc_sc[...] = jnp.zeros(acc_sc.shape, F32)

        def step(masked):
            for a in range(NH):
                sl = slice(a * 128, (a + 1) * 128)
                s_sc[a] = _dot_nt(k_ref[:, sl], q_ref[:, sl])
            m_new, al = [], []
            for a in range(NH):
                mx = [jnp.full((8, tq), NEG, F32) for _ in range(RB // 8)]
                for r in range(0, tq, RB):
                    sc = s_sc[a, r:r + RB, :]
                    if masked:
                        sc = jnp.where(_causal_keep(r, RB, tq), sc, NEG)
                        s_sc[a, r:r + RB, :] = sc
                    for i in range(RB // 8):
                        mx[i] = jnp.maximum(mx[i], sc[i * 8:(i + 1) * 8, :])
                mx8 = jnp.maximum(jnp.maximum(mx[0], mx[1]), jnp.maximum(mx[2], mx[3]))
                m_prev = m_sc[a]
                m_new.append(jnp.maximum(m_prev, jnp.max(mx8, axis=0, keepdims=True)))
                al.append(jnp.exp(m_prev - m_new[a]))
                m_sc[a] = m_new[a]
            for a in range(NH):
                ls = [jnp.zeros((8, tq), F32) for _ in range(RB // 8)]
                for r in range(0, tq, RB):
                    p = jnp.exp(s_sc[a, r:r + RB, :] - m_new[a])
                    for i in range(RB // 8):
                        ls[i] = ls[i] + p[i * 8:(i + 1) * 8, :]
                    p_sc[a, r:r + RB, :] = p.astype(BF16)
                l_sc[a] = al[a] * l_sc[a] + jnp.sum((ls[0] + ls[1]) + (ls[2] + ls[3]), axis=0, keepdims=True)
                pair = slice((a // 2) * 128, (a // 2 + 1) * 128)
                pv = jnp.dot(vt_ref[pair, :], p_sc[a], preferred_element_type=F32)
                rs = slice(a * 64, (a + 1) * 64)
                own = slice((a % 2) * 64, (a % 2 + 1) * 64)
                acc_sc[rs, :] = acc_sc[rs, :] * al[a] + pv[own, :]

        @pl.when(kb < qb)
        def _():
            step(False)

        @pl.when(kb == qb)
        def _():
            step(True)
            for a in range(NH):
                rs = slice(a * 64, (a + 1) * 64)
                acc_sc[rs, :] = acc_sc[rs, :] / l_sc[a]
                lse_ref[a:a + 1, :] = m_sc[a] + jnp.log(l_sc[a])
            o_ref[...] = acc_sc[...].T.astype(BF16)

        if n:
            @pl.when(lin == steps - 1)
            def _():
                _gather_phase(2, w_ins, w_outs, sems)

    grid_spec = pltpu.PrefetchScalarGridSpec(
        num_scalar_prefetch=2, grid=(MLA_HEADS // NH, T),
        in_specs=[pl.BlockSpec((tq, 128 * NH), lambda j, t, qb, kb: (qb[t], j)),
                  pl.BlockSpec((tq, 128 * NH), lambda j, t, qb, kb: (kb[t], j)),
                  pl.BlockSpec((64 * NH, tq), lambda j, t, qb, kb: (j, kb[t]))] + [_ANY] * n,
        out_specs=[pl.BlockSpec((tq, 64 * NH), lambda j, t, qb, kb: (qb[t], j)),
                   pl.BlockSpec((None, NH, tq), lambda j, t, qb, kb: (j, 0, qb[t]))] + [_ANY] * n,
        scratch_shapes=[pltpu.VMEM((NH, 1, tq), F32), pltpu.VMEM((NH, 1, tq), F32), pltpu.VMEM((64 * NH, tq), F32),
                        pltpu.VMEM((NH, tq, tq), F32), pltpu.VMEM((NH, tq, tq), BF16)] + (_gather_sems(n) if n else []),
    )
    out, lse, *gathered = pl.pallas_call(
        body, name="flash_fwd", grid_spec=grid_spec,
        out_shape=[_sds((S, 512), BF16), _sds((MLA_HEADS // NH, NH, S), F32)] + _gather_out_shapes(shards),
        compiler_params=_cp(("arbitrary", "arbitrary")),
    )(qb_of, kb_of, qp, kp, vt, *shards)
    return out, lse.reshape(MLA_HEADS // 2, 2, S), gathered


def _decay_table():
    log_g = np.log(1.0 - 2.0 ** (-5.0 - np.arange(RET_HEADS, dtype=np.float32))).astype(np.float32)
    return jnp.asarray(np.broadcast_to(log_g[:, None, None], (RET_HEADS, 8, 128)).copy())


def _decay_terms(lg_ref):
    C = RET_CHUNK
    lg = lg_ref[0:1, :]
    row = lax.broadcasted_iota(jnp.int32, (C, C), 0)
    col = lax.broadcasted_iota(jnp.int32, (C, C), 1)
    diff = (row - col).astype(F32)
    dmat = jnp.where(diff >= 0, jnp.exp(jnp.maximum(diff, 0.0) * jnp.tile(lg, (1, C // 128))), 0.0)
    j = lax.broadcasted_iota(jnp.int32, (C, 1), 0).astype(F32)
    lg1 = lg[:, 0:1]
    zeta = jnp.exp((C - 1 - j) * lg1)
    xi = jnp.exp((j + 1.0) * lg1)
    g_chunk = jnp.exp(C * lg1)
    return dmat, zeta, xi, g_chunk


def _ret_fwd(rq, rk, rv, rg, gn_w, S):
    C = RET_CHUNK
    N = S // C
    G = min(RET_GROUP, N)
    NB = N // G

    def body(lg_ref, q_ref, k_ref, v_ref, rg_ref, w_ref, ry_ref, ro_ref, rprev_ref, r_sc):
        @pl.when(pl.program_id(1) == 0)
        def _():
            r_sc[...] = jnp.zeros(r_sc.shape, F32)

        dmat, zeta, xi, g_chunk = _decay_terms(lg_ref)
        w = w_ref[...]
        r = r_sc[...]
        for i in range(G):
            rows = slice(i * C, (i + 1) * C)
            q = q_ref[rows, :]
            k = k_ref[rows, :]
            v = v_ref[rows, :]
            r_prev = r.astype(BF16)
            rprev_ref[i] = r_prev
            sc = _dot_nt(q, k) * dmat
            ry = _dot(sc, v) + jnp.dot(q, r_prev, preferred_element_type=F32) * xi
            ry_ref[rows, :] = ry
            r = g_chunk * r + _dot_tn(k, zeta * v.astype(F32))
            mu = jnp.mean(ry, axis=-1, keepdims=True)
            yc = ry - mu
            yh = yc * lax.rsqrt(jnp.mean(yc * yc, axis=-1, keepdims=True) + EPS)
            g = rg_ref[rows, :]
            ro_ref[rows, :] = (g * _sig(g) * (yh * w)).astype(BF16)
        r_sc[...] = r

    blk = pl.BlockSpec((G * C, 128), lambda h, n: (n, h))
    return pl.pallas_call(
        body, name="ret_fwd", grid=(RET_HEADS, NB),
        in_specs=[pl.BlockSpec((None, 8, 128), lambda h, n: (h, 0, 0)), blk, blk, blk, blk,
                  pl.BlockSpec((1, 128), lambda h, n: (0, h))],
        out_specs=[blk, blk, pl.BlockSpec((G, 128, 128), lambda h, n: (h * NB + n, 0, 0))],
        out_shape=[_sds((S, 512), F32), _sds((S, 512), BF16), _sds((RET_HEADS * N, 128, 128), BF16)],
        scratch_shapes=[pltpu.VMEM((128, 128), F32)],
        compiler_params=_cp(("parallel", "arbitrary")),
    )(_decay_table(), rq, rk, rv, rg, gn_w)


def _outproj(ro, mo, x, w_o, g_post, g_pre, S):
    tm = min(512, S)

    def body(ro_ref, mo_ref, x_ref, wo_ref, g1_ref, g2_ref, mix_ref, h1_ref, hn_ref):
        mix = (jnp.dot(ro_ref[...], wo_ref[0:512, :], preferred_element_type=F32)
               + jnp.dot(mo_ref[...], wo_ref[512:1024, :], preferred_element_type=F32))
        mix_ref[...] = mix.astype(BF16)
        h1 = x_ref[...] + _rms(mix, g1_ref[...])
        h1_ref[...] = h1
        hn_ref[...] = _rms(h1, g2_ref[...]).astype(BF16)

    return pl.pallas_call(
        body, name="outproj", grid=(S // tm,),
        in_specs=[_rows(tm, 512), _rows(tm, 512), _rows(tm, D_MODEL), _full(D_MODEL, D_MODEL), _full(1, D_MODEL),
                  _full(1, D_MODEL)],
        out_specs=[_rows(tm, D_MODEL)] * 3,
        out_shape=[_sds((S, D_MODEL), BF16), _sds((S, D_MODEL), F32), _sds((S, D_MODEL), BF16)],
        compiler_params=_cp(("parallel",)),
    )(ro, mo, x, w_o, g_post, g_pre)


def _ffn_up(hn, w_gate, w_up, S):
    tm = min(512, S)
    tn = D_FF // 2

    def body(hn_ref, wg_ref, wu_ref, gate_ref, up_ref, act_ref):
        hn_b = hn_ref[...]
        g = jnp.dot(hn_b, wg_ref[...], preferred_element_type=F32)
        u = jnp.dot(hn_b, wu_ref[...], preferred_element_type=F32)
        gate_ref[...] = g.astype(BF16)
        up_ref[...] = u.astype(BF16)
        act_ref[...] = (g * _sig(g) * u).astype(BF16)

    wspec = pl.BlockSpec((D_MODEL, tn), lambda j, i: (0, j))
    ospec = pl.BlockSpec((tm, tn), lambda j, i: (i, j))
    return pl.pallas_call(
        body, name="ffn_up", grid=(2, S // tm),
        in_specs=[pl.BlockSpec((tm, D_MODEL), lambda j, i: (i, 0)), wspec, wspec],
        out_specs=[ospec] * 3, out_shape=[_sds((S, D_FF), BF16)] * 3,
        compiler_params=_cp(("parallel", "parallel")),
    )(hn, w_gate, w_up)


def _ffn_down(act, w_down, h1, g, S):
    tm = min(512, S)

    def body(act_ref, wd_ref, h1_ref, g_ref, ff_ref, h2_ref):
        ff = jnp.dot(act_ref[...], wd_ref[...], preferred_element_type=F32)
        ff_ref[...] = ff.astype(BF16)
        h2_ref[...] = h1_ref[...] + _rms(ff, g_ref[...])

    return pl.pallas_call(
        body, name="ffn_down", grid=(S // tm,),
        in_specs=[_rows(tm, D_FF), _full(D_FF, D_MODEL), _rows(tm, D_MODEL), _full(1, D_MODEL)],
        out_specs=[_rows(tm, D_MODEL)] * 2, out_shape=[_sds((S, D_MODEL), BF16), _sds((S, D_MODEL), F32)],
        compiler_params=_cp(("parallel",)),
    )(act, w_down, h1, g)


def _ple_loss(p, h2, tgt, w_pp, w_pg, b_pg, g_ple, S):
    tm = min(512, S)

    def body(p_ref, h2_ref, t_ref, wp_ref, wg_ref, b_ref, gp_ref,
             dz_ref, dpe_ref, dh2_ref, h2b_ref, loss_ref, dgp_ref, db_ref):
        @pl.when(pl.program_id(0) == 0)
        def _():
            loss_ref[...] = jnp.zeros(loss_ref.shape, F32)
            dgp_ref[...] = jnp.zeros(dgp_ref.shape, F32)
            db_ref[...] = jnp.zeros(db_ref.shape, F32)

        gp = gp_ref[...]
        pe = _dot(p_ref[...], wp_ref[...])
        r = lax.rsqrt(jnp.mean(pe * pe, axis=-1, keepdims=True) + EPS)
        peh = pe * r
        e = peh * gp
        h2 = h2_ref[...]
        h2b = h2.astype(BF16)
        h2b_ref[...] = h2b
        gt = _sig(jnp.dot(h2b, wg_ref[...], preferred_element_type=F32) + b_ref[...])
        diff = h2 + e * gt - t_ref[...]
        loss_ref[...] += _colsum(diff * diff)
        dh3 = diff * (1.0 / D_MODEL)
        de = dh3 * gt
        dz = dh3 * e * gt * (1.0 - gt)
        db_ref[...] += _colsum(dz)
        dgp_ref[...] += _colsum(de * peh)
        dpeh = de * gp
        dpe = r * (dpeh - peh * jnp.mean(dpeh * peh, axis=-1, keepdims=True))
        dzb = dz.astype(BF16)
        dz_ref[...] = dzb
        dpe_ref[...] = dpe.astype(BF16)
        dh2_ref[...] = dh3 + _dot_nt(dzb, wg_ref[...])

    return pl.pallas_call(
        body, name="ple_loss", grid=(S // tm,),
        in_specs=[_rows(tm, PLE_DIM), _rows(tm, D_MODEL), _rows(tm, D_MODEL), _full(PLE_DIM, D_MODEL),
                  _full(D_MODEL, D_MODEL), _full(1, D_MODEL), _full(1, D_MODEL)],
        out_specs=[_rows(tm, D_MODEL)] * 4 + [_acc(1, D_MODEL)] * 3,
        out_shape=[_sds((S, D_MODEL), BF16), _sds((S, D_MODEL), BF16), _sds((S, D_MODEL), F32), _sds((S, D_MODEL), BF16)]
        + [_sds((1, D_MODEL), F32)] * 3,
        compiler_params=_cp(("arbitrary",)),
    )(p, h2, tgt, w_pp, w_pg, b_pg, g_ple)


def _wgrad(a, b, name, S):
    M = a.shape[1]
    N = b.shape[1]
    ts = min(2048, S)
    nsplit = 2 if M * N >= 2 * 1024 * 1024 else 1
    tn = N // nsplit

    def body(a_ref, b_ref, o_ref):
        @pl.when(pl.program_id(1) == 0)
        def _():
            o_ref[...] = jnp.zeros(o_ref.shape, F32)

        o_ref[...] += _dot_tn(a_ref[...], b_ref[...])

    return pl.pallas_call(
        body, name=name, grid=(nsplit, S // ts),
        in_specs=[pl.BlockSpec((ts, M), lambda j, s: (s, 0)), pl.BlockSpec((ts, tn), lambda j, s: (s, j))],
        out_specs=pl.BlockSpec((M, tn), lambda j, s: (0, j)), out_shape=_sds((M, N), F32),
        compiler_params=_cp(("parallel", "arbitrary")),
    )(a, b)


def _ffn_down_bwd(dh2, ff, g, w_down, gate, up, S):
    tm = min(512, S)
    tn = D_FF // 2

    def body(dh2_ref, ff_ref, g_ref, wd_ref, gate_ref, up_ref, dff_ref, dgate_ref, dup_ref, dg_ref):
        @pl.when(pl.program_id(0) == 0)
        def _():
            dg_ref[...] = jnp.zeros(dg_ref.shape, F32)

        dff, ga = _rms_bwd(dh2_ref[...], ff_ref[...].astype(F32), g_ref[...])
        dg_ref[...] += _colsum(ga)
        dffb = dff.astype(BF16)
        dff_ref[...] = dffb
        for seg in range(2):
            sl = slice(seg * tn, (seg + 1) * tn)
            dact = _dot_nt(dffb, wd_ref[sl, :])
            gt = gate_ref[:, sl].astype(F32)
            u = up_ref[:, sl].astype(F32)
            s = _sig(gt)
            dgate_ref[:, sl] = (dact * u * (s * (1.0 + gt * (1.0 - s)))).astype(BF16)
            dup_ref[:, sl] = (dact * (gt * s)).astype(BF16)

    return pl.pallas_call(
        body, name="ffn_down_bwd", grid=(S // tm,),
        in_specs=[_rows(tm, D_MODEL), _rows(tm, D_MODEL), _full(1, D_MODEL), _full(D_FF, D_MODEL), _rows(tm, D_FF),
                  _rows(tm, D_FF)],
        out_specs=[_rows(tm, D_MODEL), _rows(tm, D_FF), _rows(tm, D_FF), _acc(1, D_MODEL)],
        out_shape=[_sds((S, D_MODEL), BF16), _sds((S, D_FF), BF16), _sds((S, D_FF), BF16), _sds((1, D_MODEL), F32)],
        compiler_params=_cp(("arbitrary",)),
    )(dh2, ff, g, w_down, gate, up)


def _ffn_up_bwd(dgate, dup, w_gate, w_up, h1, mix, dh2, g_pre, g_post, w_o, S, grads=()):
    tm = min(512, S)
    n = len(grads)
    last = S // tm - 1

    def body(dgate_ref, dup_ref, wg_ref, wu_ref, h1_ref, mix_ref, dh2_ref, g2_ref, g1_ref, wo_ref, *rest):
        g_ins = rest[:n]
        dh1_ref, dmix_ref, dro_ref, dmo_ref, dg2_ref, dg1_ref = rest[n:n + 6]
        g_outs, sems = rest[n + 6:2 * n + 6], rest[2 * n + 6:]

        @pl.when(pl.program_id(0) == 0)
        def _():
            dg2_ref[...] = jnp.zeros(dg2_ref.shape, F32)
            dg1_ref[...] = jnp.zeros(dg1_ref.shape, F32)
            for cp in (_swap_copies(g_ins, g_outs, sems) if n else []):
                cp.start()

        dhn = _dot_nt(dgate_ref[...], wg_ref[...]) + _dot_nt(dup_ref[...], wu_ref[...])
        d1, ga = _rms_bwd(dhn, h1_ref[...], g2_ref[...])
        dg2_ref[...] += _colsum(ga)
        dh1 = dh2_ref[...] + d1
        dh1_ref[...] = dh1
        dmix, gb = _rms_bwd(dh1, mix_ref[...].astype(F32), g1_ref[...])
        dg1_ref[...] += _colsum(gb)
        dmixb = dmix.astype(BF16)
        dmix_ref[...] = dmixb
        dcat = _dot_nt(dmixb, wo_ref[...])
        dro_ref[...] = dcat[:, 0:512].astype(BF16)
        dmo_ref[...] = dcat[:, 512:1024].astype(BF16)

        if n:
            @pl.when(pl.program_id(0) == last)
            def _():
                for cp in _swap_copies(g_ins, g_outs, sems):
                    cp.wait()

    dh1, dmix, dro, dmo, dg2, dg1, *got = pl.pallas_call(
        body, name="ffn_up_bwd", grid=(S // tm,),
        in_specs=[_rows(tm, D_FF), _rows(tm, D_FF), _full(D_MODEL, D_FF), _full(D_MODEL, D_FF), _rows(tm, D_MODEL),
                  _rows(tm, D_MODEL), _rows(tm, D_MODEL), _full(1, D_MODEL), _full(1, D_MODEL), _full(D_MODEL, D_MODEL)]
        + [_ANY] * n,
        out_specs=[_rows(tm, D_MODEL), _rows(tm, D_MODEL), _rows(tm, 512), _rows(tm, 512), _acc(1, D_MODEL),
                   _acc(1, D_MODEL)] + [_ANY] * n,
        out_shape=[_sds((S, D_MODEL), F32), _sds((S, D_MODEL), BF16), _sds((S, 512), BF16), _sds((S, 512), BF16),
                   _sds((1, D_MODEL), F32), _sds((1, D_MODEL), F32)] + _swap_out_shapes(grads),
        scratch_shapes=_swap_sems(n) if n else [],
        compiler_params=_cp(("arbitrary",)),
    )(dgate, dup, w_gate, w_up, h1, mix, dh2, g_pre, g_post, w_o, *grads)
    return dh1, dmix, dro, dmo, dg2, dg1, got


def _attn_delta(o, do, S):
    tm = min(512, S)

    def body(o_ref, do_ref, dot_ref, d_ref):
        do = do_ref[...].astype(F32)
        prod_t = (o_ref[...].astype(F32) * do).T
        dot_ref[...] = do.T.astype(BF16)
        for h in range(MLA_HEADS):
            d_ref[h // 2, (h % 2):(h % 2) + 1, :] = jnp.sum(prod_t[h * 64:(h + 1) * 64, :], axis=0, keepdims=True)

    return pl.pallas_call(
        body, name="attn_delta", grid=(S // tm,),
        in_specs=[_rows(tm, 512), _rows(tm, 512)],
        out_specs=[pl.BlockSpec((512, tm), lambda i: (0, i)), pl.BlockSpec((MLA_HEADS // 2, 2, tm), lambda i: (0, 0, i))],
        out_shape=[_sds((512, S), BF16), _sds((MLA_HEADS // 2, 2, S), F32)],
        compiler_params=_cp(("parallel",)),
    )(o, do)


def _flash_bwd(qp, kp, kt, v, do, dot, lse, delta, S, sums=()):
    tq = min(512, S)
    nq = S // tq
    RB = ATT_ROWS
    NH = BWD_HEADS
    qb_of, kb_of, T = _tri_pairs(nq, k_major=True)
    n = len(sums)
    steps = (MLA_HEADS // NH) * T

    def body(qb_ref, kb_ref, q_ref, k_ref, kt_ref, v_ref, do_ref, dot_ref, lse_ref, dl_ref, *rest):
        g_ins, (dq_ref, dk_ref, dv_ref), g_outs = rest[:n], rest[n:n + 3], rest[n + 3:2 * n + 3]
        dk_sc, dv_sc, s_sc, dp_sc, p_sc, ds_sc = rest[2 * n + 3:2 * n + 9]
        sems = rest[2 * n + 9:]
        t = pl.program_id(1)
        qb = qb_ref[t]
        kb = kb_ref[t]
        lin = pl.program_id(0) * T + t

        if n:
            @pl.when(lin == 0)
            def _():
                for cp in _scatter_copies(g_ins, g_outs, sems):
                    cp.start()

        @pl.when(t == 0)
        def _():
            dq_ref[...] = jnp.zeros(dq_ref.shape, F32)

        @pl.when(qb == kb)
        def _():
            dk_sc[...] = jnp.zeros(dk_sc.shape, F32)
            dv_sc[...] = jnp.zeros(dv_sc.shape, F32)

        lane = lax.broadcasted_iota(jnp.int32, (tq, 64 * NH), 1)

        def step(masked):
            vv = v_ref[...]
            do_all = do_ref[...]
            mine = [(lane >= a * 64) & (lane < (a + 1) * 64) for a in range(NH)]
            for a in range(NH):
                sl = slice(a * 128, (a + 1) * 128)
                s_sc[a] = _dot_nt(k_ref[:, sl], q_ref[:, sl])
                dp_sc[a] = jnp.dot(jnp.where(mine[a], vv, jnp.zeros_like(vv)), dot_ref[...],
                                   preferred_element_type=F32)
            for a in range(NH):
                sl = slice(a * 128, (a + 1) * 128)
                lse = lse_ref[a:a + 1, :]
                dl = dl_ref[a:a + 1, :]
                for r in range(0, tq, RB):
                    sc = s_sc[a, r:r + RB, :]
                    if masked:
                        sc = jnp.where(_causal_keep(r, RB, tq), sc, NEG)
                    p = jnp.exp(sc - lse)
                    p_sc[a, r:r + RB, :] = p.astype(BF16)
                    ds_sc[a, r:r + RB, :] = (p * (dp_sc[a, r:r + RB, :] - dl)).astype(BF16)
                ds = ds_sc[a]
                dv_sc[...] += jnp.dot(p_sc[a], jnp.where(mine[a], do_all, jnp.zeros_like(do_all)),
                                      preferred_element_type=F32)
                dk_sc[:, sl] += jnp.dot(ds, q_ref[:, sl], preferred_element_type=F32)
                dq_ref[qb, sl, :] += jnp.dot(kt_ref[sl, :], ds, preferred_element_type=F32)

        @pl.when(qb > kb)
        def _():
            step(False)

        @pl.when(qb == kb)
        def _():
            step(True)

        @pl.when(qb == nq - 1)
        def _():
            dk_ref[...] = dk_sc[...].astype(BF16)
            dv_ref[...] = dv_sc[...].astype(BF16)

        if n:
            @pl.when(lin == steps - 1)
            def _():
                for cp in _scatter_copies(g_ins, g_outs, sems):
                    cp.wait()

    grid_spec = pltpu.PrefetchScalarGridSpec(
        num_scalar_prefetch=2, grid=(MLA_HEADS // NH, T),
        in_specs=[pl.BlockSpec((tq, 128 * NH), lambda j, t, qb, kb: (qb[t], j)),
                  pl.BlockSpec((tq, 128 * NH), lambda j, t, qb, kb: (kb[t], j)),
                  pl.BlockSpec((128 * NH, tq), lambda j, t, qb, kb: (j, kb[t])),
                  pl.BlockSpec((tq, 64 * NH), lambda j, t, qb, kb: (kb[t], j)),
                  pl.BlockSpec((tq, 64 * NH), lambda j, t, qb, kb: (qb[t], j)),
                  pl.BlockSpec((64 * NH, tq), lambda j, t, qb, kb: (j, qb[t])),
                  pl.BlockSpec((None, NH, tq), lambda j, t, qb, kb: (j, 0, qb[t])),
                  pl.BlockSpec((None, NH, tq), lambda j, t, qb, kb: (j, 0, qb[t]))] + [_ANY] * n,
        out_specs=[pl.BlockSpec((nq, 128 * NH, tq), lambda j, t, qb, kb: (0, j, 0), pipeline_mode=pl.Buffered(1)),
                   pl.BlockSpec((tq, 128 * NH), lambda j, t, qb, kb: (kb[t], j)),
                   pl.BlockSpec((tq, 64 * NH), lambda j, t, qb, kb: (kb[t], j))] + [_ANY] * n,
        scratch_shapes=[pltpu.VMEM((tq, 128 * NH), F32), pltpu.VMEM((tq, 64 * NH), F32), pltpu.VMEM((NH, tq, tq), F32),
                        pltpu.VMEM((NH, tq, tq), F32), pltpu.VMEM((NH, tq, tq), BF16), pltpu.VMEM((NH, tq, tq), BF16)]
        + (_scatter_sems(n) if n else []),
    )
    dq, dk, dv, *parts = pl.pallas_call(
        body, name="flash_bwd", grid_spec=grid_spec,
        out_shape=[_sds((nq, 1024, tq), F32), _sds((S, 1024), BF16), _sds((S, 512), BF16)] + _scatter_out_shapes(sums),
        compiler_params=_cp(("arbitrary", "arbitrary")),
    )(qb_of, kb_of, qp, kp, kt, v, do, dot, lse.reshape(MLA_HEADS // NH, NH, S), delta.reshape(MLA_HEADS // NH, NH, S),
      *sums)
    return dq, dk, dv, parts


def _mla_up_bwd(dqp, dkp, dv, cq, ckv, gq, gkv, w_uq, w_ukv, tabs, S):
    tm = min(512, S)

    def body(dq_ref, dk_ref, dv_ref, cq_ref, ckv_ref, gq_ref, gkv_ref, wuq_ref, wukv_ref, cm_ref, sa_ref, sb_ref,
             dqh_ref, dkv_ref, dcq_ref, dckv_ref, dkr_ref, dgq_ref, dgkv_ref):
        @pl.when(pl.program_id(0) == 0)
        def _():
            dgq_ref[...] = jnp.zeros(dgq_ref.shape, F32)
            dgkv_ref[...] = jnp.zeros(dgkv_ref.shape, F32)

        cm = cm_ref[...]
        sa = sa_ref[...]
        sb = sb_ref[...]
        lane = lax.broadcasted_iota(jnp.int32, (tm, 128), 1)
        dkr_r = jnp.zeros((tm, 128), F32)
        for h in range(MLA_HEADS):
            sl = slice(h * 128, (h + 1) * 128)
            dqh_ref[:, sl] = (_unrope_mla(dq_ref[sl, :].T, cm, sa, sb) * SCALE_MLA).astype(BF16)
            gk = dk_ref[:, sl]
            dkr_r = dkr_r + gk.astype(F32)
            dkv_ref[:, sl] = gk
        dkr_r = jnp.where((lane >= 64) & (lane < 96), dkr_r, 0.0)
        dkr_ref[...] = _unrope_mla(dkr_r, cm, sa, sb).astype(BF16)
        dkv_ref[:, 1024:1536] = dv_ref[...]
        dcq, ga = _rms_bwd(_dot_nt(dqh_ref[...], wuq_ref[...]), cq_ref[...], gq_ref[...])
        dcq_ref[...] = dcq.astype(BF16)
        dgq_ref[...] += _colsum(ga)
        dckv, gb = _rms_bwd(_dot_nt(dkv_ref[...], wukv_ref[...]), ckv_ref[...], gkv_ref[...])
        dckv_ref[...] = dckv.astype(BF16)
        dgkv_ref[...] += _colsum(gb)

    per_q = dqp.shape[2] // tm
    return pl.pallas_call(
        body, name="mla_up_bwd", grid=(S // tm,),
        in_specs=[pl.BlockSpec((None, 1024, tm), lambda i: (i // per_q, 0, i % per_q)),
                  _rows(tm, 1024), _rows(tm, 512), _rows(tm, Q_LORA), _rows(tm, KV_LORA),
                  _full(1, Q_LORA), _full(1, KV_LORA), _full(Q_LORA, 1024), _full(KV_LORA, 1536)] + [_rows(tm, 128)] * 3,
        out_specs=[_rows(tm, 1024), _rows(tm, 1536), _rows(tm, Q_LORA), _rows(tm, KV_LORA), _rows(tm, 128),
                   _acc(1, Q_LORA), _acc(1, KV_LORA)],
        out_shape=[_sds((S, 1024), BF16), _sds((S, 1536), BF16), _sds((S, Q_LORA), BF16), _sds((S, KV_LORA), BF16),
                   _sds((S, 128), BF16), _sds((1, Q_LORA), F32), _sds((1, KV_LORA), F32)],
        compiler_params=_cp(("arbitrary",)),
    )(dqp, dkp, dv, cq, ckv, gq, gkv, w_uq, w_ukv, *tabs[2:])


def _ret_bwd(rq, rk, rv, rprev, ry, rg, dro, gn_w, tabs, S):
    C = RET_CHUNK
    N = S // C
    G = min(RET_GROUP, N)
    NB = N // G

    def body(lg_ref, q_ref, k_ref, v_ref, rp_ref, ry_ref, rg_ref, dro_ref, w_ref, cr_ref, sr_ref,
             drq_ref, drk_ref, drv_ref, drg_ref, dw_ref, g_sc):
        @pl.when(pl.program_id(1) == 0)
        def _():
            g_sc[...] = jnp.zeros(g_sc.shape, F32)
            dw_ref[...] = jnp.zeros(dw_ref.shape, F32)

        dmat, zeta, xi, g_chunk = _decay_terms(lg_ref)
        w = w_ref[...]
        gacc = g_sc[...]
        dw = jnp.zeros((1, 128), F32)
        for i in reversed(range(G)):
            rows = slice(i * C, (i + 1) * C)
            ry = ry_ref[rows, :]
            mu = jnp.mean(ry, axis=-1, keepdims=True)
            yc = ry - mu
            rstd = lax.rsqrt(jnp.mean(yc * yc, axis=-1, keepdims=True) + EPS)
            yh = yc * rstd
            g = rg_ref[rows, :]
            s = _sig(g)
            dout = dro_ref[rows, :].astype(F32)
            drg_ref[rows, :] = (dout * (yh * w) * (s * (1.0 + g * (1.0 - s)))).astype(BF16)
            dgn = dout * (g * s)
            dw = dw + _colsum(dgn * yh)
            dyh = dgn * w
            dry = rstd * (dyh - jnp.mean(dyh, axis=-1, keepdims=True) - yh * jnp.mean(dyh * yh, axis=-1, keepdims=True))
            do = dry.astype(BF16)

            q = q_ref[rows, :]
            k = k_ref[rows, :]
            v = v_ref[rows, :]
            gfut = gacc.astype(BF16)
            sc = (_dot_nt(q, k) * dmat).astype(BF16)
            dsc = (_dot_nt(do, v) * dmat).astype(BF16)
            dq = jnp.dot(dsc, k, preferred_element_type=F32) + _dot_nt(do, rp_ref[i]) * xi
            dk = _dot_tn(dsc, q) + _dot_nt(v, gfut) * zeta
            dv = _dot_tn(sc, do) + jnp.dot(k, gfut, preferred_element_type=F32) * zeta
            gacc = g_chunk * gacc + _dot_tn(q, xi * dry)
            cr = cr_ref[rows, :]
            sr = sr_ref[rows, :]
            drq_ref[rows, :] = _unrope_ret(dq, cr, sr).astype(BF16)
            drk_ref[rows, :] = _unrope_ret(dk * SCALE_RET, cr, sr).astype(BF16)
            drv_ref[rows, :] = dv.astype(BF16)
        g_sc[...] = gacc
        dw_ref[...] += dw

    blk = pl.BlockSpec((G * C, 128), lambda h, n: (NB - 1 - n, h))
    tab = pl.BlockSpec((G * C, 128), lambda h, n: (NB - 1 - n, 0))
    return pl.pallas_call(
        body, name="ret_bwd", grid=(RET_HEADS, NB),
        in_specs=[pl.BlockSpec((None, 8, 128), lambda h, n: (h, 0, 0)), blk, blk, blk,
                  pl.BlockSpec((G, 128, 128), lambda h, n: (h * NB + NB - 1 - n, 0, 0)), blk, blk, blk,
                  pl.BlockSpec((1, 128), lambda h, n: (0, h)), tab, tab],
        out_specs=[blk, blk, blk, blk, pl.BlockSpec((1, 128), lambda h, n: (0, h))],
        out_shape=[_sds((S, 512), BF16)] * 4 + [_sds((1, 512), F32)],
        scratch_shapes=[pltpu.VMEM((128, 128), F32)],
        compiler_params=_cp(("parallel", "arbitrary")),
    )(_decay_table(), rq, rk, rv, rprev, ry, rg, dro, gn_w, tabs[0], tabs[1])


def _inproj_bwd(drq, drk, drv, drg, dcq, dckv, dkr, w_in, dh1, x, g, S):
    tm = min(512, S)

    def body(drq_ref, drk_ref, drv_ref, drg_ref, dcq_ref, dckv_ref, dkr_ref, w_ref, dh1_ref, x_ref, g_ref,
             gx_ref, dproj_ref, dg_ref):
        @pl.when(pl.program_id(0) == 0)
        def _():
            dg_ref[...] = jnp.zeros(dg_ref.shape, F32)

        dproj_ref[:, 0:512] = drq_ref[...]
        dproj_ref[:, 512:1024] = drk_ref[...]
        dproj_ref[:, 1024:1536] = drv_ref[...]
        dproj_ref[:, 1536:2048] = drg_ref[...]
        dproj_ref[:, 2048:2432] = dcq_ref[...]
        dproj_ref[:, 2432:2688] = dckv_ref[...]
        dproj_ref[:, 2688:2816] = dkr_ref[...]
        dx, ga = _rms_bwd(_dot_nt(dproj_ref[...], w_ref[...]), x_ref[...], g_ref[...])
        gx_ref[...] = dh1_ref[...] + dx
        dg_ref[...] += _colsum(ga)

    return pl.pallas_call(
        body, name="inproj_bwd", grid=(S // tm,),
        in_specs=[_rows(tm, 512)] * 4 + [_rows(tm, Q_LORA), _rows(tm, KV_LORA), _rows(tm, 128),
                                         _full(D_MODEL, IN_COLS_P), _rows(tm, D_MODEL), _rows(tm, D_MODEL),
                                         _full(1, D_MODEL)],
        out_specs=[_rows(tm, D_MODEL), _rows(tm, IN_COLS_P), _acc(1, D_MODEL)],
        out_shape=[_sds((S, D_MODEL), F32), _sds((S, IN_COLS_P), BF16), _sds((1, D_MODEL), F32)],
        compiler_params=_cp(("arbitrary",)),
    )(drq, drk, drv, drg, dcq, dckv, dkr, w_in, dh1, x, g)


def _pad_weights(w):
    w_in = w["w_in"]
    z = lambda r, c: jnp.zeros((r, c), BF16)
    w_in_p = jnp.concatenate([w_in[:, :2688], z(1024, 64), w_in[:, 2688:2720], z(1024, 32)], axis=1)
    w_uq_p = jnp.pad(w["w_uq"].reshape(Q_LORA, MLA_HEADS, 96), ((0, 0), (0, 0), (0, 32))).reshape(Q_LORA, 1024)
    ukv = w["w_ukv"].reshape(KV_LORA, MLA_HEADS, 128)
    k_part = jnp.pad(ukv[:, :, :64], ((0, 0), (0, 0), (0, 64))).reshape(KV_LORA, 1024)
    w_ukv_p = jnp.concatenate([k_part, ukv[:, :, 64:].reshape(KV_LORA, 512)], axis=1)
    return w_in_p, w_uq_p, w_ukv_p


BIG_SPEC = {n: (r, c, ax) for n, r, c, ax in BIG}
GATHER_FIRST = ("w_in", "w_uq", "w_ukv")
GATHER_LATE = tuple(n for n, _, _, _ in BIG if n not in GATHER_FIRST)
REDUCE_EARLY = ("w_ple_gate", "w_ple_proj", "w_down", "w_gate", "w_up")
REDUCE_LAST = tuple(n for n, _, _, _ in BIG if n not in REDUCE_EARLY)


def _local_step(x, p, pos_f, tgt, w, sm, late_shards=None, c_idx=None):
    S = x.shape[0]
    spread = late_shards is not None
    w = dict(w)
    tabs, first = _rope_tables(pos_f, S, [late_shards[n] for n in GATHER_FIRST] if spread else ())
    for i, n in enumerate(GATHER_FIRST if spread else ()):
        w[n] = _from_chips(first[i], BIG_SPEC[n][2])
    w_in_p, w_uq_p, w_ukv_p = _pad_weights(w)

    xn, rq, rk, rv, rg, cq, ckv, kr = _inproj(x, sm["pre_mix_norm"], w_in_p, tabs, S)
    cqn, ckvn, qp, kp, v, kt, vt = _mla_up(cq, ckv, kr, sm["mla_q_norm"], sm["mla_kv_norm"], w_uq_p, w_ukv_p, tabs, S)
    mo, lse, gathered = _flash_fwd(qp, kp, vt, S, [late_shards[n] for n in GATHER_LATE] if spread else ())
    for i, n in enumerate(GATHER_LATE if spread else ()):
        w[n] = _from_chips(gathered[i], BIG_SPEC[n][2])
    ry, ro, rprev = _ret_fwd(rq, rk, rv, rg, sm["ret_gn_w"], S)
    mix, h1, hn = _outproj(ro, mo, x, w["w_o"], sm["post_mix_norm"], sm["pre_ffn_norm"], S)
    gate, up, act = _ffn_up(hn, w["w_gate"], w["w_up"], S)
    ff, h2 = _ffn_down(act, w["w_down"], h1, sm["post_ffn_norm"], S)
    dz, dpe, dh2, h2b, loss_vec, d_ple_norm, d_b = _ple_loss(
        p, h2, tgt, w["w_ple_proj"], w["w_ple_gate"], sm["b_ple_gate"], sm["ple_norm"], S)

    gw = {}
    gs = {"ple_norm": d_ple_norm, "b_ple_gate": d_b}
    gw["w_ple_gate"] = _wgrad(h2b, dz, "wgrad_ple_gate", S)
    gw["w_ple_proj"] = _wgrad(p, dpe, "wgrad_ple_proj", S)
    dff, dgate, dup, gs["post_ffn_norm"] = _ffn_down_bwd(dh2, ff, sm["post_ffn_norm"], w["w_down"], gate, up, S)
    gw["w_down"] = _wgrad(act, dff, "wgrad_down", S)
    gw["w_gate"] = _wgrad(hn, dgate, "wgrad_gate", S)
    gw["w_up"] = _wgrad(hn, dup, "wgrad_up", S)
    g4 = [_by_chip(gw.pop(n), *BIG_SPEC[n]) for n in REDUCE_EARLY] if spread else []
    dh1, dmix, dro, dmo, gs["pre_ffn_norm"], gs["post_mix_norm"], got = _ffn_up_bwd(
        dgate, dup, w["w_gate"], w["w_up"], h1, mix, dh2, sm["pre_ffn_norm"], sm["post_mix_norm"], w["w_o"], S, g4)
    sums = [_add_half_rows(g4[i], got[i], c_idx, "rs_add_halves_" + n) for i, n in enumerate(REDUCE_EARLY)] if spread else []
    gw["w_o"] = jnp.concatenate([_wgrad(ro, dmix, "wgrad_o_ret", S), _wgrad(mo, dmix, "wgrad_o_mla", S)], axis=0)

    dmo_t, delta = _attn_delta(mo, dmo, S)
    dqp, dkp, dv, parts = _flash_bwd(qp, kp, kt, v, dmo, dmo_t, lse, delta, S, sums)
    dqh, dkv, dcq, dckv, dkr, gs["mla_q_norm"], gs["mla_kv_norm"] = _mla_up_bwd(
        dqp, dkp, dv, cq, ckv, sm["mla_q_norm"], sm["mla_kv_norm"], w_uq_p, w_ukv_p, tabs, S)
    g_uq_p = _wgrad(cqn, dqh, "wgrad_uq", S)
    g_ukv_p = _wgrad(ckvn, dkv, "wgrad_ukv", S)
    gw["w_uq"] = g_uq_p.reshape(Q_LORA, MLA_HEADS, 128)[:, :, :96].reshape(Q_LORA, 768)
    gw["w_ukv"] = jnp.concatenate(
        [g_ukv_p[:, :1024].reshape(KV_LORA, MLA_HEADS, 128)[:, :, :64], g_ukv_p[:, 1024:].reshape(KV_LORA, MLA_HEADS, 64)],
        axis=2).reshape(KV_LORA, 1024)

    drq, drk, drv, drg, gs["ret_gn_w"] = _ret_bwd(rq, rk, rv, rprev, ry, rg, dro, sm["ret_gn_w"], tabs, S)
    grad_x, dproj, gs["pre_mix_norm"] = _inproj_bwd(drq, drk, drv, drg, dcq, dckv, dkr, w_in_p, dh1, x,
                                                    sm["pre_mix_norm"], S)
    g_in_p = _wgrad(xn, dproj, "wgrad_in", S)
    gw["w_in"] = jnp.concatenate([g_in_p[:, :2688], g_in_p[:, 2752:2784]], axis=1)
    return loss_vec, grad_x, gw, gs, ((sums, parts) if spread else None)


def _my_place():
    x = lax.axis_index("x")
    y = lax.axis_index("y")
    c = lax.axis_index("c")
    return x, y, c


def _other_chips(x, y):
    return [(1 - x, y), (x, 1 - y), (1 - x, 1 - y)]


_ANY = pl.BlockSpec(memory_space=pl.ANY)


def _allreduce_small(vec):
    def body(v_ref, out_ref, slots, send, recv, lsem):
        x, y, c = _my_place()
        me = 4 * x + 2 * y + c
        mine = pltpu.make_async_copy(v_ref, slots.at[me], lsem)
        mine.start()
        cps = []
        for r in range(1, N_DEV):
            px = x ^ (r >> 2)
            py = y ^ ((r >> 1) & 1)
            pc = c ^ (r & 1)
            cps.append(pltpu.make_async_remote_copy(
                src_ref=v_ref, dst_ref=slots.at[me], send_sem=send.at[r - 1], recv_sem=recv.at[r - 1],
                device_id=(px, py, pc), device_id_type=MESH))
        for cp in cps:
            cp.start()
        for cp in cps:
            cp.wait()
        mine.wait()
        acc = slots[0]
        for d in range(1, N_DEV):
            acc = acc + slots[d]
        out_ref[...] = acc
        loss = jnp.sum(acc[9:10, :], axis=1, keepdims=True) * (0.5 / D_MODEL)
        out_ref[9:10, :] = jnp.broadcast_to(loss, (1, PACK_COLS))

    vm = pl.BlockSpec(memory_space=pltpu.VMEM)
    return pl.pallas_call(
        body, name="allreduce_small",
        in_specs=[vm], out_specs=vm, out_shape=_sds((SMALL_ROWS, PACK_COLS), F32),
        scratch_shapes=[pltpu.VMEM((N_DEV, SMALL_ROWS, PACK_COLS), F32), pltpu.SemaphoreType.DMA((N_DEV - 1,)),
                        pltpu.SemaphoreType.DMA((N_DEV - 1,)), pltpu.SemaphoreType.DMA],
    )(vec)


N_BIG = len(BIG)


def _half(c, rows, align):
    h = rows // 2
    return pl.ds(pl.multiple_of(c * h, align), h)


def _gather_out_shapes(shards):
    return [_sds((N_CHIPS,) + tuple(s.shape), BF16) for s in shards]


def _gather_sems(n):
    return [pltpu.SemaphoreType.DMA((n, 3))] * 4 + [pltpu.SemaphoreType.DMA((n,))] * 2


def _gather_phase(phase, ins, outs, sems):
    send1, recv1, send2, recv2, send3, recv3 = sems
    x, y, c = _my_place()
    me = 2 * x + y
    chips = _other_chips(x, y)
    sib = (x, y, 1 - c)
    for t in range(len(ins)):
        rows = ins[t].shape[0]
        half = _half(c, rows, 16)
        other = _half(1 - c, rows, 16)
        def own():
            return pltpu.make_async_remote_copy(
                src_ref=ins[t], dst_ref=outs[t].at[me], send_sem=send3.at[t], recv_sem=recv3.at[t],
                device_id=sib, device_id_type=MESH)

        if phase == 0:
            own().start()
        if phase == 2:
            own().wait()
        for k, (cx, cy) in enumerate(chips):
            src = 2 * cx + cy

            def over_ici(slab):
                return pltpu.make_async_remote_copy(
                    src_ref=ins[t].at[half], dst_ref=outs[t].at[slab, half], send_sem=send1.at[t, k],
                    recv_sem=recv1.at[t, k], device_id=(cx, cy, c), device_id_type=MESH)

            def over_d2d(rows):
                return pltpu.make_async_remote_copy(
                    src_ref=outs[t].at[src, rows], dst_ref=outs[t].at[src, rows], send_sem=send2.at[t, k],
                    recv_sem=recv2.at[t, k], device_id=sib, device_id_type=MESH)

            if phase == 0:
                over_ici(me).start()
            if phase == 1:
                over_ici(src).wait_recv()
                over_d2d(half).start()
            if phase == 2:
                over_d2d(other).wait_recv()
                over_ici(me).wait_send()
                over_d2d(half).wait_send()


def _swap_copies(ins, outs, sems):
    send, recv = sems
    x, y, c = _my_place()
    return [pltpu.make_async_remote_copy(
        src_ref=ins[t].at[:, _half(1 - c, ins[t].shape[1], 8)], dst_ref=outs[t], send_sem=send.at[t],
        recv_sem=recv.at[t], device_id=(x, y, 1 - c), device_id_type=MESH) for t in range(len(ins))]


def _swap_out_shapes(gs):
    return [_sds((N_CHIPS, g.shape[1] // 2, g.shape[2]), F32) for g in gs]


def _swap_sems(n):
    return [pltpu.SemaphoreType.DMA((n,)), pltpu.SemaphoreType.DMA((n,))]


def _swap_half_rows(gs):
    n = len(gs)

    def body(*refs):
        cps = _swap_copies(refs[:n], refs[n:2 * n], refs[2 * n:])
        for cp in cps:
            cp.start()
        for cp in cps:
            cp.wait()

    return pl.pallas_call(
        body, name="rs_swap_halves",
        in_specs=[_ANY] * n, out_specs=[_ANY] * n, out_shape=_swap_out_shapes(gs), scratch_shapes=_swap_sems(n),
    )(*gs)


def _add_half_rows(g, got, c_idx, name):
    _, rows, cols = g.shape
    h = rows // 2

    def body(c_ref, a_ref, b_ref, o_ref):
        o_ref[...] = (a_ref[...] + b_ref[...]).astype(BF16)

    grid_spec = pltpu.PrefetchScalarGridSpec(
        num_scalar_prefetch=1, grid=(N_CHIPS,),
        in_specs=[pl.BlockSpec((None, h, cols), lambda j, c: (j, c[0], 0)),
                  pl.BlockSpec((None, h, cols), lambda j, c: (j, 0, 0))],
        out_specs=pl.BlockSpec((None, h, cols), lambda j, c: (j, 0, 0)),
    )
    return pl.pallas_call(
        body, name=name, grid_spec=grid_spec, out_shape=_sds((N_CHIPS, h, cols), BF16),
        compiler_params=_cp(("parallel",)),
    )(c_idx, g, got)


def _scatter_to_chips(ts):
    n = len(ts)

    def body(*refs):
        cps = _scatter_copies(refs[:n], refs[n:2 * n], refs[2 * n:])
        for cp in cps:
            cp.start()
        for cp in cps:
            cp.wait()

    return pl.pallas_call(
        body, name="rs_scatter_chips",
        in_specs=[_ANY] * n, out_specs=[_ANY] * n, out_shape=_scatter_out_shapes(ts), scratch_shapes=_scatter_sems(n),
    )(*ts)


def _scatter_copies(ins, outs, sems):
    send, recv = sems
    x, y, c = _my_place()
    return [pltpu.make_async_remote_copy(
        src_ref=ins[t].at[2 * cx + cy], dst_ref=outs[t].at[k], send_sem=send.at[t, k], recv_sem=recv.at[t, k],
        device_id=(cx, cy, c), device_id_type=MESH)
        for t in range(len(ins)) for k, (cx, cy) in enumerate(_other_chips(x, y))]


def _scatter_out_shapes(ts):
    return [_sds((3,) + tuple(t.shape[1:]), BF16) for t in ts]


def _scatter_sems(n):
    return [pltpu.SemaphoreType.DMA((n, 3)), pltpu.SemaphoreType.DMA((n, 3))]


def _add_four(mine, parts, place, name):
    _, h, cols = parts.shape

    def body(pl_ref, m_ref, p_ref, o_ref):
        o_ref[...] = ((m_ref[...].astype(F32) + p_ref[0].astype(F32)) + p_ref[1].astype(F32)) + p_ref[2].astype(F32)

    grid_spec = pltpu.PrefetchScalarGridSpec(
        num_scalar_prefetch=1, grid=(1,),
        in_specs=[pl.BlockSpec((None, h, cols), lambda i, pc: (pc[0], 0, 0)),
                  pl.BlockSpec((3, h, cols), lambda i, pc: (0, 0, 0))],
        out_specs=pl.BlockSpec((h, cols), lambda i, pc: (pc[1], 0)),
    )
    return pl.pallas_call(
        body, name=name, grid_spec=grid_spec, out_shape=_sds((2 * h, cols), F32),
        compiler_params=_cp(("arbitrary",)),
    )(place, mine, parts)


def _join_half_rows(rs):
    n = len(rs)

    def body(*refs):
        ins, outs = refs[:n], refs[n:2 * n]
        send, recv = refs[2 * n:]
        x, y, c = _my_place()
        cps = []
        for t in range(n):
            half = _half(c, outs[t].shape[0], 8)
            rc = pltpu.make_async_remote_copy(
                src_ref=ins[t].at[half], dst_ref=outs[t].at[half], send_sem=send.at[t], recv_sem=recv.at[t],
                device_id=(x, y, 1 - c), device_id_type=MESH)
            rc.start()
            cps.append(rc)
        for cp in cps:
            cp.wait()

    return pl.pallas_call(
        body, name="rs_join_halves",
        in_specs=[_ANY] * n, out_specs=[_ANY] * n,
        out_shape=[_sds(r.shape, F32) for r in rs],
        input_output_aliases={i: i for i in range(n)},
        scratch_shapes=[pltpu.SemaphoreType.DMA((n,))] * 2,
    )(*rs)


def _by_chip(full, rows, cols, axis):
    if axis == 0:
        return full.reshape(N_CHIPS, rows // N_CHIPS, cols)
    return full.reshape(rows, N_CHIPS, cols // N_CHIPS).transpose(1, 0, 2)


def _from_chips(parts, axis):
    _, r, c = parts.shape
    if axis == 0:
        return parts.reshape(N_CHIPS * r, c)
    return parts.transpose(1, 0, 2).reshape(r, N_CHIPS * c)


def _adamw(wt, g, m, v, name):
    _, R, C = wt.shape
    tr = R
    for cand in (256, 128, 64, 32, 16, 8):
        if R % cand == 0:
            tr = cand
            break

    def body(w_ref, g_ref, m_ref, v_ref, d_ref, nm_ref, nv_ref):
        gg = g_ref[...]
        m_new = ADAM_B1 * m_ref[...] + (1.0 - ADAM_B1) * gg
        v_new = ADAM_B2 * v_ref[...] + (1.0 - ADAM_B2) * (gg * gg)
        m_hat = m_new / (1.0 - ADAM_B1 ** ADAM_STEP)
        v_hat = v_new / (1.0 - ADAM_B2 ** ADAM_STEP)
        d_ref[...] = -ADAM_LR * (m_hat / (jnp.sqrt(v_hat) + ADAM_EPS) + ADAM_WD * w_ref[...])
        nm_ref[...] = m_new
        nv_ref[...] = v_new

    spec = pl.BlockSpec((None, tr, C), lambda i: (0, i, 0))
    return pl.pallas_call(
        body, name=name, grid=(R // tr,), in_specs=[spec, pl.BlockSpec((tr, C), lambda i: (i, 0)), spec, spec],
        out_specs=[spec] * 3, out_shape=[_sds((1, R, C), F32)] * 3,
        compiler_params=_cp(("parallel",)),
    )(wt, g, m, v)


def _pack_small(vals, loss_vec=None):
    rows = [jnp.pad(vals[n].reshape(-1), (0, PACK_COLS - sz)) for n, sz in SMALL]
    rows.append(loss_vec.reshape(-1) if loss_vec is not None else jnp.zeros((PACK_COLS,), F32))
    rows += [jnp.zeros((PACK_COLS,), F32)] * (SMALL_ROWS - len(rows))
    return jnp.stack(rows)


def kernel(x, p, positions, pre_mix_norm, w_in, ret_gn_w, mla_q_norm, w_uq, mla_kv_norm, w_ukv, w_o, post_mix_norm, pre_ffn_norm, w_gate, w_up, w_down, post_ffn_norm, w_ple_proj, ple_norm, w_ple_gate, b_ple_gate, loss_target, m_pre_mix_norm, m_w_in, m_ret_gn_w, m_mla_q_norm, m_w_uq, m_mla_kv_norm, m_w_ukv, m_w_o, m_post_mix_norm, m_pre_ffn_norm, m_w_gate, m_w_up, m_w_down, m_post_ffn_norm, m_w_ple_proj, m_ple_norm, m_w_ple_gate, m_b_ple_gate, v_pre_mix_norm, v_w_in, v_ret_gn_w, v_mla_q_norm, v_w_uq, v_mla_kv_norm, v_w_ukv, v_w_o, v_post_mix_norm, v_pre_ffn_norm, v_w_gate, v_w_up, v_w_down, v_post_ffn_norm, v_w_ple_proj, v_ple_norm, v_w_ple_gate, v_b_ple_gate):
    wts = dict(pre_mix_norm=pre_mix_norm, w_in=w_in, ret_gn_w=ret_gn_w, mla_q_norm=mla_q_norm, w_uq=w_uq,
               mla_kv_norm=mla_kv_norm, w_ukv=w_ukv, w_o=w_o, post_mix_norm=post_mix_norm, pre_ffn_norm=pre_ffn_norm,
               w_gate=w_gate, w_up=w_up, w_down=w_down, post_ffn_norm=post_ffn_norm, w_ple_proj=w_ple_proj,
               ple_norm=ple_norm, w_ple_gate=w_ple_gate, b_ple_gate=b_ple_gate)
    mom = dict(pre_mix_norm=m_pre_mix_norm, w_in=m_w_in, ret_gn_w=m_ret_gn_w, mla_q_norm=m_mla_q_norm, w_uq=m_w_uq,
               mla_kv_norm=m_mla_kv_norm, w_ukv=m_w_ukv, w_o=m_w_o, post_mix_norm=m_post_mix_norm,
               pre_ffn_norm=m_pre_ffn_norm, w_gate=m_w_gate, w_up=m_w_up, w_down=m_w_down, post_ffn_norm=m_post_ffn_norm,
               w_ple_proj=m_w_ple_proj, ple_norm=m_ple_norm, w_ple_gate=m_w_ple_gate, b_ple_gate=m_b_ple_gate)
    var = dict(pre_mix_norm=v_pre_mix_norm, w_in=v_w_in, ret_gn_w=v_ret_gn_w, mla_q_norm=v_mla_q_norm, w_uq=v_w_uq,
               mla_kv_norm=v_mla_kv_norm, w_ukv=v_w_ukv, w_o=v_w_o, post_mix_norm=v_post_mix_norm,
               pre_ffn_norm=v_pre_ffn_norm, w_gate=v_w_gate, w_up=v_w_up, w_down=v_w_down, post_ffn_norm=v_post_ffn_norm,
               w_ple_proj=v_w_ple_proj, ple_norm=v_ple_norm, w_ple_gate=v_w_ple_gate, b_ple_gate=v_b_ple_gate)

    S = x.shape[1]
    shard2d = {n: wts[n][0] for n, _, _, _ in BIG}
    small2d = {n: wts[n] for n, _ in SMALL}

    shard_bf = {n: shard2d[n].astype(BF16) for n in shard2d}
    pos_f = positions.astype(F32).reshape(S, 1)
    c_idx = lax.axis_index("c").astype(jnp.int32).reshape(1)
    loss_vec, grad_x, gw, gs, (sums_early, parts_early) = _local_step(
        x[0], p[0, 0], pos_f, loss_target[0], {}, small2d, shard_bf, c_idx)

    g4 = [_by_chip(gw[n], *BIG_SPEC[n]) for n in REDUCE_LAST]
    got = _swap_half_rows(g4)
    sums_last = [_add_half_rows(g4[i], got[i], c_idx, "rs_add_halves_" + n) for i, n in enumerate(REDUCE_LAST)]
    parts_last = _scatter_to_chips(sums_last)
    place = jnp.stack([2 * lax.axis_index("x") + lax.axis_index("y"), lax.axis_index("c")]).astype(jnp.int32)
    names = REDUCE_EARLY + REDUCE_LAST
    reduced = _join_half_rows(
        [_add_four(sm_, pt_, place, "rs_add_chips_" + n)
         for n, sm_, pt_ in zip(names, sums_early + sums_last, list(parts_early) + list(parts_last))])
    g_shard = dict(zip(names, reduced))

    small_sum = _allreduce_small(_pack_small(gs, loss_vec))
    loss = small_sum[9, 0]
    g_small = {n: small_sum[i:i + 1, :sz] for i, (n, sz) in enumerate(SMALL)}

    grads, delta, new_m, new_v = {}, {}, {}, {}
    for n, _, _, _ in BIG:
        delta[n], new_m[n], new_v[n] = _adamw(wts[n], g_shard[n], mom[n], var[n], "adamw_" + n)
        grads[n] = g_shard[n][None]
    d, nm, nv = _adamw(_pack_small(small2d)[None], small_sum, _pack_small(mom)[None], _pack_small(var)[None],
                       "adamw_small")
    for i, (n, sz) in enumerate(SMALL):
        grads[n] = g_small[n]
        delta[n], new_m[n], new_v[n] = d[0, i:i + 1, :sz], nm[0, i:i + 1, :sz], nv[0, i:i + 1, :sz]

    return (loss, grad_x[None], *[grads[n] for n in ALL_W], *[delta[n] for n in ALL_W],
            *[new_m[n] for n in ALL_W], *[new_v[n] for n in ALL_W])
```

```python
import functools
import math

import jax
import jax.numpy as jnp
import numpy as np
from jax import lax
from jax.experimental import pallas as pl
from jax.experimental.pallas import tpu as pltpu

F32 = jnp.float32
BF16 = jnp.bfloat16
MESH = pl.DeviceIdType.MESH

D_MODEL = 1024
D_FF = 2816
PLE_DIM = 256
RET_HEADS = 4
RET_DIM = 128
RET_WIDTH = 512
RET_CHUNK = 256
RET_GROUP = 4
MLA_HEADS = 8
MLA_NOPE = 64
MLA_ROPE = 32
MLA_V = 64
Q_LORA = 384
KV_LORA = 256
IN_COLS = 2720
IN_COLS_P = 2816
ROPE_BASE = 10000.0
EPS = 1e-6
SCALE_MLA = 1.0 / math.sqrt(MLA_NOPE + MLA_ROPE)
SCALE_RET = RET_DIM ** -0.5
NEG = -1e30

ADAM_LR = 0.001
ADAM_B1 = 0.9
ADAM_B2 = 0.999
ADAM_EPS = 1e-08
ADAM_WD = 0.01
ADAM_STEP = 10

N_CHIPS = 4
N_DEV = 8
VMEM_MB = 56

BIG = (
    ("w_in", 1024, 2720, 1),
    ("w_uq", 384, 768, 1),
    ("w_ukv", 256, 1024, 1),
    ("w_o", 1024, 1024, 0),
    ("w_gate", 1024, 2816, 1),
    ("w_up", 1024, 2816, 1),
    ("w_down", 2816, 1024, 0),
    ("w_ple_proj", 256, 1024, 1),
    ("w_ple_gate", 1024, 1024, 0),
)
SMALL = (
    ("pre_mix_norm", 1024),
    ("ret_gn_w", 512),
    ("mla_q_norm", 384),
    ("mla_kv_norm", 256),
    ("post_mix_norm", 1024),
    ("pre_ffn_norm", 1024),
    ("post_ffn_norm", 1024),
    ("ple_norm", 1024),
    ("b_ple_gate", 1024),
)
ALL_W = ("pre_mix_norm", "w_in", "ret_gn_w", "mla_q_norm", "w_uq", "mla_kv_norm", "w_ukv", "w_o", "post_mix_norm",
         "pre_ffn_norm", "w_gate", "w_up", "w_down", "post_ffn_norm", "w_ple_proj", "ple_norm", "w_ple_gate", "b_ple_gate")
PACK_COLS = 1024
SMALL_ROWS = 16


def _cp(sem=None, mb=VMEM_MB, **kw):
    return pltpu.CompilerParams(dimension_semantics=sem, vmem_limit_bytes=mb * 1024 * 1024, **kw)


def _bf(x):
    return x.astype(BF16)


def _dot(a, b):
    return jnp.dot(_bf(a), _bf(b), preferred_element_type=F32)


def _dot_nt(a, b):
    return lax.dot_general(_bf(a), _bf(b), (((1,), (1,)), ((), ())), preferred_element_type=F32)


def _dot_tn(a, b):
    return lax.dot_general(_bf(a), _bf(b), (((0,), (0,)), ((), ())), preferred_element_type=F32)


def _sig(x):
    return 1.0 / (1.0 + jnp.exp(-x))


def _rms(x, g):
    r = lax.rsqrt(jnp.mean(x * x, axis=-1, keepdims=True) + EPS)
    return x * r * g


def _rms_bwd(dy, x, g):
    r = lax.rsqrt(jnp.mean(x * x, axis=-1, keepdims=True) + EPS)
    xh = x * r
    dxh = dy * g
    dx = r * (dxh - xh * jnp.mean(dxh * xh, axis=-1, keepdims=True))
    return dx, dy * xh


def _colsum(x):
    return jnp.sum(x, axis=0, keepdims=True)


def _rope_ret(x, cr, sr):
    return x * cr + pltpu.roll(x, 64, 1) * sr


def _unrope_ret(dy, cr, sr):
    return dy * cr + pltpu.roll(dy * sr, 64, 1)


def _rope_mla(x, cm, sa, sb):
    return x * cm + pltpu.roll(x, 112, 1) * sa + pltpu.roll(x, 16, 1) * sb


def _unrope_mla(dy, cm, sa, sb):
    return dy * cm + pltpu.roll(dy * sa, 16, 1) + pltpu.roll(dy * sb, 112, 1)


def _rows(tm, w, col=0):
    return pl.BlockSpec((tm, w), lambda i: (i, col))


def _full(*shape):
    return pl.BlockSpec(shape, lambda i: (0,) * len(shape), pipeline_mode=pl.Buffered(1))


def _acc(*shape):
    return pl.BlockSpec(shape, lambda i: (0,) * len(shape))


def _sds(shape, dtype):
    return jax.ShapeDtypeStruct(shape, dtype)


def _rope_tables(pos_f, S, shards=()):
    tm = min(512, S)
    n = len(shards)
    steps = S // tm
    inv_r = (1.0 / (np.float32(ROPE_BASE) ** (np.arange(64, dtype=np.float32) / np.float32(64)))).astype(np.float32)
    inv_m16 = (1.0 / (np.float32(ROPE_BASE) ** (np.arange(16, dtype=np.float32) / np.float32(16)))).astype(np.float32)
    inv_r = np.concatenate([inv_r, inv_r])[None, :]
    inv_m = np.zeros((1, 128), np.float32)
    inv_m[0, 64:80] = inv_m16
    inv_m[0, 80:96] = inv_m16

    def body(pos_ref, invr_ref, invm_ref, *rest):
        w_ins, (cr_ref, sr_ref, cm_ref, sa_ref, sb_ref) = rest[:n], rest[n:n + 5]
        w_outs, sems = rest[n + 5:2 * n + 5], rest[2 * n + 5:]
        i = pl.program_id(0)
        if n:
            @pl.when(i == 0)
            def _():
                _gather_phase(0, w_ins, w_outs, sems)

            @pl.when(i == steps // 2)
            def _():
                _gather_phase(1, w_ins, w_outs, sems)

        pos = pos_ref[...]
        lane = lax.broadcasted_iota(jnp.int32, (tm, 128), 1)
        ar = pos * invr_ref[...]
        s = jnp.sin(ar)
        cr_ref[...] = jnp.cos(ar)
        sr_ref[...] = jnp.where(lane < 64, -s, s)
        am = pos * invm_ref[...]
        c2 = jnp.cos(am)
        s2 = jnp.sin(am)
        cm_ref[...] = jnp.where(lane < 64, 1.0, jnp.where(lane < 96, c2, 0.0))
        sa_ref[...] = jnp.where((lane >= 64) & (lane < 80), -s2, 0.0)
        sb_ref[...] = jnp.where((lane >= 80) & (lane < 96), s2, 0.0)

        if n:
            @pl.when(i == steps - 1)
            def _():
                _gather_phase(2, w_ins, w_outs, sems)

    outs = pl.pallas_call(
        body, name="rope_tables", grid=(steps,),
        in_specs=[_rows(tm, 1), _full(1, 128), _full(1, 128)] + [_ANY] * n,
        out_specs=[_rows(tm, 128)] * 5 + [_ANY] * n,
        out_shape=[_sds((S, 128), F32)] * 5 + _gather_out_shapes(shards),
        scratch_shapes=_gather_sems(n) if n else [],
        compiler_params=_cp(("arbitrary",)),
    )(pos_f, jnp.asarray(inv_r), jnp.asarray(inv_m), *shards)
    return outs[:5], outs[5:]


def _inproj(x, g, w_in, tabs, S):
    tm = min(512, S)

    def body(x_ref, g_ref, w_ref, cr_ref, sr_ref, cm_ref, sa_ref, sb_ref,
             xn_ref, rq_ref, rk_ref, rv_ref, rg_ref, cq_ref, ckv_ref, kr_ref):
        xb = _rms(x_ref[...], g_ref[...]).astype(BF16)
        xn_ref[...] = xb
        cr = cr_ref[...]
        sr = sr_ref[...]
        q = jnp.dot(xb, w_ref[:, 0:512], preferred_element_type=F32)
        k = jnp.dot(xb, w_ref[:, 512:1024], preferred_element_type=F32)
        for h in range(RET_HEADS):
            sl = slice(h * 128, (h + 1) * 128)
            rq_ref[:, sl] = _rope_ret(q[:, sl], cr, sr).astype(BF16)
            rk_ref[:, sl] = (_rope_ret(k[:, sl], cr, sr) * SCALE_RET).astype(BF16)
        rv_ref[...] = jnp.dot(xb, w_ref[:, 1024:1536], preferred_element_type=F32).astype(BF16)
        rg_ref[...] = jnp.dot(xb, w_ref[:, 1536:2048], preferred_element_type=F32)
        cq_ref[...] = jnp.dot(xb, w_ref[:, 2048:2432], preferred_element_type=F32)
        ckv_ref[...] = jnp.dot(xb, w_ref[:, 2432:2688], preferred_element_type=F32)
        kr = jnp.dot(xb, w_ref[:, 2688:2816], preferred_element_type=F32)
        kr_ref[...] = _rope_mla(kr, cm_ref[...], sa_ref[...], sb_ref[...])

    return pl.pallas_call(
        body, name="inproj", grid=(S // tm,),
        in_specs=[_rows(tm, D_MODEL), _full(1, D_MODEL), _full(D_MODEL, IN_COLS_P)] + [_rows(tm, 128)] * 5,
        out_specs=[_rows(tm, D_MODEL)] + [_rows(tm, 512)] * 4 + [_rows(tm, Q_LORA), _rows(tm, KV_LORA), _rows(tm, 128)],
        out_shape=[_sds((S, D_MODEL), BF16)] + [_sds((S, 512), BF16)] * 3
        + [_sds((S, 512), F32), _sds((S, Q_LORA), F32), _sds((S, KV_LORA), F32), _sds((S, 128), F32)],
        compiler_params=_cp(("parallel",)),
    )(x, g, w_in, *tabs)


def _mla_up(cq, ckv, kr, gq, gkv, w_uq, w_ukv, tabs, S):
    tm = min(512, S)

    def body(cq_ref, ckv_ref, kr_ref, gq_ref, gkv_ref, wuq_ref, wukv_ref, cm_ref, sa_ref, sb_ref,
             cqn_ref, ckvn_ref, qp_ref, kp_ref, v_ref, kt_ref, vt_ref):
        cm = cm_ref[...]
        sa = sa_ref[...]
        sb = sb_ref[...]
        cqn = _rms(cq_ref[...], gq_ref[...]).astype(BF16)
        cqn_ref[...] = cqn
        ckvn = _rms(ckv_ref[...], gkv_ref[...]).astype(BF16)
        ckvn_ref[...] = ckvn
        qh = jnp.dot(cqn, wuq_ref[...], preferred_element_type=F32)
        kv = jnp.dot(ckvn, wukv_ref[...], preferred_element_type=F32)
        kr_blk = kr_ref[...]
        for h in range(MLA_HEADS):
            sl = slice(h * 128, (h + 1) * 128)
            qp_ref[:, sl] = (_rope_mla(qh[:, sl], cm, sa, sb) * SCALE_MLA).astype(BF16)
            kh = kv[:, sl] + kr_blk
            kp_ref[:, sl] = kh.astype(BF16)
            kt_ref[sl, :] = kh.T.astype(BF16)
        for h in range(MLA_HEADS // 2):
            vh = kv[:, 1024 + h * 128:1024 + (h + 1) * 128]
            v_ref[:, h * 128:(h + 1) * 128] = vh.astype(BF16)
            vt_ref[h * 128:(h + 1) * 128, :] = vh.T.astype(BF16)

    cols = lambda r: pl.BlockSpec((r, tm), lambda i: (0, i))
    return pl.pallas_call(
        body, name="mla_up", grid=(S // tm,),
        in_specs=[_rows(tm, Q_LORA), _rows(tm, KV_LORA), _rows(tm, 128), _full(1, Q_LORA), _full(1, KV_LORA),
                  _full(Q_LORA, 1024), _full(KV_LORA, 1536)] + [_rows(tm, 128)] * 3,
        out_specs=[_rows(tm, Q_LORA), _rows(tm, KV_LORA), _rows(tm, 1024), _rows(tm, 1024), _rows(tm, 512),
                   cols(1024), cols(512)],
        out_shape=[_sds((S, Q_LORA), BF16), _sds((S, KV_LORA), BF16), _sds((S, 1024), BF16), _sds((S, 1024), BF16),
                   _sds((S, 512), BF16), _sds((1024, S), BF16), _sds((512, S), BF16)],
        compiler_params=_cp(("parallel",)),
    )(cq, ckv, kr, gq, gkv, w_uq, w_ukv, *tabs[2:])


def _tri_pairs(nq, k_major):
    if k_major:
        pairs = [(qb, kb) for kb in range(nq) for qb in range(kb, nq)]
    else:
        pairs = [(qb, kb) for qb in range(nq) for kb in range(qb + 1)]
    qb_of = np.array([p[0] for p in pairs], np.int32)
    kb_of = np.array([p[1] for p in pairs], np.int32)
    return jnp.asarray(qb_of), jnp.asarray(kb_of), len(pairs)


ATT_ROWS = 32
FWD_HEADS = 8
BWD_HEADS = 4


def _causal_keep(r0, rows, tq):
    key = r0 + lax.broadcasted_iota(jnp.int32, (rows, tq), 0)
    qry = lax.broadcasted_iota(jnp.int32, (rows, tq), 1)
    return key <= qry


def _flash_fwd(qp, kp, vt, S, shards=()):
    tq = min(512, S)
    nq = S // tq
    RB = ATT_ROWS
    NH = FWD_HEADS
    qb_of, kb_of, T = _tri_pairs(nq, k_major=False)
    n = len(shards)
    steps = (MLA_HEADS // NH) * T

    def body(qb_ref, kb_ref, q_ref, k_ref, vt_ref, *rest):
        w_ins, (o_ref, lse_ref), w_outs = rest[:n], rest[n:n + 2], rest[n + 2:2 * n + 2]
        m_sc, l_sc, acc_sc, s_sc, p_sc = rest[2 * n + 2:2 * n + 7]
        sems = rest[2 * n + 7:]
        t = pl.program_id(1)
        qb = qb_ref[t]
        kb = kb_ref[t]
        lin = pl.program_id(0) * T + t

        if n:
            @pl.when(lin == 0)
            def _():
                _gather_phase(0, w_ins, w_outs, sems)

            @pl.when(lin == steps // 2)
            def _():
                _gather_phase(1, w_ins, w_outs, sems)

        @pl.when(kb == 0)
        def _():
            m_sc[...] = jnp.full(m_sc.shape, NEG, F32)
            l_sc[...] = jnp.zeros(l_sc.shape, F32)
            acc_sc[...] = jnp.zeros(acc_sc.shape, F32)

        def scores(a):
            sl = slice(a * 128, (a + 1) * 128)
            s_sc[a] = _dot_nt(k_ref[:, sl], q_ref[:, sl])

        def step(masked):
            for a in range(NH):
                scores(a)
            for a in range(NH):
                mx = [jnp.full((8, tq), NEG, F32) for _ in range(RB // 8)]
                for r in range(0, tq, RB):
                    sc = s_sc[a, r:r + RB, :]
                    if masked:
                        sc = jnp.where(_causal_keep(r, RB, tq), sc, NEG)
                        s_sc[a, r:r + RB, :] = sc
                    for i in range(RB // 8):
                        mx[i] = jnp.maximum(mx[i], sc[i * 8:(i + 1) * 8, :])
                mx8 = functools.reduce(jnp.maximum, mx)
                m_prev = m_sc[a]
                m_new = jnp.maximum(m_prev, jnp.max(mx8, axis=0, keepdims=True))
                al = jnp.exp(m_prev - m_new)
                m_sc[a] = m_new
                ls = [jnp.zeros((8, tq), F32) for _ in range(RB // 8)]
                for r in range(0, tq, RB):
                    p = jnp.exp(s_sc[a, r:r + RB, :] - m_new)
                    for i in range(RB // 8):
                        ls[i] = ls[i] + p[i * 8:(i + 1) * 8, :]
                    p_sc[a, r:r + RB, :] = p.astype(BF16)
                l_sc[a] = al * l_sc[a] + jnp.sum(functools.reduce(jnp.add, ls), axis=0, keepdims=True)
                pair = slice((a // 2) * 128, (a // 2 + 1) * 128)
                pv = jnp.dot(vt_ref[pair, :], p_sc[a], preferred_element_type=F32)
                rs = slice(a * 64, (a + 1) * 64)
                own = slice((a % 2) * 64, (a % 2 + 1) * 64)
                acc_sc[rs, :] = acc_sc[rs, :] * al + pv[own, :]

        @pl.when(kb < qb)
        def _():
            step(False)

        @pl.when(kb == qb)
        def _():
            step(True)
            for a in range(NH):
                rs = slice(a * 64, (a + 1) * 64)
                acc_sc[rs, :] = acc_sc[rs, :] / l_sc[a]
                lse_ref[a:a + 1, :] = m_sc[a] + jnp.log(l_sc[a])
            o_ref[...] = acc_sc[...].T.astype(BF16)

        if n:
            @pl.when(lin == steps - 1)
            def _():
                _gather_phase(2, w_ins, w_outs, sems)

    grid_spec = pltpu.PrefetchScalarGridSpec(
        num_scalar_prefetch=2, grid=(MLA_HEADS // NH, T),
        in_specs=[pl.BlockSpec((tq, 128 * NH), lambda j, t, qb, kb: (qb[t], j)),
                  pl.BlockSpec((tq, 128 * NH), lambda j, t, qb, kb: (kb[t], j)),
                  pl.BlockSpec((64 * NH, tq), lambda j, t, qb, kb: (j, kb[t]))] + [_ANY] * n,
        out_specs=[pl.BlockSpec((tq, 64 * NH), lambda j, t, qb, kb: (qb[t], j)),
                   pl.BlockSpec((None, NH, tq), lambda j, t, qb, kb: (j, 0, qb[t]))] + [_ANY] * n,
        scratch_shapes=[pltpu.VMEM((NH, 1, tq), F32), pltpu.VMEM((NH, 1, tq), F32), pltpu.VMEM((64 * NH, tq), F32),
                        pltpu.VMEM((NH, tq, tq), F32), pltpu.VMEM((NH, tq, tq), BF16)] + (_gather_sems(n) if n else []),
    )
    out, lse, *gathered = pl.pallas_call(
        body, name="flash_fwd", grid_spec=grid_spec,
        out_shape=[_sds((S, 512), BF16), _sds((MLA_HEADS // NH, NH, S), F32)] + _gather_out_shapes(shards),
        compiler_params=_cp(("arbitrary", "arbitrary")),
    )(qb_of, kb_of, qp, kp, vt, *shards)
    return out, lse.reshape(MLA_HEADS // 2, 2, S), gathered


def _decay_table():
    log_g = np.log(1.0 - 2.0 ** (-5.0 - np.arange(RET_HEADS, dtype=np.float32))).astype(np.float32)
    return jnp.asarray(np.broadcast_to(log_g[:, None, None], (RET_HEADS, 8, 128)).copy())


def _decay_terms(lg_ref):
    C = RET_CHUNK
    lg = lg_ref[0:1, :]
    row = lax.broadcasted_iota(jnp.int32, (C, C), 0)
    col = lax.broadcasted_iota(jnp.int32, (C, C), 1)
    diff = (row - col).astype(F32)
    dmat = jnp.where(diff >= 0, jnp.exp(jnp.maximum(diff, 0.0) * jnp.tile(lg, (1, C // 128))), 0.0)
    j = lax.broadcasted_iota(jnp.int32, (C, 1), 0).astype(F32)
    lg1 = lg[:, 0:1]
    zeta = jnp.exp((C - 1 - j) * lg1)
    xi = jnp.exp((j + 1.0) * lg1)
    g_chunk = jnp.exp(C * lg1)
    return dmat, zeta, xi, g_chunk


def _ret_fwd(rq, rk, rv, rg, gn_w, S):
    C = RET_CHUNK
    N = S // C
    G = min(RET_GROUP, N)
    NB = N // G

    def body(lg_ref, q_ref, k_ref, v_ref, rg_ref, w_ref, ry_ref, ro_ref, rprev_ref, r_sc):
        @pl.when(pl.program_id(1) == 0)
        def _():
            r_sc[...] = jnp.zeros(r_sc.shape, F32)

        dmat, zeta, xi, g_chunk = _decay_terms(lg_ref)
        w = w_ref[...]
        r = r_sc[...]
        for i in range(G):
            rows = slice(i * C, (i + 1) * C)
            q = q_ref[rows, :]
            k = k_ref[rows, :]
            v = v_ref[rows, :]
            r_prev = r.astype(BF16)
            rprev_ref[i] = r_prev
            sc = _dot_nt(q, k) * dmat
            ry = _dot(sc, v) + jnp.dot(q, r_prev, preferred_element_type=F32) * xi
            ry_ref[rows, :] = ry
            r = g_chunk * r + _dot_tn(k, zeta * v.astype(F32))
            mu = jnp.mean(ry, axis=-1, keepdims=True)
            yc = ry - mu
            yh = yc * lax.rsqrt(jnp.mean(yc * yc, axis=-1, keepdims=True) + EPS)
            g = rg_ref[rows, :]
            ro_ref[rows, :] = (g * _sig(g) * (yh * w)).astype(BF16)
        r_sc[...] = r

    blk = pl.BlockSpec((G * C, 128), lambda h, n: (n, h))
    return pl.pallas_call(
        body, name="ret_fwd", grid=(RET_HEADS, NB),
        in_specs=[pl.BlockSpec((None, 8, 128), lambda h, n: (h, 0, 0)), blk, blk, blk, blk,
                  pl.BlockSpec((1, 128), lambda h, n: (0, h))],
        out_specs=[blk, blk, pl.BlockSpec((G, 128, 128), lambda h, n: (h * NB + n, 0, 0))],
        out_shape=[_sds((S, 512), F32), _sds((S, 512), BF16), _sds((RET_HEADS * N, 128, 128), BF16)],
        scratch_shapes=[pltpu.VMEM((128, 128), F32)],
        compiler_params=_cp(("parallel", "arbitrary")),
    )(_decay_table(), rq, rk, rv, rg, gn_w)


def _outproj(ro, mo, x, w_o, g_post, g_pre, S):
    tm = min(512, S)

    def body(ro_ref, mo_ref, x_ref, wo_ref, g1_ref, g2_ref, mix_ref, h1_ref, hn_ref):
        mix = (jnp.dot(ro_ref[...], wo_ref[0:512, :], preferred_element_type=F32)
               + jnp.dot(mo_ref[...], wo_ref[512:1024, :], preferred_element_type=F32))
        mix_ref[...] = mix.astype(BF16)
        h1 = x_ref[...] + _rms(mix, g1_ref[...])
        h1_ref[...] = h1
        hn_ref[...] = _rms(h1, g2_ref[...]).astype(BF16)

    return pl.pallas_call(
        body, name="outproj", grid=(S // tm,),
        in_specs=[_rows(tm, 512), _rows(tm, 512), _rows(tm, D_MODEL), _full(D_MODEL, D_MODEL), _full(1, D_MODEL),
                  _full(1, D_MODEL)],
        out_specs=[_rows(tm, D_MODEL)] * 3,
        out_shape=[_sds((S, D_MODEL), BF16), _sds((S, D_MODEL), F32), _sds((S, D_MODEL), BF16)],
        compiler_params=_cp(("parallel",)),
    )(ro, mo, x, w_o, g_post, g_pre)


def _ffn_up(hn, w_gate, w_up, S):
    tm = min(512, S)
    tn = D_FF // 2

    def body(hn_ref, wg_ref, wu_ref, gate_ref, up_ref, act_ref):
        hn_b = hn_ref[...]
        g = jnp.dot(hn_b, wg_ref[...], preferred_element_type=F32)
        u = jnp.dot(hn_b, wu_ref[...], preferred_element_type=F32)
        gate_ref[...] = g.astype(BF16)
        up_ref[...] = u.astype(BF16)
        act_ref[...] = (g * _sig(g) * u).astype(BF16)

    wspec = pl.BlockSpec((D_MODEL, tn), lambda j, i: (0, j))
    ospec = pl.BlockSpec((tm, tn), lambda j, i: (i, j))
    return pl.pallas_call(
        body, name="ffn_up", grid=(2, S // tm),
        in_specs=[pl.BlockSpec((tm, D_MODEL), lambda j, i: (i, 0)), wspec, wspec],
        out_specs=[ospec] * 3, out_shape=[_sds((S, D_FF), BF16)] * 3,
        compiler_params=_cp(("parallel", "parallel")),
    )(hn, w_gate, w_up)


def _ffn_down(act, w_down, h1, g, S):
    tm = min(512, S)

    def body(act_ref, wd_ref, h1_ref, g_ref, ff_ref, h2_ref):
        ff = jnp.dot(act_ref[...], wd_ref[...], preferred_element_type=F32)
        ff_ref[...] = ff.astype(BF16)
        h2_ref[...] = h1_ref[...] + _rms(ff, g_ref[...])

    return pl.pallas_call(
        body, name="ffn_down", grid=(S // tm,),
        in_specs=[_rows(tm, D_FF), _full(D_FF, D_MODEL), _rows(tm, D_MODEL), _full(1, D_MODEL)],
        out_specs=[_rows(tm, D_MODEL)] * 2, out_shape=[_sds((S, D_MODEL), BF16), _sds((S, D_MODEL), F32)],
        compiler_params=_cp(("parallel",)),
    )(act, w_down, h1, g)


def _ple_loss(p, h2, tgt, w_pp, w_pg, b_pg, g_ple, S):
    tm = min(512, S)

    def body(p_ref, h2_ref, t_ref, wp_ref, wg_ref, b_ref, gp_ref,
             dz_ref, dpe_ref, dh2_ref, h2b_ref, loss_ref, dgp_ref, db_ref):
        @pl.when(pl.program_id(0) == 0)
        def _():
            loss_ref[...] = jnp.zeros(loss_ref.shape, F32)
            dgp_ref[...] = jnp.zeros(dgp_ref.shape, F32)
            db_ref[...] = jnp.zeros(db_ref.shape, F32)

        gp = gp_ref[...]
        pe = _dot(p_ref[...], wp_ref[...])
        r = lax.rsqrt(jnp.mean(pe * pe, axis=-1, keepdims=True) + EPS)
        peh = pe * r
        e = peh * gp
        h2 = h2_ref[...]
        h2b = h2.astype(BF16)
        h2b_ref[...] = h2b
        gt = _sig(jnp.dot(h2b, wg_ref[...], preferred_element_type=F32) + b_ref[...])
        diff = h2 + e * gt - t_ref[...]
        loss_ref[...] += _colsum(diff * diff)
        dh3 = diff * (1.0 / D_MODEL)
        de = dh3 * gt
        dz = dh3 * e * gt * (1.0 - gt)
        db_ref[...] += _colsum(dz)
        dgp_ref[...] += _colsum(de * peh)
        dpeh = de * gp
        dpe = r * (dpeh - peh * jnp.mean(dpeh * peh, axis=-1, keepdims=True))
        dzb = dz.astype(BF16)
        dz_ref[...] = dzb
        dpe_ref[...] = dpe.astype(BF16)
        dh2_ref[...] = dh3 + _dot_nt(dzb, wg_ref[...])

    return pl.pallas_call(
        body, name="ple_loss", grid=(S // tm,),
        in_specs=[_rows(tm, PLE_DIM), _rows(tm, D_MODEL), _rows(tm, D_MODEL), _full(PLE_DIM, D_MODEL),
                  _full(D_MODEL, D_MODEL), _full(1, D_MODEL), _full(1, D_MODEL)],
        out_specs=[_rows(tm, D_MODEL)] * 4 + [_acc(1, D_MODEL)] * 3,
        out_shape=[_sds((S, D_MODEL), BF16), _sds((S, D_MODEL), BF16), _sds((S, D_MODEL), F32), _sds((S, D_MODEL), BF16)]
        + [_sds((1, D_MODEL), F32)] * 3,
        compiler_params=_cp(("arbitrary",)),
    )(p, h2, tgt, w_pp, w_pg, b_pg, g_ple)


def _wgrad(a, b, name, S):
    M = a.shape[1]
    N = b.shape[1]
    ts = min(2048, S)
    nsplit = 2 if M * N >= 2 * 1024 * 1024 else 1
    tn = N // nsplit

    def body(a_ref, b_ref, o_ref):
        @pl.when(pl.program_id(1) == 0)
        def _():
            o_ref[...] = jnp.zeros(o_ref.shape, F32)

        o_ref[...] += _dot_tn(a_ref[...], b_ref[...])

    return pl.pallas_call(
        body, name=name, grid=(nsplit, S // ts),
        in_specs=[pl.BlockSpec((ts, M), lambda j, s: (s, 0)), pl.BlockSpec((ts, tn), lambda j, s: (s, j))],
        out_specs=pl.BlockSpec((M, tn), lambda j, s: (0, j)), out_shape=_sds((M, N), F32),
        compiler_params=_cp(("parallel", "arbitrary")),
    )(a, b)


def _ffn_down_bwd(dh2, ff, g, w_down, gate, up, S):
    tm = min(512, S)
    tn = D_FF // 2

    def body(dh2_ref, ff_ref, g_ref, wd_ref, gate_ref, up_ref, dff_ref, dgate_ref, dup_ref, dg_ref):
        @pl.when(pl.program_id(0) == 0)
        def _():
            dg_ref[...] = jnp.zeros(dg_ref.shape, F32)

        dff, ga = _rms_bwd(dh2_ref[...], ff_ref[...].astype(F32), g_ref[...])
        dg_ref[...] += _colsum(ga)
        dffb = dff.astype(BF16)
        dff_ref[...] = dffb
        for seg in range(2):
            sl = slice(seg * tn, (seg + 1) * tn)
            dact = _dot_nt(dffb, wd_ref[sl, :])
            gt = gate_ref[:, sl].astype(F32)
            u = up_ref[:, sl].astype(F32)
            s = _sig(gt)
            dgate_ref[:, sl] = (dact * u * (s * (1.0 + gt * (1.0 - s)))).astype(BF16)
            dup_ref[:, sl] = (dact * (gt * s)).astype(BF16)

    return pl.pallas_call(
        body, name="ffn_down_bwd", grid=(S // tm,),
        in_specs=[_rows(tm, D_MODEL), _rows(tm, D_MODEL), _full(1, D_MODEL), _full(D_FF, D_MODEL), _rows(tm, D_FF),
                  _rows(tm, D_FF)],
        out_specs=[_rows(tm, D_MODEL), _rows(tm, D_FF), _rows(tm, D_FF), _acc(1, D_MODEL)],
        out_shape=[_sds((S, D_MODEL), BF16), _sds((S, D_FF), BF16), _sds((S, D_FF), BF16), _sds((1, D_MODEL), F32)],
        compiler_params=_cp(("arbitrary",)),
    )(dh2, ff, g, w_down, gate, up)


def _ffn_up_bwd(dgate, dup, w_gate, w_up, h1, mix, dh2, g_pre, g_post, w_o, S, grads=()):
    tm = min(512, S)
    n = len(grads)
    last = S // tm - 1

    def body(dgate_ref, dup_ref, wg_ref, wu_ref, h1_ref, mix_ref, dh2_ref, g2_ref, g1_ref, wo_ref, *rest):
        g_ins = rest[:n]
        dh1_ref, dmix_ref, dro_ref, dmo_ref, dg2_ref, dg1_ref = rest[n:n + 6]
        g_outs, sems = rest[n + 6:2 * n + 6], rest[2 * n + 6:]

        @pl.when(pl.program_id(0) == 0)
        def _():
            dg2_ref[...] = jnp.zeros(dg2_ref.shape, F32)
            dg1_ref[...] = jnp.zeros(dg1_ref.shape, F32)
            for cp in (_swap_copies(g_ins, g_outs, sems) if n else []):
                cp.start()

        dhn = _dot_nt(dgate_ref[...], wg_ref[...]) + _dot_nt(dup_ref[...], wu_ref[...])
        d1, ga = _rms_bwd(dhn, h1_ref[...], g2_ref[...])
        dg2_ref[...] += _colsum(ga)
        dh1 = dh2_ref[...] + d1
        dh1_ref[...] = dh1
        dmix, gb = _rms_bwd(dh1, mix_ref[...].astype(F32), g1_ref[...])
        dg1_ref[...] += _colsum(gb)
        dmixb = dmix.astype(BF16)
        dmix_ref[...] = dmixb
        dcat = _dot_nt(dmixb, wo_ref[...])
        dro_ref[...] = dcat[:, 0:512].astype(BF16)
        dmo_ref[...] = dcat[:, 512:1024].astype(BF16)

        if n:
            @pl.when(pl.program_id(0) == last)
            def _():
                for cp in _swap_copies(g_ins, g_outs, sems):
                    cp.wait()

    dh1, dmix, dro, dmo, dg2, dg1, *got = pl.pallas_call(
        body, name="ffn_up_bwd", grid=(S // tm,),
        in_specs=[_rows(tm, D_FF), _rows(tm, D_FF), _full(D_MODEL, D_FF), _full(D_MODEL, D_FF), _rows(tm, D_MODEL),
                  _rows(tm, D_MODEL), _rows(tm, D_MODEL), _full(1, D_MODEL), _full(1, D_MODEL), _full(D_MODEL, D_MODEL)]
        + [_ANY] * n,
        out_specs=[_rows(tm, D_MODEL), _rows(tm, D_MODEL), _rows(tm, 512), _rows(tm, 512), _acc(1, D_MODEL),
                   _acc(1, D_MODEL)] + [_ANY] * n,
        out_shape=[_sds((S, D_MODEL), F32), _sds((S, D_MODEL), BF16), _sds((S, 512), BF16), _sds((S, 512), BF16),
                   _sds((1, D_MODEL), F32), _sds((1, D_MODEL), F32)] + _swap_out_shapes(grads),
        scratch_shapes=_swap_sems(n) if n else [],
        compiler_params=_cp(("arbitrary",)),
    )(dgate, dup, w_gate, w_up, h1, mix, dh2, g_pre, g_post, w_o, *grads)
    return dh1, dmix, dro, dmo, dg2, dg1, got


def _attn_delta(o, do, S):
    tm = min(512, S)

    def body(o_ref, do_ref, dot_ref, d_ref):
        do = do_ref[...].astype(F32)
        prod_t = (o_ref[...].astype(F32) * do).T
        dot_ref[...] = do.T.astype(BF16)
        for h in range(MLA_HEADS):
            d_ref[h // 2, (h % 2):(h % 2) + 1, :] = jnp.sum(prod_t[h * 64:(h + 1) * 64, :], axis=0, keepdims=True)

    return pl.pallas_call(
        body, name="attn_delta", grid=(S // tm,),
        in_specs=[_rows(tm, 512), _rows(tm, 512)],
        out_specs=[pl.BlockSpec((512, tm), lambda i: (0, i)), pl.BlockSpec((MLA_HEADS // 2, 2, tm), lambda i: (0, 0, i))],
        out_shape=[_sds((512, S), BF16), _sds((MLA_HEADS // 2, 2, S), F32)],
        compiler_params=_cp(("parallel",)),
    )(o, do)


def _flash_bwd(qp, kp, kt, v, do, dot, lse, delta, S, sums=()):
    tq = min(512, S)
    nq = S // tq
    RB = ATT_ROWS
    NH = BWD_HEADS
    qb_of, kb_of, T = _tri_pairs(nq, k_major=True)
    n = len(sums)
    steps = (MLA_HEADS // NH) * T

    def body(qb_ref, kb_ref, q_ref, k_ref, kt_ref, v_ref, do_ref, dot_ref, lse_ref, dl_ref, *rest):
        g_ins, (dq_ref, dk_ref, dv_ref), g_outs = rest[:n], rest[n:n + 3], rest[n + 3:2 * n + 3]
        dk_sc, dv_sc, s_sc, dp_sc, p_sc, ds_sc = rest[2 * n + 3:2 * n + 9]
        sems = rest[2 * n + 9:]
        t = pl.program_id(1)
        qb = qb_ref[t]
        kb = kb_ref[t]
        lin = pl.program_id(0) * T + t

        if n:
            @pl.when(lin == 0)
            def _():
                for cp in _scatter_copies(g_ins, g_outs, sems):
                    cp.start()

        @pl.when(t == 0)
        def _():
            dq_ref[...] = jnp.zeros(dq_ref.shape, F32)

        @pl.when(qb == kb)
        def _():
            dk_sc[...] = jnp.zeros(dk_sc.shape, F32)
            dv_sc[...] = jnp.zeros(dv_sc.shape, F32)

        lane = lax.broadcasted_iota(jnp.int32, (tq, 64 * NH), 1)

        def step(masked):
            vv = v_ref[...]
            do_all = do_ref[...]
            mine = [(lane >= a * 64) & (lane < (a + 1) * 64) for a in range(NH)]
            for a in range(NH):
                sl = slice(a * 128, (a + 1) * 128)
                s_sc[a] = _dot_nt(k_ref[:, sl], q_ref[:, sl])
                dp_sc[a] = jnp.dot(jnp.where(mine[a], vv, jnp.zeros_like(vv)), dot_ref[...],
                                   preferred_element_type=F32)
            for a in range(NH):
                sl = slice(a * 128, (a + 1) * 128)
                lse = lse_ref[a:a + 1, :]
                dl = dl_ref[a:a + 1, :]
                for r in range(0, tq, RB):
                    sc = s_sc[a, r:r + RB, :]
                    if masked:
                        sc = jnp.where(_causal_keep(r, RB, tq), sc, NEG)
                    p = jnp.exp(sc - lse)
                    p_sc[a, r:r + RB, :] = p.astype(BF16)
                    ds_sc[a, r:r + RB, :] = (p * (dp_sc[a, r:r + RB, :] - dl)).astype(BF16)
                ds = ds_sc[a]
                dv_sc[...] += jnp.dot(p_sc[a], jnp.where(mine[a], do_all, jnp.zeros_like(do_all)),
                                      preferred_element_type=F32)
                dk_sc[:, sl] += jnp.dot(ds, q_ref[:, sl], preferred_element_type=F32)
                dq_ref[qb, sl, :] += jnp.dot(kt_ref[sl, :], ds, preferred_element_type=F32)

        @pl.when(qb > kb)
        def _():
            step(False)

        @pl.when(qb == kb)
        def _():
            step(True)

        @pl.when(qb == nq - 1)
        def _():
            dk_ref[...] = dk_sc[...].astype(BF16)
            dv_ref[...] = dv_sc[...].astype(BF16)

        if n:
            @pl.when(lin == steps - 1)
            def _():
                for cp in _scatter_copies(g_ins, g_outs, sems):
                    cp.wait()

    grid_spec = pltpu.PrefetchScalarGridSpec(
        num_scalar_prefetch=2, grid=(MLA_HEADS // NH, T),
        in_specs=[pl.BlockSpec((tq, 128 * NH), lambda j, t, qb, kb: (qb[t], j)),
                  pl.BlockSpec((tq, 128 * NH), lambda j, t, qb, kb: (kb[t], j)),
                  pl.BlockSpec((128 * NH, tq), lambda j, t, qb, kb: (j, kb[t])),
                  pl.BlockSpec((tq, 64 * NH), lambda j, t, qb, kb: (kb[t], j)),
                  pl.BlockSpec((tq, 64 * NH), lambda j, t, qb, kb: (qb[t], j)),
                  pl.BlockSpec((64 * NH, tq), lambda j, t, qb, kb: (j, qb[t])),
                  pl.BlockSpec((None, NH, tq), lambda j, t, qb, kb: (j, 0, qb[t])),
                  pl.BlockSpec((None, NH, tq), lambda j, t, qb, kb: (j, 0, qb[t]))] + [_ANY] * n,
        out_specs=[pl.BlockSpec((nq, 128 * NH, tq), lambda j, t, qb, kb: (0, j, 0), pipeline_mode=pl.Buffered(1)),
                   pl.BlockSpec((tq, 128 * NH), lambda j, t, qb, kb: (kb[t], j)),
                   pl.BlockSpec((tq, 64 * NH), lambda j, t, qb, kb: (kb[t], j))] + [_ANY] * n,
        scratch_shapes=[pltpu.VMEM((tq, 128 * NH), F32), pltpu.VMEM((tq, 64 * NH), F32), pltpu.VMEM((NH, tq, tq), F32),
                        pltpu.VMEM((NH, tq, tq), F32), pltpu.VMEM((NH, tq, tq), BF16), pltpu.VMEM((NH, tq, tq), BF16)]
        + (_scatter_sems(n) if n else []),
    )
    dq, dk, dv, *parts = pl.pallas_call(
        body, name="flash_bwd", grid_spec=grid_spec,
        out_shape=[_sds((nq, 1024, tq), F32), _sds((S, 1024), BF16), _sds((S, 512), BF16)] + _scatter_out_shapes(sums),
        compiler_params=_cp(("arbitrary", "arbitrary")),
    )(qb_of, kb_of, qp, kp, kt, v, do, dot, lse.reshape(MLA_HEADS // NH, NH, S), delta.reshape(MLA_HEADS // NH, NH, S),
      *sums)
    return dq, dk, dv, parts


def _mla_up_bwd(dqp, dkp, dv, cq, ckv, gq, gkv, w_uq, w_ukv, tabs, S):
    tm = min(512, S)

    def body(dq_ref, dk_ref, dv_ref, cq_ref, ckv_ref, gq_ref, gkv_ref, wuq_ref, wukv_ref, cm_ref, sa_ref, sb_ref,
             dqh_ref, dkv_ref, dcq_ref, dckv_ref, dkr_ref, dgq_ref, dgkv_ref):
        @pl.when(pl.program_id(0) == 0)
        def _():
            dgq_ref[...] = jnp.zeros(dgq_ref.shape, F32)
            dgkv_ref[...] = jnp.zeros(dgkv_ref.shape, F32)

        cm = cm_ref[...]
        sa = sa_ref[...]
        sb = sb_ref[...]
        lane = lax.broadcasted_iota(jnp.int32, (tm, 128), 1)
        dkr_r = jnp.zeros((tm, 128), F32)
        for h in range(MLA_HEADS):
            sl = slice(h * 128, (h + 1) * 128)
            dqh_ref[:, sl] = (_unrope_mla(dq_ref[sl, :].T, cm, sa, sb) * SCALE_MLA).astype(BF16)
            gk = dk_ref[:, sl]
            dkr_r = dkr_r + gk.astype(F32)
            dkv_ref[:, sl] = gk
        dkr_r = jnp.where((lane >= 64) & (lane < 96), dkr_r, 0.0)
        dkr_ref[...] = _unrope_mla(dkr_r, cm, sa, sb).astype(BF16)
        dkv_ref[:, 1024:1536] = dv_ref[...]
        dcq, ga = _rms_bwd(_dot_nt(dqh_ref[...], wuq_ref[...]), cq_ref[...], gq_ref[...])
        dcq_ref[...] = dcq.astype(BF16)
        dgq_ref[...] += _colsum(ga)
        dckv, gb = _rms_bwd(_dot_nt(dkv_ref[...], wukv_ref[...]), ckv_ref[...], gkv_ref[...])
        dckv_ref[...] = dckv.astype(BF16)
        dgkv_ref[...] += _colsum(gb)

    per_q = dqp.shape[2] // tm
    return pl.pallas_call(
        body, name="mla_up_bwd", grid=(S // tm,),
        in_specs=[pl.BlockSpec((None, 1024, tm), lambda i: (i // per_q, 0, i % per_q)),
                  _rows(tm, 1024), _rows(tm, 512), _rows(tm, Q_LORA), _rows(tm, KV_LORA),
                  _full(1, Q_LORA), _full(1, KV_LORA), _full(Q_LORA, 1024), _full(KV_LORA, 1536)] + [_rows(tm, 128)] * 3,
        out_specs=[_rows(tm, 1024), _rows(tm, 1536), _rows(tm, Q_LORA), _rows(tm, KV_LORA), _rows(tm, 128),
                   _acc(1, Q_LORA), _acc(1, KV_LORA)],
        out_shape=[_sds((S, 1024), BF16), _sds((S, 1536), BF16), _sds((S, Q_LORA), BF16), _sds((S, KV_LORA), BF16),
                   _sds((S, 128), BF16), _sds((1, Q_LORA), F32), _sds((1, KV_LORA), F32)],
        compiler_params=_cp(("arbitrary",)),
    )(dqp, dkp, dv, cq, ckv, gq, gkv, w_uq, w_ukv, *tabs[2:])


def _ret_bwd(rq, rk, rv, rprev, ry, rg, dro, gn_w, tabs, S):
    C = RET_CHUNK
    N = S // C
    G = min(RET_GROUP, N)
    NB = N // G

    def body(lg_ref, q_ref, k_ref, v_ref, rp_ref, ry_ref, rg_ref, dro_ref, w_ref, cr_ref, sr_ref,
             drq_ref, drk_ref, drv_ref, drg_ref, dw_ref, g_sc):
        @pl.when(pl.program_id(1) == 0)
        def _():
            g_sc[...] = jnp.zeros(g_sc.shape, F32)
            dw_ref[...] = jnp.zeros(dw_ref.shape, F32)

        dmat, zeta, xi, g_chunk = _decay_terms(lg_ref)
        w = w_ref[...]
        gacc = g_sc[...]
        dw = jnp.zeros((1, 128), F32)
        for i in reversed(range(G)):
            rows = slice(i * C, (i + 1) * C)
            ry = ry_ref[rows, :]
            mu = jnp.mean(ry, axis=-1, keepdims=True)
            yc = ry - mu
            rstd = lax.rsqrt(jnp.mean(yc * yc, axis=-1, keepdims=True) + EPS)
            yh = yc * rstd
            g = rg_ref[rows, :]
            s = _sig(g)
            dout = dro_ref[rows, :].astype(F32)
            drg_ref[rows, :] = (dout * (yh * w) * (s * (1.0 + g * (1.0 - s)))).astype(BF16)
            dgn = dout * (g * s)
            dw = dw + _colsum(dgn * yh)
            dyh = dgn * w
            dry = rstd * (dyh - jnp.mean(dyh, axis=-1, keepdims=True) - yh * jnp.mean(dyh * yh, axis=-1, keepdims=True))
            do = dry.astype(BF16)

            q = q_ref[rows, :]
            k = k_ref[rows, :]
            v = v_ref[rows, :]
            gfut = gacc.astype(BF16)
            sc = (_dot_nt(q, k) * dmat).astype(BF16)
            dsc = (_dot_nt(do, v) * dmat).astype(BF16)
            dq = jnp.dot(dsc, k, preferred_element_type=F32) + _dot_nt(do, rp_ref[i]) * xi
            dk = _dot_tn(dsc, q) + _dot_nt(v, gfut) * zeta
            dv = _dot_tn(sc, do) + jnp.dot(k, gfut, preferred_element_type=F32) * zeta
            gacc = g_chunk * gacc + _dot_tn(q, xi * dry)
            cr = cr_ref[rows, :]
            sr = sr_ref[rows, :]
            drq_ref[rows, :] = _unrope_ret(dq, cr, sr).astype(BF16)
            drk_ref[rows, :] = _unrope_ret(dk * SCALE_RET, cr, sr).astype(BF16)
            drv_ref[rows, :] = dv.astype(BF16)
        g_sc[...] = gacc
        dw_ref[...] += dw

    blk = pl.BlockSpec((G * C, 128), lambda h, n: (NB - 1 - n, h))
    tab = pl.BlockSpec((G * C, 128), lambda h, n: (NB - 1 - n, 0))
    return pl.pallas_call(
        body, name="ret_bwd", grid=(RET_HEADS, NB),
        in_specs=[pl.BlockSpec((None, 8, 128), lambda h, n: (h, 0, 0)), blk, blk, blk,
                  pl.BlockSpec((G, 128, 128), lambda h, n: (h * NB + NB - 1 - n, 0, 0)), blk, blk, blk,
                  pl.BlockSpec((1, 128), lambda h, n: (0, h)), tab, tab],
        out_specs=[blk, blk, blk, blk, pl.BlockSpec((1, 128), lambda h, n: (0, h))],
        out_shape=[_sds((S, 512), BF16)] * 4 + [_sds((1, 512), F32)],
        scratch_shapes=[pltpu.VMEM((128, 128), F32)],
        compiler_params=_cp(("parallel", "arbitrary")),
    )(_decay_table(), rq, rk, rv, rprev, ry, rg, dro, gn_w, tabs[0], tabs[1])


def _inproj_bwd(drq, drk, drv, drg, dcq, dckv, dkr, w_in, dh1, x, g, S):
    tm = min(512, S)

    def body(drq_ref, drk_ref, drv_ref, drg_ref, dcq_ref, dckv_ref, dkr_ref, w_ref, dh1_ref, x_ref, g_ref,
             gx_ref, dproj_ref, dg_ref):
        @pl.when(pl.program_id(0) == 0)
        def _():
            dg_ref[...] = jnp.zeros(dg_ref.shape, F32)

        dproj_ref[:, 0:512] = drq_ref[...]
        dproj_ref[:, 512:1024] = drk_ref[...]
        dproj_ref[:, 1024:1536] = drv_ref[...]
        dproj_ref[:, 1536:2048] = drg_ref[...]
        dproj_ref[:, 2048:2432] = dcq_ref[...]
        dproj_ref[:, 2432:2688] = dckv_ref[...]
        dproj_ref[:, 2688:2816] = dkr_ref[...]
        dx, ga = _rms_bwd(_dot_nt(dproj_ref[...], w_ref[...]), x_ref[...], g_ref[...])
        gx_ref[...] = dh1_ref[...] + dx
        dg_ref[...] += _colsum(ga)

    return pl.pallas_call(
        body, name="inproj_bwd", grid=(S // tm,),
        in_specs=[_rows(tm, 512)] * 4 + [_rows(tm, Q_LORA), _rows(tm, KV_LORA), _rows(tm, 128),
                                         _full(D_MODEL, IN_COLS_P), _rows(tm, D_MODEL), _rows(tm, D_MODEL),
                                         _full(1, D_MODEL)],
        out_specs=[_rows(tm, D_MODEL), _rows(tm, IN_COLS_P), _acc(1, D_MODEL)],
        out_shape=[_sds((S, D_MODEL), F32), _sds((S, IN_COLS_P), BF16), _sds((1, D_MODEL), F32)],
        compiler_params=_cp(("arbitrary",)),
    )(drq, drk, drv, drg, dcq, dckv, dkr, w_in, dh1, x, g)


def _pad_weights(w):
    w_in = w["w_in"]
    z = lambda r, c: jnp.zeros((r, c), BF16)
    w_in_p = jnp.concatenate([w_in[:, :2688], z(1024, 64), w_in[:, 2688:2720], z(1024, 32)], axis=1)
    w_uq_p = jnp.pad(w["w_uq"].reshape(Q_LORA, MLA_HEADS, 96), ((0, 0), (0, 0), (0, 32))).reshape(Q_LORA, 1024)
    ukv = w["w_ukv"].reshape(KV_LORA, MLA_HEADS, 128)
    k_part = jnp.pad(ukv[:, :, :64], ((0, 0), (0, 0), (0, 64))).reshape(KV_LORA, 1024)
    w_ukv_p = jnp.concatenate([k_part, ukv[:, :, 64:].reshape(KV_LORA, 512)], axis=1)
    return w_in_p, w_uq_p, w_ukv_p


BIG_SPEC = {n: (r, c, ax) for n, r, c, ax in BIG}
COLUMN_MAJOR = ("w_in", "w_uq", "w_gate", "w_up")
GATHER_FIRST = ("w_in", "w_uq", "w_ukv")
GATHER_LATE = tuple(n for n, _, _, _ in BIG if n not in GATHER_FIRST)
REDUCE_EARLY = ("w_ple_gate", "w_ple_proj", "w_down", "w_gate", "w_up")
REDUCE_LAST = tuple(n for n, _, _, _ in BIG if n not in REDUCE_EARLY)


def _local_step(x, p, pos_f, tgt, w, sm, late_shards=None, c_idx=None):
    S = x.shape[0]
    spread = late_shards is not None
    w = dict(w)
    tabs, first = _rope_tables(pos_f, S, [late_shards[n] for n in GATHER_FIRST] if spread else ())
    for i, n in enumerate(GATHER_FIRST if spread else ()):
        w[n] = _from_chips(first[i], BIG_SPEC[n][2])
    w_in_p, w_uq_p, w_ukv_p = _pad_weights(w)

    xn, rq, rk, rv, rg, cq, ckv, kr = _inproj(x, sm["pre_mix_norm"], w_in_p, tabs, S)
    cqn, ckvn, qp, kp, v, kt, vt = _mla_up(cq, ckv, kr, sm["mla_q_norm"], sm["mla_kv_norm"], w_uq_p, w_ukv_p, tabs, S)
    mo, lse, gathered = _flash_fwd(qp, kp, vt, S, [late_shards[n] for n in GATHER_LATE] if spread else ())
    for i, n in enumerate(GATHER_LATE if spread else ()):
        w[n] = _from_chips(gathered[i], BIG_SPEC[n][2])
    ry, ro, rprev = _ret_fwd(rq, rk, rv, rg, sm["ret_gn_w"], S)
    mix, h1, hn = _outproj(ro, mo, x, w["w_o"], sm["post_mix_norm"], sm["pre_ffn_norm"], S)
    gate, up, act = _ffn_up(hn, w["w_gate"], w["w_up"], S)
    ff, h2 = _ffn_down(act, w["w_down"], h1, sm["post_ffn_norm"], S)
    dz, dpe, dh2, h2b, loss_vec, d_ple_norm, d_b = _ple_loss(
        p, h2, tgt, w["w_ple_proj"], w["w_ple_gate"], sm["b_ple_gate"], sm["ple_norm"], S)

    gw = {}
    gs = {"ple_norm": d_ple_norm, "b_ple_gate": d_b}
    gw["w_ple_gate"] = _wgrad(h2b, dz, "wgrad_ple_gate", S)
    gw["w_ple_proj"] = _wgrad(p, dpe, "wgrad_ple_proj", S)
    dff, dgate, dup, gs["post_ffn_norm"] = _ffn_down_bwd(dh2, ff, sm["post_ffn_norm"], w["w_down"], gate, up, S)
    gw["w_down"] = _wgrad(act, dff, "wgrad_down", S)
    gw["w_gate"] = _wgrad(hn, dgate, "wgrad_gate", S)
    gw["w_up"] = _wgrad(hn, dup, "wgrad_up", S)
    g4 = [_by_chip(gw.pop(n), *BIG_SPEC[n]) for n in REDUCE_EARLY] if spread else []
    dh1, dmix, dro, dmo, gs["pre_ffn_norm"], gs["post_mix_norm"], got = _ffn_up_bwd(
        dgate, dup, w["w_gate"], w["w_up"], h1, mix, dh2, sm["pre_ffn_norm"], sm["post_mix_norm"], w["w_o"], S, g4)
    sums = [_add_half_rows(g4[i], got[i], c_idx, "rs_add_halves_" + n) for i, n in enumerate(REDUCE_EARLY)] if spread else []
    gw["w_o"] = jnp.concatenate([_wgrad(ro, dmix, "wgrad_o_ret", S), _wgrad(mo, dmix, "wgrad_o_mla", S)], axis=0)

    dmo_t, delta = _attn_delta(mo, dmo, S)
    dqp, dkp, dv, parts = _flash_bwd(qp, kp, kt, v, dmo, dmo_t, lse, delta, S, sums)
    dqh, dkv, dcq, dckv, dkr, gs["mla_q_norm"], gs["mla_kv_norm"] = _mla_up_bwd(
        dqp, dkp, dv, cq, ckv, sm["mla_q_norm"], sm["mla_kv_norm"], w_uq_p, w_ukv_p, tabs, S)
    g_uq_p = _wgrad(cqn, dqh, "wgrad_uq", S)
    g_ukv_p = _wgrad(ckvn, dkv, "wgrad_ukv", S)
    gw["w_uq"] = g_uq_p.reshape(Q_LORA, MLA_HEADS, 128)[:, :, :96].reshape(Q_LORA, 768)
    gw["w_ukv"] = jnp.concatenate(
        [g_ukv_p[:, :1024].reshape(KV_LORA, MLA_HEADS, 128)[:, :, :64], g_ukv_p[:, 1024:].reshape(KV_LORA, MLA_HEADS, 64)],
        axis=2).reshape(KV_LORA, 1024)

    drq, drk, drv, drg, gs["ret_gn_w"] = _ret_bwd(rq, rk, rv, rprev, ry, rg, dro, sm["ret_gn_w"], tabs, S)
    grad_x, dproj, gs["pre_mix_norm"] = _inproj_bwd(drq, drk, drv, drg, dcq, dckv, dkr, w_in_p, dh1, x,
                                                    sm["pre_mix_norm"], S)
    g_in_p = _wgrad(xn, dproj, "wgrad_in", S)
    gw["w_in"] = jnp.concatenate([g_in_p[:, :2688], g_in_p[:, 2752:2784]], axis=1)
    return loss_vec, grad_x, gw, gs, ((sums, parts) if spread else None)


def _my_place():
    x = lax.axis_index("x")
    y = lax.axis_index("y")
    c = lax.axis_index("c")
    return x, y, c


def _other_chips(x, y):
    return [(1 - x, y), (x, 1 - y), (1 - x, 1 - y)]


_ANY = pl.BlockSpec(memory_space=pl.ANY)


def _allreduce_small(vec):
    def body(v_ref, out_ref, slots, send, recv, lsem):
        x, y, c = _my_place()
        me = 4 * x + 2 * y + c
        mine = pltpu.make_async_copy(v_ref, slots.at[me], lsem)
        mine.start()
        cps = []
        for r in range(1, N_DEV):
            px = x ^ (r >> 2)
            py = y ^ ((r >> 1) & 1)
            pc = c ^ (r & 1)
            cps.append(pltpu.make_async_remote_copy(
                src_ref=v_ref, dst_ref=slots.at[me], send_sem=send.at[r - 1], recv_sem=recv.at[r - 1],
                device_id=(px, py, pc), device_id_type=MESH))
        for cp in cps:
            cp.start()
        for cp in cps:
            cp.wait()
        mine.wait()
        acc = slots[0]
        for d in range(1, N_DEV):
            acc = acc + slots[d]
        out_ref[...] = acc
        loss = jnp.sum(acc[9:10, :], axis=1, keepdims=True) * (0.5 / D_MODEL)
        out_ref[9:10, :] = jnp.broadcast_to(loss, (1, PACK_COLS))

    vm = pl.BlockSpec(memory_space=pltpu.VMEM)
    return pl.pallas_call(
        body, name="allreduce_small",
        in_specs=[vm], out_specs=vm, out_shape=_sds((SMALL_ROWS, PACK_COLS), F32),
        scratch_shapes=[pltpu.VMEM((N_DEV, SMALL_ROWS, PACK_COLS), F32), pltpu.SemaphoreType.DMA((N_DEV - 1,)),
                        pltpu.SemaphoreType.DMA((N_DEV - 1,)), pltpu.SemaphoreType.DMA],
    )(vec)


N_BIG = len(BIG)


def _half(c, rows, align):
    h = rows // 2
    return pl.ds(pl.multiple_of(c * h, align), h)


def _gather_out_shapes(shards):
    return [_sds((N_CHIPS,) + tuple(s.shape), BF16) for s in shards]


def _gather_sems(n):
    return [pltpu.SemaphoreType.DMA((n, 3))] * 4 + [pltpu.SemaphoreType.DMA((n,))] * 2


def _gather_phase(phase, ins, outs, sems):
    send1, recv1, send2, recv2, send3, recv3 = sems
    x, y, c = _my_place()
    me = 2 * x + y
    chips = _other_chips(x, y)
    sib = (x, y, 1 - c)
    for t in range(len(ins)):
        rows = ins[t].shape[0]
        half = _half(c, rows, 16)
        other = _half(1 - c, rows, 16)
        def own():
            return pltpu.make_async_remote_copy(
                src_ref=ins[t], dst_ref=outs[t].at[me], send_sem=send3.at[t], recv_sem=recv3.at[t],
                device_id=sib, device_id_type=MESH)

        if phase == 0:
            own().start()
        if phase == 2:
            own().wait()
        for k, (cx, cy) in enumerate(chips):
            src = 2 * cx + cy

            def over_ici(slab):
                return pltpu.make_async_remote_copy(
                    src_ref=ins[t].at[half], dst_ref=outs[t].at[slab, half], send_sem=send1.at[t, k],
                    recv_sem=recv1.at[t, k], device_id=(cx, cy, c), device_id_type=MESH)

            def over_d2d(rows):
                return pltpu.make_async_remote_copy(
                    src_ref=outs[t].at[src, rows], dst_ref=outs[t].at[src, rows], send_sem=send2.at[t, k],
                    recv_sem=recv2.at[t, k], device_id=sib, device_id_type=MESH)

            if phase == 0:
                over_ici(me).start()
            if phase == 1:
                over_ici(src).wait_recv()
                over_d2d(half).start()
            if phase == 2:
                over_d2d(other).wait_recv()
                over_ici(me).wait_send()
                over_d2d(half).wait_send()


def _swap_copies(ins, outs, sems):
    send, recv = sems
    x, y, c = _my_place()
    return [pltpu.make_async_remote_copy(
        src_ref=ins[t].at[:, _half(1 - c, ins[t].shape[1], 8)], dst_ref=outs[t], send_sem=send.at[t],
        recv_sem=recv.at[t], device_id=(x, y, 1 - c), device_id_type=MESH) for t in range(len(ins))]


def _swap_out_shapes(gs):
    return [_sds((N_CHIPS, g.shape[1] // 2, g.shape[2]), F32) for g in gs]


def _swap_sems(n):
    return [pltpu.SemaphoreType.DMA((n,)), pltpu.SemaphoreType.DMA((n,))]


def _swap_half_rows(gs):
    n = len(gs)

    def body(*refs):
        cps = _swap_copies(refs[:n], refs[n:2 * n], refs[2 * n:])
        for cp in cps:
            cp.start()
        for cp in cps:
            cp.wait()

    return pl.pallas_call(
        body, name="rs_swap_halves",
        in_specs=[_ANY] * n, out_specs=[_ANY] * n, out_shape=_swap_out_shapes(gs), scratch_shapes=_swap_sems(n),
    )(*gs)


def _add_half_rows(g, got, c_idx, name):
    _, rows, cols = g.shape
    h = rows // 2

    def body(c_ref, a_ref, b_ref, o_ref):
        o_ref[...] = (a_ref[...] + b_ref[...]).astype(BF16)

    grid_spec = pltpu.PrefetchScalarGridSpec(
        num_scalar_prefetch=1, grid=(N_CHIPS,),
        in_specs=[pl.BlockSpec((None, h, cols), lambda j, c: (j, c[0], 0)),
                  pl.BlockSpec((None, h, cols), lambda j, c: (j, 0, 0))],
        out_specs=pl.BlockSpec((None, h, cols), lambda j, c: (j, 0, 0)),
    )
    return pl.pallas_call(
        body, name=name, grid_spec=grid_spec, out_shape=_sds((N_CHIPS, h, cols), BF16),
        compiler_params=_cp(("parallel",)),
    )(c_idx, g, got)


def _scatter_to_chips(ts):
    n = len(ts)

    def body(*refs):
        cps = _scatter_copies(refs[:n], refs[n:2 * n], refs[2 * n:])
        for cp in cps:
            cp.start()
        for cp in cps:
            cp.wait()

    return pl.pallas_call(
        body, name="rs_scatter_chips",
        in_specs=[_ANY] * n, out_specs=[_ANY] * n, out_shape=_scatter_out_shapes(ts), scratch_shapes=_scatter_sems(n),
    )(*ts)


def _scatter_copies(ins, outs, sems):
    send, recv = sems
    x, y, c = _my_place()
    return [pltpu.make_async_remote_copy(
        src_ref=ins[t].at[2 * cx + cy], dst_ref=outs[t].at[k], send_sem=send.at[t, k], recv_sem=recv.at[t, k],
        device_id=(cx, cy, c), device_id_type=MESH)
        for t in range(len(ins)) for k, (cx, cy) in enumerate(_other_chips(x, y))]


def _scatter_out_shapes(ts):
    return [_sds((3,) + tuple(t.shape[1:]), BF16) for t in ts]


def _scatter_sems(n):
    return [pltpu.SemaphoreType.DMA((n, 3)), pltpu.SemaphoreType.DMA((n, 3))]


def _add_four(mine, parts, place, name):
    _, h, cols = parts.shape

    def body(pl_ref, m_ref, p_ref, o_ref):
        o_ref[...] = ((m_ref[...].astype(F32) + p_ref[0].astype(F32)) + p_ref[1].astype(F32)) + p_ref[2].astype(F32)

    grid_spec = pltpu.PrefetchScalarGridSpec(
        num_scalar_prefetch=1, grid=(1,),
        in_specs=[pl.BlockSpec((None, h, cols), lambda i, pc: (pc[0], 0, 0)),
                  pl.BlockSpec((3, h, cols), lambda i, pc: (0, 0, 0))],
        out_specs=pl.BlockSpec((h, cols), lambda i, pc: (pc[1], 0)),
    )
    return pl.pallas_call(
        body, name=name, grid_spec=grid_spec, out_shape=_sds((2 * h, cols), F32),
        compiler_params=_cp(("arbitrary",)),
    )(place, mine, parts)


def _join_half_rows(rs):
    n = len(rs)

    def body(*refs):
        ins, outs = refs[:n], refs[n:2 * n]
        send, recv = refs[2 * n:]
        x, y, c = _my_place()
        cps = []
        for t in range(n):
            half = _half(c, outs[t].shape[0], 8)
            rc = pltpu.make_async_remote_copy(
                src_ref=ins[t].at[half], dst_ref=outs[t].at[half], send_sem=send.at[t], recv_sem=recv.at[t],
                device_id=(x, y, 1 - c), device_id_type=MESH)
            rc.start()
            cps.append(rc)
        for cp in cps:
            cp.wait()

    return pl.pallas_call(
        body, name="rs_join_halves",
        in_specs=[_ANY] * n, out_specs=[_ANY] * n,
        out_shape=[_sds(r.shape, F32) for r in rs],
        input_output_aliases={i: i for i in range(n)},
        scratch_shapes=[pltpu.SemaphoreType.DMA((n,))] * 2,
    )(*rs)


def _by_chip(full, rows, cols, axis):
    if axis == 0:
        return full.reshape(N_CHIPS, rows // N_CHIPS, cols)
    return full.reshape(rows, N_CHIPS, cols // N_CHIPS).transpose(1, 0, 2)


def _from_chips(parts, axis):
    _, r, c = parts.shape
    if axis == 0:
        return parts.reshape(N_CHIPS * r, c)
    return parts.transpose(1, 0, 2).reshape(r, N_CHIPS * c)


def _adamw(wt, g, m, v, name):
    _, R, C = wt.shape
    tr = max(d for d in range(8, R + 1, 8) if R % d == 0 and (d * C <= 256 * 1024 or d == 8))

    def body(w_ref, g_ref, m_ref, v_ref, d_ref, nm_ref, nv_ref):
        gg = g_ref[...]
        m_new = ADAM_B1 * m_ref[...] + (1.0 - ADAM_B1) * gg
        v_new = ADAM_B2 * v_ref[...] + (1.0 - ADAM_B2) * (gg * gg)
        m_hat = m_new / (1.0 - ADAM_B1 ** ADAM_STEP)
        v_hat = v_new / (1.0 - ADAM_B2 ** ADAM_STEP)
        d_ref[...] = -ADAM_LR * (m_hat / (jnp.sqrt(v_hat) + ADAM_EPS) + ADAM_WD * w_ref[...])
        nm_ref[...] = m_new
        nv_ref[...] = v_new

    spec = pl.BlockSpec((None, tr, C), lambda i: (0, i, 0))
    return pl.pallas_call(
        body, name=name, grid=(R // tr,), in_specs=[spec, pl.BlockSpec((tr, C), lambda i: (i, 0)), spec, spec],
        out_specs=[spec] * 3, out_shape=[_sds((1, R, C), F32)] * 3,
        compiler_params=_cp(("parallel",)),
    )(wt, g, m, v)


def _pack_small(vals, loss_vec=None):
    rows = [jnp.pad(vals[n].reshape(-1), (0, PACK_COLS - sz)) for n, sz in SMALL]
    rows.append(loss_vec.reshape(-1) if loss_vec is not None else jnp.zeros((PACK_COLS,), F32))
    rows += [jnp.zeros((PACK_COLS,), F32)] * (SMALL_ROWS - len(rows))
    return jnp.stack(rows)


def kernel(x, p, positions, pre_mix_norm, w_in, ret_gn_w, mla_q_norm, w_uq, mla_kv_norm, w_ukv, w_o, post_mix_norm, pre_ffn_norm, w_gate, w_up, w_down, post_ffn_norm, w_ple_proj, ple_norm, w_ple_gate, b_ple_gate, loss_target, m_pre_mix_norm, m_w_in, m_ret_gn_w, m_mla_q_norm, m_w_uq, m_mla_kv_norm, m_w_ukv, m_w_o, m_post_mix_norm, m_pre_ffn_norm, m_w_gate, m_w_up, m_w_down, m_post_ffn_norm, m_w_ple_proj, m_ple_norm, m_w_ple_gate, m_b_ple_gate, v_pre_mix_norm, v_w_in, v_ret_gn_w, v_mla_q_norm, v_w_uq, v_mla_kv_norm, v_w_ukv, v_w_o, v_post_mix_norm, v_pre_ffn_norm, v_w_gate, v_w_up, v_w_down, v_post_ffn_norm, v_w_ple_proj, v_ple_norm, v_w_ple_gate, v_b_ple_gate):
    wts = dict(pre_mix_norm=pre_mix_norm, w_in=w_in, ret_gn_w=ret_gn_w, mla_q_norm=mla_q_norm, w_uq=w_uq,
               mla_kv_norm=mla_kv_norm, w_ukv=w_ukv, w_o=w_o, post_mix_norm=post_mix_norm, pre_ffn_norm=pre_ffn_norm,
               w_gate=w_gate, w_up=w_up, w_down=w_down, post_ffn_norm=post_ffn_norm, w_ple_proj=w_ple_proj,
               ple_norm=ple_norm, w_ple_gate=w_ple_gate, b_ple_gate=b_ple_gate)
    mom = dict(pre_mix_norm=m_pre_mix_norm, w_in=m_w_in, ret_gn_w=m_ret_gn_w, mla_q_norm=m_mla_q_norm, w_uq=m_w_uq,
               mla_kv_norm=m_mla_kv_norm, w_ukv=m_w_ukv, w_o=m_w_o, post_mix_norm=m_post_mix_norm,
               pre_ffn_norm=m_pre_ffn_norm, w_gate=m_w_gate, w_up=m_w_up, w_down=m_w_down, post_ffn_norm=m_post_ffn_norm,
               w_ple_proj=m_w_ple_proj, ple_norm=m_ple_norm, w_ple_gate=m_w_ple_gate, b_ple_gate=m_b_ple_gate)
    var = dict(pre_mix_norm=v_pre_mix_norm, w_in=v_w_in, ret_gn_w=v_ret_gn_w, mla_q_norm=v_mla_q_norm, w_uq=v_w_uq,
               mla_kv_norm=v_mla_kv_norm, w_ukv=v_w_ukv, w_o=v_w_o, post_mix_norm=v_post_mix_norm,
               pre_ffn_norm=v_pre_ffn_norm, w_gate=v_w_gate, w_up=v_w_up, w_down=v_w_down, post_ffn_norm=v_post_ffn_norm,
               w_ple_proj=v_w_ple_proj, ple_norm=v_ple_norm, w_ple_gate=v_w_ple_gate, b_ple_gate=v_b_ple_gate)

    S = x.shape[1]
    shard2d = {n: wts[n][0] for n, _, _, _ in BIG}
    small2d = {n: wts[n] for n, _ in SMALL}

    shard_bf = {n: shard2d[n].astype(BF16) for n in shard2d}
    pos_f = positions.astype(F32).reshape(S, 1)
    c_idx = lax.axis_index("c").astype(jnp.int32).reshape(1)
    loss_vec, grad_x, gw, gs, (sums_early, parts_early) = _local_step(
        x[0], p[0, 0], pos_f, loss_target[0], {}, small2d, shard_bf, c_idx)

    g4 = [_by_chip(gw[n], *BIG_SPEC[n]) for n in REDUCE_LAST]
    got = _swap_half_rows(g4)
    sums_last = [_add_half_rows(g4[i], got[i], c_idx, "rs_add_halves_" + n) for i, n in enumerate(REDUCE_LAST)]
    parts_last = _scatter_to_chips(sums_last)
    place = jnp.stack([2 * lax.axis_index("x") + lax.axis_index("y"), lax.axis_index("c")]).astype(jnp.int32)
    names = REDUCE_EARLY + REDUCE_LAST
    reduced = _join_half_rows(
        [_add_four(sm_, pt_, place, "rs_add_chips_" + n)
         for n, sm_, pt_ in zip(names, sums_early + sums_last, list(parts_early) + list(parts_last))])
    g_shard = dict(zip(names, reduced))

    small_sum = _allreduce_small(_pack_small(gs, loss_vec))
    loss = small_sum[9, 0]
    g_small = {n: small_sum[i:i + 1, :sz] for i, (n, sz) in enumerate(SMALL)}

    grads, delta, new_m, new_v = {}, {}, {}, {}
    for n, _, _, _ in BIG:
        if n in COLUMN_MAJOR:
            turn = lambda a: jnp.swapaxes(a, 1, 2)
            g_t = g_shard[n].T
            d, nm, nv = _adamw(turn(wts[n]), g_t, turn(mom[n]), turn(var[n]), "adamw_" + n)
            grads[n], delta[n], new_m[n], new_v[n] = turn(g_t[None]), turn(d), turn(nm), turn(nv)
        else:
            delta[n], new_m[n], new_v[n] = _adamw(wts[n], g_shard[n], mom[n], var[n], "adamw_" + n)
            grads[n] = g_shard[n][None]
    d, nm, nv = _adamw(_pack_small(small2d)[None], small_sum, _pack_small(mom)[None], _pack_small(var)[None],
                       "adamw_small")
    for i, (n, sz) in enumerate(SMALL):
        grads[n] = g_small[n]
        delta[n], new_m[n], new_v[n] = d[0, i:i + 1, :sz], nm[0, i:i + 1, :sz], nv[0, i:i + 1, :sz]

    return (loss, grad_x[None], *[grads[n] for n in ALL_W], *[delta[n] for n in ALL_W],
            *[new_m[n] for n in ALL_W], *[new_v[n] for n in ALL_W])
```

```python
import functools
import math

import jax
import jax.numpy as jnp
import numpy as np
from jax import lax
from jax.experimental import pallas as pl
from jax.experimental.pallas import tpu as pltpu

F32 = jnp.float32
BF16 = jnp.bfloat16
MESH = pl.DeviceIdType.MESH

D_MODEL = 1024
D_FF = 2816
PLE_DIM = 256
RET_HEADS = 4
RET_DIM = 128
RET_WIDTH = 512
RET_CHUNK = 256
RET_GROUP = 4
MLA_HEADS = 8
MLA_NOPE = 64
MLA_ROPE = 32
MLA_V = 64
Q_LORA = 384
KV_LORA = 256
IN_COLS = 2720
IN_COLS_P = 2816
ROPE_BASE = 10000.0
EPS = 1e-6
SCALE_MLA = 1.0 / math.sqrt(MLA_NOPE + MLA_ROPE)
SCALE_RET = RET_DIM ** -0.5
NEG = -1e30

ADAM_LR = 0.001
ADAM_B1 = 0.9
ADAM_B2 = 0.999
ADAM_EPS = 1e-08
ADAM_WD = 0.01
ADAM_STEP = 10

N_CHIPS = 4
N_DEV = 8
VMEM_MB = 56

BIG = (
    ("w_in", 1024, 2720, 1),
    ("w_uq", 384, 768, 1),
    ("w_ukv", 256, 1024, 1),
    ("w_o", 1024, 1024, 0),
    ("w_gate", 1024, 2816, 1),
    ("w_up", 1024, 2816, 1),
    ("w_down", 2816, 1024, 0),
    ("w_ple_proj", 256, 1024, 1),
    ("w_ple_gate", 1024, 1024, 0),
)
SMALL = (
    ("pre_mix_norm", 1024),
    ("ret_gn_w", 512),
    ("mla_q_norm", 384),
    ("mla_kv_norm", 256),
    ("post_mix_norm", 1024),
    ("pre_ffn_norm", 1024),
    ("post_ffn_norm", 1024),
    ("ple_norm", 1024),
    ("b_ple_gate", 1024),
)
ALL_W = ("pre_mix_norm", "w_in", "ret_gn_w", "mla_q_norm", "w_uq", "mla_kv_norm", "w_ukv", "w_o", "post_mix_norm",
         "pre_ffn_norm", "w_gate", "w_up", "w_down", "post_ffn_norm", "w_ple_proj", "ple_norm", "w_ple_gate", "b_ple_gate")
PACK_COLS = 1024
SMALL_ROWS = 16


def _cp(sem=None, mb=VMEM_MB, **kw):
    return pltpu.CompilerParams(dimension_semantics=sem, vmem_limit_bytes=mb * 1024 * 1024, **kw)


def _bf(x):
    return x.astype(BF16)


def _dot(a, b):
    return jnp.dot(_bf(a), _bf(b), preferred_element_type=F32)


def _dot_nt(a, b):
    return lax.dot_general(_bf(a), _bf(b), (((1,), (1,)), ((), ())), preferred_element_type=F32)


def _dot_tn(a, b):
    return lax.dot_general(_bf(a), _bf(b), (((0,), (0,)), ((), ())), preferred_element_type=F32)


def _sig(x):
    return 1.0 / (1.0 + jnp.exp(-x))


def _rms(x, g):
    r = lax.rsqrt(jnp.mean(x * x, axis=-1, keepdims=True) + EPS)
    return x * r * g


def _rms_bwd(dy, x, g):
    r = lax.rsqrt(jnp.mean(x * x, axis=-1, keepdims=True) + EPS)
    xh = x * r
    dxh = dy * g
    dx = r * (dxh - xh * jnp.mean(dxh * xh, axis=-1, keepdims=True))
    return dx, dy * xh


def _colsum(x):
    return jnp.sum(x, axis=0, keepdims=True)


def _rope_ret(x, cr, sr):
    return x * cr + pltpu.roll(x, 64, 1) * sr


def _unrope_ret(dy, cr, sr):
    return dy * cr + pltpu.roll(dy * sr, 64, 1)


def _rope_mla(x, cm, sa, sb):
    return x * cm + pltpu.roll(x, 112, 1) * sa + pltpu.roll(x, 16, 1) * sb


def _unrope_mla(dy, cm, sa, sb):
    return dy * cm + pltpu.roll(dy * sa, 16, 1) + pltpu.roll(dy * sb, 112, 1)


def _rows(tm, w, col=0):
    return pl.BlockSpec((tm, w), lambda i: (i, col))


def _full(*shape):
    return pl.BlockSpec(shape, lambda i: (0,) * len(shape), pipeline_mode=pl.Buffered(1))


def _acc(*shape):
    return pl.BlockSpec(shape, lambda i: (0,) * len(shape))


def _sds(shape, dtype):
    return jax.ShapeDtypeStruct(shape, dtype)


def _rope_tables(pos_f, S, shards=()):
    tm = min(512, S)
    n = len(shards)
    steps = S // tm
    inv_r = (1.0 / (np.float32(ROPE_BASE) ** (np.arange(64, dtype=np.float32) / np.float32(64)))).astype(np.float32)
    inv_m16 = (1.0 / (np.float32(ROPE_BASE) ** (np.arange(16, dtype=np.float32) / np.float32(16)))).astype(np.float32)
    inv_r = np.concatenate([inv_r, inv_r])[None, :]
    inv_m = np.zeros((1, 128), np.float32)
    inv_m[0, 64:80] = inv_m16
    inv_m[0, 80:96] = inv_m16

    def body(pos_ref, invr_ref, invm_ref, *rest):
        w_ins, (cr_ref, sr_ref, cm_ref, sa_ref, sb_ref) = rest[:n], rest[n:n + 5]
        w_outs, sems = rest[n + 5:2 * n + 5], rest[2 * n + 5:]
        i = pl.program_id(0)
        if n:
            @pl.when(i == 0)
            def _():
                _gather_phase(0, w_ins, w_outs, sems)

            @pl.when(i == steps // 2)
            def _():
                _gather_phase(1, w_ins, w_outs, sems)

        pos = pos_ref[...]
        lane = lax.broadcasted_iota(jnp.int32, (tm, 128), 1)
        ar = pos * invr_ref[...]
        s = jnp.sin(ar)
        cr_ref[...] = jnp.cos(ar)
        sr_ref[...] = jnp.where(lane < 64, -s, s)
        am = pos * invm_ref[...]
        c2 = jnp.cos(am)
        s2 = jnp.sin(am)
        cm_ref[...] = jnp.where(lane < 64, 1.0, jnp.where(lane < 96, c2, 0.0))
        sa_ref[...] = jnp.where((lane >= 64) & (lane < 80), -s2, 0.0)
        sb_ref[...] = jnp.where((lane >= 80) & (lane < 96), s2, 0.0)

        if n:
            @pl.when(i == steps - 1)
            def _():
                _gather_phase(2, w_ins, w_outs, sems)

    outs = pl.pallas_call(
        body, name="rope_tables", grid=(steps,),
        in_specs=[_rows(tm, 1), _full(1, 128), _full(1, 128)] + [_ANY] * n,
        out_specs=[_rows(tm, 128)] * 5 + [_ANY] * n,
        out_shape=[_sds((S, 128), F32)] * 5 + _gather_out_shapes(shards),
        scratch_shapes=_gather_sems(n) if n else [],
        compiler_params=_cp(("arbitrary",)),
    )(pos_f, jnp.asarray(inv_r), jnp.asarray(inv_m), *shards)
    return outs[:5], outs[5:]


def _inproj(x, g, w_in, tabs, S):
    tm = min(512, S)

    def body(x_ref, g_ref, w_ref, cr_ref, sr_ref, cm_ref, sa_ref, sb_ref,
             xn_ref, rq_ref, rk_ref, rv_ref, rg_ref, cq_ref, ckv_ref, kr_ref):
        xb = _rms(x_ref[...], g_ref[...]).astype(BF16)
        xn_ref[...] = xb
        cr = cr_ref[...]
        sr = sr_ref[...]
        q = jnp.dot(xb, w_ref[:, 0:512], preferred_element_type=F32)
        k = jnp.dot(xb, w_ref[:, 512:1024], preferred_element_type=F32)
        for h in range(RET_HEADS):
            sl = slice(h * 128, (h + 1) * 128)
            rq_ref[:, sl] = _rope_ret(q[:, sl], cr, sr).astype(BF16)
            rk_ref[:, sl] = (_rope_ret(k[:, sl], cr, sr) * SCALE_RET).astype(BF16)
        rv_ref[...] = jnp.dot(xb, w_ref[:, 1024:1536], preferred_element_type=F32).astype(BF16)
        rg_ref[...] = jnp.dot(xb, w_ref[:, 1536:2048], preferred_element_type=F32)
        cq_ref[...] = jnp.dot(xb, w_ref[:, 2048:2432], preferred_element_type=F32)
        ckv_ref[...] = jnp.dot(xb, w_ref[:, 2432:2688], preferred_element_type=F32)
        kr = jnp.dot(xb, w_ref[:, 2688:2816], preferred_element_type=F32)
        kr_ref[...] = _rope_mla(kr, cm_ref[...], sa_ref[...], sb_ref[...])

    return pl.pallas_call(
        body, name="inproj", grid=(S // tm,),
        in_specs=[_rows(tm, D_MODEL), _full(1, D_MODEL), _full(D_MODEL, IN_COLS_P)] + [_rows(tm, 128)] * 5,
        out_specs=[_rows(tm, D_MODEL)] + [_rows(tm, 512)] * 4 + [_rows(tm, Q_LORA), _rows(tm, KV_LORA), _rows(tm, 128)],
        out_shape=[_sds((S, D_MODEL), BF16)] + [_sds((S, 512), BF16)] * 3
        + [_sds((S, 512), F32), _sds((S, Q_LORA), F32), _sds((S, KV_LORA), F32), _sds((S, 128), F32)],
        compiler_params=_cp(("parallel",)),
    )(x, g, w_in, *tabs)


def _mla_up(cq, ckv, kr, gq, gkv, w_uq, w_ukv, tabs, S):
    tm = min(512, S)

    def body(cq_ref, ckv_ref, kr_ref, gq_ref, gkv_ref, wuq_ref, wukv_ref, cm_ref, sa_ref, sb_ref,
             cqn_ref, ckvn_ref, qp_ref, kp_ref, v_ref, kt_ref, vt_ref):
        cm = cm_ref[...]
        sa = sa_ref[...]
        sb = sb_ref[...]
        cqn = _rms(cq_ref[...], gq_ref[...]).astype(BF16)
        cqn_ref[...] = cqn
        ckvn = _rms(ckv_ref[...], gkv_ref[...]).astype(BF16)
        ckvn_ref[...] = ckvn
        qh = jnp.dot(cqn, wuq_ref[...], preferred_element_type=F32)
        kv = jnp.dot(ckvn, wukv_ref[...], preferred_element_type=F32)
        kr_blk = kr_ref[...]
        for h in range(MLA_HEADS):
            sl = slice(h * 128, (h + 1) * 128)
            qp_ref[:, sl] = (_rope_mla(qh[:, sl], cm, sa, sb) * SCALE_MLA).astype(BF16)
            kh = kv[:, sl] + kr_blk
            kp_ref[:, sl] = kh.astype(BF16)
            kt_ref[sl, :] = kh.T.astype(BF16)
        for h in range(MLA_HEADS // 2):
            vh = kv[:, 1024 + h * 128:1024 + (h + 1) * 128]
            v_ref[:, h * 128:(h + 1) * 128] = vh.astype(BF16)
            vt_ref[h * 128:(h + 1) * 128, :] = vh.T.astype(BF16)

    cols = lambda r: pl.BlockSpec((r, tm), lambda i: (0, i))
    return pl.pallas_call(
        body, name="mla_up", grid=(S // tm,),
        in_specs=[_rows(tm, Q_LORA), _rows(tm, KV_LORA), _rows(tm, 128), _full(1, Q_LORA), _full(1, KV_LORA),
                  _full(Q_LORA, 1024), _full(KV_LORA, 1536)] + [_rows(tm, 128)] * 3,
        out_specs=[_rows(tm, Q_LORA), _rows(tm, KV_LORA), _rows(tm, 1024), _rows(tm, 1024), _rows(tm, 512),
                   cols(1024), cols(512)],
        out_shape=[_sds((S, Q_LORA), BF16), _sds((S, KV_LORA), BF16), _sds((S, 1024), BF16), _sds((S, 1024), BF16),
                   _sds((S, 512), BF16), _sds((1024, S), BF16), _sds((512, S), BF16)],
        compiler_params=_cp(("parallel",)),
    )(cq, ckv, kr, gq, gkv, w_uq, w_ukv, *tabs[2:])


def _tri_pairs(nq, k_major):
    if k_major:
        pairs = [(qb, kb) for kb in range(nq) for qb in range(kb, nq)]
    else:
        pairs = [(qb, kb) for qb in range(nq) for kb in range(qb + 1)]
    qb_of = np.array([p[0] for p in pairs], np.int32)
    kb_of = np.array([p[1] for p in pairs], np.int32)
    return jnp.asarray(qb_of), jnp.asarray(kb_of), len(pairs)


ATT_ROWS = 32
FWD_HEADS = 8
BWD_HEADS = 4


def _causal_keep(r0, rows, tq):
    key = r0 + lax.broadcasted_iota(jnp.int32, (rows, tq), 0)
    qry = lax.broadcasted_iota(jnp.int32, (rows, tq), 1)
    return key <= qry


def _flash_fwd(qp, kp, vt, S, shards=()):
    tq = min(512, S)
    nq = S // tq
    RB = ATT_ROWS
    NH = FWD_HEADS
    qb_of, kb_of, T = _tri_pairs(nq, k_major=False)
    n = len(shards)
    steps = (MLA_HEADS // NH) * T

    def body(qb_ref, kb_ref, q_ref, k_ref, vt_ref, *rest):
        w_ins, (o_ref, lse_ref), w_outs = rest[:n], rest[n:n + 2], rest[n + 2:2 * n + 2]
        m_sc, l_sc, acc_sc, s_sc, p_sc = rest[2 * n + 2:2 * n + 7]
        sems = rest[2 * n + 7:]
        t = pl.program_id(1)
        qb = qb_ref[t]
        kb = kb_ref[t]
        lin = pl.program_id(0) * T + t

        if n:
            @pl.when(lin == 0)
            def _():
                _gather_phase(0, w_ins, w_outs, sems)

            @pl.when(lin == steps // 2)
            def _():
                _gather_phase(1, w_ins, w_outs, sems)

        @pl.when(kb == 0)
        def _():
            m_sc[...] = jnp.full(m_sc.shape, NEG, F32)
            l_sc[...] = jnp.zeros(l_sc.shape, F32)
            acc_sc[...] = jnp.zeros(acc_sc.shape, F32)

        def scores(a):
            sl = slice(a * 128, (a + 1) * 128)
            s_sc[a] = _dot_nt(k_ref[:, sl], q_ref[:, sl])

        def step(masked):
            for a in range(NH):
                scores(a)
            for a in range(NH):
                mx = [jnp.full((8, tq), NEG, F32) for _ in range(RB // 8)]
                for r in range(0, tq, RB):
                    sc = s_sc[a, r:r + RB, :]
                    if masked:
                        sc = jnp.where(_causal_keep(r, RB, tq), sc, NEG)
                        s_sc[a, r:r + RB, :] = sc
                    for i in range(RB // 8):
                        mx[i] = jnp.maximum(mx[i], sc[i * 8:(i + 1) * 8, :])
                mx8 = functools.reduce(jnp.maximum, mx)
                m_prev = m_sc[a]
                m_new = jnp.maximum(m_prev, jnp.max(mx8, axis=0, keepdims=True))
                al = jnp.exp(m_prev - m_new)
                m_sc[a] = m_new
                ls = [jnp.zeros((8, tq), F32) for _ in range(RB // 8)]
                for r in range(0, tq, RB):
                    p = jnp.exp(s_sc[a, r:r + RB, :] - m_new)
                    for i in range(RB // 8):
                        ls[i] = ls[i] + p[i * 8:(i + 1) * 8, :]
                    p_sc[a, r:r + RB, :] = p.astype(BF16)
                l_sc[a] = al * l_sc[a] + jnp.sum(functools.reduce(jnp.add, ls), axis=0, keepdims=True)
                pair = slice((a // 2) * 128, (a // 2 + 1) * 128)
                pv = jnp.dot(vt_ref[pair, :], p_sc[a], preferred_element_type=F32)
                rs = slice(a * 64, (a + 1) * 64)
                own = slice((a % 2) * 64, (a % 2 + 1) * 64)
                acc_sc[rs, :] = acc_sc[rs, :] * al + pv[own, :]

        @pl.when(kb < qb)
        def _():
            step(False)

        @pl.when(kb == qb)
        def _():
            step(True)
            for a in range(NH):
                rs = slice(a * 64, (a + 1) * 64)
                acc_sc[rs, :] = acc_sc[rs, :] / l_sc[a]
                lse_ref[a:a + 1, :] = m_sc[a] + jnp.log(l_sc[a])
            o_ref[...] = acc_sc[...].T.astype(BF16)

        if n:
            @pl.when(lin == steps - 1)
            def _():
                _gather_phase(2, w_ins, w_outs, sems)

    grid_spec = pltpu.PrefetchScalarGridSpec(
        num_scalar_prefetch=2, grid=(MLA_HEADS // NH, T),
        in_specs=[pl.BlockSpec((tq, 128 * NH), lambda j, t, qb, kb: (qb[t], j)),
                  pl.BlockSpec((tq, 128 * NH), lambda j, t, qb, kb: (kb[t], j)),
                  pl.BlockSpec((64 * NH, tq), lambda j, t, qb, kb: (j, kb[t]))] + [_ANY] * n,
        out_specs=[pl.BlockSpec((tq, 64 * NH), lambda j, t, qb, kb: (qb[t], j)),
                   pl.BlockSpec((None, NH, tq), lambda j, t, qb, kb: (j, 0, qb[t]))] + [_ANY] * n,
        scratch_shapes=[pltpu.VMEM((NH, 1, tq), F32), pltpu.VMEM((NH, 1, tq), F32), pltpu.VMEM((64 * NH, tq), F32),
                        pltpu.VMEM((NH, tq, tq), F32), pltpu.VMEM((NH, tq, tq), BF16)] + (_gather_sems(n) if n else []),
    )
    out, lse, *gathered = pl.pallas_call(
        body, name="flash_fwd", grid_spec=grid_spec,
        out_shape=[_sds((S, 512), BF16), _sds((MLA_HEADS // NH, NH, S), F32)] + _gather_out_shapes(shards),
        compiler_params=_cp(("arbitrary", "arbitrary")),
    )(qb_of, kb_of, qp, kp, vt, *shards)
    return out, lse.reshape(MLA_HEADS // 2, 2, S), gathered


def _decay_table():
    log_g = np.log(1.0 - 2.0 ** (-5.0 - np.arange(RET_HEADS, dtype=np.float32))).astype(np.float32)
    return jnp.asarray(np.broadcast_to(log_g[:, None, None], (RET_HEADS, 8, 128)).copy())


def _decay_terms(lg_ref):
    C = RET_CHUNK
    lg = lg_ref[0:1, :]
    row = lax.broadcasted_iota(jnp.int32, (C, C), 0)
    col = lax.broadcasted_iota(jnp.int32, (C, C), 1)
    diff = (row - col).astype(F32)
    dmat = jnp.where(diff >= 0, jnp.exp(jnp.maximum(diff, 0.0) * jnp.tile(lg, (1, C // 128))), 0.0)
    j = lax.broadcasted_iota(jnp.int32, (C, 1), 0).astype(F32)
    lg1 = lg[:, 0:1]
    zeta = jnp.exp((C - 1 - j) * lg1)
    xi = jnp.exp((j + 1.0) * lg1)
    g_chunk = jnp.exp(C * lg1)
    return dmat, zeta, xi, g_chunk


def _ret_fwd(rq, rk, rv, rg, gn_w, S):
    C = RET_CHUNK
    N = S // C
    G = min(RET_GROUP, N)
    NB = N // G

    def body(lg_ref, q_ref, k_ref, v_ref, rg_ref, w_ref, ry_ref, ro_ref, rprev_ref, r_sc):
        @pl.when(pl.program_id(1) == 0)
        def _():
            r_sc[...] = jnp.zeros(r_sc.shape, F32)

        dmat, zeta, xi, g_chunk = _decay_terms(lg_ref)
        w = w_ref[...]
        r = r_sc[...]
        for i in range(G):
            rows = slice(i * C, (i + 1) * C)
            q = q_ref[rows, :]
            k = k_ref[rows, :]
            v = v_ref[rows, :]
            r_prev = r.astype(BF16)
            rprev_ref[i] = r_prev
            sc = _dot_nt(q, k) * dmat
            ry = _dot(sc, v) + jnp.dot(q, r_prev, preferred_element_type=F32) * xi
            ry_ref[rows, :] = ry
            r = g_chunk * r + _dot_tn(k, zeta * v.astype(F32))
            mu = jnp.mean(ry, axis=-1, keepdims=True)
            yc = ry - mu
            yh = yc * lax.rsqrt(jnp.mean(yc * yc, axis=-1, keepdims=True) + EPS)
            g = rg_ref[rows, :]
            ro_ref[rows, :] = (g * _sig(g) * (yh * w)).astype(BF16)
        r_sc[...] = r

    blk = pl.BlockSpec((G * C, 128), lambda h, n: (n, h))
    return pl.pallas_call(
        body, name="ret_fwd", grid=(RET_HEADS, NB),
        in_specs=[pl.BlockSpec((None, 8, 128), lambda h, n: (h, 0, 0)), blk, blk, blk, blk,
                  pl.BlockSpec((1, 128), lambda h, n: (0, h))],
        out_specs=[blk, blk, pl.BlockSpec((G, 128, 128), lambda h, n: (h * NB + n, 0, 0))],
        out_shape=[_sds((S, 512), F32), _sds((S, 512), BF16), _sds((RET_HEADS * N, 128, 128), BF16)],
        scratch_shapes=[pltpu.VMEM((128, 128), F32)],
        compiler_params=_cp(("parallel", "arbitrary")),
    )(_decay_table(), rq, rk, rv, rg, gn_w)


def _outproj(ro, mo, x, w_o, g_post, g_pre, S):
    tm = min(512, S)

    def body(ro_ref, mo_ref, x_ref, wo_ref, g1_ref, g2_ref, mix_ref, h1_ref, hn_ref):
        mix = (jnp.dot(ro_ref[...], wo_ref[0:512, :], preferred_element_type=F32)
               + jnp.dot(mo_ref[...], wo_ref[512:1024, :], preferred_element_type=F32))
        mix_ref[...] = mix.astype(BF16)
        h1 = x_ref[...] + _rms(mix, g1_ref[...])
        h1_ref[...] = h1
        hn_ref[...] = _rms(h1, g2_ref[...]).astype(BF16)

    return pl.pallas_call(
        body, name="outproj", grid=(S // tm,),
        in_specs=[_rows(tm, 512), _rows(tm, 512), _rows(tm, D_MODEL), _full(D_MODEL, D_MODEL), _full(1, D_MODEL),
                  _full(1, D_MODEL)],
        out_specs=[_rows(tm, D_MODEL)] * 3,
        out_shape=[_sds((S, D_MODEL), BF16), _sds((S, D_MODEL), F32), _sds((S, D_MODEL), BF16)],
        compiler_params=_cp(("parallel",)),
    )(ro, mo, x, w_o, g_post, g_pre)


def _ffn_up(hn, w_gate, w_up, S):
    tm = min(512, S)
    tn = D_FF // 2

    def body(hn_ref, wg_ref, wu_ref, gate_ref, up_ref, act_ref):
        hn_b = hn_ref[...]
        g = jnp.dot(hn_b, wg_ref[...], preferred_element_type=F32)
        u = jnp.dot(hn_b, wu_ref[...], preferred_element_type=F32)
        gate_ref[...] = g.astype(BF16)
        up_ref[...] = u.astype(BF16)
        act_ref[...] = (g * _sig(g) * u).astype(BF16)

    wspec = pl.BlockSpec((D_MODEL, tn), lambda j, i: (0, j))
    ospec = pl.BlockSpec((tm, tn), lambda j, i: (i, j))
    return pl.pallas_call(
        body, name="ffn_up", grid=(2, S // tm),
        in_specs=[pl.BlockSpec((tm, D_MODEL), lambda j, i: (i, 0)), wspec, wspec],
        out_specs=[ospec] * 3, out_shape=[_sds((S, D_FF), BF16)] * 3,
        compiler_params=_cp(("parallel", "parallel")),
    )(hn, w_gate, w_up)


def _ffn_down(act, w_down, h1, g, S):
    tm = min(512, S)

    def body(act_ref, wd_ref, h1_ref, g_ref, ff_ref, h2_ref):
        ff = jnp.dot(act_ref[...], wd_ref[...], preferred_element_type=F32)
        ff_ref[...] = ff.astype(BF16)
        h2_ref[...] = h1_ref[...] + _rms(ff, g_ref[...])

    return pl.pallas_call(
        body, name="ffn_down", grid=(S // tm,),
        in_specs=[_rows(tm, D_FF), _full(D_FF, D_MODEL), _rows(tm, D_MODEL), _full(1, D_MODEL)],
        out_specs=[_rows(tm, D_MODEL)] * 2, out_shape=[_sds((S, D_MODEL), BF16), _sds((S, D_MODEL), F32)],
        compiler_params=_cp(("parallel",)),
    )(act, w_down, h1, g)


def _ple_loss(p, h2, tgt, w_pp, w_pg, b_pg, g_ple, S):
    tm = min(512, S)

    def body(p_ref, h2_ref, t_ref, wp_ref, wg_ref, b_ref, gp_ref,
             dz_ref, dpe_ref, dh2_ref, h2b_ref, loss_ref, dgp_ref, db_ref):
        @pl.when(pl.program_id(0) == 0)
        def _():
            loss_ref[...] = jnp.zeros(loss_ref.shape, F32)
            dgp_ref[...] = jnp.zeros(dgp_ref.shape, F32)
            db_ref[...] = jnp.zeros(db_ref.shape, F32)

        gp = gp_ref[...]
        pe = _dot(p_ref[...], wp_ref[...])
        r = lax.rsqrt(jnp.mean(pe * pe, axis=-1, keepdims=True) + EPS)
        peh = pe * r
        e = peh * gp
        h2 = h2_ref[...]
        h2b = h2.astype(BF16)
        h2b_ref[...] = h2b
        gt = _sig(jnp.dot(h2b, wg_ref[...], preferred_element_type=F32) + b_ref[...])
        diff = h2 + e * gt - t_ref[...]
        loss_ref[...] += _colsum(diff * diff)
        dh3 = diff * (1.0 / D_MODEL)
        de = dh3 * gt
        dz = dh3 * e * gt * (1.0 - gt)
        db_ref[...] += _colsum(dz)
        dgp_ref[...] += _colsum(de * peh)
        dpeh = de * gp
        dpe = r * (dpeh - peh * jnp.mean(dpeh * peh, axis=-1, keepdims=True))
        dzb = dz.astype(BF16)
        dz_ref[...] = dzb
        dpe_ref[...] = dpe.astype(BF16)
        dh2_ref[...] = dh3 + _dot_nt(dzb, wg_ref[...])

    return pl.pallas_call(
        body, name="ple_loss", grid=(S // tm,),
        in_specs=[_rows(tm, PLE_DIM), _rows(tm, D_MODEL), _rows(tm, D_MODEL), _full(PLE_DIM, D_MODEL),
                  _full(D_MODEL, D_MODEL), _full(1, D_MODEL), _full(1, D_MODEL)],
        out_specs=[_rows(tm, D_MODEL)] * 4 + [_acc(1, D_MODEL)] * 3,
        out_shape=[_sds((S, D_MODEL), BF16), _sds((S, D_MODEL), BF16), _sds((S, D_MODEL), F32), _sds((S, D_MODEL), BF16)]
        + [_sds((1, D_MODEL), F32)] * 3,
        compiler_params=_cp(("arbitrary",)),
    )(p, h2, tgt, w_pp, w_pg, b_pg, g_ple)


def _wgrad(a, b, name, S):
    M = a.shape[1]
    N = b.shape[1]
    ts = min(2048, S)
    nsplit = 2 if M * N >= 2 * 1024 * 1024 else 1
    tn = N // nsplit

    def body(a_ref, b_ref, o_ref):
        @pl.when(pl.program_id(1) == 0)
        def _():
            o_ref[...] = jnp.zeros(o_ref.shape, F32)

        o_ref[...] += _dot_tn(a_ref[...], b_ref[...])

    return pl.pallas_call(
        body, name=name, grid=(nsplit, S // ts),
        in_specs=[pl.BlockSpec((ts, M), lambda j, s: (s, 0)), pl.BlockSpec((ts, tn), lambda j, s: (s, j))],
        out_specs=pl.BlockSpec((M, tn), lambda j, s: (0, j)), out_shape=_sds((M, N), F32),
        compiler_params=_cp(("parallel", "arbitrary")),
    )(a, b)


def _ffn_down_bwd(dh2, ff, g, w_down, gate, up, S):
    tm = min(512, S)
    tn = D_FF // 2

    def body(dh2_ref, ff_ref, g_ref, wd_ref, gate_ref, up_ref, dff_ref, dgate_ref, dup_ref, dg_ref):
        @pl.when(pl.program_id(0) == 0)
        def _():
            dg_ref[...] = jnp.zeros(dg_ref.shape, F32)

        dff, ga = _rms_bwd(dh2_ref[...], ff_ref[...].astype(F32), g_ref[...])
        dg_ref[...] += _colsum(ga)
        dffb = dff.astype(BF16)
        dff_ref[...] = dffb
        for seg in range(2):
            sl = slice(seg * tn, (seg + 1) * tn)
            dact = _dot_nt(dffb, wd_ref[sl, :])
            gt = gate_ref[:, sl].astype(F32)
            u = up_ref[:, sl].astype(F32)
            s = _sig(gt)
            dgate_ref[:, sl] = (dact * u * (s * (1.0 + gt * (1.0 - s)))).astype(BF16)
            dup_ref[:, sl] = (dact * (gt * s)).astype(BF16)

    return pl.pallas_call(
        body, name="ffn_down_bwd", grid=(S // tm,),
        in_specs=[_rows(tm, D_MODEL), _rows(tm, D_MODEL), _full(1, D_MODEL), _full(D_FF, D_MODEL), _rows(tm, D_FF),
                  _rows(tm, D_FF)],
        out_specs=[_rows(tm, D_MODEL), _rows(tm, D_FF), _rows(tm, D_FF), _acc(1, D_MODEL)],
        out_shape=[_sds((S, D_MODEL), BF16), _sds((S, D_FF), BF16), _sds((S, D_FF), BF16), _sds((1, D_MODEL), F32)],
        compiler_params=_cp(("arbitrary",)),
    )(dh2, ff, g, w_down, gate, up)


def _ffn_up_bwd(dgate, dup, w_gate, w_up, h1, mix, dh2, g_pre, g_post, w_o, S, grads=()):
    tm = min(512, S)
    n = len(grads)
    last = S // tm - 1

    def body(dgate_ref, dup_ref, wg_ref, wu_ref, h1_ref, mix_ref, dh2_ref, g2_ref, g1_ref, wo_ref, *rest):
        g_ins = rest[:n]
        dh1_ref, dmix_ref, dro_ref, dmo_ref, dg2_ref, dg1_ref = rest[n:n + 6]
        g_outs, sems = rest[n + 6:2 * n + 6], rest[2 * n + 6:]

        @pl.when(pl.program_id(0) == 0)
        def _():
            dg2_ref[...] = jnp.zeros(dg2_ref.shape, F32)
            dg1_ref[...] = jnp.zeros(dg1_ref.shape, F32)
            for cp in (_swap_copies(g_ins, g_outs, sems) if n else []):
                cp.start()

        dhn = _dot_nt(dgate_ref[...], wg_ref[...]) + _dot_nt(dup_ref[...], wu_ref[...])
        d1, ga = _rms_bwd(dhn, h1_ref[...], g2_ref[...])
        dg2_ref[...] += _colsum(ga)
        dh1 = dh2_ref[...] + d1
        dh1_ref[...] = dh1
        dmix, gb = _rms_bwd(dh1, mix_ref[...].astype(F32), g1_ref[...])
        dg1_ref[...] += _colsum(gb)
        dmixb = dmix.astype(BF16)
        dmix_ref[...] = dmixb
        dcat = _dot_nt(dmixb, wo_ref[...])
        dro_ref[...] = dcat[:, 0:512].astype(BF16)
        dmo_ref[...] = dcat[:, 512:1024].astype(BF16)

        if n:
            @pl.when(pl.program_id(0) == last)
            def _():
                for cp in _swap_copies(g_ins, g_outs, sems):
                    cp.wait()

    dh1, dmix, dro, dmo, dg2, dg1, *got = pl.pallas_call(
        body, name="ffn_up_bwd", grid=(S // tm,),
        in_specs=[_rows(tm, D_FF), _rows(tm, D_FF), _full(D_MODEL, D_FF), _full(D_MODEL, D_FF), _rows(tm, D_MODEL),
                  _rows(tm, D_MODEL), _rows(tm, D_MODEL), _full(1, D_MODEL), _full(1, D_MODEL), _full(D_MODEL, D_MODEL)]
        + [_ANY] * n,
        out_specs=[_rows(tm, D_MODEL), _rows(tm, D_MODEL), _rows(tm, 512), _rows(tm, 512), _acc(1, D_MODEL),
                   _acc(1, D_MODEL)] + [_ANY] * n,
        out_shape=[_sds((S, D_MODEL), F32), _sds((S, D_MODEL), BF16), _sds((S, 512), BF16), _sds((S, 512), BF16),
                   _sds((1, D_MODEL), F32), _sds((1, D_MODEL), F32)] + _swap_out_shapes(grads),
        scratch_shapes=_swap_sems(n) if n else [],
        compiler_params=_cp(("arbitrary",)),
    )(dgate, dup, w_gate, w_up, h1, mix, dh2, g_pre, g_post, w_o, *grads)
    return dh1, dmix, dro, dmo, dg2, dg1, got


def _attn_delta(o, do, S):
    tm = min(512, S)

    def body(o_ref, do_ref, dot_ref, d_ref):
        do = do_ref[...].astype(F32)
        prod_t = (o_ref[...].astype(F32) * do).T
        dot_ref[...] = do.T.astype(BF16)
        for h in range(MLA_HEADS):
            d_ref[h // 2, (h % 2):(h % 2) + 1, :] = jnp.sum(prod_t[h * 64:(h + 1) * 64, :], axis=0, keepdims=True)

    return pl.pallas_call(
        body, name="attn_delta", grid=(S // tm,),
        in_specs=[_rows(tm, 512), _rows(tm, 512)],
        out_specs=[pl.BlockSpec((512, tm), lambda i: (0, i)), pl.BlockSpec((MLA_HEADS // 2, 2, tm), lambda i: (0, 0, i))],
        out_shape=[_sds((512, S), BF16), _sds((MLA_HEADS // 2, 2, S), F32)],
        compiler_params=_cp(("parallel",)),
    )(o, do)


def _flash_bwd(qp, kp, kt, v, do, dot, lse, delta, S, sums=()):
    tq = min(512, S)
    nq = S // tq
    RB = ATT_ROWS
    NH = BWD_HEADS
    qb_of, kb_of, T = _tri_pairs(nq, k_major=True)
    n = len(sums)
    steps = (MLA_HEADS // NH) * T

    def body(qb_ref, kb_ref, q_ref, k_ref, kt_ref, v_ref, do_ref, dot_ref, lse_ref, dl_ref, *rest):
        g_ins, (dq_ref, dk_ref, dv_ref), g_outs = rest[:n], rest[n:n + 3], rest[n + 3:2 * n + 3]
        dk_sc, dv_sc, s_sc, dp_sc, p_sc, ds_sc = rest[2 * n + 3:2 * n + 9]
        sems = rest[2 * n + 9:]
        t = pl.program_id(1)
        qb = qb_ref[t]
        kb = kb_ref[t]
        lin = pl.program_id(0) * T + t

        if n:
            @pl.when(lin == 0)
            def _():
                for cp in _scatter_copies(g_ins, g_outs, sems):
                    cp.start()

        @pl.when(t == 0)
        def _():
            dq_ref[...] = jnp.zeros(dq_ref.shape, F32)

        @pl.when(qb == kb)
        def _():
            dk_sc[...] = jnp.zeros(dk_sc.shape, F32)
            dv_sc[...] = jnp.zeros(dv_sc.shape, F32)

        lane = lax.broadcasted_iota(jnp.int32, (tq, 64 * NH), 1)

        def step(masked):
            vv = v_ref[...]
            do_all = do_ref[...]
            mine = [(lane >= a * 64) & (lane < (a + 1) * 64) for a in range(NH)]
            for a in range(NH):
                sl = slice(a * 128, (a + 1) * 128)
                s_sc[a] = _dot_nt(k_ref[:, sl], q_ref[:, sl])
                dp_sc[a] = jnp.dot(jnp.where(mine[a], vv, jnp.zeros_like(vv)), dot_ref[...],
                                   preferred_element_type=F32)
            for a in range(NH):
                sl = slice(a * 128, (a + 1) * 128)
                lse = lse_ref[a:a + 1, :]
                dl = dl_ref[a:a + 1, :]
                for r in range(0, tq, RB):
                    sc = s_sc[a, r:r + RB, :]
                    if masked:
                        sc = jnp.where(_causal_keep(r, RB, tq), sc, NEG)
                    p = jnp.exp(sc - lse)
                    p_sc[a, r:r + RB, :] = p.astype(BF16)
                    ds_sc[a, r:r + RB, :] = (p * (dp_sc[a, r:r + RB, :] - dl)).astype(BF16)
                ds = ds_sc[a]
                dv_sc[...] += jnp.dot(p_sc[a], jnp.where(mine[a], do_all, jnp.zeros_like(do_all)),
                                      preferred_element_type=F32)
                dk_sc[:, sl] += jnp.dot(ds, q_ref[:, sl], preferred_element_type=F32)
                dq_ref[qb, sl, :] += jnp.dot(kt_ref[sl, :], ds, preferred_element_type=F32)

        @pl.when(qb > kb)
        def _():
            step(False)

        @pl.when(qb == kb)
        def _():
            step(True)

        @pl.when(qb == nq - 1)
        def _():
            dk_ref[...] = dk_sc[...].astype(BF16)
            dv_ref[...] = dv_sc[...].astype(BF16)

        if n:
            @pl.when(lin == steps - 1)
            def _():
                for cp in _scatter_copies(g_ins, g_outs, sems):
                    cp.wait()

    grid_spec = pltpu.PrefetchScalarGridSpec(
        num_scalar_prefetch=2, grid=(MLA_HEADS // NH, T),
        in_specs=[pl.BlockSpec((tq, 128 * NH), lambda j, t, qb, kb: (qb[t], j)),
                  pl.BlockSpec((tq, 128 * NH), lambda j, t, qb, kb: (kb[t], j)),
                  pl.BlockSpec((128 * NH, tq), lambda j, t, qb, kb: (j, kb[t])),
                  pl.BlockSpec((tq, 64 * NH), lambda j, t, qb, kb: (kb[t], j)),
                  pl.BlockSpec((tq, 64 * NH), lambda j, t, qb, kb: (qb[t], j)),
                  pl.BlockSpec((64 * NH, tq), lambda j, t, qb, kb: (j, qb[t])),
                  pl.BlockSpec((None, NH, tq), lambda j, t, qb, kb: (j, 0, qb[t])),
                  pl.BlockSpec((None, NH, tq), lambda j, t, qb, kb: (j, 0, qb[t]))] + [_ANY] * n,
        out_specs=[pl.BlockSpec((nq, 128 * NH, tq), lambda j, t, qb, kb: (0, j, 0), pipeline_mode=pl.Buffered(1)),
                   pl.BlockSpec((tq, 128 * NH), lambda j, t, qb, kb: (kb[t], j)),
                   pl.BlockSpec((tq, 64 * NH), lambda j, t, qb, kb: (kb[t], j))] + [_ANY] * n,
        scratch_shapes=[pltpu.VMEM((tq, 128 * NH), F32), pltpu.VMEM((tq, 64 * NH), F32), pltpu.VMEM((NH, tq, tq), F32),
                        pltpu.VMEM((NH, tq, tq), F32), pltpu.VMEM((NH, tq, tq), BF16), pltpu.VMEM((NH, tq, tq), BF16)]
        + (_scatter_sems(n) if n else []),
    )
    dq, dk, dv, *parts = pl.pallas_call(
        body, name="flash_bwd", grid_spec=grid_spec,
        out_shape=[_sds((nq, 1024, tq), F32), _sds((S, 1024), BF16), _sds((S, 512), BF16)] + _scatter_out_shapes(sums),
        compiler_params=_cp(("arbitrary", "arbitrary")),
    )(qb_of, kb_of, qp, kp, kt, v, do, dot, lse.reshape(MLA_HEADS // NH, NH, S), delta.reshape(MLA_HEADS // NH, NH, S),
      *sums)
    return dq, dk, dv, parts


def _mla_up_bwd(dqp, dkp, dv, cq, ckv, gq, gkv, w_uq, w_ukv, tabs, S):
    tm = min(512, S)

    def body(dq_ref, dk_ref, dv_ref, cq_ref, ckv_ref, gq_ref, gkv_ref, wuq_ref, wukv_ref, cm_ref, sa_ref, sb_ref,
             dqh_ref, dkv_ref, dcq_ref, dckv_ref, dkr_ref, dgq_ref, dgkv_ref):
        @pl.when(pl.program_id(0) == 0)
        def _():
            dgq_ref[...] = jnp.zeros(dgq_ref.shape, F32)
            dgkv_ref[...] = jnp.zeros(dgkv_ref.shape, F32)

        cm = cm_ref[...]
        sa = sa_ref[...]
        sb = sb_ref[...]
        lane = lax.broadcasted_iota(jnp.int32, (tm, 128), 1)
        dkr_r = jnp.zeros((tm, 128), F32)
        for h in range(MLA_HEADS):
            sl = slice(h * 128, (h + 1) * 128)
            dqh_ref[:, sl] = (_unrope_mla(dq_ref[sl, :].T, cm, sa, sb) * SCALE_MLA).astype(BF16)
            gk = dk_ref[:, sl]
            dkr_r = dkr_r + gk.astype(F32)
            dkv_ref[:, sl] = gk
        dkr_r = jnp.where((lane >= 64) & (lane < 96), dkr_r, 0.0)
        dkr_ref[...] = _unrope_mla(dkr_r, cm, sa, sb).astype(BF16)
        dkv_ref[:, 1024:1536] = dv_ref[...]
        dcq, ga = _rms_bwd(_dot_nt(dqh_ref[...], wuq_ref[...]), cq_ref[...], gq_ref[...])
        dcq_ref[...] = dcq.astype(BF16)
        dgq_ref[...] += _colsum(ga)
        dckv, gb = _rms_bwd(_dot_nt(dkv_ref[...], wukv_ref[...]), ckv_ref[...], gkv_ref[...])
        dckv_ref[...] = dckv.astype(BF16)
        dgkv_ref[...] += _colsum(gb)

    per_q = dqp.shape[2] // tm
    return pl.pallas_call(
        body, name="mla_up_bwd", grid=(S // tm,),
        in_specs=[pl.BlockSpec((None, 1024, tm), lambda i: (i // per_q, 0, i % per_q)),
                  _rows(tm, 1024), _rows(tm, 512), _rows(tm, Q_LORA), _rows(tm, KV_LORA),
                  _full(1, Q_LORA), _full(1, KV_LORA), _full(Q_LORA, 1024), _full(KV_LORA, 1536)] + [_rows(tm, 128)] * 3,
        out_specs=[_rows(tm, 1024), _rows(tm, 1536), _rows(tm, Q_LORA), _rows(tm, KV_LORA), _rows(tm, 128),
                   _acc(1, Q_LORA), _acc(1, KV_LORA)],
        out_shape=[_sds((S, 1024), BF16), _sds((S, 1536), BF16), _sds((S, Q_LORA), BF16), _sds((S, KV_LORA), BF16),
                   _sds((S, 128), BF16), _sds((1, Q_LORA), F32), _sds((1, KV_LORA), F32)],
        compiler_params=_cp(("arbitrary",)),
    )(dqp, dkp, dv, cq, ckv, gq, gkv, w_uq, w_ukv, *tabs[2:])


def _ret_bwd(rq, rk, rv, rprev, ry, rg, dro, gn_w, tabs, S):
    C = RET_CHUNK
    N = S // C
    G = min(RET_GROUP, N)
    NB = N // G

    def body(lg_ref, q_ref, k_ref, v_ref, rp_ref, ry_ref, rg_ref, dro_ref, w_ref, cr_ref, sr_ref,
             drq_ref, drk_ref, drv_ref, drg_ref, dw_ref, g_sc):
        @pl.when(pl.program_id(1) == 0)
        def _():
            g_sc[...] = jnp.zeros(g_sc.shape, F32)
            dw_ref[...] = jnp.zeros(dw_ref.shape, F32)

        dmat, zeta, xi, g_chunk = _decay_terms(lg_ref)
        w = w_ref[...]
        gacc = g_sc[...]
        dw = jnp.zeros((1, 128), F32)
        for i in reversed(range(G)):
            rows = slice(i * C, (i + 1) * C)
            ry = ry_ref[rows, :]
            mu = jnp.mean(ry, axis=-1, keepdims=True)
            yc = ry - mu
            rstd = lax.rsqrt(jnp.mean(yc * yc, axis=-1, keepdims=True) + EPS)
            yh = yc * rstd
            g = rg_ref[rows, :]
            s = _sig(g)
            dout = dro_ref[rows, :].astype(F32)
            drg_ref[rows, :] = (dout * (yh * w) * (s * (1.0 + g * (1.0 - s)))).astype(BF16)
            dgn = dout * (g * s)
            dw = dw + _colsum(dgn * yh)
            dyh = dgn * w
            dry = rstd * (dyh - jnp.mean(dyh, axis=-1, keepdims=True) - yh * jnp.mean(dyh * yh, axis=-1, keepdims=True))
            do = dry.astype(BF16)

            q = q_ref[rows, :]
            k = k_ref[rows, :]
            v = v_ref[rows, :]
            gfut = gacc.astype(BF16)
            sc = (_dot_nt(q, k) * dmat).astype(BF16)
            dsc = (_dot_nt(do, v) * dmat).astype(BF16)
            dq = jnp.dot(dsc, k, preferred_element_type=F32) + _dot_nt(do, rp_ref[i]) * xi
            dk = _dot_tn(dsc, q) + _dot_nt(v, gfut) * zeta
            dv = _dot_tn(sc, do) + jnp.dot(k, gfut, preferred_element_type=F32) * zeta
            gacc = g_chunk * gacc + _dot_tn(q, xi * dry)
            cr = cr_ref[rows, :]
            sr = sr_ref[rows, :]
            drq_ref[rows, :] = _unrope_ret(dq, cr, sr).astype(BF16)
            drk_ref[rows, :] = _unrope_ret(dk * SCALE_RET, cr, sr).astype(BF16)
            drv_ref[rows, :] = dv.astype(BF16)
        g_sc[...] = gacc
        dw_ref[...] += dw

    blk = pl.BlockSpec((G * C, 128), lambda h, n: (NB - 1 - n, h))
    tab = pl.BlockSpec((G * C, 128), lambda h, n: (NB - 1 - n, 0))
    return pl.pallas_call(
        body, name="ret_bwd", grid=(RET_HEADS, NB),
        in_specs=[pl.BlockSpec((None, 8, 128), lambda h, n: (h, 0, 0)), blk, blk, blk,
                  pl.BlockSpec((G, 128, 128), lambda h, n: (h * NB + NB - 1 - n, 0, 0)), blk, blk, blk,
                  pl.BlockSpec((1, 128), lambda h, n: (0, h)), tab, tab],
        out_specs=[blk, blk, blk, blk, pl.BlockSpec((1, 128), lambda h, n: (0, h))],
        out_shape=[_sds((S, 512), BF16)] * 4 + [_sds((1, 512), F32)],
        scratch_shapes=[pltpu.VMEM((128, 128), F32)],
        compiler_params=_cp(("parallel", "arbitrary")),
    )(_decay_table(), rq, rk, rv, rprev, ry, rg, dro, gn_w, tabs[0], tabs[1])


def _inproj_bwd(drq, drk, drv, drg, dcq, dckv, dkr, w_in, dh1, x, g, S):
    tm = min(512, S)

    def body(drq_ref, drk_ref, drv_ref, drg_ref, dcq_ref, dckv_ref, dkr_ref, w_ref, dh1_ref, x_ref, g_ref,
             gx_ref, dproj_ref, dg_ref):
        @pl.when(pl.program_id(0) == 0)
        def _():
            dg_ref[...] = jnp.zeros(dg_ref.shape, F32)

        dproj_ref[:, 0:512] = drq_ref[...]
        dproj_ref[:, 512:1024] = drk_ref[...]
        dproj_ref[:, 1024:1536] = drv_ref[...]
        dproj_ref[:, 1536:2048] = drg_ref[...]
        dproj_ref[:, 2048:2432] = dcq_ref[...]
        dproj_ref[:, 2432:2688] = dckv_ref[...]
        dproj_ref[:, 2688:2816] = dkr_ref[...]
        dx, ga = _rms_bwd(_dot_nt(dproj_ref[...], w_ref[...]), x_ref[...], g_ref[...])
        gx_ref[...] = dh1_ref[...] + dx
        dg_ref[...] += _colsum(ga)

    return pl.pallas_call(
        body, name="inproj_bwd", grid=(S // tm,),
        in_specs=[_rows(tm, 512)] * 4 + [_rows(tm, Q_LORA), _rows(tm, KV_LORA), _rows(tm, 128),
                                         _full(D_MODEL, IN_COLS_P), _rows(tm, D_MODEL), _rows(tm, D_MODEL),
                                         _full(1, D_MODEL)],
        out_specs=[_rows(tm, D_MODEL), _rows(tm, IN_COLS_P), _acc(1, D_MODEL)],
        out_shape=[_sds((S, D_MODEL), F32), _sds((S, IN_COLS_P), BF16), _sds((1, D_MODEL), F32)],
        compiler_params=_cp(("arbitrary",)),
    )(drq, drk, drv, drg, dcq, dckv, dkr, w_in, dh1, x, g)


def _pad_weights(w):
    w_in = w["w_in"]
    z = lambda r, c: jnp.zeros((r, c), BF16)
    w_in_p = jnp.concatenate([w_in[:, :2688], z(1024, 64), w_in[:, 2688:2720], z(1024, 32)], axis=1)
    w_uq_p = jnp.pad(w["w_uq"].reshape(Q_LORA, MLA_HEADS, 96), ((0, 0), (0, 0), (0, 32))).reshape(Q_LORA, 1024)
    ukv = w["w_ukv"].reshape(KV_LORA, MLA_HEADS, 128)
    k_part = jnp.pad(ukv[:, :, :64], ((0, 0), (0, 0), (0, 64))).reshape(KV_LORA, 1024)
    w_ukv_p = jnp.concatenate([k_part, ukv[:, :, 64:].reshape(KV_LORA, 512)], axis=1)
    return w_in_p, w_uq_p, w_ukv_p


BIG_SPEC = {n: (r, c, ax) for n, r, c, ax in BIG}
COLUMN_MAJOR = ("w_in", "w_uq", "w_gate", "w_up")
GRAD_TRANSPOSED = ("w_gate", "w_up")
GATHER_FIRST = ("w_in", "w_uq", "w_ukv")
GATHER_LATE = tuple(n for n, _, _, _ in BIG if n not in GATHER_FIRST)
REDUCE_EARLY = ("w_ple_gate", "w_ple_proj", "w_down", "w_gate", "w_up")
REDUCE_LAST = tuple(n for n, _, _, _ in BIG if n not in REDUCE_EARLY)


def _local_step(x, p, pos_f, tgt, w, sm, late_shards=None, c_idx=None):
    S = x.shape[0]
    spread = late_shards is not None
    w = dict(w)
    tabs, first = _rope_tables(pos_f, S, [late_shards[n] for n in GATHER_FIRST] if spread else ())
    for i, n in enumerate(GATHER_FIRST if spread else ()):
        w[n] = _from_chips(first[i], BIG_SPEC[n][2])
    w_in_p, w_uq_p, w_ukv_p = _pad_weights(w)

    xn, rq, rk, rv, rg, cq, ckv, kr = _inproj(x, sm["pre_mix_norm"], w_in_p, tabs, S)
    cqn, ckvn, qp, kp, v, kt, vt = _mla_up(cq, ckv, kr, sm["mla_q_norm"], sm["mla_kv_norm"], w_uq_p, w_ukv_p, tabs, S)
    mo, lse, gathered = _flash_fwd(qp, kp, vt, S, [late_shards[n] for n in GATHER_LATE] if spread else ())
    for i, n in enumerate(GATHER_LATE if spread else ()):
        w[n] = _from_chips(gathered[i], BIG_SPEC[n][2])
    ry, ro, rprev = _ret_fwd(rq, rk, rv, rg, sm["ret_gn_w"], S)
    mix, h1, hn = _outproj(ro, mo, x, w["w_o"], sm["post_mix_norm"], sm["pre_ffn_norm"], S)
    gate, up, act = _ffn_up(hn, w["w_gate"], w["w_up"], S)
    ff, h2 = _ffn_down(act, w["w_down"], h1, sm["post_ffn_norm"], S)
    dz, dpe, dh2, h2b, loss_vec, d_ple_norm, d_b = _ple_loss(
        p, h2, tgt, w["w_ple_proj"], w["w_ple_gate"], sm["b_ple_gate"], sm["ple_norm"], S)

    gw = {}
    gs = {"ple_norm": d_ple_norm, "b_ple_gate": d_b}
    gw["w_ple_gate"] = _wgrad(h2b, dz, "wgrad_ple_gate", S)
    gw["w_ple_proj"] = _wgrad(p, dpe, "wgrad_ple_proj", S)
    dff, dgate, dup, gs["post_ffn_norm"] = _ffn_down_bwd(dh2, ff, sm["post_ffn_norm"], w["w_down"], gate, up, S)
    gw["w_down"] = _wgrad(act, dff, "wgrad_down", S)
    if spread:
        gw["w_gate"] = _wgrad(dgate, hn, "wgrad_gate", S)
        gw["w_up"] = _wgrad(dup, hn, "wgrad_up", S)
    else:
        gw["w_gate"] = _wgrad(hn, dgate, "wgrad_gate", S)
        gw["w_up"] = _wgrad(hn, dup, "wgrad_up", S)
    g4 = [_by_chip(gw.pop(n), *((D_FF, D_MODEL, 0) if n in GRAD_TRANSPOSED else BIG_SPEC[n]))
          for n in REDUCE_EARLY] if spread else []
    dh1, dmix, dro, dmo, gs["pre_ffn_norm"], gs["post_mix_norm"], got = _ffn_up_bwd(
        dgate, dup, w["w_gate"], w["w_up"], h1, mix, dh2, sm["pre_ffn_norm"], sm["post_mix_norm"], w["w_o"], S, g4)
    sums = [_add_half_rows(g4[i], got[i], c_idx, "rs_add_halves_" + n) for i, n in enumerate(REDUCE_EARLY)] if spread else []
    gw["w_o"] = jnp.concatenate([_wgrad(ro, dmix, "wgrad_o_ret", S), _wgrad(mo, dmix, "wgrad_o_mla", S)], axis=0)

    dmo_t, delta = _attn_delta(mo, dmo, S)
    dqp, dkp, dv, parts = _flash_bwd(qp, kp, kt, v, dmo, dmo_t, lse, delta, S, sums)
    dqh, dkv, dcq, dckv, dkr, gs["mla_q_norm"], gs["mla_kv_norm"] = _mla_up_bwd(
        dqp, dkp, dv, cq, ckv, sm["mla_q_norm"], sm["mla_kv_norm"], w_uq_p, w_ukv_p, tabs, S)
    g_uq_p = _wgrad(cqn, dqh, "wgrad_uq", S)
    g_ukv_p = _wgrad(ckvn, dkv, "wgrad_ukv", S)
    gw["w_uq"] = g_uq_p.reshape(Q_LORA, MLA_HEADS, 128)[:, :, :96].reshape(Q_LORA, 768)
    gw["w_ukv"] = jnp.concatenate(
        [g_ukv_p[:, :1024].reshape(KV_LORA, MLA_HEADS, 128)[:, :, :64], g_ukv_p[:, 1024:].reshape(KV_LORA, MLA_HEADS, 64)],
        axis=2).reshape(KV_LORA, 1024)

    drq, drk, drv, drg, gs["ret_gn_w"] = _ret_bwd(rq, rk, rv, rprev, ry, rg, dro, sm["ret_gn_w"], tabs, S)
    grad_x, dproj, gs["pre_mix_norm"] = _inproj_bwd(drq, drk, drv, drg, dcq, dckv, dkr, w_in_p, dh1, x,
                                                    sm["pre_mix_norm"], S)
    g_in_p = _wgrad(xn, dproj, "wgrad_in", S)
    gw["w_in"] = jnp.concatenate([g_in_p[:, :2688], g_in_p[:, 2752:2784]], axis=1)
    return loss_vec, grad_x, gw, gs, ((sums, parts) if spread else None)


def _my_place():
    x = lax.axis_index("x")
    y = lax.axis_index("y")
    c = lax.axis_index("c")
    return x, y, c


def _other_chips(x, y):
    return [(1 - x, y), (x, 1 - y), (1 - x, 1 - y)]


_ANY = pl.BlockSpec(memory_space=pl.ANY)


def _small_copies(v_ref, slots, sems):
    send, recv, lsem = sems
    x, y, c = _my_place()
    me = 4 * x + 2 * y + c
    cps = [pltpu.make_async_copy(v_ref, slots.at[me], lsem)]
    for r in range(1, N_DEV):
        peer = (x ^ (r >> 2), y ^ ((r >> 1) & 1), c ^ (r & 1))
        cps.append(pltpu.make_async_remote_copy(
            src_ref=v_ref, dst_ref=slots.at[me], send_sem=send.at[r - 1], recv_sem=recv.at[r - 1],
            device_id=peer, device_id_type=MESH))
    return cps


def _small_sum(slots, out_ref):
    acc = slots[0]
    for d in range(1, N_DEV):
        acc = acc + slots[d]
    out_ref[...] = acc
    loss = jnp.sum(acc[9:10, :], axis=1, keepdims=True) * (0.5 / D_MODEL)
    out_ref[9:10, :] = jnp.broadcast_to(loss, (1, PACK_COLS))


def _small_scratch():
    return [pltpu.VMEM((N_DEV, SMALL_ROWS, PACK_COLS), F32), pltpu.SemaphoreType.DMA((N_DEV - 1,)),
            pltpu.SemaphoreType.DMA((N_DEV - 1,)), pltpu.SemaphoreType.DMA]


N_BIG = len(BIG)


def _half(c, rows, align):
    h = rows // 2
    return pl.ds(pl.multiple_of(c * h, align), h)


def _gather_out_shapes(shards):
    return [_sds((N_CHIPS,) + tuple(s.shape), BF16) for s in shards]


def _gather_sems(n):
    return [pltpu.SemaphoreType.DMA((n, 3))] * 4 + [pltpu.SemaphoreType.DMA((n,))] * 2


def _gather_phase(phase, ins, outs, sems):
    send1, recv1, send2, recv2, send3, recv3 = sems
    x, y, c = _my_place()
    me = 2 * x + y
    chips = _other_chips(x, y)
    sib = (x, y, 1 - c)
    for t in range(len(ins)):
        rows = ins[t].shape[0]
        half = _half(c, rows, 16)
        other = _half(1 - c, rows, 16)
        def own():
            return pltpu.make_async_remote_copy(
                src_ref=ins[t], dst_ref=outs[t].at[me], send_sem=send3.at[t], recv_sem=recv3.at[t],
                device_id=sib, device_id_type=MESH)

        if phase == 0:
            own().start()
        if phase == 2:
            own().wait()
        for k, (cx, cy) in enumerate(chips):
            src = 2 * cx + cy

            def over_ici(slab):
                return pltpu.make_async_remote_copy(
                    src_ref=ins[t].at[half], dst_ref=outs[t].at[slab, half], send_sem=send1.at[t, k],
                    recv_sem=recv1.at[t, k], device_id=(cx, cy, c), device_id_type=MESH)

            def over_d2d(rows):
                return pltpu.make_async_remote_copy(
                    src_ref=outs[t].at[src, rows], dst_ref=outs[t].at[src, rows], send_sem=send2.at[t, k],
                    recv_sem=recv2.at[t, k], device_id=sib, device_id_type=MESH)

            if phase == 0:
                over_ici(me).start()
            if phase == 1:
                over_ici(src).wait_recv()
                over_d2d(half).start()
            if phase == 2:
                over_d2d(other).wait_recv()
                over_ici(me).wait_send()
                over_d2d(half).wait_send()


def _swap_copies(ins, outs, sems):
    send, recv = sems
    x, y, c = _my_place()
    return [pltpu.make_async_remote_copy(
        src_ref=ins[t].at[:, _half(1 - c, ins[t].shape[1], 8)], dst_ref=outs[t], send_sem=send.at[t],
        recv_sem=recv.at[t], device_id=(x, y, 1 - c), device_id_type=MESH) for t in range(len(ins))]


def _swap_out_shapes(gs):
    return [_sds((N_CHIPS, g.shape[1] // 2, g.shape[2]), F32) for g in gs]


def _swap_sems(n):
    return [pltpu.SemaphoreType.DMA((n,)), pltpu.SemaphoreType.DMA((n,))]


def _swap_half_rows(gs):
    n = len(gs)

    def body(*refs):
        cps = _swap_copies(refs[:n], refs[n:2 * n], refs[2 * n:])
        for cp in cps:
            cp.start()
        for cp in cps:
            cp.wait()

    return pl.pallas_call(
        body, name="rs_swap_halves",
        in_specs=[_ANY] * n, out_specs=[_ANY] * n, out_shape=_swap_out_shapes(gs), scratch_shapes=_swap_sems(n),
    )(*gs)


def _add_half_rows(g, got, c_idx, name):
    _, rows, cols = g.shape
    h = rows // 2

    def body(c_ref, a_ref, b_ref, o_ref):
        o_ref[...] = (a_ref[...] + b_ref[...]).astype(BF16)

    grid_spec = pltpu.PrefetchScalarGridSpec(
        num_scalar_prefetch=1, grid=(N_CHIPS,),
        in_specs=[pl.BlockSpec((None, h, cols), lambda j, c: (j, c[0], 0)),
                  pl.BlockSpec((None, h, cols), lambda j, c: (j, 0, 0))],
        out_specs=pl.BlockSpec((None, h, cols), lambda j, c: (j, 0, 0)),
    )
    return pl.pallas_call(
        body, name=name, grid_spec=grid_spec, out_shape=_sds((N_CHIPS, h, cols), BF16),
        compiler_params=_cp(("parallel",)),
    )(c_idx, g, got)


def _scatter_to_chips(ts, vec):
    n = len(ts)

    def body(*refs):
        ins, v_ref, outs, small_ref = refs[:n], refs[n], refs[n + 1:2 * n + 1], refs[2 * n + 1]
        slots, small_sems, sems = refs[2 * n + 2], refs[2 * n + 3:2 * n + 6], refs[2 * n + 6:]
        small = _small_copies(v_ref, slots, small_sems)
        cps = _scatter_copies(ins, outs, sems)
        for cp in small + cps:
            cp.start()
        for cp in small:
            cp.wait()
        _small_sum(slots, small_ref)
        for cp in cps:
            cp.wait()

    vm = pl.BlockSpec(memory_space=pltpu.VMEM)
    *parts, small_sum = pl.pallas_call(
        body, name="rs_scatter_chips",
        in_specs=[_ANY] * n + [vm], out_specs=[_ANY] * n + [vm],
        out_shape=_scatter_out_shapes(ts) + [_sds((SMALL_ROWS, PACK_COLS), F32)],
        scratch_shapes=_small_scratch() + _scatter_sems(n),
    )(*ts, vec)
    return parts, small_sum


def _scatter_copies(ins, outs, sems):
    send, recv = sems
    x, y, c = _my_place()
    return [pltpu.make_async_remote_copy(
        src_ref=ins[t].at[2 * cx + cy], dst_ref=outs[t].at[k], send_sem=send.at[t, k], recv_sem=recv.at[t, k],
        device_id=(cx, cy, c), device_id_type=MESH)
        for t in range(len(ins)) for k, (cx, cy) in enumerate(_other_chips(x, y))]


def _scatter_out_shapes(ts):
    return [_sds((3,) + tuple(t.shape[1:]), BF16) for t in ts]


def _scatter_sems(n):
    return [pltpu.SemaphoreType.DMA((n, 3)), pltpu.SemaphoreType.DMA((n, 3))]


def _add_four(mine, parts, place, name):
    _, h, cols = parts.shape

    def body(pl_ref, m_ref, p_ref, o_ref):
        o_ref[...] = ((m_ref[...].astype(F32) + p_ref[0].astype(F32)) + p_ref[1].astype(F32)) + p_ref[2].astype(F32)

    grid_spec = pltpu.PrefetchScalarGridSpec(
        num_scalar_prefetch=1, grid=(1,),
        in_specs=[pl.BlockSpec((None, h, cols), lambda i, pc: (pc[0], 0, 0)),
                  pl.BlockSpec((3, h, cols), lambda i, pc: (0, 0, 0))],
        out_specs=pl.BlockSpec((h, cols), lambda i, pc: (pc[1], 0)),
    )
    return pl.pallas_call(
        body, name=name, grid_spec=grid_spec, out_shape=_sds((2 * h, cols), F32),
        compiler_params=_cp(("arbitrary",)),
    )(place, mine, parts)


def _join_half_rows(rs):
    n = len(rs)

    def body(*refs):
        ins, outs = refs[:n], refs[n:2 * n]
        send, recv = refs[2 * n:]
        x, y, c = _my_place()
        cps = []
        for t in range(n):
            half = _half(c, outs[t].shape[0], 8)
            rc = pltpu.make_async_remote_copy(
                src_ref=ins[t].at[half], dst_ref=outs[t].at[half], send_sem=send.at[t], recv_sem=recv.at[t],
                device_id=(x, y, 1 - c), device_id_type=MESH)
            rc.start()
            cps.append(rc)
        for cp in cps:
            cp.wait()

    return pl.pallas_call(
        body, name="rs_join_halves",
        in_specs=[_ANY] * n, out_specs=[_ANY] * n,
        out_shape=[_sds(r.shape, F32) for r in rs],
        input_output_aliases={i: i for i in range(n)},
        scratch_shapes=[pltpu.SemaphoreType.DMA((n,))] * 2,
    )(*rs)


def _by_chip(full, rows, cols, axis):
    if axis == 0:
        return full.reshape(N_CHIPS, rows // N_CHIPS, cols)
    return full.reshape(rows, N_CHIPS, cols // N_CHIPS).transpose(1, 0, 2)


def _from_chips(parts, axis):
    _, r, c = parts.shape
    if axis == 0:
        return parts.reshape(N_CHIPS * r, c)
    return parts.transpose(1, 0, 2).reshape(r, N_CHIPS * c)


def _adamw(wt, g, m, v, name):
    _, R, C = wt.shape
    tr = max(d for d in range(8, R + 1, 8) if R % d == 0 and (d * C <= 256 * 1024 or d == 8))

    def body(w_ref, g_ref, m_ref, v_ref, d_ref, nm_ref, nv_ref):
        gg = g_ref[...]
        m_new = ADAM_B1 * m_ref[...] + (1.0 - ADAM_B1) * gg
        v_new = ADAM_B2 * v_ref[...] + (1.0 - ADAM_B2) * (gg * gg)
        m_hat = m_new / (1.0 - ADAM_B1 ** ADAM_STEP)
        v_hat = v_new / (1.0 - ADAM_B2 ** ADAM_STEP)
        d_ref[...] = -ADAM_LR * (m_hat / (jnp.sqrt(v_hat) + ADAM_EPS) + ADAM_WD * w_ref[...])
        nm_ref[...] = m_new
        nv_ref[...] = v_new

    spec = pl.BlockSpec((None, tr, C), lambda i: (0, i, 0))
    return pl.pallas_call(
        body, name=name, grid=(R // tr,), in_specs=[spec, pl.BlockSpec((tr, C), lambda i: (i, 0)), spec, spec],
        out_specs=[spec] * 3, out_shape=[_sds((1, R, C), F32)] * 3,
        compiler_params=_cp(("parallel",)),
    )(wt, g, m, v)


def _pack_small(vals, loss_vec=None):
    rows = [jnp.pad(vals[n].reshape(-1), (0, PACK_COLS - sz)) for n, sz in SMALL]
    rows.append(loss_vec.reshape(-1) if loss_vec is not None else jnp.zeros((PACK_COLS,), F32))
    rows += [jnp.zeros((PACK_COLS,), F32)] * (SMALL_ROWS - len(rows))
    return jnp.stack(rows)


def kernel(x, p, positions, pre_mix_norm, w_in, ret_gn_w, mla_q_norm, w_uq, mla_kv_norm, w_ukv, w_o, post_mix_norm, pre_ffn_norm, w_gate, w_up, w_down, post_ffn_norm, w_ple_proj, ple_norm, w_ple_gate, b_ple_gate, loss_target, m_pre_mix_norm, m_w_in, m_ret_gn_w, m_mla_q_norm, m_w_uq, m_mla_kv_norm, m_w_ukv, m_w_o, m_post_mix_norm, m_pre_ffn_norm, m_w_gate, m_w_up, m_w_down, m_post_ffn_norm, m_w_ple_proj, m_ple_norm, m_w_ple_gate, m_b_ple_gate, v_pre_mix_norm, v_w_in, v_ret_gn_w, v_mla_q_norm, v_w_uq, v_mla_kv_norm, v_w_ukv, v_w_o, v_post_mix_norm, v_pre_ffn_norm, v_w_gate, v_w_up, v_w_down, v_post_ffn_norm, v_w_ple_proj, v_ple_norm, v_w_ple_gate, v_b_ple_gate):
    wts = dict(pre_mix_norm=pre_mix_norm, w_in=w_in, ret_gn_w=ret_gn_w, mla_q_norm=mla_q_norm, w_uq=w_uq,
               mla_kv_norm=mla_kv_norm, w_ukv=w_ukv, w_o=w_o, post_mix_norm=post_mix_norm, pre_ffn_norm=pre_ffn_norm,
               w_gate=w_gate, w_up=w_up, w_down=w_down, post_ffn_norm=post_ffn_norm, w_ple_proj=w_ple_proj,
               ple_norm=ple_norm, w_ple_gate=w_ple_gate, b_ple_gate=b_ple_gate)
    mom = dict(pre_mix_norm=m_pre_mix_norm, w_in=m_w_in, ret_gn_w=m_ret_gn_w, mla_q_norm=m_mla_q_norm, w_uq=m_w_uq,
               mla_kv_norm=m_mla_kv_norm, w_ukv=m_w_ukv, w_o=m_w_o, post_mix_norm=m_post_mix_norm,
               pre_ffn_norm=m_pre_ffn_norm, w_gate=m_w_gate, w_up=m_w_up, w_down=m_w_down, post_ffn_norm=m_post_ffn_norm,
               w_ple_proj=m_w_ple_proj, ple_norm=m_ple_norm, w_ple_gate=m_w_ple_gate, b_ple_gate=m_b_ple_gate)
    var = dict(pre_mix_norm=v_pre_mix_norm, w_in=v_w_in, ret_gn_w=v_ret_gn_w, mla_q_norm=v_mla_q_norm, w_uq=v_w_uq,
               mla_kv_norm=v_mla_kv_norm, w_ukv=v_w_ukv, w_o=v_w_o, post_mix_norm=v_post_mix_norm,
               pre_ffn_norm=v_pre_ffn_norm, w_gate=v_w_gate, w_up=v_w_up, w_down=v_w_down, post_ffn_norm=v_post_ffn_norm,
               w_ple_proj=v_w_ple_proj, ple_norm=v_ple_norm, w_ple_gate=v_w_ple_gate, b_ple_gate=v_b_ple_gate)

    S = x.shape[1]
    shard2d = {n: wts[n][0] for n, _, _, _ in BIG}
    small2d = {n: wts[n] for n, _ in SMALL}

    shard_bf = {n: shard2d[n].astype(BF16) for n in shard2d}
    pos_f = positions.astype(F32).reshape(S, 1)
    c_idx = lax.axis_index("c").astype(jnp.int32).reshape(1)
    loss_vec, grad_x, gw, gs, (sums_early, parts_early) = _local_step(
        x[0], p[0, 0], pos_f, loss_target[0], {}, small2d, shard_bf, c_idx)

    g4 = [_by_chip(gw[n], *BIG_SPEC[n]) for n in REDUCE_LAST]
    got = _swap_half_rows(g4)
    sums_last = [_add_half_rows(g4[i], got[i], c_idx, "rs_add_halves_" + n) for i, n in enumerate(REDUCE_LAST)]
    parts_last, small_sum = _scatter_to_chips(sums_last, _pack_small(gs, loss_vec))
    place = jnp.stack([2 * lax.axis_index("x") + lax.axis_index("y"), lax.axis_index("c")]).astype(jnp.int32)
    names = REDUCE_EARLY + REDUCE_LAST
    reduced = _join_half_rows(
        [_add_four(sm_, pt_, place, "rs_add_chips_" + n)
         for n, sm_, pt_ in zip(names, sums_early + sums_last, list(parts_early) + list(parts_last))])
    g_shard = dict(zip(names, reduced))

    loss = small_sum[9, 0]
    g_small = {n: small_sum[i:i + 1, :sz] for i, (n, sz) in enumerate(SMALL)}

    grads, delta, new_m, new_v = {}, {}, {}, {}
    for n, _, _, _ in BIG:
        if n in COLUMN_MAJOR:
            turn = lambda a: jnp.swapaxes(a, 1, 2)
            g_t = g_shard[n] if n in GRAD_TRANSPOSED else g_shard[n].T
            d, nm, nv = _adamw(turn(wts[n]), g_t, turn(mom[n]), turn(var[n]), "adamw_" + n)
            grads[n], delta[n], new_m[n], new_v[n] = turn(g_t[None]), turn(d), turn(nm), turn(nv)
        else:
            delta[n], new_m[n], new_v[n] = _adamw(wts[n], g_shard[n], mom[n], var[n], "adamw_" + n)
            grads[n] = g_shard[n][None]
    d, nm, nv = _adamw(_pack_small(small2d)[None], small_sum, _pack_small(mom)[None], _pack_small(var)[None],
                       "adamw_small")
    for i, (n, sz) in enumerate(SMALL):
        grads[n] = g_small[n]
        delta[n], new_m[n], new_v[n] = d[0, i:i + 1, :sz], nm[0, i:i + 1, :sz], nv[0, i:i + 1, :sz]

    return (loss, grad_x[None], *[grads[n] for n in ALL_W], *[delta[n] for n in ALL_W],
            *[new_m[n] for n in ALL_W], *[new_v[n] for n in ALL_W])
```

```python
import functools
import math

import jax
import jax.numpy as jnp
import numpy as np
from jax import lax
from jax.experimental import pallas as pl
from jax.experimental.pallas import tpu as pltpu

F32 = jnp.float32
BF16 = jnp.bfloat16
MESH = pl.DeviceIdType.MESH

D_MODEL = 1024
D_FF = 2816
PLE_DIM = 256
RET_HEADS = 4
RET_DIM = 128
RET_WIDTH = 512
RET_CHUNK = 256
RET_GROUP = 4
MLA_HEADS = 8
MLA_NOPE = 64
MLA_ROPE = 32
MLA_V = 64
Q_LORA = 384
KV_LORA = 256
IN_COLS = 2720
IN_COLS_P = 2816
ROPE_BASE = 10000.0
EPS = 1e-6
SCALE_MLA = 1.0 / math.sqrt(MLA_NOPE + MLA_ROPE)
SCALE_RET = RET_DIM ** -0.5
NEG = -1e30

ADAM_LR = 0.001
ADAM_B1 = 0.9
ADAM_B2 = 0.999
ADAM_EPS = 1e-08
ADAM_WD = 0.01
ADAM_STEP = 10

N_CHIPS = 4
N_DEV = 8
VMEM_MB = 56

BIG = (
    ("w_in", 1024, 2720, 1),
    ("w_uq", 384, 768, 1),
    ("w_ukv", 256, 1024, 1),
    ("w_o", 1024, 1024, 0),
    ("w_gate", 1024, 2816, 1),
    ("w_up", 1024, 2816, 1),
    ("w_down", 2816, 1024, 0),
    ("w_ple_proj", 256, 1024, 1),
    ("w_ple_gate", 1024, 1024, 0),
)
SMALL = (
    ("pre_mix_norm", 1024),
    ("ret_gn_w", 512),
    ("mla_q_norm", 384),
    ("mla_kv_norm", 256),
    ("post_mix_norm", 1024),
    ("pre_ffn_norm", 1024),
    ("post_ffn_norm", 1024),
    ("ple_norm", 1024),
    ("b_ple_gate", 1024),
)
ALL_W = ("pre_mix_norm", "w_in", "ret_gn_w", "mla_q_norm", "w_uq", "mla_kv_norm", "w_ukv", "w_o", "post_mix_norm",
         "pre_ffn_norm", "w_gate", "w_up", "w_down", "post_ffn_norm", "w_ple_proj", "ple_norm", "w_ple_gate", "b_ple_gate")
PACK_COLS = 1024
SMALL_ROWS = 16


def _cp(sem=None, mb=VMEM_MB, **kw):
    return pltpu.CompilerParams(dimension_semantics=sem, vmem_limit_bytes=mb * 1024 * 1024, **kw)


def _bf(x):
    return x.astype(BF16)


def _dot(a, b):
    return jnp.dot(_bf(a), _bf(b), preferred_element_type=F32)


def _dot_nt(a, b):
    return lax.dot_general(_bf(a), _bf(b), (((1,), (1,)), ((), ())), preferred_element_type=F32)


def _dot_tn(a, b):
    return lax.dot_general(_bf(a), _bf(b), (((0,), (0,)), ((), ())), preferred_element_type=F32)


def _sig(x):
    return 1.0 / (1.0 + jnp.exp(-x))


def _rms(x, g):
    r = lax.rsqrt(jnp.mean(x * x, axis=-1, keepdims=True) + EPS)
    return x * r * g


def _rms_bwd(dy, x, g):
    r = lax.rsqrt(jnp.mean(x * x, axis=-1, keepdims=True) + EPS)
    xh = x * r
    dxh = dy * g
    dx = r * (dxh - xh * jnp.mean(dxh * xh, axis=-1, keepdims=True))
    return dx, dy * xh


def _colsum(x):
    return jnp.sum(x, axis=0, keepdims=True)


def _rope_ret(x, cr, sr):
    return x * cr + pltpu.roll(x, 64, 1) * sr


def _unrope_ret(dy, cr, sr):
    return dy * cr + pltpu.roll(dy * sr, 64, 1)


def _rope_mla(x, cm, sa, sb):
    return x * cm + pltpu.roll(x, 112, 1) * sa + pltpu.roll(x, 16, 1) * sb


def _unrope_mla(dy, cm, sa, sb):
    return dy * cm + pltpu.roll(dy * sa, 16, 1) + pltpu.roll(dy * sb, 112, 1)


def _rows(tm, w, col=0):
    return pl.BlockSpec((tm, w), lambda i: (i, col))


def _full(*shape):
    return pl.BlockSpec(shape, lambda i: (0,) * len(shape), pipeline_mode=pl.Buffered(1))


def _acc(*shape):
    return pl.BlockSpec(shape, lambda i: (0,) * len(shape))


def _sds(shape, dtype):
    return jax.ShapeDtypeStruct(shape, dtype)


def _rope_tables(pos_f, S, shards=()):
    tm = min(512, S)
    n = len(shards)
    steps = S // tm
    inv_r = (1.0 / (np.float32(ROPE_BASE) ** (np.arange(64, dtype=np.float32) / np.float32(64)))).astype(np.float32)
    inv_m16 = (1.0 / (np.float32(ROPE_BASE) ** (np.arange(16, dtype=np.float32) / np.float32(16)))).astype(np.float32)
    inv_r = np.concatenate([inv_r, inv_r])[None, :]
    inv_m = np.zeros((1, 128), np.float32)
    inv_m[0, 64:80] = inv_m16
    inv_m[0, 80:96] = inv_m16

    def body(pos_ref, invr_ref, invm_ref, *rest):
        w_ins, (cr_ref, sr_ref, cm_ref, sa_ref, sb_ref) = rest[:n], rest[n:n + 5]
        w_outs, sems = rest[n + 5:2 * n + 5], rest[2 * n + 5:]
        i = pl.program_id(0)
        if n:
            @pl.when(i == 0)
            def _():
                _gather_phase(0, w_ins, w_outs, sems)

            @pl.when(i == steps // 2)
            def _():
                _gather_phase(1, w_ins, w_outs, sems)

        pos = pos_ref[...]
        lane = lax.broadcasted_iota(jnp.int32, (tm, 128), 1)
        ar = pos * invr_ref[...]
        s = jnp.sin(ar)
        cr_ref[...] = jnp.cos(ar)
        sr_ref[...] = jnp.where(lane < 64, -s, s)
        am = pos * invm_ref[...]
        c2 = jnp.cos(am)
        s2 = jnp.sin(am)
        cm_ref[...] = jnp.where(lane < 64, 1.0, jnp.where(lane < 96, c2, 0.0))
        sa_ref[...] = jnp.where((lane >= 64) & (lane < 80), -s2, 0.0)
        sb_ref[...] = jnp.where((lane >= 80) & (lane < 96), s2, 0.0)

        if n:
            @pl.when(i == steps - 1)
            def _():
                _gather_phase(2, w_ins, w_outs, sems)

    outs = pl.pallas_call(
        body, name="rope_tables", grid=(steps,),
        in_specs=[_rows(tm, 1), _full(1, 128), _full(1, 128)] + [_ANY] * n,
        out_specs=[_rows(tm, 128)] * 5 + [_ANY] * n,
        out_shape=[_sds((S, 128), F32)] * 5 + _gather_out_shapes(shards),
        scratch_shapes=_gather_sems(n) if n else [],
        compiler_params=_cp(("arbitrary",)),
    )(pos_f, jnp.asarray(inv_r), jnp.asarray(inv_m), *shards)
    return outs[:5], outs[5:]


def _inproj(x, g, w_in, tabs, S):
    tm = min(512, S)

    def body(x_ref, g_ref, w_ref, cr_ref, sr_ref, cm_ref, sa_ref, sb_ref,
             xn_ref, rq_ref, rk_ref, rv_ref, rg_ref, cq_ref, ckv_ref, kr_ref):
        xb = _rms(x_ref[...], g_ref[...]).astype(BF16)
        xn_ref[...] = xb
        cr = cr_ref[...]
        sr = sr_ref[...]
        q = jnp.dot(xb, w_ref[:, 0:512], preferred_element_type=F32)
        k = jnp.dot(xb, w_ref[:, 512:1024], preferred_element_type=F32)
        for h in range(RET_HEADS):
            sl = slice(h * 128, (h + 1) * 128)
            rq_ref[:, sl] = _rope_ret(q[:, sl], cr, sr).astype(BF16)
            rk_ref[:, sl] = (_rope_ret(k[:, sl], cr, sr) * SCALE_RET).astype(BF16)
        rv_ref[...] = jnp.dot(xb, w_ref[:, 1024:1536], preferred_element_type=F32).astype(BF16)
        rg_ref[...] = jnp.dot(xb, w_ref[:, 1536:2048], preferred_element_type=F32)
        cq_ref[...] = jnp.dot(xb, w_ref[:, 2048:2432], preferred_element_type=F32)
        ckv_ref[...] = jnp.dot(xb, w_ref[:, 2432:2688], preferred_element_type=F32)
        kr = jnp.dot(xb, w_ref[:, 2688:2816], preferred_element_type=F32)
        kr_ref[...] = _rope_mla(kr, cm_ref[...], sa_ref[...], sb_ref[...])

    return pl.pallas_call(
        body, name="inproj", grid=(S // tm,),
        in_specs=[_rows(tm, D_MODEL), _full(1, D_MODEL), _full(D_MODEL, IN_COLS_P)] + [_rows(tm, 128)] * 5,
        out_specs=[_rows(tm, D_MODEL)] + [_rows(tm, 512)] * 4 + [_rows(tm, Q_LORA), _rows(tm, KV_LORA), _rows(tm, 128)],
        out_shape=[_sds((S, D_MODEL), BF16)] + [_sds((S, 512), BF16)] * 3
        + [_sds((S, 512), F32), _sds((S, Q_LORA), F32), _sds((S, KV_LORA), F32), _sds((S, 128), F32)],
        compiler_params=_cp(("parallel",)),
    )(x, g, w_in, *tabs)


def _mla_up(cq, ckv, kr, gq, gkv, w_uq, w_ukv, tabs, S):
    tm = min(512, S)

    def body(cq_ref, ckv_ref, kr_ref, gq_ref, gkv_ref, wuq_ref, wukv_ref, cm_ref, sa_ref, sb_ref,
             cqn_ref, ckvn_ref, qp_ref, kp_ref, v_ref, kt_ref, vt_ref):
        cm = cm_ref[...]
        sa = sa_ref[...]
        sb = sb_ref[...]
        cqn = _rms(cq_ref[...], gq_ref[...]).astype(BF16)
        cqn_ref[...] = cqn
        ckvn = _rms(ckv_ref[...], gkv_ref[...]).astype(BF16)
        ckvn_ref[...] = ckvn
        qh = jnp.dot(cqn, wuq_ref[...], preferred_element_type=F32)
        kv = jnp.dot(ckvn, wukv_ref[...], preferred_element_type=F32)
        kr_blk = kr_ref[...]
        for h in range(MLA_HEADS):
            sl = slice(h * 128, (h + 1) * 128)
            qp_ref[:, sl] = (_rope_mla(qh[:, sl], cm, sa, sb) * SCALE_MLA).astype(BF16)
            kh = kv[:, sl] + kr_blk
            kp_ref[:, sl] = kh.astype(BF16)
            kt_ref[sl, :] = kh.T.astype(BF16)
        for h in range(MLA_HEADS // 2):
            vh = kv[:, 1024 + h * 128:1024 + (h + 1) * 128]
            v_ref[:, h * 128:(h + 1) * 128] = vh.astype(BF16)
            vt_ref[h * 128:(h + 1) * 128, :] = vh.T.astype(BF16)

    cols = lambda r: pl.BlockSpec((r, tm), lambda i: (0, i))
    return pl.pallas_call(
        body, name="mla_up", grid=(S // tm,),
        in_specs=[_rows(tm, Q_LORA), _rows(tm, KV_LORA), _rows(tm, 128), _full(1, Q_LORA), _full(1, KV_LORA),
                  _full(Q_LORA, 1024), _full(KV_LORA, 1536)] + [_rows(tm, 128)] * 3,
        out_specs=[_rows(tm, Q_LORA), _rows(tm, KV_LORA), _rows(tm, 1024), _rows(tm, 1024), _rows(tm, 512),
                   cols(1024), cols(512)],
        out_shape=[_sds((S, Q_LORA), BF16), _sds((S, KV_LORA), BF16), _sds((S, 1024), BF16), _sds((S, 1024), BF16),
                   _sds((S, 512), BF16), _sds((1024, S), BF16), _sds((512, S), BF16)],
        compiler_params=_cp(("parallel",)),
    )(cq, ckv, kr, gq, gkv, w_uq, w_ukv, *tabs[2:])


def _tri_pairs(nq, k_major):
    if k_major:
        pairs = [(qb, kb) for kb in range(nq) for qb in range(kb, nq)]
    else:
        pairs = [(qb, kb) for qb in range(nq) for kb in range(qb + 1)]
    qb_of = np.array([p[0] for p in pairs], np.int32)
    kb_of = np.array([p[1] for p in pairs], np.int32)
    return jnp.asarray(qb_of), jnp.asarray(kb_of), len(pairs)


ATT_ROWS = 32
FWD_HEADS = 8
BWD_HEADS = 4


def _causal_keep(r0, rows, tq):
    key = r0 + lax.broadcasted_iota(jnp.int32, (rows, tq), 0)
    qry = lax.broadcasted_iota(jnp.int32, (rows, tq), 1)
    return key <= qry


def _flash_fwd(qp, kp, vt, S, shards=()):
    tq = min(512, S)
    nq = S // tq
    RB = ATT_ROWS
    NH = FWD_HEADS
    qb_of, kb_of, T = _tri_pairs(nq, k_major=False)
    n = len(shards)
    steps = (MLA_HEADS // NH) * T

    def body(qb_ref, kb_ref, q_ref, k_ref, vt_ref, *rest):
        w_ins, (o_ref, lse_ref), w_outs = rest[:n], rest[n:n + 2], rest[n + 2:2 * n + 2]
        m_sc, l_sc, acc_sc, s_sc, p_sc = rest[2 * n + 2:2 * n + 7]
        sems = rest[2 * n + 7:]
        t = pl.program_id(1)
        qb = qb_ref[t]
        kb = kb_ref[t]
        lin = pl.program_id(0) * T + t

        if n:
            @pl.when(lin == 0)
            def _():
                _gather_phase(0, w_ins, w_outs, sems)

            @pl.when(lin == steps // 2)
            def _():
                _gather_phase(1, w_ins, w_outs, sems)

        @pl.when(kb == 0)
        def _():
            m_sc[...] = jnp.full(m_sc.shape, NEG, F32)
            l_sc[...] = jnp.zeros(l_sc.shape, F32)
            acc_sc[...] = jnp.zeros(acc_sc.shape, F32)

        def scores(a):
            sl = slice(a * 128, (a + 1) * 128)
            s_sc[a] = _dot_nt(k_ref[:, sl], q_ref[:, sl])

        def step(masked):
            for a in range(NH):
                scores(a)
            for a in range(NH):
                mx = [jnp.full((8, tq), NEG, F32) for _ in range(RB // 8)]
                for r in range(0, tq, RB):
                    sc = s_sc[a, r:r + RB, :]
                    if masked:
                        sc = jnp.where(_causal_keep(r, RB, tq), sc, NEG)
                        s_sc[a, r:r + RB, :] = sc
                    for i in range(RB // 8):
                        mx[i] = jnp.maximum(mx[i], sc[i * 8:(i + 1) * 8, :])
                mx8 = functools.reduce(jnp.maximum, mx)
                m_prev = m_sc[a]
                m_new = jnp.maximum(m_prev, jnp.max(mx8, axis=0, keepdims=True))
                al = jnp.exp(m_prev - m_new)
                m_sc[a] = m_new
                ls = [jnp.zeros((8, tq), F32) for _ in range(RB // 8)]
                for r in range(0, tq, RB):
                    p = jnp.exp(s_sc[a, r:r + RB, :] - m_new)
                    for i in range(RB // 8):
                        ls[i] = ls[i] + p[i * 8:(i + 1) * 8, :]
                    p_sc[a, r:r + RB, :] = p.astype(BF16)
                l_sc[a] = al * l_sc[a] + jnp.sum(functools.reduce(jnp.add, ls), axis=0, keepdims=True)
                pair = slice((a // 2) * 128, (a // 2 + 1) * 128)
                pv = jnp.dot(vt_ref[pair, :], p_sc[a], preferred_element_type=F32)
                rs = slice(a * 64, (a + 1) * 64)
                own = slice((a % 2) * 64, (a % 2 + 1) * 64)
                acc_sc[rs, :] = acc_sc[rs, :] * al + pv[own, :]

        @pl.when(kb < qb)
        def _():
            step(False)

        @pl.when(kb == qb)
        def _():
            step(True)
            for a in range(NH):
                rs = slice(a * 64, (a + 1) * 64)
                acc_sc[rs, :] = acc_sc[rs, :] / l_sc[a]
                lse_ref[a:a + 1, :] = m_sc[a] + jnp.log(l_sc[a])
            o_ref[...] = acc_sc[...].T.astype(BF16)

        if n:
            @pl.when(lin == steps - 1)
            def _():
                _gather_phase(2, w_ins, w_outs, sems)

    grid_spec = pltpu.PrefetchScalarGridSpec(
        num_scalar_prefetch=2, grid=(MLA_HEADS // NH, T),
        in_specs=[pl.BlockSpec((tq, 128 * NH), lambda j, t, qb, kb: (qb[t], j)),
                  pl.BlockSpec((tq, 128 * NH), lambda j, t, qb, kb: (kb[t], j)),
                  pl.BlockSpec((64 * NH, tq), lambda j, t, qb, kb: (j, kb[t]))] + [_ANY] * n,
        out_specs=[pl.BlockSpec((tq, 64 * NH), lambda j, t, qb, kb: (qb[t], j)),
                   pl.BlockSpec((None, NH, tq), lambda j, t, qb, kb: (j, 0, qb[t]))] + [_ANY] * n,
        scratch_shapes=[pltpu.VMEM((NH, 1, tq), F32), pltpu.VMEM((NH, 1, tq), F32), pltpu.VMEM((64 * NH, tq), F32),
                        pltpu.VMEM((NH, tq, tq), F32), pltpu.VMEM((NH, tq, tq), BF16)] + (_gather_sems(n) if n else []),
    )
    out, lse, *gathered = pl.pallas_call(
        body, name="flash_fwd", grid_spec=grid_spec,
        out_shape=[_sds((S, 512), BF16), _sds((MLA_HEADS // NH, NH, S), F32)] + _gather_out_shapes(shards),
        compiler_params=_cp(("arbitrary", "arbitrary")),
    )(qb_of, kb_of, qp, kp, vt, *shards)
    return out, lse.reshape(MLA_HEADS // 2, 2, S), gathered


def _decay_table():
    log_g = np.log(1.0 - 2.0 ** (-5.0 - np.arange(RET_HEADS, dtype=np.float32))).astype(np.float32)
    return jnp.asarray(np.broadcast_to(log_g[:, None, None], (RET_HEADS, 8, 128)).copy())


def _decay_terms(lg_ref):
    C = RET_CHUNK
    lg = lg_ref[0:1, :]
    row = lax.broadcasted_iota(jnp.int32, (C, C), 0)
    col = lax.broadcasted_iota(jnp.int32, (C, C), 1)
    diff = (row - col).astype(F32)
    dmat = jnp.where(diff >= 0, jnp.exp(jnp.maximum(diff, 0.0) * jnp.tile(lg, (1, C // 128))), 0.0)
    j = lax.broadcasted_iota(jnp.int32, (C, 1), 0).astype(F32)
    lg1 = lg[:, 0:1]
    zeta = jnp.exp((C - 1 - j) * lg1)
    xi = jnp.exp((j + 1.0) * lg1)
    g_chunk = jnp.exp(C * lg1)
    return dmat, zeta, xi, g_chunk


def _ret_fwd(rq, rk, rv, rg, gn_w, S):
    C = RET_CHUNK
    N = S // C
    G = min(RET_GROUP, N)
    NB = N // G

    def body(lg_ref, q_ref, k_ref, v_ref, rg_ref, w_ref, ry_ref, ro_ref, rprev_ref, r_sc):
        @pl.when(pl.program_id(1) == 0)
        def _():
            r_sc[...] = jnp.zeros(r_sc.shape, F32)

        dmat, zeta, xi, g_chunk = _decay_terms(lg_ref)
        w = w_ref[...]
        r = r_sc[...]
        for i in range(G):
            rows = slice(i * C, (i + 1) * C)
            q = q_ref[rows, :]
            k = k_ref[rows, :]
            v = v_ref[rows, :]
            r_prev = r.astype(BF16)
            rprev_ref[i] = r_prev
            sc = _dot_nt(q, k) * dmat
            ry = _dot(sc, v) + jnp.dot(q, r_prev, preferred_element_type=F32) * xi
            ry_ref[rows, :] = ry
            r = g_chunk * r + _dot_tn(k, zeta * v.astype(F32))
            mu = jnp.mean(ry, axis=-1, keepdims=True)
            yc = ry - mu
            yh = yc * lax.rsqrt(jnp.mean(yc * yc, axis=-1, keepdims=True) + EPS)
            g = rg_ref[rows, :]
            ro_ref[rows, :] = (g * _sig(g) * (yh * w)).astype(BF16)
        r_sc[...] = r

    blk = pl.BlockSpec((G * C, 128), lambda h, n: (n, h))
    return pl.pallas_call(
        body, name="ret_fwd", grid=(RET_HEADS, NB),
        in_specs=[pl.BlockSpec((None, 8, 128), lambda h, n: (h, 0, 0)), blk, blk, blk, blk,
                  pl.BlockSpec((1, 128), lambda h, n: (0, h))],
        out_specs=[blk, blk, pl.BlockSpec((G, 128, 128), lambda h, n: (h * NB + n, 0, 0))],
        out_shape=[_sds((S, 512), F32), _sds((S, 512), BF16), _sds((RET_HEADS * N, 128, 128), BF16)],
        scratch_shapes=[pltpu.VMEM((128, 128), F32)],
        compiler_params=_cp(("parallel", "arbitrary")),
    )(_decay_table(), rq, rk, rv, rg, gn_w)


def _outproj(ro, mo, x, w_o, g_post, g_pre, S):
    tm = min(512, S)

    def body(ro_ref, mo_ref, x_ref, wo_ref, g1_ref, g2_ref, mix_ref, h1_ref, hn_ref):
        mix = (jnp.dot(ro_ref[...], wo_ref[0:512, :], preferred_element_type=F32)
               + jnp.dot(mo_ref[...], wo_ref[512:1024, :], preferred_element_type=F32))
        mix_ref[...] = mix.astype(BF16)
        h1 = x_ref[...] + _rms(mix, g1_ref[...])
        h1_ref[...] = h1
        hn_ref[...] = _rms(h1, g2_ref[...]).astype(BF16)

    return pl.pallas_call(
        body, name="outproj", grid=(S // tm,),
        in_specs=[_rows(tm, 512), _rows(tm, 512), _rows(tm, D_MODEL), _full(D_MODEL, D_MODEL), _full(1, D_MODEL),
                  _full(1, D_MODEL)],
        out_specs=[_rows(tm, D_MODEL)] * 3,
        out_shape=[_sds((S, D_MODEL), BF16), _sds((S, D_MODEL), F32), _sds((S, D_MODEL), BF16)],
        compiler_params=_cp(("parallel",)),
    )(ro, mo, x, w_o, g_post, g_pre)


def _ffn_up(hn, w_gate_t, w_up_t, S):
    tm = min(512, S)
    tn = D_FF // 2

    def body(hn_ref, wg_ref, wu_ref, gate_ref, up_ref, act_ref):
        hn_b = hn_ref[...]
        g = _dot_nt(hn_b, wg_ref[...])
        u = _dot_nt(hn_b, wu_ref[...])
        gate_ref[...] = g.astype(BF16)
        up_ref[...] = u.astype(BF16)
        act_ref[...] = (g * _sig(g) * u).astype(BF16)

    wspec = pl.BlockSpec((tn, D_MODEL), lambda j, i: (j, 0))
    ospec = pl.BlockSpec((tm, tn), lambda j, i: (i, j))
    return pl.pallas_call(
        body, name="ffn_up", grid=(2, S // tm),
        in_specs=[pl.BlockSpec((tm, D_MODEL), lambda j, i: (i, 0)), wspec, wspec],
        out_specs=[ospec] * 3, out_shape=[_sds((S, D_FF), BF16)] * 3,
        compiler_params=_cp(("parallel", "parallel")),
    )(hn, w_gate_t, w_up_t)


def _ffn_down(act, w_down, h1, g, S):
    tm = min(512, S)

    def body(act_ref, wd_ref, h1_ref, g_ref, ff_ref, h2_ref):
        ff = jnp.dot(act_ref[...], wd_ref[...], preferred_element_type=F32)
        ff_ref[...] = ff.astype(BF16)
        h2_ref[...] = h1_ref[...] + _rms(ff, g_ref[...])

    return pl.pallas_call(
        body, name="ffn_down", grid=(S // tm,),
        in_specs=[_rows(tm, D_FF), _full(D_FF, D_MODEL), _rows(tm, D_MODEL), _full(1, D_MODEL)],
        out_specs=[_rows(tm, D_MODEL)] * 2, out_shape=[_sds((S, D_MODEL), BF16), _sds((S, D_MODEL), F32)],
        compiler_params=_cp(("parallel",)),
    )(act, w_down, h1, g)


def _ple_loss(p, h2, tgt, w_pp, w_pg, b_pg, g_ple, S):
    tm = min(512, S)

    def body(p_ref, h2_ref, t_ref, wp_ref, wg_ref, b_ref, gp_ref,
             dz_ref, dpe_ref, dh2_ref, h2b_ref, loss_ref, dgp_ref, db_ref):
        @pl.when(pl.program_id(0) == 0)
        def _():
            loss_ref[...] = jnp.zeros(loss_ref.shape, F32)
            dgp_ref[...] = jnp.zeros(dgp_ref.shape, F32)
            db_ref[...] = jnp.zeros(db_ref.shape, F32)

        gp = gp_ref[...]
        pe = _dot(p_ref[...], wp_ref[...])
        r = lax.rsqrt(jnp.mean(pe * pe, axis=-1, keepdims=True) + EPS)
        peh = pe * r
        e = peh * gp
        h2 = h2_ref[...]
        h2b = h2.astype(BF16)
        h2b_ref[...] = h2b
        gt = _sig(jnp.dot(h2b, wg_ref[...], preferred_element_type=F32) + b_ref[...])
        diff = h2 + e * gt - t_ref[...]
        loss_ref[...] += _colsum(diff * diff)
        dh3 = diff * (1.0 / D_MODEL)
        de = dh3 * gt
        dz = dh3 * e * gt * (1.0 - gt)
        db_ref[...] += _colsum(dz)
        dgp_ref[...] += _colsum(de * peh)
        dpeh = de * gp
        dpe = r * (dpeh - peh * jnp.mean(dpeh * peh, axis=-1, keepdims=True))
        dzb = dz.astype(BF16)
        dz_ref[...] = dzb
        dpe_ref[...] = dpe.astype(BF16)
        dh2_ref[...] = dh3 + _dot_nt(dzb, wg_ref[...])

    return pl.pallas_call(
        body, name="ple_loss", grid=(S // tm,),
        in_specs=[_rows(tm, PLE_DIM), _rows(tm, D_MODEL), _rows(tm, D_MODEL), _full(PLE_DIM, D_MODEL),
                  _full(D_MODEL, D_MODEL), _full(1, D_MODEL), _full(1, D_MODEL)],
        out_specs=[_rows(tm, D_MODEL)] * 4 + [_acc(1, D_MODEL)] * 3,
        out_shape=[_sds((S, D_MODEL), BF16), _sds((S, D_MODEL), BF16), _sds((S, D_MODEL), F32), _sds((S, D_MODEL), BF16)]
        + [_sds((1, D_MODEL), F32)] * 3,
        compiler_params=_cp(("arbitrary",)),
    )(p, h2, tgt, w_pp, w_pg, b_pg, g_ple)


def _wgrad(a, b, name, S):
    M = a.shape[1]
    N = b.shape[1]
    ts = min(2048, S)
    nsplit = 2 if M * N >= 2 * 1024 * 1024 else 1
    tn = N // nsplit

    def body(a_ref, b_ref, o_ref):
        @pl.when(pl.program_id(1) == 0)
        def _():
            o_ref[...] = jnp.zeros(o_ref.shape, F32)

        o_ref[...] += _dot_tn(a_ref[...], b_ref[...])

    return pl.pallas_call(
        body, name=name, grid=(nsplit, S // ts),
        in_specs=[pl.BlockSpec((ts, M), lambda j, s: (s, 0)), pl.BlockSpec((ts, tn), lambda j, s: (s, j))],
        out_specs=pl.BlockSpec((M, tn), lambda j, s: (0, j)), out_shape=_sds((M, N), F32),
        compiler_params=_cp(("parallel", "arbitrary")),
    )(a, b)


def _ffn_down_bwd(dh2, ff, g, w_down, gate, up, S):
    tm = min(512, S)
    tn = D_FF // 2

    def body(dh2_ref, ff_ref, g_ref, wd_ref, gate_ref, up_ref, dff_ref, dgate_ref, dup_ref, dg_ref):
        @pl.when(pl.program_id(0) == 0)
        def _():
            dg_ref[...] = jnp.zeros(dg_ref.shape, F32)

        dff, ga = _rms_bwd(dh2_ref[...], ff_ref[...].astype(F32), g_ref[...])
        dg_ref[...] += _colsum(ga)
        dffb = dff.astype(BF16)
        dff_ref[...] = dffb
        for seg in range(2):
            sl = slice(seg * tn, (seg + 1) * tn)
            dact = _dot_nt(dffb, wd_ref[sl, :])
            gt = gate_ref[:, sl].astype(F32)
            u = up_ref[:, sl].astype(F32)
            s = _sig(gt)
            dgate_ref[:, sl] = (dact * u * (s * (1.0 + gt * (1.0 - s)))).astype(BF16)
            dup_ref[:, sl] = (dact * (gt * s)).astype(BF16)

    return pl.pallas_call(
        body, name="ffn_down_bwd", grid=(S // tm,),
        in_specs=[_rows(tm, D_MODEL), _rows(tm, D_MODEL), _full(1, D_MODEL), _full(D_FF, D_MODEL), _rows(tm, D_FF),
                  _rows(tm, D_FF)],
        out_specs=[_rows(tm, D_MODEL), _rows(tm, D_FF), _rows(tm, D_FF), _acc(1, D_MODEL)],
        out_shape=[_sds((S, D_MODEL), BF16), _sds((S, D_FF), BF16), _sds((S, D_FF), BF16), _sds((1, D_MODEL), F32)],
        compiler_params=_cp(("arbitrary",)),
    )(dh2, ff, g, w_down, gate, up)


def _ffn_up_bwd(dgate, dup, w_gate, w_up, h1, mix, dh2, g_pre, g_post, w_o, S, grads=()):
    tm = min(512, S)
    n = len(grads)
    last = S // tm - 1

    def body(dgate_ref, dup_ref, wg_ref, wu_ref, h1_ref, mix_ref, dh2_ref, g2_ref, g1_ref, wo_ref, *rest):
        g_ins = rest[:n]
        dh1_ref, dmix_ref, dro_ref, dmo_ref, dg2_ref, dg1_ref = rest[n:n + 6]
        g_outs, sems = rest[n + 6:2 * n + 6], rest[2 * n + 6:]

        @pl.when(pl.program_id(0) == 0)
        def _():
            dg2_ref[...] = jnp.zeros(dg2_ref.shape, F32)
            dg1_ref[...] = jnp.zeros(dg1_ref.shape, F32)
            for cp in (_swap_copies(g_ins, g_outs, sems) if n else []):
                cp.start()

        dhn = (jnp.dot(dgate_ref[...], wg_ref[...], preferred_element_type=F32)
               + jnp.dot(dup_ref[...], wu_ref[...], preferred_element_type=F32))
        d1, ga = _rms_bwd(dhn, h1_ref[...], g2_ref[...])
        dg2_ref[...] += _colsum(ga)
        dh1 = dh2_ref[...] + d1
        dh1_ref[...] = dh1
        dmix, gb = _rms_bwd(dh1, mix_ref[...].astype(F32), g1_ref[...])
        dg1_ref[...] += _colsum(gb)
        dmixb = dmix.astype(BF16)
        dmix_ref[...] = dmixb
        dcat = _dot_nt(dmixb, wo_ref[...])
        dro_ref[...] = dcat[:, 0:512].astype(BF16)
        dmo_ref[...] = dcat[:, 512:1024].astype(BF16)

        if n:
            @pl.when(pl.program_id(0) == last)
            def _():
                for cp in _swap_copies(g_ins, g_outs, sems):
                    cp.wait()

    dh1, dmix, dro, dmo, dg2, dg1, *got = pl.pallas_call(
        body, name="ffn_up_bwd", grid=(S // tm,),
        in_specs=[_rows(tm, D_FF), _rows(tm, D_FF), _full(D_FF, D_MODEL), _full(D_FF, D_MODEL), _rows(tm, D_MODEL),
                  _rows(tm, D_MODEL), _rows(tm, D_MODEL), _full(1, D_MODEL), _full(1, D_MODEL), _full(D_MODEL, D_MODEL)]
        + [_ANY] * n,
        out_specs=[_rows(tm, D_MODEL), _rows(tm, D_MODEL), _rows(tm, 512), _rows(tm, 512), _acc(1, D_MODEL),
                   _acc(1, D_MODEL)] + [_ANY] * n,
        out_shape=[_sds((S, D_MODEL), F32), _sds((S, D_MODEL), BF16), _sds((S, 512), BF16), _sds((S, 512), BF16),
                   _sds((1, D_MODEL), F32), _sds((1, D_MODEL), F32)] + _swap_out_shapes(grads),
        scratch_shapes=_swap_sems(n) if n else [],
        compiler_params=_cp(("arbitrary",)),
    )(dgate, dup, w_gate, w_up, h1, mix, dh2, g_pre, g_post, w_o, *grads)
    return dh1, dmix, dro, dmo, dg2, dg1, got


def _attn_delta(o, do, S):
    tm = min(512, S)

    def body(o_ref, do_ref, dot_ref, d_ref):
        do = do_ref[...].astype(F32)
        prod_t = (o_ref[...].astype(F32) * do).T
        dot_ref[...] = do.T.astype(BF16)
        for h in range(MLA_HEADS):
            d_ref[h // 2, (h % 2):(h % 2) + 1, :] = jnp.sum(prod_t[h * 64:(h + 1) * 64, :], axis=0, keepdims=True)

    return pl.pallas_call(
        body, name="attn_delta", grid=(S // tm,),
        in_specs=[_rows(tm, 512), _rows(tm, 512)],
        out_specs=[pl.BlockSpec((512, tm), lambda i: (0, i)), pl.BlockSpec((MLA_HEADS // 2, 2, tm), lambda i: (0, 0, i))],
        out_shape=[_sds((512, S), BF16), _sds((MLA_HEADS // 2, 2, S), F32)],
        compiler_params=_cp(("parallel",)),
    )(o, do)


def _flash_bwd(qp, kp, kt, v, do, dot, lse, delta, S, sums=()):
    tq = min(512, S)
    nq = S // tq
    RB = ATT_ROWS
    NH = BWD_HEADS
    qb_of, kb_of, T = _tri_pairs(nq, k_major=True)
    n = len(sums)
    steps = (MLA_HEADS // NH) * T

    def body(qb_ref, kb_ref, q_ref, k_ref, kt_ref, v_ref, do_ref, dot_ref, lse_ref, dl_ref, *rest):
        g_ins, (dq_ref, dk_ref, dv_ref), g_outs = rest[:n], rest[n:n + 3], rest[n + 3:2 * n + 3]
        dk_sc, dv_sc, s_sc, dp_sc, p_sc, ds_sc = rest[2 * n + 3:2 * n + 9]
        sems = rest[2 * n + 9:]
        t = pl.program_id(1)
        qb = qb_ref[t]
        kb = kb_ref[t]
        lin = pl.program_id(0) * T + t

        if n:
            @pl.when(lin == 0)
            def _():
                for cp in _scatter_copies(g_ins, g_outs, sems):
                    cp.start()

        @pl.when(t == 0)
        def _():
            dq_ref[...] = jnp.zeros(dq_ref.shape, F32)

        @pl.when(qb == kb)
        def _():
            dk_sc[...] = jnp.zeros(dk_sc.shape, F32)
            dv_sc[...] = jnp.zeros(dv_sc.shape, F32)

        lane = lax.broadcasted_iota(jnp.int32, (tq, 64 * NH), 1)

        def step(masked):
            vv = v_ref[...]
            do_all = do_ref[...]
            mine = [(lane >= a * 64) & (lane < (a + 1) * 64) for a in range(NH)]
            for a in range(NH):
                sl = slice(a * 128, (a + 1) * 128)
                s_sc[a] = _dot_nt(k_ref[:, sl], q_ref[:, sl])
                dp_sc[a] = jnp.dot(jnp.where(mine[a], vv, jnp.zeros_like(vv)), dot_ref[...],
                                   preferred_element_type=F32)
            for a in range(NH):
                sl = slice(a * 128, (a + 1) * 128)
                lse = lse_ref[a:a + 1, :]
                dl = dl_ref[a:a + 1, :]
                for r in range(0, tq, RB):
                    sc = s_sc[a, r:r + RB, :]
                    if masked:
                        sc = jnp.where(_causal_keep(r, RB, tq), sc, NEG)
                    p = jnp.exp(sc - lse)
                    p_sc[a, r:r + RB, :] = p.astype(BF16)
                    ds_sc[a, r:r + RB, :] = (p * (dp_sc[a, r:r + RB, :] - dl)).astype(BF16)
                ds = ds_sc[a]
                dv_sc[...] += jnp.dot(p_sc[a], jnp.where(mine[a], do_all, jnp.zeros_like(do_all)),
                                      preferred_element_type=F32)
                dk_sc[:, sl] += jnp.dot(ds, q_ref[:, sl], preferred_element_type=F32)
                dq_ref[qb, sl, :] += jnp.dot(kt_ref[sl, :], ds, preferred_element_type=F32)

        @pl.when(qb > kb)
        def _():
            step(False)

        @pl.when(qb == kb)
        def _():
            step(True)

        @pl.when(qb == nq - 1)
        def _():
            dk_ref[...] = dk_sc[...].astype(BF16)
            dv_ref[...] = dv_sc[...].astype(BF16)

        if n:
            @pl.when(lin == steps - 1)
            def _():
                for cp in _scatter_copies(g_ins, g_outs, sems):
                    cp.wait()

    grid_spec = pltpu.PrefetchScalarGridSpec(
        num_scalar_prefetch=2, grid=(MLA_HEADS // NH, T),
        in_specs=[pl.BlockSpec((tq, 128 * NH), lambda j, t, qb, kb: (qb[t], j)),
                  pl.BlockSpec((tq, 128 * NH), lambda j, t, qb, kb: (kb[t], j)),
                  pl.BlockSpec((128 * NH, tq), lambda j, t, qb, kb: (j, kb[t])),
                  pl.BlockSpec((tq, 64 * NH), lambda j, t, qb, kb: (kb[t], j)),
                  pl.BlockSpec((tq, 64 * NH), lambda j, t, qb, kb: (qb[t], j)),
                  pl.BlockSpec((64 * NH, tq), lambda j, t, qb, kb: (j, qb[t])),
                  pl.BlockSpec((None, NH, tq), lambda j, t, qb, kb: (j, 0, qb[t])),
                  pl.BlockSpec((None, NH, tq), lambda j, t, qb, kb: (j, 0, qb[t]))] + [_ANY] * n,
        out_specs=[pl.BlockSpec((nq, 128 * NH, tq), lambda j, t, qb, kb: (0, j, 0), pipeline_mode=pl.Buffered(1)),
                   pl.BlockSpec((tq, 128 * NH), lambda j, t, qb, kb: (kb[t], j)),
                   pl.BlockSpec((tq, 64 * NH), lambda j, t, qb, kb: (kb[t], j))] + [_ANY] * n,
        scratch_shapes=[pltpu.VMEM((tq, 128 * NH), F32), pltpu.VMEM((tq, 64 * NH), F32), pltpu.VMEM((NH, tq, tq), F32),
                        pltpu.VMEM((NH, tq, tq), F32), pltpu.VMEM((NH, tq, tq), BF16), pltpu.VMEM((NH, tq, tq), BF16)]
        + (_scatter_sems(n) if n else []),
    )
    dq, dk, dv, *parts = pl.pallas_call(
        body, name="flash_bwd", grid_spec=grid_spec,
        out_shape=[_sds((nq, 1024, tq), F32), _sds((S, 1024), BF16), _sds((S, 512), BF16)] + _scatter_out_shapes(sums),
        compiler_params=_cp(("arbitrary", "arbitrary")),
    )(qb_of, kb_of, qp, kp, kt, v, do, dot, lse.reshape(MLA_HEADS // NH, NH, S), delta.reshape(MLA_HEADS // NH, NH, S),
      *sums)
    return dq, dk, dv, parts


def _mla_up_bwd(dqp, dkp, dv, cq, ckv, gq, gkv, w_uq, w_ukv, tabs, S):
    tm = min(512, S)

    def body(dq_ref, dk_ref, dv_ref, cq_ref, ckv_ref, gq_ref, gkv_ref, wuq_ref, wukv_ref, cm_ref, sa_ref, sb_ref,
             dqh_ref, dkv_ref, dcq_ref, dckv_ref, dkr_ref, dgq_ref, dgkv_ref):
        @pl.when(pl.program_id(0) == 0)
        def _():
            dgq_ref[...] = jnp.zeros(dgq_ref.shape, F32)
            dgkv_ref[...] = jnp.zeros(dgkv_ref.shape, F32)

        cm = cm_ref[...]
        sa = sa_ref[...]
        sb = sb_ref[...]
        lane = lax.broadcasted_iota(jnp.int32, (tm, 128), 1)
        dkr_r = jnp.zeros((tm, 128), F32)
        for h in range(MLA_HEADS):
            sl = slice(h * 128, (h + 1) * 128)
            dqh_ref[:, sl] = (_unrope_mla(dq_ref[sl, :].T, cm, sa, sb) * SCALE_MLA).astype(BF16)
            gk = dk_ref[:, sl]
            dkr_r = dkr_r + gk.astype(F32)
            dkv_ref[:, sl] = gk
        dkr_r = jnp.where((lane >= 64) & (lane < 96), dkr_r, 0.0)
        dkr_ref[...] = _unrope_mla(dkr_r, cm, sa, sb).astype(BF16)
        dkv_ref[:, 1024:1536] = dv_ref[...]
        dcq, ga = _rms_bwd(_dot_nt(dqh_ref[...], wuq_ref[...]), cq_ref[...], gq_ref[...])
        dcq_ref[...] = dcq.astype(BF16)
        dgq_ref[...] += _colsum(ga)
        dckv, gb = _rms_bwd(_dot_nt(dkv_ref[...], wukv_ref[...]), ckv_ref[...], gkv_ref[...])
        dckv_ref[...] = dckv.astype(BF16)
        dgkv_ref[...] += _colsum(gb)

    per_q = dqp.shape[2] // tm
    return pl.pallas_call(
        body, name="mla_up_bwd", grid=(S // tm,),
        in_specs=[pl.BlockSpec((None, 1024, tm), lambda i: (i // per_q, 0, i % per_q)),
                  _rows(tm, 1024), _rows(tm, 512), _rows(tm, Q_LORA), _rows(tm, KV_LORA),
                  _full(1, Q_LORA), _full(1, KV_LORA), _full(Q_LORA, 1024), _full(KV_LORA, 1536)] + [_rows(tm, 128)] * 3,
        out_specs=[_rows(tm, 1024), _rows(tm, 1536), _rows(tm, Q_LORA), _rows(tm, KV_LORA), _rows(tm, 128),
                   _acc(1, Q_LORA), _acc(1, KV_LORA)],
        out_shape=[_sds((S, 1024), BF16), _sds((S, 1536), BF16), _sds((S, Q_LORA), BF16), _sds((S, KV_LORA), BF16),
                   _sds((S, 128), BF16), _sds((1, Q_LORA), F32), _sds((1, KV_LORA), F32)],
        compiler_params=_cp(("arbitrary",)),
    )(dqp, dkp, dv, cq, ckv, gq, gkv, w_uq, w_ukv, *tabs[2:])


def _ret_bwd(rq, rk, rv, rprev, ry, rg, dro, gn_w, tabs, S):
    C = RET_CHUNK
    N = S // C
    G = min(RET_GROUP, N)
    NB = N // G

    def body(lg_ref, q_ref, k_ref, v_ref, rp_ref, ry_ref, rg_ref, dro_ref, w_ref, cr_ref, sr_ref,
             drq_ref, drk_ref, drv_ref, drg_ref, dw_ref, g_sc):
        @pl.when(pl.program_id(1) == 0)
        def _():
            g_sc[...] = jnp.zeros(g_sc.shape, F32)
            dw_ref[...] = jnp.zeros(dw_ref.shape, F32)

        dmat, zeta, xi, g_chunk = _decay_terms(lg_ref)
        w = w_ref[...]
        gacc = g_sc[...]
        dw = jnp.zeros((1, 128), F32)
        for i in reversed(range(G)):
            rows = slice(i * C, (i + 1) * C)
            ry = ry_ref[rows, :]
            mu = jnp.mean(ry, axis=-1, keepdims=True)
            yc = ry - mu
            rstd = lax.rsqrt(jnp.mean(yc * yc, axis=-1, keepdims=True) + EPS)
            yh = yc * rstd
            g = rg_ref[rows, :]
            s = _sig(g)
            dout = dro_ref[rows, :].astype(F32)
            drg_ref[rows, :] = (dout * (yh * w) * (s * (1.0 + g * (1.0 - s)))).astype(BF16)
            dgn = dout * (g * s)
            dw = dw + _colsum(dgn * yh)
            dyh = dgn * w
            dry = rstd * (dyh - jnp.mean(dyh, axis=-1, keepdims=True) - yh * jnp.mean(dyh * yh, axis=-1, keepdims=True))
            do = dry.astype(BF16)

            q = q_ref[rows, :]
            k = k_ref[rows, :]
            v = v_ref[rows, :]
            gfut = gacc.astype(BF16)
            sc = (_dot_nt(q, k) * dmat).astype(BF16)
            dsc = (_dot_nt(do, v) * dmat).astype(BF16)
            dq = jnp.dot(dsc, k, preferred_element_type=F32) + _dot_nt(do, rp_ref[i]) * xi
            dk = _dot_tn(dsc, q) + _dot_nt(v, gfut) * zeta
            dv = _dot_tn(sc, do) + jnp.dot(k, gfut, preferred_element_type=F32) * zeta
            gacc = g_chunk * gacc + _dot_tn(q, xi * dry)
            cr = cr_ref[rows, :]
            sr = sr_ref[rows, :]
            drq_ref[rows, :] = _unrope_ret(dq, cr, sr).astype(BF16)
            drk_ref[rows, :] = _unrope_ret(dk * SCALE_RET, cr, sr).astype(BF16)
            drv_ref[rows, :] = dv.astype(BF16)
        g_sc[...] = gacc
        dw_ref[...] += dw

    blk = pl.BlockSpec((G * C, 128), lambda h, n: (NB - 1 - n, h))
    tab = pl.BlockSpec((G * C, 128), lambda h, n: (NB - 1 - n, 0))
    return pl.pallas_call(
        body, name="ret_bwd", grid=(RET_HEADS, NB),
        in_specs=[pl.BlockSpec((None, 8, 128), lambda h, n: (h, 0, 0)), blk, blk, blk,
                  pl.BlockSpec((G, 128, 128), lambda h, n: (h * NB + NB - 1 - n, 0, 0)), blk, blk, blk,
                  pl.BlockSpec((1, 128), lambda h, n: (0, h)), tab, tab],
        out_specs=[blk, blk, blk, blk, pl.BlockSpec((1, 128), lambda h, n: (0, h))],
        out_shape=[_sds((S, 512), BF16)] * 4 + [_sds((1, 512), F32)],
        scratch_shapes=[pltpu.VMEM((128, 128), F32)],
        compiler_params=_cp(("parallel", "arbitrary")),
    )(_decay_table(), rq, rk, rv, rprev, ry, rg, dro, gn_w, tabs[0], tabs[1])


def _inproj_bwd(drq, drk, drv, drg, dcq, dckv, dkr, w_in, dh1, x, g, S):
    tm = min(512, S)

    def body(drq_ref, drk_ref, drv_ref, drg_ref, dcq_ref, dckv_ref, dkr_ref, w_ref, dh1_ref, x_ref, g_ref,
             gx_ref, dproj_ref, dg_ref):
        @pl.when(pl.program_id(0) == 0)
        def _():
            dg_ref[...] = jnp.zeros(dg_ref.shape, F32)

        dproj_ref[:, 0:512] = drq_ref[...]
        dproj_ref[:, 512:1024] = drk_ref[...]
        dproj_ref[:, 1024:1536] = drv_ref[...]
        dproj_ref[:, 1536:2048] = drg_ref[...]
        dproj_ref[:, 2048:2432] = dcq_ref[...]
        dproj_ref[:, 2432:2688] = dckv_ref[...]
        dproj_ref[:, 2688:2816] = dkr_ref[...]
        dx, ga = _rms_bwd(_dot_nt(dproj_ref[...], w_ref[...]), x_ref[...], g_ref[...])
        gx_ref[...] = dh1_ref[...] + dx
        dg_ref[...] += _colsum(ga)

    return pl.pallas_call(
        body, name="inproj_bwd", grid=(S // tm,),
        in_specs=[_rows(tm, 512)] * 4 + [_rows(tm, Q_LORA), _rows(tm, KV_LORA), _rows(tm, 128),
                                         _full(D_MODEL, IN_COLS_P), _rows(tm, D_MODEL), _rows(tm, D_MODEL),
                                         _full(1, D_MODEL)],
        out_specs=[_rows(tm, D_MODEL), _rows(tm, IN_COLS_P), _acc(1, D_MODEL)],
        out_shape=[_sds((S, D_MODEL), F32), _sds((S, IN_COLS_P), BF16), _sds((1, D_MODEL), F32)],
        compiler_params=_cp(("arbitrary",)),
    )(drq, drk, drv, drg, dcq, dckv, dkr, w_in, dh1, x, g)


def _pad_weights(w):
    w_in = w["w_in"]
    z = lambda r, c: jnp.zeros((r, c), BF16)
    w_in_p = jnp.concatenate([w_in[:, :2688], z(1024, 64), w_in[:, 2688:2720], z(1024, 32)], axis=1)
    w_uq_p = jnp.pad(w["w_uq"].reshape(Q_LORA, MLA_HEADS, 96), ((0, 0), (0, 0), (0, 32))).reshape(Q_LORA, 1024)
    ukv = w["w_ukv"].reshape(KV_LORA, MLA_HEADS, 128)
    k_part = jnp.pad(ukv[:, :, :64], ((0, 0), (0, 0), (0, 64))).reshape(KV_LORA, 1024)
    w_ukv_p = jnp.concatenate([k_part, ukv[:, :, 64:].reshape(KV_LORA, 512)], axis=1)
    return w_in_p, w_uq_p, w_ukv_p


BIG_SPEC = {n: (r, c, ax) for n, r, c, ax in BIG}
COLUMN_MAJOR = ("w_in", "w_uq", "w_gate", "w_up")
GRAD_TRANSPOSED = ("w_gate", "w_up")
GATHER_FIRST = ("w_in", "w_uq", "w_ukv")
GATHER_LATE = tuple(n for n, _, _, _ in BIG if n not in GATHER_FIRST)
REDUCE_EARLY = ("w_ple_gate", "w_ple_proj", "w_down", "w_gate", "w_up")
REDUCE_LAST = tuple(n for n, _, _, _ in BIG if n not in REDUCE_EARLY)


def _local_step(x, p, pos_f, tgt, w, sm, late_shards=None, c_idx=None):
    S = x.shape[0]
    spread = late_shards is not None
    w = dict(w)
    tabs, first = _rope_tables(pos_f, S, [late_shards[n] for n in GATHER_FIRST] if spread else ())
    for i, n in enumerate(GATHER_FIRST if spread else ()):
        w[n] = _from_chips(first[i], BIG_SPEC[n][2])
    w_in_p, w_uq_p, w_ukv_p = _pad_weights(w)

    xn, rq, rk, rv, rg, cq, ckv, kr = _inproj(x, sm["pre_mix_norm"], w_in_p, tabs, S)
    cqn, ckvn, qp, kp, v, kt, vt = _mla_up(cq, ckv, kr, sm["mla_q_norm"], sm["mla_kv_norm"], w_uq_p, w_ukv_p, tabs, S)
    mo, lse, gathered = _flash_fwd(qp, kp, vt, S, [late_shards[n] for n in GATHER_LATE] if spread else ())
    for i, n in enumerate(GATHER_LATE if spread else ()):
        w[n] = _from_chips(gathered[i], 0 if n in GRAD_TRANSPOSED else BIG_SPEC[n][2])
    if not spread:
        w.update({n: w[n].T for n in GRAD_TRANSPOSED})
    ry, ro, rprev = _ret_fwd(rq, rk, rv, rg, sm["ret_gn_w"], S)
    mix, h1, hn = _outproj(ro, mo, x, w["w_o"], sm["post_mix_norm"], sm["pre_ffn_norm"], S)
    gate, up, act = _ffn_up(hn, w["w_gate"], w["w_up"], S)
    ff, h2 = _ffn_down(act, w["w_down"], h1, sm["post_ffn_norm"], S)
    dz, dpe, dh2, h2b, loss_vec, d_ple_norm, d_b = _ple_loss(
        p, h2, tgt, w["w_ple_proj"], w["w_ple_gate"], sm["b_ple_gate"], sm["ple_norm"], S)

    gw = {}
    gs = {"ple_norm": d_ple_norm, "b_ple_gate": d_b}
    gw["w_ple_gate"] = _wgrad(h2b, dz, "wgrad_ple_gate", S)
    gw["w_ple_proj"] = _wgrad(p, dpe, "wgrad_ple_proj", S)
    dff, dgate, dup, gs["post_ffn_norm"] = _ffn_down_bwd(dh2, ff, sm["post_ffn_norm"], w["w_down"], gate, up, S)
    gw["w_down"] = _wgrad(act, dff, "wgrad_down", S)
    if spread:
        gw["w_gate"] = _wgrad(dgate, hn, "wgrad_gate", S)
        gw["w_up"] = _wgrad(dup, hn, "wgrad_up", S)
    else:
        gw["w_gate"] = _wgrad(hn, dgate, "wgrad_gate", S)
        gw["w_up"] = _wgrad(hn, dup, "wgrad_up", S)
    g4 = [_by_chip(gw.pop(n), *((D_FF, D_MODEL, 0) if n in GRAD_TRANSPOSED else BIG_SPEC[n]))
          for n in REDUCE_EARLY] if spread else []
    dh1, dmix, dro, dmo, gs["pre_ffn_norm"], gs["post_mix_norm"], got = _ffn_up_bwd(
        dgate, dup, w["w_gate"], w["w_up"], h1, mix, dh2, sm["pre_ffn_norm"], sm["post_mix_norm"], w["w_o"], S, g4)
    sums = [_add_half_rows(g4[i], got[i], c_idx, "rs_add_halves_" + n) for i, n in enumerate(REDUCE_EARLY)] if spread else []
    gw["w_o"] = jnp.concatenate([_wgrad(ro, dmix, "wgrad_o_ret", S), _wgrad(mo, dmix, "wgrad_o_mla", S)], axis=0)

    dmo_t, delta = _attn_delta(mo, dmo, S)
    dqp, dkp, dv, parts = _flash_bwd(qp, kp, kt, v, dmo, dmo_t, lse, delta, S, sums)
    dqh, dkv, dcq, dckv, dkr, gs["mla_q_norm"], gs["mla_kv_norm"] = _mla_up_bwd(
        dqp, dkp, dv, cq, ckv, sm["mla_q_norm"], sm["mla_kv_norm"], w_uq_p, w_ukv_p, tabs, S)
    g_uq_p = _wgrad(cqn, dqh, "wgrad_uq", S)
    g_ukv_p = _wgrad(ckvn, dkv, "wgrad_ukv", S)
    gw["w_uq"] = g_uq_p.reshape(Q_LORA, MLA_HEADS, 128)[:, :, :96].reshape(Q_LORA, 768)
    gw["w_ukv"] = jnp.concatenate(
        [g_ukv_p[:, :1024].reshape(KV_LORA, MLA_HEADS, 128)[:, :, :64], g_ukv_p[:, 1024:].reshape(KV_LORA, MLA_HEADS, 64)],
        axis=2).reshape(KV_LORA, 1024)

    drq, drk, drv, drg, gs["ret_gn_w"] = _ret_bwd(rq, rk, rv, rprev, ry, rg, dro, sm["ret_gn_w"], tabs, S)
    grad_x, dproj, gs["pre_mix_norm"] = _inproj_bwd(drq, drk, drv, drg, dcq, dckv, dkr, w_in_p, dh1, x,
                                                    sm["pre_mix_norm"], S)
    g_in_p = _wgrad(xn, dproj, "wgrad_in", S)
    gw["w_in"] = jnp.concatenate([g_in_p[:, :2688], g_in_p[:, 2752:2784]], axis=1)
    return loss_vec, grad_x, gw, gs, ((sums, parts) if spread else None)


def _my_place():
    x = lax.axis_index("x")
    y = lax.axis_index("y")
    c = lax.axis_index("c")
    return x, y, c


def _other_chips(x, y):
    return [(1 - x, y), (x, 1 - y), (1 - x, 1 - y)]


_ANY = pl.BlockSpec(memory_space=pl.ANY)


def _small_copies(v_ref, slots, sems):
    send, recv, lsem = sems
    x, y, c = _my_place()
    me = 4 * x + 2 * y + c
    cps = [pltpu.make_async_copy(v_ref, slots.at[me], lsem)]
    for r in range(1, N_DEV):
        peer = (x ^ (r >> 2), y ^ ((r >> 1) & 1), c ^ (r & 1))
        cps.append(pltpu.make_async_remote_copy(
            src_ref=v_ref, dst_ref=slots.at[me], send_sem=send.at[r - 1], recv_sem=recv.at[r - 1],
            device_id=peer, device_id_type=MESH))
    return cps


def _small_sum(slots, out_ref):
    acc = slots[0]
    for d in range(1, N_DEV):
        acc = acc + slots[d]
    out_ref[...] = acc
    loss = jnp.sum(acc[9:10, :], axis=1, keepdims=True) * (0.5 / D_MODEL)
    out_ref[9:10, :] = jnp.broadcast_to(loss, (1, PACK_COLS))


def _small_scratch():
    return [pltpu.VMEM((N_DEV, SMALL_ROWS, PACK_COLS), F32), pltpu.SemaphoreType.DMA((N_DEV - 1,)),
            pltpu.SemaphoreType.DMA((N_DEV - 1,)), pltpu.SemaphoreType.DMA]


N_BIG = len(BIG)


def _half(c, rows, align):
    h = rows // 2
    return pl.ds(pl.multiple_of(c * h, align), h)


def _gather_out_shapes(shards):
    return [_sds((N_CHIPS,) + tuple(s.shape), BF16) for s in shards]


def _gather_sems(n):
    return [pltpu.SemaphoreType.DMA((n, 3))] * 4 + [pltpu.SemaphoreType.DMA((n,))] * 2


def _gather_phase(phase, ins, outs, sems):
    send1, recv1, send2, recv2, send3, recv3 = sems
    x, y, c = _my_place()
    me = 2 * x + y
    chips = _other_chips(x, y)
    sib = (x, y, 1 - c)
    for t in range(len(ins)):
        rows = ins[t].shape[0]
        half = _half(c, rows, 16)
        other = _half(1 - c, rows, 16)
        def own():
            return pltpu.make_async_remote_copy(
                src_ref=ins[t], dst_ref=outs[t].at[me], send_sem=send3.at[t], recv_sem=recv3.at[t],
                device_id=sib, device_id_type=MESH)

        if phase == 0:
            own().start()
        if phase == 2:
            own().wait()
        for k, (cx, cy) in enumerate(chips):
            src = 2 * cx + cy

            def over_ici(slab):
                return pltpu.make_async_remote_copy(
                    src_ref=ins[t].at[half], dst_ref=outs[t].at[slab, half], send_sem=send1.at[t, k],
                    recv_sem=recv1.at[t, k], device_id=(cx, cy, c), device_id_type=MESH)

            def over_d2d(rows):
                return pltpu.make_async_remote_copy(
                    src_ref=outs[t].at[src, rows], dst_ref=outs[t].at[src, rows], send_sem=send2.at[t, k],
                    recv_sem=recv2.at[t, k], device_id=sib, device_id_type=MESH)

            if phase == 0:
                over_ici(me).start()
            if phase == 1:
                over_ici(src).wait_recv()
                over_d2d(half).start()
            if phase == 2:
                over_d2d(other).wait_recv()
                over_ici(me).wait_send()
                over_d2d(half).wait_send()


def _swap_copies(ins, outs, sems):
    send, recv = sems
    x, y, c = _my_place()
    return [pltpu.make_async_remote_copy(
        src_ref=ins[t].at[:, _half(1 - c, ins[t].shape[1], 8)], dst_ref=outs[t], send_sem=send.at[t],
        recv_sem=recv.at[t], device_id=(x, y, 1 - c), device_id_type=MESH) for t in range(len(ins))]


def _swap_out_shapes(gs):
    return [_sds((N_CHIPS, g.shape[1] // 2, g.shape[2]), F32) for g in gs]


def _swap_sems(n):
    return [pltpu.SemaphoreType.DMA((n,)), pltpu.SemaphoreType.DMA((n,))]


def _swap_half_rows(gs):
    n = len(gs)

    def body(*refs):
        cps = _swap_copies(refs[:n], refs[n:2 * n], refs[2 * n:])
        for cp in cps:
            cp.start()
        for cp in cps:
            cp.wait()

    return pl.pallas_call(
        body, name="rs_swap_halves",
        in_specs=[_ANY] * n, out_specs=[_ANY] * n, out_shape=_swap_out_shapes(gs), scratch_shapes=_swap_sems(n),
    )(*gs)


def _add_half_rows(g, got, c_idx, name):
    _, rows, cols = g.shape
    h = rows // 2

    def body(c_ref, a_ref, b_ref, o_ref):
        o_ref[...] = (a_ref[...] + b_ref[...]).astype(BF16)

    grid_spec = pltpu.PrefetchScalarGridSpec(
        num_scalar_prefetch=1, grid=(N_CHIPS,),
        in_specs=[pl.BlockSpec((None, h, cols), lambda j, c: (j, c[0], 0)),
                  pl.BlockSpec((None, h, cols), lambda j, c: (j, 0, 0))],
        out_specs=pl.BlockSpec((None, h, cols), lambda j, c: (j, 0, 0)),
    )
    return pl.pallas_call(
        body, name=name, grid_spec=grid_spec, out_shape=_sds((N_CHIPS, h, cols), BF16),
        compiler_params=_cp(("parallel",)),
    )(c_idx, g, got)


def _scatter_to_chips(ts, vec):
    n = len(ts)

    def body(*refs):
        ins, v_ref, outs, small_ref = refs[:n], refs[n], refs[n + 1:2 * n + 1], refs[2 * n + 1]
        slots, small_sems, sems = refs[2 * n + 2], refs[2 * n + 3:2 * n + 6], refs[2 * n + 6:]
        small = _small_copies(v_ref, slots, small_sems)
        cps = _scatter_copies(ins, outs, sems)
        for cp in small + cps:
            cp.start()
        for cp in small:
            cp.wait()
        _small_sum(slots, small_ref)
        for cp in cps:
            cp.wait()

    vm = pl.BlockSpec(memory_space=pltpu.VMEM)
    *parts, small_sum = pl.pallas_call(
        body, name="rs_scatter_chips",
        in_specs=[_ANY] * n + [vm], out_specs=[_ANY] * n + [vm],
        out_shape=_scatter_out_shapes(ts) + [_sds((SMALL_ROWS, PACK_COLS), F32)],
        scratch_shapes=_small_scratch() + _scatter_sems(n),
    )(*ts, vec)
    return parts, small_sum


def _scatter_copies(ins, outs, sems):
    send, recv = sems
    x, y, c = _my_place()
    return [pltpu.make_async_remote_copy(
        src_ref=ins[t].at[2 * cx + cy], dst_ref=outs[t].at[k], send_sem=send.at[t, k], recv_sem=recv.at[t, k],
        device_id=(cx, cy, c), device_id_type=MESH)
        for t in range(len(ins)) for k, (cx, cy) in enumerate(_other_chips(x, y))]


def _scatter_out_shapes(ts):
    return [_sds((3,) + tuple(t.shape[1:]), BF16) for t in ts]


def _scatter_sems(n):
    return [pltpu.SemaphoreType.DMA((n, 3)), pltpu.SemaphoreType.DMA((n, 3))]


def _add_four(mine, parts, place, name):
    _, h, cols = parts.shape

    def body(pl_ref, m_ref, p_ref, o_ref):
        o_ref[...] = ((m_ref[...].astype(F32) + p_ref[0].astype(F32)) + p_ref[1].astype(F32)) + p_ref[2].astype(F32)

    grid_spec = pltpu.PrefetchScalarGridSpec(
        num_scalar_prefetch=1, grid=(1,),
        in_specs=[pl.BlockSpec((None, h, cols), lambda i, pc: (pc[0], 0, 0)),
                  pl.BlockSpec((3, h, cols), lambda i, pc: (0, 0, 0))],
        out_specs=pl.BlockSpec((h, cols), lambda i, pc: (pc[1], 0)),
    )
    return pl.pallas_call(
        body, name=name, grid_spec=grid_spec, out_shape=_sds((2 * h, cols), F32),
        compiler_params=_cp(("arbitrary",)),
    )(place, mine, parts)


def _join_half_rows(rs):
    n = len(rs)

    def body(*refs):
        ins, outs = refs[:n], refs[n:2 * n]
        send, recv = refs[2 * n:]
        x, y, c = _my_place()
        cps = []
        for t in range(n):
            half = _half(c, outs[t].shape[0], 8)
            rc = pltpu.make_async_remote_copy(
                src_ref=ins[t].at[half], dst_ref=outs[t].at[half], send_sem=send.at[t], recv_sem=recv.at[t],
                device_id=(x, y, 1 - c), device_id_type=MESH)
            rc.start()
            cps.append(rc)
        for cp in cps:
            cp.wait()

    return pl.pallas_call(
        body, name="rs_join_halves",
        in_specs=[_ANY] * n, out_specs=[_ANY] * n,
        out_shape=[_sds(r.shape, F32) for r in rs],
        input_output_aliases={i: i for i in range(n)},
        scratch_shapes=[pltpu.SemaphoreType.DMA((n,))] * 2,
    )(*rs)


def _by_chip(full, rows, cols, axis):
    if axis == 0:
        return full.reshape(N_CHIPS, rows // N_CHIPS, cols)
    return full.reshape(rows, N_CHIPS, cols // N_CHIPS).transpose(1, 0, 2)


def _from_chips(parts, axis):
    _, r, c = parts.shape
    if axis == 0:
        return parts.reshape(N_CHIPS * r, c)
    return parts.transpose(1, 0, 2).reshape(r, N_CHIPS * c)


def _adamw(wt, g, m, v, name):
    _, R, C = wt.shape
    tr = max(d for d in range(8, R + 1, 8) if R % d == 0 and (d * C <= 256 * 1024 or d == 8))

    def body(w_ref, g_ref, m_ref, v_ref, d_ref, nm_ref, nv_ref):
        gg = g_ref[...]
        m_new = ADAM_B1 * m_ref[...] + (1.0 - ADAM_B1) * gg
        v_new = ADAM_B2 * v_ref[...] + (1.0 - ADAM_B2) * (gg * gg)
        m_hat = m_new / (1.0 - ADAM_B1 ** ADAM_STEP)
        v_hat = v_new / (1.0 - ADAM_B2 ** ADAM_STEP)
        d_ref[...] = -ADAM_LR * (m_hat / (jnp.sqrt(v_hat) + ADAM_EPS) + ADAM_WD * w_ref[...])
        nm_ref[...] = m_new
        nv_ref[...] = v_new

    spec = pl.BlockSpec((None, tr, C), lambda i: (0, i, 0))
    return pl.pallas_call(
        body, name=name, grid=(R // tr,), in_specs=[spec, pl.BlockSpec((tr, C), lambda i: (i, 0)), spec, spec],
        out_specs=[spec] * 3, out_shape=[_sds((1, R, C), F32)] * 3,
        compiler_params=_cp(("parallel",)),
    )(wt, g, m, v)


def _pack_small(vals, loss_vec=None):
    rows = [jnp.pad(vals[n].reshape(-1), (0, PACK_COLS - sz)) for n, sz in SMALL]
    rows.append(loss_vec.reshape(-1) if loss_vec is not None else jnp.zeros((PACK_COLS,), F32))
    rows += [jnp.zeros((PACK_COLS,), F32)] * (SMALL_ROWS - len(rows))
    return jnp.stack(rows)


def kernel(x, p, positions, pre_mix_norm, w_in, ret_gn_w, mla_q_norm, w_uq, mla_kv_norm, w_ukv, w_o, post_mix_norm, pre_ffn_norm, w_gate, w_up, w_down, post_ffn_norm, w_ple_proj, ple_norm, w_ple_gate, b_ple_gate, loss_target, m_pre_mix_norm, m_w_in, m_ret_gn_w, m_mla_q_norm, m_w_uq, m_mla_kv_norm, m_w_ukv, m_w_o, m_post_mix_norm, m_pre_ffn_norm, m_w_gate, m_w_up, m_w_down, m_post_ffn_norm, m_w_ple_proj, m_ple_norm, m_w_ple_gate, m_b_ple_gate, v_pre_mix_norm, v_w_in, v_ret_gn_w, v_mla_q_norm, v_w_uq, v_mla_kv_norm, v_w_ukv, v_w_o, v_post_mix_norm, v_pre_ffn_norm, v_w_gate, v_w_up, v_w_down, v_post_ffn_norm, v_w_ple_proj, v_ple_norm, v_w_ple_gate, v_b_ple_gate):
    wts = dict(pre_mix_norm=pre_mix_norm, w_in=w_in, ret_gn_w=ret_gn_w, mla_q_norm=mla_q_norm, w_uq=w_uq,
               mla_kv_norm=mla_kv_norm, w_ukv=w_ukv, w_o=w_o, post_mix_norm=post_mix_norm, pre_ffn_norm=pre_ffn_norm,
               w_gate=w_gate, w_up=w_up, w_down=w_down, post_ffn_norm=post_ffn_norm, w_ple_proj=w_ple_proj,
               ple_norm=ple_norm, w_ple_gate=w_ple_gate, b_ple_gate=b_ple_gate)
    mom = dict(pre_mix_norm=m_pre_mix_norm, w_in=m_w_in, ret_gn_w=m_ret_gn_w, mla_q_norm=m_mla_q_norm, w_uq=m_w_uq,
               mla_kv_norm=m_mla_kv_norm, w_ukv=m_w_ukv, w_o=m_w_o, post_mix_norm=m_post_mix_norm,
               pre_ffn_norm=m_pre_ffn_norm, w_gate=m_w_gate, w_up=m_w_up, w_down=m_w_down, post_ffn_norm=m_post_ffn_norm,
               w_ple_proj=m_w_ple_proj, ple_norm=m_ple_norm, w_ple_gate=m_w_ple_gate, b_ple_gate=m_b_ple_gate)
    var = dict(pre_mix_norm=v_pre_mix_norm, w_in=v_w_in, ret_gn_w=v_ret_gn_w, mla_q_norm=v_mla_q_norm, w_uq=v_w_uq,
               mla_kv_norm=v_mla_kv_norm, w_ukv=v_w_ukv, w_o=v_w_o, post_mix_norm=v_post_mix_norm,
               pre_ffn_norm=v_pre_ffn_norm, w_gate=v_w_gate, w_up=v_w_up, w_down=v_w_down, post_ffn_norm=v_post_ffn_norm,
               w_ple_proj=v_w_ple_proj, ple_norm=v_ple_norm, w_ple_gate=v_w_ple_gate, b_ple_gate=v_b_ple_gate)

    S = x.shape[1]
    shard2d = {n: wts[n][0] for n, _, _, _ in BIG}
    small2d = {n: wts[n] for n, _ in SMALL}

    shard_bf = {n: (jnp.swapaxes(wts[n], 1, 2)[0] if n in GRAD_TRANSPOSED else shard2d[n]).astype(BF16) for n in shard2d}
    pos_f = positions.astype(F32).reshape(S, 1)
    c_idx = lax.axis_index("c").astype(jnp.int32).reshape(1)
    loss_vec, grad_x, gw, gs, (sums_early, parts_early) = _local_step(
        x[0], p[0, 0], pos_f, loss_target[0], {}, small2d, shard_bf, c_idx)

    g4 = [_by_chip(gw[n], *BIG_SPEC[n]) for n in REDUCE_LAST]
    got = _swap_half_rows(g4)
    sums_last = [_add_half_rows(g4[i], got[i], c_idx, "rs_add_halves_" + n) for i, n in enumerate(REDUCE_LAST)]
    parts_last, small_sum = _scatter_to_chips(sums_last, _pack_small(gs, loss_vec))
    place = jnp.stack([2 * lax.axis_index("x") + lax.axis_index("y"), lax.axis_index("c")]).astype(jnp.int32)
    names = REDUCE_EARLY + REDUCE_LAST
    reduced = _join_half_rows(
        [_add_four(sm_, pt_, place, "rs_add_chips_" + n)
         for n, sm_, pt_ in zip(names, sums_early + sums_last, list(parts_early) + list(parts_last))])
    g_shard = dict(zip(names, reduced))

    loss = small_sum[9, 0]
    g_small = {n: small_sum[i:i + 1, :sz] for i, (n, sz) in enumerate(SMALL)}

    grads, delta, new_m, new_v = {}, {}, {}, {}
    for n, _, _, _ in BIG:
        if n in COLUMN_MAJOR:
            turn = lambda a: jnp.swapaxes(a, 1, 2)
            g_t = g_shard[n] if n in GRAD_TRANSPOSED else g_shard[n].T
            d, nm, nv = _adamw(turn(wts[n]), g_t, turn(mom[n]), turn(var[n]), "adamw_" + n)
            grads[n], delta[n], new_m[n], new_v[n] = turn(g_t[None]), turn(d), turn(nm), turn(nv)
        else:
            delta[n], new_m[n], new_v[n] = _adamw(wts[n], g_shard[n], mom[n], var[n], "adamw_" + n)
            grads[n] = g_shard[n][None]
    d, nm, nv = _adamw(_pack_small(small2d)[None], small_sum, _pack_small(mom)[None], _pack_small(var)[None],
                       "adamw_small")
    for i, (n, sz) in enumerate(SMALL):
        grads[n] = g_small[n]
        delta[n], new_m[n], new_v[n] = d[0, i:i + 1, :sz], nm[0, i:i + 1, :sz], nv[0, i:i + 1, :sz]

    return (loss, grad_x[None], *[grads[n] for n in ALL_W], *[delta[n] for n in ALL_W],
            *[new_m[n] for n in ALL_W], *[new_v[n] for n in ALL_W])
```

```python
import functools
import math

import jax
import jax.numpy as jnp
import numpy as np
from jax import lax
from jax.experimental import pallas as pl
from jax.experimental.pallas import tpu as pltpu

F32 = jnp.float32
BF16 = jnp.bfloat16
MESH = pl.DeviceIdType.MESH

D_MODEL = 1024
D_FF = 2816
PLE_DIM = 256
RET_HEADS = 4
RET_DIM = 128
RET_WIDTH = 512
RET_CHUNK = 256
RET_GROUP = 4
MLA_HEADS = 8
MLA_NOPE = 64
MLA_ROPE = 32
MLA_V = 64
Q_LORA = 384
KV_LORA = 256
IN_COLS = 2720
IN_COLS_P = 2816
ROPE_BASE = 10000.0
EPS = 1e-6
SCALE_MLA = 1.0 / math.sqrt(MLA_NOPE + MLA_ROPE)
SCALE_RET = RET_DIM ** -0.5
NEG = -1e30

ADAM_LR = 0.001
ADAM_B1 = 0.9
ADAM_B2 = 0.999
ADAM_EPS = 1e-08
ADAM_WD = 0.01
ADAM_STEP = 10

N_CHIPS = 4
N_DEV = 8
VMEM_MB = 56

BIG = (
    ("w_in", 1024, 2720, 1),
    ("w_uq", 384, 768, 1),
    ("w_ukv", 256, 1024, 1),
    ("w_o", 1024, 1024, 0),
    ("w_gate", 1024, 2816, 1),
    ("w_up", 1024, 2816, 1),
    ("w_down", 2816, 1024, 0),
    ("w_ple_proj", 256, 1024, 1),
    ("w_ple_gate", 1024, 1024, 0),
)
SMALL = (
    ("pre_mix_norm", 1024),
    ("ret_gn_w", 512),
    ("mla_q_norm", 384),
    ("mla_kv_norm", 256),
    ("post_mix_norm", 1024),
    ("pre_ffn_norm", 1024),
    ("post_ffn_norm", 1024),
    ("ple_norm", 1024),
    ("b_ple_gate", 1024),
)
ALL_W = ("pre_mix_norm", "w_in", "ret_gn_w", "mla_q_norm", "w_uq", "mla_kv_norm", "w_ukv", "w_o", "post_mix_norm",
         "pre_ffn_norm", "w_gate", "w_up", "w_down", "post_ffn_norm", "w_ple_proj", "ple_norm", "w_ple_gate", "b_ple_gate")
PACK_COLS = 1024
SMALL_ROWS = 16


def _cp(sem=None, mb=VMEM_MB, **kw):
    return pltpu.CompilerParams(dimension_semantics=sem, vmem_limit_bytes=mb * 1024 * 1024, **kw)


def _bf(x):
    return x.astype(BF16)


def _dot(a, b):
    return jnp.dot(_bf(a), _bf(b), preferred_element_type=F32)


def _dot_nt(a, b):
    return lax.dot_general(_bf(a), _bf(b), (((1,), (1,)), ((), ())), preferred_element_type=F32)


def _dot_tn(a, b):
    return lax.dot_general(_bf(a), _bf(b), (((0,), (0,)), ((), ())), preferred_element_type=F32)


def _sig(x):
    return 1.0 / (1.0 + jnp.exp(-x))


def _rms(x, g):
    r = lax.rsqrt(jnp.mean(x * x, axis=-1, keepdims=True) + EPS)
    return x * r * g


def _rms_bwd(dy, x, g):
    r = lax.rsqrt(jnp.mean(x * x, axis=-1, keepdims=True) + EPS)
    xh = x * r
    dxh = dy * g
    dx = r * (dxh - xh * jnp.mean(dxh * xh, axis=-1, keepdims=True))
    return dx, dy * xh


def _colsum(x):
    return jnp.sum(x, axis=0, keepdims=True)


def _rope_ret(x, cr, sr):
    return x * cr + pltpu.roll(x, 64, 1) * sr


def _unrope_ret(dy, cr, sr):
    return dy * cr + pltpu.roll(dy * sr, 64, 1)


def _rope_mla(x, cm, sa, sb):
    return x * cm + pltpu.roll(x, 112, 1) * sa + pltpu.roll(x, 16, 1) * sb


def _unrope_mla(dy, cm, sa, sb):
    return dy * cm + pltpu.roll(dy * sa, 16, 1) + pltpu.roll(dy * sb, 112, 1)


def _rows(tm, w, col=0):
    return pl.BlockSpec((tm, w), lambda i: (i, col))


def _full(*shape):
    return pl.BlockSpec(shape, lambda i: (0,) * len(shape), pipeline_mode=pl.Buffered(1))


def _acc(*shape):
    return pl.BlockSpec(shape, lambda i: (0,) * len(shape))


def _sds(shape, dtype):
    return jax.ShapeDtypeStruct(shape, dtype)


def _rope_tables(pos_f, S, shards=()):
    tm = min(512, S)
    n = len(shards)
    steps = S // tm
    inv_r = (1.0 / (np.float32(ROPE_BASE) ** (np.arange(64, dtype=np.float32) / np.float32(64)))).astype(np.float32)
    inv_m16 = (1.0 / (np.float32(ROPE_BASE) ** (np.arange(16, dtype=np.float32) / np.float32(16)))).astype(np.float32)
    inv_r = np.concatenate([inv_r, inv_r])[None, :]
    inv_m = np.zeros((1, 128), np.float32)
    inv_m[0, 64:80] = inv_m16
    inv_m[0, 80:96] = inv_m16

    def body(pos_ref, invr_ref, invm_ref, *rest):
        w_ins, (cr_ref, sr_ref, cm_ref, sa_ref, sb_ref) = rest[:n], rest[n:n + 5]
        w_outs, sems = rest[n + 5:2 * n + 5], rest[2 * n + 5:]
        i = pl.program_id(0)
        if n:
            @pl.when(i == 0)
            def _():
                _gather_phase(0, w_ins, w_outs, sems)

            @pl.when(i == steps // 2)
            def _():
                _gather_phase(1, w_ins, w_outs, sems)

        pos = pos_ref[...]
        lane = lax.broadcasted_iota(jnp.int32, (tm, 128), 1)
        ar = pos * invr_ref[...]
        s = jnp.sin(ar)
        cr_ref[...] = jnp.cos(ar)
        sr_ref[...] = jnp.where(lane < 64, -s, s)
        am = pos * invm_ref[...]
        c2 = jnp.cos(am)
        s2 = jnp.sin(am)
        cm_ref[...] = jnp.where(lane < 64, 1.0, jnp.where(lane < 96, c2, 0.0))
        sa_ref[...] = jnp.where((lane >= 64) & (lane < 80), -s2, 0.0)
        sb_ref[...] = jnp.where((lane >= 80) & (lane < 96), s2, 0.0)

        if n:
            @pl.when(i == steps - 1)
            def _():
                _gather_phase(2, w_ins, w_outs, sems)

    outs = pl.pallas_call(
        body, name="rope_tables", grid=(steps,),
        in_specs=[_rows(tm, 1), _full(1, 128), _full(1, 128)] + [_ANY] * n,
        out_specs=[_rows(tm, 128)] * 5 + [_ANY] * n,
        out_shape=[_sds((S, 128), F32)] * 5 + _gather_out_shapes(shards),
        scratch_shapes=_gather_sems(n) if n else [],
        compiler_params=_cp(("arbitrary",)),
    )(pos_f, jnp.asarray(inv_r), jnp.asarray(inv_m), *shards)
    return outs[:5], outs[5:]


def _inproj(x, g, w_in, tabs, S):
    tm = min(512, S)

    def body(x_ref, g_ref, w_ref, cr_ref, sr_ref, cm_ref, sa_ref, sb_ref,
             xn_ref, rq_ref, rk_ref, rv_ref, rg_ref, cq_ref, ckv_ref, kr_ref):
        xb = _rms(x_ref[...], g_ref[...]).astype(BF16)
        xn_ref[...] = xb
        cr = cr_ref[...]
        sr = sr_ref[...]
        q = jnp.dot(xb, w_ref[:, 0:512], preferred_element_type=F32)
        k = jnp.dot(xb, w_ref[:, 512:1024], preferred_element_type=F32)
        for h in range(RET_HEADS):
            sl = slice(h * 128, (h + 1) * 128)
            rq_ref[:, sl] = _rope_ret(q[:, sl], cr, sr).astype(BF16)
            rk_ref[:, sl] = (_rope_ret(k[:, sl], cr, sr) * SCALE_RET).astype(BF16)
        rv_ref[...] = jnp.dot(xb, w_ref[:, 1024:1536], preferred_element_type=F32).astype(BF16)
        rg_ref[...] = jnp.dot(xb, w_ref[:, 1536:2048], preferred_element_type=F32)
        cq_ref[...] = jnp.dot(xb, w_ref[:, 2048:2432], preferred_element_type=F32)
        ckv_ref[...] = jnp.dot(xb, w_ref[:, 2432:2688], preferred_element_type=F32)
        kr = jnp.dot(xb, w_ref[:, 2688:2816], preferred_element_type=F32)
        kr_ref[...] = _rope_mla(kr, cm_ref[...], sa_ref[...], sb_ref[...])

    return pl.pallas_call(
        body, name="inproj", grid=(S // tm,),
        in_specs=[_rows(tm, D_MODEL), _full(1, D_MODEL), _full(D_MODEL, IN_COLS_P)] + [_rows(tm, 128)] * 5,
        out_specs=[_rows(tm, D_MODEL)] + [_rows(tm, 512)] * 4 + [_rows(tm, Q_LORA), _rows(tm, KV_LORA), _rows(tm, 128)],
        out_shape=[_sds((S, D_MODEL), BF16)] + [_sds((S, 512), BF16)] * 3
        + [_sds((S, 512), F32), _sds((S, Q_LORA), F32), _sds((S, KV_LORA), F32), _sds((S, 128), F32)],
        compiler_params=_cp(("parallel",)),
    )(x, g, w_in, *tabs)


def _mla_up(cq, ckv, kr, gq, gkv, w_uq, w_ukv, tabs, S):
    tm = min(512, S)

    def body(cq_ref, ckv_ref, kr_ref, gq_ref, gkv_ref, wuq_ref, wukv_ref, cm_ref, sa_ref, sb_ref,
             cqn_ref, ckvn_ref, qp_ref, kp_ref, v_ref, kt_ref, vt_ref):
        cm = cm_ref[...]
        sa = sa_ref[...]
        sb = sb_ref[...]
        cqn = _rms(cq_ref[...], gq_ref[...]).astype(BF16)
        cqn_ref[...] = cqn
        ckvn = _rms(ckv_ref[...], gkv_ref[...]).astype(BF16)
        ckvn_ref[...] = ckvn
        qh = jnp.dot(cqn, wuq_ref[...], preferred_element_type=F32)
        kv = jnp.dot(ckvn, wukv_ref[...], preferred_element_type=F32)
        kr_blk = kr_ref[...]
        for h in range(MLA_HEADS):
            sl = slice(h * 128, (h + 1) * 128)
            qp_ref[:, sl] = (_rope_mla(qh[:, sl], cm, sa, sb) * SCALE_MLA).astype(BF16)
            kh = kv[:, sl] + kr_blk
            kp_ref[:, sl] = kh.astype(BF16)
            kt_ref[sl, :] = kh.T.astype(BF16)
        for h in range(MLA_HEADS // 2):
            vh = kv[:, 1024 + h * 128:1024 + (h + 1) * 128]
            v_ref[:, h * 128:(h + 1) * 128] = vh.astype(BF16)
            vt_ref[h * 128:(h + 1) * 128, :] = vh.T.astype(BF16)

    cols = lambda r: pl.BlockSpec((r, tm), lambda i: (0, i))
    return pl.pallas_call(
        body, name="mla_up", grid=(S // tm,),
        in_specs=[_rows(tm, Q_LORA), _rows(tm, KV_LORA), _rows(tm, 128), _full(1, Q_LORA), _full(1, KV_LORA),
                  _full(Q_LORA, 1024), _full(KV_LORA, 1536)] + [_rows(tm, 128)] * 3,
        out_specs=[_rows(tm, Q_LORA), _rows(tm, KV_LORA), _rows(tm, 1024), _rows(tm, 1024), _rows(tm, 512),
                   cols(1024), cols(512)],
        out_shape=[_sds((S, Q_LORA), BF16), _sds((S, KV_LORA), BF16), _sds((S, 1024), BF16), _sds((S, 1024), BF16),
                   _sds((S, 512), BF16), _sds((1024, S), BF16), _sds((512, S), BF16)],
        compiler_params=_cp(("parallel",)),
    )(cq, ckv, kr, gq, gkv, w_uq, w_ukv, *tabs[2:])


def _tri_pairs(nq, k_major):
    if k_major:
        pairs = [(qb, kb) for kb in range(nq) for qb in range(kb, nq)]
    else:
        pairs = [(qb, kb) for qb in range(nq) for kb in range(qb + 1)]
    qb_of = np.array([p[0] for p in pairs], np.int32)
    kb_of = np.array([p[1] for p in pairs], np.int32)
    return jnp.asarray(qb_of), jnp.asarray(kb_of), len(pairs)


ATT_ROWS = 32
FWD_HEADS = 8
BWD_HEADS = 4


def _causal_keep(r0, rows, tq):
    key = r0 + lax.broadcasted_iota(jnp.int32, (rows, tq), 0)
    qry = lax.broadcasted_iota(jnp.int32, (rows, tq), 1)
    return key <= qry


def _flash_fwd(qp, kp, vt, S, shards=()):
    tq = min(512, S)
    nq = S // tq
    RB = ATT_ROWS
    NH = FWD_HEADS
    qb_of, kb_of, T = _tri_pairs(nq, k_major=False)
    n = len(shards)
    steps = (MLA_HEADS // NH) * T

    def body(qb_ref, kb_ref, q_ref, k_ref, vt_ref, *rest):
        w_ins, (o_ref, lse_ref), w_outs = rest[:n], rest[n:n + 2], rest[n + 2:2 * n + 2]
        m_sc, l_sc, acc_sc, s_sc, p_sc = rest[2 * n + 2:2 * n + 7]
        sems = rest[2 * n + 7:]
        t = pl.program_id(1)
        qb = qb_ref[t]
        kb = kb_ref[t]
        lin = pl.program_id(0) * T + t

        if n:
            @pl.when(lin == 0)
            def _():
                _gather_phase(0, w_ins, w_outs, sems)

            @pl.when(lin == steps // 2)
            def _():
                _gather_phase(1, w_ins, w_outs, sems)

        @pl.when(kb == 0)
        def _():
            m_sc[...] = jnp.full(m_sc.shape, NEG, F32)
            l_sc[...] = jnp.zeros(l_sc.shape, F32)
            acc_sc[...] = jnp.zeros(acc_sc.shape, F32)

        def scores(a):
            sl = slice(a * 128, (a + 1) * 128)
            s_sc[a] = _dot_nt(k_ref[:, sl], q_ref[:, sl])

        def step(masked):
            for a in range(NH):
                scores(a)
            for a in range(NH):
                mx = [jnp.full((8, tq), NEG, F32) for _ in range(RB // 8)]
                for r in range(0, tq, RB):
                    sc = s_sc[a, r:r + RB, :]
                    if masked:
                        sc = jnp.where(_causal_keep(r, RB, tq), sc, NEG)
                        s_sc[a, r:r + RB, :] = sc
                    for i in range(RB // 8):
                        mx[i] = jnp.maximum(mx[i], sc[i * 8:(i + 1) * 8, :])
                mx8 = functools.reduce(jnp.maximum, mx)
                m_prev = m_sc[a]
                m_new = jnp.maximum(m_prev, jnp.max(mx8, axis=0, keepdims=True))
                al = jnp.exp(m_prev - m_new)
                m_sc[a] = m_new
                ls = [jnp.zeros((8, tq), F32) for _ in range(RB // 8)]
                for r in range(0, tq, RB):
                    p = jnp.exp(s_sc[a, r:r + RB, :] - m_new)
                    for i in range(RB // 8):
                        ls[i] = ls[i] + p[i * 8:(i + 1) * 8, :]
                    p_sc[a, r:r + RB, :] = p.astype(BF16)
                l_sc[a] = al * l_sc[a] + jnp.sum(functools.reduce(jnp.add, ls), axis=0, keepdims=True)
                pair = slice((a // 2) * 128, (a // 2 + 1) * 128)
                pv = jnp.dot(vt_ref[pair, :], p_sc[a], preferred_element_type=F32)
                rs = slice(a * 64, (a + 1) * 64)
                own = slice((a % 2) * 64, (a % 2 + 1) * 64)
                acc_sc[rs, :] = acc_sc[rs, :] * al + pv[own, :]

        @pl.when(kb < qb)
        def _():
            step(False)

        @pl.when(kb == qb)
        def _():
            step(True)
            for a in range(NH):
                rs = slice(a * 64, (a + 1) * 64)
                acc_sc[rs, :] = acc_sc[rs, :] / l_sc[a]
                lse_ref[a:a + 1, :] = m_sc[a] + jnp.log(l_sc[a])
            o_ref[...] = acc_sc[...].T.astype(BF16)

        if n:
            @pl.when(lin == steps - 1)
            def _():
                _gather_phase(2, w_ins, w_outs, sems)

    grid_spec = pltpu.PrefetchScalarGridSpec(
        num_scalar_prefetch=2, grid=(MLA_HEADS // NH, T),
        in_specs=[pl.BlockSpec((tq, 128 * NH), lambda j, t, qb, kb: (qb[t], j)),
                  pl.BlockSpec((tq, 128 * NH), lambda j, t, qb, kb: (kb[t], j)),
                  pl.BlockSpec((64 * NH, tq), lambda j, t, qb, kb: (j, kb[t]))] + [_ANY] * n,
        out_specs=[pl.BlockSpec((tq, 64 * NH), lambda j, t, qb, kb: (qb[t], j)),
                   pl.BlockSpec((None, NH, tq), lambda j, t, qb, kb: (j, 0, qb[t]))] + [_ANY] * n,
        scratch_shapes=[pltpu.VMEM((NH, 1, tq), F32), pltpu.VMEM((NH, 1, tq), F32), pltpu.VMEM((64 * NH, tq), F32),
                        pltpu.VMEM((NH, tq, tq), F32), pltpu.VMEM((NH, tq, tq), BF16)] + (_gather_sems(n) if n else []),
    )
    out, lse, *gathered = pl.pallas_call(
        body, name="flash_fwd", grid_spec=grid_spec,
        out_shape=[_sds((S, 512), BF16), _sds((MLA_HEADS // NH, NH, S), F32)] + _gather_out_shapes(shards),
        compiler_params=_cp(("arbitrary", "arbitrary")),
    )(qb_of, kb_of, qp, kp, vt, *shards)
    return out, lse.reshape(MLA_HEADS // 2, 2, S), gathered


def _decay_table():
    log_g = np.log(1.0 - 2.0 ** (-5.0 - np.arange(RET_HEADS, dtype=np.float32))).astype(np.float32)
    return jnp.asarray(np.broadcast_to(log_g[:, None, None], (RET_HEADS, 8, 128)).copy())


def _decay_terms(lg_ref):
    C = RET_CHUNK
    lg = lg_ref[0:1, :]
    row = lax.broadcasted_iota(jnp.int32, (C, C), 0)
    col = lax.broadcasted_iota(jnp.int32, (C, C), 1)
    diff = (row - col).astype(F32)
    dmat = jnp.where(diff >= 0, jnp.exp(jnp.maximum(diff, 0.0) * jnp.tile(lg, (1, C // 128))), 0.0)
    j = lax.broadcasted_iota(jnp.int32, (C, 1), 0).astype(F32)
    lg1 = lg[:, 0:1]
    zeta = jnp.exp((C - 1 - j) * lg1)
    xi = jnp.exp((j + 1.0) * lg1)
    g_chunk = jnp.exp(C * lg1)
    return dmat, zeta, xi, g_chunk


def _ret_fwd(rq, rk, rv, rg, gn_w, S):
    C = RET_CHUNK
    N = S // C
    G = min(RET_GROUP, N)
    NB = N // G

    def body(lg_ref, q_ref, k_ref, v_ref, rg_ref, w_ref, ry_ref, ro_ref, rprev_ref, r_sc):
        @pl.when(pl.program_id(1) == 0)
        def _():
            r_sc[...] = jnp.zeros(r_sc.shape, F32)

        dmat, zeta, xi, g_chunk = _decay_terms(lg_ref)
        w = w_ref[...]
        r = r_sc[...]
        for i in range(G):
            rows = slice(i * C, (i + 1) * C)
            q = q_ref[rows, :]
            k = k_ref[rows, :]
            v = v_ref[rows, :]
            r_prev = r.astype(BF16)
            rprev_ref[i] = r_prev
            sc = _dot_nt(q, k) * dmat
            ry = _dot(sc, v) + jnp.dot(q, r_prev, preferred_element_type=F32) * xi
            ry_ref[rows, :] = ry
            r = g_chunk * r + _dot_tn(k, zeta * v.astype(F32))
            mu = jnp.mean(ry, axis=-1, keepdims=True)
            yc = ry - mu
            yh = yc * lax.rsqrt(jnp.mean(yc * yc, axis=-1, keepdims=True) + EPS)
            g = rg_ref[rows, :]
            ro_ref[rows, :] = (g * _sig(g) * (yh * w)).astype(BF16)
        r_sc[...] = r

    blk = pl.BlockSpec((G * C, 128), lambda h, n: (n, h))
    return pl.pallas_call(
        body, name="ret_fwd", grid=(RET_HEADS, NB),
        in_specs=[pl.BlockSpec((None, 8, 128), lambda h, n: (h, 0, 0)), blk, blk, blk, blk,
                  pl.BlockSpec((1, 128), lambda h, n: (0, h))],
        out_specs=[blk, blk, pl.BlockSpec((G, 128, 128), lambda h, n: (h * NB + n, 0, 0))],
        out_shape=[_sds((S, 512), F32), _sds((S, 512), BF16), _sds((RET_HEADS * N, 128, 128), BF16)],
        scratch_shapes=[pltpu.VMEM((128, 128), F32)],
        compiler_params=_cp(("parallel", "arbitrary")),
    )(_decay_table(), rq, rk, rv, rg, gn_w)


def _outproj(ro, mo, x, w_o, g_post, g_pre, S):
    tm = min(512, S)

    def body(ro_ref, mo_ref, x_ref, wo_ref, g1_ref, g2_ref, mix_ref, h1_ref, hn_ref):
        mix = (jnp.dot(ro_ref[...], wo_ref[0:512, :], preferred_element_type=F32)
               + jnp.dot(mo_ref[...], wo_ref[512:1024, :], preferred_element_type=F32))
        mix_ref[...] = mix.astype(BF16)
        h1 = x_ref[...] + _rms(mix, g1_ref[...])
        h1_ref[...] = h1
        hn_ref[...] = _rms(h1, g2_ref[...]).astype(BF16)

    return pl.pallas_call(
        body, name="outproj", grid=(S // tm,),
        in_specs=[_rows(tm, 512), _rows(tm, 512), _rows(tm, D_MODEL), _full(D_MODEL, D_MODEL), _full(1, D_MODEL),
                  _full(1, D_MODEL)],
        out_specs=[_rows(tm, D_MODEL)] * 3,
        out_shape=[_sds((S, D_MODEL), BF16), _sds((S, D_MODEL), F32), _sds((S, D_MODEL), BF16)],
        compiler_params=_cp(("parallel",)),
    )(ro, mo, x, w_o, g_post, g_pre)


def _ffn_up(hn, w_gate_t, w_up_t, S):
    tm = min(512, S)
    tn = D_FF // 2

    def body(hn_ref, wg_ref, wu_ref, fg_ref, fu_ref, act_ref):
        hn_b = hn_ref[...]
        g = _dot_nt(hn_b, wg_ref[...])
        u = _dot_nt(hn_b, wu_ref[...])
        s = _sig(g)
        silu = g * s
        fg_ref[...] = (u * (s + silu * (1.0 - s))).astype(BF16)
        fu_ref[...] = silu.astype(BF16)
        act_ref[...] = (silu * u).astype(BF16)

    wspec = pl.BlockSpec((tn, D_MODEL), lambda j, i: (j, 0))
    ospec = pl.BlockSpec((tm, tn), lambda j, i: (i, j))
    return pl.pallas_call(
        body, name="ffn_up", grid=(2, S // tm),
        in_specs=[pl.BlockSpec((tm, D_MODEL), lambda j, i: (i, 0)), wspec, wspec],
        out_specs=[ospec] * 3, out_shape=[_sds((S, D_FF), BF16)] * 3,
        compiler_params=_cp(("parallel", "parallel")),
    )(hn, w_gate_t, w_up_t)


def _ffn_down(act, w_down, h1, g, S):
    tm = min(512, S)

    def body(act_ref, wd_ref, h1_ref, g_ref, ff_ref, h2_ref):
        ff = jnp.dot(act_ref[...], wd_ref[...], preferred_element_type=F32)
        ff_ref[...] = ff.astype(BF16)
        h2_ref[...] = h1_ref[...] + _rms(ff, g_ref[...])

    return pl.pallas_call(
        body, name="ffn_down", grid=(S // tm,),
        in_specs=[_rows(tm, D_FF), _full(D_FF, D_MODEL), _rows(tm, D_MODEL), _full(1, D_MODEL)],
        out_specs=[_rows(tm, D_MODEL)] * 2, out_shape=[_sds((S, D_MODEL), BF16), _sds((S, D_MODEL), F32)],
        compiler_params=_cp(("parallel",)),
    )(act, w_down, h1, g)


def _ple_loss(p, h2, tgt, w_pp, w_pg, b_pg, g_ple, S):
    tm = min(512, S)

    def body(p_ref, h2_ref, t_ref, wp_ref, wg_ref, b_ref, gp_ref,
             dz_ref, dpe_ref, dh2_ref, h2b_ref, loss_ref, dgp_ref, db_ref):
        @pl.when(pl.program_id(0) == 0)
        def _():
            loss_ref[...] = jnp.zeros(loss_ref.shape, F32)
            dgp_ref[...] = jnp.zeros(dgp_ref.shape, F32)
            db_ref[...] = jnp.zeros(db_ref.shape, F32)

        gp = gp_ref[...]
        pe = _dot(p_ref[...], wp_ref[...])
        r = lax.rsqrt(jnp.mean(pe * pe, axis=-1, keepdims=True) + EPS)
        peh = pe * r
        e = peh * gp
        h2 = h2_ref[...]
        h2b = h2.astype(BF16)
        h2b_ref[...] = h2b
        gt = _sig(jnp.dot(h2b, wg_ref[...], preferred_element_type=F32) + b_ref[...])
        diff = h2 + e * gt - t_ref[...]
        loss_ref[...] += _colsum(diff * diff)
        dh3 = diff * (1.0 / D_MODEL)
        de = dh3 * gt
        dz = dh3 * e * gt * (1.0 - gt)
        db_ref[...] += _colsum(dz)
        dgp_ref[...] += _colsum(de * peh)
        dpeh = de * gp
        dpe = r * (dpeh - peh * jnp.mean(dpeh * peh, axis=-1, keepdims=True))
        dzb = dz.astype(BF16)
        dz_ref[...] = dzb
        dpe_ref[...] = dpe.astype(BF16)
        dh2_ref[...] = dh3 + _dot_nt(dzb, wg_ref[...])

    return pl.pallas_call(
        body, name="ple_loss", grid=(S // tm,),
        in_specs=[_rows(tm, PLE_DIM), _rows(tm, D_MODEL), _rows(tm, D_MODEL), _full(PLE_DIM, D_MODEL),
                  _full(D_MODEL, D_MODEL), _full(1, D_MODEL), _full(1, D_MODEL)],
        out_specs=[_rows(tm, D_MODEL)] * 4 + [_acc(1, D_MODEL)] * 3,
        out_shape=[_sds((S, D_MODEL), BF16), _sds((S, D_MODEL), BF16), _sds((S, D_MODEL), F32), _sds((S, D_MODEL), BF16)]
        + [_sds((1, D_MODEL), F32)] * 3,
        compiler_params=_cp(("arbitrary",)),
    )(p, h2, tgt, w_pp, w_pg, b_pg, g_ple)


def _wgrad(a, b, name, S):
    M = a.shape[1]
    N = b.shape[1]
    ts = min(2048, S)
    nsplit = 2 if M * N >= 2 * 1024 * 1024 else 1
    tn = N // nsplit

    def body(a_ref, b_ref, o_ref):
        @pl.when(pl.program_id(1) == 0)
        def _():
            o_ref[...] = jnp.zeros(o_ref.shape, F32)

        o_ref[...] += _dot_tn(a_ref[...], b_ref[...])

    return pl.pallas_call(
        body, name=name, grid=(nsplit, S // ts),
        in_specs=[pl.BlockSpec((ts, M), lambda j, s: (s, 0)), pl.BlockSpec((ts, tn), lambda j, s: (s, j))],
        out_specs=pl.BlockSpec((M, tn), lambda j, s: (0, j)), out_shape=_sds((M, N), F32),
        compiler_params=_cp(("parallel", "arbitrary")),
    )(a, b)


def _ffn_down_bwd(dh2, ff, g, w_down, dgate_f, dup_f, S):
    tm = min(512, S)
    tn = D_FF // 2

    def body(dh2_ref, ff_ref, g_ref, wd_ref, fg_ref, fu_ref, dff_ref, dgate_ref, dup_ref, dg_ref):
        @pl.when(pl.program_id(0) == 0)
        def _():
            dg_ref[...] = jnp.zeros(dg_ref.shape, F32)

        dff, ga = _rms_bwd(dh2_ref[...], ff_ref[...].astype(F32), g_ref[...])
        dg_ref[...] += _colsum(ga)
        dffb = dff.astype(BF16)
        dff_ref[...] = dffb
        for seg in range(2):
            sl = slice(seg * tn, (seg + 1) * tn)
            dact = _dot_nt(dffb, wd_ref[sl, :])
            dgate_ref[:, sl] = (dact * fg_ref[:, sl].astype(F32)).astype(BF16)
            dup_ref[:, sl] = (dact * fu_ref[:, sl].astype(F32)).astype(BF16)

    return pl.pallas_call(
        body, name="ffn_down_bwd", grid=(S // tm,),
        in_specs=[_rows(tm, D_MODEL), _rows(tm, D_MODEL), _full(1, D_MODEL), _full(D_FF, D_MODEL), _rows(tm, D_FF),
                  _rows(tm, D_FF)],
        out_specs=[_rows(tm, D_MODEL), _rows(tm, D_FF), _rows(tm, D_FF), _acc(1, D_MODEL)],
        out_shape=[_sds((S, D_MODEL), BF16), _sds((S, D_FF), BF16), _sds((S, D_FF), BF16), _sds((1, D_MODEL), F32)],
        compiler_params=_cp(("arbitrary",)),
    )(dh2, ff, g, w_down, dgate_f, dup_f)


def _ffn_up_bwd(dgate, dup, w_gate, w_up, h1, mix, dh2, g_pre, g_post, w_o, S, grads=()):
    tm = min(512, S)
    n = len(grads)
    last = S // tm - 1

    def body(dgate_ref, dup_ref, wg_ref, wu_ref, h1_ref, mix_ref, dh2_ref, g2_ref, g1_ref, wo_ref, *rest):
        g_ins = rest[:n]
        dh1_ref, dmix_ref, dro_ref, dmo_ref, dg2_ref, dg1_ref = rest[n:n + 6]
        g_outs, sems = rest[n + 6:2 * n + 6], rest[2 * n + 6:]

        @pl.when(pl.program_id(0) == 0)
        def _():
            dg2_ref[...] = jnp.zeros(dg2_ref.shape, F32)
            dg1_ref[...] = jnp.zeros(dg1_ref.shape, F32)
            for cp in (_swap_copies(g_ins, g_outs, sems) if n else []):
                cp.start()

        dhn = (jnp.dot(dgate_ref[...], wg_ref[...], preferred_element_type=F32)
               + jnp.dot(dup_ref[...], wu_ref[...], preferred_element_type=F32))
        d1, ga = _rms_bwd(dhn, h1_ref[...], g2_ref[...])
        dg2_ref[...] += _colsum(ga)
        dh1 = dh2_ref[...] + d1
        dh1_ref[...] = dh1
        dmix, gb = _rms_bwd(dh1, mix_ref[...].astype(F32), g1_ref[...])
        dg1_ref[...] += _colsum(gb)
        dmixb = dmix.astype(BF16)
        dmix_ref[...] = dmixb
        dcat = _dot_nt(dmixb, wo_ref[...])
        dro_ref[...] = dcat[:, 0:512].astype(BF16)
        dmo_ref[...] = dcat[:, 512:1024].astype(BF16)

        if n:
            @pl.when(pl.program_id(0) == last)
            def _():
                for cp in _swap_copies(g_ins, g_outs, sems):
                    cp.wait()

    dh1, dmix, dro, dmo, dg2, dg1, *got = pl.pallas_call(
        body, name="ffn_up_bwd", grid=(S // tm,),
        in_specs=[_rows(tm, D_FF), _rows(tm, D_FF), _full(D_FF, D_MODEL), _full(D_FF, D_MODEL), _rows(tm, D_MODEL),
                  _rows(tm, D_MODEL), _rows(tm, D_MODEL), _full(1, D_MODEL), _full(1, D_MODEL), _full(D_MODEL, D_MODEL)]
        + [_ANY] * n,
        out_specs=[_rows(tm, D_MODEL), _rows(tm, D_MODEL), _rows(tm, 512), _rows(tm, 512), _acc(1, D_MODEL),
                   _acc(1, D_MODEL)] + [_ANY] * n,
        out_shape=[_sds((S, D_MODEL), F32), _sds((S, D_MODEL), BF16), _sds((S, 512), BF16), _sds((S, 512), BF16),
                   _sds((1, D_MODEL), F32), _sds((1, D_MODEL), F32)] + _swap_out_shapes(grads),
        scratch_shapes=_swap_sems(n) if n else [],
        compiler_params=_cp(("arbitrary",)),
    )(dgate, dup, w_gate, w_up, h1, mix, dh2, g_pre, g_post, w_o, *grads)
    return dh1, dmix, dro, dmo, dg2, dg1, got


def _attn_delta(o, do, S):
    tm = min(512, S)

    def body(o_ref, do_ref, dot_ref, d_ref):
        do = do_ref[...].astype(F32)
        prod_t = (o_ref[...].astype(F32) * do).T
        dot_ref[...] = do.T.astype(BF16)
        for h in range(MLA_HEADS):
            d_ref[h // 2, (h % 2):(h % 2) + 1, :] = jnp.sum(prod_t[h * 64:(h + 1) * 64, :], axis=0, keepdims=True)

    return pl.pallas_call(
        body, name="attn_delta", grid=(S // tm,),
        in_specs=[_rows(tm, 512), _rows(tm, 512)],
        out_specs=[pl.BlockSpec((512, tm), lambda i: (0, i)), pl.BlockSpec((MLA_HEADS // 2, 2, tm), lambda i: (0, 0, i))],
        out_shape=[_sds((512, S), BF16), _sds((MLA_HEADS // 2, 2, S), F32)],
        compiler_params=_cp(("parallel",)),
    )(o, do)


def _flash_bwd(qp, kp, kt, v, do, dot, lse, delta, S, sums=()):
    tq = min(512, S)
    nq = S // tq
    RB = ATT_ROWS
    NH = BWD_HEADS
    qb_of, kb_of, T = _tri_pairs(nq, k_major=True)
    n = len(sums)
    steps = (MLA_HEADS // NH) * T

    def body(qb_ref, kb_ref, q_ref, k_ref, kt_ref, v_ref, do_ref, dot_ref, lse_ref, dl_ref, *rest):
        g_ins, (dq_ref, dk_ref, dv_ref), g_outs = rest[:n], rest[n:n + 3], rest[n + 3:2 * n + 3]
        dk_sc, dv_sc, s_sc, dp_sc, p_sc, ds_sc = rest[2 * n + 3:2 * n + 9]
        sems = rest[2 * n + 9:]
        t = pl.program_id(1)
        qb = qb_ref[t]
        kb = kb_ref[t]
        lin = pl.program_id(0) * T + t

        if n:
            @pl.when(lin == 0)
            def _():
                for cp in _scatter_copies(g_ins, g_outs, sems):
                    cp.start()

        @pl.when(t == 0)
        def _():
            dq_ref[...] = jnp.zeros(dq_ref.shape, F32)

        @pl.when(qb == kb)
        def _():
            dk_sc[...] = jnp.zeros(dk_sc.shape, F32)
            dv_sc[...] = jnp.zeros(dv_sc.shape, F32)

        lane = lax.broadcasted_iota(jnp.int32, (tq, 64 * NH), 1)

        def step(masked):
            vv = v_ref[...]
            do_all = do_ref[...]
            mine = [(lane >= a * 64) & (lane < (a + 1) * 64) for a in range(NH)]
            for a in range(NH):
                sl = slice(a * 128, (a + 1) * 128)
                s_sc[a] = _dot_nt(k_ref[:, sl], q_ref[:, sl])
                dp_sc[a] = jnp.dot(jnp.where(mine[a], vv, jnp.zeros_like(vv)), dot_ref[...],
                                   preferred_element_type=F32)
            for a in range(NH):
                sl = slice(a * 128, (a + 1) * 128)
                lse = lse_ref[a:a + 1, :]
                dl = dl_ref[a:a + 1, :]
                for r in range(0, tq, RB):
                    sc = s_sc[a, r:r + RB, :]
                    if masked:
                        sc = jnp.where(_causal_keep(r, RB, tq), sc, NEG)
                    p = jnp.exp(sc - lse)
                    p_sc[a, r:r + RB, :] = p.astype(BF16)
                    ds_sc[a, r:r + RB, :] = (p * (dp_sc[a, r:r + RB, :] - dl)).astype(BF16)
                ds = ds_sc[a]
                dv_sc[...] += jnp.dot(p_sc[a], jnp.where(mine[a], do_all, jnp.zeros_like(do_all)),
                                      preferred_element_type=F32)
                dk_sc[:, sl] += jnp.dot(ds, q_ref[:, sl], preferred_element_type=F32)
                dq_ref[qb, sl, :] += jnp.dot(kt_ref[sl, :], ds, preferred_element_type=F32)

        @pl.when(qb > kb)
        def _():
            step(False)

        @pl.when(qb == kb)
        def _():
            step(True)

        @pl.when(qb == nq - 1)
        def _():
            dk_ref[...] = dk_sc[...].astype(BF16)
            dv_ref[...] = dv_sc[...].astype(BF16)

        if n:
            @pl.when(lin == steps - 1)
            def _():
                for cp in _scatter_copies(g_ins, g_outs, sems):
                    cp.wait()

    grid_spec = pltpu.PrefetchScalarGridSpec(
        num_scalar_prefetch=2, grid=(MLA_HEADS // NH, T),
        in_specs=[pl.BlockSpec((tq, 128 * NH), lambda j, t, qb, kb: (qb[t], j)),
                  pl.BlockSpec((tq, 128 * NH), lambda j, t, qb, kb: (kb[t], j)),
                  pl.BlockSpec((128 * NH, tq), lambda j, t, qb, kb: (j, kb[t])),
                  pl.BlockSpec((tq, 64 * NH), lambda j, t, qb, kb: (kb[t], j)),
                  pl.BlockSpec((tq, 64 * NH), lambda j, t, qb, kb: (qb[t], j)),
                  pl.BlockSpec((64 * NH, tq), lambda j, t, qb, kb: (j, qb[t])),
                  pl.BlockSpec((None, NH, tq), lambda j, t, qb, kb: (j, 0, qb[t])),
                  pl.BlockSpec((None, NH, tq), lambda j, t, qb, kb: (j, 0, qb[t]))] + [_ANY] * n,
        out_specs=[pl.BlockSpec((nq, 128 * NH, tq), lambda j, t, qb, kb: (0, j, 0), pipeline_mode=pl.Buffered(1)),
                   pl.BlockSpec((tq, 128 * NH), lambda j, t, qb, kb: (kb[t], j)),
                   pl.BlockSpec((tq, 64 * NH), lambda j, t, qb, kb: (kb[t], j))] + [_ANY] * n,
        scratch_shapes=[pltpu.VMEM((tq, 128 * NH), F32), pltpu.VMEM((tq, 64 * NH), F32), pltpu.VMEM((NH, tq, tq), F32),
                        pltpu.VMEM((NH, tq, tq), F32), pltpu.VMEM((NH, tq, tq), BF16), pltpu.VMEM((NH, tq, tq), BF16)]
        + (_scatter_sems(n) if n else []),
    )
    dq, dk, dv, *parts = pl.pallas_call(
        body, name="flash_bwd", grid_spec=grid_spec,
        out_shape=[_sds((nq, 1024, tq), F32), _sds((S, 1024), BF16), _sds((S, 512), BF16)] + _scatter_out_shapes(sums),
        compiler_params=_cp(("arbitrary", "arbitrary")),
    )(qb_of, kb_of, qp, kp, kt, v, do, dot, lse.reshape(MLA_HEADS // NH, NH, S), delta.reshape(MLA_HEADS // NH, NH, S),
      *sums)
    return dq, dk, dv, parts


def _mla_up_bwd(dqp, dkp, dv, cq, ckv, gq, gkv, w_uq, w_ukv, tabs, S):
    tm = min(512, S)

    def body(dq_ref, dk_ref, dv_ref, cq_ref, ckv_ref, gq_ref, gkv_ref, wuq_ref, wukv_ref, cm_ref, sa_ref, sb_ref,
             dqh_ref, dkv_ref, dcq_ref, dckv_ref, dkr_ref, dgq_ref, dgkv_ref):
        @pl.when(pl.program_id(0) == 0)
        def _():
            dgq_ref[...] = jnp.zeros(dgq_ref.shape, F32)
            dgkv_ref[...] = jnp.zeros(dgkv_ref.shape, F32)

        cm = cm_ref[...]
        sa = sa_ref[...]
        sb = sb_ref[...]
        lane = lax.broadcasted_iota(jnp.int32, (tm, 128), 1)
        dkr_r = jnp.zeros((tm, 128), F32)
        for h in range(MLA_HEADS):
            sl = slice(h * 128, (h + 1) * 128)
            dqh_ref[:, sl] = (_unrope_mla(dq_ref[sl, :].T, cm, sa, sb) * SCALE_MLA).astype(BF16)
            gk = dk_ref[:, sl]
            dkr_r = dkr_r + gk.astype(F32)
            dkv_ref[:, sl] = gk
        dkr_r = jnp.where((lane >= 64) & (lane < 96), dkr_r, 0.0)
        dkr_ref[...] = _unrope_mla(dkr_r, cm, sa, sb).astype(BF16)
        dkv_ref[:, 1024:1536] = dv_ref[...]
        dcq, ga = _rms_bwd(_dot_nt(dqh_ref[...], wuq_ref[...]), cq_ref[...], gq_ref[...])
        dcq_ref[...] = dcq.astype(BF16)
        dgq_ref[...] += _colsum(ga)
        dckv, gb = _rms_bwd(_dot_nt(dkv_ref[...], wukv_ref[...]), ckv_ref[...], gkv_ref[...])
        dckv_ref[...] = dckv.astype(BF16)
        dgkv_ref[...] += _colsum(gb)

    per_q = dqp.shape[2] // tm
    return pl.pallas_call(
        body, name="mla_up_bwd", grid=(S // tm,),
        in_specs=[pl.BlockSpec((None, 1024, tm), lambda i: (i // per_q, 0, i % per_q)),
                  _rows(tm, 1024), _rows(tm, 512), _rows(tm, Q_LORA), _rows(tm, KV_LORA),
                  _full(1, Q_LORA), _full(1, KV_LORA), _full(Q_LORA, 1024), _full(KV_LORA, 1536)] + [_rows(tm, 128)] * 3,
        out_specs=[_rows(tm, 1024), _rows(tm, 1536), _rows(tm, Q_LORA), _rows(tm, KV_LORA), _rows(tm, 128),
                   _acc(1, Q_LORA), _acc(1, KV_LORA)],
        out_shape=[_sds((S, 1024), BF16), _sds((S, 1536), BF16), _sds((S, Q_LORA), BF16), _sds((S, KV_LORA), BF16),
                   _sds((S, 128), BF16), _sds((1, Q_LORA), F32), _sds((1, KV_LORA), F32)],
        compiler_params=_cp(("arbitrary",)),
    )(dqp, dkp, dv, cq, ckv, gq, gkv, w_uq, w_ukv, *tabs[2:])


def _ret_bwd(rq, rk, rv, rprev, ry, rg, dro, gn_w, tabs, S):
    C = RET_CHUNK
    N = S // C
    G = min(RET_GROUP, N)
    NB = N // G

    def body(lg_ref, q_ref, k_ref, v_ref, rp_ref, ry_ref, rg_ref, dro_ref, w_ref, cr_ref, sr_ref,
             drq_ref, drk_ref, drv_ref, drg_ref, dw_ref, g_sc):
        @pl.when(pl.program_id(1) == 0)
        def _():
            g_sc[...] = jnp.zeros(g_sc.shape, F32)
            dw_ref[...] = jnp.zeros(dw_ref.shape, F32)

        dmat, zeta, xi, g_chunk = _decay_terms(lg_ref)
        w = w_ref[...]
        gacc = g_sc[...]
        dw = jnp.zeros((1, 128), F32)
        for i in reversed(range(G)):
            rows = slice(i * C, (i + 1) * C)
            ry = ry_ref[rows, :]
            mu = jnp.mean(ry, axis=-1, keepdims=True)
            yc = ry - mu
            rstd = lax.rsqrt(jnp.mean(yc * yc, axis=-1, keepdims=True) + EPS)
            yh = yc * rstd
            g = rg_ref[rows, :]
            s = _sig(g)
            dout = dro_ref[rows, :].astype(F32)
            drg_ref[rows, :] = (dout * (yh * w) * (s * (1.0 + g * (1.0 - s)))).astype(BF16)
            dgn = dout * (g * s)
            dw = dw + _colsum(dgn * yh)
            dyh = dgn * w
            dry = rstd * (dyh - jnp.mean(dyh, axis=-1, keepdims=True) - yh * jnp.mean(dyh * yh, axis=-1, keepdims=True))
            do = dry.astype(BF16)

            q = q_ref[rows, :]
            k = k_ref[rows, :]
            v = v_ref[rows, :]
            gfut = gacc.astype(BF16)
            sc = (_dot_nt(q, k) * dmat).astype(BF16)
            dsc = (_dot_nt(do, v) * dmat).astype(BF16)
            dq = jnp.dot(dsc, k, preferred_element_type=F32) + _dot_nt(do, rp_ref[i]) * xi
            dk = _dot_tn(dsc, q) + _dot_nt(v, gfut) * zeta
            dv = _dot_tn(sc, do) + jnp.dot(k, gfut, preferred_element_type=F32) * zeta
            gacc = g_chunk * gacc + _dot_tn(q, xi * dry)
            cr = cr_ref[rows, :]
            sr = sr_ref[rows, :]
            drq_ref[rows, :] = _unrope_ret(dq, cr, sr).astype(BF16)
            drk_ref[rows, :] = _unrope_ret(dk * SCALE_RET, cr, sr).astype(BF16)
            drv_ref[rows, :] = dv.astype(BF16)
        g_sc[...] = gacc
        dw_ref[...] += dw

    blk = pl.BlockSpec((G * C, 128), lambda h, n: (NB - 1 - n, h))
    tab = pl.BlockSpec((G * C, 128), lambda h, n: (NB - 1 - n, 0))
    return pl.pallas_call(
        body, name="ret_bwd", grid=(RET_HEADS, NB),
        in_specs=[pl.BlockSpec((None, 8, 128), lambda h, n: (h, 0, 0)), blk, blk, blk,
                  pl.BlockSpec((G, 128, 128), lambda h, n: (h * NB + NB - 1 - n, 0, 0)), blk, blk, blk,
                  pl.BlockSpec((1, 128), lambda h, n: (0, h)), tab, tab],
        out_specs=[blk, blk, blk, blk, pl.BlockSpec((1, 128), lambda h, n: (0, h))],
        out_shape=[_sds((S, 512), BF16)] * 4 + [_sds((1, 512), F32)],
        scratch_shapes=[pltpu.VMEM((128, 128), F32)],
        compiler_params=_cp(("parallel", "arbitrary")),
    )(_decay_table(), rq, rk, rv, rprev, ry, rg, dro, gn_w, tabs[0], tabs[1])


def _inproj_bwd(drq, drk, drv, drg, dcq, dckv, dkr, w_in, dh1, x, g, S):
    tm = min(512, S)

    def body(drq_ref, drk_ref, drv_ref, drg_ref, dcq_ref, dckv_ref, dkr_ref, w_ref, dh1_ref, x_ref, g_ref,
             gx_ref, dproj_ref, dg_ref):
        @pl.when(pl.program_id(0) == 0)
        def _():
            dg_ref[...] = jnp.zeros(dg_ref.shape, F32)

        dproj_ref[:, 0:512] = drq_ref[...]
        dproj_ref[:, 512:1024] = drk_ref[...]
        dproj_ref[:, 1024:1536] = drv_ref[...]
        dproj_ref[:, 1536:2048] = drg_ref[...]
        dproj_ref[:, 2048:2432] = dcq_ref[...]
        dproj_ref[:, 2432:2688] = dckv_ref[...]
        dproj_ref[:, 2688:2816] = dkr_ref[...]
        dx, ga = _rms_bwd(_dot_nt(dproj_ref[...], w_ref[...]), x_ref[...], g_ref[...])
        gx_ref[...] = dh1_ref[...] + dx
        dg_ref[...] += _colsum(ga)

    return pl.pallas_call(
        body, name="inproj_bwd", grid=(S // tm,),
        in_specs=[_rows(tm, 512)] * 4 + [_rows(tm, Q_LORA), _rows(tm, KV_LORA), _rows(tm, 128),
                                         _full(D_MODEL, IN_COLS_P), _rows(tm, D_MODEL), _rows(tm, D_MODEL),
                                         _full(1, D_MODEL)],
        out_specs=[_rows(tm, D_MODEL), _rows(tm, IN_COLS_P), _acc(1, D_MODEL)],
        out_shape=[_sds((S, D_MODEL), F32), _sds((S, IN_COLS_P), BF16), _sds((1, D_MODEL), F32)],
        compiler_params=_cp(("arbitrary",)),
    )(drq, drk, drv, drg, dcq, dckv, dkr, w_in, dh1, x, g)


def _pad_weights(w):
    w_in = w["w_in"]
    z = lambda r, c: jnp.zeros((r, c), BF16)
    w_in_p = jnp.concatenate([w_in[:, :2688], z(1024, 64), w_in[:, 2688:2720], z(1024, 32)], axis=1)
    w_uq_p = jnp.pad(w["w_uq"].reshape(Q_LORA, MLA_HEADS, 96), ((0, 0), (0, 0), (0, 32))).reshape(Q_LORA, 1024)
    ukv = w["w_ukv"].reshape(KV_LORA, MLA_HEADS, 128)
    k_part = jnp.pad(ukv[:, :, :64], ((0, 0), (0, 0), (0, 64))).reshape(KV_LORA, 1024)
    w_ukv_p = jnp.concatenate([k_part, ukv[:, :, 64:].reshape(KV_LORA, 512)], axis=1)
    return w_in_p, w_uq_p, w_ukv_p


BIG_SPEC = {n: (r, c, ax) for n, r, c, ax in BIG}
COLUMN_MAJOR = ("w_in", "w_uq", "w_gate", "w_up")
GRAD_TRANSPOSED = ("w_gate", "w_up")
GATHER_FIRST = ("w_in", "w_uq", "w_ukv")
GATHER_LATE = tuple(n for n, _, _, _ in BIG if n not in GATHER_FIRST)
REDUCE_EARLY = ("w_ple_gate", "w_ple_proj", "w_down", "w_gate", "w_up")
REDUCE_LAST = tuple(n for n, _, _, _ in BIG if n not in REDUCE_EARLY)


def _local_step(x, p, pos_f, tgt, w, sm, late_shards=None, c_idx=None):
    S = x.shape[0]
    spread = late_shards is not None
    w = dict(w)
    tabs, first = _rope_tables(pos_f, S, [late_shards[n] for n in GATHER_FIRST] if spread else ())
    for i, n in enumerate(GATHER_FIRST if spread else ()):
        w[n] = _from_chips(first[i], BIG_SPEC[n][2])
    w_in_p, w_uq_p, w_ukv_p = _pad_weights(w)

    xn, rq, rk, rv, rg, cq, ckv, kr = _inproj(x, sm["pre_mix_norm"], w_in_p, tabs, S)
    cqn, ckvn, qp, kp, v, kt, vt = _mla_up(cq, ckv, kr, sm["mla_q_norm"], sm["mla_kv_norm"], w_uq_p, w_ukv_p, tabs, S)
    mo, lse, gathered = _flash_fwd(qp, kp, vt, S, [late_shards[n] for n in GATHER_LATE] if spread else ())
    for i, n in enumerate(GATHER_LATE if spread else ()):
        w[n] = _from_chips(gathered[i], 0 if n in GRAD_TRANSPOSED else BIG_SPEC[n][2])
    if not spread:
        w.update({n: w[n].T for n in GRAD_TRANSPOSED})
    ry, ro, rprev = _ret_fwd(rq, rk, rv, rg, sm["ret_gn_w"], S)
    mix, h1, hn = _outproj(ro, mo, x, w["w_o"], sm["post_mix_norm"], sm["pre_ffn_norm"], S)
    dgate_f, dup_f, act = _ffn_up(hn, w["w_gate"], w["w_up"], S)
    ff, h2 = _ffn_down(act, w["w_down"], h1, sm["post_ffn_norm"], S)
    dz, dpe, dh2, h2b, loss_vec, d_ple_norm, d_b = _ple_loss(
        p, h2, tgt, w["w_ple_proj"], w["w_ple_gate"], sm["b_ple_gate"], sm["ple_norm"], S)

    gw = {}
    gs = {"ple_norm": d_ple_norm, "b_ple_gate": d_b}
    gw["w_ple_gate"] = _wgrad(h2b, dz, "wgrad_ple_gate", S)
    gw["w_ple_proj"] = _wgrad(p, dpe, "wgrad_ple_proj", S)
    dff, dgate, dup, gs["post_ffn_norm"] = _ffn_down_bwd(dh2, ff, sm["post_ffn_norm"], w["w_down"], dgate_f, dup_f, S)
    gw["w_down"] = _wgrad(act, dff, "wgrad_down", S)
    if spread:
        gw["w_gate"] = _wgrad(dgate, hn, "wgrad_gate", S)
        gw["w_up"] = _wgrad(dup, hn, "wgrad_up", S)
    else:
        gw["w_gate"] = _wgrad(hn, dgate, "wgrad_gate", S)
        gw["w_up"] = _wgrad(hn, dup, "wgrad_up", S)
    g4 = [_by_chip(gw.pop(n), *((D_FF, D_MODEL, 0) if n in GRAD_TRANSPOSED else BIG_SPEC[n]))
          for n in REDUCE_EARLY] if spread else []
    dh1, dmix, dro, dmo, gs["pre_ffn_norm"], gs["post_mix_norm"], got = _ffn_up_bwd(
        dgate, dup, w["w_gate"], w["w_up"], h1, mix, dh2, sm["pre_ffn_norm"], sm["post_mix_norm"], w["w_o"], S, g4)
    sums = [_add_half_rows(g4[i], got[i], c_idx, "rs_add_halves_" + n) for i, n in enumerate(REDUCE_EARLY)] if spread else []
    gw["w_o"] = jnp.concatenate([_wgrad(ro, dmix, "wgrad_o_ret", S), _wgrad(mo, dmix, "wgrad_o_mla", S)], axis=0)

    dmo_t, delta = _attn_delta(mo, dmo, S)
    dqp, dkp, dv, parts = _flash_bwd(qp, kp, kt, v, dmo, dmo_t, lse, delta, S, sums)
    dqh, dkv, dcq, dckv, dkr, gs["mla_q_norm"], gs["mla_kv_norm"] = _mla_up_bwd(
        dqp, dkp, dv, cq, ckv, sm["mla_q_norm"], sm["mla_kv_norm"], w_uq_p, w_ukv_p, tabs, S)
    g_uq_p = _wgrad(cqn, dqh, "wgrad_uq", S)
    g_ukv_p = _wgrad(ckvn, dkv, "wgrad_ukv", S)
    gw["w_uq"] = g_uq_p.reshape(Q_LORA, MLA_HEADS, 128)[:, :, :96].reshape(Q_LORA, 768)
    gw["w_ukv"] = jnp.concatenate(
        [g_ukv_p[:, :1024].reshape(KV_LORA, MLA_HEADS, 128)[:, :, :64], g_ukv_p[:, 1024:].reshape(KV_LORA, MLA_HEADS, 64)],
        axis=2).reshape(KV_LORA, 1024)

    drq, drk, drv, drg, gs["ret_gn_w"] = _ret_bwd(rq, rk, rv, rprev, ry, rg, dro, sm["ret_gn_w"], tabs, S)
    grad_x, dproj, gs["pre_mix_norm"] = _inproj_bwd(drq, drk, drv, drg, dcq, dckv, dkr, w_in_p, dh1, x,
                                                    sm["pre_mix_norm"], S)
    g_in_p = _wgrad(xn, dproj, "wgrad_in", S)
    gw["w_in"] = jnp.concatenate([g_in_p[:, :2688], g_in_p[:, 2752:2784]], axis=1)
    return loss_vec, grad_x, gw, gs, ((sums, parts) if spread else None)


def _my_place():
    x = lax.axis_index("x")
    y = lax.axis_index("y")
    c = lax.axis_index("c")
    return x, y, c


def _other_chips(x, y):
    return [(1 - x, y), (x, 1 - y), (1 - x, 1 - y)]


_ANY = pl.BlockSpec(memory_space=pl.ANY)


def _small_copies(v_ref, slots, sems):
    send, recv, lsem = sems
    x, y, c = _my_place()
    me = 4 * x + 2 * y + c
    cps = [pltpu.make_async_copy(v_ref, slots.at[me], lsem)]
    for r in range(1, N_DEV):
        peer = (x ^ (r >> 2), y ^ ((r >> 1) & 1), c ^ (r & 1))
        cps.append(pltpu.make_async_remote_copy(
            src_ref=v_ref, dst_ref=slots.at[me], send_sem=send.at[r - 1], recv_sem=recv.at[r - 1],
            device_id=peer, device_id_type=MESH))
    return cps


def _small_sum(slots, out_ref):
    acc = slots[0]
    for d in range(1, N_DEV):
        acc = acc + slots[d]
    out_ref[...] = acc
    loss = jnp.sum(acc[9:10, :], axis=1, keepdims=True) * (0.5 / D_MODEL)
    out_ref[9:10, :] = jnp.broadcast_to(loss, (1, PACK_COLS))


def _small_scratch():
    return [pltpu.VMEM((N_DEV, SMALL_ROWS, PACK_COLS), F32), pltpu.SemaphoreType.DMA((N_DEV - 1,)),
            pltpu.SemaphoreType.DMA((N_DEV - 1,)), pltpu.SemaphoreType.DMA]


N_BIG = len(BIG)


def _half(c, rows, align):
    h = rows // 2
    return pl.ds(pl.multiple_of(c * h, align), h)


def _gather_out_shapes(shards):
    return [_sds((N_CHIPS,) + tuple(s.shape), BF16) for s in shards]


def _gather_sems(n):
    return [pltpu.SemaphoreType.DMA((n, 3))] * 4 + [pltpu.SemaphoreType.DMA((n,))] * 2


def _gather_phase(phase, ins, outs, sems):
    send1, recv1, send2, recv2, send3, recv3 = sems
    x, y, c = _my_place()
    me = 2 * x + y
    chips = _other_chips(x, y)
    sib = (x, y, 1 - c)
    for t in range(len(ins)):
        rows = ins[t].shape[0]
        half = _half(c, rows, 16)
        other = _half(1 - c, rows, 16)
        def own():
            return pltpu.make_async_remote_copy(
                src_ref=ins[t], dst_ref=outs[t].at[me], send_sem=send3.at[t], recv_sem=recv3.at[t],
                device_id=sib, device_id_type=MESH)

        if phase == 0:
            own().start()
        if phase == 2:
            own().wait()
        for k, (cx, cy) in enumerate(chips):
            src = 2 * cx + cy

            def over_ici(slab):
                return pltpu.make_async_remote_copy(
                    src_ref=ins[t].at[half], dst_ref=outs[t].at[slab, half], send_sem=send1.at[t, k],
                    recv_sem=recv1.at[t, k], device_id=(cx, cy, c), device_id_type=MESH)

            def over_d2d(rows):
                return pltpu.make_async_remote_copy(
                    src_ref=outs[t].at[src, rows], dst_ref=outs[t].at[src, rows], send_sem=send2.at[t, k],
                    recv_sem=recv2.at[t, k], device_id=sib, device_id_type=MESH)

            if phase == 0:
                over_ici(me).start()
            if phase == 1:
                over_ici(src).wait_recv()
                over_d2d(half).start()
            if phase == 2:
                over_d2d(other).wait_recv()
                over_ici(me).wait_send()
                over_d2d(half).wait_send()


def _swap_copies(ins, outs, sems):
    send, recv = sems
    x, y, c = _my_place()
    return [pltpu.make_async_remote_copy(
        src_ref=ins[t].at[:, _half(1 - c, ins[t].shape[1], 8)], dst_ref=outs[t], send_sem=send.at[t],
        recv_sem=recv.at[t], device_id=(x, y, 1 - c), device_id_type=MESH) for t in range(len(ins))]


def _swap_out_shapes(gs):
    return [_sds((N_CHIPS, g.shape[1] // 2, g.shape[2]), F32) for g in gs]


def _swap_sems(n):
    return [pltpu.SemaphoreType.DMA((n,)), pltpu.SemaphoreType.DMA((n,))]


def _swap_half_rows(gs):
    n = len(gs)

    def body(*refs):
        cps = _swap_copies(refs[:n], refs[n:2 * n], refs[2 * n:])
        for cp in cps:
            cp.start()
        for cp in cps:
            cp.wait()

    return pl.pallas_call(
        body, name="rs_swap_halves",
        in_specs=[_ANY] * n, out_specs=[_ANY] * n, out_shape=_swap_out_shapes(gs), scratch_shapes=_swap_sems(n),
    )(*gs)


def _add_half_rows(g, got, c_idx, name):
    _, rows, cols = g.shape
    h = rows // 2

    def body(c_ref, a_ref, b_ref, o_ref):
        o_ref[...] = (a_ref[...] + b_ref[...]).astype(BF16)

    grid_spec = pltpu.PrefetchScalarGridSpec(
        num_scalar_prefetch=1, grid=(N_CHIPS,),
        in_specs=[pl.BlockSpec((None, h, cols), lambda j, c: (j, c[0], 0)),
                  pl.BlockSpec((None, h, cols), lambda j, c: (j, 0, 0))],
        out_specs=pl.BlockSpec((None, h, cols), lambda j, c: (j, 0, 0)),
    )
    return pl.pallas_call(
        body, name=name, grid_spec=grid_spec, out_shape=_sds((N_CHIPS, h, cols), BF16),
        compiler_params=_cp(("parallel",)),
    )(c_idx, g, got)


def _scatter_to_chips(ts, vec):
    n = len(ts)

    def body(*refs):
        ins, v_ref, outs, small_ref = refs[:n], refs[n], refs[n + 1:2 * n + 1], refs[2 * n + 1]
        slots, small_sems, sems = refs[2 * n + 2], refs[2 * n + 3:2 * n + 6], refs[2 * n + 6:]
        small = _small_copies(v_ref, slots, small_sems)
        cps = _scatter_copies(ins, outs, sems)
        for cp in small + cps:
            cp.start()
        for cp in small:
            cp.wait()
        _small_sum(slots, small_ref)
        for cp in cps:
            cp.wait()

    vm = pl.BlockSpec(memory_space=pltpu.VMEM)
    *parts, small_sum = pl.pallas_call(
        body, name="rs_scatter_chips",
        in_specs=[_ANY] * n + [vm], out_specs=[_ANY] * n + [vm],
        out_shape=_scatter_out_shapes(ts) + [_sds((SMALL_ROWS, PACK_COLS), F32)],
        scratch_shapes=_small_scratch() + _scatter_sems(n),
    )(*ts, vec)
    return parts, small_sum


def _scatter_copies(ins, outs, sems):
    send, recv = sems
    x, y, c = _my_place()
    return [pltpu.make_async_remote_copy(
        src_ref=ins[t].at[2 * cx + cy], dst_ref=outs[t].at[k], send_sem=send.at[t, k], recv_sem=recv.at[t, k],
        device_id=(cx, cy, c), device_id_type=MESH)
        for t in range(len(ins)) for k, (cx, cy) in enumerate(_other_chips(x, y))]


def _scatter_out_shapes(ts):
    return [_sds((3,) + tuple(t.shape[1:]), BF16) for t in ts]


def _scatter_sems(n):
    return [pltpu.SemaphoreType.DMA((n, 3)), pltpu.SemaphoreType.DMA((n, 3))]


def _add_four(mine, parts, place, name):
    _, h, cols = parts.shape

    def body(pl_ref, m_ref, p_ref, o_ref):
        o_ref[...] = ((m_ref[...].astype(F32) + p_ref[0].astype(F32)) + p_ref[1].astype(F32)) + p_ref[2].astype(F32)

    grid_spec = pltpu.PrefetchScalarGridSpec(
        num_scalar_prefetch=1, grid=(1,),
        in_specs=[pl.BlockSpec((None, h, cols), lambda i, pc: (pc[0], 0, 0)),
                  pl.BlockSpec((3, h, cols), lambda i, pc: (0, 0, 0))],
        out_specs=pl.BlockSpec((h, cols), lambda i, pc: (pc[1], 0)),
    )
    return pl.pallas_call(
        body, name=name, grid_spec=grid_spec, out_shape=_sds((2 * h, cols), F32),
        compiler_params=_cp(("arbitrary",)),
    )(place, mine, parts)


def _join_half_rows(rs):
    n = len(rs)

    def body(*refs):
        ins, outs = refs[:n], refs[n:2 * n]
        send, recv = refs[2 * n:]
        x, y, c = _my_place()
        cps = []
        for t in range(n):
            half = _half(c, outs[t].shape[0], 8)
            rc = pltpu.make_async_remote_copy(
                src_ref=ins[t].at[half], dst_ref=outs[t].at[half], send_sem=send.at[t], recv_sem=recv.at[t],
                device_id=(x, y, 1 - c), device_id_type=MESH)
            rc.start()
            cps.append(rc)
        for cp in cps:
            cp.wait()

    return pl.pallas_call(
        body, name="rs_join_halves",
        in_specs=[_ANY] * n, out_specs=[_ANY] * n,
        out_shape=[_sds(r.shape, F32) for r in rs],
        input_output_aliases={i: i for i in range(n)},
        scratch_shapes=[pltpu.SemaphoreType.DMA((n,))] * 2,
    )(*rs)


def _by_chip(full, rows, cols, axis):
    if axis == 0:
        return full.reshape(N_CHIPS, rows // N_CHIPS, cols)
    return full.reshape(rows, N_CHIPS, cols // N_CHIPS).transpose(1, 0, 2)


def _from_chips(parts, axis):
    _, r, c = parts.shape
    if axis == 0:
        return parts.reshape(N_CHIPS * r, c)
    return parts.transpose(1, 0, 2).reshape(r, N_CHIPS * c)


def _adamw(wt, g, m, v, name):
    _, R, C = wt.shape
    tr = max(d for d in range(8, R + 1, 8) if R % d == 0 and (d * C <= 256 * 1024 or d == 8))

    def body(w_ref, g_ref, m_ref, v_ref, d_ref, nm_ref, nv_ref):
        gg = g_ref[...]
        m_new = ADAM_B1 * m_ref[...] + (1.0 - ADAM_B1) * gg
        v_new = ADAM_B2 * v_ref[...] + (1.0 - ADAM_B2) * (gg * gg)
        m_hat = m_new / (1.0 - ADAM_B1 ** ADAM_STEP)
        v_hat = v_new / (1.0 - ADAM_B2 ** ADAM_STEP)
        d_ref[...] = -ADAM_LR * (m_hat / (jnp.sqrt(v_hat) + ADAM_EPS) + ADAM_WD * w_ref[...])
        nm_ref[...] = m_new
        nv_ref[...] = v_new

    spec = pl.BlockSpec((None, tr, C), lambda i: (0, i, 0))
    return pl.pallas_call(
        body, name=name, grid=(R // tr,), in_specs=[spec, pl.BlockSpec((tr, C), lambda i: (i, 0)), spec, spec],
        out_specs=[spec] * 3, out_shape=[_sds((1, R, C), F32)] * 3,
        compiler_params=_cp(("parallel",)),
    )(wt, g, m, v)


def _pack_small(vals, loss_vec=None):
    rows = [jnp.pad(vals[n].reshape(-1), (0, PACK_COLS - sz)) for n, sz in SMALL]
    rows.append(loss_vec.reshape(-1) if loss_vec is not None else jnp.zeros((PACK_COLS,), F32))
    rows += [jnp.zeros((PACK_COLS,), F32)] * (SMALL_ROWS - len(rows))
    return jnp.stack(rows)


def kernel(x, p, positions, pre_mix_norm, w_in, ret_gn_w, mla_q_norm, w_uq, mla_kv_norm, w_ukv, w_o, post_mix_norm, pre_ffn_norm, w_gate, w_up, w_down, post_ffn_norm, w_ple_proj, ple_norm, w_ple_gate, b_ple_gate, loss_target, m_pre_mix_norm, m_w_in, m_ret_gn_w, m_mla_q_norm, m_w_uq, m_mla_kv_norm, m_w_ukv, m_w_o, m_post_mix_norm, m_pre_ffn_norm, m_w_gate, m_w_up, m_w_down, m_post_ffn_norm, m_w_ple_proj, m_ple_norm, m_w_ple_gate, m_b_ple_gate, v_pre_mix_norm, v_w_in, v_ret_gn_w, v_mla_q_norm, v_w_uq, v_mla_kv_norm, v_w_ukv, v_w_o, v_post_mix_norm, v_pre_ffn_norm, v_w_gate, v_w_up, v_w_down, v_post_ffn_norm, v_w_ple_proj, v_ple_norm, v_w_ple_gate, v_b_ple_gate):
    wts = dict(pre_mix_norm=pre_mix_norm, w_in=w_in, ret_gn_w=ret_gn_w, mla_q_norm=mla_q_norm, w_uq=w_uq,
               mla_kv_norm=mla_kv_norm, w_ukv=w_ukv, w_o=w_o, post_mix_norm=post_mix_norm, pre_ffn_norm=pre_ffn_norm,
               w_gate=w_gate, w_up=w_up, w_down=w_down, post_ffn_norm=post_ffn_norm, w_ple_proj=w_ple_proj,
               ple_norm=ple_norm, w_ple_gate=w_ple_gate, b_ple_gate=b_ple_gate)
    mom = dict(pre_mix_norm=m_pre_mix_norm, w_in=m_w_in, ret_gn_w=m_ret_gn_w, mla_q_norm=m_mla_q_norm, w_uq=m_w_uq,
               mla_kv_norm=m_mla_kv_norm, w_ukv=m_w_ukv, w_o=m_w_o, post_mix_norm=m_post_mix_norm,
               pre_ffn_norm=m_pre_ffn_norm, w_gate=m_w_gate, w_up=m_w_up, w_down=m_w_down, post_ffn_norm=m_post_ffn_norm,
               w_ple_proj=m_w_ple_proj, ple_norm=m_ple_norm, w_ple_gate=m_w_ple_gate, b_ple_gate=m_b_ple_gate)
    var = dict(pre_mix_norm=v_pre_mix_norm, w_in=v_w_in, ret_gn_w=v_ret_gn_w, mla_q_norm=v_mla_q_norm, w_uq=v_w_uq,
               mla_kv_norm=v_mla_kv_norm, w_ukv=v_w_ukv, w_o=v_w_o, post_mix_norm=v_post_mix_norm,
               pre_ffn_norm=v_pre_ffn_norm, w_gate=v_w_gate, w_up=v_w_up, w_down=v_w_down, post_ffn_norm=v_post_ffn_norm,
               w_ple_proj=v_w_ple_proj, ple_norm=v_ple_norm, w_ple_gate=v_w_ple_gate, b_ple_gate=v_b_ple_gate)

    S = x.shape[1]
    shard2d = {n: wts[n][0] for n, _, _, _ in BIG}
    small2d = {n: wts[n] for n, _ in SMALL}

    shard_bf = {n: (jnp.swapaxes(wts[n], 1, 2)[0] if n in GRAD_TRANSPOSED else shard2d[n]).astype(BF16) for n in shard2d}
    pos_f = positions.astype(F32).reshape(S, 1)
    c_idx = lax.axis_index("c").astype(jnp.int32).reshape(1)
    loss_vec, grad_x, gw, gs, (sums_early, parts_early) = _local_step(
        x[0], p[0, 0], pos_f, loss_target[0], {}, small2d, shard_bf, c_idx)

    g4 = [_by_chip(gw[n], *BIG_SPEC[n]) for n in REDUCE_LAST]
    got = _swap_half_rows(g4)
    sums_last = [_add_half_rows(g4[i], got[i], c_idx, "rs_add_halves_" + n) for i, n in enumerate(REDUCE_LAST)]
    parts_last, small_sum = _scatter_to_chips(sums_last, _pack_small(gs, loss_vec))
    place = jnp.stack([2 * lax.axis_index("x") + lax.axis_index("y"), lax.axis_index("c")]).astype(jnp.int32)
    names = REDUCE_EARLY + REDUCE_LAST
    reduced = _join_half_rows(
        [_add_four(sm_, pt_, place, "rs_add_chips_" + n)
         for n, sm_, pt_ in zip(names, sums_early + sums_last, list(parts_early) + list(parts_last))])
    g_shard = dict(zip(names, reduced))

    loss = small_sum[9, 0]
    g_small = {n: small_sum[i:i + 1, :sz] for i, (n, sz) in enumerate(SMALL)}

    grads, delta, new_m, new_v = {}, {}, {}, {}
    for n, _, _, _ in BIG:
        if n in COLUMN_MAJOR:
            turn = lambda a: jnp.swapaxes(a, 1, 2)
            g_t = g_shard[n] if n in GRAD_TRANSPOSED else g_shard[n].T
            d, nm, nv = _adamw(turn(wts[n]), g_t, turn(mom[n]), turn(var[n]), "adamw_" + n)
            grads[n], delta[n], new_m[n], new_v[n] = turn(g_t[None]), turn(d), turn(nm), turn(nv)
        else:
            delta[n], new_m[n], new_v[n] = _adamw(wts[n], g_shard[n], mom[n], var[n], "adamw_" + n)
            grads[n] = g_shard[n][None]
    d, nm, nv = _adamw(_pack_small(small2d)[None], small_sum, _pack_small(mom)[None], _pack_small(var)[None],
                       "adamw_small")
    for i, (n, sz) in enumerate(SMALL):
        grads[n] = g_small[n]
        delta[n], new_m[n], new_v[n] = d[0, i:i + 1, :sz], nm[0, i:i + 1, :sz], nv[0, i:i + 1, :sz]

    return (loss, grad_x[None], *[grads[n] for n in ALL_W], *[delta[n] for n in ALL_W],
            *[new_m[n] for n in ALL_W], *[new_v[n] for n in ALL_W])
```

```python
import functools
import math

import jax
import jax.numpy as jnp
import numpy as np
from jax import lax
from jax.experimental import pallas as pl
from jax.experimental.pallas import tpu as pltpu

F32 = jnp.float32
BF16 = jnp.bfloat16
MESH = pl.DeviceIdType.MESH

D_MODEL = 1024
D_FF = 2816
PLE_DIM = 256
RET_HEADS = 4
RET_DIM = 128
RET_WIDTH = 512
RET_CHUNK = 256
RET_GROUP = 4
MLA_HEADS = 8
MLA_NOPE = 64
MLA_ROPE = 32
MLA_V = 64
Q_LORA = 384
KV_LORA = 256
IN_COLS = 2720
IN_COLS_P = 2816
ROPE_BASE = 10000.0
EPS = 1e-6
SCALE_MLA = 1.0 / math.sqrt(MLA_NOPE + MLA_ROPE)
SCALE_RET = RET_DIM ** -0.5
NEG = -1e30

ADAM_LR = 0.001
ADAM_B1 = 0.9
ADAM_B2 = 0.999
ADAM_EPS = 1e-08
ADAM_WD = 0.01
ADAM_STEP = 10

N_CHIPS = 4
N_DEV = 8
VMEM_MB = 56

BIG = (
    ("w_in", 1024, 2720, 1),
    ("w_uq", 384, 768, 1),
    ("w_ukv", 256, 1024, 1),
    ("w_o", 1024, 1024, 0),
    ("w_gate", 1024, 2816, 1),
    ("w_up", 1024, 2816, 1),
    ("w_down", 2816, 1024, 0),
    ("w_ple_proj", 256, 1024, 1),
    ("w_ple_gate", 1024, 1024, 0),
)
SMALL = (
    ("pre_mix_norm", 1024),
    ("ret_gn_w", 512),
    ("mla_q_norm", 384),
    ("mla_kv_norm", 256),
    ("post_mix_norm", 1024),
    ("pre_ffn_norm", 1024),
    ("post_ffn_norm", 1024),
    ("ple_norm", 1024),
    ("b_ple_gate", 1024),
)
ALL_W = ("pre_mix_norm", "w_in", "ret_gn_w", "mla_q_norm", "w_uq", "mla_kv_norm", "w_ukv", "w_o", "post_mix_norm",
         "pre_ffn_norm", "w_gate", "w_up", "w_down", "post_ffn_norm", "w_ple_proj", "ple_norm", "w_ple_gate", "b_ple_gate")
PACK_COLS = 1024
SMALL_ROWS = 16


def _cp(sem=None, mb=VMEM_MB, **kw):
    return pltpu.CompilerParams(dimension_semantics=sem, vmem_limit_bytes=mb * 1024 * 1024, **kw)


def _bf(x):
    return x.astype(BF16)


def _dot(a, b):
    return jnp.dot(_bf(a), _bf(b), preferred_element_type=F32)


def _dot_nt(a, b):
    return lax.dot_general(_bf(a), _bf(b), (((1,), (1,)), ((), ())), preferred_element_type=F32)


def _dot_tn(a, b):
    return lax.dot_general(_bf(a), _bf(b), (((0,), (0,)), ((), ())), preferred_element_type=F32)


def _sig(x):
    return 1.0 / (1.0 + jnp.exp(-x))


def _rms(x, g):
    r = lax.rsqrt(jnp.mean(x * x, axis=-1, keepdims=True) + EPS)
    return x * r * g


def _rms_bwd(dy, x, g):
    r = lax.rsqrt(jnp.mean(x * x, axis=-1, keepdims=True) + EPS)
    xh = x * r
    dxh = dy * g
    dx = r * (dxh - xh * jnp.mean(dxh * xh, axis=-1, keepdims=True))
    return dx, dy * xh


def _colsum(x):
    return jnp.sum(x, axis=0, keepdims=True)


def _rope_ret(x, cr, sr):
    return x * cr + pltpu.roll(x, 64, 1) * sr


def _unrope_ret(dy, cr, sr):
    return dy * cr + pltpu.roll(dy * sr, 64, 1)


def _rope_mla(x, cm, sa, sb):
    return x * cm + pltpu.roll(x, 112, 1) * sa + pltpu.roll(x, 16, 1) * sb


def _unrope_mla(dy, cm, sa, sb):
    return dy * cm + pltpu.roll(dy * sa, 16, 1) + pltpu.roll(dy * sb, 112, 1)


def _rows(tm, w, col=0):
    return pl.BlockSpec((tm, w), lambda i: (i, col))


def _full(*shape):
    return pl.BlockSpec(shape, lambda i: (0,) * len(shape), pipeline_mode=pl.Buffered(1))


def _acc(*shape):
    return pl.BlockSpec(shape, lambda i: (0,) * len(shape))


def _sds(shape, dtype):
    return jax.ShapeDtypeStruct(shape, dtype)


def _rope_tables(pos_f, S, shards=()):
    tm = min(512, S)
    n = len(shards)
    steps = S // tm
    inv_r = (1.0 / (np.float32(ROPE_BASE) ** (np.arange(64, dtype=np.float32) / np.float32(64)))).astype(np.float32)
    inv_m16 = (1.0 / (np.float32(ROPE_BASE) ** (np.arange(16, dtype=np.float32) / np.float32(16)))).astype(np.float32)
    inv_r = np.concatenate([inv_r, inv_r])[None, :]
    inv_m = np.zeros((1, 128), np.float32)
    inv_m[0, 64:80] = inv_m16
    inv_m[0, 80:96] = inv_m16

    def body(pos_ref, invr_ref, invm_ref, *rest):
        w_ins, (cr_ref, sr_ref, cm_ref, sa_ref, sb_ref) = rest[:n], rest[n:n + 5]
        w_outs, sems = rest[n + 5:2 * n + 5], rest[2 * n + 5:]
        i = pl.program_id(0)
        if n:
            @pl.when(i == 0)
            def _():
                _gather_phase(0, w_ins, w_outs, sems)

            @pl.when(i == steps // 2)
            def _():
                _gather_phase(1, w_ins, w_outs, sems)

        pos = pos_ref[...]
        lane = lax.broadcasted_iota(jnp.int32, (tm, 128), 1)
        ar = pos * invr_ref[...]
        s = jnp.sin(ar)
        cr_ref[...] = jnp.cos(ar)
        sr_ref[...] = jnp.where(lane < 64, -s, s)
        am = pos * invm_ref[...]
        c2 = jnp.cos(am)
        s2 = jnp.sin(am)
        cm_ref[...] = jnp.where(lane < 64, 1.0, jnp.where(lane < 96, c2, 0.0))
        sa_ref[...] = jnp.where((lane >= 64) & (lane < 80), -s2, 0.0)
        sb_ref[...] = jnp.where((lane >= 80) & (lane < 96), s2, 0.0)

        if n:
            @pl.when(i == steps - 1)
            def _():
                _gather_phase(2, w_ins, w_outs, sems)

    outs = pl.pallas_call(
        body, name="rope_tables", grid=(steps,),
        in_specs=[_rows(tm, 1), _full(1, 128), _full(1, 128)] + [_ANY] * n,
        out_specs=[_rows(tm, 128)] * 5 + [_ANY] * n,
        out_shape=[_sds((S, 128), F32)] * 5 + _gather_out_shapes(shards),
        scratch_shapes=_gather_sems(n) if n else [],
        compiler_params=_cp(("arbitrary",)),
    )(pos_f, jnp.asarray(inv_r), jnp.asarray(inv_m), *shards)
    return outs[:5], outs[5:]


def _inproj(x, g, w_in, tabs, S):
    tm = min(512, S)

    def body(x_ref, g_ref, w_ref, cr_ref, sr_ref, cm_ref, sa_ref, sb_ref,
             xn_ref, rq_ref, rk_ref, rv_ref, rg_ref, cq_ref, ckv_ref, kr_ref):
        xb = _rms(x_ref[...], g_ref[...]).astype(BF16)
        xn_ref[...] = xb
        cr = cr_ref[...]
        sr = sr_ref[...]
        q = jnp.dot(xb, w_ref[:, 0:512], preferred_element_type=F32)
        k = jnp.dot(xb, w_ref[:, 512:1024], preferred_element_type=F32)
        for h in range(RET_HEADS):
            sl = slice(h * 128, (h + 1) * 128)
            rq_ref[:, sl] = _rope_ret(q[:, sl], cr, sr).astype(BF16)
            rk_ref[:, sl] = (_rope_ret(k[:, sl], cr, sr) * SCALE_RET).astype(BF16)
        rv_ref[...] = jnp.dot(xb, w_ref[:, 1024:1536], preferred_element_type=F32).astype(BF16)
        rg_ref[...] = jnp.dot(xb, w_ref[:, 1536:2048], preferred_element_type=F32)
        cq_ref[...] = jnp.dot(xb, w_ref[:, 2048:2432], preferred_element_type=F32)
        ckv_ref[...] = jnp.dot(xb, w_ref[:, 2432:2688], preferred_element_type=F32)
        kr = jnp.dot(xb, w_ref[:, 2688:2816], preferred_element_type=F32)
        kr_ref[...] = _rope_mla(kr, cm_ref[...], sa_ref[...], sb_ref[...])

    return pl.pallas_call(
        body, name="inproj", grid=(S // tm,),
        in_specs=[_rows(tm, D_MODEL), _full(1, D_MODEL), _full(D_MODEL, IN_COLS_P)] + [_rows(tm, 128)] * 5,
        out_specs=[_rows(tm, D_MODEL)] + [_rows(tm, 512)] * 4 + [_rows(tm, Q_LORA), _rows(tm, KV_LORA), _rows(tm, 128)],
        out_shape=[_sds((S, D_MODEL), BF16)] + [_sds((S, 512), BF16)] * 3
        + [_sds((S, 512), F32), _sds((S, Q_LORA), F32), _sds((S, KV_LORA), F32), _sds((S, 128), F32)],
        compiler_params=_cp(("parallel",)),
    )(x, g, w_in, *tabs)


def _mla_up(cq, ckv, kr, gq, gkv, w_uq, w_ukv, tabs, S):
    tm = min(512, S)

    def body(cq_ref, ckv_ref, kr_ref, gq_ref, gkv_ref, wuq_ref, wukv_ref, cm_ref, sa_ref, sb_ref,
             cqn_ref, ckvn_ref, qp_ref, kp_ref, v_ref, kt_ref, vt_ref):
        cm = cm_ref[...]
        sa = sa_ref[...]
        sb = sb_ref[...]
        cqn = _rms(cq_ref[...], gq_ref[...]).astype(BF16)
        cqn_ref[...] = cqn
        ckvn = _rms(ckv_ref[...], gkv_ref[...]).astype(BF16)
        ckvn_ref[...] = ckvn
        qh = jnp.dot(cqn, wuq_ref[...], preferred_element_type=F32)
        kv = jnp.dot(ckvn, wukv_ref[...], preferred_element_type=F32)
        kr_blk = kr_ref[...]
        for h in range(MLA_HEADS):
            sl = slice(h * 128, (h + 1) * 128)
            qp_ref[:, sl] = (_rope_mla(qh[:, sl], cm, sa, sb) * SCALE_MLA).astype(BF16)
            kh = kv[:, sl] + kr_blk
            kp_ref[:, sl] = kh.astype(BF16)
            kt_ref[sl, :] = kh.T.astype(BF16)
        for h in range(MLA_HEADS // 2):
            vh = kv[:, 1024 + h * 128:1024 + (h + 1) * 128]
            v_ref[:, h * 128:(h + 1) * 128] = vh.astype(BF16)
            vt_ref[h * 128:(h + 1) * 128, :] = vh.T.astype(BF16)

    cols = lambda r: pl.BlockSpec((r, tm), lambda i: (0, i))
    return pl.pallas_call(
        body, name="mla_up", grid=(S // tm,),
        in_specs=[_rows(tm, Q_LORA), _rows(tm, KV_LORA), _rows(tm, 128), _full(1, Q_LORA), _full(1, KV_LORA),
                  _full(Q_LORA, 1024), _full(KV_LORA, 1536)] + [_rows(tm, 128)] * 3,
        out_specs=[_rows(tm, Q_LORA), _rows(tm, KV_LORA), _rows(tm, 1024), _rows(tm, 1024), _rows(tm, 512),
                   cols(1024), cols(512)],
        out_shape=[_sds((S, Q_LORA), BF16), _sds((S, KV_LORA), BF16), _sds((S, 1024), BF16), _sds((S, 1024), BF16),
                   _sds((S, 512), BF16), _sds((1024, S), BF16), _sds((512, S), BF16)],
        compiler_params=_cp(("parallel",)),
    )(cq, ckv, kr, gq, gkv, w_uq, w_ukv, *tabs[2:])


def _tri_pairs(nq, k_major):
    if k_major:
        pairs = [(qb, kb) for kb in range(nq) for qb in range(kb, nq)]
    else:
        pairs = [(qb, kb) for qb in range(nq) for kb in range(qb + 1)]
    qb_of = np.array([p[0] for p in pairs], np.int32)
    kb_of = np.array([p[1] for p in pairs], np.int32)
    return jnp.asarray(qb_of), jnp.asarray(kb_of), len(pairs)


ATT_ROWS = 32
FWD_HEADS = 8
BWD_HEADS = 4


def _causal_keep(r0, rows, tq):
    key = r0 + lax.broadcasted_iota(jnp.int32, (rows, tq), 0)
    qry = lax.broadcasted_iota(jnp.int32, (rows, tq), 1)
    return key <= qry


def _flash_fwd(qp, kp, vt, S, shards=()):
    tq = min(512, S)
    nq = S // tq
    RB = ATT_ROWS
    NH = FWD_HEADS
    qb_of, kb_of, T = _tri_pairs(nq, k_major=False)
    n = len(shards)
    steps = (MLA_HEADS // NH) * T

    def body(qb_ref, kb_ref, q_ref, k_ref, vt_ref, *rest):
        w_ins, (o_ref, lse_ref), w_outs = rest[:n], rest[n:n + 2], rest[n + 2:2 * n + 2]
        m_sc, l_sc, acc_sc, s_sc, p_sc = rest[2 * n + 2:2 * n + 7]
        sems = rest[2 * n + 7:]
        t = pl.program_id(1)
        qb = qb_ref[t]
        kb = kb_ref[t]
        lin = pl.program_id(0) * T + t

        if n:
            @pl.when(lin == 0)
            def _():
                _gather_phase(0, w_ins, w_outs, sems)

            @pl.when(lin == steps // 2)
            def _():
                _gather_phase(1, w_ins, w_outs, sems)

        @pl.when(kb == 0)
        def _():
            m_sc[...] = jnp.full(m_sc.shape, NEG, F32)
            l_sc[...] = jnp.zeros(l_sc.shape, F32)
            acc_sc[...] = jnp.zeros(acc_sc.shape, F32)

        def scores(a):
            sl = slice(a * 128, (a + 1) * 128)
            s_sc[a] = _dot_nt(k_ref[:, sl], q_ref[:, sl])

        def step(masked):
            for a in range(NH):
                scores(a)
            for a in range(NH):
                mx = [jnp.full((8, tq), NEG, F32) for _ in range(RB // 8)]
                for r in range(0, tq, RB):
                    sc = s_sc[a, r:r + RB, :]
                    if masked:
                        sc = jnp.where(_causal_keep(r, RB, tq), sc, NEG)
                        s_sc[a, r:r + RB, :] = sc
                    for i in range(RB // 8):
                        mx[i] = jnp.maximum(mx[i], sc[i * 8:(i + 1) * 8, :])
                mx8 = functools.reduce(jnp.maximum, mx)
                m_prev = m_sc[a]
                m_new = jnp.maximum(m_prev, jnp.max(mx8, axis=0, keepdims=True))
                al = jnp.exp(m_prev - m_new)
                m_sc[a] = m_new
                ls = [jnp.zeros((8, tq), F32) for _ in range(RB // 8)]
                for r in range(0, tq, RB):
                    p = jnp.exp(s_sc[a, r:r + RB, :] - m_new)
                    for i in range(RB // 8):
                        ls[i] = ls[i] + p[i * 8:(i + 1) * 8, :]
                    p_sc[a, r:r + RB, :] = p.astype(BF16)
                l_sc[a] = al * l_sc[a] + jnp.sum(functools.reduce(jnp.add, ls), axis=0, keepdims=True)
                pair = slice((a // 2) * 128, (a // 2 + 1) * 128)
                pv = jnp.dot(vt_ref[pair, :], p_sc[a], preferred_element_type=F32)
                rs = slice(a * 64, (a + 1) * 64)
                own = slice((a % 2) * 64, (a % 2 + 1) * 64)
                acc_sc[rs, :] = acc_sc[rs, :] * al + pv[own, :]

        @pl.when(kb < qb)
        def _():
            step(False)

        @pl.when(kb == qb)
        def _():
            step(True)
            for a in range(NH):
                rs = slice(a * 64, (a + 1) * 64)
                acc_sc[rs, :] = acc_sc[rs, :] / l_sc[a]
                lse_ref[a:a + 1, :] = m_sc[a] + jnp.log(l_sc[a])
            o_ref[...] = acc_sc[...].T.astype(BF16)

        if n:
            @pl.when(lin == steps - 1)
            def _():
                _gather_phase(2, w_ins, w_outs, sems)

    grid_spec = pltpu.PrefetchScalarGridSpec(
        num_scalar_prefetch=2, grid=(MLA_HEADS // NH, T),
        in_specs=[pl.BlockSpec((tq, 128 * NH), lambda j, t, qb, kb: (qb[t], j)),
                  pl.BlockSpec((tq, 128 * NH), lambda j, t, qb, kb: (kb[t], j)),
                  pl.BlockSpec((64 * NH, tq), lambda j, t, qb, kb: (j, kb[t]))] + [_ANY] * n,
        out_specs=[pl.BlockSpec((tq, 64 * NH), lambda j, t, qb, kb: (qb[t], j)),
                   pl.BlockSpec((None, NH, tq), lambda j, t, qb, kb: (j, 0, qb[t]))] + [_ANY] * n,
        scratch_shapes=[pltpu.VMEM((NH, 1, tq), F32), pltpu.VMEM((NH, 1, tq), F32), pltpu.VMEM((64 * NH, tq), F32),
                        pltpu.VMEM((NH, tq, tq), F32), pltpu.VMEM((NH, tq, tq), BF16)] + (_gather_sems(n) if n else []),
    )
    out, lse, *gathered = pl.pallas_call(
        body, name="flash_fwd", grid_spec=grid_spec,
        out_shape=[_sds((S, 512), BF16), _sds((MLA_HEADS // NH, NH, S), F32)] + _gather_out_shapes(shards),
        compiler_params=_cp(("arbitrary", "arbitrary")),
    )(qb_of, kb_of, qp, kp, vt, *shards)
    return out, lse.reshape(MLA_HEADS // 2, 2, S), gathered


def _decay_table():
    log_g = np.log(1.0 - 2.0 ** (-5.0 - np.arange(RET_HEADS, dtype=np.float32))).astype(np.float32)
    return jnp.asarray(np.broadcast_to(log_g[:, None, None], (RET_HEADS, 8, 128)).copy())


def _decay_terms(lg_ref):
    C = RET_CHUNK
    lg = lg_ref[0:1, :]
    row = lax.broadcasted_iota(jnp.int32, (C, C), 0)
    col = lax.broadcasted_iota(jnp.int32, (C, C), 1)
    diff = (row - col).astype(F32)
    dmat = jnp.where(diff >= 0, jnp.exp(jnp.maximum(diff, 0.0) * jnp.tile(lg, (1, C // 128))), 0.0)
    j = lax.broadcasted_iota(jnp.int32, (C, 1), 0).astype(F32)
    lg1 = lg[:, 0:1]
    zeta = jnp.exp((C - 1 - j) * lg1)
    xi = jnp.exp((j + 1.0) * lg1)
    g_chunk = jnp.exp(C * lg1)
    return dmat, zeta, xi, g_chunk


def _ret_fwd(rq, rk, rv, rg, gn_w, S):
    C = RET_CHUNK
    N = S // C
    G = min(RET_GROUP, N)
    NB = N // G

    def body(lg_ref, q_ref, k_ref, v_ref, rg_ref, w_ref, ry_ref, ro_ref, rprev_ref, r_sc):
        @pl.when(pl.program_id(1) == 0)
        def _():
            r_sc[...] = jnp.zeros(r_sc.shape, F32)

        dmat, zeta, xi, g_chunk = _decay_terms(lg_ref)
        w = w_ref[...]
        r = r_sc[...]
        for i in range(G):
            rows = slice(i * C, (i + 1) * C)
            q = q_ref[rows, :]
            k = k_ref[rows, :]
            v = v_ref[rows, :]
            r_prev = r.astype(BF16)
            rprev_ref[i] = r_prev
            sc = _dot_nt(q, k) * dmat
            ry = _dot(sc, v) + jnp.dot(q, r_prev, preferred_element_type=F32) * xi
            ry_ref[rows, :] = ry
            r = g_chunk * r + _dot_tn(k, zeta * v.astype(F32))
            mu = jnp.mean(ry, axis=-1, keepdims=True)
            yc = ry - mu
            yh = yc * lax.rsqrt(jnp.mean(yc * yc, axis=-1, keepdims=True) + EPS)
            g = rg_ref[rows, :]
            ro_ref[rows, :] = (g * _sig(g) * (yh * w)).astype(BF16)
        r_sc[...] = r

    blk = pl.BlockSpec((G * C, 128), lambda h, n: (n, h))
    return pl.pallas_call(
        body, name="ret_fwd", grid=(RET_HEADS, NB),
        in_specs=[pl.BlockSpec((None, 8, 128), lambda h, n: (h, 0, 0)), blk, blk, blk, blk,
                  pl.BlockSpec((1, 128), lambda h, n: (0, h))],
        out_specs=[blk, blk, pl.BlockSpec((G, 128, 128), lambda h, n: (h * NB + n, 0, 0))],
        out_shape=[_sds((S, 512), F32), _sds((S, 512), BF16), _sds((RET_HEADS * N, 128, 128), BF16)],
        scratch_shapes=[pltpu.VMEM((128, 128), F32)],
        compiler_params=_cp(("parallel", "arbitrary")),
    )(_decay_table(), rq, rk, rv, rg, gn_w)


def _outproj(ro, mo, x, w_o, g_post, g_pre, S):
    tm = min(512, S)

    def body(ro_ref, mo_ref, x_ref, wo_ref, g1_ref, g2_ref, mix_ref, h1_ref, hn_ref):
        mix = (jnp.dot(ro_ref[...], wo_ref[0:512, :], preferred_element_type=F32)
               + jnp.dot(mo_ref[...], wo_ref[512:1024, :], preferred_element_type=F32))
        mix_ref[...] = mix.astype(BF16)
        h1 = x_ref[...] + _rms(mix, g1_ref[...])
        h1_ref[...] = h1
        hn_ref[...] = _rms(h1, g2_ref[...]).astype(BF16)

    return pl.pallas_call(
        body, name="outproj", grid=(S // tm,),
        in_specs=[_rows(tm, 512), _rows(tm, 512), _rows(tm, D_MODEL), _full(D_MODEL, D_MODEL), _full(1, D_MODEL),
                  _full(1, D_MODEL)],
        out_specs=[_rows(tm, D_MODEL)] * 3,
        out_shape=[_sds((S, D_MODEL), BF16), _sds((S, D_MODEL), F32), _sds((S, D_MODEL), BF16)],
        compiler_params=_cp(("parallel",)),
    )(ro, mo, x, w_o, g_post, g_pre)


def _ffn_up(hn, w_gate_t, w_up_t, S):
    tm = min(512, S)
    tn = D_FF // 2

    def body(hn_ref, wg_ref, wu_ref, fg_ref, fu_ref, act_ref):
        hn_b = hn_ref[...]
        g = _dot_nt(hn_b, wg_ref[...])
        u = _dot_nt(hn_b, wu_ref[...])
        s = _sig(g)
        silu = g * s
        fg_ref[...] = (u * (s + silu * (1.0 - s))).astype(BF16)
        fu_ref[...] = silu.astype(BF16)
        act_ref[...] = (silu * u).astype(BF16)

    wspec = pl.BlockSpec((tn, D_MODEL), lambda j, i: (j, 0))
    ospec = pl.BlockSpec((tm, tn), lambda j, i: (i, j))
    return pl.pallas_call(
        body, name="ffn_up", grid=(2, S // tm),
        in_specs=[pl.BlockSpec((tm, D_MODEL), lambda j, i: (i, 0)), wspec, wspec],
        out_specs=[ospec] * 3, out_shape=[_sds((S, D_FF), BF16)] * 3,
        compiler_params=_cp(("parallel", "parallel")),
    )(hn, w_gate_t, w_up_t)


def _ffn_down(act, w_down, h1, g, S):
    tm = min(512, S)

    def body(act_ref, wd_ref, h1_ref, g_ref, ff_ref, h2_ref):
        ff = jnp.dot(act_ref[...], wd_ref[...], preferred_element_type=F32)
        ff_ref[...] = ff.astype(BF16)
        h2_ref[...] = h1_ref[...] + _rms(ff, g_ref[...])

    return pl.pallas_call(
        body, name="ffn_down", grid=(S // tm,),
        in_specs=[_rows(tm, D_FF), _full(D_FF, D_MODEL), _rows(tm, D_MODEL), _full(1, D_MODEL)],
        out_specs=[_rows(tm, D_MODEL)] * 2, out_shape=[_sds((S, D_MODEL), BF16), _sds((S, D_MODEL), F32)],
        compiler_params=_cp(("parallel",)),
    )(act, w_down, h1, g)


def _ple_loss(p, h2, tgt, w_pp, w_pg, b_pg, g_ple, S):
    tm = min(512, S)

    def body(p_ref, h2_ref, t_ref, wp_ref, wg_ref, b_ref, gp_ref,
             dz_ref, dpe_ref, dh2_ref, h2b_ref, loss_ref, dgp_ref, db_ref):
        @pl.when(pl.program_id(0) == 0)
        def _():
            loss_ref[...] = jnp.zeros(loss_ref.shape, F32)
            dgp_ref[...] = jnp.zeros(dgp_ref.shape, F32)
            db_ref[...] = jnp.zeros(db_ref.shape, F32)

        gp = gp_ref[...]
        pe = _dot(p_ref[...], wp_ref[...])
        r = lax.rsqrt(jnp.mean(pe * pe, axis=-1, keepdims=True) + EPS)
        peh = pe * r
        e = peh * gp
        h2 = h2_ref[...]
        h2b = h2.astype(BF16)
        h2b_ref[...] = h2b
        gt = _sig(jnp.dot(h2b, wg_ref[...], preferred_element_type=F32) + b_ref[...])
        diff = h2 + e * gt - t_ref[...]
        loss_ref[...] += _colsum(diff * diff)
        dh3 = diff * (1.0 / D_MODEL)
        de = dh3 * gt
        dz = dh3 * e * gt * (1.0 - gt)
        db_ref[...] += _colsum(dz)
        dgp_ref[...] += _colsum(de * peh)
        dpeh = de * gp
        dpe = r * (dpeh - peh * jnp.mean(dpeh * peh, axis=-1, keepdims=True))
        dzb = dz.astype(BF16)
        dz_ref[...] = dzb
        dpe_ref[...] = dpe.astype(BF16)
        dh2_ref[...] = dh3 + _dot_nt(dzb, wg_ref[...])

    return pl.pallas_call(
        body, name="ple_loss", grid=(S // tm,),
        in_specs=[_rows(tm, PLE_DIM), _rows(tm, D_MODEL), _rows(tm, D_MODEL), _full(PLE_DIM, D_MODEL),
                  _full(D_MODEL, D_MODEL), _full(1, D_MODEL), _full(1, D_MODEL)],
        out_specs=[_rows(tm, D_MODEL)] * 4 + [_acc(1, D_MODEL)] * 3,
        out_shape=[_sds((S, D_MODEL), BF16), _sds((S, D_MODEL), BF16), _sds((S, D_MODEL), F32), _sds((S, D_MODEL), BF16)]
        + [_sds((1, D_MODEL), F32)] * 3,
        compiler_params=_cp(("arbitrary",)),
    )(p, h2, tgt, w_pp, w_pg, b_pg, g_ple)


def _wgrad(a, b, name, S):
    M = a.shape[1]
    N = b.shape[1]
    ts = min(2048, S)
    nsplit = 2 if M * N >= 2 * 1024 * 1024 else 1
    tn = N // nsplit

    def body(a_ref, b_ref, o_ref):
        @pl.when(pl.program_id(1) == 0)
        def _():
            o_ref[...] = jnp.zeros(o_ref.shape, F32)

        o_ref[...] += _dot_tn(a_ref[...], b_ref[...])

    return pl.pallas_call(
        body, name=name, grid=(nsplit, S // ts),
        in_specs=[pl.BlockSpec((ts, M), lambda j, s: (s, 0)), pl.BlockSpec((ts, tn), lambda j, s: (s, j))],
        out_specs=pl.BlockSpec((M, tn), lambda j, s: (0, j)), out_shape=_sds((M, N), F32),
        compiler_params=_cp(("parallel", "arbitrary")),
    )(a, b)


def _ffn_down_bwd(dh2, ff, g, w_down, dgate_f, dup_f, S):
    tm = min(512, S)
    tn = D_FF // 2

    def body(dh2_ref, ff_ref, g_ref, wd_ref, fg_ref, fu_ref, dff_ref, dgate_ref, dup_ref, dg_ref):
        @pl.when(pl.program_id(0) == 0)
        def _():
            dg_ref[...] = jnp.zeros(dg_ref.shape, F32)

        dff, ga = _rms_bwd(dh2_ref[...], ff_ref[...].astype(F32), g_ref[...])
        dg_ref[...] += _colsum(ga)
        dffb = dff.astype(BF16)
        dff_ref[...] = dffb
        for seg in range(2):
            sl = slice(seg * tn, (seg + 1) * tn)
            dact = _dot_nt(dffb, wd_ref[sl, :])
            dgate_ref[:, sl] = (dact * fg_ref[:, sl].astype(F32)).astype(BF16)
            dup_ref[:, sl] = (dact * fu_ref[:, sl].astype(F32)).astype(BF16)

    return pl.pallas_call(
        body, name="ffn_down_bwd", grid=(S // tm,),
        in_specs=[_rows(tm, D_MODEL), _rows(tm, D_MODEL), _full(1, D_MODEL), _full(D_FF, D_MODEL), _rows(tm, D_FF),
                  _rows(tm, D_FF)],
        out_specs=[_rows(tm, D_MODEL), _rows(tm, D_FF), _rows(tm, D_FF), _acc(1, D_MODEL)],
        out_shape=[_sds((S, D_MODEL), BF16), _sds((S, D_FF), BF16), _sds((S, D_FF), BF16), _sds((1, D_MODEL), F32)],
        compiler_params=_cp(("arbitrary",)),
    )(dh2, ff, g, w_down, dgate_f, dup_f)


def _ffn_up_bwd(dgate, dup, w_gate, w_up, h1, mix, dh2, g_pre, g_post, w_o, S, grads=()):
    tm = min(512, S)
    n = len(grads)
    last = S // tm - 1

    def body(dgate_ref, dup_ref, wg_ref, wu_ref, h1_ref, mix_ref, dh2_ref, g2_ref, g1_ref, wo_ref, *rest):
        g_ins = rest[:n]
        dh1_ref, dmix_ref, dro_ref, dmo_ref, dg2_ref, dg1_ref = rest[n:n + 6]
        g_outs, sems = rest[n + 6:2 * n + 6], rest[2 * n + 6:]

        @pl.when(pl.program_id(0) == 0)
        def _():
            dg2_ref[...] = jnp.zeros(dg2_ref.shape, F32)
            dg1_ref[...] = jnp.zeros(dg1_ref.shape, F32)
            for cp in (_swap_copies(g_ins, g_outs, sems) if n else []):
                cp.start()

        dhn = (jnp.dot(dgate_ref[...], wg_ref[...], preferred_element_type=F32)
               + jnp.dot(dup_ref[...], wu_ref[...], preferred_element_type=F32))
        d1, ga = _rms_bwd(dhn, h1_ref[...], g2_ref[...])
        dg2_ref[...] += _colsum(ga)
        dh1 = dh2_ref[...] + d1
        dh1_ref[...] = dh1
        dmix, gb = _rms_bwd(dh1, mix_ref[...].astype(F32), g1_ref[...])
        dg1_ref[...] += _colsum(gb)
        dmixb = dmix.astype(BF16)
        dmix_ref[...] = dmixb
        dcat = _dot_nt(dmixb, wo_ref[...])
        dro_ref[...] = dcat[:, 0:512].astype(BF16)
        dmo_ref[...] = dcat[:, 512:1024].astype(BF16)

        if n:
            @pl.when(pl.program_id(0) == last)
            def _():
                for cp in _swap_copies(g_ins, g_outs, sems):
                    cp.wait()

    dh1, dmix, dro, dmo, dg2, dg1, *got = pl.pallas_call(
        body, name="ffn_up_bwd", grid=(S // tm,),
        in_specs=[_rows(tm, D_FF), _rows(tm, D_FF), _full(D_FF, D_MODEL), _full(D_FF, D_MODEL), _rows(tm, D_MODEL),
                  _rows(tm, D_MODEL), _rows(tm, D_MODEL), _full(1, D_MODEL), _full(1, D_MODEL), _full(D_MODEL, D_MODEL)]
        + [_ANY] * n,
        out_specs=[_rows(tm, D_MODEL), _rows(tm, D_MODEL), _rows(tm, 512), _rows(tm, 512), _acc(1, D_MODEL),
                   _acc(1, D_MODEL)] + [_ANY] * n,
        out_shape=[_sds((S, D_MODEL), F32), _sds((S, D_MODEL), BF16), _sds((S, 512), BF16), _sds((S, 512), BF16),
                   _sds((1, D_MODEL), F32), _sds((1, D_MODEL), F32)] + _swap_out_shapes(grads),
        scratch_shapes=_swap_sems(n) if n else [],
        compiler_params=_cp(("arbitrary",)),
    )(dgate, dup, w_gate, w_up, h1, mix, dh2, g_pre, g_post, w_o, *grads)
    return dh1, dmix, dro, dmo, dg2, dg1, got


def _attn_delta(o, do, S, grads=()):
    tm = min(512, S)
    n = len(grads)
    last = S // tm - 1

    def body(o_ref, do_ref, *rest):
        g_ins, (dot_ref, d_ref), g_outs, sems = rest[:n], rest[n:n + 2], rest[n + 2:2 * n + 2], rest[2 * n + 2:]
        if n:
            @pl.when(pl.program_id(0) == 0)
            def _():
                for cp in _swap_copies(g_ins, g_outs, sems):
                    cp.start()

        do = do_ref[...].astype(F32)
        prod_t = (o_ref[...].astype(F32) * do).T
        dot_ref[...] = do.T.astype(BF16)
        for h in range(MLA_HEADS):
            d_ref[h // 2, (h % 2):(h % 2) + 1, :] = jnp.sum(prod_t[h * 64:(h + 1) * 64, :], axis=0, keepdims=True)

        if n:
            @pl.when(pl.program_id(0) == last)
            def _():
                for cp in _swap_copies(g_ins, g_outs, sems):
                    cp.wait()

    dot, delta, *got = pl.pallas_call(
        body, name="attn_delta", grid=(S // tm,),
        in_specs=[_rows(tm, 512), _rows(tm, 512)] + [_ANY] * n,
        out_specs=[pl.BlockSpec((512, tm), lambda i: (0, i)), pl.BlockSpec((MLA_HEADS // 2, 2, tm), lambda i: (0, 0, i))]
        + [_ANY] * n,
        out_shape=[_sds((512, S), BF16), _sds((MLA_HEADS // 2, 2, S), F32)] + _swap_out_shapes(grads),
        scratch_shapes=_swap_sems(n) if n else [],
        compiler_params=_cp(("arbitrary",)),
    )(o, do, *grads)
    return dot, delta, got


def _flash_bwd(qp, kp, kt, v, do, dot, lse, delta, S, sums=()):
    tq = min(512, S)
    nq = S // tq
    RB = ATT_ROWS
    NH = BWD_HEADS
    qb_of, kb_of, T = _tri_pairs(nq, k_major=True)
    n = len(sums)
    steps = (MLA_HEADS // NH) * T

    def body(qb_ref, kb_ref, q_ref, k_ref, kt_ref, v_ref, do_ref, dot_ref, lse_ref, dl_ref, *rest):
        g_ins, (dq_ref, dk_ref, dv_ref), g_outs = rest[:n], rest[n:n + 3], rest[n + 3:2 * n + 3]
        dk_sc, dv_sc, s_sc, dp_sc, p_sc, ds_sc = rest[2 * n + 3:2 * n + 9]
        sems = rest[2 * n + 9:]
        t = pl.program_id(1)
        qb = qb_ref[t]
        kb = kb_ref[t]
        lin = pl.program_id(0) * T + t

        if n:
            @pl.when(lin == 0)
            def _():
                for cp in _scatter_copies(g_ins, g_outs, sems):
                    cp.start()

        @pl.when(t == 0)
        def _():
            dq_ref[...] = jnp.zeros(dq_ref.shape, F32)

        @pl.when(qb == kb)
        def _():
            dk_sc[...] = jnp.zeros(dk_sc.shape, F32)
            dv_sc[...] = jnp.zeros(dv_sc.shape, F32)

        lane = lax.broadcasted_iota(jnp.int32, (tq, 64 * NH), 1)

        def step(masked):
            vv = v_ref[...]
            do_all = do_ref[...]
            mine = [(lane >= a * 64) & (lane < (a + 1) * 64) for a in range(NH)]
            for a in range(NH):
                sl = slice(a * 128, (a + 1) * 128)
                s_sc[a] = _dot_nt(k_ref[:, sl], q_ref[:, sl])
                dp_sc[a] = jnp.dot(jnp.where(mine[a], vv, jnp.zeros_like(vv)), dot_ref[...],
                                   preferred_element_type=F32)
            for a in range(NH):
                sl = slice(a * 128, (a + 1) * 128)
                lse = lse_ref[a:a + 1, :]
                dl = dl_ref[a:a + 1, :]
                for r in range(0, tq, RB):
                    sc = s_sc[a, r:r + RB, :]
                    if masked:
                        sc = jnp.where(_causal_keep(r, RB, tq), sc, NEG)
                    p = jnp.exp(sc - lse)
                    p_sc[a, r:r + RB, :] = p.astype(BF16)
                    ds_sc[a, r:r + RB, :] = (p * (dp_sc[a, r:r + RB, :] - dl)).astype(BF16)
                ds = ds_sc[a]
                dv_sc[...] += jnp.dot(p_sc[a], jnp.where(mine[a], do_all, jnp.zeros_like(do_all)),
                                      preferred_element_type=F32)
                dk_sc[:, sl] += jnp.dot(ds, q_ref[:, sl], preferred_element_type=F32)
                dq_ref[qb, sl, :] += jnp.dot(kt_ref[sl, :], ds, preferred_element_type=F32)

        @pl.when(qb > kb)
        def _():
            step(False)

        @pl.when(qb == kb)
        def _():
            step(True)

        @pl.when(qb == nq - 1)
        def _():
            dk_ref[...] = dk_sc[...].astype(BF16)
            dv_ref[...] = dv_sc[...].astype(BF16)

        if n:
            @pl.when(lin == steps - 1)
            def _():
                for cp in _scatter_copies(g_ins, g_outs, sems):
                    cp.wait()

    grid_spec = pltpu.PrefetchScalarGridSpec(
        num_scalar_prefetch=2, grid=(MLA_HEADS // NH, T),
        in_specs=[pl.BlockSpec((tq, 128 * NH), lambda j, t, qb, kb: (qb[t], j)),
                  pl.BlockSpec((tq, 128 * NH), lambda j, t, qb, kb: (kb[t], j)),
                  pl.BlockSpec((128 * NH, tq), lambda j, t, qb, kb: (j, kb[t])),
                  pl.BlockSpec((tq, 64 * NH), lambda j, t, qb, kb: (kb[t], j)),
                  pl.BlockSpec((tq, 64 * NH), lambda j, t, qb, kb: (qb[t], j)),
                  pl.BlockSpec((64 * NH, tq), lambda j, t, qb, kb: (j, qb[t])),
                  pl.BlockSpec((None, NH, tq), lambda j, t, qb, kb: (j, 0, qb[t])),
                  pl.BlockSpec((None, NH, tq), lambda j, t, qb, kb: (j, 0, qb[t]))] + [_ANY] * n,
        out_specs=[pl.BlockSpec((nq, 128 * NH, tq), lambda j, t, qb, kb: (0, j, 0), pipeline_mode=pl.Buffered(1)),
                   pl.BlockSpec((tq, 128 * NH), lambda j, t, qb, kb: (kb[t], j)),
                   pl.BlockSpec((tq, 64 * NH), lambda j, t, qb, kb: (kb[t], j))] + [_ANY] * n,
        scratch_shapes=[pltpu.VMEM((tq, 128 * NH), F32), pltpu.VMEM((tq, 64 * NH), F32), pltpu.VMEM((NH, tq, tq), F32),
                        pltpu.VMEM((NH, tq, tq), F32), pltpu.VMEM((NH, tq, tq), BF16), pltpu.VMEM((NH, tq, tq), BF16)]
        + (_scatter_sems(n) if n else []),
    )
    dq, dk, dv, *parts = pl.pallas_call(
        body, name="flash_bwd", grid_spec=grid_spec,
        out_shape=[_sds((nq, 1024, tq), F32), _sds((S, 1024), BF16), _sds((S, 512), BF16)] + _scatter_out_shapes(sums),
        compiler_params=_cp(("arbitrary", "arbitrary")),
    )(qb_of, kb_of, qp, kp, kt, v, do, dot, lse.reshape(MLA_HEADS // NH, NH, S), delta.reshape(MLA_HEADS // NH, NH, S),
      *sums)
    return dq, dk, dv, parts


def _mla_up_bwd(dqp, dkp, dv, cq, ckv, gq, gkv, w_uq, w_ukv, tabs, S):
    tm = min(512, S)

    def body(dq_ref, dk_ref, dv_ref, cq_ref, ckv_ref, gq_ref, gkv_ref, wuq_ref, wukv_ref, cm_ref, sa_ref, sb_ref,
             dqh_ref, dkv_ref, dcq_ref, dckv_ref, dkr_ref, dgq_ref, dgkv_ref):
        @pl.when(pl.program_id(0) == 0)
        def _():
            dgq_ref[...] = jnp.zeros(dgq_ref.shape, F32)
            dgkv_ref[...] = jnp.zeros(dgkv_ref.shape, F32)

        cm = cm_ref[...]
        sa = sa_ref[...]
        sb = sb_ref[...]
        lane = lax.broadcasted_iota(jnp.int32, (tm, 128), 1)
        dkr_r = jnp.zeros((tm, 128), F32)
        for h in range(MLA_HEADS):
            sl = slice(h * 128, (h + 1) * 128)
            dqh_ref[:, sl] = (_unrope_mla(dq_ref[sl, :].T, cm, sa, sb) * SCALE_MLA).astype(BF16)
            gk = dk_ref[:, sl]
            dkr_r = dkr_r + gk.astype(F32)
            dkv_ref[:, sl] = gk
        dkr_r = jnp.where((lane >= 64) & (lane < 96), dkr_r, 0.0)
        dkr_ref[...] = _unrope_mla(dkr_r, cm, sa, sb).astype(BF16)
        dkv_ref[:, 1024:1536] = dv_ref[...]
        dcq, ga = _rms_bwd(_dot_nt(dqh_ref[...], wuq_ref[...]), cq_ref[...], gq_ref[...])
        dcq_ref[...] = dcq.astype(BF16)
        dgq_ref[...] += _colsum(ga)
        dckv, gb = _rms_bwd(_dot_nt(dkv_ref[...], wukv_ref[...]), ckv_ref[...], gkv_ref[...])
        dckv_ref[...] = dckv.astype(BF16)
        dgkv_ref[...] += _colsum(gb)

    per_q = dqp.shape[2] // tm
    return pl.pallas_call(
        body, name="mla_up_bwd", grid=(S // tm,),
        in_specs=[pl.BlockSpec((None, 1024, tm), lambda i: (i // per_q, 0, i % per_q)),
                  _rows(tm, 1024), _rows(tm, 512), _rows(tm, Q_LORA), _rows(tm, KV_LORA),
                  _full(1, Q_LORA), _full(1, KV_LORA), _full(Q_LORA, 1024), _full(KV_LORA, 1536)] + [_rows(tm, 128)] * 3,
        out_specs=[_rows(tm, 1024), _rows(tm, 1536), _rows(tm, Q_LORA), _rows(tm, KV_LORA), _rows(tm, 128),
                   _acc(1, Q_LORA), _acc(1, KV_LORA)],
        out_shape=[_sds((S, 1024), BF16), _sds((S, 1536), BF16), _sds((S, Q_LORA), BF16), _sds((S, KV_LORA), BF16),
                   _sds((S, 128), BF16), _sds((1, Q_LORA), F32), _sds((1, KV_LORA), F32)],
        compiler_params=_cp(("arbitrary",)),
    )(dqp, dkp, dv, cq, ckv, gq, gkv, w_uq, w_ukv, *tabs[2:])


def _ret_bwd(rq, rk, rv, rprev, ry, rg, dro, gn_w, tabs, S):
    C = RET_CHUNK
    N = S // C
    G = min(RET_GROUP, N)
    NB = N // G

    def body(lg_ref, q_ref, k_ref, v_ref, rp_ref, ry_ref, rg_ref, dro_ref, w_ref, cr_ref, sr_ref,
             drq_ref, drk_ref, drv_ref, drg_ref, dw_ref, g_sc):
        @pl.when(pl.program_id(1) == 0)
        def _():
            g_sc[...] = jnp.zeros(g_sc.shape, F32)
            dw_ref[...] = jnp.zeros(dw_ref.shape, F32)

        dmat, zeta, xi, g_chunk = _decay_terms(lg_ref)
        w = w_ref[...]
        gacc = g_sc[...]
        dw = jnp.zeros((1, 128), F32)
        for i in reversed(range(G)):
            rows = slice(i * C, (i + 1) * C)
            ry = ry_ref[rows, :]
            mu = jnp.mean(ry, axis=-1, keepdims=True)
            yc = ry - mu
            rstd = lax.rsqrt(jnp.mean(yc * yc, axis=-1, keepdims=True) + EPS)
            yh = yc * rstd
            g = rg_ref[rows, :]
            s = _sig(g)
            dout = dro_ref[rows, :].astype(F32)
            drg_ref[rows, :] = (dout * (yh * w) * (s * (1.0 + g * (1.0 - s)))).astype(BF16)
            dgn = dout * (g * s)
            dw = dw + _colsum(dgn * yh)
            dyh = dgn * w
            dry = rstd * (dyh - jnp.mean(dyh, axis=-1, keepdims=True) - yh * jnp.mean(dyh * yh, axis=-1, keepdims=True))
            do = dry.astype(BF16)

            q = q_ref[rows, :]
            k = k_ref[rows, :]
            v = v_ref[rows, :]
            gfut = gacc.astype(BF16)
            sc = (_dot_nt(q, k) * dmat).astype(BF16)
            dsc = (_dot_nt(do, v) * dmat).astype(BF16)
            dq = jnp.dot(dsc, k, preferred_element_type=F32) + _dot_nt(do, rp_ref[i]) * xi
            dk = _dot_tn(dsc, q) + _dot_nt(v, gfut) * zeta
            dv = _dot_tn(sc, do) + jnp.dot(k, gfut, preferred_element_type=F32) * zeta
            gacc = g_chunk * gacc + _dot_tn(q, xi * dry)
            cr = cr_ref[rows, :]
            sr = sr_ref[rows, :]
            drq_ref[rows, :] = _unrope_ret(dq, cr, sr).astype(BF16)
            drk_ref[rows, :] = _unrope_ret(dk * SCALE_RET, cr, sr).astype(BF16)
            drv_ref[rows, :] = dv.astype(BF16)
        g_sc[...] = gacc
        dw_ref[...] += dw

    blk = pl.BlockSpec((G * C, 128), lambda h, n: (NB - 1 - n, h))
    tab = pl.BlockSpec((G * C, 128), lambda h, n: (NB - 1 - n, 0))
    return pl.pallas_call(
        body, name="ret_bwd", grid=(RET_HEADS, NB),
        in_specs=[pl.BlockSpec((None, 8, 128), lambda h, n: (h, 0, 0)), blk, blk, blk,
                  pl.BlockSpec((G, 128, 128), lambda h, n: (h * NB + NB - 1 - n, 0, 0)), blk, blk, blk,
                  pl.BlockSpec((1, 128), lambda h, n: (0, h)), tab, tab],
        out_specs=[blk, blk, blk, blk, pl.BlockSpec((1, 128), lambda h, n: (0, h))],
        out_shape=[_sds((S, 512), BF16)] * 4 + [_sds((1, 512), F32)],
        scratch_shapes=[pltpu.VMEM((128, 128), F32)],
        compiler_params=_cp(("parallel", "arbitrary")),
    )(_decay_table(), rq, rk, rv, rprev, ry, rg, dro, gn_w, tabs[0], tabs[1])


def _inproj_bwd(drq, drk, drv, drg, dcq, dckv, dkr, w_in, dh1, x, g, S):
    tm = min(512, S)

    def body(drq_ref, drk_ref, drv_ref, drg_ref, dcq_ref, dckv_ref, dkr_ref, w_ref, dh1_ref, x_ref, g_ref,
             gx_ref, dproj_ref, dg_ref):
        @pl.when(pl.program_id(0) == 0)
        def _():
            dg_ref[...] = jnp.zeros(dg_ref.shape, F32)

        dproj_ref[:, 0:512] = drq_ref[...]
        dproj_ref[:, 512:1024] = drk_ref[...]
        dproj_ref[:, 1024:1536] = drv_ref[...]
        dproj_ref[:, 1536:2048] = drg_ref[...]
        dproj_ref[:, 2048:2432] = dcq_ref[...]
        dproj_ref[:, 2432:2688] = dckv_ref[...]
        dproj_ref[:, 2688:2816] = dkr_ref[...]
        dx, ga = _rms_bwd(_dot_nt(dproj_ref[...], w_ref[...]), x_ref[...], g_ref[...])
        gx_ref[...] = dh1_ref[...] + dx
        dg_ref[...] += _colsum(ga)

    return pl.pallas_call(
        body, name="inproj_bwd", grid=(S // tm,),
        in_specs=[_rows(tm, 512)] * 4 + [_rows(tm, Q_LORA), _rows(tm, KV_LORA), _rows(tm, 128),
                                         _full(D_MODEL, IN_COLS_P), _rows(tm, D_MODEL), _rows(tm, D_MODEL),
                                         _full(1, D_MODEL)],
        out_specs=[_rows(tm, D_MODEL), _rows(tm, IN_COLS_P), _acc(1, D_MODEL)],
        out_shape=[_sds((S, D_MODEL), F32), _sds((S, IN_COLS_P), BF16), _sds((1, D_MODEL), F32)],
        compiler_params=_cp(("arbitrary",)),
    )(drq, drk, drv, drg, dcq, dckv, dkr, w_in, dh1, x, g)


def _pad_weights(w):
    w_in = w["w_in"]
    z = lambda r, c: jnp.zeros((r, c), BF16)
    w_in_p = jnp.concatenate([w_in[:, :2688], z(1024, 64), w_in[:, 2688:2720], z(1024, 32)], axis=1)
    w_uq_p = jnp.pad(w["w_uq"].reshape(Q_LORA, MLA_HEADS, 96), ((0, 0), (0, 0), (0, 32))).reshape(Q_LORA, 1024)
    ukv = w["w_ukv"].reshape(KV_LORA, MLA_HEADS, 128)
    k_part = jnp.pad(ukv[:, :, :64], ((0, 0), (0, 0), (0, 64))).reshape(KV_LORA, 1024)
    w_ukv_p = jnp.concatenate([k_part, ukv[:, :, 64:].reshape(KV_LORA, 512)], axis=1)
    return w_in_p, w_uq_p, w_ukv_p


BIG_SPEC = {n: (r, c, ax) for n, r, c, ax in BIG}
COLUMN_MAJOR = ("w_in", "w_uq", "w_gate", "w_up")
GRAD_TRANSPOSED = ("w_gate", "w_up")
GATHER_FIRST = ("w_in", "w_uq", "w_ukv")
GATHER_LATE = tuple(n for n, _, _, _ in BIG if n not in GATHER_FIRST)
REDUCE_EARLY = ("w_ple_gate", "w_ple_proj", "w_down", "w_gate", "w_up", "w_o")
REDUCE_LAST = tuple(n for n, _, _, _ in BIG if n not in REDUCE_EARLY)


def _local_step(x, p, pos_f, tgt, w, sm, late_shards=None, c_idx=None):
    S = x.shape[0]
    spread = late_shards is not None
    w = dict(w)
    tabs, first = _rope_tables(pos_f, S, [late_shards[n] for n in GATHER_FIRST] if spread else ())
    for i, n in enumerate(GATHER_FIRST if spread else ()):
        w[n] = _from_chips(first[i], BIG_SPEC[n][2])
    w_in_p, w_uq_p, w_ukv_p = _pad_weights(w)

    xn, rq, rk, rv, rg, cq, ckv, kr = _inproj(x, sm["pre_mix_norm"], w_in_p, tabs, S)
    cqn, ckvn, qp, kp, v, kt, vt = _mla_up(cq, ckv, kr, sm["mla_q_norm"], sm["mla_kv_norm"], w_uq_p, w_ukv_p, tabs, S)
    mo, lse, gathered = _flash_fwd(qp, kp, vt, S, [late_shards[n] for n in GATHER_LATE] if spread else ())
    for i, n in enumerate(GATHER_LATE if spread else ()):
        w[n] = _from_chips(gathered[i], 0 if n in GRAD_TRANSPOSED else BIG_SPEC[n][2])
    if not spread:
        w.update({n: w[n].T for n in GRAD_TRANSPOSED})
    ry, ro, rprev = _ret_fwd(rq, rk, rv, rg, sm["ret_gn_w"], S)
    mix, h1, hn = _outproj(ro, mo, x, w["w_o"], sm["post_mix_norm"], sm["pre_ffn_norm"], S)
    dgate_f, dup_f, act = _ffn_up(hn, w["w_gate"], w["w_up"], S)
    ff, h2 = _ffn_down(act, w["w_down"], h1, sm["post_ffn_norm"], S)
    dz, dpe, dh2, h2b, loss_vec, d_ple_norm, d_b = _ple_loss(
        p, h2, tgt, w["w_ple_proj"], w["w_ple_gate"], sm["b_ple_gate"], sm["ple_norm"], S)

    gw = {}
    gs = {"ple_norm": d_ple_norm, "b_ple_gate": d_b}
    gw["w_ple_gate"] = _wgrad(h2b, dz, "wgrad_ple_gate", S)
    gw["w_ple_proj"] = _wgrad(p, dpe, "wgrad_ple_proj", S)
    dff, dgate, dup, gs["post_ffn_norm"] = _ffn_down_bwd(dh2, ff, sm["post_ffn_norm"], w["w_down"], dgate_f, dup_f, S)
    gw["w_down"] = _wgrad(act, dff, "wgrad_down", S)
    if spread:
        gw["w_gate"] = _wgrad(dgate, hn, "wgrad_gate", S)
        gw["w_up"] = _wgrad(dup, hn, "wgrad_up", S)
    else:
        gw["w_gate"] = _wgrad(hn, dgate, "wgrad_gate", S)
        gw["w_up"] = _wgrad(hn, dup, "wgrad_up", S)
    first = REDUCE_EARLY[:-1]
    g4 = [_by_chip(gw.pop(n), *((D_FF, D_MODEL, 0) if n in GRAD_TRANSPOSED else BIG_SPEC[n]))
          for n in first] if spread else []
    dh1, dmix, dro, dmo, gs["pre_ffn_norm"], gs["post_mix_norm"], got = _ffn_up_bwd(
        dgate, dup, w["w_gate"], w["w_up"], h1, mix, dh2, sm["pre_ffn_norm"], sm["post_mix_norm"], w["w_o"], S, g4)
    gw["w_o"] = jnp.concatenate([_wgrad(ro, dmix, "wgrad_o_ret", S), _wgrad(mo, dmix, "wgrad_o_mla", S)], axis=0)
    g4_o = [_by_chip(gw.pop("w_o"), *BIG_SPEC["w_o"])] if spread else []

    dmo_t, delta, got_o = _attn_delta(mo, dmo, S, g4_o)
    sums = [_add_half_rows(a, b, c_idx, "rs_add_halves_" + n)
            for n, a, b in zip(REDUCE_EARLY, g4 + g4_o, list(got) + list(got_o))] if spread else []
    dqp, dkp, dv, parts = _flash_bwd(qp, kp, kt, v, dmo, dmo_t, lse, delta, S, sums)
    dqh, dkv, dcq, dckv, dkr, gs["mla_q_norm"], gs["mla_kv_norm"] = _mla_up_bwd(
        dqp, dkp, dv, cq, ckv, sm["mla_q_norm"], sm["mla_kv_norm"], w_uq_p, w_ukv_p, tabs, S)
    g_uq_p = _wgrad(cqn, dqh, "wgrad_uq", S)
    g_ukv_p = _wgrad(ckvn, dkv, "wgrad_ukv", S)
    gw["w_uq"] = g_uq_p.reshape(Q_LORA, MLA_HEADS, 128)[:, :, :96].reshape(Q_LORA, 768)
    gw["w_ukv"] = jnp.concatenate(
        [g_ukv_p[:, :1024].reshape(KV_LORA, MLA_HEADS, 128)[:, :, :64], g_ukv_p[:, 1024:].reshape(KV_LORA, MLA_HEADS, 64)],
        axis=2).reshape(KV_LORA, 1024)

    drq, drk, drv, drg, gs["ret_gn_w"] = _ret_bwd(rq, rk, rv, rprev, ry, rg, dro, sm["ret_gn_w"], tabs, S)
    grad_x, dproj, gs["pre_mix_norm"] = _inproj_bwd(drq, drk, drv, drg, dcq, dckv, dkr, w_in_p, dh1, x,
                                                    sm["pre_mix_norm"], S)
    g_in_p = _wgrad(xn, dproj, "wgrad_in", S)
    gw["w_in"] = jnp.concatenate([g_in_p[:, :2688], g_in_p[:, 2752:2784]], axis=1)
    return loss_vec, grad_x, gw, gs, ((sums, parts) if spread else None)


def _my_place():
    x = lax.axis_index("x")
    y = lax.axis_index("y")
    c = lax.axis_index("c")
    return x, y, c


def _other_chips(x, y):
    return [(1 - x, y), (x, 1 - y), (1 - x, 1 - y)]


_ANY = pl.BlockSpec(memory_space=pl.ANY)


def _small_copies(v_ref, slots, sems):
    send, recv, lsem = sems
    x, y, c = _my_place()
    me = 4 * x + 2 * y + c
    cps = [pltpu.make_async_copy(v_ref, slots.at[me], lsem)]
    for r in range(1, N_DEV):
        peer = (x ^ (r >> 2), y ^ ((r >> 1) & 1), c ^ (r & 1))
        cps.append(pltpu.make_async_remote_copy(
            src_ref=v_ref, dst_ref=slots.at[me], send_sem=send.at[r - 1], recv_sem=recv.at[r - 1],
            device_id=peer, device_id_type=MESH))
    return cps


def _small_sum(slots, out_ref):
    acc = slots[0]
    for d in range(1, N_DEV):
        acc = acc + slots[d]
    out_ref[...] = acc
    loss = jnp.sum(acc[9:10, :], axis=1, keepdims=True) * (0.5 / D_MODEL)
    out_ref[9:10, :] = jnp.broadcast_to(loss, (1, PACK_COLS))


def _small_scratch():
    return [pltpu.VMEM((N_DEV, SMALL_ROWS, PACK_COLS), F32), pltpu.SemaphoreType.DMA((N_DEV - 1,)),
            pltpu.SemaphoreType.DMA((N_DEV - 1,)), pltpu.SemaphoreType.DMA]


N_BIG = len(BIG)


def _half(c, rows, align):
    h = rows // 2
    return pl.ds(pl.multiple_of(c * h, align), h)


def _gather_out_shapes(shards):
    return [_sds((N_CHIPS,) + tuple(s.shape), BF16) for s in shards]


def _gather_sems(n):
    return [pltpu.SemaphoreType.DMA((n, 3))] * 4 + [pltpu.SemaphoreType.DMA((n,))] * 2


def _gather_phase(phase, ins, outs, sems):
    send1, recv1, send2, recv2, send3, recv3 = sems
    x, y, c = _my_place()
    me = 2 * x + y
    chips = _other_chips(x, y)
    sib = (x, y, 1 - c)
    for t in range(len(ins)):
        rows = ins[t].shape[0]
        half = _half(c, rows, 16)
        other = _half(1 - c, rows, 16)
        def own():
            return pltpu.make_async_remote_copy(
                src_ref=ins[t], dst_ref=outs[t].at[me], send_sem=send3.at[t], recv_sem=recv3.at[t],
                device_id=sib, device_id_type=MESH)

        if phase == 0:
            own().start()
        if phase == 2:
            own().wait()
        for k, (cx, cy) in enumerate(chips):
            src = 2 * cx + cy

            def over_ici(slab):
                return pltpu.make_async_remote_copy(
                    src_ref=ins[t].at[half], dst_ref=outs[t].at[slab, half], send_sem=send1.at[t, k],
                    recv_sem=recv1.at[t, k], device_id=(cx, cy, c), device_id_type=MESH)

            def over_d2d(rows):
                return pltpu.make_async_remote_copy(
                    src_ref=outs[t].at[src, rows], dst_ref=outs[t].at[src, rows], send_sem=send2.at[t, k],
                    recv_sem=recv2.at[t, k], device_id=sib, device_id_type=MESH)

            if phase == 0:
                over_ici(me).start()
            if phase == 1:
                over_ici(src).wait_recv()
                over_d2d(half).start()
            if phase == 2:
                over_d2d(other).wait_recv()
                over_ici(me).wait_send()
                over_d2d(half).wait_send()


def _swap_copies(ins, outs, sems):
    send, recv = sems
    x, y, c = _my_place()
    return [pltpu.make_async_remote_copy(
        src_ref=ins[t].at[:, _half(1 - c, ins[t].shape[1], 8)], dst_ref=outs[t], send_sem=send.at[t],
        recv_sem=recv.at[t], device_id=(x, y, 1 - c), device_id_type=MESH) for t in range(len(ins))]


def _swap_out_shapes(gs):
    return [_sds((N_CHIPS, g.shape[1] // 2, g.shape[2]), F32) for g in gs]


def _swap_sems(n):
    return [pltpu.SemaphoreType.DMA((n,)), pltpu.SemaphoreType.DMA((n,))]


def _swap_half_rows(gs):
    n = len(gs)

    def body(*refs):
        cps = _swap_copies(refs[:n], refs[n:2 * n], refs[2 * n:])
        for cp in cps:
            cp.start()
        for cp in cps:
            cp.wait()

    return pl.pallas_call(
        body, name="rs_swap_halves",
        in_specs=[_ANY] * n, out_specs=[_ANY] * n, out_shape=_swap_out_shapes(gs), scratch_shapes=_swap_sems(n),
    )(*gs)


def _add_half_rows(g, got, c_idx, name):
    _, rows, cols = g.shape
    h = rows // 2

    def body(c_ref, a_ref, b_ref, o_ref):
        o_ref[...] = (a_ref[...] + b_ref[...]).astype(BF16)

    grid_spec = pltpu.PrefetchScalarGridSpec(
        num_scalar_prefetch=1, grid=(N_CHIPS,),
        in_specs=[pl.BlockSpec((None, h, cols), lambda j, c: (j, c[0], 0)),
                  pl.BlockSpec((None, h, cols), lambda j, c: (j, 0, 0))],
        out_specs=pl.BlockSpec((None, h, cols), lambda j, c: (j, 0, 0)),
    )
    return pl.pallas_call(
        body, name=name, grid_spec=grid_spec, out_shape=_sds((N_CHIPS, h, cols), BF16),
        compiler_params=_cp(("parallel",)),
    )(c_idx, g, got)


def _scatter_to_chips(ts, vec):
    n = len(ts)

    def body(*refs):
        ins, v_ref, outs, small_ref = refs[:n], refs[n], refs[n + 1:2 * n + 1], refs[2 * n + 1]
        slots, small_sems, sems = refs[2 * n + 2], refs[2 * n + 3:2 * n + 6], refs[2 * n + 6:]
        small = _small_copies(v_ref, slots, small_sems)
        cps = _scatter_copies(ins, outs, sems)
        for cp in small + cps:
            cp.start()
        for cp in small:
            cp.wait()
        _small_sum(slots, small_ref)
        for cp in cps:
            cp.wait()

    vm = pl.BlockSpec(memory_space=pltpu.VMEM)
    *parts, small_sum = pl.pallas_call(
        body, name="rs_scatter_chips",
        in_specs=[_ANY] * n + [vm], out_specs=[_ANY] * n + [vm],
        out_shape=_scatter_out_shapes(ts) + [_sds((SMALL_ROWS, PACK_COLS), F32)],
        scratch_shapes=_small_scratch() + _scatter_sems(n),
    )(*ts, vec)
    return parts, small_sum


def _scatter_copies(ins, outs, sems):
    send, recv = sems
    x, y, c = _my_place()
    return [pltpu.make_async_remote_copy(
        src_ref=ins[t].at[2 * cx + cy], dst_ref=outs[t].at[k], send_sem=send.at[t, k], recv_sem=recv.at[t, k],
        device_id=(cx, cy, c), device_id_type=MESH)
        for t in range(len(ins)) for k, (cx, cy) in enumerate(_other_chips(x, y))]


def _scatter_out_shapes(ts):
    return [_sds((3,) + tuple(t.shape[1:]), BF16) for t in ts]


def _scatter_sems(n):
    return [pltpu.SemaphoreType.DMA((n, 3)), pltpu.SemaphoreType.DMA((n, 3))]


def _add_four(mine, parts, place, name):
    _, h, cols = parts.shape

    def body(pl_ref, m_ref, p_ref, o_ref):
        o_ref[...] = ((m_ref[...].astype(F32) + p_ref[0].astype(F32)) + p_ref[1].astype(F32)) + p_ref[2].astype(F32)

    grid_spec = pltpu.PrefetchScalarGridSpec(
        num_scalar_prefetch=1, grid=(1,),
        in_specs=[pl.BlockSpec((None, h, cols), lambda i, pc: (pc[0], 0, 0)),
                  pl.BlockSpec((3, h, cols), lambda i, pc: (0, 0, 0))],
        out_specs=pl.BlockSpec((h, cols), lambda i, pc: (pc[1], 0)),
    )
    return pl.pallas_call(
        body, name=name, grid_spec=grid_spec, out_shape=_sds((2 * h, cols), F32),
        compiler_params=_cp(("arbitrary",)),
    )(place, mine, parts)


def _join_half_rows(rs):
    n = len(rs)

    def body(*refs):
        ins, outs = refs[:n], refs[n:2 * n]
        send, recv = refs[2 * n:]
        x, y, c = _my_place()
        cps = []
        for t in range(n):
            half = _half(c, outs[t].shape[0], 8)
            rc = pltpu.make_async_remote_copy(
                src_ref=ins[t].at[half], dst_ref=outs[t].at[half], send_sem=send.at[t], recv_sem=recv.at[t],
                device_id=(x, y, 1 - c), device_id_type=MESH)
            rc.start()
            cps.append(rc)
        for cp in cps:
            cp.wait()

    return pl.pallas_call(
        body, name="rs_join_halves",
        in_specs=[_ANY] * n, out_specs=[_ANY] * n,
        out_shape=[_sds(r.shape, F32) for r in rs],
        input_output_aliases={i: i for i in range(n)},
        scratch_shapes=[pltpu.SemaphoreType.DMA((n,))] * 2,
    )(*rs)


def _by_chip(full, rows, cols, axis):
    if axis == 0:
        return full.reshape(N_CHIPS, rows // N_CHIPS, cols)
    return full.reshape(rows, N_CHIPS, cols // N_CHIPS).transpose(1, 0, 2)


def _from_chips(parts, axis):
    _, r, c = parts.shape
    if axis == 0:
        return parts.reshape(N_CHIPS * r, c)
    return parts.transpose(1, 0, 2).reshape(r, N_CHIPS * c)


def _adamw(wt, g, m, v, name):
    _, R, C = wt.shape
    tr = max(d for d in range(8, R + 1, 8) if R % d == 0 and (d * C <= 256 * 1024 or d == 8))

    def body(w_ref, g_ref, m_ref, v_ref, d_ref, nm_ref, nv_ref):
        gg = g_ref[...]
        m_new = ADAM_B1 * m_ref[...] + (1.0 - ADAM_B1) * gg
        v_new = ADAM_B2 * v_ref[...] + (1.0 - ADAM_B2) * (gg * gg)
        m_hat = m_new / (1.0 - ADAM_B1 ** ADAM_STEP)
        v_hat = v_new / (1.0 - ADAM_B2 ** ADAM_STEP)
        d_ref[...] = -ADAM_LR * (m_hat / (jnp.sqrt(v_hat) + ADAM_EPS) + ADAM_WD * w_ref[...])
        nm_ref[...] = m_new
        nv_ref[...] = v_new

    spec = pl.BlockSpec((None, tr, C), lambda i: (0, i, 0))
    return pl.pallas_call(
        body, name=name, grid=(R // tr,), in_specs=[spec, pl.BlockSpec((tr, C), lambda i: (i, 0)), spec, spec],
        out_specs=[spec] * 3, out_shape=[_sds((1, R, C), F32)] * 3,
        compiler_params=_cp(("parallel",)),
    )(wt, g, m, v)


def _pack_small(vals, loss_vec=None):
    rows = [jnp.pad(vals[n].reshape(-1), (0, PACK_COLS - sz)) for n, sz in SMALL]
    rows.append(loss_vec.reshape(-1) if loss_vec is not None else jnp.zeros((PACK_COLS,), F32))
    rows += [jnp.zeros((PACK_COLS,), F32)] * (SMALL_ROWS - len(rows))
    return jnp.stack(rows)


def kernel(x, p, positions, pre_mix_norm, w_in, ret_gn_w, mla_q_norm, w_uq, mla_kv_norm, w_ukv, w_o, post_mix_norm, pre_ffn_norm, w_gate, w_up, w_down, post_ffn_norm, w_ple_proj, ple_norm, w_ple_gate, b_ple_gate, loss_target, m_pre_mix_norm, m_w_in, m_ret_gn_w, m_mla_q_norm, m_w_uq, m_mla_kv_norm, m_w_ukv, m_w_o, m_post_mix_norm, m_pre_ffn_norm, m_w_gate, m_w_up, m_w_down, m_post_ffn_norm, m_w_ple_proj, m_ple_norm, m_w_ple_gate, m_b_ple_gate, v_pre_mix_norm, v_w_in, v_ret_gn_w, v_mla_q_norm, v_w_uq, v_mla_kv_norm, v_w_ukv, v_w_o, v_post_mix_norm, v_pre_ffn_norm, v_w_gate, v_w_up, v_w_down, v_post_ffn_norm, v_w_ple_proj, v_ple_norm, v_w_ple_gate, v_b_ple_gate):
    wts = dict(pre_mix_norm=pre_mix_norm, w_in=w_in, ret_gn_w=ret_gn_w, mla_q_norm=mla_q_norm, w_uq=w_uq,
               mla_kv_norm=mla_kv_norm, w_ukv=w_ukv, w_o=w_o, post_mix_norm=post_mix_norm, pre_ffn_norm=pre_ffn_norm,
               w_gate=w_gate, w_up=w_up, w_down=w_down, post_ffn_norm=post_ffn_norm, w_ple_proj=w_ple_proj,
               ple_norm=ple_norm, w_ple_gate=w_ple_gate, b_ple_gate=b_ple_gate)
    mom = dict(pre_mix_norm=m_pre_mix_norm, w_in=m_w_in, ret_gn_w=m_ret_gn_w, mla_q_norm=m_mla_q_norm, w_uq=m_w_uq,
               mla_kv_norm=m_mla_kv_norm, w_ukv=m_w_ukv, w_o=m_w_o, post_mix_norm=m_post_mix_norm,
               pre_ffn_norm=m_pre_ffn_norm, w_gate=m_w_gate, w_up=m_w_up, w_down=m_w_down, post_ffn_norm=m_post_ffn_norm,
               w_ple_proj=m_w_ple_proj, ple_norm=m_ple_norm, w_ple_gate=m_w_ple_gate, b_ple_gate=m_b_ple_gate)
    var = dict(pre_mix_norm=v_pre_mix_norm, w_in=v_w_in, ret_gn_w=v_ret_gn_w, mla_q_norm=v_mla_q_norm, w_uq=v_w_uq,
               mla_kv_norm=v_mla_kv_norm, w_ukv=v_w_ukv, w_o=v_w_o, post_mix_norm=v_post_mix_norm,
               pre_ffn_norm=v_pre_ffn_norm, w_gate=v_w_gate, w_up=v_w_up, w_down=v_w_down, post_ffn_norm=v_post_ffn_norm,
               w_ple_proj=v_w_ple_proj, ple_norm=v_ple_norm, w_ple_gate=v_w_ple_gate, b_ple_gate=v_b_ple_gate)

    S = x.shape[1]
    shard2d = {n: wts[n][0] for n, _, _, _ in BIG}
    small2d = {n: wts[n] for n, _ in SMALL}

    shard_bf = {n: (jnp.swapaxes(wts[n], 1, 2)[0] if n in GRAD_TRANSPOSED else shard2d[n]).astype(BF16) for n in shard2d}
    pos_f = positions.astype(F32).reshape(S, 1)
    c_idx = lax.axis_index("c").astype(jnp.int32).reshape(1)
    loss_vec, grad_x, gw, gs, (sums_early, parts_early) = _local_step(
        x[0], p[0, 0], pos_f, loss_target[0], {}, small2d, shard_bf, c_idx)

    g4 = [_by_chip(gw[n], *BIG_SPEC[n]) for n in REDUCE_LAST]
    got = _swap_half_rows(g4)
    sums_last = [_add_half_rows(g4[i], got[i], c_idx, "rs_add_halves_" + n) for i, n in enumerate(REDUCE_LAST)]
    parts_last, small_sum = _scatter_to_chips(sums_last, _pack_small(gs, loss_vec))
    place = jnp.stack([2 * lax.axis_index("x") + lax.axis_index("y"), lax.axis_index("c")]).astype(jnp.int32)
    names = REDUCE_EARLY + REDUCE_LAST
    reduced = _join_half_rows(
        [_add_four(sm_, pt_, place, "rs_add_chips_" + n)
         for n, sm_, pt_ in zip(names, sums_early + sums_last, list(parts_early) + list(parts_last))])
    g_shard = dict(zip(names, reduced))

    loss = small_sum[9, 0]
    g_small = {n: small_sum[i:i + 1, :sz] for i, (n, sz) in enumerate(SMALL)}

    grads, delta, new_m, new_v = {}, {}, {}, {}
    for n, _, _, _ in BIG:
        if n in COLUMN_MAJOR:
            turn = lambda a: jnp.swapaxes(a, 1, 2)
            g_t = g_shard[n] if n in GRAD_TRANSPOSED else g_shard[n].T
            d, nm, nv = _adamw(turn(wts[n]), g_t, turn(mom[n]), turn(var[n]), "adamw_" + n)
            grads[n], delta[n], new_m[n], new_v[n] = turn(g_t[None]), turn(d), turn(nm), turn(nv)
        else:
            delta[n], new_m[n], new_v[n] = _adamw(wts[n], g_shard[n], mom[n], var[n], "adamw_" + n)
            grads[n] = g_shard[n][None]
    d, nm, nv = _adamw(_pack_small(small2d)[None], small_sum, _pack_small(mom)[None], _pack_small(var)[None],
                       "adamw_small")
    for i, (n, sz) in enumerate(SMALL):
        grads[n] = g_small[n]
        delta[n], new_m[n], new_v[n] = d[0, i:i + 1, :sz], nm[0, i:i + 1, :sz], nv[0, i:i + 1, :sz]

    return (loss, grad_x[None], *[grads[n] for n in ALL_W], *[delta[n] for n in ALL_W],
            *[new_m[n] for n in ALL_W], *[new_v[n] for n in ALL_W])
```

```python
import functools
import math

import jax
import jax.numpy as jnp
import numpy as np
from jax import lax
from jax.experimental import pallas as pl
from jax.experimental.pallas import tpu as pltpu

F32 = jnp.float32
BF16 = jnp.bfloat16
MESH = pl.DeviceIdType.MESH

D_MODEL = 1024
D_FF = 2816
PLE_DIM = 256
RET_HEADS = 4
RET_DIM = 128
RET_WIDTH = 512
RET_CHUNK = 256
RET_GROUP = 4
MLA_HEADS = 8
MLA_NOPE = 64
MLA_ROPE = 32
MLA_V = 64
Q_LORA = 384
KV_LORA = 256
IN_COLS = 2720
IN_COLS_P = 2816
ROPE_BASE = 10000.0
EPS = 1e-6
SCALE_MLA = 1.0 / math.sqrt(MLA_NOPE + MLA_ROPE)
SCALE_RET = RET_DIM ** -0.5
NEG = -1e30

ADAM_LR = 0.001
ADAM_B1 = 0.9
ADAM_B2 = 0.999
ADAM_EPS = 1e-08
ADAM_WD = 0.01
ADAM_STEP = 10

N_CHIPS = 4
N_DEV = 8
VMEM_MB = 56

BIG = (
    ("w_in", 1024, 2720, 1),
    ("w_uq", 384, 768, 1),
    ("w_ukv", 256, 1024, 1),
    ("w_o", 1024, 1024, 0),
    ("w_gate", 1024, 2816, 1),
    ("w_up", 1024, 2816, 1),
    ("w_down", 2816, 1024, 0),
    ("w_ple_proj", 256, 1024, 1),
    ("w_ple_gate", 1024, 1024, 0),
)
SMALL = (
    ("pre_mix_norm", 1024),
    ("ret_gn_w", 512),
    ("mla_q_norm", 384),
    ("mla_kv_norm", 256),
    ("post_mix_norm", 1024),
    ("pre_ffn_norm", 1024),
    ("post_ffn_norm", 1024),
    ("ple_norm", 1024),
    ("b_ple_gate", 1024),
)
ALL_W = ("pre_mix_norm", "w_in", "ret_gn_w", "mla_q_norm", "w_uq", "mla_kv_norm", "w_ukv", "w_o", "post_mix_norm",
         "pre_ffn_norm", "w_gate", "w_up", "w_down", "post_ffn_norm", "w_ple_proj", "ple_norm", "w_ple_gate", "b_ple_gate")
PACK_COLS = 1024
SMALL_ROWS = 16


def _cp(sem=None, mb=VMEM_MB, **kw):
    return pltpu.CompilerParams(dimension_semantics=sem, vmem_limit_bytes=mb * 1024 * 1024, **kw)


def _bf(x):
    return x.astype(BF16)


def _dot(a, b):
    return jnp.dot(_bf(a), _bf(b), preferred_element_type=F32)


def _dot_nt(a, b):
    return lax.dot_general(_bf(a), _bf(b), (((1,), (1,)), ((), ())), preferred_element_type=F32)


def _dot_tn(a, b):
    return lax.dot_general(_bf(a), _bf(b), (((0,), (0,)), ((), ())), preferred_element_type=F32)


def _sig(x):
    return 1.0 / (1.0 + jnp.exp(-x))


def _rms(x, g):
    r = lax.rsqrt(jnp.mean(x * x, axis=-1, keepdims=True) + EPS)
    return x * r * g


def _rms_bwd(dy, x, g):
    r = lax.rsqrt(jnp.mean(x * x, axis=-1, keepdims=True) + EPS)
    xh = x * r
    dxh = dy * g
    dx = r * (dxh - xh * jnp.mean(dxh * xh, axis=-1, keepdims=True))
    return dx, dy * xh


def _colsum(x):
    return jnp.sum(x, axis=0, keepdims=True)


def _rope_ret(x, cr, sr):
    return x * cr + pltpu.roll(x, 64, 1) * sr


def _unrope_ret(dy, cr, sr):
    return dy * cr + pltpu.roll(dy * sr, 64, 1)


def _rope_mla(x, cm, sa, sb):
    return x * cm + pltpu.roll(x, 112, 1) * sa + pltpu.roll(x, 16, 1) * sb


def _unrope_mla(dy, cm, sa, sb):
    return dy * cm + pltpu.roll(dy * sa, 16, 1) + pltpu.roll(dy * sb, 112, 1)


def _rows(tm, w, col=0):
    return pl.BlockSpec((tm, w), lambda i: (i, col))


def _full(*shape):
    return pl.BlockSpec(shape, lambda i: (0,) * len(shape), pipeline_mode=pl.Buffered(1))


def _acc(*shape):
    return pl.BlockSpec(shape, lambda i: (0,) * len(shape))


def _sds(shape, dtype):
    return jax.ShapeDtypeStruct(shape, dtype)


def _prologue(pos_f, x, g, S, shards=()):
    tm = min(512, S)
    n = len(shards)
    steps = S // tm
    inv_r = (1.0 / (np.float32(ROPE_BASE) ** (np.arange(64, dtype=np.float32) / np.float32(64)))).astype(np.float32)
    inv_m16 = (1.0 / (np.float32(ROPE_BASE) ** (np.arange(16, dtype=np.float32) / np.float32(16)))).astype(np.float32)
    inv_r = np.concatenate([inv_r, inv_r])[None, :]
    inv_m = np.zeros((1, 128), np.float32)
    inv_m[0, 64:80] = inv_m16
    inv_m[0, 80:96] = inv_m16

    def body(pos_ref, invr_ref, invm_ref, x_ref, g_ref, *rest):
        w_ins, (cr_ref, sr_ref, cm_ref, sa_ref, sb_ref, xn_ref) = rest[:n], rest[n:n + 6]
        w_outs, sems = rest[n + 6:2 * n + 6], rest[2 * n + 6:]
        i = pl.program_id(0)
        if n:
            @pl.when(i == 0)
            def _():
                _gather_phase(0, w_ins, w_outs, sems)

            @pl.when(i == steps // 2)
            def _():
                _gather_phase(1, w_ins, w_outs, sems)

        pos = pos_ref[...]
        lane = lax.broadcasted_iota(jnp.int32, (tm, 128), 1)
        ar = pos * invr_ref[...]
        s = jnp.sin(ar)
        cr_ref[...] = jnp.cos(ar)
        sr_ref[...] = jnp.where(lane < 64, -s, s)
        am = pos * invm_ref[...]
        c2 = jnp.cos(am)
        s2 = jnp.sin(am)
        cm_ref[...] = jnp.where(lane < 64, 1.0, jnp.where(lane < 96, c2, 0.0))
        sa_ref[...] = jnp.where((lane >= 64) & (lane < 80), -s2, 0.0)
        sb_ref[...] = jnp.where((lane >= 80) & (lane < 96), s2, 0.0)
        xn_ref[...] = _rms(x_ref[...], g_ref[...]).astype(BF16)

        if n:
            @pl.when(i == steps - 1)
            def _():
                _gather_phase(2, w_ins, w_outs, sems)

    outs = pl.pallas_call(
        body, name="prologue", grid=(steps,),
        in_specs=[_rows(tm, 1), _full(1, 128), _full(1, 128), _rows(tm, D_MODEL), _full(1, D_MODEL)] + [_ANY] * n,
        out_specs=[_rows(tm, 128)] * 5 + [_rows(tm, D_MODEL)] + [_ANY] * n,
        out_shape=[_sds((S, 128), F32)] * 5 + [_sds((S, D_MODEL), BF16)] + _gather_out_shapes(shards),
        scratch_shapes=_gather_sems(n) if n else [],
        compiler_params=_cp(("arbitrary",)),
    )(pos_f, jnp.asarray(inv_r), jnp.asarray(inv_m), x, g, *shards)
    return outs[:5], outs[5], outs[6:]


def _inproj(xn, w_in, tabs, S):
    tm = min(512, S)

    def body(xn_ref, w_ref, cr_ref, sr_ref, cm_ref, sa_ref, sb_ref,
             rq_ref, rk_ref, rv_ref, rg_ref, cq_ref, ckv_ref, kr_ref):
        xb = xn_ref[...]
        cr = cr_ref[...]
        sr = sr_ref[...]
        q = jnp.dot(xb, w_ref[:, 0:512], preferred_element_type=F32)
        k = jnp.dot(xb, w_ref[:, 512:1024], preferred_element_type=F32)
        for h in range(RET_HEADS):
            sl = slice(h * 128, (h + 1) * 128)
            rq_ref[:, sl] = _rope_ret(q[:, sl], cr, sr).astype(BF16)
            rk_ref[:, sl] = (_rope_ret(k[:, sl], cr, sr) * SCALE_RET).astype(BF16)
        rv_ref[...] = jnp.dot(xb, w_ref[:, 1024:1536], preferred_element_type=F32).astype(BF16)
        rg_ref[...] = jnp.dot(xb, w_ref[:, 1536:2048], preferred_element_type=F32)
        cq_ref[...] = jnp.dot(xb, w_ref[:, 2048:2432], preferred_element_type=F32)
        ckv_ref[...] = jnp.dot(xb, w_ref[:, 2432:2688], preferred_element_type=F32)
        kr = jnp.dot(xb, w_ref[:, 2688:2816], preferred_element_type=F32)
        kr_ref[...] = _rope_mla(kr, cm_ref[...], sa_ref[...], sb_ref[...])

    return pl.pallas_call(
        body, name="inproj", grid=(S // tm,),
        in_specs=[_rows(tm, D_MODEL), _full(D_MODEL, IN_COLS_P)] + [_rows(tm, 128)] * 5,
        out_specs=[_rows(tm, 512)] * 4 + [_rows(tm, Q_LORA), _rows(tm, KV_LORA), _rows(tm, 128)],
        out_shape=[_sds((S, 512), BF16)] * 3
        + [_sds((S, 512), F32), _sds((S, Q_LORA), F32), _sds((S, KV_LORA), F32), _sds((S, 128), F32)],
        compiler_params=_cp(("parallel",)),
    )(xn, w_in, *tabs)


def _mla_up(cq, ckv, kr, gq, gkv, w_uq, w_ukv, tabs, S):
    tm = min(512, S)

    def body(cq_ref, ckv_ref, kr_ref, gq_ref, gkv_ref, wuq_ref, wukv_ref, cm_ref, sa_ref, sb_ref,
             cqn_ref, ckvn_ref, qp_ref, kp_ref, v_ref, kt_ref, vt_ref):
        cm = cm_ref[...]
        sa = sa_ref[...]
        sb = sb_ref[...]
        cqn = _rms(cq_ref[...], gq_ref[...]).astype(BF16)
        cqn_ref[...] = cqn
        ckvn = _rms(ckv_ref[...], gkv_ref[...]).astype(BF16)
        ckvn_ref[...] = ckvn
        qh = jnp.dot(cqn, wuq_ref[...], preferred_element_type=F32)
        kv = jnp.dot(ckvn, wukv_ref[...], preferred_element_type=F32)
        kr_blk = kr_ref[...]
        for h in range(MLA_HEADS):
            sl = slice(h * 128, (h + 1) * 128)
            qp_ref[:, sl] = (_rope_mla(qh[:, sl], cm, sa, sb) * SCALE_MLA).astype(BF16)
            kh = kv[:, sl] + kr_blk
            kp_ref[:, sl] = kh.astype(BF16)
            kt_ref[sl, :] = kh.T.astype(BF16)
        for h in range(MLA_HEADS // 2):
            vh = kv[:, 1024 + h * 128:1024 + (h + 1) * 128]
            v_ref[:, h * 128:(h + 1) * 128] = vh.astype(BF16)
            vt_ref[h * 128:(h + 1) * 128, :] = vh.T.astype(BF16)

    cols = lambda r: pl.BlockSpec((r, tm), lambda i: (0, i))
    return pl.pallas_call(
        body, name="mla_up", grid=(S // tm,),
        in_specs=[_rows(tm, Q_LORA), _rows(tm, KV_LORA), _rows(tm, 128), _full(1, Q_LORA), _full(1, KV_LORA),
                  _full(Q_LORA, 1024), _full(KV_LORA, 1536)] + [_rows(tm, 128)] * 3,
        out_specs=[_rows(tm, Q_LORA), _rows(tm, KV_LORA), _rows(tm, 1024), _rows(tm, 1024), _rows(tm, 512),
                   cols(1024), cols(512)],
        out_shape=[_sds((S, Q_LORA), BF16), _sds((S, KV_LORA), BF16), _sds((S, 1024), BF16), _sds((S, 1024), BF16),
                   _sds((S, 512), BF16), _sds((1024, S), BF16), _sds((512, S), BF16)],
        compiler_params=_cp(("parallel",)),
    )(cq, ckv, kr, gq, gkv, w_uq, w_ukv, *tabs[2:])


def _tri_pairs(nq, k_major):
    if k_major:
        pairs = [(qb, kb) for kb in range(nq) for qb in range(kb, nq)]
    else:
        pairs = [(qb, kb) for qb in range(nq) for kb in range(qb + 1)]
    qb_of = np.array([p[0] for p in pairs], np.int32)
    kb_of = np.array([p[1] for p in pairs], np.int32)
    return jnp.asarray(qb_of), jnp.asarray(kb_of), len(pairs)


ATT_ROWS = 32
FWD_HEADS = 8
BWD_HEADS = 4


def _causal_keep(r0, rows, tq):
    key = r0 + lax.broadcasted_iota(jnp.int32, (rows, tq), 0)
    qry = lax.broadcasted_iota(jnp.int32, (rows, tq), 1)
    return key <= qry


def _flash_fwd(qp, kp, vt, S, shards=()):
    tq = min(512, S)
    nq = S // tq
    RB = ATT_ROWS
    NH = FWD_HEADS
    qb_of, kb_of, T = _tri_pairs(nq, k_major=False)
    n = len(shards)
    steps = (MLA_HEADS // NH) * T

    def body(qb_ref, kb_ref, q_ref, k_ref, vt_ref, *rest):
        w_ins, (o_ref, lse_ref), w_outs = rest[:n], rest[n:n + 2], rest[n + 2:2 * n + 2]
        m_sc, l_sc, acc_sc, s_sc, p_sc = rest[2 * n + 2:2 * n + 7]
        sems = rest[2 * n + 7:]
        t = pl.program_id(1)
        qb = qb_ref[t]
        kb = kb_ref[t]
        lin = pl.program_id(0) * T + t

        if n:
            @pl.when(lin == 0)
            def _():
                _gather_phase(0, w_ins, w_outs, sems)

            @pl.when(lin == steps // 2)
            def _():
                _gather_phase(1, w_ins, w_outs, sems)

        @pl.when(kb == 0)
        def _():
            m_sc[...] = jnp.full(m_sc.shape, NEG, F32)
            l_sc[...] = jnp.zeros(l_sc.shape, F32)
            acc_sc[...] = jnp.zeros(acc_sc.shape, F32)

        def scores(a):
            sl = slice(a * 128, (a + 1) * 128)
            s_sc[a] = _dot_nt(k_ref[:, sl], q_ref[:, sl])

        def step(masked):
            for a in range(NH):
                scores(a)
            for a in range(NH):
                mx = [jnp.full((8, tq), NEG, F32) for _ in range(RB // 8)]
                for r in range(0, tq, RB):
                    sc = s_sc[a, r:r + RB, :]
                    if masked:
                        sc = jnp.where(_causal_keep(r, RB, tq), sc, NEG)
                        s_sc[a, r:r + RB, :] = sc
                    for i in range(RB // 8):
                        mx[i] = jnp.maximum(mx[i], sc[i * 8:(i + 1) * 8, :])
                mx8 = functools.reduce(jnp.maximum, mx)
                m_prev = m_sc[a]
                m_new = jnp.maximum(m_prev, jnp.max(mx8, axis=0, keepdims=True))
                al = jnp.exp(m_prev - m_new)
                m_sc[a] = m_new
                ls = [jnp.zeros((8, tq), F32) for _ in range(RB // 8)]
                for r in range(0, tq, RB):
                    p = jnp.exp(s_sc[a, r:r + RB, :] - m_new)
                    for i in range(RB // 8):
                        ls[i] = ls[i] + p[i * 8:(i + 1) * 8, :]
                    p_sc[a, r:r + RB, :] = p.astype(BF16)
                l_sc[a] = al * l_sc[a] + jnp.sum(functools.reduce(jnp.add, ls), axis=0, keepdims=True)
                pair = slice((a // 2) * 128, (a // 2 + 1) * 128)
                pv = jnp.dot(vt_ref[pair, :], p_sc[a], preferred_element_type=F32)
                rs = slice(a * 64, (a + 1) * 64)
                own = slice((a % 2) * 64, (a % 2 + 1) * 64)
                acc_sc[rs, :] = acc_sc[rs, :] * al + pv[own, :]

        @pl.when(kb < qb)
        def _():
            step(False)

        @pl.when(kb == qb)
        def _():
            step(True)
            for a in range(NH):
                rs = slice(a * 64, (a + 1) * 64)
                acc_sc[rs, :] = acc_sc[rs, :] / l_sc[a]
                lse_ref[a:a + 1, :] = m_sc[a] + jnp.log(l_sc[a])
            o_ref[...] = acc_sc[...].T.astype(BF16)

        if n:
            @pl.when(lin == steps - 1)
            def _():
                _gather_phase(2, w_ins, w_outs, sems)

    grid_spec = pltpu.PrefetchScalarGridSpec(
        num_scalar_prefetch=2, grid=(MLA_HEADS // NH, T),
        in_specs=[pl.BlockSpec((tq, 128 * NH), lambda j, t, qb, kb: (qb[t], j)),
                  pl.BlockSpec((tq, 128 * NH), lambda j, t, qb, kb: (kb[t], j)),
                  pl.BlockSpec((64 * NH, tq), lambda j, t, qb, kb: (j, kb[t]))] + [_ANY] * n,
        out_specs=[pl.BlockSpec((tq, 64 * NH), lambda j, t, qb, kb: (qb[t], j)),
                   pl.BlockSpec((None, NH, tq), lambda j, t, qb, kb: (j, 0, qb[t]))] + [_ANY] * n,
        scratch_shapes=[pltpu.VMEM((NH, 1, tq), F32), pltpu.VMEM((NH, 1, tq), F32), pltpu.VMEM((64 * NH, tq), F32),
                        pltpu.VMEM((NH, tq, tq), F32), pltpu.VMEM((NH, tq, tq), BF16)] + (_gather_sems(n) if n else []),
    )
    out, lse, *gathered = pl.pallas_call(
        body, name="flash_fwd", grid_spec=grid_spec,
        out_shape=[_sds((S, 512), BF16), _sds((MLA_HEADS // NH, NH, S), F32)] + _gather_out_shapes(shards),
        compiler_params=_cp(("arbitrary", "arbitrary")),
    )(qb_of, kb_of, qp, kp, vt, *shards)
    return out, lse.reshape(MLA_HEADS // 2, 2, S), gathered


def _decay_table():
    log_g = np.log(1.0 - 2.0 ** (-5.0 - np.arange(RET_HEADS, dtype=np.float32))).astype(np.float32)
    return jnp.asarray(np.broadcast_to(log_g[:, None, None], (RET_HEADS, 8, 128)).copy())


def _decay_terms(lg_ref):
    C = RET_CHUNK
    lg = lg_ref[0:1, :]
    row = lax.broadcasted_iota(jnp.int32, (C, C), 0)
    col = lax.broadcasted_iota(jnp.int32, (C, C), 1)
    diff = (row - col).astype(F32)
    dmat = jnp.where(diff >= 0, jnp.exp(jnp.maximum(diff, 0.0) * jnp.tile(lg, (1, C // 128))), 0.0)
    j = lax.broadcasted_iota(jnp.int32, (C, 1), 0).astype(F32)
    lg1 = lg[:, 0:1]
    zeta = jnp.exp((C - 1 - j) * lg1)
    xi = jnp.exp((j + 1.0) * lg1)
    g_chunk = jnp.exp(C * lg1)
    return dmat, zeta, xi, g_chunk


def _ret_fwd(rq, rk, rv, rg, gn_w, S):
    C = RET_CHUNK
    N = S // C
    G = min(RET_GROUP, N)
    NB = N // G

    def body(lg_ref, q_ref, k_ref, v_ref, rg_ref, w_ref, ry_ref, ro_ref, rprev_ref, r_sc):
        @pl.when(pl.program_id(1) == 0)
        def _():
            r_sc[...] = jnp.zeros(r_sc.shape, F32)

        dmat, zeta, xi, g_chunk = _decay_terms(lg_ref)
        w = w_ref[...]
        r = r_sc[...]
        for i in range(G):
            rows = slice(i * C, (i + 1) * C)
            q = q_ref[rows, :]
            k = k_ref[rows, :]
            v = v_ref[rows, :]
            r_prev = r.astype(BF16)
            rprev_ref[i] = r_prev
            sc = _dot_nt(q, k) * dmat
            ry = _dot(sc, v) + jnp.dot(q, r_prev, preferred_element_type=F32) * xi
            ry_ref[rows, :] = ry
            r = g_chunk * r + _dot_tn(k, zeta * v.astype(F32))
            mu = jnp.mean(ry, axis=-1, keepdims=True)
            yc = ry - mu
            yh = yc * lax.rsqrt(jnp.mean(yc * yc, axis=-1, keepdims=True) + EPS)
            g = rg_ref[rows, :]
            ro_ref[rows, :] = (g * _sig(g) * (yh * w)).astype(BF16)
        r_sc[...] = r

    blk = pl.BlockSpec((G * C, 128), lambda h, n: (n, h))
    return pl.pallas_call(
        body, name="ret_fwd", grid=(RET_HEADS, NB),
        in_specs=[pl.BlockSpec((None, 8, 128), lambda h, n: (h, 0, 0)), blk, blk, blk, blk,
                  pl.BlockSpec((1, 128), lambda h, n: (0, h))],
        out_specs=[blk, blk, pl.BlockSpec((G, 128, 128), lambda h, n: (h * NB + n, 0, 0))],
        out_shape=[_sds((S, 512), F32), _sds((S, 512), BF16), _sds((RET_HEADS * N, 128, 128), BF16)],
        scratch_shapes=[pltpu.VMEM((128, 128), F32)],
        compiler_params=_cp(("parallel", "arbitrary")),
    )(_decay_table(), rq, rk, rv, rg, gn_w)


def _outproj(ro, mo, x, w_o, g_post, g_pre, S):
    tm = min(512, S)

    def body(ro_ref, mo_ref, x_ref, wo_ref, g1_ref, g2_ref, mix_ref, h1_ref, hn_ref):
        mix = (jnp.dot(ro_ref[...], wo_ref[0:512, :], preferred_element_type=F32)
               + jnp.dot(mo_ref[...], wo_ref[512:1024, :], preferred_element_type=F32))
        mix_ref[...] = mix.astype(BF16)
        h1 = x_ref[...] + _rms(mix, g1_ref[...])
        h1_ref[...] = h1
        hn_ref[...] = _rms(h1, g2_ref[...]).astype(BF16)

    return pl.pallas_call(
        body, name="outproj", grid=(S // tm,),
        in_specs=[_rows(tm, 512), _rows(tm, 512), _rows(tm, D_MODEL), _full(D_MODEL, D_MODEL), _full(1, D_MODEL),
                  _full(1, D_MODEL)],
        out_specs=[_rows(tm, D_MODEL)] * 3,
        out_shape=[_sds((S, D_MODEL), BF16), _sds((S, D_MODEL), F32), _sds((S, D_MODEL), BF16)],
        compiler_params=_cp(("parallel",)),
    )(ro, mo, x, w_o, g_post, g_pre)


def _ffn_up(hn, w_gate_t, w_up_t, S):
    tm = min(512, S)
    tn = D_FF // 2

    def body(hn_ref, wg_ref, wu_ref, fg_ref, fu_ref, act_ref):
        hn_b = hn_ref[...]
        g = _dot_nt(hn_b, wg_ref[...])
        u = _dot_nt(hn_b, wu_ref[...])
        s = _sig(g)
        silu = g * s
        fg_ref[...] = (u * (s + silu * (1.0 - s))).astype(BF16)
        fu_ref[...] = silu.astype(BF16)
        act_ref[...] = (silu * u).astype(BF16)

    wspec = pl.BlockSpec((tn, D_MODEL), lambda j, i: (j, 0))
    ospec = pl.BlockSpec((tm, tn), lambda j, i: (i, j))
    return pl.pallas_call(
        body, name="ffn_up", grid=(2, S // tm),
        in_specs=[pl.BlockSpec((tm, D_MODEL), lambda j, i: (i, 0)), wspec, wspec],
        out_specs=[ospec] * 3, out_shape=[_sds((S, D_FF), BF16)] * 3,
        compiler_params=_cp(("parallel", "parallel")),
    )(hn, w_gate_t, w_up_t)


def _ffn_down(act, w_down, h1, g, S):
    tm = min(512, S)

    def body(act_ref, wd_ref, h1_ref, g_ref, ff_ref, h2_ref):
        ff = jnp.dot(act_ref[...], wd_ref[...], preferred_element_type=F32)
        ff_ref[...] = ff.astype(BF16)
        h2_ref[...] = h1_ref[...] + _rms(ff, g_ref[...])

    return pl.pallas_call(
        body, name="ffn_down", grid=(S // tm,),
        in_specs=[_rows(tm, D_FF), _full(D_FF, D_MODEL), _rows(tm, D_MODEL), _full(1, D_MODEL)],
        out_specs=[_rows(tm, D_MODEL)] * 2, out_shape=[_sds((S, D_MODEL), BF16), _sds((S, D_MODEL), F32)],
        compiler_params=_cp(("parallel",)),
    )(act, w_down, h1, g)


def _ple_loss(p, h2, tgt, w_pp, w_pg, b_pg, g_ple, S):
    tm = min(512, S)

    def body(p_ref, h2_ref, t_ref, wp_ref, wg_ref, b_ref, gp_ref,
             dz_ref, dpe_ref, dh2_ref, h2b_ref, loss_ref, dgp_ref, db_ref):
        @pl.when(pl.program_id(0) == 0)
        def _():
            loss_ref[...] = jnp.zeros(loss_ref.shape, F32)
            dgp_ref[...] = jnp.zeros(dgp_ref.shape, F32)
            db_ref[...] = jnp.zeros(db_ref.shape, F32)

        gp = gp_ref[...]
        pe = _dot(p_ref[...], wp_ref[...])
        r = lax.rsqrt(jnp.mean(pe * pe, axis=-1, keepdims=True) + EPS)
        peh = pe * r
        e = peh * gp
        h2 = h2_ref[...]
        h2b = h2.astype(BF16)
        h2b_ref[...] = h2b
        gt = _sig(jnp.dot(h2b, wg_ref[...], preferred_element_type=F32) + b_ref[...])
        diff = h2 + e * gt - t_ref[...]
        loss_ref[...] += _colsum(diff * diff)
        dh3 = diff * (1.0 / D_MODEL)
        de = dh3 * gt
        dz = dh3 * e * gt * (1.0 - gt)
        db_ref[...] += _colsum(dz)
        dgp_ref[...] += _colsum(de * peh)
        dpeh = de * gp
        dpe = r * (dpeh - peh * jnp.mean(dpeh * peh, axis=-1, keepdims=True))
        dzb = dz.astype(BF16)
        dz_ref[...] = dzb
        dpe_ref[...] = dpe.astype(BF16)
        dh2_ref[...] = dh3 + _dot_nt(dzb, wg_ref[...])

    return pl.pallas_call(
        body, name="ple_loss", grid=(S // tm,),
        in_specs=[_rows(tm, PLE_DIM), _rows(tm, D_MODEL), _rows(tm, D_MODEL), _full(PLE_DIM, D_MODEL),
                  _full(D_MODEL, D_MODEL), _full(1, D_MODEL), _full(1, D_MODEL)],
        out_specs=[_rows(tm, D_MODEL)] * 4 + [_acc(1, D_MODEL)] * 3,
        out_shape=[_sds((S, D_MODEL), BF16), _sds((S, D_MODEL), BF16), _sds((S, D_MODEL), F32), _sds((S, D_MODEL), BF16)]
        + [_sds((1, D_MODEL), F32)] * 3,
        compiler_params=_cp(("arbitrary",)),
    )(p, h2, tgt, w_pp, w_pg, b_pg, g_ple)


def _wgrad(a, b, name, S):
    M = a.shape[1]
    N = b.shape[1]
    ts = min(2048, S)
    nsplit = 2 if M * N >= 2 * 1024 * 1024 else 1
    tn = N // nsplit

    def body(a_ref, b_ref, o_ref):
        @pl.when(pl.program_id(1) == 0)
        def _():
            o_ref[...] = jnp.zeros(o_ref.shape, F32)

        o_ref[...] += _dot_tn(a_ref[...], b_ref[...])

    return pl.pallas_call(
        body, name=name, grid=(nsplit, S // ts),
        in_specs=[pl.BlockSpec((ts, M), lambda j, s: (s, 0)), pl.BlockSpec((ts, tn), lambda j, s: (s, j))],
        out_specs=pl.BlockSpec((M, tn), lambda j, s: (0, j)), out_shape=_sds((M, N), F32),
        compiler_params=_cp(("parallel", "arbitrary")),
    )(a, b)


def _ffn_down_bwd(dh2, ff, g, w_down, dgate_f, dup_f, S):
    tm = min(512, S)
    tn = D_FF // 2

    def body(dh2_ref, ff_ref, g_ref, wd_ref, fg_ref, fu_ref, dff_ref, dgate_ref, dup_ref, dg_ref):
        @pl.when(pl.program_id(0) == 0)
        def _():
            dg_ref[...] = jnp.zeros(dg_ref.shape, F32)

        dff, ga = _rms_bwd(dh2_ref[...], ff_ref[...].astype(F32), g_ref[...])
        dg_ref[...] += _colsum(ga)
        dffb = dff.astype(BF16)
        dff_ref[...] = dffb
        for seg in range(2):
            sl = slice(seg * tn, (seg + 1) * tn)
            dact = _dot_nt(dffb, wd_ref[sl, :])
            dgate_ref[:, sl] = (dact * fg_ref[:, sl].astype(F32)).astype(BF16)
            dup_ref[:, sl] = (dact * fu_ref[:, sl].astype(F32)).astype(BF16)

    return pl.pallas_call(
        body, name="ffn_down_bwd", grid=(S // tm,),
        in_specs=[_rows(tm, D_MODEL), _rows(tm, D_MODEL), _full(1, D_MODEL), _full(D_FF, D_MODEL), _rows(tm, D_FF),
                  _rows(tm, D_FF)],
        out_specs=[_rows(tm, D_MODEL), _rows(tm, D_FF), _rows(tm, D_FF), _acc(1, D_MODEL)],
        out_shape=[_sds((S, D_MODEL), BF16), _sds((S, D_FF), BF16), _sds((S, D_FF), BF16), _sds((1, D_MODEL), F32)],
        compiler_params=_cp(("arbitrary",)),
    )(dh2, ff, g, w_down, dgate_f, dup_f)


def _ffn_up_bwd(dgate, dup, w_gate, w_up, h1, mix, dh2, g_pre, g_post, w_o, S, grads=()):
    tm = min(512, S)
    n = len(grads)
    last = S // tm - 1

    def body(dgate_ref, dup_ref, wg_ref, wu_ref, h1_ref, mix_ref, dh2_ref, g2_ref, g1_ref, wo_ref, *rest):
        g_ins = rest[:n]
        dh1_ref, dmix_ref, dro_ref, dmo_ref, dg2_ref, dg1_ref = rest[n:n + 6]
        g_outs, sems = rest[n + 6:2 * n + 6], rest[2 * n + 6:]

        @pl.when(pl.program_id(0) == 0)
        def _():
            dg2_ref[...] = jnp.zeros(dg2_ref.shape, F32)
            dg1_ref[...] = jnp.zeros(dg1_ref.shape, F32)
            for cp in (_swap_copies(g_ins, g_outs, sems) if n else []):
                cp.start()

        dhn = (jnp.dot(dgate_ref[...], wg_ref[...], preferred_element_type=F32)
               + jnp.dot(dup_ref[...], wu_ref[...], preferred_element_type=F32))
        d1, ga = _rms_bwd(dhn, h1_ref[...], g2_ref[...])
        dg2_ref[...] += _colsum(ga)
        dh1 = dh2_ref[...] + d1
        dh1_ref[...] = dh1
        dmix, gb = _rms_bwd(dh1, mix_ref[...].astype(F32), g1_ref[...])
        dg1_ref[...] += _colsum(gb)
        dmixb = dmix.astype(BF16)
        dmix_ref[...] = dmixb
        dcat = _dot_nt(dmixb, wo_ref[...])
        dro_ref[...] = dcat[:, 0:512].astype(BF16)
        dmo_ref[...] = dcat[:, 512:1024].astype(BF16)

        if n:
            @pl.when(pl.program_id(0) == last)
            def _():
                for cp in _swap_copies(g_ins, g_outs, sems):
                    cp.wait()

    dh1, dmix, dro, dmo, dg2, dg1, *got = pl.pallas_call(
        body, name="ffn_up_bwd", grid=(S // tm,),
        in_specs=[_rows(tm, D_FF), _rows(tm, D_FF), _full(D_FF, D_MODEL), _full(D_FF, D_MODEL), _rows(tm, D_MODEL),
                  _rows(tm, D_MODEL), _rows(tm, D_MODEL), _full(1, D_MODEL), _full(1, D_MODEL), _full(D_MODEL, D_MODEL)]
        + [_ANY] * n,
        out_specs=[_rows(tm, D_MODEL), _rows(tm, D_MODEL), _rows(tm, 512), _rows(tm, 512), _acc(1, D_MODEL),
                   _acc(1, D_MODEL)] + [_ANY] * n,
        out_shape=[_sds((S, D_MODEL), F32), _sds((S, D_MODEL), BF16), _sds((S, 512), BF16), _sds((S, 512), BF16),
                   _sds((1, D_MODEL), F32), _sds((1, D_MODEL), F32)] + _swap_out_shapes(grads),
        scratch_shapes=_swap_sems(n) if n else [],
        compiler_params=_cp(("arbitrary",)),
    )(dgate, dup, w_gate, w_up, h1, mix, dh2, g_pre, g_post, w_o, *grads)
    return dh1, dmix, dro, dmo, dg2, dg1, got


def _attn_delta(o, do, S, grads=()):
    tm = min(512, S)
    n = len(grads)
    last = S // tm - 1

    def body(o_ref, do_ref, *rest):
        g_ins, (dot_ref, d_ref), g_outs, sems = rest[:n], rest[n:n + 2], rest[n + 2:2 * n + 2], rest[2 * n + 2:]
        if n:
            @pl.when(pl.program_id(0) == 0)
            def _():
                for cp in _swap_copies(g_ins, g_outs, sems):
                    cp.start()

        do = do_ref[...].astype(F32)
        prod_t = (o_ref[...].astype(F32) * do).T
        dot_ref[...] = do.T.astype(BF16)
        for h in range(MLA_HEADS):
            d_ref[h // 2, (h % 2):(h % 2) + 1, :] = jnp.sum(prod_t[h * 64:(h + 1) * 64, :], axis=0, keepdims=True)

        if n:
            @pl.when(pl.program_id(0) == last)
            def _():
                for cp in _swap_copies(g_ins, g_outs, sems):
                    cp.wait()

    dot, delta, *got = pl.pallas_call(
        body, name="attn_delta", grid=(S // tm,),
        in_specs=[_rows(tm, 512), _rows(tm, 512)] + [_ANY] * n,
        out_specs=[pl.BlockSpec((512, tm), lambda i: (0, i)), pl.BlockSpec((MLA_HEADS // 2, 2, tm), lambda i: (0, 0, i))]
        + [_ANY] * n,
        out_shape=[_sds((512, S), BF16), _sds((MLA_HEADS // 2, 2, S), F32)] + _swap_out_shapes(grads),
        scratch_shapes=_swap_sems(n) if n else [],
        compiler_params=_cp(("arbitrary",)),
    )(o, do, *grads)
    return dot, delta, got


def _flash_bwd(qp, kp, kt, v, do, dot, lse, delta, S, sums=()):
    tq = min(512, S)
    nq = S // tq
    RB = ATT_ROWS
    NH = BWD_HEADS
    qb_of, kb_of, T = _tri_pairs(nq, k_major=True)
    n = len(sums)
    steps = (MLA_HEADS // NH) * T

    def body(qb_ref, kb_ref, q_ref, k_ref, kt_ref, v_ref, do_ref, dot_ref, lse_ref, dl_ref, *rest):
        g_ins, (dq_ref, dk_ref, dv_ref), g_outs = rest[:n], rest[n:n + 3], rest[n + 3:2 * n + 3]
        dk_sc, dv_sc, s_sc, dp_sc, p_sc, ds_sc = rest[2 * n + 3:2 * n + 9]
        sems = rest[2 * n + 9:]
        t = pl.program_id(1)
        qb = qb_ref[t]
        kb = kb_ref[t]
        lin = pl.program_id(0) * T + t

        if n:
            @pl.when(lin == 0)
            def _():
                for cp in _scatter_copies(g_ins, g_outs, sems):
                    cp.start()

        @pl.when(t == 0)
        def _():
            dq_ref[...] = jnp.zeros(dq_ref.shape, F32)

        @pl.when(qb == kb)
        def _():
            dk_sc[...] = jnp.zeros(dk_sc.shape, F32)
            dv_sc[...] = jnp.zeros(dv_sc.shape, F32)

        lane = lax.broadcasted_iota(jnp.int32, (tq, 64 * NH), 1)

        def step(masked):
            vv = v_ref[...]
            do_all = do_ref[...]
            mine = [(lane >= a * 64) & (lane < (a + 1) * 64) for a in range(NH)]
            for a in range(NH):
                sl = slice(a * 128, (a + 1) * 128)
                s_sc[a] = _dot_nt(k_ref[:, sl], q_ref[:, sl])
                dp_sc[a] = jnp.dot(jnp.where(mine[a], vv, jnp.zeros_like(vv)), dot_ref[...],
                                   preferred_element_type=F32)
            for a in range(NH):
                sl = slice(a * 128, (a + 1) * 128)
                lse = lse_ref[a:a + 1, :]
                dl = dl_ref[a:a + 1, :]
                for r in range(0, tq, RB):
                    sc = s_sc[a, r:r + RB, :]
                    if masked:
                        sc = jnp.where(_causal_keep(r, RB, tq), sc, NEG)
                    p = jnp.exp(sc - lse)
                    p_sc[a, r:r + RB, :] = p.astype(BF16)
                    ds_sc[a, r:r + RB, :] = (p * (dp_sc[a, r:r + RB, :] - dl)).astype(BF16)
                ds = ds_sc[a]
                dv_sc[...] += jnp.dot(p_sc[a], jnp.where(mine[a], do_all, jnp.zeros_like(do_all)),
                                      preferred_element_type=F32)
                dk_sc[:, sl] += jnp.dot(ds, q_ref[:, sl], preferred_element_type=F32)
                dq_ref[qb, sl, :] += jnp.dot(kt_ref[sl, :], ds, preferred_element_type=F32)

        @pl.when(qb > kb)
        def _():
            step(False)

        @pl.when(qb == kb)
        def _():
            step(True)

        @pl.when(qb == nq - 1)
        def _():
            dk_ref[...] = dk_sc[...].astype(BF16)
            dv_ref[...] = dv_sc[...].astype(BF16)

        if n:
            @pl.when(lin == steps - 1)
            def _():
                for cp in _scatter_copies(g_ins, g_outs, sems):
                    cp.wait()

    grid_spec = pltpu.PrefetchScalarGridSpec(
        num_scalar_prefetch=2, grid=(MLA_HEADS // NH, T),
        in_specs=[pl.BlockSpec((tq, 128 * NH), lambda j, t, qb, kb: (qb[t], j)),
                  pl.BlockSpec((tq, 128 * NH), lambda j, t, qb, kb: (kb[t], j)),
                  pl.BlockSpec((128 * NH, tq), lambda j, t, qb, kb: (j, kb[t])),
                  pl.BlockSpec((tq, 64 * NH), lambda j, t, qb, kb: (kb[t], j)),
                  pl.BlockSpec((tq, 64 * NH), lambda j, t, qb, kb: (qb[t], j)),
                  pl.BlockSpec((64 * NH, tq), lambda j, t, qb, kb: (j, qb[t])),
                  pl.BlockSpec((None, NH, tq), lambda j, t, qb, kb: (j, 0, qb[t])),
                  pl.BlockSpec((None, NH, tq), lambda j, t, qb, kb: (j, 0, qb[t]))] + [_ANY] * n,
        out_specs=[pl.BlockSpec((nq, 128 * NH, tq), lambda j, t, qb, kb: (0, j, 0), pipeline_mode=pl.Buffered(1)),
                   pl.BlockSpec((tq, 128 * NH), lambda j, t, qb, kb: (kb[t], j)),
                   pl.BlockSpec((tq, 64 * NH), lambda j, t, qb, kb: (kb[t], j))] + [_ANY] * n,
        scratch_shapes=[pltpu.VMEM((tq, 128 * NH), F32), pltpu.VMEM((tq, 64 * NH), F32), pltpu.VMEM((NH, tq, tq), F32),
                        pltpu.VMEM((NH, tq, tq), F32), pltpu.VMEM((NH, tq, tq), BF16), pltpu.VMEM((NH, tq, tq), BF16)]
        + (_scatter_sems(n) if n else []),
    )
    dq, dk, dv, *parts = pl.pallas_call(
        body, name="flash_bwd", grid_spec=grid_spec,
        out_shape=[_sds((nq, 1024, tq), F32), _sds((S, 1024), BF16), _sds((S, 512), BF16)] + _scatter_out_shapes(sums),
        compiler_params=_cp(("arbitrary", "arbitrary")),
    )(qb_of, kb_of, qp, kp, kt, v, do, dot, lse.reshape(MLA_HEADS // NH, NH, S), delta.reshape(MLA_HEADS // NH, NH, S),
      *sums)
    return dq, dk, dv, parts


def _mla_up_bwd(dqp, dkp, dv, cq, ckv, gq, gkv, w_uq, w_ukv, tabs, S):
    tm = min(512, S)

    def body(dq_ref, dk_ref, dv_ref, cq_ref, ckv_ref, gq_ref, gkv_ref, wuq_ref, wukv_ref, cm_ref, sa_ref, sb_ref,
             dqh_ref, dkv_ref, dcq_ref, dckv_ref, dkr_ref, dgq_ref, dgkv_ref):
        @pl.when(pl.program_id(0) == 0)
        def _():
            dgq_ref[...] = jnp.zeros(dgq_ref.shape, F32)
            dgkv_ref[...] = jnp.zeros(dgkv_ref.shape, F32)

        cm = cm_ref[...]
        sa = sa_ref[...]
        sb = sb_ref[...]
        lane = lax.broadcasted_iota(jnp.int32, (tm, 128), 1)
        dkr_r = jnp.zeros((tm, 128), F32)
        for h in range(MLA_HEADS):
            sl = slice(h * 128, (h + 1) * 128)
            dqh_ref[:, sl] = (_unrope_mla(dq_ref[sl, :].T, cm, sa, sb) * SCALE_MLA).astype(BF16)
            gk = dk_ref[:, sl]
            dkr_r = dkr_r + gk.astype(F32)
            dkv_ref[:, sl] = gk
        dkr_r = jnp.where((lane >= 64) & (lane < 96), dkr_r, 0.0)
        dkr_ref[...] = _unrope_mla(dkr_r, cm, sa, sb).astype(BF16)
        dkv_ref[:, 1024:1536] = dv_ref[...]
        dcq, ga = _rms_bwd(_dot_nt(dqh_ref[...], wuq_ref[...]), cq_ref[...], gq_ref[...])
        dcq_ref[...] = dcq.astype(BF16)
        dgq_ref[...] += _colsum(ga)
        dckv, gb = _rms_bwd(_dot_nt(dkv_ref[...], wukv_ref[...]), ckv_ref[...], gkv_ref[...])
        dckv_ref[...] = dckv.astype(BF16)
        dgkv_ref[...] += _colsum(gb)

    per_q = dqp.shape[2] // tm
    return pl.pallas_call(
        body, name="mla_up_bwd", grid=(S // tm,),
        in_specs=[pl.BlockSpec((None, 1024, tm), lambda i: (i // per_q, 0, i % per_q)),
                  _rows(tm, 1024), _rows(tm, 512), _rows(tm, Q_LORA), _rows(tm, KV_LORA),
                  _full(1, Q_LORA), _full(1, KV_LORA), _full(Q_LORA, 1024), _full(KV_LORA, 1536)] + [_rows(tm, 128)] * 3,
        out_specs=[_rows(tm, 1024), _rows(tm, 1536), _rows(tm, Q_LORA), _rows(tm, KV_LORA), _rows(tm, 128),
                   _acc(1, Q_LORA), _acc(1, KV_LORA)],
        out_shape=[_sds((S, 1024), BF16), _sds((S, 1536), BF16), _sds((S, Q_LORA), BF16), _sds((S, KV_LORA), BF16),
                   _sds((S, 128), BF16), _sds((1, Q_LORA), F32), _sds((1, KV_LORA), F32)],
        compiler_params=_cp(("arbitrary",)),
    )(dqp, dkp, dv, cq, ckv, gq, gkv, w_uq, w_ukv, *tabs[2:])


def _ret_bwd(rq, rk, rv, rprev, ry, rg, dro, gn_w, tabs, S):
    C = RET_CHUNK
    N = S // C
    G = min(RET_GROUP, N)
    NB = N // G

    def body(lg_ref, q_ref, k_ref, v_ref, rp_ref, ry_ref, rg_ref, dro_ref, w_ref, cr_ref, sr_ref,
             drq_ref, drk_ref, drv_ref, drg_ref, dw_ref, g_sc):
        @pl.when(pl.program_id(1) == 0)
        def _():
            g_sc[...] = jnp.zeros(g_sc.shape, F32)
            dw_ref[...] = jnp.zeros(dw_ref.shape, F32)

        dmat, zeta, xi, g_chunk = _decay_terms(lg_ref)
        w = w_ref[...]
        gacc = g_sc[...]
        dw = jnp.zeros((1, 128), F32)
        for i in reversed(range(G)):
            rows = slice(i * C, (i + 1) * C)
            ry = ry_ref[rows, :]
            mu = jnp.mean(ry, axis=-1, keepdims=True)
            yc = ry - mu
            rstd = lax.rsqrt(jnp.mean(yc * yc, axis=-1, keepdims=True) + EPS)
            yh = yc * rstd
            g = rg_ref[rows, :]
            s = _sig(g)
            dout = dro_ref[rows, :].astype(F32)
            drg_ref[rows, :] = (dout * (yh * w) * (s * (1.0 + g * (1.0 - s)))).astype(BF16)
            dgn = dout * (g * s)
            dw = dw + _colsum(dgn * yh)
            dyh = dgn * w
            dry = rstd * (dyh - jnp.mean(dyh, axis=-1, keepdims=True) - yh * jnp.mean(dyh * yh, axis=-1, keepdims=True))
            do = dry.astype(BF16)

            q = q_ref[rows, :]
            k = k_ref[rows, :]
            v = v_ref[rows, :]
            gfut = gacc.astype(BF16)
            sc = (_dot_nt(q, k) * dmat).astype(BF16)
            dsc = (_dot_nt(do, v) * dmat).astype(BF16)
            dq = jnp.dot(dsc, k, preferred_element_type=F32) + _dot_nt(do, rp_ref[i]) * xi
            dk = _dot_tn(dsc, q) + _dot_nt(v, gfut) * zeta
            dv = _dot_tn(sc, do) + jnp.dot(k, gfut, preferred_element_type=F32) * zeta
            gacc = g_chunk * gacc + _dot_tn(q, xi * dry)
            cr = cr_ref[rows, :]
            sr = sr_ref[rows, :]
            drq_ref[rows, :] = _unrope_ret(dq, cr, sr).astype(BF16)
            drk_ref[rows, :] = _unrope_ret(dk * SCALE_RET, cr, sr).astype(BF16)
            drv_ref[rows, :] = dv.astype(BF16)
        g_sc[...] = gacc
        dw_ref[...] += dw

    blk = pl.BlockSpec((G * C, 128), lambda h, n: (NB - 1 - n, h))
    tab = pl.BlockSpec((G * C, 128), lambda h, n: (NB - 1 - n, 0))
    return pl.pallas_call(
        body, name="ret_bwd", grid=(RET_HEADS, NB),
        in_specs=[pl.BlockSpec((None, 8, 128), lambda h, n: (h, 0, 0)), blk, blk, blk,
                  pl.BlockSpec((G, 128, 128), lambda h, n: (h * NB + NB - 1 - n, 0, 0)), blk, blk, blk,
                  pl.BlockSpec((1, 128), lambda h, n: (0, h)), tab, tab],
        out_specs=[blk, blk, blk, blk, pl.BlockSpec((1, 128), lambda h, n: (0, h))],
        out_shape=[_sds((S, 512), BF16)] * 4 + [_sds((1, 512), F32)],
        scratch_shapes=[pltpu.VMEM((128, 128), F32)],
        compiler_params=_cp(("parallel", "arbitrary")),
    )(_decay_table(), rq, rk, rv, rprev, ry, rg, dro, gn_w, tabs[0], tabs[1])


def _inproj_bwd(drq, drk, drv, drg, dcq, dckv, dkr, w_in, dh1, x, g, S):
    tm = min(512, S)

    def body(drq_ref, drk_ref, drv_ref, drg_ref, dcq_ref, dckv_ref, dkr_ref, w_ref, dh1_ref, x_ref, g_ref,
             gx_ref, dproj_ref, dg_ref):
        @pl.when(pl.program_id(0) == 0)
        def _():
            dg_ref[...] = jnp.zeros(dg_ref.shape, F32)

        dproj_ref[:, 0:512] = drq_ref[...]
        dproj_ref[:, 512:1024] = drk_ref[...]
        dproj_ref[:, 1024:1536] = drv_ref[...]
        dproj_ref[:, 1536:2048] = drg_ref[...]
        dproj_ref[:, 2048:2432] = dcq_ref[...]
        dproj_ref[:, 2432:2688] = dckv_ref[...]
        dproj_ref[:, 2688:2816] = dkr_ref[...]
        dx, ga = _rms_bwd(_dot_nt(dproj_ref[...], w_ref[...]), x_ref[...], g_ref[...])
        gx_ref[...] = dh1_ref[...] + dx
        dg_ref[...] += _colsum(ga)

    return pl.pallas_call(
        body, name="inproj_bwd", grid=(S // tm,),
        in_specs=[_rows(tm, 512)] * 4 + [_rows(tm, Q_LORA), _rows(tm, KV_LORA), _rows(tm, 128),
                                         _full(D_MODEL, IN_COLS_P), _rows(tm, D_MODEL), _rows(tm, D_MODEL),
                                         _full(1, D_MODEL)],
        out_specs=[_rows(tm, D_MODEL), _rows(tm, IN_COLS_P), _acc(1, D_MODEL)],
        out_shape=[_sds((S, D_MODEL), F32), _sds((S, IN_COLS_P), BF16), _sds((1, D_MODEL), F32)],
        compiler_params=_cp(("arbitrary",)),
    )(drq, drk, drv, drg, dcq, dckv, dkr, w_in, dh1, x, g)


def _pad_weights(w):
    w_in = w["w_in"]
    z = lambda r, c: jnp.zeros((r, c), BF16)
    w_in_p = jnp.concatenate([w_in[:, :2688], z(1024, 64), w_in[:, 2688:2720], z(1024, 32)], axis=1)
    w_uq_p = jnp.pad(w["w_uq"].reshape(Q_LORA, MLA_HEADS, 96), ((0, 0), (0, 0), (0, 32))).reshape(Q_LORA, 1024)
    ukv = w["w_ukv"].reshape(KV_LORA, MLA_HEADS, 128)
    k_part = jnp.pad(ukv[:, :, :64], ((0, 0), (0, 0), (0, 64))).reshape(KV_LORA, 1024)
    w_ukv_p = jnp.concatenate([k_part, ukv[:, :, 64:].reshape(KV_LORA, 512)], axis=1)
    return w_in_p, w_uq_p, w_ukv_p


BIG_SPEC = {n: (r, c, ax) for n, r, c, ax in BIG}
COLUMN_MAJOR = ("w_in", "w_uq", "w_gate", "w_up")
GRAD_TRANSPOSED = ("w_gate", "w_up")
GATHER_FIRST = ("w_in", "w_uq", "w_ukv")
GATHER_LATE = tuple(n for n, _, _, _ in BIG if n not in GATHER_FIRST)
REDUCE_EARLY = ("w_ple_gate", "w_ple_proj", "w_down", "w_gate", "w_up", "w_o")
REDUCE_LAST = tuple(n for n, _, _, _ in BIG if n not in REDUCE_EARLY)


def _local_step(x, p, pos_f, tgt, w, sm, late_shards=None, c_idx=None):
    S = x.shape[0]
    spread = late_shards is not None
    w = dict(w)
    tabs, xn, first = _prologue(pos_f, x, sm["pre_mix_norm"], S,
                                [late_shards[n] for n in GATHER_FIRST] if spread else ())
    for i, n in enumerate(GATHER_FIRST if spread else ()):
        w[n] = _from_chips(first[i], BIG_SPEC[n][2])
    w_in_p, w_uq_p, w_ukv_p = _pad_weights(w)

    rq, rk, rv, rg, cq, ckv, kr = _inproj(xn, w_in_p, tabs, S)
    cqn, ckvn, qp, kp, v, kt, vt = _mla_up(cq, ckv, kr, sm["mla_q_norm"], sm["mla_kv_norm"], w_uq_p, w_ukv_p, tabs, S)
    mo, lse, gathered = _flash_fwd(qp, kp, vt, S, [late_shards[n] for n in GATHER_LATE] if spread else ())
    for i, n in enumerate(GATHER_LATE if spread else ()):
        w[n] = _from_chips(gathered[i], 0 if n in GRAD_TRANSPOSED else BIG_SPEC[n][2])
    if not spread:
        w.update({n: w[n].T for n in GRAD_TRANSPOSED})
    ry, ro, rprev = _ret_fwd(rq, rk, rv, rg, sm["ret_gn_w"], S)
    mix, h1, hn = _outproj(ro, mo, x, w["w_o"], sm["post_mix_norm"], sm["pre_ffn_norm"], S)
    dgate_f, dup_f, act = _ffn_up(hn, w["w_gate"], w["w_up"], S)
    ff, h2 = _ffn_down(act, w["w_down"], h1, sm["post_ffn_norm"], S)
    dz, dpe, dh2, h2b, loss_vec, d_ple_norm, d_b = _ple_loss(
        p, h2, tgt, w["w_ple_proj"], w["w_ple_gate"], sm["b_ple_gate"], sm["ple_norm"], S)

    gw = {}
    gs = {"ple_norm": d_ple_norm, "b_ple_gate": d_b}
    gw["w_ple_gate"] = _wgrad(h2b, dz, "wgrad_ple_gate", S)
    gw["w_ple_proj"] = _wgrad(p, dpe, "wgrad_ple_proj", S)
    dff, dgate, dup, gs["post_ffn_norm"] = _ffn_down_bwd(dh2, ff, sm["post_ffn_norm"], w["w_down"], dgate_f, dup_f, S)
    gw["w_down"] = _wgrad(act, dff, "wgrad_down", S)
    if spread:
        gw["w_gate"] = _wgrad(dgate, hn, "wgrad_gate", S)
        gw["w_up"] = _wgrad(dup, hn, "wgrad_up", S)
    else:
        gw["w_gate"] = _wgrad(hn, dgate, "wgrad_gate", S)
        gw["w_up"] = _wgrad(hn, dup, "wgrad_up", S)
    first = REDUCE_EARLY[:-1]
    g4 = [_by_chip(gw.pop(n), *((D_FF, D_MODEL, 0) if n in GRAD_TRANSPOSED else BIG_SPEC[n]))
          for n in first] if spread else []
    dh1, dmix, dro, dmo, gs["pre_ffn_norm"], gs["post_mix_norm"], got = _ffn_up_bwd(
        dgate, dup, w["w_gate"], w["w_up"], h1, mix, dh2, sm["pre_ffn_norm"], sm["post_mix_norm"], w["w_o"], S, g4)
    gw["w_o"] = jnp.concatenate([_wgrad(ro, dmix, "wgrad_o_ret", S), _wgrad(mo, dmix, "wgrad_o_mla", S)], axis=0)
    g4_o = [_by_chip(gw.pop("w_o"), *BIG_SPEC["w_o"])] if spread else []

    dmo_t, delta, got_o = _attn_delta(mo, dmo, S, g4_o)
    sums = [_add_half_rows(a, b, c_idx, "rs_add_halves_" + n)
            for n, a, b in zip(REDUCE_EARLY, g4 + g4_o, list(got) + list(got_o))] if spread else []
    dqp, dkp, dv, parts = _flash_bwd(qp, kp, kt, v, dmo, dmo_t, lse, delta, S, sums)
    dqh, dkv, dcq, dckv, dkr, gs["mla_q_norm"], gs["mla_kv_norm"] = _mla_up_bwd(
        dqp, dkp, dv, cq, ckv, sm["mla_q_norm"], sm["mla_kv_norm"], w_uq_p, w_ukv_p, tabs, S)
    g_uq_p = _wgrad(cqn, dqh, "wgrad_uq", S)
    g_ukv_p = _wgrad(ckvn, dkv, "wgrad_ukv", S)
    gw["w_uq"] = g_uq_p.reshape(Q_LORA, MLA_HEADS, 128)[:, :, :96].reshape(Q_LORA, 768)
    gw["w_ukv"] = jnp.concatenate(
        [g_ukv_p[:, :1024].reshape(KV_LORA, MLA_HEADS, 128)[:, :, :64], g_ukv_p[:, 1024:].reshape(KV_LORA, MLA_HEADS, 64)],
        axis=2).reshape(KV_LORA, 1024)

    drq, drk, drv, drg, gs["ret_gn_w"] = _ret_bwd(rq, rk, rv, rprev, ry, rg, dro, sm["ret_gn_w"], tabs, S)
    grad_x, dproj, gs["pre_mix_norm"] = _inproj_bwd(drq, drk, drv, drg, dcq, dckv, dkr, w_in_p, dh1, x,
                                                    sm["pre_mix_norm"], S)
    g_in_p = _wgrad(xn, dproj, "wgrad_in", S)
    gw["w_in"] = jnp.concatenate([g_in_p[:, :2688], g_in_p[:, 2752:2784]], axis=1)
    return loss_vec, grad_x, gw, gs, ((sums, parts) if spread else None)


def _my_place():
    x = lax.axis_index("x")
    y = lax.axis_index("y")
    c = lax.axis_index("c")
    return x, y, c


def _other_chips(x, y):
    return [(1 - x, y), (x, 1 - y), (1 - x, 1 - y)]


_ANY = pl.BlockSpec(memory_space=pl.ANY)


def _small_copies(v_ref, slots, sems):
    send, recv, lsem = sems
    x, y, c = _my_place()
    me = 4 * x + 2 * y + c
    cps = [pltpu.make_async_copy(v_ref, slots.at[me], lsem)]
    for r in range(1, N_DEV):
        peer = (x ^ (r >> 2), y ^ ((r >> 1) & 1), c ^ (r & 1))
        cps.append(pltpu.make_async_remote_copy(
            src_ref=v_ref, dst_ref=slots.at[me], send_sem=send.at[r - 1], recv_sem=recv.at[r - 1],
            device_id=peer, device_id_type=MESH))
    return cps


def _small_sum(slots, out_ref):
    acc = slots[0]
    for d in range(1, N_DEV):
        acc = acc + slots[d]
    out_ref[...] = acc
    loss = jnp.sum(acc[9:10, :], axis=1, keepdims=True) * (0.5 / D_MODEL)
    out_ref[9:10, :] = jnp.broadcast_to(loss, (1, PACK_COLS))


def _small_scratch():
    return [pltpu.VMEM((N_DEV, SMALL_ROWS, PACK_COLS), F32), pltpu.SemaphoreType.DMA((N_DEV - 1,)),
            pltpu.SemaphoreType.DMA((N_DEV - 1,)), pltpu.SemaphoreType.DMA]


N_BIG = len(BIG)


def _half(c, rows, align):
    h = rows // 2
    return pl.ds(pl.multiple_of(c * h, align), h)


def _gather_out_shapes(shards):
    return [_sds((N_CHIPS,) + tuple(s.shape), BF16) for s in shards]


def _gather_sems(n):
    return [pltpu.SemaphoreType.DMA((n, 3))] * 4 + [pltpu.SemaphoreType.DMA((n,))] * 2


def _gather_phase(phase, ins, outs, sems):
    send1, recv1, send2, recv2, send3, recv3 = sems
    x, y, c = _my_place()
    me = 2 * x + y
    chips = _other_chips(x, y)
    sib = (x, y, 1 - c)
    for t in range(len(ins)):
        rows = ins[t].shape[0]
        half = _half(c, rows, 16)
        other = _half(1 - c, rows, 16)
        def own():
            return pltpu.make_async_remote_copy(
                src_ref=ins[t], dst_ref=outs[t].at[me], send_sem=send3.at[t], recv_sem=recv3.at[t],
                device_id=sib, device_id_type=MESH)

        if phase == 0:
            own().start()
        if phase == 2:
            own().wait()
        for k, (cx, cy) in enumerate(chips):
            src = 2 * cx + cy

            def over_ici(slab):
                return pltpu.make_async_remote_copy(
                    src_ref=ins[t].at[half], dst_ref=outs[t].at[slab, half], send_sem=send1.at[t, k],
                    recv_sem=recv1.at[t, k], device_id=(cx, cy, c), device_id_type=MESH)

            def over_d2d(rows):
                return pltpu.make_async_remote_copy(
                    src_ref=outs[t].at[src, rows], dst_ref=outs[t].at[src, rows], send_sem=send2.at[t, k],
                    recv_sem=recv2.at[t, k], device_id=sib, device_id_type=MESH)

            if phase == 0:
                over_ici(me).start()
            if phase == 1:
                over_ici(src).wait_recv()
                over_d2d(half).start()
            if phase == 2:
                over_d2d(other).wait_recv()
                over_ici(me).wait_send()
                over_d2d(half).wait_send()


def _swap_copies(ins, outs, sems):
    send, recv = sems
    x, y, c = _my_place()
    return [pltpu.make_async_remote_copy(
        src_ref=ins[t].at[:, _half(1 - c, ins[t].shape[1], 8)], dst_ref=outs[t], send_sem=send.at[t],
        recv_sem=recv.at[t], device_id=(x, y, 1 - c), device_id_type=MESH) for t in range(len(ins))]


def _swap_out_shapes(gs):
    return [_sds((N_CHIPS, g.shape[1] // 2, g.shape[2]), F32) for g in gs]


def _swap_sems(n):
    return [pltpu.SemaphoreType.DMA((n,)), pltpu.SemaphoreType.DMA((n,))]


def _swap_half_rows(gs):
    n = len(gs)

    def body(*refs):
        cps = _swap_copies(refs[:n], refs[n:2 * n], refs[2 * n:])
        for cp in cps:
            cp.start()
        for cp in cps:
            cp.wait()

    return pl.pallas_call(
        body, name="rs_swap_halves",
        in_specs=[_ANY] * n, out_specs=[_ANY] * n, out_shape=_swap_out_shapes(gs), scratch_shapes=_swap_sems(n),
    )(*gs)


def _add_half_rows(g, got, c_idx, name):
    _, rows, cols = g.shape
    h = rows // 2

    def body(c_ref, a_ref, b_ref, o_ref):
        o_ref[...] = (a_ref[...] + b_ref[...]).astype(BF16)

    grid_spec = pltpu.PrefetchScalarGridSpec(
        num_scalar_prefetch=1, grid=(N_CHIPS,),
        in_specs=[pl.BlockSpec((None, h, cols), lambda j, c: (j, c[0], 0)),
                  pl.BlockSpec((None, h, cols), lambda j, c: (j, 0, 0))],
        out_specs=pl.BlockSpec((None, h, cols), lambda j, c: (j, 0, 0)),
    )
    return pl.pallas_call(
        body, name=name, grid_spec=grid_spec, out_shape=_sds((N_CHIPS, h, cols), BF16),
        compiler_params=_cp(("parallel",)),
    )(c_idx, g, got)


def _scatter_to_chips(ts, vec):
    n = len(ts)

    def body(*refs):
        ins, v_ref, outs, small_ref = refs[:n], refs[n], refs[n + 1:2 * n + 1], refs[2 * n + 1]
        slots, small_sems, sems = refs[2 * n + 2], refs[2 * n + 3:2 * n + 6], refs[2 * n + 6:]
        small = _small_copies(v_ref, slots, small_sems)
        cps = _scatter_copies(ins, outs, sems)
        for cp in small + cps:
            cp.start()
        for cp in small:
            cp.wait()
        _small_sum(slots, small_ref)
        for cp in cps:
            cp.wait()

    vm = pl.BlockSpec(memory_space=pltpu.VMEM)
    *parts, small_sum = pl.pallas_call(
        body, name="rs_scatter_chips",
        in_specs=[_ANY] * n + [vm], out_specs=[_ANY] * n + [vm],
        out_shape=_scatter_out_shapes(ts) + [_sds((SMALL_ROWS, PACK_COLS), F32)],
        scratch_shapes=_small_scratch() + _scatter_sems(n),
    )(*ts, vec)
    return parts, small_sum


def _scatter_copies(ins, outs, sems):
    send, recv = sems
    x, y, c = _my_place()
    return [pltpu.make_async_remote_copy(
        src_ref=ins[t].at[2 * cx + cy], dst_ref=outs[t].at[k], send_sem=send.at[t, k], recv_sem=recv.at[t, k],
        device_id=(cx, cy, c), device_id_type=MESH)
        for t in range(len(ins)) for k, (cx, cy) in enumerate(_other_chips(x, y))]


def _scatter_out_shapes(ts):
    return [_sds((3,) + tuple(t.shape[1:]), BF16) for t in ts]


def _scatter_sems(n):
    return [pltpu.SemaphoreType.DMA((n, 3)), pltpu.SemaphoreType.DMA((n, 3))]


def _add_four(mine, parts, place, name):
    _, h, cols = parts.shape

    def body(pl_ref, m_ref, p_ref, o_ref):
        o_ref[...] = ((m_ref[...].astype(F32) + p_ref[0].astype(F32)) + p_ref[1].astype(F32)) + p_ref[2].astype(F32)

    grid_spec = pltpu.PrefetchScalarGridSpec(
        num_scalar_prefetch=1, grid=(1,),
        in_specs=[pl.BlockSpec((None, h, cols), lambda i, pc: (pc[0], 0, 0)),
                  pl.BlockSpec((3, h, cols), lambda i, pc: (0, 0, 0))],
        out_specs=pl.BlockSpec((h, cols), lambda i, pc: (pc[1], 0)),
    )
    return pl.pallas_call(
        body, name=name, grid_spec=grid_spec, out_shape=_sds((2 * h, cols), F32),
        compiler_params=_cp(("arbitrary",)),
    )(place, mine, parts)


def _join_half_rows(rs):
    n = len(rs)

    def body(*refs):
        ins, outs = refs[:n], refs[n:2 * n]
        send, recv = refs[2 * n:]
        x, y, c = _my_place()
        cps = []
        for t in range(n):
            half = _half(c, outs[t].shape[0], 8)
            rc = pltpu.make_async_remote_copy(
                src_ref=ins[t].at[half], dst_ref=outs[t].at[half], send_sem=send.at[t], recv_sem=recv.at[t],
                device_id=(x, y, 1 - c), device_id_type=MESH)
            rc.start()
            cps.append(rc)
        for cp in cps:
            cp.wait()

    return pl.pallas_call(
        body, name="rs_join_halves",
        in_specs=[_ANY] * n, out_specs=[_ANY] * n,
        out_shape=[_sds(r.shape, F32) for r in rs],
        input_output_aliases={i: i for i in range(n)},
        scratch_shapes=[pltpu.SemaphoreType.DMA((n,))] * 2,
    )(*rs)


def _by_chip(full, rows, cols, axis):
    if axis == 0:
        return full.reshape(N_CHIPS, rows // N_CHIPS, cols)
    return full.reshape(rows, N_CHIPS, cols // N_CHIPS).transpose(1, 0, 2)


def _from_chips(parts, axis):
    _, r, c = parts.shape
    if axis == 0:
        return parts.reshape(N_CHIPS * r, c)
    return parts.transpose(1, 0, 2).reshape(r, N_CHIPS * c)


def _adamw(wt, g, m, v, name):
    _, R, C = wt.shape
    tr = max(d for d in range(8, R + 1, 8) if R % d == 0 and (d * C <= 256 * 1024 or d == 8))

    def body(w_ref, g_ref, m_ref, v_ref, d_ref, nm_ref, nv_ref):
        gg = g_ref[...]
        m_new = ADAM_B1 * m_ref[...] + (1.0 - ADAM_B1) * gg
        v_new = ADAM_B2 * v_ref[...] + (1.0 - ADAM_B2) * (gg * gg)
        m_hat = m_new / (1.0 - ADAM_B1 ** ADAM_STEP)
        v_hat = v_new / (1.0 - ADAM_B2 ** ADAM_STEP)
        d_ref[...] = -ADAM_LR * (m_hat / (jnp.sqrt(v_hat) + ADAM_EPS) + ADAM_WD * w_ref[...])
        nm_ref[...] = m_new
        nv_ref[...] = v_new

    spec = pl.BlockSpec((None, tr, C), lambda i: (0, i, 0))
    return pl.pallas_call(
        body, name=name, grid=(R // tr,), in_specs=[spec, pl.BlockSpec((tr, C), lambda i: (i, 0)), spec, spec],
        out_specs=[spec] * 3, out_shape=[_sds((1, R, C), F32)] * 3,
        compiler_params=_cp(("parallel",)),
    )(wt, g, m, v)


def _pack_small(vals, loss_vec=None):
    rows = [jnp.pad(vals[n].reshape(-1), (0, PACK_COLS - sz)) for n, sz in SMALL]
    rows.append(loss_vec.reshape(-1) if loss_vec is not None else jnp.zeros((PACK_COLS,), F32))
    rows += [jnp.zeros((PACK_COLS,), F32)] * (SMALL_ROWS - len(rows))
    return jnp.stack(rows)


def kernel(x, p, positions, pre_mix_norm, w_in, ret_gn_w, mla_q_norm, w_uq, mla_kv_norm, w_ukv, w_o, post_mix_norm, pre_ffn_norm, w_gate, w_up, w_down, post_ffn_norm, w_ple_proj, ple_norm, w_ple_gate, b_ple_gate, loss_target, m_pre_mix_norm, m_w_in, m_ret_gn_w, m_mla_q_norm, m_w_uq, m_mla_kv_norm, m_w_ukv, m_w_o, m_post_mix_norm, m_pre_ffn_norm, m_w_gate, m_w_up, m_w_down, m_post_ffn_norm, m_w_ple_proj, m_ple_norm, m_w_ple_gate, m_b_ple_gate, v_pre_mix_norm, v_w_in, v_ret_gn_w, v_mla_q_norm, v_w_uq, v_mla_kv_norm, v_w_ukv, v_w_o, v_post_mix_norm, v_pre_ffn_norm, v_w_gate, v_w_up, v_w_down, v_post_ffn_norm, v_w_ple_proj, v_ple_norm, v_w_ple_gate, v_b_ple_gate):
    wts = dict(pre_mix_norm=pre_mix_norm, w_in=w_in, ret_gn_w=ret_gn_w, mla_q_norm=mla_q_norm, w_uq=w_uq,
               mla_kv_norm=mla_kv_norm, w_ukv=w_ukv, w_o=w_o, post_mix_norm=post_mix_norm, pre_ffn_norm=pre_ffn_norm,
               w_gate=w_gate, w_up=w_up, w_down=w_down, post_ffn_norm=post_ffn_norm, w_ple_proj=w_ple_proj,
               ple_norm=ple_norm, w_ple_gate=w_ple_gate, b_ple_gate=b_ple_gate)
    mom = dict(pre_mix_norm=m_pre_mix_norm, w_in=m_w_in, ret_gn_w=m_ret_gn_w, mla_q_norm=m_mla_q_norm, w_uq=m_w_uq,
               mla_kv_norm=m_mla_kv_norm, w_ukv=m_w_ukv, w_o=m_w_o, post_mix_norm=m_post_mix_norm,
               pre_ffn_norm=m_pre_ffn_norm, w_gate=m_w_gate, w_up=m_w_up, w_down=m_w_down, post_ffn_norm=m_post_ffn_norm,
               w_ple_proj=m_w_ple_proj, ple_norm=m_ple_norm, w_ple_gate=m_w_ple_gate, b_ple_gate=m_b_ple_gate)
    var = dict(pre_mix_norm=v_pre_mix_norm, w_in=v_w_in, ret_gn_w=v_ret_gn_w, mla_q_norm=v_mla_q_norm, w_uq=v_w_uq,
               mla_kv_norm=v_mla_kv_norm, w_ukv=v_w_ukv, w_o=v_w_o, post_mix_norm=v_post_mix_norm,
               pre_ffn_norm=v_pre_ffn_norm, w_gate=v_w_gate, w_up=v_w_up, w_down=v_w_down, post_ffn_norm=v_post_ffn_norm,
               w_ple_proj=v_w_ple_proj, ple_norm=v_ple_norm, w_ple_gate=v_w_ple_gate, b_ple_gate=v_b_ple_gate)

    S = x.shape[1]
    shard2d = {n: wts[n][0] for n, _, _, _ in BIG}
    small2d = {n: wts[n] for n, _ in SMALL}

    shard_bf = {n: (jnp.swapaxes(wts[n], 1, 2)[0] if n in GRAD_TRANSPOSED else shard2d[n]).astype(BF16) for n in shard2d}
    pos_f = positions.astype(F32).reshape(S, 1)
    c_idx = lax.axis_index("c").astype(jnp.int32).reshape(1)
    loss_vec, grad_x, gw, gs, (sums_early, parts_early) = _local_step(
        x[0], p[0, 0], pos_f, loss_target[0], {}, small2d, shard_bf, c_idx)

    g4 = [_by_chip(gw[n], *BIG_SPEC[n]) for n in REDUCE_LAST]
    got = _swap_half_rows(g4)
    sums_last = [_add_half_rows(g4[i], got[i], c_idx, "rs_add_halves_" + n) for i, n in enumerate(REDUCE_LAST)]
    parts_last, small_sum = _scatter_to_chips(sums_last, _pack_small(gs, loss_vec))
    place = jnp.stack([2 * lax.axis_index("x") + lax.axis_index("y"), lax.axis_index("c")]).astype(jnp.int32)
    names = REDUCE_EARLY + REDUCE_LAST
    reduced = _join_half_rows(
        [_add_four(sm_, pt_, place, "rs_add_chips_" + n)
         for n, sm_, pt_ in zip(names, sums_early + sums_last, list(parts_early) + list(parts_last))])
    g_shard = dict(zip(names, reduced))

    loss = small_sum[9, 0]
    g_small = {n: small_sum[i:i + 1, :sz] for i, (n, sz) in enumerate(SMALL)}

    grads, delta, new_m, new_v = {}, {}, {}, {}
    for n, _, _, _ in BIG:
        if n in COLUMN_MAJOR:
            turn = lambda a: jnp.swapaxes(a, 1, 2)
            g_t = g_shard[n] if n in GRAD_TRANSPOSED else g_shard[n].T
            d, nm, nv = _adamw(turn(wts[n]), g_t, turn(mom[n]), turn(var[n]), "adamw_" + n)
            grads[n], delta[n], new_m[n], new_v[n] = turn(g_t[None]), turn(d), turn(nm), turn(nv)
        else:
            delta[n], new_m[n], new_v[n] = _adamw(wts[n], g_shard[n], mom[n], var[n], "adamw_" + n)
            grads[n] = g_shard[n][None]
    d, nm, nv = _adamw(_pack_small(small2d)[None], small_sum, _pack_small(mom)[None], _pack_small(var)[None],
                       "adamw_small")
    for i, (n, sz) in enumerate(SMALL):
        grads[n] = g_small[n]
        delta[n], new_m[n], new_v[n] = d[0, i:i + 1, :sz], nm[0, i:i + 1, :sz], nv[0, i:i + 1, :sz]

    return (loss, grad_x[None], *[grads[n] for n in ALL_W], *[delta[n] for n in ALL_W],
            *[new_m[n] for n in ALL_W], *[new_v[n] for n in ALL_W])
```

```python
import functools
import math

import jax
import jax.numpy as jnp
import numpy as np
from jax import lax
from jax.experimental import pallas as pl
from jax.experimental.pallas import tpu as pltpu

F32 = jnp.float32
BF16 = jnp.bfloat16
MESH = pl.DeviceIdType.MESH

D_MODEL = 1024
D_FF = 2816
PLE_DIM = 256
RET_HEADS = 4
RET_DIM = 128
RET_WIDTH = 512
RET_CHUNK = 256
RET_GROUP = 4
MLA_HEADS = 8
MLA_NOPE = 64
MLA_ROPE = 32
MLA_V = 64
Q_LORA = 384
KV_LORA = 256
IN_COLS = 2720
IN_COLS_P = 2816
ROPE_BASE = 10000.0
EPS = 1e-6
SCALE_MLA = 1.0 / math.sqrt(MLA_NOPE + MLA_ROPE)
SCALE_RET = RET_DIM ** -0.5
NEG = -1e30

ADAM_LR = 0.001
ADAM_B1 = 0.9
ADAM_B2 = 0.999
ADAM_EPS = 1e-08
ADAM_WD = 0.01
ADAM_STEP = 10

N_CHIPS = 4
N_DEV = 8
VMEM_MB = 56

BIG = (
    ("w_in", 1024, 2720, 1),
    ("w_uq", 384, 768, 1),
    ("w_ukv", 256, 1024, 1),
    ("w_o", 1024, 1024, 0),
    ("w_gate", 1024, 2816, 1),
    ("w_up", 1024, 2816, 1),
    ("w_down", 2816, 1024, 0),
    ("w_ple_proj", 256, 1024, 1),
    ("w_ple_gate", 1024, 1024, 0),
)
SMALL = (
    ("pre_mix_norm", 1024),
    ("ret_gn_w", 512),
    ("mla_q_norm", 384),
    ("mla_kv_norm", 256),
    ("post_mix_norm", 1024),
    ("pre_ffn_norm", 1024),
    ("post_ffn_norm", 1024),
    ("ple_norm", 1024),
    ("b_ple_gate", 1024),
)
ALL_W = ("pre_mix_norm", "w_in", "ret_gn_w", "mla_q_norm", "w_uq", "mla_kv_norm", "w_ukv", "w_o", "post_mix_norm",
         "pre_ffn_norm", "w_gate", "w_up", "w_down", "post_ffn_norm", "w_ple_proj", "ple_norm", "w_ple_gate", "b_ple_gate")
PACK_COLS = 1024
SMALL_ROWS = 16


def _cp(sem=None, mb=VMEM_MB, **kw):
    return pltpu.CompilerParams(dimension_semantics=sem, vmem_limit_bytes=mb * 1024 * 1024, **kw)


def _bf(x):
    return x.astype(BF16)


def _dot(a, b):
    return jnp.dot(_bf(a), _bf(b), preferred_element_type=F32)


def _dot_nt(a, b):
    return lax.dot_general(_bf(a), _bf(b), (((1,), (1,)), ((), ())), preferred_element_type=F32)


def _dot_tn(a, b):
    return lax.dot_general(_bf(a), _bf(b), (((0,), (0,)), ((), ())), preferred_element_type=F32)


def _sig(x):
    return 1.0 / (1.0 + jnp.exp(-x))


def _rms(x, g):
    r = lax.rsqrt(jnp.mean(x * x, axis=-1, keepdims=True) + EPS)
    return x * r * g


def _rms_bwd(dy, x, g):
    r = lax.rsqrt(jnp.mean(x * x, axis=-1, keepdims=True) + EPS)
    xh = x * r
    dxh = dy * g
    dx = r * (dxh - xh * jnp.mean(dxh * xh, axis=-1, keepdims=True))
    return dx, dy * xh


def _colsum(x):
    return jnp.sum(x, axis=0, keepdims=True)


def _rope_ret(x, cr, sr):
    return x * cr + pltpu.roll(x, 64, 1) * sr


def _unrope_ret(dy, cr, sr):
    return dy * cr + pltpu.roll(dy * sr, 64, 1)


def _rope_mla(x, cm, sa, sb):
    return x * cm + pltpu.roll(x, 112, 1) * sa + pltpu.roll(x, 16, 1) * sb


def _unrope_mla(dy, cm, sa, sb):
    return dy * cm + pltpu.roll(dy * sa, 16, 1) + pltpu.roll(dy * sb, 112, 1)


def _rows(tm, w, col=0):
    return pl.BlockSpec((tm, w), lambda i: (i, col))


def _full(*shape):
    return pl.BlockSpec(shape, lambda i: (0,) * len(shape), pipeline_mode=pl.Buffered(1))


def _acc(*shape):
    return pl.BlockSpec(shape, lambda i: (0,) * len(shape))


def _sds(shape, dtype):
    return jax.ShapeDtypeStruct(shape, dtype)


def _rope_tables(pos_f, S, shards=()):
    tm = min(512, S)
    n = len(shards)
    steps = S // tm
    inv_r = (1.0 / (np.float32(ROPE_BASE) ** (np.arange(64, dtype=np.float32) / np.float32(64)))).astype(np.float32)
    inv_m16 = (1.0 / (np.float32(ROPE_BASE) ** (np.arange(16, dtype=np.float32) / np.float32(16)))).astype(np.float32)
    inv_r = np.concatenate([inv_r, inv_r])[None, :]
    inv_m = np.zeros((1, 128), np.float32)
    inv_m[0, 64:80] = inv_m16
    inv_m[0, 80:96] = inv_m16

    def body(pos_ref, invr_ref, invm_ref, *rest):
        w_ins, (cr_ref, sr_ref, cm_ref, sa_ref, sb_ref) = rest[:n], rest[n:n + 5]
        w_outs, sems = rest[n + 5:2 * n + 5], rest[2 * n + 5:]
        i = pl.program_id(0)
        if n:
            @pl.when(i == 0)
            def _():
                _gather_phase(0, w_ins, w_outs, sems)

            @pl.when(i == steps - 1)
            def _():
                _gather_phase(1, w_ins, w_outs, sems)

        pos = pos_ref[...]
        lane = lax.broadcasted_iota(jnp.int32, (tm, 128), 1)
        ar = pos * invr_ref[...]
        s = jnp.sin(ar)
        cr_ref[...] = jnp.cos(ar)
        sr_ref[...] = jnp.where(lane < 64, -s, s)
        am = pos * invm_ref[...]
        c2 = jnp.cos(am)
        s2 = jnp.sin(am)
        cm_ref[...] = jnp.where(lane < 64, 1.0, jnp.where(lane < 96, c2, 0.0))
        sa_ref[...] = jnp.where((lane >= 64) & (lane < 80), -s2, 0.0)
        sb_ref[...] = jnp.where((lane >= 80) & (lane < 96), s2, 0.0)

        if n:
            @pl.when(i == steps - 1)
            def _():
                _gather_phase(2, w_ins, w_outs, sems)

    outs = pl.pallas_call(
        body, name="rope_tables", grid=(steps,),
        in_specs=[_rows(tm, 1), _full(1, 128), _full(1, 128)] + [_ANY] * n,
        out_specs=[_rows(tm, 128)] * 5 + [_ANY] * n,
        out_shape=[_sds((S, 128), F32)] * 5 + _gather_out_shapes(shards),
        scratch_shapes=_gather_sems(n) if n else [],
        compiler_params=_cp(("arbitrary",)),
    )(pos_f, jnp.asarray(inv_r), jnp.asarray(inv_m), *shards)
    return outs[:5], outs[5:]


def _inproj(x, g, w_in, tabs, S):
    tm = min(512, S)

    def body(x_ref, g_ref, w_ref, cr_ref, sr_ref, cm_ref, sa_ref, sb_ref,
             xn_ref, rq_ref, rk_ref, rv_ref, rg_ref, cq_ref, ckv_ref, kr_ref):
        xb = _rms(x_ref[...], g_ref[...]).astype(BF16)
        xn_ref[...] = xb
        cr = cr_ref[...]
        sr = sr_ref[...]
        q = jnp.dot(xb, w_ref[:, 0:512], preferred_element_type=F32)
        k = jnp.dot(xb, w_ref[:, 512:1024], preferred_element_type=F32)
        for h in range(RET_HEADS):
            sl = slice(h * 128, (h + 1) * 128)
            rq_ref[:, sl] = _rope_ret(q[:, sl], cr, sr).astype(BF16)
            rk_ref[:, sl] = (_rope_ret(k[:, sl], cr, sr) * SCALE_RET).astype(BF16)
        rv_ref[...] = jnp.dot(xb, w_ref[:, 1024:1536], preferred_element_type=F32).astype(BF16)
        rg_ref[...] = jnp.dot(xb, w_ref[:, 1536:2048], preferred_element_type=F32)
        cq_ref[...] = jnp.dot(xb, w_ref[:, 2048:2432], preferred_element_type=F32)
        ckv_ref[...] = jnp.dot(xb, w_ref[:, 2432:2688], preferred_element_type=F32)
        kr = jnp.dot(xb, w_ref[:, 2688:2816], preferred_element_type=F32)
        kr_ref[...] = _rope_mla(kr, cm_ref[...], sa_ref[...], sb_ref[...])

    return pl.pallas_call(
        body, name="inproj", grid=(S // tm,),
        in_specs=[_rows(tm, D_MODEL), _full(1, D_MODEL), _full(D_MODEL, IN_COLS_P)] + [_rows(tm, 128)] * 5,
        out_specs=[_rows(tm, D_MODEL)] + [_rows(tm, 512)] * 4 + [_rows(tm, Q_LORA), _rows(tm, KV_LORA), _rows(tm, 128)],
        out_shape=[_sds((S, D_MODEL), BF16)] + [_sds((S, 512), BF16)] * 3
        + [_sds((S, 512), F32), _sds((S, Q_LORA), F32), _sds((S, KV_LORA), F32), _sds((S, 128), F32)],
        compiler_params=_cp(("parallel",)),
    )(x, g, w_in, *tabs)


def _mla_up(cq, ckv, kr, gq, gkv, w_uq, w_ukv, tabs, S):
    tm = min(512, S)

    def body(cq_ref, ckv_ref, kr_ref, gq_ref, gkv_ref, wuq_ref, wukv_ref, cm_ref, sa_ref, sb_ref,
             cqn_ref, ckvn_ref, qp_ref, kp_ref, v_ref, kt_ref, vt_ref):
        cm = cm_ref[...]
        sa = sa_ref[...]
        sb = sb_ref[...]
        cqn = _rms(cq_ref[...], gq_ref[...]).astype(BF16)
        cqn_ref[...] = cqn
        ckvn = _rms(ckv_ref[...], gkv_ref[...]).astype(BF16)
        ckvn_ref[...] = ckvn
        qh = jnp.dot(cqn, wuq_ref[...], preferred_element_type=F32)
        kv = jnp.dot(ckvn, wukv_ref[...], preferred_element_type=F32)
        kr_blk = kr_ref[...]
        for h in range(MLA_HEADS):
            sl = slice(h * 128, (h + 1) * 128)
            qp_ref[:, sl] = (_rope_mla(qh[:, sl], cm, sa, sb) * SCALE_MLA).astype(BF16)
            kh = kv[:, sl] + kr_blk
            kp_ref[:, sl] = kh.astype(BF16)
            kt_ref[sl, :] = kh.T.astype(BF16)
        for h in range(MLA_HEADS // 2):
            vh = kv[:, 1024 + h * 128:1024 + (h + 1) * 128]
            v_ref[:, h * 128:(h + 1) * 128] = vh.astype(BF16)
            vt_ref[h * 128:(h + 1) * 128, :] = vh.T.astype(BF16)

    cols = lambda r: pl.BlockSpec((r, tm), lambda i: (0, i))
    return pl.pallas_call(
        body, name="mla_up", grid=(S // tm,),
        in_specs=[_rows(tm, Q_LORA), _rows(tm, KV_LORA), _rows(tm, 128), _full(1, Q_LORA), _full(1, KV_LORA),
                  _full(Q_LORA, 1024), _full(KV_LORA, 1536)] + [_rows(tm, 128)] * 3,
        out_specs=[_rows(tm, Q_LORA), _rows(tm, KV_LORA), _rows(tm, 1024), _rows(tm, 1024), _rows(tm, 512),
                   cols(1024), cols(512)],
        out_shape=[_sds((S, Q_LORA), BF16), _sds((S, KV_LORA), BF16), _sds((S, 1024), BF16), _sds((S, 1024), BF16),
                   _sds((S, 512), BF16), _sds((1024, S), BF16), _sds((512, S), BF16)],
        compiler_params=_cp(("parallel",)),
    )(cq, ckv, kr, gq, gkv, w_uq, w_ukv, *tabs[2:])


def _tri_pairs(nq, k_major):
    if k_major:
        pairs = [(qb, kb) for kb in range(nq) for qb in range(kb, nq)]
    else:
        pairs = [(qb, kb) for qb in range(nq) for kb in range(qb + 1)]
    qb_of = np.array([p[0] for p in pairs], np.int32)
    kb_of = np.array([p[1] for p in pairs], np.int32)
    return jnp.asarray(qb_of), jnp.asarray(kb_of), len(pairs)


ATT_ROWS = 32
FWD_HEADS = 8
BWD_HEADS = 4


def _causal_keep(r0, rows, tq):
    key = r0 + lax.broadcasted_iota(jnp.int32, (rows, tq), 0)
    qry = lax.broadcasted_iota(jnp.int32, (rows, tq), 1)
    return key <= qry


def _flash_fwd(qp, kp, vt, S, shards=()):
    tq = min(512, S)
    nq = S // tq
    RB = ATT_ROWS
    NH = FWD_HEADS
    qb_of, kb_of, T = _tri_pairs(nq, k_major=False)
    n = len(shards)
    steps = (MLA_HEADS // NH) * T

    def body(qb_ref, kb_ref, q_ref, k_ref, vt_ref, *rest):
        w_ins, (o_ref, lse_ref), w_outs = rest[:n], rest[n:n + 2], rest[n + 2:2 * n + 2]
        m_sc, l_sc, acc_sc, s_sc, p_sc = rest[2 * n + 2:2 * n + 7]
        sems = rest[2 * n + 7:]
        t = pl.program_id(1)
        qb = qb_ref[t]
        kb = kb_ref[t]
        lin = pl.program_id(0) * T + t

        if n:
            @pl.when(lin == 0)
            def _():
                _gather_phase(0, w_ins, w_outs, sems)

            @pl.when(lin == steps // 2)
            def _():
                _gather_phase(1, w_ins, w_outs, sems)

        @pl.when(kb == 0)
        def _():
            m_sc[...] = jnp.full(m_sc.shape, NEG, F32)
            l_sc[...] = jnp.zeros(l_sc.shape, F32)
            acc_sc[...] = jnp.zeros(acc_sc.shape, F32)

        def scores(a):
            sl = slice(a * 128, (a + 1) * 128)
            s_sc[a] = _dot_nt(k_ref[:, sl], q_ref[:, sl])

        def step(masked):
            for a in range(NH):
                scores(a)
            for a in range(NH):
                mx = [jnp.full((8, tq), NEG, F32) for _ in range(RB // 8)]
                for r in range(0, tq, RB):
                    sc = s_sc[a, r:r + RB, :]
                    if masked:
                        sc = jnp.where(_causal_keep(r, RB, tq), sc, NEG)
                        s_sc[a, r:r + RB, :] = sc
                    for i in range(RB // 8):
                        mx[i] = jnp.maximum(mx[i], sc[i * 8:(i + 1) * 8, :])
                mx8 = functools.reduce(jnp.maximum, mx)
                m_prev = m_sc[a]
                m_new = jnp.maximum(m_prev, jnp.max(mx8, axis=0, keepdims=True))
                al = jnp.exp(m_prev - m_new)
                m_sc[a] = m_new
                ls = [jnp.zeros((8, tq), F32) for _ in range(RB // 8)]
                for r in range(0, tq, RB):
                    p = jnp.exp(s_sc[a, r:r + RB, :] - m_new)
                    for i in range(RB // 8):
                        ls[i] = ls[i] + p[i * 8:(i + 1) * 8, :]
                    p_sc[a, r:r + RB, :] = p.astype(BF16)
                l_sc[a] = al * l_sc[a] + jnp.sum(functools.reduce(jnp.add, ls), axis=0, keepdims=True)
                pair = slice((a // 2) * 128, (a // 2 + 1) * 128)
                pv = jnp.dot(vt_ref[pair, :], p_sc[a], preferred_element_type=F32)
                rs = slice(a * 64, (a + 1) * 64)
                own = slice((a % 2) * 64, (a % 2 + 1) * 64)
                acc_sc[rs, :] = acc_sc[rs, :] * al + pv[own, :]

        @pl.when(kb < qb)
        def _():
            step(False)

        @pl.when(kb == qb)
        def _():
            step(True)
            for a in range(NH):
                rs = slice(a * 64, (a + 1) * 64)
                acc_sc[rs, :] = acc_sc[rs, :] / l_sc[a]
                lse_ref[a:a + 1, :] = m_sc[a] + jnp.log(l_sc[a])
            o_ref[...] = acc_sc[...].T.astype(BF16)

        if n:
            @pl.when(lin == steps - 1)
            def _():
                _gather_phase(2, w_ins, w_outs, sems)

    grid_spec = pltpu.PrefetchScalarGridSpec(
        num_scalar_prefetch=2, grid=(MLA_HEADS // NH, T),
        in_specs=[pl.BlockSpec((tq, 128 * NH), lambda j, t, qb, kb: (qb[t], j)),
                  pl.BlockSpec((tq, 128 * NH), lambda j, t, qb, kb: (kb[t], j)),
                  pl.BlockSpec((64 * NH, tq), lambda j, t, qb, kb: (j, kb[t]))] + [_ANY] * n,
        out_specs=[pl.BlockSpec((tq, 64 * NH), lambda j, t, qb, kb: (qb[t], j)),
                   pl.BlockSpec((None, NH, tq), lambda j, t, qb, kb: (j, 0, qb[t]))] + [_ANY] * n,
        scratch_shapes=[pltpu.VMEM((NH, 1, tq), F32), pltpu.VMEM((NH, 1, tq), F32), pltpu.VMEM((64 * NH, tq), F32),
                        pltpu.VMEM((NH, tq, tq), F32), pltpu.VMEM((NH, tq, tq), BF16)] + (_gather_sems(n) if n else []),
    )
    out, lse, *gathered = pl.pallas_call(
        body, name="flash_fwd", grid_spec=grid_spec,
        out_shape=[_sds((S, 512), BF16), _sds((MLA_HEADS // NH, NH, S), F32)] + _gather_out_shapes(shards),
        compiler_params=_cp(("arbitrary", "arbitrary")),
    )(qb_of, kb_of, qp, kp, vt, *shards)
    return out, lse.reshape(MLA_HEADS // 2, 2, S), gathered


def _decay_table():
    log_g = np.log(1.0 - 2.0 ** (-5.0 - np.arange(RET_HEADS, dtype=np.float32))).astype(np.float32)
    return jnp.asarray(np.broadcast_to(log_g[:, None, None], (RET_HEADS, 8, 128)).copy())


def _decay_terms(lg_ref):
    C = RET_CHUNK
    lg = lg_ref[0:1, :]
    row = lax.broadcasted_iota(jnp.int32, (C, C), 0)
    col = lax.broadcasted_iota(jnp.int32, (C, C), 1)
    diff = (row - col).astype(F32)
    dmat = jnp.where(diff >= 0, jnp.exp(jnp.maximum(diff, 0.0) * jnp.tile(lg, (1, C // 128))), 0.0)
    j = lax.broadcasted_iota(jnp.int32, (C, 1), 0).astype(F32)
    lg1 = lg[:, 0:1]
    zeta = jnp.exp((C - 1 - j) * lg1)
    xi = jnp.exp((j + 1.0) * lg1)
    g_chunk = jnp.exp(C * lg1)
    return dmat, zeta, xi, g_chunk


def _ret_fwd(rq, rk, rv, rg, gn_w, S):
    C = RET_CHUNK
    N = S // C
    G = min(RET_GROUP, N)
    NB = N // G

    def body(lg_ref, q_ref, k_ref, v_ref, rg_ref, w_ref, ry_ref, ro_ref, rprev_ref, r_sc):
        @pl.when(pl.program_id(1) == 0)
        def _():
            r_sc[...] = jnp.zeros(r_sc.shape, F32)

        dmat, zeta, xi, g_chunk = _decay_terms(lg_ref)
        w = w_ref[...]
        r = r_sc[...]
        for i in range(G):
            rows = slice(i * C, (i + 1) * C)
            q = q_ref[rows, :]
            k = k_ref[rows, :]
            v = v_ref[rows, :]
            r_prev = r.astype(BF16)
            rprev_ref[i] = r_prev
            sc = _dot_nt(q, k) * dmat
            ry = _dot(sc, v) + jnp.dot(q, r_prev, preferred_element_type=F32) * xi
            ry_ref[rows, :] = ry
            r = g_chunk * r + _dot_tn(k, zeta * v.astype(F32))
            mu = jnp.mean(ry, axis=-1, keepdims=True)
            yc = ry - mu
            yh = yc * lax.rsqrt(jnp.mean(yc * yc, axis=-1, keepdims=True) + EPS)
            g = rg_ref[rows, :]
            ro_ref[rows, :] = (g * _sig(g) * (yh * w)).astype(BF16)
        r_sc[...] = r

    blk = pl.BlockSpec((G * C, 128), lambda h, n: (n, h))
    return pl.pallas_call(
        body, name="ret_fwd", grid=(RET_HEADS, NB),
        in_specs=[pl.BlockSpec((None, 8, 128), lambda h, n: (h, 0, 0)), blk, blk, blk, blk,
                  pl.BlockSpec((1, 128), lambda h, n: (0, h))],
        out_specs=[blk, blk, pl.BlockSpec((G, 128, 128), lambda h, n: (h * NB + n, 0, 0))],
        out_shape=[_sds((S, 512), F32), _sds((S, 512), BF16), _sds((RET_HEADS * N, 128, 128), BF16)],
        scratch_shapes=[pltpu.VMEM((128, 128), F32)],
        compiler_params=_cp(("parallel", "arbitrary")),
    )(_decay_table(), rq, rk, rv, rg, gn_w)


def _outproj(ro, mo, x, w_o, g_post, g_pre, S):
    tm = min(512, S)

    def body(ro_ref, mo_ref, x_ref, wo_ref, g1_ref, g2_ref, mix_ref, h1_ref, hn_ref):
        mix = (jnp.dot(ro_ref[...], wo_ref[0:512, :], preferred_element_type=F32)
               + jnp.dot(mo_ref[...], wo_ref[512:1024, :], preferred_element_type=F32))
        mix_ref[...] = mix.astype(BF16)
        h1 = x_ref[...] + _rms(mix, g1_ref[...])
        h1_ref[...] = h1
        hn_ref[...] = _rms(h1, g2_ref[...]).astype(BF16)

    return pl.pallas_call(
        body, name="outproj", grid=(S // tm,),
        in_specs=[_rows(tm, 512), _rows(tm, 512), _rows(tm, D_MODEL), _full(D_MODEL, D_MODEL), _full(1, D_MODEL),
                  _full(1, D_MODEL)],
        out_specs=[_rows(tm, D_MODEL)] * 3,
        out_shape=[_sds((S, D_MODEL), BF16), _sds((S, D_MODEL), F32), _sds((S, D_MODEL), BF16)],
        compiler_params=_cp(("parallel",)),
    )(ro, mo, x, w_o, g_post, g_pre)


def _ffn_up(hn, w_gate_t, w_up_t, S):
    tm = min(512, S)
    tn = D_FF // 2

    def body(hn_ref, wg_ref, wu_ref, fg_ref, fu_ref, act_ref):
        hn_b = hn_ref[...]
        g = _dot_nt(hn_b, wg_ref[...])
        u = _dot_nt(hn_b, wu_ref[...])
        s = _sig(g)
        silu = g * s
        fg_ref[...] = (u * (s + silu * (1.0 - s))).astype(BF16)
        fu_ref[...] = silu.astype(BF16)
        act_ref[...] = (silu * u).astype(BF16)

    wspec = pl.BlockSpec((tn, D_MODEL), lambda j, i: (j, 0))
    ospec = pl.BlockSpec((tm, tn), lambda j, i: (i, j))
    return pl.pallas_call(
        body, name="ffn_up", grid=(2, S // tm),
        in_specs=[pl.BlockSpec((tm, D_MODEL), lambda j, i: (i, 0)), wspec, wspec],
        out_specs=[ospec] * 3, out_shape=[_sds((S, D_FF), BF16)] * 3,
        compiler_params=_cp(("parallel", "parallel")),
    )(hn, w_gate_t, w_up_t)


def _ffn_down(act, w_down, h1, g, S):
    tm = min(512, S)

    def body(act_ref, wd_ref, h1_ref, g_ref, ff_ref, h2_ref):
        ff = jnp.dot(act_ref[...], wd_ref[...], preferred_element_type=F32)
        ff_ref[...] = ff.astype(BF16)
        h2_ref[...] = h1_ref[...] + _rms(ff, g_ref[...])

    return pl.pallas_call(
        body, name="ffn_down", grid=(S // tm,),
        in_specs=[_rows(tm, D_FF), _full(D_FF, D_MODEL), _rows(tm, D_MODEL), _full(1, D_MODEL)],
        out_specs=[_rows(tm, D_MODEL)] * 2, out_shape=[_sds((S, D_MODEL), BF16), _sds((S, D_MODEL), F32)],
        compiler_params=_cp(("parallel",)),
    )(act, w_down, h1, g)


def _ple_loss(p, h2, tgt, w_pp, w_pg, b_pg, g_ple, S):
    tm = min(512, S)

    def body(p_ref, h2_ref, t_ref, wp_ref, wg_ref, b_ref, gp_ref,
             dz_ref, dpe_ref, dh2_ref, h2b_ref, loss_ref, dgp_ref, db_ref):
        @pl.when(pl.program_id(0) == 0)
        def _():
            loss_ref[...] = jnp.zeros(loss_ref.shape, F32)
            dgp_ref[...] = jnp.zeros(dgp_ref.shape, F32)
            db_ref[...] = jnp.zeros(db_ref.shape, F32)

        gp = gp_ref[...]
        pe = _dot(p_ref[...], wp_ref[...])
        r = lax.rsqrt(jnp.mean(pe * pe, axis=-1, keepdims=True) + EPS)
        peh = pe * r
        e = peh * gp
        h2 = h2_ref[...]
        h2b = h2.astype(BF16)
        h2b_ref[...] = h2b
        gt = _sig(jnp.dot(h2b, wg_ref[...], preferred_element_type=F32) + b_ref[...])
        diff = h2 + e * gt - t_ref[...]
        loss_ref[...] += _colsum(diff * diff)
        dh3 = diff * (1.0 / D_MODEL)
        de = dh3 * gt
        dz = dh3 * e * gt * (1.0 - gt)
        db_ref[...] += _colsum(dz)
        dgp_ref[...] += _colsum(de * peh)
        dpeh = de * gp
        dpe = r * (dpeh - peh * jnp.mean(dpeh * peh, axis=-1, keepdims=True))
        dzb = dz.astype(BF16)
        dz_ref[...] = dzb
        dpe_ref[...] = dpe.astype(BF16)
        dh2_ref[...] = dh3 + _dot_nt(dzb, wg_ref[...])

    return pl.pallas_call(
        body, name="ple_loss", grid=(S // tm,),
        in_specs=[_rows(tm, PLE_DIM), _rows(tm, D_MODEL), _rows(tm, D_MODEL), _full(PLE_DIM, D_MODEL),
                  _full(D_MODEL, D_MODEL), _full(1, D_MODEL), _full(1, D_MODEL)],
        out_specs=[_rows(tm, D_MODEL)] * 4 + [_acc(1, D_MODEL)] * 3,
        out_shape=[_sds((S, D_MODEL), BF16), _sds((S, D_MODEL), BF16), _sds((S, D_MODEL), F32), _sds((S, D_MODEL), BF16)]
        + [_sds((1, D_MODEL), F32)] * 3,
        compiler_params=_cp(("arbitrary",)),
    )(p, h2, tgt, w_pp, w_pg, b_pg, g_ple)


def _wgrad(a, b, name, S):
    M = a.shape[1]
    N = b.shape[1]
    ts = min(2048, S)
    nsplit = 2 if M * N >= 2 * 1024 * 1024 else 1
    tn = N // nsplit

    def body(a_ref, b_ref, o_ref):
        @pl.when(pl.program_id(1) == 0)
        def _():
            o_ref[...] = jnp.zeros(o_ref.shape, F32)

        o_ref[...] += _dot_tn(a_ref[...], b_ref[...])

    return pl.pallas_call(
        body, name=name, grid=(nsplit, S // ts),
        in_specs=[pl.BlockSpec((ts, M), lambda j, s: (s, 0)), pl.BlockSpec((ts, tn), lambda j, s: (s, j))],
        out_specs=pl.BlockSpec((M, tn), lambda j, s: (0, j)), out_shape=_sds((M, N), F32),
        compiler_params=_cp(("parallel", "arbitrary")),
    )(a, b)


def _ffn_down_bwd(dh2, ff, g, w_down, dgate_f, dup_f, S):
    tm = min(512, S)
    tn = D_FF // 2

    def body(dh2_ref, ff_ref, g_ref, wd_ref, fg_ref, fu_ref, dff_ref, dgate_ref, dup_ref, dg_ref):
        @pl.when(pl.program_id(0) == 0)
        def _():
            dg_ref[...] = jnp.zeros(dg_ref.shape, F32)

        dff, ga = _rms_bwd(dh2_ref[...], ff_ref[...].astype(F32), g_ref[...])
        dg_ref[...] += _colsum(ga)
        dffb = dff.astype(BF16)
        dff_ref[...] = dffb
        for seg in range(2):
            sl = slice(seg * tn, (seg + 1) * tn)
            dact = _dot_nt(dffb, wd_ref[sl, :])
            dgate_ref[:, sl] = (dact * fg_ref[:, sl].astype(F32)).astype(BF16)
            dup_ref[:, sl] = (dact * fu_ref[:, sl].astype(F32)).astype(BF16)

    return pl.pallas_call(
        body, name="ffn_down_bwd", grid=(S // tm,),
        in_specs=[_rows(tm, D_MODEL), _rows(tm, D_MODEL), _full(1, D_MODEL), _full(D_FF, D_MODEL), _rows(tm, D_FF),
                  _rows(tm, D_FF)],
        out_specs=[_rows(tm, D_MODEL), _rows(tm, D_FF), _rows(tm, D_FF), _acc(1, D_MODEL)],
        out_shape=[_sds((S, D_MODEL), BF16), _sds((S, D_FF), BF16), _sds((S, D_FF), BF16), _sds((1, D_MODEL), F32)],
        compiler_params=_cp(("arbitrary",)),
    )(dh2, ff, g, w_down, dgate_f, dup_f)


def _ffn_up_bwd(dgate, dup, w_gate, w_up, h1, mix, dh2, g_pre, g_post, w_o, S, grads=()):
    tm = min(512, S)
    n = len(grads)
    last = S // tm - 1

    def body(dgate_ref, dup_ref, wg_ref, wu_ref, h1_ref, mix_ref, dh2_ref, g2_ref, g1_ref, wo_ref, *rest):
        g_ins = rest[:n]
        dh1_ref, dmix_ref, dro_ref, dmo_ref, dg2_ref, dg1_ref = rest[n:n + 6]
        g_outs, sems = rest[n + 6:2 * n + 6], rest[2 * n + 6:]

        @pl.when(pl.program_id(0) == 0)
        def _():
            dg2_ref[...] = jnp.zeros(dg2_ref.shape, F32)
            dg1_ref[...] = jnp.zeros(dg1_ref.shape, F32)
            for cp in (_swap_copies(g_ins, g_outs, sems) if n else []):
                cp.start()

        dhn = (jnp.dot(dgate_ref[...], wg_ref[...], preferred_element_type=F32)
               + jnp.dot(dup_ref[...], wu_ref[...], preferred_element_type=F32))
        d1, ga = _rms_bwd(dhn, h1_ref[...], g2_ref[...])
        dg2_ref[...] += _colsum(ga)
        dh1 = dh2_ref[...] + d1
        dh1_ref[...] = dh1
        dmix, gb = _rms_bwd(dh1, mix_ref[...].astype(F32), g1_ref[...])
        dg1_ref[...] += _colsum(gb)
        dmixb = dmix.astype(BF16)
        dmix_ref[...] = dmixb
        dcat = _dot_nt(dmixb, wo_ref[...])
        dro_ref[...] = dcat[:, 0:512].astype(BF16)
        dmo_ref[...] = dcat[:, 512:1024].astype(BF16)

        if n:
            @pl.when(pl.program_id(0) == last)
            def _():
                for cp in _swap_copies(g_ins, g_outs, sems):
                    cp.wait()

    dh1, dmix, dro, dmo, dg2, dg1, *got = pl.pallas_call(
        body, name="ffn_up_bwd", grid=(S // tm,),
        in_specs=[_rows(tm, D_FF), _rows(tm, D_FF), _full(D_FF, D_MODEL), _full(D_FF, D_MODEL), _rows(tm, D_MODEL),
                  _rows(tm, D_MODEL), _rows(tm, D_MODEL), _full(1, D_MODEL), _full(1, D_MODEL), _full(D_MODEL, D_MODEL)]
        + [_ANY] * n,
        out_specs=[_rows(tm, D_MODEL), _rows(tm, D_MODEL), _rows(tm, 512), _rows(tm, 512), _acc(1, D_MODEL),
                   _acc(1, D_MODEL)] + [_ANY] * n,
        out_shape=[_sds((S, D_MODEL), F32), _sds((S, D_MODEL), BF16), _sds((S, 512), BF16), _sds((S, 512), BF16),
                   _sds((1, D_MODEL), F32), _sds((1, D_MODEL), F32)] + _swap_out_shapes(grads),
        scratch_shapes=_swap_sems(n) if n else [],
        compiler_params=_cp(("arbitrary",)),
    )(dgate, dup, w_gate, w_up, h1, mix, dh2, g_pre, g_post, w_o, *grads)
    return dh1, dmix, dro, dmo, dg2, dg1, got


def _attn_delta(o, do, S, grads=()):
    tm = min(512, S)
    n = len(grads)
    last = S // tm - 1

    def body(o_ref, do_ref, *rest):
        g_ins, (dot_ref, d_ref), g_outs, sems = rest[:n], rest[n:n + 2], rest[n + 2:2 * n + 2], rest[2 * n + 2:]
        if n:
            @pl.when(pl.program_id(0) == 0)
            def _():
                for cp in _swap_copies(g_ins, g_outs, sems):
                    cp.start()

        do = do_ref[...].astype(F32)
        prod_t = (o_ref[...].astype(F32) * do).T
        dot_ref[...] = do.T.astype(BF16)
        for h in range(MLA_HEADS):
            d_ref[h // 2, (h % 2):(h % 2) + 1, :] = jnp.sum(prod_t[h * 64:(h + 1) * 64, :], axis=0, keepdims=True)

        if n:
            @pl.when(pl.program_id(0) == last)
            def _():
                for cp in _swap_copies(g_ins, g_outs, sems):
                    cp.wait()

    dot, delta, *got = pl.pallas_call(
        body, name="attn_delta", grid=(S // tm,),
        in_specs=[_rows(tm, 512), _rows(tm, 512)] + [_ANY] * n,
        out_specs=[pl.BlockSpec((512, tm), lambda i: (0, i)), pl.BlockSpec((MLA_HEADS // 2, 2, tm), lambda i: (0, 0, i))]
        + [_ANY] * n,
        out_shape=[_sds((512, S), BF16), _sds((MLA_HEADS // 2, 2, S), F32)] + _swap_out_shapes(grads),
        scratch_shapes=_swap_sems(n) if n else [],
        compiler_params=_cp(("arbitrary",)),
    )(o, do, *grads)
    return dot, delta, got


def _flash_bwd(qp, kp, kt, v, do, dot, lse, delta, S, sums=()):
    tq = min(512, S)
    nq = S // tq
    RB = ATT_ROWS
    NH = BWD_HEADS
    qb_of, kb_of, T = _tri_pairs(nq, k_major=True)
    n = len(sums)
    steps = (MLA_HEADS // NH) * T

    def body(qb_ref, kb_ref, q_ref, k_ref, kt_ref, v_ref, do_ref, dot_ref, lse_ref, dl_ref, *rest):
        g_ins, (dq_ref, dk_ref, dv_ref), g_outs = rest[:n], rest[n:n + 3], rest[n + 3:2 * n + 3]
        dk_sc, dv_sc, s_sc, dp_sc, p_sc, ds_sc = rest[2 * n + 3:2 * n + 9]
        sems = rest[2 * n + 9:]
        t = pl.program_id(1)
        qb = qb_ref[t]
        kb = kb_ref[t]
        lin = pl.program_id(0) * T + t

        if n:
            @pl.when(lin == 0)
            def _():
                for cp in _scatter_copies(g_ins, g_outs, sems):
                    cp.start()

        @pl.when(t == 0)
        def _():
            dq_ref[...] = jnp.zeros(dq_ref.shape, F32)

        @pl.when(qb == kb)
        def _():
            dk_sc[...] = jnp.zeros(dk_sc.shape, F32)
            dv_sc[...] = jnp.zeros(dv_sc.shape, F32)

        lane = lax.broadcasted_iota(jnp.int32, (tq, 64 * NH), 1)

        def step(masked):
            vv = v_ref[...]
            do_all = do_ref[...]
            mine = [(lane >= a * 64) & (lane < (a + 1) * 64) for a in range(NH)]
            for a in range(NH):
                sl = slice(a * 128, (a + 1) * 128)
                s_sc[a] = _dot_nt(k_ref[:, sl], q_ref[:, sl])
                dp_sc[a] = jnp.dot(jnp.where(mine[a], vv, jnp.zeros_like(vv)), dot_ref[...],
                                   preferred_element_type=F32)
            for a in range(NH):
                sl = slice(a * 128, (a + 1) * 128)
                lse = lse_ref[a:a + 1, :]
                dl = dl_ref[a:a + 1, :]
                for r in range(0, tq, RB):
                    sc = s_sc[a, r:r + RB, :]
                    if masked:
                        sc = jnp.where(_causal_keep(r, RB, tq), sc, NEG)
                    p = jnp.exp(sc - lse)
                    p_sc[a, r:r + RB, :] = p.astype(BF16)
                    ds_sc[a, r:r + RB, :] = (p * (dp_sc[a, r:r + RB, :] - dl)).astype(BF16)
                ds = ds_sc[a]
                dv_sc[...] += jnp.dot(p_sc[a], jnp.where(mine[a], do_all, jnp.zeros_like(do_all)),
                                      preferred_element_type=F32)
                dk_sc[:, sl] += jnp.dot(ds, q_ref[:, sl], preferred_element_type=F32)
                dq_ref[qb, sl, :] += jnp.dot(kt_ref[sl, :], ds, preferred_element_type=F32)

        @pl.when(qb > kb)
        def _():
            step(False)

        @pl.when(qb == kb)
        def _():
            step(True)

        @pl.when(qb == nq - 1)
        def _():
            dk_ref[...] = dk_sc[...].astype(BF16)
            dv_ref[...] = dv_sc[...].astype(BF16)

        if n:
            @pl.when(lin == steps - 1)
            def _():
                for cp in _scatter_copies(g_ins, g_outs, sems):
                    cp.wait()

    grid_spec = pltpu.PrefetchScalarGridSpec(
        num_scalar_prefetch=2, grid=(MLA_HEADS // NH, T),
        in_specs=[pl.BlockSpec((tq, 128 * NH), lambda j, t, qb, kb: (qb[t], j)),
                  pl.BlockSpec((tq, 128 * NH), lambda j, t, qb, kb: (kb[t], j)),
                  pl.BlockSpec((128 * NH, tq), lambda j, t, qb, kb: (j, kb[t])),
                  pl.BlockSpec((tq, 64 * NH), lambda j, t, qb, kb: (kb[t], j)),
                  pl.BlockSpec((tq, 64 * NH), lambda j, t, qb, kb: (qb[t], j)),
                  pl.BlockSpec((64 * NH, tq), lambda j, t, qb, kb: (j, qb[t])),
                  pl.BlockSpec((None, NH, tq), lambda j, t, qb, kb: (j, 0, qb[t])),
                  pl.BlockSpec((None, NH, tq), lambda j, t, qb, kb: (j, 0, qb[t]))] + [_ANY] * n,
        out_specs=[pl.BlockSpec((nq, 128 * NH, tq), lambda j, t, qb, kb: (0, j, 0), pipeline_mode=pl.Buffered(1)),
                   pl.BlockSpec((tq, 128 * NH), lambda j, t, qb, kb: (kb[t], j)),
                   pl.BlockSpec((tq, 64 * NH), lambda j, t, qb, kb: (kb[t], j))] + [_ANY] * n,
        scratch_shapes=[pltpu.VMEM((tq, 128 * NH), F32), pltpu.VMEM((tq, 64 * NH), F32), pltpu.VMEM((NH, tq, tq), F32),
                        pltpu.VMEM((NH, tq, tq), F32), pltpu.VMEM((NH, tq, tq), BF16), pltpu.VMEM((NH, tq, tq), BF16)]
        + (_scatter_sems(n) if n else []),
    )
    dq, dk, dv, *parts = pl.pallas_call(
        body, name="flash_bwd", grid_spec=grid_spec,
        out_shape=[_sds((nq, 1024, tq), F32), _sds((S, 1024), BF16), _sds((S, 512), BF16)] + _scatter_out_shapes(sums),
        compiler_params=_cp(("arbitrary", "arbitrary")),
    )(qb_of, kb_of, qp, kp, kt, v, do, dot, lse.reshape(MLA_HEADS // NH, NH, S), delta.reshape(MLA_HEADS // NH, NH, S),
      *sums)
    return dq, dk, dv, parts


def _mla_up_bwd(dqp, dkp, dv, cq, ckv, gq, gkv, w_uq, w_ukv, tabs, S):
    tm = min(512, S)

    def body(dq_ref, dk_ref, dv_ref, cq_ref, ckv_ref, gq_ref, gkv_ref, wuq_ref, wukv_ref, cm_ref, sa_ref, sb_ref,
             dqh_ref, dkv_ref, dcq_ref, dckv_ref, dkr_ref, dgq_ref, dgkv_ref):
        @pl.when(pl.program_id(0) == 0)
        def _():
            dgq_ref[...] = jnp.zeros(dgq_ref.shape, F32)
            dgkv_ref[...] = jnp.zeros(dgkv_ref.shape, F32)

        cm = cm_ref[...]
        sa = sa_ref[...]
        sb = sb_ref[...]
        lane = lax.broadcasted_iota(jnp.int32, (tm, 128), 1)
        dkr_r = jnp.zeros((tm, 128), F32)
        for h in range(MLA_HEADS):
            sl = slice(h * 128, (h + 1) * 128)
            dqh_ref[:, sl] = (_unrope_mla(dq_ref[sl, :].T, cm, sa, sb) * SCALE_MLA).astype(BF16)
            gk = dk_ref[:, sl]
            dkr_r = dkr_r + gk.astype(F32)
            dkv_ref[:, sl] = gk
        dkr_r = jnp.where((lane >= 64) & (lane < 96), dkr_r, 0.0)
        dkr_ref[...] = _unrope_mla(dkr_r, cm, sa, sb).astype(BF16)
        dkv_ref[:, 1024:1536] = dv_ref[...]
        dcq, ga = _rms_bwd(_dot_nt(dqh_ref[...], wuq_ref[...]), cq_ref[...], gq_ref[...])
        dcq_ref[...] = dcq.astype(BF16)
        dgq_ref[...] += _colsum(ga)
        dckv, gb = _rms_bwd(_dot_nt(dkv_ref[...], wukv_ref[...]), ckv_ref[...], gkv_ref[...])
        dckv_ref[...] = dckv.astype(BF16)
        dgkv_ref[...] += _colsum(gb)

    per_q = dqp.shape[2] // tm
    return pl.pallas_call(
        body, name="mla_up_bwd", grid=(S // tm,),
        in_specs=[pl.BlockSpec((None, 1024, tm), lambda i: (i // per_q, 0, i % per_q)),
                  _rows(tm, 1024), _rows(tm, 512), _rows(tm, Q_LORA), _rows(tm, KV_LORA),
                  _full(1, Q_LORA), _full(1, KV_LORA), _full(Q_LORA, 1024), _full(KV_LORA, 1536)] + [_rows(tm, 128)] * 3,
        out_specs=[_rows(tm, 1024), _rows(tm, 1536), _rows(tm, Q_LORA), _rows(tm, KV_LORA), _rows(tm, 128),
                   _acc(1, Q_LORA), _acc(1, KV_LORA)],
        out_shape=[_sds((S, 1024), BF16), _sds((S, 1536), BF16), _sds((S, Q_LORA), BF16), _sds((S, KV_LORA), BF16),
                   _sds((S, 128), BF16), _sds((1, Q_LORA), F32), _sds((1, KV_LORA), F32)],
        compiler_params=_cp(("arbitrary",)),
    )(dqp, dkp, dv, cq, ckv, gq, gkv, w_uq, w_ukv, *tabs[2:])


def _ret_bwd(rq, rk, rv, rprev, ry, rg, dro, gn_w, tabs, S):
    C = RET_CHUNK
    N = S // C
    G = min(RET_GROUP, N)
    NB = N // G

    def body(lg_ref, q_ref, k_ref, v_ref, rp_ref, ry_ref, rg_ref, dro_ref, w_ref, cr_ref, sr_ref,
             drq_ref, drk_ref, drv_ref, drg_ref, dw_ref, g_sc):
        @pl.when(pl.program_id(1) == 0)
        def _():
            g_sc[...] = jnp.zeros(g_sc.shape, F32)
            dw_ref[...] = jnp.zeros(dw_ref.shape, F32)

        dmat, zeta, xi, g_chunk = _decay_terms(lg_ref)
        w = w_ref[...]
        gacc = g_sc[...]
        dw = jnp.zeros((1, 128), F32)
        for i in reversed(range(G)):
            rows = slice(i * C, (i + 1) * C)
            ry = ry_ref[rows, :]
            mu = jnp.mean(ry, axis=-1, keepdims=True)
            yc = ry - mu
            rstd = lax.rsqrt(jnp.mean(yc * yc, axis=-1, keepdims=True) + EPS)
            yh = yc * rstd
            g = rg_ref[rows, :]
            s = _sig(g)
            dout = dro_ref[rows, :].astype(F32)
            drg_ref[rows, :] = (dout * (yh * w) * (s * (1.0 + g * (1.0 - s)))).astype(BF16)
            dgn = dout * (g * s)
            dw = dw + _colsum(dgn * yh)
            dyh = dgn * w
            dry = rstd * (dyh - jnp.mean(dyh, axis=-1, keepdims=True) - yh * jnp.mean(dyh * yh, axis=-1, keepdims=True))
            do = dry.astype(BF16)

            q = q_ref[rows, :]
            k = k_ref[rows, :]
            v = v_ref[rows, :]
            gfut = gacc.astype(BF16)
            sc = (_dot_nt(q, k) * dmat).astype(BF16)
            dsc = (_dot_nt(do, v) * dmat).astype(BF16)
            dq = jnp.dot(dsc, k, preferred_element_type=F32) + _dot_nt(do, rp_ref[i]) * xi
            dk = _dot_tn(dsc, q) + _dot_nt(v, gfut) * zeta
            dv = _dot_tn(sc, do) + jnp.dot(k, gfut, preferred_element_type=F32) * zeta
            gacc = g_chunk * gacc + _dot_tn(q, xi * dry)
            cr = cr_ref[rows, :]
            sr = sr_ref[rows, :]
            drq_ref[rows, :] = _unrope_ret(dq, cr, sr).astype(BF16)
            drk_ref[rows, :] = _unrope_ret(dk * SCALE_RET, cr, sr).astype(BF16)
            drv_ref[rows, :] = dv.astype(BF16)
        g_sc[...] = gacc
        dw_ref[...] += dw

    blk = pl.BlockSpec((G * C, 128), lambda h, n: (NB - 1 - n, h))
    tab = pl.BlockSpec((G * C, 128), lambda h, n: (NB - 1 - n, 0))
    return pl.pallas_call(
        body, name="ret_bwd", grid=(RET_HEADS, NB),
        in_specs=[pl.BlockSpec((None, 8, 128), lambda h, n: (h, 0, 0)), blk, blk, blk,
                  pl.BlockSpec((G, 128, 128), lambda h, n: (h * NB + NB - 1 - n, 0, 0)), blk, blk, blk,
                  pl.BlockSpec((1, 128), lambda h, n: (0, h)), tab, tab],
        out_specs=[blk, blk, blk, blk, pl.BlockSpec((1, 128), lambda h, n: (0, h))],
        out_shape=[_sds((S, 512), BF16)] * 4 + [_sds((1, 512), F32)],
        scratch_shapes=[pltpu.VMEM((128, 128), F32)],
        compiler_params=_cp(("parallel", "arbitrary")),
    )(_decay_table(), rq, rk, rv, rprev, ry, rg, dro, gn_w, tabs[0], tabs[1])


def _inproj_bwd(drq, drk, drv, drg, dcq, dckv, dkr, w_in, dh1, x, g, S):
    tm = min(512, S)

    def body(drq_ref, drk_ref, drv_ref, drg_ref, dcq_ref, dckv_ref, dkr_ref, w_ref, dh1_ref, x_ref, g_ref,
             gx_ref, dproj_ref, dg_ref):
        @pl.when(pl.program_id(0) == 0)
        def _():
            dg_ref[...] = jnp.zeros(dg_ref.shape, F32)

        dproj_ref[:, 0:512] = drq_ref[...]
        dproj_ref[:, 512:1024] = drk_ref[...]
        dproj_ref[:, 1024:1536] = drv_ref[...]
        dproj_ref[:, 1536:2048] = drg_ref[...]
        dproj_ref[:, 2048:2432] = dcq_ref[...]
        dproj_ref[:, 2432:2688] = dckv_ref[...]
        dproj_ref[:, 2688:2816] = dkr_ref[...]
        dx, ga = _rms_bwd(_dot_nt(dproj_ref[...], w_ref[...]), x_ref[...], g_ref[...])
        gx_ref[...] = dh1_ref[...] + dx
        dg_ref[...] += _colsum(ga)

    return pl.pallas_call(
        body, name="inproj_bwd", grid=(S // tm,),
        in_specs=[_rows(tm, 512)] * 4 + [_rows(tm, Q_LORA), _rows(tm, KV_LORA), _rows(tm, 128),
                                         _full(D_MODEL, IN_COLS_P), _rows(tm, D_MODEL), _rows(tm, D_MODEL),
                                         _full(1, D_MODEL)],
        out_specs=[_rows(tm, D_MODEL), _rows(tm, IN_COLS_P), _acc(1, D_MODEL)],
        out_shape=[_sds((S, D_MODEL), F32), _sds((S, IN_COLS_P), BF16), _sds((1, D_MODEL), F32)],
        compiler_params=_cp(("arbitrary",)),
    )(drq, drk, drv, drg, dcq, dckv, dkr, w_in, dh1, x, g)


def _pad_weights(w):
    w_in = w["w_in"]
    z = lambda r, c: jnp.zeros((r, c), BF16)
    w_in_p = jnp.concatenate([w_in[:, :2688], z(1024, 64), w_in[:, 2688:2720], z(1024, 32)], axis=1)
    w_uq_p = jnp.pad(w["w_uq"].reshape(Q_LORA, MLA_HEADS, 96), ((0, 0), (0, 0), (0, 32))).reshape(Q_LORA, 1024)
    ukv = w["w_ukv"].reshape(KV_LORA, MLA_HEADS, 128)
    k_part = jnp.pad(ukv[:, :, :64], ((0, 0), (0, 0), (0, 64))).reshape(KV_LORA, 1024)
    w_ukv_p = jnp.concatenate([k_part, ukv[:, :, 64:].reshape(KV_LORA, 512)], axis=1)
    return w_in_p, w_uq_p, w_ukv_p


BIG_SPEC = {n: (r, c, ax) for n, r, c, ax in BIG}
COLUMN_MAJOR = ("w_in", "w_uq", "w_gate", "w_up")
GRAD_TRANSPOSED = ("w_gate", "w_up")
GATHER_FIRST = ("w_in", "w_uq", "w_ukv")
GATHER_LATE = tuple(n for n, _, _, _ in BIG if n not in GATHER_FIRST)
REDUCE_EARLY = ("w_ple_gate", "w_ple_proj", "w_down", "w_gate", "w_up", "w_o")
REDUCE_LAST = tuple(n for n, _, _, _ in BIG if n not in REDUCE_EARLY)


def _local_step(x, p, pos_f, tgt, w, sm, late_shards=None, c_idx=None):
    S = x.shape[0]
    spread = late_shards is not None
    w = dict(w)
    tabs, first = _rope_tables(pos_f, S, [late_shards[n] for n in GATHER_FIRST] if spread else ())
    for i, n in enumerate(GATHER_FIRST if spread else ()):
        w[n] = _from_chips(first[i], BIG_SPEC[n][2])
    w_in_p, w_uq_p, w_ukv_p = _pad_weights(w)

    xn, rq, rk, rv, rg, cq, ckv, kr = _inproj(x, sm["pre_mix_norm"], w_in_p, tabs, S)
    cqn, ckvn, qp, kp, v, kt, vt = _mla_up(cq, ckv, kr, sm["mla_q_norm"], sm["mla_kv_norm"], w_uq_p, w_ukv_p, tabs, S)
    mo, lse, gathered = _flash_fwd(qp, kp, vt, S, [late_shards[n] for n in GATHER_LATE] if spread else ())
    for i, n in enumerate(GATHER_LATE if spread else ()):
        w[n] = _from_chips(gathered[i], 0 if n in GRAD_TRANSPOSED else BIG_SPEC[n][2])
    if not spread:
        w.update({n: w[n].T for n in GRAD_TRANSPOSED})
    ry, ro, rprev = _ret_fwd(rq, rk, rv, rg, sm["ret_gn_w"], S)
    mix, h1, hn = _outproj(ro, mo, x, w["w_o"], sm["post_mix_norm"], sm["pre_ffn_norm"], S)
    dgate_f, dup_f, act = _ffn_up(hn, w["w_gate"], w["w_up"], S)
    ff, h2 = _ffn_down(act, w["w_down"], h1, sm["post_ffn_norm"], S)
    dz, dpe, dh2, h2b, loss_vec, d_ple_norm, d_b = _ple_loss(
        p, h2, tgt, w["w_ple_proj"], w["w_ple_gate"], sm["b_ple_gate"], sm["ple_norm"], S)

    gw = {}
    gs = {"ple_norm": d_ple_norm, "b_ple_gate": d_b}
    gw["w_ple_gate"] = _wgrad(h2b, dz, "wgrad_ple_gate", S)
    gw["w_ple_proj"] = _wgrad(p, dpe, "wgrad_ple_proj", S)
    dff, dgate, dup, gs["post_ffn_norm"] = _ffn_down_bwd(dh2, ff, sm["post_ffn_norm"], w["w_down"], dgate_f, dup_f, S)
    gw["w_down"] = _wgrad(act, dff, "wgrad_down", S)
    if spread:
        gw["w_gate"] = _wgrad(dgate, hn, "wgrad_gate", S)
        gw["w_up"] = _wgrad(dup, hn, "wgrad_up", S)
    else:
        gw["w_gate"] = _wgrad(hn, dgate, "wgrad_gate", S)
        gw["w_up"] = _wgrad(hn, dup, "wgrad_up", S)
    first = REDUCE_EARLY[:-1]
    g4 = [_by_chip(gw.pop(n), *((D_FF, D_MODEL, 0) if n in GRAD_TRANSPOSED else BIG_SPEC[n]))
          for n in first] if spread else []
    dh1, dmix, dro, dmo, gs["pre_ffn_norm"], gs["post_mix_norm"], got = _ffn_up_bwd(
        dgate, dup, w["w_gate"], w["w_up"], h1, mix, dh2, sm["pre_ffn_norm"], sm["post_mix_norm"], w["w_o"], S, g4)
    gw["w_o"] = jnp.concatenate([_wgrad(ro, dmix, "wgrad_o_ret", S), _wgrad(mo, dmix, "wgrad_o_mla", S)], axis=0)
    g4_o = [_by_chip(gw.pop("w_o"), *BIG_SPEC["w_o"])] if spread else []

    dmo_t, delta, got_o = _attn_delta(mo, dmo, S, g4_o)
    sums = [_add_half_rows(a, b, c_idx, "rs_add_halves_" + n)
            for n, a, b in zip(REDUCE_EARLY, g4 + g4_o, list(got) + list(got_o))] if spread else []
    dqp, dkp, dv, parts = _flash_bwd(qp, kp, kt, v, dmo, dmo_t, lse, delta, S, sums)
    dqh, dkv, dcq, dckv, dkr, gs["mla_q_norm"], gs["mla_kv_norm"] = _mla_up_bwd(
        dqp, dkp, dv, cq, ckv, sm["mla_q_norm"], sm["mla_kv_norm"], w_uq_p, w_ukv_p, tabs, S)
    g_uq_p = _wgrad(cqn, dqh, "wgrad_uq", S)
    g_ukv_p = _wgrad(ckvn, dkv, "wgrad_ukv", S)
    gw["w_uq"] = g_uq_p.reshape(Q_LORA, MLA_HEADS, 128)[:, :, :96].reshape(Q_LORA, 768)
    gw["w_ukv"] = jnp.concatenate(
        [g_ukv_p[:, :1024].reshape(KV_LORA, MLA_HEADS, 128)[:, :, :64], g_ukv_p[:, 1024:].reshape(KV_LORA, MLA_HEADS, 64)],
        axis=2).reshape(KV_LORA, 1024)

    drq, drk, drv, drg, gs["ret_gn_w"] = _ret_bwd(rq, rk, rv, rprev, ry, rg, dro, sm["ret_gn_w"], tabs, S)
    grad_x, dproj, gs["pre_mix_norm"] = _inproj_bwd(drq, drk, drv, drg, dcq, dckv, dkr, w_in_p, dh1, x,
                                                    sm["pre_mix_norm"], S)
    g_in_p = _wgrad(xn, dproj, "wgrad_in", S)
    gw["w_in"] = jnp.concatenate([g_in_p[:, :2688], g_in_p[:, 2752:2784]], axis=1)
    return loss_vec, grad_x, gw, gs, ((sums, parts) if spread else None)


def _my_place():
    x = lax.axis_index("x")
    y = lax.axis_index("y")
    c = lax.axis_index("c")
    return x, y, c


def _other_chips(x, y):
    return [(1 - x, y), (x, 1 - y), (1 - x, 1 - y)]


_ANY = pl.BlockSpec(memory_space=pl.ANY)


def _small_copies(v_ref, slots, sems):
    send, recv, lsem = sems
    x, y, c = _my_place()
    me = 4 * x + 2 * y + c
    cps = [pltpu.make_async_copy(v_ref, slots.at[me], lsem)]
    for r in range(1, N_DEV):
        peer = (x ^ (r >> 2), y ^ ((r >> 1) & 1), c ^ (r & 1))
        cps.append(pltpu.make_async_remote_copy(
            src_ref=v_ref, dst_ref=slots.at[me], send_sem=send.at[r - 1], recv_sem=recv.at[r - 1],
            device_id=peer, device_id_type=MESH))
    return cps


def _small_sum(slots, out_ref):
    acc = slots[0]
    for d in range(1, N_DEV):
        acc = acc + slots[d]
    out_ref[...] = acc
    loss = jnp.sum(acc[9:10, :], axis=1, keepdims=True) * (0.5 / D_MODEL)
    out_ref[9:10, :] = jnp.broadcast_to(loss, (1, PACK_COLS))


def _small_scratch():
    return [pltpu.VMEM((N_DEV, SMALL_ROWS, PACK_COLS), F32), pltpu.SemaphoreType.DMA((N_DEV - 1,)),
            pltpu.SemaphoreType.DMA((N_DEV - 1,)), pltpu.SemaphoreType.DMA]


N_BIG = len(BIG)


def _half(c, rows, align):
    h = rows // 2
    return pl.ds(pl.multiple_of(c * h, align), h)


def _gather_out_shapes(shards):
    return [_sds((N_CHIPS,) + tuple(s.shape), BF16) for s in shards]


def _gather_sems(n):
    return [pltpu.SemaphoreType.DMA((n, 3))] * 4 + [pltpu.SemaphoreType.DMA((n,))] * 2


def _gather_phase(phase, ins, outs, sems):
    send1, recv1, send2, recv2, send3, recv3 = sems
    x, y, c = _my_place()
    me = 2 * x + y
    chips = _other_chips(x, y)
    sib = (x, y, 1 - c)
    for t in range(len(ins)):
        rows = ins[t].shape[0]
        half = _half(c, rows, 16)
        other = _half(1 - c, rows, 16)
        def own():
            return pltpu.make_async_remote_copy(
                src_ref=ins[t], dst_ref=outs[t].at[me], send_sem=send3.at[t], recv_sem=recv3.at[t],
                device_id=sib, device_id_type=MESH)

        if phase == 0:
            own().start()
        if phase == 2:
            own().wait()
        for k, (cx, cy) in enumerate(chips):
            src = 2 * cx + cy

            def over_ici(slab):
                return pltpu.make_async_remote_copy(
                    src_ref=ins[t].at[half], dst_ref=outs[t].at[slab, half], send_sem=send1.at[t, k],
                    recv_sem=recv1.at[t, k], device_id=(cx, cy, c), device_id_type=MESH)

            def over_d2d(rows):
                return pltpu.make_async_remote_copy(
                    src_ref=outs[t].at[src, rows], dst_ref=outs[t].at[src, rows], send_sem=send2.at[t, k],
                    recv_sem=recv2.at[t, k], device_id=sib, device_id_type=MESH)

            if phase == 0:
                over_ici(me).start()
            if phase == 1:
                over_ici(src).wait_recv()
                over_d2d(half).start()
            if phase == 2:
                over_d2d(other).wait_recv()
                over_ici(me).wait_send()
                over_d2d(half).wait_send()


def _swap_copies(ins, outs, sems):
    send, recv = sems
    x, y, c = _my_place()
    return [pltpu.make_async_remote_copy(
        src_ref=ins[t].at[:, _half(1 - c, ins[t].shape[1], 8)], dst_ref=outs[t], send_sem=send.at[t],
        recv_sem=recv.at[t], device_id=(x, y, 1 - c), device_id_type=MESH) for t in range(len(ins))]


def _swap_out_shapes(gs):
    return [_sds((N_CHIPS, g.shape[1] // 2, g.shape[2]), F32) for g in gs]


def _swap_sems(n):
    return [pltpu.SemaphoreType.DMA((n,)), pltpu.SemaphoreType.DMA((n,))]


def _swap_half_rows(gs):
    n = len(gs)

    def body(*refs):
        cps = _swap_copies(refs[:n], refs[n:2 * n], refs[2 * n:])
        for cp in cps:
            cp.start()
        for cp in cps:
            cp.wait()

    return pl.pallas_call(
        body, name="rs_swap_halves",
        in_specs=[_ANY] * n, out_specs=[_ANY] * n, out_shape=_swap_out_shapes(gs), scratch_shapes=_swap_sems(n),
    )(*gs)


def _add_half_rows(g, got, c_idx, name):
    _, rows, cols = g.shape
    h = rows // 2

    def body(c_ref, a_ref, b_ref, o_ref):
        o_ref[...] = (a_ref[...] + b_ref[...]).astype(BF16)

    grid_spec = pltpu.PrefetchScalarGridSpec(
        num_scalar_prefetch=1, grid=(N_CHIPS,),
        in_specs=[pl.BlockSpec((None, h, cols), lambda j, c: (j, c[0], 0)),
                  pl.BlockSpec((None, h, cols), lambda j, c: (j, 0, 0))],
        out_specs=pl.BlockSpec((None, h, cols), lambda j, c: (j, 0, 0)),
    )
    return pl.pallas_call(
        body, name=name, grid_spec=grid_spec, out_shape=_sds((N_CHIPS, h, cols), BF16),
        compiler_params=_cp(("parallel",)),
    )(c_idx, g, got)


def _scatter_to_chips(ts, vec):
    n = len(ts)

    def body(*refs):
        ins, v_ref, outs, small_ref = refs[:n], refs[n], refs[n + 1:2 * n + 1], refs[2 * n + 1]
        slots, small_sems, sems = refs[2 * n + 2], refs[2 * n + 3:2 * n + 6], refs[2 * n + 6:]
        small = _small_copies(v_ref, slots, small_sems)
        cps = _scatter_copies(ins, outs, sems)
        for cp in small + cps:
            cp.start()
        for cp in small:
            cp.wait()
        _small_sum(slots, small_ref)
        for cp in cps:
            cp.wait()

    vm = pl.BlockSpec(memory_space=pltpu.VMEM)
    *parts, small_sum = pl.pallas_call(
        body, name="rs_scatter_chips",
        in_specs=[_ANY] * n + [vm], out_specs=[_ANY] * n + [vm],
        out_shape=_scatter_out_shapes(ts) + [_sds((SMALL_ROWS, PACK_COLS), F32)],
        scratch_shapes=_small_scratch() + _scatter_sems(n),
    )(*ts, vec)
    return parts, small_sum


def _scatter_copies(ins, outs, sems):
    send, recv = sems
    x, y, c = _my_place()
    return [pltpu.make_async_remote_copy(
        src_ref=ins[t].at[2 * cx + cy], dst_ref=outs[t].at[k], send_sem=send.at[t, k], recv_sem=recv.at[t, k],
        device_id=(cx, cy, c), device_id_type=MESH)
        for t in range(len(ins)) for k, (cx, cy) in enumerate(_other_chips(x, y))]


def _scatter_out_shapes(ts):
    return [_sds((3,) + tuple(t.shape[1:]), BF16) for t in ts]


def _scatter_sems(n):
    return [pltpu.SemaphoreType.DMA((n, 3)), pltpu.SemaphoreType.DMA((n, 3))]


def _add_four(mine, parts, place, name):
    _, h, cols = parts.shape

    def body(pl_ref, m_ref, p_ref, o_ref):
        o_ref[...] = ((m_ref[...].astype(F32) + p_ref[0].astype(F32)) + p_ref[1].astype(F32)) + p_ref[2].astype(F32)

    grid_spec = pltpu.PrefetchScalarGridSpec(
        num_scalar_prefetch=1, grid=(1,),
        in_specs=[pl.BlockSpec((None, h, cols), lambda i, pc: (pc[0], 0, 0)),
                  pl.BlockSpec((3, h, cols), lambda i, pc: (0, 0, 0))],
        out_specs=pl.BlockSpec((h, cols), lambda i, pc: (pc[1], 0)),
    )
    return pl.pallas_call(
        body, name=name, grid_spec=grid_spec, out_shape=_sds((2 * h, cols), F32),
        compiler_params=_cp(("arbitrary",)),
    )(place, mine, parts)


def _join_half_rows(rs):
    n = len(rs)

    def body(*refs):
        ins, outs = refs[:n], refs[n:2 * n]
        send, recv = refs[2 * n:]
        x, y, c = _my_place()
        cps = []
        for t in range(n):
            half = _half(c, outs[t].shape[0], 8)
            rc = pltpu.make_async_remote_copy(
                src_ref=ins[t].at[half], dst_ref=outs[t].at[half], send_sem=send.at[t], recv_sem=recv.at[t],
                device_id=(x, y, 1 - c), device_id_type=MESH)
            rc.start()
            cps.append(rc)
        for cp in cps:
            cp.wait()

    return pl.pallas_call(
        body, name="rs_join_halves",
        in_specs=[_ANY] * n, out_specs=[_ANY] * n,
        out_shape=[_sds(r.shape, F32) for r in rs],
        input_output_aliases={i: i for i in range(n)},
        scratch_shapes=[pltpu.SemaphoreType.DMA((n,))] * 2,
    )(*rs)


def _by_chip(full, rows, cols, axis):
    if axis == 0:
        return full.reshape(N_CHIPS, rows // N_CHIPS, cols)
    return full.reshape(rows, N_CHIPS, cols // N_CHIPS).transpose(1, 0, 2)


def _from_chips(parts, axis):
    _, r, c = parts.shape
    if axis == 0:
        return parts.reshape(N_CHIPS * r, c)
    return parts.transpose(1, 0, 2).reshape(r, N_CHIPS * c)


def _adamw(wt, g, m, v, name):
    _, R, C = wt.shape
    tr = max(d for d in range(8, R + 1, 8) if R % d == 0 and (d * C <= 256 * 1024 or d == 8))

    def body(w_ref, g_ref, m_ref, v_ref, d_ref, nm_ref, nv_ref):
        gg = g_ref[...]
        m_new = ADAM_B1 * m_ref[...] + (1.0 - ADAM_B1) * gg
        v_new = ADAM_B2 * v_ref[...] + (1.0 - ADAM_B2) * (gg * gg)
        m_hat = m_new / (1.0 - ADAM_B1 ** ADAM_STEP)
        v_hat = v_new / (1.0 - ADAM_B2 ** ADAM_STEP)
        d_ref[...] = -ADAM_LR * (m_hat / (jnp.sqrt(v_hat) + ADAM_EPS) + ADAM_WD * w_ref[...])
        nm_ref[...] = m_new
        nv_ref[...] = v_new

    spec = pl.BlockSpec((None, tr, C), lambda i: (0, i, 0))
    return pl.pallas_call(
        body, name=name, grid=(R // tr,), in_specs=[spec, pl.BlockSpec((tr, C), lambda i: (i, 0)), spec, spec],
        out_specs=[spec] * 3, out_shape=[_sds((1, R, C), F32)] * 3,
        compiler_params=_cp(("parallel",)),
    )(wt, g, m, v)


def _pack_small(vals, loss_vec=None):
    rows = [jnp.pad(vals[n].reshape(-1), (0, PACK_COLS - sz)) for n, sz in SMALL]
    rows.append(loss_vec.reshape(-1) if loss_vec is not None else jnp.zeros((PACK_COLS,), F32))
    rows += [jnp.zeros((PACK_COLS,), F32)] * (SMALL_ROWS - len(rows))
    return jnp.stack(rows)


def kernel(x, p, positions, pre_mix_norm, w_in, ret_gn_w, mla_q_norm, w_uq, mla_kv_norm, w_ukv, w_o, post_mix_norm, pre_ffn_norm, w_gate, w_up, w_down, post_ffn_norm, w_ple_proj, ple_norm, w_ple_gate, b_ple_gate, loss_target, m_pre_mix_norm, m_w_in, m_ret_gn_w, m_mla_q_norm, m_w_uq, m_mla_kv_norm, m_w_ukv, m_w_o, m_post_mix_norm, m_pre_ffn_norm, m_w_gate, m_w_up, m_w_down, m_post_ffn_norm, m_w_ple_proj, m_ple_norm, m_w_ple_gate, m_b_ple_gate, v_pre_mix_norm, v_w_in, v_ret_gn_w, v_mla_q_norm, v_w_uq, v_mla_kv_norm, v_w_ukv, v_w_o, v_post_mix_norm, v_pre_ffn_norm, v_w_gate, v_w_up, v_w_down, v_post_ffn_norm, v_w_ple_proj, v_ple_norm, v_w_ple_gate, v_b_ple_gate):
    wts = dict(pre_mix_norm=pre_mix_norm, w_in=w_in, ret_gn_w=ret_gn_w, mla_q_norm=mla_q_norm, w_uq=w_uq,
               mla_kv_norm=mla_kv_norm, w_ukv=w_ukv, w_o=w_o, post_mix_norm=post_mix_norm, pre_ffn_norm=pre_ffn_norm,
               w_gate=w_gate, w_up=w_up, w_down=w_down, post_ffn_norm=post_ffn_norm, w_ple_proj=w_ple_proj,
               ple_norm=ple_norm, w_ple_gate=w_ple_gate, b_ple_gate=b_ple_gate)
    mom = dict(pre_mix_norm=m_pre_mix_norm, w_in=m_w_in, ret_gn_w=m_ret_gn_w, mla_q_norm=m_mla_q_norm, w_uq=m_w_uq,
               mla_kv_norm=m_mla_kv_norm, w_ukv=m_w_ukv, w_o=m_w_o, post_mix_norm=m_post_mix_norm,
               pre_ffn_norm=m_pre_ffn_norm, w_gate=m_w_gate, w_up=m_w_up, w_down=m_w_down, post_ffn_norm=m_post_ffn_norm,
               w_ple_proj=m_w_ple_proj, ple_norm=m_ple_norm, w_ple_gate=m_w_ple_gate, b_ple_gate=m_b_ple_gate)
    var = dict(pre_mix_norm=v_pre_mix_norm, w_in=v_w_in, ret_gn_w=v_ret_gn_w, mla_q_norm=v_mla_q_norm, w_uq=v_w_uq,
               mla_kv_norm=v_mla_kv_norm, w_ukv=v_w_ukv, w_o=v_w_o, post_mix_norm=v_post_mix_norm,
               pre_ffn_norm=v_pre_ffn_norm, w_gate=v_w_gate, w_up=v_w_up, w_down=v_w_down, post_ffn_norm=v_post_ffn_norm,
               w_ple_proj=v_w_ple_proj, ple_norm=v_ple_norm, w_ple_gate=v_w_ple_gate, b_ple_gate=v_b_ple_gate)

    S = x.shape[1]
    shard2d = {n: wts[n][0] for n, _, _, _ in BIG}
    small2d = {n: wts[n] for n, _ in SMALL}

    shard_bf = {n: (jnp.swapaxes(wts[n], 1, 2)[0] if n in GRAD_TRANSPOSED else shard2d[n]).astype(BF16) for n in shard2d}
    pos_f = positions.astype(F32).reshape(S, 1)
    c_idx = lax.axis_index("c").astype(jnp.int32).reshape(1)
    loss_vec, grad_x, gw, gs, (sums_early, parts_early) = _local_step(
        x[0], p[0, 0], pos_f, loss_target[0], {}, small2d, shard_bf, c_idx)

    g4 = [_by_chip(gw[n], *BIG_SPEC[n]) for n in REDUCE_LAST]
    got = _swap_half_rows(g4)
    sums_last = [_add_half_rows(g4[i], got[i], c_idx, "rs_add_halves_" + n) for i, n in enumerate(REDUCE_LAST)]
    parts_last, small_sum = _scatter_to_chips(sums_last, _pack_small(gs, loss_vec))
    place = jnp.stack([2 * lax.axis_index("x") + lax.axis_index("y"), lax.axis_index("c")]).astype(jnp.int32)
    names = REDUCE_EARLY + REDUCE_LAST
    reduced = _join_half_rows(
        [_add_four(sm_, pt_, place, "rs_add_chips_" + n)
         for n, sm_, pt_ in zip(names, sums_early + sums_last, list(parts_early) + list(parts_last))])
    g_shard = dict(zip(names, reduced))

    loss = small_sum[9, 0]
    g_small = {n: small_sum[i:i + 1, :sz] for i, (n, sz) in enumerate(SMALL)}

    grads, delta, new_m, new_v = {}, {}, {}, {}
    for n, _, _, _ in BIG:
        if n in COLUMN_MAJOR:
            turn = lambda a: jnp.swapaxes(a, 1, 2)
            g_t = g_shard[n] if n in GRAD_TRANSPOSED else g_shard[n].T
            d, nm, nv = _adamw(turn(wts[n]), g_t, turn(mom[n]), turn(var[n]), "adamw_" + n)
            grads[n], delta[n], new_m[n], new_v[n] = turn(g_t[None]), turn(d), turn(nm), turn(nv)
        else:
            delta[n], new_m[n], new_v[n] = _adamw(wts[n], g_shard[n], mom[n], var[n], "adamw_" + n)
            grads[n] = g_shard[n][None]
    d, nm, nv = _adamw(_pack_small(small2d)[None], small_sum, _pack_small(mom)[None], _pack_small(var)[None],
                       "adamw_small")
    for i, (n, sz) in enumerate(SMALL):
        grads[n] = g_small[n]
        delta[n], new_m[n], new_v[n] = d[0, i:i + 1, :sz], nm[0, i:i + 1, :sz], nv[0, i:i + 1, :sz]

    return (loss, grad_x[None], *[grads[n] for n in ALL_W], *[delta[n] for n in ALL_W],
            *[new_m[n] for n in ALL_W], *[new_v[n] for n in ALL_W])
```

```python
import functools
import math

import jax
import jax.numpy as jnp
import numpy as np
from jax import lax
from jax.experimental import pallas as pl
from jax.experimental.pallas import tpu as pltpu

F32 = jnp.float32
BF16 = jnp.bfloat16
MESH = pl.DeviceIdType.MESH

D_MODEL = 1024
D_FF = 2816
PLE_DIM = 256
RET_HEADS = 4
RET_DIM = 128
RET_WIDTH = 512
RET_CHUNK = 256
RET_GROUP = 4
MLA_HEADS = 8
MLA_NOPE = 64
MLA_ROPE = 32
MLA_V = 64
Q_LORA = 384
KV_LORA = 256
IN_COLS = 2720
IN_COLS_P = 2816
ROPE_BASE = 10000.0
EPS = 1e-6
SCALE_MLA = 1.0 / math.sqrt(MLA_NOPE + MLA_ROPE)
SCALE_RET = RET_DIM ** -0.5
NEG = -1e30

ADAM_LR = 0.001
ADAM_B1 = 0.9
ADAM_B2 = 0.999
ADAM_EPS = 1e-08
ADAM_WD = 0.01
ADAM_STEP = 10

N_CHIPS = 4
N_DEV = 8
VMEM_MB = 56

BIG = (
    ("w_in", 1024, 2720, 1),
    ("w_uq", 384, 768, 1),
    ("w_ukv", 256, 1024, 1),
    ("w_o", 1024, 1024, 0),
    ("w_gate", 1024, 2816, 1),
    ("w_up", 1024, 2816, 1),
    ("w_down", 2816, 1024, 0),
    ("w_ple_proj", 256, 1024, 1),
    ("w_ple_gate", 1024, 1024, 0),
)
SMALL = (
    ("pre_mix_norm", 1024),
    ("ret_gn_w", 512),
    ("mla_q_norm", 384),
    ("mla_kv_norm", 256),
    ("post_mix_norm", 1024),
    ("pre_ffn_norm", 1024),
    ("post_ffn_norm", 1024),
    ("ple_norm", 1024),
    ("b_ple_gate", 1024),
)
ALL_W = ("pre_mix_norm", "w_in", "ret_gn_w", "mla_q_norm", "w_uq", "mla_kv_norm", "w_ukv", "w_o", "post_mix_norm",
         "pre_ffn_norm", "w_gate", "w_up", "w_down", "post_ffn_norm", "w_ple_proj", "ple_norm", "w_ple_gate", "b_ple_gate")
PACK_COLS = 1024
SMALL_ROWS = 16


def _cp(sem=None, mb=VMEM_MB, **kw):
    return pltpu.CompilerParams(dimension_semantics=sem, vmem_limit_bytes=mb * 1024 * 1024, **kw)


def _bf(x):
    return x.astype(BF16)


def _dot(a, b):
    return jnp.dot(_bf(a), _bf(b), preferred_element_type=F32)


def _dot_nt(a, b):
    return lax.dot_general(_bf(a), _bf(b), (((1,), (1,)), ((), ())), preferred_element_type=F32)


def _dot_tn(a, b):
    return lax.dot_general(_bf(a), _bf(b), (((0,), (0,)), ((), ())), preferred_element_type=F32)


def _sig(x):
    return 1.0 / (1.0 + jnp.exp(-x))


def _rms(x, g):
    r = lax.rsqrt(jnp.mean(x * x, axis=-1, keepdims=True) + EPS)
    return x * r * g


def _rms_bwd(dy, x, g):
    r = lax.rsqrt(jnp.mean(x * x, axis=-1, keepdims=True) + EPS)
    xh = x * r
    dxh = dy * g
    dx = r * (dxh - xh * jnp.mean(dxh * xh, axis=-1, keepdims=True))
    return dx, dy * xh


def _colsum(x):
    return jnp.sum(x, axis=0, keepdims=True)


def _rope_ret(x, cr, sr):
    return x * cr + pltpu.roll(x, 64, 1) * sr


def _unrope_ret(dy, cr, sr):
    return dy * cr + pltpu.roll(dy * sr, 64, 1)


def _rope_mla(x, cm, sa, sb):
    return x * cm + pltpu.roll(x, 112, 1) * sa + pltpu.roll(x, 16, 1) * sb


def _unrope_mla(dy, cm, sa, sb):
    return dy * cm + pltpu.roll(dy * sa, 16, 1) + pltpu.roll(dy * sb, 112, 1)


def _rows(tm, w, col=0):
    return pl.BlockSpec((tm, w), lambda i: (i, col))


def _full(*shape):
    return pl.BlockSpec(shape, lambda i: (0,) * len(shape), pipeline_mode=pl.Buffered(1))


def _acc(*shape):
    return pl.BlockSpec(shape, lambda i: (0,) * len(shape))


def _sds(shape, dtype):
    return jax.ShapeDtypeStruct(shape, dtype)


def _rope_tables(pos_f, S, shards=()):
    tm = min(512, S)
    n = len(shards)
    steps = S // tm
    inv_r = (1.0 / (np.float32(ROPE_BASE) ** (np.arange(64, dtype=np.float32) / np.float32(64)))).astype(np.float32)
    inv_m16 = (1.0 / (np.float32(ROPE_BASE) ** (np.arange(16, dtype=np.float32) / np.float32(16)))).astype(np.float32)
    inv_r = np.concatenate([inv_r, inv_r])[None, :]
    inv_m = np.zeros((1, 128), np.float32)
    inv_m[0, 64:80] = inv_m16
    inv_m[0, 80:96] = inv_m16

    def body(pos_ref, invr_ref, invm_ref, *rest):
        w_ins, (cr_ref, sr_ref, cm_ref, sa_ref, sb_ref) = rest[:n], rest[n:n + 5]
        w_outs, sems = rest[n + 5:2 * n + 5], rest[2 * n + 5:]
        i = pl.program_id(0)
        if n:
            @pl.when(i == 0)
            def _():
                _gather_phase(0, w_ins, w_outs, sems)

            @pl.when(i == steps - 1)
            def _():
                _gather_phase(1, w_ins, w_outs, sems)

        pos = pos_ref[...]
        lane = lax.broadcasted_iota(jnp.int32, (tm, 128), 1)
        ar = pos * invr_ref[...]
        s = jnp.sin(ar)
        cr_ref[...] = jnp.cos(ar)
        sr_ref[...] = jnp.where(lane < 64, -s, s)
        am = pos * invm_ref[...]
        c2 = jnp.cos(am)
        s2 = jnp.sin(am)
        cm_ref[...] = jnp.where(lane < 64, 1.0, jnp.where(lane < 96, c2, 0.0))
        sa_ref[...] = jnp.where((lane >= 64) & (lane < 80), -s2, 0.0)
        sb_ref[...] = jnp.where((lane >= 80) & (lane < 96), s2, 0.0)

        if n:
            @pl.when(i == steps - 1)
            def _():
                _gather_phase(2, w_ins, w_outs, sems)

    outs = pl.pallas_call(
        body, name="rope_tables", grid=(steps,),
        in_specs=[_rows(tm, 1), _full(1, 128), _full(1, 128)] + [_ANY] * n,
        out_specs=[_rows(tm, 128)] * 5 + [_ANY] * n,
        out_shape=[_sds((S, 128), F32)] * 5 + _gather_out_shapes(shards),
        scratch_shapes=_gather_sems(n) if n else [],
        compiler_params=_cp(("arbitrary",)),
    )(pos_f, jnp.asarray(inv_r), jnp.asarray(inv_m), *shards)
    return outs[:5], outs[5:]


def _inproj(x, g, w_in, tabs, S):
    tm = min(512, S)

    def body(x_ref, g_ref, w_ref, cr_ref, sr_ref, cm_ref, sa_ref, sb_ref,
             xn_ref, rq_ref, rk_ref, rv_ref, rg_ref, cq_ref, ckv_ref, kr_ref):
        xb = _rms(x_ref[...], g_ref[...]).astype(BF16)
        xn_ref[...] = xb
        cr = cr_ref[...]
        sr = sr_ref[...]
        q = jnp.dot(xb, w_ref[:, 0:512], preferred_element_type=F32)
        k = jnp.dot(xb, w_ref[:, 512:1024], preferred_element_type=F32)
        for h in range(RET_HEADS):
            sl = slice(h * 128, (h + 1) * 128)
            rq_ref[:, sl] = _rope_ret(q[:, sl], cr, sr).astype(BF16)
            rk_ref[:, sl] = (_rope_ret(k[:, sl], cr, sr) * SCALE_RET).astype(BF16)
        rv_ref[...] = jnp.dot(xb, w_ref[:, 1024:1536], preferred_element_type=F32).astype(BF16)
        rg_ref[...] = jnp.dot(xb, w_ref[:, 1536:2048], preferred_element_type=F32)
        cq_ref[...] = jnp.dot(xb, w_ref[:, 2048:2432], preferred_element_type=F32)
        ckv_ref[...] = jnp.dot(xb, w_ref[:, 2432:2688], preferred_element_type=F32)
        kr = jnp.dot(xb, w_ref[:, 2688:2816], preferred_element_type=F32)
        kr_ref[...] = _rope_mla(kr, cm_ref[...], sa_ref[...], sb_ref[...])

    return pl.pallas_call(
        body, name="inproj", grid=(S // tm,),
        in_specs=[_rows(tm, D_MODEL), _full(1, D_MODEL), _full(D_MODEL, IN_COLS_P)] + [_rows(tm, 128)] * 5,
        out_specs=[_rows(tm, D_MODEL)] + [_rows(tm, 512)] * 4 + [_rows(tm, Q_LORA), _rows(tm, KV_LORA), _rows(tm, 128)],
        out_shape=[_sds((S, D_MODEL), BF16)] + [_sds((S, 512), BF16)] * 3
        + [_sds((S, 512), F32), _sds((S, Q_LORA), F32), _sds((S, KV_LORA), F32), _sds((S, 128), F32)],
        compiler_params=_cp(("parallel",)),
    )(x, g, w_in, *tabs)


def _mla_up(cq, ckv, kr, gq, gkv, w_uq, w_ukv, tabs, S):
    tm = min(512, S)

    def body(cq_ref, ckv_ref, kr_ref, gq_ref, gkv_ref, wuq_ref, wukv_ref, cm_ref, sa_ref, sb_ref,
             cqn_ref, ckvn_ref, qp_ref, kp_ref, v_ref, kt_ref, vt_ref):
        cm = cm_ref[...]
        sa = sa_ref[...]
        sb = sb_ref[...]
        cqn = _rms(cq_ref[...], gq_ref[...]).astype(BF16)
        cqn_ref[...] = cqn
        ckvn = _rms(ckv_ref[...], gkv_ref[...]).astype(BF16)
        ckvn_ref[...] = ckvn
        qh = jnp.dot(cqn, wuq_ref[...], preferred_element_type=F32)
        kv = jnp.dot(ckvn, wukv_ref[...], preferred_element_type=F32)
        kr_blk = kr_ref[...]
        for h in range(MLA_HEADS):
            sl = slice(h * 128, (h + 1) * 128)
            qp_ref[:, sl] = (_rope_mla(qh[:, sl], cm, sa, sb) * SCALE_MLA).astype(BF16)
            kh = kv[:, sl] + kr_blk
            kp_ref[:, sl] = kh.astype(BF16)
            kt_ref[sl, :] = kh.T.astype(BF16)
        for h in range(MLA_HEADS // 2):
            vh = kv[:, 1024 + h * 128:1024 + (h + 1) * 128]
            v_ref[:, h * 128:(h + 1) * 128] = vh.astype(BF16)
            vt_ref[h * 128:(h + 1) * 128, :] = vh.T.astype(BF16)

    cols = lambda r: pl.BlockSpec((r, tm), lambda i: (0, i))
    return pl.pallas_call(
        body, name="mla_up", grid=(S // tm,),
        in_specs=[_rows(tm, Q_LORA), _rows(tm, KV_LORA), _rows(tm, 128), _full(1, Q_LORA), _full(1, KV_LORA),
                  _full(Q_LORA, 1024), _full(KV_LORA, 1536)] + [_rows(tm, 128)] * 3,
        out_specs=[_rows(tm, Q_LORA), _rows(tm, KV_LORA), _rows(tm, 1024), _rows(tm, 1024), _rows(tm, 512),
                   cols(1024), cols(512)],
        out_shape=[_sds((S, Q_LORA), BF16), _sds((S, KV_LORA), BF16), _sds((S, 1024), BF16), _sds((S, 1024), BF16),
                   _sds((S, 512), BF16), _sds((1024, S), BF16), _sds((512, S), BF16)],
        compiler_params=_cp(("parallel",)),
    )(cq, ckv, kr, gq, gkv, w_uq, w_ukv, *tabs[2:])


def _tri_pairs(nq, k_major):
    if k_major:
        pairs = [(qb, kb) for kb in range(nq) for qb in range(kb, nq)]
    else:
        pairs = [(qb, kb) for qb in range(nq) for kb in range(qb + 1)]
    qb_of = np.array([p[0] for p in pairs], np.int32)
    kb_of = np.array([p[1] for p in pairs], np.int32)
    return jnp.asarray(qb_of), jnp.asarray(kb_of), len(pairs)


ATT_ROWS = 32
FWD_HEADS = 8
BWD_HEADS = 4


def _causal_keep(r0, rows, tq):
    key = r0 + lax.broadcasted_iota(jnp.int32, (rows, tq), 0)
    qry = lax.broadcasted_iota(jnp.int32, (rows, tq), 1)
    return key <= qry


def _flash_fwd(qp, kp, vt, S, shards=()):
    tq = min(512, S)
    nq = S // tq
    RB = ATT_ROWS
    NH = FWD_HEADS
    qb_of, kb_of, T = _tri_pairs(nq, k_major=False)
    n = len(shards)
    steps = (MLA_HEADS // NH) * T

    def body(qb_ref, kb_ref, q_ref, k_ref, vt_ref, *rest):
        w_ins, (o_ref, lse_ref), w_outs = rest[:n], rest[n:n + 2], rest[n + 2:2 * n + 2]
        m_sc, l_sc, acc_sc, s_sc, p_sc = rest[2 * n + 2:2 * n + 7]
        sems = rest[2 * n + 7:]
        t = pl.program_id(1)
        qb = qb_ref[t]
        kb = kb_ref[t]
        lin = pl.program_id(0) * T + t

        if n:
            @pl.when(lin == 0)
            def _():
                _gather_phase(0, w_ins, w_outs, sems)

            @pl.when(lin == steps // 2)
            def _():
                _gather_phase(1, w_ins, w_outs, sems)

        @pl.when(kb == 0)
        def _():
            m_sc[...] = jnp.full(m_sc.shape, NEG, F32)
            l_sc[...] = jnp.zeros(l_sc.shape, F32)
            acc_sc[...] = jnp.zeros(acc_sc.shape, F32)

        def scores(a):
            sl = slice(a * 128, (a + 1) * 128)
            s_sc[a] = _dot_nt(k_ref[:, sl], q_ref[:, sl])

        def step(masked):
            for a in range(NH):
                scores(a)
            for a in range(NH):
                mx = [jnp.full((8, tq), NEG, F32) for _ in range(RB // 8)]
                for r in range(0, tq, RB):
                    sc = s_sc[a, r:r + RB, :]
                    if masked:
                        sc = jnp.where(_causal_keep(r, RB, tq), sc, NEG)
                        s_sc[a, r:r + RB, :] = sc
                    for i in range(RB // 8):
                        mx[i] = jnp.maximum(mx[i], sc[i * 8:(i + 1) * 8, :])
                mx8 = functools.reduce(jnp.maximum, mx)
                m_prev = m_sc[a]
                m_new = jnp.maximum(m_prev, jnp.max(mx8, axis=0, keepdims=True))
                al = jnp.exp(m_prev - m_new)
                m_sc[a] = m_new
                ls = [jnp.zeros((8, tq), F32) for _ in range(RB // 8)]
                for r in range(0, tq, RB):
                    p = jnp.exp(s_sc[a, r:r + RB, :] - m_new)
                    for i in range(RB // 8):
                        ls[i] = ls[i] + p[i * 8:(i + 1) * 8, :]
                    p_sc[a, r:r + RB, :] = p.astype(BF16)
                l_sc[a] = al * l_sc[a] + jnp.sum(functools.reduce(jnp.add, ls), axis=0, keepdims=True)
                pair = slice((a // 2) * 128, (a // 2 + 1) * 128)
                pv = jnp.dot(vt_ref[pair, :], p_sc[a], preferred_element_type=F32)
                rs = slice(a * 64, (a + 1) * 64)
                own = slice((a % 2) * 64, (a % 2 + 1) * 64)
                acc_sc[rs, :] = acc_sc[rs, :] * al + pv[own, :]

        @pl.when(kb < qb)
        def _():
            step(False)

        @pl.when(kb == qb)
        def _():
            step(True)
            for a in range(NH):
                rs = slice(a * 64, (a + 1) * 64)
                acc_sc[rs, :] = acc_sc[rs, :] / l_sc[a]
                lse_ref[a:a + 1, :] = m_sc[a] + jnp.log(l_sc[a])
            o_ref[...] = acc_sc[...].T.astype(BF16)

        if n:
            @pl.when(lin == steps - 1)
            def _():
                _gather_phase(2, w_ins, w_outs, sems)

    grid_spec = pltpu.PrefetchScalarGridSpec(
        num_scalar_prefetch=2, grid=(MLA_HEADS // NH, T),
        in_specs=[pl.BlockSpec((tq, 128 * NH), lambda j, t, qb, kb: (qb[t], j)),
                  pl.BlockSpec((tq, 128 * NH), lambda j, t, qb, kb: (kb[t], j)),
                  pl.BlockSpec((64 * NH, tq), lambda j, t, qb, kb: (j, kb[t]))] + [_ANY] * n,
        out_specs=[pl.BlockSpec((tq, 64 * NH), lambda j, t, qb, kb: (qb[t], j)),
                   pl.BlockSpec((None, NH, tq), lambda j, t, qb, kb: (j, 0, qb[t]))] + [_ANY] * n,
        scratch_shapes=[pltpu.VMEM((NH, 1, tq), F32), pltpu.VMEM((NH, 1, tq), F32), pltpu.VMEM((64 * NH, tq), F32),
                        pltpu.VMEM((NH, tq, tq), F32), pltpu.VMEM((NH, tq, tq), BF16)] + (_gather_sems(n) if n else []),
    )
    out, lse, *gathered = pl.pallas_call(
        body, name="flash_fwd", grid_spec=grid_spec,
        out_shape=[_sds((S, 512), BF16), _sds((MLA_HEADS // NH, NH, S), F32)] + _gather_out_shapes(shards),
        compiler_params=_cp(("arbitrary", "arbitrary")),
    )(qb_of, kb_of, qp, kp, vt, *shards)
    return out, lse.reshape(MLA_HEADS // 2, 2, S), gathered


def _decay_table():
    log_g = np.log(1.0 - 2.0 ** (-5.0 - np.arange(RET_HEADS, dtype=np.float32))).astype(np.float32)
    return jnp.asarray(np.broadcast_to(log_g[:, None, None], (RET_HEADS, 8, 128)).copy())


def _decay_terms(lg_ref):
    C = RET_CHUNK
    lg = lg_ref[0:1, :]
    row = lax.broadcasted_iota(jnp.int32, (C, C), 0)
    col = lax.broadcasted_iota(jnp.int32, (C, C), 1)
    diff = (row - col).astype(F32)
    dmat = jnp.where(diff >= 0, jnp.exp(jnp.maximum(diff, 0.0) * jnp.tile(lg, (1, C // 128))), 0.0)
    j = lax.broadcasted_iota(jnp.int32, (C, 1), 0).astype(F32)
    lg1 = lg[:, 0:1]
    zeta = jnp.exp((C - 1 - j) * lg1)
    xi = jnp.exp((j + 1.0) * lg1)
    g_chunk = jnp.exp(C * lg1)
    return dmat, zeta, xi, g_chunk


def _ret_fwd(rq, rk, rv, rg, gn_w, S):
    C = RET_CHUNK
    N = S // C
    G = min(RET_GROUP, N)
    NB = N // G

    def body(lg_ref, q_ref, k_ref, v_ref, rg_ref, w_ref, ry_ref, ro_ref, rprev_ref, r_sc):
        @pl.when(pl.program_id(1) == 0)
        def _():
            r_sc[...] = jnp.zeros(r_sc.shape, F32)

        dmat, zeta, xi, g_chunk = _decay_terms(lg_ref)
        w = w_ref[...]
        r = r_sc[...]
        for i in range(G):
            rows = slice(i * C, (i + 1) * C)
            q = q_ref[rows, :]
            k = k_ref[rows, :]
            v = v_ref[rows, :]
            r_prev = r.astype(BF16)
            rprev_ref[i] = r_prev
            sc = _dot_nt(q, k) * dmat
            ry = _dot(sc, v) + jnp.dot(q, r_prev, preferred_element_type=F32) * xi
            ry_ref[rows, :] = ry
            r = g_chunk * r + _dot_tn(k, zeta * v.astype(F32))
            mu = jnp.mean(ry, axis=-1, keepdims=True)
            yc = ry - mu
            yh = yc * lax.rsqrt(jnp.mean(yc * yc, axis=-1, keepdims=True) + EPS)
            g = rg_ref[rows, :]
            ro_ref[rows, :] = (g * _sig(g) * (yh * w)).astype(BF16)
        r_sc[...] = r

    blk = pl.BlockSpec((G * C, 128), lambda h, n: (n, h))
    return pl.pallas_call(
        body, name="ret_fwd", grid=(RET_HEADS, NB),
        in_specs=[pl.BlockSpec((None, 8, 128), lambda h, n: (h, 0, 0)), blk, blk, blk, blk,
                  pl.BlockSpec((1, 128), lambda h, n: (0, h))],
        out_specs=[blk, blk, pl.BlockSpec((G, 128, 128), lambda h, n: (h * NB + n, 0, 0))],
        out_shape=[_sds((S, 512), F32), _sds((S, 512), BF16), _sds((RET_HEADS * N, 128, 128), BF16)],
        scratch_shapes=[pltpu.VMEM((128, 128), F32)],
        compiler_params=_cp(("parallel", "arbitrary")),
    )(_decay_table(), rq, rk, rv, rg, gn_w)


def _outproj(ro, mo, x, w_o, g_post, g_pre, S):
    tm = min(512, S)

    def body(ro_ref, mo_ref, x_ref, wo_ref, g1_ref, g2_ref, mix_ref, h1_ref, hn_ref):
        mix = (jnp.dot(ro_ref[...], wo_ref[0:512, :], preferred_element_type=F32)
               + jnp.dot(mo_ref[...], wo_ref[512:1024, :], preferred_element_type=F32))
        mix_ref[...] = mix.astype(BF16)
        h1 = x_ref[...] + _rms(mix, g1_ref[...])
        h1_ref[...] = h1
        hn_ref[...] = _rms(h1, g2_ref[...]).astype(BF16)

    return pl.pallas_call(
        body, name="outproj", grid=(S // tm,),
        in_specs=[_rows(tm, 512), _rows(tm, 512), _rows(tm, D_MODEL), _full(D_MODEL, D_MODEL), _full(1, D_MODEL),
                  _full(1, D_MODEL)],
        out_specs=[_rows(tm, D_MODEL)] * 3,
        out_shape=[_sds((S, D_MODEL), BF16), _sds((S, D_MODEL), F32), _sds((S, D_MODEL), BF16)],
        compiler_params=_cp(("parallel",)),
    )(ro, mo, x, w_o, g_post, g_pre)


def _ffn_up(hn, w_gate_t, w_up_t, S):
    tm = min(512, S)
    tn = D_FF // 2

    def body(hn_ref, wg_ref, wu_ref, fg_ref, fu_ref, act_ref):
        hn_b = hn_ref[...]
        g = _dot_nt(hn_b, wg_ref[...])
        u = _dot_nt(hn_b, wu_ref[...])
        s = _sig(g)
        silu = g * s
        fg_ref[...] = (u * (s + silu * (1.0 - s))).astype(BF16)
        fu_ref[...] = silu.astype(BF16)
        act_ref[...] = (silu * u).astype(BF16)

    wspec = pl.BlockSpec((tn, D_MODEL), lambda j, i: (j, 0))
    ospec = pl.BlockSpec((tm, tn), lambda j, i: (i, j))
    return pl.pallas_call(
        body, name="ffn_up", grid=(2, S // tm),
        in_specs=[pl.BlockSpec((tm, D_MODEL), lambda j, i: (i, 0)), wspec, wspec],
        out_specs=[ospec] * 3, out_shape=[_sds((S, D_FF), BF16)] * 3,
        compiler_params=_cp(("parallel", "parallel")),
    )(hn, w_gate_t, w_up_t)


def _ffn_down(act, w_down, h1, g, S):
    tm = min(512, S)

    def body(act_ref, wd_ref, h1_ref, g_ref, ff_ref, h2_ref):
        ff = jnp.dot(act_ref[...], wd_ref[...], preferred_element_type=F32)
        ff_ref[...] = ff.astype(BF16)
        h2_ref[...] = h1_ref[...] + _rms(ff, g_ref[...])

    return pl.pallas_call(
        body, name="ffn_down", grid=(S // tm,),
        in_specs=[_rows(tm, D_FF), _full(D_FF, D_MODEL), _rows(tm, D_MODEL), _full(1, D_MODEL)],
        out_specs=[_rows(tm, D_MODEL)] * 2, out_shape=[_sds((S, D_MODEL), BF16), _sds((S, D_MODEL), F32)],
        compiler_params=_cp(("parallel",)),
    )(act, w_down, h1, g)


def _ple_loss(p, h2, tgt, w_pp, w_pg, b_pg, g_ple, S):
    tm = min(512, S)

    def body(p_ref, h2_ref, t_ref, wp_ref, wg_ref, b_ref, gp_ref,
             dz_ref, dpe_ref, dh2_ref, h2b_ref, loss_ref, dgp_ref, db_ref):
        @pl.when(pl.program_id(0) == 0)
        def _():
            loss_ref[...] = jnp.zeros(loss_ref.shape, F32)
            dgp_ref[...] = jnp.zeros(dgp_ref.shape, F32)
            db_ref[...] = jnp.zeros(db_ref.shape, F32)

        gp = gp_ref[...]
        pe = _dot(p_ref[...], wp_ref[...])
        r = lax.rsqrt(jnp.mean(pe * pe, axis=-1, keepdims=True) + EPS)
        peh = pe * r
        e = peh * gp
        h2 = h2_ref[...]
        h2b = h2.astype(BF16)
        h2b_ref[...] = h2b
        gt = _sig(jnp.dot(h2b, wg_ref[...], preferred_element_type=F32) + b_ref[...])
        diff = h2 + e * gt - t_ref[...]
        loss_ref[...] += _colsum(diff * diff)
        dh3 = diff * (1.0 / D_MODEL)
        de = dh3 * gt
        dz = dh3 * e * gt * (1.0 - gt)
        db_ref[...] += _colsum(dz)
        dgp_ref[...] += _colsum(de * peh)
        dpeh = de * gp
        dpe = r * (dpeh - peh * jnp.mean(dpeh * peh, axis=-1, keepdims=True))
        dzb = dz.astype(BF16)
        dz_ref[...] = dzb
        dpe_ref[...] = dpe.astype(BF16)
        dh2_ref[...] = dh3 + _dot_nt(dzb, wg_ref[...])

    return pl.pallas_call(
        body, name="ple_loss", grid=(S // tm,),
        in_specs=[_rows(tm, PLE_DIM), _rows(tm, D_MODEL), _rows(tm, D_MODEL), _full(PLE_DIM, D_MODEL),
                  _full(D_MODEL, D_MODEL), _full(1, D_MODEL), _full(1, D_MODEL)],
        out_specs=[_rows(tm, D_MODEL)] * 4 + [_acc(1, D_MODEL)] * 3,
        out_shape=[_sds((S, D_MODEL), BF16), _sds((S, D_MODEL), BF16), _sds((S, D_MODEL), F32), _sds((S, D_MODEL), BF16)]
        + [_sds((1, D_MODEL), F32)] * 3,
        compiler_params=_cp(("arbitrary",)),
    )(p, h2, tgt, w_pp, w_pg, b_pg, g_ple)


def _wgrad(a, b, name, S):
    M = a.shape[1]
    N = b.shape[1]
    ts = min(2048, S)
    nsplit = 2 if M * N >= 2 * 1024 * 1024 else 1
    tn = N // nsplit

    def body(a_ref, b_ref, o_ref):
        @pl.when(pl.program_id(1) == 0)
        def _():
            o_ref[...] = jnp.zeros(o_ref.shape, F32)

        o_ref[...] += _dot_tn(a_ref[...], b_ref[...])

    return pl.pallas_call(
        body, name=name, grid=(nsplit, S // ts),
        in_specs=[pl.BlockSpec((ts, M), lambda j, s: (s, 0)), pl.BlockSpec((ts, tn), lambda j, s: (s, j))],
        out_specs=pl.BlockSpec((M, tn), lambda j, s: (0, j)), out_shape=_sds((M, N), F32),
        compiler_params=_cp(("parallel", "arbitrary")),
    )(a, b)


def _ffn_down_bwd(dh2, ff, g, w_down, dgate_f, dup_f, S):
    tm = min(512, S)
    tn = D_FF // 2

    def body(dh2_ref, ff_ref, g_ref, wd_ref, fg_ref, fu_ref, dff_ref, dgate_ref, dup_ref, dg_ref):
        @pl.when(pl.program_id(0) == 0)
        def _():
            dg_ref[...] = jnp.zeros(dg_ref.shape, F32)

        dff, ga = _rms_bwd(dh2_ref[...], ff_ref[...].astype(F32), g_ref[...])
        dg_ref[...] += _colsum(ga)
        dffb = dff.astype(BF16)
        dff_ref[...] = dffb
        for seg in range(2):
            sl = slice(seg * tn, (seg + 1) * tn)
            dact = _dot_nt(dffb, wd_ref[sl, :])
            dgate_ref[:, sl] = (dact * fg_ref[:, sl].astype(F32)).astype(BF16)
            dup_ref[:, sl] = (dact * fu_ref[:, sl].astype(F32)).astype(BF16)

    return pl.pallas_call(
        body, name="ffn_down_bwd", grid=(S // tm,),
        in_specs=[_rows(tm, D_MODEL), _rows(tm, D_MODEL), _full(1, D_MODEL), _full(D_FF, D_MODEL), _rows(tm, D_FF),
                  _rows(tm, D_FF)],
        out_specs=[_rows(tm, D_MODEL), _rows(tm, D_FF), _rows(tm, D_FF), _acc(1, D_MODEL)],
        out_shape=[_sds((S, D_MODEL), BF16), _sds((S, D_FF), BF16), _sds((S, D_FF), BF16), _sds((1, D_MODEL), F32)],
        compiler_params=_cp(("arbitrary",)),
    )(dh2, ff, g, w_down, dgate_f, dup_f)


def _ffn_up_bwd(dgate, dup, w_gate, w_up, h1, mix, dh2, g_pre, g_post, w_o, S, grads=()):
    tm = min(512, S)
    n = len(grads)
    last = S // tm - 1

    def body(dgate_ref, dup_ref, wg_ref, wu_ref, h1_ref, mix_ref, dh2_ref, g2_ref, g1_ref, wo_ref, *rest):
        g_ins = rest[:n]
        dh1_ref, dmix_ref, dro_ref, dmo_ref, dg2_ref, dg1_ref = rest[n:n + 6]
        g_outs, sems = rest[n + 6:2 * n + 6], rest[2 * n + 6:]

        @pl.when(pl.program_id(0) == 0)
        def _():
            dg2_ref[...] = jnp.zeros(dg2_ref.shape, F32)
            dg1_ref[...] = jnp.zeros(dg1_ref.shape, F32)
            for cp in (_swap_copies(g_ins, g_outs, sems) if n else []):
                cp.start()

        dhn = (jnp.dot(dgate_ref[...], wg_ref[...], preferred_element_type=F32)
               + jnp.dot(dup_ref[...], wu_ref[...], preferred_element_type=F32))
        d1, ga = _rms_bwd(dhn, h1_ref[...], g2_ref[...])
        dg2_ref[...] += _colsum(ga)
        dh1 = dh2_ref[...] + d1
        dh1_ref[...] = dh1
        dmix, gb = _rms_bwd(dh1, mix_ref[...].astype(F32), g1_ref[...])
        dg1_ref[...] += _colsum(gb)
        dmixb = dmix.astype(BF16)
        dmix_ref[...] = dmixb
        dcat = _dot_nt(dmixb, wo_ref[...])
        dro_ref[...] = dcat[:, 0:512].astype(BF16)
        dmo_ref[...] = dcat[:, 512:1024].astype(BF16)

        if n:
            @pl.when(pl.program_id(0) == last)
            def _():
                for cp in _swap_copies(g_ins, g_outs, sems):
                    cp.wait()

    dh1, dmix, dro, dmo, dg2, dg1, *got = pl.pallas_call(
        body, name="ffn_up_bwd", grid=(S // tm,),
        in_specs=[_rows(tm, D_FF), _rows(tm, D_FF), _full(D_FF, D_MODEL), _full(D_FF, D_MODEL), _rows(tm, D_MODEL),
                  _rows(tm, D_MODEL), _rows(tm, D_MODEL), _full(1, D_MODEL), _full(1, D_MODEL), _full(D_MODEL, D_MODEL)]
        + [_ANY] * n,
        out_specs=[_rows(tm, D_MODEL), _rows(tm, D_MODEL), _rows(tm, 512), _rows(tm, 512), _acc(1, D_MODEL),
                   _acc(1, D_MODEL)] + [_ANY] * n,
        out_shape=[_sds((S, D_MODEL), F32), _sds((S, D_MODEL), BF16), _sds((S, 512), BF16), _sds((S, 512), BF16),
                   _sds((1, D_MODEL), F32), _sds((1, D_MODEL), F32)] + _swap_out_shapes(grads),
        scratch_shapes=_swap_sems(n) if n else [],
        compiler_params=_cp(("arbitrary",)),
    )(dgate, dup, w_gate, w_up, h1, mix, dh2, g_pre, g_post, w_o, *grads)
    return dh1, dmix, dro, dmo, dg2, dg1, got


def _attn_delta(o, do, S, grads=()):
    tm = min(512, S)
    n = len(grads)
    last = S // tm - 1

    def body(o_ref, do_ref, *rest):
        g_ins, (dot_ref, d_ref), g_outs, sems = rest[:n], rest[n:n + 2], rest[n + 2:2 * n + 2], rest[2 * n + 2:]
        if n:
            @pl.when(pl.program_id(0) == 0)
            def _():
                for cp in _swap_copies(g_ins, g_outs, sems):
                    cp.start()

        do = do_ref[...].astype(F32)
        prod_t = (o_ref[...].astype(F32) * do).T
        dot_ref[...] = do.T.astype(BF16)
        for h in range(MLA_HEADS):
            d_ref[h // 2, (h % 2):(h % 2) + 1, :] = jnp.sum(prod_t[h * 64:(h + 1) * 64, :], axis=0, keepdims=True)

        if n:
            @pl.when(pl.program_id(0) == last)
            def _():
                for cp in _swap_copies(g_ins, g_outs, sems):
                    cp.wait()

    dot, delta, *got = pl.pallas_call(
        body, name="attn_delta", grid=(S // tm,),
        in_specs=[_rows(tm, 512), _rows(tm, 512)] + [_ANY] * n,
        out_specs=[pl.BlockSpec((512, tm), lambda i: (0, i)), pl.BlockSpec((MLA_HEADS // 2, 2, tm), lambda i: (0, 0, i))]
        + [_ANY] * n,
        out_shape=[_sds((512, S), BF16), _sds((MLA_HEADS // 2, 2, S), F32)] + _swap_out_shapes(grads),
        scratch_shapes=_swap_sems(n) if n else [],
        compiler_params=_cp(("arbitrary",)),
    )(o, do, *grads)
    return dot, delta, got


def _flash_bwd(qp, kp, kt, v, do, dot, lse, delta, S, sums=()):
    tq = min(512, S)
    nq = S // tq
    RB = ATT_ROWS
    NH = BWD_HEADS
    qb_of, kb_of, T = _tri_pairs(nq, k_major=True)
    n = len(sums)
    steps = (MLA_HEADS // NH) * T

    def body(qb_ref, kb_ref, q_ref, k_ref, kt_ref, v_ref, do_ref, dot_ref, lse_ref, dl_ref, *rest):
        g_ins, (dq_ref, dk_ref, dv_ref), g_outs = rest[:n], rest[n:n + 3], rest[n + 3:2 * n + 3]
        dk_sc, dv_sc, s_sc, dp_sc, p_sc, ds_sc = rest[2 * n + 3:2 * n + 9]
        sems = rest[2 * n + 9:]
        t = pl.program_id(1)
        qb = qb_ref[t]
        kb = kb_ref[t]
        lin = pl.program_id(0) * T + t

        if n:
            @pl.when(lin == 0)
            def _():
                for cp in _scatter_copies(g_ins, g_outs, sems):
                    cp.start()

        @pl.when(t == 0)
        def _():
            dq_ref[...] = jnp.zeros(dq_ref.shape, F32)

        @pl.when(qb == kb)
        def _():
            dk_sc[...] = jnp.zeros(dk_sc.shape, F32)
            dv_sc[...] = jnp.zeros(dv_sc.shape, F32)

        lane = lax.broadcasted_iota(jnp.int32, (tq, 64 * NH), 1)

        def step(masked):
            vv = v_ref[...]
            do_all = do_ref[...]
            mine = [(lane >= a * 64) & (lane < (a + 1) * 64) for a in range(NH)]
            for a in range(NH):
                sl = slice(a * 128, (a + 1) * 128)
                s_sc[a] = _dot_nt(k_ref[:, sl], q_ref[:, sl])
                dp_sc[a] = jnp.dot(jnp.where(mine[a], vv, jnp.zeros_like(vv)), dot_ref[...],
                                   preferred_element_type=F32)
            for a in range(NH):
                sl = slice(a * 128, (a + 1) * 128)
                lse = lse_ref[a:a + 1, :]
                dl = dl_ref[a:a + 1, :]
                for r in range(0, tq, RB):
                    sc = s_sc[a, r:r + RB, :]
                    if masked:
                        sc = jnp.where(_causal_keep(r, RB, tq), sc, NEG)
                    p = jnp.exp(sc - lse)
                    p_sc[a, r:r + RB, :] = p.astype(BF16)
                    ds_sc[a, r:r + RB, :] = (p * (dp_sc[a, r:r + RB, :] - dl)).astype(BF16)
                ds = ds_sc[a]
                dv_sc[...] += jnp.dot(p_sc[a], jnp.where(mine[a], do_all, jnp.zeros_like(do_all)),
                                      preferred_element_type=F32)
                dk_sc[:, sl] += jnp.dot(ds, q_ref[:, sl], preferred_element_type=F32)
                dq_ref[qb, sl, :] += jnp.dot(kt_ref[sl, :], ds, preferred_element_type=F32)

        @pl.when(qb > kb)
        def _():
            step(False)

        @pl.when(qb == kb)
        def _():
            step(True)

        @pl.when(qb == nq - 1)
        def _():
            dk_ref[...] = dk_sc[...].astype(BF16)
            dv_ref[...] = dv_sc[...].astype(BF16)

        if n:
            @pl.when(lin == steps - 1)
            def _():
                for cp in _scatter_copies(g_ins, g_outs, sems):
                    cp.wait()

    grid_spec = pltpu.PrefetchScalarGridSpec(
        num_scalar_prefetch=2, grid=(MLA_HEADS // NH, T),
        in_specs=[pl.BlockSpec((tq, 128 * NH), lambda j, t, qb, kb: (qb[t], j)),
                  pl.BlockSpec((tq, 128 * NH), lambda j, t, qb, kb: (kb[t], j)),
                  pl.BlockSpec((128 * NH, tq), lambda j, t, qb, kb: (j, kb[t])),
                  pl.BlockSpec((tq, 64 * NH), lambda j, t, qb, kb: (kb[t], j)),
                  pl.BlockSpec((tq, 64 * NH), lambda j, t, qb, kb: (qb[t], j)),
                  pl.BlockSpec((64 * NH, tq), lambda j, t, qb, kb: (j, qb[t])),
                  pl.BlockSpec((None, NH, tq), lambda j, t, qb, kb: (j, 0, qb[t])),
                  pl.BlockSpec((None, NH, tq), lambda j, t, qb, kb: (j, 0, qb[t]))] + [_ANY] * n,
        out_specs=[pl.BlockSpec((nq, 128 * NH, tq), lambda j, t, qb, kb: (0, j, 0), pipeline_mode=pl.Buffered(1)),
                   pl.BlockSpec((tq, 128 * NH), lambda j, t, qb, kb: (kb[t], j)),
                   pl.BlockSpec((tq, 64 * NH), lambda j, t, qb, kb: (kb[t], j))] + [_ANY] * n,
        scratch_shapes=[pltpu.VMEM((tq, 128 * NH), F32), pltpu.VMEM((tq, 64 * NH), F32), pltpu.VMEM((NH, tq, tq), F32),
                        pltpu.VMEM((NH, tq, tq), F32), pltpu.VMEM((NH, tq, tq), BF16), pltpu.VMEM((NH, tq, tq), BF16)]
        + (_scatter_sems(n) if n else []),
    )
    dq, dk, dv, *parts = pl.pallas_call(
        body, name="flash_bwd", grid_spec=grid_spec,
        out_shape=[_sds((nq, 1024, tq), F32), _sds((S, 1024), BF16), _sds((S, 512), BF16)] + _scatter_out_shapes(sums),
        compiler_params=_cp(("arbitrary", "arbitrary")),
    )(qb_of, kb_of, qp, kp, kt, v, do, dot, lse.reshape(MLA_HEADS // NH, NH, S), delta.reshape(MLA_HEADS // NH, NH, S),
      *sums)
    return dq, dk, dv, parts


def _mla_up_bwd(dqp, dkp, dv, cq, ckv, gq, gkv, w_uq, w_ukv, tabs, S):
    tm = min(512, S)

    def body(dq_ref, dk_ref, dv_ref, cq_ref, ckv_ref, gq_ref, gkv_ref, wuq_ref, wukv_ref, cm_ref, sa_ref, sb_ref,
             dqh_ref, dkv_ref, dcq_ref, dckv_ref, dkr_ref, dgq_ref, dgkv_ref):
        @pl.when(pl.program_id(0) == 0)
        def _():
            dgq_ref[...] = jnp.zeros(dgq_ref.shape, F32)
            dgkv_ref[...] = jnp.zeros(dgkv_ref.shape, F32)

        cm = cm_ref[...]
        sa = sa_ref[...]
        sb = sb_ref[...]
        lane = lax.broadcasted_iota(jnp.int32, (tm, 128), 1)
        dkr_r = jnp.zeros((tm, 128), F32)
        for h in range(MLA_HEADS):
            sl = slice(h * 128, (h + 1) * 128)
            dqh_ref[:, sl] = (_unrope_mla(dq_ref[sl, :].T, cm, sa, sb) * SCALE_MLA).astype(BF16)
            gk = dk_ref[:, sl]
            dkr_r = dkr_r + gk.astype(F32)
            dkv_ref[:, sl] = gk
        dkr_r = jnp.where((lane >= 64) & (lane < 96), dkr_r, 0.0)
        dkr_ref[...] = _unrope_mla(dkr_r, cm, sa, sb).astype(BF16)
        dkv_ref[:, 1024:1536] = dv_ref[...]
        dcq, ga = _rms_bwd(_dot_nt(dqh_ref[...], wuq_ref[...]), cq_ref[...], gq_ref[...])
        dcq_ref[...] = dcq.astype(BF16)
        dgq_ref[...] += _colsum(ga)
        dckv, gb = _rms_bwd(_dot_nt(dkv_ref[...], wukv_ref[...]), ckv_ref[...], gkv_ref[...])
        dckv_ref[...] = dckv.astype(BF16)
        dgkv_ref[...] += _colsum(gb)

    per_q = dqp.shape[2] // tm
    return pl.pallas_call(
        body, name="mla_up_bwd", grid=(S // tm,),
        in_specs=[pl.BlockSpec((None, 1024, tm), lambda i: (i // per_q, 0, i % per_q)),
                  _rows(tm, 1024), _rows(tm, 512), _rows(tm, Q_LORA), _rows(tm, KV_LORA),
                  _full(1, Q_LORA), _full(1, KV_LORA), _full(Q_LORA, 1024), _full(KV_LORA, 1536)] + [_rows(tm, 128)] * 3,
        out_specs=[_rows(tm, 1024), _rows(tm, 1536), _rows(tm, Q_LORA), _rows(tm, KV_LORA), _rows(tm, 128),
                   _acc(1, Q_LORA), _acc(1, KV_LORA)],
        out_shape=[_sds((S, 1024), BF16), _sds((S, 1536), BF16), _sds((S, Q_LORA), BF16), _sds((S, KV_LORA), BF16),
                   _sds((S, 128), BF16), _sds((1, Q_LORA), F32), _sds((1, KV_LORA), F32)],
        compiler_params=_cp(("arbitrary",)),
    )(dqp, dkp, dv, cq, ckv, gq, gkv, w_uq, w_ukv, *tabs[2:])


def _ret_bwd(rq, rk, rv, rprev, ry, rg, dro, gn_w, tabs, S, grads=()):
    C = RET_CHUNK
    N = S // C
    G = min(RET_GROUP, N)
    NB = N // G
    n_sw = len(grads)

    def body(lg_ref, q_ref, k_ref, v_ref, rp_ref, ry_ref, rg_ref, dro_ref, w_ref, cr_ref, sr_ref, *rest):
        g_ins, (drq_ref, drk_ref, drv_ref, drg_ref, dw_ref) = rest[:n_sw], rest[n_sw:n_sw + 5]
        g_outs, g_sc, sems = rest[n_sw + 5:2 * n_sw + 5], rest[2 * n_sw + 5], rest[2 * n_sw + 6:]
        lin = pl.program_id(0) * NB + pl.program_id(1)
        if n_sw:
            @pl.when(lin == 0)
            def _():
                for cp in _swap_copies(g_ins, g_outs, sems):
                    cp.start()

        @pl.when(pl.program_id(1) == 0)
        def _():
            g_sc[...] = jnp.zeros(g_sc.shape, F32)
            dw_ref[...] = jnp.zeros(dw_ref.shape, F32)

        dmat, zeta, xi, g_chunk = _decay_terms(lg_ref)
        w = w_ref[...]
        gacc = g_sc[...]
        dw = jnp.zeros((1, 128), F32)
        for i in reversed(range(G)):
            rows = slice(i * C, (i + 1) * C)
            ry = ry_ref[rows, :]
            mu = jnp.mean(ry, axis=-1, keepdims=True)
            yc = ry - mu
            rstd = lax.rsqrt(jnp.mean(yc * yc, axis=-1, keepdims=True) + EPS)
            yh = yc * rstd
            g = rg_ref[rows, :]
            s = _sig(g)
            dout = dro_ref[rows, :].astype(F32)
            drg_ref[rows, :] = (dout * (yh * w) * (s * (1.0 + g * (1.0 - s)))).astype(BF16)
            dgn = dout * (g * s)
            dw = dw + _colsum(dgn * yh)
            dyh = dgn * w
            dry = rstd * (dyh - jnp.mean(dyh, axis=-1, keepdims=True) - yh * jnp.mean(dyh * yh, axis=-1, keepdims=True))
            do = dry.astype(BF16)

            q = q_ref[rows, :]
            k = k_ref[rows, :]
            v = v_ref[rows, :]
            gfut = gacc.astype(BF16)
            sc = (_dot_nt(q, k) * dmat).astype(BF16)
            dsc = (_dot_nt(do, v) * dmat).astype(BF16)
            dq = jnp.dot(dsc, k, preferred_element_type=F32) + _dot_nt(do, rp_ref[i]) * xi
            dk = _dot_tn(dsc, q) + _dot_nt(v, gfut) * zeta
            dv = _dot_tn(sc, do) + jnp.dot(k, gfut, preferred_element_type=F32) * zeta
            gacc = g_chunk * gacc + _dot_tn(q, xi * dry)
            cr = cr_ref[rows, :]
            sr = sr_ref[rows, :]
            drq_ref[rows, :] = _unrope_ret(dq, cr, sr).astype(BF16)
            drk_ref[rows, :] = _unrope_ret(dk * SCALE_RET, cr, sr).astype(BF16)
            drv_ref[rows, :] = dv.astype(BF16)
        g_sc[...] = gacc
        dw_ref[...] += dw

        if n_sw:
            @pl.when(lin == RET_HEADS * NB - 1)
            def _():
                for cp in _swap_copies(g_ins, g_outs, sems):
                    cp.wait()

    blk = pl.BlockSpec((G * C, 128), lambda h, n: (NB - 1 - n, h))
    tab = pl.BlockSpec((G * C, 128), lambda h, n: (NB - 1 - n, 0))
    drq, drk, drv, drg, dw, *got = pl.pallas_call(
        body, name="ret_bwd", grid=(RET_HEADS, NB),
        in_specs=[pl.BlockSpec((None, 8, 128), lambda h, n: (h, 0, 0)), blk, blk, blk,
                  pl.BlockSpec((G, 128, 128), lambda h, n: (h * NB + NB - 1 - n, 0, 0)), blk, blk, blk,
                  pl.BlockSpec((1, 128), lambda h, n: (0, h)), tab, tab] + [_ANY] * n_sw,
        out_specs=[blk, blk, blk, blk, pl.BlockSpec((1, 128), lambda h, n: (0, h))] + [_ANY] * n_sw,
        out_shape=[_sds((S, 512), BF16)] * 4 + [_sds((1, 512), F32)] + _swap_out_shapes(grads),
        scratch_shapes=[pltpu.VMEM((128, 128), F32)] + (_swap_sems(n_sw) if n_sw else []),
        compiler_params=_cp(("arbitrary", "arbitrary")),
    )(_decay_table(), rq, rk, rv, rprev, ry, rg, dro, gn_w, tabs[0], tabs[1], *grads)
    return drq, drk, drv, drg, dw, got


def _inproj_bwd(drq, drk, drv, drg, dcq, dckv, dkr, w_in, dh1, x, g, S, sums=()):
    tm = min(512, S)
    n = len(sums)
    last = S // tm - 1

    def body(drq_ref, drk_ref, drv_ref, drg_ref, dcq_ref, dckv_ref, dkr_ref, w_ref, dh1_ref, x_ref, g_ref, *rest):
        g_ins, (gx_ref, dproj_ref, dg_ref), g_outs, sems = rest[:n], rest[n:n + 3], rest[n + 3:2 * n + 3], rest[2 * n + 3:]

        @pl.when(pl.program_id(0) == 0)
        def _():
            dg_ref[...] = jnp.zeros(dg_ref.shape, F32)
            for cp in (_scatter_copies(g_ins, g_outs, sems) if n else []):
                cp.start()

        dproj_ref[:, 0:512] = drq_ref[...]
        dproj_ref[:, 512:1024] = drk_ref[...]
        dproj_ref[:, 1024:1536] = drv_ref[...]
        dproj_ref[:, 1536:2048] = drg_ref[...]
        dproj_ref[:, 2048:2432] = dcq_ref[...]
        dproj_ref[:, 2432:2688] = dckv_ref[...]
        dproj_ref[:, 2688:2816] = dkr_ref[...]
        dx, ga = _rms_bwd(_dot_nt(dproj_ref[...], w_ref[...]), x_ref[...], g_ref[...])
        gx_ref[...] = dh1_ref[...] + dx
        dg_ref[...] += _colsum(ga)

        if n:
            @pl.when(pl.program_id(0) == last)
            def _():
                for cp in _scatter_copies(g_ins, g_outs, sems):
                    cp.wait()

    gx, dproj, dg, *parts = pl.pallas_call(
        body, name="inproj_bwd", grid=(S // tm,),
        in_specs=[_rows(tm, 512)] * 4 + [_rows(tm, Q_LORA), _rows(tm, KV_LORA), _rows(tm, 128),
                                         _full(D_MODEL, IN_COLS_P), _rows(tm, D_MODEL), _rows(tm, D_MODEL),
                                         _full(1, D_MODEL)] + [_ANY] * n,
        out_specs=[_rows(tm, D_MODEL), _rows(tm, IN_COLS_P), _acc(1, D_MODEL)] + [_ANY] * n,
        out_shape=[_sds((S, D_MODEL), F32), _sds((S, IN_COLS_P), BF16), _sds((1, D_MODEL), F32)]
        + _scatter_out_shapes(sums),
        scratch_shapes=_scatter_sems(n) if n else [],
        compiler_params=_cp(("arbitrary",)),
    )(drq, drk, drv, drg, dcq, dckv, dkr, w_in, dh1, x, g, *sums)
    return gx, dproj, dg, parts


def _pad_weights(w):
    w_in = w["w_in"]
    z = lambda r, c: jnp.zeros((r, c), BF16)
    w_in_p = jnp.concatenate([w_in[:, :2688], z(1024, 64), w_in[:, 2688:2720], z(1024, 32)], axis=1)
    w_uq_p = jnp.pad(w["w_uq"].reshape(Q_LORA, MLA_HEADS, 96), ((0, 0), (0, 0), (0, 32))).reshape(Q_LORA, 1024)
    ukv = w["w_ukv"].reshape(KV_LORA, MLA_HEADS, 128)
    k_part = jnp.pad(ukv[:, :, :64], ((0, 0), (0, 0), (0, 64))).reshape(KV_LORA, 1024)
    w_ukv_p = jnp.concatenate([k_part, ukv[:, :, 64:].reshape(KV_LORA, 512)], axis=1)
    return w_in_p, w_uq_p, w_ukv_p


BIG_SPEC = {n: (r, c, ax) for n, r, c, ax in BIG}
COLUMN_MAJOR = ("w_in", "w_uq", "w_gate", "w_up")
GRAD_TRANSPOSED = ("w_gate", "w_up")
GATHER_FIRST = ("w_in", "w_uq", "w_ukv")
GATHER_LATE = tuple(n for n, _, _, _ in BIG if n not in GATHER_FIRST)
REDUCE_EARLY = ("w_ple_gate", "w_ple_proj", "w_down", "w_gate", "w_up", "w_o")
REDUCE_MID = ("w_uq", "w_ukv")
REDUCE_LAST = ("w_in",)


def _local_step(x, p, pos_f, tgt, w, sm, late_shards=None, c_idx=None):
    S = x.shape[0]
    spread = late_shards is not None
    w = dict(w)
    tabs, first = _rope_tables(pos_f, S, [late_shards[n] for n in GATHER_FIRST] if spread else ())
    for i, n in enumerate(GATHER_FIRST if spread else ()):
        w[n] = _from_chips(first[i], BIG_SPEC[n][2])
    w_in_p, w_uq_p, w_ukv_p = _pad_weights(w)

    xn, rq, rk, rv, rg, cq, ckv, kr = _inproj(x, sm["pre_mix_norm"], w_in_p, tabs, S)
    cqn, ckvn, qp, kp, v, kt, vt = _mla_up(cq, ckv, kr, sm["mla_q_norm"], sm["mla_kv_norm"], w_uq_p, w_ukv_p, tabs, S)
    mo, lse, gathered = _flash_fwd(qp, kp, vt, S, [late_shards[n] for n in GATHER_LATE] if spread else ())
    for i, n in enumerate(GATHER_LATE if spread else ()):
        w[n] = _from_chips(gathered[i], 0 if n in GRAD_TRANSPOSED else BIG_SPEC[n][2])
    if not spread:
        w.update({n: w[n].T for n in GRAD_TRANSPOSED})
    ry, ro, rprev = _ret_fwd(rq, rk, rv, rg, sm["ret_gn_w"], S)
    mix, h1, hn = _outproj(ro, mo, x, w["w_o"], sm["post_mix_norm"], sm["pre_ffn_norm"], S)
    dgate_f, dup_f, act = _ffn_up(hn, w["w_gate"], w["w_up"], S)
    ff, h2 = _ffn_down(act, w["w_down"], h1, sm["post_ffn_norm"], S)
    dz, dpe, dh2, h2b, loss_vec, d_ple_norm, d_b = _ple_loss(
        p, h2, tgt, w["w_ple_proj"], w["w_ple_gate"], sm["b_ple_gate"], sm["ple_norm"], S)

    gw = {}
    gs = {"ple_norm": d_ple_norm, "b_ple_gate": d_b}
    gw["w_ple_gate"] = _wgrad(h2b, dz, "wgrad_ple_gate", S)
    gw["w_ple_proj"] = _wgrad(p, dpe, "wgrad_ple_proj", S)
    dff, dgate, dup, gs["post_ffn_norm"] = _ffn_down_bwd(dh2, ff, sm["post_ffn_norm"], w["w_down"], dgate_f, dup_f, S)
    gw["w_down"] = _wgrad(act, dff, "wgrad_down", S)
    if spread:
        gw["w_gate"] = _wgrad(dgate, hn, "wgrad_gate", S)
        gw["w_up"] = _wgrad(dup, hn, "wgrad_up", S)
    else:
        gw["w_gate"] = _wgrad(hn, dgate, "wgrad_gate", S)
        gw["w_up"] = _wgrad(hn, dup, "wgrad_up", S)
    first = REDUCE_EARLY[:-1]
    g4 = [_by_chip(gw.pop(n), *((D_FF, D_MODEL, 0) if n in GRAD_TRANSPOSED else BIG_SPEC[n]))
          for n in first] if spread else []
    dh1, dmix, dro, dmo, gs["pre_ffn_norm"], gs["post_mix_norm"], got = _ffn_up_bwd(
        dgate, dup, w["w_gate"], w["w_up"], h1, mix, dh2, sm["pre_ffn_norm"], sm["post_mix_norm"], w["w_o"], S, g4)
    gw["w_o"] = jnp.concatenate([_wgrad(ro, dmix, "wgrad_o_ret", S), _wgrad(mo, dmix, "wgrad_o_mla", S)], axis=0)
    g4_o = [_by_chip(gw.pop("w_o"), *BIG_SPEC["w_o"])] if spread else []

    dmo_t, delta, got_o = _attn_delta(mo, dmo, S, g4_o)
    sums = [_add_half_rows(a, b, c_idx, "rs_add_halves_" + n)
            for n, a, b in zip(REDUCE_EARLY, g4 + g4_o, list(got) + list(got_o))] if spread else []
    dqp, dkp, dv, parts = _flash_bwd(qp, kp, kt, v, dmo, dmo_t, lse, delta, S, sums)
    dqh, dkv, dcq, dckv, dkr, gs["mla_q_norm"], gs["mla_kv_norm"] = _mla_up_bwd(
        dqp, dkp, dv, cq, ckv, sm["mla_q_norm"], sm["mla_kv_norm"], w_uq_p, w_ukv_p, tabs, S)
    g_uq_p = _wgrad(cqn, dqh, "wgrad_uq", S)
    g_ukv_p = _wgrad(ckvn, dkv, "wgrad_ukv", S)
    gw["w_uq"] = g_uq_p.reshape(Q_LORA, MLA_HEADS, 128)[:, :, :96].reshape(Q_LORA, 768)
    gw["w_ukv"] = jnp.concatenate(
        [g_ukv_p[:, :1024].reshape(KV_LORA, MLA_HEADS, 128)[:, :, :64], g_ukv_p[:, 1024:].reshape(KV_LORA, MLA_HEADS, 64)],
        axis=2).reshape(KV_LORA, 1024)

    g4_m = [_by_chip(gw.pop(n), *BIG_SPEC[n]) for n in REDUCE_MID] if spread else []
    drq, drk, drv, drg, gs["ret_gn_w"], got_m = _ret_bwd(rq, rk, rv, rprev, ry, rg, dro, sm["ret_gn_w"], tabs, S, g4_m)
    sums_m = [_add_half_rows(a, b, c_idx, "rs_add_halves_" + n) for n, a, b in zip(REDUCE_MID, g4_m, got_m)]
    grad_x, dproj, gs["pre_mix_norm"], parts_m = _inproj_bwd(drq, drk, drv, drg, dcq, dckv, dkr, w_in_p, dh1, x,
                                                             sm["pre_mix_norm"], S, sums_m)
    g_in_p = _wgrad(xn, dproj, "wgrad_in", S)
    gw["w_in"] = jnp.concatenate([g_in_p[:, :2688], g_in_p[:, 2752:2784]], axis=1)
    return loss_vec, grad_x, gw, gs, ((sums + sums_m, list(parts) + list(parts_m)) if spread else None)


def _my_place():
    x = lax.axis_index("x")
    y = lax.axis_index("y")
    c = lax.axis_index("c")
    return x, y, c


def _other_chips(x, y):
    return [(1 - x, y), (x, 1 - y), (1 - x, 1 - y)]


_ANY = pl.BlockSpec(memory_space=pl.ANY)


def _small_copies(v_ref, slots, sems):
    send, recv, lsem = sems
    x, y, c = _my_place()
    me = 4 * x + 2 * y + c
    cps = [pltpu.make_async_copy(v_ref, slots.at[me], lsem)]
    for r in range(1, N_DEV):
        peer = (x ^ (r >> 2), y ^ ((r >> 1) & 1), c ^ (r & 1))
        cps.append(pltpu.make_async_remote_copy(
            src_ref=v_ref, dst_ref=slots.at[me], send_sem=send.at[r - 1], recv_sem=recv.at[r - 1],
            device_id=peer, device_id_type=MESH))
    return cps


def _small_sum(slots, out_ref):
    acc = slots[0]
    for d in range(1, N_DEV):
        acc = acc + slots[d]
    out_ref[...] = acc
    loss = jnp.sum(acc[9:10, :], axis=1, keepdims=True) * (0.5 / D_MODEL)
    out_ref[9:10, :] = jnp.broadcast_to(loss, (1, PACK_COLS))


def _small_scratch():
    return [pltpu.VMEM((N_DEV, SMALL_ROWS, PACK_COLS), F32), pltpu.SemaphoreType.DMA((N_DEV - 1,)),
            pltpu.SemaphoreType.DMA((N_DEV - 1,)), pltpu.SemaphoreType.DMA]


N_BIG = len(BIG)


def _half(c, rows, align):
    h = rows // 2
    return pl.ds(pl.multiple_of(c * h, align), h)


def _gather_out_shapes(shards):
    return [_sds((N_CHIPS,) + tuple(s.shape), BF16) for s in shards]


def _gather_sems(n):
    return [pltpu.SemaphoreType.DMA((n, 3))] * 4 + [pltpu.SemaphoreType.DMA((n,))] * 2


def _gather_phase(phase, ins, outs, sems):
    send1, recv1, send2, recv2, send3, recv3 = sems
    x, y, c = _my_place()
    me = 2 * x + y
    chips = _other_chips(x, y)
    sib = (x, y, 1 - c)
    for t in range(len(ins)):
        rows = ins[t].shape[0]
        half = _half(c, rows, 16)
        other = _half(1 - c, rows, 16)
        def own():
            return pltpu.make_async_remote_copy(
                src_ref=ins[t], dst_ref=outs[t].at[me], send_sem=send3.at[t], recv_sem=recv3.at[t],
                device_id=sib, device_id_type=MESH)

        if phase == 0:
            own().start()
        if phase == 2:
            own().wait()
        for k, (cx, cy) in enumerate(chips):
            src = 2 * cx + cy

            def over_ici(slab):
                return pltpu.make_async_remote_copy(
                    src_ref=ins[t].at[half], dst_ref=outs[t].at[slab, half], send_sem=send1.at[t, k],
                    recv_sem=recv1.at[t, k], device_id=(cx, cy, c), device_id_type=MESH)

            def over_d2d(rows):
                return pltpu.make_async_remote_copy(
                    src_ref=outs[t].at[src, rows], dst_ref=outs[t].at[src, rows], send_sem=send2.at[t, k],
                    recv_sem=recv2.at[t, k], device_id=sib, device_id_type=MESH)

            if phase == 0:
                over_ici(me).start()
            if phase == 1:
                over_ici(src).wait_recv()
                over_d2d(half).start()
            if phase == 2:
                over_d2d(other).wait_recv()
                over_ici(me).wait_send()
                over_d2d(half).wait_send()


def _swap_copies(ins, outs, sems):
    send, recv = sems
    x, y, c = _my_place()
    return [pltpu.make_async_remote_copy(
        src_ref=ins[t].at[:, _half(1 - c, ins[t].shape[1], 8)], dst_ref=outs[t], send_sem=send.at[t],
        recv_sem=recv.at[t], device_id=(x, y, 1 - c), device_id_type=MESH) for t in range(len(ins))]


def _swap_out_shapes(gs):
    return [_sds((N_CHIPS, g.shape[1] // 2, g.shape[2]), F32) for g in gs]


def _swap_sems(n):
    return [pltpu.SemaphoreType.DMA((n,)), pltpu.SemaphoreType.DMA((n,))]


def _swap_half_rows(gs):
    n = len(gs)

    def body(*refs):
        cps = _swap_copies(refs[:n], refs[n:2 * n], refs[2 * n:])
        for cp in cps:
            cp.start()
        for cp in cps:
            cp.wait()

    return pl.pallas_call(
        body, name="rs_swap_halves",
        in_specs=[_ANY] * n, out_specs=[_ANY] * n, out_shape=_swap_out_shapes(gs), scratch_shapes=_swap_sems(n),
    )(*gs)


def _add_half_rows(g, got, c_idx, name):
    _, rows, cols = g.shape
    h = rows // 2

    def body(c_ref, a_ref, b_ref, o_ref):
        o_ref[...] = (a_ref[...] + b_ref[...]).astype(BF16)

    grid_spec = pltpu.PrefetchScalarGridSpec(
        num_scalar_prefetch=1, grid=(N_CHIPS,),
        in_specs=[pl.BlockSpec((None, h, cols), lambda j, c: (j, c[0], 0)),
                  pl.BlockSpec((None, h, cols), lambda j, c: (j, 0, 0))],
        out_specs=pl.BlockSpec((None, h, cols), lambda j, c: (j, 0, 0)),
    )
    return pl.pallas_call(
        body, name=name, grid_spec=grid_spec, out_shape=_sds((N_CHIPS, h, cols), BF16),
        compiler_params=_cp(("parallel",)),
    )(c_idx, g, got)


def _scatter_to_chips(ts, vec):
    n = len(ts)

    def body(*refs):
        ins, v_ref, outs, small_ref = refs[:n], refs[n], refs[n + 1:2 * n + 1], refs[2 * n + 1]
        slots, small_sems, sems = refs[2 * n + 2], refs[2 * n + 3:2 * n + 6], refs[2 * n + 6:]
        small = _small_copies(v_ref, slots, small_sems)
        cps = _scatter_copies(ins, outs, sems)
        for cp in small + cps:
            cp.start()
        for cp in small:
            cp.wait()
        _small_sum(slots, small_ref)
        for cp in cps:
            cp.wait()

    vm = pl.BlockSpec(memory_space=pltpu.VMEM)
    *parts, small_sum = pl.pallas_call(
        body, name="rs_scatter_chips",
        in_specs=[_ANY] * n + [vm], out_specs=[_ANY] * n + [vm],
        out_shape=_scatter_out_shapes(ts) + [_sds((SMALL_ROWS, PACK_COLS), F32)],
        scratch_shapes=_small_scratch() + _scatter_sems(n),
    )(*ts, vec)
    return parts, small_sum


def _scatter_copies(ins, outs, sems):
    send, recv = sems
    x, y, c = _my_place()
    return [pltpu.make_async_remote_copy(
        src_ref=ins[t].at[2 * cx + cy], dst_ref=outs[t].at[k], send_sem=send.at[t, k], recv_sem=recv.at[t, k],
        device_id=(cx, cy, c), device_id_type=MESH)
        for t in range(len(ins)) for k, (cx, cy) in enumerate(_other_chips(x, y))]


def _scatter_out_shapes(ts):
    return [_sds((3,) + tuple(t.shape[1:]), BF16) for t in ts]


def _scatter_sems(n):
    return [pltpu.SemaphoreType.DMA((n, 3)), pltpu.SemaphoreType.DMA((n, 3))]


def _add_four(mine, parts, place, name):
    _, h, cols = parts.shape

    def body(pl_ref, m_ref, p_ref, o_ref):
        o_ref[...] = ((m_ref[...].astype(F32) + p_ref[0].astype(F32)) + p_ref[1].astype(F32)) + p_ref[2].astype(F32)

    grid_spec = pltpu.PrefetchScalarGridSpec(
        num_scalar_prefetch=1, grid=(1,),
        in_specs=[pl.BlockSpec((None, h, cols), lambda i, pc: (pc[0], 0, 0)),
                  pl.BlockSpec((3, h, cols), lambda i, pc: (0, 0, 0))],
        out_specs=pl.BlockSpec((h, cols), lambda i, pc: (pc[1], 0)),
    )
    return pl.pallas_call(
        body, name=name, grid_spec=grid_spec, out_shape=_sds((2 * h, cols), F32),
        compiler_params=_cp(("arbitrary",)),
    )(place, mine, parts)


def _join_half_rows(rs):
    n = len(rs)

    def body(*refs):
        ins, outs = refs[:n], refs[n:2 * n]
        send, recv = refs[2 * n:]
        x, y, c = _my_place()
        cps = []
        for t in range(n):
            half = _half(c, outs[t].shape[0], 8)
            rc = pltpu.make_async_remote_copy(
                src_ref=ins[t].at[half], dst_ref=outs[t].at[half], send_sem=send.at[t], recv_sem=recv.at[t],
                device_id=(x, y, 1 - c), device_id_type=MESH)
            rc.start()
            cps.append(rc)
        for cp in cps:
            cp.wait()

    return pl.pallas_call(
        body, name="rs_join_halves",
        in_specs=[_ANY] * n, out_specs=[_ANY] * n,
        out_shape=[_sds(r.shape, F32) for r in rs],
        input_output_aliases={i: i for i in range(n)},
        scratch_shapes=[pltpu.SemaphoreType.DMA((n,))] * 2,
    )(*rs)


def _by_chip(full, rows, cols, axis):
    if axis == 0:
        return full.reshape(N_CHIPS, rows // N_CHIPS, cols)
    return full.reshape(rows, N_CHIPS, cols // N_CHIPS).transpose(1, 0, 2)


def _from_chips(parts, axis):
    _, r, c = parts.shape
    if axis == 0:
        return parts.reshape(N_CHIPS * r, c)
    return parts.transpose(1, 0, 2).reshape(r, N_CHIPS * c)


def _adamw(wt, g, m, v, name):
    _, R, C = wt.shape
    tr = max(d for d in range(8, R + 1, 8) if R % d == 0 and (d * C <= 256 * 1024 or d == 8))

    def body(w_ref, g_ref, m_ref, v_ref, d_ref, nm_ref, nv_ref):
        gg = g_ref[...]
        m_new = ADAM_B1 * m_ref[...] + (1.0 - ADAM_B1) * gg
        v_new = ADAM_B2 * v_ref[...] + (1.0 - ADAM_B2) * (gg * gg)
        m_hat = m_new / (1.0 - ADAM_B1 ** ADAM_STEP)
        v_hat = v_new / (1.0 - ADAM_B2 ** ADAM_STEP)
        d_ref[...] = -ADAM_LR * (m_hat / (jnp.sqrt(v_hat) + ADAM_EPS) + ADAM_WD * w_ref[...])
        nm_ref[...] = m_new
        nv_ref[...] = v_new

    spec = pl.BlockSpec((None, tr, C), lambda i: (0, i, 0))
    return pl.pallas_call(
        body, name=name, grid=(R // tr,), in_specs=[spec, pl.BlockSpec((tr, C), lambda i: (i, 0)), spec, spec],
        out_specs=[spec] * 3, out_shape=[_sds((1, R, C), F32)] * 3,
        compiler_params=_cp(("parallel",)),
    )(wt, g, m, v)


def _pack_small(vals, loss_vec=None):
    rows = [jnp.pad(vals[n].reshape(-1), (0, PACK_COLS - sz)) for n, sz in SMALL]
    rows.append(loss_vec.reshape(-1) if loss_vec is not None else jnp.zeros((PACK_COLS,), F32))
    rows += [jnp.zeros((PACK_COLS,), F32)] * (SMALL_ROWS - len(rows))
    return jnp.stack(rows)


def kernel(x, p, positions, pre_mix_norm, w_in, ret_gn_w, mla_q_norm, w_uq, mla_kv_norm, w_ukv, w_o, post_mix_norm, pre_ffn_norm, w_gate, w_up, w_down, post_ffn_norm, w_ple_proj, ple_norm, w_ple_gate, b_ple_gate, loss_target, m_pre_mix_norm, m_w_in, m_ret_gn_w, m_mla_q_norm, m_w_uq, m_mla_kv_norm, m_w_ukv, m_w_o, m_post_mix_norm, m_pre_ffn_norm, m_w_gate, m_w_up, m_w_down, m_post_ffn_norm, m_w_ple_proj, m_ple_norm, m_w_ple_gate, m_b_ple_gate, v_pre_mix_norm, v_w_in, v_ret_gn_w, v_mla_q_norm, v_w_uq, v_mla_kv_norm, v_w_ukv, v_w_o, v_post_mix_norm, v_pre_ffn_norm, v_w_gate, v_w_up, v_w_down, v_post_ffn_norm, v_w_ple_proj, v_ple_norm, v_w_ple_gate, v_b_ple_gate):
    wts = dict(pre_mix_norm=pre_mix_norm, w_in=w_in, ret_gn_w=ret_gn_w, mla_q_norm=mla_q_norm, w_uq=w_uq,
               mla_kv_norm=mla_kv_norm, w_ukv=w_ukv, w_o=w_o, post_mix_norm=post_mix_norm, pre_ffn_norm=pre_ffn_norm,
               w_gate=w_gate, w_up=w_up, w_down=w_down, post_ffn_norm=post_ffn_norm, w_ple_proj=w_ple_proj,
               ple_norm=ple_norm, w_ple_gate=w_ple_gate, b_ple_gate=b_ple_gate)
    mom = dict(pre_mix_norm=m_pre_mix_norm, w_in=m_w_in, ret_gn_w=m_ret_gn_w, mla_q_norm=m_mla_q_norm, w_uq=m_w_uq,
               mla_kv_norm=m_mla_kv_norm, w_ukv=m_w_ukv, w_o=m_w_o, post_mix_norm=m_post_mix_norm,
               pre_ffn_norm=m_pre_ffn_norm, w_gate=m_w_gate, w_up=m_w_up, w_down=m_w_down, post_ffn_norm=m_post_ffn_norm,
               w_ple_proj=m_w_ple_proj, ple_norm=m_ple_norm, w_ple_gate=m_w_ple_gate, b_ple_gate=m_b_ple_gate)
    var = dict(pre_mix_norm=v_pre_mix_norm, w_in=v_w_in, ret_gn_w=v_ret_gn_w, mla_q_norm=v_mla_q_norm, w_uq=v_w_uq,
               mla_kv_norm=v_mla_kv_norm, w_ukv=v_w_ukv, w_o=v_w_o, post_mix_norm=v_post_mix_norm,
               pre_ffn_norm=v_pre_ffn_norm, w_gate=v_w_gate, w_up=v_w_up, w_down=v_w_down, post_ffn_norm=v_post_ffn_norm,
               w_ple_proj=v_w_ple_proj, ple_norm=v_ple_norm, w_ple_gate=v_w_ple_gate, b_ple_gate=v_b_ple_gate)

    S = x.shape[1]
    shard2d = {n: wts[n][0] for n, _, _, _ in BIG}
    small2d = {n: wts[n] for n, _ in SMALL}

    shard_bf = {n: (jnp.swapaxes(wts[n], 1, 2)[0] if n in GRAD_TRANSPOSED else shard2d[n]).astype(BF16) for n in shard2d}
    pos_f = positions.astype(F32).reshape(S, 1)
    c_idx = lax.axis_index("c").astype(jnp.int32).reshape(1)
    loss_vec, grad_x, gw, gs, (sums_early, parts_early) = _local_step(
        x[0], p[0, 0], pos_f, loss_target[0], {}, small2d, shard_bf, c_idx)

    g4 = [_by_chip(gw[n], *BIG_SPEC[n]) for n in REDUCE_LAST]
    got = _swap_half_rows(g4)
    sums_last = [_add_half_rows(g4[i], got[i], c_idx, "rs_add_halves_" + n) for i, n in enumerate(REDUCE_LAST)]
    parts_last, small_sum = _scatter_to_chips(sums_last, _pack_small(gs, loss_vec))
    place = jnp.stack([2 * lax.axis_index("x") + lax.axis_index("y"), lax.axis_index("c")]).astype(jnp.int32)
    names = REDUCE_EARLY + REDUCE_MID + REDUCE_LAST
    reduced = _join_half_rows(
        [_add_four(sm_, pt_, place, "rs_add_chips_" + n)
         for n, sm_, pt_ in zip(names, sums_early + sums_last, list(parts_early) + list(parts_last))])
    g_shard = dict(zip(names, reduced))

    loss = small_sum[9, 0]
    g_small = {n: small_sum[i:i + 1, :sz] for i, (n, sz) in enumerate(SMALL)}

    grads, delta, new_m, new_v = {}, {}, {}, {}
    for n, _, _, _ in BIG:
        if n in COLUMN_MAJOR:
            turn = lambda a: jnp.swapaxes(a, 1, 2)
            g_t = g_shard[n] if n in GRAD_TRANSPOSED else g_shard[n].T
            d, nm, nv = _adamw(turn(wts[n]), g_t, turn(mom[n]), turn(var[n]), "adamw_" + n)
            grads[n], delta[n], new_m[n], new_v[n] = turn(g_t[None]), turn(d), turn(nm), turn(nv)
        else:
            delta[n], new_m[n], new_v[n] = _adamw(wts[n], g_shard[n], mom[n], var[n], "adamw_" + n)
            grads[n] = g_shard[n][None]
    d, nm, nv = _adamw(_pack_small(small2d)[None], small_sum, _pack_small(mom)[None], _pack_small(var)[None],
                       "adamw_small")
    for i, (n, sz) in enumerate(SMALL):
        grads[n] = g_small[n]
        delta[n], new_m[n], new_v[n] = d[0, i:i + 1, :sz], nm[0, i:i + 1, :sz], nv[0, i:i + 1, :sz]

    return (loss, grad_x[None], *[grads[n] for n in ALL_W], *[delta[n] for n in ALL_W],
            *[new_m[n] for n in ALL_W], *[new_v[n] for n in ALL_W])
```

```python
import functools
import math

import jax
import jax.numpy as jnp
import numpy as np
from jax import lax
from jax.experimental import pallas as pl
from jax.experimental.pallas import tpu as pltpu

F32 = jnp.float32
BF16 = jnp.bfloat16
MESH = pl.DeviceIdType.MESH

D_MODEL = 1024
D_FF = 2816
PLE_DIM = 256
RET_HEADS = 4
RET_DIM = 128
RET_WIDTH = 512
RET_CHUNK = 256
RET_GROUP = 4
MLA_HEADS = 8
MLA_NOPE = 64
MLA_ROPE = 32
MLA_V = 64
Q_LORA = 384
KV_LORA = 256
IN_COLS = 2720
IN_COLS_P = 2816
ROPE_BASE = 10000.0
EPS = 1e-6
SCALE_MLA = 1.0 / math.sqrt(MLA_NOPE + MLA_ROPE)
SCALE_RET = RET_DIM ** -0.5
NEG = -1e30

ADAM_LR = 0.001
ADAM_B1 = 0.9
ADAM_B2 = 0.999
ADAM_EPS = 1e-08
ADAM_WD = 0.01
ADAM_STEP = 10

N_CHIPS = 4
N_DEV = 8
VMEM_MB = 56

BIG = (
    ("w_in", 1024, 2720, 1),
    ("w_uq", 384, 768, 1),
    ("w_ukv", 256, 1024, 1),
    ("w_o", 1024, 1024, 0),
    ("w_gate", 1024, 2816, 1),
    ("w_up", 1024, 2816, 1),
    ("w_down", 2816, 1024, 0),
    ("w_ple_proj", 256, 1024, 1),
    ("w_ple_gate", 1024, 1024, 0),
)
SMALL = (
    ("pre_mix_norm", 1024),
    ("ret_gn_w", 512),
    ("mla_q_norm", 384),
    ("mla_kv_norm", 256),
    ("post_mix_norm", 1024),
    ("pre_ffn_norm", 1024),
    ("post_ffn_norm", 1024),
    ("ple_norm", 1024),
    ("b_ple_gate", 1024),
)
ALL_W = ("pre_mix_norm", "w_in", "ret_gn_w", "mla_q_norm", "w_uq", "mla_kv_norm", "w_ukv", "w_o", "post_mix_norm",
         "pre_ffn_norm", "w_gate", "w_up", "w_down", "post_ffn_norm", "w_ple_proj", "ple_norm", "w_ple_gate", "b_ple_gate")
PACK_COLS = 1024
SMALL_ROWS = 16


def _cp(sem=None, mb=VMEM_MB, **kw):
    return pltpu.CompilerParams(dimension_semantics=sem, vmem_limit_bytes=mb * 1024 * 1024, **kw)


def _bf(x):
    return x.astype(BF16)


def _dot(a, b):
    return jnp.dot(_bf(a), _bf(b), preferred_element_type=F32)


def _dot_nt(a, b):
    return lax.dot_general(_bf(a), _bf(b), (((1,), (1,)), ((), ())), preferred_element_type=F32)


def _dot_tn(a, b):
    return lax.dot_general(_bf(a), _bf(b), (((0,), (0,)), ((), ())), preferred_element_type=F32)


def _sig(x):
    return 1.0 / (1.0 + jnp.exp(-x))


def _rms(x, g):
    r = lax.rsqrt(jnp.mean(x * x, axis=-1, keepdims=True) + EPS)
    return x * r * g


def _rms_bwd(dy, x, g):
    r = lax.rsqrt(jnp.mean(x * x, axis=-1, keepdims=True) + EPS)
    xh = x * r
    dxh = dy * g
    dx = r * (dxh - xh * jnp.mean(dxh * xh, axis=-1, keepdims=True))
    return dx, dy * xh


def _colsum(x):
    return jnp.sum(x, axis=0, keepdims=True)


def _rope_ret(x, cr, sr):
    return x * cr + pltpu.roll(x, 64, 1) * sr


def _unrope_ret(dy, cr, sr):
    return dy * cr + pltpu.roll(dy * sr, 64, 1)


def _rope_mla(x, cm, sa, sb):
    return x * cm + pltpu.roll(x, 112, 1) * sa + pltpu.roll(x, 16, 1) * sb


def _unrope_mla(dy, cm, sa, sb):
    return dy * cm + pltpu.roll(dy * sa, 16, 1) + pltpu.roll(dy * sb, 112, 1)


def _rows(tm, w, col=0):
    return pl.BlockSpec((tm, w), lambda i: (i, col))


def _full(*shape):
    return pl.BlockSpec(shape, lambda i: (0,) * len(shape), pipeline_mode=pl.Buffered(1))


def _acc(*shape):
    return pl.BlockSpec(shape, lambda i: (0,) * len(shape))


def _sds(shape, dtype):
    return jax.ShapeDtypeStruct(shape, dtype)


def _rope_tables(pos_f, S, shards=()):
    tm = min(512, S)
    n = len(shards)
    steps = S // tm
    inv_r = (1.0 / (np.float32(ROPE_BASE) ** (np.arange(64, dtype=np.float32) / np.float32(64)))).astype(np.float32)
    inv_m16 = (1.0 / (np.float32(ROPE_BASE) ** (np.arange(16, dtype=np.float32) / np.float32(16)))).astype(np.float32)
    inv_r = np.concatenate([inv_r, inv_r])[None, :]
    inv_m = np.zeros((1, 128), np.float32)
    inv_m[0, 64:80] = inv_m16
    inv_m[0, 80:96] = inv_m16

    def body(pos_ref, invr_ref, invm_ref, *rest):
        w_ins, (cr_ref, sr_ref, cm_ref, sa_ref, sb_ref) = rest[:n], rest[n:n + 5]
        w_outs, sems = rest[n + 5:2 * n + 5], rest[2 * n + 5:]
        i = pl.program_id(0)
        if n:
            @pl.when(i == 0)
            def _():
                _gather_phase(0, w_ins, w_outs, sems)

            @pl.when(i == steps - 1)
            def _():
                _gather_phase(1, w_ins, w_outs, sems)

        pos = pos_ref[...]
        lane = lax.broadcasted_iota(jnp.int32, (tm, 128), 1)
        ar = pos * invr_ref[...]
        s = jnp.sin(ar)
        cr_ref[...] = jnp.cos(ar)
        sr_ref[...] = jnp.where(lane < 64, -s, s)
        am = pos * invm_ref[...]
        c2 = jnp.cos(am)
        s2 = jnp.sin(am)
        cm_ref[...] = jnp.where(lane < 64, 1.0, jnp.where(lane < 96, c2, 0.0))
        sa_ref[...] = jnp.where((lane >= 64) & (lane < 80), -s2, 0.0)
        sb_ref[...] = jnp.where((lane >= 80) & (lane < 96), s2, 0.0)

        if n:
            @pl.when(i == steps - 1)
            def _():
                _gather_phase(2, w_ins, w_outs, sems)

    outs = pl.pallas_call(
        body, name="rope_tables", grid=(steps,),
        in_specs=[_rows(tm, 1), _full(1, 128), _full(1, 128)] + [_ANY] * n,
        out_specs=[_rows(tm, 128)] * 5 + [_ANY] * n,
        out_shape=[_sds((S, 128), F32)] * 5 + _gather_out_shapes(shards),
        scratch_shapes=_gather_sems(n) if n else [],
        compiler_params=_cp(("arbitrary",)),
    )(pos_f, jnp.asarray(inv_r), jnp.asarray(inv_m), *shards)
    return outs[:5], outs[5:]


def _inproj(x, g, w_in, tabs, S):
    tm = min(512, S)

    def body(x_ref, g_ref, w_ref, cr_ref, sr_ref, cm_ref, sa_ref, sb_ref,
             xn_ref, rq_ref, rk_ref, rv_ref, rg_ref, cq_ref, ckv_ref, kr_ref):
        xb = _rms(x_ref[...], g_ref[...]).astype(BF16)
        xn_ref[...] = xb
        cr = cr_ref[...]
        sr = sr_ref[...]
        q = jnp.dot(xb, w_ref[:, 0:512], preferred_element_type=F32)
        k = jnp.dot(xb, w_ref[:, 512:1024], preferred_element_type=F32)
        for h in range(RET_HEADS):
            sl = slice(h * 128, (h + 1) * 128)
            rq_ref[:, sl] = _rope_ret(q[:, sl], cr, sr).astype(BF16)
            rk_ref[:, sl] = (_rope_ret(k[:, sl], cr, sr) * SCALE_RET).astype(BF16)
        rv_ref[...] = jnp.dot(xb, w_ref[:, 1024:1536], preferred_element_type=F32).astype(BF16)
        rg_ref[...] = jnp.dot(xb, w_ref[:, 1536:2048], preferred_element_type=F32)
        cq_ref[...] = jnp.dot(xb, w_ref[:, 2048:2432], preferred_element_type=F32)
        ckv_ref[...] = jnp.dot(xb, w_ref[:, 2432:2688], preferred_element_type=F32)
        kr = pltpu.roll(jnp.dot(xb, w_ref[:, 2688:2816], preferred_element_type=F32), 64, 1)
        kr_ref[...] = _rope_mla(kr, cm_ref[...], sa_ref[...], sb_ref[...])

    return pl.pallas_call(
        body, name="inproj", grid=(S // tm,),
        in_specs=[_rows(tm, D_MODEL), _full(1, D_MODEL), _full(D_MODEL, IN_COLS_P)] + [_rows(tm, 128)] * 5,
        out_specs=[_rows(tm, D_MODEL)] + [_rows(tm, 512)] * 4 + [_rows(tm, Q_LORA), _rows(tm, KV_LORA), _rows(tm, 128)],
        out_shape=[_sds((S, D_MODEL), BF16)] + [_sds((S, 512), BF16)] * 3
        + [_sds((S, 512), F32), _sds((S, Q_LORA), F32), _sds((S, KV_LORA), F32), _sds((S, 128), F32)],
        compiler_params=_cp(("parallel",)),
    )(x, g, w_in, *tabs)


def _mla_up(cq, ckv, kr, gq, gkv, w_uq, w_ukv, tabs, S):
    tm = min(512, S)

    def body(cq_ref, ckv_ref, kr_ref, gq_ref, gkv_ref, wuq_ref, wukv_ref, cm_ref, sa_ref, sb_ref,
             cqn_ref, ckvn_ref, qp_ref, kp_ref, v_ref, kt_ref, vt_ref):
        cm = cm_ref[...]
        sa = sa_ref[...]
        sb = sb_ref[...]
        cqn = _rms(cq_ref[...], gq_ref[...]).astype(BF16)
        cqn_ref[...] = cqn
        ckvn = _rms(ckv_ref[...], gkv_ref[...]).astype(BF16)
        ckvn_ref[...] = ckvn
        qh = jnp.dot(cqn, wuq_ref[...], preferred_element_type=F32)
        kv = jnp.dot(ckvn, wukv_ref[...], preferred_element_type=F32)
        kr_blk = kr_ref[...]
        for h in range(MLA_HEADS):
            sl = slice(h * 128, (h + 1) * 128)
            qp_ref[:, sl] = (_rope_mla(qh[:, sl], cm, sa, sb) * SCALE_MLA).astype(BF16)
            kh = kv[:, sl] + kr_blk
            kp_ref[:, sl] = kh.astype(BF16)
            kt_ref[sl, :] = kh.T.astype(BF16)
        for h in range(MLA_HEADS // 2):
            vh = kv[:, 1024 + h * 128:1024 + (h + 1) * 128]
            v_ref[:, h * 128:(h + 1) * 128] = vh.astype(BF16)
            vt_ref[h * 128:(h + 1) * 128, :] = vh.T.astype(BF16)

    cols = lambda r: pl.BlockSpec((r, tm), lambda i: (0, i))
    return pl.pallas_call(
        body, name="mla_up", grid=(S // tm,),
        in_specs=[_rows(tm, Q_LORA), _rows(tm, KV_LORA), _rows(tm, 128), _full(1, Q_LORA), _full(1, KV_LORA),
                  _full(Q_LORA, 1024), _full(KV_LORA, 1536)] + [_rows(tm, 128)] * 3,
        out_specs=[_rows(tm, Q_LORA), _rows(tm, KV_LORA), _rows(tm, 1024), _rows(tm, 1024), _rows(tm, 512),
                   cols(1024), cols(512)],
        out_shape=[_sds((S, Q_LORA), BF16), _sds((S, KV_LORA), BF16), _sds((S, 1024), BF16), _sds((S, 1024), BF16),
                   _sds((S, 512), BF16), _sds((1024, S), BF16), _sds((512, S), BF16)],
        compiler_params=_cp(("parallel",)),
    )(cq, ckv, kr, gq, gkv, w_uq, w_ukv, *tabs[2:])


def _tri_pairs(nq, k_major):
    if k_major:
        pairs = [(qb, kb) for kb in range(nq) for qb in range(kb, nq)]
    else:
        pairs = [(qb, kb) for qb in range(nq) for kb in range(qb + 1)]
    qb_of = np.array([p[0] for p in pairs], np.int32)
    kb_of = np.array([p[1] for p in pairs], np.int32)
    return jnp.asarray(qb_of), jnp.asarray(kb_of), len(pairs)


ATT_ROWS = 32
FWD_HEADS = 8
BWD_HEADS = 4


def _causal_keep(r0, rows, tq):
    key = r0 + lax.broadcasted_iota(jnp.int32, (rows, tq), 0)
    qry = lax.broadcasted_iota(jnp.int32, (rows, tq), 1)
    return key <= qry


def _flash_fwd(qp, kp, vt, S, shards=()):
    tq = min(512, S)
    nq = S // tq
    RB = ATT_ROWS
    NH = FWD_HEADS
    qb_of, kb_of, T = _tri_pairs(nq, k_major=False)
    n = len(shards)
    steps = (MLA_HEADS // NH) * T

    def body(qb_ref, kb_ref, q_ref, k_ref, vt_ref, *rest):
        w_ins, (o_ref, lse_ref), w_outs = rest[:n], rest[n:n + 2], rest[n + 2:2 * n + 2]
        m_sc, l_sc, acc_sc, s_sc, p_sc = rest[2 * n + 2:2 * n + 7]
        sems = rest[2 * n + 7:]
        t = pl.program_id(1)
        qb = qb_ref[t]
        kb = kb_ref[t]
        lin = pl.program_id(0) * T + t

        if n:
            @pl.when(lin == 0)
            def _():
                _gather_phase(0, w_ins, w_outs, sems)

            @pl.when(lin == steps // 2)
            def _():
                _gather_phase(1, w_ins, w_outs, sems)

        @pl.when(kb == 0)
        def _():
            m_sc[...] = jnp.full(m_sc.shape, NEG, F32)
            l_sc[...] = jnp.zeros(l_sc.shape, F32)
            acc_sc[...] = jnp.zeros(acc_sc.shape, F32)

        def scores(a):
            sl = slice(a * 128, (a + 1) * 128)
            s_sc[a] = _dot_nt(k_ref[:, sl], q_ref[:, sl])

        def step(masked):
            for a in range(NH):
                scores(a)
            for a in range(NH):
                mx = [jnp.full((8, tq), NEG, F32) for _ in range(RB // 8)]
                for r in range(0, tq, RB):
                    sc = s_sc[a, r:r + RB, :]
                    if masked:
                        sc = jnp.where(_causal_keep(r, RB, tq), sc, NEG)
                        s_sc[a, r:r + RB, :] = sc
                    for i in range(RB // 8):
                        mx[i] = jnp.maximum(mx[i], sc[i * 8:(i + 1) * 8, :])
                mx8 = functools.reduce(jnp.maximum, mx)
                m_prev = m_sc[a]
                m_new = jnp.maximum(m_prev, jnp.max(mx8, axis=0, keepdims=True))
                al = jnp.exp(m_prev - m_new)
                m_sc[a] = m_new
                ls = [jnp.zeros((8, tq), F32) for _ in range(RB // 8)]
                for r in range(0, tq, RB):
                    p = jnp.exp(s_sc[a, r:r + RB, :] - m_new)
                    for i in range(RB // 8):
                        ls[i] = ls[i] + p[i * 8:(i + 1) * 8, :]
                    p_sc[a, r:r + RB, :] = p.astype(BF16)
                l_sc[a] = al * l_sc[a] + jnp.sum(functools.reduce(jnp.add, ls), axis=0, keepdims=True)
                pair = slice((a // 2) * 128, (a // 2 + 1) * 128)
                pv = jnp.dot(vt_ref[pair, :], p_sc[a], preferred_element_type=F32)
                rs = slice(a * 64, (a + 1) * 64)
                own = slice((a % 2) * 64, (a % 2 + 1) * 64)
                acc_sc[rs, :] = acc_sc[rs, :] * al + pv[own, :]

        @pl.when(kb < qb)
        def _():
            step(False)

        @pl.when(kb == qb)
        def _():
            step(True)
            for a in range(NH):
                rs = slice(a * 64, (a + 1) * 64)
                acc_sc[rs, :] = acc_sc[rs, :] / l_sc[a]
                lse_ref[a:a + 1, :] = m_sc[a] + jnp.log(l_sc[a])
            o_ref[...] = acc_sc[...].T.astype(BF16)

        if n:
            @pl.when(lin == steps - 1)
            def _():
                _gather_phase(2, w_ins, w_outs, sems)

    grid_spec = pltpu.PrefetchScalarGridSpec(
        num_scalar_prefetch=2, grid=(MLA_HEADS // NH, T),
        in_specs=[pl.BlockSpec((tq, 128 * NH), lambda j, t, qb, kb: (qb[t], j)),
                  pl.BlockSpec((tq, 128 * NH), lambda j, t, qb, kb: (kb[t], j)),
                  pl.BlockSpec((64 * NH, tq), lambda j, t, qb, kb: (j, kb[t]))] + [_ANY] * n,
        out_specs=[pl.BlockSpec((tq, 64 * NH), lambda j, t, qb, kb: (qb[t], j)),
                   pl.BlockSpec((None, NH, tq), lambda j, t, qb, kb: (j, 0, qb[t]))] + [_ANY] * n,
        scratch_shapes=[pltpu.VMEM((NH, 1, tq), F32), pltpu.VMEM((NH, 1, tq), F32), pltpu.VMEM((64 * NH, tq), F32),
                        pltpu.VMEM((NH, tq, tq), F32), pltpu.VMEM((NH, tq, tq), BF16)] + (_gather_sems(n) if n else []),
    )
    out, lse, *gathered = pl.pallas_call(
        body, name="flash_fwd", grid_spec=grid_spec,
        out_shape=[_sds((S, 512), BF16), _sds((MLA_HEADS // NH, NH, S), F32)] + _gather_out_shapes(shards),
        compiler_params=_cp(("arbitrary", "arbitrary")),
    )(qb_of, kb_of, qp, kp, vt, *shards)
    return out, lse.reshape(MLA_HEADS // 2, 2, S), gathered


def _decay_table():
    log_g = np.log(1.0 - 2.0 ** (-5.0 - np.arange(RET_HEADS, dtype=np.float32))).astype(np.float32)
    return jnp.asarray(np.broadcast_to(log_g[:, None, None], (RET_HEADS, 8, 128)).copy())


def _decay_terms(lg_ref):
    C = RET_CHUNK
    lg = lg_ref[0:1, :]
    row = lax.broadcasted_iota(jnp.int32, (C, C), 0)
    col = lax.broadcasted_iota(jnp.int32, (C, C), 1)
    diff = (row - col).astype(F32)
    dmat = jnp.where(diff >= 0, jnp.exp(jnp.maximum(diff, 0.0) * jnp.tile(lg, (1, C // 128))), 0.0)
    j = lax.broadcasted_iota(jnp.int32, (C, 1), 0).astype(F32)
    lg1 = lg[:, 0:1]
    zeta = jnp.exp((C - 1 - j) * lg1)
    xi = jnp.exp((j + 1.0) * lg1)
    g_chunk = jnp.exp(C * lg1)
    return dmat, zeta, xi, g_chunk


def _ret_fwd(rq, rk, rv, rg, gn_w, S):
    C = RET_CHUNK
    N = S // C
    G = min(RET_GROUP, N)
    NB = N // G

    def body(lg_ref, q_ref, k_ref, v_ref, rg_ref, w_ref, ry_ref, ro_ref, rprev_ref, r_sc):
        @pl.when(pl.program_id(1) == 0)
        def _():
            r_sc[...] = jnp.zeros(r_sc.shape, F32)

        dmat, zeta, xi, g_chunk = _decay_terms(lg_ref)
        w = w_ref[...]
        r = r_sc[...]
        for i in range(G):
            rows = slice(i * C, (i + 1) * C)
            q = q_ref[rows, :]
            k = k_ref[rows, :]
            v = v_ref[rows, :]
            r_prev = r.astype(BF16)
            rprev_ref[i] = r_prev
            sc = _dot_nt(q, k) * dmat
            ry = _dot(sc, v) + jnp.dot(q, r_prev, preferred_element_type=F32) * xi
            ry_ref[rows, :] = ry
            r = g_chunk * r + _dot_tn(k, zeta * v.astype(F32))
            mu = jnp.mean(ry, axis=-1, keepdims=True)
            yc = ry - mu
            yh = yc * lax.rsqrt(jnp.mean(yc * yc, axis=-1, keepdims=True) + EPS)
            g = rg_ref[rows, :]
            ro_ref[rows, :] = (g * _sig(g) * (yh * w)).astype(BF16)
        r_sc[...] = r

    blk = pl.BlockSpec((G * C, 128), lambda h, n: (n, h))
    return pl.pallas_call(
        body, name="ret_fwd", grid=(RET_HEADS, NB),
        in_specs=[pl.BlockSpec((None, 8, 128), lambda h, n: (h, 0, 0)), blk, blk, blk, blk,
                  pl.BlockSpec((1, 128), lambda h, n: (0, h))],
        out_specs=[blk, blk, pl.BlockSpec((G, 128, 128), lambda h, n: (h * NB + n, 0, 0))],
        out_shape=[_sds((S, 512), F32), _sds((S, 512), BF16), _sds((RET_HEADS * N, 128, 128), BF16)],
        scratch_shapes=[pltpu.VMEM((128, 128), F32)],
        compiler_params=_cp(("parallel", "arbitrary")),
    )(_decay_table(), rq, rk, rv, rg, gn_w)


def _outproj(ro, mo, x, w_o, g_post, g_pre, S):
    tm = min(512, S)

    def body(ro_ref, mo_ref, x_ref, wo_ref, g1_ref, g2_ref, mix_ref, h1_ref, hn_ref):
        mix = (jnp.dot(ro_ref[...], wo_ref[0:512, :], preferred_element_type=F32)
               + jnp.dot(mo_ref[...], wo_ref[512:1024, :], preferred_element_type=F32))
        mix_ref[...] = mix.astype(BF16)
        h1 = x_ref[...] + _rms(mix, g1_ref[...])
        h1_ref[...] = h1
        hn_ref[...] = _rms(h1, g2_ref[...]).astype(BF16)

    return pl.pallas_call(
        body, name="outproj", grid=(S // tm,),
        in_specs=[_rows(tm, 512), _rows(tm, 512), _rows(tm, D_MODEL), _full(D_MODEL, D_MODEL), _full(1, D_MODEL),
                  _full(1, D_MODEL)],
        out_specs=[_rows(tm, D_MODEL)] * 3,
        out_shape=[_sds((S, D_MODEL), BF16), _sds((S, D_MODEL), F32), _sds((S, D_MODEL), BF16)],
        compiler_params=_cp(("parallel",)),
    )(ro, mo, x, w_o, g_post, g_pre)


def _ffn_up(hn, w_gate_t, w_up_t, S):
    tm = min(512, S)
    tn = D_FF // 2

    def body(hn_ref, wg_ref, wu_ref, fg_ref, fu_ref, act_ref):
        hn_b = hn_ref[...]
        g = _dot_nt(hn_b, wg_ref[...])
        u = _dot_nt(hn_b, wu_ref[...])
        s = _sig(g)
        silu = g * s
        fg_ref[...] = (u * (s + silu * (1.0 - s))).astype(BF16)
        fu_ref[...] = silu.astype(BF16)
        act_ref[...] = (silu * u).astype(BF16)

    wspec = pl.BlockSpec((tn, D_MODEL), lambda j, i: (j, 0))
    ospec = pl.BlockSpec((tm, tn), lambda j, i: (i, j))
    return pl.pallas_call(
        body, name="ffn_up", grid=(2, S // tm),
        in_specs=[pl.BlockSpec((tm, D_MODEL), lambda j, i: (i, 0)), wspec, wspec],
        out_specs=[ospec] * 3, out_shape=[_sds((S, D_FF), BF16)] * 3,
        compiler_params=_cp(("parallel", "parallel")),
    )(hn, w_gate_t, w_up_t)


def _ffn_down(act, w_down, h1, g, S):
    tm = min(512, S)

    def body(act_ref, wd_ref, h1_ref, g_ref, ff_ref, h2_ref):
        ff = jnp.dot(act_ref[...], wd_ref[...], preferred_element_type=F32)
        ff_ref[...] = ff.astype(BF16)
        h2_ref[...] = h1_ref[...] + _rms(ff, g_ref[...])

    return pl.pallas_call(
        body, name="ffn_down", grid=(S // tm,),
        in_specs=[_rows(tm, D_FF), _full(D_FF, D_MODEL), _rows(tm, D_MODEL), _full(1, D_MODEL)],
        out_specs=[_rows(tm, D_MODEL)] * 2, out_shape=[_sds((S, D_MODEL), BF16), _sds((S, D_MODEL), F32)],
        compiler_params=_cp(("parallel",)),
    )(act, w_down, h1, g)


def _ple_loss(p, h2, tgt, w_pp, w_pg, b_pg, g_ple, S):
    tm = min(512, S)

    def body(p_ref, h2_ref, t_ref, wp_ref, wg_ref, b_ref, gp_ref,
             dz_ref, dpe_ref, dh2_ref, h2b_ref, loss_ref, dgp_ref, db_ref):
        @pl.when(pl.program_id(0) == 0)
        def _():
            loss_ref[...] = jnp.zeros(loss_ref.shape, F32)
            dgp_ref[...] = jnp.zeros(dgp_ref.shape, F32)
            db_ref[...] = jnp.zeros(db_ref.shape, F32)

        gp = gp_ref[...]
        pe = _dot(p_ref[...], wp_ref[...])
        r = lax.rsqrt(jnp.mean(pe * pe, axis=-1, keepdims=True) + EPS)
        peh = pe * r
        e = peh * gp
        h2 = h2_ref[...]
        h2b = h2.astype(BF16)
        h2b_ref[...] = h2b
        gt = _sig(jnp.dot(h2b, wg_ref[...], preferred_element_type=F32) + b_ref[...])
        diff = h2 + e * gt - t_ref[...]
        loss_ref[...] += _colsum(diff * diff)
        dh3 = diff * (1.0 / D_MODEL)
        de = dh3 * gt
        dz = dh3 * e * gt * (1.0 - gt)
        db_ref[...] += _colsum(dz)
        dgp_ref[...] += _colsum(de * peh)
        dpeh = de * gp
        dpe = r * (dpeh - peh * jnp.mean(dpeh * peh, axis=-1, keepdims=True))
        dzb = dz.astype(BF16)
        dz_ref[...] = dzb
        dpe_ref[...] = dpe.astype(BF16)
        dh2_ref[...] = dh3 + _dot_nt(dzb, wg_ref[...])

    return pl.pallas_call(
        body, name="ple_loss", grid=(S // tm,),
        in_specs=[_rows(tm, PLE_DIM), _rows(tm, D_MODEL), _rows(tm, D_MODEL), _full(PLE_DIM, D_MODEL),
                  _full(D_MODEL, D_MODEL), _full(1, D_MODEL), _full(1, D_MODEL)],
        out_specs=[_rows(tm, D_MODEL)] * 4 + [_acc(1, D_MODEL)] * 3,
        out_shape=[_sds((S, D_MODEL), BF16), _sds((S, D_MODEL), BF16), _sds((S, D_MODEL), F32), _sds((S, D_MODEL), BF16)]
        + [_sds((1, D_MODEL), F32)] * 3,
        compiler_params=_cp(("arbitrary",)),
    )(p, h2, tgt, w_pp, w_pg, b_pg, g_ple)


def _wgrad(a, b, name, S):
    M = a.shape[1]
    N = b.shape[1]
    ts = min(2048, S)
    nsplit = 2 if M * N >= 2 * 1024 * 1024 else 1
    tn = N // nsplit

    def body(a_ref, b_ref, o_ref):
        @pl.when(pl.program_id(1) == 0)
        def _():
            o_ref[...] = jnp.zeros(o_ref.shape, F32)

        o_ref[...] += _dot_tn(a_ref[...], b_ref[...])

    return pl.pallas_call(
        body, name=name, grid=(nsplit, S // ts),
        in_specs=[pl.BlockSpec((ts, M), lambda j, s: (s, 0)), pl.BlockSpec((ts, tn), lambda j, s: (s, j))],
        out_specs=pl.BlockSpec((M, tn), lambda j, s: (0, j)), out_shape=_sds((M, N), F32),
        compiler_params=_cp(("parallel", "arbitrary")),
    )(a, b)


def _ffn_down_bwd(dh2, ff, g, w_down, dgate_f, dup_f, S):
    tm = min(512, S)
    tn = D_FF // 2

    def body(dh2_ref, ff_ref, g_ref, wd_ref, fg_ref, fu_ref, dff_ref, dgate_ref, dup_ref, dg_ref):
        @pl.when(pl.program_id(0) == 0)
        def _():
            dg_ref[...] = jnp.zeros(dg_ref.shape, F32)

        dff, ga = _rms_bwd(dh2_ref[...], ff_ref[...].astype(F32), g_ref[...])
        dg_ref[...] += _colsum(ga)
        dffb = dff.astype(BF16)
        dff_ref[...] = dffb
        for seg in range(2):
            sl = slice(seg * tn, (seg + 1) * tn)
            dact = _dot_nt(dffb, wd_ref[sl, :])
            dgate_ref[:, sl] = (dact * fg_ref[:, sl].astype(F32)).astype(BF16)
            dup_ref[:, sl] = (dact * fu_ref[:, sl].astype(F32)).astype(BF16)

    return pl.pallas_call(
        body, name="ffn_down_bwd", grid=(S // tm,),
        in_specs=[_rows(tm, D_MODEL), _rows(tm, D_MODEL), _full(1, D_MODEL), _full(D_FF, D_MODEL), _rows(tm, D_FF),
                  _rows(tm, D_FF)],
        out_specs=[_rows(tm, D_MODEL), _rows(tm, D_FF), _rows(tm, D_FF), _acc(1, D_MODEL)],
        out_shape=[_sds((S, D_MODEL), BF16), _sds((S, D_FF), BF16), _sds((S, D_FF), BF16), _sds((1, D_MODEL), F32)],
        compiler_params=_cp(("arbitrary",)),
    )(dh2, ff, g, w_down, dgate_f, dup_f)


def _ffn_up_bwd(dgate, dup, w_gate, w_up, h1, mix, dh2, g_pre, g_post, w_o, S, grads=()):
    tm = min(512, S)
    n = len(grads)
    last = S // tm - 1

    def body(dgate_ref, dup_ref, wg_ref, wu_ref, h1_ref, mix_ref, dh2_ref, g2_ref, g1_ref, wo_ref, *rest):
        g_ins = rest[:n]
        dh1_ref, dmix_ref, dro_ref, dmo_ref, dg2_ref, dg1_ref = rest[n:n + 6]
        g_outs, sems = rest[n + 6:2 * n + 6], rest[2 * n + 6:]

        @pl.when(pl.program_id(0) == 0)
        def _():
            dg2_ref[...] = jnp.zeros(dg2_ref.shape, F32)
            dg1_ref[...] = jnp.zeros(dg1_ref.shape, F32)
            for cp in (_swap_copies(g_ins, g_outs, sems) if n else []):
                cp.start()

        dhn = (jnp.dot(dgate_ref[...], wg_ref[...], preferred_element_type=F32)
               + jnp.dot(dup_ref[...], wu_ref[...], preferred_element_type=F32))
        d1, ga = _rms_bwd(dhn, h1_ref[...], g2_ref[...])
        dg2_ref[...] += _colsum(ga)
        dh1 = dh2_ref[...] + d1
        dh1_ref[...] = dh1
        dmix, gb = _rms_bwd(dh1, mix_ref[...].astype(F32), g1_ref[...])
        dg1_ref[...] += _colsum(gb)
        dmixb = dmix.astype(BF16)
        dmix_ref[...] = dmixb
        dcat = _dot_nt(dmixb, wo_ref[...])
        dro_ref[...] = dcat[:, 0:512].astype(BF16)
        dmo_ref[...] = dcat[:, 512:1024].astype(BF16)

        if n:
            @pl.when(pl.program_id(0) == last)
            def _():
                for cp in _swap_copies(g_ins, g_outs, sems):
                    cp.wait()

    dh1, dmix, dro, dmo, dg2, dg1, *got = pl.pallas_call(
        body, name="ffn_up_bwd", grid=(S // tm,),
        in_specs=[_rows(tm, D_FF), _rows(tm, D_FF), _full(D_FF, D_MODEL), _full(D_FF, D_MODEL), _rows(tm, D_MODEL),
                  _rows(tm, D_MODEL), _rows(tm, D_MODEL), _full(1, D_MODEL), _full(1, D_MODEL), _full(D_MODEL, D_MODEL)]
        + [_ANY] * n,
        out_specs=[_rows(tm, D_MODEL), _rows(tm, D_MODEL), _rows(tm, 512), _rows(tm, 512), _acc(1, D_MODEL),
                   _acc(1, D_MODEL)] + [_ANY] * n,
        out_shape=[_sds((S, D_MODEL), F32), _sds((S, D_MODEL), BF16), _sds((S, 512), BF16), _sds((S, 512), BF16),
                   _sds((1, D_MODEL), F32), _sds((1, D_MODEL), F32)] + _swap_out_shapes(grads),
        scratch_shapes=_swap_sems(n) if n else [],
        compiler_params=_cp(("arbitrary",)),
    )(dgate, dup, w_gate, w_up, h1, mix, dh2, g_pre, g_post, w_o, *grads)
    return dh1, dmix, dro, dmo, dg2, dg1, got


def _attn_delta(o, do, S, grads=()):
    tm = min(512, S)
    n = len(grads)
    last = S // tm - 1

    def body(o_ref, do_ref, *rest):
        g_ins, (dot_ref, d_ref), g_outs, sems = rest[:n], rest[n:n + 2], rest[n + 2:2 * n + 2], rest[2 * n + 2:]
        if n:
            @pl.when(pl.program_id(0) == 0)
            def _():
                for cp in _swap_copies(g_ins, g_outs, sems):
                    cp.start()

        do = do_ref[...].astype(F32)
        prod_t = (o_ref[...].astype(F32) * do).T
        dot_ref[...] = do.T.astype(BF16)
        for h in range(MLA_HEADS):
            d_ref[h // 2, (h % 2):(h % 2) + 1, :] = jnp.sum(prod_t[h * 64:(h + 1) * 64, :], axis=0, keepdims=True)

        if n:
            @pl.when(pl.program_id(0) == last)
            def _():
                for cp in _swap_copies(g_ins, g_outs, sems):
                    cp.wait()

    dot, delta, *got = pl.pallas_call(
        body, name="attn_delta", grid=(S // tm,),
        in_specs=[_rows(tm, 512), _rows(tm, 512)] + [_ANY] * n,
        out_specs=[pl.BlockSpec((512, tm), lambda i: (0, i)), pl.BlockSpec((MLA_HEADS // 2, 2, tm), lambda i: (0, 0, i))]
        + [_ANY] * n,
        out_shape=[_sds((512, S), BF16), _sds((MLA_HEADS // 2, 2, S), F32)] + _swap_out_shapes(grads),
        scratch_shapes=_swap_sems(n) if n else [],
        compiler_params=_cp(("arbitrary",)),
    )(o, do, *grads)
    return dot, delta, got


def _flash_bwd(qp, kp, kt, v, do, dot, lse, delta, S, sums=()):
    tq = min(512, S)
    nq = S // tq
    RB = ATT_ROWS
    NH = BWD_HEADS
    qb_of, kb_of, T = _tri_pairs(nq, k_major=True)
    n = len(sums)
    steps = (MLA_HEADS // NH) * T

    def body(qb_ref, kb_ref, q_ref, k_ref, kt_ref, v_ref, do_ref, dot_ref, lse_ref, dl_ref, *rest):
        g_ins, (dq_ref, dk_ref, dv_ref), g_outs = rest[:n], rest[n:n + 3], rest[n + 3:2 * n + 3]
        dk_sc, dv_sc, s_sc, dp_sc, p_sc, ds_sc = rest[2 * n + 3:2 * n + 9]
        sems = rest[2 * n + 9:]
        t = pl.program_id(1)
        qb = qb_ref[t]
        kb = kb_ref[t]
        lin = pl.program_id(0) * T + t

        if n:
            @pl.when(lin == 0)
            def _():
                for cp in _scatter_copies(g_ins, g_outs, sems):
                    cp.start()

        @pl.when(t == 0)
        def _():
            dq_ref[...] = jnp.zeros(dq_ref.shape, F32)

        @pl.when(qb == kb)
        def _():
            dk_sc[...] = jnp.zeros(dk_sc.shape, F32)
            dv_sc[...] = jnp.zeros(dv_sc.shape, F32)

        lane = lax.broadcasted_iota(jnp.int32, (tq, 64 * NH), 1)

        def step(masked):
            vv = v_ref[...]
            do_all = do_ref[...]
            mine = [(lane >= a * 64) & (lane < (a + 1) * 64) for a in range(NH)]
            for a in range(NH):
                sl = slice(a * 128, (a + 1) * 128)
                s_sc[a] = _dot_nt(k_ref[:, sl], q_ref[:, sl])
                dp_sc[a] = jnp.dot(jnp.where(mine[a], vv, jnp.zeros_like(vv)), dot_ref[...],
                                   preferred_element_type=F32)
            for a in range(NH):
                sl = slice(a * 128, (a + 1) * 128)
                lse = lse_ref[a:a + 1, :]
                dl = dl_ref[a:a + 1, :]
                for r in range(0, tq, RB):
                    sc = s_sc[a, r:r + RB, :]
                    if masked:
                        sc = jnp.where(_causal_keep(r, RB, tq), sc, NEG)
                    p = jnp.exp(sc - lse)
                    p_sc[a, r:r + RB, :] = p.astype(BF16)
                    ds_sc[a, r:r + RB, :] = (p * (dp_sc[a, r:r + RB, :] - dl)).astype(BF16)
                ds = ds_sc[a]
                dv_sc[...] += jnp.dot(p_sc[a], jnp.where(mine[a], do_all, jnp.zeros_like(do_all)),
                                      preferred_element_type=F32)
                dk_sc[:, sl] += jnp.dot(ds, q_ref[:, sl], preferred_element_type=F32)
                dq_ref[qb, sl, :] += jnp.dot(kt_ref[sl, :], ds, preferred_element_type=F32)

        @pl.when(qb > kb)
        def _():
            step(False)

        @pl.when(qb == kb)
        def _():
            step(True)

        @pl.when(qb == nq - 1)
        def _():
            dk_ref[...] = dk_sc[...].astype(BF16)
            dv_ref[...] = dv_sc[...].astype(BF16)

        if n:
            @pl.when(lin == steps - 1)
            def _():
                for cp in _scatter_copies(g_ins, g_outs, sems):
                    cp.wait()

    grid_spec = pltpu.PrefetchScalarGridSpec(
        num_scalar_prefetch=2, grid=(MLA_HEADS // NH, T),
        in_specs=[pl.BlockSpec((tq, 128 * NH), lambda j, t, qb, kb: (qb[t], j)),
                  pl.BlockSpec((tq, 128 * NH), lambda j, t, qb, kb: (kb[t], j)),
                  pl.BlockSpec((128 * NH, tq), lambda j, t, qb, kb: (j, kb[t])),
                  pl.BlockSpec((tq, 64 * NH), lambda j, t, qb, kb: (kb[t], j)),
                  pl.BlockSpec((tq, 64 * NH), lambda j, t, qb, kb: (qb[t], j)),
                  pl.BlockSpec((64 * NH, tq), lambda j, t, qb, kb: (j, qb[t])),
                  pl.BlockSpec((None, NH, tq), lambda j, t, qb, kb: (j, 0, qb[t])),
                  pl.BlockSpec((None, NH, tq), lambda j, t, qb, kb: (j, 0, qb[t]))] + [_ANY] * n,
        out_specs=[pl.BlockSpec((nq, 128 * NH, tq), lambda j, t, qb, kb: (0, j, 0), pipeline_mode=pl.Buffered(1)),
                   pl.BlockSpec((tq, 128 * NH), lambda j, t, qb, kb: (kb[t], j)),
                   pl.BlockSpec((tq, 64 * NH), lambda j, t, qb, kb: (kb[t], j))] + [_ANY] * n,
        scratch_shapes=[pltpu.VMEM((tq, 128 * NH), F32), pltpu.VMEM((tq, 64 * NH), F32), pltpu.VMEM((NH, tq, tq), F32),
                        pltpu.VMEM((NH, tq, tq), F32), pltpu.VMEM((NH, tq, tq), BF16), pltpu.VMEM((NH, tq, tq), BF16)]
        + (_scatter_sems(n) if n else []),
    )
    dq, dk, dv, *parts = pl.pallas_call(
        body, name="flash_bwd", grid_spec=grid_spec,
        out_shape=[_sds((nq, 1024, tq), F32), _sds((S, 1024), BF16), _sds((S, 512), BF16)] + _scatter_out_shapes(sums),
        compiler_params=_cp(("arbitrary", "arbitrary")),
    )(qb_of, kb_of, qp, kp, kt, v, do, dot, lse.reshape(MLA_HEADS // NH, NH, S), delta.reshape(MLA_HEADS // NH, NH, S),
      *sums)
    return dq, dk, dv, parts


def _mla_up_bwd(dqp, dkp, dv, cq, ckv, gq, gkv, w_uq, w_ukv, tabs, S):
    tm = min(512, S)

    def body(dq_ref, dk_ref, dv_ref, cq_ref, ckv_ref, gq_ref, gkv_ref, wuq_ref, wukv_ref, cm_ref, sa_ref, sb_ref,
             dqh_ref, dkv_ref, dcq_ref, dckv_ref, dkr_ref, dgq_ref, dgkv_ref):
        @pl.when(pl.program_id(0) == 0)
        def _():
            dgq_ref[...] = jnp.zeros(dgq_ref.shape, F32)
            dgkv_ref[...] = jnp.zeros(dgkv_ref.shape, F32)

        cm = cm_ref[...]
        sa = sa_ref[...]
        sb = sb_ref[...]
        lane = lax.broadcasted_iota(jnp.int32, (tm, 128), 1)
        dkr_r = jnp.zeros((tm, 128), F32)
        for h in range(MLA_HEADS):
            sl = slice(h * 128, (h + 1) * 128)
            dqh_ref[:, sl] = (_unrope_mla(dq_ref[sl, :].T, cm, sa, sb) * SCALE_MLA).astype(BF16)
            gk = dk_ref[:, sl]
            dkr_r = dkr_r + gk.astype(F32)
            dkv_ref[:, sl] = gk
        dkr_r = jnp.where((lane >= 64) & (lane < 96), dkr_r, 0.0)
        dkr_ref[...] = _unrope_mla(dkr_r, cm, sa, sb).astype(BF16)
        dkv_ref[:, 1024:1536] = dv_ref[...]
        dcq, ga = _rms_bwd(_dot_nt(dqh_ref[...], wuq_ref[...]), cq_ref[...], gq_ref[...])
        dcq_ref[...] = dcq.astype(BF16)
        dgq_ref[...] += _colsum(ga)
        dckv, gb = _rms_bwd(_dot_nt(dkv_ref[...], wukv_ref[...]), ckv_ref[...], gkv_ref[...])
        dckv_ref[...] = dckv.astype(BF16)
        dgkv_ref[...] += _colsum(gb)

    per_q = dqp.shape[2] // tm
    return pl.pallas_call(
        body, name="mla_up_bwd", grid=(S // tm,),
        in_specs=[pl.BlockSpec((None, 1024, tm), lambda i: (i // per_q, 0, i % per_q)),
                  _rows(tm, 1024), _rows(tm, 512), _rows(tm, Q_LORA), _rows(tm, KV_LORA),
                  _full(1, Q_LORA), _full(1, KV_LORA), _full(Q_LORA, 1024), _full(KV_LORA, 1536)] + [_rows(tm, 128)] * 3,
        out_specs=[_rows(tm, 1024), _rows(tm, 1536), _rows(tm, Q_LORA), _rows(tm, KV_LORA), _rows(tm, 128),
                   _acc(1, Q_LORA), _acc(1, KV_LORA)],
        out_shape=[_sds((S, 1024), BF16), _sds((S, 1536), BF16), _sds((S, Q_LORA), BF16), _sds((S, KV_LORA), BF16),
                   _sds((S, 128), BF16), _sds((1, Q_LORA), F32), _sds((1, KV_LORA), F32)],
        compiler_params=_cp(("arbitrary",)),
    )(dqp, dkp, dv, cq, ckv, gq, gkv, w_uq, w_ukv, *tabs[2:])


def _ret_bwd(rq, rk, rv, rprev, ry, rg, dro, gn_w, tabs, S):
    C = RET_CHUNK
    N = S // C
    G = min(RET_GROUP, N)
    NB = N // G

    def body(lg_ref, q_ref, k_ref, v_ref, rp_ref, ry_ref, rg_ref, dro_ref, w_ref, cr_ref, sr_ref,
             drq_ref, drk_ref, drv_ref, drg_ref, dw_ref, g_sc):
        @pl.when(pl.program_id(1) == 0)
        def _():
            g_sc[...] = jnp.zeros(g_sc.shape, F32)
            dw_ref[...] = jnp.zeros(dw_ref.shape, F32)

        dmat, zeta, xi, g_chunk = _decay_terms(lg_ref)
        w = w_ref[...]
        gacc = g_sc[...]
        dw = jnp.zeros((1, 128), F32)
        for i in reversed(range(G)):
            rows = slice(i * C, (i + 1) * C)
            ry = ry_ref[rows, :]
            mu = jnp.mean(ry, axis=-1, keepdims=True)
            yc = ry - mu
            rstd = lax.rsqrt(jnp.mean(yc * yc, axis=-1, keepdims=True) + EPS)
            yh = yc * rstd
            g = rg_ref[rows, :]
            s = _sig(g)
            dout = dro_ref[rows, :].astype(F32)
            drg_ref[rows, :] = (dout * (yh * w) * (s * (1.0 + g * (1.0 - s)))).astype(BF16)
            dgn = dout * (g * s)
            dw = dw + _colsum(dgn * yh)
            dyh = dgn * w
            dry = rstd * (dyh - jnp.mean(dyh, axis=-1, keepdims=True) - yh * jnp.mean(dyh * yh, axis=-1, keepdims=True))
            do = dry.astype(BF16)

            q = q_ref[rows, :]
            k = k_ref[rows, :]
            v = v_ref[rows, :]
            gfut = gacc.astype(BF16)
            sc = (_dot_nt(q, k) * dmat).astype(BF16)
            dsc = (_dot_nt(do, v) * dmat).astype(BF16)
            dq = jnp.dot(dsc, k, preferred_element_type=F32) + _dot_nt(do, rp_ref[i]) * xi
            dk = _dot_tn(dsc, q) + _dot_nt(v, gfut) * zeta
            dv = _dot_tn(sc, do) + jnp.dot(k, gfut, preferred_element_type=F32) * zeta
            gacc = g_chunk * gacc + _dot_tn(q, xi * dry)
            cr = cr_ref[rows, :]
            sr = sr_ref[rows, :]
            drq_ref[rows, :] = _unrope_ret(dq, cr, sr).astype(BF16)
            drk_ref[rows, :] = _unrope_ret(dk * SCALE_RET, cr, sr).astype(BF16)
            drv_ref[rows, :] = dv.astype(BF16)
        g_sc[...] = gacc
        dw_ref[...] += dw

    blk = pl.BlockSpec((G * C, 128), lambda h, n: (NB - 1 - n, h))
    tab = pl.BlockSpec((G * C, 128), lambda h, n: (NB - 1 - n, 0))
    return pl.pallas_call(
        body, name="ret_bwd", grid=(RET_HEADS, NB),
        in_specs=[pl.BlockSpec((None, 8, 128), lambda h, n: (h, 0, 0)), blk, blk, blk,
                  pl.BlockSpec((G, 128, 128), lambda h, n: (h * NB + NB - 1 - n, 0, 0)), blk, blk, blk,
                  pl.BlockSpec((1, 128), lambda h, n: (0, h)), tab, tab],
        out_specs=[blk, blk, blk, blk, pl.BlockSpec((1, 128), lambda h, n: (0, h))],
        out_shape=[_sds((S, 512), BF16)] * 4 + [_sds((1, 512), F32)],
        scratch_shapes=[pltpu.VMEM((128, 128), F32)],
        compiler_params=_cp(("parallel", "arbitrary")),
    )(_decay_table(), rq, rk, rv, rprev, ry, rg, dro, gn_w, tabs[0], tabs[1])


def _inproj_bwd(drq, drk, drv, drg, dcq, dckv, dkr, w_in, dh1, x, g, S):
    tm = min(512, S)

    def body(drq_ref, drk_ref, drv_ref, drg_ref, dcq_ref, dckv_ref, dkr_ref, w_ref, dh1_ref, x_ref, g_ref,
             gx_ref, dproj_ref, dg_ref):
        @pl.when(pl.program_id(0) == 0)
        def _():
            dg_ref[...] = jnp.zeros(dg_ref.shape, F32)

        dproj_ref[:, 0:512] = drq_ref[...]
        dproj_ref[:, 512:1024] = drk_ref[...]
        dproj_ref[:, 1024:1536] = drv_ref[...]
        dproj_ref[:, 1536:2048] = drg_ref[...]
        dproj_ref[:, 2048:2432] = dcq_ref[...]
        dproj_ref[:, 2432:2688] = dckv_ref[...]
        dproj_ref[:, 2688:2816] = pltpu.roll(dkr_ref[...].astype(F32), 64, 1).astype(BF16)
        dx, ga = _rms_bwd(_dot_nt(dproj_ref[...], w_ref[...]), x_ref[...], g_ref[...])
        gx_ref[...] = dh1_ref[...] + dx
        dg_ref[...] += _colsum(ga)

    return pl.pallas_call(
        body, name="inproj_bwd", grid=(S // tm,),
        in_specs=[_rows(tm, 512)] * 4 + [_rows(tm, Q_LORA), _rows(tm, KV_LORA), _rows(tm, 128),
                                         _full(D_MODEL, IN_COLS_P), _rows(tm, D_MODEL), _rows(tm, D_MODEL),
                                         _full(1, D_MODEL)],
        out_specs=[_rows(tm, D_MODEL), _rows(tm, IN_COLS_P), _acc(1, D_MODEL)],
        out_shape=[_sds((S, D_MODEL), F32), _sds((S, IN_COLS_P), BF16), _sds((1, D_MODEL), F32)],
        compiler_params=_cp(("arbitrary",)),
    )(drq, drk, drv, drg, dcq, dckv, dkr, w_in, dh1, x, g)


def _pad_weights(w):
    w_in_p = jnp.pad(w["w_in"], ((0, 0), (0, IN_COLS_P - IN_COLS)))
    w_uq_p = jnp.pad(w["w_uq"].reshape(Q_LORA, MLA_HEADS, 96), ((0, 0), (0, 0), (0, 32))).reshape(Q_LORA, 1024)
    ukv = w["w_ukv"].reshape(KV_LORA, MLA_HEADS, 128)
    k_part = jnp.pad(ukv[:, :, :64], ((0, 0), (0, 0), (0, 64))).reshape(KV_LORA, 1024)
    w_ukv_p = jnp.concatenate([k_part, ukv[:, :, 64:].reshape(KV_LORA, 512)], axis=1)
    return w_in_p, w_uq_p, w_ukv_p


BIG_SPEC = {n: (r, c, ax) for n, r, c, ax in BIG}
COLUMN_MAJOR = ("w_in", "w_uq", "w_gate", "w_up")
GRAD_TRANSPOSED = ("w_gate", "w_up")
GATHER_FIRST = ("w_in", "w_uq", "w_ukv")
GATHER_LATE = tuple(n for n, _, _, _ in BIG if n not in GATHER_FIRST)
REDUCE_EARLY = ("w_ple_gate", "w_ple_proj", "w_down", "w_gate", "w_up", "w_o")
REDUCE_LAST = tuple(n for n, _, _, _ in BIG if n not in REDUCE_EARLY)


def _local_step(x, p, pos_f, tgt, w, sm, late_shards=None, c_idx=None):
    S = x.shape[0]
    spread = late_shards is not None
    w = dict(w)
    tabs, first = _rope_tables(pos_f, S, [late_shards[n] for n in GATHER_FIRST] if spread else ())
    for i, n in enumerate(GATHER_FIRST if spread else ()):
        w[n] = _from_chips(first[i], BIG_SPEC[n][2])
    w_in_p, w_uq_p, w_ukv_p = _pad_weights(w)

    xn, rq, rk, rv, rg, cq, ckv, kr = _inproj(x, sm["pre_mix_norm"], w_in_p, tabs, S)
    cqn, ckvn, qp, kp, v, kt, vt = _mla_up(cq, ckv, kr, sm["mla_q_norm"], sm["mla_kv_norm"], w_uq_p, w_ukv_p, tabs, S)
    mo, lse, gathered = _flash_fwd(qp, kp, vt, S, [late_shards[n] for n in GATHER_LATE] if spread else ())
    for i, n in enumerate(GATHER_LATE if spread else ()):
        w[n] = _from_chips(gathered[i], 0 if n in GRAD_TRANSPOSED else BIG_SPEC[n][2])
    if not spread:
        w.update({n: w[n].T for n in GRAD_TRANSPOSED})
    ry, ro, rprev = _ret_fwd(rq, rk, rv, rg, sm["ret_gn_w"], S)
    mix, h1, hn = _outproj(ro, mo, x, w["w_o"], sm["post_mix_norm"], sm["pre_ffn_norm"], S)
    dgate_f, dup_f, act = _ffn_up(hn, w["w_gate"], w["w_up"], S)
    ff, h2 = _ffn_down(act, w["w_down"], h1, sm["post_ffn_norm"], S)
    dz, dpe, dh2, h2b, loss_vec, d_ple_norm, d_b = _ple_loss(
        p, h2, tgt, w["w_ple_proj"], w["w_ple_gate"], sm["b_ple_gate"], sm["ple_norm"], S)

    gw = {}
    gs = {"ple_norm": d_ple_norm, "b_ple_gate": d_b}
    gw["w_ple_gate"] = _wgrad(h2b, dz, "wgrad_ple_gate", S)
    gw["w_ple_proj"] = _wgrad(p, dpe, "wgrad_ple_proj", S)
    dff, dgate, dup, gs["post_ffn_norm"] = _ffn_down_bwd(dh2, ff, sm["post_ffn_norm"], w["w_down"], dgate_f, dup_f, S)
    gw["w_down"] = _wgrad(act, dff, "wgrad_down", S)
    if spread:
        gw["w_gate"] = _wgrad(dgate, hn, "wgrad_gate", S)
        gw["w_up"] = _wgrad(dup, hn, "wgrad_up", S)
    else:
        gw["w_gate"] = _wgrad(hn, dgate, "wgrad_gate", S)
        gw["w_up"] = _wgrad(hn, dup, "wgrad_up", S)
    first = REDUCE_EARLY[:-1]
    g4 = [_by_chip(gw.pop(n), *((D_FF, D_MODEL, 0) if n in GRAD_TRANSPOSED else BIG_SPEC[n]))
          for n in first] if spread else []
    dh1, dmix, dro, dmo, gs["pre_ffn_norm"], gs["post_mix_norm"], got = _ffn_up_bwd(
        dgate, dup, w["w_gate"], w["w_up"], h1, mix, dh2, sm["pre_ffn_norm"], sm["post_mix_norm"], w["w_o"], S, g4)
    gw["w_o"] = jnp.concatenate([_wgrad(ro, dmix, "wgrad_o_ret", S), _wgrad(mo, dmix, "wgrad_o_mla", S)], axis=0)
    g4_o = [_by_chip(gw.pop("w_o"), *BIG_SPEC["w_o"])] if spread else []

    dmo_t, delta, got_o = _attn_delta(mo, dmo, S, g4_o)
    sums = [_add_half_rows(a, b, c_idx, "rs_add_halves_" + n)
            for n, a, b in zip(REDUCE_EARLY, g4 + g4_o, list(got) + list(got_o))] if spread else []
    dqp, dkp, dv, parts = _flash_bwd(qp, kp, kt, v, dmo, dmo_t, lse, delta, S, sums)
    dqh, dkv, dcq, dckv, dkr, gs["mla_q_norm"], gs["mla_kv_norm"] = _mla_up_bwd(
        dqp, dkp, dv, cq, ckv, sm["mla_q_norm"], sm["mla_kv_norm"], w_uq_p, w_ukv_p, tabs, S)
    g_uq_p = _wgrad(cqn, dqh, "wgrad_uq", S)
    g_ukv_p = _wgrad(ckvn, dkv, "wgrad_ukv", S)
    gw["w_uq"] = g_uq_p.reshape(Q_LORA, MLA_HEADS, 128)[:, :, :96].reshape(Q_LORA, 768)
    gw["w_ukv"] = jnp.concatenate(
        [g_ukv_p[:, :1024].reshape(KV_LORA, MLA_HEADS, 128)[:, :, :64], g_ukv_p[:, 1024:].reshape(KV_LORA, MLA_HEADS, 64)],
        axis=2).reshape(KV_LORA, 1024)

    drq, drk, drv, drg, gs["ret_gn_w"] = _ret_bwd(rq, rk, rv, rprev, ry, rg, dro, sm["ret_gn_w"], tabs, S)
    grad_x, dproj, gs["pre_mix_norm"] = _inproj_bwd(drq, drk, drv, drg, dcq, dckv, dkr, w_in_p, dh1, x,
                                                    sm["pre_mix_norm"], S)
    g_in_p = _wgrad(xn, dproj, "wgrad_in", S)
    gw["w_in"] = g_in_p[:, :IN_COLS]
    return loss_vec, grad_x, gw, gs, ((sums, parts) if spread else None)


def _my_place():
    x = lax.axis_index("x")
    y = lax.axis_index("y")
    c = lax.axis_index("c")
    return x, y, c


def _other_chips(x, y):
    return [(1 - x, y), (x, 1 - y), (1 - x, 1 - y)]


_ANY = pl.BlockSpec(memory_space=pl.ANY)


def _small_copies(v_ref, slots, sems):
    send, recv, lsem = sems
    x, y, c = _my_place()
    me = 4 * x + 2 * y + c
    cps = [pltpu.make_async_copy(v_ref, slots.at[me], lsem)]
    for r in range(1, N_DEV):
        peer = (x ^ (r >> 2), y ^ ((r >> 1) & 1), c ^ (r & 1))
        cps.append(pltpu.make_async_remote_copy(
            src_ref=v_ref, dst_ref=slots.at[me], send_sem=send.at[r - 1], recv_sem=recv.at[r - 1],
            device_id=peer, device_id_type=MESH))
    return cps


def _small_sum(slots, out_ref):
    acc = slots[0]
    for d in range(1, N_DEV):
        acc = acc + slots[d]
    out_ref[...] = acc
    loss = jnp.sum(acc[9:10, :], axis=1, keepdims=True) * (0.5 / D_MODEL)
    out_ref[9:10, :] = jnp.broadcast_to(loss, (1, PACK_COLS))


def _small_scratch():
    return [pltpu.VMEM((N_DEV, SMALL_ROWS, PACK_COLS), F32), pltpu.SemaphoreType.DMA((N_DEV - 1,)),
            pltpu.SemaphoreType.DMA((N_DEV - 1,)), pltpu.SemaphoreType.DMA]


N_BIG = len(BIG)


def _half(c, rows, align):
    h = rows // 2
    return pl.ds(pl.multiple_of(c * h, align), h)


def _gather_out_shapes(shards):
    return [_sds((N_CHIPS,) + tuple(s.shape), BF16) for s in shards]


def _gather_sems(n):
    return [pltpu.SemaphoreType.DMA((n, 3))] * 4 + [pltpu.SemaphoreType.DMA((n,))] * 2


def _gather_phase(phase, ins, outs, sems):
    send1, recv1, send2, recv2, send3, recv3 = sems
    x, y, c = _my_place()
    me = 2 * x + y
    chips = _other_chips(x, y)
    sib = (x, y, 1 - c)
    for t in range(len(ins)):
        rows = ins[t].shape[0]
        half = _half(c, rows, 16)
        other = _half(1 - c, rows, 16)
        def own():
            return pltpu.make_async_remote_copy(
                src_ref=ins[t], dst_ref=outs[t].at[me], send_sem=send3.at[t], recv_sem=recv3.at[t],
                device_id=sib, device_id_type=MESH)

        if phase == 0:
            own().start()
        if phase == 2:
            own().wait()
        for k, (cx, cy) in enumerate(chips):
            src = 2 * cx + cy

            def over_ici(slab):
                return pltpu.make_async_remote_copy(
                    src_ref=ins[t].at[half], dst_ref=outs[t].at[slab, half], send_sem=send1.at[t, k],
                    recv_sem=recv1.at[t, k], device_id=(cx, cy, c), device_id_type=MESH)

            def over_d2d(rows):
                return pltpu.make_async_remote_copy(
                    src_ref=outs[t].at[src, rows], dst_ref=outs[t].at[src, rows], send_sem=send2.at[t, k],
                    recv_sem=recv2.at[t, k], device_id=sib, device_id_type=MESH)

            if phase == 0:
                over_ici(me).start()
            if phase == 1:
                over_ici(src).wait_recv()
                over_d2d(half).start()
            if phase == 2:
                over_d2d(other).wait_recv()
                over_ici(me).wait_send()
                over_d2d(half).wait_send()


def _swap_copies(ins, outs, sems):
    send, recv = sems
    x, y, c = _my_place()
    return [pltpu.make_async_remote_copy(
        src_ref=ins[t].at[:, _half(1 - c, ins[t].shape[1], 8)], dst_ref=outs[t], send_sem=send.at[t],
        recv_sem=recv.at[t], device_id=(x, y, 1 - c), device_id_type=MESH) for t in range(len(ins))]


def _swap_out_shapes(gs):
    return [_sds((N_CHIPS, g.shape[1] // 2, g.shape[2]), F32) for g in gs]


def _swap_sems(n):
    return [pltpu.SemaphoreType.DMA((n,)), pltpu.SemaphoreType.DMA((n,))]


def _swap_half_rows(gs):
    n = len(gs)

    def body(*refs):
        cps = _swap_copies(refs[:n], refs[n:2 * n], refs[2 * n:])
        for cp in cps:
            cp.start()
        for cp in cps:
            cp.wait()

    return pl.pallas_call(
        body, name="rs_swap_halves",
        in_specs=[_ANY] * n, out_specs=[_ANY] * n, out_shape=_swap_out_shapes(gs), scratch_shapes=_swap_sems(n),
    )(*gs)


def _add_half_rows(g, got, c_idx, name):
    _, rows, cols = g.shape
    h = rows // 2

    def body(c_ref, a_ref, b_ref, o_ref):
        o_ref[...] = (a_ref[...] + b_ref[...]).astype(BF16)

    grid_spec = pltpu.PrefetchScalarGridSpec(
        num_scalar_prefetch=1, grid=(N_CHIPS,),
        in_specs=[pl.BlockSpec((None, h, cols), lambda j, c: (j, c[0], 0)),
                  pl.BlockSpec((None, h, cols), lambda j, c: (j, 0, 0))],
        out_specs=pl.BlockSpec((None, h, cols), lambda j, c: (j, 0, 0)),
    )
    return pl.pallas_call(
        body, name=name, grid_spec=grid_spec, out_shape=_sds((N_CHIPS, h, cols), BF16),
        compiler_params=_cp(("parallel",)),
    )(c_idx, g, got)


def _scatter_to_chips(ts, vec):
    n = len(ts)

    def body(*refs):
        ins, v_ref, outs, small_ref = refs[:n], refs[n], refs[n + 1:2 * n + 1], refs[2 * n + 1]
        slots, small_sems, sems = refs[2 * n + 2], refs[2 * n + 3:2 * n + 6], refs[2 * n + 6:]
        small = _small_copies(v_ref, slots, small_sems)
        cps = _scatter_copies(ins, outs, sems)
        for cp in small + cps:
            cp.start()
        for cp in small:
            cp.wait()
        _small_sum(slots, small_ref)
        for cp in cps:
            cp.wait()

    vm = pl.BlockSpec(memory_space=pltpu.VMEM)
    *parts, small_sum = pl.pallas_call(
        body, name="rs_scatter_chips",
        in_specs=[_ANY] * n + [vm], out_specs=[_ANY] * n + [vm],
        out_shape=_scatter_out_shapes(ts) + [_sds((SMALL_ROWS, PACK_COLS), F32)],
        scratch_shapes=_small_scratch() + _scatter_sems(n),
    )(*ts, vec)
    return parts, small_sum


def _scatter_copies(ins, outs, sems):
    send, recv = sems
    x, y, c = _my_place()
    return [pltpu.make_async_remote_copy(
        src_ref=ins[t].at[2 * cx + cy], dst_ref=outs[t].at[k], send_sem=send.at[t, k], recv_sem=recv.at[t, k],
        device_id=(cx, cy, c), device_id_type=MESH)
        for t in range(len(ins)) for k, (cx, cy) in enumerate(_other_chips(x, y))]


def _scatter_out_shapes(ts):
    return [_sds((3,) + tuple(t.shape[1:]), BF16) for t in ts]


def _scatter_sems(n):
    return [pltpu.SemaphoreType.DMA((n, 3)), pltpu.SemaphoreType.DMA((n, 3))]


def _add_four(mine, parts, place, name):
    _, h, cols = parts.shape

    def body(pl_ref, m_ref, p_ref, o_ref):
        o_ref[...] = ((m_ref[...].astype(F32) + p_ref[0].astype(F32)) + p_ref[1].astype(F32)) + p_ref[2].astype(F32)

    grid_spec = pltpu.PrefetchScalarGridSpec(
        num_scalar_prefetch=1, grid=(1,),
        in_specs=[pl.BlockSpec((None, h, cols), lambda i, pc: (pc[0], 0, 0)),
                  pl.BlockSpec((3, h, cols), lambda i, pc: (0, 0, 0))],
        out_specs=pl.BlockSpec((h, cols), lambda i, pc: (pc[1], 0)),
    )
    return pl.pallas_call(
        body, name=name, grid_spec=grid_spec, out_shape=_sds((2 * h, cols), F32),
        compiler_params=_cp(("arbitrary",)),
    )(place, mine, parts)


def _join_half_rows(rs):
    n = len(rs)

    def body(*refs):
        ins, outs = refs[:n], refs[n:2 * n]
        send, recv = refs[2 * n:]
        x, y, c = _my_place()
        cps = []
        for t in range(n):
            half = _half(c, outs[t].shape[0], 8)
            rc = pltpu.make_async_remote_copy(
                src_ref=ins[t].at[half], dst_ref=outs[t].at[half], send_sem=send.at[t], recv_sem=recv.at[t],
                device_id=(x, y, 1 - c), device_id_type=MESH)
            rc.start()
            cps.append(rc)
        for cp in cps:
            cp.wait()

    return pl.pallas_call(
        body, name="rs_join_halves",
        in_specs=[_ANY] * n, out_specs=[_ANY] * n,
        out_shape=[_sds(r.shape, F32) for r in rs],
        input_output_aliases={i: i for i in range(n)},
        scratch_shapes=[pltpu.SemaphoreType.DMA((n,))] * 2,
    )(*rs)


def _by_chip(full, rows, cols, axis):
    if axis == 0:
        return full.reshape(N_CHIPS, rows // N_CHIPS, cols)
    return full.reshape(rows, N_CHIPS, cols // N_CHIPS).transpose(1, 0, 2)


def _from_chips(parts, axis):
    _, r, c = parts.shape
    if axis == 0:
        return parts.reshape(N_CHIPS * r, c)
    return parts.transpose(1, 0, 2).reshape(r, N_CHIPS * c)


def _adamw(wt, g, m, v, name):
    _, R, C = wt.shape
    tr = max(d for d in range(8, R + 1, 8) if R % d == 0 and (d * C <= 256 * 1024 or d == 8))

    def body(w_ref, g_ref, m_ref, v_ref, d_ref, nm_ref, nv_ref):
        gg = g_ref[...]
        m_new = ADAM_B1 * m_ref[...] + (1.0 - ADAM_B1) * gg
        v_new = ADAM_B2 * v_ref[...] + (1.0 - ADAM_B2) * (gg * gg)
        m_hat = m_new / (1.0 - ADAM_B1 ** ADAM_STEP)
        v_hat = v_new / (1.0 - ADAM_B2 ** ADAM_STEP)
        d_ref[...] = -ADAM_LR * (m_hat / (jnp.sqrt(v_hat) + ADAM_EPS) + ADAM_WD * w_ref[...])
        nm_ref[...] = m_new
        nv_ref[...] = v_new

    spec = pl.BlockSpec((None, tr, C), lambda i: (0, i, 0))
    return pl.pallas_call(
        body, name=name, grid=(R // tr,), in_specs=[spec, pl.BlockSpec((tr, C), lambda i: (i, 0)), spec, spec],
        out_specs=[spec] * 3, out_shape=[_sds((1, R, C), F32)] * 3,
        compiler_params=_cp(("parallel",)),
    )(wt, g, m, v)


def _pack_small(vals, loss_vec=None):
    rows = [jnp.pad(vals[n].reshape(-1), (0, PACK_COLS - sz)) for n, sz in SMALL]
    rows.append(loss_vec.reshape(-1) if loss_vec is not None else jnp.zeros((PACK_COLS,), F32))
    rows += [jnp.zeros((PACK_COLS,), F32)] * (SMALL_ROWS - len(rows))
    return jnp.stack(rows)


def kernel(x, p, positions, pre_mix_norm, w_in, ret_gn_w, mla_q_norm, w_uq, mla_kv_norm, w_ukv, w_o, post_mix_norm, pre_ffn_norm, w_gate, w_up, w_down, post_ffn_norm, w_ple_proj, ple_norm, w_ple_gate, b_ple_gate, loss_target, m_pre_mix_norm, m_w_in, m_ret_gn_w, m_mla_q_norm, m_w_uq, m_mla_kv_norm, m_w_ukv, m_w_o, m_post_mix_norm, m_pre_ffn_norm, m_w_gate, m_w_up, m_w_down, m_post_ffn_norm, m_w_ple_proj, m_ple_norm, m_w_ple_gate, m_b_ple_gate, v_pre_mix_norm, v_w_in, v_ret_gn_w, v_mla_q_norm, v_w_uq, v_mla_kv_norm, v_w_ukv, v_w_o, v_post_mix_norm, v_pre_ffn_norm, v_w_gate, v_w_up, v_w_down, v_post_ffn_norm, v_w_ple_proj, v_ple_norm, v_w_ple_gate, v_b_ple_gate):
    wts = dict(pre_mix_norm=pre_mix_norm, w_in=w_in, ret_gn_w=ret_gn_w, mla_q_norm=mla_q_norm, w_uq=w_uq,
               mla_kv_norm=mla_kv_norm, w_ukv=w_ukv, w_o=w_o, post_mix_norm=post_mix_norm, pre_ffn_norm=pre_ffn_norm,
               w_gate=w_gate, w_up=w_up, w_down=w_down, post_ffn_norm=post_ffn_norm, w_ple_proj=w_ple_proj,
               ple_norm=ple_norm, w_ple_gate=w_ple_gate, b_ple_gate=b_ple_gate)
    mom = dict(pre_mix_norm=m_pre_mix_norm, w_in=m_w_in, ret_gn_w=m_ret_gn_w, mla_q_norm=m_mla_q_norm, w_uq=m_w_uq,
               mla_kv_norm=m_mla_kv_norm, w_ukv=m_w_ukv, w_o=m_w_o, post_mix_norm=m_post_mix_norm,
               pre_ffn_norm=m_pre_ffn_norm, w_gate=m_w_gate, w_up=m_w_up, w_down=m_w_down, post_ffn_norm=m_post_ffn_norm,
               w_ple_proj=m_w_ple_proj, ple_norm=m_ple_norm, w_ple_gate=m_w_ple_gate, b_ple_gate=m_b_ple_gate)
    var = dict(pre_mix_norm=v_pre_mix_norm, w_in=v_w_in, ret_gn_w=v_ret_gn_w, mla_q_norm=v_mla_q_norm, w_uq=v_w_uq,
               mla_kv_norm=v_mla_kv_norm, w_ukv=v_w_ukv, w_o=v_w_o, post_mix_norm=v_post_mix_norm,
               pre_ffn_norm=v_pre_ffn_norm, w_gate=v_w_gate, w_up=v_w_up, w_down=v_w_down, post_ffn_norm=v_post_ffn_norm,
               w_ple_proj=v_w_ple_proj, ple_norm=v_ple_norm, w_ple_gate=v_w_ple_gate, b_ple_gate=v_b_ple_gate)

    S = x.shape[1]
    shard2d = {n: wts[n][0] for n, _, _, _ in BIG}
    small2d = {n: wts[n] for n, _ in SMALL}

    shard_bf = {n: (jnp.swapaxes(wts[n], 1, 2)[0] if n in GRAD_TRANSPOSED else shard2d[n]).astype(BF16) for n in shard2d}
    pos_f = positions.astype(F32).reshape(S, 1)
    c_idx = lax.axis_index("c").astype(jnp.int32).reshape(1)
    loss_vec, grad_x, gw, gs, (sums_early, parts_early) = _local_step(
        x[0], p[0, 0], pos_f, loss_target[0], {}, small2d, shard_bf, c_idx)

    g4 = [_by_chip(gw[n], *BIG_SPEC[n]) for n in REDUCE_LAST]
    got = _swap_half_rows(g4)
    sums_last = [_add_half_rows(g4[i], got[i], c_idx, "rs_add_halves_" + n) for i, n in enumerate(REDUCE_LAST)]
    parts_last, small_sum = _scatter_to_chips(sums_last, _pack_small(gs, loss_vec))
    place = jnp.stack([2 * lax.axis_index("x") + lax.axis_index("y"), lax.axis_index("c")]).astype(jnp.int32)
    names = REDUCE_EARLY + REDUCE_LAST
    reduced = _join_half_rows(
        [_add_four(sm_, pt_, place, "rs_add_chips_" + n)
         for n, sm_, pt_ in zip(names, sums_early + sums_last, list(parts_early) + list(parts_last))])
    g_shard = dict(zip(names, reduced))

    loss = small_sum[9, 0]
    g_small = {n: small_sum[i:i + 1, :sz] for i, (n, sz) in enumerate(SMALL)}

    grads, delta, new_m, new_v = {}, {}, {}, {}
    for n, _, _, _ in BIG:
        if n in COLUMN_MAJOR:
            turn = lambda a: jnp.swapaxes(a, 1, 2)
            g_t = g_shard[n] if n in GRAD_TRANSPOSED else g_shard[n].T
            d, nm, nv = _adamw(turn(wts[n]), g_t, turn(mom[n]), turn(var[n]), "adamw_" + n)
            grads[n], delta[n], new_m[n], new_v[n] = turn(g_t[None]), turn(d), turn(nm), turn(nv)
        else:
            delta[n], new_m[n], new_v[n] = _adamw(wts[n], g_shard[n], mom[n], var[n], "adamw_" + n)
            grads[n] = g_shard[n][None]
    d, nm, nv = _adamw(_pack_small(small2d)[None], small_sum, _pack_small(mom)[None], _pack_small(var)[None],
                       "adamw_small")
    for i, (n, sz) in enumerate(SMALL):
        grads[n] = g_small[n]
        delta[n], new_m[n], new_v[n] = d[0, i:i + 1, :sz], nm[0, i:i + 1, :sz], nv[0, i:i + 1, :sz]

    return (loss, grad_x[None], *[grads[n] for n in ALL_W], *[delta[n] for n in ALL_W],
            *[new_m[n] for n in ALL_W], *[new_v[n] for n in ALL_W])
```

```python
import functools
import math

import jax
import jax.numpy as jnp
import numpy as np
from jax import lax
from jax.experimental import pallas as pl
from jax.experimental.pallas import tpu as pltpu

F32 = jnp.float32
BF16 = jnp.bfloat16
MESH = pl.DeviceIdType.MESH

D_MODEL = 1024
D_FF = 2816
PLE_DIM = 256
RET_HEADS = 4
RET_DIM = 128
RET_WIDTH = 512
RET_CHUNK = 256
RET_GROUP_FWD = 16
RET_GROUP_BWD = 8
MLA_HEADS = 8
MLA_NOPE = 64
MLA_ROPE = 32
MLA_V = 64
Q_LORA = 384
KV_LORA = 256
IN_COLS = 2720
IN_COLS_P = 2816
ROPE_BASE = 10000.0
EPS = 1e-6
SCALE_MLA = 1.0 / math.sqrt(MLA_NOPE + MLA_ROPE)
SCALE_RET = RET_DIM ** -0.5
NEG = -1e30

ADAM_LR = 0.001
ADAM_B1 = 0.9
ADAM_B2 = 0.999
ADAM_EPS = 1e-08
ADAM_WD = 0.01
ADAM_STEP = 10

N_CHIPS = 4
N_DEV = 8
VMEM_MB = 56

BIG = (
    ("w_in", 1024, 2720, 1),
    ("w_uq", 384, 768, 1),
    ("w_ukv", 256, 1024, 1),
    ("w_o", 1024, 1024, 0),
    ("w_gate", 1024, 2816, 1),
    ("w_up", 1024, 2816, 1),
    ("w_down", 2816, 1024, 0),
    ("w_ple_proj", 256, 1024, 1),
    ("w_ple_gate", 1024, 1024, 0),
)
SMALL = (
    ("pre_mix_norm", 1024),
    ("ret_gn_w", 512),
    ("mla_q_norm", 384),
    ("mla_kv_norm", 256),
    ("post_mix_norm", 1024),
    ("pre_ffn_norm", 1024),
    ("post_ffn_norm", 1024),
    ("ple_norm", 1024),
    ("b_ple_gate", 1024),
)
ALL_W = ("pre_mix_norm", "w_in", "ret_gn_w", "mla_q_norm", "w_uq", "mla_kv_norm", "w_ukv", "w_o", "post_mix_norm",
         "pre_ffn_norm", "w_gate", "w_up", "w_down", "post_ffn_norm", "w_ple_proj", "ple_norm", "w_ple_gate", "b_ple_gate")
PACK_COLS = 1024
SMALL_ROWS = 16


def _cp(sem=None, mb=VMEM_MB, **kw):
    return pltpu.CompilerParams(dimension_semantics=sem, vmem_limit_bytes=mb * 1024 * 1024, **kw)


def _bf(x):
    return x.astype(BF16)


def _dot(a, b):
    return jnp.dot(_bf(a), _bf(b), preferred_element_type=F32)


def _dot_nt(a, b):
    return lax.dot_general(_bf(a), _bf(b), (((1,), (1,)), ((), ())), preferred_element_type=F32)


def _dot_tn(a, b):
    return lax.dot_general(_bf(a), _bf(b), (((0,), (0,)), ((), ())), preferred_element_type=F32)


def _sig(x):
    return 1.0 / (1.0 + jnp.exp(-x))


def _rms(x, g):
    r = lax.rsqrt(jnp.mean(x * x, axis=-1, keepdims=True) + EPS)
    return x * r * g


def _rms_bwd(dy, x, g):
    r = lax.rsqrt(jnp.mean(x * x, axis=-1, keepdims=True) + EPS)
    xh = x * r
    dxh = dy * g
    dx = r * (dxh - xh * jnp.mean(dxh * xh, axis=-1, keepdims=True))
    return dx, dy * xh


def _colsum(x):
    return jnp.sum(x, axis=0, keepdims=True)


def _rope_ret(x, cr, sr):
    return x * cr + pltpu.roll(x, 64, 1) * sr


def _unrope_ret(dy, cr, sr):
    return dy * cr + pltpu.roll(dy * sr, 64, 1)


def _rope_mla(x, cm, sa, sb):
    return x * cm + pltpu.roll(x, 112, 1) * sa + pltpu.roll(x, 16, 1) * sb


def _unrope_mla(dy, cm, sa, sb):
    return dy * cm + pltpu.roll(dy * sa, 16, 1) + pltpu.roll(dy * sb, 112, 1)


def _rows(tm, w, col=0):
    return pl.BlockSpec((tm, w), lambda i: (i, col))


def _full(*shape):
    return pl.BlockSpec(shape, lambda i: (0,) * len(shape), pipeline_mode=pl.Buffered(1))


def _acc(*shape):
    return pl.BlockSpec(shape, lambda i: (0,) * len(shape))


def _sds(shape, dtype):
    return jax.ShapeDtypeStruct(shape, dtype)


def _rope_tables(pos_f, S, shards=()):
    tm = min(512, S)
    n = len(shards)
    steps = S // tm
    inv_r = (1.0 / (np.float32(ROPE_BASE) ** (np.arange(64, dtype=np.float32) / np.float32(64)))).astype(np.float32)
    inv_m16 = (1.0 / (np.float32(ROPE_BASE) ** (np.arange(16, dtype=np.float32) / np.float32(16)))).astype(np.float32)
    inv_r = np.concatenate([inv_r, inv_r])[None, :]
    inv_m = np.zeros((1, 128), np.float32)
    inv_m[0, 64:80] = inv_m16
    inv_m[0, 80:96] = inv_m16

    def body(pos_ref, invr_ref, invm_ref, *rest):
        w_ins, (cr_ref, sr_ref, cm_ref, sa_ref, sb_ref) = rest[:n], rest[n:n + 5]
        w_outs, sems = rest[n + 5:2 * n + 5], rest[2 * n + 5:]
        i = pl.program_id(0)
        if n:
            @pl.when(i == 0)
            def _():
                _gather_phase(0, w_ins, w_outs, sems)

            @pl.when(i == steps - 1)
            def _():
                _gather_phase(1, w_ins, w_outs, sems)

        pos = pos_ref[...]
        lane = lax.broadcasted_iota(jnp.int32, (tm, 128), 1)
        ar = pos * invr_ref[...]
        s = jnp.sin(ar)
        cr_ref[...] = jnp.cos(ar)
        sr_ref[...] = jnp.where(lane < 64, -s, s)
        am = pos * invm_ref[...]
        c2 = jnp.cos(am)
        s2 = jnp.sin(am)
        cm_ref[...] = jnp.where(lane < 64, 1.0, jnp.where(lane < 96, c2, 0.0))
        sa_ref[...] = jnp.where((lane >= 64) & (lane < 80), -s2, 0.0)
        sb_ref[...] = jnp.where((lane >= 80) & (lane < 96), s2, 0.0)

        if n:
            @pl.when(i == steps - 1)
            def _():
                _gather_phase(2, w_ins, w_outs, sems)

    outs = pl.pallas_call(
        body, name="rope_tables", grid=(steps,),
        in_specs=[_rows(tm, 1), _full(1, 128), _full(1, 128)] + [_ANY] * n,
        out_specs=[_rows(tm, 128)] * 5 + [_ANY] * n,
        out_shape=[_sds((S, 128), F32)] * 5 + _gather_out_shapes(shards),
        scratch_shapes=_gather_sems(n) if n else [],
        compiler_params=_cp(("arbitrary",)),
    )(pos_f, jnp.asarray(inv_r), jnp.asarray(inv_m), *shards)
    return outs[:5], outs[5:]


def _inproj(x, g, w_in, tabs, S):
    tm = min(512, S)

    def body(x_ref, g_ref, w_ref, cr_ref, sr_ref, cm_ref, sa_ref, sb_ref,
             xn_ref, rq_ref, rk_ref, rv_ref, rg_ref, cq_ref, ckv_ref, kr_ref):
        xb = _rms(x_ref[...], g_ref[...]).astype(BF16)
        xn_ref[...] = xb
        cr = cr_ref[...]
        sr = sr_ref[...]
        q = jnp.dot(xb, w_ref[:, 0:512], preferred_element_type=F32)
        k = jnp.dot(xb, w_ref[:, 512:1024], preferred_element_type=F32)
        for h in range(RET_HEADS):
            sl = slice(h * 128, (h + 1) * 128)
            rq_ref[:, sl] = _rope_ret(q[:, sl], cr, sr).astype(BF16)
            rk_ref[:, sl] = (_rope_ret(k[:, sl], cr, sr) * SCALE_RET).astype(BF16)
        rv_ref[...] = jnp.dot(xb, w_ref[:, 1024:1536], preferred_element_type=F32).astype(BF16)
        rg_ref[...] = jnp.dot(xb, w_ref[:, 1536:2048], preferred_element_type=F32)
        cq_ref[...] = jnp.dot(xb, w_ref[:, 2048:2432], preferred_element_type=F32)
        ckv_ref[...] = jnp.dot(xb, w_ref[:, 2432:2688], preferred_element_type=F32)
        kr = pltpu.roll(jnp.dot(xb, w_ref[:, 2688:2816], preferred_element_type=F32), 64, 1)
        kr_ref[...] = _rope_mla(kr, cm_ref[...], sa_ref[...], sb_ref[...])

    return pl.pallas_call(
        body, name="inproj", grid=(S // tm,),
        in_specs=[_rows(tm, D_MODEL), _full(1, D_MODEL), _full(D_MODEL, IN_COLS_P)] + [_rows(tm, 128)] * 5,
        out_specs=[_rows(tm, D_MODEL)] + [_rows(tm, 512)] * 4 + [_rows(tm, Q_LORA), _rows(tm, KV_LORA), _rows(tm, 128)],
        out_shape=[_sds((S, D_MODEL), BF16)] + [_sds((S, 512), BF16)] * 3
        + [_sds((S, 512), F32), _sds((S, Q_LORA), F32), _sds((S, KV_LORA), F32), _sds((S, 128), F32)],
        compiler_params=_cp(("parallel",)),
    )(x, g, w_in, *tabs)


def _mla_up(cq, ckv, kr, gq, gkv, w_uq, w_ukv, tabs, S):
    tm = min(512, S)

    def body(cq_ref, ckv_ref, kr_ref, gq_ref, gkv_ref, wuq_ref, wukv_ref, cm_ref, sa_ref, sb_ref,
             cqn_ref, ckvn_ref, qp_ref, kp_ref, v_ref, kt_ref, vt_ref):
        cm = cm_ref[...]
        sa = sa_ref[...]
        sb = sb_ref[...]
        cqn = _rms(cq_ref[...], gq_ref[...]).astype(BF16)
        cqn_ref[...] = cqn
        ckvn = _rms(ckv_ref[...], gkv_ref[...]).astype(BF16)
        ckvn_ref[...] = ckvn
        qh = jnp.dot(cqn, wuq_ref[...], preferred_element_type=F32)
        kv = jnp.dot(ckvn, wukv_ref[...], preferred_element_type=F32)
        kr_blk = kr_ref[...]
        for h in range(MLA_HEADS):
            sl = slice(h * 128, (h + 1) * 128)
            qp_ref[:, sl] = (_rope_mla(qh[:, sl], cm, sa, sb) * SCALE_MLA).astype(BF16)
            kh = kv[:, sl] + kr_blk
            kp_ref[:, sl] = kh.astype(BF16)
            kt_ref[sl, :] = kh.T.astype(BF16)
        for h in range(MLA_HEADS // 2):
            vh = kv[:, 1024 + h * 128:1024 + (h + 1) * 128]
            v_ref[:, h * 128:(h + 1) * 128] = vh.astype(BF16)
            vt_ref[h * 128:(h + 1) * 128, :] = vh.T.astype(BF16)

    cols = lambda r: pl.BlockSpec((r, tm), lambda i: (0, i))
    return pl.pallas_call(
        body, name="mla_up", grid=(S // tm,),
        in_specs=[_rows(tm, Q_LORA), _rows(tm, KV_LORA), _rows(tm, 128), _full(1, Q_LORA), _full(1, KV_LORA),
                  _full(Q_LORA, 1024), _full(KV_LORA, 1536)] + [_rows(tm, 128)] * 3,
        out_specs=[_rows(tm, Q_LORA), _rows(tm, KV_LORA), _rows(tm, 1024), _rows(tm, 1024), _rows(tm, 512),
                   cols(1024), cols(512)],
        out_shape=[_sds((S, Q_LORA), BF16), _sds((S, KV_LORA), BF16), _sds((S, 1024), BF16), _sds((S, 1024), BF16),
                   _sds((S, 512), BF16), _sds((1024, S), BF16), _sds((512, S), BF16)],
        compiler_params=_cp(("parallel",)),
    )(cq, ckv, kr, gq, gkv, w_uq, w_ukv, *tabs[2:])


def _tri_pairs(nq, k_major):
    if k_major:
        pairs = [(qb, kb) for kb in range(nq) for qb in range(kb, nq)]
    else:
        pairs = [(qb, kb) for qb in range(nq) for kb in range(qb + 1)]
    qb_of = np.array([p[0] for p in pairs], np.int32)
    kb_of = np.array([p[1] for p in pairs], np.int32)
    return jnp.asarray(qb_of), jnp.asarray(kb_of), len(pairs)


ATT_ROWS = 32
FWD_HEADS = 8
BWD_HEADS = 4


def _causal_keep(r0, rows, tq):
    key = r0 + lax.broadcasted_iota(jnp.int32, (rows, tq), 0)
    qry = lax.broadcasted_iota(jnp.int32, (rows, tq), 1)
    return key <= qry


def _flash_fwd(qp, kp, vt, S, shards=()):
    tq = min(512, S)
    nq = S // tq
    RB = ATT_ROWS
    NH = FWD_HEADS
    qb_of, kb_of, T = _tri_pairs(nq, k_major=False)
    n = len(shards)
    steps = (MLA_HEADS // NH) * T

    def body(qb_ref, kb_ref, q_ref, k_ref, vt_ref, *rest):
        w_ins, (o_ref, lse_ref), w_outs = rest[:n], rest[n:n + 2], rest[n + 2:2 * n + 2]
        m_sc, l_sc, acc_sc, s_sc, p_sc = rest[2 * n + 2:2 * n + 7]
        sems = rest[2 * n + 7:]
        t = pl.program_id(1)
        qb = qb_ref[t]
        kb = kb_ref[t]
        lin = pl.program_id(0) * T + t

        if n:
            @pl.when(lin == 0)
            def _():
                _gather_phase(0, w_ins, w_outs, sems)

            @pl.when(lin == steps // 2)
            def _():
                _gather_phase(1, w_ins, w_outs, sems)

        @pl.when(kb == 0)
        def _():
            m_sc[...] = jnp.full(m_sc.shape, NEG, F32)
            l_sc[...] = jnp.zeros(l_sc.shape, F32)
            acc_sc[...] = jnp.zeros(acc_sc.shape, F32)

        def scores(a):
            sl = slice(a * 128, (a + 1) * 128)
            s_sc[a] = _dot_nt(k_ref[:, sl], q_ref[:, sl])

        def step(masked):
            for a in range(NH):
                scores(a)
            for a in range(NH):
                mx = [jnp.full((8, tq), NEG, F32) for _ in range(RB // 8)]
                for r in range(0, tq, RB):
                    sc = s_sc[a, r:r + RB, :]
                    if masked:
                        sc = jnp.where(_causal_keep(r, RB, tq), sc, NEG)
                        s_sc[a, r:r + RB, :] = sc
                    for i in range(RB // 8):
                        mx[i] = jnp.maximum(mx[i], sc[i * 8:(i + 1) * 8, :])
                mx8 = functools.reduce(jnp.maximum, mx)
                m_prev = m_sc[a]
                m_new = jnp.maximum(m_prev, jnp.max(mx8, axis=0, keepdims=True))
                al = jnp.exp(m_prev - m_new)
                m_sc[a] = m_new
                ls = [jnp.zeros((8, tq), F32) for _ in range(RB // 8)]
                for r in range(0, tq, RB):
                    p = jnp.exp(s_sc[a, r:r + RB, :] - m_new)
                    for i in range(RB // 8):
                        ls[i] = ls[i] + p[i * 8:(i + 1) * 8, :]
                    p_sc[a, r:r + RB, :] = p.astype(BF16)
                l_sc[a] = al * l_sc[a] + jnp.sum(functools.reduce(jnp.add, ls), axis=0, keepdims=True)
                pair = slice((a // 2) * 128, (a // 2 + 1) * 128)
                pv = jnp.dot(vt_ref[pair, :], p_sc[a], preferred_element_type=F32)
                rs = slice(a * 64, (a + 1) * 64)
                own = slice((a % 2) * 64, (a % 2 + 1) * 64)
                acc_sc[rs, :] = acc_sc[rs, :] * al + pv[own, :]

        @pl.when(kb < qb)
        def _():
            step(False)

        @pl.when(kb == qb)
        def _():
            step(True)
            for a in range(NH):
                rs = slice(a * 64, (a + 1) * 64)
                acc_sc[rs, :] = acc_sc[rs, :] / l_sc[a]
                lse_ref[a:a + 1, :] = m_sc[a] + jnp.log(l_sc[a])
            o_ref[...] = acc_sc[...].T.astype(BF16)

        if n:
            @pl.when(lin == steps - 1)
            def _():
                _gather_phase(2, w_ins, w_outs, sems)

    grid_spec = pltpu.PrefetchScalarGridSpec(
        num_scalar_prefetch=2, grid=(MLA_HEADS // NH, T),
        in_specs=[pl.BlockSpec((tq, 128 * NH), lambda j, t, qb, kb: (qb[t], j)),
                  pl.BlockSpec((tq, 128 * NH), lambda j, t, qb, kb: (kb[t], j)),
                  pl.BlockSpec((64 * NH, tq), lambda j, t, qb, kb: (j, kb[t]))] + [_ANY] * n,
        out_specs=[pl.BlockSpec((tq, 64 * NH), lambda j, t, qb, kb: (qb[t], j)),
                   pl.BlockSpec((None, NH, tq), lambda j, t, qb, kb: (j, 0, qb[t]))] + [_ANY] * n,
        scratch_shapes=[pltpu.VMEM((NH, 1, tq), F32), pltpu.VMEM((NH, 1, tq), F32), pltpu.VMEM((64 * NH, tq), F32),
                        pltpu.VMEM((NH, tq, tq), F32), pltpu.VMEM((NH, tq, tq), BF16)] + (_gather_sems(n) if n else []),
    )
    out, lse, *gathered = pl.pallas_call(
        body, name="flash_fwd", grid_spec=grid_spec,
        out_shape=[_sds((S, 512), BF16), _sds((MLA_HEADS // NH, NH, S), F32)] + _gather_out_shapes(shards),
        compiler_params=_cp(("arbitrary", "arbitrary")),
    )(qb_of, kb_of, qp, kp, vt, *shards)
    return out, lse.reshape(MLA_HEADS // 2, 2, S), gathered


def _decay_table():
    log_g = np.log(1.0 - 2.0 ** (-5.0 - np.arange(RET_HEADS, dtype=np.float32))).astype(np.float32)
    return jnp.asarray(np.broadcast_to(log_g[:, None, None], (RET_HEADS, 8, 128)).copy())


def _decay_terms(lg_ref):
    C = RET_CHUNK
    lg = lg_ref[0:1, :]
    row = lax.broadcasted_iota(jnp.int32, (C, C), 0)
    col = lax.broadcasted_iota(jnp.int32, (C, C), 1)
    diff = (row - col).astype(F32)
    dmat = jnp.where(diff >= 0, jnp.exp(jnp.maximum(diff, 0.0) * jnp.tile(lg, (1, C // 128))), 0.0)
    j = lax.broadcasted_iota(jnp.int32, (C, 1), 0).astype(F32)
    lg1 = lg[:, 0:1]
    zeta = jnp.exp((C - 1 - j) * lg1)
    xi = jnp.exp((j + 1.0) * lg1)
    g_chunk = jnp.exp(C * lg1)
    return dmat, zeta, xi, g_chunk


def _ret_fwd(rq, rk, rv, rg, gn_w, S):
    C = RET_CHUNK
    N = S // C
    G = min(RET_GROUP_FWD, N)
    NB = N // G

    def body(lg_ref, q_ref, k_ref, v_ref, rg_ref, w_ref, ry_ref, ro_ref, rprev_ref, r_sc):
        @pl.when(pl.program_id(1) == 0)
        def _():
            r_sc[...] = jnp.zeros(r_sc.shape, F32)

        dmat, zeta, xi, g_chunk = _decay_terms(lg_ref)
        w = w_ref[...]
        r = r_sc[...]
        for i in range(G):
            rows = slice(i * C, (i + 1) * C)
            q = q_ref[rows, :]
            k = k_ref[rows, :]
            v = v_ref[rows, :]
            r_prev = r.astype(BF16)
            rprev_ref[i] = r_prev
            sc = _dot_nt(q, k) * dmat
            ry = _dot(sc, v) + jnp.dot(q, r_prev, preferred_element_type=F32) * xi
            ry_ref[rows, :] = ry
            r = g_chunk * r + _dot_tn(k, zeta * v.astype(F32))
            mu = jnp.mean(ry, axis=-1, keepdims=True)
            yc = ry - mu
            yh = yc * lax.rsqrt(jnp.mean(yc * yc, axis=-1, keepdims=True) + EPS)
            g = rg_ref[rows, :]
            ro_ref[rows, :] = (g * _sig(g) * (yh * w)).astype(BF16)
        r_sc[...] = r

    blk = pl.BlockSpec((G * C, 128), lambda h, n: (n, h))
    return pl.pallas_call(
        body, name="ret_fwd", grid=(RET_HEADS, NB),
        in_specs=[pl.BlockSpec((None, 8, 128), lambda h, n: (h, 0, 0)), blk, blk, blk, blk,
                  pl.BlockSpec((1, 128), lambda h, n: (0, h))],
        out_specs=[blk, blk, pl.BlockSpec((G, 128, 128), lambda h, n: (h * NB + n, 0, 0))],
        out_shape=[_sds((S, 512), F32), _sds((S, 512), BF16), _sds((RET_HEADS * N, 128, 128), BF16)],
        scratch_shapes=[pltpu.VMEM((128, 128), F32)],
        compiler_params=_cp(("parallel", "arbitrary")),
    )(_decay_table(), rq, rk, rv, rg, gn_w)


def _outproj(ro, mo, x, w_o, g_post, g_pre, S):
    tm = min(512, S)

    def body(ro_ref, mo_ref, x_ref, wo_ref, g1_ref, g2_ref, mix_ref, h1_ref, hn_ref):
        mix = (jnp.dot(ro_ref[...], wo_ref[0:512, :], preferred_element_type=F32)
               + jnp.dot(mo_ref[...], wo_ref[512:1024, :], preferred_element_type=F32))
        mix_ref[...] = mix.astype(BF16)
        h1 = x_ref[...] + _rms(mix, g1_ref[...])
        h1_ref[...] = h1
        hn_ref[...] = _rms(h1, g2_ref[...]).astype(BF16)

    return pl.pallas_call(
        body, name="outproj", grid=(S // tm,),
        in_specs=[_rows(tm, 512), _rows(tm, 512), _rows(tm, D_MODEL), _full(D_MODEL, D_MODEL), _full(1, D_MODEL),
                  _full(1, D_MODEL)],
        out_specs=[_rows(tm, D_MODEL)] * 3,
        out_shape=[_sds((S, D_MODEL), BF16), _sds((S, D_MODEL), F32), _sds((S, D_MODEL), BF16)],
        compiler_params=_cp(("parallel",)),
    )(ro, mo, x, w_o, g_post, g_pre)


def _ffn_up(hn, w_gate_t, w_up_t, S):
    tm = min(512, S)
    tn = D_FF // 2

    def body(hn_ref, wg_ref, wu_ref, fg_ref, fu_ref, act_ref):
        hn_b = hn_ref[...]
        g = _dot_nt(hn_b, wg_ref[...])
        u = _dot_nt(hn_b, wu_ref[...])
        s = _sig(g)
        silu = g * s
        fg_ref[...] = (u * (s + silu * (1.0 - s))).astype(BF16)
        fu_ref[...] = silu.astype(BF16)
        act_ref[...] = (silu * u).astype(BF16)

    wspec = pl.BlockSpec((tn, D_MODEL), lambda j, i: (j, 0))
    ospec = pl.BlockSpec((tm, tn), lambda j, i: (i, j))
    return pl.pallas_call(
        body, name="ffn_up", grid=(2, S // tm),
        in_specs=[pl.BlockSpec((tm, D_MODEL), lambda j, i: (i, 0)), wspec, wspec],
        out_specs=[ospec] * 3, out_shape=[_sds((S, D_FF), BF16)] * 3,
        compiler_params=_cp(("parallel", "parallel")),
    )(hn, w_gate_t, w_up_t)


def _ffn_down(act, w_down, h1, g, S):
    tm = min(512, S)

    def body(act_ref, wd_ref, h1_ref, g_ref, ff_ref, h2_ref):
        ff = jnp.dot(act_ref[...], wd_ref[...], preferred_element_type=F32)
        ff_ref[...] = ff.astype(BF16)
        h2_ref[...] = h1_ref[...] + _rms(ff, g_ref[...])

    return pl.pallas_call(
        body, name="ffn_down", grid=(S // tm,),
        in_specs=[_rows(tm, D_FF), _full(D_FF, D_MODEL), _rows(tm, D_MODEL), _full(1, D_MODEL)],
        out_specs=[_rows(tm, D_MODEL)] * 2, out_shape=[_sds((S, D_MODEL), BF16), _sds((S, D_MODEL), F32)],
        compiler_params=_cp(("parallel",)),
    )(act, w_down, h1, g)


def _ple_loss(p, h2, tgt, w_pp, w_pg, b_pg, g_ple, S):
    tm = min(512, S)

    def body(p_ref, h2_ref, t_ref, wp_ref, wg_ref, b_ref, gp_ref,
             dz_ref, dpe_ref, dh2_ref, h2b_ref, loss_ref, dgp_ref, db_ref):
        @pl.when(pl.program_id(0) == 0)
        def _():
            loss_ref[...] = jnp.zeros(loss_ref.shape, F32)
            dgp_ref[...] = jnp.zeros(dgp_ref.shape, F32)
            db_ref[...] = jnp.zeros(db_ref.shape, F32)

        gp = gp_ref[...]
        pe = _dot(p_ref[...], wp_ref[...])
        r = lax.rsqrt(jnp.mean(pe * pe, axis=-1, keepdims=True) + EPS)
        peh = pe * r
        e = peh * gp
        h2 = h2_ref[...]
        h2b = h2.astype(BF16)
        h2b_ref[...] = h2b
        gt = _sig(jnp.dot(h2b, wg_ref[...], preferred_element_type=F32) + b_ref[...])
        diff = h2 + e * gt - t_ref[...]
        loss_ref[...] += _colsum(diff * diff)
        dh3 = diff * (1.0 / D_MODEL)
        de = dh3 * gt
        dz = dh3 * e * gt * (1.0 - gt)
        db_ref[...] += _colsum(dz)
        dgp_ref[...] += _colsum(de * peh)
        dpeh = de * gp
        dpe = r * (dpeh - peh * jnp.mean(dpeh * peh, axis=-1, keepdims=True))
        dzb = dz.astype(BF16)
        dz_ref[...] = dzb
        dpe_ref[...] = dpe.astype(BF16)
        dh2_ref[...] = dh3 + _dot_nt(dzb, wg_ref[...])

    return pl.pallas_call(
        body, name="ple_loss", grid=(S // tm,),
        in_specs=[_rows(tm, PLE_DIM), _rows(tm, D_MODEL), _rows(tm, D_MODEL), _full(PLE_DIM, D_MODEL),
                  _full(D_MODEL, D_MODEL), _full(1, D_MODEL), _full(1, D_MODEL)],
        out_specs=[_rows(tm, D_MODEL)] * 4 + [_acc(1, D_MODEL)] * 3,
        out_shape=[_sds((S, D_MODEL), BF16), _sds((S, D_MODEL), BF16), _sds((S, D_MODEL), F32), _sds((S, D_MODEL), BF16)]
        + [_sds((1, D_MODEL), F32)] * 3,
        compiler_params=_cp(("arbitrary",)),
    )(p, h2, tgt, w_pp, w_pg, b_pg, g_ple)


def _wgrad(a, b, name, S):
    M = a.shape[1]
    N = b.shape[1]
    ts = min(2048, S)
    nsplit = 2 if M * N >= 2 * 1024 * 1024 else 1
    tn = N // nsplit

    def body(a_ref, b_ref, o_ref):
        @pl.when(pl.program_id(1) == 0)
        def _():
            o_ref[...] = jnp.zeros(o_ref.shape, F32)

        o_ref[...] += _dot_tn(a_ref[...], b_ref[...])

    return pl.pallas_call(
        body, name=name, grid=(nsplit, S // ts),
        in_specs=[pl.BlockSpec((ts, M), lambda j, s: (s, 0)), pl.BlockSpec((ts, tn), lambda j, s: (s, j))],
        out_specs=pl.BlockSpec((M, tn), lambda j, s: (0, j)), out_shape=_sds((M, N), F32),
        compiler_params=_cp(("parallel", "arbitrary")),
    )(a, b)


def _ffn_down_bwd(dh2, ff, g, w_down, dgate_f, dup_f, S):
    tm = min(512, S)
    tn = D_FF // 2

    def body(dh2_ref, ff_ref, g_ref, wd_ref, fg_ref, fu_ref, dff_ref, dgate_ref, dup_ref, dg_ref):
        @pl.when(pl.program_id(0) == 0)
        def _():
            dg_ref[...] = jnp.zeros(dg_ref.shape, F32)

        dff, ga = _rms_bwd(dh2_ref[...], ff_ref[...].astype(F32), g_ref[...])
        dg_ref[...] += _colsum(ga)
        dffb = dff.astype(BF16)
        dff_ref[...] = dffb
        for seg in range(2):
            sl = slice(seg * tn, (seg + 1) * tn)
            dact = _dot_nt(dffb, wd_ref[sl, :])
            dgate_ref[:, sl] = (dact * fg_ref[:, sl].astype(F32)).astype(BF16)
            dup_ref[:, sl] = (dact * fu_ref[:, sl].astype(F32)).astype(BF16)

    return pl.pallas_call(
        body, name="ffn_down_bwd", grid=(S // tm,),
        in_specs=[_rows(tm, D_MODEL), _rows(tm, D_MODEL), _full(1, D_MODEL), _full(D_FF, D_MODEL), _rows(tm, D_FF),
                  _rows(tm, D_FF)],
        out_specs=[_rows(tm, D_MODEL), _rows(tm, D_FF), _rows(tm, D_FF), _acc(1, D_MODEL)],
        out_shape=[_sds((S, D_MODEL), BF16), _sds((S, D_FF), BF16), _sds((S, D_FF), BF16), _sds((1, D_MODEL), F32)],
        compiler_params=_cp(("arbitrary",)),
    )(dh2, ff, g, w_down, dgate_f, dup_f)


def _ffn_up_bwd(dgate, dup, w_gate, w_up, h1, mix, dh2, g_pre, g_post, w_o, S, grads=()):
    tm = min(512, S)
    n = len(grads)
    last = S // tm - 1

    def body(dgate_ref, dup_ref, wg_ref, wu_ref, h1_ref, mix_ref, dh2_ref, g2_ref, g1_ref, wo_ref, *rest):
        g_ins = rest[:n]
        dh1_ref, dmix_ref, dro_ref, dmo_ref, dg2_ref, dg1_ref = rest[n:n + 6]
        g_outs, sems = rest[n + 6:2 * n + 6], rest[2 * n + 6:]

        @pl.when(pl.program_id(0) == 0)
        def _():
            dg2_ref[...] = jnp.zeros(dg2_ref.shape, F32)
            dg1_ref[...] = jnp.zeros(dg1_ref.shape, F32)
            for cp in (_swap_copies(g_ins, g_outs, sems) if n else []):
                cp.start()

        dhn = (jnp.dot(dgate_ref[...], wg_ref[...], preferred_element_type=F32)
               + jnp.dot(dup_ref[...], wu_ref[...], preferred_element_type=F32))
        d1, ga = _rms_bwd(dhn, h1_ref[...], g2_ref[...])
        dg2_ref[...] += _colsum(ga)
        dh1 = dh2_ref[...] + d1
        dh1_ref[...] = dh1
        dmix, gb = _rms_bwd(dh1, mix_ref[...].astype(F32), g1_ref[...])
        dg1_ref[...] += _colsum(gb)
        dmixb = dmix.astype(BF16)
        dmix_ref[...] = dmixb
        dcat = _dot_nt(dmixb, wo_ref[...])
        dro_ref[...] = dcat[:, 0:512].astype(BF16)
        dmo_ref[...] = dcat[:, 512:1024].astype(BF16)

        if n:
            @pl.when(pl.program_id(0) == last)
            def _():
                for cp in _swap_copies(g_ins, g_outs, sems):
                    cp.wait()

    dh1, dmix, dro, dmo, dg2, dg1, *got = pl.pallas_call(
        body, name="ffn_up_bwd", grid=(S // tm,),
        in_specs=[_rows(tm, D_FF), _rows(tm, D_FF), _full(D_FF, D_MODEL), _full(D_FF, D_MODEL), _rows(tm, D_MODEL),
                  _rows(tm, D_MODEL), _rows(tm, D_MODEL), _full(1, D_MODEL), _full(1, D_MODEL), _full(D_MODEL, D_MODEL)]
        + [_ANY] * n,
        out_specs=[_rows(tm, D_MODEL), _rows(tm, D_MODEL), _rows(tm, 512), _rows(tm, 512), _acc(1, D_MODEL),
                   _acc(1, D_MODEL)] + [_ANY] * n,
        out_shape=[_sds((S, D_MODEL), F32), _sds((S, D_MODEL), BF16), _sds((S, 512), BF16), _sds((S, 512), BF16),
                   _sds((1, D_MODEL), F32), _sds((1, D_MODEL), F32)] + _swap_out_shapes(grads),
        scratch_shapes=_swap_sems(n) if n else [],
        compiler_params=_cp(("arbitrary",)),
    )(dgate, dup, w_gate, w_up, h1, mix, dh2, g_pre, g_post, w_o, *grads)
    return dh1, dmix, dro, dmo, dg2, dg1, got


def _attn_delta(o, do, S, grads=()):
    tm = min(512, S)
    n = len(grads)
    last = S // tm - 1

    def body(o_ref, do_ref, *rest):
        g_ins, (dot_ref, d_ref), g_outs, sems = rest[:n], rest[n:n + 2], rest[n + 2:2 * n + 2], rest[2 * n + 2:]
        if n:
            @pl.when(pl.program_id(0) == 0)
            def _():
                for cp in _swap_copies(g_ins, g_outs, sems):
                    cp.start()

        do = do_ref[...].astype(F32)
        prod_t = (o_ref[...].astype(F32) * do).T
        dot_ref[...] = do.T.astype(BF16)
        for h in range(MLA_HEADS):
            d_ref[h // 2, (h % 2):(h % 2) + 1, :] = jnp.sum(prod_t[h * 64:(h + 1) * 64, :], axis=0, keepdims=True)

        if n:
            @pl.when(pl.program_id(0) == last)
            def _():
                for cp in _swap_copies(g_ins, g_outs, sems):
                    cp.wait()

    dot, delta, *got = pl.pallas_call(
        body, name="attn_delta", grid=(S // tm,),
        in_specs=[_rows(tm, 512), _rows(tm, 512)] + [_ANY] * n,
        out_specs=[pl.BlockSpec((512, tm), lambda i: (0, i)), pl.BlockSpec((MLA_HEADS // 2, 2, tm), lambda i: (0, 0, i))]
        + [_ANY] * n,
        out_shape=[_sds((512, S), BF16), _sds((MLA_HEADS // 2, 2, S), F32)] + _swap_out_shapes(grads),
        scratch_shapes=_swap_sems(n) if n else [],
        compiler_params=_cp(("arbitrary",)),
    )(o, do, *grads)
    return dot, delta, got


def _flash_bwd(qp, kp, kt, v, do, dot, lse, delta, S, sums=()):
    tq = min(512, S)
    nq = S // tq
    RB = ATT_ROWS
    NH = BWD_HEADS
    qb_of, kb_of, T = _tri_pairs(nq, k_major=True)
    n = len(sums)
    steps = (MLA_HEADS // NH) * T

    def body(qb_ref, kb_ref, q_ref, k_ref, kt_ref, v_ref, do_ref, dot_ref, lse_ref, dl_ref, *rest):
        g_ins, (dq_ref, dk_ref, dv_ref), g_outs = rest[:n], rest[n:n + 3], rest[n + 3:2 * n + 3]
        dk_sc, dv_sc, s_sc, dp_sc, p_sc, ds_sc = rest[2 * n + 3:2 * n + 9]
        sems = rest[2 * n + 9:]
        t = pl.program_id(1)
        qb = qb_ref[t]
        kb = kb_ref[t]
        lin = pl.program_id(0) * T + t

        if n:
            @pl.when(lin == 0)
            def _():
                for cp in _scatter_copies(g_ins, g_outs, sems):
                    cp.start()

        @pl.when(t == 0)
        def _():
            dq_ref[...] = jnp.zeros(dq_ref.shape, F32)

        @pl.when(qb == kb)
        def _():
            dk_sc[...] = jnp.zeros(dk_sc.shape, F32)
            dv_sc[...] = jnp.zeros(dv_sc.shape, F32)

        lane = lax.broadcasted_iota(jnp.int32, (tq, 64 * NH), 1)

        def step(masked):
            vv = v_ref[...]
            do_all = do_ref[...]
            mine = [(lane >= a * 64) & (lane < (a + 1) * 64) for a in range(NH)]
            for a in range(NH):
                sl = slice(a * 128, (a + 1) * 128)
                s_sc[a] = _dot_nt(k_ref[:, sl], q_ref[:, sl])
                dp_sc[a] = jnp.dot(jnp.where(mine[a], vv, jnp.zeros_like(vv)), dot_ref[...],
                                   preferred_element_type=F32)
            for a in range(NH):
                sl = slice(a * 128, (a + 1) * 128)
                lse = lse_ref[a:a + 1, :]
                dl = dl_ref[a:a + 1, :]
                for r in range(0, tq, RB):
                    sc = s_sc[a, r:r + RB, :]
                    if masked:
                        sc = jnp.where(_causal_keep(r, RB, tq), sc, NEG)
                    p = jnp.exp(sc - lse)
                    p_sc[a, r:r + RB, :] = p.astype(BF16)
                    ds_sc[a, r:r + RB, :] = (p * (dp_sc[a, r:r + RB, :] - dl)).astype(BF16)
                ds = ds_sc[a]
                dv_sc[...] += jnp.dot(p_sc[a], jnp.where(mine[a], do_all, jnp.zeros_like(do_all)),
                                      preferred_element_type=F32)
                dk_sc[:, sl] += jnp.dot(ds, q_ref[:, sl], preferred_element_type=F32)
                dq_ref[qb, sl, :] += jnp.dot(kt_ref[sl, :], ds, preferred_element_type=F32)

        @pl.when(qb > kb)
        def _():
            step(False)

        @pl.when(qb == kb)
        def _():
            step(True)

        @pl.when(qb == nq - 1)
        def _():
            dk_ref[...] = dk_sc[...].astype(BF16)
            dv_ref[...] = dv_sc[...].astype(BF16)

        if n:
            @pl.when(lin == steps - 1)
            def _():
                for cp in _scatter_copies(g_ins, g_outs, sems):
                    cp.wait()

    grid_spec = pltpu.PrefetchScalarGridSpec(
        num_scalar_prefetch=2, grid=(MLA_HEADS // NH, T),
        in_specs=[pl.BlockSpec((tq, 128 * NH), lambda j, t, qb, kb: (qb[t], j)),
                  pl.BlockSpec((tq, 128 * NH), lambda j, t, qb, kb: (kb[t], j)),
                  pl.BlockSpec((128 * NH, tq), lambda j, t, qb, kb: (j, kb[t])),
                  pl.BlockSpec((tq, 64 * NH), lambda j, t, qb, kb: (kb[t], j)),
                  pl.BlockSpec((tq, 64 * NH), lambda j, t, qb, kb: (qb[t], j)),
                  pl.BlockSpec((64 * NH, tq), lambda j, t, qb, kb: (j, qb[t])),
                  pl.BlockSpec((None, NH, tq), lambda j, t, qb, kb: (j, 0, qb[t])),
                  pl.BlockSpec((None, NH, tq), lambda j, t, qb, kb: (j, 0, qb[t]))] + [_ANY] * n,
        out_specs=[pl.BlockSpec((nq, 128 * NH, tq), lambda j, t, qb, kb: (0, j, 0), pipeline_mode=pl.Buffered(1)),
                   pl.BlockSpec((tq, 128 * NH), lambda j, t, qb, kb: (kb[t], j)),
                   pl.BlockSpec((tq, 64 * NH), lambda j, t, qb, kb: (kb[t], j))] + [_ANY] * n,
        scratch_shapes=[pltpu.VMEM((tq, 128 * NH), F32), pltpu.VMEM((tq, 64 * NH), F32), pltpu.VMEM((NH, tq, tq), F32),
                        pltpu.VMEM((NH, tq, tq), F32), pltpu.VMEM((NH, tq, tq), BF16), pltpu.VMEM((NH, tq, tq), BF16)]
        + (_scatter_sems(n) if n else []),
    )
    dq, dk, dv, *parts = pl.pallas_call(
        body, name="flash_bwd", grid_spec=grid_spec,
        out_shape=[_sds((nq, 1024, tq), F32), _sds((S, 1024), BF16), _sds((S, 512), BF16)] + _scatter_out_shapes(sums),
        compiler_params=_cp(("arbitrary", "arbitrary")),
    )(qb_of, kb_of, qp, kp, kt, v, do, dot, lse.reshape(MLA_HEADS // NH, NH, S), delta.reshape(MLA_HEADS // NH, NH, S),
      *sums)
    return dq, dk, dv, parts


def _mla_up_bwd(dqp, dkp, dv, cq, ckv, gq, gkv, w_uq, w_ukv, tabs, S):
    tm = min(512, S)

    def body(dq_ref, dk_ref, dv_ref, cq_ref, ckv_ref, gq_ref, gkv_ref, wuq_ref, wukv_ref, cm_ref, sa_ref, sb_ref,
             dqh_ref, dkv_ref, dcq_ref, dckv_ref, dkr_ref, dgq_ref, dgkv_ref):
        @pl.when(pl.program_id(0) == 0)
        def _():
            dgq_ref[...] = jnp.zeros(dgq_ref.shape, F32)
            dgkv_ref[...] = jnp.zeros(dgkv_ref.shape, F32)

        cm = cm_ref[...]
        sa = sa_ref[...]
        sb = sb_ref[...]
        lane = lax.broadcasted_iota(jnp.int32, (tm, 128), 1)
        dkr_r = jnp.zeros((tm, 128), F32)
        for h in range(MLA_HEADS):
            sl = slice(h * 128, (h + 1) * 128)
            dqh_ref[:, sl] = (_unrope_mla(dq_ref[sl, :].T, cm, sa, sb) * SCALE_MLA).astype(BF16)
            gk = dk_ref[:, sl]
            dkr_r = dkr_r + gk.astype(F32)
            dkv_ref[:, sl] = gk
        dkr_r = jnp.where((lane >= 64) & (lane < 96), dkr_r, 0.0)
        dkr_ref[...] = _unrope_mla(dkr_r, cm, sa, sb).astype(BF16)
        dkv_ref[:, 1024:1536] = dv_ref[...]
        dcq, ga = _rms_bwd(_dot_nt(dqh_ref[...], wuq_ref[...]), cq_ref[...], gq_ref[...])
        dcq_ref[...] = dcq.astype(BF16)
        dgq_ref[...] += _colsum(ga)
        dckv, gb = _rms_bwd(_dot_nt(dkv_ref[...], wukv_ref[...]), ckv_ref[...], gkv_ref[...])
        dckv_ref[...] = dckv.astype(BF16)
        dgkv_ref[...] += _colsum(gb)

    per_q = dqp.shape[2] // tm
    return pl.pallas_call(
        body, name="mla_up_bwd", grid=(S // tm,),
        in_specs=[pl.BlockSpec((None, 1024, tm), lambda i: (i // per_q, 0, i % per_q)),
                  _rows(tm, 1024), _rows(tm, 512), _rows(tm, Q_LORA), _rows(tm, KV_LORA),
                  _full(1, Q_LORA), _full(1, KV_LORA), _full(Q_LORA, 1024), _full(KV_LORA, 1536)] + [_rows(tm, 128)] * 3,
        out_specs=[_rows(tm, 1024), _rows(tm, 1536), _rows(tm, Q_LORA), _rows(tm, KV_LORA), _rows(tm, 128),
                   _acc(1, Q_LORA), _acc(1, KV_LORA)],
        out_shape=[_sds((S, 1024), BF16), _sds((S, 1536), BF16), _sds((S, Q_LORA), BF16), _sds((S, KV_LORA), BF16),
                   _sds((S, 128), BF16), _sds((1, Q_LORA), F32), _sds((1, KV_LORA), F32)],
        compiler_params=_cp(("arbitrary",)),
    )(dqp, dkp, dv, cq, ckv, gq, gkv, w_uq, w_ukv, *tabs[2:])


def _ret_bwd(rq, rk, rv, rprev, ry, rg, dro, gn_w, tabs, S):
    C = RET_CHUNK
    N = S // C
    G = min(RET_GROUP_BWD, N)
    NB = N // G

    def body(lg_ref, q_ref, k_ref, v_ref, rp_ref, ry_ref, rg_ref, dro_ref, w_ref, cr_ref, sr_ref,
             drq_ref, drk_ref, drv_ref, drg_ref, dw_ref, g_sc):
        @pl.when(pl.program_id(1) == 0)
        def _():
            g_sc[...] = jnp.zeros(g_sc.shape, F32)
            dw_ref[...] = jnp.zeros(dw_ref.shape, F32)

        dmat, zeta, xi, g_chunk = _decay_terms(lg_ref)
        w = w_ref[...]
        gacc = g_sc[...]
        dw = jnp.zeros((1, 128), F32)
        for i in reversed(range(G)):
            rows = slice(i * C, (i + 1) * C)
            ry = ry_ref[rows, :]
            mu = jnp.mean(ry, axis=-1, keepdims=True)
            yc = ry - mu
            rstd = lax.rsqrt(jnp.mean(yc * yc, axis=-1, keepdims=True) + EPS)
            yh = yc * rstd
            g = rg_ref[rows, :]
            s = _sig(g)
            dout = dro_ref[rows, :].astype(F32)
            drg_ref[rows, :] = (dout * (yh * w) * (s * (1.0 + g * (1.0 - s)))).astype(BF16)
            dgn = dout * (g * s)
            dw = dw + _colsum(dgn * yh)
            dyh = dgn * w
            dry = rstd * (dyh - jnp.mean(dyh, axis=-1, keepdims=True) - yh * jnp.mean(dyh * yh, axis=-1, keepdims=True))
            do = dry.astype(BF16)

            q = q_ref[rows, :]
            k = k_ref[rows, :]
            v = v_ref[rows, :]
            gfut = gacc.astype(BF16)
            sc = (_dot_nt(q, k) * dmat).astype(BF16)
            dsc = (_dot_nt(do, v) * dmat).astype(BF16)
            dq = jnp.dot(dsc, k, preferred_element_type=F32) + _dot_nt(do, rp_ref[i]) * xi
            dk = _dot_tn(dsc, q) + _dot_nt(v, gfut) * zeta
            dv = _dot_tn(sc, do) + jnp.dot(k, gfut, preferred_element_type=F32) * zeta
            gacc = g_chunk * gacc + _dot_tn(q, xi * dry)
            cr = cr_ref[rows, :]
            sr = sr_ref[rows, :]
            drq_ref[rows, :] = _unrope_ret(dq, cr, sr).astype(BF16)
            drk_ref[rows, :] = _unrope_ret(dk * SCALE_RET, cr, sr).astype(BF16)
            drv_ref[rows, :] = dv.astype(BF16)
        g_sc[...] = gacc
        dw_ref[...] += dw

    blk = pl.BlockSpec((G * C, 128), lambda h, n: (NB - 1 - n, h))
    tab = pl.BlockSpec((G * C, 128), lambda h, n: (NB - 1 - n, 0))
    return pl.pallas_call(
        body, name="ret_bwd", grid=(RET_HEADS, NB),
        in_specs=[pl.BlockSpec((None, 8, 128), lambda h, n: (h, 0, 0)), blk, blk, blk,
                  pl.BlockSpec((G, 128, 128), lambda h, n: (h * NB + NB - 1 - n, 0, 0)), blk, blk, blk,
                  pl.BlockSpec((1, 128), lambda h, n: (0, h)), tab, tab],
        out_specs=[blk, blk, blk, blk, pl.BlockSpec((1, 128), lambda h, n: (0, h))],
        out_shape=[_sds((S, 512), BF16)] * 4 + [_sds((1, 512), F32)],
        scratch_shapes=[pltpu.VMEM((128, 128), F32)],
        compiler_params=_cp(("parallel", "arbitrary")),
    )(_decay_table(), rq, rk, rv, rprev, ry, rg, dro, gn_w, tabs[0], tabs[1])


def _inproj_bwd(drq, drk, drv, drg, dcq, dckv, dkr, w_in, dh1, x, g, S):
    tm = min(512, S)

    def body(drq_ref, drk_ref, drv_ref, drg_ref, dcq_ref, dckv_ref, dkr_ref, w_ref, dh1_ref, x_ref, g_ref,
             gx_ref, dproj_ref, dg_ref):
        @pl.when(pl.program_id(0) == 0)
        def _():
            dg_ref[...] = jnp.zeros(dg_ref.shape, F32)

        dproj_ref[:, 0:512] = drq_ref[...]
        dproj_ref[:, 512:1024] = drk_ref[...]
        dproj_ref[:, 1024:1536] = drv_ref[...]
        dproj_ref[:, 1536:2048] = drg_ref[...]
        dproj_ref[:, 2048:2432] = dcq_ref[...]
        dproj_ref[:, 2432:2688] = dckv_ref[...]
        dproj_ref[:, 2688:2816] = pltpu.roll(dkr_ref[...].astype(F32), 64, 1).astype(BF16)
        dx, ga = _rms_bwd(_dot_nt(dproj_ref[...], w_ref[...]), x_ref[...], g_ref[...])
        gx_ref[...] = dh1_ref[...] + dx
        dg_ref[...] += _colsum(ga)

    return pl.pallas_call(
        body, name="inproj_bwd", grid=(S // tm,),
        in_specs=[_rows(tm, 512)] * 4 + [_rows(tm, Q_LORA), _rows(tm, KV_LORA), _rows(tm, 128),
                                         _full(D_MODEL, IN_COLS_P), _rows(tm, D_MODEL), _rows(tm, D_MODEL),
                                         _full(1, D_MODEL)],
        out_specs=[_rows(tm, D_MODEL), _rows(tm, IN_COLS_P), _acc(1, D_MODEL)],
        out_shape=[_sds((S, D_MODEL), F32), _sds((S, IN_COLS_P), BF16), _sds((1, D_MODEL), F32)],
        compiler_params=_cp(("arbitrary",)),
    )(drq, drk, drv, drg, dcq, dckv, dkr, w_in, dh1, x, g)


def _pad_weights(w):
    w_in_p = jnp.pad(w["w_in"], ((0, 0), (0, IN_COLS_P - IN_COLS)))
    w_uq_p = jnp.pad(w["w_uq"].reshape(Q_LORA, MLA_HEADS, 96), ((0, 0), (0, 0), (0, 32))).reshape(Q_LORA, 1024)
    ukv = w["w_ukv"].reshape(KV_LORA, MLA_HEADS, 128)
    k_part = jnp.pad(ukv[:, :, :64], ((0, 0), (0, 0), (0, 64))).reshape(KV_LORA, 1024)
    w_ukv_p = jnp.concatenate([k_part, ukv[:, :, 64:].reshape(KV_LORA, 512)], axis=1)
    return w_in_p, w_uq_p, w_ukv_p


BIG_SPEC = {n: (r, c, ax) for n, r, c, ax in BIG}
COLUMN_MAJOR = ("w_in", "w_uq", "w_gate", "w_up")
GRAD_TRANSPOSED = ("w_gate", "w_up")
GATHER_FIRST = ("w_in", "w_uq", "w_ukv")
GATHER_LATE = tuple(n for n, _, _, _ in BIG if n not in GATHER_FIRST)
REDUCE_EARLY = ("w_ple_gate", "w_ple_proj", "w_down", "w_gate", "w_up", "w_o")
REDUCE_LAST = tuple(n for n, _, _, _ in BIG if n not in REDUCE_EARLY)


def _local_step(x, p, pos_f, tgt, w, sm, late_shards=None, c_idx=None):
    S = x.shape[0]
    spread = late_shards is not None
    w = dict(w)
    tabs, first = _rope_tables(pos_f, S, [late_shards[n] for n in GATHER_FIRST] if spread else ())
    for i, n in enumerate(GATHER_FIRST if spread else ()):
        w[n] = _from_chips(first[i], BIG_SPEC[n][2])
    w_in_p, w_uq_p, w_ukv_p = _pad_weights(w)

    xn, rq, rk, rv, rg, cq, ckv, kr = _inproj(x, sm["pre_mix_norm"], w_in_p, tabs, S)
    cqn, ckvn, qp, kp, v, kt, vt = _mla_up(cq, ckv, kr, sm["mla_q_norm"], sm["mla_kv_norm"], w_uq_p, w_ukv_p, tabs, S)
    mo, lse, gathered = _flash_fwd(qp, kp, vt, S, [late_shards[n] for n in GATHER_LATE] if spread else ())
    for i, n in enumerate(GATHER_LATE if spread else ()):
        w[n] = _from_chips(gathered[i], 0 if n in GRAD_TRANSPOSED else BIG_SPEC[n][2])
    if not spread:
        w.update({n: w[n].T for n in GRAD_TRANSPOSED})
    ry, ro, rprev = _ret_fwd(rq, rk, rv, rg, sm["ret_gn_w"], S)
    mix, h1, hn = _outproj(ro, mo, x, w["w_o"], sm["post_mix_norm"], sm["pre_ffn_norm"], S)
    dgate_f, dup_f, act = _ffn_up(hn, w["w_gate"], w["w_up"], S)
    ff, h2 = _ffn_down(act, w["w_down"], h1, sm["post_ffn_norm"], S)
    dz, dpe, dh2, h2b, loss_vec, d_ple_norm, d_b = _ple_loss(
        p, h2, tgt, w["w_ple_proj"], w["w_ple_gate"], sm["b_ple_gate"], sm["ple_norm"], S)

    gw = {}
    gs = {"ple_norm": d_ple_norm, "b_ple_gate": d_b}
    gw["w_ple_gate"] = _wgrad(h2b, dz, "wgrad_ple_gate", S)
    gw["w_ple_proj"] = _wgrad(p, dpe, "wgrad_ple_proj", S)
    dff, dgate, dup, gs["post_ffn_norm"] = _ffn_down_bwd(dh2, ff, sm["post_ffn_norm"], w["w_down"], dgate_f, dup_f, S)
    gw["w_down"] = _wgrad(act, dff, "wgrad_down", S)
    if spread:
        gw["w_gate"] = _wgrad(dgate, hn, "wgrad_gate", S)
        gw["w_up"] = _wgrad(dup, hn, "wgrad_up", S)
    else:
        gw["w_gate"] = _wgrad(hn, dgate, "wgrad_gate", S)
        gw["w_up"] = _wgrad(hn, dup, "wgrad_up", S)
    first = REDUCE_EARLY[:-1]
    g4 = [_by_chip(gw.pop(n), *((D_FF, D_MODEL, 0) if n in GRAD_TRANSPOSED else BIG_SPEC[n]))
          for n in first] if spread else []
    dh1, dmix, dro, dmo, gs["pre_ffn_norm"], gs["post_mix_norm"], got = _ffn_up_bwd(
        dgate, dup, w["w_gate"], w["w_up"], h1, mix, dh2, sm["pre_ffn_norm"], sm["post_mix_norm"], w["w_o"], S, g4)
    gw["w_o"] = jnp.concatenate([_wgrad(ro, dmix, "wgrad_o_ret", S), _wgrad(mo, dmix, "wgrad_o_mla", S)], axis=0)
    g4_o = [_by_chip(gw.pop("w_o"), *BIG_SPEC["w_o"])] if spread else []

    dmo_t, delta, got_o = _attn_delta(mo, dmo, S, g4_o)
    sums = [_add_half_rows(a, b, c_idx, "rs_add_halves_" + n)
            for n, a, b in zip(REDUCE_EARLY, g4 + g4_o, list(got) + list(got_o))] if spread else []
    dqp, dkp, dv, parts = _flash_bwd(qp, kp, kt, v, dmo, dmo_t, lse, delta, S, sums)
    dqh, dkv, dcq, dckv, dkr, gs["mla_q_norm"], gs["mla_kv_norm"] = _mla_up_bwd(
        dqp, dkp, dv, cq, ckv, sm["mla_q_norm"], sm["mla_kv_norm"], w_uq_p, w_ukv_p, tabs, S)
    g_uq_p = _wgrad(cqn, dqh, "wgrad_uq", S)
    g_ukv_p = _wgrad(ckvn, dkv, "wgrad_ukv", S)
    gw["w_uq"] = g_uq_p.reshape(Q_LORA, MLA_HEADS, 128)[:, :, :96].reshape(Q_LORA, 768)
    gw["w_ukv"] = jnp.concatenate(
        [g_ukv_p[:, :1024].reshape(KV_LORA, MLA_HEADS, 128)[:, :, :64], g_ukv_p[:, 1024:].reshape(KV_LORA, MLA_HEADS, 64)],
        axis=2).reshape(KV_LORA, 1024)

    drq, drk, drv, drg, gs["ret_gn_w"] = _ret_bwd(rq, rk, rv, rprev, ry, rg, dro, sm["ret_gn_w"], tabs, S)
    grad_x, dproj, gs["pre_mix_norm"] = _inproj_bwd(drq, drk, drv, drg, dcq, dckv, dkr, w_in_p, dh1, x,
                                                    sm["pre_mix_norm"], S)
    g_in_p = _wgrad(xn, dproj, "wgrad_in", S)
    gw["w_in"] = g_in_p[:, :IN_COLS]
    return loss_vec, grad_x, gw, gs, ((sums, parts) if spread else None)


def _my_place():
    x = lax.axis_index("x")
    y = lax.axis_index("y")
    c = lax.axis_index("c")
    return x, y, c


def _other_chips(x, y):
    return [(1 - x, y), (x, 1 - y), (1 - x, 1 - y)]


_ANY = pl.BlockSpec(memory_space=pl.ANY)


def _small_copies(v_ref, slots, sems):
    send, recv, lsem = sems
    x, y, c = _my_place()
    me = 4 * x + 2 * y + c
    cps = [pltpu.make_async_copy(v_ref, slots.at[me], lsem)]
    for r in range(1, N_DEV):
        peer = (x ^ (r >> 2), y ^ ((r >> 1) & 1), c ^ (r & 1))
        cps.append(pltpu.make_async_remote_copy(
            src_ref=v_ref, dst_ref=slots.at[me], send_sem=send.at[r - 1], recv_sem=recv.at[r - 1],
            device_id=peer, device_id_type=MESH))
    return cps


def _small_sum(slots, out_ref):
    acc = slots[0]
    for d in range(1, N_DEV):
        acc = acc + slots[d]
    out_ref[...] = acc
    loss = jnp.sum(acc[9:10, :], axis=1, keepdims=True) * (0.5 / D_MODEL)
    out_ref[9:10, :] = jnp.broadcast_to(loss, (1, PACK_COLS))


def _small_scratch():
    return [pltpu.VMEM((N_DEV, SMALL_ROWS, PACK_COLS), F32), pltpu.SemaphoreType.DMA((N_DEV - 1,)),
            pltpu.SemaphoreType.DMA((N_DEV - 1,)), pltpu.SemaphoreType.DMA]


N_BIG = len(BIG)


def _half(c, rows, align):
    h = rows // 2
    return pl.ds(pl.multiple_of(c * h, align), h)


def _gather_out_shapes(shards):
    return [_sds((N_CHIPS,) + tuple(s.shape), BF16) for s in shards]


def _gather_sems(n):
    return [pltpu.SemaphoreType.DMA((n, 3))] * 4 + [pltpu.SemaphoreType.DMA((n,))] * 2


def _gather_phase(phase, ins, outs, sems):
    send1, recv1, send2, recv2, send3, recv3 = sems
    x, y, c = _my_place()
    me = 2 * x + y
    chips = _other_chips(x, y)
    sib = (x, y, 1 - c)
    for t in range(len(ins)):
        rows = ins[t].shape[0]
        half = _half(c, rows, 16)
        other = _half(1 - c, rows, 16)
        def own():
            return pltpu.make_async_remote_copy(
                src_ref=ins[t], dst_ref=outs[t].at[me], send_sem=send3.at[t], recv_sem=recv3.at[t],
                device_id=sib, device_id_type=MESH)

        if phase == 0:
            own().start()
        if phase == 2:
            own().wait()
        for k, (cx, cy) in enumerate(chips):
            src = 2 * cx + cy

            def over_ici(slab):
                return pltpu.make_async_remote_copy(
                    src_ref=ins[t].at[half], dst_ref=outs[t].at[slab, half], send_sem=send1.at[t, k],
                    recv_sem=recv1.at[t, k], device_id=(cx, cy, c), device_id_type=MESH)

            def over_d2d(rows):
                return pltpu.make_async_remote_copy(
                    src_ref=outs[t].at[src, rows], dst_ref=outs[t].at[src, rows], send_sem=send2.at[t, k],
                    recv_sem=recv2.at[t, k], device_id=sib, device_id_type=MESH)

            if phase == 0:
                over_ici(me).start()
            if phase == 1:
                over_ici(src).wait_recv()
                over_d2d(half).start()
            if phase == 2:
                over_d2d(other).wait_recv()
                over_ici(me).wait_send()
                over_d2d(half).wait_send()


def _swap_copies(ins, outs, sems):
    send, recv = sems
    x, y, c = _my_place()
    return [pltpu.make_async_remote_copy(
        src_ref=ins[t].at[:, _half(1 - c, ins[t].shape[1], 8)], dst_ref=outs[t], send_sem=send.at[t],
        recv_sem=recv.at[t], device_id=(x, y, 1 - c), device_id_type=MESH) for t in range(len(ins))]


def _swap_out_shapes(gs):
    return [_sds((N_CHIPS, g.shape[1] // 2, g.shape[2]), F32) for g in gs]


def _swap_sems(n):
    return [pltpu.SemaphoreType.DMA((n,)), pltpu.SemaphoreType.DMA((n,))]


def _swap_half_rows(gs):
    n = len(gs)

    def body(*refs):
        cps = _swap_copies(refs[:n], refs[n:2 * n], refs[2 * n:])
        for cp in cps:
            cp.start()
        for cp in cps:
            cp.wait()

    return pl.pallas_call(
        body, name="rs_swap_halves",
        in_specs=[_ANY] * n, out_specs=[_ANY] * n, out_shape=_swap_out_shapes(gs), scratch_shapes=_swap_sems(n),
    )(*gs)


def _add_half_rows(g, got, c_idx, name):
    _, rows, cols = g.shape
    h = rows // 2

    def body(c_ref, a_ref, b_ref, o_ref):
        o_ref[...] = (a_ref[...] + b_ref[...]).astype(BF16)

    grid_spec = pltpu.PrefetchScalarGridSpec(
        num_scalar_prefetch=1, grid=(N_CHIPS,),
        in_specs=[pl.BlockSpec((None, h, cols), lambda j, c: (j, c[0], 0)),
                  pl.BlockSpec((None, h, cols), lambda j, c: (j, 0, 0))],
        out_specs=pl.BlockSpec((None, h, cols), lambda j, c: (j, 0, 0)),
    )
    return pl.pallas_call(
        body, name=name, grid_spec=grid_spec, out_shape=_sds((N_CHIPS, h, cols), BF16),
        compiler_params=_cp(("parallel",)),
    )(c_idx, g, got)


def _scatter_to_chips(ts, vec):
    n = len(ts)

    def body(*refs):
        ins, v_ref, outs, small_ref = refs[:n], refs[n], refs[n + 1:2 * n + 1], refs[2 * n + 1]
        slots, small_sems, sems = refs[2 * n + 2], refs[2 * n + 3:2 * n + 6], refs[2 * n + 6:]
        small = _small_copies(v_ref, slots, small_sems)
        cps = _scatter_copies(ins, outs, sems)
        for cp in small + cps:
            cp.start()
        for cp in small:
            cp.wait()
        _small_sum(slots, small_ref)
        for cp in cps:
            cp.wait()

    vm = pl.BlockSpec(memory_space=pltpu.VMEM)
    *parts, small_sum = pl.pallas_call(
        body, name="rs_scatter_chips",
        in_specs=[_ANY] * n + [vm], out_specs=[_ANY] * n + [vm],
        out_shape=_scatter_out_shapes(ts) + [_sds((SMALL_ROWS, PACK_COLS), F32)],
        scratch_shapes=_small_scratch() + _scatter_sems(n),
    )(*ts, vec)
    return parts, small_sum


def _scatter_copies(ins, outs, sems):
    send, recv = sems
    x, y, c = _my_place()
    return [pltpu.make_async_remote_copy(
        src_ref=ins[t].at[2 * cx + cy], dst_ref=outs[t].at[k], send_sem=send.at[t, k], recv_sem=recv.at[t, k],
        device_id=(cx, cy, c), device_id_type=MESH)
        for t in range(len(ins)) for k, (cx, cy) in enumerate(_other_chips(x, y))]


def _scatter_out_shapes(ts):
    return [_sds((3,) + tuple(t.shape[1:]), BF16) for t in ts]


def _scatter_sems(n):
    return [pltpu.SemaphoreType.DMA((n, 3)), pltpu.SemaphoreType.DMA((n, 3))]


def _add_four(mine, parts, place, name):
    _, h, cols = parts.shape

    def body(pl_ref, m_ref, p_ref, o_ref):
        o_ref[...] = ((m_ref[...].astype(F32) + p_ref[0].astype(F32)) + p_ref[1].astype(F32)) + p_ref[2].astype(F32)

    grid_spec = pltpu.PrefetchScalarGridSpec(
        num_scalar_prefetch=1, grid=(1,),
        in_specs=[pl.BlockSpec((None, h, cols), lambda i, pc: (pc[0], 0, 0)),
                  pl.BlockSpec((3, h, cols), lambda i, pc: (0, 0, 0))],
        out_specs=pl.BlockSpec((h, cols), lambda i, pc: (pc[1], 0)),
    )
    return pl.pallas_call(
        body, name=name, grid_spec=grid_spec, out_shape=_sds((2 * h, cols), F32),
        compiler_params=_cp(("arbitrary",)),
    )(place, mine, parts)


def _join_half_rows(rs):
    n = len(rs)

    def body(*refs):
        ins, outs = refs[:n], refs[n:2 * n]
        send, recv = refs[2 * n:]
        x, y, c = _my_place()
        cps = []
        for t in range(n):
            half = _half(c, outs[t].shape[0], 8)
            rc = pltpu.make_async_remote_copy(
                src_ref=ins[t].at[half], dst_ref=outs[t].at[half], send_sem=send.at[t], recv_sem=recv.at[t],
                device_id=(x, y, 1 - c), device_id_type=MESH)
            rc.start()
            cps.append(rc)
        for cp in cps:
            cp.wait()

    return pl.pallas_call(
        body, name="rs_join_halves",
        in_specs=[_ANY] * n, out_specs=[_ANY] * n,
        out_shape=[_sds(r.shape, F32) for r in rs],
        input_output_aliases={i: i for i in range(n)},
        scratch_shapes=[pltpu.SemaphoreType.DMA((n,))] * 2,
    )(*rs)


def _by_chip(full, rows, cols, axis):
    if axis == 0:
        return full.reshape(N_CHIPS, rows // N_CHIPS, cols)
    return full.reshape(rows, N_CHIPS, cols // N_CHIPS).transpose(1, 0, 2)


def _from_chips(parts, axis):
    _, r, c = parts.shape
    if axis == 0:
        return parts.reshape(N_CHIPS * r, c)
    return parts.transpose(1, 0, 2).reshape(r, N_CHIPS * c)


def _adamw(wt, g, m, v, name):
    _, R, C = wt.shape
    tr = max(d for d in range(8, R + 1, 8) if R % d == 0 and (d * C <= 256 * 1024 or d == 8))

    def body(w_ref, g_ref, m_ref, v_ref, d_ref, nm_ref, nv_ref):
        gg = g_ref[...]
        m_new = ADAM_B1 * m_ref[...] + (1.0 - ADAM_B1) * gg
        v_new = ADAM_B2 * v_ref[...] + (1.0 - ADAM_B2) * (gg * gg)
        m_hat = m_new / (1.0 - ADAM_B1 ** ADAM_STEP)
        v_hat = v_new / (1.0 - ADAM_B2 ** ADAM_STEP)
        d_ref[...] = -ADAM_LR * (m_hat / (jnp.sqrt(v_hat) + ADAM_EPS) + ADAM_WD * w_ref[...])
        nm_ref[...] = m_new
        nv_ref[...] = v_new

    spec = pl.BlockSpec((None, tr, C), lambda i: (0, i, 0))
    return pl.pallas_call(
        body, name=name, grid=(R // tr,), in_specs=[spec, pl.BlockSpec((tr, C), lambda i: (i, 0)), spec, spec],
        out_specs=[spec] * 3, out_shape=[_sds((1, R, C), F32)] * 3,
        compiler_params=_cp(("parallel",)),
    )(wt, g, m, v)


def _pack_small(vals, loss_vec=None):
    rows = [jnp.pad(vals[n].reshape(-1), (0, PACK_COLS - sz)) for n, sz in SMALL]
    rows.append(loss_vec.reshape(-1) if loss_vec is not None else jnp.zeros((PACK_COLS,), F32))
    rows += [jnp.zeros((PACK_COLS,), F32)] * (SMALL_ROWS - len(rows))
    return jnp.stack(rows)


def kernel(x, p, positions, pre_mix_norm, w_in, ret_gn_w, mla_q_norm, w_uq, mla_kv_norm, w_ukv, w_o, post_mix_norm, pre_ffn_norm, w_gate, w_up, w_down, post_ffn_norm, w_ple_proj, ple_norm, w_ple_gate, b_ple_gate, loss_target, m_pre_mix_norm, m_w_in, m_ret_gn_w, m_mla_q_norm, m_w_uq, m_mla_kv_norm, m_w_ukv, m_w_o, m_post_mix_norm, m_pre_ffn_norm, m_w_gate, m_w_up, m_w_down, m_post_ffn_norm, m_w_ple_proj, m_ple_norm, m_w_ple_gate, m_b_ple_gate, v_pre_mix_norm, v_w_in, v_ret_gn_w, v_mla_q_norm, v_w_uq, v_mla_kv_norm, v_w_ukv, v_w_o, v_post_mix_norm, v_pre_ffn_norm, v_w_gate, v_w_up, v_w_down, v_post_ffn_norm, v_w_ple_proj, v_ple_norm, v_w_ple_gate, v_b_ple_gate):
    wts = dict(pre_mix_norm=pre_mix_norm, w_in=w_in, ret_gn_w=ret_gn_w, mla_q_norm=mla_q_norm, w_uq=w_uq,
               mla_kv_norm=mla_kv_norm, w_ukv=w_ukv, w_o=w_o, post_mix_norm=post_mix_norm, pre_ffn_norm=pre_ffn_norm,
               w_gate=w_gate, w_up=w_up, w_down=w_down, post_ffn_norm=post_ffn_norm, w_ple_proj=w_ple_proj,
               ple_norm=ple_norm, w_ple_gate=w_ple_gate, b_ple_gate=b_ple_gate)
    mom = dict(pre_mix_norm=m_pre_mix_norm, w_in=m_w_in, ret_gn_w=m_ret_gn_w, mla_q_norm=m_mla_q_norm, w_uq=m_w_uq,
               mla_kv_norm=m_mla_kv_norm, w_ukv=m_w_ukv, w_o=m_w_o, post_mix_norm=m_post_mix_norm,
               pre_ffn_norm=m_pre_ffn_norm, w_gate=m_w_gate, w_up=m_w_up, w_down=m_w_down, post_ffn_norm=m_post_ffn_norm,
               w_ple_proj=m_w_ple_proj, ple_norm=m_ple_norm, w_ple_gate=m_w_ple_gate, b_ple_gate=m_b_ple_gate)
    var = dict(pre_mix_norm=v_pre_mix_norm, w_in=v_w_in, ret_gn_w=v_ret_gn_w, mla_q_norm=v_mla_q_norm, w_uq=v_w_uq,
               mla_kv_norm=v_mla_kv_norm, w_ukv=v_w_ukv, w_o=v_w_o, post_mix_norm=v_post_mix_norm,
               pre_ffn_norm=v_pre_ffn_norm, w_gate=v_w_gate, w_up=v_w_up, w_down=v_w_down, post_ffn_norm=v_post_ffn_norm,
               w_ple_proj=v_w_ple_proj, ple_norm=v_ple_norm, w_ple_gate=v_w_ple_gate, b_ple_gate=v_b_ple_gate)

    S = x.shape[1]
    shard2d = {n: wts[n][0] for n, _, _, _ in BIG}
    small2d = {n: wts[n] for n, _ in SMALL}

    shard_bf = {n: (jnp.swapaxes(wts[n], 1, 2)[0] if n in GRAD_TRANSPOSED else shard2d[n]).astype(BF16) for n in shard2d}
    pos_f = positions.astype(F32).reshape(S, 1)
    c_idx = lax.axis_index("c").astype(jnp.int32).reshape(1)
    loss_vec, grad_x, gw, gs, (sums_early, parts_early) = _local_step(
        x[0], p[0, 0], pos_f, loss_target[0], {}, small2d, shard_bf, c_idx)

    g4 = [_by_chip(gw[n], *BIG_SPEC[n]) for n in REDUCE_LAST]
    got = _swap_half_rows(g4)
    sums_last = [_add_half_rows(g4[i], got[i], c_idx, "rs_add_halves_" + n) for i, n in enumerate(REDUCE_LAST)]
    parts_last, small_sum = _scatter_to_chips(sums_last, _pack_small(gs, loss_vec))
    place = jnp.stack([2 * lax.axis_index("x") + lax.axis_index("y"), lax.axis_index("c")]).astype(jnp.int32)
    names = REDUCE_EARLY + REDUCE_LAST
    reduced = _join_half_rows(
        [_add_four(sm_, pt_, place, "rs_add_chips_" + n)
         for n, sm_, pt_ in zip(names, sums_early + sums_last, list(parts_early) + list(parts_last))])
    g_shard = dict(zip(names, reduced))

    loss = small_sum[9, 0]
    g_small = {n: small_sum[i:i + 1, :sz] for i, (n, sz) in enumerate(SMALL)}

    grads, delta, new_m, new_v = {}, {}, {}, {}
    for n, _, _, _ in BIG:
        if n in COLUMN_MAJOR:
            turn = lambda a: jnp.swapaxes(a, 1, 2)
            g_t = g_shard[n] if n in GRAD_TRANSPOSED else g_shard[n].T
            d, nm, nv = _adamw(turn(wts[n]), g_t, turn(mom[n]), turn(var[n]), "adamw_" + n)
            grads[n], delta[n], new_m[n], new_v[n] = turn(g_t[None]), turn(d), turn(nm), turn(nv)
        else:
            delta[n], new_m[n], new_v[n] = _adamw(wts[n], g_shard[n], mom[n], var[n], "adamw_" + n)
            grads[n] = g_shard[n][None]
    d, nm, nv = _adamw(_pack_small(small2d)[None], small_sum, _pack_small(mom)[None], _pack_small(var)[None],
                       "adamw_small")
    for i, (n, sz) in enumerate(SMALL):
        grads[n] = g_small[n]
        delta[n], new_m[n], new_v[n] = d[0, i:i + 1, :sz], nm[0, i:i + 1, :sz], nv[0, i:i + 1, :sz]

    return (loss, grad_x[None], *[grads[n] for n in ALL_W], *[delta[n] for n in ALL_W],
            *[new_m[n] for n in ALL_W], *[new_v[n] for n in ALL_W])
```

```python
import functools
import math

import jax
import jax.numpy as jnp
import numpy as np
from jax import lax
from jax.experimental import pallas as pl
from jax.experimental.pallas import tpu as pltpu

F32 = jnp.float32
BF16 = jnp.bfloat16
MESH = pl.DeviceIdType.MESH

D_MODEL = 1024
D_FF = 2816
PLE_DIM = 256
RET_HEADS = 4
RET_DIM = 128
RET_WIDTH = 512
RET_CHUNK = 256
RET_GROUP_FWD = 16
RET_GROUP_BWD = 8
MLA_HEADS = 8
MLA_NOPE = 64
MLA_ROPE = 32
MLA_V = 64
Q_LORA = 384
KV_LORA = 256
IN_COLS = 2720
IN_COLS_P = 2816
IN_SHARD = IN_COLS // 4
IN_SHARD_P = 688
ROPE_BASE = 10000.0
EPS = 1e-6
SCALE_MLA = 1.0 / math.sqrt(MLA_NOPE + MLA_ROPE)
SCALE_RET = RET_DIM ** -0.5
NEG = -1e30

ADAM_LR = 0.001
ADAM_B1 = 0.9
ADAM_B2 = 0.999
ADAM_EPS = 1e-08
ADAM_WD = 0.01
ADAM_STEP = 10

N_CHIPS = 4
N_DEV = 8
VMEM_MB = 56

BIG = (
    ("w_in", 1024, 2720, 1),
    ("w_uq", 384, 768, 1),
    ("w_ukv", 256, 1024, 1),
    ("w_o", 1024, 1024, 0),
    ("w_gate", 1024, 2816, 1),
    ("w_up", 1024, 2816, 1),
    ("w_down", 2816, 1024, 0),
    ("w_ple_proj", 256, 1024, 1),
    ("w_ple_gate", 1024, 1024, 0),
)
SMALL = (
    ("pre_mix_norm", 1024),
    ("ret_gn_w", 512),
    ("mla_q_norm", 384),
    ("mla_kv_norm", 256),
    ("post_mix_norm", 1024),
    ("pre_ffn_norm", 1024),
    ("post_ffn_norm", 1024),
    ("ple_norm", 1024),
    ("b_ple_gate", 1024),
)
ALL_W = ("pre_mix_norm", "w_in", "ret_gn_w", "mla_q_norm", "w_uq", "mla_kv_norm", "w_ukv", "w_o", "post_mix_norm",
         "pre_ffn_norm", "w_gate", "w_up", "w_down", "post_ffn_norm", "w_ple_proj", "ple_norm", "w_ple_gate", "b_ple_gate")
PACK_COLS = 1024
SMALL_ROWS = 16


def _cp(sem=None, mb=VMEM_MB, **kw):
    return pltpu.CompilerParams(dimension_semantics=sem, vmem_limit_bytes=mb * 1024 * 1024, **kw)


def _bf(x):
    return x.astype(BF16)


def _dot(a, b):
    return jnp.dot(_bf(a), _bf(b), preferred_element_type=F32)


def _dot_nt(a, b):
    return lax.dot_general(_bf(a), _bf(b), (((1,), (1,)), ((), ())), preferred_element_type=F32)


def _dot_tn(a, b):
    return lax.dot_general(_bf(a), _bf(b), (((0,), (0,)), ((), ())), preferred_element_type=F32)


def _sig(x):
    return 1.0 / (1.0 + jnp.exp(-x))


def _rms(x, g):
    r = lax.rsqrt(jnp.mean(x * x, axis=-1, keepdims=True) + EPS)
    return x * r * g


def _rms_bwd(dy, x, g):
    r = lax.rsqrt(jnp.mean(x * x, axis=-1, keepdims=True) + EPS)
    xh = x * r
    dxh = dy * g
    dx = r * (dxh - xh * jnp.mean(dxh * xh, axis=-1, keepdims=True))
    return dx, dy * xh


def _colsum(x):
    return jnp.sum(x, axis=0, keepdims=True)


def _rope_ret(x, cr, sr):
    return x * cr + pltpu.roll(x, 64, 1) * sr


def _unrope_ret(dy, cr, sr):
    return dy * cr + pltpu.roll(dy * sr, 64, 1)


def _rope_mla(x, cm, sa, sb):
    return x * cm + pltpu.roll(x, 112, 1) * sa + pltpu.roll(x, 16, 1) * sb


def _unrope_mla(dy, cm, sa, sb):
    return dy * cm + pltpu.roll(dy * sa, 16, 1) + pltpu.roll(dy * sb, 112, 1)


def _rows(tm, w, col=0):
    return pl.BlockSpec((tm, w), lambda i: (i, col))


def _full(*shape):
    return pl.BlockSpec(shape, lambda i: (0,) * len(shape), pipeline_mode=pl.Buffered(1))


def _acc(*shape):
    return pl.BlockSpec(shape, lambda i: (0,) * len(shape))


def _sds(shape, dtype):
    return jax.ShapeDtypeStruct(shape, dtype)


def _rope_tables(pos_f, S, shards=()):
    tm = min(512, S)
    n = len(shards)
    steps = S // tm
    inv_r = (1.0 / (np.float32(ROPE_BASE) ** (np.arange(64, dtype=np.float32) / np.float32(64)))).astype(np.float32)
    inv_m16 = (1.0 / (np.float32(ROPE_BASE) ** (np.arange(16, dtype=np.float32) / np.float32(16)))).astype(np.float32)
    inv_r = np.concatenate([inv_r, inv_r])[None, :]
    inv_m = np.zeros((1, 128), np.float32)
    inv_m[0, 64:80] = inv_m16
    inv_m[0, 80:96] = inv_m16

    def body(pos_ref, invr_ref, invm_ref, *rest):
        w_ins, (cr_ref, sr_ref, cm_ref, sa_ref, sb_ref) = rest[:n], rest[n:n + 5]
        w_outs, sems = rest[n + 5:2 * n + 5], rest[2 * n + 5:]
        i = pl.program_id(0)
        if n:
            @pl.when(i == 0)
            def _():
                _gather_phase(0, w_ins, w_outs, sems)

            @pl.when(i == steps - 1)
            def _():
                _gather_phase(1, w_ins, w_outs, sems)

        pos = pos_ref[...]
        lane = lax.broadcasted_iota(jnp.int32, (tm, 128), 1)
        ar = pos * invr_ref[...]
        s = jnp.sin(ar)
        cr_ref[...] = jnp.cos(ar)
        sr_ref[...] = jnp.where(lane < 64, -s, s)
        am = pos * invm_ref[...]
        c2 = jnp.cos(am)
        s2 = jnp.sin(am)
        cm_ref[...] = jnp.where(lane < 64, 1.0, jnp.where(lane < 96, c2, 0.0))
        sa_ref[...] = jnp.where((lane >= 64) & (lane < 80), -s2, 0.0)
        sb_ref[...] = jnp.where((lane >= 80) & (lane < 96), s2, 0.0)

        if n:
            @pl.when(i == steps - 1)
            def _():
                _gather_phase(2, w_ins, w_outs, sems)

    outs = pl.pallas_call(
        body, name="rope_tables", grid=(steps,),
        in_specs=[_rows(tm, 1), _full(1, 128), _full(1, 128)] + [_ANY] * n,
        out_specs=[_rows(tm, 128)] * 5 + [_ANY] * n,
        out_shape=[_sds((S, 128), F32)] * 5 + _gather_out_shapes(shards),
        scratch_shapes=_gather_sems(n) if n else [],
        compiler_params=_cp(("arbitrary",)),
    )(pos_f, jnp.asarray(inv_r), jnp.asarray(inv_m), *shards)
    return outs[:5], outs[5:]


def _inproj(x, g, w_in, tabs, S):
    tm = min(512, S)

    def body(x_ref, g_ref, w_ref, cr_ref, sr_ref, cm_ref, sa_ref, sb_ref,
             xn_ref, rq_ref, rk_ref, rv_ref, rg_ref, cq_ref, ckv_ref, kr_ref):
        xb = _rms(x_ref[...], g_ref[...]).astype(BF16)
        xn_ref[...] = xb
        cr = cr_ref[...]
        sr = sr_ref[...]
        q = jnp.dot(xb, w_ref[:, 0:512], preferred_element_type=F32)
        k = jnp.dot(xb, w_ref[:, 512:1024], preferred_element_type=F32)
        for h in range(RET_HEADS):
            sl = slice(h * 128, (h + 1) * 128)
            rq_ref[:, sl] = _rope_ret(q[:, sl], cr, sr).astype(BF16)
            rk_ref[:, sl] = (_rope_ret(k[:, sl], cr, sr) * SCALE_RET).astype(BF16)
        rv_ref[...] = jnp.dot(xb, w_ref[:, 1024:1536], preferred_element_type=F32).astype(BF16)
        rg_ref[...] = jnp.dot(xb, w_ref[:, 1536:2048], preferred_element_type=F32)
        cq_ref[...] = jnp.dot(xb, w_ref[:, 2048:2432], preferred_element_type=F32)
        ckv_ref[...] = jnp.dot(xb, w_ref[:, 2432:2688], preferred_element_type=F32)
        kr = pltpu.roll(jnp.dot(xb, w_ref[:, 2688:2816], preferred_element_type=F32), 64, 1)
        kr_ref[...] = _rope_mla(kr, cm_ref[...], sa_ref[...], sb_ref[...])

    return pl.pallas_call(
        body, name="inproj", grid=(S // tm,),
        in_specs=[_rows(tm, D_MODEL), _full(1, D_MODEL), _full(D_MODEL, IN_COLS_P)] + [_rows(tm, 128)] * 5,
        out_specs=[_rows(tm, D_MODEL)] + [_rows(tm, 512)] * 4 + [_rows(tm, Q_LORA), _rows(tm, KV_LORA), _rows(tm, 128)],
        out_shape=[_sds((S, D_MODEL), BF16)] + [_sds((S, 512), BF16)] * 3
        + [_sds((S, 512), F32), _sds((S, Q_LORA), F32), _sds((S, KV_LORA), F32), _sds((S, 128), F32)],
        compiler_params=_cp(("parallel",)),
    )(x, g, w_in, *tabs)


def _mla_up(cq, ckv, kr, gq, gkv, w_uq, w_ukv, tabs, S):
    tm = min(512, S)

    def body(cq_ref, ckv_ref, kr_ref, gq_ref, gkv_ref, wuq_ref, wukv_ref, cm_ref, sa_ref, sb_ref,
             cqn_ref, ckvn_ref, qp_ref, kp_ref, v_ref, kt_ref, vt_ref):
        cm = cm_ref[...]
        sa = sa_ref[...]
        sb = sb_ref[...]
        cqn = _rms(cq_ref[...], gq_ref[...]).astype(BF16)
        cqn_ref[...] = cqn
        ckvn = _rms(ckv_ref[...], gkv_ref[...]).astype(BF16)
        ckvn_ref[...] = ckvn
        qh = jnp.dot(cqn, wuq_ref[...], preferred_element_type=F32)
        kv = jnp.dot(ckvn, wukv_ref[...], preferred_element_type=F32)
        kr_blk = kr_ref[...]
        for h in range(MLA_HEADS):
            sl = slice(h * 128, (h + 1) * 128)
            qp_ref[:, sl] = (_rope_mla(qh[:, sl], cm, sa, sb) * SCALE_MLA).astype(BF16)
            kh = kv[:, sl] + kr_blk
            kp_ref[:, sl] = kh.astype(BF16)
            kt_ref[sl, :] = kh.T.astype(BF16)
        for h in range(MLA_HEADS // 2):
            vh = kv[:, 1024 + h * 128:1024 + (h + 1) * 128]
            v_ref[:, h * 128:(h + 1) * 128] = vh.astype(BF16)
            vt_ref[h * 128:(h + 1) * 128, :] = vh.T.astype(BF16)

    cols = lambda r: pl.BlockSpec((r, tm), lambda i: (0, i))
    return pl.pallas_call(
        body, name="mla_up", grid=(S // tm,),
        in_specs=[_rows(tm, Q_LORA), _rows(tm, KV_LORA), _rows(tm, 128), _full(1, Q_LORA), _full(1, KV_LORA),
                  _full(Q_LORA, 1024), _full(KV_LORA, 1536)] + [_rows(tm, 128)] * 3,
        out_specs=[_rows(tm, Q_LORA), _rows(tm, KV_LORA), _rows(tm, 1024), _rows(tm, 1024), _rows(tm, 512),
                   cols(1024), cols(512)],
        out_shape=[_sds((S, Q_LORA), BF16), _sds((S, KV_LORA), BF16), _sds((S, 1024), BF16), _sds((S, 1024), BF16),
                   _sds((S, 512), BF16), _sds((1024, S), BF16), _sds((512, S), BF16)],
        compiler_params=_cp(("parallel",)),
    )(cq, ckv, kr, gq, gkv, w_uq, w_ukv, *tabs[2:])


def _tri_pairs(nq, k_major):
    if k_major:
        pairs = [(qb, kb) for kb in range(nq) for qb in range(kb, nq)]
    else:
        pairs = [(qb, kb) for qb in range(nq) for kb in range(qb + 1)]
    qb_of = np.array([p[0] for p in pairs], np.int32)
    kb_of = np.array([p[1] for p in pairs], np.int32)
    return jnp.asarray(qb_of), jnp.asarray(kb_of), len(pairs)


ATT_ROWS = 32
FWD_HEADS = 8
BWD_HEADS = 4


def _causal_keep(r0, rows, tq):
    key = r0 + lax.broadcasted_iota(jnp.int32, (rows, tq), 0)
    qry = lax.broadcasted_iota(jnp.int32, (rows, tq), 1)
    return key <= qry


def _flash_fwd(qp, kp, vt, S, shards=()):
    tq = min(512, S)
    nq = S // tq
    RB = ATT_ROWS
    NH = FWD_HEADS
    qb_of, kb_of, T = _tri_pairs(nq, k_major=False)
    n = len(shards)
    steps = (MLA_HEADS // NH) * T

    def body(qb_ref, kb_ref, q_ref, k_ref, vt_ref, *rest):
        w_ins, (o_ref, lse_ref), w_outs = rest[:n], rest[n:n + 2], rest[n + 2:2 * n + 2]
        m_sc, l_sc, acc_sc, s_sc, p_sc = rest[2 * n + 2:2 * n + 7]
        sems = rest[2 * n + 7:]
        t = pl.program_id(1)
        qb = qb_ref[t]
        kb = kb_ref[t]
        lin = pl.program_id(0) * T + t

        if n:
            @pl.when(lin == 0)
            def _():
                _gather_phase(0, w_ins, w_outs, sems)

            @pl.when(lin == steps // 2)
            def _():
                _gather_phase(1, w_ins, w_outs, sems)

        @pl.when(kb == 0)
        def _():
            m_sc[...] = jnp.full(m_sc.shape, NEG, F32)
            l_sc[...] = jnp.zeros(l_sc.shape, F32)
            acc_sc[...] = jnp.zeros(acc_sc.shape, F32)

        def scores(a):
            sl = slice(a * 128, (a + 1) * 128)
            s_sc[a] = _dot_nt(k_ref[:, sl], q_ref[:, sl])

        def step(masked):
            for a in range(NH):
                scores(a)
            for a in range(NH):
                mx = [jnp.full((8, tq), NEG, F32) for _ in range(RB // 8)]
                for r in range(0, tq, RB):
                    sc = s_sc[a, r:r + RB, :]
                    if masked:
                        sc = jnp.where(_causal_keep(r, RB, tq), sc, NEG)
                        s_sc[a, r:r + RB, :] = sc
                    for i in range(RB // 8):
                        mx[i] = jnp.maximum(mx[i], sc[i * 8:(i + 1) * 8, :])
                mx8 = functools.reduce(jnp.maximum, mx)
                m_prev = m_sc[a]
                m_new = jnp.maximum(m_prev, jnp.max(mx8, axis=0, keepdims=True))
                al = jnp.exp(m_prev - m_new)
                m_sc[a] = m_new
                ls = [jnp.zeros((8, tq), F32) for _ in range(RB // 8)]
                for r in range(0, tq, RB):
                    p = jnp.exp(s_sc[a, r:r + RB, :] - m_new)
                    for i in range(RB // 8):
                        ls[i] = ls[i] + p[i * 8:(i + 1) * 8, :]
                    p_sc[a, r:r + RB, :] = p.astype(BF16)
                l_sc[a] = al * l_sc[a] + jnp.sum(functools.reduce(jnp.add, ls), axis=0, keepdims=True)
                pair = slice((a // 2) * 128, (a // 2 + 1) * 128)
                pv = jnp.dot(vt_ref[pair, :], p_sc[a], preferred_element_type=F32)
                rs = slice(a * 64, (a + 1) * 64)
                own = slice((a % 2) * 64, (a % 2 + 1) * 64)
                acc_sc[rs, :] = acc_sc[rs, :] * al + pv[own, :]

        @pl.when(kb < qb)
        def _():
            step(False)

        @pl.when(kb == qb)
        def _():
            step(True)
            for a in range(NH):
                rs = slice(a * 64, (a + 1) * 64)
                acc_sc[rs, :] = acc_sc[rs, :] / l_sc[a]
                lse_ref[a:a + 1, :] = m_sc[a] + jnp.log(l_sc[a])
            o_ref[...] = acc_sc[...].T.astype(BF16)

        if n:
            @pl.when(lin == steps - 1)
            def _():
                _gather_phase(2, w_ins, w_outs, sems)

    grid_spec = pltpu.PrefetchScalarGridSpec(
        num_scalar_prefetch=2, grid=(MLA_HEADS // NH, T),
        in_specs=[pl.BlockSpec((tq, 128 * NH), lambda j, t, qb, kb: (qb[t], j)),
                  pl.BlockSpec((tq, 128 * NH), lambda j, t, qb, kb: (kb[t], j)),
                  pl.BlockSpec((64 * NH, tq), lambda j, t, qb, kb: (j, kb[t]))] + [_ANY] * n,
        out_specs=[pl.BlockSpec((tq, 64 * NH), lambda j, t, qb, kb: (qb[t], j)),
                   pl.BlockSpec((None, NH, tq), lambda j, t, qb, kb: (j, 0, qb[t]))] + [_ANY] * n,
        scratch_shapes=[pltpu.VMEM((NH, 1, tq), F32), pltpu.VMEM((NH, 1, tq), F32), pltpu.VMEM((64 * NH, tq), F32),
                        pltpu.VMEM((NH, tq, tq), F32), pltpu.VMEM((NH, tq, tq), BF16)] + (_gather_sems(n) if n else []),
    )
    out, lse, *gathered = pl.pallas_call(
        body, name="flash_fwd", grid_spec=grid_spec,
        out_shape=[_sds((S, 512), BF16), _sds((MLA_HEADS // NH, NH, S), F32)] + _gather_out_shapes(shards),
        compiler_params=_cp(("arbitrary", "arbitrary")),
    )(qb_of, kb_of, qp, kp, vt, *shards)
    return out, lse.reshape(MLA_HEADS // 2, 2, S), gathered


def _decay_table():
    log_g = np.log(1.0 - 2.0 ** (-5.0 - np.arange(RET_HEADS, dtype=np.float32))).astype(np.float32)
    return jnp.asarray(np.broadcast_to(log_g[:, None, None], (RET_HEADS, 8, 128)).copy())


def _decay_terms(lg_ref):
    C = RET_CHUNK
    lg = lg_ref[0:1, :]
    row = lax.broadcasted_iota(jnp.int32, (C, C), 0)
    col = lax.broadcasted_iota(jnp.int32, (C, C), 1)
    diff = (row - col).astype(F32)
    dmat = jnp.where(diff >= 0, jnp.exp(jnp.maximum(diff, 0.0) * jnp.tile(lg, (1, C // 128))), 0.0)
    j = lax.broadcasted_iota(jnp.int32, (C, 1), 0).astype(F32)
    lg1 = lg[:, 0:1]
    zeta = jnp.exp((C - 1 - j) * lg1)
    xi = jnp.exp((j + 1.0) * lg1)
    g_chunk = jnp.exp(C * lg1)
    return dmat, zeta, xi, g_chunk


def _ret_fwd(rq, rk, rv, rg, gn_w, S):
    C = RET_CHUNK
    N = S // C
    G = min(RET_GROUP_FWD, N)
    NB = N // G

    def body(lg_ref, q_ref, k_ref, v_ref, rg_ref, w_ref, ry_ref, ro_ref, rprev_ref, r_sc):
        @pl.when(pl.program_id(1) == 0)
        def _():
            r_sc[...] = jnp.zeros(r_sc.shape, F32)

        dmat, zeta, xi, g_chunk = _decay_terms(lg_ref)
        w = w_ref[...]
        r = r_sc[...]
        for i in range(G):
            rows = slice(i * C, (i + 1) * C)
            q = q_ref[rows, :]
            k = k_ref[rows, :]
            v = v_ref[rows, :]
            r_prev = r.astype(BF16)
            rprev_ref[i] = r_prev
            sc = _dot_nt(q, k) * dmat
            ry = _dot(sc, v) + jnp.dot(q, r_prev, preferred_element_type=F32) * xi
            ry_ref[rows, :] = ry
            r = g_chunk * r + _dot_tn(k, zeta * v.astype(F32))
            mu = jnp.mean(ry, axis=-1, keepdims=True)
            yc = ry - mu
            yh = yc * lax.rsqrt(jnp.mean(yc * yc, axis=-1, keepdims=True) + EPS)
            g = rg_ref[rows, :]
            ro_ref[rows, :] = (g * _sig(g) * (yh * w)).astype(BF16)
        r_sc[...] = r

    blk = pl.BlockSpec((G * C, 128), lambda h, n: (n, h))
    return pl.pallas_call(
        body, name="ret_fwd", grid=(RET_HEADS, NB),
        in_specs=[pl.BlockSpec((None, 8, 128), lambda h, n: (h, 0, 0)), blk, blk, blk, blk,
                  pl.BlockSpec((1, 128), lambda h, n: (0, h))],
        out_specs=[blk, blk, pl.BlockSpec((G, 128, 128), lambda h, n: (h * NB + n, 0, 0))],
        out_shape=[_sds((S, 512), F32), _sds((S, 512), BF16), _sds((RET_HEADS * N, 128, 128), BF16)],
        scratch_shapes=[pltpu.VMEM((128, 128), F32)],
        compiler_params=_cp(("parallel", "arbitrary")),
    )(_decay_table(), rq, rk, rv, rg, gn_w)


def _outproj(ro, mo, x, w_o, g_post, g_pre, S):
    tm = min(512, S)

    def body(ro_ref, mo_ref, x_ref, wo_ref, g1_ref, g2_ref, mix_ref, h1_ref, hn_ref):
        mix = (jnp.dot(ro_ref[...], wo_ref[0:512, :], preferred_element_type=F32)
               + jnp.dot(mo_ref[...], wo_ref[512:1024, :], preferred_element_type=F32))
        mix_ref[...] = mix.astype(BF16)
        h1 = x_ref[...] + _rms(mix, g1_ref[...])
        h1_ref[...] = h1
        hn_ref[...] = _rms(h1, g2_ref[...]).astype(BF16)

    return pl.pallas_call(
        body, name="outproj", grid=(S // tm,),
        in_specs=[_rows(tm, 512), _rows(tm, 512), _rows(tm, D_MODEL), _full(D_MODEL, D_MODEL), _full(1, D_MODEL),
                  _full(1, D_MODEL)],
        out_specs=[_rows(tm, D_MODEL)] * 3,
        out_shape=[_sds((S, D_MODEL), BF16), _sds((S, D_MODEL), F32), _sds((S, D_MODEL), BF16)],
        compiler_params=_cp(("parallel",)),
    )(ro, mo, x, w_o, g_post, g_pre)


def _ffn_up(hn, w_gate_t, w_up_t, S):
    tm = min(512, S)
    tn = D_FF // 2

    def body(hn_ref, wg_ref, wu_ref, fg_ref, fu_ref, act_ref):
        hn_b = hn_ref[...]
        g = _dot_nt(hn_b, wg_ref[...])
        u = _dot_nt(hn_b, wu_ref[...])
        s = _sig(g)
        silu = g * s
        fg_ref[...] = (u * (s + silu * (1.0 - s))).astype(BF16)
        fu_ref[...] = silu.astype(BF16)
        act_ref[...] = (silu * u).astype(BF16)

    wspec = pl.BlockSpec((tn, D_MODEL), lambda j, i: (j, 0))
    ospec = pl.BlockSpec((tm, tn), lambda j, i: (i, j))
    return pl.pallas_call(
        body, name="ffn_up", grid=(2, S // tm),
        in_specs=[pl.BlockSpec((tm, D_MODEL), lambda j, i: (i, 0)), wspec, wspec],
        out_specs=[ospec] * 3, out_shape=[_sds((S, D_FF), BF16)] * 3,
        compiler_params=_cp(("parallel", "parallel")),
    )(hn, w_gate_t, w_up_t)


def _ffn_down(act, w_down, h1, g, S):
    tm = min(512, S)

    def body(act_ref, wd_ref, h1_ref, g_ref, ff_ref, h2_ref):
        ff = jnp.dot(act_ref[...], wd_ref[...], preferred_element_type=F32)
        ff_ref[...] = ff.astype(BF16)
        h2_ref[...] = h1_ref[...] + _rms(ff, g_ref[...])

    return pl.pallas_call(
        body, name="ffn_down", grid=(S // tm,),
        in_specs=[_rows(tm, D_FF), _full(D_FF, D_MODEL), _rows(tm, D_MODEL), _full(1, D_MODEL)],
        out_specs=[_rows(tm, D_MODEL)] * 2, out_shape=[_sds((S, D_MODEL), BF16), _sds((S, D_MODEL), F32)],
        compiler_params=_cp(("parallel",)),
    )(act, w_down, h1, g)


def _ple_loss(p, h2, tgt, w_pp, w_pg, b_pg, g_ple, S):
    tm = min(512, S)

    def body(p_ref, h2_ref, t_ref, wp_ref, wg_ref, b_ref, gp_ref,
             dz_ref, dpe_ref, dh2_ref, h2b_ref, loss_ref, dgp_ref, db_ref):
        @pl.when(pl.program_id(0) == 0)
        def _():
            loss_ref[...] = jnp.zeros(loss_ref.shape, F32)
            dgp_ref[...] = jnp.zeros(dgp_ref.shape, F32)
            db_ref[...] = jnp.zeros(db_ref.shape, F32)

        gp = gp_ref[...]
        pe = _dot(p_ref[...], wp_ref[...])
        r = lax.rsqrt(jnp.mean(pe * pe, axis=-1, keepdims=True) + EPS)
        peh = pe * r
        e = peh * gp
        h2 = h2_ref[...]
        h2b = h2.astype(BF16)
        h2b_ref[...] = h2b
        gt = _sig(jnp.dot(h2b, wg_ref[...], preferred_element_type=F32) + b_ref[...])
        diff = h2 + e * gt - t_ref[...]
        loss_ref[...] += _colsum(diff * diff)
        dh3 = diff * (1.0 / D_MODEL)
        de = dh3 * gt
        dz = dh3 * e * gt * (1.0 - gt)
        db_ref[...] += _colsum(dz)
        dgp_ref[...] += _colsum(de * peh)
        dpeh = de * gp
        dpe = r * (dpeh - peh * jnp.mean(dpeh * peh, axis=-1, keepdims=True))
        dzb = dz.astype(BF16)
        dz_ref[...] = dzb
        dpe_ref[...] = dpe.astype(BF16)
        dh2_ref[...] = dh3 + _dot_nt(dzb, wg_ref[...])

    return pl.pallas_call(
        body, name="ple_loss", grid=(S // tm,),
        in_specs=[_rows(tm, PLE_DIM), _rows(tm, D_MODEL), _rows(tm, D_MODEL), _full(PLE_DIM, D_MODEL),
                  _full(D_MODEL, D_MODEL), _full(1, D_MODEL), _full(1, D_MODEL)],
        out_specs=[_rows(tm, D_MODEL)] * 4 + [_acc(1, D_MODEL)] * 3,
        out_shape=[_sds((S, D_MODEL), BF16), _sds((S, D_MODEL), BF16), _sds((S, D_MODEL), F32), _sds((S, D_MODEL), BF16)]
        + [_sds((1, D_MODEL), F32)] * 3,
        compiler_params=_cp(("arbitrary",)),
    )(p, h2, tgt, w_pp, w_pg, b_pg, g_ple)


def _wgrad(a, b, name, S):
    M = a.shape[1]
    N = b.shape[1]
    ts = min(2048, S)
    nsplit = 2 if M * N >= 2 * 1024 * 1024 else 1
    tn = N // nsplit

    def body(a_ref, b_ref, o_ref):
        @pl.when(pl.program_id(1) == 0)
        def _():
            o_ref[...] = jnp.zeros(o_ref.shape, F32)

        o_ref[...] += _dot_tn(a_ref[...], b_ref[...])

    return pl.pallas_call(
        body, name=name, grid=(nsplit, S // ts),
        in_specs=[pl.BlockSpec((ts, M), lambda j, s: (s, 0)), pl.BlockSpec((ts, tn), lambda j, s: (s, j))],
        out_specs=pl.BlockSpec((M, tn), lambda j, s: (0, j)), out_shape=_sds((M, N), F32),
        compiler_params=_cp(("parallel", "arbitrary")),
    )(a, b)


def _ffn_down_bwd(dh2, ff, g, w_down, dgate_f, dup_f, S):
    tm = min(512, S)
    tn = D_FF // 2

    def body(dh2_ref, ff_ref, g_ref, wd_ref, fg_ref, fu_ref, dff_ref, dgate_ref, dup_ref, dg_ref):
        @pl.when(pl.program_id(0) == 0)
        def _():
            dg_ref[...] = jnp.zeros(dg_ref.shape, F32)

        dff, ga = _rms_bwd(dh2_ref[...], ff_ref[...].astype(F32), g_ref[...])
        dg_ref[...] += _colsum(ga)
        dffb = dff.astype(BF16)
        dff_ref[...] = dffb
        for seg in range(2):
            sl = slice(seg * tn, (seg + 1) * tn)
            dact = _dot_nt(dffb, wd_ref[sl, :])
            dgate_ref[:, sl] = (dact * fg_ref[:, sl].astype(F32)).astype(BF16)
            dup_ref[:, sl] = (dact * fu_ref[:, sl].astype(F32)).astype(BF16)

    return pl.pallas_call(
        body, name="ffn_down_bwd", grid=(S // tm,),
        in_specs=[_rows(tm, D_MODEL), _rows(tm, D_MODEL), _full(1, D_MODEL), _full(D_FF, D_MODEL), _rows(tm, D_FF),
                  _rows(tm, D_FF)],
        out_specs=[_rows(tm, D_MODEL), _rows(tm, D_FF), _rows(tm, D_FF), _acc(1, D_MODEL)],
        out_shape=[_sds((S, D_MODEL), BF16), _sds((S, D_FF), BF16), _sds((S, D_FF), BF16), _sds((1, D_MODEL), F32)],
        compiler_params=_cp(("arbitrary",)),
    )(dh2, ff, g, w_down, dgate_f, dup_f)


def _ffn_up_bwd(dgate, dup, w_gate, w_up, h1, mix, dh2, g_pre, g_post, w_o, S, grads=()):
    tm = min(512, S)
    n = len(grads)
    last = S // tm - 1

    def body(dgate_ref, dup_ref, wg_ref, wu_ref, h1_ref, mix_ref, dh2_ref, g2_ref, g1_ref, wo_ref, *rest):
        g_ins = rest[:n]
        dh1_ref, dmix_ref, dro_ref, dmo_ref, dg2_ref, dg1_ref = rest[n:n + 6]
        g_outs, sems = rest[n + 6:2 * n + 6], rest[2 * n + 6:]

        @pl.when(pl.program_id(0) == 0)
        def _():
            dg2_ref[...] = jnp.zeros(dg2_ref.shape, F32)
            dg1_ref[...] = jnp.zeros(dg1_ref.shape, F32)
            for cp in (_swap_copies(g_ins, g_outs, sems) if n else []):
                cp.start()

        dhn = (jnp.dot(dgate_ref[...], wg_ref[...], preferred_element_type=F32)
               + jnp.dot(dup_ref[...], wu_ref[...], preferred_element_type=F32))
        d1, ga = _rms_bwd(dhn, h1_ref[...], g2_ref[...])
        dg2_ref[...] += _colsum(ga)
        dh1 = dh2_ref[...] + d1
        dh1_ref[...] = dh1
        dmix, gb = _rms_bwd(dh1, mix_ref[...].astype(F32), g1_ref[...])
        dg1_ref[...] += _colsum(gb)
        dmixb = dmix.astype(BF16)
        dmix_ref[...] = dmixb
        dcat = _dot_nt(dmixb, wo_ref[...])
        dro_ref[...] = dcat[:, 0:512].astype(BF16)
        dmo_ref[...] = dcat[:, 512:1024].astype(BF16)

        if n:
            @pl.when(pl.program_id(0) == last)
            def _():
                for cp in _swap_copies(g_ins, g_outs, sems):
                    cp.wait()

    dh1, dmix, dro, dmo, dg2, dg1, *got = pl.pallas_call(
        body, name="ffn_up_bwd", grid=(S // tm,),
        in_specs=[_rows(tm, D_FF), _rows(tm, D_FF), _full(D_FF, D_MODEL), _full(D_FF, D_MODEL), _rows(tm, D_MODEL),
                  _rows(tm, D_MODEL), _rows(tm, D_MODEL), _full(1, D_MODEL), _full(1, D_MODEL), _full(D_MODEL, D_MODEL)]
        + [_ANY] * n,
        out_specs=[_rows(tm, D_MODEL), _rows(tm, D_MODEL), _rows(tm, 512), _rows(tm, 512), _acc(1, D_MODEL),
                   _acc(1, D_MODEL)] + [_ANY] * n,
        out_shape=[_sds((S, D_MODEL), F32), _sds((S, D_MODEL), BF16), _sds((S, 512), BF16), _sds((S, 512), BF16),
                   _sds((1, D_MODEL), F32), _sds((1, D_MODEL), F32)] + _swap_out_shapes(grads),
        scratch_shapes=_swap_sems(n) if n else [],
        compiler_params=_cp(("arbitrary",)),
    )(dgate, dup, w_gate, w_up, h1, mix, dh2, g_pre, g_post, w_o, *grads)
    return dh1, dmix, dro, dmo, dg2, dg1, got


def _attn_delta(o, do, S, grads=()):
    tm = min(512, S)
    n = len(grads)
    last = S // tm - 1

    def body(o_ref, do_ref, *rest):
        g_ins, (dot_ref, d_ref), g_outs, sems = rest[:n], rest[n:n + 2], rest[n + 2:2 * n + 2], rest[2 * n + 2:]
        if n:
            @pl.when(pl.program_id(0) == 0)
            def _():
                for cp in _swap_copies(g_ins, g_outs, sems):
                    cp.start()

        do = do_ref[...].astype(F32)
        prod_t = (o_ref[...].astype(F32) * do).T
        dot_ref[...] = do.T.astype(BF16)
        for h in range(MLA_HEADS):
            d_ref[h // 2, (h % 2):(h % 2) + 1, :] = jnp.sum(prod_t[h * 64:(h + 1) * 64, :], axis=0, keepdims=True)

        if n:
            @pl.when(pl.program_id(0) == last)
            def _():
                for cp in _swap_copies(g_ins, g_outs, sems):
                    cp.wait()

    dot, delta, *got = pl.pallas_call(
        body, name="attn_delta", grid=(S // tm,),
        in_specs=[_rows(tm, 512), _rows(tm, 512)] + [_ANY] * n,
        out_specs=[pl.BlockSpec((512, tm), lambda i: (0, i)), pl.BlockSpec((MLA_HEADS // 2, 2, tm), lambda i: (0, 0, i))]
        + [_ANY] * n,
        out_shape=[_sds((512, S), BF16), _sds((MLA_HEADS // 2, 2, S), F32)] + _swap_out_shapes(grads),
        scratch_shapes=_swap_sems(n) if n else [],
        compiler_params=_cp(("arbitrary",)),
    )(o, do, *grads)
    return dot, delta, got


def _flash_bwd(qp, kp, kt, v, do, dot, lse, delta, S, sums=()):
    tq = min(512, S)
    nq = S // tq
    RB = ATT_ROWS
    NH = BWD_HEADS
    qb_of, kb_of, T = _tri_pairs(nq, k_major=True)
    n = len(sums)
    steps = (MLA_HEADS // NH) * T

    def body(qb_ref, kb_ref, q_ref, k_ref, kt_ref, v_ref, do_ref, dot_ref, lse_ref, dl_ref, *rest):
        g_ins, (dq_ref, dk_ref, dv_ref), g_outs = rest[:n], rest[n:n + 3], rest[n + 3:2 * n + 3]
        dk_sc, dv_sc, s_sc, dp_sc, p_sc, ds_sc = rest[2 * n + 3:2 * n + 9]
        sems = rest[2 * n + 9:]
        t = pl.program_id(1)
        qb = qb_ref[t]
        kb = kb_ref[t]
        lin = pl.program_id(0) * T + t

        if n:
            @pl.when(lin == 0)
            def _():
                for cp in _scatter_copies(g_ins, g_outs, sems):
                    cp.start()

        @pl.when(t == 0)
        def _():
            dq_ref[...] = jnp.zeros(dq_ref.shape, F32)

        @pl.when(qb == kb)
        def _():
            dk_sc[...] = jnp.zeros(dk_sc.shape, F32)
            dv_sc[...] = jnp.zeros(dv_sc.shape, F32)

        lane = lax.broadcasted_iota(jnp.int32, (tq, 64 * NH), 1)

        def step(masked):
            vv = v_ref[...]
            do_all = do_ref[...]
            mine = [(lane >= a * 64) & (lane < (a + 1) * 64) for a in range(NH)]
            for a in range(NH):
                sl = slice(a * 128, (a + 1) * 128)
                s_sc[a] = _dot_nt(k_ref[:, sl], q_ref[:, sl])
                dp_sc[a] = jnp.dot(jnp.where(mine[a], vv, jnp.zeros_like(vv)), dot_ref[...],
                                   preferred_element_type=F32)
            for a in range(NH):
                sl = slice(a * 128, (a + 1) * 128)
                lse = lse_ref[a:a + 1, :]
                dl = dl_ref[a:a + 1, :]
                for r in range(0, tq, RB):
                    sc = s_sc[a, r:r + RB, :]
                    if masked:
                        sc = jnp.where(_causal_keep(r, RB, tq), sc, NEG)
                    p = jnp.exp(sc - lse)
                    p_sc[a, r:r + RB, :] = p.astype(BF16)
                    ds_sc[a, r:r + RB, :] = (p * (dp_sc[a, r:r + RB, :] - dl)).astype(BF16)
                ds = ds_sc[a]
                dv_sc[...] += jnp.dot(p_sc[a], jnp.where(mine[a], do_all, jnp.zeros_like(do_all)),
                                      preferred_element_type=F32)
                dk_sc[:, sl] += jnp.dot(ds, q_ref[:, sl], preferred_element_type=F32)
                dq_ref[qb, sl, :] += jnp.dot(kt_ref[sl, :], ds, preferred_element_type=F32)

        @pl.when(qb > kb)
        def _():
            step(False)

        @pl.when(qb == kb)
        def _():
            step(True)

        @pl.when(qb == nq - 1)
        def _():
            dk_ref[...] = dk_sc[...].astype(BF16)
            dv_ref[...] = dv_sc[...].astype(BF16)

        if n:
            @pl.when(lin == steps - 1)
            def _():
                for cp in _scatter_copies(g_ins, g_outs, sems):
                    cp.wait()

    grid_spec = pltpu.PrefetchScalarGridSpec(
        num_scalar_prefetch=2, grid=(MLA_HEADS // NH, T),
        in_specs=[pl.BlockSpec((tq, 128 * NH), lambda j, t, qb, kb: (qb[t], j)),
                  pl.BlockSpec((tq, 128 * NH), lambda j, t, qb, kb: (kb[t], j)),
                  pl.BlockSpec((128 * NH, tq), lambda j, t, qb, kb: (j, kb[t])),
                  pl.BlockSpec((tq, 64 * NH), lambda j, t, qb, kb: (kb[t], j)),
                  pl.BlockSpec((tq, 64 * NH), lambda j, t, qb, kb: (qb[t], j)),
                  pl.BlockSpec((64 * NH, tq), lambda j, t, qb, kb: (j, qb[t])),
                  pl.BlockSpec((None, NH, tq), lambda j, t, qb, kb: (j, 0, qb[t])),
                  pl.BlockSpec((None, NH, tq), lambda j, t, qb, kb: (j, 0, qb[t]))] + [_ANY] * n,
        out_specs=[pl.BlockSpec((nq, 128 * NH, tq), lambda j, t, qb, kb: (0, j, 0), pipeline_mode=pl.Buffered(1)),
                   pl.BlockSpec((tq, 128 * NH), lambda j, t, qb, kb: (kb[t], j)),
                   pl.BlockSpec((tq, 64 * NH), lambda j, t, qb, kb: (kb[t], j))] + [_ANY] * n,
        scratch_shapes=[pltpu.VMEM((tq, 128 * NH), F32), pltpu.VMEM((tq, 64 * NH), F32), pltpu.VMEM((NH, tq, tq), F32),
                        pltpu.VMEM((NH, tq, tq), F32), pltpu.VMEM((NH, tq, tq), BF16), pltpu.VMEM((NH, tq, tq), BF16)]
        + (_scatter_sems(n) if n else []),
    )
    dq, dk, dv, *parts = pl.pallas_call(
        body, name="flash_bwd", grid_spec=grid_spec,
        out_shape=[_sds((nq, 1024, tq), F32), _sds((S, 1024), BF16), _sds((S, 512), BF16)] + _scatter_out_shapes(sums),
        compiler_params=_cp(("arbitrary", "arbitrary")),
    )(qb_of, kb_of, qp, kp, kt, v, do, dot, lse.reshape(MLA_HEADS // NH, NH, S), delta.reshape(MLA_HEADS // NH, NH, S),
      *sums)
    return dq, dk, dv, parts


def _mla_up_bwd(dqp, dkp, dv, cq, ckv, gq, gkv, w_uq, w_ukv, tabs, S):
    tm = min(512, S)

    def body(dq_ref, dk_ref, dv_ref, cq_ref, ckv_ref, gq_ref, gkv_ref, wuq_ref, wukv_ref, cm_ref, sa_ref, sb_ref,
             dqh_ref, dkv_ref, dcq_ref, dckv_ref, dkr_ref, dgq_ref, dgkv_ref):
        @pl.when(pl.program_id(0) == 0)
        def _():
            dgq_ref[...] = jnp.zeros(dgq_ref.shape, F32)
            dgkv_ref[...] = jnp.zeros(dgkv_ref.shape, F32)

        cm = cm_ref[...]
        sa = sa_ref[...]
        sb = sb_ref[...]
        lane = lax.broadcasted_iota(jnp.int32, (tm, 128), 1)
        dkr_r = jnp.zeros((tm, 128), F32)
        for h in range(MLA_HEADS):
            sl = slice(h * 128, (h + 1) * 128)
            dqh_ref[:, sl] = (_unrope_mla(dq_ref[sl, :].T, cm, sa, sb) * SCALE_MLA).astype(BF16)
            gk = dk_ref[:, sl]
            dkr_r = dkr_r + gk.astype(F32)
            dkv_ref[:, sl] = gk
        dkr_r = jnp.where((lane >= 64) & (lane < 96), dkr_r, 0.0)
        dkr_ref[...] = _unrope_mla(dkr_r, cm, sa, sb).astype(BF16)
        dkv_ref[:, 1024:1536] = dv_ref[...]
        dcq, ga = _rms_bwd(_dot_nt(dqh_ref[...], wuq_ref[...]), cq_ref[...], gq_ref[...])
        dcq_ref[...] = dcq.astype(BF16)
        dgq_ref[...] += _colsum(ga)
        dckv, gb = _rms_bwd(_dot_nt(dkv_ref[...], wukv_ref[...]), ckv_ref[...], gkv_ref[...])
        dckv_ref[...] = dckv.astype(BF16)
        dgkv_ref[...] += _colsum(gb)

    per_q = dqp.shape[2] // tm
    return pl.pallas_call(
        body, name="mla_up_bwd", grid=(S // tm,),
        in_specs=[pl.BlockSpec((None, 1024, tm), lambda i: (i // per_q, 0, i % per_q)),
                  _rows(tm, 1024), _rows(tm, 512), _rows(tm, Q_LORA), _rows(tm, KV_LORA),
                  _full(1, Q_LORA), _full(1, KV_LORA), _full(Q_LORA, 1024), _full(KV_LORA, 1536)] + [_rows(tm, 128)] * 3,
        out_specs=[_rows(tm, 1024), _rows(tm, 1536), _rows(tm, Q_LORA), _rows(tm, KV_LORA), _rows(tm, 128),
                   _acc(1, Q_LORA), _acc(1, KV_LORA)],
        out_shape=[_sds((S, 1024), BF16), _sds((S, 1536), BF16), _sds((S, Q_LORA), BF16), _sds((S, KV_LORA), BF16),
                   _sds((S, 128), BF16), _sds((1, Q_LORA), F32), _sds((1, KV_LORA), F32)],
        compiler_params=_cp(("arbitrary",)),
    )(dqp, dkp, dv, cq, ckv, gq, gkv, w_uq, w_ukv, *tabs[2:])


def _ret_bwd(rq, rk, rv, rprev, ry, rg, dro, gn_w, tabs, S):
    C = RET_CHUNK
    N = S // C
    G = min(RET_GROUP_BWD, N)
    NB = N // G

    def body(lg_ref, q_ref, k_ref, v_ref, rp_ref, ry_ref, rg_ref, dro_ref, w_ref, cr_ref, sr_ref,
             drq_ref, drk_ref, drv_ref, drg_ref, dw_ref, g_sc):
        @pl.when(pl.program_id(1) == 0)
        def _():
            g_sc[...] = jnp.zeros(g_sc.shape, F32)
            dw_ref[...] = jnp.zeros(dw_ref.shape, F32)

        dmat, zeta, xi, g_chunk = _decay_terms(lg_ref)
        w = w_ref[...]
        gacc = g_sc[...]
        dw = jnp.zeros((1, 128), F32)
        for i in reversed(range(G)):
            rows = slice(i * C, (i + 1) * C)
            ry = ry_ref[rows, :]
            mu = jnp.mean(ry, axis=-1, keepdims=True)
            yc = ry - mu
            rstd = lax.rsqrt(jnp.mean(yc * yc, axis=-1, keepdims=True) + EPS)
            yh = yc * rstd
            g = rg_ref[rows, :]
            s = _sig(g)
            dout = dro_ref[rows, :].astype(F32)
            drg_ref[rows, :] = (dout * (yh * w) * (s * (1.0 + g * (1.0 - s)))).astype(BF16)
            dgn = dout * (g * s)
            dw = dw + _colsum(dgn * yh)
            dyh = dgn * w
            dry = rstd * (dyh - jnp.mean(dyh, axis=-1, keepdims=True) - yh * jnp.mean(dyh * yh, axis=-1, keepdims=True))
            do = dry.astype(BF16)

            q = q_ref[rows, :]
            k = k_ref[rows, :]
            v = v_ref[rows, :]
            gfut = gacc.astype(BF16)
            sc = (_dot_nt(q, k) * dmat).astype(BF16)
            dsc = (_dot_nt(do, v) * dmat).astype(BF16)
            dq = jnp.dot(dsc, k, preferred_element_type=F32) + _dot_nt(do, rp_ref[i]) * xi
            dk = _dot_tn(dsc, q) + _dot_nt(v, gfut) * zeta
            dv = _dot_tn(sc, do) + jnp.dot(k, gfut, preferred_element_type=F32) * zeta
            gacc = g_chunk * gacc + _dot_tn(q, xi * dry)
            cr = cr_ref[rows, :]
            sr = sr_ref[rows, :]
            drq_ref[rows, :] = _unrope_ret(dq, cr, sr).astype(BF16)
            drk_ref[rows, :] = _unrope_ret(dk * SCALE_RET, cr, sr).astype(BF16)
            drv_ref[rows, :] = dv.astype(BF16)
        g_sc[...] = gacc
        dw_ref[...] += dw

    blk = pl.BlockSpec((G * C, 128), lambda h, n: (NB - 1 - n, h))
    tab = pl.BlockSpec((G * C, 128), lambda h, n: (NB - 1 - n, 0))
    return pl.pallas_call(
        body, name="ret_bwd", grid=(RET_HEADS, NB),
        in_specs=[pl.BlockSpec((None, 8, 128), lambda h, n: (h, 0, 0)), blk, blk, blk,
                  pl.BlockSpec((G, 128, 128), lambda h, n: (h * NB + NB - 1 - n, 0, 0)), blk, blk, blk,
                  pl.BlockSpec((1, 128), lambda h, n: (0, h)), tab, tab],
        out_specs=[blk, blk, blk, blk, pl.BlockSpec((1, 128), lambda h, n: (0, h))],
        out_shape=[_sds((S, 512), BF16)] * 4 + [_sds((1, 512), F32)],
        scratch_shapes=[pltpu.VMEM((128, 128), F32)],
        compiler_params=_cp(("parallel", "arbitrary")),
    )(_decay_table(), rq, rk, rv, rprev, ry, rg, dro, gn_w, tabs[0], tabs[1])


def _inproj_bwd(drq, drk, drv, drg, dcq, dckv, dkr, w_in, dh1, x, g, S):
    tm = min(512, S)

    def body(drq_ref, drk_ref, drv_ref, drg_ref, dcq_ref, dckv_ref, dkr_ref, w_ref, dh1_ref, x_ref, g_ref,
             gx_ref, dproj_ref, dg_ref):
        @pl.when(pl.program_id(0) == 0)
        def _():
            dg_ref[...] = jnp.zeros(dg_ref.shape, F32)

        dproj_ref[:, 0:512] = drq_ref[...]
        dproj_ref[:, 512:1024] = drk_ref[...]
        dproj_ref[:, 1024:1536] = drv_ref[...]
        dproj_ref[:, 1536:2048] = drg_ref[...]
        dproj_ref[:, 2048:2432] = dcq_ref[...]
        dproj_ref[:, 2432:2688] = dckv_ref[...]
        dproj_ref[:, 2688:2816] = pltpu.roll(dkr_ref[...].astype(F32), 64, 1).astype(BF16)
        dx, ga = _rms_bwd(_dot_nt(dproj_ref[...], w_ref[...]), x_ref[...], g_ref[...])
        gx_ref[...] = dh1_ref[...] + dx
        dg_ref[...] += _colsum(ga)

    return pl.pallas_call(
        body, name="inproj_bwd", grid=(S // tm,),
        in_specs=[_rows(tm, 512)] * 4 + [_rows(tm, Q_LORA), _rows(tm, KV_LORA), _rows(tm, 128),
                                         _full(D_MODEL, IN_COLS_P), _rows(tm, D_MODEL), _rows(tm, D_MODEL),
                                         _full(1, D_MODEL)],
        out_specs=[_rows(tm, D_MODEL), _rows(tm, IN_COLS_P), _acc(1, D_MODEL)],
        out_shape=[_sds((S, D_MODEL), F32), _sds((S, IN_COLS_P), BF16), _sds((1, D_MODEL), F32)],
        compiler_params=_cp(("arbitrary",)),
    )(drq, drk, drv, drg, dcq, dckv, dkr, w_in, dh1, x, g)


def _pad_weights(w):
    w_in_p = jnp.pad(w["w_in"], ((0, 0), (0, IN_COLS_P - IN_COLS)))
    w_uq_p = jnp.pad(w["w_uq"].reshape(Q_LORA, MLA_HEADS, 96), ((0, 0), (0, 0), (0, 32))).reshape(Q_LORA, 1024)
    ukv = w["w_ukv"].reshape(KV_LORA, MLA_HEADS, 128)
    k_part = jnp.pad(ukv[:, :, :64], ((0, 0), (0, 0), (0, 64))).reshape(KV_LORA, 1024)
    w_ukv_p = jnp.concatenate([k_part, ukv[:, :, 64:].reshape(KV_LORA, 512)], axis=1)
    return w_in_p, w_uq_p, w_ukv_p


BIG_SPEC = {n: (r, c, ax) for n, r, c, ax in BIG}
COLUMN_MAJOR = ("w_in", "w_uq", "w_gate", "w_up")
GRAD_TRANSPOSED = ("w_gate", "w_up")
GATHER_FIRST = ("w_in", "w_uq", "w_ukv")
GATHER_LATE = tuple(n for n, _, _, _ in BIG if n not in GATHER_FIRST)
REDUCE_EARLY = ("w_ple_gate", "w_ple_proj", "w_down", "w_gate", "w_up", "w_o")
REDUCE_LAST = tuple(n for n, _, _, _ in BIG if n not in REDUCE_EARLY)


def _local_step(x, p, pos_f, tgt, w, sm, late_shards=None, c_idx=None):
    S = x.shape[0]
    spread = late_shards is not None
    w = dict(w)
    tabs, first = _rope_tables(pos_f, S, [late_shards[n] for n in GATHER_FIRST] if spread else ())
    for i, n in enumerate(GATHER_FIRST if spread else ()):
        w[n] = _from_chips(first[i], BIG_SPEC[n][2])
    w_in_p, w_uq_p, w_ukv_p = _pad_weights(w)

    xn, rq, rk, rv, rg, cq, ckv, kr = _inproj(x, sm["pre_mix_norm"], w_in_p, tabs, S)
    cqn, ckvn, qp, kp, v, kt, vt = _mla_up(cq, ckv, kr, sm["mla_q_norm"], sm["mla_kv_norm"], w_uq_p, w_ukv_p, tabs, S)
    mo, lse, gathered = _flash_fwd(qp, kp, vt, S, [late_shards[n] for n in GATHER_LATE] if spread else ())
    for i, n in enumerate(GATHER_LATE if spread else ()):
        w[n] = _from_chips(gathered[i], 0 if n in GRAD_TRANSPOSED else BIG_SPEC[n][2])
    if not spread:
        w.update({n: w[n].T for n in GRAD_TRANSPOSED})
    ry, ro, rprev = _ret_fwd(rq, rk, rv, rg, sm["ret_gn_w"], S)
    mix, h1, hn = _outproj(ro, mo, x, w["w_o"], sm["post_mix_norm"], sm["pre_ffn_norm"], S)
    dgate_f, dup_f, act = _ffn_up(hn, w["w_gate"], w["w_up"], S)
    ff, h2 = _ffn_down(act, w["w_down"], h1, sm["post_ffn_norm"], S)
    dz, dpe, dh2, h2b, loss_vec, d_ple_norm, d_b = _ple_loss(
        p, h2, tgt, w["w_ple_proj"], w["w_ple_gate"], sm["b_ple_gate"], sm["ple_norm"], S)

    gw = {}
    gs = {"ple_norm": d_ple_norm, "b_ple_gate": d_b}
    gw["w_ple_gate"] = _wgrad(h2b, dz, "wgrad_ple_gate", S)
    gw["w_ple_proj"] = _wgrad(p, dpe, "wgrad_ple_proj", S)
    dff, dgate, dup, gs["post_ffn_norm"] = _ffn_down_bwd(dh2, ff, sm["post_ffn_norm"], w["w_down"], dgate_f, dup_f, S)
    gw["w_down"] = _wgrad(act, dff, "wgrad_down", S)
    if spread:
        gw["w_gate"] = _wgrad(dgate, hn, "wgrad_gate", S)
        gw["w_up"] = _wgrad(dup, hn, "wgrad_up", S)
    else:
        gw["w_gate"] = _wgrad(hn, dgate, "wgrad_gate", S)
        gw["w_up"] = _wgrad(hn, dup, "wgrad_up", S)
    first = REDUCE_EARLY[:-1]
    g4 = [_by_chip(gw.pop(n), *((D_FF, D_MODEL, 0) if n in GRAD_TRANSPOSED else BIG_SPEC[n]))
          for n in first] if spread else []
    dh1, dmix, dro, dmo, gs["pre_ffn_norm"], gs["post_mix_norm"], got = _ffn_up_bwd(
        dgate, dup, w["w_gate"], w["w_up"], h1, mix, dh2, sm["pre_ffn_norm"], sm["post_mix_norm"], w["w_o"], S, g4)
    gw["w_o"] = jnp.concatenate([_wgrad(ro, dmix, "wgrad_o_ret", S), _wgrad(mo, dmix, "wgrad_o_mla", S)], axis=0)
    g4_o = [_by_chip(gw.pop("w_o"), *BIG_SPEC["w_o"])] if spread else []

    dmo_t, delta, got_o = _attn_delta(mo, dmo, S, g4_o)
    sums = [_add_half_rows(a, b, c_idx, "rs_add_halves_" + n)
            for n, a, b in zip(REDUCE_EARLY, g4 + g4_o, list(got) + list(got_o))] if spread else []
    dqp, dkp, dv, parts = _flash_bwd(qp, kp, kt, v, dmo, dmo_t, lse, delta, S, sums)
    dqh, dkv, dcq, dckv, dkr, gs["mla_q_norm"], gs["mla_kv_norm"] = _mla_up_bwd(
        dqp, dkp, dv, cq, ckv, sm["mla_q_norm"], sm["mla_kv_norm"], w_uq_p, w_ukv_p, tabs, S)
    g_uq_p = _wgrad(cqn, dqh, "wgrad_uq", S)
    g_ukv_p = _wgrad(ckvn, dkv, "wgrad_ukv", S)
    gw["w_uq"] = g_uq_p.reshape(Q_LORA, MLA_HEADS, 128)[:, :, :96].reshape(Q_LORA, 768)
    gw["w_ukv"] = jnp.concatenate(
        [g_ukv_p[:, :1024].reshape(KV_LORA, MLA_HEADS, 128)[:, :, :64], g_ukv_p[:, 1024:].reshape(KV_LORA, MLA_HEADS, 64)],
        axis=2).reshape(KV_LORA, 1024)

    drq, drk, drv, drg, gs["ret_gn_w"] = _ret_bwd(rq, rk, rv, rprev, ry, rg, dro, sm["ret_gn_w"], tabs, S)
    grad_x, dproj, gs["pre_mix_norm"] = _inproj_bwd(drq, drk, drv, drg, dcq, dckv, dkr, w_in_p, dh1, x,
                                                    sm["pre_mix_norm"], S)
    if spread:
        gw["w_in"] = _wgrad(dproj, xn, "wgrad_in", S)[:IN_COLS]
    else:
        gw["w_in"] = _wgrad(xn, dproj, "wgrad_in", S)[:, :IN_COLS]
    return loss_vec, grad_x, gw, gs, ((sums, parts) if spread else None)


def _my_place():
    x = lax.axis_index("x")
    y = lax.axis_index("y")
    c = lax.axis_index("c")
    return x, y, c


def _other_chips(x, y):
    return [(1 - x, y), (x, 1 - y), (1 - x, 1 - y)]


_ANY = pl.BlockSpec(memory_space=pl.ANY)


def _small_copies(v_ref, slots, sems):
    send, recv, lsem = sems
    x, y, c = _my_place()
    me = 4 * x + 2 * y + c
    cps = [pltpu.make_async_copy(v_ref, slots.at[me], lsem)]
    for r in range(1, N_DEV):
        peer = (x ^ (r >> 2), y ^ ((r >> 1) & 1), c ^ (r & 1))
        cps.append(pltpu.make_async_remote_copy(
            src_ref=v_ref, dst_ref=slots.at[me], send_sem=send.at[r - 1], recv_sem=recv.at[r - 1],
            device_id=peer, device_id_type=MESH))
    return cps


def _small_sum(slots, out_ref):
    acc = slots[0]
    for d in range(1, N_DEV):
        acc = acc + slots[d]
    out_ref[...] = acc
    loss = jnp.sum(acc[9:10, :], axis=1, keepdims=True) * (0.5 / D_MODEL)
    out_ref[9:10, :] = jnp.broadcast_to(loss, (1, PACK_COLS))


def _small_scratch():
    return [pltpu.VMEM((N_DEV, SMALL_ROWS, PACK_COLS), F32), pltpu.SemaphoreType.DMA((N_DEV - 1,)),
            pltpu.SemaphoreType.DMA((N_DEV - 1,)), pltpu.SemaphoreType.DMA]


N_BIG = len(BIG)


def _half(c, rows, align):
    h = rows // 2
    return pl.ds(pl.multiple_of(c * h, align), h)


def _gather_out_shapes(shards):
    return [_sds((N_CHIPS,) + tuple(s.shape), BF16) for s in shards]


def _gather_sems(n):
    return [pltpu.SemaphoreType.DMA((n, 3))] * 4 + [pltpu.SemaphoreType.DMA((n,))] * 2


def _gather_phase(phase, ins, outs, sems):
    send1, recv1, send2, recv2, send3, recv3 = sems
    x, y, c = _my_place()
    me = 2 * x + y
    chips = _other_chips(x, y)
    sib = (x, y, 1 - c)
    for t in range(len(ins)):
        rows = ins[t].shape[0]
        half = _half(c, rows, 16)
        other = _half(1 - c, rows, 16)
        def own():
            return pltpu.make_async_remote_copy(
                src_ref=ins[t], dst_ref=outs[t].at[me], send_sem=send3.at[t], recv_sem=recv3.at[t],
                device_id=sib, device_id_type=MESH)

        if phase == 0:
            own().start()
        if phase == 2:
            own().wait()
        for k, (cx, cy) in enumerate(chips):
            src = 2 * cx + cy

            def over_ici(slab):
                return pltpu.make_async_remote_copy(
                    src_ref=ins[t].at[half], dst_ref=outs[t].at[slab, half], send_sem=send1.at[t, k],
                    recv_sem=recv1.at[t, k], device_id=(cx, cy, c), device_id_type=MESH)

            def over_d2d(rows):
                return pltpu.make_async_remote_copy(
                    src_ref=outs[t].at[src, rows], dst_ref=outs[t].at[src, rows], send_sem=send2.at[t, k],
                    recv_sem=recv2.at[t, k], device_id=sib, device_id_type=MESH)

            if phase == 0:
                over_ici(me).start()
            if phase == 1:
                over_ici(src).wait_recv()
                over_d2d(half).start()
            if phase == 2:
                over_d2d(other).wait_recv()
                over_ici(me).wait_send()
                over_d2d(half).wait_send()


def _swap_copies(ins, outs, sems):
    send, recv = sems
    x, y, c = _my_place()
    return [pltpu.make_async_remote_copy(
        src_ref=ins[t].at[:, _half(1 - c, ins[t].shape[1], 8)], dst_ref=outs[t], send_sem=send.at[t],
        recv_sem=recv.at[t], device_id=(x, y, 1 - c), device_id_type=MESH) for t in range(len(ins))]


def _swap_out_shapes(gs):
    return [_sds((N_CHIPS, g.shape[1] // 2, g.shape[2]), F32) for g in gs]


def _swap_sems(n):
    return [pltpu.SemaphoreType.DMA((n,)), pltpu.SemaphoreType.DMA((n,))]


def _swap_half_rows(gs):
    n = len(gs)

    def body(*refs):
        cps = _swap_copies(refs[:n], refs[n:2 * n], refs[2 * n:])
        for cp in cps:
            cp.start()
        for cp in cps:
            cp.wait()

    return pl.pallas_call(
        body, name="rs_swap_halves",
        in_specs=[_ANY] * n, out_specs=[_ANY] * n, out_shape=_swap_out_shapes(gs), scratch_shapes=_swap_sems(n),
    )(*gs)


def _add_half_rows(g, got, c_idx, name):
    _, rows, cols = g.shape
    h = rows // 2

    def body(c_ref, a_ref, b_ref, o_ref):
        o_ref[...] = (a_ref[...] + b_ref[...]).astype(BF16)

    grid_spec = pltpu.PrefetchScalarGridSpec(
        num_scalar_prefetch=1, grid=(N_CHIPS,),
        in_specs=[pl.BlockSpec((None, h, cols), lambda j, c: (j, c[0], 0)),
                  pl.BlockSpec((None, h, cols), lambda j, c: (j, 0, 0))],
        out_specs=pl.BlockSpec((None, h, cols), lambda j, c: (j, 0, 0)),
    )
    return pl.pallas_call(
        body, name=name, grid_spec=grid_spec, out_shape=_sds((N_CHIPS, h, cols), BF16),
        compiler_params=_cp(("parallel",)),
    )(c_idx, g, got)


def _scatter_to_chips(ts, vec):
    n = len(ts)

    def body(*refs):
        ins, v_ref, outs, small_ref = refs[:n], refs[n], refs[n + 1:2 * n + 1], refs[2 * n + 1]
        slots, small_sems, sems = refs[2 * n + 2], refs[2 * n + 3:2 * n + 6], refs[2 * n + 6:]
        small = _small_copies(v_ref, slots, small_sems)
        cps = _scatter_copies(ins, outs, sems)
        for cp in small + cps:
            cp.start()
        for cp in small:
            cp.wait()
        _small_sum(slots, small_ref)
        for cp in cps:
            cp.wait()

    vm = pl.BlockSpec(memory_space=pltpu.VMEM)
    *parts, small_sum = pl.pallas_call(
        body, name="rs_scatter_chips",
        in_specs=[_ANY] * n + [vm], out_specs=[_ANY] * n + [vm],
        out_shape=_scatter_out_shapes(ts) + [_sds((SMALL_ROWS, PACK_COLS), F32)],
        scratch_shapes=_small_scratch() + _scatter_sems(n),
    )(*ts, vec)
    return parts, small_sum


def _scatter_copies(ins, outs, sems):
    send, recv = sems
    x, y, c = _my_place()
    return [pltpu.make_async_remote_copy(
        src_ref=ins[t].at[2 * cx + cy], dst_ref=outs[t].at[k], send_sem=send.at[t, k], recv_sem=recv.at[t, k],
        device_id=(cx, cy, c), device_id_type=MESH)
        for t in range(len(ins)) for k, (cx, cy) in enumerate(_other_chips(x, y))]


def _scatter_out_shapes(ts):
    return [_sds((3,) + tuple(t.shape[1:]), BF16) for t in ts]


def _scatter_sems(n):
    return [pltpu.SemaphoreType.DMA((n, 3)), pltpu.SemaphoreType.DMA((n, 3))]


def _add_four(mine, parts, place, name):
    _, h, cols = parts.shape

    def body(pl_ref, m_ref, p_ref, o_ref):
        o_ref[...] = ((m_ref[...].astype(F32) + p_ref[0].astype(F32)) + p_ref[1].astype(F32)) + p_ref[2].astype(F32)

    grid_spec = pltpu.PrefetchScalarGridSpec(
        num_scalar_prefetch=1, grid=(1,),
        in_specs=[pl.BlockSpec((None, h, cols), lambda i, pc: (pc[0], 0, 0)),
                  pl.BlockSpec((3, h, cols), lambda i, pc: (0, 0, 0))],
        out_specs=pl.BlockSpec((h, cols), lambda i, pc: (pc[1], 0)),
    )
    return pl.pallas_call(
        body, name=name, grid_spec=grid_spec, out_shape=_sds((2 * h, cols), F32),
        compiler_params=_cp(("arbitrary",)),
    )(place, mine, parts)


def _join_half_rows(rs):
    n = len(rs)

    def body(*refs):
        ins, outs = refs[:n], refs[n:2 * n]
        send, recv = refs[2 * n:]
        x, y, c = _my_place()
        cps = []
        for t in range(n):
            half = _half(c, outs[t].shape[0], 8)
            rc = pltpu.make_async_remote_copy(
                src_ref=ins[t].at[half], dst_ref=outs[t].at[half], send_sem=send.at[t], recv_sem=recv.at[t],
                device_id=(x, y, 1 - c), device_id_type=MESH)
            rc.start()
            cps.append(rc)
        for cp in cps:
            cp.wait()

    return pl.pallas_call(
        body, name="rs_join_halves",
        in_specs=[_ANY] * n, out_specs=[_ANY] * n,
        out_shape=[_sds(r.shape, F32) for r in rs],
        input_output_aliases={i: i for i in range(n)},
        scratch_shapes=[pltpu.SemaphoreType.DMA((n,))] * 2,
    )(*rs)


def _by_chip(full, rows, cols, axis):
    if axis == 0:
        return full.reshape(N_CHIPS, rows // N_CHIPS, cols)
    return full.reshape(rows, N_CHIPS, cols // N_CHIPS).transpose(1, 0, 2)


def _from_chips(parts, axis):
    _, r, c = parts.shape
    if axis == 0:
        return parts.reshape(N_CHIPS * r, c)
    return parts.transpose(1, 0, 2).reshape(r, N_CHIPS * c)


def _adamw(wt, g, m, v, name):
    _, R, C = wt.shape
    tr = max(d for d in range(8, R + 1, 8) if R % d == 0 and (d * C <= 256 * 1024 or d == 8))

    def body(w_ref, g_ref, m_ref, v_ref, d_ref, nm_ref, nv_ref):
        gg = g_ref[...]
        m_new = ADAM_B1 * m_ref[...] + (1.0 - ADAM_B1) * gg
        v_new = ADAM_B2 * v_ref[...] + (1.0 - ADAM_B2) * (gg * gg)
        m_hat = m_new / (1.0 - ADAM_B1 ** ADAM_STEP)
        v_hat = v_new / (1.0 - ADAM_B2 ** ADAM_STEP)
        d_ref[...] = -ADAM_LR * (m_hat / (jnp.sqrt(v_hat) + ADAM_EPS) + ADAM_WD * w_ref[...])
        nm_ref[...] = m_new
        nv_ref[...] = v_new

    spec = pl.BlockSpec((None, tr, C), lambda i: (0, i, 0))
    return pl.pallas_call(
        body, name=name, grid=(R // tr,), in_specs=[spec, pl.BlockSpec((tr, C), lambda i: (i, 0)), spec, spec],
        out_specs=[spec] * 3, out_shape=[_sds((1, R, C), F32)] * 3,
        compiler_params=_cp(("parallel",)),
    )(wt, g, m, v)


def _pack_small(vals, loss_vec=None):
    rows = [jnp.pad(vals[n].reshape(-1), (0, PACK_COLS - sz)) for n, sz in SMALL]
    rows.append(loss_vec.reshape(-1) if loss_vec is not None else jnp.zeros((PACK_COLS,), F32))
    rows += [jnp.zeros((PACK_COLS,), F32)] * (SMALL_ROWS - len(rows))
    return jnp.stack(rows)


def kernel(x, p, positions, pre_mix_norm, w_in, ret_gn_w, mla_q_norm, w_uq, mla_kv_norm, w_ukv, w_o, post_mix_norm, pre_ffn_norm, w_gate, w_up, w_down, post_ffn_norm, w_ple_proj, ple_norm, w_ple_gate, b_ple_gate, loss_target, m_pre_mix_norm, m_w_in, m_ret_gn_w, m_mla_q_norm, m_w_uq, m_mla_kv_norm, m_w_ukv, m_w_o, m_post_mix_norm, m_pre_ffn_norm, m_w_gate, m_w_up, m_w_down, m_post_ffn_norm, m_w_ple_proj, m_ple_norm, m_w_ple_gate, m_b_ple_gate, v_pre_mix_norm, v_w_in, v_ret_gn_w, v_mla_q_norm, v_w_uq, v_mla_kv_norm, v_w_ukv, v_w_o, v_post_mix_norm, v_pre_ffn_norm, v_w_gate, v_w_up, v_w_down, v_post_ffn_norm, v_w_ple_proj, v_ple_norm, v_w_ple_gate, v_b_ple_gate):
    wts = dict(pre_mix_norm=pre_mix_norm, w_in=w_in, ret_gn_w=ret_gn_w, mla_q_norm=mla_q_norm, w_uq=w_uq,
               mla_kv_norm=mla_kv_norm, w_ukv=w_ukv, w_o=w_o, post_mix_norm=post_mix_norm, pre_ffn_norm=pre_ffn_norm,
               w_gate=w_gate, w_up=w_up, w_down=w_down, post_ffn_norm=post_ffn_norm, w_ple_proj=w_ple_proj,
               ple_norm=ple_norm, w_ple_gate=w_ple_gate, b_ple_gate=b_ple_gate)
    mom = dict(pre_mix_norm=m_pre_mix_norm, w_in=m_w_in, ret_gn_w=m_ret_gn_w, mla_q_norm=m_mla_q_norm, w_uq=m_w_uq,
               mla_kv_norm=m_mla_kv_norm, w_ukv=m_w_ukv, w_o=m_w_o, post_mix_norm=m_post_mix_norm,
               pre_ffn_norm=m_pre_ffn_norm, w_gate=m_w_gate, w_up=m_w_up, w_down=m_w_down, post_ffn_norm=m_post_ffn_norm,
               w_ple_proj=m_w_ple_proj, ple_norm=m_ple_norm, w_ple_gate=m_w_ple_gate, b_ple_gate=m_b_ple_gate)
    var = dict(pre_mix_norm=v_pre_mix_norm, w_in=v_w_in, ret_gn_w=v_ret_gn_w, mla_q_norm=v_mla_q_norm, w_uq=v_w_uq,
               mla_kv_norm=v_mla_kv_norm, w_ukv=v_w_ukv, w_o=v_w_o, post_mix_norm=v_post_mix_norm,
               pre_ffn_norm=v_pre_ffn_norm, w_gate=v_w_gate, w_up=v_w_up, w_down=v_w_down, post_ffn_norm=v_post_ffn_norm,
               w_ple_proj=v_w_ple_proj, ple_norm=v_ple_norm, w_ple_gate=v_w_ple_gate, b_ple_gate=v_b_ple_gate)

    S = x.shape[1]
    shard2d = {n: wts[n][0] for n, _, _, _ in BIG}
    small2d = {n: wts[n] for n, _ in SMALL}

    shard_bf = {n: (jnp.swapaxes(wts[n], 1, 2)[0] if n in GRAD_TRANSPOSED else shard2d[n]).astype(BF16) for n in shard2d}
    pos_f = positions.astype(F32).reshape(S, 1)
    c_idx = lax.axis_index("c").astype(jnp.int32).reshape(1)
    loss_vec, grad_x, gw, gs, (sums_early, parts_early) = _local_step(
        x[0], p[0, 0], pos_f, loss_target[0], {}, small2d, shard_bf, c_idx)

    g4 = [_by_chip(gw[n], *BIG_SPEC[n]) for n in REDUCE_LAST if n != "w_in"]
    g4.insert(REDUCE_LAST.index("w_in"), jnp.pad(gw["w_in"].reshape(N_CHIPS, IN_SHARD, D_MODEL),
                                                 ((0, 0), (0, IN_SHARD_P - IN_SHARD), (0, 0))))
    got = _swap_half_rows(g4)
    sums_last = [_add_half_rows(g4[i], got[i], c_idx, "rs_add_halves_" + n) for i, n in enumerate(REDUCE_LAST)]
    parts_last, small_sum = _scatter_to_chips(sums_last, _pack_small(gs, loss_vec))
    place = jnp.stack([2 * lax.axis_index("x") + lax.axis_index("y"), lax.axis_index("c")]).astype(jnp.int32)
    names = REDUCE_EARLY + REDUCE_LAST
    reduced = _join_half_rows(
        [_add_four(sm_, pt_, place, "rs_add_chips_" + n)
         for n, sm_, pt_ in zip(names, sums_early + sums_last, list(parts_early) + list(parts_last))])
    g_shard = dict(zip(names, reduced))

    loss = small_sum[9, 0]
    g_small = {n: small_sum[i:i + 1, :sz] for i, (n, sz) in enumerate(SMALL)}

    grads, delta, new_m, new_v = {}, {}, {}, {}
    for n, _, _, _ in BIG:
        if n in COLUMN_MAJOR:
            turn = lambda a: jnp.swapaxes(a, 1, 2)
            g_t = g_shard[n][:IN_SHARD] if n == "w_in" else g_shard[n] if n in GRAD_TRANSPOSED else g_shard[n].T
            d, nm, nv = _adamw(turn(wts[n]), g_t, turn(mom[n]), turn(var[n]), "adamw_" + n)
            grads[n], delta[n], new_m[n], new_v[n] = turn(g_t[None]), turn(d), turn(nm), turn(nv)
        else:
            delta[n], new_m[n], new_v[n] = _adamw(wts[n], g_shard[n], mom[n], var[n], "adamw_" + n)
            grads[n] = g_shard[n][None]
    d, nm, nv = _adamw(_pack_small(small2d)[None], small_sum, _pack_small(mom)[None], _pack_small(var)[None],
                       "adamw_small")
    for i, (n, sz) in enumerate(SMALL):
        grads[n] = g_small[n]
        delta[n], new_m[n], new_v[n] = d[0, i:i + 1, :sz], nm[0, i:i + 1, :sz], nv[0, i:i + 1, :sz]

    return (loss, grad_x[None], *[grads[n] for n in ALL_W], *[delta[n] for n in ALL_W],
            *[new_m[n] for n in ALL_W], *[new_v[n] for n in ALL_W])
```

```python
import functools
import math

import jax
import jax.numpy as jnp
import numpy as np
from jax import lax
from jax.experimental import pallas as pl
from jax.experimental.pallas import tpu as pltpu

F32 = jnp.float32
BF16 = jnp.bfloat16
MESH = pl.DeviceIdType.MESH

D_MODEL = 1024
D_FF = 2816
PLE_DIM = 256
RET_HEADS = 4
RET_DIM = 128
RET_WIDTH = 512
RET_CHUNK = 256
RET_GROUP_FWD = 16
RET_GROUP_BWD = 8
MLA_HEADS = 8
MLA_NOPE = 64
MLA_ROPE = 32
MLA_V = 64
Q_LORA = 384
KV_LORA = 256
IN_COLS = 2720
IN_COLS_P = 2816
IN_SHARD = IN_COLS // 4
IN_SHARD_P = 688
ROPE_BASE = 10000.0
EPS = 1e-6
SCALE_MLA = 1.0 / math.sqrt(MLA_NOPE + MLA_ROPE)
SCALE_RET = RET_DIM ** -0.5
NEG = -1e30

ADAM_LR = 0.001
ADAM_B1 = 0.9
ADAM_B2 = 0.999
ADAM_EPS = 1e-08
ADAM_WD = 0.01
ADAM_STEP = 10

N_CHIPS = 4
N_DEV = 8
VMEM_MB = 56

BIG = (
    ("w_in", 1024, 2720, 1),
    ("w_uq", 384, 768, 1),
    ("w_ukv", 256, 1024, 1),
    ("w_o", 1024, 1024, 0),
    ("w_gate", 1024, 2816, 1),
    ("w_up", 1024, 2816, 1),
    ("w_down", 2816, 1024, 0),
    ("w_ple_proj", 256, 1024, 1),
    ("w_ple_gate", 1024, 1024, 0),
)
SMALL = (
    ("pre_mix_norm", 1024),
    ("ret_gn_w", 512),
    ("mla_q_norm", 384),
    ("mla_kv_norm", 256),
    ("post_mix_norm", 1024),
    ("pre_ffn_norm", 1024),
    ("post_ffn_norm", 1024),
    ("ple_norm", 1024),
    ("b_ple_gate", 1024),
)
ALL_W = ("pre_mix_norm", "w_in", "ret_gn_w", "mla_q_norm", "w_uq", "mla_kv_norm", "w_ukv", "w_o", "post_mix_norm",
         "pre_ffn_norm", "w_gate", "w_up", "w_down", "post_ffn_norm", "w_ple_proj", "ple_norm", "w_ple_gate", "b_ple_gate")
PACK_COLS = 1024
SMALL_ROWS = 16


def _cp(sem=None, mb=VMEM_MB, **kw):
    return pltpu.CompilerParams(dimension_semantics=sem, vmem_limit_bytes=mb * 1024 * 1024, **kw)


def _bf(x):
    return x.astype(BF16)


def _dot(a, b):
    return jnp.dot(_bf(a), _bf(b), preferred_element_type=F32)


def _dot_nt(a, b):
    return lax.dot_general(_bf(a), _bf(b), (((1,), (1,)), ((), ())), preferred_element_type=F32)


def _dot_tn(a, b):
    return lax.dot_general(_bf(a), _bf(b), (((0,), (0,)), ((), ())), preferred_element_type=F32)


def _sig(x):
    return 1.0 / (1.0 + jnp.exp(-x))


def _rms(x, g):
    r = lax.rsqrt(jnp.mean(x * x, axis=-1, keepdims=True) + EPS)
    return x * r * g


def _rms_bwd(dy, x, g):
    r = lax.rsqrt(jnp.mean(x * x, axis=-1, keepdims=True) + EPS)
    xh = x * r
    dxh = dy * g
    dx = r * (dxh - xh * jnp.mean(dxh * xh, axis=-1, keepdims=True))
    return dx, dy * xh


def _colsum(x):
    return jnp.sum(x, axis=0, keepdims=True)


def _rope_ret(x, cr, sr):
    return x * cr + pltpu.roll(x, 64, 1) * sr


def _unrope_ret(dy, cr, sr):
    return dy * cr + pltpu.roll(dy * sr, 64, 1)


def _rope_mla(x, cm, sa, sb):
    return x * cm + pltpu.roll(x, 112, 1) * sa + pltpu.roll(x, 16, 1) * sb


def _unrope_mla(dy, cm, sa, sb):
    return dy * cm + pltpu.roll(dy * sa, 16, 1) + pltpu.roll(dy * sb, 112, 1)


def _rows(tm, w, col=0):
    return pl.BlockSpec((tm, w), lambda i: (i, col))


def _full(*shape):
    return pl.BlockSpec(shape, lambda i: (0,) * len(shape), pipeline_mode=pl.Buffered(1))


def _acc(*shape):
    return pl.BlockSpec(shape, lambda i: (0,) * len(shape))


def _sds(shape, dtype):
    return jax.ShapeDtypeStruct(shape, dtype)


def _rope_tables(pos_f, S, shards=()):
    tm = min(512, S)
    n = len(shards)
    steps = S // tm
    inv_r = (1.0 / (np.float32(ROPE_BASE) ** (np.arange(64, dtype=np.float32) / np.float32(64)))).astype(np.float32)
    inv_m16 = (1.0 / (np.float32(ROPE_BASE) ** (np.arange(16, dtype=np.float32) / np.float32(16)))).astype(np.float32)
    inv_r = np.concatenate([inv_r, inv_r])[None, :]
    inv_m = np.zeros((1, 128), np.float32)
    inv_m[0, 64:80] = inv_m16
    inv_m[0, 80:96] = inv_m16

    def body(pos_ref, invr_ref, invm_ref, *rest):
        w_ins, (cr_ref, sr_ref, cm_ref, sa_ref, sb_ref) = rest[:n], rest[n:n + 5]
        w_outs, sems = rest[n + 5:2 * n + 5], rest[2 * n + 5:]
        i = pl.program_id(0)
        if n:
            @pl.when(i == 0)
            def _():
                _gather_phase(0, w_ins, w_outs, sems)

            @pl.when(i == steps - 1)
            def _():
                _gather_phase(1, w_ins, w_outs, sems)

        pos = pos_ref[...]
        lane = lax.broadcasted_iota(jnp.int32, (tm, 128), 1)
        ar = pos * invr_ref[...]
        s = jnp.sin(ar)
        cr_ref[...] = jnp.cos(ar)
        sr_ref[...] = jnp.where(lane < 64, -s, s)
        am = pos * invm_ref[...]
        c2 = jnp.cos(am)
        s2 = jnp.sin(am)
        cm_ref[...] = jnp.where(lane < 64, 1.0, jnp.where(lane < 96, c2, 0.0))
        sa_ref[...] = jnp.where((lane >= 64) & (lane < 80), -s2, 0.0)
        sb_ref[...] = jnp.where((lane >= 80) & (lane < 96), s2, 0.0)

        if n:
            @pl.when(i == steps - 1)
            def _():
                _gather_phase(2, w_ins, w_outs, sems)

    outs = pl.pallas_call(
        body, name="rope_tables", grid=(steps,),
        in_specs=[_rows(tm, 1), _full(1, 128), _full(1, 128)] + [_ANY] * n,
        out_specs=[_rows(tm, 128)] * 5 + [_ANY] * n,
        out_shape=[_sds((S, 128), F32)] * 5 + _gather_out_shapes(shards),
        scratch_shapes=_gather_sems(n) if n else [],
        compiler_params=_cp(("arbitrary",)),
    )(pos_f, jnp.asarray(inv_r), jnp.asarray(inv_m), *shards)
    return outs[:5], outs[5:]


def _inproj(x, g, w_in, tabs, S):
    tm = min(512, S)

    def body(x_ref, g_ref, w_ref, cr_ref, sr_ref, cm_ref, sa_ref, sb_ref,
             xn_ref, rq_ref, rk_ref, rv_ref, rg_ref, cq_ref, ckv_ref, kr_ref):
        xb = _rms(x_ref[...], g_ref[...]).astype(BF16)
        xn_ref[...] = xb
        cr = cr_ref[...]
        sr = sr_ref[...]
        q = jnp.dot(xb, w_ref[:, 0:512], preferred_element_type=F32)
        k = jnp.dot(xb, w_ref[:, 512:1024], preferred_element_type=F32)
        for h in range(RET_HEADS):
            sl = slice(h * 128, (h + 1) * 128)
            rq_ref[:, sl] = _rope_ret(q[:, sl], cr, sr).astype(BF16)
            rk_ref[:, sl] = (_rope_ret(k[:, sl], cr, sr) * SCALE_RET).astype(BF16)
        rv_ref[...] = jnp.dot(xb, w_ref[:, 1024:1536], preferred_element_type=F32).astype(BF16)
        rg_ref[...] = jnp.dot(xb, w_ref[:, 1536:2048], preferred_element_type=F32)
        cq_ref[...] = jnp.dot(xb, w_ref[:, 2048:2432], preferred_element_type=F32)
        ckv_ref[...] = jnp.dot(xb, w_ref[:, 2432:2688], preferred_element_type=F32)
        kr = pltpu.roll(jnp.dot(xb, w_ref[:, 2688:2816], preferred_element_type=F32), 64, 1)
        kr_ref[...] = _rope_mla(kr, cm_ref[...], sa_ref[...], sb_ref[...])

    return pl.pallas_call(
        body, name="inproj", grid=(S // tm,),
        in_specs=[_rows(tm, D_MODEL), _full(1, D_MODEL), _full(D_MODEL, IN_COLS_P)] + [_rows(tm, 128)] * 5,
        out_specs=[_rows(tm, D_MODEL)] + [_rows(tm, 512)] * 4 + [_rows(tm, Q_LORA), _rows(tm, KV_LORA), _rows(tm, 128)],
        out_shape=[_sds((S, D_MODEL), BF16)] + [_sds((S, 512), BF16)] * 3
        + [_sds((S, 512), F32), _sds((S, Q_LORA), F32), _sds((S, KV_LORA), F32), _sds((S, 128), F32)],
        compiler_params=_cp(("parallel",)),
    )(x, g, w_in, *tabs)


def _mla_up(cq, ckv, kr, gq, gkv, w_uq, w_ukv, tabs, S):
    tm = min(512, S)

    def body(cq_ref, ckv_ref, kr_ref, gq_ref, gkv_ref, wuq_ref, wukv_ref, cm_ref, sa_ref, sb_ref,
             cqn_ref, ckvn_ref, qp_ref, kp_ref, v_ref, kt_ref, vt_ref):
        cm = cm_ref[...]
        sa = sa_ref[...]
        sb = sb_ref[...]
        cqn = _rms(cq_ref[...], gq_ref[...]).astype(BF16)
        cqn_ref[...] = cqn
        ckvn = _rms(ckv_ref[...], gkv_ref[...]).astype(BF16)
        ckvn_ref[...] = ckvn
        qh = jnp.dot(cqn, wuq_ref[...], preferred_element_type=F32)
        kv = jnp.dot(ckvn, wukv_ref[...], preferred_element_type=F32)
        kr_blk = kr_ref[...]
        for h in range(MLA_HEADS):
            sl = slice(h * 128, (h + 1) * 128)
            qp_ref[:, sl] = (_rope_mla(qh[:, sl], cm, sa, sb) * SCALE_MLA).astype(BF16)
            kh = kv[:, sl] + kr_blk
            kp_ref[:, sl] = kh.astype(BF16)
            kt_ref[sl, :] = kh.T.astype(BF16)
        for h in range(MLA_HEADS // 2):
            vh = kv[:, 1024 + h * 128:1024 + (h + 1) * 128]
            v_ref[:, h * 128:(h + 1) * 128] = vh.astype(BF16)
            vt_ref[h * 128:(h + 1) * 128, :] = vh.T.astype(BF16)

    cols = lambda r: pl.BlockSpec((r, tm), lambda i: (0, i))
    return pl.pallas_call(
        body, name="mla_up", grid=(S // tm,),
        in_specs=[_rows(tm, Q_LORA), _rows(tm, KV_LORA), _rows(tm, 128), _full(1, Q_LORA), _full(1, KV_LORA),
                  _full(Q_LORA, 1024), _full(KV_LORA, 1536)] + [_rows(tm, 128)] * 3,
        out_specs=[_rows(tm, Q_LORA), _rows(tm, KV_LORA), _rows(tm, 1024), _rows(tm, 1024), _rows(tm, 512),
                   cols(1024), cols(512)],
        out_shape=[_sds((S, Q_LORA), BF16), _sds((S, KV_LORA), BF16), _sds((S, 1024), BF16), _sds((S, 1024), BF16),
                   _sds((S, 512), BF16), _sds((1024, S), BF16), _sds((512, S), BF16)],
        compiler_params=_cp(("parallel",)),
    )(cq, ckv, kr, gq, gkv, w_uq, w_ukv, *tabs[2:])


def _tri_pairs(nq, k_major):
    if k_major:
        pairs = [(qb, kb) for kb in range(nq) for qb in range(kb, nq)]
    else:
        pairs = [(qb, kb) for qb in range(nq) for kb in range(qb + 1)]
    qb_of = np.array([p[0] for p in pairs], np.int32)
    kb_of = np.array([p[1] for p in pairs], np.int32)
    return jnp.asarray(qb_of), jnp.asarray(kb_of), len(pairs)


ATT_ROWS = 32
FWD_HEADS = 8
BWD_HEADS = 4


def _causal_keep(r0, rows, tq):
    key = r0 + lax.broadcasted_iota(jnp.int32, (rows, tq), 0)
    qry = lax.broadcasted_iota(jnp.int32, (rows, tq), 1)
    return key <= qry


def _flash_fwd(qp, kp, vt, S, shards=()):
    tq = min(512, S)
    nq = S // tq
    RB = ATT_ROWS
    NH = FWD_HEADS
    qb_of, kb_of, T = _tri_pairs(nq, k_major=False)
    n = len(shards)
    steps = (MLA_HEADS // NH) * T

    def body(qb_ref, kb_ref, q_ref, k_ref, vt_ref, *rest):
        w_ins, (o_ref, lse_ref), w_outs = rest[:n], rest[n:n + 2], rest[n + 2:2 * n + 2]
        m_sc, l_sc, acc_sc, s_sc, p_sc = rest[2 * n + 2:2 * n + 7]
        sems = rest[2 * n + 7:]
        t = pl.program_id(1)
        qb = qb_ref[t]
        kb = kb_ref[t]
        lin = pl.program_id(0) * T + t

        if n:
            @pl.when(lin == 0)
            def _():
                _gather_phase(0, w_ins, w_outs, sems)

            @pl.when(lin == steps // 2)
            def _():
                _gather_phase(1, w_ins, w_outs, sems)

        @pl.when(kb == 0)
        def _():
            m_sc[...] = jnp.full(m_sc.shape, NEG, F32)
            l_sc[...] = jnp.zeros(l_sc.shape, F32)
            acc_sc[...] = jnp.zeros(acc_sc.shape, F32)

        def scores(a):
            sl = slice(a * 128, (a + 1) * 128)
            s_sc[a] = _dot_nt(k_ref[:, sl], q_ref[:, sl])

        def step(masked):
            for a in range(NH):
                scores(a)
            for a in range(NH):
                mx = [jnp.full((8, tq), NEG, F32) for _ in range(RB // 8)]
                for r in range(0, tq, RB):
                    sc = s_sc[a, r:r + RB, :]
                    if masked:
                        sc = jnp.where(_causal_keep(r, RB, tq), sc, NEG)
                        s_sc[a, r:r + RB, :] = sc
                    for i in range(RB // 8):
                        mx[i] = jnp.maximum(mx[i], sc[i * 8:(i + 1) * 8, :])
                mx8 = functools.reduce(jnp.maximum, mx)
                m_prev = m_sc[a]
                m_new = jnp.maximum(m_prev, jnp.max(mx8, axis=0, keepdims=True))
                al = jnp.exp(m_prev - m_new)
                m_sc[a] = m_new
                ls = [jnp.zeros((8, tq), F32) for _ in range(RB // 8)]
                for r in range(0, tq, RB):
                    p = jnp.exp(s_sc[a, r:r + RB, :] - m_new)
                    for i in range(RB // 8):
                        ls[i] = ls[i] + p[i * 8:(i + 1) * 8, :]
                    p_sc[a, r:r + RB, :] = p.astype(BF16)
                l_sc[a] = al * l_sc[a] + jnp.sum(functools.reduce(jnp.add, ls), axis=0, keepdims=True)
                pair = slice((a // 2) * 128, (a // 2 + 1) * 128)
                pv = jnp.dot(vt_ref[pair, :], p_sc[a], preferred_element_type=F32)
                rs = slice(a * 64, (a + 1) * 64)
                own = slice((a % 2) * 64, (a % 2 + 1) * 64)
                acc_sc[rs, :] = acc_sc[rs, :] * al + pv[own, :]

        @pl.when(kb < qb)
        def _():
            step(False)

        @pl.when(kb == qb)
        def _():
            step(True)
            for a in range(NH):
                rs = slice(a * 64, (a + 1) * 64)
                acc_sc[rs, :] = acc_sc[rs, :] / l_sc[a]
                lse_ref[a:a + 1, :] = m_sc[a] + jnp.log(l_sc[a])
            o_ref[...] = acc_sc[...].T.astype(BF16)

        if n:
            @pl.when(lin == steps - 1)
            def _():
                _gather_phase(2, w_ins, w_outs, sems)

    grid_spec = pltpu.PrefetchScalarGridSpec(
        num_scalar_prefetch=2, grid=(MLA_HEADS // NH, T),
        in_specs=[pl.BlockSpec((tq, 128 * NH), lambda j, t, qb, kb: (qb[t], j)),
                  pl.BlockSpec((tq, 128 * NH), lambda j, t, qb, kb: (kb[t], j)),
                  pl.BlockSpec((64 * NH, tq), lambda j, t, qb, kb: (j, kb[t]))] + [_ANY] * n,
        out_specs=[pl.BlockSpec((tq, 64 * NH), lambda j, t, qb, kb: (qb[t], j)),
                   pl.BlockSpec((None, NH, tq), lambda j, t, qb, kb: (j, 0, qb[t]))] + [_ANY] * n,
        scratch_shapes=[pltpu.VMEM((NH, 1, tq), F32), pltpu.VMEM((NH, 1, tq), F32), pltpu.VMEM((64 * NH, tq), F32),
                        pltpu.VMEM((NH, tq, tq), F32), pltpu.VMEM((NH, tq, tq), BF16)] + (_gather_sems(n) if n else []),
    )
    out, lse, *gathered = pl.pallas_call(
        body, name="flash_fwd", grid_spec=grid_spec,
        out_shape=[_sds((S, 512), BF16), _sds((MLA_HEADS // NH, NH, S), F32)] + _gather_out_shapes(shards),
        compiler_params=_cp(("arbitrary", "arbitrary")),
    )(qb_of, kb_of, qp, kp, vt, *shards)
    return out, lse.reshape(MLA_HEADS // 2, 2, S), gathered


def _decay_table():
    log_g = np.log(1.0 - 2.0 ** (-5.0 - np.arange(RET_HEADS, dtype=np.float32))).astype(np.float32)
    return jnp.asarray(np.broadcast_to(log_g[:, None, None], (RET_HEADS, 8, 128)).copy())


def _decay_terms(lg_ref):
    C = RET_CHUNK
    lg = lg_ref[0:1, :]
    row = lax.broadcasted_iota(jnp.int32, (C, C), 0)
    col = lax.broadcasted_iota(jnp.int32, (C, C), 1)
    diff = (row - col).astype(F32)
    dmat = jnp.where(diff >= 0, jnp.exp(jnp.maximum(diff, 0.0) * jnp.tile(lg, (1, C // 128))), 0.0)
    j = lax.broadcasted_iota(jnp.int32, (C, 1), 0).astype(F32)
    lg1 = lg[:, 0:1]
    zeta = jnp.exp((C - 1 - j) * lg1)
    xi = jnp.exp((j + 1.0) * lg1)
    g_chunk = jnp.exp(C * lg1)
    return dmat, zeta, xi, g_chunk


def _ret_fwd(rq, rk, rv, rg, gn_w, S):
    C = RET_CHUNK
    N = S // C
    G = min(RET_GROUP_FWD, N)
    NB = N // G

    def body(lg_ref, q_ref, k_ref, v_ref, rg_ref, w_ref, ry_ref, ro_ref, rprev_ref, r_sc):
        @pl.when(pl.program_id(1) == 0)
        def _():
            r_sc[...] = jnp.zeros(r_sc.shape, F32)

        dmat, zeta, xi, g_chunk = _decay_terms(lg_ref)
        w = w_ref[...]
        r = r_sc[...]
        for i in range(G):
            rows = slice(i * C, (i + 1) * C)
            q = q_ref[rows, :]
            k = k_ref[rows, :]
            v = v_ref[rows, :]
            r_prev = r.astype(BF16)
            rprev_ref[i] = r_prev
            sc = _dot_nt(q, k) * dmat
            ry = _dot(sc, v) + jnp.dot(q, r_prev, preferred_element_type=F32) * xi
            ry_ref[rows, :] = ry
            r = g_chunk * r + _dot_tn(k, zeta * v.astype(F32))
            mu = jnp.mean(ry, axis=-1, keepdims=True)
            yc = ry - mu
            yh = yc * lax.rsqrt(jnp.mean(yc * yc, axis=-1, keepdims=True) + EPS)
            g = rg_ref[rows, :]
            ro_ref[rows, :] = (g * _sig(g) * (yh * w)).astype(BF16)
        r_sc[...] = r

    blk = pl.BlockSpec((G * C, 128), lambda h, n: (n, h))
    return pl.pallas_call(
        body, name="ret_fwd", grid=(RET_HEADS, NB),
        in_specs=[pl.BlockSpec((None, 8, 128), lambda h, n: (h, 0, 0)), blk, blk, blk, blk,
                  pl.BlockSpec((1, 128), lambda h, n: (0, h))],
        out_specs=[blk, blk, pl.BlockSpec((G, 128, 128), lambda h, n: (h * NB + n, 0, 0))],
        out_shape=[_sds((S, 512), F32), _sds((S, 512), BF16), _sds((RET_HEADS * N, 128, 128), BF16)],
        scratch_shapes=[pltpu.VMEM((128, 128), F32)],
        compiler_params=_cp(("parallel", "arbitrary")),
    )(_decay_table(), rq, rk, rv, rg, gn_w)


def _outproj(ro, mo, x, w_o, g_post, g_pre, S):
    tm = min(512, S)

    def body(ro_ref, mo_ref, x_ref, wo_ref, g1_ref, g2_ref, mix_ref, h1_ref, hn_ref):
        mix = (jnp.dot(ro_ref[...], wo_ref[0:512, :], preferred_element_type=F32)
               + jnp.dot(mo_ref[...], wo_ref[512:1024, :], preferred_element_type=F32))
        mix_ref[...] = mix.astype(BF16)
        h1 = x_ref[...] + _rms(mix, g1_ref[...])
        h1_ref[...] = h1
        hn_ref[...] = _rms(h1, g2_ref[...]).astype(BF16)

    return pl.pallas_call(
        body, name="outproj", grid=(S // tm,),
        in_specs=[_rows(tm, 512), _rows(tm, 512), _rows(tm, D_MODEL), _full(D_MODEL, D_MODEL), _full(1, D_MODEL),
                  _full(1, D_MODEL)],
        out_specs=[_rows(tm, D_MODEL)] * 3,
        out_shape=[_sds((S, D_MODEL), BF16), _sds((S, D_MODEL), F32), _sds((S, D_MODEL), BF16)],
        compiler_params=_cp(("parallel",)),
    )(ro, mo, x, w_o, g_post, g_pre)


def _ffn_up(hn, w_gate_t, w_up_t, S):
    tm = min(512, S)
    tn = D_FF // 2

    def body(hn_ref, wg_ref, wu_ref, fg_ref, fu_ref, act_ref):
        hn_b = hn_ref[...]
        for seg in range(2):
            sl = slice(seg * tn, (seg + 1) * tn)
            g = _dot_nt(hn_b, wg_ref[sl, :])
            u = _dot_nt(hn_b, wu_ref[sl, :])
            s = _sig(g)
            silu = g * s
            fg_ref[:, sl] = (u * (s + silu * (1.0 - s))).astype(BF16)
            fu_ref[:, sl] = silu.astype(BF16)
            act_ref[:, sl] = (silu * u).astype(BF16)

    return pl.pallas_call(
        body, name="ffn_up", grid=(S // tm,),
        in_specs=[_rows(tm, D_MODEL), _full(D_FF, D_MODEL), _full(D_FF, D_MODEL)],
        out_specs=[_rows(tm, D_FF)] * 3, out_shape=[_sds((S, D_FF), BF16)] * 3,
        compiler_params=_cp(("parallel",)),
    )(hn, w_gate_t, w_up_t)


def _ffn_down(act, w_down, h1, g, S):
    tm = min(512, S)

    def body(act_ref, wd_ref, h1_ref, g_ref, ff_ref, h2_ref):
        ff = jnp.dot(act_ref[...], wd_ref[...], preferred_element_type=F32)
        ff_ref[...] = ff.astype(BF16)
        h2_ref[...] = h1_ref[...] + _rms(ff, g_ref[...])

    return pl.pallas_call(
        body, name="ffn_down", grid=(S // tm,),
        in_specs=[_rows(tm, D_FF), _full(D_FF, D_MODEL), _rows(tm, D_MODEL), _full(1, D_MODEL)],
        out_specs=[_rows(tm, D_MODEL)] * 2, out_shape=[_sds((S, D_MODEL), BF16), _sds((S, D_MODEL), F32)],
        compiler_params=_cp(("parallel",)),
    )(act, w_down, h1, g)


def _ple_loss(p, h2, tgt, w_pp, w_pg, b_pg, g_ple, S):
    tm = min(512, S)

    def body(p_ref, h2_ref, t_ref, wp_ref, wg_ref, b_ref, gp_ref,
             dz_ref, dpe_ref, dh2_ref, h2b_ref, loss_ref, dgp_ref, db_ref):
        @pl.when(pl.program_id(0) == 0)
        def _():
            loss_ref[...] = jnp.zeros(loss_ref.shape, F32)
            dgp_ref[...] = jnp.zeros(dgp_ref.shape, F32)
            db_ref[...] = jnp.zeros(db_ref.shape, F32)

        gp = gp_ref[...]
        pe = _dot(p_ref[...], wp_ref[...])
        r = lax.rsqrt(jnp.mean(pe * pe, axis=-1, keepdims=True) + EPS)
        peh = pe * r
        e = peh * gp
        h2 = h2_ref[...]
        h2b = h2.astype(BF16)
        h2b_ref[...] = h2b
        gt = _sig(jnp.dot(h2b, wg_ref[...], preferred_element_type=F32) + b_ref[...])
        diff = h2 + e * gt - t_ref[...]
        loss_ref[...] += _colsum(diff * diff)
        dh3 = diff * (1.0 / D_MODEL)
        de = dh3 * gt
        dz = dh3 * e * gt * (1.0 - gt)
        db_ref[...] += _colsum(dz)
        dgp_ref[...] += _colsum(de * peh)
        dpeh = de * gp
        dpe = r * (dpeh - peh * jnp.mean(dpeh * peh, axis=-1, keepdims=True))
        dzb = dz.astype(BF16)
        dz_ref[...] = dzb
        dpe_ref[...] = dpe.astype(BF16)
        dh2_ref[...] = dh3 + _dot_nt(dzb, wg_ref[...])

    return pl.pallas_call(
        body, name="ple_loss", grid=(S // tm,),
        in_specs=[_rows(tm, PLE_DIM), _rows(tm, D_MODEL), _rows(tm, D_MODEL), _full(PLE_DIM, D_MODEL),
                  _full(D_MODEL, D_MODEL), _full(1, D_MODEL), _full(1, D_MODEL)],
        out_specs=[_rows(tm, D_MODEL)] * 4 + [_acc(1, D_MODEL)] * 3,
        out_shape=[_sds((S, D_MODEL), BF16), _sds((S, D_MODEL), BF16), _sds((S, D_MODEL), F32), _sds((S, D_MODEL), BF16)]
        + [_sds((1, D_MODEL), F32)] * 3,
        compiler_params=_cp(("arbitrary",)),
    )(p, h2, tgt, w_pp, w_pg, b_pg, g_ple)


def _wgrad(a, b, name, S):
    M = a.shape[1]
    N = b.shape[1]
    ts = min(2048, S)
    nsplit = 2 if M * N >= 2 * 1024 * 1024 else 1
    tn = N // nsplit

    def body(a_ref, b_ref, o_ref):
        @pl.when(pl.program_id(1) == 0)
        def _():
            o_ref[...] = jnp.zeros(o_ref.shape, F32)

        o_ref[...] += _dot_tn(a_ref[...], b_ref[...])

    return pl.pallas_call(
        body, name=name, grid=(nsplit, S // ts),
        in_specs=[pl.BlockSpec((ts, M), lambda j, s: (s, 0)), pl.BlockSpec((ts, tn), lambda j, s: (s, j))],
        out_specs=pl.BlockSpec((M, tn), lambda j, s: (0, j)), out_shape=_sds((M, N), F32),
        compiler_params=_cp(("parallel", "arbitrary")),
    )(a, b)


def _ffn_down_bwd(dh2, ff, g, w_down, dgate_f, dup_f, S):
    tm = min(512, S)
    tn = D_FF // 2

    def body(dh2_ref, ff_ref, g_ref, wd_ref, fg_ref, fu_ref, dff_ref, dgate_ref, dup_ref, dg_ref):
        @pl.when(pl.program_id(0) == 0)
        def _():
            dg_ref[...] = jnp.zeros(dg_ref.shape, F32)

        dff, ga = _rms_bwd(dh2_ref[...], ff_ref[...].astype(F32), g_ref[...])
        dg_ref[...] += _colsum(ga)
        dffb = dff.astype(BF16)
        dff_ref[...] = dffb
        for seg in range(2):
            sl = slice(seg * tn, (seg + 1) * tn)
            dact = _dot_nt(dffb, wd_ref[sl, :])
            dgate_ref[:, sl] = (dact * fg_ref[:, sl].astype(F32)).astype(BF16)
            dup_ref[:, sl] = (dact * fu_ref[:, sl].astype(F32)).astype(BF16)

    return pl.pallas_call(
        body, name="ffn_down_bwd", grid=(S // tm,),
        in_specs=[_rows(tm, D_MODEL), _rows(tm, D_MODEL), _full(1, D_MODEL), _full(D_FF, D_MODEL), _rows(tm, D_FF),
                  _rows(tm, D_FF)],
        out_specs=[_rows(tm, D_MODEL), _rows(tm, D_FF), _rows(tm, D_FF), _acc(1, D_MODEL)],
        out_shape=[_sds((S, D_MODEL), BF16), _sds((S, D_FF), BF16), _sds((S, D_FF), BF16), _sds((1, D_MODEL), F32)],
        compiler_params=_cp(("arbitrary",)),
    )(dh2, ff, g, w_down, dgate_f, dup_f)


def _ffn_up_bwd(dgate, dup, w_gate, w_up, h1, mix, dh2, g_pre, g_post, w_o, S, grads=()):
    tm = min(512, S)
    n = len(grads)
    last = S // tm - 1

    def body(dgate_ref, dup_ref, wg_ref, wu_ref, h1_ref, mix_ref, dh2_ref, g2_ref, g1_ref, wo_ref, *rest):
        g_ins = rest[:n]
        dh1_ref, dmix_ref, dro_ref, dmo_ref, dg2_ref, dg1_ref = rest[n:n + 6]
        g_outs, sems = rest[n + 6:2 * n + 6], rest[2 * n + 6:]

        @pl.when(pl.program_id(0) == 0)
        def _():
            dg2_ref[...] = jnp.zeros(dg2_ref.shape, F32)
            dg1_ref[...] = jnp.zeros(dg1_ref.shape, F32)
            for cp in (_swap_copies(g_ins, g_outs, sems) if n else []):
                cp.start()

        dhn = (jnp.dot(dgate_ref[...], wg_ref[...], preferred_element_type=F32)
               + jnp.dot(dup_ref[...], wu_ref[...], preferred_element_type=F32))
        d1, ga = _rms_bwd(dhn, h1_ref[...], g2_ref[...])
        dg2_ref[...] += _colsum(ga)
        dh1 = dh2_ref[...] + d1
        dh1_ref[...] = dh1
        dmix, gb = _rms_bwd(dh1, mix_ref[...].astype(F32), g1_ref[...])
        dg1_ref[...] += _colsum(gb)
        dmixb = dmix.astype(BF16)
        dmix_ref[...] = dmixb
        dcat = _dot_nt(dmixb, wo_ref[...])
        dro_ref[...] = dcat[:, 0:512].astype(BF16)
        dmo_ref[...] = dcat[:, 512:1024].astype(BF16)

        if n:
            @pl.when(pl.program_id(0) == last)
            def _():
                for cp in _swap_copies(g_ins, g_outs, sems):
                    cp.wait()

    dh1, dmix, dro, dmo, dg2, dg1, *got = pl.pallas_call(
        body, name="ffn_up_bwd", grid=(S // tm,),
        in_specs=[_rows(tm, D_FF), _rows(tm, D_FF), _full(D_FF, D_MODEL), _full(D_FF, D_MODEL), _rows(tm, D_MODEL),
                  _rows(tm, D_MODEL), _rows(tm, D_MODEL), _full(1, D_MODEL), _full(1, D_MODEL), _full(D_MODEL, D_MODEL)]
        + [_ANY] * n,
        out_specs=[_rows(tm, D_MODEL), _rows(tm, D_MODEL), _rows(tm, 512), _rows(tm, 512), _acc(1, D_MODEL),
                   _acc(1, D_MODEL)] + [_ANY] * n,
        out_shape=[_sds((S, D_MODEL), F32), _sds((S, D_MODEL), BF16), _sds((S, 512), BF16), _sds((S, 512), BF16),
                   _sds((1, D_MODEL), F32), _sds((1, D_MODEL), F32)] + _swap_out_shapes(grads),
        scratch_shapes=_swap_sems(n) if n else [],
        compiler_params=_cp(("arbitrary",)),
    )(dgate, dup, w_gate, w_up, h1, mix, dh2, g_pre, g_post, w_o, *grads)
    return dh1, dmix, dro, dmo, dg2, dg1, got


def _attn_delta(o, do, S, grads=()):
    tm = min(512, S)
    n = len(grads)
    last = S // tm - 1

    def body(o_ref, do_ref, *rest):
        g_ins, (dot_ref, d_ref), g_outs, sems = rest[:n], rest[n:n + 2], rest[n + 2:2 * n + 2], rest[2 * n + 2:]
        if n:
            @pl.when(pl.program_id(0) == 0)
            def _():
                for cp in _swap_copies(g_ins, g_outs, sems):
                    cp.start()

        do = do_ref[...].astype(F32)
        prod_t = (o_ref[...].astype(F32) * do).T
        dot_ref[...] = do.T.astype(BF16)
        for h in range(MLA_HEADS):
            d_ref[h // 2, (h % 2):(h % 2) + 1, :] = jnp.sum(prod_t[h * 64:(h + 1) * 64, :], axis=0, keepdims=True)

        if n:
            @pl.when(pl.program_id(0) == last)
            def _():
                for cp in _swap_copies(g_ins, g_outs, sems):
                    cp.wait()

    dot, delta, *got = pl.pallas_call(
        body, name="attn_delta", grid=(S // tm,),
        in_specs=[_rows(tm, 512), _rows(tm, 512)] + [_ANY] * n,
        out_specs=[pl.BlockSpec((512, tm), lambda i: (0, i)), pl.BlockSpec((MLA_HEADS // 2, 2, tm), lambda i: (0, 0, i))]
        + [_ANY] * n,
        out_shape=[_sds((512, S), BF16), _sds((MLA_HEADS // 2, 2, S), F32)] + _swap_out_shapes(grads),
        scratch_shapes=_swap_sems(n) if n else [],
        compiler_params=_cp(("arbitrary",)),
    )(o, do, *grads)
    return dot, delta, got


def _flash_bwd(qp, kp, kt, v, do, dot, lse, delta, S, sums=()):
    tq = min(512, S)
    nq = S // tq
    RB = ATT_ROWS
    NH = BWD_HEADS
    qb_of, kb_of, T = _tri_pairs(nq, k_major=True)
    n = len(sums)
    steps = (MLA_HEADS // NH) * T

    def body(qb_ref, kb_ref, q_ref, k_ref, kt_ref, v_ref, do_ref, dot_ref, lse_ref, dl_ref, *rest):
        g_ins, (dq_ref, dk_ref, dv_ref), g_outs = rest[:n], rest[n:n + 3], rest[n + 3:2 * n + 3]
        dk_sc, dv_sc, s_sc, dp_sc, p_sc, ds_sc = rest[2 * n + 3:2 * n + 9]
        sems = rest[2 * n + 9:]
        t = pl.program_id(1)
        qb = qb_ref[t]
        kb = kb_ref[t]
        lin = pl.program_id(0) * T + t

        if n:
            @pl.when(lin == 0)
            def _():
                for cp in _scatter_copies(g_ins, g_outs, sems):
                    cp.start()

        @pl.when(t == 0)
        def _():
            dq_ref[...] = jnp.zeros(dq_ref.shape, F32)

        @pl.when(qb == kb)
        def _():
            dk_sc[...] = jnp.zeros(dk_sc.shape, F32)
            dv_sc[...] = jnp.zeros(dv_sc.shape, F32)

        lane = lax.broadcasted_iota(jnp.int32, (tq, 64 * NH), 1)

        def step(masked):
            vv = v_ref[...]
            do_all = do_ref[...]
            mine = [(lane >= a * 64) & (lane < (a + 1) * 64) for a in range(NH)]
            for a in range(NH):
                sl = slice(a * 128, (a + 1) * 128)
                s_sc[a] = _dot_nt(k_ref[:, sl], q_ref[:, sl])
                dp_sc[a] = jnp.dot(jnp.where(mine[a], vv, jnp.zeros_like(vv)), dot_ref[...],
                                   preferred_element_type=F32)
            for a in range(NH):
                sl = slice(a * 128, (a + 1) * 128)
                lse = lse_ref[a:a + 1, :]
                dl = dl_ref[a:a + 1, :]
                for r in range(0, tq, RB):
                    sc = s_sc[a, r:r + RB, :]
                    if masked:
                        sc = jnp.where(_causal_keep(r, RB, tq), sc, NEG)
                    p = jnp.exp(sc - lse)
                    p_sc[a, r:r + RB, :] = p.astype(BF16)
                    ds_sc[a, r:r + RB, :] = (p * (dp_sc[a, r:r + RB, :] - dl)).astype(BF16)
                ds = ds_sc[a]
                dv_sc[...] += jnp.dot(p_sc[a], jnp.where(mine[a], do_all, jnp.zeros_like(do_all)),
                                      preferred_element_type=F32)
                dk_sc[:, sl] += jnp.dot(ds, q_ref[:, sl], preferred_element_type=F32)
                dq_ref[qb, sl, :] += jnp.dot(kt_ref[sl, :], ds, preferred_element_type=F32)

        @pl.when(qb > kb)
        def _():
            step(False)

        @pl.when(qb == kb)
        def _():
            step(True)

        @pl.when(qb == nq - 1)
        def _():
            dk_ref[...] = dk_sc[...].astype(BF16)
            dv_ref[...] = dv_sc[...].astype(BF16)

        if n:
            @pl.when(lin == steps - 1)
            def _():
                for cp in _scatter_copies(g_ins, g_outs, sems):
                    cp.wait()

    grid_spec = pltpu.PrefetchScalarGridSpec(
        num_scalar_prefetch=2, grid=(MLA_HEADS // NH, T),
        in_specs=[pl.BlockSpec((tq, 128 * NH), lambda j, t, qb, kb: (qb[t], j)),
                  pl.BlockSpec((tq, 128 * NH), lambda j, t, qb, kb: (kb[t], j)),
                  pl.BlockSpec((128 * NH, tq), lambda j, t, qb, kb: (j, kb[t])),
                  pl.BlockSpec((tq, 64 * NH), lambda j, t, qb, kb: (kb[t], j)),
                  pl.BlockSpec((tq, 64 * NH), lambda j, t, qb, kb: (qb[t], j)),
                  pl.BlockSpec((64 * NH, tq), lambda j, t, qb, kb: (j, qb[t])),
                  pl.BlockSpec((None, NH, tq), lambda j, t, qb, kb: (j, 0, qb[t])),
                  pl.BlockSpec((None, NH, tq), lambda j, t, qb, kb: (j, 0, qb[t]))] + [_ANY] * n,
        out_specs=[pl.BlockSpec((nq, 128 * NH, tq), lambda j, t, qb, kb: (0, j, 0), pipeline_mode=pl.Buffered(1)),
                   pl.BlockSpec((tq, 128 * NH), lambda j, t, qb, kb: (kb[t], j)),
                   pl.BlockSpec((tq, 64 * NH), lambda j, t, qb, kb: (kb[t], j))] + [_ANY] * n,
        scratch_shapes=[pltpu.VMEM((tq, 128 * NH), F32), pltpu.VMEM((tq, 64 * NH), F32), pltpu.VMEM((NH, tq, tq), F32),
                        pltpu.VMEM((NH, tq, tq), F32), pltpu.VMEM((NH, tq, tq), BF16), pltpu.VMEM((NH, tq, tq), BF16)]
        + (_scatter_sems(n) if n else []),
    )
    dq, dk, dv, *parts = pl.pallas_call(
        body, name="flash_bwd", grid_spec=grid_spec,
        out_shape=[_sds((nq, 1024, tq), F32), _sds((S, 1024), BF16), _sds((S, 512), BF16)] + _scatter_out_shapes(sums),
        compiler_params=_cp(("arbitrary", "arbitrary")),
    )(qb_of, kb_of, qp, kp, kt, v, do, dot, lse.reshape(MLA_HEADS // NH, NH, S), delta.reshape(MLA_HEADS // NH, NH, S),
      *sums)
    return dq, dk, dv, parts


def _mla_up_bwd(dqp, dkp, dv, cq, ckv, gq, gkv, w_uq, w_ukv, tabs, S):
    tm = min(512, S)

    def body(dq_ref, dk_ref, dv_ref, cq_ref, ckv_ref, gq_ref, gkv_ref, wuq_ref, wukv_ref, cm_ref, sa_ref, sb_ref,
             dqh_ref, dkv_ref, dcq_ref, dckv_ref, dkr_ref, dgq_ref, dgkv_ref):
        @pl.when(pl.program_id(0) == 0)
        def _():
            dgq_ref[...] = jnp.zeros(dgq_ref.shape, F32)
            dgkv_ref[...] = jnp.zeros(dgkv_ref.shape, F32)

        cm = cm_ref[...]
        sa = sa_ref[...]
        sb = sb_ref[...]
        lane = lax.broadcasted_iota(jnp.int32, (tm, 128), 1)
        dkr_r = jnp.zeros((tm, 128), F32)
        for h in range(MLA_HEADS):
            sl = slice(h * 128, (h + 1) * 128)
            dqh_ref[:, sl] = (_unrope_mla(dq_ref[sl, :].T, cm, sa, sb) * SCALE_MLA).astype(BF16)
            gk = dk_ref[:, sl]
            dkr_r = dkr_r + gk.astype(F32)
            dkv_ref[:, sl] = gk
        dkr_r = jnp.where((lane >= 64) & (lane < 96), dkr_r, 0.0)
        dkr_ref[...] = _unrope_mla(dkr_r, cm, sa, sb).astype(BF16)
        dkv_ref[:, 1024:1536] = dv_ref[...]
        dcq, ga = _rms_bwd(_dot_nt(dqh_ref[...], wuq_ref[...]), cq_ref[...], gq_ref[...])
        dcq_ref[...] = dcq.astype(BF16)
        dgq_ref[...] += _colsum(ga)
        dckv, gb = _rms_bwd(_dot_nt(dkv_ref[...], wukv_ref[...]), ckv_ref[...], gkv_ref[...])
        dckv_ref[...] = dckv.astype(BF16)
        dgkv_ref[...] += _colsum(gb)

    per_q = dqp.shape[2] // tm
    return pl.pallas_call(
        body, name="mla_up_bwd", grid=(S // tm,),
        in_specs=[pl.BlockSpec((None, 1024, tm), lambda i: (i // per_q, 0, i % per_q)),
                  _rows(tm, 1024), _rows(tm, 512), _rows(tm, Q_LORA), _rows(tm, KV_LORA),
                  _full(1, Q_LORA), _full(1, KV_LORA), _full(Q_LORA, 1024), _full(KV_LORA, 1536)] + [_rows(tm, 128)] * 3,
        out_specs=[_rows(tm, 1024), _rows(tm, 1536), _rows(tm, Q_LORA), _rows(tm, KV_LORA), _rows(tm, 128),
                   _acc(1, Q_LORA), _acc(1, KV_LORA)],
        out_shape=[_sds((S, 1024), BF16), _sds((S, 1536), BF16), _sds((S, Q_LORA), BF16), _sds((S, KV_LORA), BF16),
                   _sds((S, 128), BF16), _sds((1, Q_LORA), F32), _sds((1, KV_LORA), F32)],
        compiler_params=_cp(("arbitrary",)),
    )(dqp, dkp, dv, cq, ckv, gq, gkv, w_uq, w_ukv, *tabs[2:])


def _ret_bwd(rq, rk, rv, rprev, ry, rg, dro, gn_w, tabs, S):
    C = RET_CHUNK
    N = S // C
    G = min(RET_GROUP_BWD, N)
    NB = N // G

    def body(lg_ref, q_ref, k_ref, v_ref, rp_ref, ry_ref, rg_ref, dro_ref, w_ref, cr_ref, sr_ref,
             drq_ref, drk_ref, drv_ref, drg_ref, dw_ref, g_sc):
        @pl.when(pl.program_id(1) == 0)
        def _():
            g_sc[...] = jnp.zeros(g_sc.shape, F32)
            dw_ref[...] = jnp.zeros(dw_ref.shape, F32)

        dmat, zeta, xi, g_chunk = _decay_terms(lg_ref)
        w = w_ref[...]
        gacc = g_sc[...]
        dw = jnp.zeros((1, 128), F32)
        for i in reversed(range(G)):
            rows = slice(i * C, (i + 1) * C)
            ry = ry_ref[rows, :]
            mu = jnp.mean(ry, axis=-1, keepdims=True)
            yc = ry - mu
            rstd = lax.rsqrt(jnp.mean(yc * yc, axis=-1, keepdims=True) + EPS)
            yh = yc * rstd
            g = rg_ref[rows, :]
            s = _sig(g)
            dout = dro_ref[rows, :].astype(F32)
            drg_ref[rows, :] = (dout * (yh * w) * (s * (1.0 + g * (1.0 - s)))).astype(BF16)
            dgn = dout * (g * s)
            dw = dw + _colsum(dgn * yh)
            dyh = dgn * w
            dry = rstd * (dyh - jnp.mean(dyh, axis=-1, keepdims=True) - yh * jnp.mean(dyh * yh, axis=-1, keepdims=True))
            do = dry.astype(BF16)

            q = q_ref[rows, :]
            k = k_ref[rows, :]
            v = v_ref[rows, :]
            gfut = gacc.astype(BF16)
            sc = (_dot_nt(q, k) * dmat).astype(BF16)
            dsc = (_dot_nt(do, v) * dmat).astype(BF16)
            dq = jnp.dot(dsc, k, preferred_element_type=F32) + _dot_nt(do, rp_ref[i]) * xi
            dk = _dot_tn(dsc, q) + _dot_nt(v, gfut) * zeta
            dv = _dot_tn(sc, do) + jnp.dot(k, gfut, preferred_element_type=F32) * zeta
            gacc = g_chunk * gacc + _dot_tn(q, xi * dry)
            cr = cr_ref[rows, :]
            sr = sr_ref[rows, :]
            drq_ref[rows, :] = _unrope_ret(dq, cr, sr).astype(BF16)
            drk_ref[rows, :] = _unrope_ret(dk * SCALE_RET, cr, sr).astype(BF16)
            drv_ref[rows, :] = dv.astype(BF16)
        g_sc[...] = gacc
        dw_ref[...] += dw

    blk = pl.BlockSpec((G * C, 128), lambda h, n: (NB - 1 - n, h))
    tab = pl.BlockSpec((G * C, 128), lambda h, n: (NB - 1 - n, 0))
    return pl.pallas_call(
        body, name="ret_bwd", grid=(RET_HEADS, NB),
        in_specs=[pl.BlockSpec((None, 8, 128), lambda h, n: (h, 0, 0)), blk, blk, blk,
                  pl.BlockSpec((G, 128, 128), lambda h, n: (h * NB + NB - 1 - n, 0, 0)), blk, blk, blk,
                  pl.BlockSpec((1, 128), lambda h, n: (0, h)), tab, tab],
        out_specs=[blk, blk, blk, blk, pl.BlockSpec((1, 128), lambda h, n: (0, h))],
        out_shape=[_sds((S, 512), BF16)] * 4 + [_sds((1, 512), F32)],
        scratch_shapes=[pltpu.VMEM((128, 128), F32)],
        compiler_params=_cp(("parallel", "arbitrary")),
    )(_decay_table(), rq, rk, rv, rprev, ry, rg, dro, gn_w, tabs[0], tabs[1])


def _inproj_bwd(drq, drk, drv, drg, dcq, dckv, dkr, w_in, dh1, x, g, S):
    tm = min(512, S)

    def body(drq_ref, drk_ref, drv_ref, drg_ref, dcq_ref, dckv_ref, dkr_ref, w_ref, dh1_ref, x_ref, g_ref,
             gx_ref, dproj_ref, dg_ref):
        @pl.when(pl.program_id(0) == 0)
        def _():
            dg_ref[...] = jnp.zeros(dg_ref.shape, F32)

        dproj_ref[:, 0:512] = drq_ref[...]
        dproj_ref[:, 512:1024] = drk_ref[...]
        dproj_ref[:, 1024:1536] = drv_ref[...]
        dproj_ref[:, 1536:2048] = drg_ref[...]
        dproj_ref[:, 2048:2432] = dcq_ref[...]
        dproj_ref[:, 2432:2688] = dckv_ref[...]
        dproj_ref[:, 2688:2816] = pltpu.roll(dkr_ref[...].astype(F32), 64, 1).astype(BF16)
        dx, ga = _rms_bwd(_dot_nt(dproj_ref[...], w_ref[...]), x_ref[...], g_ref[...])
        gx_ref[...] = dh1_ref[...] + dx
        dg_ref[...] += _colsum(ga)

    return pl.pallas_call(
        body, name="inproj_bwd", grid=(S // tm,),
        in_specs=[_rows(tm, 512)] * 4 + [_rows(tm, Q_LORA), _rows(tm, KV_LORA), _rows(tm, 128),
                                         _full(D_MODEL, IN_COLS_P), _rows(tm, D_MODEL), _rows(tm, D_MODEL),
                                         _full(1, D_MODEL)],
        out_specs=[_rows(tm, D_MODEL), _rows(tm, IN_COLS_P), _acc(1, D_MODEL)],
        out_shape=[_sds((S, D_MODEL), F32), _sds((S, IN_COLS_P), BF16), _sds((1, D_MODEL), F32)],
        compiler_params=_cp(("arbitrary",)),
    )(drq, drk, drv, drg, dcq, dckv, dkr, w_in, dh1, x, g)


def _pad_weights(w):
    w_in_p = jnp.pad(w["w_in"], ((0, 0), (0, IN_COLS_P - IN_COLS)))
    w_uq_p = jnp.pad(w["w_uq"].reshape(Q_LORA, MLA_HEADS, 96), ((0, 0), (0, 0), (0, 32))).reshape(Q_LORA, 1024)
    ukv = w["w_ukv"].reshape(KV_LORA, MLA_HEADS, 128)
    k_part = jnp.pad(ukv[:, :, :64], ((0, 0), (0, 0), (0, 64))).reshape(KV_LORA, 1024)
    w_ukv_p = jnp.concatenate([k_part, ukv[:, :, 64:].reshape(KV_LORA, 512)], axis=1)
    return w_in_p, w_uq_p, w_ukv_p


BIG_SPEC = {n: (r, c, ax) for n, r, c, ax in BIG}
COLUMN_MAJOR = ("w_in", "w_uq", "w_gate", "w_up")
GRAD_TRANSPOSED = ("w_gate", "w_up")
GATHER_FIRST = ("w_in", "w_uq", "w_ukv")
GATHER_LATE = tuple(n for n, _, _, _ in BIG if n not in GATHER_FIRST)
REDUCE_EARLY = ("w_ple_gate", "w_ple_proj", "w_down", "w_gate", "w_up", "w_o")
REDUCE_LAST = tuple(n for n, _, _, _ in BIG if n not in REDUCE_EARLY)


def _local_step(x, p, pos_f, tgt, w, sm, late_shards=None, c_idx=None):
    S = x.shape[0]
    spread = late_shards is not None
    w = dict(w)
    tabs, first = _rope_tables(pos_f, S, [late_shards[n] for n in GATHER_FIRST] if spread else ())
    for i, n in enumerate(GATHER_FIRST if spread else ()):
        w[n] = _from_chips(first[i], BIG_SPEC[n][2])
    w_in_p, w_uq_p, w_ukv_p = _pad_weights(w)
    if spread:
        w_in_p = jnp.concatenate([first[0][j] for j in range(N_CHIPS)]
                                 + [jnp.zeros((D_MODEL, IN_COLS_P - IN_COLS), BF16)], axis=1)

    xn, rq, rk, rv, rg, cq, ckv, kr = _inproj(x, sm["pre_mix_norm"], w_in_p, tabs, S)
    cqn, ckvn, qp, kp, v, kt, vt = _mla_up(cq, ckv, kr, sm["mla_q_norm"], sm["mla_kv_norm"], w_uq_p, w_ukv_p, tabs, S)
    mo, lse, gathered = _flash_fwd(qp, kp, vt, S, [late_shards[n] for n in GATHER_LATE] if spread else ())
    for i, n in enumerate(GATHER_LATE if spread else ()):
        w[n] = _from_chips(gathered[i], 0 if n in GRAD_TRANSPOSED else BIG_SPEC[n][2])
    if not spread:
        w.update({n: w[n].T for n in GRAD_TRANSPOSED})
    ry, ro, rprev = _ret_fwd(rq, rk, rv, rg, sm["ret_gn_w"], S)
    mix, h1, hn = _outproj(ro, mo, x, w["w_o"], sm["post_mix_norm"], sm["pre_ffn_norm"], S)
    dgate_f, dup_f, act = _ffn_up(hn, w["w_gate"], w["w_up"], S)
    ff, h2 = _ffn_down(act, w["w_down"], h1, sm["post_ffn_norm"], S)
    dz, dpe, dh2, h2b, loss_vec, d_ple_norm, d_b = _ple_loss(
        p, h2, tgt, w["w_ple_proj"], w["w_ple_gate"], sm["b_ple_gate"], sm["ple_norm"], S)

    gw = {}
    gs = {"ple_norm": d_ple_norm, "b_ple_gate": d_b}
    gw["w_ple_gate"] = _wgrad(h2b, dz, "wgrad_ple_gate", S)
    gw["w_ple_proj"] = _wgrad(p, dpe, "wgrad_ple_proj", S)
    dff, dgate, dup, gs["post_ffn_norm"] = _ffn_down_bwd(dh2, ff, sm["post_ffn_norm"], w["w_down"], dgate_f, dup_f, S)
    gw["w_down"] = _wgrad(act, dff, "wgrad_down", S)
    if spread:
        gw["w_gate"] = _wgrad(dgate, hn, "wgrad_gate", S)
        gw["w_up"] = _wgrad(dup, hn, "wgrad_up", S)
    else:
        gw["w_gate"] = _wgrad(hn, dgate, "wgrad_gate", S)
        gw["w_up"] = _wgrad(hn, dup, "wgrad_up", S)
    first = REDUCE_EARLY[:-1]
    g4 = [_by_chip(gw.pop(n), *((D_FF, D_MODEL, 0) if n in GRAD_TRANSPOSED else BIG_SPEC[n]))
          for n in first] if spread else []
    dh1, dmix, dro, dmo, gs["pre_ffn_norm"], gs["post_mix_norm"], got = _ffn_up_bwd(
        dgate, dup, w["w_gate"], w["w_up"], h1, mix, dh2, sm["pre_ffn_norm"], sm["post_mix_norm"], w["w_o"], S, g4)
    gw["w_o"] = jnp.concatenate([_wgrad(ro, dmix, "wgrad_o_ret", S), _wgrad(mo, dmix, "wgrad_o_mla", S)], axis=0)
    g4_o = [_by_chip(gw.pop("w_o"), *BIG_SPEC["w_o"])] if spread else []

    dmo_t, delta, got_o = _attn_delta(mo, dmo, S, g4_o)
    sums = [_add_half_rows(a, b, c_idx, "rs_add_halves_" + n)
            for n, a, b in zip(REDUCE_EARLY, g4 + g4_o, list(got) + list(got_o))] if spread else []
    dqp, dkp, dv, parts = _flash_bwd(qp, kp, kt, v, dmo, dmo_t, lse, delta, S, sums)
    dqh, dkv, dcq, dckv, dkr, gs["mla_q_norm"], gs["mla_kv_norm"] = _mla_up_bwd(
        dqp, dkp, dv, cq, ckv, sm["mla_q_norm"], sm["mla_kv_norm"], w_uq_p, w_ukv_p, tabs, S)
    g_uq_p = _wgrad(cqn, dqh, "wgrad_uq", S)
    g_ukv_p = _wgrad(ckvn, dkv, "wgrad_ukv", S)
    gw["w_uq"] = g_uq_p.reshape(Q_LORA, MLA_HEADS, 128)[:, :, :96].reshape(Q_LORA, 768)
    gw["w_ukv"] = jnp.concatenate(
        [g_ukv_p[:, :1024].reshape(KV_LORA, MLA_HEADS, 128)[:, :, :64], g_ukv_p[:, 1024:].reshape(KV_LORA, MLA_HEADS, 64)],
        axis=2).reshape(KV_LORA, 1024)

    drq, drk, drv, drg, gs["ret_gn_w"] = _ret_bwd(rq, rk, rv, rprev, ry, rg, dro, sm["ret_gn_w"], tabs, S)
    grad_x, dproj, gs["pre_mix_norm"] = _inproj_bwd(drq, drk, drv, drg, dcq, dckv, dkr, w_in_p, dh1, x,
                                                    sm["pre_mix_norm"], S)
    if spread:
        gw["w_in"] = _wgrad(dproj, xn, "wgrad_in", S)[:IN_COLS]
    else:
        gw["w_in"] = _wgrad(xn, dproj, "wgrad_in", S)[:, :IN_COLS]
    return loss_vec, grad_x, gw, gs, ((sums, parts) if spread else None)


def _my_place():
    x = lax.axis_index("x")
    y = lax.axis_index("y")
    c = lax.axis_index("c")
    return x, y, c


def _other_chips(x, y):
    return [(1 - x, y), (x, 1 - y), (1 - x, 1 - y)]


_ANY = pl.BlockSpec(memory_space=pl.ANY)


def _small_copies(v_ref, slots, sems):
    send, recv, lsem = sems
    x, y, c = _my_place()
    me = 4 * x + 2 * y + c
    cps = [pltpu.make_async_copy(v_ref, slots.at[me], lsem)]
    for r in range(1, N_DEV):
        peer = (x ^ (r >> 2), y ^ ((r >> 1) & 1), c ^ (r & 1))
        cps.append(pltpu.make_async_remote_copy(
            src_ref=v_ref, dst_ref=slots.at[me], send_sem=send.at[r - 1], recv_sem=recv.at[r - 1],
            device_id=peer, device_id_type=MESH))
    return cps


def _small_sum(slots, out_ref):
    acc = slots[0]
    for d in range(1, N_DEV):
        acc = acc + slots[d]
    out_ref[...] = acc
    loss = jnp.sum(acc[9:10, :], axis=1, keepdims=True) * (0.5 / D_MODEL)
    out_ref[9:10, :] = jnp.broadcast_to(loss, (1, PACK_COLS))


def _small_scratch():
    return [pltpu.VMEM((N_DEV, SMALL_ROWS, PACK_COLS), F32), pltpu.SemaphoreType.DMA((N_DEV - 1,)),
            pltpu.SemaphoreType.DMA((N_DEV - 1,)), pltpu.SemaphoreType.DMA]


N_BIG = len(BIG)


def _half(c, rows, align):
    h = rows // 2
    return pl.ds(pl.multiple_of(c * h, align), h)


def _gather_out_shapes(shards):
    return [_sds((N_CHIPS,) + tuple(s.shape), BF16) for s in shards]


def _gather_sems(n):
    return [pltpu.SemaphoreType.DMA((n, 3))] * 4 + [pltpu.SemaphoreType.DMA((n,))] * 2


def _gather_phase(phase, ins, outs, sems):
    send1, recv1, send2, recv2, send3, recv3 = sems
    x, y, c = _my_place()
    me = 2 * x + y
    chips = _other_chips(x, y)
    sib = (x, y, 1 - c)
    for t in range(len(ins)):
        rows = ins[t].shape[0]
        half = _half(c, rows, 16)
        other = _half(1 - c, rows, 16)
        def own():
            return pltpu.make_async_remote_copy(
                src_ref=ins[t], dst_ref=outs[t].at[me], send_sem=send3.at[t], recv_sem=recv3.at[t],
                device_id=sib, device_id_type=MESH)

        if phase == 0:
            own().start()
        if phase == 2:
            own().wait()
        for k, (cx, cy) in enumerate(chips):
            src = 2 * cx + cy

            def over_ici(slab):
                return pltpu.make_async_remote_copy(
                    src_ref=ins[t].at[half], dst_ref=outs[t].at[slab, half], send_sem=send1.at[t, k],
                    recv_sem=recv1.at[t, k], device_id=(cx, cy, c), device_id_type=MESH)

            def over_d2d(rows):
                return pltpu.make_async_remote_copy(
                    src_ref=outs[t].at[src, rows], dst_ref=outs[t].at[src, rows], send_sem=send2.at[t, k],
                    recv_sem=recv2.at[t, k], device_id=sib, device_id_type=MESH)

            if phase == 0:
                over_ici(me).start()
            if phase == 1:
                over_ici(src).wait_recv()
                over_d2d(half).start()
            if phase == 2:
                over_d2d(other).wait_recv()
                over_ici(me).wait_send()
                over_d2d(half).wait_send()


def _swap_copies(ins, outs, sems):
    send, recv = sems
    x, y, c = _my_place()
    return [pltpu.make_async_remote_copy(
        src_ref=ins[t].at[:, _half(1 - c, ins[t].shape[1], 8)], dst_ref=outs[t], send_sem=send.at[t],
        recv_sem=recv.at[t], device_id=(x, y, 1 - c), device_id_type=MESH) for t in range(len(ins))]


def _swap_out_shapes(gs):
    return [_sds((N_CHIPS, g.shape[1] // 2, g.shape[2]), F32) for g in gs]


def _swap_sems(n):
    return [pltpu.SemaphoreType.DMA((n,)), pltpu.SemaphoreType.DMA((n,))]


def _swap_half_rows(gs):
    n = len(gs)

    def body(*refs):
        cps = _swap_copies(refs[:n], refs[n:2 * n], refs[2 * n:])
        for cp in cps:
            cp.start()
        for cp in cps:
            cp.wait()

    return pl.pallas_call(
        body, name="rs_swap_halves",
        in_specs=[_ANY] * n, out_specs=[_ANY] * n, out_shape=_swap_out_shapes(gs), scratch_shapes=_swap_sems(n),
    )(*gs)


def _add_half_rows(g, got, c_idx, name):
    _, rows, cols = g.shape
    h = rows // 2

    def body(c_ref, a_ref, b_ref, o_ref):
        o_ref[...] = (a_ref[...] + b_ref[...]).astype(BF16)

    grid_spec = pltpu.PrefetchScalarGridSpec(
        num_scalar_prefetch=1, grid=(N_CHIPS,),
        in_specs=[pl.BlockSpec((None, h, cols), lambda j, c: (j, c[0], 0)),
                  pl.BlockSpec((None, h, cols), lambda j, c: (j, 0, 0))],
        out_specs=pl.BlockSpec((None, h, cols), lambda j, c: (j, 0, 0)),
    )
    return pl.pallas_call(
        body, name=name, grid_spec=grid_spec, out_shape=_sds((N_CHIPS, h, cols), BF16),
        compiler_params=_cp(("parallel",)),
    )(c_idx, g, got)


def _scatter_to_chips(ts, vec):
    n = len(ts)

    def body(*refs):
        ins, v_ref, outs, small_ref = refs[:n], refs[n], refs[n + 1:2 * n + 1], refs[2 * n + 1]
        slots, small_sems, sems = refs[2 * n + 2], refs[2 * n + 3:2 * n + 6], refs[2 * n + 6:]
        small = _small_copies(v_ref, slots, small_sems)
        cps = _scatter_copies(ins, outs, sems)
        for cp in small + cps:
            cp.start()
        for cp in small:
            cp.wait()
        _small_sum(slots, small_ref)
        for cp in cps:
            cp.wait()

    vm = pl.BlockSpec(memory_space=pltpu.VMEM)
    *parts, small_sum = pl.pallas_call(
        body, name="rs_scatter_chips",
        in_specs=[_ANY] * n + [vm], out_specs=[_ANY] * n + [vm],
        out_shape=_scatter_out_shapes(ts) + [_sds((SMALL_ROWS, PACK_COLS), F32)],
        scratch_shapes=_small_scratch() + _scatter_sems(n),
    )(*ts, vec)
    return parts, small_sum


def _scatter_copies(ins, outs, sems):
    send, recv = sems
    x, y, c = _my_place()
    return [pltpu.make_async_remote_copy(
        src_ref=ins[t].at[2 * cx + cy], dst_ref=outs[t].at[k], send_sem=send.at[t, k], recv_sem=recv.at[t, k],
        device_id=(cx, cy, c), device_id_type=MESH)
        for t in range(len(ins)) for k, (cx, cy) in enumerate(_other_chips(x, y))]


def _scatter_out_shapes(ts):
    return [_sds((3,) + tuple(t.shape[1:]), BF16) for t in ts]


def _scatter_sems(n):
    return [pltpu.SemaphoreType.DMA((n, 3)), pltpu.SemaphoreType.DMA((n, 3))]


def _add_four(mine, parts, place, name):
    _, h, cols = parts.shape

    def body(pl_ref, m_ref, p_ref, o_ref):
        o_ref[...] = ((m_ref[...].astype(F32) + p_ref[0].astype(F32)) + p_ref[1].astype(F32)) + p_ref[2].astype(F32)

    grid_spec = pltpu.PrefetchScalarGridSpec(
        num_scalar_prefetch=1, grid=(1,),
        in_specs=[pl.BlockSpec((None, h, cols), lambda i, pc: (pc[0], 0, 0)),
                  pl.BlockSpec((3, h, cols), lambda i, pc: (0, 0, 0))],
        out_specs=pl.BlockSpec((h, cols), lambda i, pc: (pc[1], 0)),
    )
    return pl.pallas_call(
        body, name=name, grid_spec=grid_spec, out_shape=_sds((2 * h, cols), F32),
        compiler_params=_cp(("arbitrary",)),
    )(place, mine, parts)


def _join_half_rows(rs):
    n = len(rs)

    def body(*refs):
        ins, outs = refs[:n], refs[n:2 * n]
        send, recv = refs[2 * n:]
        x, y, c = _my_place()
        cps = []
        for t in range(n):
            half = _half(c, outs[t].shape[0], 8)
            rc = pltpu.make_async_remote_copy(
                src_ref=ins[t].at[half], dst_ref=outs[t].at[half], send_sem=send.at[t], recv_sem=recv.at[t],
                device_id=(x, y, 1 - c), device_id_type=MESH)
            rc.start()
            cps.append(rc)
        for cp in cps:
            cp.wait()

    return pl.pallas_call(
        body, name="rs_join_halves",
        in_specs=[_ANY] * n, out_specs=[_ANY] * n,
        out_shape=[_sds(r.shape, F32) for r in rs],
        input_output_aliases={i: i for i in range(n)},
        scratch_shapes=[pltpu.SemaphoreType.DMA((n,))] * 2,
    )(*rs)


def _by_chip(full, rows, cols, axis):
    if axis == 0:
        return full.reshape(N_CHIPS, rows // N_CHIPS, cols)
    return full.reshape(rows, N_CHIPS, cols // N_CHIPS).transpose(1, 0, 2)


def _from_chips(parts, axis):
    _, r, c = parts.shape
    if axis == 0:
        return parts.reshape(N_CHIPS * r, c)
    return parts.transpose(1, 0, 2).reshape(r, N_CHIPS * c)


def _adamw(wt, g, m, v, name):
    _, R, C = wt.shape
    tr = max(d for d in range(8, R + 1, 8) if R % d == 0 and (d * C <= 256 * 1024 or d == 8))

    def body(w_ref, g_ref, m_ref, v_ref, d_ref, nm_ref, nv_ref):
        gg = g_ref[...]
        m_new = ADAM_B1 * m_ref[...] + (1.0 - ADAM_B1) * gg
        v_new = ADAM_B2 * v_ref[...] + (1.0 - ADAM_B2) * (gg * gg)
        m_hat = m_new / (1.0 - ADAM_B1 ** ADAM_STEP)
        v_hat = v_new / (1.0 - ADAM_B2 ** ADAM_STEP)
        d_ref[...] = -ADAM_LR * (m_hat / (jnp.sqrt(v_hat) + ADAM_EPS) + ADAM_WD * w_ref[...])
        nm_ref[...] = m_new
        nv_ref[...] = v_new

    spec = pl.BlockSpec((None, tr, C), lambda i: (0, i, 0))
    return pl.pallas_call(
        body, name=name, grid=(R // tr,), in_specs=[spec, pl.BlockSpec((tr, C), lambda i: (i, 0)), spec, spec],
        out_specs=[spec] * 3, out_shape=[_sds((1, R, C), F32)] * 3,
        compiler_params=_cp(("parallel",)),
    )(wt, g, m, v)


def _pack_small(vals, loss_vec=None):
    rows = [jnp.pad(vals[n].reshape(-1), (0, PACK_COLS - sz)) for n, sz in SMALL]
    rows.append(loss_vec.reshape(-1) if loss_vec is not None else jnp.zeros((PACK_COLS,), F32))
    rows += [jnp.zeros((PACK_COLS,), F32)] * (SMALL_ROWS - len(rows))
    return jnp.stack(rows)


def kernel(x, p, positions, pre_mix_norm, w_in, ret_gn_w, mla_q_norm, w_uq, mla_kv_norm, w_ukv, w_o, post_mix_norm, pre_ffn_norm, w_gate, w_up, w_down, post_ffn_norm, w_ple_proj, ple_norm, w_ple_gate, b_ple_gate, loss_target, m_pre_mix_norm, m_w_in, m_ret_gn_w, m_mla_q_norm, m_w_uq, m_mla_kv_norm, m_w_ukv, m_w_o, m_post_mix_norm, m_pre_ffn_norm, m_w_gate, m_w_up, m_w_down, m_post_ffn_norm, m_w_ple_proj, m_ple_norm, m_w_ple_gate, m_b_ple_gate, v_pre_mix_norm, v_w_in, v_ret_gn_w, v_mla_q_norm, v_w_uq, v_mla_kv_norm, v_w_ukv, v_w_o, v_post_mix_norm, v_pre_ffn_norm, v_w_gate, v_w_up, v_w_down, v_post_ffn_norm, v_w_ple_proj, v_ple_norm, v_w_ple_gate, v_b_ple_gate):
    wts = dict(pre_mix_norm=pre_mix_norm, w_in=w_in, ret_gn_w=ret_gn_w, mla_q_norm=mla_q_norm, w_uq=w_uq,
               mla_kv_norm=mla_kv_norm, w_ukv=w_ukv, w_o=w_o, post_mix_norm=post_mix_norm, pre_ffn_norm=pre_ffn_norm,
               w_gate=w_gate, w_up=w_up, w_down=w_down, post_ffn_norm=post_ffn_norm, w_ple_proj=w_ple_proj,
               ple_norm=ple_norm, w_ple_gate=w_ple_gate, b_ple_gate=b_ple_gate)
    mom = dict(pre_mix_norm=m_pre_mix_norm, w_in=m_w_in, ret_gn_w=m_ret_gn_w, mla_q_norm=m_mla_q_norm, w_uq=m_w_uq,
               mla_kv_norm=m_mla_kv_norm, w_ukv=m_w_ukv, w_o=m_w_o, post_mix_norm=m_post_mix_norm,
               pre_ffn_norm=m_pre_ffn_norm, w_gate=m_w_gate, w_up=m_w_up, w_down=m_w_down, post_ffn_norm=m_post_ffn_norm,
               w_ple_proj=m_w_ple_proj, ple_norm=m_ple_norm, w_ple_gate=m_w_ple_gate, b_ple_gate=m_b_ple_gate)
    var = dict(pre_mix_norm=v_pre_mix_norm, w_in=v_w_in, ret_gn_w=v_ret_gn_w, mla_q_norm=v_mla_q_norm, w_uq=v_w_uq,
               mla_kv_norm=v_mla_kv_norm, w_ukv=v_w_ukv, w_o=v_w_o, post_mix_norm=v_post_mix_norm,
               pre_ffn_norm=v_pre_ffn_norm, w_gate=v_w_gate, w_up=v_w_up, w_down=v_w_down, post_ffn_norm=v_post_ffn_norm,
               w_ple_proj=v_w_ple_proj, ple_norm=v_ple_norm, w_ple_gate=v_w_ple_gate, b_ple_gate=v_b_ple_gate)

    S = x.shape[1]
    shard2d = {n: wts[n][0] for n, _, _, _ in BIG}
    small2d = {n: wts[n] for n, _ in SMALL}

    shard_bf = {n: (jnp.swapaxes(wts[n], 1, 2)[0] if n in GRAD_TRANSPOSED else shard2d[n]).astype(BF16) for n in shard2d}
    pos_f = positions.astype(F32).reshape(S, 1)
    c_idx = lax.axis_index("c").astype(jnp.int32).reshape(1)
    loss_vec, grad_x, gw, gs, (sums_early, parts_early) = _local_step(
        x[0], p[0, 0], pos_f, loss_target[0], {}, small2d, shard_bf, c_idx)

    g4 = [_by_chip(gw[n], *BIG_SPEC[n]) for n in REDUCE_LAST if n != "w_in"]
    g4.insert(REDUCE_LAST.index("w_in"), jnp.pad(gw["w_in"].reshape(N_CHIPS, IN_SHARD, D_MODEL),
                                                 ((0, 0), (0, IN_SHARD_P - IN_SHARD), (0, 0))))
    got = _swap_half_rows(g4)
    sums_last = [_add_half_rows(g4[i], got[i], c_idx, "rs_add_halves_" + n) for i, n in enumerate(REDUCE_LAST)]
    parts_last, small_sum = _scatter_to_chips(sums_last, _pack_small(gs, loss_vec))
    place = jnp.stack([2 * lax.axis_index("x") + lax.axis_index("y"), lax.axis_index("c")]).astype(jnp.int32)
    names = REDUCE_EARLY + REDUCE_LAST
    reduced = _join_half_rows(
        [_add_four(sm_, pt_, place, "rs_add_chips_" + n)
         for n, sm_, pt_ in zip(names, sums_early + sums_last, list(parts_early) + list(parts_last))])
    g_shard = dict(zip(names, reduced))

    loss = small_sum[9, 0]
    g_small = {n: small_sum[i:i + 1, :sz] for i, (n, sz) in enumerate(SMALL)}

    grads, delta, new_m, new_v = {}, {}, {}, {}
    for n, _, _, _ in BIG:
        if n in COLUMN_MAJOR:
            turn = lambda a: jnp.swapaxes(a, 1, 2)
            g_t = g_shard[n][:IN_SHARD] if n == "w_in" else g_shard[n] if n in GRAD_TRANSPOSED else g_shard[n].T
            d, nm, nv = _adamw(turn(wts[n]), g_t, turn(mom[n]), turn(var[n]), "adamw_" + n)
            grads[n], delta[n], new_m[n], new_v[n] = turn(g_t[None]), turn(d), turn(nm), turn(nv)
        else:
            delta[n], new_m[n], new_v[n] = _adamw(wts[n], g_shard[n], mom[n], var[n], "adamw_" + n)
            grads[n] = g_shard[n][None]
    d, nm, nv = _adamw(_pack_small(small2d)[None], small_sum, _pack_small(mom)[None], _pack_small(var)[None],
                       "adamw_small")
    for i, (n, sz) in enumerate(SMALL):
        grads[n] = g_small[n]
        delta[n], new_m[n], new_v[n] = d[0, i:i + 1, :sz], nm[0, i:i + 1, :sz], nv[0, i:i + 1, :sz]

    return (loss, grad_x[None], *[grads[n] for n in ALL_W], *[delta[n] for n in ALL_W],
            *[new_m[n] for n in ALL_W], *[new_v[n] for n in ALL_W])
```

```python
import functools
import math

import jax
import jax.numpy as jnp
import numpy as np
from jax import lax
from jax.experimental import pallas as pl
from jax.experimental.pallas import tpu as pltpu

F32 = jnp.float32
BF16 = jnp.bfloat16
MESH = pl.DeviceIdType.MESH

D_MODEL = 1024
D_FF = 2816
PLE_DIM = 256
RET_HEADS = 4
RET_DIM = 128
RET_WIDTH = 512
RET_CHUNK = 256
RET_GROUP_FWD = 16
RET_GROUP_BWD = 8
MLA_HEADS = 8
MLA_NOPE = 64
MLA_ROPE = 32
MLA_V = 64
Q_LORA = 384
KV_LORA = 256
IN_COLS = 2720
IN_COLS_P = 2816
IN_SHARD = IN_COLS // 4
IN_SHARD_P = 688
ROPE_BASE = 10000.0
EPS = 1e-6
SCALE_MLA = 1.0 / math.sqrt(MLA_NOPE + MLA_ROPE)
SCALE_RET = RET_DIM ** -0.5
NEG = -1e30

ADAM_LR = 0.001
ADAM_B1 = 0.9
ADAM_B2 = 0.999
ADAM_EPS = 1e-08
ADAM_WD = 0.01
ADAM_STEP = 10

N_CHIPS = 4
N_DEV = 8
VMEM_MB = 56

BIG = (
    ("w_in", 1024, 2720, 1),
    ("w_uq", 384, 768, 1),
    ("w_ukv", 256, 1024, 1),
    ("w_o", 1024, 1024, 0),
    ("w_gate", 1024, 2816, 1),
    ("w_up", 1024, 2816, 1),
    ("w_down", 2816, 1024, 0),
    ("w_ple_proj", 256, 1024, 1),
    ("w_ple_gate", 1024, 1024, 0),
)
SMALL = (
    ("pre_mix_norm", 1024),
    ("ret_gn_w", 512),
    ("mla_q_norm", 384),
    ("mla_kv_norm", 256),
    ("post_mix_norm", 1024),
    ("pre_ffn_norm", 1024),
    ("post_ffn_norm", 1024),
    ("ple_norm", 1024),
    ("b_ple_gate", 1024),
)
ALL_W = ("pre_mix_norm", "w_in", "ret_gn_w", "mla_q_norm", "w_uq", "mla_kv_norm", "w_ukv", "w_o", "post_mix_norm",
         "pre_ffn_norm", "w_gate", "w_up", "w_down", "post_ffn_norm", "w_ple_proj", "ple_norm", "w_ple_gate", "b_ple_gate")
PACK_COLS = 1024
SMALL_ROWS = 16


def _cp(sem=None, mb=VMEM_MB, **kw):
    return pltpu.CompilerParams(dimension_semantics=sem, vmem_limit_bytes=mb * 1024 * 1024, **kw)


def _bf(x):
    return x.astype(BF16)


def _dot(a, b):
    return jnp.dot(_bf(a), _bf(b), preferred_element_type=F32)


def _dot_nt(a, b):
    return lax.dot_general(_bf(a), _bf(b), (((1,), (1,)), ((), ())), preferred_element_type=F32)


def _dot_tn(a, b):
    return lax.dot_general(_bf(a), _bf(b), (((0,), (0,)), ((), ())), preferred_element_type=F32)


def _sig(x):
    return 1.0 / (1.0 + jnp.exp(-x))


def _rms(x, g):
    r = lax.rsqrt(jnp.mean(x * x, axis=-1, keepdims=True) + EPS)
    return x * r * g


def _rms_bwd(dy, x, g):
    r = lax.rsqrt(jnp.mean(x * x, axis=-1, keepdims=True) + EPS)
    xh = x * r
    dxh = dy * g
    dx = r * (dxh - xh * jnp.mean(dxh * xh, axis=-1, keepdims=True))
    return dx, dy * xh


def _colsum(x):
    return jnp.sum(x, axis=0, keepdims=True)


def _rope_ret(x, cr, sr):
    return x * cr + pltpu.roll(x, 64, 1) * sr


def _unrope_ret(dy, cr, sr):
    return dy * cr + pltpu.roll(dy * sr, 64, 1)


def _rope_mla(x, cm, sa, sb):
    return x * cm + pltpu.roll(x, 112, 1) * sa + pltpu.roll(x, 16, 1) * sb


def _unrope_mla(dy, cm, sa, sb):
    return dy * cm + pltpu.roll(dy * sa, 16, 1) + pltpu.roll(dy * sb, 112, 1)


def _rows(tm, w, col=0):
    return pl.BlockSpec((tm, w), lambda i: (i, col))


def _full(*shape):
    return pl.BlockSpec(shape, lambda i: (0,) * len(shape), pipeline_mode=pl.Buffered(1))


def _acc(*shape):
    return pl.BlockSpec(shape, lambda i: (0,) * len(shape))


def _sds(shape, dtype):
    return jax.ShapeDtypeStruct(shape, dtype)


def _rope_tables(pos_f, S, shards=()):
    tm = min(512, S)
    n = len(shards)
    steps = S // tm
    inv_r = (1.0 / (np.float32(ROPE_BASE) ** (np.arange(64, dtype=np.float32) / np.float32(64)))).astype(np.float32)
    inv_m16 = (1.0 / (np.float32(ROPE_BASE) ** (np.arange(16, dtype=np.float32) / np.float32(16)))).astype(np.float32)
    inv_r = np.concatenate([inv_r, inv_r])[None, :]
    inv_m = np.zeros((1, 128), np.float32)
    inv_m[0, 64:80] = inv_m16
    inv_m[0, 80:96] = inv_m16

    def body(pos_ref, invr_ref, invm_ref, *rest):
        w_ins, (cr_ref, sr_ref, cm_ref, sa_ref, sb_ref) = rest[:n], rest[n:n + 5]
        w_outs, sems = rest[n + 5:2 * n + 5], rest[2 * n + 5:]
        i = pl.program_id(0)
        if n:
            @pl.when(i == 0)
            def _():
                _gather_phase(0, w_ins, w_outs, sems)

            @pl.when(i == steps - 1)
            def _():
                _gather_phase(1, w_ins, w_outs, sems)

        pos = pos_ref[...]
        lane = lax.broadcasted_iota(jnp.int32, (tm, 128), 1)
        ar = pos * invr_ref[...]
        s = jnp.sin(ar)
        cr_ref[...] = jnp.cos(ar)
        sr_ref[...] = jnp.where(lane < 64, -s, s)
        am = pos * invm_ref[...]
        c2 = jnp.cos(am)
        s2 = jnp.sin(am)
        cm_ref[...] = jnp.where(lane < 64, 1.0, jnp.where(lane < 96, c2, 0.0))
        sa_ref[...] = jnp.where((lane >= 64) & (lane < 80), -s2, 0.0)
        sb_ref[...] = jnp.where((lane >= 80) & (lane < 96), s2, 0.0)

        if n:
            @pl.when(i == steps - 1)
            def _():
                _gather_phase(2, w_ins, w_outs, sems)

    outs = pl.pallas_call(
        body, name="rope_tables", grid=(steps,),
        in_specs=[_rows(tm, 1), _full(1, 128), _full(1, 128)] + [_ANY] * n,
        out_specs=[_rows(tm, 128)] * 5 + [_ANY] * n,
        out_shape=[_sds((S, 128), F32)] * 5 + _gather_out_shapes(shards),
        scratch_shapes=_gather_sems(n) if n else [],
        compiler_params=_cp(("arbitrary",)),
    )(pos_f, jnp.asarray(inv_r), jnp.asarray(inv_m), *shards)
    return outs[:5], outs[5:]


def _inproj(x, g, w_in, tabs, S):
    tm = min(512, S)

    def body(x_ref, g_ref, w_ref, cr_ref, sr_ref, cm_ref, sa_ref, sb_ref,
             xn_ref, rq_ref, rk_ref, rv_ref, rg_ref, cq_ref, ckv_ref, kr_ref):
        xb = _rms(x_ref[...], g_ref[...]).astype(BF16)
        xn_ref[...] = xb
        cr = cr_ref[...]
        sr = sr_ref[...]
        q = jnp.dot(xb, w_ref[:, 0:512], preferred_element_type=F32)
        k = jnp.dot(xb, w_ref[:, 512:1024], preferred_element_type=F32)
        for h in range(RET_HEADS):
            sl = slice(h * 128, (h + 1) * 128)
            rq_ref[:, sl] = _rope_ret(q[:, sl], cr, sr).astype(BF16)
            rk_ref[:, sl] = (_rope_ret(k[:, sl], cr, sr) * SCALE_RET).astype(BF16)
        rv_ref[...] = jnp.dot(xb, w_ref[:, 1024:1536], preferred_element_type=F32).astype(BF16)
        rg_ref[...] = jnp.dot(xb, w_ref[:, 1536:2048], preferred_element_type=F32)
        cq_ref[...] = jnp.dot(xb, w_ref[:, 2048:2432], preferred_element_type=F32)
        ckv_ref[...] = jnp.dot(xb, w_ref[:, 2432:2688], preferred_element_type=F32)
        kr = pltpu.roll(jnp.dot(xb, w_ref[:, 2688:2816], preferred_element_type=F32), 64, 1)
        kr_ref[...] = _rope_mla(kr, cm_ref[...], sa_ref[...], sb_ref[...])

    return pl.pallas_call(
        body, name="inproj", grid=(S // tm,),
        in_specs=[_rows(tm, D_MODEL), _full(1, D_MODEL), _full(D_MODEL, IN_COLS_P)] + [_rows(tm, 128)] * 5,
        out_specs=[_rows(tm, D_MODEL)] + [_rows(tm, 512)] * 4 + [_rows(tm, Q_LORA), _rows(tm, KV_LORA), _rows(tm, 128)],
        out_shape=[_sds((S, D_MODEL), BF16)] + [_sds((S, 512), BF16)] * 3
        + [_sds((S, 512), F32), _sds((S, Q_LORA), F32), _sds((S, KV_LORA), F32), _sds((S, 128), F32)],
        compiler_params=_cp(("parallel",)),
    )(x, g, w_in, *tabs)


def _mla_up(cq, ckv, kr, gq, gkv, w_uq, w_ukv, tabs, S):
    tm = min(512, S)

    def body(cq_ref, ckv_ref, kr_ref, gq_ref, gkv_ref, wuq_ref, wukv_ref, cm_ref, sa_ref, sb_ref,
             cqn_ref, ckvn_ref, qp_ref, kp_ref, v_ref, kt_ref, vt_ref):
        cm = cm_ref[...]
        sa = sa_ref[...]
        sb = sb_ref[...]
        cqn = _rms(cq_ref[...], gq_ref[...]).astype(BF16)
        cqn_ref[...] = cqn
        ckvn = _rms(ckv_ref[...], gkv_ref[...]).astype(BF16)
        ckvn_ref[...] = ckvn
        qh = jnp.dot(cqn, wuq_ref[...], preferred_element_type=F32)
        kv = jnp.dot(ckvn, wukv_ref[...], preferred_element_type=F32)
        kr_blk = kr_ref[...]
        for h in range(MLA_HEADS):
            sl = slice(h * 128, (h + 1) * 128)
            qp_ref[:, sl] = (_rope_mla(qh[:, sl], cm, sa, sb) * SCALE_MLA).astype(BF16)
            kh = kv[:, sl] + kr_blk
            kp_ref[:, sl] = kh.astype(BF16)
            kt_ref[sl, :] = kh.T.astype(BF16)
        for h in range(MLA_HEADS // 2):
            vh = kv[:, 1024 + h * 128:1024 + (h + 1) * 128]
            v_ref[:, h * 128:(h + 1) * 128] = vh.astype(BF16)
            vt_ref[h * 128:(h + 1) * 128, :] = vh.T.astype(BF16)

    cols = lambda r: pl.BlockSpec((r, tm), lambda i: (0, i))
    return pl.pallas_call(
        body, name="mla_up", grid=(S // tm,),
        in_specs=[_rows(tm, Q_LORA), _rows(tm, KV_LORA), _rows(tm, 128), _full(1, Q_LORA), _full(1, KV_LORA),
                  _full(Q_LORA, 1024), _full(KV_LORA, 1536)] + [_rows(tm, 128)] * 3,
        out_specs=[_rows(tm, Q_LORA), _rows(tm, KV_LORA), _rows(tm, 1024), _rows(tm, 1024), _rows(tm, 512),
                   cols(1024), cols(512)],
        out_shape=[_sds((S, Q_LORA), BF16), _sds((S, KV_LORA), BF16), _sds((S, 1024), BF16), _sds((S, 1024), BF16),
                   _sds((S, 512), BF16), _sds((1024, S), BF16), _sds((512, S), BF16)],
        compiler_params=_cp(("parallel",)),
    )(cq, ckv, kr, gq, gkv, w_uq, w_ukv, *tabs[2:])


def _tri_pairs(nq, k_major):
    if k_major:
        pairs = [(qb, kb) for kb in range(nq) for qb in range(kb, nq)]
    else:
        pairs = [(qb, kb) for qb in range(nq) for kb in range(qb + 1)]
    qb_of = np.array([p[0] for p in pairs], np.int32)
    kb_of = np.array([p[1] for p in pairs], np.int32)
    return jnp.asarray(qb_of), jnp.asarray(kb_of), len(pairs)


ATT_ROWS = 32
FWD_HEADS = 8
BWD_HEADS = 4


def _causal_keep(r0, rows, tq):
    key = r0 + lax.broadcasted_iota(jnp.int32, (rows, tq), 0)
    qry = lax.broadcasted_iota(jnp.int32, (rows, tq), 1)
    return key <= qry


def _flash_fwd(qp, kp, vt, S, shards=()):
    tq = min(512, S)
    nq = S // tq
    RB = ATT_ROWS
    NH = FWD_HEADS
    qb_of, kb_of, T = _tri_pairs(nq, k_major=False)
    n = len(shards)
    steps = (MLA_HEADS // NH) * T

    def body(qb_ref, kb_ref, q_ref, k_ref, vt_ref, *rest):
        w_ins, (o_ref, lse_ref), w_outs = rest[:n], rest[n:n + 2], rest[n + 2:2 * n + 2]
        m_sc, l_sc, acc_sc, s_sc, p_sc = rest[2 * n + 2:2 * n + 7]
        sems = rest[2 * n + 7:]
        t = pl.program_id(1)
        qb = qb_ref[t]
        kb = kb_ref[t]
        lin = pl.program_id(0) * T + t

        if n:
            @pl.when(lin == 0)
            def _():
                _gather_phase(0, w_ins, w_outs, sems)

            @pl.when(lin == steps // 2)
            def _():
                _gather_phase(1, w_ins, w_outs, sems)

        @pl.when(kb == 0)
        def _():
            m_sc[...] = jnp.full(m_sc.shape, NEG, F32)
            l_sc[...] = jnp.zeros(l_sc.shape, F32)
            acc_sc[...] = jnp.zeros(acc_sc.shape, F32)

        def scores(a):
            sl = slice(a * 128, (a + 1) * 128)
            s_sc[a] = _dot_nt(k_ref[:, sl], q_ref[:, sl])

        def step(masked):
            for a in range(NH):
                scores(a)
            for a in range(NH):
                mx = [jnp.full((8, tq), NEG, F32) for _ in range(RB // 8)]
                for r in range(0, tq, RB):
                    sc = s_sc[a, r:r + RB, :]
                    if masked:
                        sc = jnp.where(_causal_keep(r, RB, tq), sc, NEG)
                        s_sc[a, r:r + RB, :] = sc
                    for i in range(RB // 8):
                        mx[i] = jnp.maximum(mx[i], sc[i * 8:(i + 1) * 8, :])
                mx8 = functools.reduce(jnp.maximum, mx)
                m_prev = m_sc[a]
                m_new = jnp.maximum(m_prev, jnp.max(mx8, axis=0, keepdims=True))
                al = jnp.exp(m_prev - m_new)
                m_sc[a] = m_new
                ls = [jnp.zeros((8, tq), F32) for _ in range(RB // 8)]
                for r in range(0, tq, RB):
                    p = jnp.exp(s_sc[a, r:r + RB, :] - m_new)
                    for i in range(RB // 8):
                        ls[i] = ls[i] + p[i * 8:(i + 1) * 8, :]
                    p_sc[a, r:r + RB, :] = p.astype(BF16)
                l_sc[a] = al * l_sc[a] + jnp.sum(functools.reduce(jnp.add, ls), axis=0, keepdims=True)
                pair = slice((a // 2) * 128, (a // 2 + 1) * 128)
                pv = jnp.dot(vt_ref[pair, :], p_sc[a], preferred_element_type=F32)
                rs = slice(a * 64, (a + 1) * 64)
                own = slice((a % 2) * 64, (a % 2 + 1) * 64)
                acc_sc[rs, :] = acc_sc[rs, :] * al + pv[own, :]

        @pl.when(kb < qb)
        def _():
            step(False)

        @pl.when(kb == qb)
        def _():
            step(True)
            for a in range(NH):
                rs = slice(a * 64, (a + 1) * 64)
                acc_sc[rs, :] = acc_sc[rs, :] / l_sc[a]
                lse_ref[a:a + 1, :] = m_sc[a] + jnp.log(l_sc[a])
            o_ref[...] = acc_sc[...].T.astype(BF16)

        if n:
            @pl.when(lin == steps - 1)
            def _():
                _gather_phase(2, w_ins, w_outs, sems)

    grid_spec = pltpu.PrefetchScalarGridSpec(
        num_scalar_prefetch=2, grid=(MLA_HEADS // NH, T),
        in_specs=[pl.BlockSpec((tq, 128 * NH), lambda j, t, qb, kb: (qb[t], j)),
                  pl.BlockSpec((tq, 128 * NH), lambda j, t, qb, kb: (kb[t], j)),
                  pl.BlockSpec((64 * NH, tq), lambda j, t, qb, kb: (j, kb[t]))] + [_ANY] * n,
        out_specs=[pl.BlockSpec((tq, 64 * NH), lambda j, t, qb, kb: (qb[t], j)),
                   pl.BlockSpec((None, NH, tq), lambda j, t, qb, kb: (j, 0, qb[t]))] + [_ANY] * n,
        scratch_shapes=[pltpu.VMEM((NH, 1, tq), F32), pltpu.VMEM((NH, 1, tq), F32), pltpu.VMEM((64 * NH, tq), F32),
                        pltpu.VMEM((NH, tq, tq), F32), pltpu.VMEM((NH, tq, tq), BF16)] + (_gather_sems(n) if n else []),
    )
    out, lse, *gathered = pl.pallas_call(
        body, name="flash_fwd", grid_spec=grid_spec,
        out_shape=[_sds((S, 512), BF16), _sds((MLA_HEADS // NH, NH, S), F32)] + _gather_out_shapes(shards),
        compiler_params=_cp(("arbitrary", "arbitrary")),
    )(qb_of, kb_of, qp, kp, vt, *shards)
    return out, lse.reshape(MLA_HEADS // 2, 2, S), gathered


def _decay_table():
    log_g = np.log(1.0 - 2.0 ** (-5.0 - np.arange(RET_HEADS, dtype=np.float32))).astype(np.float32)
    return jnp.asarray(np.broadcast_to(log_g[:, None, None], (RET_HEADS, 8, 128)).copy())


def _decay_terms(lg_ref):
    C = RET_CHUNK
    lg = lg_ref[0:1, :]
    row = lax.broadcasted_iota(jnp.int32, (C, C), 0)
    col = lax.broadcasted_iota(jnp.int32, (C, C), 1)
    diff = (row - col).astype(F32)
    dmat = jnp.where(diff >= 0, jnp.exp(jnp.maximum(diff, 0.0) * jnp.tile(lg, (1, C // 128))), 0.0)
    j = lax.broadcasted_iota(jnp.int32, (C, 1), 0).astype(F32)
    lg1 = lg[:, 0:1]
    zeta = jnp.exp((C - 1 - j) * lg1)
    xi = jnp.exp((j + 1.0) * lg1)
    g_chunk = jnp.exp(C * lg1)
    return dmat, zeta, xi, g_chunk


def _ret_fwd(rq, rk, rv, rg, gn_w, S):
    C = RET_CHUNK
    N = S // C
    G = min(RET_GROUP_FWD, N)
    NB = N // G

    def body(lg_ref, q_ref, k_ref, v_ref, rg_ref, w_ref, ry_ref, ro_ref, rprev_ref, r_sc):
        @pl.when(pl.program_id(1) == 0)
        def _():
            r_sc[...] = jnp.zeros(r_sc.shape, F32)

        dmat, zeta, xi, g_chunk = _decay_terms(lg_ref)
        w = w_ref[...]
        r = r_sc[...]
        for i in range(G):
            rows = slice(i * C, (i + 1) * C)
            q = q_ref[rows, :]
            k = k_ref[rows, :]
            v = v_ref[rows, :]
            r_prev = r.astype(BF16)
            rprev_ref[i] = r_prev
            sc = _dot_nt(q, k) * dmat
            ry = _dot(sc, v) + jnp.dot(q, r_prev, preferred_element_type=F32) * xi
            ry_ref[rows, :] = ry
            r = g_chunk * r + _dot_tn(k, zeta * v.astype(F32))
            mu = jnp.mean(ry, axis=-1, keepdims=True)
            yc = ry - mu
            yh = yc * lax.rsqrt(jnp.mean(yc * yc, axis=-1, keepdims=True) + EPS)
            g = rg_ref[rows, :]
            ro_ref[rows, :] = (g * _sig(g) * (yh * w)).astype(BF16)
        r_sc[...] = r

    blk = pl.BlockSpec((G * C, 128), lambda h, n: (n, h))
    return pl.pallas_call(
        body, name="ret_fwd", grid=(RET_HEADS, NB),
        in_specs=[pl.BlockSpec((None, 8, 128), lambda h, n: (h, 0, 0)), blk, blk, blk, blk,
                  pl.BlockSpec((1, 128), lambda h, n: (0, h))],
        out_specs=[blk, blk, pl.BlockSpec((G, 128, 128), lambda h, n: (h * NB + n, 0, 0))],
        out_shape=[_sds((S, 512), F32), _sds((S, 512), BF16), _sds((RET_HEADS * N, 128, 128), BF16)],
        scratch_shapes=[pltpu.VMEM((128, 128), F32)],
        compiler_params=_cp(("parallel", "arbitrary")),
    )(_decay_table(), rq, rk, rv, rg, gn_w)


def _outproj(ro, mo, x, w_o, g_post, g_pre, S):
    tm = min(512, S)

    def body(ro_ref, mo_ref, x_ref, wo_ref, g1_ref, g2_ref, mix_ref, h1_ref, hn_ref):
        mix = (jnp.dot(ro_ref[...], wo_ref[0:512, :], preferred_element_type=F32)
               + jnp.dot(mo_ref[...], wo_ref[512:1024, :], preferred_element_type=F32))
        mix_ref[...] = mix.astype(BF16)
        h1 = x_ref[...] + _rms(mix, g1_ref[...])
        h1_ref[...] = h1
        hn_ref[...] = _rms(h1, g2_ref[...]).astype(BF16)

    return pl.pallas_call(
        body, name="outproj", grid=(S // tm,),
        in_specs=[_rows(tm, 512), _rows(tm, 512), _rows(tm, D_MODEL), _full(D_MODEL, D_MODEL), _full(1, D_MODEL),
                  _full(1, D_MODEL)],
        out_specs=[_rows(tm, D_MODEL)] * 3,
        out_shape=[_sds((S, D_MODEL), BF16), _sds((S, D_MODEL), F32), _sds((S, D_MODEL), BF16)],
        compiler_params=_cp(("parallel",)),
    )(ro, mo, x, w_o, g_post, g_pre)


def _ffn_up(hn, w_gate_t, w_up_t, S):
    tm = min(512, S)
    tn = D_FF // 2

    def body(hn_ref, wg_ref, wu_ref, fg_ref, fu_ref, act_ref):
        hn_b = hn_ref[...]
        for seg in range(2):
            sl = slice(seg * tn, (seg + 1) * tn)
            g = _dot_nt(hn_b, wg_ref[sl, :])
            u = _dot_nt(hn_b, wu_ref[sl, :])
            s = _sig(g)
            silu = g * s
            fg_ref[:, sl] = (u * (s + silu * (1.0 - s))).astype(BF16)
            fu_ref[:, sl] = silu.astype(BF16)
            act_ref[:, sl] = (silu * u).astype(BF16)

    return pl.pallas_call(
        body, name="ffn_up", grid=(S // tm,),
        in_specs=[_rows(tm, D_MODEL), _full(D_FF, D_MODEL), _full(D_FF, D_MODEL)],
        out_specs=[_rows(tm, D_FF)] * 3, out_shape=[_sds((S, D_FF), BF16)] * 3,
        compiler_params=_cp(("parallel",)),
    )(hn, w_gate_t, w_up_t)


def _ffn_down(act, w_down, h1, g, S):
    tm = min(512, S)

    def body(act_ref, wd_ref, h1_ref, g_ref, ff_ref, h2_ref):
        ff = jnp.dot(act_ref[...], wd_ref[...], preferred_element_type=F32)
        ff_ref[...] = ff.astype(BF16)
        h2_ref[...] = h1_ref[...] + _rms(ff, g_ref[...])

    return pl.pallas_call(
        body, name="ffn_down", grid=(S // tm,),
        in_specs=[_rows(tm, D_FF), _full(D_FF, D_MODEL), _rows(tm, D_MODEL), _full(1, D_MODEL)],
        out_specs=[_rows(tm, D_MODEL)] * 2, out_shape=[_sds((S, D_MODEL), BF16), _sds((S, D_MODEL), F32)],
        compiler_params=_cp(("parallel",)),
    )(act, w_down, h1, g)


def _ple_loss(p, h2, tgt, w_pp, w_pg, b_pg, g_ple, S):
    tm = min(512, S)

    def body(p_ref, h2_ref, t_ref, wp_ref, wg_ref, b_ref, gp_ref,
             dz_ref, dpe_ref, dh2_ref, h2b_ref, loss_ref, dgp_ref, db_ref):
        @pl.when(pl.program_id(0) == 0)
        def _():
            loss_ref[...] = jnp.zeros(loss_ref.shape, F32)
            dgp_ref[...] = jnp.zeros(dgp_ref.shape, F32)
            db_ref[...] = jnp.zeros(db_ref.shape, F32)

        gp = gp_ref[...]
        pe = _dot(p_ref[...], wp_ref[...])
        r = lax.rsqrt(jnp.mean(pe * pe, axis=-1, keepdims=True) + EPS)
        peh = pe * r
        e = peh * gp
        h2 = h2_ref[...]
        h2b = h2.astype(BF16)
        h2b_ref[...] = h2b
        gt = _sig(jnp.dot(h2b, wg_ref[...], preferred_element_type=F32) + b_ref[...])
        diff = h2 + e * gt - t_ref[...]
        loss_ref[...] += _colsum(diff * diff)
        dh3 = diff * (1.0 / D_MODEL)
        de = dh3 * gt
        dz = dh3 * e * gt * (1.0 - gt)
        db_ref[...] += _colsum(dz)
        dgp_ref[...] += _colsum(de * peh)
        dpeh = de * gp
        dpe = r * (dpeh - peh * jnp.mean(dpeh * peh, axis=-1, keepdims=True))
        dzb = dz.astype(BF16)
        dz_ref[...] = dzb
        dpe_ref[...] = dpe.astype(BF16)
        dh2_ref[...] = dh3 + _dot_nt(dzb, wg_ref[...])

    return pl.pallas_call(
        body, name="ple_loss", grid=(S // tm,),
        in_specs=[_rows(tm, PLE_DIM), _rows(tm, D_MODEL), _rows(tm, D_MODEL), _full(PLE_DIM, D_MODEL),
                  _full(D_MODEL, D_MODEL), _full(1, D_MODEL), _full(1, D_MODEL)],
        out_specs=[_rows(tm, D_MODEL)] * 4 + [_acc(1, D_MODEL)] * 3,
        out_shape=[_sds((S, D_MODEL), BF16), _sds((S, D_MODEL), BF16), _sds((S, D_MODEL), F32), _sds((S, D_MODEL), BF16)]
        + [_sds((1, D_MODEL), F32)] * 3,
        compiler_params=_cp(("arbitrary",)),
    )(p, h2, tgt, w_pp, w_pg, b_pg, g_ple)


def _wgrad(a, b, name, S):
    M = a.shape[1]
    N = b.shape[1]
    ts = min(2048, S)
    nsplit = 2 if M * N >= 2 * 1024 * 1024 else 1
    tn = N // nsplit

    def body(a_ref, b_ref, o_ref):
        @pl.when(pl.program_id(1) == 0)
        def _():
            o_ref[...] = jnp.zeros(o_ref.shape, F32)

        o_ref[...] += _dot_tn(a_ref[...], b_ref[...])

    return pl.pallas_call(
        body, name=name, grid=(nsplit, S // ts),
        in_specs=[pl.BlockSpec((ts, M), lambda j, s: (s, 0)), pl.BlockSpec((ts, tn), lambda j, s: (s, j))],
        out_specs=pl.BlockSpec((M, tn), lambda j, s: (0, j)), out_shape=_sds((M, N), F32),
        compiler_params=_cp(("parallel", "arbitrary")),
    )(a, b)


def _ffn_down_bwd(dh2, ff, g, w_down, dgate_f, dup_f, S):
    tm = min(512, S)
    tn = D_FF // 2

    def body(dh2_ref, ff_ref, g_ref, wd_ref, fg_ref, fu_ref, dff_ref, dgate_ref, dup_ref, dg_ref):
        @pl.when(pl.program_id(0) == 0)
        def _():
            dg_ref[...] = jnp.zeros(dg_ref.shape, F32)

        dff, ga = _rms_bwd(dh2_ref[...], ff_ref[...].astype(F32), g_ref[...])
        dg_ref[...] += _colsum(ga)
        dffb = dff.astype(BF16)
        dff_ref[...] = dffb
        for seg in range(2):
            sl = slice(seg * tn, (seg + 1) * tn)
            dact = _dot_nt(dffb, wd_ref[sl, :])
            dgate_ref[:, sl] = (dact * fg_ref[:, sl].astype(F32)).astype(BF16)
            dup_ref[:, sl] = (dact * fu_ref[:, sl].astype(F32)).astype(BF16)

    return pl.pallas_call(
        body, name="ffn_down_bwd", grid=(S // tm,),
        in_specs=[_rows(tm, D_MODEL), _rows(tm, D_MODEL), _full(1, D_MODEL), _full(D_FF, D_MODEL), _rows(tm, D_FF),
                  _rows(tm, D_FF)],
        out_specs=[_rows(tm, D_MODEL), _rows(tm, D_FF), _rows(tm, D_FF), _acc(1, D_MODEL)],
        out_shape=[_sds((S, D_MODEL), BF16), _sds((S, D_FF), BF16), _sds((S, D_FF), BF16), _sds((1, D_MODEL), F32)],
        compiler_params=_cp(("arbitrary",)),
    )(dh2, ff, g, w_down, dgate_f, dup_f)


def _ffn_up_bwd(dgate, dup, w_gate, w_up, h1, mix, dh2, g_pre, g_post, w_o, S, grads=()):
    tm = min(512, S)
    n = len(grads)
    last = S // tm - 1

    def body(dgate_ref, dup_ref, wg_ref, wu_ref, h1_ref, mix_ref, dh2_ref, g2_ref, g1_ref, wo_ref, *rest):
        g_ins = rest[:n]
        dh1_ref, dmix_ref, dro_ref, dmo_ref, dg2_ref, dg1_ref = rest[n:n + 6]
        g_outs, sems = rest[n + 6:2 * n + 6], rest[2 * n + 6:]

        @pl.when(pl.program_id(0) == 0)
        def _():
            dg2_ref[...] = jnp.zeros(dg2_ref.shape, F32)
            dg1_ref[...] = jnp.zeros(dg1_ref.shape, F32)
            for cp in (_swap_copies(g_ins, g_outs, sems) if n else []):
                cp.start()

        dhn = (jnp.dot(dgate_ref[...], wg_ref[...], preferred_element_type=F32)
               + jnp.dot(dup_ref[...], wu_ref[...], preferred_element_type=F32))
        d1, ga = _rms_bwd(dhn, h1_ref[...], g2_ref[...])
        dg2_ref[...] += _colsum(ga)
        dh1 = dh2_ref[...] + d1
        dh1_ref[...] = dh1
        dmix, gb = _rms_bwd(dh1, mix_ref[...].astype(F32), g1_ref[...])
        dg1_ref[...] += _colsum(gb)
        dmixb = dmix.astype(BF16)
        dmix_ref[...] = dmixb
        dcat = _dot_nt(dmixb, wo_ref[...])
        dro_ref[...] = dcat[:, 0:512].astype(BF16)
        dmo_ref[...] = dcat[:, 512:1024].astype(BF16)

        if n:
            @pl.when(pl.program_id(0) == last)
            def _():
                for cp in _swap_copies(g_ins, g_outs, sems):
                    cp.wait()

    dh1, dmix, dro, dmo, dg2, dg1, *got = pl.pallas_call(
        body, name="ffn_up_bwd", grid=(S // tm,),
        in_specs=[_rows(tm, D_FF), _rows(tm, D_FF), _full(D_FF, D_MODEL), _full(D_FF, D_MODEL), _rows(tm, D_MODEL),
                  _rows(tm, D_MODEL), _rows(tm, D_MODEL), _full(1, D_MODEL), _full(1, D_MODEL), _full(D_MODEL, D_MODEL)]
        + [_ANY] * n,
        out_specs=[_rows(tm, D_MODEL), _rows(tm, D_MODEL), _rows(tm, 512), _rows(tm, 512), _acc(1, D_MODEL),
                   _acc(1, D_MODEL)] + [_ANY] * n,
        out_shape=[_sds((S, D_MODEL), F32), _sds((S, D_MODEL), BF16), _sds((S, 512), BF16), _sds((S, 512), BF16),
                   _sds((1, D_MODEL), F32), _sds((1, D_MODEL), F32)] + _swap_out_shapes(grads),
        scratch_shapes=_swap_sems(n) if n else [],
        compiler_params=_cp(("arbitrary",)),
    )(dgate, dup, w_gate, w_up, h1, mix, dh2, g_pre, g_post, w_o, *grads)
    return dh1, dmix, dro, dmo, dg2, dg1, got


def _attn_delta(o, do, S, grads=()):
    tm = min(512, S)
    n = len(grads)
    last = S // tm - 1

    def body(o_ref, do_ref, *rest):
        g_ins, (dot_ref, d_ref), g_outs, sems = rest[:n], rest[n:n + 2], rest[n + 2:2 * n + 2], rest[2 * n + 2:]
        if n:
            @pl.when(pl.program_id(0) == 0)
            def _():
                for cp in _swap_copies(g_ins, g_outs, sems):
                    cp.start()

        do = do_ref[...].astype(F32)
        prod_t = (o_ref[...].astype(F32) * do).T
        dot_ref[...] = do.T.astype(BF16)
        for h in range(MLA_HEADS):
            d_ref[h // 2, (h % 2):(h % 2) + 1, :] = jnp.sum(prod_t[h * 64:(h + 1) * 64, :], axis=0, keepdims=True)

        if n:
            @pl.when(pl.program_id(0) == last)
            def _():
                for cp in _swap_copies(g_ins, g_outs, sems):
                    cp.wait()

    dot, delta, *got = pl.pallas_call(
        body, name="attn_delta", grid=(S // tm,),
        in_specs=[_rows(tm, 512), _rows(tm, 512)] + [_ANY] * n,
        out_specs=[pl.BlockSpec((512, tm), lambda i: (0, i)), pl.BlockSpec((MLA_HEADS // 2, 2, tm), lambda i: (0, 0, i))]
        + [_ANY] * n,
        out_shape=[_sds((512, S), BF16), _sds((MLA_HEADS // 2, 2, S), F32)] + _swap_out_shapes(grads),
        scratch_shapes=_swap_sems(n) if n else [],
        compiler_params=_cp(("arbitrary",)),
    )(o, do, *grads)
    return dot, delta, got


def _flash_bwd(qp, kp, kt, v, do, dot, lse, delta, S, sums=()):
    tq = min(512, S)
    nq = S // tq
    RB = ATT_ROWS
    NH = BWD_HEADS
    qb_of, kb_of, T = _tri_pairs(nq, k_major=True)
    n = len(sums)
    steps = (MLA_HEADS // NH) * T

    def body(qb_ref, kb_ref, q_ref, k_ref, kt_ref, v_ref, do_ref, dot_ref, lse_ref, dl_ref, *rest):
        g_ins, (dq_ref, dk_ref, dv_ref), g_outs = rest[:n], rest[n:n + 3], rest[n + 3:2 * n + 3]
        dk_sc, dv_sc, s_sc, dp_sc, p_sc, ds_sc = rest[2 * n + 3:2 * n + 9]
        sems = rest[2 * n + 9:]
        t = pl.program_id(1)
        qb = qb_ref[t]
        kb = kb_ref[t]
        lin = pl.program_id(0) * T + t

        if n:
            @pl.when(lin == 0)
            def _():
                for cp in _scatter_copies(g_ins, g_outs, sems):
                    cp.start()

        @pl.when(t == 0)
        def _():
            dq_ref[...] = jnp.zeros(dq_ref.shape, F32)

        @pl.when(qb == kb)
        def _():
            dk_sc[...] = jnp.zeros(dk_sc.shape, F32)
            dv_sc[...] = jnp.zeros(dv_sc.shape, F32)

        lane = lax.broadcasted_iota(jnp.int32, (tq, 64 * NH), 1)

        def step(masked):
            vv = v_ref[...]
            do_all = do_ref[...]
            mine = [(lane >= a * 64) & (lane < (a + 1) * 64) for a in range(NH)]
            for a in range(NH):
                sl = slice(a * 128, (a + 1) * 128)
                s_sc[a] = _dot_nt(k_ref[:, sl], q_ref[:, sl])
                dp_sc[a] = jnp.dot(jnp.where(mine[a], vv, jnp.zeros_like(vv)), dot_ref[...],
                                   preferred_element_type=F32)
            for a in range(NH):
                sl = slice(a * 128, (a + 1) * 128)
                lse = lse_ref[a:a + 1, :]
                dl = dl_ref[a:a + 1, :]
                for r in range(0, tq, RB):
                    sc = s_sc[a, r:r + RB, :]
                    if masked:
                        sc = jnp.where(_causal_keep(r, RB, tq), sc, NEG)
                    p = jnp.exp(sc - lse)
                    p_sc[a, r:r + RB, :] = p.astype(BF16)
                    ds_sc[a, r:r + RB, :] = (p * (dp_sc[a, r:r + RB, :] - dl)).astype(BF16)
                ds = ds_sc[a]
                dv_sc[...] += jnp.dot(p_sc[a], jnp.where(mine[a], do_all, jnp.zeros_like(do_all)),
                                      preferred_element_type=F32)
                dk_sc[:, sl] += jnp.dot(ds, q_ref[:, sl], preferred_element_type=F32)
                dq_ref[qb, sl, :] += jnp.dot(kt_ref[sl, :], ds, preferred_element_type=F32)

        @pl.when(qb > kb)
        def _():
            step(False)

        @pl.when(qb == kb)
        def _():
            step(True)

        @pl.when(qb == nq - 1)
        def _():
            dk_ref[...] = dk_sc[...].astype(BF16)
            dv_ref[...] = dv_sc[...].astype(BF16)

        if n:
            @pl.when(lin == steps - 1)
            def _():
                for cp in _scatter_copies(g_ins, g_outs, sems):
                    cp.wait()

    grid_spec = pltpu.PrefetchScalarGridSpec(
        num_scalar_prefetch=2, grid=(MLA_HEADS // NH, T),
        in_specs=[pl.BlockSpec((tq, 128 * NH), lambda j, t, qb, kb: (qb[t], j)),
                  pl.BlockSpec((tq, 128 * NH), lambda j, t, qb, kb: (kb[t], j)),
                  pl.BlockSpec((128 * NH, tq), lambda j, t, qb, kb: (j, kb[t])),
                  pl.BlockSpec((tq, 64 * NH), lambda j, t, qb, kb: (kb[t], j)),
                  pl.BlockSpec((tq, 64 * NH), lambda j, t, qb, kb: (qb[t], j)),
                  pl.BlockSpec((64 * NH, tq), lambda j, t, qb, kb: (j, qb[t])),
                  pl.BlockSpec((None, NH, tq), lambda j, t, qb, kb: (j, 0, qb[t])),
                  pl.BlockSpec((None, NH, tq), lambda j, t, qb, kb: (j, 0, qb[t]))] + [_ANY] * n,
        out_specs=[pl.BlockSpec((nq, 128 * NH, tq), lambda j, t, qb, kb: (0, j, 0), pipeline_mode=pl.Buffered(1)),
                   pl.BlockSpec((tq, 128 * NH), lambda j, t, qb, kb: (kb[t], j)),
                   pl.BlockSpec((tq, 64 * NH), lambda j, t, qb, kb: (kb[t], j))] + [_ANY] * n,
        scratch_shapes=[pltpu.VMEM((tq, 128 * NH), F32), pltpu.VMEM((tq, 64 * NH), F32), pltpu.VMEM((NH, tq, tq), F32),
                        pltpu.VMEM((NH, tq, tq), F32), pltpu.VMEM((NH, tq, tq), BF16), pltpu.VMEM((NH, tq, tq), BF16)]
        + (_scatter_sems(n) if n else []),
    )
    dq, dk, dv, *parts = pl.pallas_call(
        body, name="flash_bwd", grid_spec=grid_spec,
        out_shape=[_sds((nq, 1024, tq), F32), _sds((S, 1024), BF16), _sds((S, 512), BF16)] + _scatter_out_shapes(sums),
        compiler_params=_cp(("arbitrary", "arbitrary")),
    )(qb_of, kb_of, qp, kp, kt, v, do, dot, lse.reshape(MLA_HEADS // NH, NH, S), delta.reshape(MLA_HEADS // NH, NH, S),
      *sums)
    return dq, dk, dv, parts


def _mla_up_bwd(dqp, dkp, dv, cq, ckv, gq, gkv, w_uq, w_ukv, tabs, S):
    tm = min(512, S)

    def body(dq_ref, dk_ref, dv_ref, cq_ref, ckv_ref, gq_ref, gkv_ref, wuq_ref, wukv_ref, cm_ref, sa_ref, sb_ref,
             dqh_ref, dkv_ref, dcq_ref, dckv_ref, dkr_ref, dgq_ref, dgkv_ref):
        @pl.when(pl.program_id(0) == 0)
        def _():
            dgq_ref[...] = jnp.zeros(dgq_ref.shape, F32)
            dgkv_ref[...] = jnp.zeros(dgkv_ref.shape, F32)

        cm = cm_ref[...]
        sa = sa_ref[...]
        sb = sb_ref[...]
        lane = lax.broadcasted_iota(jnp.int32, (tm, 128), 1)
        dkr_r = jnp.zeros((tm, 128), F32)
        for h in range(MLA_HEADS):
            sl = slice(h * 128, (h + 1) * 128)
            dqh_ref[:, sl] = (_unrope_mla(dq_ref[sl, :].T, cm, sa, sb) * SCALE_MLA).astype(BF16)
            gk = dk_ref[:, sl]
            dkr_r = dkr_r + gk.astype(F32)
            dkv_ref[:, sl] = gk
        dkr_r = jnp.where((lane >= 64) & (lane < 96), dkr_r, 0.0)
        dkr_ref[...] = _unrope_mla(dkr_r, cm, sa, sb).astype(BF16)
        dkv_ref[:, 1024:1536] = dv_ref[...]
        dcq, ga = _rms_bwd(_dot_nt(dqh_ref[...], wuq_ref[...]), cq_ref[...], gq_ref[...])
        dcq_ref[...] = dcq.astype(BF16)
        dgq_ref[...] += _colsum(ga)
        dckv, gb = _rms_bwd(_dot_nt(dkv_ref[...], wukv_ref[...]), ckv_ref[...], gkv_ref[...])
        dckv_ref[...] = dckv.astype(BF16)
        dgkv_ref[...] += _colsum(gb)

    per_q = dqp.shape[2] // tm
    return pl.pallas_call(
        body, name="mla_up_bwd", grid=(S // tm,),
        in_specs=[pl.BlockSpec((None, 1024, tm), lambda i: (i // per_q, 0, i % per_q)),
                  _rows(tm, 1024), _rows(tm, 512), _rows(tm, Q_LORA), _rows(tm, KV_LORA),
                  _full(1, Q_LORA), _full(1, KV_LORA), _full(Q_LORA, 1024), _full(KV_LORA, 1536)] + [_rows(tm, 128)] * 3,
        out_specs=[_rows(tm, 1024), _rows(tm, 1536), _rows(tm, Q_LORA), _rows(tm, KV_LORA), _rows(tm, 128),
                   _acc(1, Q_LORA), _acc(1, KV_LORA)],
        out_shape=[_sds((S, 1024), BF16), _sds((S, 1536), BF16), _sds((S, Q_LORA), BF16), _sds((S, KV_LORA), BF16),
                   _sds((S, 128), BF16), _sds((1, Q_LORA), F32), _sds((1, KV_LORA), F32)],
        compiler_params=_cp(("arbitrary",)),
    )(dqp, dkp, dv, cq, ckv, gq, gkv, w_uq, w_ukv, *tabs[2:])


def _ret_bwd(rq, rk, rv, rprev, ry, rg, dro, gn_w, tabs, S):
    C = RET_CHUNK
    N = S // C
    G = min(RET_GROUP_BWD, N)
    NB = N // G

    def body(lg_ref, q_ref, k_ref, v_ref, rp_ref, ry_ref, rg_ref, dro_ref, w_ref, cr_ref, sr_ref,
             drq_ref, drk_ref, drv_ref, drg_ref, dw_ref, g_sc):
        @pl.when(pl.program_id(1) == 0)
        def _():
            g_sc[...] = jnp.zeros(g_sc.shape, F32)
            dw_ref[...] = jnp.zeros(dw_ref.shape, F32)

        dmat, zeta, xi, g_chunk = _decay_terms(lg_ref)
        w = w_ref[...]
        gacc = g_sc[...]
        dw = jnp.zeros((1, 128), F32)
        for i in reversed(range(G)):
            rows = slice(i * C, (i + 1) * C)
            ry = ry_ref[rows, :]
            mu = jnp.mean(ry, axis=-1, keepdims=True)
            yc = ry - mu
            rstd = lax.rsqrt(jnp.mean(yc * yc, axis=-1, keepdims=True) + EPS)
            yh = yc * rstd
            g = rg_ref[rows, :]
            s = _sig(g)
            dout = dro_ref[rows, :].astype(F32)
            drg_ref[rows, :] = (dout * (yh * w) * (s * (1.0 + g * (1.0 - s)))).astype(BF16)
            dgn = dout * (g * s)
            dw = dw + _colsum(dgn * yh)
            dyh = dgn * w
            dry = rstd * (dyh - jnp.mean(dyh, axis=-1, keepdims=True) - yh * jnp.mean(dyh * yh, axis=-1, keepdims=True))
            do = dry.astype(BF16)

            q = q_ref[rows, :]
            k = k_ref[rows, :]
            v = v_ref[rows, :]
            gfut = gacc.astype(BF16)
            sc = (_dot_nt(q, k) * dmat).astype(BF16)
            dsc = (_dot_nt(do, v) * dmat).astype(BF16)
            dq = jnp.dot(dsc, k, preferred_element_type=F32) + _dot_nt(do, rp_ref[i]) * xi
            dk = _dot_tn(dsc, q) + _dot_nt(v, gfut) * zeta
            dv = _dot_tn(sc, do) + jnp.dot(k, gfut, preferred_element_type=F32) * zeta
            gacc = g_chunk * gacc + _dot_tn(q, xi * dry)
            cr = cr_ref[rows, :]
            sr = sr_ref[rows, :]
            drq_ref[rows, :] = _unrope_ret(dq, cr, sr).astype(BF16)
            drk_ref[rows, :] = _unrope_ret(dk * SCALE_RET, cr, sr).astype(BF16)
            drv_ref[rows, :] = dv.astype(BF16)
        g_sc[...] = gacc
        dw_ref[...] += dw

    blk = pl.BlockSpec((G * C, 128), lambda h, n: (NB - 1 - n, h))
    tab = pl.BlockSpec((G * C, 128), lambda h, n: (NB - 1 - n, 0))
    return pl.pallas_call(
        body, name="ret_bwd", grid=(RET_HEADS, NB),
        in_specs=[pl.BlockSpec((None, 8, 128), lambda h, n: (h, 0, 0)), blk, blk, blk,
                  pl.BlockSpec((G, 128, 128), lambda h, n: (h * NB + NB - 1 - n, 0, 0)), blk, blk, blk,
                  pl.BlockSpec((1, 128), lambda h, n: (0, h)), tab, tab],
        out_specs=[blk, blk, blk, blk, pl.BlockSpec((1, 128), lambda h, n: (0, h))],
        out_shape=[_sds((S, 512), BF16)] * 4 + [_sds((1, 512), F32)],
        scratch_shapes=[pltpu.VMEM((128, 128), F32)],
        compiler_params=_cp(("parallel", "arbitrary")),
    )(_decay_table(), rq, rk, rv, rprev, ry, rg, dro, gn_w, tabs[0], tabs[1])


def _inproj_bwd(drq, drk, drv, drg, dcq, dckv, dkr, w_in, dh1, x, g, S):
    tm = min(512, S)

    def body(drq_ref, drk_ref, drv_ref, drg_ref, dcq_ref, dckv_ref, dkr_ref, w_ref, dh1_ref, x_ref, g_ref,
             gx_ref, dproj_ref, dg_ref):
        @pl.when(pl.program_id(0) == 0)
        def _():
            dg_ref[...] = jnp.zeros(dg_ref.shape, F32)

        dproj_ref[:, 0:512] = drq_ref[...]
        dproj_ref[:, 512:1024] = drk_ref[...]
        dproj_ref[:, 1024:1536] = drv_ref[...]
        dproj_ref[:, 1536:2048] = drg_ref[...]
        dproj_ref[:, 2048:2432] = dcq_ref[...]
        dproj_ref[:, 2432:2688] = dckv_ref[...]
        dproj_ref[:, 2688:2816] = pltpu.roll(dkr_ref[...].astype(F32), 64, 1).astype(BF16)
        dx, ga = _rms_bwd(_dot_nt(dproj_ref[...], w_ref[...]), x_ref[...], g_ref[...])
        gx_ref[...] = dh1_ref[...] + dx
        dg_ref[...] += _colsum(ga)

    return pl.pallas_call(
        body, name="inproj_bwd", grid=(S // tm,),
        in_specs=[_rows(tm, 512)] * 4 + [_rows(tm, Q_LORA), _rows(tm, KV_LORA), _rows(tm, 128),
                                         _full(D_MODEL, IN_COLS_P), _rows(tm, D_MODEL), _rows(tm, D_MODEL),
                                         _full(1, D_MODEL)],
        out_specs=[_rows(tm, D_MODEL), _rows(tm, IN_COLS_P), _acc(1, D_MODEL)],
        out_shape=[_sds((S, D_MODEL), F32), _sds((S, IN_COLS_P), BF16), _sds((1, D_MODEL), F32)],
        compiler_params=_cp(("arbitrary",)),
    )(drq, drk, drv, drg, dcq, dckv, dkr, w_in, dh1, x, g)


def _pad_weights(w):
    w_in_p = jnp.pad(w["w_in"], ((0, 0), (0, IN_COLS_P - IN_COLS)))
    w_uq_p = jnp.pad(w["w_uq"].reshape(Q_LORA, MLA_HEADS, 96), ((0, 0), (0, 0), (0, 32))).reshape(Q_LORA, 1024)
    ukv = w["w_ukv"].reshape(KV_LORA, MLA_HEADS, 128)
    k_part = jnp.pad(ukv[:, :, :64], ((0, 0), (0, 0), (0, 64))).reshape(KV_LORA, 1024)
    w_ukv_p = jnp.concatenate([k_part, ukv[:, :, 64:].reshape(KV_LORA, 512)], axis=1)
    return w_in_p, w_uq_p, w_ukv_p


BIG_SPEC = {n: (r, c, ax) for n, r, c, ax in BIG}
COLUMN_MAJOR = ("w_in", "w_uq", "w_gate", "w_up")
GRAD_TRANSPOSED = ("w_gate", "w_up")
GATHER_FIRST = ("w_in", "w_uq", "w_ukv")
GATHER_LATE = tuple(n for n, _, _, _ in BIG if n not in GATHER_FIRST)
REDUCE_EARLY = ("w_ple_gate", "w_ple_proj", "w_down", "w_gate", "w_up", "w_o")
REDUCE_LAST = tuple(n for n, _, _, _ in BIG if n not in REDUCE_EARLY)


def _local_step(x, p, pos_f, tgt, w, sm, late_shards=None, c_idx=None):
    S = x.shape[0]
    spread = late_shards is not None
    w = dict(w)
    tabs, first = _rope_tables(pos_f, S, [late_shards[n] for n in GATHER_FIRST] if spread else ())
    for i, n in enumerate(GATHER_FIRST if spread else ()):
        w[n] = _from_chips(first[i], BIG_SPEC[n][2])
    w_in_p, w_uq_p, w_ukv_p = _pad_weights(w)
    if spread:
        w_in_p = jnp.concatenate([first[0][j] for j in range(N_CHIPS)]
                                 + [jnp.zeros((D_MODEL, IN_COLS_P - IN_COLS), BF16)], axis=1)

    xn, rq, rk, rv, rg, cq, ckv, kr = _inproj(x, sm["pre_mix_norm"], w_in_p, tabs, S)
    cqn, ckvn, qp, kp, v, kt, vt = _mla_up(cq, ckv, kr, sm["mla_q_norm"], sm["mla_kv_norm"], w_uq_p, w_ukv_p, tabs, S)
    mo, lse, gathered = _flash_fwd(qp, kp, vt, S, [late_shards[n] for n in GATHER_LATE] if spread else ())
    for i, n in enumerate(GATHER_LATE if spread else ()):
        w[n] = _from_chips(gathered[i], 0 if n in GRAD_TRANSPOSED else BIG_SPEC[n][2])
    if not spread:
        w.update({n: w[n].T for n in GRAD_TRANSPOSED})
    ry, ro, rprev = _ret_fwd(rq, rk, rv, rg, sm["ret_gn_w"], S)
    mix, h1, hn = _outproj(ro, mo, x, w["w_o"], sm["post_mix_norm"], sm["pre_ffn_norm"], S)
    dgate_f, dup_f, act = _ffn_up(hn, w["w_gate"], w["w_up"], S)
    ff, h2 = _ffn_down(act, w["w_down"], h1, sm["post_ffn_norm"], S)
    dz, dpe, dh2, h2b, loss_vec, d_ple_norm, d_b = _ple_loss(
        p, h2, tgt, w["w_ple_proj"], w["w_ple_gate"], sm["b_ple_gate"], sm["ple_norm"], S)

    gw = {}
    gs = {"ple_norm": d_ple_norm, "b_ple_gate": d_b}
    gw["w_ple_gate"] = _wgrad(h2b, dz, "wgrad_ple_gate", S)
    gw["w_ple_proj"] = _wgrad(p, dpe, "wgrad_ple_proj", S)
    dff, dgate, dup, gs["post_ffn_norm"] = _ffn_down_bwd(dh2, ff, sm["post_ffn_norm"], w["w_down"], dgate_f, dup_f, S)
    gw["w_down"] = _wgrad(act, dff, "wgrad_down", S)
    if spread:
        gw["w_gate"] = _wgrad(dgate, hn, "wgrad_gate", S)
        gw["w_up"] = _wgrad(dup, hn, "wgrad_up", S)
    else:
        gw["w_gate"] = _wgrad(hn, dgate, "wgrad_gate", S)
        gw["w_up"] = _wgrad(hn, dup, "wgrad_up", S)
    first = REDUCE_EARLY[:-1]
    g4 = [_by_chip(gw.pop(n), *((D_FF, D_MODEL, 0) if n in GRAD_TRANSPOSED else BIG_SPEC[n]))
          for n in first] if spread else []
    dh1, dmix, dro, dmo, gs["pre_ffn_norm"], gs["post_mix_norm"], got = _ffn_up_bwd(
        dgate, dup, w["w_gate"], w["w_up"], h1, mix, dh2, sm["pre_ffn_norm"], sm["post_mix_norm"], w["w_o"], S, g4)
    gw["w_o"] = jnp.concatenate([_wgrad(ro, dmix, "wgrad_o_ret", S), _wgrad(mo, dmix, "wgrad_o_mla", S)], axis=0)
    g4_o = [_by_chip(gw.pop("w_o"), *BIG_SPEC["w_o"])] if spread else []

    dmo_t, delta, got_o = _attn_delta(mo, dmo, S, g4_o)
    sums = [_add_half_rows(a, b, c_idx, "rs_add_halves_" + n)
            for n, a, b in zip(REDUCE_EARLY, g4 + g4_o, list(got) + list(got_o))] if spread else []
    dqp, dkp, dv, parts = _flash_bwd(qp, kp, kt, v, dmo, dmo_t, lse, delta, S, sums)
    dqh, dkv, dcq, dckv, dkr, gs["mla_q_norm"], gs["mla_kv_norm"] = _mla_up_bwd(
        dqp, dkp, dv, cq, ckv, sm["mla_q_norm"], sm["mla_kv_norm"], w_uq_p, w_ukv_p, tabs, S)
    g_uq_p = _wgrad(cqn, dqh, "wgrad_uq", S)
    g_ukv_p = _wgrad(ckvn, dkv, "wgrad_ukv", S)
    gw["w_uq"] = g_uq_p.reshape(Q_LORA, MLA_HEADS, 128)[:, :, :96].reshape(Q_LORA, 768)
    gw["w_ukv"] = jnp.concatenate(
        [g_ukv_p[:, :1024].reshape(KV_LORA, MLA_HEADS, 128)[:, :, :64], g_ukv_p[:, 1024:].reshape(KV_LORA, MLA_HEADS, 64)],
        axis=2).reshape(KV_LORA, 1024)

    drq, drk, drv, drg, gs["ret_gn_w"] = _ret_bwd(rq, rk, rv, rprev, ry, rg, dro, sm["ret_gn_w"], tabs, S)
    grad_x, dproj, gs["pre_mix_norm"] = _inproj_bwd(drq, drk, drv, drg, dcq, dckv, dkr, w_in_p, dh1, x,
                                                    sm["pre_mix_norm"], S)
    if spread:
        gw["w_in"] = _wgrad(dproj, xn, "wgrad_in", S)[:IN_COLS]
    else:
        gw["w_in"] = _wgrad(xn, dproj, "wgrad_in", S)[:, :IN_COLS]
    return loss_vec, grad_x, gw, gs, ((sums, parts) if spread else None)


def _my_place():
    x = lax.axis_index("x")
    y = lax.axis_index("y")
    c = lax.axis_index("c")
    return x, y, c


def _other_chips(x, y):
    return [(1 - x, y), (x, 1 - y), (1 - x, 1 - y)]


_ANY = pl.BlockSpec(memory_space=pl.ANY)


def _small_copies(v_ref, slots, sems):
    send, recv, lsem = sems
    x, y, c = _my_place()
    me = 4 * x + 2 * y + c
    cps = [pltpu.make_async_copy(v_ref, slots.at[me], lsem)]
    for r in range(1, N_DEV):
        peer = (x ^ (r >> 2), y ^ ((r >> 1) & 1), c ^ (r & 1))
        cps.append(pltpu.make_async_remote_copy(
            src_ref=v_ref, dst_ref=slots.at[me], send_sem=send.at[r - 1], recv_sem=recv.at[r - 1],
            device_id=peer, device_id_type=MESH))
    return cps


def _small_sum(slots, out_ref):
    acc = slots[0]
    for d in range(1, N_DEV):
        acc = acc + slots[d]
    out_ref[...] = acc
    loss = jnp.sum(acc[9:10, :], axis=1, keepdims=True) * (0.5 / D_MODEL)
    out_ref[9:10, :] = jnp.broadcast_to(loss, (1, PACK_COLS))


def _small_scratch():
    return [pltpu.VMEM((N_DEV, SMALL_ROWS, PACK_COLS), F32), pltpu.SemaphoreType.DMA((N_DEV - 1,)),
            pltpu.SemaphoreType.DMA((N_DEV - 1,)), pltpu.SemaphoreType.DMA]


N_BIG = len(BIG)


def _half(c, rows, align):
    h = rows // 2
    return pl.ds(pl.multiple_of(c * h, align), h)


def _gather_out_shapes(shards):
    return [_sds((N_CHIPS,) + tuple(s.shape), BF16) for s in shards]


def _gather_sems(n):
    return [pltpu.SemaphoreType.DMA((n, 3))] * 4 + [pltpu.SemaphoreType.DMA((n,))] * 2


def _gather_phase(phase, ins, outs, sems):
    send1, recv1, send2, recv2, send3, recv3 = sems
    x, y, c = _my_place()
    me = 2 * x + y
    chips = _other_chips(x, y)
    sib = (x, y, 1 - c)
    for t in range(len(ins)):
        rows = ins[t].shape[0]
        half = _half(c, rows, 16)
        other = _half(1 - c, rows, 16)
        def own():
            return pltpu.make_async_remote_copy(
                src_ref=ins[t], dst_ref=outs[t].at[me], send_sem=send3.at[t], recv_sem=recv3.at[t],
                device_id=sib, device_id_type=MESH)

        if phase == 0:
            own().start()
        if phase == 2:
            own().wait()
        for k, (cx, cy) in enumerate(chips):
            src = 2 * cx + cy

            def over_ici(slab):
                return pltpu.make_async_remote_copy(
                    src_ref=ins[t].at[half], dst_ref=outs[t].at[slab, half], send_sem=send1.at[t, k],
                    recv_sem=recv1.at[t, k], device_id=(cx, cy, c), device_id_type=MESH)

            def over_d2d(rows):
                return pltpu.make_async_remote_copy(
                    src_ref=outs[t].at[src, rows], dst_ref=outs[t].at[src, rows], send_sem=send2.at[t, k],
                    recv_sem=recv2.at[t, k], device_id=sib, device_id_type=MESH)

            if phase == 0:
                over_ici(me).start()
            if phase == 1:
                over_ici(src).wait_recv()
                over_d2d(half).start()
            if phase == 2:
                over_d2d(other).wait_recv()
                over_ici(me).wait_send()
                over_d2d(half).wait_send()


def _swap_copies(ins, outs, sems):
    send, recv = sems
    x, y, c = _my_place()
    return [pltpu.make_async_remote_copy(
        src_ref=ins[t].at[:, _half(1 - c, ins[t].shape[1], 8)], dst_ref=outs[t], send_sem=send.at[t],
        recv_sem=recv.at[t], device_id=(x, y, 1 - c), device_id_type=MESH) for t in range(len(ins))]


def _swap_out_shapes(gs):
    return [_sds((N_CHIPS, g.shape[1] // 2, g.shape[2]), F32) for g in gs]


def _swap_sems(n):
    return [pltpu.SemaphoreType.DMA((n,)), pltpu.SemaphoreType.DMA((n,))]


def _swap_half_rows(gs):
    n = len(gs)

    def body(*refs):
        cps = _swap_copies(refs[:n], refs[n:2 * n], refs[2 * n:])
        for cp in cps:
            cp.start()
        for cp in cps:
            cp.wait()

    return pl.pallas_call(
        body, name="rs_swap_halves",
        in_specs=[_ANY] * n, out_specs=[_ANY] * n, out_shape=_swap_out_shapes(gs), scratch_shapes=_swap_sems(n),
    )(*gs)


def _add_half_rows(g, got, c_idx, name):
    _, rows, cols = g.shape
    h = rows // 2

    def body(c_ref, a_ref, b_ref, o_ref):
        o_ref[...] = (a_ref[...] + b_ref[...]).astype(BF16)

    grid_spec = pltpu.PrefetchScalarGridSpec(
        num_scalar_prefetch=1, grid=(N_CHIPS,),
        in_specs=[pl.BlockSpec((None, h, cols), lambda j, c: (j, c[0], 0)),
                  pl.BlockSpec((None, h, cols), lambda j, c: (j, 0, 0))],
        out_specs=pl.BlockSpec((None, h, cols), lambda j, c: (j, 0, 0)),
    )
    return pl.pallas_call(
        body, name=name, grid_spec=grid_spec, out_shape=_sds((N_CHIPS, h, cols), BF16),
        compiler_params=_cp(("parallel",)),
    )(c_idx, g, got)


def _scatter_to_chips(ts, vec):
    n = len(ts)

    def body(*refs):
        ins, v_ref, outs, small_ref = refs[:n], refs[n], refs[n + 1:2 * n + 1], refs[2 * n + 1]
        slots, small_sems, sems = refs[2 * n + 2], refs[2 * n + 3:2 * n + 6], refs[2 * n + 6:]
        small = _small_copies(v_ref, slots, small_sems)
        cps = _scatter_copies(ins, outs, sems)
        for cp in small + cps:
            cp.start()
        for cp in small:
            cp.wait()
        _small_sum(slots, small_ref)
        for cp in cps:
            cp.wait()

    vm = pl.BlockSpec(memory_space=pltpu.VMEM)
    *parts, small_sum = pl.pallas_call(
        body, name="rs_scatter_chips",
        in_specs=[_ANY] * n + [vm], out_specs=[_ANY] * n + [vm],
        out_shape=_scatter_out_shapes(ts) + [_sds((SMALL_ROWS, PACK_COLS), F32)],
        scratch_shapes=_small_scratch() + _scatter_sems(n),
    )(*ts, vec)
    return parts, small_sum


def _scatter_copies(ins, outs, sems):
    send, recv = sems
    x, y, c = _my_place()
    return [pltpu.make_async_remote_copy(
        src_ref=ins[t].at[2 * cx + cy], dst_ref=outs[t].at[k], send_sem=send.at[t, k], recv_sem=recv.at[t, k],
        device_id=(cx, cy, c), device_id_type=MESH)
        for t in range(len(ins)) for k, (cx, cy) in enumerate(_other_chips(x, y))]


def _scatter_out_shapes(ts):
    return [_sds((3,) + tuple(t.shape[1:]), BF16) for t in ts]


def _scatter_sems(n):
    return [pltpu.SemaphoreType.DMA((n, 3)), pltpu.SemaphoreType.DMA((n, 3))]


def _add_four(mines, parts, place):
    n = len(mines)

    def body(pl_ref, *refs):
        for t in range(n):
            m_ref, p_ref, o_ref = refs[t], refs[n + t], refs[2 * n + t]
            o_ref[...] = ((m_ref[...].astype(F32) + p_ref[0].astype(F32)) + p_ref[1].astype(F32)) + p_ref[2].astype(F32)

    shapes = [p.shape[1:] for p in parts]
    grid_spec = pltpu.PrefetchScalarGridSpec(
        num_scalar_prefetch=1, grid=(1,),
        in_specs=[pl.BlockSpec((None,) + hc, lambda i, pc: (pc[0], 0, 0)) for hc in shapes]
        + [pl.BlockSpec((3,) + hc, lambda i, pc: (0, 0, 0)) for hc in shapes],
        out_specs=[pl.BlockSpec(hc, lambda i, pc: (pc[1], 0)) for hc in shapes],
    )
    return pl.pallas_call(
        body, name="rs_add_chips", grid_spec=grid_spec, out_shape=[_sds((2 * h, c), F32) for h, c in shapes],
        compiler_params=_cp(("arbitrary",)),
    )(place, *mines, *parts)


def _join_half_rows(rs):
    n = len(rs)

    def body(*refs):
        ins, outs = refs[:n], refs[n:2 * n]
        send, recv = refs[2 * n:]
        x, y, c = _my_place()
        cps = []
        for t in range(n):
            half = _half(c, outs[t].shape[0], 8)
            rc = pltpu.make_async_remote_copy(
                src_ref=ins[t].at[half], dst_ref=outs[t].at[half], send_sem=send.at[t], recv_sem=recv.at[t],
                device_id=(x, y, 1 - c), device_id_type=MESH)
            rc.start()
            cps.append(rc)
        for cp in cps:
            cp.wait()

    return pl.pallas_call(
        body, name="rs_join_halves",
        in_specs=[_ANY] * n, out_specs=[_ANY] * n,
        out_shape=[_sds(r.shape, F32) for r in rs],
        input_output_aliases={i: i for i in range(n)},
        scratch_shapes=[pltpu.SemaphoreType.DMA((n,))] * 2,
    )(*rs)


def _by_chip(full, rows, cols, axis):
    if axis == 0:
        return full.reshape(N_CHIPS, rows // N_CHIPS, cols)
    return full.reshape(rows, N_CHIPS, cols // N_CHIPS).transpose(1, 0, 2)


def _from_chips(parts, axis):
    _, r, c = parts.shape
    if axis == 0:
        return parts.reshape(N_CHIPS * r, c)
    return parts.transpose(1, 0, 2).reshape(r, N_CHIPS * c)


def _adamw(wt, g, m, v, name):
    _, R, C = wt.shape
    tr = max(d for d in range(8, R + 1, 8) if R % d == 0 and (d * C <= 256 * 1024 or d == 8))

    def body(w_ref, g_ref, m_ref, v_ref, d_ref, nm_ref, nv_ref):
        gg = g_ref[...]
        m_new = ADAM_B1 * m_ref[...] + (1.0 - ADAM_B1) * gg
        v_new = ADAM_B2 * v_ref[...] + (1.0 - ADAM_B2) * (gg * gg)
        m_hat = m_new / (1.0 - ADAM_B1 ** ADAM_STEP)
        v_hat = v_new / (1.0 - ADAM_B2 ** ADAM_STEP)
        d_ref[...] = -ADAM_LR * (m_hat / (jnp.sqrt(v_hat) + ADAM_EPS) + ADAM_WD * w_ref[...])
        nm_ref[...] = m_new
        nv_ref[...] = v_new

    spec = pl.BlockSpec((None, tr, C), lambda i: (0, i, 0))
    return pl.pallas_call(
        body, name=name, grid=(R // tr,), in_specs=[spec, pl.BlockSpec((tr, C), lambda i: (i, 0)), spec, spec],
        out_specs=[spec] * 3, out_shape=[_sds((1, R, C), F32)] * 3,
        compiler_params=_cp(("parallel",)),
    )(wt, g, m, v)


def _pack_small(vals, loss_vec=None):
    rows = [jnp.pad(vals[n].reshape(-1), (0, PACK_COLS - sz)) for n, sz in SMALL]
    rows.append(loss_vec.reshape(-1) if loss_vec is not None else jnp.zeros((PACK_COLS,), F32))
    rows += [jnp.zeros((PACK_COLS,), F32)] * (SMALL_ROWS - len(rows))
    return jnp.stack(rows)


def kernel(x, p, positions, pre_mix_norm, w_in, ret_gn_w, mla_q_norm, w_uq, mla_kv_norm, w_ukv, w_o, post_mix_norm, pre_ffn_norm, w_gate, w_up, w_down, post_ffn_norm, w_ple_proj, ple_norm, w_ple_gate, b_ple_gate, loss_target, m_pre_mix_norm, m_w_in, m_ret_gn_w, m_mla_q_norm, m_w_uq, m_mla_kv_norm, m_w_ukv, m_w_o, m_post_mix_norm, m_pre_ffn_norm, m_w_gate, m_w_up, m_w_down, m_post_ffn_norm, m_w_ple_proj, m_ple_norm, m_w_ple_gate, m_b_ple_gate, v_pre_mix_norm, v_w_in, v_ret_gn_w, v_mla_q_norm, v_w_uq, v_mla_kv_norm, v_w_ukv, v_w_o, v_post_mix_norm, v_pre_ffn_norm, v_w_gate, v_w_up, v_w_down, v_post_ffn_norm, v_w_ple_proj, v_ple_norm, v_w_ple_gate, v_b_ple_gate):
    wts = dict(pre_mix_norm=pre_mix_norm, w_in=w_in, ret_gn_w=ret_gn_w, mla_q_norm=mla_q_norm, w_uq=w_uq,
               mla_kv_norm=mla_kv_norm, w_ukv=w_ukv, w_o=w_o, post_mix_norm=post_mix_norm, pre_ffn_norm=pre_ffn_norm,
               w_gate=w_gate, w_up=w_up, w_down=w_down, post_ffn_norm=post_ffn_norm, w_ple_proj=w_ple_proj,
               ple_norm=ple_norm, w_ple_gate=w_ple_gate, b_ple_gate=b_ple_gate)
    mom = dict(pre_mix_norm=m_pre_mix_norm, w_in=m_w_in, ret_gn_w=m_ret_gn_w, mla_q_norm=m_mla_q_norm, w_uq=m_w_uq,
               mla_kv_norm=m_mla_kv_norm, w_ukv=m_w_ukv, w_o=m_w_o, post_mix_norm=m_post_mix_norm,
               pre_ffn_norm=m_pre_ffn_norm, w_gate=m_w_gate, w_up=m_w_up, w_down=m_w_down, post_ffn_norm=m_post_ffn_norm,
               w_ple_proj=m_w_ple_proj, ple_norm=m_ple_norm, w_ple_gate=m_w_ple_gate, b_ple_gate=m_b_ple_gate)
    var = dict(pre_mix_norm=v_pre_mix_norm, w_in=v_w_in, ret_gn_w=v_ret_gn_w, mla_q_norm=v_mla_q_norm, w_uq=v_w_uq,
               mla_kv_norm=v_mla_kv_norm, w_ukv=v_w_ukv, w_o=v_w_o, post_mix_norm=v_post_mix_norm,
               pre_ffn_norm=v_pre_ffn_norm, w_gate=v_w_gate, w_up=v_w_up, w_down=v_w_down, post_ffn_norm=v_post_ffn_norm,
               w_ple_proj=v_w_ple_proj, ple_norm=v_ple_norm, w_ple_gate=v_w_ple_gate, b_ple_gate=v_b_ple_gate)

    S = x.shape[1]
    shard2d = {n: wts[n][0] for n, _, _, _ in BIG}
    small2d = {n: wts[n] for n, _ in SMALL}

    shard_bf = {n: (jnp.swapaxes(wts[n], 1, 2)[0] if n in GRAD_TRANSPOSED else shard2d[n]).astype(BF16) for n in shard2d}
    pos_f = positions.astype(F32).reshape(S, 1)
    c_idx = lax.axis_index("c").astype(jnp.int32).reshape(1)
    loss_vec, grad_x, gw, gs, (sums_early, parts_early) = _local_step(
        x[0], p[0, 0], pos_f, loss_target[0], {}, small2d, shard_bf, c_idx)

    g4 = [_by_chip(gw[n], *BIG_SPEC[n]) for n in REDUCE_LAST if n != "w_in"]
    g4.insert(REDUCE_LAST.index("w_in"), jnp.pad(gw["w_in"].reshape(N_CHIPS, IN_SHARD, D_MODEL),
                                                 ((0, 0), (0, IN_SHARD_P - IN_SHARD), (0, 0))))
    got = _swap_half_rows(g4)
    sums_last = [_add_half_rows(g4[i], got[i], c_idx, "rs_add_halves_" + n) for i, n in enumerate(REDUCE_LAST)]
    parts_last, small_sum = _scatter_to_chips(sums_last, _pack_small(gs, loss_vec))
    place = jnp.stack([2 * lax.axis_index("x") + lax.axis_index("y"), lax.axis_index("c")]).astype(jnp.int32)
    names = REDUCE_EARLY + REDUCE_LAST
    reduced = _join_half_rows(_add_four(sums_early + sums_last, list(parts_early) + list(parts_last), place))
    g_shard = dict(zip(names, reduced))

    loss = small_sum[9, 0]
    g_small = {n: small_sum[i:i + 1, :sz] for i, (n, sz) in enumerate(SMALL)}

    grads, delta, new_m, new_v = {}, {}, {}, {}
    for n, _, _, _ in BIG:
        if n in COLUMN_MAJOR:
            turn = lambda a: jnp.swapaxes(a, 1, 2)
            g_t = g_shard[n][:IN_SHARD] if n == "w_in" else g_shard[n] if n in GRAD_TRANSPOSED else g_shard[n].T
            d, nm, nv = _adamw(turn(wts[n]), g_t, turn(mom[n]), turn(var[n]), "adamw_" + n)
            grads[n], delta[n], new_m[n], new_v[n] = turn(g_t[None]), turn(d), turn(nm), turn(nv)
        else:
            delta[n], new_m[n], new_v[n] = _adamw(wts[n], g_shard[n], mom[n], var[n], "adamw_" + n)
            grads[n] = g_shard[n][None]
    d, nm, nv = _adamw(_pack_small(small2d)[None], small_sum, _pack_small(mom)[None], _pack_small(var)[None],
                       "adamw_small")
    for i, (n, sz) in enumerate(SMALL):
        grads[n] = g_small[n]
        delta[n], new_m[n], new_v[n] = d[0, i:i + 1, :sz], nm[0, i:i + 1, :sz], nv[0, i:i + 1, :sz]

    return (loss, grad_x[None], *[grads[n] for n in ALL_W], *[delta[n] for n in ALL_W],
            *[new_m[n] for n in ALL_W], *[new_v[n] for n in ALL_W])
```

```python
import functools
import math

import jax
import jax.numpy as jnp
import numpy as np
from jax import lax
from jax.experimental import pallas as pl
from jax.experimental.pallas import tpu as pltpu

F32 = jnp.float32
BF16 = jnp.bfloat16
MESH = pl.DeviceIdType.MESH

D_MODEL = 1024
D_FF = 2816
PLE_DIM = 256
RET_HEADS = 4
RET_DIM = 128
RET_WIDTH = 512
RET_CHUNK = 256
RET_GROUP_FWD = 16
RET_GROUP_BWD = 8
MLA_HEADS = 8
MLA_NOPE = 64
MLA_ROPE = 32
MLA_V = 64
Q_LORA = 384
KV_LORA = 256
IN_COLS = 2720
IN_COLS_P = 2816
IN_SHARD = IN_COLS // 4
IN_SHARD_P = 688
ROPE_BASE = 10000.0
EPS = 1e-6
SCALE_MLA = 1.0 / math.sqrt(MLA_NOPE + MLA_ROPE)
SCALE_RET = RET_DIM ** -0.5
NEG = -1e30

ADAM_LR = 0.001
ADAM_B1 = 0.9
ADAM_B2 = 0.999
ADAM_EPS = 1e-08
ADAM_WD = 0.01
ADAM_STEP = 10

N_CHIPS = 4
N_DEV = 8
VMEM_MB = 56

BIG = (
    ("w_in", 1024, 2720, 1),
    ("w_uq", 384, 768, 1),
    ("w_ukv", 256, 1024, 1),
    ("w_o", 1024, 1024, 0),
    ("w_gate", 1024, 2816, 1),
    ("w_up", 1024, 2816, 1),
    ("w_down", 2816, 1024, 0),
    ("w_ple_proj", 256, 1024, 1),
    ("w_ple_gate", 1024, 1024, 0),
)
SMALL = (
    ("pre_mix_norm", 1024),
    ("ret_gn_w", 512),
    ("mla_q_norm", 384),
    ("mla_kv_norm", 256),
    ("post_mix_norm", 1024),
    ("pre_ffn_norm", 1024),
    ("post_ffn_norm", 1024),
    ("ple_norm", 1024),
    ("b_ple_gate", 1024),
)
ALL_W = ("pre_mix_norm", "w_in", "ret_gn_w", "mla_q_norm", "w_uq", "mla_kv_norm", "w_ukv", "w_o", "post_mix_norm",
         "pre_ffn_norm", "w_gate", "w_up", "w_down", "post_ffn_norm", "w_ple_proj", "ple_norm", "w_ple_gate", "b_ple_gate")
PACK_COLS = 1024
SMALL_ROWS = 16


def _cp(sem=None, mb=VMEM_MB, **kw):
    return pltpu.CompilerParams(dimension_semantics=sem, vmem_limit_bytes=mb * 1024 * 1024, **kw)


def _bf(x):
    return x.astype(BF16)


def _dot(a, b):
    return jnp.dot(_bf(a), _bf(b), preferred_element_type=F32)


def _dot_nt(a, b):
    return lax.dot_general(_bf(a), _bf(b), (((1,), (1,)), ((), ())), preferred_element_type=F32)


def _dot_tn(a, b):
    return lax.dot_general(_bf(a), _bf(b), (((0,), (0,)), ((), ())), preferred_element_type=F32)


def _sig(x):
    return 1.0 / (1.0 + jnp.exp(-x))


def _rms(x, g):
    r = lax.rsqrt(jnp.mean(x * x, axis=-1, keepdims=True) + EPS)
    return x * r * g


def _rms_bwd(dy, x, g):
    r = lax.rsqrt(jnp.mean(x * x, axis=-1, keepdims=True) + EPS)
    xh = x * r
    dxh = dy * g
    dx = r * (dxh - xh * jnp.mean(dxh * xh, axis=-1, keepdims=True))
    return dx, dy * xh


def _colsum(x):
    return jnp.sum(x, axis=0, keepdims=True)


def _rope_ret(x, cr, sr):
    return x * cr + pltpu.roll(x, 64, 1) * sr


def _unrope_ret(dy, cr, sr):
    return dy * cr + pltpu.roll(dy * sr, 64, 1)


def _rope_mla(x, cm, sa, sb):
    return x * cm + pltpu.roll(x, 112, 1) * sa + pltpu.roll(x, 16, 1) * sb


def _unrope_mla(dy, cm, sa, sb):
    return dy * cm + pltpu.roll(dy * sa, 16, 1) + pltpu.roll(dy * sb, 112, 1)


def _rows(tm, w, col=0):
    return pl.BlockSpec((tm, w), lambda i: (i, col))


def _full(*shape):
    return pl.BlockSpec(shape, lambda i: (0,) * len(shape), pipeline_mode=pl.Buffered(1))


def _acc(*shape):
    return pl.BlockSpec(shape, lambda i: (0,) * len(shape))


def _sds(shape, dtype):
    return jax.ShapeDtypeStruct(shape, dtype)


def _rope_tables(pos_f, S, shards=()):
    tm = min(512, S)
    n = len(shards)
    steps = S // tm
    inv_r = (1.0 / (np.float32(ROPE_BASE) ** (np.arange(64, dtype=np.float32) / np.float32(64)))).astype(np.float32)
    inv_m16 = (1.0 / (np.float32(ROPE_BASE) ** (np.arange(16, dtype=np.float32) / np.float32(16)))).astype(np.float32)
    inv_r = np.concatenate([inv_r, inv_r])[None, :]
    inv_m = np.zeros((1, 128), np.float32)
    inv_m[0, 64:80] = inv_m16
    inv_m[0, 80:96] = inv_m16

    def body(pos_ref, invr_ref, invm_ref, *rest):
        w_ins, (cr_ref, sr_ref, cm_ref, sa_ref, sb_ref) = rest[:n], rest[n:n + 5]
        w_outs, sems = rest[n + 5:2 * n + 5], rest[2 * n + 5:]
        i = pl.program_id(0)
        if n:
            @pl.when(i == 0)
            def _():
                _gather_phase(0, w_ins, w_outs, sems)

            @pl.when(i == steps - 1)
            def _():
                _gather_phase(1, w_ins, w_outs, sems)

        pos = pos_ref[...]
        lane = lax.broadcasted_iota(jnp.int32, (tm, 128), 1)
        ar = pos * invr_ref[...]
        s = jnp.sin(ar)
        cr_ref[...] = jnp.cos(ar)
        sr_ref[...] = jnp.where(lane < 64, -s, s)
        am = pos * invm_ref[...]
        c2 = jnp.cos(am)
        s2 = jnp.sin(am)
        cm_ref[...] = jnp.where(lane < 64, 1.0, jnp.where(lane < 96, c2, 0.0))
        sa_ref[...] = jnp.where((lane >= 64) & (lane < 80), -s2, 0.0)
        sb_ref[...] = jnp.where((lane >= 80) & (lane < 96), s2, 0.0)

        if n:
            @pl.when(i == steps - 1)
            def _():
                _gather_phase(2, w_ins, w_outs, sems)

    outs = pl.pallas_call(
        body, name="rope_tables", grid=(steps,),
        in_specs=[_rows(tm, 1), _full(1, 128), _full(1, 128)] + [_ANY] * n,
        out_specs=[_rows(tm, 128)] * 5 + [_ANY] * n,
        out_shape=[_sds((S, 128), F32)] * 5 + _gather_out_shapes(shards),
        scratch_shapes=_gather_sems(n) if n else [],
        compiler_params=_cp(("arbitrary",)),
    )(pos_f, jnp.asarray(inv_r), jnp.asarray(inv_m), *shards)
    return outs[:5], outs[5:]


def _inproj(x, g, w_in, tabs, S):
    tm = min(512, S)

    def body(x_ref, g_ref, w_ref, cr_ref, sr_ref, cm_ref, sa_ref, sb_ref,
             xn_ref, rq_ref, rk_ref, rv_ref, rg_ref, cq_ref, ckv_ref, kr_ref):
        xb = _rms(x_ref[...], g_ref[...]).astype(BF16)
        xn_ref[...] = xb
        cr = cr_ref[...]
        sr = sr_ref[...]
        q = jnp.dot(xb, w_ref[:, 0:512], preferred_element_type=F32)
        k = jnp.dot(xb, w_ref[:, 512:1024], preferred_element_type=F32)
        for h in range(RET_HEADS):
            sl = slice(h * 128, (h + 1) * 128)
            rq_ref[:, sl] = _rope_ret(q[:, sl], cr, sr).astype(BF16)
            rk_ref[:, sl] = (_rope_ret(k[:, sl], cr, sr) * SCALE_RET).astype(BF16)
        rv_ref[...] = jnp.dot(xb, w_ref[:, 1024:1536], preferred_element_type=F32).astype(BF16)
        rg_ref[...] = jnp.dot(xb, w_ref[:, 1536:2048], preferred_element_type=F32)
        cq_ref[...] = jnp.dot(xb, w_ref[:, 2048:2432], preferred_element_type=F32)
        ckv_ref[...] = jnp.dot(xb, w_ref[:, 2432:2688], preferred_element_type=F32)
        kr = pltpu.roll(jnp.dot(xb, w_ref[:, 2688:2816], preferred_element_type=F32), 64, 1)
        kr_ref[...] = _rope_mla(kr, cm_ref[...], sa_ref[...], sb_ref[...])

    return pl.pallas_call(
        body, name="inproj", grid=(S // tm,),
        in_specs=[_rows(tm, D_MODEL), _full(1, D_MODEL), _full(D_MODEL, IN_COLS_P)] + [_rows(tm, 128)] * 5,
        out_specs=[_rows(tm, D_MODEL)] + [_rows(tm, 512)] * 4 + [_rows(tm, Q_LORA), _rows(tm, KV_LORA), _rows(tm, 128)],
        out_shape=[_sds((S, D_MODEL), BF16)] + [_sds((S, 512), BF16)] * 3
        + [_sds((S, 512), F32), _sds((S, Q_LORA), F32), _sds((S, KV_LORA), F32), _sds((S, 128), F32)],
        compiler_params=_cp(("parallel",)),
    )(x, g, w_in, *tabs)


def _mla_up(cq, ckv, kr, gq, gkv, w_uq, w_ukv, tabs, S):
    tm = min(512, S)

    def body(cq_ref, ckv_ref, kr_ref, gq_ref, gkv_ref, wuq_ref, wukv_ref, cm_ref, sa_ref, sb_ref,
             cqn_ref, ckvn_ref, qp_ref, kp_ref, v_ref, kt_ref, vt_ref):
        cm = cm_ref[...]
        sa = sa_ref[...]
        sb = sb_ref[...]
        cqn = _rms(cq_ref[...], gq_ref[...]).astype(BF16)
        cqn_ref[...] = cqn
        ckvn = _rms(ckv_ref[...], gkv_ref[...]).astype(BF16)
        ckvn_ref[...] = ckvn
        qh = jnp.dot(cqn, wuq_ref[...], preferred_element_type=F32)
        kv = jnp.dot(ckvn, wukv_ref[...], preferred_element_type=F32)
        kr_blk = kr_ref[...]
        for h in range(MLA_HEADS):
            sl = slice(h * 128, (h + 1) * 128)
            qp_ref[:, sl] = (_rope_mla(qh[:, sl], cm, sa, sb) * SCALE_MLA).astype(BF16)
            kh = kv[:, sl] + kr_blk
            kp_ref[:, sl] = kh.astype(BF16)
            kt_ref[sl, :] = kh.T.astype(BF16)
        for h in range(MLA_HEADS // 2):
            vh = kv[:, 1024 + h * 128:1024 + (h + 1) * 128]
            v_ref[:, h * 128:(h + 1) * 128] = vh.astype(BF16)
            vt_ref[h * 128:(h + 1) * 128, :] = vh.T.astype(BF16)

    cols = lambda r: pl.BlockSpec((r, tm), lambda i: (0, i))
    return pl.pallas_call(
        body, name="mla_up", grid=(S // tm,),
        in_specs=[_rows(tm, Q_LORA), _rows(tm, KV_LORA), _rows(tm, 128), _full(1, Q_LORA), _full(1, KV_LORA),
                  _full(Q_LORA, 1024), _full(KV_LORA, 1536)] + [_rows(tm, 128)] * 3,
        out_specs=[_rows(tm, Q_LORA), _rows(tm, KV_LORA), _rows(tm, 1024), _rows(tm, 1024), _rows(tm, 512),
                   cols(1024), cols(512)],
        out_shape=[_sds((S, Q_LORA), BF16), _sds((S, KV_LORA), BF16), _sds((S, 1024), BF16), _sds((S, 1024), BF16),
                   _sds((S, 512), BF16), _sds((1024, S), BF16), _sds((512, S), BF16)],
        compiler_params=_cp(("parallel",)),
    )(cq, ckv, kr, gq, gkv, w_uq, w_ukv, *tabs[2:])


def _tri_pairs(nq, k_major):
    if k_major:
        pairs = [(qb, kb) for kb in range(nq) for qb in range(kb, nq)]
    else:
        pairs = [(qb, kb) for qb in range(nq) for kb in range(qb + 1)]
    qb_of = np.array([p[0] for p in pairs], np.int32)
    kb_of = np.array([p[1] for p in pairs], np.int32)
    return jnp.asarray(qb_of), jnp.asarray(kb_of), len(pairs)


ATT_ROWS = 32
FWD_HEADS = 8
BWD_HEADS = 4


def _causal_keep(r0, rows, tq):
    key = r0 + lax.broadcasted_iota(jnp.int32, (rows, tq), 0)
    qry = lax.broadcasted_iota(jnp.int32, (rows, tq), 1)
    return key <= qry


def _flash_fwd(qp, kp, vt, S, shards=()):
    tq = min(512, S)
    nq = S // tq
    RB = ATT_ROWS
    NH = FWD_HEADS
    qb_of, kb_of, T = _tri_pairs(nq, k_major=False)
    n = len(shards)
    steps = (MLA_HEADS // NH) * T

    def body(qb_ref, kb_ref, q_ref, k_ref, vt_ref, *rest):
        w_ins, (o_ref, lse_ref), w_outs = rest[:n], rest[n:n + 2], rest[n + 2:2 * n + 2]
        m_sc, l_sc, acc_sc, s_sc, p_sc = rest[2 * n + 2:2 * n + 7]
        sems = rest[2 * n + 7:]
        t = pl.program_id(1)
        qb = qb_ref[t]
        kb = kb_ref[t]
        lin = pl.program_id(0) * T + t

        if n:
            @pl.when(lin == 0)
            def _():
                _gather_phase(0, w_ins, w_outs, sems)

            @pl.when(lin == steps // 2)
            def _():
                _gather_phase(1, w_ins, w_outs, sems)

        @pl.when(kb == 0)
        def _():
            m_sc[...] = jnp.full(m_sc.shape, NEG, F32)
            l_sc[...] = jnp.zeros(l_sc.shape, F32)
            acc_sc[...] = jnp.zeros(acc_sc.shape, F32)

        def scores(a):
            sl = slice(a * 128, (a + 1) * 128)
            s_sc[a] = _dot_nt(k_ref[:, sl], q_ref[:, sl])

        def step(masked):
            for a in range(NH):
                scores(a)
            for a in range(NH):
                mx = [jnp.full((8, tq), NEG, F32) for _ in range(RB // 8)]
                for r in range(0, tq, RB):
                    sc = s_sc[a, r:r + RB, :]
                    if masked:
                        sc = jnp.where(_causal_keep(r, RB, tq), sc, NEG)
                        s_sc[a, r:r + RB, :] = sc
                    for i in range(RB // 8):
                        mx[i] = jnp.maximum(mx[i], sc[i * 8:(i + 1) * 8, :])
                mx8 = functools.reduce(jnp.maximum, mx)
                m_prev = m_sc[a]
                m_new = jnp.maximum(m_prev, jnp.max(mx8, axis=0, keepdims=True))
                al = jnp.exp(m_prev - m_new)
                m_sc[a] = m_new
                ls = [jnp.zeros((8, tq), F32) for _ in range(RB // 8)]
                for r in range(0, tq, RB):
                    p = jnp.exp(s_sc[a, r:r + RB, :] - m_new)
                    for i in range(RB // 8):
                        ls[i] = ls[i] + p[i * 8:(i + 1) * 8, :]
                    p_sc[a, r:r + RB, :] = p.astype(BF16)
                l_sc[a] = al * l_sc[a] + jnp.sum(functools.reduce(jnp.add, ls), axis=0, keepdims=True)
                pair = slice((a // 2) * 128, (a // 2 + 1) * 128)
                pv = jnp.dot(vt_ref[pair, :], p_sc[a], preferred_element_type=F32)
                rs = slice(a * 64, (a + 1) * 64)
                own = slice((a % 2) * 64, (a % 2 + 1) * 64)
                acc_sc[rs, :] = acc_sc[rs, :] * al + pv[own, :]

        @pl.when(kb < qb)
        def _():
            step(False)

        @pl.when(kb == qb)
        def _():
            step(True)
            for a in range(NH):
                rs = slice(a * 64, (a + 1) * 64)
                acc_sc[rs, :] = acc_sc[rs, :] / l_sc[a]
                lse_ref[a:a + 1, :] = m_sc[a] + jnp.log(l_sc[a])
            o_ref[...] = acc_sc[...].T.astype(BF16)

        if n:
            @pl.when(lin == steps - 1)
            def _():
                _gather_phase(2, w_ins, w_outs, sems)

    grid_spec = pltpu.PrefetchScalarGridSpec(
        num_scalar_prefetch=2, grid=(MLA_HEADS // NH, T),
        in_specs=[pl.BlockSpec((tq, 128 * NH), lambda j, t, qb, kb: (qb[t], j)),
                  pl.BlockSpec((tq, 128 * NH), lambda j, t, qb, kb: (kb[t], j)),
                  pl.BlockSpec((64 * NH, tq), lambda j, t, qb, kb: (j, kb[t]))] + [_ANY] * n,
        out_specs=[pl.BlockSpec((tq, 64 * NH), lambda j, t, qb, kb: (qb[t], j)),
                   pl.BlockSpec((None, NH, tq), lambda j, t, qb, kb: (j, 0, qb[t]))] + [_ANY] * n,
        scratch_shapes=[pltpu.VMEM((NH, 1, tq), F32), pltpu.VMEM((NH, 1, tq), F32), pltpu.VMEM((64 * NH, tq), F32),
                        pltpu.VMEM((NH, tq, tq), F32), pltpu.VMEM((NH, tq, tq), BF16)] + (_gather_sems(n) if n else []),
    )
    out, lse, *gathered = pl.pallas_call(
        body, name="flash_fwd", grid_spec=grid_spec,
        out_shape=[_sds((S, 512), BF16), _sds((MLA_HEADS // NH, NH, S), F32)] + _gather_out_shapes(shards),
        compiler_params=_cp(("arbitrary", "arbitrary")),
    )(qb_of, kb_of, qp, kp, vt, *shards)
    return out, lse.reshape(MLA_HEADS // 2, 2, S), gathered


def _decay_table():
    log_g = np.log(1.0 - 2.0 ** (-5.0 - np.arange(RET_HEADS, dtype=np.float32))).astype(np.float32)
    return jnp.asarray(np.broadcast_to(log_g[:, None, None], (RET_HEADS, 8, 128)).copy())


def _decay_terms(lg_ref):
    C = RET_CHUNK
    lg = lg_ref[0:1, :]
    row = lax.broadcasted_iota(jnp.int32, (C, C), 0)
    col = lax.broadcasted_iota(jnp.int32, (C, C), 1)
    diff = (row - col).astype(F32)
    dmat = jnp.where(diff >= 0, jnp.exp(jnp.maximum(diff, 0.0) * jnp.tile(lg, (1, C // 128))), 0.0)
    j = lax.broadcasted_iota(jnp.int32, (C, 1), 0).astype(F32)
    lg1 = lg[:, 0:1]
    zeta = jnp.exp((C - 1 - j) * lg1)
    xi = jnp.exp((j + 1.0) * lg1)
    g_chunk = jnp.exp(C * lg1)
    return dmat, zeta, xi, g_chunk


def _ret_fwd(rq, rk, rv, rg, gn_w, S):
    C = RET_CHUNK
    N = S // C
    G = min(RET_GROUP_FWD, N)
    NB = N // G

    def body(lg_ref, q_ref, k_ref, v_ref, rg_ref, w_ref, ry_ref, ro_ref, rprev_ref, r_sc):
        @pl.when(pl.program_id(1) == 0)
        def _():
            r_sc[...] = jnp.zeros(r_sc.shape, F32)

        dmat, zeta, xi, g_chunk = _decay_terms(lg_ref)
        w = w_ref[...]
        r = r_sc[...]
        for i in range(G):
            rows = slice(i * C, (i + 1) * C)
            q = q_ref[rows, :]
            k = k_ref[rows, :]
            v = v_ref[rows, :]
            r_prev = r.astype(BF16)
            rprev_ref[i] = r_prev
            sc = _dot_nt(q, k) * dmat
            ry = _dot(sc, v) + jnp.dot(q, r_prev, preferred_element_type=F32) * xi
            ry_ref[rows, :] = ry
            r = g_chunk * r + _dot_tn(k, zeta * v.astype(F32))
            mu = jnp.mean(ry, axis=-1, keepdims=True)
            yc = ry - mu
            yh = yc * lax.rsqrt(jnp.mean(yc * yc, axis=-1, keepdims=True) + EPS)
            g = rg_ref[rows, :]
            ro_ref[rows, :] = (g * _sig(g) * (yh * w)).astype(BF16)
        r_sc[...] = r

    blk = pl.BlockSpec((G * C, 128), lambda h, n: (n, h))
    return pl.pallas_call(
        body, name="ret_fwd", grid=(RET_HEADS, NB),
        in_specs=[pl.BlockSpec((None, 8, 128), lambda h, n: (h, 0, 0)), blk, blk, blk, blk,
                  pl.BlockSpec((1, 128), lambda h, n: (0, h))],
        out_specs=[blk, blk, pl.BlockSpec((G, 128, 128), lambda h, n: (h * NB + n, 0, 0))],
        out_shape=[_sds((S, 512), F32), _sds((S, 512), BF16), _sds((RET_HEADS * N, 128, 128), BF16)],
        scratch_shapes=[pltpu.VMEM((128, 128), F32)],
        compiler_params=_cp(("parallel", "arbitrary")),
    )(_decay_table(), rq, rk, rv, rg, gn_w)


def _outproj(ro, mo, x, w_o, g_post, g_pre, S):
    tm = min(512, S)

    def body(ro_ref, mo_ref, x_ref, wo_ref, g1_ref, g2_ref, mix_ref, h1_ref, hn_ref):
        mix = (jnp.dot(ro_ref[...], wo_ref[0:512, :], preferred_element_type=F32)
               + jnp.dot(mo_ref[...], wo_ref[512:1024, :], preferred_element_type=F32))
        mix_ref[...] = mix.astype(BF16)
        h1 = x_ref[...] + _rms(mix, g1_ref[...])
        h1_ref[...] = h1
        hn_ref[...] = _rms(h1, g2_ref[...]).astype(BF16)

    return pl.pallas_call(
        body, name="outproj", grid=(S // tm,),
        in_specs=[_rows(tm, 512), _rows(tm, 512), _rows(tm, D_MODEL), _full(D_MODEL, D_MODEL), _full(1, D_MODEL),
                  _full(1, D_MODEL)],
        out_specs=[_rows(tm, D_MODEL)] * 3,
        out_shape=[_sds((S, D_MODEL), BF16), _sds((S, D_MODEL), F32), _sds((S, D_MODEL), BF16)],
        compiler_params=_cp(("parallel",)),
    )(ro, mo, x, w_o, g_post, g_pre)


def _ffn_up(hn, w_gate_t, w_up_t, S):
    tm = min(512, S)
    tn = D_FF // 2

    def body(hn_ref, wg_ref, wu_ref, fg_ref, fu_ref, act_ref):
        hn_b = hn_ref[...]
        for seg in range(2):
            sl = slice(seg * tn, (seg + 1) * tn)
            g = _dot_nt(hn_b, wg_ref[sl, :])
            u = _dot_nt(hn_b, wu_ref[sl, :])
            s = _sig(g)
            silu = g * s
            fg_ref[:, sl] = (u * (s + silu * (1.0 - s))).astype(BF16)
            fu_ref[:, sl] = silu.astype(BF16)
            act_ref[:, sl] = (silu * u).astype(BF16)

    return pl.pallas_call(
        body, name="ffn_up", grid=(S // tm,),
        in_specs=[_rows(tm, D_MODEL), _full(D_FF, D_MODEL), _full(D_FF, D_MODEL)],
        out_specs=[_rows(tm, D_FF)] * 3, out_shape=[_sds((S, D_FF), BF16)] * 3,
        compiler_params=_cp(("parallel",)),
    )(hn, w_gate_t, w_up_t)


def _ffn_down(act, w_down, h1, g, S):
    tm = min(512, S)

    def body(act_ref, wd_ref, h1_ref, g_ref, ff_ref, h2_ref):
        ff = jnp.dot(act_ref[...], wd_ref[...], preferred_element_type=F32)
        ff_ref[...] = ff.astype(BF16)
        h2_ref[...] = h1_ref[...] + _rms(ff, g_ref[...])

    return pl.pallas_call(
        body, name="ffn_down", grid=(S // tm,),
        in_specs=[_rows(tm, D_FF), _full(D_FF, D_MODEL), _rows(tm, D_MODEL), _full(1, D_MODEL)],
        out_specs=[_rows(tm, D_MODEL)] * 2, out_shape=[_sds((S, D_MODEL), BF16), _sds((S, D_MODEL), F32)],
        compiler_params=_cp(("parallel",)),
    )(act, w_down, h1, g)


def _ple_loss(p, h2, tgt, w_pp, w_pg, b_pg, g_ple, S):
    tm = min(512, S)

    def body(p_ref, h2_ref, t_ref, wp_ref, wg_ref, b_ref, gp_ref,
             dz_ref, dpe_ref, dh2_ref, h2b_ref, loss_ref, dgp_ref, db_ref):
        @pl.when(pl.program_id(0) == 0)
        def _():
            loss_ref[...] = jnp.zeros(loss_ref.shape, F32)
            dgp_ref[...] = jnp.zeros(dgp_ref.shape, F32)
            db_ref[...] = jnp.zeros(db_ref.shape, F32)

        gp = gp_ref[...]
        pe = _dot(p_ref[...], wp_ref[...])
        r = lax.rsqrt(jnp.mean(pe * pe, axis=-1, keepdims=True) + EPS)
        peh = pe * r
        e = peh * gp
        h2 = h2_ref[...]
        h2b = h2.astype(BF16)
        h2b_ref[...] = h2b
        gt = _sig(jnp.dot(h2b, wg_ref[...], preferred_element_type=F32) + b_ref[...])
        diff = h2 + e * gt - t_ref[...]
        loss_ref[...] += _colsum(diff * diff)
        dh3 = diff * (1.0 / D_MODEL)
        de = dh3 * gt
        dz = dh3 * e * gt * (1.0 - gt)
        db_ref[...] += _colsum(dz)
        dgp_ref[...] += _colsum(de * peh)
        dpeh = de * gp
        dpe = r * (dpeh - peh * jnp.mean(dpeh * peh, axis=-1, keepdims=True))
        dzb = dz.astype(BF16)
        dz_ref[...] = dzb
        dpe_ref[...] = dpe.astype(BF16)
        dh2_ref[...] = dh3 + _dot_nt(dzb, wg_ref[...])

    return pl.pallas_call(
        body, name="ple_loss", grid=(S // tm,),
        in_specs=[_rows(tm, PLE_DIM), _rows(tm, D_MODEL), _rows(tm, D_MODEL), _full(PLE_DIM, D_MODEL),
                  _full(D_MODEL, D_MODEL), _full(1, D_MODEL), _full(1, D_MODEL)],
        out_specs=[_rows(tm, D_MODEL)] * 4 + [_acc(1, D_MODEL)] * 3,
        out_shape=[_sds((S, D_MODEL), BF16), _sds((S, D_MODEL), BF16), _sds((S, D_MODEL), F32), _sds((S, D_MODEL), BF16)]
        + [_sds((1, D_MODEL), F32)] * 3,
        compiler_params=_cp(("arbitrary",)),
    )(p, h2, tgt, w_pp, w_pg, b_pg, g_ple)


def _wgrad(a, b, name, S):
    M = a.shape[1]
    N = b.shape[1]
    ts = min(2048, S)
    nsplit = 2 if M * N >= 2 * 1024 * 1024 else 1
    tn = N // nsplit

    def body(a_ref, b_ref, o_ref):
        @pl.when(pl.program_id(1) == 0)
        def _():
            o_ref[...] = jnp.zeros(o_ref.shape, F32)

        o_ref[...] += _dot_tn(a_ref[...], b_ref[...])

    return pl.pallas_call(
        body, name=name, grid=(nsplit, S // ts),
        in_specs=[pl.BlockSpec((ts, M), lambda j, s: (s, 0)), pl.BlockSpec((ts, tn), lambda j, s: (s, j))],
        out_specs=pl.BlockSpec((M, tn), lambda j, s: (0, j)), out_shape=_sds((M, N), F32),
        compiler_params=_cp(("parallel", "arbitrary")),
    )(a, b)


def _ffn_down_bwd(dh2, ff, g, w_down, dgate_f, dup_f, S):
    tm = min(512, S)
    tn = D_FF // 2

    def body(dh2_ref, ff_ref, g_ref, wd_ref, fg_ref, fu_ref, dff_ref, dgate_ref, dup_ref, dg_ref):
        @pl.when(pl.program_id(0) == 0)
        def _():
            dg_ref[...] = jnp.zeros(dg_ref.shape, F32)

        dff, ga = _rms_bwd(dh2_ref[...], ff_ref[...].astype(F32), g_ref[...])
        dg_ref[...] += _colsum(ga)
        dffb = dff.astype(BF16)
        dff_ref[...] = dffb
        for seg in range(2):
            sl = slice(seg * tn, (seg + 1) * tn)
            dact = _dot_nt(dffb, wd_ref[sl, :])
            dgate_ref[:, sl] = (dact * fg_ref[:, sl].astype(F32)).astype(BF16)
            dup_ref[:, sl] = (dact * fu_ref[:, sl].astype(F32)).astype(BF16)

    return pl.pallas_call(
        body, name="ffn_down_bwd", grid=(S // tm,),
        in_specs=[_rows(tm, D_MODEL), _rows(tm, D_MODEL), _full(1, D_MODEL), _full(D_FF, D_MODEL), _rows(tm, D_FF),
                  _rows(tm, D_FF)],
        out_specs=[_rows(tm, D_MODEL), _rows(tm, D_FF), _rows(tm, D_FF), _acc(1, D_MODEL)],
        out_shape=[_sds((S, D_MODEL), BF16), _sds((S, D_FF), BF16), _sds((S, D_FF), BF16), _sds((1, D_MODEL), F32)],
        compiler_params=_cp(("arbitrary",)),
    )(dh2, ff, g, w_down, dgate_f, dup_f)


def _ffn_up_bwd(dgate, dup, w_gate, w_up, h1, mix, dh2, g_pre, g_post, w_o, S, grads=()):
    tm = min(512, S)
    n = len(grads)
    last = S // tm - 1

    def body(dgate_ref, dup_ref, wg_ref, wu_ref, h1_ref, mix_ref, dh2_ref, g2_ref, g1_ref, wo_ref, *rest):
        g_ins = rest[:n]
        dh1_ref, dmix_ref, dro_ref, dmo_ref, dg2_ref, dg1_ref = rest[n:n + 6]
        g_outs, sems = rest[n + 6:2 * n + 6], rest[2 * n + 6:]

        @pl.when(pl.program_id(0) == 0)
        def _():
            dg2_ref[...] = jnp.zeros(dg2_ref.shape, F32)
            dg1_ref[...] = jnp.zeros(dg1_ref.shape, F32)
            for cp in (_swap_copies(g_ins, g_outs, sems) if n else []):
                cp.start()

        dhn = (jnp.dot(dgate_ref[...], wg_ref[...], preferred_element_type=F32)
               + jnp.dot(dup_ref[...], wu_ref[...], preferred_element_type=F32))
        d1, ga = _rms_bwd(dhn, h1_ref[...], g2_ref[...])
        dg2_ref[...] += _colsum(ga)
        dh1 = dh2_ref[...] + d1
        dh1_ref[...] = dh1
        dmix, gb = _rms_bwd(dh1, mix_ref[...].astype(F32), g1_ref[...])
        dg1_ref[...] += _colsum(gb)
        dmixb = dmix.astype(BF16)
        dmix_ref[...] = dmixb
        dcat = _dot_nt(dmixb, wo_ref[...])
        dro_ref[...] = dcat[:, 0:512].astype(BF16)
        dmo_ref[...] = dcat[:, 512:1024].astype(BF16)

        if n:
            @pl.when(pl.program_id(0) == last)
            def _():
                for cp in _swap_copies(g_ins, g_outs, sems):
                    cp.wait()

    dh1, dmix, dro, dmo, dg2, dg1, *got = pl.pallas_call(
        body, name="ffn_up_bwd", grid=(S // tm,),
        in_specs=[_rows(tm, D_FF), _rows(tm, D_FF), _full(D_FF, D_MODEL), _full(D_FF, D_MODEL), _rows(tm, D_MODEL),
                  _rows(tm, D_MODEL), _rows(tm, D_MODEL), _full(1, D_MODEL), _full(1, D_MODEL), _full(D_MODEL, D_MODEL)]
        + [_ANY] * n,
        out_specs=[_rows(tm, D_MODEL), _rows(tm, D_MODEL), _rows(tm, 512), _rows(tm, 512), _acc(1, D_MODEL),
                   _acc(1, D_MODEL)] + [_ANY] * n,
        out_shape=[_sds((S, D_MODEL), F32), _sds((S, D_MODEL), BF16), _sds((S, 512), BF16), _sds((S, 512), BF16),
                   _sds((1, D_MODEL), F32), _sds((1, D_MODEL), F32)] + _swap_out_shapes(grads),
        scratch_shapes=_swap_sems(n) if n else [],
        compiler_params=_cp(("arbitrary",)),
    )(dgate, dup, w_gate, w_up, h1, mix, dh2, g_pre, g_post, w_o, *grads)
    return dh1, dmix, dro, dmo, dg2, dg1, got


def _attn_delta(o, do, S, grads=()):
    tm = min(512, S)
    n = len(grads)
    last = S // tm - 1

    def body(o_ref, do_ref, *rest):
        g_ins, (dot_ref, d_ref), g_outs, sems = rest[:n], rest[n:n + 2], rest[n + 2:2 * n + 2], rest[2 * n + 2:]
        if n:
            @pl.when(pl.program_id(0) == 0)
            def _():
                for cp in _swap_copies(g_ins, g_outs, sems):
                    cp.start()

        do = do_ref[...].astype(F32)
        prod_t = (o_ref[...].astype(F32) * do).T
        dot_ref[...] = do.T.astype(BF16)
        for h in range(MLA_HEADS):
            d_ref[h // 2, (h % 2):(h % 2) + 1, :] = jnp.sum(prod_t[h * 64:(h + 1) * 64, :], axis=0, keepdims=True)

        if n:
            @pl.when(pl.program_id(0) == last)
            def _():
                for cp in _swap_copies(g_ins, g_outs, sems):
                    cp.wait()

    dot, delta, *got = pl.pallas_call(
        body, name="attn_delta", grid=(S // tm,),
        in_specs=[_rows(tm, 512), _rows(tm, 512)] + [_ANY] * n,
        out_specs=[pl.BlockSpec((512, tm), lambda i: (0, i)), pl.BlockSpec((MLA_HEADS // 2, 2, tm), lambda i: (0, 0, i))]
        + [_ANY] * n,
        out_shape=[_sds((512, S), BF16), _sds((MLA_HEADS // 2, 2, S), F32)] + _swap_out_shapes(grads),
        scratch_shapes=_swap_sems(n) if n else [],
        compiler_params=_cp(("arbitrary",)),
    )(o, do, *grads)
    return dot, delta, got


def _flash_bwd(qp, kp, kt, v, do, dot, lse, delta, S, sums=()):
    tq = min(512, S)
    nq = S // tq
    RB = ATT_ROWS
    NH = BWD_HEADS
    qb_of, kb_of, T = _tri_pairs(nq, k_major=True)
    n = len(sums)
    steps = (MLA_HEADS // NH) * T

    def body(qb_ref, kb_ref, q_ref, k_ref, kt_ref, v_ref, do_ref, dot_ref, lse_ref, dl_ref, *rest):
        g_ins, (dq_ref, dk_ref, dv_ref), g_outs = rest[:n], rest[n:n + 3], rest[n + 3:2 * n + 3]
        dk_sc, dv_sc, s_sc, dp_sc, p_sc, ds_sc = rest[2 * n + 3:2 * n + 9]
        sems = rest[2 * n + 9:]
        t = pl.program_id(1)
        qb = qb_ref[t]
        kb = kb_ref[t]
        lin = pl.program_id(0) * T + t

        if n:
            @pl.when(lin == 0)
            def _():
                for cp in _scatter_copies(g_ins, g_outs, sems):
                    cp.start()

        @pl.when(t == 0)
        def _():
            dq_ref[...] = jnp.zeros(dq_ref.shape, F32)

        @pl.when(qb == kb)
        def _():
            dk_sc[...] = jnp.zeros(dk_sc.shape, F32)
            dv_sc[...] = jnp.zeros(dv_sc.shape, F32)

        lane = lax.broadcasted_iota(jnp.int32, (tq, 64 * NH), 1)

        def step(masked):
            vv = v_ref[...]
            do_all = do_ref[...]
            mine = [(lane >= a * 64) & (lane < (a + 1) * 64) for a in range(NH)]
            for a in range(NH):
                sl = slice(a * 128, (a + 1) * 128)
                s_sc[a] = _dot_nt(k_ref[:, sl], q_ref[:, sl])
                dp_sc[a] = jnp.dot(jnp.where(mine[a], vv, jnp.zeros_like(vv)), dot_ref[...],
                                   preferred_element_type=F32)
            for a in range(NH):
                sl = slice(a * 128, (a + 1) * 128)
                lse = lse_ref[a:a + 1, :]
                dl = dl_ref[a:a + 1, :]
                for r in range(0, tq, RB):
                    sc = s_sc[a, r:r + RB, :]
                    if masked:
                        sc = jnp.where(_causal_keep(r, RB, tq), sc, NEG)
                    p = jnp.exp(sc - lse)
                    p_sc[a, r:r + RB, :] = p.astype(BF16)
                    ds_sc[a, r:r + RB, :] = (p * (dp_sc[a, r:r + RB, :] - dl)).astype(BF16)
                ds = ds_sc[a]
                dv_sc[...] += jnp.dot(p_sc[a], jnp.where(mine[a], do_all, jnp.zeros_like(do_all)),
                                      preferred_element_type=F32)
                dk_sc[:, sl] += jnp.dot(ds, q_ref[:, sl], preferred_element_type=F32)
                dq_ref[qb, sl, :] += jnp.dot(kt_ref[sl, :], ds, preferred_element_type=F32)

        @pl.when(qb > kb)
        def _():
            step(False)

        @pl.when(qb == kb)
        def _():
            step(True)

        @pl.when(qb == nq - 1)
        def _():
            dk_ref[...] = dk_sc[...].astype(BF16)
            dv_ref[...] = dv_sc[...].astype(BF16)

        if n:
            @pl.when(lin == steps - 1)
            def _():
                for cp in _scatter_copies(g_ins, g_outs, sems):
                    cp.wait()

    grid_spec = pltpu.PrefetchScalarGridSpec(
        num_scalar_prefetch=2, grid=(MLA_HEADS // NH, T),
        in_specs=[pl.BlockSpec((tq, 128 * NH), lambda j, t, qb, kb: (qb[t], j)),
                  pl.BlockSpec((tq, 128 * NH), lambda j, t, qb, kb: (kb[t], j)),
                  pl.BlockSpec((128 * NH, tq), lambda j, t, qb, kb: (j, kb[t])),
                  pl.BlockSpec((tq, 64 * NH), lambda j, t, qb, kb: (kb[t], j)),
                  pl.BlockSpec((tq, 64 * NH), lambda j, t, qb, kb: (qb[t], j)),
                  pl.BlockSpec((64 * NH, tq), lambda j, t, qb, kb: (j, qb[t])),
                  pl.BlockSpec((None, NH, tq), lambda j, t, qb, kb: (j, 0, qb[t])),
                  pl.BlockSpec((None, NH, tq), lambda j, t, qb, kb: (j, 0, qb[t]))] + [_ANY] * n,
        out_specs=[pl.BlockSpec((nq, 128 * NH, tq), lambda j, t, qb, kb: (0, j, 0), pipeline_mode=pl.Buffered(1)),
                   pl.BlockSpec((tq, 128 * NH), lambda j, t, qb, kb: (kb[t], j)),
                   pl.BlockSpec((tq, 64 * NH), lambda j, t, qb, kb: (kb[t], j))] + [_ANY] * n,
        scratch_shapes=[pltpu.VMEM((tq, 128 * NH), F32), pltpu.VMEM((tq, 64 * NH), F32), pltpu.VMEM((NH, tq, tq), F32),
                        pltpu.VMEM((NH, tq, tq), F32), pltpu.VMEM((NH, tq, tq), BF16), pltpu.VMEM((NH, tq, tq), BF16)]
        + (_scatter_sems(n) if n else []),
    )
    dq, dk, dv, *parts = pl.pallas_call(
        body, name="flash_bwd", grid_spec=grid_spec,
        out_shape=[_sds((nq, 1024, tq), F32), _sds((S, 1024), BF16), _sds((S, 512), BF16)] + _scatter_out_shapes(sums),
        compiler_params=_cp(("arbitrary", "arbitrary")),
    )(qb_of, kb_of, qp, kp, kt, v, do, dot, lse.reshape(MLA_HEADS // NH, NH, S), delta.reshape(MLA_HEADS // NH, NH, S),
      *sums)
    return dq, dk, dv, parts


def _mla_up_bwd(dqp, dkp, dv, cq, ckv, gq, gkv, w_uq, w_ukv, tabs, S):
    tm = min(512, S)

    def body(dq_ref, dk_ref, dv_ref, cq_ref, ckv_ref, gq_ref, gkv_ref, wuq_ref, wukv_ref, cm_ref, sa_ref, sb_ref,
             dqh_ref, dkv_ref, dcq_ref, dckv_ref, dkr_ref, dgq_ref, dgkv_ref):
        @pl.when(pl.program_id(0) == 0)
        def _():
            dgq_ref[...] = jnp.zeros(dgq_ref.shape, F32)
            dgkv_ref[...] = jnp.zeros(dgkv_ref.shape, F32)

        cm = cm_ref[...]
        sa = sa_ref[...]
        sb = sb_ref[...]
        lane = lax.broadcasted_iota(jnp.int32, (tm, 128), 1)
        dkr_r = jnp.zeros((tm, 128), F32)
        for h in range(MLA_HEADS):
            sl = slice(h * 128, (h + 1) * 128)
            dqh_ref[:, sl] = (_unrope_mla(dq_ref[sl, :].T, cm, sa, sb) * SCALE_MLA).astype(BF16)
            gk = dk_ref[:, sl]
            dkr_r = dkr_r + gk.astype(F32)
            dkv_ref[:, sl] = gk
        dkr_r = jnp.where((lane >= 64) & (lane < 96), dkr_r, 0.0)
        dkr_ref[...] = _unrope_mla(dkr_r, cm, sa, sb).astype(BF16)
        dkv_ref[:, 1024:1536] = dv_ref[...]
        dcq, ga = _rms_bwd(_dot_nt(dqh_ref[...], wuq_ref[...]), cq_ref[...], gq_ref[...])
        dcq_ref[...] = dcq.astype(BF16)
        dgq_ref[...] += _colsum(ga)
        dckv, gb = _rms_bwd(_dot_nt(dkv_ref[...], wukv_ref[...]), ckv_ref[...], gkv_ref[...])
        dckv_ref[...] = dckv.astype(BF16)
        dgkv_ref[...] += _colsum(gb)

    per_q = dqp.shape[2] // tm
    return pl.pallas_call(
        body, name="mla_up_bwd", grid=(S // tm,),
        in_specs=[pl.BlockSpec((None, 1024, tm), lambda i: (i // per_q, 0, i % per_q)),
                  _rows(tm, 1024), _rows(tm, 512), _rows(tm, Q_LORA), _rows(tm, KV_LORA),
                  _full(1, Q_LORA), _full(1, KV_LORA), _full(Q_LORA, 1024), _full(KV_LORA, 1536)] + [_rows(tm, 128)] * 3,
        out_specs=[_rows(tm, 1024), _rows(tm, 1536), _rows(tm, Q_LORA), _rows(tm, KV_LORA), _rows(tm, 128),
                   _acc(1, Q_LORA), _acc(1, KV_LORA)],
        out_shape=[_sds((S, 1024), BF16), _sds((S, 1536), BF16), _sds((S, Q_LORA), BF16), _sds((S, KV_LORA), BF16),
                   _sds((S, 128), BF16), _sds((1, Q_LORA), F32), _sds((1, KV_LORA), F32)],
        compiler_params=_cp(("arbitrary",)),
    )(dqp, dkp, dv, cq, ckv, gq, gkv, w_uq, w_ukv, *tabs[2:])


def _ret_bwd(rq, rk, rv, rprev, ry, rg, dro, gn_w, tabs, S):
    C = RET_CHUNK
    N = S // C
    G = min(RET_GROUP_BWD, N)
    NB = N // G

    def body(lg_ref, q_ref, k_ref, v_ref, rp_ref, ry_ref, rg_ref, dro_ref, w_ref, cr_ref, sr_ref,
             drq_ref, drk_ref, drv_ref, drg_ref, dw_ref, g_sc):
        @pl.when(pl.program_id(1) == 0)
        def _():
            g_sc[...] = jnp.zeros(g_sc.shape, F32)
            dw_ref[...] = jnp.zeros(dw_ref.shape, F32)

        dmat, zeta, xi, g_chunk = _decay_terms(lg_ref)
        w = w_ref[...]
        gacc = g_sc[...]
        dw = jnp.zeros((1, 128), F32)
        for i in reversed(range(G)):
            rows = slice(i * C, (i + 1) * C)
            ry = ry_ref[rows, :]
            mu = jnp.mean(ry, axis=-1, keepdims=True)
            yc = ry - mu
            rstd = lax.rsqrt(jnp.mean(yc * yc, axis=-1, keepdims=True) + EPS)
            yh = yc * rstd
            g = rg_ref[rows, :]
            s = _sig(g)
            dout = dro_ref[rows, :].astype(F32)
            drg_ref[rows, :] = (dout * (yh * w) * (s * (1.0 + g * (1.0 - s)))).astype(BF16)
            dgn = dout * (g * s)
            dw = dw + _colsum(dgn * yh)
            dyh = dgn * w
            dry = rstd * (dyh - jnp.mean(dyh, axis=-1, keepdims=True) - yh * jnp.mean(dyh * yh, axis=-1, keepdims=True))
            do = dry.astype(BF16)

            q = q_ref[rows, :]
            k = k_ref[rows, :]
            v = v_ref[rows, :]
            gfut = gacc.astype(BF16)
            sc = (_dot_nt(q, k) * dmat).astype(BF16)
            dsc = (_dot_nt(do, v) * dmat).astype(BF16)
            dq = jnp.dot(dsc, k, preferred_element_type=F32) + _dot_nt(do, rp_ref[i]) * xi
            dk = _dot_tn(dsc, q) + _dot_nt(v, gfut) * zeta
            dv = _dot_tn(sc, do) + jnp.dot(k, gfut, preferred_element_type=F32) * zeta
            gacc = g_chunk * gacc + _dot_tn(q, xi * dry)
            cr = cr_ref[rows, :]
            sr = sr_ref[rows, :]
            drq_ref[rows, :] = _unrope_ret(dq, cr, sr).astype(BF16)
            drk_ref[rows, :] = _unrope_ret(dk * SCALE_RET, cr, sr).astype(BF16)
            drv_ref[rows, :] = dv.astype(BF16)
        g_sc[...] = gacc
        dw_ref[...] += dw

    blk = pl.BlockSpec((G * C, 128), lambda h, n: (NB - 1 - n, h))
    tab = pl.BlockSpec((G * C, 128), lambda h, n: (NB - 1 - n, 0))
    return pl.pallas_call(
        body, name="ret_bwd", grid=(RET_HEADS, NB),
        in_specs=[pl.BlockSpec((None, 8, 128), lambda h, n: (h, 0, 0)), blk, blk, blk,
                  pl.BlockSpec((G, 128, 128), lambda h, n: (h * NB + NB - 1 - n, 0, 0)), blk, blk, blk,
                  pl.BlockSpec((1, 128), lambda h, n: (0, h)), tab, tab],
        out_specs=[blk, blk, blk, blk, pl.BlockSpec((1, 128), lambda h, n: (0, h))],
        out_shape=[_sds((S, 512), BF16)] * 4 + [_sds((1, 512), F32)],
        scratch_shapes=[pltpu.VMEM((128, 128), F32)],
        compiler_params=_cp(("parallel", "arbitrary")),
    )(_decay_table(), rq, rk, rv, rprev, ry, rg, dro, gn_w, tabs[0], tabs[1])


def _inproj_bwd(drq, drk, drv, drg, dcq, dckv, dkr, w_in, dh1, x, g, S):
    tm = min(512, S)

    def body(drq_ref, drk_ref, drv_ref, drg_ref, dcq_ref, dckv_ref, dkr_ref, w_ref, dh1_ref, x_ref, g_ref,
             gx_ref, dproj_ref, dg_ref):
        @pl.when(pl.program_id(0) == 0)
        def _():
            dg_ref[...] = jnp.zeros(dg_ref.shape, F32)

        dproj_ref[:, 0:512] = drq_ref[...]
        dproj_ref[:, 512:1024] = drk_ref[...]
        dproj_ref[:, 1024:1536] = drv_ref[...]
        dproj_ref[:, 1536:2048] = drg_ref[...]
        dproj_ref[:, 2048:2432] = dcq_ref[...]
        dproj_ref[:, 2432:2688] = dckv_ref[...]
        dproj_ref[:, 2688:2816] = pltpu.roll(dkr_ref[...].astype(F32), 64, 1).astype(BF16)
        dx, ga = _rms_bwd(_dot_nt(dproj_ref[...], w_ref[...]), x_ref[...], g_ref[...])
        gx_ref[...] = dh1_ref[...] + dx
        dg_ref[...] += _colsum(ga)

    return pl.pallas_call(
        body, name="inproj_bwd", grid=(S // tm,),
        in_specs=[_rows(tm, 512)] * 4 + [_rows(tm, Q_LORA), _rows(tm, KV_LORA), _rows(tm, 128),
                                         _full(D_MODEL, IN_COLS_P), _rows(tm, D_MODEL), _rows(tm, D_MODEL),
                                         _full(1, D_MODEL)],
        out_specs=[_rows(tm, D_MODEL), _rows(tm, IN_COLS_P), _acc(1, D_MODEL)],
        out_shape=[_sds((S, D_MODEL), F32), _sds((S, IN_COLS_P), BF16), _sds((1, D_MODEL), F32)],
        compiler_params=_cp(("arbitrary",)),
    )(drq, drk, drv, drg, dcq, dckv, dkr, w_in, dh1, x, g)


def _pad_weights(w):
    w_in_p = jnp.pad(w["w_in"], ((0, 0), (0, IN_COLS_P - IN_COLS)))
    w_uq_p = jnp.pad(w["w_uq"].reshape(Q_LORA, MLA_HEADS, 96), ((0, 0), (0, 0), (0, 32))).reshape(Q_LORA, 1024)
    ukv = w["w_ukv"].reshape(KV_LORA, MLA_HEADS, 128)
    k_part = jnp.pad(ukv[:, :, :64], ((0, 0), (0, 0), (0, 64))).reshape(KV_LORA, 1024)
    w_ukv_p = jnp.concatenate([k_part, ukv[:, :, 64:].reshape(KV_LORA, 512)], axis=1)
    return w_in_p, w_uq_p, w_ukv_p


BIG_SPEC = {n: (r, c, ax) for n, r, c, ax in BIG}
COLUMN_MAJOR = ("w_in", "w_uq", "w_gate", "w_up")
GRAD_TRANSPOSED = ("w_gate", "w_up")
GATHER_FIRST = ("w_in", "w_uq", "w_ukv")
GATHER_LATE = tuple(n for n, _, _, _ in BIG if n not in GATHER_FIRST)
REDUCE_EARLY = ("w_ple_gate", "w_ple_proj", "w_down", "w_gate", "w_up", "w_o")
REDUCE_LAST = tuple(n for n, _, _, _ in BIG if n not in REDUCE_EARLY)


def _local_step(x, p, pos_f, tgt, w, sm, late_shards=None, c_idx=None):
    S = x.shape[0]
    spread = late_shards is not None
    w = dict(w)
    tabs, first = _rope_tables(pos_f, S, [late_shards[n] for n in GATHER_FIRST] if spread else ())
    for i, n in enumerate(GATHER_FIRST if spread else ()):
        w[n] = _from_chips(first[i], BIG_SPEC[n][2])
    w_in_p, w_uq_p, w_ukv_p = _pad_weights(w)
    if spread:
        w_in_p = jnp.concatenate([first[0][j] for j in range(N_CHIPS)]
                                 + [jnp.zeros((D_MODEL, IN_COLS_P - IN_COLS), BF16)], axis=1)

    xn, rq, rk, rv, rg, cq, ckv, kr = _inproj(x, sm["pre_mix_norm"], w_in_p, tabs, S)
    cqn, ckvn, qp, kp, v, kt, vt = _mla_up(cq, ckv, kr, sm["mla_q_norm"], sm["mla_kv_norm"], w_uq_p, w_ukv_p, tabs, S)
    mo, lse, gathered = _flash_fwd(qp, kp, vt, S, [late_shards[n] for n in GATHER_LATE] if spread else ())
    for i, n in enumerate(GATHER_LATE if spread else ()):
        w[n] = _from_chips(gathered[i], 0 if n in GRAD_TRANSPOSED else BIG_SPEC[n][2])
    if not spread:
        w.update({n: w[n].T for n in GRAD_TRANSPOSED})
    ry, ro, rprev = _ret_fwd(rq, rk, rv, rg, sm["ret_gn_w"], S)
    mix, h1, hn = _outproj(ro, mo, x, w["w_o"], sm["post_mix_norm"], sm["pre_ffn_norm"], S)
    dgate_f, dup_f, act = _ffn_up(hn, w["w_gate"], w["w_up"], S)
    ff, h2 = _ffn_down(act, w["w_down"], h1, sm["post_ffn_norm"], S)
    dz, dpe, dh2, h2b, loss_vec, d_ple_norm, d_b = _ple_loss(
        p, h2, tgt, w["w_ple_proj"], w["w_ple_gate"], sm["b_ple_gate"], sm["ple_norm"], S)

    gw = {}
    gs = {"ple_norm": d_ple_norm, "b_ple_gate": d_b}
    gw["w_ple_gate"] = _wgrad(h2b, dz, "wgrad_ple_gate", S)
    gw["w_ple_proj"] = _wgrad(p, dpe, "wgrad_ple_proj", S)
    dff, dgate, dup, gs["post_ffn_norm"] = _ffn_down_bwd(dh2, ff, sm["post_ffn_norm"], w["w_down"], dgate_f, dup_f, S)
    gw["w_down"] = _wgrad(act, dff, "wgrad_down", S)
    if spread:
        gw["w_gate"] = _wgrad(dgate, hn, "wgrad_gate", S)
        gw["w_up"] = _wgrad(dup, hn, "wgrad_up", S)
    else:
        gw["w_gate"] = _wgrad(hn, dgate, "wgrad_gate", S)
        gw["w_up"] = _wgrad(hn, dup, "wgrad_up", S)
    first = REDUCE_EARLY[:-1]
    g4 = [_by_chip(gw.pop(n), *((D_FF, D_MODEL, 0) if n in GRAD_TRANSPOSED else BIG_SPEC[n]))
          for n in first] if spread else []
    dh1, dmix, dro, dmo, gs["pre_ffn_norm"], gs["post_mix_norm"], got = _ffn_up_bwd(
        dgate, dup, w["w_gate"], w["w_up"], h1, mix, dh2, sm["pre_ffn_norm"], sm["post_mix_norm"], w["w_o"], S, g4)
    gw["w_o"] = jnp.concatenate([_wgrad(ro, dmix, "wgrad_o_ret", S), _wgrad(mo, dmix, "wgrad_o_mla", S)], axis=0)
    g4_o = [_by_chip(gw.pop("w_o"), *BIG_SPEC["w_o"])] if spread else []

    dmo_t, delta, got_o = _attn_delta(mo, dmo, S, g4_o)
    sums = _add_half_rows(g4 + g4_o, list(got) + list(got_o), c_idx, "rs_add_halves_early") if spread else []
    dqp, dkp, dv, parts = _flash_bwd(qp, kp, kt, v, dmo, dmo_t, lse, delta, S, sums)
    dqh, dkv, dcq, dckv, dkr, gs["mla_q_norm"], gs["mla_kv_norm"] = _mla_up_bwd(
        dqp, dkp, dv, cq, ckv, sm["mla_q_norm"], sm["mla_kv_norm"], w_uq_p, w_ukv_p, tabs, S)
    g_uq_p = _wgrad(cqn, dqh, "wgrad_uq", S)
    g_ukv_p = _wgrad(ckvn, dkv, "wgrad_ukv", S)
    gw["w_uq"] = g_uq_p.reshape(Q_LORA, MLA_HEADS, 128)[:, :, :96].reshape(Q_LORA, 768)
    gw["w_ukv"] = jnp.concatenate(
        [g_ukv_p[:, :1024].reshape(KV_LORA, MLA_HEADS, 128)[:, :, :64], g_ukv_p[:, 1024:].reshape(KV_LORA, MLA_HEADS, 64)],
        axis=2).reshape(KV_LORA, 1024)

    drq, drk, drv, drg, gs["ret_gn_w"] = _ret_bwd(rq, rk, rv, rprev, ry, rg, dro, sm["ret_gn_w"], tabs, S)
    grad_x, dproj, gs["pre_mix_norm"] = _inproj_bwd(drq, drk, drv, drg, dcq, dckv, dkr, w_in_p, dh1, x,
                                                    sm["pre_mix_norm"], S)
    if spread:
        gw["w_in"] = _wgrad(dproj, xn, "wgrad_in", S)[:IN_COLS]
    else:
        gw["w_in"] = _wgrad(xn, dproj, "wgrad_in", S)[:, :IN_COLS]
    return loss_vec, grad_x, gw, gs, ((sums, parts) if spread else None)


def _my_place():
    x = lax.axis_index("x")
    y = lax.axis_index("y")
    c = lax.axis_index("c")
    return x, y, c


def _other_chips(x, y):
    return [(1 - x, y), (x, 1 - y), (1 - x, 1 - y)]


_ANY = pl.BlockSpec(memory_space=pl.ANY)


def _small_copies(v_ref, slots, sems):
    send, recv, lsem = sems
    x, y, c = _my_place()
    me = 4 * x + 2 * y + c
    cps = [pltpu.make_async_copy(v_ref, slots.at[me], lsem)]
    for r in range(1, N_DEV):
        peer = (x ^ (r >> 2), y ^ ((r >> 1) & 1), c ^ (r & 1))
        cps.append(pltpu.make_async_remote_copy(
            src_ref=v_ref, dst_ref=slots.at[me], send_sem=send.at[r - 1], recv_sem=recv.at[r - 1],
            device_id=peer, device_id_type=MESH))
    return cps


def _small_sum(slots, out_ref):
    acc = slots[0]
    for d in range(1, N_DEV):
        acc = acc + slots[d]
    out_ref[...] = acc
    loss = jnp.sum(acc[9:10, :], axis=1, keepdims=True) * (0.5 / D_MODEL)
    out_ref[9:10, :] = jnp.broadcast_to(loss, (1, PACK_COLS))


def _small_scratch():
    return [pltpu.VMEM((N_DEV, SMALL_ROWS, PACK_COLS), F32), pltpu.SemaphoreType.DMA((N_DEV - 1,)),
            pltpu.SemaphoreType.DMA((N_DEV - 1,)), pltpu.SemaphoreType.DMA]


N_BIG = len(BIG)


def _half(c, rows, align):
    h = rows // 2
    return pl.ds(pl.multiple_of(c * h, align), h)


def _gather_out_shapes(shards):
    return [_sds((N_CHIPS,) + tuple(s.shape), BF16) for s in shards]


def _gather_sems(n):
    return [pltpu.SemaphoreType.DMA((n, 3))] * 4 + [pltpu.SemaphoreType.DMA((n,))] * 2


def _gather_phase(phase, ins, outs, sems):
    send1, recv1, send2, recv2, send3, recv3 = sems
    x, y, c = _my_place()
    me = 2 * x + y
    chips = _other_chips(x, y)
    sib = (x, y, 1 - c)
    for t in range(len(ins)):
        rows = ins[t].shape[0]
        half = _half(c, rows, 16)
        other = _half(1 - c, rows, 16)
        def own():
            return pltpu.make_async_remote_copy(
                src_ref=ins[t], dst_ref=outs[t].at[me], send_sem=send3.at[t], recv_sem=recv3.at[t],
                device_id=sib, device_id_type=MESH)

        if phase == 0:
            own().start()
        if phase == 2:
            own().wait()
        for k, (cx, cy) in enumerate(chips):
            src = 2 * cx + cy

            def over_ici(slab):
                return pltpu.make_async_remote_copy(
                    src_ref=ins[t].at[half], dst_ref=outs[t].at[slab, half], send_sem=send1.at[t, k],
                    recv_sem=recv1.at[t, k], device_id=(cx, cy, c), device_id_type=MESH)

            def over_d2d(rows):
                return pltpu.make_async_remote_copy(
                    src_ref=outs[t].at[src, rows], dst_ref=outs[t].at[src, rows], send_sem=send2.at[t, k],
                    recv_sem=recv2.at[t, k], device_id=sib, device_id_type=MESH)

            if phase == 0:
                over_ici(me).start()
            if phase == 1:
                over_ici(src).wait_recv()
                over_d2d(half).start()
            if phase == 2:
                over_d2d(other).wait_recv()
                over_ici(me).wait_send()
                over_d2d(half).wait_send()


def _swap_copies(ins, outs, sems):
    send, recv = sems
    x, y, c = _my_place()
    return [pltpu.make_async_remote_copy(
        src_ref=ins[t].at[:, _half(1 - c, ins[t].shape[1], 8)], dst_ref=outs[t], send_sem=send.at[t],
        recv_sem=recv.at[t], device_id=(x, y, 1 - c), device_id_type=MESH) for t in range(len(ins))]


def _swap_out_shapes(gs):
    return [_sds((N_CHIPS, g.shape[1] // 2, g.shape[2]), F32) for g in gs]


def _swap_sems(n):
    return [pltpu.SemaphoreType.DMA((n,)), pltpu.SemaphoreType.DMA((n,))]


def _swap_half_rows(gs):
    n = len(gs)

    def body(*refs):
        cps = _swap_copies(refs[:n], refs[n:2 * n], refs[2 * n:])
        for cp in cps:
            cp.start()
        for cp in cps:
            cp.wait()

    return pl.pallas_call(
        body, name="rs_swap_halves",
        in_specs=[_ANY] * n, out_specs=[_ANY] * n, out_shape=_swap_out_shapes(gs), scratch_shapes=_swap_sems(n),
    )(*gs)


def _add_half_rows(gs, gots, c_idx, name):
    n = len(gs)
    shapes = [(g.shape[1] // 2, g.shape[2]) for g in gs]

    def body(c_ref, *refs):
        for t in range(n):
            refs[2 * n + t][...] = (refs[t][...] + refs[n + t][...]).astype(BF16)

    grid_spec = pltpu.PrefetchScalarGridSpec(
        num_scalar_prefetch=1, grid=(N_CHIPS,),
        in_specs=[pl.BlockSpec((None,) + hc, lambda j, c: (j, c[0], 0)) for hc in shapes]
        + [pl.BlockSpec((None,) + hc, lambda j, c: (j, 0, 0)) for hc in shapes],
        out_specs=[pl.BlockSpec((None,) + hc, lambda j, c: (j, 0, 0)) for hc in shapes],
    )
    return pl.pallas_call(
        body, name=name, grid_spec=grid_spec, out_shape=[_sds((N_CHIPS,) + hc, BF16) for hc in shapes],
        compiler_params=_cp(("parallel",)),
    )(c_idx, *gs, *gots)


def _scatter_to_chips(ts, vec):
    n = len(ts)

    def body(*refs):
        ins, v_ref, outs, small_ref = refs[:n], refs[n], refs[n + 1:2 * n + 1], refs[2 * n + 1]
        slots, small_sems, sems = refs[2 * n + 2], refs[2 * n + 3:2 * n + 6], refs[2 * n + 6:]
        small = _small_copies(v_ref, slots, small_sems)
        cps = _scatter_copies(ins, outs, sems)
        for cp in small + cps:
            cp.start()
        for cp in small:
            cp.wait()
        _small_sum(slots, small_ref)
        for cp in cps:
            cp.wait()

    vm = pl.BlockSpec(memory_space=pltpu.VMEM)
    *parts, small_sum = pl.pallas_call(
        body, name="rs_scatter_chips",
        in_specs=[_ANY] * n + [vm], out_specs=[_ANY] * n + [vm],
        out_shape=_scatter_out_shapes(ts) + [_sds((SMALL_ROWS, PACK_COLS), F32)],
        scratch_shapes=_small_scratch() + _scatter_sems(n),
    )(*ts, vec)
    return parts, small_sum


def _scatter_copies(ins, outs, sems):
    send, recv = sems
    x, y, c = _my_place()
    return [pltpu.make_async_remote_copy(
        src_ref=ins[t].at[2 * cx + cy], dst_ref=outs[t].at[k], send_sem=send.at[t, k], recv_sem=recv.at[t, k],
        device_id=(cx, cy, c), device_id_type=MESH)
        for t in range(len(ins)) for k, (cx, cy) in enumerate(_other_chips(x, y))]


def _scatter_out_shapes(ts):
    return [_sds((3,) + tuple(t.shape[1:]), BF16) for t in ts]


def _scatter_sems(n):
    return [pltpu.SemaphoreType.DMA((n, 3)), pltpu.SemaphoreType.DMA((n, 3))]


def _add_four(mines, parts, place):
    n = len(mines)

    def body(pl_ref, *refs):
        for t in range(n):
            m_ref, p_ref, o_ref = refs[t], refs[n + t], refs[2 * n + t]
            o_ref[...] = ((m_ref[...].astype(F32) + p_ref[0].astype(F32)) + p_ref[1].astype(F32)) + p_ref[2].astype(F32)

    shapes = [p.shape[1:] for p in parts]
    grid_spec = pltpu.PrefetchScalarGridSpec(
        num_scalar_prefetch=1, grid=(1,),
        in_specs=[pl.BlockSpec((None,) + hc, lambda i, pc: (pc[0], 0, 0)) for hc in shapes]
        + [pl.BlockSpec((3,) + hc, lambda i, pc: (0, 0, 0)) for hc in shapes],
        out_specs=[pl.BlockSpec(hc, lambda i, pc: (pc[1], 0)) for hc in shapes],
    )
    return pl.pallas_call(
        body, name="rs_add_chips", grid_spec=grid_spec, out_shape=[_sds((2 * h, c), F32) for h, c in shapes],
        compiler_params=_cp(("arbitrary",)),
    )(place, *mines, *parts)


def _join_half_rows(rs):
    n = len(rs)

    def body(*refs):
        ins, outs = refs[:n], refs[n:2 * n]
        send, recv = refs[2 * n:]
        x, y, c = _my_place()
        cps = []
        for t in range(n):
            half = _half(c, outs[t].shape[0], 8)
            rc = pltpu.make_async_remote_copy(
                src_ref=ins[t].at[half], dst_ref=outs[t].at[half], send_sem=send.at[t], recv_sem=recv.at[t],
                device_id=(x, y, 1 - c), device_id_type=MESH)
            rc.start()
            cps.append(rc)
        for cp in cps:
            cp.wait()

    return pl.pallas_call(
        body, name="rs_join_halves",
        in_specs=[_ANY] * n, out_specs=[_ANY] * n,
        out_shape=[_sds(r.shape, F32) for r in rs],
        input_output_aliases={i: i for i in range(n)},
        scratch_shapes=[pltpu.SemaphoreType.DMA((n,))] * 2,
    )(*rs)


def _by_chip(full, rows, cols, axis):
    if axis == 0:
        return full.reshape(N_CHIPS, rows // N_CHIPS, cols)
    return full.reshape(rows, N_CHIPS, cols // N_CHIPS).transpose(1, 0, 2)


def _from_chips(parts, axis):
    _, r, c = parts.shape
    if axis == 0:
        return parts.reshape(N_CHIPS * r, c)
    return parts.transpose(1, 0, 2).reshape(r, N_CHIPS * c)


def _adamw(wt, g, m, v, name):
    _, R, C = wt.shape
    tr = max(d for d in range(8, R + 1, 8) if R % d == 0 and (d * C <= 256 * 1024 or d == 8))

    def body(w_ref, g_ref, m_ref, v_ref, d_ref, nm_ref, nv_ref):
        gg = g_ref[...]
        m_new = ADAM_B1 * m_ref[...] + (1.0 - ADAM_B1) * gg
        v_new = ADAM_B2 * v_ref[...] + (1.0 - ADAM_B2) * (gg * gg)
        m_hat = m_new / (1.0 - ADAM_B1 ** ADAM_STEP)
        v_hat = v_new / (1.0 - ADAM_B2 ** ADAM_STEP)
        d_ref[...] = -ADAM_LR * (m_hat / (jnp.sqrt(v_hat) + ADAM_EPS) + ADAM_WD * w_ref[...])
        nm_ref[...] = m_new
        nv_ref[...] = v_new

    spec = pl.BlockSpec((None, tr, C), lambda i: (0, i, 0))
    return pl.pallas_call(
        body, name=name, grid=(R // tr,), in_specs=[spec, pl.BlockSpec((tr, C), lambda i: (i, 0)), spec, spec],
        out_specs=[spec] * 3, out_shape=[_sds((1, R, C), F32)] * 3,
        compiler_params=_cp(("parallel",)),
    )(wt, g, m, v)


def _pack_small(vals, loss_vec=None):
    rows = [jnp.pad(vals[n].reshape(-1), (0, PACK_COLS - sz)) for n, sz in SMALL]
    rows.append(loss_vec.reshape(-1) if loss_vec is not None else jnp.zeros((PACK_COLS,), F32))
    rows += [jnp.zeros((PACK_COLS,), F32)] * (SMALL_ROWS - len(rows))
    return jnp.stack(rows)


def kernel(x, p, positions, pre_mix_norm, w_in, ret_gn_w, mla_q_norm, w_uq, mla_kv_norm, w_ukv, w_o, post_mix_norm, pre_ffn_norm, w_gate, w_up, w_down, post_ffn_norm, w_ple_proj, ple_norm, w_ple_gate, b_ple_gate, loss_target, m_pre_mix_norm, m_w_in, m_ret_gn_w, m_mla_q_norm, m_w_uq, m_mla_kv_norm, m_w_ukv, m_w_o, m_post_mix_norm, m_pre_ffn_norm, m_w_gate, m_w_up, m_w_down, m_post_ffn_norm, m_w_ple_proj, m_ple_norm, m_w_ple_gate, m_b_ple_gate, v_pre_mix_norm, v_w_in, v_ret_gn_w, v_mla_q_norm, v_w_uq, v_mla_kv_norm, v_w_ukv, v_w_o, v_post_mix_norm, v_pre_ffn_norm, v_w_gate, v_w_up, v_w_down, v_post_ffn_norm, v_w_ple_proj, v_ple_norm, v_w_ple_gate, v_b_ple_gate):
    wts = dict(pre_mix_norm=pre_mix_norm, w_in=w_in, ret_gn_w=ret_gn_w, mla_q_norm=mla_q_norm, w_uq=w_uq,
               mla_kv_norm=mla_kv_norm, w_ukv=w_ukv, w_o=w_o, post_mix_norm=post_mix_norm, pre_ffn_norm=pre_ffn_norm,
               w_gate=w_gate, w_up=w_up, w_down=w_down, post_ffn_norm=post_ffn_norm, w_ple_proj=w_ple_proj,
               ple_norm=ple_norm, w_ple_gate=w_ple_gate, b_ple_gate=b_ple_gate)
    mom = dict(pre_mix_norm=m_pre_mix_norm, w_in=m_w_in, ret_gn_w=m_ret_gn_w, mla_q_norm=m_mla_q_norm, w_uq=m_w_uq,
               mla_kv_norm=m_mla_kv_norm, w_ukv=m_w_ukv, w_o=m_w_o, post_mix_norm=m_post_mix_norm,
               pre_ffn_norm=m_pre_ffn_norm, w_gate=m_w_gate, w_up=m_w_up, w_down=m_w_down, post_ffn_norm=m_post_ffn_norm,
               w_ple_proj=m_w_ple_proj, ple_norm=m_ple_norm, w_ple_gate=m_w_ple_gate, b_ple_gate=m_b_ple_gate)
    var = dict(pre_mix_norm=v_pre_mix_norm, w_in=v_w_in, ret_gn_w=v_ret_gn_w, mla_q_norm=v_mla_q_norm, w_uq=v_w_uq,
               mla_kv_norm=v_mla_kv_norm, w_ukv=v_w_ukv, w_o=v_w_o, post_mix_norm=v_post_mix_norm,
               pre_ffn_norm=v_pre_ffn_norm, w_gate=v_w_gate, w_up=v_w_up, w_down=v_w_down, post_ffn_norm=v_post_ffn_norm,
               w_ple_proj=v_w_ple_proj, ple_norm=v_ple_norm, w_ple_gate=v_w_ple_gate, b_ple_gate=v_b_ple_gate)

    S = x.shape[1]
    shard2d = {n: wts[n][0] for n, _, _, _ in BIG}
    small2d = {n: wts[n] for n, _ in SMALL}

    shard_bf = {n: (jnp.swapaxes(wts[n], 1, 2)[0] if n in GRAD_TRANSPOSED else shard2d[n]).astype(BF16) for n in shard2d}
    pos_f = positions.astype(F32).reshape(S, 1)
    c_idx = lax.axis_index("c").astype(jnp.int32).reshape(1)
    loss_vec, grad_x, gw, gs, (sums_early, parts_early) = _local_step(
        x[0], p[0, 0], pos_f, loss_target[0], {}, small2d, shard_bf, c_idx)

    g4 = [_by_chip(gw[n], *BIG_SPEC[n]) for n in REDUCE_LAST if n != "w_in"]
    g4.insert(REDUCE_LAST.index("w_in"), jnp.pad(gw["w_in"].reshape(N_CHIPS, IN_SHARD, D_MODEL),
                                                 ((0, 0), (0, IN_SHARD_P - IN_SHARD), (0, 0))))
    got = _swap_half_rows(g4)
    sums_last = _add_half_rows(g4, got, c_idx, "rs_add_halves_last")
    parts_last, small_sum = _scatter_to_chips(sums_last, _pack_small(gs, loss_vec))
    place = jnp.stack([2 * lax.axis_index("x") + lax.axis_index("y"), lax.axis_index("c")]).astype(jnp.int32)
    names = REDUCE_EARLY + REDUCE_LAST
    reduced = _join_half_rows(_add_four(sums_early + sums_last, list(parts_early) + list(parts_last), place))
    g_shard = dict(zip(names, reduced))

    loss = small_sum[9, 0]
    g_small = {n: small_sum[i:i + 1, :sz] for i, (n, sz) in enumerate(SMALL)}

    grads, delta, new_m, new_v = {}, {}, {}, {}
    for n, _, _, _ in BIG:
        if n in COLUMN_MAJOR:
            turn = lambda a: jnp.swapaxes(a, 1, 2)
            g_t = g_shard[n][:IN_SHARD] if n == "w_in" else g_shard[n] if n in GRAD_TRANSPOSED else g_shard[n].T
            d, nm, nv = _adamw(turn(wts[n]), g_t, turn(mom[n]), turn(var[n]), "adamw_" + n)
            grads[n], delta[n], new_m[n], new_v[n] = turn(g_t[None]), turn(d), turn(nm), turn(nv)
        else:
            delta[n], new_m[n], new_v[n] = _adamw(wts[n], g_shard[n], mom[n], var[n], "adamw_" + n)
            grads[n] = g_shard[n][None]
    d, nm, nv = _adamw(_pack_small(small2d)[None], small_sum, _pack_small(mom)[None], _pack_small(var)[None],
                       "adamw_small")
    for i, (n, sz) in enumerate(SMALL):
        grads[n] = g_small[n]
        delta[n], new_m[n], new_v[n] = d[0, i:i + 1, :sz], nm[0, i:i + 1, :sz], nv[0, i:i + 1, :sz]

    return (loss, grad_x[None], *[grads[n] for n in ALL_W], *[delta[n] for n in ALL_W],
            *[new_m[n] for n in ALL_W], *[new_v[n] for n in ALL_W])
```

```python
import functools
import math

import jax
import jax.numpy as jnp
import numpy as np
from jax import lax
from jax.experimental import pallas as pl
from jax.experimental.pallas import tpu as pltpu

F32 = jnp.float32
BF16 = jnp.bfloat16
MESH = pl.DeviceIdType.MESH

D_MODEL = 1024
D_FF = 2816
PLE_DIM = 256
RET_HEADS = 4
RET_DIM = 128
RET_WIDTH = 512
RET_CHUNK = 256
RET_GROUP_FWD = 16
RET_GROUP_BWD = 8
MLA_HEADS = 8
MLA_NOPE = 64
MLA_ROPE = 32
MLA_V = 64
Q_LORA = 384
KV_LORA = 256
IN_COLS = 2720
IN_COLS_P = 2816
IN_SHARD = IN_COLS // 4
IN_SHARD_P = 688
ROPE_BASE = 10000.0
EPS = 1e-6
SCALE_MLA = 1.0 / math.sqrt(MLA_NOPE + MLA_ROPE)
SCALE_RET = RET_DIM ** -0.5
NEG = -1e30

ADAM_LR = 0.001
ADAM_B1 = 0.9
ADAM_B2 = 0.999
ADAM_EPS = 1e-08
ADAM_WD = 0.01
ADAM_STEP = 10

N_CHIPS = 4
N_DEV = 8
VMEM_MB = 56

BIG = (
    ("w_in", 1024, 2720, 1),
    ("w_uq", 384, 768, 1),
    ("w_ukv", 256, 1024, 1),
    ("w_o", 1024, 1024, 0),
    ("w_gate", 1024, 2816, 1),
    ("w_up", 1024, 2816, 1),
    ("w_down", 2816, 1024, 0),
    ("w_ple_proj", 256, 1024, 1),
    ("w_ple_gate", 1024, 1024, 0),
)
SMALL = (
    ("pre_mix_norm", 1024),
    ("ret_gn_w", 512),
    ("mla_q_norm", 384),
    ("mla_kv_norm", 256),
    ("post_mix_norm", 1024),
    ("pre_ffn_norm", 1024),
    ("post_ffn_norm", 1024),
    ("ple_norm", 1024),
    ("b_ple_gate", 1024),
)
ALL_W = ("pre_mix_norm", "w_in", "ret_gn_w", "mla_q_norm", "w_uq", "mla_kv_norm", "w_ukv", "w_o", "post_mix_norm",
         "pre_ffn_norm", "w_gate", "w_up", "w_down", "post_ffn_norm", "w_ple_proj", "ple_norm", "w_ple_gate", "b_ple_gate")
PACK_COLS = 1024
SMALL_ROWS = 16


def _cp(sem=None, mb=VMEM_MB, **kw):
    return pltpu.CompilerParams(dimension_semantics=sem, vmem_limit_bytes=mb * 1024 * 1024, **kw)


def _bf(x):
    return x.astype(BF16)


def _dot(a, b):
    return jnp.dot(_bf(a), _bf(b), preferred_element_type=F32)


def _dot_nt(a, b):
    return lax.dot_general(_bf(a), _bf(b), (((1,), (1,)), ((), ())), preferred_element_type=F32)


def _dot_tn(a, b):
    return lax.dot_general(_bf(a), _bf(b), (((0,), (0,)), ((), ())), preferred_element_type=F32)


def _sig(x):
    return 1.0 / (1.0 + jnp.exp(-x))


def _rms(x, g):
    r = lax.rsqrt(jnp.mean(x * x, axis=-1, keepdims=True) + EPS)
    return x * r * g


def _rms_bwd(dy, x, g):
    r = lax.rsqrt(jnp.mean(x * x, axis=-1, keepdims=True) + EPS)
    xh = x * r
    dxh = dy * g
    dx = r * (dxh - xh * jnp.mean(dxh * xh, axis=-1, keepdims=True))
    return dx, dy * xh


def _colsum(x):
    return jnp.sum(x, axis=0, keepdims=True)


def _rope_ret(x, cr, sr):
    return x * cr + pltpu.roll(x, 64, 1) * sr


def _unrope_ret(dy, cr, sr):
    return dy * cr + pltpu.roll(dy * sr, 64, 1)


def _rope_mla(x, cm, sa, sb):
    return x * cm + pltpu.roll(x, 112, 1) * sa + pltpu.roll(x, 16, 1) * sb


def _unrope_mla(dy, cm, sa, sb):
    return dy * cm + pltpu.roll(dy * sa, 16, 1) + pltpu.roll(dy * sb, 112, 1)


def _rows(tm, w, col=0):
    return pl.BlockSpec((tm, w), lambda i: (i, col))


def _full(*shape):
    return pl.BlockSpec(shape, lambda i: (0,) * len(shape), pipeline_mode=pl.Buffered(1))


def _acc(*shape):
    return pl.BlockSpec(shape, lambda i: (0,) * len(shape))


def _sds(shape, dtype):
    return jax.ShapeDtypeStruct(shape, dtype)


def _rope_tables(pos_f, S, shards=()):
    tm = min(512, S)
    n = len(shards)
    steps = S // tm
    inv_r = (1.0 / (np.float32(ROPE_BASE) ** (np.arange(64, dtype=np.float32) / np.float32(64)))).astype(np.float32)
    inv_m16 = (1.0 / (np.float32(ROPE_BASE) ** (np.arange(16, dtype=np.float32) / np.float32(16)))).astype(np.float32)
    inv_r = np.concatenate([inv_r, inv_r])[None, :]
    inv_m = np.zeros((1, 128), np.float32)
    inv_m[0, 64:80] = inv_m16
    inv_m[0, 80:96] = inv_m16

    def body(pos_ref, invr_ref, invm_ref, *rest):
        w_ins, (cr_ref, sr_ref, cm_ref, sa_ref, sb_ref) = rest[:n], rest[n:n + 5]
        w_outs, sems = rest[n + 5:2 * n + 5], rest[2 * n + 5:]
        i = pl.program_id(0)
        if n:
            @pl.when(i == 0)
            def _():
                _gather_phase(0, w_ins, w_outs, sems)

            @pl.when(i == steps - 1)
            def _():
                _gather_phase(1, w_ins, w_outs, sems)

        pos = pos_ref[...]
        lane = lax.broadcasted_iota(jnp.int32, (tm, 128), 1)
        ar = pos * invr_ref[...]
        s = jnp.sin(ar)
        cr_ref[...] = jnp.cos(ar)
        sr_ref[...] = jnp.where(lane < 64, -s, s)
        am = pos * invm_ref[...]
        c2 = jnp.cos(am)
        s2 = jnp.sin(am)
        cm_ref[...] = jnp.where(lane < 64, 1.0, jnp.where(lane < 96, c2, 0.0))
        sa_ref[...] = jnp.where((lane >= 64) & (lane < 80), -s2, 0.0)
        sb_ref[...] = jnp.where((lane >= 80) & (lane < 96), s2, 0.0)

        if n:
            @pl.when(i == steps - 1)
            def _():
                _gather_phase(2, w_ins, w_outs, sems)

    outs = pl.pallas_call(
        body, name="rope_tables", grid=(steps,),
        in_specs=[_rows(tm, 1), _full(1, 128), _full(1, 128)] + [_ANY] * n,
        out_specs=[_rows(tm, 128)] * 5 + [_ANY] * n,
        out_shape=[_sds((S, 128), F32)] * 5 + _gather_out_shapes(shards),
        scratch_shapes=_gather_sems(n) if n else [],
        compiler_params=_cp(("arbitrary",)),
    )(pos_f, jnp.asarray(inv_r), jnp.asarray(inv_m), *shards)
    return outs[:5], outs[5:]


def _inproj(x, g, w_in, tabs, S):
    tm = min(512, S)

    def body(x_ref, g_ref, w_ref, cr_ref, sr_ref, cm_ref, sa_ref, sb_ref,
             xn_ref, rq_ref, rk_ref, rv_ref, rg_ref, cq_ref, ckv_ref, kr_ref):
        xb = _rms(x_ref[...], g_ref[...]).astype(BF16)
        xn_ref[...] = xb
        cr = cr_ref[...]
        sr = sr_ref[...]
        q = jnp.dot(xb, w_ref[:, 0:512], preferred_element_type=F32)
        k = jnp.dot(xb, w_ref[:, 512:1024], preferred_element_type=F32)
        for h in range(RET_HEADS):
            sl = slice(h * 128, (h + 1) * 128)
            rq_ref[:, sl] = _rope_ret(q[:, sl], cr, sr).astype(BF16)
            rk_ref[:, sl] = (_rope_ret(k[:, sl], cr, sr) * SCALE_RET).astype(BF16)
        rv_ref[...] = jnp.dot(xb, w_ref[:, 1024:1536], preferred_element_type=F32).astype(BF16)
        rg_ref[...] = jnp.dot(xb, w_ref[:, 1536:2048], preferred_element_type=F32)
        cq_ref[...] = jnp.dot(xb, w_ref[:, 2048:2432], preferred_element_type=F32)
        ckv_ref[...] = jnp.dot(xb, w_ref[:, 2432:2688], preferred_element_type=F32)
        kr = pltpu.roll(jnp.dot(xb, w_ref[:, 2688:2816], preferred_element_type=F32), 64, 1)
        kr_ref[...] = _rope_mla(kr, cm_ref[...], sa_ref[...], sb_ref[...])

    return pl.pallas_call(
        body, name="inproj", grid=(S // tm,),
        in_specs=[_rows(tm, D_MODEL), _full(1, D_MODEL), _full(D_MODEL, IN_COLS_P)] + [_rows(tm, 128)] * 5,
        out_specs=[_rows(tm, D_MODEL)] + [_rows(tm, 512)] * 4 + [_rows(tm, Q_LORA), _rows(tm, KV_LORA), _rows(tm, 128)],
        out_shape=[_sds((S, D_MODEL), BF16)] + [_sds((S, 512), BF16)] * 3
        + [_sds((S, 512), F32), _sds((S, Q_LORA), F32), _sds((S, KV_LORA), F32), _sds((S, 128), F32)],
        compiler_params=_cp(("parallel",)),
    )(x, g, w_in, *tabs)


def _mla_up(cq, ckv, kr, gq, gkv, w_uq, w_ukv, tabs, S):
    tm = min(512, S)

    def body(cq_ref, ckv_ref, kr_ref, gq_ref, gkv_ref, wuq_ref, wukv_ref, cm_ref, sa_ref, sb_ref,
             cqn_ref, ckvn_ref, qp_ref, kp_ref, v_ref, kt_ref, vt_ref):
        cm = cm_ref[...]
        sa = sa_ref[...]
        sb = sb_ref[...]
        cqn = _rms(cq_ref[...], gq_ref[...]).astype(BF16)
        cqn_ref[...] = cqn
        ckvn = _rms(ckv_ref[...], gkv_ref[...]).astype(BF16)
        ckvn_ref[...] = ckvn
        qh = jnp.dot(cqn, wuq_ref[...], preferred_element_type=F32)
        kv = jnp.dot(ckvn, wukv_ref[...], preferred_element_type=F32)
        kr_blk = kr_ref[...]
        for h in range(MLA_HEADS):
            sl = slice(h * 128, (h + 1) * 128)
            qp_ref[:, sl] = (_rope_mla(qh[:, sl], cm, sa, sb) * SCALE_MLA).astype(BF16)
            kh = kv[:, sl] + kr_blk
            kp_ref[:, sl] = kh.astype(BF16)
            kt_ref[sl, :] = kh.T.astype(BF16)
        for h in range(MLA_HEADS // 2):
            vh = kv[:, 1024 + h * 128:1024 + (h + 1) * 128]
            v_ref[:, h * 128:(h + 1) * 128] = vh.astype(BF16)
            vt_ref[h * 128:(h + 1) * 128, :] = vh.T.astype(BF16)

    cols = lambda r: pl.BlockSpec((r, tm), lambda i: (0, i))
    return pl.pallas_call(
        body, name="mla_up", grid=(S // tm,),
        in_specs=[_rows(tm, Q_LORA), _rows(tm, KV_LORA), _rows(tm, 128), _full(1, Q_LORA), _full(1, KV_LORA),
                  _full(Q_LORA, 1024), _full(KV_LORA, 1536)] + [_rows(tm, 128)] * 3,
        out_specs=[_rows(tm, Q_LORA), _rows(tm, KV_LORA), _rows(tm, 1024), _rows(tm, 1024), _rows(tm, 512),
                   cols(1024), cols(512)],
        out_shape=[_sds((S, Q_LORA), BF16), _sds((S, KV_LORA), BF16), _sds((S, 1024), BF16), _sds((S, 1024), BF16),
                   _sds((S, 512), BF16), _sds((1024, S), BF16), _sds((512, S), BF16)],
        compiler_params=_cp(("parallel",)),
    )(cq, ckv, kr, gq, gkv, w_uq, w_ukv, *tabs[2:])


def _tri_pairs(nq, k_major):
    if k_major:
        pairs = [(qb, kb) for kb in range(nq) for qb in range(kb, nq)]
    else:
        pairs = [(qb, kb) for qb in range(nq) for kb in range(qb + 1)]
    qb_of = np.array([p[0] for p in pairs], np.int32)
    kb_of = np.array([p[1] for p in pairs], np.int32)
    return jnp.asarray(qb_of), jnp.asarray(kb_of), len(pairs)


ATT_ROWS = 32
FWD_HEADS = 8
BWD_HEADS = 4


def _causal_keep(r0, rows, tq):
    key = r0 + lax.broadcasted_iota(jnp.int32, (rows, tq), 0)
    qry = lax.broadcasted_iota(jnp.int32, (rows, tq), 1)
    return key <= qry


def _flash_fwd(qp, kp, vt, S, shards=()):
    tq = min(512, S)
    nq = S // tq
    RB = ATT_ROWS
    NH = FWD_HEADS
    qb_of, kb_of, T = _tri_pairs(nq, k_major=False)
    n = len(shards)
    steps = (MLA_HEADS // NH) * T

    def body(qb_ref, kb_ref, q_ref, k_ref, vt_ref, *rest):
        w_ins, (o_ref, lse_ref), w_outs = rest[:n], rest[n:n + 2], rest[n + 2:2 * n + 2]
        m_sc, l_sc, acc_sc, s_sc, p_sc = rest[2 * n + 2:2 * n + 7]
        sems = rest[2 * n + 7:]
        t = pl.program_id(1)
        qb = qb_ref[t]
        kb = kb_ref[t]
        lin = pl.program_id(0) * T + t

        if n:
            @pl.when(lin == 0)
            def _():
                _gather_phase(0, w_ins, w_outs, sems)

            @pl.when(lin == steps // 2)
            def _():
                _gather_phase(1, w_ins, w_outs, sems)

        @pl.when(kb == 0)
        def _():
            m_sc[...] = jnp.full(m_sc.shape, NEG, F32)
            l_sc[...] = jnp.zeros(l_sc.shape, F32)
            acc_sc[...] = jnp.zeros(acc_sc.shape, F32)

        def scores(a):
            sl = slice(a * 128, (a + 1) * 128)
            s_sc[a] = _dot_nt(k_ref[:, sl], q_ref[:, sl])

        def step(masked):
            for a in range(NH):
                scores(a)
            for a in range(NH):
                mx = [jnp.full((8, tq), NEG, F32) for _ in range(RB // 8)]
                for r in range(0, tq, RB):
                    sc = s_sc[a, r:r + RB, :]
                    if masked:
                        sc = jnp.where(_causal_keep(r, RB, tq), sc, NEG)
                        s_sc[a, r:r + RB, :] = sc
                    for i in range(RB // 8):
                        mx[i] = jnp.maximum(mx[i], sc[i * 8:(i + 1) * 8, :])
                mx8 = functools.reduce(jnp.maximum, mx)
                m_prev = m_sc[a]
                m_new = jnp.maximum(m_prev, jnp.max(mx8, axis=0, keepdims=True))
                al = jnp.exp(m_prev - m_new)
                m_sc[a] = m_new
                ls = [jnp.zeros((8, tq), F32) for _ in range(RB // 8)]
                for r in range(0, tq, RB):
                    p = jnp.exp(s_sc[a, r:r + RB, :] - m_new)
                    for i in range(RB // 8):
                        ls[i] = ls[i] + p[i * 8:(i + 1) * 8, :]
                    p_sc[a, r:r + RB, :] = p.astype(BF16)
                l_sc[a] = al * l_sc[a] + jnp.sum(functools.reduce(jnp.add, ls), axis=0, keepdims=True)
                pair = slice((a // 2) * 128, (a // 2 + 1) * 128)
                pv = jnp.dot(vt_ref[pair, :], p_sc[a], preferred_element_type=F32)
                rs = slice(a * 64, (a + 1) * 64)
                own = slice((a % 2) * 64, (a % 2 + 1) * 64)
                acc_sc[rs, :] = acc_sc[rs, :] * al + pv[own, :]

        @pl.when(kb < qb)
        def _():
            step(False)

        @pl.when(kb == qb)
        def _():
            step(True)
            for a in range(NH):
                rs = slice(a * 64, (a + 1) * 64)
                acc_sc[rs, :] = acc_sc[rs, :] / l_sc[a]
                lse_ref[a:a + 1, :] = m_sc[a] + jnp.log(l_sc[a])
            o_ref[...] = acc_sc[...].T.astype(BF16)

        if n:
            @pl.when(lin == steps - 1)
            def _():
                _gather_phase(2, w_ins, w_outs, sems)

    grid_spec = pltpu.PrefetchScalarGridSpec(
        num_scalar_prefetch=2, grid=(MLA_HEADS // NH, T),
        in_specs=[pl.BlockSpec((tq, 128 * NH), lambda j, t, qb, kb: (qb[t], j)),
                  pl.BlockSpec((tq, 128 * NH), lambda j, t, qb, kb: (kb[t], j)),
                  pl.BlockSpec((64 * NH, tq), lambda j, t, qb, kb: (j, kb[t]))] + [_ANY] * n,
        out_specs=[pl.BlockSpec((tq, 64 * NH), lambda j, t, qb, kb: (qb[t], j)),
                   pl.BlockSpec((None, NH, tq), lambda j, t, qb, kb: (j, 0, qb[t]))] + [_ANY] * n,
        scratch_shapes=[pltpu.VMEM((NH, 1, tq), F32), pltpu.VMEM((NH, 1, tq), F32), pltpu.VMEM((64 * NH, tq), F32),
                        pltpu.VMEM((NH, tq, tq), F32), pltpu.VMEM((NH, tq, tq), BF16)] + (_gather_sems(n) if n else []),
    )
    out, lse, *gathered = pl.pallas_call(
        body, name="flash_fwd", grid_spec=grid_spec,
        out_shape=[_sds((S, 512), BF16), _sds((MLA_HEADS // NH, NH, S), F32)] + _gather_out_shapes(shards),
        compiler_params=_cp(("arbitrary", "arbitrary")),
    )(qb_of, kb_of, qp, kp, vt, *shards)
    return out, lse.reshape(MLA_HEADS // 2, 2, S), gathered


def _decay_table():
    log_g = np.log(1.0 - 2.0 ** (-5.0 - np.arange(RET_HEADS, dtype=np.float32))).astype(np.float32)
    return jnp.asarray(np.broadcast_to(log_g[:, None, None], (RET_HEADS, 8, 128)).copy())


def _decay_terms(lg_ref):
    C = RET_CHUNK
    lg = lg_ref[0:1, :]
    row = lax.broadcasted_iota(jnp.int32, (C, C), 0)
    col = lax.broadcasted_iota(jnp.int32, (C, C), 1)
    diff = (row - col).astype(F32)
    dmat = jnp.where(diff >= 0, jnp.exp(jnp.maximum(diff, 0.0) * jnp.tile(lg, (1, C // 128))), 0.0)
    j = lax.broadcasted_iota(jnp.int32, (C, 1), 0).astype(F32)
    lg1 = lg[:, 0:1]
    zeta = jnp.exp((C - 1 - j) * lg1)
    xi = jnp.exp((j + 1.0) * lg1)
    g_chunk = jnp.exp(C * lg1)
    return dmat, zeta, xi, g_chunk


def _ret_fwd(rq, rk, rv, rg, gn_w, S):
    C = RET_CHUNK
    N = S // C
    G = min(RET_GROUP_FWD, N)
    NB = N // G

    def body(lg_ref, q_ref, k_ref, v_ref, rg_ref, w_ref, ry_ref, ro_ref, rprev_ref, r_sc):
        @pl.when(pl.program_id(1) == 0)
        def _():
            r_sc[...] = jnp.zeros(r_sc.shape, F32)

        dmat, zeta, xi, g_chunk = _decay_terms(lg_ref)
        w = w_ref[...]
        r = r_sc[...]
        for i in range(G):
            rows = slice(i * C, (i + 1) * C)
            q = q_ref[rows, :]
            k = k_ref[rows, :]
            v = v_ref[rows, :]
            r_prev = r.astype(BF16)
            rprev_ref[i] = r_prev
            sc = _dot_nt(q, k) * dmat
            ry = _dot(sc, v) + jnp.dot(q, r_prev, preferred_element_type=F32) * xi
            ry_ref[rows, :] = ry
            r = g_chunk * r + _dot_tn(k, zeta * v.astype(F32))
            mu = jnp.mean(ry, axis=-1, keepdims=True)
            yc = ry - mu
            yh = yc * lax.rsqrt(jnp.mean(yc * yc, axis=-1, keepdims=True) + EPS)
            g = rg_ref[rows, :]
            ro_ref[rows, :] = (g * _sig(g) * (yh * w)).astype(BF16)
        r_sc[...] = r

    blk = pl.BlockSpec((G * C, 128), lambda h, n: (n, h))
    return pl.pallas_call(
        body, name="ret_fwd", grid=(RET_HEADS, NB),
        in_specs=[pl.BlockSpec((None, 8, 128), lambda h, n: (h, 0, 0)), blk, blk, blk, blk,
                  pl.BlockSpec((1, 128), lambda h, n: (0, h))],
        out_specs=[blk, blk, pl.BlockSpec((G, 128, 128), lambda h, n: (h * NB + n, 0, 0))],
        out_shape=[_sds((S, 512), F32), _sds((S, 512), BF16), _sds((RET_HEADS * N, 128, 128), BF16)],
        scratch_shapes=[pltpu.VMEM((128, 128), F32)],
        compiler_params=_cp(("parallel", "arbitrary")),
    )(_decay_table(), rq, rk, rv, rg, gn_w)


def _outproj(ro, mo, x, w_o, g_post, g_pre, S):
    tm = min(512, S)

    def body(ro_ref, mo_ref, x_ref, wo_ref, g1_ref, g2_ref, mix_ref, h1_ref, hn_ref):
        mix = (jnp.dot(ro_ref[...], wo_ref[0:512, :], preferred_element_type=F32)
               + jnp.dot(mo_ref[...], wo_ref[512:1024, :], preferred_element_type=F32))
        mix_ref[...] = mix.astype(BF16)
        h1 = x_ref[...] + _rms(mix, g1_ref[...])
        h1_ref[...] = h1
        hn_ref[...] = _rms(h1, g2_ref[...]).astype(BF16)

    return pl.pallas_call(
        body, name="outproj", grid=(S // tm,),
        in_specs=[_rows(tm, 512), _rows(tm, 512), _rows(tm, D_MODEL), _full(D_MODEL, D_MODEL), _full(1, D_MODEL),
                  _full(1, D_MODEL)],
        out_specs=[_rows(tm, D_MODEL)] * 3,
        out_shape=[_sds((S, D_MODEL), BF16), _sds((S, D_MODEL), F32), _sds((S, D_MODEL), BF16)],
        compiler_params=_cp(("parallel",)),
    )(ro, mo, x, w_o, g_post, g_pre)


def _ffn_up(hn, w_gate_t, w_up_t, S):
    tm = min(512, S)
    tn = D_FF // 2

    def body(hn_ref, wg_ref, wu_ref, fg_ref, fu_ref, act_ref):
        hn_b = hn_ref[...]
        for seg in range(2):
            sl = slice(seg * tn, (seg + 1) * tn)
            g = _dot_nt(hn_b, wg_ref[sl, :])
            u = _dot_nt(hn_b, wu_ref[sl, :])
            s = _sig(g)
            silu = g * s
            fg_ref[:, sl] = (u * (s + silu * (1.0 - s))).astype(BF16)
            fu_ref[:, sl] = silu.astype(BF16)
            act_ref[:, sl] = (silu * u).astype(BF16)

    return pl.pallas_call(
        body, name="ffn_up", grid=(S // tm,),
        in_specs=[_rows(tm, D_MODEL), _full(D_FF, D_MODEL), _full(D_FF, D_MODEL)],
        out_specs=[_rows(tm, D_FF)] * 3, out_shape=[_sds((S, D_FF), BF16)] * 3,
        compiler_params=_cp(("parallel",)),
    )(hn, w_gate_t, w_up_t)


def _ffn_down(act, w_down, h1, g, S):
    tm = min(512, S)

    def body(act_ref, wd_ref, h1_ref, g_ref, ff_ref, h2_ref):
        ff = jnp.dot(act_ref[...], wd_ref[...], preferred_element_type=F32)
        ff_ref[...] = ff.astype(BF16)
        h2_ref[...] = h1_ref[...] + _rms(ff, g_ref[...])

    return pl.pallas_call(
        body, name="ffn_down", grid=(S // tm,),
        in_specs=[_rows(tm, D_FF), _full(D_FF, D_MODEL), _rows(tm, D_MODEL), _full(1, D_MODEL)],
        out_specs=[_rows(tm, D_MODEL)] * 2, out_shape=[_sds((S, D_MODEL), BF16), _sds((S, D_MODEL), F32)],
        compiler_params=_cp(("parallel",)),
    )(act, w_down, h1, g)


def _ple_loss(p, h2, tgt, w_pp, w_pg, b_pg, g_ple, S):
    tm = min(512, S)

    def body(p_ref, h2_ref, t_ref, wp_ref, wg_ref, b_ref, gp_ref,
             dz_ref, dpe_ref, dh2_ref, h2b_ref, loss_ref, dgp_ref, db_ref):
        @pl.when(pl.program_id(0) == 0)
        def _():
            loss_ref[...] = jnp.zeros(loss_ref.shape, F32)
            dgp_ref[...] = jnp.zeros(dgp_ref.shape, F32)
            db_ref[...] = jnp.zeros(db_ref.shape, F32)

        gp = gp_ref[...]
        pe = _dot(p_ref[...], wp_ref[...])
        r = lax.rsqrt(jnp.mean(pe * pe, axis=-1, keepdims=True) + EPS)
        peh = pe * r
        e = peh * gp
        h2 = h2_ref[...]
        h2b = h2.astype(BF16)
        h2b_ref[...] = h2b
        gt = _sig(jnp.dot(h2b, wg_ref[...], preferred_element_type=F32) + b_ref[...])
        diff = h2 + e * gt - t_ref[...]
        loss_ref[...] += _colsum(diff * diff)
        dh3 = diff * (1.0 / D_MODEL)
        de = dh3 * gt
        dz = dh3 * e * gt * (1.0 - gt)
        db_ref[...] += _colsum(dz)
        dgp_ref[...] += _colsum(de * peh)
        dpeh = de * gp
        dpe = r * (dpeh - peh * jnp.mean(dpeh * peh, axis=-1, keepdims=True))
        dzb = dz.astype(BF16)
        dz_ref[...] = dzb
        dpe_ref[...] = dpe.astype(BF16)
        dh2_ref[...] = dh3 + _dot_nt(dzb, wg_ref[...])

    return pl.pallas_call(
        body, name="ple_loss", grid=(S // tm,),
        in_specs=[_rows(tm, PLE_DIM), _rows(tm, D_MODEL), _rows(tm, D_MODEL), _full(PLE_DIM, D_MODEL),
                  _full(D_MODEL, D_MODEL), _full(1, D_MODEL), _full(1, D_MODEL)],
        out_specs=[_rows(tm, D_MODEL)] * 4 + [_acc(1, D_MODEL)] * 3,
        out_shape=[_sds((S, D_MODEL), BF16), _sds((S, D_MODEL), BF16), _sds((S, D_MODEL), F32), _sds((S, D_MODEL), BF16)]
        + [_sds((1, D_MODEL), F32)] * 3,
        compiler_params=_cp(("arbitrary",)),
    )(p, h2, tgt, w_pp, w_pg, b_pg, g_ple)


def _wgrad(a, b, name, S):
    M = a.shape[1]
    N = b.shape[1]
    ts = min(2048, S)
    nsplit = 2 if M * N >= 2 * 1024 * 1024 else 1
    tn = N // nsplit

    def body(a_ref, b_ref, o_ref):
        @pl.when(pl.program_id(1) == 0)
        def _():
            o_ref[...] = jnp.zeros(o_ref.shape, F32)

        o_ref[...] += _dot_tn(a_ref[...], b_ref[...])

    return pl.pallas_call(
        body, name=name, grid=(nsplit, S // ts),
        in_specs=[pl.BlockSpec((ts, M), lambda j, s: (s, 0)), pl.BlockSpec((ts, tn), lambda j, s: (s, j))],
        out_specs=pl.BlockSpec((M, tn), lambda j, s: (0, j)), out_shape=_sds((M, N), F32),
        compiler_params=_cp(("parallel", "arbitrary")),
    )(a, b)


def _ffn_down_bwd(dh2, ff, g, w_down, dgate_f, dup_f, S):
    tm = min(512, S)
    tn = D_FF // 2

    def body(dh2_ref, ff_ref, g_ref, wd_ref, fg_ref, fu_ref, dff_ref, dgate_ref, dup_ref, dg_ref):
        @pl.when(pl.program_id(0) == 0)
        def _():
            dg_ref[...] = jnp.zeros(dg_ref.shape, F32)

        dff, ga = _rms_bwd(dh2_ref[...], ff_ref[...].astype(F32), g_ref[...])
        dg_ref[...] += _colsum(ga)
        dffb = dff.astype(BF16)
        dff_ref[...] = dffb
        for seg in range(2):
            sl = slice(seg * tn, (seg + 1) * tn)
            dact = _dot_nt(dffb, wd_ref[sl, :])
            dgate_ref[:, sl] = (dact * fg_ref[:, sl].astype(F32)).astype(BF16)
            dup_ref[:, sl] = (dact * fu_ref[:, sl].astype(F32)).astype(BF16)

    return pl.pallas_call(
        body, name="ffn_down_bwd", grid=(S // tm,),
        in_specs=[_rows(tm, D_MODEL), _rows(tm, D_MODEL), _full(1, D_MODEL), _full(D_FF, D_MODEL), _rows(tm, D_FF),
                  _rows(tm, D_FF)],
        out_specs=[_rows(tm, D_MODEL), _rows(tm, D_FF), _rows(tm, D_FF), _acc(1, D_MODEL)],
        out_shape=[_sds((S, D_MODEL), BF16), _sds((S, D_FF), BF16), _sds((S, D_FF), BF16), _sds((1, D_MODEL), F32)],
        compiler_params=_cp(("arbitrary",)),
    )(dh2, ff, g, w_down, dgate_f, dup_f)


def _ffn_up_bwd(dgate, dup, w_gate, w_up, h1, mix, dh2, g_pre, g_post, w_o, S, grads=()):
    tm = min(512, S)
    n = len(grads)
    last = S // tm - 1

    def body(dgate_ref, dup_ref, wg_ref, wu_ref, h1_ref, mix_ref, dh2_ref, g2_ref, g1_ref, wo_ref, *rest):
        g_ins = rest[:n]
        dh1_ref, dmix_ref, dro_ref, dmo_ref, dg2_ref, dg1_ref = rest[n:n + 6]
        g_outs, sems = rest[n + 6:2 * n + 6], rest[2 * n + 6:]

        @pl.when(pl.program_id(0) == 0)
        def _():
            dg2_ref[...] = jnp.zeros(dg2_ref.shape, F32)
            dg1_ref[...] = jnp.zeros(dg1_ref.shape, F32)
            for cp in (_swap_copies(g_ins, g_outs, sems) if n else []):
                cp.start()

        dhn = (jnp.dot(dgate_ref[...], wg_ref[...], preferred_element_type=F32)
               + jnp.dot(dup_ref[...], wu_ref[...], preferred_element_type=F32))
        d1, ga = _rms_bwd(dhn, h1_ref[...], g2_ref[...])
        dg2_ref[...] += _colsum(ga)
        dh1 = dh2_ref[...] + d1
        dh1_ref[...] = dh1
        dmix, gb = _rms_bwd(dh1, mix_ref[...].astype(F32), g1_ref[...])
        dg1_ref[...] += _colsum(gb)
        dmixb = dmix.astype(BF16)
        dmix_ref[...] = dmixb
        dcat = _dot_nt(dmixb, wo_ref[...])
        dro_ref[...] = dcat[:, 0:512].astype(BF16)
        dmo_ref[...] = dcat[:, 512:1024].astype(BF16)

        if n:
            @pl.when(pl.program_id(0) == last)
            def _():
                for cp in _swap_copies(g_ins, g_outs, sems):
                    cp.wait()

    dh1, dmix, dro, dmo, dg2, dg1, *got = pl.pallas_call(
        body, name="ffn_up_bwd", grid=(S // tm,),
        in_specs=[_rows(tm, D_FF), _rows(tm, D_FF), _full(D_FF, D_MODEL), _full(D_FF, D_MODEL), _rows(tm, D_MODEL),
                  _rows(tm, D_MODEL), _rows(tm, D_MODEL), _full(1, D_MODEL), _full(1, D_MODEL), _full(D_MODEL, D_MODEL)]
        + [_ANY] * n,
        out_specs=[_rows(tm, D_MODEL), _rows(tm, D_MODEL), _rows(tm, 512), _rows(tm, 512), _acc(1, D_MODEL),
                   _acc(1, D_MODEL)] + [_ANY] * n,
        out_shape=[_sds((S, D_MODEL), F32), _sds((S, D_MODEL), BF16), _sds((S, 512), BF16), _sds((S, 512), BF16),
                   _sds((1, D_MODEL), F32), _sds((1, D_MODEL), F32)] + _swap_out_shapes(grads),
        scratch_shapes=_swap_sems(n) if n else [],
        compiler_params=_cp(("arbitrary",)),
    )(dgate, dup, w_gate, w_up, h1, mix, dh2, g_pre, g_post, w_o, *grads)
    return dh1, dmix, dro, dmo, dg2, dg1, got


def _attn_delta(o, do, S, grads=()):
    tm = min(512, S)
    n = len(grads)
    last = S // tm - 1

    def body(o_ref, do_ref, *rest):
        g_ins, (dot_ref, d_ref), g_outs, sems = rest[:n], rest[n:n + 2], rest[n + 2:2 * n + 2], rest[2 * n + 2:]
        if n:
            @pl.when(pl.program_id(0) == 0)
            def _():
                for cp in _swap_copies(g_ins, g_outs, sems):
                    cp.start()

        do = do_ref[...].astype(F32)
        prod_t = (o_ref[...].astype(F32) * do).T
        dot_ref[...] = do.T.astype(BF16)
        for h in range(MLA_HEADS):
            d_ref[h // 2, (h % 2):(h % 2) + 1, :] = jnp.sum(prod_t[h * 64:(h + 1) * 64, :], axis=0, keepdims=True)

        if n:
            @pl.when(pl.program_id(0) == last)
            def _():
                for cp in _swap_copies(g_ins, g_outs, sems):
                    cp.wait()

    dot, delta, *got = pl.pallas_call(
        body, name="attn_delta", grid=(S // tm,),
        in_specs=[_rows(tm, 512), _rows(tm, 512)] + [_ANY] * n,
        out_specs=[pl.BlockSpec((512, tm), lambda i: (0, i)), pl.BlockSpec((MLA_HEADS // 2, 2, tm), lambda i: (0, 0, i))]
        + [_ANY] * n,
        out_shape=[_sds((512, S), BF16), _sds((MLA_HEADS // 2, 2, S), F32)] + _swap_out_shapes(grads),
        scratch_shapes=_swap_sems(n) if n else [],
        compiler_params=_cp(("arbitrary",)),
    )(o, do, *grads)
    return dot, delta, got


def _flash_bwd(qp, kp, kt, v, do, dot, lse, delta, S, sums=()):
    tq = min(512, S)
    nq = S // tq
    RB = ATT_ROWS
    NH = BWD_HEADS
    qb_of, kb_of, T = _tri_pairs(nq, k_major=True)
    n = len(sums)
    steps = (MLA_HEADS // NH) * T

    def body(qb_ref, kb_ref, q_ref, k_ref, kt_ref, v_ref, do_ref, dot_ref, lse_ref, dl_ref, *rest):
        g_ins, (dq_ref, dk_ref, dv_ref), g_outs = rest[:n], rest[n:n + 3], rest[n + 3:2 * n + 3]
        dk_sc, dv_sc, s_sc, dp_sc, p_sc, ds_sc = rest[2 * n + 3:2 * n + 9]
        sems = rest[2 * n + 9:]
        t = pl.program_id(1)
        qb = qb_ref[t]
        kb = kb_ref[t]
        lin = pl.program_id(0) * T + t

        if n:
            @pl.when(lin == 0)
            def _():
                for cp in _scatter_copies(g_ins, g_outs, sems):
                    cp.start()

        @pl.when(t == 0)
        def _():
            dq_ref[...] = jnp.zeros(dq_ref.shape, F32)

        @pl.when(qb == kb)
        def _():
            dk_sc[...] = jnp.zeros(dk_sc.shape, F32)
            dv_sc[...] = jnp.zeros(dv_sc.shape, F32)

        lane = lax.broadcasted_iota(jnp.int32, (tq, 64 * NH), 1)

        def step(masked):
            vv = v_ref[...]
            do_all = do_ref[...]
            mine = [(lane >= a * 64) & (lane < (a + 1) * 64) for a in range(NH)]
            for a in range(NH):
                sl = slice(a * 128, (a + 1) * 128)
                s_sc[a] = _dot_nt(k_ref[:, sl], q_ref[:, sl])
                dp_sc[a] = jnp.dot(jnp.where(mine[a], vv, jnp.zeros_like(vv)), dot_ref[...],
                                   preferred_element_type=F32)
            for a in range(NH):
                sl = slice(a * 128, (a + 1) * 128)
                lse = lse_ref[a:a + 1, :]
                dl = dl_ref[a:a + 1, :]
                for r in range(0, tq, RB):
                    sc = s_sc[a, r:r + RB, :]
                    if masked:
                        sc = jnp.where(_causal_keep(r, RB, tq), sc, NEG)
                    p = jnp.exp(sc - lse)
                    p_sc[a, r:r + RB, :] = p.astype(BF16)
                    ds_sc[a, r:r + RB, :] = (p * (dp_sc[a, r:r + RB, :] - dl)).astype(BF16)
                ds = ds_sc[a]
                dv_sc[...] += jnp.dot(p_sc[a], jnp.where(mine[a], do_all, jnp.zeros_like(do_all)),
                                      preferred_element_type=F32)
                dk_sc[:, sl] += jnp.dot(ds, q_ref[:, sl], preferred_element_type=F32)
                dq_ref[qb, sl, :] += jnp.dot(kt_ref[sl, :], ds, preferred_element_type=F32)

        @pl.when(qb > kb)
        def _():
            step(False)

        @pl.when(qb == kb)
        def _():
            step(True)

        @pl.when(qb == nq - 1)
        def _():
            dk_ref[...] = dk_sc[...].astype(BF16)
            dv_ref[...] = dv_sc[...].astype(BF16)

        if n:
            @pl.when(lin == steps - 1)
            def _():
                for cp in _scatter_copies(g_ins, g_outs, sems):
                    cp.wait()

    grid_spec = pltpu.PrefetchScalarGridSpec(
        num_scalar_prefetch=2, grid=(MLA_HEADS // NH, T),
        in_specs=[pl.BlockSpec((tq, 128 * NH), lambda j, t, qb, kb: (qb[t], j)),
                  pl.BlockSpec((tq, 128 * NH), lambda j, t, qb, kb: (kb[t], j)),
                  pl.BlockSpec((128 * NH, tq), lambda j, t, qb, kb: (j, kb[t])),
                  pl.BlockSpec((tq, 64 * NH), lambda j, t, qb, kb: (kb[t], j)),
                  pl.BlockSpec((tq, 64 * NH), lambda j, t, qb, kb: (qb[t], j)),
                  pl.BlockSpec((64 * NH, tq), lambda j, t, qb, kb: (j, qb[t])),
                  pl.BlockSpec((None, NH, tq), lambda j, t, qb, kb: (j, 0, qb[t])),
                  pl.BlockSpec((None, NH, tq), lambda j, t, qb, kb: (j, 0, qb[t]))] + [_ANY] * n,
        out_specs=[pl.BlockSpec((nq, 128 * NH, tq), lambda j, t, qb, kb: (0, j, 0), pipeline_mode=pl.Buffered(1)),
                   pl.BlockSpec((tq, 128 * NH), lambda j, t, qb, kb: (kb[t], j)),
                   pl.BlockSpec((tq, 64 * NH), lambda j, t, qb, kb: (kb[t], j))] + [_ANY] * n,
        scratch_shapes=[pltpu.VMEM((tq, 128 * NH), F32), pltpu.VMEM((tq, 64 * NH), F32), pltpu.VMEM((NH, tq, tq), F32),
                        pltpu.VMEM((NH, tq, tq), F32), pltpu.VMEM((NH, tq, tq), BF16), pltpu.VMEM((NH, tq, tq), BF16)]
        + (_scatter_sems(n) if n else []),
    )
    dq, dk, dv, *parts = pl.pallas_call(
        body, name="flash_bwd", grid_spec=grid_spec,
        out_shape=[_sds((nq, 1024, tq), F32), _sds((S, 1024), BF16), _sds((S, 512), BF16)] + _scatter_out_shapes(sums),
        compiler_params=_cp(("arbitrary", "arbitrary")),
    )(qb_of, kb_of, qp, kp, kt, v, do, dot, lse.reshape(MLA_HEADS // NH, NH, S), delta.reshape(MLA_HEADS // NH, NH, S),
      *sums)
    return dq, dk, dv, parts


def _mla_up_bwd(dqp, dkp, dv, cq, ckv, gq, gkv, w_uq, w_ukv, tabs, S):
    tm = min(512, S)

    def body(dq_ref, dk_ref, dv_ref, cq_ref, ckv_ref, gq_ref, gkv_ref, wuq_ref, wukv_ref, cm_ref, sa_ref, sb_ref,
             dqh_ref, dkv_ref, dcq_ref, dckv_ref, dkr_ref, dgq_ref, dgkv_ref):
        @pl.when(pl.program_id(0) == 0)
        def _():
            dgq_ref[...] = jnp.zeros(dgq_ref.shape, F32)
            dgkv_ref[...] = jnp.zeros(dgkv_ref.shape, F32)

        cm = cm_ref[...]
        sa = sa_ref[...]
        sb = sb_ref[...]
        lane = lax.broadcasted_iota(jnp.int32, (tm, 128), 1)
        dkr_r = jnp.zeros((tm, 128), F32)
        for h in range(MLA_HEADS):
            sl = slice(h * 128, (h + 1) * 128)
            dqh_ref[:, sl] = (_unrope_mla(dq_ref[sl, :].T, cm, sa, sb) * SCALE_MLA).astype(BF16)
            gk = dk_ref[:, sl]
            dkr_r = dkr_r + gk.astype(F32)
            dkv_ref[:, sl] = gk
        dkr_r = jnp.where((lane >= 64) & (lane < 96), dkr_r, 0.0)
        dkr_ref[...] = _unrope_mla(dkr_r, cm, sa, sb).astype(BF16)
        dkv_ref[:, 1024:1536] = dv_ref[...]
        dcq, ga = _rms_bwd(_dot_nt(dqh_ref[...], wuq_ref[...]), cq_ref[...], gq_ref[...])
        dcq_ref[...] = dcq.astype(BF16)
        dgq_ref[...] += _colsum(ga)
        dckv, gb = _rms_bwd(_dot_nt(dkv_ref[...], wukv_ref[...]), ckv_ref[...], gkv_ref[...])
        dckv_ref[...] = dckv.astype(BF16)
        dgkv_ref[...] += _colsum(gb)

    per_q = dqp.shape[2] // tm
    return pl.pallas_call(
        body, name="mla_up_bwd", grid=(S // tm,),
        in_specs=[pl.BlockSpec((None, 1024, tm), lambda i: (i // per_q, 0, i % per_q)),
                  _rows(tm, 1024), _rows(tm, 512), _rows(tm, Q_LORA), _rows(tm, KV_LORA),
                  _full(1, Q_LORA), _full(1, KV_LORA), _full(Q_LORA, 1024), _full(KV_LORA, 1536)] + [_rows(tm, 128)] * 3,
        out_specs=[_rows(tm, 1024), _rows(tm, 1536), _rows(tm, Q_LORA), _rows(tm, KV_LORA), _rows(tm, 128),
                   _acc(1, Q_LORA), _acc(1, KV_LORA)],
        out_shape=[_sds((S, 1024), BF16), _sds((S, 1536), BF16), _sds((S, Q_LORA), BF16), _sds((S, KV_LORA), BF16),
                   _sds((S, 128), BF16), _sds((1, Q_LORA), F32), _sds((1, KV_LORA), F32)],
        compiler_params=_cp(("arbitrary",)),
    )(dqp, dkp, dv, cq, ckv, gq, gkv, w_uq, w_ukv, *tabs[2:])


def _ret_bwd(rq, rk, rv, rprev, ry, rg, dro, gn_w, tabs, S):
    C = RET_CHUNK
    N = S // C
    G = min(RET_GROUP_BWD, N)
    NB = N // G

    def body(lg_ref, q_ref, k_ref, v_ref, rp_ref, ry_ref, rg_ref, dro_ref, w_ref, cr_ref, sr_ref,
             drq_ref, drk_ref, drv_ref, drg_ref, dw_ref, g_sc):
        @pl.when(pl.program_id(1) == 0)
        def _():
            g_sc[...] = jnp.zeros(g_sc.shape, F32)
            dw_ref[...] = jnp.zeros(dw_ref.shape, F32)

        dmat, zeta, xi, g_chunk = _decay_terms(lg_ref)
        w = w_ref[...]
        gacc = g_sc[...]
        dw = jnp.zeros((1, 128), F32)
        for i in reversed(range(G)):
            rows = slice(i * C, (i + 1) * C)
            ry = ry_ref[rows, :]
            mu = jnp.mean(ry, axis=-1, keepdims=True)
            yc = ry - mu
            rstd = lax.rsqrt(jnp.mean(yc * yc, axis=-1, keepdims=True) + EPS)
            yh = yc * rstd
            g = rg_ref[rows, :]
            s = _sig(g)
            dout = dro_ref[rows, :].astype(F32)
            drg_ref[rows, :] = (dout * (yh * w) * (s * (1.0 + g * (1.0 - s)))).astype(BF16)
            dgn = dout * (g * s)
            dw = dw + _colsum(dgn * yh)
            dyh = dgn * w
            dry = rstd * (dyh - jnp.mean(dyh, axis=-1, keepdims=True) - yh * jnp.mean(dyh * yh, axis=-1, keepdims=True))
            do = dry.astype(BF16)

            q = q_ref[rows, :]
            k = k_ref[rows, :]
            v = v_ref[rows, :]
            gfut = gacc.astype(BF16)
            sc = (_dot_nt(q, k) * dmat).astype(BF16)
            dsc = (_dot_nt(do, v) * dmat).astype(BF16)
            dq = jnp.dot(dsc, k, preferred_element_type=F32) + _dot_nt(do, rp_ref[i]) * xi
            dk = _dot_tn(dsc, q) + _dot_nt(v, gfut) * zeta
            dv = _dot_tn(sc, do) + jnp.dot(k, gfut, preferred_element_type=F32) * zeta
            gacc = g_chunk * gacc + _dot_tn(q, xi * dry)
            cr = cr_ref[rows, :]
            sr = sr_ref[rows, :]
            drq_ref[rows, :] = _unrope_ret(dq, cr, sr).astype(BF16)
            drk_ref[rows, :] = _unrope_ret(dk * SCALE_RET, cr, sr).astype(BF16)
            drv_ref[rows, :] = dv.astype(BF16)
        g_sc[...] = gacc
        dw_ref[...] += dw

    blk = pl.BlockSpec((G * C, 128), lambda h, n: (NB - 1 - n, h))
    tab = pl.BlockSpec((G * C, 128), lambda h, n: (NB - 1 - n, 0))
    return pl.pallas_call(
        body, name="ret_bwd", grid=(RET_HEADS, NB),
        in_specs=[pl.BlockSpec((None, 8, 128), lambda h, n: (h, 0, 0)), blk, blk, blk,
                  pl.BlockSpec((G, 128, 128), lambda h, n: (h * NB + NB - 1 - n, 0, 0)), blk, blk, blk,
                  pl.BlockSpec((1, 128), lambda h, n: (0, h)), tab, tab],
        out_specs=[blk, blk, blk, blk, pl.BlockSpec((1, 128), lambda h, n: (0, h))],
        out_shape=[_sds((S, 512), BF16)] * 4 + [_sds((1, 512), F32)],
        scratch_shapes=[pltpu.VMEM((128, 128), F32)],
        compiler_params=_cp(("parallel", "arbitrary")),
    )(_decay_table(), rq, rk, rv, rprev, ry, rg, dro, gn_w, tabs[0], tabs[1])


def _inproj_bwd(drq, drk, drv, drg, dcq, dckv, dkr, w_in, dh1, x, g, S):
    tm = min(512, S)

    def body(drq_ref, drk_ref, drv_ref, drg_ref, dcq_ref, dckv_ref, dkr_ref, w_ref, dh1_ref, x_ref, g_ref,
             gx_ref, dproj_ref, dg_ref):
        @pl.when(pl.program_id(0) == 0)
        def _():
            dg_ref[...] = jnp.zeros(dg_ref.shape, F32)

        dproj_ref[:, 0:512] = drq_ref[...]
        dproj_ref[:, 512:1024] = drk_ref[...]
        dproj_ref[:, 1024:1536] = drv_ref[...]
        dproj_ref[:, 1536:2048] = drg_ref[...]
        dproj_ref[:, 2048:2432] = dcq_ref[...]
        dproj_ref[:, 2432:2688] = dckv_ref[...]
        dproj_ref[:, 2688:2816] = pltpu.roll(dkr_ref[...].astype(F32), 64, 1).astype(BF16)
        dx, ga = _rms_bwd(_dot_nt(dproj_ref[...], w_ref[...]), x_ref[...], g_ref[...])
        gx_ref[...] = dh1_ref[...] + dx
        dg_ref[...] += _colsum(ga)

    return pl.pallas_call(
        body, name="inproj_bwd", grid=(S // tm,),
        in_specs=[_rows(tm, 512)] * 4 + [_rows(tm, Q_LORA), _rows(tm, KV_LORA), _rows(tm, 128),
                                         _full(D_MODEL, IN_COLS_P), _rows(tm, D_MODEL), _rows(tm, D_MODEL),
                                         _full(1, D_MODEL)],
        out_specs=[_rows(tm, D_MODEL), _rows(tm, IN_COLS_P), _acc(1, D_MODEL)],
        out_shape=[_sds((S, D_MODEL), F32), _sds((S, IN_COLS_P), BF16), _sds((1, D_MODEL), F32)],
        compiler_params=_cp(("arbitrary",)),
    )(drq, drk, drv, drg, dcq, dckv, dkr, w_in, dh1, x, g)


def _pad_weights(w):
    w_in_p = jnp.pad(w["w_in"], ((0, 0), (0, IN_COLS_P - IN_COLS)))
    w_uq_p = jnp.pad(w["w_uq"].reshape(Q_LORA, MLA_HEADS, 96), ((0, 0), (0, 0), (0, 32))).reshape(Q_LORA, 1024)
    ukv = w["w_ukv"].reshape(KV_LORA, MLA_HEADS, 128)
    k_part = jnp.pad(ukv[:, :, :64], ((0, 0), (0, 0), (0, 64))).reshape(KV_LORA, 1024)
    w_ukv_p = jnp.concatenate([k_part, ukv[:, :, 64:].reshape(KV_LORA, 512)], axis=1)
    return w_in_p, w_uq_p, w_ukv_p


BIG_SPEC = {n: (r, c, ax) for n, r, c, ax in BIG}
COLUMN_MAJOR = ("w_in", "w_uq", "w_gate", "w_up")
GRAD_TRANSPOSED = ("w_gate", "w_up")
GATHER_FIRST = ("w_in", "w_uq", "w_ukv")
GATHER_LATE = tuple(n for n, _, _, _ in BIG if n not in GATHER_FIRST)
REDUCE_EARLY = ("w_ple_gate", "w_ple_proj", "w_down", "w_gate", "w_up", "w_o")
REDUCE_LAST = tuple(n for n, _, _, _ in BIG if n not in REDUCE_EARLY)


def _local_step(x, p, pos_f, tgt, w, sm, late_shards=None, c_idx=None):
    S = x.shape[0]
    spread = late_shards is not None
    w = dict(w)
    tabs, first = _rope_tables(pos_f, S, [late_shards[n] for n in GATHER_FIRST] if spread else ())
    for i, n in enumerate(GATHER_FIRST if spread else ()):
        w[n] = _from_chips(first[i], BIG_SPEC[n][2])
    w_in_p, w_uq_p, w_ukv_p = _pad_weights(w)
    if spread:
        w_in_p = jnp.concatenate([first[0][j] for j in range(N_CHIPS)]
                                 + [jnp.zeros((D_MODEL, IN_COLS_P - IN_COLS), BF16)], axis=1)

    xn, rq, rk, rv, rg, cq, ckv, kr = _inproj(x, sm["pre_mix_norm"], w_in_p, tabs, S)
    cqn, ckvn, qp, kp, v, kt, vt = _mla_up(cq, ckv, kr, sm["mla_q_norm"], sm["mla_kv_norm"], w_uq_p, w_ukv_p, tabs, S)
    mo, lse, gathered = _flash_fwd(qp, kp, vt, S, [late_shards[n] for n in GATHER_LATE] if spread else ())
    for i, n in enumerate(GATHER_LATE if spread else ()):
        w[n] = _from_chips(gathered[i], 0 if n in GRAD_TRANSPOSED else BIG_SPEC[n][2])
    if not spread:
        w.update({n: w[n].T for n in GRAD_TRANSPOSED})
    ry, ro, rprev = _ret_fwd(rq, rk, rv, rg, sm["ret_gn_w"], S)
    mix, h1, hn = _outproj(ro, mo, x, w["w_o"], sm["post_mix_norm"], sm["pre_ffn_norm"], S)
    dgate_f, dup_f, act = _ffn_up(hn, w["w_gate"], w["w_up"], S)
    ff, h2 = _ffn_down(act, w["w_down"], h1, sm["post_ffn_norm"], S)
    dz, dpe, dh2, h2b, loss_vec, d_ple_norm, d_b = _ple_loss(
        p, h2, tgt, w["w_ple_proj"], w["w_ple_gate"], sm["b_ple_gate"], sm["ple_norm"], S)

    gw = {}
    gs = {"ple_norm": d_ple_norm, "b_ple_gate": d_b}
    gw["w_ple_gate"] = _wgrad(h2b, dz, "wgrad_ple_gate", S)
    gw["w_ple_proj"] = _wgrad(p, dpe, "wgrad_ple_proj", S)
    dff, dgate, dup, gs["post_ffn_norm"] = _ffn_down_bwd(dh2, ff, sm["post_ffn_norm"], w["w_down"], dgate_f, dup_f, S)
    gw["w_down"] = _wgrad(act, dff, "wgrad_down", S)
    if spread:
        gw["w_gate"] = _wgrad(dgate, hn, "wgrad_gate", S)
        gw["w_up"] = _wgrad(dup, hn, "wgrad_up", S)
    else:
        gw["w_gate"] = _wgrad(hn, dgate, "wgrad_gate", S)
        gw["w_up"] = _wgrad(hn, dup, "wgrad_up", S)
    first = REDUCE_EARLY[:-1]
    g4 = [_by_chip(gw.pop(n), *((D_FF, D_MODEL, 0) if n in GRAD_TRANSPOSED else BIG_SPEC[n]))
          for n in first] if spread else []
    dh1, dmix, dro, dmo, gs["pre_ffn_norm"], gs["post_mix_norm"], got = _ffn_up_bwd(
        dgate, dup, w["w_gate"], w["w_up"], h1, mix, dh2, sm["pre_ffn_norm"], sm["post_mix_norm"], w["w_o"], S, g4)
    gw["w_o"] = jnp.concatenate([_wgrad(ro, dmix, "wgrad_o_ret", S), _wgrad(mo, dmix, "wgrad_o_mla", S)], axis=0)
    g4_o = [_by_chip(gw.pop("w_o"), *BIG_SPEC["w_o"])] if spread else []

    dmo_t, delta, got_o = _attn_delta(mo, dmo, S, g4_o)
    sums = _add_half_rows(g4 + g4_o, list(got) + list(got_o), c_idx, "rs_add_halves_early") if spread else []
    dqp, dkp, dv, parts = _flash_bwd(qp, kp, kt, v, dmo, dmo_t, lse, delta, S, sums)
    dqh, dkv, dcq, dckv, dkr, gs["mla_q_norm"], gs["mla_kv_norm"] = _mla_up_bwd(
        dqp, dkp, dv, cq, ckv, sm["mla_q_norm"], sm["mla_kv_norm"], w_uq_p, w_ukv_p, tabs, S)
    g_uq_p = _wgrad(cqn, dqh, "wgrad_uq", S)
    g_ukv_p = _wgrad(ckvn, dkv, "wgrad_ukv", S)
    gw["w_uq"] = g_uq_p.reshape(Q_LORA, MLA_HEADS, 128)[:, :, :96].reshape(Q_LORA, 768)
    gw["w_ukv"] = jnp.concatenate(
        [g_ukv_p[:, :1024].reshape(KV_LORA, MLA_HEADS, 128)[:, :, :64], g_ukv_p[:, 1024:].reshape(KV_LORA, MLA_HEADS, 64)],
        axis=2).reshape(KV_LORA, 1024)

    drq, drk, drv, drg, gs["ret_gn_w"] = _ret_bwd(rq, rk, rv, rprev, ry, rg, dro, sm["ret_gn_w"], tabs, S)
    grad_x, dproj, gs["pre_mix_norm"] = _inproj_bwd(drq, drk, drv, drg, dcq, dckv, dkr, w_in_p, dh1, x,
                                                    sm["pre_mix_norm"], S)
    if spread:
        gw["w_in"] = _wgrad(dproj, xn, "wgrad_in", S)[:IN_COLS]
    else:
        gw["w_in"] = _wgrad(xn, dproj, "wgrad_in", S)[:, :IN_COLS]
    return loss_vec, grad_x, gw, gs, ((sums, parts) if spread else None)


def _my_place():
    x = lax.axis_index("x")
    y = lax.axis_index("y")
    c = lax.axis_index("c")
    return x, y, c


def _other_chips(x, y):
    return [(1 - x, y), (x, 1 - y), (1 - x, 1 - y)]


_ANY = pl.BlockSpec(memory_space=pl.ANY)


def _small_copies(v_ref, slots, sems):
    send, recv, lsem = sems
    x, y, c = _my_place()
    me = 4 * x + 2 * y + c
    cps = [pltpu.make_async_copy(v_ref, slots.at[me], lsem)]
    for r in range(1, N_DEV):
        peer = (x ^ (r >> 2), y ^ ((r >> 1) & 1), c ^ (r & 1))
        cps.append(pltpu.make_async_remote_copy(
            src_ref=v_ref, dst_ref=slots.at[me], send_sem=send.at[r - 1], recv_sem=recv.at[r - 1],
            device_id=peer, device_id_type=MESH))
    return cps


def _small_sum(slots, out_ref):
    acc = slots[0]
    for d in range(1, N_DEV):
        acc = acc + slots[d]
    out_ref[...] = acc
    loss = jnp.sum(acc[9:10, :], axis=1, keepdims=True) * (0.5 / D_MODEL)
    out_ref[9:10, :] = jnp.broadcast_to(loss, (1, PACK_COLS))


def _small_scratch():
    return [pltpu.VMEM((N_DEV, SMALL_ROWS, PACK_COLS), F32), pltpu.SemaphoreType.DMA((N_DEV - 1,)),
            pltpu.SemaphoreType.DMA((N_DEV - 1,)), pltpu.SemaphoreType.DMA]


N_BIG = len(BIG)


def _half(c, rows, align):
    h = rows // 2
    return pl.ds(pl.multiple_of(c * h, align), h)


def _gather_out_shapes(shards):
    return [_sds((N_CHIPS,) + tuple(s.shape), BF16) for s in shards]


def _gather_sems(n):
    return [pltpu.SemaphoreType.DMA((n, 3))] * 4 + [pltpu.SemaphoreType.DMA((n,))] * 2


def _gather_phase(phase, ins, outs, sems):
    send1, recv1, send2, recv2, send3, recv3 = sems
    x, y, c = _my_place()
    me = 2 * x + y
    chips = _other_chips(x, y)
    sib = (x, y, 1 - c)
    for t in range(len(ins)):
        rows = ins[t].shape[0]
        half = _half(c, rows, 16)
        other = _half(1 - c, rows, 16)
        def own():
            return pltpu.make_async_remote_copy(
                src_ref=ins[t], dst_ref=outs[t].at[me], send_sem=send3.at[t], recv_sem=recv3.at[t],
                device_id=sib, device_id_type=MESH)

        if phase == 0:
            own().start()
        if phase == 2:
            own().wait()
        for k, (cx, cy) in enumerate(chips):
            src = 2 * cx + cy

            def over_ici(slab):
                return pltpu.make_async_remote_copy(
                    src_ref=ins[t].at[half], dst_ref=outs[t].at[slab, half], send_sem=send1.at[t, k],
                    recv_sem=recv1.at[t, k], device_id=(cx, cy, c), device_id_type=MESH)

            def over_d2d(rows):
                return pltpu.make_async_remote_copy(
                    src_ref=outs[t].at[src, rows], dst_ref=outs[t].at[src, rows], send_sem=send2.at[t, k],
                    recv_sem=recv2.at[t, k], device_id=sib, device_id_type=MESH)

            if phase == 0:
                over_ici(me).start()
            if phase == 1:
                over_ici(src).wait_recv()
                over_d2d(half).start()
            if phase == 2:
                over_d2d(other).wait_recv()
                over_ici(me).wait_send()
                over_d2d(half).wait_send()


def _swap_copies(ins, outs, sems):
    send, recv = sems
    x, y, c = _my_place()
    return [pltpu.make_async_remote_copy(
        src_ref=ins[t].at[:, _half(1 - c, ins[t].shape[1], 8)], dst_ref=outs[t], send_sem=send.at[t],
        recv_sem=recv.at[t], device_id=(x, y, 1 - c), device_id_type=MESH) for t in range(len(ins))]


def _swap_out_shapes(gs):
    return [_sds((N_CHIPS, g.shape[1] // 2, g.shape[2]), F32) for g in gs]


def _swap_sems(n):
    return [pltpu.SemaphoreType.DMA((n,)), pltpu.SemaphoreType.DMA((n,))]


def _swap_half_rows(gs):
    n = len(gs)

    def body(*refs):
        cps = _swap_copies(refs[:n], refs[n:2 * n], refs[2 * n:])
        for cp in cps:
            cp.start()
        for cp in cps:
            cp.wait()

    return pl.pallas_call(
        body, name="rs_swap_halves",
        in_specs=[_ANY] * n, out_specs=[_ANY] * n, out_shape=_swap_out_shapes(gs), scratch_shapes=_swap_sems(n),
    )(*gs)


def _add_half_rows(gs, gots, c_idx, name):
    n = len(gs)
    shapes = [(g.shape[1] // 2, g.shape[2]) for g in gs]

    def body(c_ref, *refs):
        for t in range(n):
            refs[2 * n + t][...] = (refs[t][...] + refs[n + t][...]).astype(BF16)

    grid_spec = pltpu.PrefetchScalarGridSpec(
        num_scalar_prefetch=1, grid=(N_CHIPS,),
        in_specs=[pl.BlockSpec((None,) + hc, lambda j, c: (j, c[0], 0)) for hc in shapes]
        + [pl.BlockSpec((None,) + hc, lambda j, c: (j, 0, 0)) for hc in shapes],
        out_specs=[pl.BlockSpec((None,) + hc, lambda j, c: (j, 0, 0)) for hc in shapes],
    )
    return pl.pallas_call(
        body, name=name, grid_spec=grid_spec, out_shape=[_sds((N_CHIPS,) + hc, BF16) for hc in shapes],
        compiler_params=_cp(("parallel",)),
    )(c_idx, *gs, *gots)


def _scatter_to_chips(ts, vec):
    n = len(ts)

    def body(*refs):
        ins, v_ref, outs, small_ref = refs[:n], refs[n], refs[n + 1:2 * n + 1], refs[2 * n + 1]
        slots, small_sems, sems = refs[2 * n + 2], refs[2 * n + 3:2 * n + 6], refs[2 * n + 6:]
        small = _small_copies(v_ref, slots, small_sems)
        cps = _scatter_copies(ins, outs, sems)
        for cp in small + cps:
            cp.start()
        for cp in small:
            cp.wait()
        _small_sum(slots, small_ref)
        for cp in cps:
            cp.wait()

    vm = pl.BlockSpec(memory_space=pltpu.VMEM)
    *parts, small_sum = pl.pallas_call(
        body, name="rs_scatter_chips",
        in_specs=[_ANY] * n + [vm], out_specs=[_ANY] * n + [vm],
        out_shape=_scatter_out_shapes(ts) + [_sds((SMALL_ROWS, PACK_COLS), F32)],
        scratch_shapes=_small_scratch() + _scatter_sems(n),
    )(*ts, vec)
    return parts, small_sum


def _scatter_copies(ins, outs, sems):
    send, recv = sems
    x, y, c = _my_place()
    return [pltpu.make_async_remote_copy(
        src_ref=ins[t].at[2 * cx + cy], dst_ref=outs[t].at[k], send_sem=send.at[t, k], recv_sem=recv.at[t, k],
        device_id=(cx, cy, c), device_id_type=MESH)
        for t in range(len(ins)) for k, (cx, cy) in enumerate(_other_chips(x, y))]


def _scatter_out_shapes(ts):
    return [_sds((3,) + tuple(t.shape[1:]), BF16) for t in ts]


def _scatter_sems(n):
    return [pltpu.SemaphoreType.DMA((n, 3)), pltpu.SemaphoreType.DMA((n, 3))]


def _add_four(mines, parts, place):
    n = len(mines)

    def body(pl_ref, *refs):
        for t in range(n):
            m_ref, p_ref, o_ref = refs[t], refs[n + t], refs[2 * n + t]
            o_ref[...] = ((m_ref[...].astype(F32) + p_ref[0].astype(F32)) + p_ref[1].astype(F32)) + p_ref[2].astype(F32)

    shapes = [p.shape[1:] for p in parts]
    grid_spec = pltpu.PrefetchScalarGridSpec(
        num_scalar_prefetch=1, grid=(1,),
        in_specs=[pl.BlockSpec((None,) + hc, lambda i, pc: (pc[0], 0, 0)) for hc in shapes]
        + [pl.BlockSpec((3,) + hc, lambda i, pc: (0, 0, 0)) for hc in shapes],
        out_specs=[pl.BlockSpec(hc, lambda i, pc: (pc[1], 0)) for hc in shapes],
    )
    return pl.pallas_call(
        body, name="rs_add_chips", grid_spec=grid_spec, out_shape=[_sds((2 * h, c), F32) for h, c in shapes],
        compiler_params=_cp(("arbitrary",)),
    )(place, *mines, *parts)


def _join_half_rows(rs):
    n = len(rs)

    def body(*refs):
        ins, outs = refs[:n], refs[n:2 * n]
        send, recv = refs[2 * n:]
        x, y, c = _my_place()
        cps = []
        for t in range(n):
            half = _half(c, outs[t].shape[0], 8)
            rc = pltpu.make_async_remote_copy(
                src_ref=ins[t].at[half], dst_ref=outs[t].at[half], send_sem=send.at[t], recv_sem=recv.at[t],
                device_id=(x, y, 1 - c), device_id_type=MESH)
            rc.start()
            cps.append(rc)
        for cp in cps:
            cp.wait()

    return pl.pallas_call(
        body, name="rs_join_halves",
        in_specs=[_ANY] * n, out_specs=[_ANY] * n,
        out_shape=[_sds(r.shape, F32) for r in rs],
        input_output_aliases={i: i for i in range(n)},
        scratch_shapes=[pltpu.SemaphoreType.DMA((n,))] * 2,
    )(*rs)


def _by_chip(full, rows, cols, axis):
    if axis == 0:
        return full.reshape(N_CHIPS, rows // N_CHIPS, cols)
    return full.reshape(rows, N_CHIPS, cols // N_CHIPS).transpose(1, 0, 2)


def _from_chips(parts, axis):
    _, r, c = parts.shape
    if axis == 0:
        return parts.reshape(N_CHIPS * r, c)
    return parts.transpose(1, 0, 2).reshape(r, N_CHIPS * c)


def _adamw(wt, g, m, v, name):
    _, R, C = wt.shape
    tr = max(d for d in range(8, R + 1, 8) if R % d == 0 and (d * C <= 512 * 1024 or d == 8))

    def body(w_ref, g_ref, m_ref, v_ref, d_ref, nm_ref, nv_ref):
        gg = g_ref[...]
        m_new = ADAM_B1 * m_ref[...] + (1.0 - ADAM_B1) * gg
        v_new = ADAM_B2 * v_ref[...] + (1.0 - ADAM_B2) * (gg * gg)
        m_hat = m_new / (1.0 - ADAM_B1 ** ADAM_STEP)
        v_hat = v_new / (1.0 - ADAM_B2 ** ADAM_STEP)
        d_ref[...] = -ADAM_LR * (m_hat / (jnp.sqrt(v_hat) + ADAM_EPS) + ADAM_WD * w_ref[...])
        nm_ref[...] = m_new
        nv_ref[...] = v_new

    spec = pl.BlockSpec((None, tr, C), lambda i: (0, i, 0))
    return pl.pallas_call(
        body, name=name, grid=(R // tr,), in_specs=[spec, pl.BlockSpec((tr, C), lambda i: (i, 0)), spec, spec],
        out_specs=[spec] * 3, out_shape=[_sds((1, R, C), F32)] * 3,
        compiler_params=_cp(("parallel",)),
    )(wt, g, m, v)


def _pack_small(vals, loss_vec=None):
    rows = [jnp.pad(vals[n].reshape(-1), (0, PACK_COLS - sz)) for n, sz in SMALL]
    rows.append(loss_vec.reshape(-1) if loss_vec is not None else jnp.zeros((PACK_COLS,), F32))
    rows += [jnp.zeros((PACK_COLS,), F32)] * (SMALL_ROWS - len(rows))
    return jnp.stack(rows)


def kernel(x, p, positions, pre_mix_norm, w_in, ret_gn_w, mla_q_norm, w_uq, mla_kv_norm, w_ukv, w_o, post_mix_norm, pre_ffn_norm, w_gate, w_up, w_down, post_ffn_norm, w_ple_proj, ple_norm, w_ple_gate, b_ple_gate, loss_target, m_pre_mix_norm, m_w_in, m_ret_gn_w, m_mla_q_norm, m_w_uq, m_mla_kv_norm, m_w_ukv, m_w_o, m_post_mix_norm, m_pre_ffn_norm, m_w_gate, m_w_up, m_w_down, m_post_ffn_norm, m_w_ple_proj, m_ple_norm, m_w_ple_gate, m_b_ple_gate, v_pre_mix_norm, v_w_in, v_ret_gn_w, v_mla_q_norm, v_w_uq, v_mla_kv_norm, v_w_ukv, v_w_o, v_post_mix_norm, v_pre_ffn_norm, v_w_gate, v_w_up, v_w_down, v_post_ffn_norm, v_w_ple_proj, v_ple_norm, v_w_ple_gate, v_b_ple_gate):
    wts = dict(pre_mix_norm=pre_mix_norm, w_in=w_in, ret_gn_w=ret_gn_w, mla_q_norm=mla_q_norm, w_uq=w_uq,
               mla_kv_norm=mla_kv_norm, w_ukv=w_ukv, w_o=w_o, post_mix_norm=post_mix_norm, pre_ffn_norm=pre_ffn_norm,
               w_gate=w_gate, w_up=w_up, w_down=w_down, post_ffn_norm=post_ffn_norm, w_ple_proj=w_ple_proj,
               ple_norm=ple_norm, w_ple_gate=w_ple_gate, b_ple_gate=b_ple_gate)
    mom = dict(pre_mix_norm=m_pre_mix_norm, w_in=m_w_in, ret_gn_w=m_ret_gn_w, mla_q_norm=m_mla_q_norm, w_uq=m_w_uq,
               mla_kv_norm=m_mla_kv_norm, w_ukv=m_w_ukv, w_o=m_w_o, post_mix_norm=m_post_mix_norm,
               pre_ffn_norm=m_pre_ffn_norm, w_gate=m_w_gate, w_up=m_w_up, w_down=m_w_down, post_ffn_norm=m_post_ffn_norm,
               w_ple_proj=m_w_ple_proj, ple_norm=m_ple_norm, w_ple_gate=m_w_ple_gate, b_ple_gate=m_b_ple_gate)
    var = dict(pre_mix_norm=v_pre_mix_norm, w_in=v_w_in, ret_gn_w=v_ret_gn_w, mla_q_norm=v_mla_q_norm, w_uq=v_w_uq,
               mla_kv_norm=v_mla_kv_norm, w_ukv=v_w_ukv, w_o=v_w_o, post_mix_norm=v_post_mix_norm,
               pre_ffn_norm=v_pre_ffn_norm, w_gate=v_w_gate, w_up=v_w_up, w_down=v_w_down, post_ffn_norm=v_post_ffn_norm,
               w_ple_proj=v_w_ple_proj, ple_norm=v_ple_norm, w_ple_gate=v_w_ple_gate, b_ple_gate=v_b_ple_gate)

    S = x.shape[1]
    shard2d = {n: wts[n][0] for n, _, _, _ in BIG}
    small2d = {n: wts[n] for n, _ in SMALL}

    shard_bf = {n: (jnp.swapaxes(wts[n], 1, 2)[0] if n in GRAD_TRANSPOSED else shard2d[n]).astype(BF16) for n in shard2d}
    pos_f = positions.astype(F32).reshape(S, 1)
    c_idx = lax.axis_index("c").astype(jnp.int32).reshape(1)
    loss_vec, grad_x, gw, gs, (sums_early, parts_early) = _local_step(
        x[0], p[0, 0], pos_f, loss_target[0], {}, small2d, shard_bf, c_idx)

    g4 = [_by_chip(gw[n], *BIG_SPEC[n]) for n in REDUCE_LAST if n != "w_in"]
    g4.insert(REDUCE_LAST.index("w_in"), jnp.pad(gw["w_in"].reshape(N_CHIPS, IN_SHARD, D_MODEL),
                                                 ((0, 0), (0, IN_SHARD_P - IN_SHARD), (0, 0))))
    got = _swap_half_rows(g4)
    sums_last = _add_half_rows(g4, got, c_idx, "rs_add_halves_last")
    parts_last, small_sum = _scatter_to_chips(sums_last, _pack_small(gs, loss_vec))
    place = jnp.stack([2 * lax.axis_index("x") + lax.axis_index("y"), lax.axis_index("c")]).astype(jnp.int32)
    names = REDUCE_EARLY + REDUCE_LAST
    reduced = _join_half_rows(_add_four(sums_early + sums_last, list(parts_early) + list(parts_last), place))
    g_shard = dict(zip(names, reduced))

    loss = small_sum[9, 0]
    g_small = {n: small_sum[i:i + 1, :sz] for i, (n, sz) in enumerate(SMALL)}

    grads, delta, new_m, new_v = {}, {}, {}, {}
    for n, _, _, _ in BIG:
        if n in COLUMN_MAJOR:
            turn = lambda a: jnp.swapaxes(a, 1, 2)
            g_t = g_shard[n][:IN_SHARD] if n == "w_in" else g_shard[n] if n in GRAD_TRANSPOSED else g_shard[n].T
            d, nm, nv = _adamw(turn(wts[n]), g_t, turn(mom[n]), turn(var[n]), "adamw_" + n)
            grads[n], delta[n], new_m[n], new_v[n] = turn(g_t[None]), turn(d), turn(nm), turn(nv)
        else:
            delta[n], new_m[n], new_v[n] = _adamw(wts[n], g_shard[n], mom[n], var[n], "adamw_" + n)
            grads[n] = g_shard[n][None]
    d, nm, nv = _adamw(_pack_small(small2d)[None], small_sum, _pack_small(mom)[None], _pack_small(var)[None],
                       "adamw_small")
    for i, (n, sz) in enumerate(SMALL):
        grads[n] = g_small[n]
        delta[n], new_m[n], new_v[n] = d[0, i:i + 1, :sz], nm[0, i:i + 1, :sz], nv[0, i:i + 1, :sz]

    return (loss, grad_x[None], *[grads[n] for n in ALL_W], *[delta[n] for n in ALL_W],
            *[new_m[n] for n in ALL_W], *[new_v[n] for n in ALL_W])
```
